```python
import math
import jax, jax.numpy as jnp
from jax import lax
import numpy as np

D_MODEL = 1024
BATCH = 16
SEQ = 2048
DEPTH = 1

PLE_DIM = 256
D_RNN = 1280
RNN_BLOCKS = 10
RNN_BLOCK_W = D_RNN // RNN_BLOCKS
CONV_W = 4
LRU_C = 8.0
HEAD_DIM = 128
HEADS_PER_GROUP = 4
ATTN_PATTERNS = ((128, 1), (512, 4), (2048, 16))
N_GROUPS = len(ATTN_PATTERNS)
ATT_W = HEADS_PER_GROUP * HEAD_DIM
QKV_W = N_GROUPS * 3 * ATT_W
N_BRANCH = 2
N_IN = 2 * D_RNN + QKV_W + ATT_W + N_BRANCH * D_MODEL
ROPE_THETA = 10000.0
EPS = 1e-6

OFF_Z_RNN = D_RNN
OFF_QKV = 2 * D_RNN
OFF_Z_ATT = OFF_QKV + QKV_W
OFF_GATES = OFF_Z_ATT + ATT_W

kernel_name = "hybrid_rglru_dilated_attn_block"


def rms_norm(x, gain):
    xf = x.astype(jnp.float32)
    var = jnp.mean(xf * xf, axis=-1, keepdims=True)
    return (xf * lax.rsqrt(var + EPS) * gain.astype(jnp.float32)).astype(x.dtype)


def rope(t, cos, sin):
    t1, t2 = jnp.split(t.astype(jnp.float32), 2, axis=-1)
    c = cos[None, :, None, None, :]
    s = sin[None, :, None, None, :]
    return jnp.concatenate([t1 * c - t2 * s, t2 * c + t1 * s], axis=-1).astype(t.dtype)


def causal_depthwise_conv(x, w, b):
    s = x.shape[1]
    xp = jnp.pad(x, ((0, 0), (CONV_W - 1, 0), (0, 0)))
    y = b[None, None, :]
    for k in range(CONV_W):
        y = y + w[k][None, None, :] * xp[:, k:k + s]
    return y


def rg_lru(x, w_a, b_a, w_x, b_x, lam):
    b, s, _ = x.shape
    xf = x.astype(jnp.float32)
    xb = xf.reshape(b, s, RNN_BLOCKS, RNN_BLOCK_W)
    r = jax.nn.sigmoid(jnp.einsum('bsni,nij->bsnj', xb, w_a.astype(jnp.float32)).reshape(b, s, D_RNN) + b_a.astype(jnp.float32))
    i = jax.nn.sigmoid(jnp.einsum('bsni,nij->bsnj', xb, w_x.astype(jnp.float32)).reshape(b, s, D_RNN) + b_x.astype(jnp.float32))
    log_a = -LRU_C * r * jax.nn.softplus(-lam.astype(jnp.float32))[None, None, :]
    a = jnp.exp(log_a)
    mult = jnp.sqrt(-jnp.expm1(2.0 * log_a))
    mult = mult.at[:, 0].set(1.0)
    u = mult * (i * xf)

    def combine(left, right):
        a_l, u_l = left
        a_r, u_r = right
        return a_l * a_r, a_r * u_l + u_r

    _, h = lax.associative_scan(combine, (a, u), axis=1)
    return h.astype(x.dtype)


def dilated_window_attention(q, k, v, window, dilation):
    b, s, h, hd = q.shape
    blk = window // dilation
    sub_len = s // dilation
    n_blk = -(-sub_len // blk)
    padded = n_blk * blk

    def sub(t):
        return t.reshape(b, sub_len, dilation, h, hd)

    qb = jnp.pad(sub(q), ((0, 0), (0, padded - sub_len), (0, 0), (0, 0), (0, 0)))
    qb = qb.reshape(b, n_blk, blk, dilation, h, hd)

    def key_blocks(t):
        tp = jnp.pad(sub(t), ((0, 0), (blk, padded - sub_len), (0, 0), (0, 0), (0, 0)))
        prev = tp[:, :padded].reshape(b, n_blk, blk, dilation, h, hd)
        cur = tp[:, blk:].reshape(b, n_blk, blk, dilation, h, hd)
        return jnp.concatenate([prev, cur], axis=2)

    kb = key_blocks(k)
    vb = key_blocks(v)
    scores = jnp.einsum('bnqchd,bnkchd->bnchqk', qb, kb,
                        preferred_element_type=jnp.float32) * (hd ** -0.5)
    qi = jnp.arange(blk)[:, None]
    kj = jnp.arange(2 * blk)[None, :]
    dist = qi + blk - kj
    key_pos = jnp.arange(n_blk)[:, None, None] * blk + kj[None] - blk
    valid = ((dist >= 0) & (dist <= blk))[None] & (key_pos >= 0)
    scores = jnp.where(valid[None, :, None, None], scores, -jnp.inf)
    m = jnp.max(scores, axis=-1, keepdims=True)
    e = jnp.exp(scores - m)
    den = jnp.sum(e, axis=-1)
    o = jnp.einsum('bnchqk,bnkchd->bnqchd', e, vb.astype(jnp.float32))
    o = o / jnp.moveaxis(den, -1, 2)[..., None]
    lse = jnp.moveaxis(m[..., 0] + jnp.log(den), -1, 2)
    o = o.reshape(b, padded, dilation, h, hd)[:, :sub_len].reshape(b, s, h, hd)
    lse = lse.reshape(b, padded, dilation, h)[:, :sub_len].reshape(b, s, h)
    return o, lse


def _fwd_setup_inputs(seed: int = 0) -> dict:
    key = jax.random.key(seed)
    ks = jax.random.split(key, 24)
    f32 = jnp.float32

    def nrm(k, shape, scale):
        return jax.random.normal(k, shape, f32) * scale

    u = jax.random.uniform(ks[9], (DEPTH, D_RNN), f32, minval=0.9, maxval=0.999)
    return {
        "x": nrm(ks[0], (BATCH, SEQ, D_MODEL), 1.0),
        "p": nrm(ks[1], (DEPTH, BATCH, SEQ, PLE_DIM), 1.0),
        "norm_mix": 1.0 + nrm(ks[2], (DEPTH, D_MODEL), 0.02),
        "w_in": nrm(ks[3], (DEPTH, D_MODEL, N_IN), D_MODEL ** -0.5),
        "b_in": nrm(ks[4], (DEPTH, N_IN), 0.01),
        "conv_w": nrm(ks[5], (DEPTH, CONV_W, D_RNN), CONV_W ** -0.5),
        "conv_b": nrm(ks[6], (DEPTH, D_RNN), 0.01),
        "w_rg_a": nrm(ks[7], (DEPTH, RNN_BLOCKS, RNN_BLOCK_W, RNN_BLOCK_W), RNN_BLOCK_W ** -0.5),
        "b_rg_a": nrm(ks[8], (DEPTH, D_RNN), 0.01),
        "w_rg_x": nrm(ks[10], (DEPTH, RNN_BLOCKS, RNN_BLOCK_W, RNN_BLOCK_W), RNN_BLOCK_W ** -0.5),
        "b_rg_x": nrm(ks[11], (DEPTH, D_RNN), 0.01),
        "lru_lambda": jnp.log(u) - jnp.log1p(-u),
        "q_norm": 1.0 + nrm(ks[12], (DEPTH, N_GROUPS, HEAD_DIM), 0.02),
        "k_norm": 1.0 + nrm(ks[13], (DEPTH, N_GROUPS, HEAD_DIM), 0.02),
        "w_o_rnn": nrm(ks[14], (DEPTH, D_RNN, D_MODEL), D_RNN ** -0.5),
        "w_o_att": nrm(ks[15], (DEPTH, ATT_W, D_MODEL), ATT_W ** -0.5),
        "w_out": nrm(ks[16], (DEPTH, D_MODEL, D_MODEL), D_MODEL ** -0.5),
        "norm_ple": 1.0 + nrm(ks[17], (DEPTH, D_MODEL), 0.02),
        "w_ple_gate": nrm(ks[18], (DEPTH, D_MODEL, D_MODEL), D_MODEL ** -0.5),
        "b_ple_gate": nrm(ks[19], (DEPTH, D_MODEL), 0.01),
        "w_ple": nrm(ks[20], (DEPTH, PLE_DIM, D_MODEL), PLE_DIM ** -0.5),
    }


def _fwd_reference(x, p, norm_mix, w_in, b_in, conv_w, conv_b, w_rg_a, b_rg_a, w_rg_x, b_rg_x,
              lru_lambda, q_norm, k_norm, w_o_rnn, w_o_att, w_out, norm_ple, w_ple_gate,
              b_ple_gate, w_ple):
    b, s, _ = x.shape
    pos = jnp.arange(s, dtype=jnp.float32)
    inv_freq = ROPE_THETA ** (-jnp.arange(0, HEAD_DIM, 2, dtype=jnp.float32) / HEAD_DIM)
    ang = pos[:, None] * inv_freq[None, :]
    cos, sin = jnp.cos(ang), jnp.sin(ang)

    for layer in range(DEPTH):
        hn = rms_norm(x, norm_mix[layer])
        proj = jnp.einsum('bsd,dn->bsn', hn, w_in[layer]) + b_in[layer]
        x_rnn = proj[..., :OFF_Z_RNN]
        z_rnn = proj[..., OFF_Z_RNN:OFF_QKV]
        qkv = proj[..., OFF_QKV:OFF_Z_ATT]
        z_att = proj[..., OFF_Z_ATT:OFF_GATES]
        gates = jax.nn.sigmoid(proj[..., OFF_GATES:].reshape(b, s, N_BRANCH, D_MODEL))

        xc = causal_depthwise_conv(x_rnn, conv_w[layer], conv_b[layer])
        h_rnn = rg_lru(xc, w_rg_a[layer], b_rg_a[layer], w_rg_x[layer], b_rg_x[layer], lru_lambda[layer])
        y_rnn = h_rnn * jax.nn.silu(z_rnn)

        qkv = qkv.reshape(b, s, N_GROUPS, 3, HEADS_PER_GROUP, HEAD_DIM)
        q = rms_norm(qkv[:, :, :, 0], q_norm[layer][:, None, :])
        k = rms_norm(qkv[:, :, :, 1], k_norm[layer][:, None, :])
        v = qkv[:, :, :, 2]
        q = rope(q, cos, sin)
        k = rope(k, cos, sin)
        outs, lses = [], []
        for g, (window, dilation) in enumerate(ATTN_PATTERNS):
            o_g, lse_g = dilated_window_attention(q[:, :, g], k[:, :, g], v[:, :, g], window, dilation)
            outs.append(o_g)
            lses.append(lse_g)
        wts = jax.nn.softmax(jnp.stack(lses, axis=0), axis=0)
        att = jnp.sum(wts[..., None] * jnp.stack(outs, axis=0), axis=0)
        att = att.astype(x.dtype).reshape(b, s, ATT_W)
        y_att = att * jax.nn.silu(z_att)

        yr = jnp.einsum('bsc,cd->bsd', y_rnn, w_o_rnn[layer])
        ya = jnp.einsum('bsc,cd->bsd', y_att, w_o_att[layer])
        merged = gates[:, :, 0] * yr + gates[:, :, 1] * ya
        x = x + jnp.einsum('bsd,de->bse', merged, w_out[layer])

        pe = jnp.einsum('bsk,kd->bsd', p[layer], w_ple[layer])
        pg = jax.nn.sigmoid(jnp.einsum('bsd,de->bse', rms_norm(x, norm_ple[layer]), w_ple_gate[layer]) + b_ple_gate[layer])
        x = x + pg * pe
    return x


import jax as _jax
import jax.numpy as _jnp

TWIN_FORMAT = 'train_step'
FWD_PARAMS = ['x', 'p', 'norm_mix', 'w_in', 'b_in', 'conv_w', 'conv_b', 'w_rg_a', 'b_rg_a', 'w_rg_x', 'b_rg_x', 'lru_lambda', 'q_norm', 'k_norm', 'w_o_rnn', 'w_o_att', 'w_out', 'norm_ple', 'w_ple_gate', 'b_ple_gate', 'w_ple']
TWIN_WEIGHTS = ['norm_mix', 'w_in', 'b_in', 'conv_w', 'conv_b', 'w_rg_a', 'b_rg_a', 'w_rg_x', 'b_rg_x', 'lru_lambda', 'q_norm', 'k_norm', 'w_o_rnn', 'w_o_att', 'w_out', 'norm_ple', 'w_ple_gate', 'b_ple_gate', 'w_ple']
TWIN_DIFF_INPUT = 'x'
TWIN_INPUTS = ['x', 'p', 'norm_mix', 'w_in', 'b_in', 'conv_w', 'conv_b', 'w_rg_a', 'b_rg_a', 'w_rg_x', 'b_rg_x', 'lru_lambda', 'q_norm', 'k_norm', 'w_o_rnn', 'w_o_att', 'w_out', 'norm_ple', 'w_ple_gate', 'b_ple_gate', 'w_ple', 'loss_target', 'm_norm_mix', 'm_w_in', 'm_b_in', 'm_conv_w', 'm_conv_b', 'm_w_rg_a', 'm_b_rg_a', 'm_w_rg_x', 'm_b_rg_x', 'm_lru_lambda', 'm_q_norm', 'm_k_norm', 'm_w_o_rnn', 'm_w_o_att', 'm_w_out', 'm_norm_ple', 'm_w_ple_gate', 'm_b_ple_gate', 'm_w_ple', 'v_norm_mix', 'v_w_in', 'v_b_in', 'v_conv_w', 'v_conv_b', 'v_w_rg_a', 'v_b_rg_a', 'v_w_rg_x', 'v_b_rg_x', 'v_lru_lambda', 'v_q_norm', 'v_k_norm', 'v_w_o_rnn', 'v_w_o_att', 'v_w_out', 'v_norm_ple', 'v_w_ple_gate', 'v_b_ple_gate', 'v_w_ple']
TWIN_OUTPUTS = ['loss', 'grad_x', 'grad_norm_mix', 'grad_w_in', 'grad_b_in', 'grad_conv_w', 'grad_conv_b', 'grad_w_rg_a', 'grad_b_rg_a', 'grad_w_rg_x', 'grad_b_rg_x', 'grad_lru_lambda', 'grad_q_norm', 'grad_k_norm', 'grad_w_o_rnn', 'grad_w_o_att', 'grad_w_out', 'grad_norm_ple', 'grad_w_ple_gate', 'grad_b_ple_gate', 'grad_w_ple', 'delta_norm_mix', 'delta_w_in', 'delta_b_in', 'delta_conv_w', 'delta_conv_b', 'delta_w_rg_a', 'delta_b_rg_a', 'delta_w_rg_x', 'delta_b_rg_x', 'delta_lru_lambda', 'delta_q_norm', 'delta_k_norm', 'delta_w_o_rnn', 'delta_w_o_att', 'delta_w_out', 'delta_norm_ple', 'delta_w_ple_gate', 'delta_b_ple_gate', 'delta_w_ple', 'new_m_norm_mix', 'new_m_w_in', 'new_m_b_in', 'new_m_conv_w', 'new_m_conv_b', 'new_m_w_rg_a', 'new_m_b_rg_a', 'new_m_w_rg_x', 'new_m_b_rg_x', 'new_m_lru_lambda', 'new_m_q_norm', 'new_m_k_norm', 'new_m_w_o_rnn', 'new_m_w_o_att', 'new_m_w_out', 'new_m_norm_ple', 'new_m_w_ple_gate', 'new_m_b_ple_gate', 'new_m_w_ple', 'new_v_norm_mix', 'new_v_w_in', 'new_v_b_in', 'new_v_conv_w', 'new_v_conv_b', 'new_v_w_rg_a', 'new_v_b_rg_a', 'new_v_w_rg_x', 'new_v_b_rg_x', 'new_v_lru_lambda', 'new_v_q_norm', 'new_v_k_norm', 'new_v_w_o_rnn', 'new_v_w_o_att', 'new_v_w_out', 'new_v_norm_ple', 'new_v_w_ple_gate', 'new_v_b_ple_gate', 'new_v_w_ple']
TWIN_LEAF_KINDS = {'loss': 'loss', 'grad_x': 'grad_x', 'grad_norm_mix': 'grad_w', 'grad_w_in': 'grad_w', 'grad_b_in': 'grad_w', 'grad_conv_w': 'grad_w', 'grad_conv_b': 'grad_w', 'grad_w_rg_a': 'grad_w', 'grad_b_rg_a': 'grad_w', 'grad_w_rg_x': 'grad_w', 'grad_b_rg_x': 'grad_w', 'grad_lru_lambda': 'grad_w', 'grad_q_norm': 'grad_w', 'grad_k_norm': 'grad_w', 'grad_w_o_rnn': 'grad_w', 'grad_w_o_att': 'grad_w', 'grad_w_out': 'grad_w', 'grad_norm_ple': 'grad_w', 'grad_w_ple_gate': 'grad_w', 'grad_b_ple_gate': 'grad_w', 'grad_w_ple': 'grad_w', 'delta_norm_mix': 'delta_w', 'delta_w_in': 'delta_w', 'delta_b_in': 'delta_w', 'delta_conv_w': 'delta_w', 'delta_conv_b': 'delta_w', 'delta_w_rg_a': 'delta_w', 'delta_b_rg_a': 'delta_w', 'delta_w_rg_x': 'delta_w', 'delta_b_rg_x': 'delta_w', 'delta_lru_lambda': 'delta_w', 'delta_q_norm': 'delta_w', 'delta_k_norm': 'delta_w', 'delta_w_o_rnn': 'delta_w', 'delta_w_o_att': 'delta_w', 'delta_w_out': 'delta_w', 'delta_norm_ple': 'delta_w', 'delta_w_ple_gate': 'delta_w', 'delta_b_ple_gate': 'delta_w', 'delta_w_ple': 'delta_w', 'new_m_norm_mix': 'new_m', 'new_m_w_in': 'new_m', 'new_m_b_in': 'new_m', 'new_m_conv_w': 'new_m', 'new_m_conv_b': 'new_m', 'new_m_w_rg_a': 'new_m', 'new_m_b_rg_a': 'new_m', 'new_m_w_rg_x': 'new_m', 'new_m_b_rg_x': 'new_m', 'new_m_lru_lambda': 'new_m', 'new_m_q_norm': 'new_m', 'new_m_k_norm': 'new_m', 'new_m_w_o_rnn': 'new_m', 'new_m_w_o_att': 'new_m', 'new_m_w_out': 'new_m', 'new_m_norm_ple': 'new_m', 'new_m_w_ple_gate': 'new_m', 'new_m_b_ple_gate': 'new_m', 'new_m_w_ple': 'new_m', 'new_v_norm_mix': 'new_v', 'new_v_w_in': 'new_v', 'new_v_b_in': 'new_v', 'new_v_conv_w': 'new_v', 'new_v_conv_b': 'new_v', 'new_v_w_rg_a': 'new_v', 'new_v_b_rg_a': 'new_v', 'new_v_w_rg_x': 'new_v', 'new_v_b_rg_x': 'new_v', 'new_v_lru_lambda': 'new_v', 'new_v_q_norm': 'new_v', 'new_v_k_norm': 'new_v', 'new_v_w_o_rnn': 'new_v', 'new_v_w_o_att': 'new_v', 'new_v_w_out': 'new_v', 'new_v_norm_ple': 'new_v', 'new_v_w_ple_gate': 'new_v', 'new_v_b_ple_gate': 'new_v', 'new_v_w_ple': 'new_v'}


def _forward(args):
    return _fwd_reference(*[args[k] for k in FWD_PARAMS])


def _output_shape():
    out = _jax.eval_shape(lambda: _forward(_fwd_setup_inputs(0)))
    return out.shape, out.dtype

N_MICROBATCH = 1
ADAM_LR = 0.001
ADAM_B1 = 0.9
ADAM_B2 = 0.999
ADAM_EPS = 1e-08
ADAM_WD = 0.01
ADAM_STEP = 10
PER_EXAMPLE_BATCH_AXIS = {'x': 0, 'p': 1, 'loss_target': 0}
SHARED_INPUTS = []
_WEIGHT_DTYPES = {'norm_mix': _jnp.float32, 'w_in': _jnp.float32, 'b_in': _jnp.float32, 'conv_w': _jnp.float32, 'conv_b': _jnp.float32, 'w_rg_a': _jnp.float32, 'b_rg_a': _jnp.float32, 'w_rg_x': _jnp.float32, 'b_rg_x': _jnp.float32, 'lru_lambda': _jnp.float32, 'q_norm': _jnp.float32, 'k_norm': _jnp.float32, 'w_o_rnn': _jnp.float32, 'w_o_att': _jnp.float32, 'w_out': _jnp.float32, 'norm_ple': _jnp.float32, 'w_ple_gate': _jnp.float32, 'b_ple_gate': _jnp.float32, 'w_ple': _jnp.float32}
MOMENT_SCALE = {'norm_mix': 2.427410e+00, 'w_in': 3.126394e-02, 'b_in': 7.934831e-01, 'conv_w': 6.143984e-01, 'conv_b': 1.691341e+00, 'w_rg_a': 4.947208e-02, 'b_rg_a': 4.415494e-02, 'w_rg_x': 8.950646e-02, 'b_rg_x': 4.016844e-01, 'lru_lambda': 1.156188e-01, 'q_norm': 5.178470e-02, 'k_norm': 5.153535e-02, 'w_o_rnn': 6.657327e-02, 'w_o_att': 1.713194e-02, 'w_out': 6.309111e-02, 'norm_ple': 9.292118e-01, 'w_ple_gate': 6.924349e-02, 'b_ple_gate': 3.310561e+00, 'w_ple': 4.237319e-01}


def _to_microbatches(a, axis):
    t = _jnp.moveaxis(a, axis, 0)
    t = t.reshape((N_MICROBATCH, t.shape[0] // N_MICROBATCH) + t.shape[1:])
    return _jnp.moveaxis(t, 1, axis + 1)


def setup_inputs(seed: int = 0) -> dict:
    inp = _fwd_setup_inputs(seed)
    key = _jax.random.fold_in(_jax.random.key(seed), 7919)
    shape, _ = _output_shape()
    out = dict(inp)
    out["loss_target"] = _jax.random.normal(_jax.random.fold_in(key, 0), shape, _jnp.float32)
    for i, name in enumerate(TWIN_WEIGHTS):
        w = inp[name].astype(_jnp.float32)
        if MOMENT_SCALE is None:
            s = _jnp.sqrt(_jnp.mean(_jnp.square(w)) + 1e-30)
        else:
            s = MOMENT_SCALE[name]
        km, kv = _jax.random.split(_jax.random.fold_in(key, i + 1))
        out[name] = w
        out["m_" + name] = s * _jax.random.normal(km, w.shape, _jnp.float32)
        out["v_" + name] = (s * s) * _jax.random.uniform(kv, w.shape, _jnp.float32, 0.5, 1.5)
    if N_MICROBATCH > 1:
        for name, axis in PER_EXAMPLE_BATCH_AXIS.items():
            out[name] = _to_microbatches(out[name], axis)
    return {'x': out['x'], 'p': out['p'], 'norm_mix': out['norm_mix'], 'w_in': out['w_in'], 'b_in': out['b_in'], 'conv_w': out['conv_w'], 'conv_b': out['conv_b'], 'w_rg_a': out['w_rg_a'], 'b_rg_a': out['b_rg_a'], 'w_rg_x': out['w_rg_x'], 'b_rg_x': out['b_rg_x'], 'lru_lambda': out['lru_lambda'], 'q_norm': out['q_norm'], 'k_norm': out['k_norm'], 'w_o_rnn': out['w_o_rnn'], 'w_o_att': out['w_o_att'], 'w_out': out['w_out'], 'norm_ple': out['norm_ple'], 'w_ple_gate': out['w_ple_gate'], 'b_ple_gate': out['b_ple_gate'], 'w_ple': out['w_ple'], 'loss_target': out['loss_target'], 'm_norm_mix': out['m_norm_mix'], 'm_w_in': out['m_w_in'], 'm_b_in': out['m_b_in'], 'm_conv_w': out['m_conv_w'], 'm_conv_b': out['m_conv_b'], 'm_w_rg_a': out['m_w_rg_a'], 'm_b_rg_a': out['m_b_rg_a'], 'm_w_rg_x': out['m_w_rg_x'], 'm_b_rg_x': out['m_b_rg_x'], 'm_lru_lambda': out['m_lru_lambda'], 'm_q_norm': out['m_q_norm'], 'm_k_norm': out['m_k_norm'], 'm_w_o_rnn': out['m_w_o_rnn'], 'm_w_o_att': out['m_w_o_att'], 'm_w_out': out['m_w_out'], 'm_norm_ple': out['m_norm_ple'], 'm_w_ple_gate': out['m_w_ple_gate'], 'm_b_ple_gate': out['m_b_ple_gate'], 'm_w_ple': out['m_w_ple'], 'v_norm_mix': out['v_norm_mix'], 'v_w_in': out['v_w_in'], 'v_b_in': out['v_b_in'], 'v_conv_w': out['v_conv_w'], 'v_conv_b': out['v_conv_b'], 'v_w_rg_a': out['v_w_rg_a'], 'v_b_rg_a': out['v_b_rg_a'], 'v_w_rg_x': out['v_w_rg_x'], 'v_b_rg_x': out['v_b_rg_x'], 'v_lru_lambda': out['v_lru_lambda'], 'v_q_norm': out['v_q_norm'], 'v_k_norm': out['v_k_norm'], 'v_w_o_rnn': out['v_w_o_rnn'], 'v_w_o_att': out['v_w_o_att'], 'v_w_out': out['v_w_out'], 'v_norm_ple': out['v_norm_ple'], 'v_w_ple_gate': out['v_w_ple_gate'], 'v_b_ple_gate': out['v_b_ple_gate'], 'v_w_ple': out['v_w_ple']}


def _loss(weights, diff, rest, loss_target):
    with _jax.named_scope("forward"):
        args = {**rest, TWIN_DIFF_INPUT: diff, **{k: w.astype(_WEIGHT_DTYPES[k]) for k, w in weights.items()}}
        y = _forward(args)
    with _jax.named_scope("loss_head"):
        err = _jnp.square(y.astype(_jnp.float32) - loss_target)
        return 0.5 * _jnp.sum(_jnp.mean(err, axis=-1)) if err.ndim else 0.5 * err


def _adamw(w, g, m, v):
    m = ADAM_B1 * m + (1.0 - ADAM_B1) * g
    v = ADAM_B2 * v + (1.0 - ADAM_B2) * _jnp.square(g)
    m_hat = m / (1.0 - ADAM_B1 ** ADAM_STEP)
    v_hat = v / (1.0 - ADAM_B2 ** ADAM_STEP)
    delta = -ADAM_LR * (m_hat / (_jnp.sqrt(v_hat) + ADAM_EPS) + ADAM_WD * w)
    return delta, m, v


def reference(x, p, norm_mix, w_in, b_in, conv_w, conv_b, w_rg_a, b_rg_a, w_rg_x, b_rg_x, lru_lambda, q_norm, k_norm, w_o_rnn, w_o_att, w_out, norm_ple, w_ple_gate, b_ple_gate, w_ple, loss_target, m_norm_mix, m_w_in, m_b_in, m_conv_w, m_conv_b, m_w_rg_a, m_b_rg_a, m_w_rg_x, m_b_rg_x, m_lru_lambda, m_q_norm, m_k_norm, m_w_o_rnn, m_w_o_att, m_w_out, m_norm_ple, m_w_ple_gate, m_b_ple_gate, m_w_ple, v_norm_mix, v_w_in, v_b_in, v_conv_w, v_conv_b, v_w_rg_a, v_b_rg_a, v_w_rg_x, v_b_rg_x, v_lru_lambda, v_q_norm, v_k_norm, v_w_o_rnn, v_w_o_att, v_w_out, v_norm_ple, v_w_ple_gate, v_b_ple_gate, v_w_ple):
    given = dict(x=x, p=p, norm_mix=norm_mix, w_in=w_in, b_in=b_in, conv_w=conv_w, conv_b=conv_b, w_rg_a=w_rg_a, b_rg_a=b_rg_a, w_rg_x=w_rg_x, b_rg_x=b_rg_x, lru_lambda=lru_lambda, q_norm=q_norm, k_norm=k_norm, w_o_rnn=w_o_rnn, w_o_att=w_o_att, w_out=w_out, norm_ple=norm_ple, w_ple_gate=w_ple_gate, b_ple_gate=b_ple_gate, w_ple=w_ple, loss_target=loss_target, m_norm_mix=m_norm_mix, m_w_in=m_w_in, m_b_in=m_b_in, m_conv_w=m_conv_w, m_conv_b=m_conv_b, m_w_rg_a=m_w_rg_a, m_b_rg_a=m_b_rg_a, m_w_rg_x=m_w_rg_x, m_b_rg_x=m_b_rg_x, m_lru_lambda=m_lru_lambda, m_q_norm=m_q_norm, m_k_norm=m_k_norm, m_w_o_rnn=m_w_o_rnn, m_w_o_att=m_w_o_att, m_w_out=m_w_out, m_norm_ple=m_norm_ple, m_w_ple_gate=m_w_ple_gate, m_b_ple_gate=m_b_ple_gate, m_w_ple=m_w_ple, v_norm_mix=v_norm_mix, v_w_in=v_w_in, v_b_in=v_b_in, v_conv_w=v_conv_w, v_conv_b=v_conv_b, v_w_rg_a=v_w_rg_a, v_b_rg_a=v_b_rg_a, v_w_rg_x=v_w_rg_x, v_b_rg_x=v_b_rg_x, v_lru_lambda=v_lru_lambda, v_q_norm=v_q_norm, v_k_norm=v_k_norm, v_w_o_rnn=v_w_o_rnn, v_w_o_att=v_w_o_att, v_w_out=v_w_out, v_norm_ple=v_norm_ple, v_w_ple_gate=v_w_ple_gate, v_b_ple_gate=v_b_ple_gate, v_w_ple=v_w_ple)
    weights = {n: given[n] for n in TWIN_WEIGHTS}
    shared = {n: given[n] for n in SHARED_INPUTS}
    per_example = {n: given[n] for n in ['x', 'p']}
    grad_fn = _jax.value_and_grad(_loss, argnums=(0, 1))

    def one_microbatch(ex, loss_target):
        ex = dict(ex)
        diff = ex.pop(TWIN_DIFF_INPUT)
        return grad_fn(weights, diff, {**shared, **ex}, loss_target)

    if N_MICROBATCH == 1:
        loss, (grad_w, grad_x) = one_microbatch(per_example, given["loss_target"])
    else:
        def body(carry, xs):
            loss_sum, grad_sum = carry
            l_k, (gw_k, gx_k) = one_microbatch(xs[0], xs[1])
            with _jax.named_scope("update"):
                return (loss_sum + l_k, _jax.tree.map(_jnp.add, grad_sum, gw_k)), gx_k

        init = (_jnp.zeros((), _jnp.float32), _jax.tree.map(_jnp.zeros_like, weights))
        (loss, grad_w), grad_x = _jax.lax.scan(body, init, (per_example, given["loss_target"]))
    with _jax.named_scope("update"):
        delta_w, new_m, new_v = {}, {}, {}
        for n in TWIN_WEIGHTS:
            delta_w[n], new_m[n], new_v[n] = _adamw(weights[n], grad_w[n], given["m_" + n], given["v_" + n])
    return (loss, grad_x, *[grad_w[n] for n in TWIN_WEIGHTS], *[delta_w[n] for n in TWIN_WEIGHTS],
            *[new_m[n] for n in TWIN_WEIGHTS], *[new_v[n] for n in TWIN_WEIGHTS])
```

```python
import functools
import math

import jax
import jax.numpy as jnp
from jax import lax
from jax.experimental import pallas as pl
from jax.experimental.pallas import tpu as pltpu

F32 = jnp.float32
BF16 = jnp.bfloat16

D = 1024
S = 2048
BL = 2
T = BL * S
NDEV = 8
PLE = 256
DR = 1280
NRB = 10
RBW = 128
CONVW = 4
LRU_C = 8.0
HD = 128
NH = 4
PATTERNS = ((128, 1), (512, 4), (2048, 16))
NG = 3
ATT = NH * HD
QKV = NG * 3 * ATT
NIN = 2 * DR + QKV + ATT + 2 * D
OFF_ZR = DR
OFF_QKV = 2 * DR
OFF_ZA = OFF_QKV + QKV
OFF_G = OFF_ZA + ATT
ROPE_THETA = 10000.0
EPS = 1e-6
SCALE = HD ** -0.5
NEG = -1e30
QB = 128

LR, B1, B2, AEPS, WD, STEP = 0.001, 0.9, 0.999, 1e-08, 0.01, 10

NSHARD_IN = NIN // NDEV
LANES = 128
BULK_ROWS = (D * NSHARD_IN + (DR // NDEV) * D + ATT * (D // NDEV) + 2 * (D // NDEV) * D + PLE * (D // NDEV)) // LANES
REP_SIZES = (NRB * RBW * RBW, NRB * RBW * RBW, D, NIN, DR, DR, DR, DR, NG * HD, NG * HD, D, D)
REP_TOTAL = sum(REP_SIZES)
REP_ROWS_DEV = 344
CONV_ROWS = 5
SMALL_ROWS = 352

VMEM_BIG = 56 * 1024 * 1024


def _cp(sem=None, vmem=None):
    return pltpu.CompilerParams(dimension_semantics=sem, vmem_limit_bytes=vmem)


def _dot(a, b):
    return jnp.dot(a, b, preferred_element_type=F32)


def _dot_nt(a, b):
    return lax.dot_general(a, b, (((1,), (1,)), ((), ())), preferred_element_type=F32)


def _dot_tn(a, b):
    return lax.dot_general(a, b, (((0,), (0,)), ((), ())), preferred_element_type=F32)


def _sigmoid(x):
    return jax.nn.sigmoid(x)


def _rmsnorm_fwd(x, gain, tm=512):
    def body(x_ref, g_ref, o_ref):
        xv = x_ref[...]
        var = jnp.mean(xv * xv, axis=-1, keepdims=True)
        o_ref[...] = (xv * lax.rsqrt(var + EPS) * g_ref[...]).astype(BF16)

    return pl.pallas_call(
        body, grid=(T // tm,), name="rmsnorm_fwd",
        in_specs=[pl.BlockSpec((tm, D), lambda i: (i, 0)), pl.BlockSpec((1, D), lambda i: (0, 0))],
        out_specs=pl.BlockSpec((tm, D), lambda i: (i, 0)),
        out_shape=jax.ShapeDtypeStruct((T, D), BF16),
        compiler_params=_cp(("parallel",)),
    )(x, gain)


def _mm_bias(a, w, bias, tm, tn, name):
    m, k = a.shape
    n = w.shape[1]

    def body(a_ref, w_ref, b_ref, o_ref):
        o_ref[...] = _dot(a_ref[...], w_ref[...]) + b_ref[...]

    return pl.pallas_call(
        body, grid=(n // tn, m // tm), name=name,
        in_specs=[pl.BlockSpec((tm, k), lambda j, i: (i, 0)),
                  pl.BlockSpec((k, tn), lambda j, i: (0, j)),
                  pl.BlockSpec((1, tn), lambda j, i: (0, j))],
        out_specs=pl.BlockSpec((tm, tn), lambda j, i: (i, j)),
        out_shape=jax.ShapeDtypeStruct((m, n), F32),
        compiler_params=_cp(("parallel", "parallel"), 40 * 1024 * 1024),
    )(a, w, bias)


def _mm_nt_acc(a, w, tm, tk, name):
    m, k = a.shape
    n = w.shape[0]

    def body(a_ref, w_ref, o_ref):
        p = _dot_nt(a_ref[...], w_ref[...])

        @pl.when(pl.program_id(1) == 0)
        def _():
            o_ref[...] = p

        @pl.when(pl.program_id(1) > 0)
        def _():
            o_ref[...] += p

    return pl.pallas_call(
        body, grid=(m // tm, k // tk), name=name,
        in_specs=[pl.BlockSpec((tm, tk), lambda i, kk: (i, kk)),
                  pl.BlockSpec((n, tk), lambda i, kk: (0, kk))],
        out_specs=pl.BlockSpec((tm, n), lambda i, kk: (i, 0)),
        out_shape=jax.ShapeDtypeStruct((m, n), F32),
        compiler_params=_cp(("parallel", "arbitrary"), 40 * 1024 * 1024),
    )(a, w)


def _mm_tn_acc(a, b, tn, tt, name, colsum=False):
    tt_all, m = a.shape
    n = b.shape[1]

    def body(a_ref, b_ref, o_ref, *rest):
        bv = b_ref[...]
        p = _dot_tn(a_ref[...], bv)
        first = pl.program_id(1) == 0

        @pl.when(first)
        def _():
            o_ref[...] = p

        @pl.when(jnp.logical_not(first))
        def _():
            o_ref[...] += p

        if colsum:
            s_ref = rest[0]
            cs = jnp.sum(bv.astype(F32), axis=0, keepdims=True)

            @pl.when(first)
            def _():
                s_ref[...] = cs

            @pl.when(jnp.logical_not(first))
            def _():
                s_ref[...] += cs

    out_shape = [jax.ShapeDtypeStruct((m, n), F32)]
    out_specs = [pl.BlockSpec((m, tn), lambda j, t: (0, j))]
    if colsum:
        out_shape.append(jax.ShapeDtypeStruct((1, n), F32))
        out_specs.append(pl.BlockSpec((1, tn), lambda j, t: (0, j)))
    res = pl.pallas_call(
        body, grid=(n // tn, tt_all // tt), name=name,
        in_specs=[pl.BlockSpec((tt, m), lambda j, t: (t, 0)),
                  pl.BlockSpec((tt, tn), lambda j, t: (t, j))],
        out_specs=out_specs, out_shape=out_shape,
        compiler_params=_cp(("parallel", "arbitrary"), 40 * 1024 * 1024),
    )(a, b)
    return res if colsum else res[0]


def _row_iota():
    return lax.broadcasted_iota(jnp.int32, (S, RBW), 0)


def _shift_down(v, d, row, fill):
    return jnp.where(row >= d, pltpu.roll(v, d, 0), fill)


def _shift_up(v, d, row, fill):
    return jnp.where(row < S - d, pltpu.roll(v, S - d, 0), fill)


def _neg_expm1(x):
    series = -x * (1.0 + x * (0.5 + x * (1.0 / 6.0 + x * (1.0 / 24.0))))
    return jnp.where(x > -0.03, series, 1.0 - jnp.exp(x))


def _softplus(x):
    return jnp.maximum(x, 0.0) + jnp.log1p(jnp.exp(-jnp.abs(x)))


def _rnn_gates(x, cw, cb, wa, ba, wx, bx, lam, row):
    xc = cb + cw[3:4, :] * x
    for j in (1, 2, 3):
        xc = xc + cw[3 - j:4 - j, :] * _shift_down(x, j, row, 0.0)
    xcb = xc.astype(BF16)
    r = _sigmoid(_dot(xcb, wa) + ba)
    i = _sigmoid(_dot(xcb, wx) + bx)
    sp = _softplus(-lam)
    log_a = (-LRU_C) * r * sp
    a = jnp.exp(log_a)
    mult = jnp.where(row == 0, 1.0, jnp.sqrt(_neg_expm1(2.0 * log_a)))
    return xc, xcb, r, i, sp, a, mult


def _rnn_fwd(proj3, conv_w, conv_b, wa, ba, wx, bx, lam):
    def body(x_ref, cw_ref, cb_ref, wa_ref, ba_ref, wx_ref, bx_ref, lam_ref, h_ref):
        row = _row_iota()
        x = x_ref[0]
        xc, _, _, i, _, a, mult = _rnn_gates(x, cw_ref[...], cb_ref[...], wa_ref[0], ba_ref[...],
                                             wx_ref[0], bx_ref[...], lam_ref[...], row)
        u = mult * (i * xc)
        d = 1
        while d < S:
            u = a * _shift_down(u, d, row, 0.0) + u
            if 2 * d < S:
                a = a * _shift_down(a, d, row, 1.0)
            d *= 2
        h_ref[0] = u

    vec = lambda: pl.BlockSpec((1, RBW), lambda b, n: (0, n))
    mat = lambda: pl.BlockSpec((1, RBW, RBW), lambda b, n: (n, 0, 0))
    return pl.pallas_call(
        body, grid=(BL, NRB), name="rnn_fwd",
        in_specs=[pl.BlockSpec((1, S, RBW), lambda b, n: (b, 0, n)),
                  pl.BlockSpec((CONVW, RBW), lambda b, n: (0, n)),
                  vec(), mat(), vec(), mat(), vec(), vec()],
        out_specs=pl.BlockSpec((1, S, RBW), lambda b, n: (b, 0, n)),
        out_shape=jax.ShapeDtypeStruct((BL, S, DR), F32),
        compiler_params=_cp(("parallel", "parallel"), 40 * 1024 * 1024),
    )(proj3, conv_w, conv_b, wa, ba, wx, bx, lam)


def _rnn_bwd(proj3, h3, dh3, conv_w, conv_b, wa, ba, wx, bx, lam):
    def body(x_ref, h_ref, dh_ref, cw_ref, cb_ref, wa_ref, ba_ref, wx_ref, bx_ref, lam_ref,
             dx_ref, dcw_ref, dcb_ref, dwa_ref, dba_ref, dwx_ref, dbx_ref, dlam_ref):
        row = _row_iota()
        x = x_ref[0]
        cw = cw_ref[...]
        wa_v = wa_ref[0]
        wx_v = wx_ref[0]
        lam_v = lam_ref[...]
        xc, xcb, r, i, sp, a, mult = _rnn_gates(x, cw, cb_ref[...], wa_v, ba_ref[...], wx_v, bx_ref[...], lam_v, row)
        h = h_ref[0]
        g = dh_ref[0]
        bcoef = _shift_up(a, 1, row, 0.0)
        d = 1
        while d < S:
            g = g + bcoef * _shift_up(g, d, row, 0.0)
            if 2 * d < S:
                bcoef = bcoef * _shift_up(bcoef, d, row, 0.0)
            d *= 2
        da = g * _shift_down(h, 1, row, 0.0)
        dmult = jnp.where(row == 0, 0.0, g * (i * xc))
        gm = g * mult
        di = gm * xc
        dxc = gm * i
        dlog_a = da * a - dmult * (a * a) / mult
        dr = dlog_a * ((-LRU_C) * sp)
        dsp = jnp.sum(dlog_a * ((-LRU_C) * r), axis=0, keepdims=True)
        dlam = dsp * (-_sigmoid(-lam_v))
        dpa = dr * r * (1.0 - r)
        dpx = di * i * (1.0 - i)
        dpab = dpa.astype(BF16)
        dpxb = dpx.astype(BF16)
        dwa = _dot_tn(xcb, dpab)
        dwx = _dot_tn(xcb, dpxb)
        dba = jnp.sum(dpa, axis=0, keepdims=True)
        dbx = jnp.sum(dpx, axis=0, keepdims=True)
        dxc = dxc + _dot_nt(dpab, wa_v) + _dot_nt(dpxb, wx_v)
        dcb = jnp.sum(dxc, axis=0, keepdims=True)
        dx = cw[3:4, :] * dxc
        dcw_rows = [None] * CONVW
        dcw_rows[3] = jnp.sum(dxc * x, axis=0, keepdims=True)
        for j in (1, 2, 3):
            dx = dx + cw[3 - j:4 - j, :] * _shift_up(dxc, j, row, 0.0)
            dcw_rows[3 - j] = jnp.sum(dxc * _shift_down(x, j, row, 0.0), axis=0, keepdims=True)
        dx_ref[0] = dx
        dcw = jnp.concatenate(dcw_rows, axis=0)
        first = pl.program_id(1) == 0

        @pl.when(first)
        def _():
            dcw_ref[...] = dcw
            dcb_ref[...] = dcb
            dwa_ref[0] = dwa
            dba_ref[...] = dba
            dwx_ref[0] = dwx
            dbx_ref[...] = dbx
            dlam_ref[...] = dlam

        @pl.when(jnp.logical_not(first))
        def _():
            dcw_ref[...] += dcw
            dcb_ref[...] += dcb
            dwa_ref[0] += dwa
            dba_ref[...] += dba
            dwx_ref[0] += dwx
            dbx_ref[...] += dbx
            dlam_ref[...] += dlam

    slab = lambda: pl.BlockSpec((1, S, RBW), lambda n, b: (b, 0, n))
    vec = lambda: pl.BlockSpec((1, RBW), lambda n, b: (0, n))
    mat = lambda: pl.BlockSpec((1, RBW, RBW), lambda n, b: (n, 0, 0))
    taps = lambda: pl.BlockSpec((CONVW, RBW), lambda n, b: (0, n))
    vshape = jax.ShapeDtypeStruct((1, DR), F32)
    mshape = jax.ShapeDtypeStruct((NRB, RBW, RBW), F32)
    return pl.pallas_call(
        body, grid=(NRB, BL), name="rnn_bwd",
        in_specs=[slab(), slab(), slab(), taps(), vec(), mat(), vec(), mat(), vec(), vec()],
        out_specs=[slab(), taps(), vec(), mat(), vec(), mat(), vec(), vec()],
        out_shape=[jax.ShapeDtypeStruct((BL, S, DR), F32), jax.ShapeDtypeStruct((CONVW, DR), F32),
                   vshape, mshape, vshape, mshape, vshape, vshape],
        compiler_params=_cp(("parallel", "arbitrary"), 48 * 1024 * 1024),
    )(proj3, h3, dh3, conv_w, conv_b, wa, ba, wx, bx, lam)


def _att_blocks(dil):
    nb = (S // dil) // QB
    return [(n, c + n * QB * dil) for c in range(dil) for n in range(nb)]


def _rows(start, dil):
    return pl.ds(start, QB) if dil == 1 else pl.ds(start, QB, stride=dil)


def _rms_head(t, gain):
    rstd = lax.rsqrt(jnp.mean(t * t, axis=-1, keepdims=True) + EPS)
    return t * rstd * gain, rstd


def _rope(t, cs, sn):
    return t * cs + pltpu.roll(t, HD // 2, 1) * sn


def _rope_t(dy, cs, sn):
    return dy * cs - pltpu.roll(dy, HD // 2, 1) * sn


def _att_prep(q_ref, k_ref, v_ref, cos_ref, sin_ref, qn, kn, dil, qs, ks, vs):
    for j, (n, st) in enumerate(_att_blocks(dil)):
        sl = _rows(st, dil)
        cs = cos_ref[sl, :]
        sn = sin_ref[sl, :]
        dst = pl.ds(j * QB, QB)
        qs[dst, :] = _rope(_rms_head(q_ref.at[0][sl, :], qn)[0], cs, sn).astype(BF16)
        ks[dst, :] = _rope(_rms_head(k_ref.at[0][sl, :], kn)[0], cs, sn).astype(BF16)
        vs[dst, :] = v_ref.at[0][sl, :].astype(BF16)


def _att_scores(qs, ks, j, n):
    q = qs[pl.ds(j * QB, QB), :]
    if n > 0:
        kr = pl.ds((j - 1) * QB, 2 * QB)
        nk = 2 * QB
    else:
        kr = pl.ds(j * QB, QB)
        nk = QB
    s = _dot_nt(q, ks[kr, :]) * SCALE
    qi = lax.broadcasted_iota(jnp.int32, (QB, nk), 0)
    kj = lax.broadcasted_iota(jnp.int32, (QB, nk), 1)
    if n > 0:
        dist = qi + QB - kj
        valid = (dist >= 0) & (dist <= QB)
    else:
        valid = qi >= kj
    return jnp.where(valid, s, NEG), q, kr


def _attn_fwd(proj3, cos_t, sin_t, q_norm, k_norm):
    def body(*refs):
        qkv_refs = refs[:9]
        cos_ref, sin_ref, qn_ref, kn_ref, att_ref, lse_ref, w_ref, qs, ks, vs, og = refs[9:]
        for g, (window, dil) in enumerate(PATTERNS):
            q_ref, k_ref, v_ref = qkv_refs[3 * g:3 * g + 3]
            _att_prep(q_ref, k_ref, v_ref, cos_ref, sin_ref, qn_ref[g:g + 1, :], kn_ref[g:g + 1, :], dil, qs, ks, vs)
            for j, (n, st) in enumerate(_att_blocks(dil)):
                s, _, kr = _att_scores(qs, ks, j, n)
                m = jnp.max(s, axis=-1, keepdims=True)
                e = jnp.exp(s - m)
                den = jnp.sum(e, axis=-1, keepdims=True)
                o = _dot(e.astype(BF16), vs[kr, :]) / den
                sl = _rows(st, dil)
                og.at[g][sl, :] = o
                lse_ref.at[g, 0][sl, :] = jnp.broadcast_to(m + jnp.log(den), (QB, HD))
        l0 = lse_ref[0, 0]
        l1 = lse_ref[1, 0]
        l2 = lse_ref[2, 0]
        mx = jnp.maximum(jnp.maximum(l0, l1), l2)
        e0 = jnp.exp(l0 - mx)
        e1 = jnp.exp(l1 - mx)
        e2 = jnp.exp(l2 - mx)
        inv = 1.0 / (e0 + e1 + e2)
        w0 = e0 * inv
        w1 = e1 * inv
        w2 = e2 * inv
        w_ref[0, 0] = w0
        w_ref[1, 0] = w1
        w_ref[2, 0] = w2
        att_ref[0] = w0 * og[0] + w1 * og[1] + w2 * og[2]

    def qkv_spec(g, j):
        base = OFF_QKV // HD + g * 12 + j * 4
        return pl.BlockSpec((1, S, HD), lambda b, h: (b, 0, base + h))

    in_specs = [qkv_spec(g, j) for g in range(NG) for j in range(3)]
    in_specs += [pl.BlockSpec((S, HD), lambda b, h: (0, 0)), pl.BlockSpec((S, HD), lambda b, h: (0, 0)),
                 pl.BlockSpec((NG, HD), lambda b, h: (0, 0)), pl.BlockSpec((NG, HD), lambda b, h: (0, 0))]
    stat = lambda: pl.BlockSpec((NG, 1, S, HD), lambda b, h: (0, b, 0, h))
    return pl.pallas_call(
        body, grid=(BL, NH), name="attn_fwd",
        in_specs=in_specs,
        out_specs=[pl.BlockSpec((1, S, HD), lambda b, h: (b, 0, h)), stat(), stat()],
        out_shape=[jax.ShapeDtypeStruct((BL, S, ATT), F32),
                   jax.ShapeDtypeStruct((NG, BL, S, ATT), F32),
                   jax.ShapeDtypeStruct((NG, BL, S, ATT), F32)],
        scratch_shapes=[pltpu.VMEM((S, HD), BF16), pltpu.VMEM((S, HD), BF16), pltpu.VMEM((S, HD), BF16),
                        pltpu.VMEM((NG, S, HD), F32)],
        compiler_params=_cp(("parallel", "parallel"), VMEM_BIG),
    )(*([proj3] * 9), cos_t, sin_t, q_norm, k_norm)


def _attn_bwd_group(g, proj3, cos_t, sin_t, qn_g, kn_g, lse, wts, datt3, sbar3):
    dil = PATTERNS[g][1]
    blocks = _att_blocks(dil)

    def norm_rope_bwd(dpost, raw, gain, cs, sn):
        dn = _rope_t(dpost, cs, sn)
        rstd = lax.rsqrt(jnp.mean(raw * raw, axis=-1, keepdims=True) + EPS)
        xh = raw * rstd
        dgain = jnp.sum(dn * xh, axis=0, keepdims=True)
        gd = dn * gain
        draw = rstd * (gd - xh * jnp.mean(gd * xh, axis=-1, keepdims=True))
        return draw, dgain

    def body(q_ref, k_ref, v_ref, cos_ref, sin_ref, qn_ref, kn_ref, lse_ref, w_ref, datt_ref, sbar_ref,
             dq_ref, dk_ref, dv_ref, dqn_ref, dkn_ref, qs, ks, vs, dks, dvs, dos, cvs):
        qn = qn_ref[...]
        kn = kn_ref[...]
        _att_prep(q_ref, k_ref, v_ref, cos_ref, sin_ref, qn, kn, dil, qs, ks, vs)
        wv = w_ref[0, 0]
        dos[...] = wv * datt_ref[0]
        cvs[...] = wv * sbar_ref[0]
        dks[...] = jnp.zeros((S, HD), F32)
        dvs[...] = jnp.zeros((S, HD), F32)
        dqn = jnp.zeros((1, HD), F32)
        dkn = jnp.zeros((1, HD), F32)
        for j, (n, st) in enumerate(blocks):
            sl = _rows(st, dil)
            s, q, kr = _att_scores(qs, ks, j, n)
            p = jnp.exp(s - lse_ref.at[0, 0][sl, :][:, 0:1])
            dob = dos[sl, :].astype(BF16)
            dp = _dot_nt(dob, vs[kr, :])
            ds = p * (dp - cvs[sl, :][:, 0:1])
            dsb = ds.astype(BF16)
            dvs[kr, :] += _dot_tn(p.astype(BF16), dob)
            dks[kr, :] += _dot_tn(dsb, q) * SCALE
            dq_post = _dot(dsb, ks[kr, :]) * SCALE
            draw, dgain = norm_rope_bwd(dq_post, q_ref.at[0][sl, :], qn, cos_ref[sl, :], sin_ref[sl, :])
            dq_ref.at[0][sl, :] = draw
            dqn = dqn + dgain
        for j, (n, st) in enumerate(blocks):
            sl = _rows(st, dil)
            src = pl.ds(j * QB, QB)
            draw, dgain = norm_rope_bwd(dks[src, :], k_ref.at[0][sl, :], kn, cos_ref[sl, :], sin_ref[sl, :])
            dk_ref.at[0][sl, :] = draw
            dkn = dkn + dgain
            dv_ref.at[0][sl, :] = dvs[src, :]
        first = (pl.program_id(0) == 0) & (pl.program_id(1) == 0)

        @pl.when(first)
        def _():
            dqn_ref[...] = dqn
            dkn_ref[...] = dkn

        @pl.when(jnp.logical_not(first))
        def _():
            dqn_ref[...] += dqn
            dkn_ref[...] += dkn

    def qkv_spec(j):
        base = OFF_QKV // HD + g * 12 + j * 4
        return pl.BlockSpec((1, S, HD), lambda b, h: (b, 0, base + h))

    full = lambda r: pl.BlockSpec((r, HD), lambda b, h: (0, 0))
    stat = lambda: pl.BlockSpec((1, 1, S, HD), lambda b, h: (g, b, 0, h))
    slab = lambda: pl.BlockSpec((1, S, HD), lambda b, h: (b, 0, h))
    big = jax.ShapeDtypeStruct((BL, S, ATT), F32)
    vecs = jax.ShapeDtypeStruct((1, HD), F32)
    return pl.pallas_call(
        body, grid=(BL, NH), name="attn_bwd_g%d" % g,
        in_specs=[qkv_spec(0), qkv_spec(1), qkv_spec(2), full(S), full(S), full(1), full(1),
                  stat(), stat(), slab(), slab()],
        out_specs=[slab(), slab(), slab(), full(1), full(1)],
        out_shape=[big, big, big, vecs, vecs],
        scratch_shapes=[pltpu.VMEM((S, HD), BF16), pltpu.VMEM((S, HD), BF16), pltpu.VMEM((S, HD), BF16),
                        pltpu.VMEM((S, HD), F32), pltpu.VMEM((S, HD), F32),
                        pltpu.VMEM((S, HD), F32), pltpu.VMEM((S, HD), F32)],
        compiler_params=_cp(("arbitrary", "arbitrary"), VMEM_BIG),
    )(proj3, proj3, proj3, cos_t, sin_t, qn_g, kn_g, lse, wts, datt3, sbar3)


def _tail(x, proj, h, att, p, tgt, w_o_rnn, w_o_att, w_out, w_pg, w_ple, norm_ple, b_pg, tm=256):
    nt = T // tm
    inv_d = 1.0 / D

    def body(x_ref, h_ref, zr_ref, att_ref, za_ref, g0a_ref, g0b_ref, g1a_ref, g1b_ref, p_ref, tgt_ref,
             np_ref, bpg_ref, wor_hbm, woa_hbm, wout_hbm, wpg_hbm, wple_hbm,
             dx1_ref, merged_ref, n1_ref, dpre_ref, dpe_ref, dyr_ref, dya_ref, dgp_ref, dh_ref, dzr_ref,
             datt_ref, sbar_ref, dza_ref, yrnn_ref, yatt_ref, loss_ref, dnp_ref, dbpg_ref,
             wor, woa, wout, wpg, wple):
        first = pl.program_id(0) == 0

        @pl.when(first)
        def _():
            pltpu.sync_copy(wor_hbm, wor)
            pltpu.sync_copy(woa_hbm, woa)
            pltpu.sync_copy(wout_hbm, wout)
            pltpu.sync_copy(wpg_hbm, wpg)
            pltpu.sync_copy(wple_hbm, wple)

        xv = x_ref[...]
        hv = h_ref[...]
        zr = zr_ref[...]
        av = att_ref[...]
        za = za_ref[...]
        szr = _sigmoid(zr)
        silu_r = zr * szr
        yrnn_b = (hv * silu_r).astype(BF16)
        sza = _sigmoid(za)
        silu_a = za * sza
        yatt_b = (av * silu_a).astype(BF16)
        yrnn_ref[...] = yrnn_b
        yatt_ref[...] = yatt_b
        yr = _dot(yrnn_b, wor[...])
        ya = _dot(yatt_b, woa[...])
        g0 = _sigmoid(jnp.concatenate([g0a_ref[...], g0b_ref[...]], axis=1))
        g1 = _sigmoid(jnp.concatenate([g1a_ref[...], g1b_ref[...]], axis=1))
        merged_b = (g0 * yr + g1 * ya).astype(BF16)
        merged_ref[...] = merged_b
        x1 = xv + _dot(merged_b, wout[...])
        rstd = lax.rsqrt(jnp.mean(x1 * x1, axis=-1, keepdims=True) + EPS)
        xh = x1 * rstd
        npl = np_ref[...]
        n1_b = (xh * npl).astype(BF16)
        n1_ref[...] = n1_b
        pg = _sigmoid(_dot(n1_b, wpg[...]) + bpg_ref[...])
        pe = _dot(p_ref[...].astype(BF16), wple[...])
        err = x1 + pg * pe - tgt_ref[...]
        loss_t = 0.5 * inv_d * jnp.sum(err * err)
        dy = err * inv_d
        dpe_ref[...] = (dy * pg).astype(BF16)
        dpre = dy * pe * pg * (1.0 - pg)
        dpre_b = dpre.astype(BF16)
        dpre_ref[...] = dpre_b
        dn1 = _dot_nt(dpre_b, wpg[...])
        dnp = jnp.sum(dn1 * xh, axis=0, keepdims=True)
        dbpg = jnp.sum(dpre, axis=0, keepdims=True)
        gd = dn1 * npl
        dx1 = dy + rstd * (gd - xh * jnp.mean(gd * xh, axis=-1, keepdims=True))
        dx1_ref[...] = dx1
        dmerged = _dot_nt(dx1.astype(BF16), wout[...])
        dyr_b = (dmerged * g0).astype(BF16)
        dya_b = (dmerged * g1).astype(BF16)
        dyr_ref[...] = dyr_b
        dya_ref[...] = dya_b
        dgp_ref[:, 0:D] = (dmerged * yr * g0 * (1.0 - g0)).astype(BF16)
        dgp_ref[:, D:2 * D] = (dmerged * ya * g1 * (1.0 - g1)).astype(BF16)
        dyrnn = _dot_nt(dyr_b, wor[...])
        dyatt = _dot_nt(dya_b, woa[...])
        dh_ref[...] = dyrnn * silu_r
        dzr_ref[...] = (dyrnn * hv * szr * (1.0 + zr * (1.0 - szr))).astype(BF16)
        datt = dyatt * silu_a
        datt_ref[...] = datt
        dza_ref[...] = (dyatt * av * sza * (1.0 + za * (1.0 - sza))).astype(BF16)
        da = datt * av
        for hh in range(NH):
            seg = slice(hh * HD, (hh + 1) * HD)
            sbar_ref[:, seg] = jnp.broadcast_to(jnp.sum(da[:, seg], axis=-1, keepdims=True), (tm, HD))

        @pl.when(first)
        def _():
            loss_ref[...] = jnp.full((8, LANES), loss_t, F32)
            dnp_ref[...] = dnp
            dbpg_ref[...] = dbpg

        @pl.when(jnp.logical_not(first))
        def _():
            loss_ref[...] += jnp.full((8, LANES), loss_t, F32)
            dnp_ref[...] += dnp
            dbpg_ref[...] += dbpg

    tok = lambda w: pl.BlockSpec((tm, w), lambda i: (i, 0))
    col = lambda w, blk: pl.BlockSpec((tm, w), lambda i: (i, blk))
    vec = lambda: pl.BlockSpec((1, D), lambda i: (0, 0))
    hbm = lambda: pl.BlockSpec(memory_space=pl.ANY)
    gb = OFF_G // 512
    in_specs = [tok(D), tok(DR), col(DR, 1), tok(ATT), col(ATT, OFF_ZA // ATT),
                col(512, gb), col(512, gb + 1), col(512, gb + 2), col(512, gb + 3),
                tok(PLE), tok(D), vec(), vec(), hbm(), hbm(), hbm(), hbm(), hbm()]
    sh = lambda w, dt: jax.ShapeDtypeStruct((T, w), dt)
    out_shape = [sh(D, F32), sh(D, BF16), sh(D, BF16), sh(D, BF16), sh(D, BF16), sh(D, BF16), sh(D, BF16),
                 sh(2 * D, BF16), sh(DR, F32), sh(DR, BF16), sh(ATT, F32), sh(ATT, F32), sh(ATT, BF16),
                 sh(DR, BF16), sh(ATT, BF16),
                 jax.ShapeDtypeStruct((8, LANES), F32), jax.ShapeDtypeStruct((1, D), F32),
                 jax.ShapeDtypeStruct((1, D), F32)]
    out_specs = [tok(D), tok(D), tok(D), tok(D), tok(D), tok(D), tok(D), tok(2 * D), tok(DR), tok(DR),
                 tok(ATT), tok(ATT), tok(ATT), tok(DR), tok(ATT),
                 pl.BlockSpec((8, LANES), lambda i: (0, 0)), vec(), vec()]
    return pl.pallas_call(
        body, grid=(nt,), name="tail_fwd_bwd",
        in_specs=in_specs, out_specs=out_specs, out_shape=out_shape,
        scratch_shapes=[pltpu.VMEM((DR, D), BF16), pltpu.VMEM((ATT, D), BF16), pltpu.VMEM((D, D), BF16),
                        pltpu.VMEM((D, D), BF16), pltpu.VMEM((PLE, D), BF16)],
        compiler_params=_cp(("arbitrary",), VMEM_BIG),
    )(x, h, proj, att, proj, proj, proj, proj, proj, p, tgt, norm_ple, b_pg, w_o_rnn, w_o_att, w_out, w_pg, w_ple)


def _input_norm_bwd(x, dhn, dx1, gain, tm=512):
    def body(x_ref, dhn_ref, dx1_ref, g_ref, dx_ref, dg_ref):
        xv = x_ref[...]
        rstd = lax.rsqrt(jnp.mean(xv * xv, axis=-1, keepdims=True) + EPS)
        xh = xv * rstd
        dn = dhn_ref[...]
        dg = jnp.sum(dn * xh, axis=0, keepdims=True)
        gd = dn * g_ref[...]
        dx_ref[...] = dx1_ref[...] + rstd * (gd - xh * jnp.mean(gd * xh, axis=-1, keepdims=True))
        first = pl.program_id(0) == 0

        @pl.when(first)
        def _():
            dg_ref[...] = dg

        @pl.when(jnp.logical_not(first))
        def _():
            dg_ref[...] += dg

    tok = lambda: pl.BlockSpec((tm, D), lambda i: (i, 0))
    vec = lambda: pl.BlockSpec((1, D), lambda i: (0, 0))
    return pl.pallas_call(
        body, grid=(T // tm,), name="input_norm_bwd",
        in_specs=[tok(), tok(), tok(), vec()], out_specs=[tok(), vec()],
        out_shape=[jax.ShapeDtypeStruct((T, D), F32), jax.ShapeDtypeStruct((1, D), F32)],
        compiler_params=_cp(("arbitrary",), 40 * 1024 * 1024),
    )(x, dhn, dx1, gain)


def _rope_tables():
    pos = jnp.arange(S, dtype=F32)
    inv_freq = ROPE_THETA ** (-jnp.arange(0, HD, 2, dtype=F32) / HD)
    ang = pos[:, None] * inv_freq[None, :]
    cos, sin = jnp.cos(ang), jnp.sin(ang)
    return jnp.concatenate([cos, cos], axis=1), jnp.concatenate([-sin, sin], axis=1)


def _local_step(x, p, tgt, w_in, w_o_rnn, w_o_att, w_out, w_pg, w_ple, conv_w, norm_mix, b_in, conv_b,
                w_rg_a, b_rg_a, w_rg_x, b_rg_x, lam, q_norm, k_norm, norm_ple, b_pg):
    cos_t, sin_t = _rope_tables()
    wa_b = w_rg_a.astype(BF16)
    wx_b = w_rg_x.astype(BF16)

    hn = _rmsnorm_fwd(x, norm_mix)
    proj = _mm_bias(hn, w_in, b_in, 512, 512, "in_proj")
    proj3 = proj.reshape(BL, S, NIN)
    h3 = _rnn_fwd(proj3, conv_w, conv_b, wa_b, b_rg_a, wx_b, b_rg_x, lam)
    att3, lse, wts = _attn_fwd(proj3, cos_t, sin_t, q_norm, k_norm)
    (dx1, merged, n1, dpre, dpe, dyr, dya, dgp, dh, dzr, datt, sbar, dza, yrnn, yatt, loss8, dnp, dbpg) = _tail(
        x, proj, h3.reshape(T, DR), att3.reshape(T, ATT), p, tgt, w_o_rnn, w_o_att, w_out, w_pg, w_ple, norm_ple, b_pg)

    dxr3, dcw, dcb, dwa, dba, dwx, dbx, dlam = _rnn_bwd(
        proj3, h3, dh.reshape(BL, S, DR), conv_w, conv_b, wa_b, b_rg_a, wx_b, b_rg_x, lam)
    datt3 = datt.reshape(BL, S, ATT)
    sbar3 = sbar.reshape(BL, S, ATT)
    dqkv = []
    dqn = []
    dkn = []
    for g in range(NG):
        dq, dk, dv, dqn_g, dkn_g = _attn_bwd_group(g, proj3, cos_t, sin_t, q_norm[g:g + 1], k_norm[g:g + 1],
                                                   lse, wts, datt3, sbar3)
        dqkv += [dq.reshape(T, ATT), dk.reshape(T, ATT), dv.reshape(T, ATT)]
        dqn.append(dqn_g)
        dkn.append(dkn_g)
    dproj = jnp.concatenate([dxr3.reshape(T, DR).astype(BF16), dzr] + [t.astype(BF16) for t in dqkv] + [dza, dgp],
                            axis=1)
    dw_in, db_in = _mm_tn_acc(hn, dproj, 512, 512, "dw_in", colsum=True)
    dhn = _mm_nt_acc(dproj, w_in, 1024, 512, "dhn")
    grad_x, dnm = _input_norm_bwd(x, dhn, dx1, norm_mix)

    grads = dict(
        norm_mix=dnm, w_in=dw_in, b_in=db_in, conv_w=dcw, conv_b=dcb, w_rg_a=dwa, b_rg_a=dba, w_rg_x=dwx,
        b_rg_x=dbx, lru_lambda=dlam, q_norm=jnp.concatenate(dqn, axis=0), k_norm=jnp.concatenate(dkn, axis=0),
        w_o_rnn=_mm_tn_acc(yrnn, dyr, 512, 512, "dw_o_rnn"),
        w_o_att=_mm_tn_acc(yatt, dya, 512, 512, "dw_o_att"),
        w_out=_mm_tn_acc(merged, dx1.astype(BF16), 512, 512, "dw_out"),
        norm_ple=dnp,
        w_ple_gate=_mm_tn_acc(n1, dpre, 512, 512, "dw_ple_gate"),
        b_ple_gate=dbpg,
        w_ple=_mm_tn_acc(p.astype(BF16), dpe, 512, 512, "dw_ple"),
    )
    return loss8[0, 0], grad_x, grads


MESH = pl.DeviceIdType.MESH


def _my_pos():
    return lax.axis_index("x"), lax.axis_index("y"), lax.axis_index("c")


def _flip(pos, k):
    x, y, c = pos
    return (1 - x if k & 4 else x, 1 - y if k & 2 else y, 1 - c if k & 1 else c)


def _lin(pos):
    return 4 * pos[0] + 2 * pos[1] + pos[2]


def _all_gather_two_level(shard, name):
    r, c_ = shard.shape

    def body(x_ref, out_ref, send_sems, recv_sems, local_sem):
        me = _my_pos()
        sibling = _flip(me, 1)
        chips = [_flip(me, 4), _flip(me, 2), _flip(me, 6)]

        def copy(k, block, to, src=None):
            return pltpu.make_async_remote_copy(
                src_ref=out_ref.at[_lin(block)] if src is None else src, dst_ref=out_ref.at[_lin(block)],
                send_sem=send_sems.at[k], recv_sem=recv_sems.at[k], device_id=to, device_id_type=MESH)

        mine = pltpu.make_async_copy(x_ref, out_ref.at[_lin(me)], local_sem)
        mine.start()
        first = [copy(0, me, sibling, src=x_ref)]
        first += [copy(1 + j, me, chip, src=x_ref) for j, chip in enumerate(chips)]
        for cp in first:
            cp.start()
        passed = [copy(4 + j, chip, sibling) for j, chip in enumerate(chips)]
        for j, chip in enumerate(chips):
            copy(1 + j, chip, me).wait_recv()
            passed[j].start()
        copy(0, sibling, me).wait_recv()
        for j, chip in enumerate(chips):
            copy(4 + j, _flip(chip, 1), me).wait_recv()
        for cp in first + passed:
            cp.wait_send()
        mine.wait()

    return pl.pallas_call(
        body, name=name,
        out_shape=jax.ShapeDtypeStruct((NDEV, r, c_), shard.dtype),
        in_specs=[pl.BlockSpec(memory_space=pl.ANY)],
        out_specs=pl.BlockSpec(memory_space=pl.ANY),
        scratch_shapes=[pltpu.SemaphoreType.DMA((7,)), pltpu.SemaphoreType.DMA((7,)), pltpu.SemaphoreType.DMA],
    )(shard)


def _all_gather_direct(shard, name):
    r, c_ = shard.shape

    def body(x_ref, out_ref, send_sems, recv_sems, local_sem):
        me = _my_pos()
        mine = pltpu.make_async_copy(x_ref, out_ref.at[_lin(me)], local_sem)
        mine.start()
        sends = []
        for k in range(1, NDEV):
            cp = pltpu.make_async_remote_copy(
                src_ref=x_ref, dst_ref=out_ref.at[_lin(me)], send_sem=send_sems.at[k - 1],
                recv_sem=recv_sems.at[k - 1], device_id=_flip(me, k), device_id_type=MESH)
            cp.start()
            sends.append(cp)
        for k in range(1, NDEV):
            peer = _flip(me, k)
            pltpu.make_async_remote_copy(
                src_ref=x_ref, dst_ref=out_ref.at[_lin(peer)], send_sem=send_sems.at[k - 1],
                recv_sem=recv_sems.at[k - 1], device_id=peer, device_id_type=MESH).wait_recv()
        for cp in sends:
            cp.wait_send()
        mine.wait()

    return pl.pallas_call(
        body, name=name,
        out_shape=jax.ShapeDtypeStruct((NDEV, r, c_), shard.dtype),
        in_specs=[pl.BlockSpec(memory_space=pl.ANY)],
        out_specs=pl.BlockSpec(memory_space=pl.ANY),
        scratch_shapes=[pltpu.SemaphoreType.DMA((7,)), pltpu.SemaphoreType.DMA((7,)), pltpu.SemaphoreType.DMA],
    )(shard)


def _exchange_direct(parts, name):
    _, r, c_ = parts.shape

    def body(g_ref, out_ref, send_sems, recv_sems, local_sem):
        me = _my_pos()
        mine = pltpu.make_async_copy(g_ref.at[_lin(me)], out_ref.at[_lin(me)], local_sem)
        mine.start()
        sends = []
        for k in range(1, NDEV):
            peer = _flip(me, k)
            cp = pltpu.make_async_remote_copy(
                src_ref=g_ref.at[_lin(peer)], dst_ref=out_ref.at[_lin(me)], send_sem=send_sems.at[k - 1],
                recv_sem=recv_sems.at[k - 1], device_id=peer, device_id_type=MESH)
            cp.start()
            sends.append(cp)
        for k in range(1, NDEV):
            peer = _flip(me, k)
            pltpu.make_async_remote_copy(
                src_ref=g_ref.at[_lin(peer)], dst_ref=out_ref.at[_lin(peer)], send_sem=send_sems.at[k - 1],
                recv_sem=recv_sems.at[k - 1], device_id=peer, device_id_type=MESH).wait_recv()
        for cp in sends:
            cp.wait_send()
        mine.wait()

    return pl.pallas_call(
        body, name=name,
        out_shape=jax.ShapeDtypeStruct(parts.shape, parts.dtype),
        in_specs=[pl.BlockSpec(memory_space=pl.ANY)],
        out_specs=pl.BlockSpec(memory_space=pl.ANY),
        scratch_shapes=[pltpu.SemaphoreType.DMA((7,)), pltpu.SemaphoreType.DMA((7,)), pltpu.SemaphoreType.DMA],
    )(parts)


def _sum_blocks(parts, tr, name):
    nblk, r, c_ = parts.shape

    def body(p_ref, o_ref):
        acc = p_ref[0].astype(F32)
        for s in range(1, nblk):
            acc = acc + p_ref[s].astype(F32)
        o_ref[...] = acc

    return pl.pallas_call(
        body, grid=(r // tr,), name=name,
        in_specs=[pl.BlockSpec((nblk, tr, c_), lambda i: (0, i, 0))],
        out_specs=pl.BlockSpec((tr, c_), lambda i: (i, 0)),
        out_shape=jax.ShapeDtypeStruct((r, c_), F32),
        compiler_params=_cp(("parallel",)),
    )(parts)


def _adamw(w, g, m, v, tr, name):
    r, c_ = w.shape
    c1 = 1.0 - B1 ** STEP
    c2 = 1.0 - B2 ** STEP

    def body(w_ref, g_ref, m_ref, v_ref, d_ref, m2_ref, v2_ref):
        gv = g_ref[...]
        m2 = B1 * m_ref[...] + (1.0 - B1) * gv
        v2 = B2 * v_ref[...] + (1.0 - B2) * (gv * gv)
        m2_ref[...] = m2
        v2_ref[...] = v2
        d_ref[...] = (-LR) * ((m2 / c1) / (jnp.sqrt(v2 / c2) + AEPS) + WD * w_ref[...])

    spec = lambda: pl.BlockSpec((tr, c_), lambda i: (i, 0))
    shp = jax.ShapeDtypeStruct((r, c_), F32)
    return pl.pallas_call(
        body, grid=(r // tr,), name=name,
        in_specs=[spec(), spec(), spec(), spec()], out_specs=[spec(), spec(), spec()],
        out_shape=[shp, shp, shp], compiler_params=_cp(("parallel",)),
    )(w, g, m, v)


REP_NAMES = ("w_rg_a", "w_rg_x", "norm_mix", "b_in", "conv_b", "b_rg_a", "b_rg_x", "lru_lambda", "q_norm",
             "k_norm", "norm_ple", "b_ple_gate")
REP_SHAPES = ((1, NRB, RBW, RBW), (1, NRB, RBW, RBW), (1, D), (1, NIN), (1, DR), (1, DR), (1, DR), (1, DR),
              (1, NG, HD), (1, NG, HD), (1, D), (1, D))
BULK_NAMES = ("w_in", "w_o_rnn", "w_o_att", "w_out", "w_ple_gate", "w_ple")
REST_NAMES = BULK_NAMES[1:]
REST_ROWS = (DR // NDEV * D // LANES, ATT, D // NDEV * D // LANES, D // NDEV * D // LANES, PLE)
REST_SHAPES = ((1, DR // NDEV, D), (1, ATT, D // NDEV), (1, D // NDEV, D), (1, D // NDEV, D), (1, PLE, D // NDEV))


def _pack_rep(d):
    flat = jnp.concatenate([d[n].reshape(-1) for n in REP_NAMES])
    return jnp.pad(flat, (0, NDEV * REP_ROWS_DEV * LANES - REP_TOTAL)).reshape(NDEV * REP_ROWS_DEV, LANES)


def _unpack_rep(a):
    flat = a.reshape(-1)
    out = {}
    off = 0
    for n, sz, shp in zip(REP_NAMES, REP_SIZES, REP_SHAPES):
        out[n] = flat[off:off + sz].reshape(shp)
        off += sz
    return out


def _pack_shard_bulk(d, dtype):
    return jnp.concatenate([d[n].astype(dtype).reshape(-1, LANES) for n in BULK_NAMES], axis=0)


def _unpack_gathered_bulk(gat):
    o = 0
    r = D * NSHARD_IN // LANES
    w_in = gat[:, o:o + r].reshape(NDEV, D, NSHARD_IN).transpose(1, 0, 2).reshape(D, NIN)
    o += r
    w_o_rnn = gat[:, o:o + REST_ROWS[0]].reshape(DR, D)
    o += REST_ROWS[0]
    w_o_att = gat[:, o:o + REST_ROWS[1]].transpose(1, 0, 2).reshape(ATT, D)
    o += REST_ROWS[1]
    w_out = gat[:, o:o + REST_ROWS[2]].reshape(D, D)
    o += REST_ROWS[2]
    w_pg = gat[:, o:o + REST_ROWS[3]].reshape(D, D)
    o += REST_ROWS[3]
    w_ple = gat[:, o:o + REST_ROWS[4]].transpose(1, 0, 2).reshape(PLE, D)
    return w_in, w_o_rnn, w_o_att, w_out, w_pg, w_ple


def _pack_full_bulk(g, dtype):
    parts = [
        g["w_in"].reshape(D, NDEV, NSHARD_IN).transpose(1, 0, 2).reshape(NDEV, -1, LANES),
        g["w_o_rnn"].reshape(NDEV, -1, LANES),
        g["w_o_att"].reshape(ATT, NDEV, LANES).transpose(1, 0, 2),
        g["w_out"].reshape(NDEV, -1, LANES),
        g["w_ple_gate"].reshape(NDEV, -1, LANES),
        g["w_ple"].reshape(PLE, NDEV, LANES).transpose(1, 0, 2),
    ]
    return jnp.concatenate([t.astype(dtype) for t in parts], axis=1)


def kernel(x, p, norm_mix, w_in, b_in, conv_w, conv_b, w_rg_a, b_rg_a, w_rg_x, b_rg_x, lru_lambda, q_norm, k_norm, w_o_rnn, w_o_att, w_out, norm_ple, w_ple_gate, b_ple_gate, w_ple, loss_target, m_norm_mix, m_w_in, m_b_in, m_conv_w, m_conv_b, m_w_rg_a, m_b_rg_a, m_w_rg_x, m_b_rg_x, m_lru_lambda, m_q_norm, m_k_norm, m_w_o_rnn, m_w_o_att, m_w_out, m_norm_ple, m_w_ple_gate, m_b_ple_gate, m_w_ple, v_norm_mix, v_w_in, v_b_in, v_conv_w, v_conv_b, v_w_rg_a, v_b_rg_a, v_w_rg_x, v_b_rg_x, v_lru_lambda, v_q_norm, v_k_norm, v_w_o_rnn, v_w_o_att, v_w_out, v_norm_ple, v_w_ple_gate, v_b_ple_gate, v_w_ple):
    w = dict(norm_mix=norm_mix, w_in=w_in, b_in=b_in, conv_w=conv_w, conv_b=conv_b, w_rg_a=w_rg_a, b_rg_a=b_rg_a,
             w_rg_x=w_rg_x, b_rg_x=b_rg_x, lru_lambda=lru_lambda, q_norm=q_norm, k_norm=k_norm, w_o_rnn=w_o_rnn,
             w_o_att=w_o_att, w_out=w_out, norm_ple=norm_ple, w_ple_gate=w_ple_gate, b_ple_gate=b_ple_gate,
             w_ple=w_ple)
    m = dict(norm_mix=m_norm_mix, w_in=m_w_in, b_in=m_b_in, conv_w=m_conv_w, conv_b=m_conv_b, w_rg_a=m_w_rg_a,
             b_rg_a=m_b_rg_a, w_rg_x=m_w_rg_x, b_rg_x=m_b_rg_x, lru_lambda=m_lru_lambda, q_norm=m_q_norm,
             k_norm=m_k_norm, w_o_rnn=m_w_o_rnn, w_o_att=m_w_o_att, w_out=m_w_out, norm_ple=m_norm_ple,
             w_ple_gate=m_w_ple_gate, b_ple_gate=m_b_ple_gate, w_ple=m_w_ple)
    v = dict(norm_mix=v_norm_mix, w_in=v_w_in, b_in=v_b_in, conv_w=v_conv_w, conv_b=v_conv_b, w_rg_a=v_w_rg_a,
             b_rg_a=v_b_rg_a, w_rg_x=v_w_rg_x, b_rg_x=v_b_rg_x, lru_lambda=v_lru_lambda, q_norm=v_q_norm,
             k_norm=v_k_norm, w_o_rnn=v_w_o_rnn, w_o_att=v_w_o_att, w_out=v_w_out, norm_ple=v_norm_ple,
             w_ple_gate=v_w_ple_gate, b_ple_gate=v_b_ple_gate, w_ple=v_w_ple)
    names = list(w.keys())

    gat = _all_gather_two_level(_pack_shard_bulk(w, BF16), "gather_weights")
    w_in_f, w_o_rnn_f, w_o_att_f, w_out_f, w_pg_f, w_ple_f = _unpack_gathered_bulk(gat)
    conv_pad = jnp.pad(conv_w.reshape(CONV_ROWS, LANES), ((0, 8 - CONV_ROWS), (0, 0)))
    conv_gat = _all_gather_direct(conv_pad, "gather_conv")
    conv_f = conv_gat[:, :CONV_ROWS].reshape(NDEV, CONVW, DR // NDEV).transpose(1, 0, 2).reshape(CONVW, DR)

    loss_part, grad_x, g = _local_step(
        x.reshape(T, D), p.reshape(T, PLE), loss_target.reshape(T, D),
        w_in_f, w_o_rnn_f, w_o_att_f, w_out_f, w_pg_f, w_ple_f, conv_f,
        norm_mix, b_in, conv_b, w_rg_a[0], b_rg_a, w_rg_x[0], b_rg_x, lru_lambda, q_norm[0], k_norm[0],
        norm_ple, b_ple_gate)
    loss = lax.psum(loss_part, ("x", "y", "c"))

    bulk_recv = _exchange_direct(_pack_full_bulk(g, BF16), "scatter_bulk")
    g_bulk = _sum_blocks(bulk_recv, 864, "sum_bulk")
    rep = _pack_rep(g).reshape(NDEV, REP_ROWS_DEV, LANES)
    conv_parts = g["conv_w"].reshape(CONVW, NDEV, DR // NDEV).transpose(1, 0, 2).reshape(NDEV, CONV_ROWS, LANES)
    small = jnp.concatenate([rep, conv_parts, jnp.zeros((NDEV, SMALL_ROWS - REP_ROWS_DEV - CONV_ROWS, LANES), F32)],
                            axis=1)
    small_recv = _exchange_direct(small, "scatter_small")
    g_small = _sum_blocks(small_recv, SMALL_ROWS, "sum_small")
    rep_all = _all_gather_direct(g_small[:REP_ROWS_DEV], "gather_small")
    grad = _unpack_rep(rep_all)
    grad["conv_w"] = g_small[REP_ROWS_DEV:REP_ROWS_DEV + CONV_ROWS].reshape(1, CONVW, DR // NDEV)
    r_in = D * NSHARD_IN // LANES
    grad["w_in"] = g_bulk[:r_in].reshape(1, D, NSHARD_IN)
    off = r_in
    for n, rows, shp in zip(REST_NAMES, REST_ROWS, REST_SHAPES):
        grad[n] = g_bulk[off:off + rows].reshape(shp)
        off += rows

    delta, new_m, new_v = {}, {}, {}
    d_, m_, v_ = _adamw(w_in[0], grad["w_in"][0], m_w_in[0], v_w_in[0], 128, "adamw_w_in")
    delta["w_in"], new_m["w_in"], new_v["w_in"] = d_[None], m_[None], v_[None]

    def pack_rest(d):
        parts = [d[n].reshape(-1, LANES) for n in REST_NAMES]
        parts.append(jnp.pad(d["conv_w"].reshape(CONV_ROWS, LANES), ((0, 8 - CONV_ROWS), (0, 0))))
        return jnp.concatenate(parts, axis=0)

    rest_out = _adamw(pack_rest(w), pack_rest(grad), pack_rest(m), pack_rest(v), 1368, "adamw_rest")
    for res, dst in zip(rest_out, (delta, new_m, new_v)):
        off = 0
        for n, rows, shp in zip(REST_NAMES, REST_ROWS, REST_SHAPES):
            dst[n] = res[off:off + rows].reshape(shp)
            off += rows
        dst["conv_w"] = res[off:off + CONV_ROWS].reshape(1, CONVW, DR // NDEV)
    rep_out = _adamw(_pack_rep(w), _pack_rep(grad), _pack_rep(m), _pack_rep(v), REP_ROWS_DEV, "adamw_rep")
    for res, dst in zip(rep_out, (delta, new_m, new_v)):
        dst.update(_unpack_rep(res))

    return (loss, grad_x.reshape(BL, S, D), *[grad[n] for n in names], *[delta[n] for n in names],
            *[new_m[n] for n in names], *[new_v[n] for n in names])
```

```python
import jax
import jax.numpy as jnp
from jax import lax
from jax.experimental import pallas as pl
from jax.experimental.pallas import tpu as pltpu

F32 = jnp.float32
BF16 = jnp.bfloat16

D = 1024
S = 2048
BL = 2
T = BL * S
NDEV = 8
NCHIP = 4
PLE = 256
DR = 1280
NRB = 10
RBW = 128
CONVW = 4
LRU_C = 8.0
HD = 128
NH = 4
PATTERNS = ((128, 1), (512, 4), (2048, 16))
NG = 3
ATT = NH * HD
GW = NG * ATT
NIN = 2 * DR + 3 * GW + ATT + 2 * D
OFF_ZR = DR
OFF_Q = 2 * DR
OFF_ZA = OFF_Q + 3 * GW
OFF_G = OFF_ZA + ATT
ROPE_THETA = 10000.0
EPS = 1e-6
SCALE = HD ** -0.5
NEG = -1e30
QB = 128
LANES = 128
CT = 512
NCT = NIN // CT
A_W = 2 * DR
C_W = ATT + 2 * D

LR, B1, B2, AEPS, WD, STEP = 0.001, 0.9, 0.999, 1e-08, 0.01, 10

NSHARD_IN = NIN // NDEV
REP_NAMES = ("w_rg_a", "w_rg_x", "norm_mix", "b_in", "conv_b", "b_rg_a", "b_rg_x", "lru_lambda", "q_norm",
             "k_norm", "norm_ple", "b_ple_gate")
REP_ROWS = (NRB * RBW, NRB * RBW, D // LANES, NIN // LANES, DR // LANES, DR // LANES, DR // LANES, DR // LANES,
            NG, NG, D // LANES, D // LANES)
REP_TOTAL_ROWS = sum(REP_ROWS)
REP_ROWS_DEV = 344
BIG_NAMES = ("w_in", "w_o_rnn", "w_o_att", "w_out", "w_ple_gate", "w_ple")

VMEM_BIG = 56 * 1024 * 1024
VMEM_MID = 40 * 1024 * 1024


def _cp(sem=None, vmem=None):
    return pltpu.CompilerParams(dimension_semantics=sem, vmem_limit_bytes=vmem)


def _dot(a, b):
    return jnp.dot(a, b, preferred_element_type=F32)


def _dot_nt(a, b):
    return lax.dot_general(a, b, (((1,), (1,)), ((), ())), preferred_element_type=F32)


def _dot_tn(a, b):
    return lax.dot_general(a, b, (((0,), (0,)), ((), ())), preferred_element_type=F32)


def _sigmoid(x):
    return jax.nn.sigmoid(x)


def _perm(j):
    jq = j - OFF_Q // CT
    inside = (j >= OFF_Q // CT) & (j < OFF_ZA // CT)
    return jnp.where(inside, OFF_Q // CT + (jq % 3) * 3 + jq // 3, j)


PIECES = ((0, A_W // CT), (OFF_Q // CT, GW // CT), (OFF_Q // CT + 3, GW // CT), (OFF_Q // CT + 6, GW // CT),
          (OFF_ZA // CT, C_W // CT))


def _rmsnorm_fwd(x, gain, tm=512):
    def body(x_ref, g_ref, o_ref):
        xv = x_ref[...]
        var = jnp.mean(xv * xv, axis=-1, keepdims=True)
        o_ref[...] = (xv * lax.rsqrt(var + EPS) * g_ref[...]).astype(BF16)

    return pl.pallas_call(
        body, grid=(T // tm,), name="rmsnorm_fwd",
        in_specs=[pl.BlockSpec((tm, D), lambda i: (i, 0)), pl.BlockSpec((1, D), lambda i: (0, 0))],
        out_specs=pl.BlockSpec((tm, D), lambda i: (i, 0)),
        out_shape=jax.ShapeDtypeStruct((T, D), BF16),
        compiler_params=_cp(("parallel",)),
    )(x, gain)


def _in_proj(hn, w_t, bias, tm=512):
    def body(a_ref, w_ref, b_ref, o_ref):
        o_ref[...] = _dot_nt(a_ref[...], w_ref[...]) + b_ref[...]

    return pl.pallas_call(
        body, grid=(NCT, T // tm), name="in_proj",
        in_specs=[pl.BlockSpec((tm, D), lambda j, i: (i, 0)),
                  pl.BlockSpec((CT, D), lambda j, i: (_perm(j), 0)),
                  pl.BlockSpec((1, CT), lambda j, i: (0, _perm(j)))],
        out_specs=pl.BlockSpec((tm, CT), lambda j, i: (i, j)),
        out_shape=jax.ShapeDtypeStruct((T, NIN), F32),
        compiler_params=_cp(("parallel", "parallel"), VMEM_MID),
    )(hn, w_t, bias)


def _dhn(pieces, w_t, tm=1024):
    def body(a_ref, q_ref, k_ref, v_ref, c_ref, w_ref, o_ref):
        kk = pl.program_id(1)

        def step(x_ref):
            p = _dot(x_ref[...], w_ref[...])

            @pl.when(kk == 0)
            def _():
                o_ref[...] = p

            @pl.when(kk > 0)
            def _():
                o_ref[...] += p

        for x_ref, (lo, n) in zip((a_ref, q_ref, k_ref, v_ref, c_ref), PIECES):
            pl.when((kk >= lo) & (kk < lo + n))(lambda x_ref=x_ref: step(x_ref))

    def piece_spec(lo, n):
        return pl.BlockSpec((tm, CT), lambda i, kk: (i, jnp.clip(kk - lo, 0, n - 1)))

    return pl.pallas_call(
        body, grid=(T // tm, NCT), name="dhn",
        in_specs=[piece_spec(lo, n) for lo, n in PIECES] + [pl.BlockSpec((CT, D), lambda i, kk: (_perm(kk), 0))],
        out_specs=pl.BlockSpec((tm, D), lambda i, kk: (i, 0)),
        out_shape=jax.ShapeDtypeStruct((T, D), F32),
        compiler_params=_cp(("parallel", "arbitrary"), VMEM_MID),
    )(*pieces, w_t)


def _dw_in(pieces, hn, tt=512):
    nt = T // tt

    def body(a_ref, q_ref, k_ref, v_ref, c_ref, h_ref, o_ref, s_ref, acc, cs):
        j = pl.program_id(0)
        t = pl.program_id(1)

        def step(x_ref):
            xv = x_ref[...]
            p = _dot_tn(xv, h_ref[...])
            c = jnp.sum(xv.astype(F32), axis=0, keepdims=True)

            @pl.when(t == 0)
            def _():
                acc[...] = p
                cs[...] = c

            @pl.when(t > 0)
            def _():
                acc[...] += p
                cs[...] += c

        for x_ref, (lo, n) in zip((a_ref, q_ref, k_ref, v_ref, c_ref), PIECES):
            pl.when((j >= lo) & (j < lo + n))(lambda x_ref=x_ref: step(x_ref))

        @pl.when(t == nt - 1)
        def _():
            o_ref[...] = acc[...].astype(BF16)
            s_ref[...] = cs[...]

    def piece_spec(lo, n):
        def imap(j, t):
            used = (j >= lo) & (j < lo + n)
            return (jnp.where(used, t, 0), jnp.clip(j - lo, 0, n - 1))
        return pl.BlockSpec((tt, CT), imap)

    return pl.pallas_call(
        body, grid=(NCT, nt), name="dw_in",
        in_specs=[piece_spec(lo, n) for lo, n in PIECES] + [pl.BlockSpec((tt, D), lambda j, t: (t, 0))],
        out_specs=[pl.BlockSpec((CT, D), lambda j, t: (_perm(j), 0)), pl.BlockSpec((1, CT), lambda j, t: (0, _perm(j)))],
        out_shape=[jax.ShapeDtypeStruct((NIN, D), BF16), jax.ShapeDtypeStruct((1, NIN), F32)],
        scratch_shapes=[pltpu.VMEM((CT, D), F32), pltpu.VMEM((1, CT), F32)],
        compiler_params=_cp(("parallel", "arbitrary"), VMEM_MID),
    )(*pieces, hn)


def _mm_tn(a, b, ta, tt, name):
    m = a.shape[1]
    n = b.shape[1]
    nt = T // tt

    def body(a_ref, b_ref, o_ref, acc):
        t = pl.program_id(1)
        p = _dot_tn(a_ref[...].astype(BF16), b_ref[...].astype(BF16))

        @pl.when(t == 0)
        def _():
            acc[...] = p

        @pl.when(t > 0)
        def _():
            acc[...] += p

        @pl.when(t == nt - 1)
        def _():
            o_ref[...] = acc[...].astype(BF16)

    return pl.pallas_call(
        body, grid=(m // ta, nt), name=name,
        in_specs=[pl.BlockSpec((tt, ta), lambda j, t: (t, j)), pl.BlockSpec((tt, n), lambda j, t: (t, 0))],
        out_specs=pl.BlockSpec((ta, n), lambda j, t: (j, 0)),
        out_shape=jax.ShapeDtypeStruct((m, n), BF16),
        scratch_shapes=[pltpu.VMEM((ta, n), F32)],
        compiler_params=_cp(("parallel", "arbitrary"), VMEM_MID),
    )(a, b)


def _row_iota():
    return lax.broadcasted_iota(jnp.int32, (S, RBW), 0)


def _shift_down(v, d, row, fill):
    return jnp.where(row >= d, pltpu.roll(v, d, 0), fill)


def _shift_up(v, d, row, fill):
    return jnp.where(row < S - d, pltpu.roll(v, S - d, 0), fill)


def _neg_expm1(x):
    series = -x * (1.0 + x * (0.5 + x * (1.0 / 6.0 + x * (1.0 / 24.0))))
    return jnp.where(x > -0.03, series, 1.0 - jnp.exp(x))


def _softplus(x):
    return jnp.maximum(x, 0.0) + jnp.log1p(jnp.exp(-jnp.abs(x)))


def _rnn_gates(x, cw, cb, wa, ba, wx, bx, lam, row):
    xc = cb + cw[3:4, :] * x
    for j in (1, 2, 3):
        xc = xc + cw[3 - j:4 - j, :] * _shift_down(x, j, row, 0.0)
    xcb = xc.astype(BF16)
    r = _sigmoid(_dot(xcb, wa) + ba)
    i = _sigmoid(_dot(xcb, wx) + bx)
    sp = _softplus(-lam)
    log_a = (-LRU_C) * r * sp
    a = jnp.exp(log_a)
    mult = jnp.where(row == 0, 1.0, jnp.sqrt(_neg_expm1(2.0 * log_a)))
    return xc, xcb, r, i, sp, a, mult


def _rnn_fwd(proj3, conv_w, conv_b, wa, ba, wx, bx, lam):
    def body(x_ref, cw_ref, cb_ref, wa_ref, ba_ref, wx_ref, bx_ref, lam_ref, h_ref):
        row = _row_iota()
        x = x_ref[0]
        xc, _, _, i, _, a, mult = _rnn_gates(x, cw_ref[...], cb_ref[...], wa_ref[0], ba_ref[...],
                                             wx_ref[0], bx_ref[...], lam_ref[...], row)
        u = mult * (i * xc)
        d = 1
        while d < S:
            u = a * _shift_down(u, d, row, 0.0) + u
            if 2 * d < S:
                a = a * _shift_down(a, d, row, 1.0)
            d *= 2
        h_ref[0] = u

    vec = lambda: pl.BlockSpec((1, RBW), lambda b, n: (0, n))
    mat = lambda: pl.BlockSpec((1, RBW, RBW), lambda b, n: (n, 0, 0))
    return pl.pallas_call(
        body, grid=(BL, NRB), name="rnn_fwd",
        in_specs=[pl.BlockSpec((1, S, RBW), lambda b, n: (b, 0, n)),
                  pl.BlockSpec((CONVW, RBW), lambda b, n: (0, n)),
                  vec(), mat(), vec(), mat(), vec(), vec()],
        out_specs=pl.BlockSpec((1, S, RBW), lambda b, n: (b, 0, n)),
        out_shape=jax.ShapeDtypeStruct((BL, S, DR), F32),
        compiler_params=_cp(("parallel", "parallel"), VMEM_MID),
    )(proj3, conv_w, conv_b, wa, ba, wx, bx, lam)


def _rnn_bwd(proj3, h3, dh3, slab_a3, conv_w, conv_b, wa, ba, wx, bx, lam):
    def body(x_ref, h_ref, dh_ref, cw_ref, cb_ref, wa_ref, ba_ref, wx_ref, bx_ref, lam_ref, _alias,
             dx_ref, dcw_ref, dcb_ref, dwa_ref, dba_ref, dwx_ref, dbx_ref, dlam_ref):
        row = _row_iota()
        x = x_ref[0]
        cw = cw_ref[...]
        wa_v = wa_ref[0]
        wx_v = wx_ref[0]
        lam_v = lam_ref[...]
        xc, xcb, r, i, sp, a, mult = _rnn_gates(x, cw, cb_ref[...], wa_v, ba_ref[...], wx_v, bx_ref[...], lam_v, row)
        h = h_ref[0]
        g = dh_ref[0]
        bcoef = _shift_up(a, 1, row, 0.0)
        d = 1
        while d < S:
            g = g + bcoef * _shift_up(g, d, row, 0.0)
            if 2 * d < S:
                bcoef = bcoef * _shift_up(bcoef, d, row, 0.0)
            d *= 2
        da = g * _shift_down(h, 1, row, 0.0)
        dmult = jnp.where(row == 0, 0.0, g * (i * xc))
        gm = g * mult
        di = gm * xc
        dxc = gm * i
        dlog_a = da * a - dmult * (a * a) / mult
        dr = dlog_a * ((-LRU_C) * sp)
        dsp = jnp.sum(dlog_a * ((-LRU_C) * r), axis=0, keepdims=True)
        dlam = dsp * (-_sigmoid(-lam_v))
        dpa = dr * r * (1.0 - r)
        dpx = di * i * (1.0 - i)
        dpab = dpa.astype(BF16)
        dpxb = dpx.astype(BF16)
        dwa = _dot_tn(xcb, dpab)
        dwx = _dot_tn(xcb, dpxb)
        dba = jnp.sum(dpa, axis=0, keepdims=True)
        dbx = jnp.sum(dpx, axis=0, keepdims=True)
        dxc = dxc + _dot_nt(dpab, wa_v) + _dot_nt(dpxb, wx_v)
        dcb = jnp.sum(dxc, axis=0, keepdims=True)
        dx = cw[3:4, :] * dxc
        dcw_rows = [None] * CONVW
        dcw_rows[3] = jnp.sum(dxc * x, axis=0, keepdims=True)
        for j in (1, 2, 3):
            dx = dx + cw[3 - j:4 - j, :] * _shift_up(dxc, j, row, 0.0)
            dcw_rows[3 - j] = jnp.sum(dxc * _shift_down(x, j, row, 0.0), axis=0, keepdims=True)
        dx_ref[0] = dx.astype(BF16)
        dcw = jnp.concatenate(dcw_rows, axis=0)
        first = pl.program_id(1) == 0

        @pl.when(first)
        def _():
            dcw_ref[...] = dcw
            dcb_ref[...] = dcb
            dwa_ref[0] = dwa
            dba_ref[...] = dba
            dwx_ref[0] = dwx
            dbx_ref[...] = dbx
            dlam_ref[...] = dlam

        @pl.when(jnp.logical_not(first))
        def _():
            dcw_ref[...] += dcw
            dcb_ref[...] += dcb
            dwa_ref[0] += dwa
            dba_ref[...] += dba
            dwx_ref[0] += dwx
            dbx_ref[...] += dbx
            dlam_ref[...] += dlam

    slab = lambda: pl.BlockSpec((1, S, RBW), lambda n, b: (b, 0, n))
    vec = lambda: pl.BlockSpec((1, RBW), lambda n, b: (0, n))
    mat = lambda: pl.BlockSpec((1, RBW, RBW), lambda n, b: (n, 0, 0))
    taps = lambda: pl.BlockSpec((CONVW, RBW), lambda n, b: (0, n))
    vshape = jax.ShapeDtypeStruct((1, DR), F32)
    mshape = jax.ShapeDtypeStruct((NRB, RBW, RBW), F32)
    return pl.pallas_call(
        body, grid=(NRB, BL), name="rnn_bwd",
        in_specs=[slab(), slab(), slab(), taps(), vec(), mat(), vec(), mat(), vec(), vec(),
                  pl.BlockSpec(memory_space=pl.ANY)],
        out_specs=[slab(), taps(), vec(), mat(), vec(), mat(), vec(), vec()],
        out_shape=[jax.ShapeDtypeStruct((BL, S, A_W), BF16), jax.ShapeDtypeStruct((CONVW, DR), F32),
                   vshape, mshape, vshape, mshape, vshape, vshape],
        input_output_aliases={10: 0},
        compiler_params=_cp(("parallel", "arbitrary"), 48 * 1024 * 1024),
    )(proj3, h3, dh3, conv_w, conv_b, wa, ba, wx, bx, lam, slab_a3)


def _att_blocks(dil):
    nb = (S // dil) // QB
    return [(n, c + n * QB * dil) for c in range(dil) for n in range(nb)]


def _rows(start, dil):
    return pl.ds(start, QB) if dil == 1 else pl.ds(start, QB, stride=dil)


def _rms_head(t, gain):
    rstd = lax.rsqrt(jnp.mean(t * t, axis=-1, keepdims=True) + EPS)
    return t * rstd * gain


def _rope(t, cs, sn):
    return t * cs + pltpu.roll(t, HD // 2, 1) * sn


def _rope_t(dy, cs, sn):
    return dy * cs - pltpu.roll(dy, HD // 2, 1) * sn


def _att_prep(q_ref, k_ref, v_ref, cos_ref, sin_ref, qn, kn, dil, qs, ks, vs):
    for j, (n, st) in enumerate(_att_blocks(dil)):
        sl = _rows(st, dil)
        cs = cos_ref[sl, :]
        sn = sin_ref[sl, :]
        dst = pl.ds(j * QB, QB)
        qs[dst, :] = _rope(_rms_head(q_ref.at[0][sl, :], qn), cs, sn).astype(BF16)
        ks[dst, :] = _rope(_rms_head(k_ref.at[0][sl, :], kn), cs, sn).astype(BF16)
        vs[dst, :] = v_ref.at[0][sl, :].astype(BF16)


def _att_scores(qs, ks, j, n):
    q = qs[pl.ds(j * QB, QB), :]
    if n > 0:
        kr = pl.ds((j - 1) * QB, 2 * QB)
        nk = 2 * QB
    else:
        kr = pl.ds(j * QB, QB)
        nk = QB
    s = _dot_nt(q, ks[kr, :]) * SCALE
    qi = lax.broadcasted_iota(jnp.int32, (QB, nk), 0)
    kj = lax.broadcasted_iota(jnp.int32, (QB, nk), 1)
    if n > 0:
        dist = qi + QB - kj
        valid = (dist >= 0) & (dist <= QB)
    else:
        valid = qi >= kj
    return jnp.where(valid, s, NEG), q, kr


def _qkv_spec(kind, g):
    base = OFF_Q // HD + kind * (GW // HD) + g * NH
    return pl.BlockSpec((1, S, HD), lambda b, h: (b, 0, base + h))


def _attn_fwd(proj3, cos_t, sin_t, q_norm, k_norm):
    def body(*refs):
        qkv_refs = refs[:9]
        cos_ref, sin_ref, qn_ref, kn_ref, att_ref, lse_ref, w_ref, qs, ks, vs, og = refs[9:]
        for g, (window, dil) in enumerate(PATTERNS):
            q_ref, k_ref, v_ref = qkv_refs[3 * g:3 * g + 3]
            _att_prep(q_ref, k_ref, v_ref, cos_ref, sin_ref, qn_ref[g:g + 1, :], kn_ref[g:g + 1, :], dil, qs, ks, vs)
            for j, (n, st) in enumerate(_att_blocks(dil)):
                s, _, kr = _att_scores(qs, ks, j, n)
                m = jnp.max(s, axis=-1, keepdims=True)
                e = jnp.exp(s - m)
                den = jnp.sum(e, axis=-1, keepdims=True)
                o = _dot(e.astype(BF16), vs[kr, :]) / den
                sl = _rows(st, dil)
                og.at[g][sl, :] = o
                lse_ref.at[g, 0][sl, :] = jnp.broadcast_to(m + jnp.log(den), (QB, HD))
        l0 = lse_ref[0, 0]
        l1 = lse_ref[1, 0]
        l2 = lse_ref[2, 0]
        mx = jnp.maximum(jnp.maximum(l0, l1), l2)
        e0 = jnp.exp(l0 - mx)
        e1 = jnp.exp(l1 - mx)
        e2 = jnp.exp(l2 - mx)
        inv = 1.0 / (e0 + e1 + e2)
        w0 = e0 * inv
        w1 = e1 * inv
        w2 = e2 * inv
        w_ref[0, 0] = w0
        w_ref[1, 0] = w1
        w_ref[2, 0] = w2
        att_ref[0] = w0 * og[0] + w1 * og[1] + w2 * og[2]

    in_specs = [_qkv_spec(kind, g) for g in range(NG) for kind in range(3)]
    in_specs += [pl.BlockSpec((S, HD), lambda b, h: (0, 0)), pl.BlockSpec((S, HD), lambda b, h: (0, 0)),
                 pl.BlockSpec((NG, HD), lambda b, h: (0, 0)), pl.BlockSpec((NG, HD), lambda b, h: (0, 0))]
    stat = lambda: pl.BlockSpec((NG, 1, S, HD), lambda b, h: (0, b, 0, h))
    return pl.pallas_call(
        body, grid=(BL, NH), name="attn_fwd",
        in_specs=in_specs,
        out_specs=[pl.BlockSpec((1, S, HD), lambda b, h: (b, 0, h)), stat(), stat()],
        out_shape=[jax.ShapeDtypeStruct((BL, S, ATT), F32),
                   jax.ShapeDtypeStruct((NG, BL, S, ATT), F32),
                   jax.ShapeDtypeStruct((NG, BL, S, ATT), F32)],
        scratch_shapes=[pltpu.VMEM((S, HD), BF16), pltpu.VMEM((S, HD), BF16), pltpu.VMEM((S, HD), BF16),
                        pltpu.VMEM((NG, S, HD), F32)],
        compiler_params=_cp(("parallel", "parallel"), VMEM_BIG),
    )(*([proj3] * 9), cos_t, sin_t, q_norm, k_norm)


def _attn_bwd_group(g, proj3, cos_t, sin_t, qn_g, kn_g, lse, wts, datt3, sbar3, slabs):
    dil = PATTERNS[g][1]
    blocks = _att_blocks(dil)
    n_alias = 0 if slabs is None else 3

    def norm_rope_bwd(dpost, raw, gain, cs, sn):
        dn = _rope_t(dpost, cs, sn)
        rstd = lax.rsqrt(jnp.mean(raw * raw, axis=-1, keepdims=True) + EPS)
        xh = raw * rstd
        dgain = jnp.sum(dn * xh, axis=0, keepdims=True)
        gd = dn * gain
        draw = rstd * (gd - xh * jnp.mean(gd * xh, axis=-1, keepdims=True))
        return draw, dgain

    def body(*refs):
        (q_ref, k_ref, v_ref, cos_ref, sin_ref, qn_ref, kn_ref, lse_ref, w_ref, datt_ref, sbar_ref) = refs[:11]
        (dq_ref, dk_ref, dv_ref, dqn_ref, dkn_ref, qs, ks, vs, dks, dvs, dos, cvs, nat) = refs[11 + n_alias:]
        qn = qn_ref[...]
        kn = kn_ref[...]
        _att_prep(q_ref, k_ref, v_ref, cos_ref, sin_ref, qn, kn, dil, qs, ks, vs)
        wv = w_ref[0, 0]
        dos[...] = wv * datt_ref[0]
        cvs[...] = wv * sbar_ref[0]
        dks[...] = jnp.zeros((S, HD), F32)
        dvs[...] = jnp.zeros((S, HD), F32)
        dqn = jnp.zeros((1, HD), F32)
        dkn = jnp.zeros((1, HD), F32)
        for j, (n, st) in enumerate(blocks):
            sl = _rows(st, dil)
            s, q, kr = _att_scores(qs, ks, j, n)
            p = jnp.exp(s - lse_ref.at[0, 0][sl, :][:, 0:1])
            dob = dos[sl, :].astype(BF16)
            dp = _dot_nt(dob, vs[kr, :])
            ds = p * (dp - cvs[sl, :][:, 0:1])
            dsb = ds.astype(BF16)
            dvs[kr, :] += _dot_tn(p.astype(BF16), dob)
            dks[kr, :] += _dot_tn(dsb, q) * SCALE
            dq_post = _dot(dsb, ks[kr, :]) * SCALE
            draw, dgain = norm_rope_bwd(dq_post, q_ref.at[0][sl, :], qn, cos_ref[sl, :], sin_ref[sl, :])
            nat[sl, :] = draw
            dqn = dqn + dgain
        dq_ref[0] = nat[...].astype(BF16)
        for j, (n, st) in enumerate(blocks):
            sl = _rows(st, dil)
            draw, dgain = norm_rope_bwd(dks[pl.ds(j * QB, QB), :], k_ref.at[0][sl, :], kn, cos_ref[sl, :], sin_ref[sl, :])
            nat[sl, :] = draw
            dkn = dkn + dgain
        dk_ref[0] = nat[...].astype(BF16)
        for j, (n, st) in enumerate(blocks):
            nat[_rows(st, dil), :] = dvs[pl.ds(j * QB, QB), :]
        dv_ref[0] = nat[...].astype(BF16)
        first = (pl.program_id(0) == 0) & (pl.program_id(1) == 0)

        @pl.when(first)
        def _():
            dqn_ref[...] = dqn
            dkn_ref[...] = dkn

        @pl.when(jnp.logical_not(first))
        def _():
            dqn_ref[...] += dqn
            dkn_ref[...] += dkn

    full = lambda r: pl.BlockSpec((r, HD), lambda b, h: (0, 0))
    stat = lambda: pl.BlockSpec((1, 1, S, HD), lambda b, h: (g, b, 0, h))
    slab = lambda: pl.BlockSpec((1, S, HD), lambda b, h: (b, 0, h))
    out_slab = lambda: pl.BlockSpec((1, S, HD), lambda b, h: (b, 0, g * NH + h))
    big = jax.ShapeDtypeStruct((BL, S, GW), BF16)
    vecs = jax.ShapeDtypeStruct((1, HD), F32)
    in_specs = [_qkv_spec(0, g), _qkv_spec(1, g), _qkv_spec(2, g), full(S), full(S), full(1), full(1),
                stat(), stat(), slab(), slab()]
    args = [proj3, proj3, proj3, cos_t, sin_t, qn_g, kn_g, lse, wts, datt3, sbar3]
    aliases = {}
    if slabs is not None:
        in_specs += [pl.BlockSpec(memory_space=pl.ANY)] * 3
        args += list(slabs)
        aliases = {11: 0, 12: 1, 13: 2}
    return pl.pallas_call(
        body, grid=(BL, NH), name="attn_bwd_g%d" % g,
        in_specs=in_specs,
        out_specs=[out_slab(), out_slab(), out_slab(), full(1), full(1)],
        out_shape=[big, big, big, vecs, vecs],
        scratch_shapes=[pltpu.VMEM((S, HD), BF16), pltpu.VMEM((S, HD), BF16), pltpu.VMEM((S, HD), BF16),
                        pltpu.VMEM((S, HD), F32), pltpu.VMEM((S, HD), F32),
                        pltpu.VMEM((S, HD), F32), pltpu.VMEM((S, HD), F32), pltpu.VMEM((S, HD), F32)],
        input_output_aliases=aliases,
        compiler_params=_cp(("arbitrary", "arbitrary"), VMEM_BIG),
    )(*args)


def _tail(x, proj, h, att, p, tgt, w_o_rnn, w_o_att_t, w_out, w_pg, w_ple_t, norm_ple, b_pg, tm=256):
    nt = T // tm
    inv_d = 1.0 / D

    def body(x_ref, h_ref, zr_ref, att_ref, za_ref, g0a_ref, g0b_ref, g1a_ref, g1b_ref, p_ref, tgt_ref,
             np_ref, bpg_ref, wor_hbm, woa_hbm, wout_hbm, wpg_hbm, wple_hbm,
             dx1_ref, merged_ref, n1_ref, dpre_ref, dpe_ref, dyr_ref, dya_ref, slab_a_ref, slab_c_ref, dh_ref,
             datt_ref, sbar_ref, yrnn_ref, yatt_ref, loss_ref, dnp_ref, dbpg_ref,
             wor, woa, wout, wpg, wple):
        first = pl.program_id(0) == 0

        @pl.when(first)
        def _():
            pltpu.sync_copy(wor_hbm, wor)
            pltpu.sync_copy(woa_hbm, woa)
            pltpu.sync_copy(wout_hbm, wout)
            pltpu.sync_copy(wpg_hbm, wpg)
            pltpu.sync_copy(wple_hbm, wple)

        xv = x_ref[...]
        hv = h_ref[...]
        zr = zr_ref[...]
        av = att_ref[...]
        za = za_ref[...]
        szr = _sigmoid(zr)
        silu_r = zr * szr
        yrnn_b = (hv * silu_r).astype(BF16)
        sza = _sigmoid(za)
        silu_a = za * sza
        yatt_b = (av * silu_a).astype(BF16)
        yrnn_ref[...] = yrnn_b
        yatt_ref[...] = yatt_b
        yr = _dot(yrnn_b, wor[...])
        ya = _dot_nt(yatt_b, woa[...])
        g0 = _sigmoid(jnp.concatenate([g0a_ref[...], g0b_ref[...]], axis=1))
        g1 = _sigmoid(jnp.concatenate([g1a_ref[...], g1b_ref[...]], axis=1))
        merged_b = (g0 * yr + g1 * ya).astype(BF16)
        merged_ref[...] = merged_b
        x1 = xv + _dot(merged_b, wout[...])
        rstd = lax.rsqrt(jnp.mean(x1 * x1, axis=-1, keepdims=True) + EPS)
        xh = x1 * rstd
        npl = np_ref[...]
        n1_b = (xh * npl).astype(BF16)
        n1_ref[...] = n1_b
        pg = _sigmoid(_dot(n1_b, wpg[...]) + bpg_ref[...])
        pe = _dot_nt(p_ref[...].astype(BF16), wple[...])
        err = x1 + pg * pe - tgt_ref[...]
        loss_t = 0.5 * inv_d * jnp.sum(err * err)
        dy = err * inv_d
        dpe_ref[...] = (dy * pg).astype(BF16)
        dpre = dy * pe * pg * (1.0 - pg)
        dpre_b = dpre.astype(BF16)
        dpre_ref[...] = dpre_b
        dn1 = _dot_nt(dpre_b, wpg[...])
        dnp = jnp.sum(dn1 * xh, axis=0, keepdims=True)
        dbpg = jnp.sum(dpre, axis=0, keepdims=True)
        gd = dn1 * npl
        dx1 = dy + rstd * (gd - xh * jnp.mean(gd * xh, axis=-1, keepdims=True))
        dx1_ref[...] = dx1
        dmerged = _dot_nt(dx1.astype(BF16), wout[...])
        dyr_b = (dmerged * g0).astype(BF16)
        dya_b = (dmerged * g1).astype(BF16)
        dyr_ref[...] = dyr_b
        dya_ref[...] = dya_b
        slab_c_ref[:, ATT:ATT + D] = (dmerged * yr * g0 * (1.0 - g0)).astype(BF16)
        slab_c_ref[:, ATT + D:ATT + 2 * D] = (dmerged * ya * g1 * (1.0 - g1)).astype(BF16)
        dyrnn = _dot_nt(dyr_b, wor[...])
        dyatt = _dot(dya_b, woa[...])
        dh_ref[...] = dyrnn * silu_r
        slab_a_ref[...] = (dyrnn * hv * szr * (1.0 + zr * (1.0 - szr))).astype(BF16)
        datt = dyatt * silu_a
        datt_ref[...] = datt
        slab_c_ref[:, 0:ATT] = (dyatt * av * sza * (1.0 + za * (1.0 - sza))).astype(BF16)
        da = datt * av
        for hh in range(NH):
            seg = slice(hh * HD, (hh + 1) * HD)
            sbar_ref[:, seg] = jnp.broadcast_to(jnp.sum(da[:, seg], axis=-1, keepdims=True), (tm, HD))

        @pl.when(first)
        def _():
            loss_ref[...] = jnp.full((8, LANES), loss_t, F32)
            dnp_ref[...] = dnp
            dbpg_ref[...] = dbpg

        @pl.when(jnp.logical_not(first))
        def _():
            loss_ref[...] += jnp.full((8, LANES), loss_t, F32)
            dnp_ref[...] += dnp
            dbpg_ref[...] += dbpg

    tok = lambda w: pl.BlockSpec((tm, w), lambda i: (i, 0))
    col = lambda w, blk: pl.BlockSpec((tm, w), lambda i: (i, blk))
    vec = lambda: pl.BlockSpec((1, D), lambda i: (0, 0))
    hbm = lambda: pl.BlockSpec(memory_space=pl.ANY)
    gb = OFF_G // 512
    in_specs = [tok(D), tok(DR), col(DR, 1), tok(ATT), col(ATT, OFF_ZA // ATT),
                col(512, gb), col(512, gb + 1), col(512, gb + 2), col(512, gb + 3),
                tok(PLE), tok(D), vec(), vec(), hbm(), hbm(), hbm(), hbm(), hbm()]
    sh = lambda w, dt: jax.ShapeDtypeStruct((T, w), dt)
    out_shape = [sh(D, F32), sh(D, BF16), sh(D, BF16), sh(D, BF16), sh(D, BF16), sh(D, BF16), sh(D, BF16),
                 sh(A_W, BF16), sh(C_W, BF16), sh(DR, F32), sh(ATT, F32), sh(ATT, F32),
                 sh(DR, BF16), sh(ATT, BF16),
                 jax.ShapeDtypeStruct((8, LANES), F32), jax.ShapeDtypeStruct((1, D), F32),
                 jax.ShapeDtypeStruct((1, D), F32)]
    out_specs = [tok(D), tok(D), tok(D), tok(D), tok(D), tok(D), tok(D), col(DR, 1), tok(C_W), tok(DR),
                 tok(ATT), tok(ATT), tok(DR), tok(ATT),
                 pl.BlockSpec((8, LANES), lambda i: (0, 0)), vec(), vec()]
    return pl.pallas_call(
        body, grid=(nt,), name="tail_fwd_bwd",
        in_specs=in_specs, out_specs=out_specs, out_shape=out_shape,
        scratch_shapes=[pltpu.VMEM((DR, D), BF16), pltpu.VMEM((D, ATT), BF16), pltpu.VMEM((D, D), BF16),
                        pltpu.VMEM((D, D), BF16), pltpu.VMEM((D, PLE), BF16)],
        compiler_params=_cp(("arbitrary",), VMEM_BIG),
    )(x, h, proj, att, proj, proj, proj, proj, proj, p, tgt, norm_ple, b_pg, w_o_rnn, w_o_att_t, w_out, w_pg, w_ple_t)


def _input_norm_bwd(x, dhn, dx1, gain, tm=512):
    def body(x_ref, dhn_ref, dx1_ref, g_ref, dx_ref, dg_ref):
        xv = x_ref[...]
        rstd = lax.rsqrt(jnp.mean(xv * xv, axis=-1, keepdims=True) + EPS)
        xh = xv * rstd
        dn = dhn_ref[...]
        dg = jnp.sum(dn * xh, axis=0, keepdims=True)
        gd = dn * g_ref[...]
        dx_ref[...] = dx1_ref[...] + rstd * (gd - xh * jnp.mean(gd * xh, axis=-1, keepdims=True))
        first = pl.program_id(0) == 0

        @pl.when(first)
        def _():
            dg_ref[...] = dg

        @pl.when(jnp.logical_not(first))
        def _():
            dg_ref[...] += dg

    tok = lambda: pl.BlockSpec((tm, D), lambda i: (i, 0))
    vec = lambda: pl.BlockSpec((1, D), lambda i: (0, 0))
    return pl.pallas_call(
        body, grid=(T // tm,), name="input_norm_bwd",
        in_specs=[tok(), tok(), tok(), vec()], out_specs=[tok(), vec()],
        out_shape=[jax.ShapeDtypeStruct((T, D), F32), jax.ShapeDtypeStruct((1, D), F32)],
        compiler_params=_cp(("arbitrary",), VMEM_MID),
    )(x, dhn, dx1, gain)


def _rope_tables():
    pos = jnp.arange(S, dtype=F32)
    inv_freq = ROPE_THETA ** (-jnp.arange(0, HD, 2, dtype=F32) / HD)
    ang = pos[:, None] * inv_freq[None, :]
    cos, sin = jnp.cos(ang), jnp.sin(ang)
    return jnp.concatenate([cos, cos], axis=1), jnp.concatenate([-sin, sin], axis=1)


def _local_step(x, p, tgt, w_in_t, w_o_rnn, w_o_att_t, w_out, w_pg, w_ple_t, conv_w, norm_mix, b_in, conv_b,
                w_rg_a, b_rg_a, w_rg_x, b_rg_x, lam, q_norm, k_norm, norm_ple, b_pg):
    cos_t, sin_t = _rope_tables()
    wa_b = w_rg_a.astype(BF16)
    wx_b = w_rg_x.astype(BF16)

    hn = _rmsnorm_fwd(x, norm_mix)
    proj = _in_proj(hn, w_in_t, b_in)
    proj3 = proj.reshape(BL, S, NIN)
    h3 = _rnn_fwd(proj3, conv_w, conv_b, wa_b, b_rg_a, wx_b, b_rg_x, lam)
    att3, lse, wts = _attn_fwd(proj3, cos_t, sin_t, q_norm, k_norm)
    (dx1, merged, n1, dpre, dpe, dyr, dya, slab_a, slab_c, dh, datt, sbar, yrnn, yatt, loss8, dnp, dbpg) = _tail(
        x, proj, h3.reshape(T, DR), att3.reshape(T, ATT), p, tgt, w_o_rnn, w_o_att_t, w_out, w_pg, w_ple_t,
        norm_ple, b_pg)

    big = [None,
           _mm_tn(yrnn, dyr, 640, 512, "dw_o_rnn"),
           _mm_tn(dya, yatt, 512, 512, "dw_o_att_t"),
           _mm_tn(merged, dx1, 512, 512, "dw_out"),
           _mm_tn(n1, dpre, 512, 512, "dw_ple_gate"),
           _mm_tn(dpe, p, 512, 512, "dw_ple_t")]

    slab_a3, dcw, dcb, dwa, dba, dwx, dbx, dlam = _rnn_bwd(
        proj3, h3, dh.reshape(BL, S, DR), slab_a.reshape(BL, S, A_W), conv_w, conv_b, wa_b, b_rg_a, wx_b, b_rg_x, lam)
    datt3 = datt.reshape(BL, S, ATT)
    sbar3 = sbar.reshape(BL, S, ATT)
    slabs = None
    dqn = []
    dkn = []
    for g in range(NG):
        dq, dk, dv, dqn_g, dkn_g = _attn_bwd_group(g, proj3, cos_t, sin_t, q_norm[g:g + 1], k_norm[g:g + 1],
                                                   lse, wts, datt3, sbar3, slabs)
        slabs = (dq, dk, dv)
        dqn.append(dqn_g)
        dkn.append(dkn_g)
    pieces = [slab_a3.reshape(T, A_W)] + [t.reshape(T, GW) for t in slabs] + [slab_c]
    big[0], db_in = _dw_in(pieces, hn)
    dhn = _dhn(pieces, w_in_t)
    grad_x, dnm = _input_norm_bwd(x, dhn, dx1, norm_mix)

    small = dict(w_rg_a=dwa, w_rg_x=dwx, norm_mix=dnm, b_in=db_in, conv_b=dcb, b_rg_a=dba, b_rg_x=dbx,
                 lru_lambda=dlam, q_norm=dqn, k_norm=dkn, norm_ple=dnp, b_ple_gate=dbpg, conv_w=dcw)
    return loss8[0, 0], grad_x, big, small


MESH = pl.DeviceIdType.MESH
HBM_SPEC = pl.BlockSpec(memory_space=pl.ANY)


def _my_pos():
    return lax.axis_index("x"), lax.axis_index("y"), lax.axis_index("c")


def _flip(pos, k):
    x, y, c = pos
    return (1 - x if k & 4 else x, 1 - y if k & 2 else y, 1 - c if k & 1 else c)


def _lin(pos):
    return 4 * pos[0] + 2 * pos[1] + pos[2]


def _chip(pos):
    return 2 * pos[0] + pos[1]


def _all_gather_two_level(shards, name):
    na = len(shards)

    def body(*refs):
        x_refs = refs[:na]
        out_refs = refs[na:2 * na]
        send_sems, recv_sems, local_sems = refs[2 * na:]
        me = _my_pos()
        sibling = _flip(me, 1)
        chips = [_flip(me, 4), _flip(me, 2), _flip(me, 6)]

        def copy(i, k, block, to, from_x=False):
            dst = out_refs[i].at[_lin(block)]
            return pltpu.make_async_remote_copy(
                src_ref=x_refs[i] if from_x else dst, dst_ref=dst,
                send_sem=send_sems.at[7 * i + k], recv_sem=recv_sems.at[7 * i + k], device_id=to, device_id_type=MESH)

        started = []
        for i in range(na):
            mine = pltpu.make_async_copy(x_refs[i], out_refs[i].at[_lin(me)], local_sems.at[i])
            mine.start()
            started.append(mine)
        sends = []
        for i in range(na):
            cps = [copy(i, 0, me, sibling, True)] + [copy(i, 1 + j, me, chip, True) for j, chip in enumerate(chips)]
            for cp in cps:
                cp.start()
            sends += cps
        for i in range(na):
            for j, chip in enumerate(chips):
                copy(i, 1 + j, chip, me).wait_recv()
                fwd = copy(i, 4 + j, chip, sibling)
                fwd.start()
                sends.append(fwd)
        for i in range(na):
            copy(i, 0, sibling, me).wait_recv()
            for j, chip in enumerate(chips):
                copy(i, 4 + j, _flip(chip, 1), me).wait_recv()
        for cp in sends:
            cp.wait_send()
        for mine in started:
            mine.wait()

    return pl.pallas_call(
        body, name=name,
        out_shape=[jax.ShapeDtypeStruct((NDEV,) + s.shape, s.dtype) for s in shards],
        in_specs=[HBM_SPEC] * na, out_specs=[HBM_SPEC] * na,
        scratch_shapes=[pltpu.SemaphoreType.DMA((7 * na,)), pltpu.SemaphoreType.DMA((7 * na,)),
                        pltpu.SemaphoreType.DMA((na,))],
    )(*shards)


def _all_gather_direct(shard, name):
    def body(x_ref, out_ref, send_sems, recv_sems, local_sem):
        me = _my_pos()
        mine = pltpu.make_async_copy(x_ref, out_ref.at[_lin(me)], local_sem)
        mine.start()
        sends = []
        for k in range(1, NDEV):
            cp = pltpu.make_async_remote_copy(
                src_ref=x_ref, dst_ref=out_ref.at[_lin(me)], send_sem=send_sems.at[k - 1],
                recv_sem=recv_sems.at[k - 1], device_id=_flip(me, k), device_id_type=MESH)
            cp.start()
            sends.append(cp)
        for k in range(1, NDEV):
            peer = _flip(me, k)
            pltpu.make_async_remote_copy(
                src_ref=x_ref, dst_ref=out_ref.at[_lin(peer)], send_sem=send_sems.at[k - 1],
                recv_sem=recv_sems.at[k - 1], device_id=peer, device_id_type=MESH).wait_recv()
        for cp in sends:
            cp.wait_send()
        mine.wait()

    return pl.pallas_call(
        body, name=name,
        out_shape=jax.ShapeDtypeStruct((NDEV,) + shard.shape, shard.dtype),
        in_specs=[HBM_SPEC], out_specs=HBM_SPEC,
        scratch_shapes=[pltpu.SemaphoreType.DMA((7,)), pltpu.SemaphoreType.DMA((7,)), pltpu.SemaphoreType.DMA],
    )(shard)


def _exchange_within_chip(parts, name):
    na = len(parts)

    def body(*refs):
        a_refs = refs[:na]
        mine_refs = refs[na:2 * na]
        recv_refs = refs[2 * na:3 * na]
        send_sems, recv_sems, local_sems = refs[3 * na:]
        me = _my_pos()
        c = me[2]
        sibling = _flip(me, 1)
        local, remote = [], []
        for i in range(na):
            for q in range(NCHIP):
                lc = pltpu.make_async_copy(a_refs[i].at[q, c], mine_refs[i].at[q], local_sems.at[NCHIP * i + q])
                lc.start()
                local.append(lc)
                rc = pltpu.make_async_remote_copy(
                    src_ref=a_refs[i].at[q, 1 - c], dst_ref=recv_refs[i].at[q],
                    send_sem=send_sems.at[NCHIP * i + q], recv_sem=recv_sems.at[NCHIP * i + q],
                    device_id=sibling, device_id_type=MESH)
                rc.start()
                remote.append(rc)
        for rc in remote:
            rc.wait_recv()
        for rc in remote:
            rc.wait_send()
        for lc in local:
            lc.wait()

    half = [jax.ShapeDtypeStruct((NCHIP,) + a.shape[2:], a.dtype) for a in parts]
    res = pl.pallas_call(
        body, name=name, out_shape=half + half,
        in_specs=[HBM_SPEC] * na, out_specs=[HBM_SPEC] * (2 * na),
        scratch_shapes=[pltpu.SemaphoreType.DMA((NCHIP * na,)), pltpu.SemaphoreType.DMA((NCHIP * na,)),
                        pltpu.SemaphoreType.DMA((NCHIP * na,))],
    )(*parts)
    return res[:na], res[na:]


def _exchange_between_chips(parts, name):
    na = len(parts)

    def body(*refs):
        a_refs = refs[:na]
        out_refs = refs[na:2 * na]
        send_sems, recv_sems, local_sems = refs[2 * na:]
        me = _my_pos()
        myq = _chip(me)
        peers = [_flip(me, 4), _flip(me, 2), _flip(me, 6)]
        local, remote = [], []
        for i in range(na):
            lc = pltpu.make_async_copy(a_refs[i].at[myq], out_refs[i].at[myq], local_sems.at[i])
            lc.start()
            local.append(lc)
            for j, peer in enumerate(peers):
                rc = pltpu.make_async_remote_copy(
                    src_ref=a_refs[i].at[_chip(peer)], dst_ref=out_refs[i].at[myq],
                    send_sem=send_sems.at[3 * i + j], recv_sem=recv_sems.at[3 * i + j],
                    device_id=peer, device_id_type=MESH)
                rc.start()
                remote.append(rc)
        for i in range(na):
            for j, peer in enumerate(peers):
                pltpu.make_async_remote_copy(
                    src_ref=a_refs[i].at[myq], dst_ref=out_refs[i].at[_chip(peer)],
                    send_sem=send_sems.at[3 * i + j], recv_sem=recv_sems.at[3 * i + j],
                    device_id=peer, device_id_type=MESH).wait_recv()
        for rc in remote:
            rc.wait_send()
        for lc in local:
            lc.wait()

    return pl.pallas_call(
        body, name=name, out_shape=[jax.ShapeDtypeStruct(a.shape, a.dtype) for a in parts],
        in_specs=[HBM_SPEC] * na, out_specs=[HBM_SPEC] * na,
        scratch_shapes=[pltpu.SemaphoreType.DMA((3 * na,)), pltpu.SemaphoreType.DMA((3 * na,)),
                        pltpu.SemaphoreType.DMA((na,))],
    )(*parts)


def _sum_pairs(mine, theirs):
    na = len(mine)

    def body(*refs):
        for i in range(na):
            o_ref = refs[2 * na + i]
            o_ref[0] = (refs[i][0].astype(F32) + refs[na + i][0].astype(F32)).astype(o_ref.dtype)

    def spec(a):
        nd = a.ndim - 1
        return pl.BlockSpec((1,) + a.shape[1:], lambda q: (q,) + (0,) * nd)

    specs = [spec(a) for a in mine]
    return pl.pallas_call(
        body, grid=(NCHIP,), name="sum_pairs",
        in_specs=specs + specs, out_specs=specs,
        out_shape=[jax.ShapeDtypeStruct(a.shape, a.dtype) for a in mine],
        compiler_params=_cp(("parallel",), VMEM_BIG),
    )(*mine, *theirs)


def _sum_chips_rows(parts, tr, name):
    _, r, w = parts.shape

    def body(p_ref, o_ref):
        acc = p_ref[0].astype(F32)
        for q in range(1, NCHIP):
            acc = acc + p_ref[q].astype(F32)
        o_ref[...] = acc

    return pl.pallas_call(
        body, grid=(r // tr,), name=name,
        in_specs=[pl.BlockSpec((NCHIP, tr, w), lambda i: (0, i, 0))],
        out_specs=pl.BlockSpec((tr, w), lambda i: (i, 0)),
        out_shape=jax.ShapeDtypeStruct((r, w), F32),
        compiler_params=_cp(("parallel",), VMEM_MID),
    )(parts)


def _sum_chips_small(parts, transpose):
    na = len(parts)

    def body(*refs):
        for i in range(na):
            acc = refs[i][0].astype(F32)
            for q in range(1, NCHIP):
                acc = acc + refs[i][q].astype(F32)
            refs[na + i][...] = acc.T if transpose[i] else acc

    def oshape(a, tr):
        r, w = a.shape[1:]
        return (w, r) if tr else (r, w)

    full = lambda shp: pl.BlockSpec(shp, lambda: (0,) * len(shp))
    return pl.pallas_call(
        body, name="sum_chips_small",
        in_specs=[full(a.shape) for a in parts],
        out_specs=[full(oshape(a, tr)) for a, tr in zip(parts, transpose)],
        out_shape=[jax.ShapeDtypeStruct(oshape(a, tr), F32) for a, tr in zip(parts, transpose)],
        compiler_params=_cp(None, VMEM_MID),
    )(*parts)


def _rep_offsets():
    offs = []
    o = 0
    for r in REP_ROWS:
        offs.append(o)
        o += r
    return offs


def _pack_small_grads(g):
    offs = _rep_offsets()

    def body(dwa, dwx, dnm, dbin, dcb, dba, dbx, dlam, dq0, dq1, dq2, dk0, dk1, dk2, dnp, dbpg, o_ref):
        o_ref[pl.ds(REP_TOTAL_ROWS - 2, NDEV * REP_ROWS_DEV - REP_TOTAL_ROWS + 2), :] = jnp.zeros(
            (NDEV * REP_ROWS_DEV - REP_TOTAL_ROWS + 2, LANES), F32)
        for n in range(NRB):
            o_ref[pl.ds(offs[0] + n * RBW, RBW), :] = dwa[n]
            o_ref[pl.ds(offs[1] + n * RBW, RBW), :] = dwx[n]

        def put_vec(off, ref, rows):
            for k in range(rows):
                o_ref[pl.ds(off + k, 1), :] = ref[:, k * LANES:(k + 1) * LANES]

        put_vec(offs[2], dnm, REP_ROWS[2])
        put_vec(offs[3], dbin, REP_ROWS[3])
        put_vec(offs[4], dcb, REP_ROWS[4])
        put_vec(offs[5], dba, REP_ROWS[5])
        put_vec(offs[6], dbx, REP_ROWS[6])
        put_vec(offs[7], dlam, REP_ROWS[7])
        for k, ref in enumerate((dq0, dq1, dq2)):
            o_ref[pl.ds(offs[8] + k, 1), :] = ref[...]
        for k, ref in enumerate((dk0, dk1, dk2)):
            o_ref[pl.ds(offs[9] + k, 1), :] = ref[...]
        put_vec(offs[10], dnp, REP_ROWS[10])
        put_vec(offs[11], dbpg, REP_ROWS[11])

    args = [g["w_rg_a"], g["w_rg_x"], g["norm_mix"], g["b_in"], g["conv_b"], g["b_rg_a"], g["b_rg_x"],
            g["lru_lambda"], *g["q_norm"], *g["k_norm"], g["norm_ple"], g["b_ple_gate"]]
    full = lambda shp: pl.BlockSpec(shp, lambda: (0,) * len(shp))
    return pl.pallas_call(
        body, name="pack_small_grads",
        in_specs=[full(a.shape) for a in args],
        out_specs=full((NDEV * REP_ROWS_DEV, LANES)),
        out_shape=jax.ShapeDtypeStruct((NDEV * REP_ROWS_DEV, LANES), F32),
    )(*args)


def _adam_math(wv, gv, mv, vv):
    c1 = 1.0 - B1 ** STEP
    c2 = 1.0 - B2 ** STEP
    m2 = B1 * mv + (1.0 - B1) * gv
    v2 = B2 * vv + (1.0 - B2) * (gv * gv)
    delta = (-LR) * ((m2 / c1) / (jnp.sqrt(v2 / c2) + AEPS) + WD * wv)
    return delta, m2, v2


def _adamw_small(rep_flat, w, m, v):
    offs = _rep_offsets()
    n = len(REP_NAMES)

    def body(*refs):
        g_ref = refs[0]
        w_refs = refs[1:1 + n]
        m_refs = refs[1 + n:1 + 2 * n]
        v_refs = refs[1 + 2 * n:1 + 3 * n]
        outs = refs[1 + 3 * n:]
        go, do, mo, vo = outs[:n], outs[n:2 * n], outs[2 * n:3 * n], outs[3 * n:]

        def emit(i, idx, gv):
            go[i][idx] = gv
            delta, m2, v2 = _adam_math(w_refs[i][idx], gv, m_refs[i][idx], v_refs[i][idx])
            do[i][idx] = delta
            mo[i][idx] = m2
            vo[i][idx] = v2

        for i in range(n):
            if i < 2:
                for b in range(NRB):
                    emit(i, b, g_ref[pl.ds(offs[i] + b * RBW, RBW), :])
            elif REP_NAMES[i] in ("q_norm", "k_norm"):
                emit(i, slice(None), g_ref[pl.ds(offs[i], NG), :])
            else:
                gv = jnp.concatenate([g_ref[pl.ds(offs[i] + k, 1), :] for k in range(REP_ROWS[i])], axis=1)
                emit(i, slice(None), gv)

    full = lambda shp: pl.BlockSpec(shp, lambda: (0,) * len(shp))
    pspecs = [full(a.shape) for a in w]
    pshapes = [jax.ShapeDtypeStruct(a.shape, F32) for a in w]
    res = pl.pallas_call(
        body, name="adamw_small",
        in_specs=[full(rep_flat.shape)] + pspecs * 3,
        out_specs=pspecs * 4, out_shape=pshapes * 4,
        compiler_params=_cp(None, VMEM_MID),
    )(rep_flat, *w, *m, *v)
    return res[:n], res[n:2 * n], res[2 * n:3 * n], res[3 * n:]


def _adamw_rows(w, g, m, v, tr, name):
    r, c_ = w.shape

    def body(w_ref, g_ref, m_ref, v_ref, d_ref, m2_ref, v2_ref):
        delta, m2, v2 = _adam_math(w_ref[...], g_ref[...], m_ref[...], v_ref[...])
        d_ref[...] = delta
        m2_ref[...] = m2
        v2_ref[...] = v2

    spec = lambda: pl.BlockSpec((tr, c_), lambda i: (i, 0))
    shp = jax.ShapeDtypeStruct((r, c_), F32)
    return pl.pallas_call(
        body, grid=(r // tr,), name=name,
        in_specs=[spec(), spec(), spec(), spec()], out_specs=[spec(), spec(), spec()],
        out_shape=[shp, shp, shp], compiler_params=_cp(("parallel",)),
    )(w, g, m, v)


def _adamw_many(w, g, m, v):
    n = len(w)

    def body(*refs):
        for i in range(n):
            delta, m2, v2 = _adam_math(refs[i][...], refs[n + i][...], refs[2 * n + i][...], refs[3 * n + i][...])
            refs[4 * n + i][...] = delta
            refs[5 * n + i][...] = m2
            refs[6 * n + i][...] = v2

    full = lambda shp: pl.BlockSpec(shp, lambda: (0,) * len(shp))
    specs = [full(a.shape) for a in w]
    shapes = [jax.ShapeDtypeStruct(a.shape, F32) for a in w]
    res = pl.pallas_call(
        body, name="adamw_shards",
        in_specs=specs * 4, out_specs=specs * 3, out_shape=shapes * 3,
        compiler_params=_cp(None, VMEM_MID),
    )(*w, *g, *m, *v)
    return res[:n], res[n:2 * n], res[2 * n:]


def kernel(x, p, norm_mix, w_in, b_in, conv_w, conv_b, w_rg_a, b_rg_a, w_rg_x, b_rg_x, lru_lambda, q_norm, k_norm, w_o_rnn, w_o_att, w_out, norm_ple, w_ple_gate, b_ple_gate, w_ple, loss_target, m_norm_mix, m_w_in, m_b_in, m_conv_w, m_conv_b, m_w_rg_a, m_b_rg_a, m_w_rg_x, m_b_rg_x, m_lru_lambda, m_q_norm, m_k_norm, m_w_o_rnn, m_w_o_att, m_w_out, m_norm_ple, m_w_ple_gate, m_b_ple_gate, m_w_ple, v_norm_mix, v_w_in, v_b_in, v_conv_w, v_conv_b, v_w_rg_a, v_b_rg_a, v_w_rg_x, v_b_rg_x, v_lru_lambda, v_q_norm, v_k_norm, v_w_o_rnn, v_w_o_att, v_w_out, v_norm_ple, v_w_ple_gate, v_b_ple_gate, v_w_ple):
    w = dict(norm_mix=norm_mix, w_in=w_in, b_in=b_in, conv_w=conv_w, conv_b=conv_b, w_rg_a=w_rg_a, b_rg_a=b_rg_a,
             w_rg_x=w_rg_x, b_rg_x=b_rg_x, lru_lambda=lru_lambda, q_norm=q_norm, k_norm=k_norm, w_o_rnn=w_o_rnn,
             w_o_att=w_o_att, w_out=w_out, norm_ple=norm_ple, w_ple_gate=w_ple_gate, b_ple_gate=b_ple_gate,
             w_ple=w_ple)
    m = dict(norm_mix=m_norm_mix, w_in=m_w_in, b_in=m_b_in, conv_w=m_conv_w, conv_b=m_conv_b, w_rg_a=m_w_rg_a,
             b_rg_a=m_b_rg_a, w_rg_x=m_w_rg_x, b_rg_x=m_b_rg_x, lru_lambda=m_lru_lambda, q_norm=m_q_norm,
             k_norm=m_k_norm, w_o_rnn=m_w_o_rnn, w_o_att=m_w_o_att, w_out=m_w_out, norm_ple=m_norm_ple,
             w_ple_gate=m_w_ple_gate, b_ple_gate=m_b_ple_gate, w_ple=m_w_ple)
    v = dict(norm_mix=v_norm_mix, w_in=v_w_in, b_in=v_b_in, conv_w=v_conv_w, conv_b=v_conv_b, w_rg_a=v_w_rg_a,
             b_rg_a=v_b_rg_a, w_rg_x=v_w_rg_x, b_rg_x=v_b_rg_x, lru_lambda=v_lru_lambda, q_norm=v_q_norm,
             k_norm=v_k_norm, w_o_rnn=v_w_o_rnn, w_o_att=v_w_o_att, w_out=v_w_out, norm_ple=v_norm_ple,
             w_ple_gate=v_w_ple_gate, b_ple_gate=v_b_ple_gate, w_ple=v_w_ple)
    names = list(w.keys())

    shards = [w_in[0].T.astype(BF16), w_o_rnn[0].astype(BF16), w_o_att[0].T.astype(BF16), w_out[0].astype(BF16),
              w_ple_gate[0].astype(BF16), w_ple[0].T.astype(BF16), conv_w[0]]
    gat = _all_gather_two_level(shards, "gather_weights")
    w_in_t, w_o_rnn_f, w_o_att_t, w_out_f, w_pg_f, w_ple_t = [
        a.reshape((NDEV * a.shape[1], a.shape[2])) for a in gat[:6]]
    conv_f = gat[6].transpose(1, 0, 2).reshape(CONVW, DR)

    loss_part, grad_x, big, small = _local_step(
        x.reshape(T, D), p.reshape(T, PLE), loss_target.reshape(T, D),
        w_in_t, w_o_rnn_f, w_o_att_t, w_out_f, w_pg_f, w_ple_t, conv_f,
        norm_mix, b_in, conv_b, w_rg_a[0], b_rg_a, w_rg_x[0], b_rg_x, lru_lambda, q_norm[0], k_norm[0],
        norm_ple, b_ple_gate)
    loss = lax.psum(loss_part, ("x", "y", "c"))

    rep_parts = _pack_small_grads(small)
    conv_parts = small["conv_w"].reshape(CONVW, NDEV, DR // NDEV).transpose(1, 0, 2)
    parts = [a.reshape((NCHIP, 2, a.shape[0] // NDEV, a.shape[1])) for a in big]
    parts += [rep_parts.reshape(NCHIP, 2, REP_ROWS_DEV, LANES), conv_parts.reshape(NCHIP, 2, CONVW, DR // NDEV)]
    mine, theirs = _exchange_within_chip(parts, "reduce_within_chip")
    chip_sums = _sum_pairs(mine, theirs)
    recv = _exchange_between_chips(chip_sums, "reduce_between_chips")
    g_in_t = _sum_chips_rows(recv[0], 304, "sum_chips_w_in")
    g_o_rnn, g_o_att, g_out, g_pg, g_ple, g_rep, g_conv = _sum_chips_small(
        recv[1:], (False, True, False, False, True, False, False))
    rep_all = _all_gather_direct(g_rep, "gather_small").reshape(NDEV * REP_ROWS_DEV, LANES)

    grad, delta, new_m, new_v = {}, {}, {}, {}
    rep_shape = lambda a: a if a.ndim == 2 else a.reshape(a.shape[1:])
    res = _adamw_small(rep_all, [rep_shape(w[n]) for n in REP_NAMES], [rep_shape(m[n]) for n in REP_NAMES],
                       [rep_shape(v[n]) for n in REP_NAMES])
    for dst, vals in zip((grad, delta, new_m, new_v), res):
        for n, a in zip(REP_NAMES, vals):
            dst[n] = a.reshape(w[n].shape)
    g_in = g_in_t.T
    d_, m_, v_ = _adamw_rows(w_in[0], g_in, m_w_in[0], v_w_in[0], 128, "adamw_w_in")
    grad["w_in"], delta["w_in"], new_m["w_in"], new_v["w_in"] = g_in[None], d_[None], m_[None], v_[None]
    rest = ("w_o_rnn", "w_o_att", "w_out", "w_ple_gate", "w_ple", "conv_w")
    g_rest = [g_o_rnn, g_o_att, g_out, g_pg, g_ple, g_conv]
    res = _adamw_many([w[n][0] for n in rest], g_rest, [m[n][0] for n in rest], [v[n][0] for n in rest])
    for n, a in zip(rest, g_rest):
        grad[n] = a[None]
    for dst, vals in zip((delta, new_m, new_v), res):
        for n, a in zip(rest, vals):
            dst[n] = a[None]

    return (loss, grad_x.reshape(BL, S, D), *[grad[n] for n in names], *[delta[n] for n in names],
            *[new_m[n] for n in names], *[new_v[n] for n in names])
```

```python
import jax
import jax.numpy as jnp
from jax import lax
from jax.experimental import pallas as pl
from jax.experimental.pallas import tpu as pltpu

F32 = jnp.float32
BF16 = jnp.bfloat16

D = 1024
S = 2048
BL = 2
T = BL * S
NDEV = 8
NCHIP = 4
PLE = 256
DR = 1280
NRB = 10
RBW = 128
CONVW = 4
LRU_C = 8.0
HD = 128
NH = 4
PATTERNS = ((128, 1), (512, 4), (2048, 16))
NG = 3
ATT = NH * HD
GW = NG * ATT
NIN = 2 * DR + 3 * GW + ATT + 2 * D
OFF_ZR = DR
OFF_Q = 2 * DR
OFF_ZA = OFF_Q + 3 * GW
OFF_G = OFF_ZA + ATT
ROPE_THETA = 10000.0
EPS = 1e-6
SCALE = HD ** -0.5
NEG = -1e30
QB = 128
LANES = 128
CT = 512
NCT = NIN // CT
A_W = 2 * DR
C_W = ATT + 2 * D

LR, B1, B2, AEPS, WD, STEP = 0.001, 0.9, 0.999, 1e-08, 0.01, 10

NSHARD_IN = NIN // NDEV
REP_NAMES = ("w_rg_a", "w_rg_x", "norm_mix", "b_in", "conv_b", "b_rg_a", "b_rg_x", "lru_lambda", "q_norm",
             "k_norm", "norm_ple", "b_ple_gate")
REP_ROWS = (NRB * RBW, NRB * RBW, D // LANES, NIN // LANES, DR // LANES, DR // LANES, DR // LANES, DR // LANES,
            NG, NG, D // LANES, D // LANES)
REP_TOTAL_ROWS = sum(REP_ROWS)
REP_ROWS_DEV = 344
BIG_NAMES = ("w_in", "w_o_rnn", "w_o_att", "w_out", "w_ple_gate", "w_ple")

VMEM_BIG = 56 * 1024 * 1024
VMEM_MID = 40 * 1024 * 1024


def _cp(sem=None, vmem=None):
    return pltpu.CompilerParams(dimension_semantics=sem, vmem_limit_bytes=vmem)


def _dot(a, b):
    return jnp.dot(a, b, preferred_element_type=F32)


def _dot_nt(a, b):
    return lax.dot_general(a, b, (((1,), (1,)), ((), ())), preferred_element_type=F32)


def _dot_tn(a, b):
    return lax.dot_general(a, b, (((0,), (0,)), ((), ())), preferred_element_type=F32)


def _sigmoid(x):
    return jax.nn.sigmoid(x)


def _perm(j):
    jq = j - OFF_Q // CT
    inside = (j >= OFF_Q // CT) & (j < OFF_ZA // CT)
    return jnp.where(inside, OFF_Q // CT + (jq % 3) * 3 + jq // 3, j)


PIECES = ((0, A_W // CT), (OFF_Q // CT, GW // CT), (OFF_Q // CT + 3, GW // CT), (OFF_Q // CT + 6, GW // CT),
          (OFF_ZA // CT, C_W // CT))


def _rmsnorm_fwd(x, gain, tm=512):
    def body(x_ref, g_ref, o_ref):
        xv = x_ref[...]
        var = jnp.mean(xv * xv, axis=-1, keepdims=True)
        o_ref[...] = (xv * lax.rsqrt(var + EPS) * g_ref[...]).astype(BF16)

    return pl.pallas_call(
        body, grid=(T // tm,), name="rmsnorm_fwd",
        in_specs=[pl.BlockSpec((tm, D), lambda i: (i, 0)), pl.BlockSpec((1, D), lambda i: (0, 0))],
        out_specs=pl.BlockSpec((tm, D), lambda i: (i, 0)),
        out_shape=jax.ShapeDtypeStruct((T, D), BF16),
        compiler_params=_cp(("parallel",)),
    )(x, gain)


def _in_proj(hn, w_t, bias, tm=1024):
    def body(a_ref, w_ref, b_ref, o_ref):
        o_ref[...] = _dot_nt(a_ref[...], w_ref[...]) + b_ref[...]

    return pl.pallas_call(
        body, grid=(T // tm, NCT), name="in_proj",
        in_specs=[pl.BlockSpec((tm, D), lambda i, j: (i, 0)),
                  pl.BlockSpec((CT, D), lambda i, j: (_perm(j), 0)),
                  pl.BlockSpec((1, CT), lambda i, j: (0, _perm(j)))],
        out_specs=pl.BlockSpec((tm, CT), lambda i, j: (i, j)),
        out_shape=jax.ShapeDtypeStruct((T, NIN), F32),
        compiler_params=_cp(("parallel", "parallel"), VMEM_MID),
    )(hn, w_t, bias)


def _dhn(pieces, w_t, tm=512):
    def body(a_ref, q_ref, k_ref, v_ref, c_ref, w_hbm, o_ref, w):
        @pl.when(pl.program_id(0) == 0)
        def _():
            pltpu.sync_copy(w_hbm, w)

        acc = _dot(a_ref[...], w[pl.ds(0, A_W), :])
        for kind, x_ref in enumerate((q_ref, k_ref, v_ref)):
            for g in range(NG):
                row = OFF_Q + (3 * g + kind) * CT
                acc = acc + _dot(x_ref[:, g * CT:(g + 1) * CT], w[pl.ds(row, CT), :])
        o_ref[...] = acc + _dot(c_ref[...], w[pl.ds(OFF_ZA, C_W), :])

    tok = lambda wd: pl.BlockSpec((tm, wd), lambda i: (i, 0))
    return pl.pallas_call(
        body, grid=(T // tm,), name="dhn",
        in_specs=[tok(A_W), tok(GW), tok(GW), tok(GW), tok(C_W), pl.BlockSpec(memory_space=pl.ANY)],
        out_specs=tok(D),
        out_shape=jax.ShapeDtypeStruct((T, D), F32),
        scratch_shapes=[pltpu.VMEM((NIN, D), BF16)],
        compiler_params=_cp(("arbitrary",), VMEM_BIG),
    )(*pieces, w_t)


def _dw_in(pieces, hn, tt=2048):
    nt = T // tt

    def body(a_ref, q_ref, k_ref, v_ref, c_ref, h_ref, o_ref, s_ref, acc, cs):
        j = pl.program_id(0)
        t = pl.program_id(1)

        def step(x_ref):
            xv = x_ref[...]
            p = _dot_tn(xv, h_ref[...])
            c = jnp.sum(xv.astype(F32), axis=0, keepdims=True)

            @pl.when(t == 0)
            def _():
                acc[...] = p
                cs[...] = c

            @pl.when(t > 0)
            def _():
                acc[...] += p
                cs[...] += c

        for x_ref, (lo, n) in zip((a_ref, q_ref, k_ref, v_ref, c_ref), PIECES):
            pl.when((j >= lo) & (j < lo + n))(lambda x_ref=x_ref: step(x_ref))

        @pl.when(t == nt - 1)
        def _():
            o_ref[...] = acc[...].astype(BF16)
            s_ref[...] = cs[...]

    def piece_spec(lo, n):
        def imap(j, t):
            used = (j >= lo) & (j < lo + n)
            return (jnp.where(used, t, 0), jnp.clip(j - lo, 0, n - 1))
        return pl.BlockSpec((tt, CT), imap)

    return pl.pallas_call(
        body, grid=(NCT, nt), name="dw_in",
        in_specs=[piece_spec(lo, n) for lo, n in PIECES] + [pl.BlockSpec((tt, D), lambda j, t: (t, 0))],
        out_specs=[pl.BlockSpec((CT, D), lambda j, t: (_perm(j), 0)), pl.BlockSpec((1, CT), lambda j, t: (0, _perm(j)))],
        out_shape=[jax.ShapeDtypeStruct((NIN, D), BF16), jax.ShapeDtypeStruct((1, NIN), F32)],
        scratch_shapes=[pltpu.VMEM((CT, D), F32), pltpu.VMEM((1, CT), F32)],
        compiler_params=_cp(("parallel", "arbitrary"), VMEM_MID),
    )(*pieces, hn)


def _mm_tn(a, b, ta, tt, name):
    m = a.shape[1]
    n = b.shape[1]
    nt = T // tt

    def body(a_ref, b_ref, o_ref, acc):
        t = pl.program_id(1)
        p = _dot_tn(a_ref[...].astype(BF16), b_ref[...].astype(BF16))

        @pl.when(t == 0)
        def _():
            acc[...] = p

        @pl.when(t > 0)
        def _():
            acc[...] += p

        @pl.when(t == nt - 1)
        def _():
            o_ref[...] = acc[...].astype(BF16)

    return pl.pallas_call(
        body, grid=(m // ta, nt), name=name,
        in_specs=[pl.BlockSpec((tt, ta), lambda j, t: (t, j)), pl.BlockSpec((tt, n), lambda j, t: (t, 0))],
        out_specs=pl.BlockSpec((ta, n), lambda j, t: (j, 0)),
        out_shape=jax.ShapeDtypeStruct((m, n), BF16),
        scratch_shapes=[pltpu.VMEM((ta, n), F32)],
        compiler_params=_cp(("parallel", "arbitrary"), VMEM_MID),
    )(a, b)


def _row_iota():
    return lax.broadcasted_iota(jnp.int32, (S, RBW), 0)


def _shift_down(v, d, row, fill):
    return jnp.where(row >= d, pltpu.roll(v, d, 0), fill)


def _shift_up(v, d, row, fill):
    return jnp.where(row < S - d, pltpu.roll(v, S - d, 0), fill)


def _neg_expm1(x):
    series = -x * (1.0 + x * (0.5 + x * (1.0 / 6.0 + x * (1.0 / 24.0))))
    return jnp.where(x > -0.03, series, 1.0 - jnp.exp(x))


def _softplus(x):
    return jnp.maximum(x, 0.0) + jnp.log1p(jnp.exp(-jnp.abs(x)))


def _rnn_gates(x, cw, cb, wa, ba, wx, bx, lam, row):
    xc = cb + cw[3:4, :] * x
    for j in (1, 2, 3):
        xc = xc + cw[3 - j:4 - j, :] * _shift_down(x, j, row, 0.0)
    xcb = xc.astype(BF16)
    r = _sigmoid(_dot(xcb, wa) + ba)
    i = _sigmoid(_dot(xcb, wx) + bx)
    sp = _softplus(-lam)
    log_a = (-LRU_C) * r * sp
    a = jnp.exp(log_a)
    mult = jnp.where(row == 0, 1.0, jnp.sqrt(_neg_expm1(2.0 * log_a)))
    return xc, xcb, r, i, sp, a, mult


def _rnn_fwd(proj3, conv_w, conv_b, wa, ba, wx, bx, lam):
    def body(x_ref, cw_ref, cb_ref, wa_ref, ba_ref, wx_ref, bx_ref, lam_ref, h_ref):
        row = _row_iota()
        x = x_ref[0]
        xc, _, _, i, _, a, mult = _rnn_gates(x, cw_ref[...], cb_ref[...], wa_ref[0], ba_ref[...],
                                             wx_ref[0], bx_ref[...], lam_ref[...], row)
        u = mult * (i * xc)
        d = 1
        while d < S:
            u = a * _shift_down(u, d, row, 0.0) + u
            if 2 * d < S:
                a = a * _shift_down(a, d, row, 1.0)
            d *= 2
        h_ref[0] = u

    vec = lambda: pl.BlockSpec((1, RBW), lambda b, n: (0, n))
    mat = lambda: pl.BlockSpec((1, RBW, RBW), lambda b, n: (n, 0, 0))
    return pl.pallas_call(
        body, grid=(BL, NRB), name="rnn_fwd",
        in_specs=[pl.BlockSpec((1, S, RBW), lambda b, n: (b, 0, n)),
                  pl.BlockSpec((CONVW, RBW), lambda b, n: (0, n)),
                  vec(), mat(), vec(), mat(), vec(), vec()],
        out_specs=pl.BlockSpec((1, S, RBW), lambda b, n: (b, 0, n)),
        out_shape=jax.ShapeDtypeStruct((BL, S, DR), F32),
        compiler_params=_cp(("parallel", "parallel"), VMEM_MID),
    )(proj3, conv_w, conv_b, wa, ba, wx, bx, lam)


def _rnn_bwd(proj3, h3, dh3, slab_a3, conv_w, conv_b, wa, ba, wx, bx, lam):
    def body(x_ref, h_ref, dh_ref, cw_ref, cb_ref, wa_ref, ba_ref, wx_ref, bx_ref, lam_ref, _alias,
             dx_ref, dcw_ref, dcb_ref, dwa_ref, dba_ref, dwx_ref, dbx_ref, dlam_ref):
        row = _row_iota()
        x = x_ref[0]
        cw = cw_ref[...]
        wa_v = wa_ref[0]
        wx_v = wx_ref[0]
        lam_v = lam_ref[...]
        xc, xcb, r, i, sp, a, mult = _rnn_gates(x, cw, cb_ref[...], wa_v, ba_ref[...], wx_v, bx_ref[...], lam_v, row)
        h = h_ref[0]
        g = dh_ref[0]
        bcoef = _shift_up(a, 1, row, 0.0)
        d = 1
        while d < S:
            g = g + bcoef * _shift_up(g, d, row, 0.0)
            if 2 * d < S:
                bcoef = bcoef * _shift_up(bcoef, d, row, 0.0)
            d *= 2
        da = g * _shift_down(h, 1, row, 0.0)
        dmult = jnp.where(row == 0, 0.0, g * (i * xc))
        gm = g * mult
        di = gm * xc
        dxc = gm * i
        dlog_a = da * a - dmult * (a * a) / mult
        dr = dlog_a * ((-LRU_C) * sp)
        dsp = jnp.sum(dlog_a * ((-LRU_C) * r), axis=0, keepdims=True)
        dlam = dsp * (-_sigmoid(-lam_v))
        dpa = dr * r * (1.0 - r)
        dpx = di * i * (1.0 - i)
        dpab = dpa.astype(BF16)
        dpxb = dpx.astype(BF16)
        dwa = _dot_tn(xcb, dpab)
        dwx = _dot_tn(xcb, dpxb)
        dba = jnp.sum(dpa, axis=0, keepdims=True)
        dbx = jnp.sum(dpx, axis=0, keepdims=True)
        dxc = dxc + _dot_nt(dpab, wa_v) + _dot_nt(dpxb, wx_v)
        dcb = jnp.sum(dxc, axis=0, keepdims=True)
        dx = cw[3:4, :] * dxc
        dcw_rows = [None] * CONVW
        dcw_rows[3] = jnp.sum(dxc * x, axis=0, keepdims=True)
        for j in (1, 2, 3):
            dx = dx + cw[3 - j:4 - j, :] * _shift_up(dxc, j, row, 0.0)
            dcw_rows[3 - j] = jnp.sum(dxc * _shift_down(x, j, row, 0.0), axis=0, keepdims=True)
        dx_ref[0] = dx.astype(BF16)
        dcw = jnp.concatenate(dcw_rows, axis=0)
        first = pl.program_id(1) == 0

        @pl.when(first)
        def _():
            dcw_ref[...] = dcw
            dcb_ref[...] = dcb
            dwa_ref[0] = dwa
            dba_ref[...] = dba
            dwx_ref[0] = dwx
            dbx_ref[...] = dbx
            dlam_ref[...] = dlam

        @pl.when(jnp.logical_not(first))
        def _():
            dcw_ref[...] += dcw
            dcb_ref[...] += dcb
            dwa_ref[0] += dwa
            dba_ref[...] += dba
            dwx_ref[0] += dwx
            dbx_ref[...] += dbx
            dlam_ref[...] += dlam

    slab = lambda: pl.BlockSpec((1, S, RBW), lambda n, b: (b, 0, n))
    vec = lambda: pl.BlockSpec((1, RBW), lambda n, b: (0, n))
    mat = lambda: pl.BlockSpec((1, RBW, RBW), lambda n, b: (n, 0, 0))
    taps = lambda: pl.BlockSpec((CONVW, RBW), lambda n, b: (0, n))
    vshape = jax.ShapeDtypeStruct((1, DR), F32)
    mshape = jax.ShapeDtypeStruct((NRB, RBW, RBW), F32)
    return pl.pallas_call(
        body, grid=(NRB, BL), name="rnn_bwd",
        in_specs=[slab(), slab(), slab(), taps(), vec(), mat(), vec(), mat(), vec(), vec(),
                  pl.BlockSpec(memory_space=pl.ANY)],
        out_specs=[slab(), taps(), vec(), mat(), vec(), mat(), vec(), vec()],
        out_shape=[jax.ShapeDtypeStruct((BL, S, A_W), BF16), jax.ShapeDtypeStruct((CONVW, DR), F32),
                   vshape, mshape, vshape, mshape, vshape, vshape],
        input_output_aliases={10: 0},
        compiler_params=_cp(("parallel", "arbitrary"), 48 * 1024 * 1024),
    )(proj3, h3, dh3, conv_w, conv_b, wa, ba, wx, bx, lam, slab_a3)


def _att_blocks(dil):
    nb = (S // dil) // QB
    return [(n, c + n * QB * dil) for c in range(dil) for n in range(nb)]


def _rows(start, dil):
    return pl.ds(start, QB) if dil == 1 else pl.ds(start, QB, stride=dil)


def _rms_head(t, gain):
    rstd = lax.rsqrt(jnp.mean(t * t, axis=-1, keepdims=True) + EPS)
    return t * rstd * gain


def _rope(t, cs, sn):
    return t * cs + pltpu.roll(t, HD // 2, 1) * sn


def _rope_t(dy, cs, sn):
    return dy * cs - pltpu.roll(dy, HD // 2, 1) * sn


def _att_prep(q_ref, k_ref, v_ref, cos_ref, sin_ref, qn, kn, dil, qs, ks, vs):
    for j, (n, st) in enumerate(_att_blocks(dil)):
        sl = _rows(st, dil)
        cs = cos_ref[sl, :]
        sn = sin_ref[sl, :]
        dst = pl.ds(j * QB, QB)
        qs[dst, :] = _rope(_rms_head(q_ref.at[0][sl, :], qn), cs, sn).astype(BF16)
        ks[dst, :] = _rope(_rms_head(k_ref.at[0][sl, :], kn), cs, sn).astype(BF16)
        vs[dst, :] = v_ref.at[0][sl, :].astype(BF16)


def _att_scores(qs, ks, j, n):
    q = qs[pl.ds(j * QB, QB), :]
    if n > 0:
        kr = pl.ds((j - 1) * QB, 2 * QB)
        nk = 2 * QB
    else:
        kr = pl.ds(j * QB, QB)
        nk = QB
    s = _dot_nt(q, ks[kr, :]) * SCALE
    qi = lax.broadcasted_iota(jnp.int32, (QB, nk), 0)
    kj = lax.broadcasted_iota(jnp.int32, (QB, nk), 1)
    if n > 0:
        dist = qi + QB - kj
        valid = (dist >= 0) & (dist <= QB)
    else:
        valid = qi >= kj
    return jnp.where(valid, s, NEG), q, kr


def _qkv_spec(kind, g):
    base = OFF_Q // HD + kind * (GW // HD) + g * NH
    return pl.BlockSpec((1, S, HD), lambda b, h: (b, 0, base + h))


def _attn_fwd(proj3, cos_t, sin_t, q_norm, k_norm):
    def body(*refs):
        qkv_refs = refs[:9]
        cos_ref, sin_ref, qn_ref, kn_ref, att_ref, lse_ref, w_ref, qs, ks, vs, og = refs[9:]
        for g, (window, dil) in enumerate(PATTERNS):
            q_ref, k_ref, v_ref = qkv_refs[3 * g:3 * g + 3]
            _att_prep(q_ref, k_ref, v_ref, cos_ref, sin_ref, qn_ref[g:g + 1, :], kn_ref[g:g + 1, :], dil, qs, ks, vs)
            for j, (n, st) in enumerate(_att_blocks(dil)):
                s, _, kr = _att_scores(qs, ks, j, n)
                m = jnp.max(s, axis=-1, keepdims=True)
                e = jnp.exp(s - m)
                den = jnp.sum(e, axis=-1, keepdims=True)
                o = _dot(e.astype(BF16), vs[kr, :]) / den
                sl = _rows(st, dil)
                og.at[g][sl, :] = o
                lse_ref.at[g, 0][sl, :] = jnp.broadcast_to(m + jnp.log(den), (QB, HD))
        l0 = lse_ref[0, 0]
        l1 = lse_ref[1, 0]
        l2 = lse_ref[2, 0]
        mx = jnp.maximum(jnp.maximum(l0, l1), l2)
        e0 = jnp.exp(l0 - mx)
        e1 = jnp.exp(l1 - mx)
        e2 = jnp.exp(l2 - mx)
        inv = 1.0 / (e0 + e1 + e2)
        w0 = e0 * inv
        w1 = e1 * inv
        w2 = e2 * inv
        w_ref[0, 0] = w0
        w_ref[1, 0] = w1
        w_ref[2, 0] = w2
        att_ref[0] = w0 * og[0] + w1 * og[1] + w2 * og[2]

    in_specs = [_qkv_spec(kind, g) for g in range(NG) for kind in range(3)]
    in_specs += [pl.BlockSpec((S, HD), lambda b, h: (0, 0)), pl.BlockSpec((S, HD), lambda b, h: (0, 0)),
                 pl.BlockSpec((NG, HD), lambda b, h: (0, 0)), pl.BlockSpec((NG, HD), lambda b, h: (0, 0))]
    stat = lambda: pl.BlockSpec((NG, 1, S, HD), lambda b, h: (0, b, 0, h))
    return pl.pallas_call(
        body, grid=(BL, NH), name="attn_fwd",
        in_specs=in_specs,
        out_specs=[pl.BlockSpec((1, S, HD), lambda b, h: (b, 0, h)), stat(), stat()],
        out_shape=[jax.ShapeDtypeStruct((BL, S, ATT), F32),
                   jax.ShapeDtypeStruct((NG, BL, S, ATT), F32),
                   jax.ShapeDtypeStruct((NG, BL, S, ATT), F32)],
        scratch_shapes=[pltpu.VMEM((S, HD), BF16), pltpu.VMEM((S, HD), BF16), pltpu.VMEM((S, HD), BF16),
                        pltpu.VMEM((NG, S, HD), F32)],
        compiler_params=_cp(("parallel", "parallel"), VMEM_BIG),
    )(*([proj3] * 9), cos_t, sin_t, q_norm, k_norm)


def _attn_bwd_group(g, proj3, cos_t, sin_t, qn_g, kn_g, lse, wts, datt3, sbar3, slabs):
    dil = PATTERNS[g][1]
    blocks = _att_blocks(dil)
    n_alias = 0 if slabs is None else 3

    def norm_rope_bwd(dpost, raw, gain, cs, sn):
        dn = _rope_t(dpost, cs, sn)
        rstd = lax.rsqrt(jnp.mean(raw * raw, axis=-1, keepdims=True) + EPS)
        xh = raw * rstd
        dgain = jnp.sum(dn * xh, axis=0, keepdims=True)
        gd = dn * gain
        draw = rstd * (gd - xh * jnp.mean(gd * xh, axis=-1, keepdims=True))
        return draw, dgain

    def body(*refs):
        (q_ref, k_ref, v_ref, cos_ref, sin_ref, qn_ref, kn_ref, lse_ref, w_ref, datt_ref, sbar_ref) = refs[:11]
        (dq_ref, dk_ref, dv_ref, dqn_ref, dkn_ref, qs, ks, vs, dks, dvs, dos, cvs, nat) = refs[11 + n_alias:]
        qn = qn_ref[...]
        kn = kn_ref[...]
        _att_prep(q_ref, k_ref, v_ref, cos_ref, sin_ref, qn, kn, dil, qs, ks, vs)
        wv = w_ref[0, 0]
        dos[...] = wv * datt_ref[0]
        cvs[...] = wv * sbar_ref[0]
        dks[...] = jnp.zeros((S, HD), F32)
        dvs[...] = jnp.zeros((S, HD), F32)
        dqn = jnp.zeros((1, HD), F32)
        dkn = jnp.zeros((1, HD), F32)
        for j, (n, st) in enumerate(blocks):
            sl = _rows(st, dil)
            s, q, kr = _att_scores(qs, ks, j, n)
            p = jnp.exp(s - lse_ref.at[0, 0][sl, :][:, 0:1])
            dob = dos[sl, :].astype(BF16)
            dp = _dot_nt(dob, vs[kr, :])
            ds = p * (dp - cvs[sl, :][:, 0:1])
            dsb = ds.astype(BF16)
            dvs[kr, :] += _dot_tn(p.astype(BF16), dob)
            dks[kr, :] += _dot_tn(dsb, q) * SCALE
            dq_post = _dot(dsb, ks[kr, :]) * SCALE
            draw, dgain = norm_rope_bwd(dq_post, q_ref.at[0][sl, :], qn, cos_ref[sl, :], sin_ref[sl, :])
            nat[sl, :] = draw
            dqn = dqn + dgain
        dq_ref[0] = nat[...].astype(BF16)
        for j, (n, st) in enumerate(blocks):
            sl = _rows(st, dil)
            draw, dgain = norm_rope_bwd(dks[pl.ds(j * QB, QB), :], k_ref.at[0][sl, :], kn, cos_ref[sl, :], sin_ref[sl, :])
            nat[sl, :] = draw
            dkn = dkn + dgain
        dk_ref[0] = nat[...].astype(BF16)
        for j, (n, st) in enumerate(blocks):
            nat[_rows(st, dil), :] = dvs[pl.ds(j * QB, QB), :]
        dv_ref[0] = nat[...].astype(BF16)
        first = (pl.program_id(0) == 0) & (pl.program_id(1) == 0)

        @pl.when(first)
        def _():
            dqn_ref[...] = dqn
            dkn_ref[...] = dkn

        @pl.when(jnp.logical_not(first))
        def _():
            dqn_ref[...] += dqn
            dkn_ref[...] += dkn

    full = lambda r: pl.BlockSpec((r, HD), lambda b, h: (0, 0))
    stat = lambda: pl.BlockSpec((1, 1, S, HD), lambda b, h: (g, b, 0, h))
    slab = lambda: pl.BlockSpec((1, S, HD), lambda b, h: (b, 0, h))
    out_slab = lambda: pl.BlockSpec((1, S, HD), lambda b, h: (b, 0, g * NH + h))
    big = jax.ShapeDtypeStruct((BL, S, GW), BF16)
    vecs = jax.ShapeDtypeStruct((1, HD), F32)
    in_specs = [_qkv_spec(0, g), _qkv_spec(1, g), _qkv_spec(2, g), full(S), full(S), full(1), full(1),
                stat(), stat(), slab(), slab()]
    args = [proj3, proj3, proj3, cos_t, sin_t, qn_g, kn_g, lse, wts, datt3, sbar3]
    aliases = {}
    if slabs is not None:
        in_specs += [pl.BlockSpec(memory_space=pl.ANY)] * 3
        args += list(slabs)
        aliases = {11: 0, 12: 1, 13: 2}
    return pl.pallas_call(
        body, grid=(BL, NH), name="attn_bwd_g%d" % g,
        in_specs=in_specs,
        out_specs=[out_slab(), out_slab(), out_slab(), full(1), full(1)],
        out_shape=[big, big, big, vecs, vecs],
        scratch_shapes=[pltpu.VMEM((S, HD), BF16), pltpu.VMEM((S, HD), BF16), pltpu.VMEM((S, HD), BF16),
                        pltpu.VMEM((S, HD), F32), pltpu.VMEM((S, HD), F32),
                        pltpu.VMEM((S, HD), F32), pltpu.VMEM((S, HD), F32), pltpu.VMEM((S, HD), F32)],
        input_output_aliases=aliases,
        compiler_params=_cp(("arbitrary", "arbitrary"), VMEM_BIG),
    )(*args)


def _tail(x, proj, h, att, p, tgt, w_o_rnn, w_o_att_t, w_out, w_pg, w_ple_t, norm_ple, b_pg, tm=256):
    nt = T // tm
    inv_d = 1.0 / D

    def body(x_ref, h_ref, zr_ref, att_ref, za_ref, g0a_ref, g0b_ref, g1a_ref, g1b_ref, p_ref, tgt_ref,
             np_ref, bpg_ref, wor_hbm, woa_hbm, wout_hbm, wpg_hbm, wple_hbm,
             dx1_ref, merged_ref, n1_ref, dpre_ref, dpe_ref, dyr_ref, dya_ref, slab_a_ref, slab_c_ref, dh_ref,
             datt_ref, sbar_ref, yrnn_ref, yatt_ref, loss_ref, dnp_ref, dbpg_ref,
             wor, woa, wout, wpg, wple):
        first = pl.program_id(0) == 0

        @pl.when(first)
        def _():
            pltpu.sync_copy(wor_hbm, wor)
            pltpu.sync_copy(woa_hbm, woa)
            pltpu.sync_copy(wout_hbm, wout)
            pltpu.sync_copy(wpg_hbm, wpg)
            pltpu.sync_copy(wple_hbm, wple)

        xv = x_ref[...]
        hv = h_ref[...]
        zr = zr_ref[...]
        av = att_ref[...]
        za = za_ref[...]
        szr = _sigmoid(zr)
        silu_r = zr * szr
        yrnn_b = (hv * silu_r).astype(BF16)
        sza = _sigmoid(za)
        silu_a = za * sza
        yatt_b = (av * silu_a).astype(BF16)
        yrnn_ref[...] = yrnn_b
        yatt_ref[...] = yatt_b
        yr = _dot(yrnn_b, wor[...])
        ya = _dot_nt(yatt_b, woa[...])
        g0 = _sigmoid(jnp.concatenate([g0a_ref[...], g0b_ref[...]], axis=1))
        g1 = _sigmoid(jnp.concatenate([g1a_ref[...], g1b_ref[...]], axis=1))
        merged_b = (g0 * yr + g1 * ya).astype(BF16)
        merged_ref[...] = merged_b
        x1 = xv + _dot(merged_b, wout[...])
        rstd = lax.rsqrt(jnp.mean(x1 * x1, axis=-1, keepdims=True) + EPS)
        xh = x1 * rstd
        npl = np_ref[...]
        n1_b = (xh * npl).astype(BF16)
        n1_ref[...] = n1_b
        pg = _sigmoid(_dot(n1_b, wpg[...]) + bpg_ref[...])
        pe = _dot_nt(p_ref[...].astype(BF16), wple[...])
        err = x1 + pg * pe - tgt_ref[...]
        loss_t = 0.5 * inv_d * jnp.sum(err * err)
        dy = err * inv_d
        dpe_ref[...] = (dy * pg).astype(BF16)
        dpre = dy * pe * pg * (1.0 - pg)
        dpre_b = dpre.astype(BF16)
        dpre_ref[...] = dpre_b
        dn1 = _dot_nt(dpre_b, wpg[...])
        dnp = jnp.sum(dn1 * xh, axis=0, keepdims=True)
        dbpg = jnp.sum(dpre, axis=0, keepdims=True)
        gd = dn1 * npl
        dx1 = dy + rstd * (gd - xh * jnp.mean(gd * xh, axis=-1, keepdims=True))
        dx1_ref[...] = dx1
        dmerged = _dot_nt(dx1.astype(BF16), wout[...])
        dyr_b = (dmerged * g0).astype(BF16)
        dya_b = (dmerged * g1).astype(BF16)
        dyr_ref[...] = dyr_b
        dya_ref[...] = dya_b
        slab_c_ref[:, ATT:ATT + D] = (dmerged * yr * g0 * (1.0 - g0)).astype(BF16)
        slab_c_ref[:, ATT + D:ATT + 2 * D] = (dmerged * ya * g1 * (1.0 - g1)).astype(BF16)
        dyrnn = _dot_nt(dyr_b, wor[...])
        dyatt = _dot(dya_b, woa[...])
        dh_ref[...] = dyrnn * silu_r
        slab_a_ref[...] = (dyrnn * hv * szr * (1.0 + zr * (1.0 - szr))).astype(BF16)
        datt = dyatt * silu_a
        datt_ref[...] = datt
        slab_c_ref[:, 0:ATT] = (dyatt * av * sza * (1.0 + za * (1.0 - sza))).astype(BF16)
        da = datt * av
        for hh in range(NH):
            seg = slice(hh * HD, (hh + 1) * HD)
            sbar_ref[:, seg] = jnp.broadcast_to(jnp.sum(da[:, seg], axis=-1, keepdims=True), (tm, HD))

        @pl.when(first)
        def _():
            loss_ref[...] = jnp.full((8, LANES), loss_t, F32)
            dnp_ref[...] = dnp
            dbpg_ref[...] = dbpg

        @pl.when(jnp.logical_not(first))
        def _():
            loss_ref[...] += jnp.full((8, LANES), loss_t, F32)
            dnp_ref[...] += dnp
            dbpg_ref[...] += dbpg

    tok = lambda w: pl.BlockSpec((tm, w), lambda i: (i, 0))
    col = lambda w, blk: pl.BlockSpec((tm, w), lambda i: (i, blk))
    vec = lambda: pl.BlockSpec((1, D), lambda i: (0, 0))
    hbm = lambda: pl.BlockSpec(memory_space=pl.ANY)
    gb = OFF_G // 512
    in_specs = [tok(D), tok(DR), col(DR, 1), tok(ATT), col(ATT, OFF_ZA // ATT),
                col(512, gb), col(512, gb + 1), col(512, gb + 2), col(512, gb + 3),
                tok(PLE), tok(D), vec(), vec(), hbm(), hbm(), hbm(), hbm(), hbm()]
    sh = lambda w, dt: jax.ShapeDtypeStruct((T, w), dt)
    out_shape = [sh(D, F32), sh(D, BF16), sh(D, BF16), sh(D, BF16), sh(D, BF16), sh(D, BF16), sh(D, BF16),
                 sh(A_W, BF16), sh(C_W, BF16), sh(DR, F32), sh(ATT, F32), sh(ATT, F32),
                 sh(DR, BF16), sh(ATT, BF16),
                 jax.ShapeDtypeStruct((8, LANES), F32), jax.ShapeDtypeStruct((1, D), F32),
                 jax.ShapeDtypeStruct((1, D), F32)]
    out_specs = [tok(D), tok(D), tok(D), tok(D), tok(D), tok(D), tok(D), col(DR, 1), tok(C_W), tok(DR),
                 tok(ATT), tok(ATT), tok(DR), tok(ATT),
                 pl.BlockSpec((8, LANES), lambda i: (0, 0)), vec(), vec()]
    return pl.pallas_call(
        body, grid=(nt,), name="tail_fwd_bwd",
        in_specs=in_specs, out_specs=out_specs, out_shape=out_shape,
        scratch_shapes=[pltpu.VMEM((DR, D), BF16), pltpu.VMEM((D, ATT), BF16), pltpu.VMEM((D, D), BF16),
                        pltpu.VMEM((D, D), BF16), pltpu.VMEM((D, PLE), BF16)],
        compiler_params=_cp(("arbitrary",), VMEM_BIG),
    )(x, h, proj, att, proj, proj, proj, proj, proj, p, tgt, norm_ple, b_pg, w_o_rnn, w_o_att_t, w_out, w_pg, w_ple_t)


def _input_norm_bwd(x, dhn, dx1, gain, tm=512):
    def body(x_ref, dhn_ref, dx1_ref, g_ref, dx_ref, dg_ref):
        xv = x_ref[...]
        rstd = lax.rsqrt(jnp.mean(xv * xv, axis=-1, keepdims=True) + EPS)
        xh = xv * rstd
        dn = dhn_ref[...]
        dg = jnp.sum(dn * xh, axis=0, keepdims=True)
        gd = dn * g_ref[...]
        dx_ref[...] = dx1_ref[...] + rstd * (gd - xh * jnp.mean(gd * xh, axis=-1, keepdims=True))
        first = pl.program_id(0) == 0

        @pl.when(first)
        def _():
            dg_ref[...] = dg

        @pl.when(jnp.logical_not(first))
        def _():
            dg_ref[...] += dg

    tok = lambda: pl.BlockSpec((tm, D), lambda i: (i, 0))
    vec = lambda: pl.BlockSpec((1, D), lambda i: (0, 0))
    return pl.pallas_call(
        body, grid=(T // tm,), name="input_norm_bwd",
        in_specs=[tok(), tok(), tok(), vec()], out_specs=[tok(), vec()],
        out_shape=[jax.ShapeDtypeStruct((T, D), F32), jax.ShapeDtypeStruct((1, D), F32)],
        compiler_params=_cp(("arbitrary",), VMEM_MID),
    )(x, dhn, dx1, gain)


def _rope_tables():
    pos = jnp.arange(S, dtype=F32)
    inv_freq = ROPE_THETA ** (-jnp.arange(0, HD, 2, dtype=F32) / HD)
    ang = pos[:, None] * inv_freq[None, :]
    cos, sin = jnp.cos(ang), jnp.sin(ang)
    return jnp.concatenate([cos, cos], axis=1), jnp.concatenate([-sin, sin], axis=1)


def _local_step(x, p, tgt, w_in_t, w_o_rnn, w_o_att_t, w_out, w_pg, w_ple_t, conv_w, norm_mix, b_in, conv_b,
                w_rg_a, b_rg_a, w_rg_x, b_rg_x, lam, q_norm, k_norm, norm_ple, b_pg):
    cos_t, sin_t = _rope_tables()
    wa_b = w_rg_a.astype(BF16)
    wx_b = w_rg_x.astype(BF16)

    hn = _rmsnorm_fwd(x, norm_mix)
    proj = _in_proj(hn, w_in_t, b_in)
    proj3 = proj.reshape(BL, S, NIN)
    h3 = _rnn_fwd(proj3, conv_w, conv_b, wa_b, b_rg_a, wx_b, b_rg_x, lam)
    att3, lse, wts = _attn_fwd(proj3, cos_t, sin_t, q_norm, k_norm)
    (dx1, merged, n1, dpre, dpe, dyr, dya, slab_a, slab_c, dh, datt, sbar, yrnn, yatt, loss8, dnp, dbpg) = _tail(
        x, proj, h3.reshape(T, DR), att3.reshape(T, ATT), p, tgt, w_o_rnn, w_o_att_t, w_out, w_pg, w_ple_t,
        norm_ple, b_pg)

    big = [None,
           _mm_tn(yrnn, dyr, 640, 512, "dw_o_rnn"),
           _mm_tn(dya, yatt, 512, 512, "dw_o_att_t"),
           _mm_tn(merged, dx1, 512, 512, "dw_out"),
           _mm_tn(n1, dpre, 512, 512, "dw_ple_gate"),
           _mm_tn(dpe, p, 512, 512, "dw_ple_t")]

    slab_a3, dcw, dcb, dwa, dba, dwx, dbx, dlam = _rnn_bwd(
        proj3, h3, dh.reshape(BL, S, DR), slab_a.reshape(BL, S, A_W), conv_w, conv_b, wa_b, b_rg_a, wx_b, b_rg_x, lam)
    datt3 = datt.reshape(BL, S, ATT)
    sbar3 = sbar.reshape(BL, S, ATT)
    slabs = None
    dqn = []
    dkn = []
    for g in range(NG):
        dq, dk, dv, dqn_g, dkn_g = _attn_bwd_group(g, proj3, cos_t, sin_t, q_norm[g:g + 1], k_norm[g:g + 1],
                                                   lse, wts, datt3, sbar3, slabs)
        slabs = (dq, dk, dv)
        dqn.append(dqn_g)
        dkn.append(dkn_g)
    pieces = [slab_a3.reshape(T, A_W)] + [t.reshape(T, GW) for t in slabs] + [slab_c]
    big[0], db_in = _dw_in(pieces, hn)
    dhn = _dhn(pieces, w_in_t)
    grad_x, dnm = _input_norm_bwd(x, dhn, dx1, norm_mix)

    small = dict(w_rg_a=dwa, w_rg_x=dwx, norm_mix=dnm, b_in=db_in, conv_b=dcb, b_rg_a=dba, b_rg_x=dbx,
                 lru_lambda=dlam, q_norm=dqn, k_norm=dkn, norm_ple=dnp, b_ple_gate=dbpg, conv_w=dcw)
    return loss8[0, 0], grad_x, big, small


MESH = pl.DeviceIdType.MESH
HBM_SPEC = pl.BlockSpec(memory_space=pl.ANY)


def _my_pos():
    return lax.axis_index("x"), lax.axis_index("y"), lax.axis_index("c")


def _flip(pos, k):
    x, y, c = pos
    return (1 - x if k & 4 else x, 1 - y if k & 2 else y, 1 - c if k & 1 else c)


def _lin(pos):
    return 4 * pos[0] + 2 * pos[1] + pos[2]


def _chip(pos):
    return 2 * pos[0] + pos[1]


def _all_gather_two_level(shards, name):
    na = len(shards)

    def body(*refs):
        x_refs = refs[:na]
        out_refs = refs[na:2 * na]
        send_sems, recv_sems, local_sems = refs[2 * na:]
        me = _my_pos()
        sibling = _flip(me, 1)
        chips = [_flip(me, 4), _flip(me, 2), _flip(me, 6)]

        def copy(i, k, block, to, from_x=False):
            dst = out_refs[i].at[_lin(block)]
            return pltpu.make_async_remote_copy(
                src_ref=x_refs[i] if from_x else dst, dst_ref=dst,
                send_sem=send_sems.at[7 * i + k], recv_sem=recv_sems.at[7 * i + k], device_id=to, device_id_type=MESH)

        started = []
        for i in range(na):
            mine = pltpu.make_async_copy(x_refs[i], out_refs[i].at[_lin(me)], local_sems.at[i])
            mine.start()
            started.append(mine)
        sends = []
        for i in range(na):
            cps = [copy(i, 0, me, sibling, True)] + [copy(i, 1 + j, me, chip, True) for j, chip in enumerate(chips)]
            for cp in cps:
                cp.start()
            sends += cps
        for i in range(na):
            for j, chip in enumerate(chips):
                copy(i, 1 + j, chip, me).wait_recv()
                fwd = copy(i, 4 + j, chip, sibling)
                fwd.start()
                sends.append(fwd)
        for i in range(na):
            copy(i, 0, sibling, me).wait_recv()
            for j, chip in enumerate(chips):
                copy(i, 4 + j, _flip(chip, 1), me).wait_recv()
        for cp in sends:
            cp.wait_send()
        for mine in started:
            mine.wait()

    return pl.pallas_call(
        body, name=name,
        out_shape=[jax.ShapeDtypeStruct((NDEV,) + s.shape, s.dtype) for s in shards],
        in_specs=[HBM_SPEC] * na, out_specs=[HBM_SPEC] * na,
        scratch_shapes=[pltpu.SemaphoreType.DMA((7 * na,)), pltpu.SemaphoreType.DMA((7 * na,)),
                        pltpu.SemaphoreType.DMA((na,))],
    )(*shards)


def _all_gather_direct(shard, name):
    def body(x_ref, out_ref, send_sems, recv_sems, local_sem):
        me = _my_pos()
        mine = pltpu.make_async_copy(x_ref, out_ref.at[_lin(me)], local_sem)
        mine.start()
        sends = []
        for k in range(1, NDEV):
            cp = pltpu.make_async_remote_copy(
                src_ref=x_ref, dst_ref=out_ref.at[_lin(me)], send_sem=send_sems.at[k - 1],
                recv_sem=recv_sems.at[k - 1], device_id=_flip(me, k), device_id_type=MESH)
            cp.start()
            sends.append(cp)
        for k in range(1, NDEV):
            peer = _flip(me, k)
            pltpu.make_async_remote_copy(
                src_ref=x_ref, dst_ref=out_ref.at[_lin(peer)], send_sem=send_sems.at[k - 1],
                recv_sem=recv_sems.at[k - 1], device_id=peer, device_id_type=MESH).wait_recv()
        for cp in sends:
            cp.wait_send()
        mine.wait()

    return pl.pallas_call(
        body, name=name,
        out_shape=jax.ShapeDtypeStruct((NDEV,) + shard.shape, shard.dtype),
        in_specs=[HBM_SPEC], out_specs=HBM_SPEC,
        scratch_shapes=[pltpu.SemaphoreType.DMA((7,)), pltpu.SemaphoreType.DMA((7,)), pltpu.SemaphoreType.DMA],
    )(shard)


def _exchange_within_chip(parts, name):
    na = len(parts)

    def body(*refs):
        a_refs = refs[:na]
        recv_refs = refs[na:2 * na]
        send_sems, recv_sems = refs[2 * na:]
        me = _my_pos()
        c = me[2]
        sibling = _flip(me, 1)
        remote = []
        for i in range(na):
            for q in range(NCHIP):
                rc = pltpu.make_async_remote_copy(
                    src_ref=a_refs[i].at[q, 1 - c], dst_ref=recv_refs[i].at[q],
                    send_sem=send_sems.at[NCHIP * i + q], recv_sem=recv_sems.at[NCHIP * i + q],
                    device_id=sibling, device_id_type=MESH)
                rc.start()
                remote.append(rc)
        for rc in remote:
            rc.wait_recv()
        for rc in remote:
            rc.wait_send()

    return pl.pallas_call(
        body, name=name, out_shape=[jax.ShapeDtypeStruct((NCHIP,) + a.shape[2:], a.dtype) for a in parts],
        in_specs=[HBM_SPEC] * na, out_specs=[HBM_SPEC] * na,
        scratch_shapes=[pltpu.SemaphoreType.DMA((NCHIP * na,)), pltpu.SemaphoreType.DMA((NCHIP * na,))],
    )(*parts)


def _exchange_between_chips(parts, name):
    na = len(parts)

    def body(*refs):
        a_refs = refs[:na]
        out_refs = refs[na:2 * na]
        send_sems, recv_sems = refs[2 * na:]
        me = _my_pos()
        myq = _chip(me)
        peers = [_flip(me, 4), _flip(me, 2), _flip(me, 6)]
        remote = []
        for i in range(na):
            for j, peer in enumerate(peers):
                rc = pltpu.make_async_remote_copy(
                    src_ref=a_refs[i].at[_chip(peer)], dst_ref=out_refs[i].at[myq],
                    send_sem=send_sems.at[3 * i + j], recv_sem=recv_sems.at[3 * i + j],
                    device_id=peer, device_id_type=MESH)
                rc.start()
                remote.append(rc)
        for i in range(na):
            for j, peer in enumerate(peers):
                pltpu.make_async_remote_copy(
                    src_ref=a_refs[i].at[myq], dst_ref=out_refs[i].at[_chip(peer)],
                    send_sem=send_sems.at[3 * i + j], recv_sem=recv_sems.at[3 * i + j],
                    device_id=peer, device_id_type=MESH).wait_recv()
        for rc in remote:
            rc.wait_send()

    return pl.pallas_call(
        body, name=name, out_shape=[jax.ShapeDtypeStruct(a.shape, a.dtype) for a in parts],
        in_specs=[HBM_SPEC] * na, out_specs=[HBM_SPEC] * na,
        scratch_shapes=[pltpu.SemaphoreType.DMA((3 * na,)), pltpu.SemaphoreType.DMA((3 * na,))],
    )(*parts)


def _scalar(v):
    return jnp.asarray(v, jnp.int32).reshape(1)


def _sum_pairs(parts, theirs):
    na = len(parts)

    def body(c_ref, *refs):
        for i in range(na):
            o_ref = refs[2 * na + i]
            o_ref[0] = (refs[i][0, 0].astype(F32) + refs[na + i][0].astype(F32)).astype(o_ref.dtype)

    def mine_spec(a):
        return pl.BlockSpec((1, 1) + a.shape[2:], lambda q, c_ref: (q, c_ref[0], 0, 0))

    def spec(a):
        return pl.BlockSpec((1,) + a.shape[1:], lambda q, c_ref: (q, 0, 0))

    return pl.pallas_call(
        body, name="sum_pairs",
        grid_spec=pltpu.PrefetchScalarGridSpec(
            num_scalar_prefetch=1, grid=(NCHIP,),
            in_specs=[mine_spec(a) for a in parts] + [spec(a) for a in theirs],
            out_specs=[spec(a) for a in theirs]),
        out_shape=[jax.ShapeDtypeStruct(a.shape, a.dtype) for a in theirs],
        compiler_params=_cp(("arbitrary",), VMEM_BIG),
    )(_scalar(lax.axis_index("c")), *parts, *theirs)


def _others(q, myq):
    return jnp.where(q == myq, (q + 1) % NCHIP, q)


def _sum_chips_rows(own, recv, tr, name):
    _, r, w = recv.shape

    def body(q_ref, own_ref, r0, r1, r2, r3, o_ref):
        myq = q_ref[0]
        acc = None
        for q, r_ref in enumerate((r0, r1, r2, r3)):
            term = jnp.where(myq == q, own_ref[0], r_ref[0]).astype(F32)
            acc = term if acc is None else acc + term
        o_ref[...] = acc

    def recv_spec(q):
        return pl.BlockSpec((1, tr, w), lambda i, q_ref: (_others(q, q_ref[0]), i, 0))

    return pl.pallas_call(
        body, name=name,
        grid_spec=pltpu.PrefetchScalarGridSpec(
            num_scalar_prefetch=1, grid=(r // tr,),
            in_specs=[pl.BlockSpec((1, tr, w), lambda i, q_ref: (q_ref[0], i, 0))] + [recv_spec(q) for q in range(NCHIP)],
            out_specs=pl.BlockSpec((tr, w), lambda i, q_ref: (i, 0))),
        out_shape=jax.ShapeDtypeStruct((r, w), F32),
        compiler_params=_cp(("arbitrary",), VMEM_MID),
    )(_scalar(_chip(_my_pos())), own, recv, recv, recv, recv)


def _sum_chips_small(own, recv, transpose):
    na = len(recv)

    def body(q_ref, *refs):
        myq = q_ref[0]
        for i in range(na):
            acc = None
            for q in range(NCHIP):
                term = jnp.where(myq == q, refs[i][0], refs[na * (1 + q) + i][0]).astype(F32)
                acc = term if acc is None else acc + term
            refs[na * (1 + NCHIP) + i][...] = acc.T if transpose[i] else acc

    def oshape(a, tr):
        r, w = a.shape[1:]
        return (w, r) if tr else (r, w)

    own_spec = lambda a: pl.BlockSpec((1,) + a.shape[1:], lambda s, q_ref: (q_ref[0], 0, 0))
    recv_spec = lambda a, q: pl.BlockSpec((1,) + a.shape[1:], lambda s, q_ref: (_others(q, q_ref[0]), 0, 0))
    out_spec = lambda shp: pl.BlockSpec(shp, lambda s, q_ref: (0, 0))
    in_specs = [own_spec(a) for a in own]
    for q in range(NCHIP):
        in_specs += [recv_spec(a, q) for a in recv]
    return pl.pallas_call(
        body, name="sum_chips_small",
        grid_spec=pltpu.PrefetchScalarGridSpec(
            num_scalar_prefetch=1, grid=(1,), in_specs=in_specs,
            out_specs=[out_spec(oshape(a, tr)) for a, tr in zip(recv, transpose)]),
        out_shape=[jax.ShapeDtypeStruct(oshape(a, tr), F32) for a, tr in zip(recv, transpose)],
        compiler_params=_cp(("arbitrary",), VMEM_MID),
    )(_scalar(_chip(_my_pos())), *own, *(list(recv) * NCHIP))


def _rep_offsets():
    offs = []
    o = 0
    for r in REP_ROWS:
        offs.append(o)
        o += r
    return offs


def _pack_small_grads(g):
    offs = _rep_offsets()

    def body(dwa, dwx, dnm, dbin, dcb, dba, dbx, dlam, dq0, dq1, dq2, dk0, dk1, dk2, dnp, dbpg, o_ref):
        o_ref[pl.ds(REP_TOTAL_ROWS - 2, NDEV * REP_ROWS_DEV - REP_TOTAL_ROWS + 2), :] = jnp.zeros(
            (NDEV * REP_ROWS_DEV - REP_TOTAL_ROWS + 2, LANES), F32)
        for n in range(NRB):
            o_ref[pl.ds(offs[0] + n * RBW, RBW), :] = dwa[n]
            o_ref[pl.ds(offs[1] + n * RBW, RBW), :] = dwx[n]

        def put_vec(off, ref, rows):
            for k in range(rows):
                o_ref[pl.ds(off + k, 1), :] = ref[:, k * LANES:(k + 1) * LANES]

        put_vec(offs[2], dnm, REP_ROWS[2])
        put_vec(offs[3], dbin, REP_ROWS[3])
        put_vec(offs[4], dcb, REP_ROWS[4])
        put_vec(offs[5], dba, REP_ROWS[5])
        put_vec(offs[6], dbx, REP_ROWS[6])
        put_vec(offs[7], dlam, REP_ROWS[7])
        for k, ref in enumerate((dq0, dq1, dq2)):
            o_ref[pl.ds(offs[8] + k, 1), :] = ref[...]
        for k, ref in enumerate((dk0, dk1, dk2)):
            o_ref[pl.ds(offs[9] + k, 1), :] = ref[...]
        put_vec(offs[10], dnp, REP_ROWS[10])
        put_vec(offs[11], dbpg, REP_ROWS[11])

    args = [g["w_rg_a"], g["w_rg_x"], g["norm_mix"], g["b_in"], g["conv_b"], g["b_rg_a"], g["b_rg_x"],
            g["lru_lambda"], *g["q_norm"], *g["k_norm"], g["norm_ple"], g["b_ple_gate"]]
    full = lambda shp: pl.BlockSpec(shp, lambda: (0,) * len(shp))
    return pl.pallas_call(
        body, name="pack_small_grads",
        in_specs=[full(a.shape) for a in args],
        out_specs=full((NDEV * REP_ROWS_DEV, LANES)),
        out_shape=jax.ShapeDtypeStruct((NDEV * REP_ROWS_DEV, LANES), F32),
    )(*args)


def _adam_math(wv, gv, mv, vv):
    c1 = 1.0 - B1 ** STEP
    c2 = 1.0 - B2 ** STEP
    m2 = B1 * mv + (1.0 - B1) * gv
    v2 = B2 * vv + (1.0 - B2) * (gv * gv)
    delta = (-LR) * ((m2 / c1) / (jnp.sqrt(v2 / c2) + AEPS) + WD * wv)
    return delta, m2, v2


def _adamw_small(rep_flat, w, m, v):
    offs = _rep_offsets()
    n = len(REP_NAMES)

    def body(*refs):
        g_ref = refs[0]
        w_refs = refs[1:1 + n]
        m_refs = refs[1 + n:1 + 2 * n]
        v_refs = refs[1 + 2 * n:1 + 3 * n]
        outs = refs[1 + 3 * n:]
        go, do, mo, vo = outs[:n], outs[n:2 * n], outs[2 * n:3 * n], outs[3 * n:]

        def emit(i, idx, gv):
            go[i][idx] = gv
            delta, m2, v2 = _adam_math(w_refs[i][idx], gv, m_refs[i][idx], v_refs[i][idx])
            do[i][idx] = delta
            mo[i][idx] = m2
            vo[i][idx] = v2

        for i in range(n):
            if i < 2:
                for b in range(NRB):
                    emit(i, b, g_ref[pl.ds(offs[i] + b * RBW, RBW), :])
            elif REP_NAMES[i] in ("q_norm", "k_norm"):
                emit(i, slice(None), g_ref[pl.ds(offs[i], NG), :])
            else:
                gv = jnp.concatenate([g_ref[pl.ds(offs[i] + k, 1), :] for k in range(REP_ROWS[i])], axis=1)
                emit(i, slice(None), gv)

    full = lambda shp: pl.BlockSpec(shp, lambda: (0,) * len(shp))
    pspecs = [full(a.shape) for a in w]
    pshapes = [jax.ShapeDtypeStruct(a.shape, F32) for a in w]
    res = pl.pallas_call(
        body, name="adamw_small",
        in_specs=[full(rep_flat.shape)] + pspecs * 3,
        out_specs=pspecs * 4, out_shape=pshapes * 4,
        compiler_params=_cp(None, VMEM_MID),
    )(rep_flat, *w, *m, *v)
    return res[:n], res[n:2 * n], res[2 * n:3 * n], res[3 * n:]


def _adamw_rows(w, g, m, v, tr, name):
    r, c_ = w.shape

    def body(w_ref, g_ref, m_ref, v_ref, d_ref, m2_ref, v2_ref):
        delta, m2, v2 = _adam_math(w_ref[...], g_ref[...], m_ref[...], v_ref[...])
        d_ref[...] = delta
        m2_ref[...] = m2
        v2_ref[...] = v2

    spec = lambda: pl.BlockSpec((tr, c_), lambda i: (i, 0))
    shp = jax.ShapeDtypeStruct((r, c_), F32)
    return pl.pallas_call(
        body, grid=(r // tr,), name=name,
        in_specs=[spec(), spec(), spec(), spec()], out_specs=[spec(), spec(), spec()],
        out_shape=[shp, shp, shp], compiler_params=_cp(("parallel",)),
    )(w, g, m, v)


def _adamw_many(w, g, m, v):
    n = len(w)

    def body(*refs):
        for i in range(n):
            delta, m2, v2 = _adam_math(refs[i][...], refs[n + i][...], refs[2 * n + i][...], refs[3 * n + i][...])
            refs[4 * n + i][...] = delta
            refs[5 * n + i][...] = m2
            refs[6 * n + i][...] = v2

    full = lambda shp: pl.BlockSpec(shp, lambda: (0,) * len(shp))
    specs = [full(a.shape) for a in w]
    shapes = [jax.ShapeDtypeStruct(a.shape, F32) for a in w]
    res = pl.pallas_call(
        body, name="adamw_shards",
        in_specs=specs * 4, out_specs=specs * 3, out_shape=shapes * 3,
        compiler_params=_cp(None, VMEM_MID),
    )(*w, *g, *m, *v)
    return res[:n], res[n:2 * n], res[2 * n:]


def kernel(x, p, norm_mix, w_in, b_in, conv_w, conv_b, w_rg_a, b_rg_a, w_rg_x, b_rg_x, lru_lambda, q_norm, k_norm, w_o_rnn, w_o_att, w_out, norm_ple, w_ple_gate, b_ple_gate, w_ple, loss_target, m_norm_mix, m_w_in, m_b_in, m_conv_w, m_conv_b, m_w_rg_a, m_b_rg_a, m_w_rg_x, m_b_rg_x, m_lru_lambda, m_q_norm, m_k_norm, m_w_o_rnn, m_w_o_att, m_w_out, m_norm_ple, m_w_ple_gate, m_b_ple_gate, m_w_ple, v_norm_mix, v_w_in, v_b_in, v_conv_w, v_conv_b, v_w_rg_a, v_b_rg_a, v_w_rg_x, v_b_rg_x, v_lru_lambda, v_q_norm, v_k_norm, v_w_o_rnn, v_w_o_att, v_w_out, v_norm_ple, v_w_ple_gate, v_b_ple_gate, v_w_ple):
    w = dict(norm_mix=norm_mix, w_in=w_in, b_in=b_in, conv_w=conv_w, conv_b=conv_b, w_rg_a=w_rg_a, b_rg_a=b_rg_a,
             w_rg_x=w_rg_x, b_rg_x=b_rg_x, lru_lambda=lru_lambda, q_norm=q_norm, k_norm=k_norm, w_o_rnn=w_o_rnn,
             w_o_att=w_o_att, w_out=w_out, norm_ple=norm_ple, w_ple_gate=w_ple_gate, b_ple_gate=b_ple_gate,
             w_ple=w_ple)
    m = dict(norm_mix=m_norm_mix, w_in=m_w_in, b_in=m_b_in, conv_w=m_conv_w, conv_b=m_conv_b, w_rg_a=m_w_rg_a,
             b_rg_a=m_b_rg_a, w_rg_x=m_w_rg_x, b_rg_x=m_b_rg_x, lru_lambda=m_lru_lambda, q_norm=m_q_norm,
             k_norm=m_k_norm, w_o_rnn=m_w_o_rnn, w_o_att=m_w_o_att, w_out=m_w_out, norm_ple=m_norm_ple,
             w_ple_gate=m_w_ple_gate, b_ple_gate=m_b_ple_gate, w_ple=m_w_ple)
    v = dict(norm_mix=v_norm_mix, w_in=v_w_in, b_in=v_b_in, conv_w=v_conv_w, conv_b=v_conv_b, w_rg_a=v_w_rg_a,
             b_rg_a=v_b_rg_a, w_rg_x=v_w_rg_x, b_rg_x=v_b_rg_x, lru_lambda=v_lru_lambda, q_norm=v_q_norm,
             k_norm=v_k_norm, w_o_rnn=v_w_o_rnn, w_o_att=v_w_o_att, w_out=v_w_out, norm_ple=v_norm_ple,
             w_ple_gate=v_w_ple_gate, b_ple_gate=v_b_ple_gate, w_ple=v_w_ple)
    names = list(w.keys())

    shards = [w_in[0].T.astype(BF16), w_o_rnn[0].astype(BF16), w_o_att[0].T.astype(BF16), w_out[0].astype(BF16),
              w_ple_gate[0].astype(BF16), w_ple[0].T.astype(BF16), conv_w[0]]
    gat = _all_gather_two_level(shards, "gather_weights")
    w_in_t, w_o_rnn_f, w_o_att_t, w_out_f, w_pg_f, w_ple_t = [
        a.reshape((NDEV * a.shape[1], a.shape[2])) for a in gat[:6]]
    conv_f = gat[6].transpose(1, 0, 2).reshape(CONVW, DR)

    loss_part, grad_x, big, small = _local_step(
        x.reshape(T, D), p.reshape(T, PLE), loss_target.reshape(T, D),
        w_in_t, w_o_rnn_f, w_o_att_t, w_out_f, w_pg_f, w_ple_t, conv_f,
        norm_mix, b_in, conv_b, w_rg_a[0], b_rg_a, w_rg_x[0], b_rg_x, lru_lambda, q_norm[0], k_norm[0],
        norm_ple, b_ple_gate)
    loss = lax.psum(loss_part, ("x", "y", "c"))

    rep_parts = _pack_small_grads(small)
    conv_parts = small["conv_w"].reshape(CONVW, NDEV, DR // NDEV).transpose(1, 0, 2)
    parts = [a.reshape((NCHIP, 2, a.shape[0] // NDEV, a.shape[1])) for a in big]
    parts += [rep_parts.reshape(NCHIP, 2, REP_ROWS_DEV, LANES), conv_parts.reshape(NCHIP, 2, CONVW, DR // NDEV)]
    theirs = _exchange_within_chip(parts, "reduce_within_chip")
    chip_sums = _sum_pairs(parts, theirs)
    recv = _exchange_between_chips(chip_sums, "reduce_between_chips")
    g_in_t = _sum_chips_rows(chip_sums[0], recv[0], 304, "sum_chips_w_in")
    g_o_rnn, g_o_att, g_out, g_pg, g_ple, g_rep, g_conv = _sum_chips_small(
        chip_sums[1:], recv[1:], (False, True, False, False, True, False, False))
    rep_all = _all_gather_direct(g_rep, "gather_small").reshape(NDEV * REP_ROWS_DEV, LANES)

    grad, delta, new_m, new_v = {}, {}, {}, {}
    rep_shape = lambda a: a if a.ndim == 2 else a.reshape(a.shape[1:])
    res = _adamw_small(rep_all, [rep_shape(w[n]) for n in REP_NAMES], [rep_shape(m[n]) for n in REP_NAMES],
                       [rep_shape(v[n]) for n in REP_NAMES])
    for dst, vals in zip((grad, delta, new_m, new_v), res):
        for n, a in zip(REP_NAMES, vals):
            dst[n] = a.reshape(w[n].shape)
    g_in = g_in_t.T
    d_, m_, v_ = _adamw_rows(w_in[0], g_in, m_w_in[0], v_w_in[0], 128, "adamw_w_in")
    grad["w_in"], delta["w_in"], new_m["w_in"], new_v["w_in"] = g_in[None], d_[None], m_[None], v_[None]
    rest = ("w_o_rnn", "w_o_att", "w_out", "w_ple_gate", "w_ple", "conv_w")
    g_rest = [g_o_rnn, g_o_att, g_out, g_pg, g_ple, g_conv]
    res = _adamw_many([w[n][0] for n in rest], g_rest, [m[n][0] for n in rest], [v[n][0] for n in rest])
    for n, a in zip(rest, g_rest):
        grad[n] = a[None]
    for dst, vals in zip((delta, new_m, new_v), res):
        for n, a in zip(rest, vals):
            dst[n] = a[None]

    return (loss, grad_x.reshape(BL, S, D), *[grad[n] for n in names], *[delta[n] for n in names],
            *[new_m[n] for n in names], *[new_v[n] for n in names])
```

```python
import jax
import jax.numpy as jnp
from jax import lax
from jax.experimental import pallas as pl
from jax.experimental.pallas import tpu as pltpu

F32 = jnp.float32
BF16 = jnp.bfloat16

D = 1024
S = 2048
BL = 2
T = BL * S
NDEV = 8
NCHIP = 4
PLE = 256
DR = 1280
NRB = 10
RBW = 128
CONVW = 4
LRU_C = 8.0
HD = 128
NH = 4
PATTERNS = ((128, 1), (512, 4), (2048, 16))
NG = 3
ATT = NH * HD
GW = NG * ATT
NIN = 2 * DR + 3 * GW + ATT + 2 * D
OFF_ZR = DR
OFF_Q = 2 * DR
OFF_ZA = OFF_Q + 3 * GW
OFF_G = OFF_ZA + ATT
ROPE_THETA = 10000.0
EPS = 1e-6
SCALE = HD ** -0.5
NEG = -1e30
QB = 128
LANES = 128
CT = 512
NCT = NIN // CT
A_W = 2 * DR
C_W = ATT + 2 * D

LR, B1, B2, AEPS, WD, STEP = 0.001, 0.9, 0.999, 1e-08, 0.01, 10

NSHARD_IN = NIN // NDEV
REP_NAMES = ("w_rg_a", "w_rg_x", "norm_mix", "b_in", "conv_b", "b_rg_a", "b_rg_x", "lru_lambda", "q_norm",
             "k_norm", "norm_ple", "b_ple_gate")
REP_ROWS = (NRB * RBW, NRB * RBW, D // LANES, NIN // LANES, DR // LANES, DR // LANES, DR // LANES, DR // LANES,
            NG, NG, D // LANES, D // LANES)
REP_TOTAL_ROWS = sum(REP_ROWS)
REP_ROWS_DEV = 344
BIG_NAMES = ("w_in", "w_o_rnn", "w_o_att", "w_out", "w_ple_gate", "w_ple")

VMEM_BIG = 56 * 1024 * 1024
VMEM_MID = 40 * 1024 * 1024


def _cp(sem=None, vmem=None):
    return pltpu.CompilerParams(dimension_semantics=sem, vmem_limit_bytes=vmem)


def _dot(a, b):
    return jnp.dot(a, b, preferred_element_type=F32)


def _dot_nt(a, b):
    return lax.dot_general(a, b, (((1,), (1,)), ((), ())), preferred_element_type=F32)


def _dot_tn(a, b):
    return lax.dot_general(a, b, (((0,), (0,)), ((), ())), preferred_element_type=F32)


def _sigmoid(x):
    return jax.nn.sigmoid(x)


def _perm(j):
    jq = j - OFF_Q // CT
    inside = (j >= OFF_Q // CT) & (j < OFF_ZA // CT)
    return jnp.where(inside, OFF_Q // CT + (jq % 3) * 3 + jq // 3, j)


PIECES = ((0, A_W // CT), (OFF_Q // CT, GW // CT), (OFF_Q // CT + 3, GW // CT), (OFF_Q // CT + 6, GW // CT),
          (OFF_ZA // CT, C_W // CT))


def _rmsnorm_fwd(x, gain, tm=512):
    def body(x_ref, g_ref, o_ref):
        xv = x_ref[...]
        var = jnp.mean(xv * xv, axis=-1, keepdims=True)
        o_ref[...] = (xv * lax.rsqrt(var + EPS) * g_ref[...]).astype(BF16)

    return pl.pallas_call(
        body, grid=(T // tm,), name="rmsnorm_fwd",
        in_specs=[pl.BlockSpec((tm, D), lambda i: (i, 0)), pl.BlockSpec((1, D), lambda i: (0, 0))],
        out_specs=pl.BlockSpec((tm, D), lambda i: (i, 0)),
        out_shape=jax.ShapeDtypeStruct((T, D), BF16),
        compiler_params=_cp(("parallel",)),
    )(x, gain)


def _in_proj(hn, w_t, bias, tm=1024):
    def body(a_ref, w_ref, b_ref, o_ref):
        o_ref[...] = _dot_nt(a_ref[...], w_ref[...]) + b_ref[...]

    return pl.pallas_call(
        body, grid=(T // tm, NCT), name="in_proj",
        in_specs=[pl.BlockSpec((tm, D), lambda i, j: (i, 0)),
                  pl.BlockSpec((CT, D), lambda i, j: (_perm(j), 0)),
                  pl.BlockSpec((1, CT), lambda i, j: (0, _perm(j)))],
        out_specs=pl.BlockSpec((tm, CT), lambda i, j: (i, j)),
        out_shape=jax.ShapeDtypeStruct((T, NIN), F32),
        compiler_params=_cp(("parallel", "parallel"), VMEM_MID),
    )(hn, w_t, bias)


def _dhn(pieces, w_t, tm=512):
    def body(a_ref, q_ref, k_ref, v_ref, c_ref, w_hbm, o_ref, w):
        @pl.when(pl.program_id(0) == 0)
        def _():
            pltpu.sync_copy(w_hbm, w)

        acc = _dot(a_ref[...], w[pl.ds(0, A_W), :])
        for kind, x_ref in enumerate((q_ref, k_ref, v_ref)):
            for g in range(NG):
                row = OFF_Q + (3 * g + kind) * CT
                acc = acc + _dot(x_ref[:, g * CT:(g + 1) * CT], w[pl.ds(row, CT), :])
        o_ref[...] = acc + _dot(c_ref[...], w[pl.ds(OFF_ZA, C_W), :])

    tok = lambda wd: pl.BlockSpec((tm, wd), lambda i: (i, 0))
    return pl.pallas_call(
        body, grid=(T // tm,), name="dhn",
        in_specs=[tok(A_W), tok(GW), tok(GW), tok(GW), tok(C_W), pl.BlockSpec(memory_space=pl.ANY)],
        out_specs=tok(D),
        out_shape=jax.ShapeDtypeStruct((T, D), F32),
        scratch_shapes=[pltpu.VMEM((NIN, D), BF16)],
        compiler_params=_cp(("arbitrary",), VMEM_BIG),
    )(*pieces, w_t)


def _dw_in(pieces, hn, tt=2048):
    nt = T // tt

    def body(a_ref, q_ref, k_ref, v_ref, c_ref, h_ref, o_ref, s_ref, acc, cs):
        j = pl.program_id(0)
        t = pl.program_id(1)

        def step(x_ref):
            xv = x_ref[...]
            p = _dot_tn(xv, h_ref[...])
            c = jnp.sum(xv.astype(F32), axis=0, keepdims=True)

            @pl.when(t == 0)
            def _():
                acc[...] = p
                cs[...] = c

            @pl.when(t > 0)
            def _():
                acc[...] += p
                cs[...] += c

        for x_ref, (lo, n) in zip((a_ref, q_ref, k_ref, v_ref, c_ref), PIECES):
            pl.when((j >= lo) & (j < lo + n))(lambda x_ref=x_ref: step(x_ref))

        @pl.when(t == nt - 1)
        def _():
            o_ref[...] = acc[...].astype(BF16)
            s_ref[...] = cs[...]

    def piece_spec(lo, n):
        def imap(j, t):
            used = (j >= lo) & (j < lo + n)
            return (jnp.where(used, t, 0), jnp.clip(j - lo, 0, n - 1))
        return pl.BlockSpec((tt, CT), imap)

    return pl.pallas_call(
        body, grid=(NCT, nt), name="dw_in",
        in_specs=[piece_spec(lo, n) for lo, n in PIECES] + [pl.BlockSpec((tt, D), lambda j, t: (t, 0))],
        out_specs=[pl.BlockSpec((CT, D), lambda j, t: (_perm(j), 0)), pl.BlockSpec((1, CT), lambda j, t: (0, _perm(j)))],
        out_shape=[jax.ShapeDtypeStruct((NIN, D), BF16), jax.ShapeDtypeStruct((1, NIN), F32)],
        scratch_shapes=[pltpu.VMEM((CT, D), F32), pltpu.VMEM((1, CT), F32)],
        compiler_params=_cp(("parallel", "arbitrary"), VMEM_MID),
    )(*pieces, hn)


def _mm_tn(a, b, ta, tt, name):
    m = a.shape[1]
    n = b.shape[1]
    nt = T // tt

    def body(a_ref, b_ref, o_ref, acc):
        t = pl.program_id(1)
        p = _dot_tn(a_ref[...].astype(BF16), b_ref[...].astype(BF16))

        @pl.when(t == 0)
        def _():
            acc[...] = p

        @pl.when(t > 0)
        def _():
            acc[...] += p

        @pl.when(t == nt - 1)
        def _():
            o_ref[...] = acc[...].astype(BF16)

    return pl.pallas_call(
        body, grid=(m // ta, nt), name=name,
        in_specs=[pl.BlockSpec((tt, ta), lambda j, t: (t, j)), pl.BlockSpec((tt, n), lambda j, t: (t, 0))],
        out_specs=pl.BlockSpec((ta, n), lambda j, t: (j, 0)),
        out_shape=jax.ShapeDtypeStruct((m, n), BF16),
        scratch_shapes=[pltpu.VMEM((ta, n), F32)],
        compiler_params=_cp(("parallel", "arbitrary"), VMEM_MID),
    )(a, b)


def _row_iota():
    return lax.broadcasted_iota(jnp.int32, (S, RBW), 0)


def _shift_down(v, d, row, fill):
    return jnp.where(row >= d, pltpu.roll(v, d, 0), fill)


def _shift_up(v, d, row, fill):
    return jnp.where(row < S - d, pltpu.roll(v, S - d, 0), fill)


def _neg_expm1(x):
    series = -x * (1.0 + x * (0.5 + x * (1.0 / 6.0 + x * (1.0 / 24.0))))
    return jnp.where(x > -0.03, series, 1.0 - jnp.exp(x))


def _softplus(x):
    return jnp.maximum(x, 0.0) + jnp.log1p(jnp.exp(-jnp.abs(x)))


def _rnn_gates(x, cw, cb, wa, ba, wx, bx, lam, row):
    xc = cb + cw[3:4, :] * x
    for j in (1, 2, 3):
        xc = xc + cw[3 - j:4 - j, :] * _shift_down(x, j, row, 0.0)
    xcb = xc.astype(BF16)
    r = _sigmoid(_dot(xcb, wa) + ba)
    i = _sigmoid(_dot(xcb, wx) + bx)
    sp = _softplus(-lam)
    log_a = (-LRU_C) * r * sp
    a = jnp.exp(log_a)
    mult = jnp.where(row == 0, 1.0, jnp.sqrt(_neg_expm1(2.0 * log_a)))
    return xc, xcb, r, i, sp, a, mult


def _rnn_fwd(proj3, conv_w, conv_b, wa, ba, wx, bx, lam):
    def body(x_ref, cw_ref, cb_ref, wa_ref, ba_ref, wx_ref, bx_ref, lam_ref, h_ref):
        row = _row_iota()
        x = x_ref[0]
        xc, _, _, i, _, a, mult = _rnn_gates(x, cw_ref[...], cb_ref[...], wa_ref[0], ba_ref[...],
                                             wx_ref[0], bx_ref[...], lam_ref[...], row)
        u = mult * (i * xc)
        d = 1
        while d < S:
            u = a * _shift_down(u, d, row, 0.0) + u
            if 2 * d < S:
                a = a * _shift_down(a, d, row, 1.0)
            d *= 2
        h_ref[0] = u

    vec = lambda: pl.BlockSpec((1, RBW), lambda b, n: (0, n))
    mat = lambda: pl.BlockSpec((1, RBW, RBW), lambda b, n: (n, 0, 0))
    return pl.pallas_call(
        body, grid=(BL, NRB), name="rnn_fwd",
        in_specs=[pl.BlockSpec((1, S, RBW), lambda b, n: (b, 0, n)),
                  pl.BlockSpec((CONVW, RBW), lambda b, n: (0, n)),
                  vec(), mat(), vec(), mat(), vec(), vec()],
        out_specs=pl.BlockSpec((1, S, RBW), lambda b, n: (b, 0, n)),
        out_shape=jax.ShapeDtypeStruct((BL, S, DR), F32),
        compiler_params=_cp(("parallel", "parallel"), VMEM_MID),
    )(proj3, conv_w, conv_b, wa, ba, wx, bx, lam)


def _rnn_bwd(proj3, h3, dh3, slab_a3, conv_w, conv_b, wa, ba, wx, bx, lam):
    def body(x_ref, h_ref, dh_ref, cw_ref, cb_ref, wa_ref, ba_ref, wx_ref, bx_ref, lam_ref, _alias,
             dx_ref, dcw_ref, dcb_ref, dwa_ref, dba_ref, dwx_ref, dbx_ref, dlam_ref):
        row = _row_iota()
        x = x_ref[0]
        cw = cw_ref[...]
        wa_v = wa_ref[0]
        wx_v = wx_ref[0]
        lam_v = lam_ref[...]
        xc, xcb, r, i, sp, a, mult = _rnn_gates(x, cw, cb_ref[...], wa_v, ba_ref[...], wx_v, bx_ref[...], lam_v, row)
        h = h_ref[0]
        g = dh_ref[0]
        bcoef = _shift_up(a, 1, row, 0.0)
        d = 1
        while d < S:
            g = g + bcoef * _shift_up(g, d, row, 0.0)
            if 2 * d < S:
                bcoef = bcoef * _shift_up(bcoef, d, row, 0.0)
            d *= 2
        da = g * _shift_down(h, 1, row, 0.0)
        dmult = jnp.where(row == 0, 0.0, g * (i * xc))
        gm = g * mult
        di = gm * xc
        dxc = gm * i
        dlog_a = da * a - dmult * (a * a) / mult
        dr = dlog_a * ((-LRU_C) * sp)
        dsp = jnp.sum(dlog_a * ((-LRU_C) * r), axis=0, keepdims=True)
        dlam = dsp * (-_sigmoid(-lam_v))
        dpa = dr * r * (1.0 - r)
        dpx = di * i * (1.0 - i)
        dpab = dpa.astype(BF16)
        dpxb = dpx.astype(BF16)
        dwa = _dot_tn(xcb, dpab)
        dwx = _dot_tn(xcb, dpxb)
        dba = jnp.sum(dpa, axis=0, keepdims=True)
        dbx = jnp.sum(dpx, axis=0, keepdims=True)
        dxc = dxc + _dot_nt(dpab, wa_v) + _dot_nt(dpxb, wx_v)
        dcb = jnp.sum(dxc, axis=0, keepdims=True)
        dx = cw[3:4, :] * dxc
        dcw_rows = [None] * CONVW
        dcw_rows[3] = jnp.sum(dxc * x, axis=0, keepdims=True)
        for j in (1, 2, 3):
            dx = dx + cw[3 - j:4 - j, :] * _shift_up(dxc, j, row, 0.0)
            dcw_rows[3 - j] = jnp.sum(dxc * _shift_down(x, j, row, 0.0), axis=0, keepdims=True)
        dx_ref[0] = dx.astype(BF16)
        dcw = jnp.concatenate(dcw_rows, axis=0)
        first = pl.program_id(1) == 0

        @pl.when(first)
        def _():
            dcw_ref[...] = dcw
            dcb_ref[...] = dcb
            dwa_ref[0] = dwa
            dba_ref[...] = dba
            dwx_ref[0] = dwx
            dbx_ref[...] = dbx
            dlam_ref[...] = dlam

        @pl.when(jnp.logical_not(first))
        def _():
            dcw_ref[...] += dcw
            dcb_ref[...] += dcb
            dwa_ref[0] += dwa
            dba_ref[...] += dba
            dwx_ref[0] += dwx
            dbx_ref[...] += dbx
            dlam_ref[...] += dlam

    slab = lambda: pl.BlockSpec((1, S, RBW), lambda n, b: (b, 0, n))
    vec = lambda: pl.BlockSpec((1, RBW), lambda n, b: (0, n))
    mat = lambda: pl.BlockSpec((1, RBW, RBW), lambda n, b: (n, 0, 0))
    taps = lambda: pl.BlockSpec((CONVW, RBW), lambda n, b: (0, n))
    vshape = jax.ShapeDtypeStruct((1, DR), F32)
    mshape = jax.ShapeDtypeStruct((NRB, RBW, RBW), F32)
    return pl.pallas_call(
        body, grid=(NRB, BL), name="rnn_bwd",
        in_specs=[slab(), slab(), slab(), taps(), vec(), mat(), vec(), mat(), vec(), vec(),
                  pl.BlockSpec(memory_space=pl.ANY)],
        out_specs=[slab(), taps(), vec(), mat(), vec(), mat(), vec(), vec()],
        out_shape=[jax.ShapeDtypeStruct((BL, S, A_W), BF16), jax.ShapeDtypeStruct((CONVW, DR), F32),
                   vshape, mshape, vshape, mshape, vshape, vshape],
        input_output_aliases={10: 0},
        compiler_params=_cp(("parallel", "arbitrary"), 48 * 1024 * 1024),
    )(proj3, h3, dh3, conv_w, conv_b, wa, ba, wx, bx, lam, slab_a3)


NQB = S // QB


def _rms_head(t, gain):
    rstd = lax.rsqrt(jnp.mean(t * t, axis=-1, keepdims=True) + EPS)
    return t * rstd * gain


def _rope(t, cs, sn):
    return t * cs + pltpu.roll(t, HD // 2, 1) * sn


def _rope_t(dy, cs, sn):
    return dy * cs - pltpu.roll(dy, HD // 2, 1) * sn


def _bdot_nt(a, b):
    return lax.dot_general(a, b, (((2,), (2,)), ((0,), (0,))), preferred_element_type=F32)


def _bdot(a, b):
    return lax.dot_general(a, b, (((2,), (1,)), ((0,), (0,))), preferred_element_type=F32)


def _bdot_tn(a, b):
    return lax.dot_general(a, b, (((1,), (1,)), ((0,), (0,))), preferred_element_type=F32)


STRIDE_MAX = 4


def _permute(buf, x, dil, dst, off=0):
    ln = S // dil
    if dil == 1:
        dst[pl.ds(off, S), :] = x.astype(dst.dtype)
        return
    buf[0] = x
    if dil <= STRIDE_MAX:
        for c in range(dil):
            dst[pl.ds(off + c * ln, ln), :] = buf.at[0][pl.ds(c, ln, stride=dil), :].astype(dst.dtype)
        return
    f, r = STRIDE_MAX, dil // STRIDE_MAX
    part = S // f
    for c1 in range(f):
        buf.at[1][pl.ds(c1 * part, part), :] = buf.at[0][pl.ds(c1, part, stride=f), :]
    for c1 in range(f):
        for c2 in range(r):
            dst[pl.ds(off + (c1 + f * c2) * ln, ln), :] = (
                buf.at[1][pl.ds(c1 * part + c2, ln, stride=r), :].astype(dst.dtype))


def _unpermute(buf, xp, dil, dst):
    ln = S // dil
    if dil == 1:
        dst[...] = xp
        return
    if dil <= STRIDE_MAX:
        for c in range(dil):
            dst[pl.ds(c, ln, stride=dil), :] = xp[c * ln:(c + 1) * ln]
        return
    f, r = STRIDE_MAX, dil // STRIDE_MAX
    part = S // f
    for c1 in range(f):
        for c2 in range(r):
            c = c1 + f * c2
            buf.at[1][pl.ds(c1 * part + c2, ln, stride=r), :] = xp[c * ln:(c + 1) * ln]
    for c1 in range(f):
        dst[pl.ds(c1, part, stride=f), :] = buf[1, pl.ds(c1 * part, part), :]


def _blocks3(ref, off=0):
    return ref[pl.ds(off, S), :].reshape(NQB, QB, HD)


def _att_prep(q_ref, k_ref, v_ref, cos_ref, sin_ref, qn, kn, dil, nat, qs, ksp, vsp):
    cs = cos_ref[...]
    sn = sin_ref[...]
    zero = jnp.zeros((QB, HD), BF16)
    ksp[pl.ds(0, QB), :] = zero
    vsp[pl.ds(0, QB), :] = zero
    _permute(nat, _rope(_rms_head(q_ref[0], qn), cs, sn), dil, qs)
    _permute(nat, _rope(_rms_head(k_ref[0], kn), cs, sn), dil, ksp, QB)
    _permute(nat, v_ref[0], dil, vsp, QB)


def _att_scores(qs, ksp, dil):
    nb = S // dil // QB
    q3 = _blocks3(qs)
    shape = (NQB, QB, QB)
    qi = lax.broadcasted_iota(jnp.int32, shape, 1)
    kj = lax.broadcasted_iota(jnp.int32, shape, 2)
    s_c = jnp.where(qi >= kj, _bdot_nt(q3, _blocks3(ksp, QB)) * SCALE, NEG)
    if nb == 1:
        return q3, s_c, None
    jj = lax.broadcasted_iota(jnp.int32, shape, 0)
    ok = (kj >= qi) & ((jj & (nb - 1)) != 0)
    s_p = jnp.where(ok, _bdot_nt(q3, _blocks3(ksp)) * SCALE, NEG)
    return q3, s_c, s_p


def _qkv_spec(kind, g):
    base = OFF_Q // HD + kind * (GW // HD) + g * NH
    return pl.BlockSpec((1, S, HD), lambda b, h: (b, 0, base + h))


def _attn_fwd(proj3, cos_t, sin_t, q_norm, k_norm):
    def body(*refs):
        qkv_refs = refs[:9]
        cos_ref, sin_ref, qn_ref, kn_ref, att_ref, lse_ref, w_ref, nat, qs, ksp, vsp, og = refs[9:]
        for g, (window, dil) in enumerate(PATTERNS):
            q_ref, k_ref, v_ref = qkv_refs[3 * g:3 * g + 3]
            _att_prep(q_ref, k_ref, v_ref, cos_ref, sin_ref, qn_ref[g:g + 1, :], kn_ref[g:g + 1, :], dil,
                      nat, qs, ksp, vsp)
            _, s_c, s_p = _att_scores(qs, ksp, dil)
            m = jnp.max(s_c, axis=-1, keepdims=True)
            if s_p is not None:
                m = jnp.maximum(m, jnp.max(s_p, axis=-1, keepdims=True))
            e_c = jnp.exp(s_c - m)
            den = jnp.sum(e_c, axis=-1, keepdims=True)
            o = _bdot(e_c.astype(BF16), _blocks3(vsp, QB))
            if s_p is not None:
                e_p = jnp.exp(s_p - m)
                den = den + jnp.sum(e_p, axis=-1, keepdims=True)
                o = o + _bdot(e_p.astype(BF16), _blocks3(vsp))
            _unpermute(nat, (o / den).reshape(S, HD), dil, og.at[g])
            _unpermute(nat, jnp.broadcast_to(m + jnp.log(den), (NQB, QB, HD)).reshape(S, HD), dil,
                       lse_ref.at[g, 0])
        l0 = lse_ref[0, 0]
        l1 = lse_ref[1, 0]
        l2 = lse_ref[2, 0]
        mx = jnp.maximum(jnp.maximum(l0, l1), l2)
        e0 = jnp.exp(l0 - mx)
        e1 = jnp.exp(l1 - mx)
        e2 = jnp.exp(l2 - mx)
        inv = 1.0 / (e0 + e1 + e2)
        w0 = e0 * inv
        w1 = e1 * inv
        w2 = e2 * inv
        w_ref[0, 0] = w0
        w_ref[1, 0] = w1
        w_ref[2, 0] = w2
        att_ref[0] = w0 * og[0] + w1 * og[1] + w2 * og[2]

    in_specs = [_qkv_spec(kind, g) for g in range(NG) for kind in range(3)]
    in_specs += [pl.BlockSpec((S, HD), lambda b, h: (0, 0)), pl.BlockSpec((S, HD), lambda b, h: (0, 0)),
                 pl.BlockSpec((NG, HD), lambda b, h: (0, 0)), pl.BlockSpec((NG, HD), lambda b, h: (0, 0))]
    stat = lambda: pl.BlockSpec((NG, 1, S, HD), lambda b, h: (0, b, 0, h))
    return pl.pallas_call(
        body, grid=(BL, NH), name="attn_fwd",
        in_specs=in_specs,
        out_specs=[pl.BlockSpec((1, S, HD), lambda b, h: (b, 0, h)), stat(), stat()],
        out_shape=[jax.ShapeDtypeStruct((BL, S, ATT), F32),
                   jax.ShapeDtypeStruct((NG, BL, S, ATT), F32),
                   jax.ShapeDtypeStruct((NG, BL, S, ATT), F32)],
        scratch_shapes=[pltpu.VMEM((2, S, HD), F32), pltpu.VMEM((S, HD), BF16), pltpu.VMEM((S + QB, HD), BF16),
                        pltpu.VMEM((S + QB, HD), BF16), pltpu.VMEM((NG, S, HD), F32)],
        compiler_params=_cp(("parallel", "parallel"), VMEM_BIG),
    )(*([proj3] * 9), cos_t, sin_t, q_norm, k_norm)


def _attn_bwd_group(g, proj3, cos_t, sin_t, qn_g, kn_g, lse, wts, datt3, sbar3, slabs):
    dil = PATTERNS[g][1]
    n_alias = 0 if slabs is None else 3

    def norm_rope_bwd(dpost, raw, gain, cs, sn):
        dn = _rope_t(dpost, cs, sn)
        rstd = lax.rsqrt(jnp.mean(raw * raw, axis=-1, keepdims=True) + EPS)
        xh = raw * rstd
        dgain = jnp.sum(dn * xh, axis=0, keepdims=True)
        gd = dn * gain
        draw = rstd * (gd - xh * jnp.mean(gd * xh, axis=-1, keepdims=True))
        return draw, dgain

    def body(*refs):
        (q_ref, k_ref, v_ref, cos_ref, sin_ref, qn_ref, kn_ref, lse_ref, w_ref, datt_ref, sbar_ref) = refs[:11]
        (dq_ref, dk_ref, dv_ref, dqn_ref, dkn_ref, nat, qs, ksp, vsp, dos, cvp, lsp, acc) = refs[11 + n_alias:]
        qn = qn_ref[...]
        kn = kn_ref[...]
        cs = cos_ref[...]
        sn = sin_ref[...]
        _att_prep(q_ref, k_ref, v_ref, cos_ref, sin_ref, qn, kn, dil, nat, qs, ksp, vsp)
        wv = w_ref[0, 0]
        _permute(nat, wv * datt_ref[0], dil, dos)
        _permute(nat, wv * sbar_ref[0], dil, cvp)
        _permute(nat, lse_ref[0, 0], dil, lsp)
        q3, s_c, s_p = _att_scores(qs, ksp, dil)
        do3 = _blocks3(dos)
        lse3 = _blocks3(lsp)[:, :, 0:1]
        cv3 = _blocks3(cvp)[:, :, 0:1]
        p_c = jnp.exp(s_c - lse3)
        ds_c = (p_c * (_bdot_nt(do3, _blocks3(vsp, QB)) - cv3)).astype(BF16)
        dq = _bdot(ds_c, _blocks3(ksp, QB))
        acc[0] = _bdot_tn(ds_c, q3).reshape(S, HD)
        acc[1] = _bdot_tn(p_c.astype(BF16), do3).reshape(S, HD)
        if s_p is not None:
            p_p = jnp.exp(s_p - lse3)
            ds_p = (p_p * (_bdot_nt(do3, _blocks3(vsp)) - cv3)).astype(BF16)
            dq = dq + _bdot(ds_p, _blocks3(ksp))
            early = pl.ds(0, S - QB)
            acc[0, early, :] += _bdot_tn(ds_p, q3).reshape(S, HD)[QB:]
            acc[1, early, :] += _bdot_tn(p_p.astype(BF16), do3).reshape(S, HD)[QB:]
        _unpermute(nat, (dq * SCALE).reshape(S, HD), dil, nat.at[0])
        draw, dqn = norm_rope_bwd(nat[0], q_ref[0], qn, cs, sn)
        dq_ref[0] = draw.astype(BF16)
        _unpermute(nat, acc[0] * SCALE, dil, nat.at[0])
        draw, dkn = norm_rope_bwd(nat[0], k_ref[0], kn, cs, sn)
        dk_ref[0] = draw.astype(BF16)
        _unpermute(nat, acc[1], dil, nat.at[0])
        dv_ref[0] = nat[0].astype(BF16)
        first = (pl.program_id(0) == 0) & (pl.program_id(1) == 0)

        @pl.when(first)
        def _():
            dqn_ref[...] = dqn
            dkn_ref[...] = dkn

        @pl.when(jnp.logical_not(first))
        def _():
            dqn_ref[...] += dqn
            dkn_ref[...] += dkn

    full = lambda r: pl.BlockSpec((r, HD), lambda b, h: (0, 0))
    stat = lambda: pl.BlockSpec((1, 1, S, HD), lambda b, h: (g, b, 0, h))
    slab = lambda: pl.BlockSpec((1, S, HD), lambda b, h: (b, 0, h))
    out_slab = lambda: pl.BlockSpec((1, S, HD), lambda b, h: (b, 0, g * NH + h))
    big = jax.ShapeDtypeStruct((BL, S, GW), BF16)
    vecs = jax.ShapeDtypeStruct((1, HD), F32)
    in_specs = [_qkv_spec(0, g), _qkv_spec(1, g), _qkv_spec(2, g), full(S), full(S), full(1), full(1),
                stat(), stat(), slab(), slab()]
    args = [proj3, proj3, proj3, cos_t, sin_t, qn_g, kn_g, lse, wts, datt3, sbar3]
    aliases = {}
    if slabs is not None:
        in_specs += [pl.BlockSpec(memory_space=pl.ANY)] * 3
        args += list(slabs)
        aliases = {11: 0, 12: 1, 13: 2}
    return pl.pallas_call(
        body, grid=(BL, NH), name="attn_bwd_g%d" % g,
        in_specs=in_specs,
        out_specs=[out_slab(), out_slab(), out_slab(), full(1), full(1)],
        out_shape=[big, big, big, vecs, vecs],
        scratch_shapes=[pltpu.VMEM((2, S, HD), F32), pltpu.VMEM((S, HD), BF16), pltpu.VMEM((S + QB, HD), BF16),
                        pltpu.VMEM((S + QB, HD), BF16), pltpu.VMEM((S, HD), BF16), pltpu.VMEM((S, HD), F32),
                        pltpu.VMEM((S, HD), F32), pltpu.VMEM((2, S, HD), F32)],
        input_output_aliases=aliases,
        compiler_params=_cp(("arbitrary", "arbitrary"), VMEM_BIG),
    )(*args)


def _tail(x, proj, h, att, p, tgt, w_o_rnn, w_o_att_t, w_out, w_pg, w_ple_t, norm_ple, b_pg, tm=256):
    nt = T // tm
    inv_d = 1.0 / D

    def body(x_ref, h_ref, zr_ref, att_ref, za_ref, g0a_ref, g0b_ref, g1a_ref, g1b_ref, p_ref, tgt_ref,
             np_ref, bpg_ref, wor_hbm, woa_hbm, wout_hbm, wpg_hbm, wple_hbm,
             dx1_ref, merged_ref, n1_ref, dpre_ref, dpe_ref, dyr_ref, dya_ref, slab_a_ref, slab_c_ref, dh_ref,
             datt_ref, sbar_ref, yrnn_ref, yatt_ref, loss_ref, dnp_ref, dbpg_ref,
             wor, woa, wout, wpg, wple):
        first = pl.program_id(0) == 0

        @pl.when(first)
        def _():
            pltpu.sync_copy(wor_hbm, wor)
            pltpu.sync_copy(woa_hbm, woa)
            pltpu.sync_copy(wout_hbm, wout)
            pltpu.sync_copy(wpg_hbm, wpg)
            pltpu.sync_copy(wple_hbm, wple)

        xv = x_ref[...]
        hv = h_ref[...]
        zr = zr_ref[...]
        av = att_ref[...]
        za = za_ref[...]
        szr = _sigmoid(zr)
        silu_r = zr * szr
        yrnn_b = (hv * silu_r).astype(BF16)
        sza = _sigmoid(za)
        silu_a = za * sza
        yatt_b = (av * silu_a).astype(BF16)
        yrnn_ref[...] = yrnn_b
        yatt_ref[...] = yatt_b
        yr = _dot(yrnn_b, wor[...])
        ya = _dot_nt(yatt_b, woa[...])
        g0 = _sigmoid(jnp.concatenate([g0a_ref[...], g0b_ref[...]], axis=1))
        g1 = _sigmoid(jnp.concatenate([g1a_ref[...], g1b_ref[...]], axis=1))
        merged_b = (g0 * yr + g1 * ya).astype(BF16)
        merged_ref[...] = merged_b
        x1 = xv + _dot(merged_b, wout[...])
        rstd = lax.rsqrt(jnp.mean(x1 * x1, axis=-1, keepdims=True) + EPS)
        xh = x1 * rstd
        npl = np_ref[...]
        n1_b = (xh * npl).astype(BF16)
        n1_ref[...] = n1_b
        pg = _sigmoid(_dot(n1_b, wpg[...]) + bpg_ref[...])
        pe = _dot_nt(p_ref[...].astype(BF16), wple[...])
        err = x1 + pg * pe - tgt_ref[...]
        loss_t = 0.5 * inv_d * jnp.sum(err * err)
        dy = err * inv_d
        dpe_ref[...] = (dy * pg).astype(BF16)
        dpre = dy * pe * pg * (1.0 - pg)
        dpre_b = dpre.astype(BF16)
        dpre_ref[...] = dpre_b
        dn1 = _dot_nt(dpre_b, wpg[...])
        dnp = jnp.sum(dn1 * xh, axis=0, keepdims=True)
        dbpg = jnp.sum(dpre, axis=0, keepdims=True)
        gd = dn1 * npl
        dx1 = dy + rstd * (gd - xh * jnp.mean(gd * xh, axis=-1, keepdims=True))
        dx1_ref[...] = dx1
        dmerged = _dot_nt(dx1.astype(BF16), wout[...])
        dyr_b = (dmerged * g0).astype(BF16)
        dya_b = (dmerged * g1).astype(BF16)
        dyr_ref[...] = dyr_b
        dya_ref[...] = dya_b
        slab_c_ref[:, ATT:ATT + D] = (dmerged * yr * g0 * (1.0 - g0)).astype(BF16)
        slab_c_ref[:, ATT + D:ATT + 2 * D] = (dmerged * ya * g1 * (1.0 - g1)).astype(BF16)
        dyrnn = _dot_nt(dyr_b, wor[...])
        dyatt = _dot(dya_b, woa[...])
        dh_ref[...] = dyrnn * silu_r
        slab_a_ref[...] = (dyrnn * hv * szr * (1.0 + zr * (1.0 - szr))).astype(BF16)
        datt = dyatt * silu_a
        datt_ref[...] = datt
        slab_c_ref[:, 0:ATT] = (dyatt * av * sza * (1.0 + za * (1.0 - sza))).astype(BF16)
        da = datt * av
        for hh in range(NH):
            seg = slice(hh * HD, (hh + 1) * HD)
            sbar_ref[:, seg] = jnp.broadcast_to(jnp.sum(da[:, seg], axis=-1, keepdims=True), (tm, HD))

        @pl.when(first)
        def _():
            loss_ref[...] = jnp.full((8, LANES), loss_t, F32)
            dnp_ref[...] = dnp
            dbpg_ref[...] = dbpg

        @pl.when(jnp.logical_not(first))
        def _():
            loss_ref[...] += jnp.full((8, LANES), loss_t, F32)
            dnp_ref[...] += dnp
            dbpg_ref[...] += dbpg

    tok = lambda w: pl.BlockSpec((tm, w), lambda i: (i, 0))
    col = lambda w, blk: pl.BlockSpec((tm, w), lambda i: (i, blk))
    vec = lambda: pl.BlockSpec((1, D), lambda i: (0, 0))
    hbm = lambda: pl.BlockSpec(memory_space=pl.ANY)
    gb = OFF_G // 512
    in_specs = [tok(D), tok(DR), col(DR, 1), tok(ATT), col(ATT, OFF_ZA // ATT),
                col(512, gb), col(512, gb + 1), col(512, gb + 2), col(512, gb + 3),
                tok(PLE), tok(D), vec(), vec(), hbm(), hbm(), hbm(), hbm(), hbm()]
    sh = lambda w, dt: jax.ShapeDtypeStruct((T, w), dt)
    out_shape = [sh(D, F32), sh(D, BF16), sh(D, BF16), sh(D, BF16), sh(D, BF16), sh(D, BF16), sh(D, BF16),
                 sh(A_W, BF16), sh(C_W, BF16), sh(DR, F32), sh(ATT, F32), sh(ATT, F32),
                 sh(DR, BF16), sh(ATT, BF16),
                 jax.ShapeDtypeStruct((8, LANES), F32), jax.ShapeDtypeStruct((1, D), F32),
                 jax.ShapeDtypeStruct((1, D), F32)]
    out_specs = [tok(D), tok(D), tok(D), tok(D), tok(D), tok(D), tok(D), col(DR, 1), tok(C_W), tok(DR),
                 tok(ATT), tok(ATT), tok(DR), tok(ATT),
                 pl.BlockSpec((8, LANES), lambda i: (0, 0)), vec(), vec()]
    return pl.pallas_call(
        body, grid=(nt,), name="tail_fwd_bwd",
        in_specs=in_specs, out_specs=out_specs, out_shape=out_shape,
        scratch_shapes=[pltpu.VMEM((DR, D), BF16), pltpu.VMEM((D, ATT), BF16), pltpu.VMEM((D, D), BF16),
                        pltpu.VMEM((D, D), BF16), pltpu.VMEM((D, PLE), BF16)],
        compiler_params=_cp(("arbitrary",), VMEM_BIG),
    )(x, h, proj, att, proj, proj, proj, proj, proj, p, tgt, norm_ple, b_pg, w_o_rnn, w_o_att_t, w_out, w_pg, w_ple_t)


def _input_norm_bwd(x, dhn, dx1, gain, tm=512):
    def body(x_ref, dhn_ref, dx1_ref, g_ref, dx_ref, dg_ref):
        xv = x_ref[...]
        rstd = lax.rsqrt(jnp.mean(xv * xv, axis=-1, keepdims=True) + EPS)
        xh = xv * rstd
        dn = dhn_ref[...]
        dg = jnp.sum(dn * xh, axis=0, keepdims=True)
        gd = dn * g_ref[...]
        dx_ref[...] = dx1_ref[...] + rstd * (gd - xh * jnp.mean(gd * xh, axis=-1, keepdims=True))
        first = pl.program_id(0) == 0

        @pl.when(first)
        def _():
            dg_ref[...] = dg

        @pl.when(jnp.logical_not(first))
        def _():
            dg_ref[...] += dg

    tok = lambda: pl.BlockSpec((tm, D), lambda i: (i, 0))
    vec = lambda: pl.BlockSpec((1, D), lambda i: (0, 0))
    return pl.pallas_call(
        body, grid=(T // tm,), name="input_norm_bwd",
        in_specs=[tok(), tok(), tok(), vec()], out_specs=[tok(), vec()],
        out_shape=[jax.ShapeDtypeStruct((T, D), F32), jax.ShapeDtypeStruct((1, D), F32)],
        compiler_params=_cp(("arbitrary",), VMEM_MID),
    )(x, dhn, dx1, gain)


def _rope_tables():
    pos = jnp.arange(S, dtype=F32)
    inv_freq = ROPE_THETA ** (-jnp.arange(0, HD, 2, dtype=F32) / HD)
    ang = pos[:, None] * inv_freq[None, :]
    cos, sin = jnp.cos(ang), jnp.sin(ang)
    return jnp.concatenate([cos, cos], axis=1), jnp.concatenate([-sin, sin], axis=1)


def _local_step(x, p, tgt, w_in_t, w_o_rnn, w_o_att_t, w_out, w_pg, w_ple_t, conv_w, norm_mix, b_in, conv_b,
                w_rg_a, b_rg_a, w_rg_x, b_rg_x, lam, q_norm, k_norm, norm_ple, b_pg):
    cos_t, sin_t = _rope_tables()
    wa_b = w_rg_a.astype(BF16)
    wx_b = w_rg_x.astype(BF16)

    hn = _rmsnorm_fwd(x, norm_mix)
    proj = _in_proj(hn, w_in_t, b_in)
    proj3 = proj.reshape(BL, S, NIN)
    h3 = _rnn_fwd(proj3, conv_w, conv_b, wa_b, b_rg_a, wx_b, b_rg_x, lam)
    att3, lse, wts = _attn_fwd(proj3, cos_t, sin_t, q_norm, k_norm)
    (dx1, merged, n1, dpre, dpe, dyr, dya, slab_a, slab_c, dh, datt, sbar, yrnn, yatt, loss8, dnp, dbpg) = _tail(
        x, proj, h3.reshape(T, DR), att3.reshape(T, ATT), p, tgt, w_o_rnn, w_o_att_t, w_out, w_pg, w_ple_t,
        norm_ple, b_pg)

    big = [None,
           _mm_tn(yrnn, dyr, 640, 512, "dw_o_rnn"),
           _mm_tn(dya, yatt, 512, 512, "dw_o_att_t"),
           _mm_tn(merged, dx1, 512, 512, "dw_out"),
           _mm_tn(n1, dpre, 512, 512, "dw_ple_gate"),
           _mm_tn(dpe, p, 512, 512, "dw_ple_t")]

    slab_a3, dcw, dcb, dwa, dba, dwx, dbx, dlam = _rnn_bwd(
        proj3, h3, dh.reshape(BL, S, DR), slab_a.reshape(BL, S, A_W), conv_w, conv_b, wa_b, b_rg_a, wx_b, b_rg_x, lam)
    datt3 = datt.reshape(BL, S, ATT)
    sbar3 = sbar.reshape(BL, S, ATT)
    slabs = None
    dqn = []
    dkn = []
    for g in range(NG):
        dq, dk, dv, dqn_g, dkn_g = _attn_bwd_group(g, proj3, cos_t, sin_t, q_norm[g:g + 1], k_norm[g:g + 1],
                                                   lse, wts, datt3, sbar3, slabs)
        slabs = (dq, dk, dv)
        dqn.append(dqn_g)
        dkn.append(dkn_g)
    pieces = [slab_a3.reshape(T, A_W)] + [t.reshape(T, GW) for t in slabs] + [slab_c]
    big[0], db_in = _dw_in(pieces, hn)
    dhn = _dhn(pieces, w_in_t)
    grad_x, dnm = _input_norm_bwd(x, dhn, dx1, norm_mix)

    small = dict(w_rg_a=dwa, w_rg_x=dwx, norm_mix=dnm, b_in=db_in, conv_b=dcb, b_rg_a=dba, b_rg_x=dbx,
                 lru_lambda=dlam, q_norm=dqn, k_norm=dkn, norm_ple=dnp, b_ple_gate=dbpg, conv_w=dcw)
    return loss8[0, 0], grad_x, big, small


MESH = pl.DeviceIdType.MESH
HBM_SPEC = pl.BlockSpec(memory_space=pl.ANY)


def _my_pos():
    return lax.axis_index("x"), lax.axis_index("y"), lax.axis_index("c")


def _flip(pos, k):
    x, y, c = pos
    return (1 - x if k & 4 else x, 1 - y if k & 2 else y, 1 - c if k & 1 else c)


def _lin(pos):
    return 4 * pos[0] + 2 * pos[1] + pos[2]


def _chip(pos):
    return 2 * pos[0] + pos[1]


def _all_gather_two_level(shards, name):
    na = len(shards)

    def body(*refs):
        x_refs = refs[:na]
        out_refs = refs[na:2 * na]
        send_sems, recv_sems, local_sems = refs[2 * na:]
        me = _my_pos()
        sibling = _flip(me, 1)
        chips = [_flip(me, 4), _flip(me, 2), _flip(me, 6)]

        def copy(i, k, block, to, from_x=False):
            dst = out_refs[i].at[_lin(block)]
            return pltpu.make_async_remote_copy(
                src_ref=x_refs[i] if from_x else dst, dst_ref=dst,
                send_sem=send_sems.at[7 * i + k], recv_sem=recv_sems.at[7 * i + k], device_id=to, device_id_type=MESH)

        started = []
        for i in range(na):
            mine = pltpu.make_async_copy(x_refs[i], out_refs[i].at[_lin(me)], local_sems.at[i])
            mine.start()
            started.append(mine)
        sends = []
        for i in range(na):
            cps = [copy(i, 0, me, sibling, True)] + [copy(i, 1 + j, me, chip, True) for j, chip in enumerate(chips)]
            for cp in cps:
                cp.start()
            sends += cps
        for i in range(na):
            for j, chip in enumerate(chips):
                copy(i, 1 + j, chip, me).wait_recv()
                fwd = copy(i, 4 + j, chip, sibling)
                fwd.start()
                sends.append(fwd)
        for i in range(na):
            copy(i, 0, sibling, me).wait_recv()
            for j, chip in enumerate(chips):
                copy(i, 4 + j, _flip(chip, 1), me).wait_recv()
        for cp in sends:
            cp.wait_send()
        for mine in started:
            mine.wait()

    return pl.pallas_call(
        body, name=name,
        out_shape=[jax.ShapeDtypeStruct((NDEV,) + s.shape, s.dtype) for s in shards],
        in_specs=[HBM_SPEC] * na, out_specs=[HBM_SPEC] * na,
        scratch_shapes=[pltpu.SemaphoreType.DMA((7 * na,)), pltpu.SemaphoreType.DMA((7 * na,)),
                        pltpu.SemaphoreType.DMA((na,))],
    )(*shards)


def _all_gather_direct(shard, name):
    def body(x_ref, out_ref, send_sems, recv_sems, local_sem):
        me = _my_pos()
        mine = pltpu.make_async_copy(x_ref, out_ref.at[_lin(me)], local_sem)
        mine.start()
        sends = []
        for k in range(1, NDEV):
            cp = pltpu.make_async_remote_copy(
                src_ref=x_ref, dst_ref=out_ref.at[_lin(me)], send_sem=send_sems.at[k - 1],
                recv_sem=recv_sems.at[k - 1], device_id=_flip(me, k), device_id_type=MESH)
            cp.start()
            sends.append(cp)
        for k in range(1, NDEV):
            peer = _flip(me, k)
            pltpu.make_async_remote_copy(
                src_ref=x_ref, dst_ref=out_ref.at[_lin(peer)], send_sem=send_sems.at[k - 1],
                recv_sem=recv_sems.at[k - 1], device_id=peer, device_id_type=MESH).wait_recv()
        for cp in sends:
            cp.wait_send()
        mine.wait()

    return pl.pallas_call(
        body, name=name,
        out_shape=jax.ShapeDtypeStruct((NDEV,) + shard.shape, shard.dtype),
        in_specs=[HBM_SPEC], out_specs=HBM_SPEC,
        scratch_shapes=[pltpu.SemaphoreType.DMA((7,)), pltpu.SemaphoreType.DMA((7,)), pltpu.SemaphoreType.DMA],
    )(shard)


def _exchange_within_chip(parts, name):
    na = len(parts)

    def body(*refs):
        a_refs = refs[:na]
        recv_refs = refs[na:2 * na]
        send_sems, recv_sems = refs[2 * na:]
        me = _my_pos()
        c = me[2]
        sibling = _flip(me, 1)
        remote = []
        for i in range(na):
            for q in range(NCHIP):
                rc = pltpu.make_async_remote_copy(
                    src_ref=a_refs[i].at[q, 1 - c], dst_ref=recv_refs[i].at[q],
                    send_sem=send_sems.at[NCHIP * i + q], recv_sem=recv_sems.at[NCHIP * i + q],
                    device_id=sibling, device_id_type=MESH)
                rc.start()
                remote.append(rc)
        for rc in remote:
            rc.wait_recv()
        for rc in remote:
            rc.wait_send()

    return pl.pallas_call(
        body, name=name, out_shape=[jax.ShapeDtypeStruct((NCHIP,) + a.shape[2:], a.dtype) for a in parts],
        in_specs=[HBM_SPEC] * na, out_specs=[HBM_SPEC] * na,
        scratch_shapes=[pltpu.SemaphoreType.DMA((NCHIP * na,)), pltpu.SemaphoreType.DMA((NCHIP * na,))],
    )(*parts)


def _exchange_between_chips(parts, name):
    na = len(parts)

    def body(*refs):
        a_refs = refs[:na]
        out_refs = refs[na:2 * na]
        send_sems, recv_sems = refs[2 * na:]
        me = _my_pos()
        myq = _chip(me)
        peers = [_flip(me, 4), _flip(me, 2), _flip(me, 6)]
        remote = []
        for i in range(na):
            for j, peer in enumerate(peers):
                rc = pltpu.make_async_remote_copy(
                    src_ref=a_refs[i].at[_chip(peer)], dst_ref=out_refs[i].at[myq],
                    send_sem=send_sems.at[3 * i + j], recv_sem=recv_sems.at[3 * i + j],
                    device_id=peer, device_id_type=MESH)
                rc.start()
                remote.append(rc)
        for i in range(na):
            for j, peer in enumerate(peers):
                pltpu.make_async_remote_copy(
                    src_ref=a_refs[i].at[myq], dst_ref=out_refs[i].at[_chip(peer)],
                    send_sem=send_sems.at[3 * i + j], recv_sem=recv_sems.at[3 * i + j],
                    device_id=peer, device_id_type=MESH).wait_recv()
        for rc in remote:
            rc.wait_send()

    return pl.pallas_call(
        body, name=name, out_shape=[jax.ShapeDtypeStruct(a.shape, a.dtype) for a in parts],
        in_specs=[HBM_SPEC] * na, out_specs=[HBM_SPEC] * na,
        scratch_shapes=[pltpu.SemaphoreType.DMA((3 * na,)), pltpu.SemaphoreType.DMA((3 * na,))],
    )(*parts)


def _scalar(v):
    return jnp.asarray(v, jnp.int32).reshape(1)


def _sum_pairs(parts, theirs):
    na = len(parts)

    def body(c_ref, *refs):
        for i in range(na):
            o_ref = refs[2 * na + i]
            o_ref[0] = (refs[i][0, 0].astype(F32) + refs[na + i][0].astype(F32)).astype(o_ref.dtype)

    def mine_spec(a):
        return pl.BlockSpec((1, 1) + a.shape[2:], lambda q, c_ref: (q, c_ref[0], 0, 0))

    def spec(a):
        return pl.BlockSpec((1,) + a.shape[1:], lambda q, c_ref: (q, 0, 0))

    return pl.pallas_call(
        body, name="sum_pairs",
        grid_spec=pltpu.PrefetchScalarGridSpec(
            num_scalar_prefetch=1, grid=(NCHIP,),
            in_specs=[mine_spec(a) for a in parts] + [spec(a) for a in theirs],
            out_specs=[spec(a) for a in theirs]),
        out_shape=[jax.ShapeDtypeStruct(a.shape, a.dtype) for a in theirs],
        compiler_params=_cp(("arbitrary",), VMEM_BIG),
    )(_scalar(lax.axis_index("c")), *parts, *theirs)


def _others(q, myq):
    return jnp.where(q == myq, (q + 1) % NCHIP, q)


def _sum_chips_rows(own, recv, tr, name):
    _, r, w = recv.shape

    def body(q_ref, own_ref, r0, r1, r2, r3, o_ref):
        myq = q_ref[0]
        acc = None
        for q, r_ref in enumerate((r0, r1, r2, r3)):
            term = jnp.where(myq == q, own_ref[0], r_ref[0]).astype(F32)
            acc = term if acc is None else acc + term
        o_ref[...] = acc

    def recv_spec(q):
        return pl.BlockSpec((1, tr, w), lambda i, q_ref: (_others(q, q_ref[0]), i, 0))

    return pl.pallas_call(
        body, name=name,
        grid_spec=pltpu.PrefetchScalarGridSpec(
            num_scalar_prefetch=1, grid=(r // tr,),
            in_specs=[pl.BlockSpec((1, tr, w), lambda i, q_ref: (q_ref[0], i, 0))] + [recv_spec(q) for q in range(NCHIP)],
            out_specs=pl.BlockSpec((tr, w), lambda i, q_ref: (i, 0))),
        out_shape=jax.ShapeDtypeStruct((r, w), F32),
        compiler_params=_cp(("arbitrary",), VMEM_MID),
    )(_scalar(_chip(_my_pos())), own, recv, recv, recv, recv)


def _sum_chips_small(own, recv, transpose):
    na = len(recv)

    def body(q_ref, *refs):
        myq = q_ref[0]
        for i in range(na):
            acc = None
            for q in range(NCHIP):
                term = jnp.where(myq == q, refs[i][0], refs[na * (1 + q) + i][0]).astype(F32)
                acc = term if acc is None else acc + term
            refs[na * (1 + NCHIP) + i][...] = acc.T if transpose[i] else acc

    def oshape(a, tr):
        r, w = a.shape[1:]
        return (w, r) if tr else (r, w)

    own_spec = lambda a: pl.BlockSpec((1,) + a.shape[1:], lambda s, q_ref: (q_ref[0], 0, 0))
    recv_spec = lambda a, q: pl.BlockSpec((1,) + a.shape[1:], lambda s, q_ref: (_others(q, q_ref[0]), 0, 0))
    out_spec = lambda shp: pl.BlockSpec(shp, lambda s, q_ref: (0, 0))
    in_specs = [own_spec(a) for a in own]
    for q in range(NCHIP):
        in_specs += [recv_spec(a, q) for a in recv]
    return pl.pallas_call(
        body, name="sum_chips_small",
        grid_spec=pltpu.PrefetchScalarGridSpec(
            num_scalar_prefetch=1, grid=(1,), in_specs=in_specs,
            out_specs=[out_spec(oshape(a, tr)) for a, tr in zip(recv, transpose)]),
        out_shape=[jax.ShapeDtypeStruct(oshape(a, tr), F32) for a, tr in zip(recv, transpose)],
        compiler_params=_cp(("arbitrary",), VMEM_MID),
    )(_scalar(_chip(_my_pos())), *own, *(list(recv) * NCHIP))


def _rep_offsets():
    offs = []
    o = 0
    for r in REP_ROWS:
        offs.append(o)
        o += r
    return offs


def _pack_small_grads(g):
    offs = _rep_offsets()

    def body(dwa, dwx, dnm, dbin, dcb, dba, dbx, dlam, dq0, dq1, dq2, dk0, dk1, dk2, dnp, dbpg, o_ref):
        o_ref[pl.ds(REP_TOTAL_ROWS - 2, NDEV * REP_ROWS_DEV - REP_TOTAL_ROWS + 2), :] = jnp.zeros(
            (NDEV * REP_ROWS_DEV - REP_TOTAL_ROWS + 2, LANES), F32)
        for n in range(NRB):
            o_ref[pl.ds(offs[0] + n * RBW, RBW), :] = dwa[n]
            o_ref[pl.ds(offs[1] + n * RBW, RBW), :] = dwx[n]

        def put_vec(off, ref, rows):
            for k in range(rows):
                o_ref[pl.ds(off + k, 1), :] = ref[:, k * LANES:(k + 1) * LANES]

        put_vec(offs[2], dnm, REP_ROWS[2])
        put_vec(offs[3], dbin, REP_ROWS[3])
        put_vec(offs[4], dcb, REP_ROWS[4])
        put_vec(offs[5], dba, REP_ROWS[5])
        put_vec(offs[6], dbx, REP_ROWS[6])
        put_vec(offs[7], dlam, REP_ROWS[7])
        for k, ref in enumerate((dq0, dq1, dq2)):
            o_ref[pl.ds(offs[8] + k, 1), :] = ref[...]
        for k, ref in enumerate((dk0, dk1, dk2)):
            o_ref[pl.ds(offs[9] + k, 1), :] = ref[...]
        put_vec(offs[10], dnp, REP_ROWS[10])
        put_vec(offs[11], dbpg, REP_ROWS[11])

    args = [g["w_rg_a"], g["w_rg_x"], g["norm_mix"], g["b_in"], g["conv_b"], g["b_rg_a"], g["b_rg_x"],
            g["lru_lambda"], *g["q_norm"], *g["k_norm"], g["norm_ple"], g["b_ple_gate"]]
    full = lambda shp: pl.BlockSpec(shp, lambda: (0,) * len(shp))
    return pl.pallas_call(
        body, name="pack_small_grads",
        in_specs=[full(a.shape) for a in args],
        out_specs=full((NDEV * REP_ROWS_DEV, LANES)),
        out_shape=jax.ShapeDtypeStruct((NDEV * REP_ROWS_DEV, LANES), F32),
    )(*args)


def _adam_math(wv, gv, mv, vv):
    c1 = 1.0 - B1 ** STEP
    c2 = 1.0 - B2 ** STEP
    m2 = B1 * mv + (1.0 - B1) * gv
    v2 = B2 * vv + (1.0 - B2) * (gv * gv)
    delta = (-LR) * ((m2 / c1) / (jnp.sqrt(v2 / c2) + AEPS) + WD * wv)
    return delta, m2, v2


def _adamw_small(rep_flat, w, m, v):
    offs = _rep_offsets()
    n = len(REP_NAMES)

    def body(*refs):
        g_ref = refs[0]
        w_refs = refs[1:1 + n]
        m_refs = refs[1 + n:1 + 2 * n]
        v_refs = refs[1 + 2 * n:1 + 3 * n]
        outs = refs[1 + 3 * n:]
        go, do, mo, vo = outs[:n], outs[n:2 * n], outs[2 * n:3 * n], outs[3 * n:]

        def emit(i, idx, gv):
            go[i][idx] = gv
            delta, m2, v2 = _adam_math(w_refs[i][idx], gv, m_refs[i][idx], v_refs[i][idx])
            do[i][idx] = delta
            mo[i][idx] = m2
            vo[i][idx] = v2

        for i in range(n):
            if i < 2:
                for b in range(NRB):
                    emit(i, b, g_ref[pl.ds(offs[i] + b * RBW, RBW), :])
            elif REP_NAMES[i] in ("q_norm", "k_norm"):
                emit(i, slice(None), g_ref[pl.ds(offs[i], NG), :])
            else:
                gv = jnp.concatenate([g_ref[pl.ds(offs[i] + k, 1), :] for k in range(REP_ROWS[i])], axis=1)
                emit(i, slice(None), gv)

    full = lambda shp: pl.BlockSpec(shp, lambda: (0,) * len(shp))
    pspecs = [full(a.shape) for a in w]
    pshapes = [jax.ShapeDtypeStruct(a.shape, F32) for a in w]
    res = pl.pallas_call(
        body, name="adamw_small",
        in_specs=[full(rep_flat.shape)] + pspecs * 3,
        out_specs=pspecs * 4, out_shape=pshapes * 4,
        compiler_params=_cp(None, VMEM_MID),
    )(rep_flat, *w, *m, *v)
    return res[:n], res[n:2 * n], res[2 * n:3 * n], res[3 * n:]


def _adamw_rows(w, g, m, v, tr, name):
    r, c_ = w.shape

    def body(w_ref, g_ref, m_ref, v_ref, d_ref, m2_ref, v2_ref):
        delta, m2, v2 = _adam_math(w_ref[...], g_ref[...], m_ref[...], v_ref[...])
        d_ref[...] = delta
        m2_ref[...] = m2
        v2_ref[...] = v2

    spec = lambda: pl.BlockSpec((tr, c_), lambda i: (i, 0))
    shp = jax.ShapeDtypeStruct((r, c_), F32)
    return pl.pallas_call(
        body, grid=(r // tr,), name=name,
        in_specs=[spec(), spec(), spec(), spec()], out_specs=[spec(), spec(), spec()],
        out_shape=[shp, shp, shp], compiler_params=_cp(("parallel",)),
    )(w, g, m, v)


def _adamw_many(w, g, m, v):
    n = len(w)

    def body(*refs):
        for i in range(n):
            delta, m2, v2 = _adam_math(refs[i][...], refs[n + i][...], refs[2 * n + i][...], refs[3 * n + i][...])
            refs[4 * n + i][...] = delta
            refs[5 * n + i][...] = m2
            refs[6 * n + i][...] = v2

    full = lambda shp: pl.BlockSpec(shp, lambda: (0,) * len(shp))
    specs = [full(a.shape) for a in w]
    shapes = [jax.ShapeDtypeStruct(a.shape, F32) for a in w]
    res = pl.pallas_call(
        body, name="adamw_shards",
        in_specs=specs * 4, out_specs=specs * 3, out_shape=shapes * 3,
        compiler_params=_cp(None, VMEM_MID),
    )(*w, *g, *m, *v)
    return res[:n], res[n:2 * n], res[2 * n:]


def kernel(x, p, norm_mix, w_in, b_in, conv_w, conv_b, w_rg_a, b_rg_a, w_rg_x, b_rg_x, lru_lambda, q_norm, k_norm, w_o_rnn, w_o_att, w_out, norm_ple, w_ple_gate, b_ple_gate, w_ple, loss_target, m_norm_mix, m_w_in, m_b_in, m_conv_w, m_conv_b, m_w_rg_a, m_b_rg_a, m_w_rg_x, m_b_rg_x, m_lru_lambda, m_q_norm, m_k_norm, m_w_o_rnn, m_w_o_att, m_w_out, m_norm_ple, m_w_ple_gate, m_b_ple_gate, m_w_ple, v_norm_mix, v_w_in, v_b_in, v_conv_w, v_conv_b, v_w_rg_a, v_b_rg_a, v_w_rg_x, v_b_rg_x, v_lru_lambda, v_q_norm, v_k_norm, v_w_o_rnn, v_w_o_att, v_w_out, v_norm_ple, v_w_ple_gate, v_b_ple_gate, v_w_ple):
    w = dict(norm_mix=norm_mix, w_in=w_in, b_in=b_in, conv_w=conv_w, conv_b=conv_b, w_rg_a=w_rg_a, b_rg_a=b_rg_a,
             w_rg_x=w_rg_x, b_rg_x=b_rg_x, lru_lambda=lru_lambda, q_norm=q_norm, k_norm=k_norm, w_o_rnn=w_o_rnn,
             w_o_att=w_o_att, w_out=w_out, norm_ple=norm_ple, w_ple_gate=w_ple_gate, b_ple_gate=b_ple_gate,
             w_ple=w_ple)
    m = dict(norm_mix=m_norm_mix, w_in=m_w_in, b_in=m_b_in, conv_w=m_conv_w, conv_b=m_conv_b, w_rg_a=m_w_rg_a,
             b_rg_a=m_b_rg_a, w_rg_x=m_w_rg_x, b_rg_x=m_b_rg_x, lru_lambda=m_lru_lambda, q_norm=m_q_norm,
             k_norm=m_k_norm, w_o_rnn=m_w_o_rnn, w_o_att=m_w_o_att, w_out=m_w_out, norm_ple=m_norm_ple,
             w_ple_gate=m_w_ple_gate, b_ple_gate=m_b_ple_gate, w_ple=m_w_ple)
    v = dict(norm_mix=v_norm_mix, w_in=v_w_in, b_in=v_b_in, conv_w=v_conv_w, conv_b=v_conv_b, w_rg_a=v_w_rg_a,
             b_rg_a=v_b_rg_a, w_rg_x=v_w_rg_x, b_rg_x=v_b_rg_x, lru_lambda=v_lru_lambda, q_norm=v_q_norm,
             k_norm=v_k_norm, w_o_rnn=v_w_o_rnn, w_o_att=v_w_o_att, w_out=v_w_out, norm_ple=v_norm_ple,
             w_ple_gate=v_w_ple_gate, b_ple_gate=v_b_ple_gate, w_ple=v_w_ple)
    names = list(w.keys())

    shards = [w_in[0].T.astype(BF16), w_o_rnn[0].astype(BF16), w_o_att[0].T.astype(BF16), w_out[0].astype(BF16),
              w_ple_gate[0].astype(BF16), w_ple[0].T.astype(BF16), conv_w[0]]
    gat = _all_gather_two_level(shards, "gather_weights")
    w_in_t, w_o_rnn_f, w_o_att_t, w_out_f, w_pg_f, w_ple_t = [
        a.reshape((NDEV * a.shape[1], a.shape[2])) for a in gat[:6]]
    conv_f = gat[6].transpose(1, 0, 2).reshape(CONVW, DR)

    loss_part, grad_x, big, small = _local_step(
        x.reshape(T, D), p.reshape(T, PLE), loss_target.reshape(T, D),
        w_in_t, w_o_rnn_f, w_o_att_t, w_out_f, w_pg_f, w_ple_t, conv_f,
        norm_mix, b_in, conv_b, w_rg_a[0], b_rg_a, w_rg_x[0], b_rg_x, lru_lambda, q_norm[0], k_norm[0],
        norm_ple, b_ple_gate)
    loss = lax.psum(loss_part, ("x", "y", "c"))

    rep_parts = _pack_small_grads(small)
    conv_parts = small["conv_w"].reshape(CONVW, NDEV, DR // NDEV).transpose(1, 0, 2)
    parts = [a.reshape((NCHIP, 2, a.shape[0] // NDEV, a.shape[1])) for a in big]
    parts += [rep_parts.reshape(NCHIP, 2, REP_ROWS_DEV, LANES), conv_parts.reshape(NCHIP, 2, CONVW, DR // NDEV)]
    theirs = _exchange_within_chip(parts, "reduce_within_chip")
    chip_sums = _sum_pairs(parts, theirs)
    recv = _exchange_between_chips(chip_sums, "reduce_between_chips")
    g_in_t = _sum_chips_rows(chip_sums[0], recv[0], 304, "sum_chips_w_in")
    g_o_rnn, g_o_att, g_out, g_pg, g_ple, g_rep, g_conv = _sum_chips_small(
        chip_sums[1:], recv[1:], (False, True, False, False, True, False, False))
    rep_all = _all_gather_direct(g_rep, "gather_small").reshape(NDEV * REP_ROWS_DEV, LANES)

    grad, delta, new_m, new_v = {}, {}, {}, {}
    rep_shape = lambda a: a if a.ndim == 2 else a.reshape(a.shape[1:])
    res = _adamw_small(rep_all, [rep_shape(w[n]) for n in REP_NAMES], [rep_shape(m[n]) for n in REP_NAMES],
                       [rep_shape(v[n]) for n in REP_NAMES])
    for dst, vals in zip((grad, delta, new_m, new_v), res):
        for n, a in zip(REP_NAMES, vals):
            dst[n] = a.reshape(w[n].shape)
    g_in = g_in_t.T
    d_, m_, v_ = _adamw_rows(w_in[0], g_in, m_w_in[0], v_w_in[0], 128, "adamw_w_in")
    grad["w_in"], delta["w_in"], new_m["w_in"], new_v["w_in"] = g_in[None], d_[None], m_[None], v_[None]
    rest = ("w_o_rnn", "w_o_att", "w_out", "w_ple_gate", "w_ple", "conv_w")
    g_rest = [g_o_rnn, g_o_att, g_out, g_pg, g_ple, g_conv]
    res = _adamw_many([w[n][0] for n in rest], g_rest, [m[n][0] for n in rest], [v[n][0] for n in rest])
    for n, a in zip(rest, g_rest):
        grad[n] = a[None]
    for dst, vals in zip((delta, new_m, new_v), res):
        for n, a in zip(rest, vals):
            dst[n] = a[None]

    return (loss, grad_x.reshape(BL, S, D), *[grad[n] for n in names], *[delta[n] for n in names],
            *[new_m[n] for n in names], *[new_v[n] for n in names])
```

```python
import jax
import jax.numpy as jnp
from jax import lax
from jax.experimental import pallas as pl
from jax.experimental.pallas import tpu as pltpu

F32 = jnp.float32
BF16 = jnp.bfloat16

D = 1024
S = 2048
BL = 2
T = BL * S
NDEV = 8
NCHIP = 4
PLE = 256
DR = 1280
NRB = 10
RBW = 128
CONVW = 4
LRU_C = 8.0
HD = 128
NH = 4
PATTERNS = ((128, 1), (512, 4), (2048, 16))
NG = 3
ATT = NH * HD
GW = NG * ATT
NIN = 2 * DR + 3 * GW + ATT + 2 * D
OFF_ZR = DR
OFF_Q = 2 * DR
OFF_ZA = OFF_Q + 3 * GW
OFF_G = OFF_ZA + ATT
ROPE_THETA = 10000.0
EPS = 1e-6
SCALE = HD ** -0.5
NEG = -1e30
QB = 128
LANES = 128
CT = 512
NCT = NIN // CT
A_W = 2 * DR
C_W = ATT + 2 * D

LR, B1, B2, AEPS, WD, STEP = 0.001, 0.9, 0.999, 1e-08, 0.01, 10

NSHARD_IN = NIN // NDEV
REP_NAMES = ("w_rg_a", "w_rg_x", "norm_mix", "b_in", "conv_b", "b_rg_a", "b_rg_x", "lru_lambda", "q_norm",
             "k_norm", "norm_ple", "b_ple_gate")
REP_ROWS = (NRB * RBW, NRB * RBW, D // LANES, NIN // LANES, DR // LANES, DR // LANES, DR // LANES, DR // LANES,
            NG, NG, D // LANES, D // LANES)
REP_TOTAL_ROWS = sum(REP_ROWS)
REP_ROWS_DEV = 344
BIG_NAMES = ("w_in", "w_o_rnn", "w_o_att", "w_out", "w_ple_gate", "w_ple")

VMEM_BIG = 56 * 1024 * 1024
VMEM_MID = 40 * 1024 * 1024


def _cp(sem=None, vmem=None):
    return pltpu.CompilerParams(dimension_semantics=sem, vmem_limit_bytes=vmem)


def _dot(a, b):
    return jnp.dot(a, b, preferred_element_type=F32)


def _dot_nt(a, b):
    return lax.dot_general(a, b, (((1,), (1,)), ((), ())), preferred_element_type=F32)


def _dot_tn(a, b):
    return lax.dot_general(a, b, (((0,), (0,)), ((), ())), preferred_element_type=F32)


def _sigmoid(x):
    return jax.nn.sigmoid(x)


def _perm(j):
    jq = j - OFF_Q // CT
    inside = (j >= OFF_Q // CT) & (j < OFF_ZA // CT)
    return jnp.where(inside, OFF_Q // CT + (jq % 3) * 3 + jq // 3, j)


PIECES = ((0, A_W // CT), (OFF_Q // CT, GW // CT), (OFF_Q // CT + 3, GW // CT), (OFF_Q // CT + 6, GW // CT),
          (OFF_ZA // CT, C_W // CT))


def _rmsnorm_fwd(x, gain, tm=512):
    def body(x_ref, g_ref, o_ref):
        xv = x_ref[...]
        var = jnp.mean(xv * xv, axis=-1, keepdims=True)
        o_ref[...] = (xv * lax.rsqrt(var + EPS) * g_ref[...]).astype(BF16)

    return pl.pallas_call(
        body, grid=(T // tm,), name="rmsnorm_fwd",
        in_specs=[pl.BlockSpec((tm, D), lambda i: (i, 0)), pl.BlockSpec((1, D), lambda i: (0, 0))],
        out_specs=pl.BlockSpec((tm, D), lambda i: (i, 0)),
        out_shape=jax.ShapeDtypeStruct((T, D), BF16),
        compiler_params=_cp(("parallel",)),
    )(x, gain)


def _in_proj(hn, w_t, bias, tm=1024):
    def body(a_ref, w_ref, b_ref, o_ref):
        o_ref[...] = _dot_nt(a_ref[...], w_ref[...]) + b_ref[...]

    return pl.pallas_call(
        body, grid=(T // tm, NCT), name="in_proj",
        in_specs=[pl.BlockSpec((tm, D), lambda i, j: (i, 0)),
                  pl.BlockSpec((CT, D), lambda i, j: (_perm(j), 0)),
                  pl.BlockSpec((1, CT), lambda i, j: (0, _perm(j)))],
        out_specs=pl.BlockSpec((tm, CT), lambda i, j: (i, j)),
        out_shape=jax.ShapeDtypeStruct((T, NIN), F32),
        compiler_params=_cp(("parallel", "parallel"), VMEM_MID),
    )(hn, w_t, bias)


def _dhn(pieces, w_t, token, tm=512):
    def body(a_ref, q_ref, k_ref, v_ref, c_ref, w_hbm, _token, o_ref, w):
        @pl.when(pl.program_id(0) == 0)
        def _():
            pltpu.sync_copy(w_hbm, w)

        acc = _dot(a_ref[...], w[pl.ds(0, A_W), :])
        for kind, x_ref in enumerate((q_ref, k_ref, v_ref)):
            for g in range(NG):
                row = OFF_Q + (3 * g + kind) * CT
                acc = acc + _dot(x_ref[:, g * CT:(g + 1) * CT], w[pl.ds(row, CT), :])
        o_ref[...] = acc + _dot(c_ref[...], w[pl.ds(OFF_ZA, C_W), :])

    tok = lambda wd: pl.BlockSpec((tm, wd), lambda i: (i, 0))
    return pl.pallas_call(
        body, grid=(T // tm,), name="dhn",
        in_specs=[tok(A_W), tok(GW), tok(GW), tok(GW), tok(C_W), pl.BlockSpec(memory_space=pl.ANY),
                  pl.BlockSpec((8, LANES), lambda i: (0, 0))],
        out_specs=tok(D),
        out_shape=jax.ShapeDtypeStruct((T, D), F32),
        scratch_shapes=[pltpu.VMEM((NIN, D), BF16)],
        compiler_params=_cp(("arbitrary",), VMEM_BIG),
    )(*pieces, w_t, token)


def _dw_in(pieces, hn, tt=2048):
    nt = T // tt

    def body(a_ref, q_ref, k_ref, v_ref, c_ref, h_ref, o_ref, s_ref, acc, cs):
        j = pl.program_id(0)
        t = pl.program_id(1)

        def step(x_ref):
            xv = x_ref[...]
            p = _dot_tn(xv, h_ref[...])
            c = jnp.sum(xv.astype(F32), axis=0, keepdims=True)

            @pl.when(t == 0)
            def _():
                acc[...] = p
                cs[...] = c

            @pl.when(t > 0)
            def _():
                acc[...] += p
                cs[...] += c

        for x_ref, (lo, n) in zip((a_ref, q_ref, k_ref, v_ref, c_ref), PIECES):
            pl.when((j >= lo) & (j < lo + n))(lambda x_ref=x_ref: step(x_ref))

        @pl.when(t == nt - 1)
        def _():
            o_ref[...] = acc[...].astype(BF16)
            s_ref[...] = cs[...]

    def piece_spec(lo, n):
        def imap(j, t):
            used = (j >= lo) & (j < lo + n)
            return (jnp.where(used, t, 0), jnp.clip(j - lo, 0, n - 1))
        return pl.BlockSpec((tt, CT), imap)

    return pl.pallas_call(
        body, grid=(NCT, nt), name="dw_in",
        in_specs=[piece_spec(lo, n) for lo, n in PIECES] + [pl.BlockSpec((tt, D), lambda j, t: (t, 0))],
        out_specs=[pl.BlockSpec((CT, D), lambda j, t: (_perm(j), 0)), pl.BlockSpec((1, CT), lambda j, t: (0, _perm(j)))],
        out_shape=[jax.ShapeDtypeStruct((NIN, D), BF16), jax.ShapeDtypeStruct((1, NIN), F32)],
        scratch_shapes=[pltpu.VMEM((CT, D), F32), pltpu.VMEM((1, CT), F32)],
        compiler_params=_cp(("parallel", "arbitrary"), VMEM_MID),
    )(*pieces, hn)


def _mm_tn(a, b, ta, tt, name):
    m = a.shape[1]
    n = b.shape[1]
    nt = T // tt

    def body(a_ref, b_ref, o_ref, acc):
        t = pl.program_id(1)
        p = _dot_tn(a_ref[...].astype(BF16), b_ref[...].astype(BF16))

        @pl.when(t == 0)
        def _():
            acc[...] = p

        @pl.when(t > 0)
        def _():
            acc[...] += p

        @pl.when(t == nt - 1)
        def _():
            o_ref[...] = acc[...].astype(BF16)

    return pl.pallas_call(
        body, grid=(m // ta, nt), name=name,
        in_specs=[pl.BlockSpec((tt, ta), lambda j, t: (t, j)), pl.BlockSpec((tt, n), lambda j, t: (t, 0))],
        out_specs=pl.BlockSpec((ta, n), lambda j, t: (j, 0)),
        out_shape=jax.ShapeDtypeStruct((m, n), BF16),
        scratch_shapes=[pltpu.VMEM((ta, n), F32)],
        compiler_params=_cp(("parallel", "arbitrary"), VMEM_MID),
    )(a, b)


def _row_iota():
    return lax.broadcasted_iota(jnp.int32, (S, RBW), 0)


def _shift_down(v, d, row, fill):
    return jnp.where(row >= d, pltpu.roll(v, d, 0), fill)


def _shift_up(v, d, row, fill):
    return jnp.where(row < S - d, pltpu.roll(v, S - d, 0), fill)


def _neg_expm1(x):
    series = -x * (1.0 + x * (0.5 + x * (1.0 / 6.0 + x * (1.0 / 24.0))))
    return jnp.where(x > -0.03, series, 1.0 - jnp.exp(x))


def _softplus(x):
    return jnp.maximum(x, 0.0) + jnp.log1p(jnp.exp(-jnp.abs(x)))


def _rnn_gates(x, cw, cb, wa, ba, wx, bx, lam, row):
    xc = cb + cw[3:4, :] * x
    for j in (1, 2, 3):
        xc = xc + cw[3 - j:4 - j, :] * _shift_down(x, j, row, 0.0)
    xcb = xc.astype(BF16)
    r = _sigmoid(_dot(xcb, wa) + ba)
    i = _sigmoid(_dot(xcb, wx) + bx)
    sp = _softplus(-lam)
    log_a = (-LRU_C) * r * sp
    a = jnp.exp(log_a)
    mult = jnp.where(row == 0, 1.0, jnp.sqrt(_neg_expm1(2.0 * log_a)))
    return xc, xcb, r, i, sp, a, mult


def _rnn_fwd(proj3, conv_w, conv_b, wa, ba, wx, bx, lam):
    def body(x_ref, cw_ref, cb_ref, wa_ref, ba_ref, wx_ref, bx_ref, lam_ref, h_ref):
        row = _row_iota()
        x = x_ref[0]
        xc, _, _, i, _, a, mult = _rnn_gates(x, cw_ref[...], cb_ref[...], wa_ref[0], ba_ref[...],
                                             wx_ref[0], bx_ref[...], lam_ref[...], row)
        u = mult * (i * xc)
        d = 1
        while d < S:
            u = a * _shift_down(u, d, row, 0.0) + u
            if 2 * d < S:
                a = a * _shift_down(a, d, row, 1.0)
            d *= 2
        h_ref[0] = u

    vec = lambda: pl.BlockSpec((1, RBW), lambda b, n: (0, n))
    mat = lambda: pl.BlockSpec((1, RBW, RBW), lambda b, n: (n, 0, 0))
    return pl.pallas_call(
        body, grid=(BL, NRB), name="rnn_fwd",
        in_specs=[pl.BlockSpec((1, S, RBW), lambda b, n: (b, 0, n)),
                  pl.BlockSpec((CONVW, RBW), lambda b, n: (0, n)),
                  vec(), mat(), vec(), mat(), vec(), vec()],
        out_specs=pl.BlockSpec((1, S, RBW), lambda b, n: (b, 0, n)),
        out_shape=jax.ShapeDtypeStruct((BL, S, DR), F32),
        compiler_params=_cp(("parallel", "parallel"), VMEM_MID),
    )(proj3, conv_w, conv_b, wa, ba, wx, bx, lam)


def _rnn_bwd(proj3, h3, dh3, slab_a3, conv_w, conv_b, wa, ba, wx, bx, lam, token):
    def body(x_ref, h_ref, dh_ref, cw_ref, cb_ref, wa_ref, ba_ref, wx_ref, bx_ref, lam_ref, _alias, _token,
             dx_ref, dcw_ref, dcb_ref, dwa_ref, dba_ref, dwx_ref, dbx_ref, dlam_ref):
        row = _row_iota()
        x = x_ref[0]
        cw = cw_ref[...]
        wa_v = wa_ref[0]
        wx_v = wx_ref[0]
        lam_v = lam_ref[...]
        xc, xcb, r, i, sp, a, mult = _rnn_gates(x, cw, cb_ref[...], wa_v, ba_ref[...], wx_v, bx_ref[...], lam_v, row)
        h = h_ref[0]
        g = dh_ref[0]
        bcoef = _shift_up(a, 1, row, 0.0)
        d = 1
        while d < S:
            g = g + bcoef * _shift_up(g, d, row, 0.0)
            if 2 * d < S:
                bcoef = bcoef * _shift_up(bcoef, d, row, 0.0)
            d *= 2
        da = g * _shift_down(h, 1, row, 0.0)
        dmult = jnp.where(row == 0, 0.0, g * (i * xc))
        gm = g * mult
        di = gm * xc
        dxc = gm * i
        dlog_a = da * a - dmult * (a * a) / mult
        dr = dlog_a * ((-LRU_C) * sp)
        dsp = jnp.sum(dlog_a * ((-LRU_C) * r), axis=0, keepdims=True)
        dlam = dsp * (-_sigmoid(-lam_v))
        dpa = dr * r * (1.0 - r)
        dpx = di * i * (1.0 - i)
        dpab = dpa.astype(BF16)
        dpxb = dpx.astype(BF16)
        dwa = _dot_tn(xcb, dpab)
        dwx = _dot_tn(xcb, dpxb)
        dba = jnp.sum(dpa, axis=0, keepdims=True)
        dbx = jnp.sum(dpx, axis=0, keepdims=True)
        dxc = dxc + _dot_nt(dpab, wa_v) + _dot_nt(dpxb, wx_v)
        dcb = jnp.sum(dxc, axis=0, keepdims=True)
        dx = cw[3:4, :] * dxc
        dcw_rows = [None] * CONVW
        dcw_rows[3] = jnp.sum(dxc * x, axis=0, keepdims=True)
        for j in (1, 2, 3):
            dx = dx + cw[3 - j:4 - j, :] * _shift_up(dxc, j, row, 0.0)
            dcw_rows[3 - j] = jnp.sum(dxc * _shift_down(x, j, row, 0.0), axis=0, keepdims=True)
        dx_ref[0] = dx.astype(BF16)
        dcw = jnp.concatenate(dcw_rows, axis=0)
        first = pl.program_id(1) == 0

        @pl.when(first)
        def _():
            dcw_ref[...] = dcw
            dcb_ref[...] = dcb
            dwa_ref[0] = dwa
            dba_ref[...] = dba
            dwx_ref[0] = dwx
            dbx_ref[...] = dbx
            dlam_ref[...] = dlam

        @pl.when(jnp.logical_not(first))
        def _():
            dcw_ref[...] += dcw
            dcb_ref[...] += dcb
            dwa_ref[0] += dwa
            dba_ref[...] += dba
            dwx_ref[0] += dwx
            dbx_ref[...] += dbx
            dlam_ref[...] += dlam

    slab = lambda: pl.BlockSpec((1, S, RBW), lambda n, b: (b, 0, n))
    vec = lambda: pl.BlockSpec((1, RBW), lambda n, b: (0, n))
    mat = lambda: pl.BlockSpec((1, RBW, RBW), lambda n, b: (n, 0, 0))
    taps = lambda: pl.BlockSpec((CONVW, RBW), lambda n, b: (0, n))
    vshape = jax.ShapeDtypeStruct((1, DR), F32)
    mshape = jax.ShapeDtypeStruct((NRB, RBW, RBW), F32)
    return pl.pallas_call(
        body, grid=(NRB, BL), name="rnn_bwd",
        in_specs=[slab(), slab(), slab(), taps(), vec(), mat(), vec(), mat(), vec(), vec(),
                  pl.BlockSpec(memory_space=pl.ANY), pl.BlockSpec((8, LANES), lambda n, b: (0, 0))],
        out_specs=[slab(), taps(), vec(), mat(), vec(), mat(), vec(), vec()],
        out_shape=[jax.ShapeDtypeStruct((BL, S, A_W), BF16), jax.ShapeDtypeStruct((CONVW, DR), F32),
                   vshape, mshape, vshape, mshape, vshape, vshape],
        input_output_aliases={10: 0},
        compiler_params=_cp(("parallel", "arbitrary"), 48 * 1024 * 1024),
    )(proj3, h3, dh3, conv_w, conv_b, wa, ba, wx, bx, lam, slab_a3, token)


NQB = S // QB


def _rms_head(t, gain):
    rstd = lax.rsqrt(jnp.mean(t * t, axis=-1, keepdims=True) + EPS)
    return t * rstd * gain


def _rope(t, cs, sn):
    return t * cs + pltpu.roll(t, HD // 2, 1) * sn


def _rope_t(dy, cs, sn):
    return dy * cs - pltpu.roll(dy, HD // 2, 1) * sn


def _bdot_nt(a, b):
    return lax.dot_general(a, b, (((2,), (2,)), ((0,), (0,))), preferred_element_type=F32)


def _bdot(a, b):
    return lax.dot_general(a, b, (((2,), (1,)), ((0,), (0,))), preferred_element_type=F32)


def _bdot_tn(a, b):
    return lax.dot_general(a, b, (((1,), (1,)), ((0,), (0,))), preferred_element_type=F32)


STRIDE_MAX = 4


def _permute(buf, x, dil, dst, off=0):
    ln = S // dil
    if dil == 1:
        dst[pl.ds(off, S), :] = x.astype(dst.dtype)
        return
    buf[0] = x
    if dil <= STRIDE_MAX:
        for c in range(dil):
            dst[pl.ds(off + c * ln, ln), :] = buf.at[0][pl.ds(c, ln, stride=dil), :].astype(dst.dtype)
        return
    f, r = STRIDE_MAX, dil // STRIDE_MAX
    part = S // f
    for c1 in range(f):
        buf.at[1][pl.ds(c1 * part, part), :] = buf.at[0][pl.ds(c1, part, stride=f), :]
    for c1 in range(f):
        for c2 in range(r):
            dst[pl.ds(off + (c1 + f * c2) * ln, ln), :] = (
                buf.at[1][pl.ds(c1 * part + c2, ln, stride=r), :].astype(dst.dtype))


def _unpermute(buf, xp, dil, dst):
    ln = S // dil
    if dil == 1:
        dst[...] = xp
        return
    if dil <= STRIDE_MAX:
        for c in range(dil):
            dst[pl.ds(c, ln, stride=dil), :] = xp[c * ln:(c + 1) * ln]
        return
    f, r = STRIDE_MAX, dil // STRIDE_MAX
    part = S // f
    for c1 in range(f):
        for c2 in range(r):
            c = c1 + f * c2
            buf.at[1][pl.ds(c1 * part + c2, ln, stride=r), :] = xp[c * ln:(c + 1) * ln]
    for c1 in range(f):
        dst[pl.ds(c1, part, stride=f), :] = buf[1, pl.ds(c1 * part, part), :]


def _blocks3(ref, off=0):
    return ref[pl.ds(off, S), :].reshape(NQB, QB, HD)


def _att_prep(q_ref, k_ref, v_ref, cos_ref, sin_ref, qn, kn, dil, nat, qs, ksp, vsp):
    cs = cos_ref[...]
    sn = sin_ref[...]
    zero = jnp.zeros((QB, HD), BF16)
    ksp[pl.ds(0, QB), :] = zero
    vsp[pl.ds(0, QB), :] = zero
    _permute(nat, _rope(_rms_head(q_ref[0], qn), cs, sn), dil, qs)
    _permute(nat, _rope(_rms_head(k_ref[0], kn), cs, sn), dil, ksp, QB)
    _permute(nat, v_ref[0], dil, vsp, QB)


def _att_scores(qs, ksp, dil):
    nb = S // dil // QB
    q3 = _blocks3(qs)
    shape = (NQB, QB, QB)
    qi = lax.broadcasted_iota(jnp.int32, shape, 1)
    kj = lax.broadcasted_iota(jnp.int32, shape, 2)
    s_c = jnp.where(qi >= kj, _bdot_nt(q3, _blocks3(ksp, QB)) * SCALE, NEG)
    if nb == 1:
        return q3, s_c, None
    jj = lax.broadcasted_iota(jnp.int32, shape, 0)
    ok = (kj >= qi) & ((jj & (nb - 1)) != 0)
    s_p = jnp.where(ok, _bdot_nt(q3, _blocks3(ksp)) * SCALE, NEG)
    return q3, s_c, s_p


def _qkv_spec(kind, g):
    base = OFF_Q // HD + kind * (GW // HD) + g * NH
    return pl.BlockSpec((1, S, HD), lambda b, h: (b, 0, base + h))


def _attn_fwd(proj3, cos_t, sin_t, q_norm, k_norm):
    def body(*refs):
        qkv_refs = refs[:9]
        cos_ref, sin_ref, qn_ref, kn_ref, att_ref, lse_ref, w_ref, nat, qs, ksp, vsp, og = refs[9:]
        for g, (window, dil) in enumerate(PATTERNS):
            q_ref, k_ref, v_ref = qkv_refs[3 * g:3 * g + 3]
            _att_prep(q_ref, k_ref, v_ref, cos_ref, sin_ref, qn_ref[g:g + 1, :], kn_ref[g:g + 1, :], dil,
                      nat, qs, ksp, vsp)
            _, s_c, s_p = _att_scores(qs, ksp, dil)
            m = jnp.max(s_c, axis=-1, keepdims=True)
            if s_p is not None:
                m = jnp.maximum(m, jnp.max(s_p, axis=-1, keepdims=True))
            e_c = jnp.exp(s_c - m)
            den = jnp.sum(e_c, axis=-1, keepdims=True)
            o = _bdot(e_c.astype(BF16), _blocks3(vsp, QB))
            if s_p is not None:
                e_p = jnp.exp(s_p - m)
                den = den + jnp.sum(e_p, axis=-1, keepdims=True)
                o = o + _bdot(e_p.astype(BF16), _blocks3(vsp))
            _unpermute(nat, (o / den).reshape(S, HD), dil, og.at[g])
            _unpermute(nat, jnp.broadcast_to(m + jnp.log(den), (NQB, QB, HD)).reshape(S, HD), dil,
                       lse_ref.at[g, 0])
        l0 = lse_ref[0, 0]
        l1 = lse_ref[1, 0]
        l2 = lse_ref[2, 0]
        mx = jnp.maximum(jnp.maximum(l0, l1), l2)
        e0 = jnp.exp(l0 - mx)
        e1 = jnp.exp(l1 - mx)
        e2 = jnp.exp(l2 - mx)
        inv = 1.0 / (e0 + e1 + e2)
        w0 = e0 * inv
        w1 = e1 * inv
        w2 = e2 * inv
        w_ref[0, 0] = w0
        w_ref[1, 0] = w1
        w_ref[2, 0] = w2
        att_ref[0] = w0 * og[0] + w1 * og[1] + w2 * og[2]

    in_specs = [_qkv_spec(kind, g) for g in range(NG) for kind in range(3)]
    in_specs += [pl.BlockSpec((S, HD), lambda b, h: (0, 0)), pl.BlockSpec((S, HD), lambda b, h: (0, 0)),
                 pl.BlockSpec((NG, HD), lambda b, h: (0, 0)), pl.BlockSpec((NG, HD), lambda b, h: (0, 0))]
    stat = lambda: pl.BlockSpec((NG, 1, S, HD), lambda b, h: (0, b, 0, h))
    return pl.pallas_call(
        body, grid=(BL, NH), name="attn_fwd",
        in_specs=in_specs,
        out_specs=[pl.BlockSpec((1, S, HD), lambda b, h: (b, 0, h)), stat(), stat()],
        out_shape=[jax.ShapeDtypeStruct((BL, S, ATT), F32),
                   jax.ShapeDtypeStruct((NG, BL, S, ATT), F32),
                   jax.ShapeDtypeStruct((NG, BL, S, ATT), F32)],
        scratch_shapes=[pltpu.VMEM((2, S, HD), F32), pltpu.VMEM((S, HD), BF16), pltpu.VMEM((S + QB, HD), BF16),
                        pltpu.VMEM((S + QB, HD), BF16), pltpu.VMEM((NG, S, HD), F32)],
        compiler_params=_cp(("parallel", "parallel"), VMEM_BIG),
    )(*([proj3] * 9), cos_t, sin_t, q_norm, k_norm)


def _attn_bwd_group(g, proj3, cos_t, sin_t, qn_g, kn_g, lse, wts, datt3, sbar3, slabs):
    dil = PATTERNS[g][1]
    n_alias = 0 if slabs is None else 3

    def norm_rope_bwd(dpost, raw, gain, cs, sn):
        dn = _rope_t(dpost, cs, sn)
        rstd = lax.rsqrt(jnp.mean(raw * raw, axis=-1, keepdims=True) + EPS)
        xh = raw * rstd
        dgain = jnp.sum(dn * xh, axis=0, keepdims=True)
        gd = dn * gain
        draw = rstd * (gd - xh * jnp.mean(gd * xh, axis=-1, keepdims=True))
        return draw, dgain

    def body(*refs):
        (q_ref, k_ref, v_ref, cos_ref, sin_ref, qn_ref, kn_ref, lse_ref, w_ref, datt_ref, sbar_ref) = refs[:11]
        (dq_ref, dk_ref, dv_ref, dqn_ref, dkn_ref, nat, qs, ksp, vsp, dos, cvp, lsp, acc) = refs[11 + n_alias:]
        qn = qn_ref[...]
        kn = kn_ref[...]
        cs = cos_ref[...]
        sn = sin_ref[...]
        _att_prep(q_ref, k_ref, v_ref, cos_ref, sin_ref, qn, kn, dil, nat, qs, ksp, vsp)
        wv = w_ref[0, 0]
        _permute(nat, wv * datt_ref[0], dil, dos)
        _permute(nat, wv * sbar_ref[0], dil, cvp)
        _permute(nat, lse_ref[0, 0], dil, lsp)
        q3, s_c, s_p = _att_scores(qs, ksp, dil)
        do3 = _blocks3(dos)
        lse3 = _blocks3(lsp)[:, :, 0:1]
        cv3 = _blocks3(cvp)[:, :, 0:1]
        p_c = jnp.exp(s_c - lse3)
        ds_c = (p_c * (_bdot_nt(do3, _blocks3(vsp, QB)) - cv3)).astype(BF16)
        dq = _bdot(ds_c, _blocks3(ksp, QB))
        acc[0] = _bdot_tn(ds_c, q3).reshape(S, HD)
        acc[1] = _bdot_tn(p_c.astype(BF16), do3).reshape(S, HD)
        if s_p is not None:
            p_p = jnp.exp(s_p - lse3)
            ds_p = (p_p * (_bdot_nt(do3, _blocks3(vsp)) - cv3)).astype(BF16)
            dq = dq + _bdot(ds_p, _blocks3(ksp))
            early = pl.ds(0, S - QB)
            acc[0, early, :] += _bdot_tn(ds_p, q3).reshape(S, HD)[QB:]
            acc[1, early, :] += _bdot_tn(p_p.astype(BF16), do3).reshape(S, HD)[QB:]
        _unpermute(nat, (dq * SCALE).reshape(S, HD), dil, nat.at[0])
        draw, dqn = norm_rope_bwd(nat[0], q_ref[0], qn, cs, sn)
        dq_ref[0] = draw.astype(BF16)
        _unpermute(nat, acc[0] * SCALE, dil, nat.at[0])
        draw, dkn = norm_rope_bwd(nat[0], k_ref[0], kn, cs, sn)
        dk_ref[0] = draw.astype(BF16)
        _unpermute(nat, acc[1], dil, nat.at[0])
        dv_ref[0] = nat[0].astype(BF16)
        first = (pl.program_id(0) == 0) & (pl.program_id(1) == 0)

        @pl.when(first)
        def _():
            dqn_ref[...] = dqn
            dkn_ref[...] = dkn

        @pl.when(jnp.logical_not(first))
        def _():
            dqn_ref[...] += dqn
            dkn_ref[...] += dkn

    full = lambda r: pl.BlockSpec((r, HD), lambda b, h: (0, 0))
    stat = lambda: pl.BlockSpec((1, 1, S, HD), lambda b, h: (g, b, 0, h))
    slab = lambda: pl.BlockSpec((1, S, HD), lambda b, h: (b, 0, h))
    out_slab = lambda: pl.BlockSpec((1, S, HD), lambda b, h: (b, 0, g * NH + h))
    big = jax.ShapeDtypeStruct((BL, S, GW), BF16)
    vecs = jax.ShapeDtypeStruct((1, HD), F32)
    in_specs = [_qkv_spec(0, g), _qkv_spec(1, g), _qkv_spec(2, g), full(S), full(S), full(1), full(1),
                stat(), stat(), slab(), slab()]
    args = [proj3, proj3, proj3, cos_t, sin_t, qn_g, kn_g, lse, wts, datt3, sbar3]
    aliases = {}
    if slabs is not None:
        in_specs += [pl.BlockSpec(memory_space=pl.ANY)] * 3
        args += list(slabs)
        aliases = {11: 0, 12: 1, 13: 2}
    return pl.pallas_call(
        body, grid=(BL, NH), name="attn_bwd_g%d" % g,
        in_specs=in_specs,
        out_specs=[out_slab(), out_slab(), out_slab(), full(1), full(1)],
        out_shape=[big, big, big, vecs, vecs],
        scratch_shapes=[pltpu.VMEM((2, S, HD), F32), pltpu.VMEM((S, HD), BF16), pltpu.VMEM((S + QB, HD), BF16),
                        pltpu.VMEM((S + QB, HD), BF16), pltpu.VMEM((S, HD), BF16), pltpu.VMEM((S, HD), F32),
                        pltpu.VMEM((S, HD), F32), pltpu.VMEM((2, S, HD), F32)],
        input_output_aliases=aliases,
        compiler_params=_cp(("arbitrary", "arbitrary"), VMEM_BIG),
    )(*args)


def _tail(x, proj, h, att, p, tgt, w_o_rnn, w_o_att_t, w_out, w_pg, w_ple_t, norm_ple, b_pg, tm=256):
    nt = T // tm
    inv_d = 1.0 / D

    def body(x_ref, h_ref, zr_ref, att_ref, za_ref, g0a_ref, g0b_ref, g1a_ref, g1b_ref, p_ref, tgt_ref,
             np_ref, bpg_ref, wor_hbm, woa_hbm, wout_hbm, wpg_hbm, wple_hbm,
             dx1_ref, merged_ref, n1_ref, dpre_ref, dpe_ref, dyr_ref, dya_ref, slab_a_ref, slab_c_ref, dh_ref,
             datt_ref, sbar_ref, yrnn_ref, yatt_ref, loss_ref, dnp_ref, dbpg_ref,
             wor, woa, wout, wpg, wple):
        first = pl.program_id(0) == 0

        @pl.when(first)
        def _():
            pltpu.sync_copy(wor_hbm, wor)
            pltpu.sync_copy(woa_hbm, woa)
            pltpu.sync_copy(wout_hbm, wout)
            pltpu.sync_copy(wpg_hbm, wpg)
            pltpu.sync_copy(wple_hbm, wple)

        xv = x_ref[...]
        hv = h_ref[...]
        zr = zr_ref[...]
        av = att_ref[...]
        za = za_ref[...]
        szr = _sigmoid(zr)
        silu_r = zr * szr
        yrnn_b = (hv * silu_r).astype(BF16)
        sza = _sigmoid(za)
        silu_a = za * sza
        yatt_b = (av * silu_a).astype(BF16)
        yrnn_ref[...] = yrnn_b
        yatt_ref[...] = yatt_b
        yr = _dot(yrnn_b, wor[...])
        ya = _dot_nt(yatt_b, woa[...])
        g0 = _sigmoid(jnp.concatenate([g0a_ref[...], g0b_ref[...]], axis=1))
        g1 = _sigmoid(jnp.concatenate([g1a_ref[...], g1b_ref[...]], axis=1))
        merged_b = (g0 * yr + g1 * ya).astype(BF16)
        merged_ref[...] = merged_b
        x1 = xv + _dot(merged_b, wout[...])
        rstd = lax.rsqrt(jnp.mean(x1 * x1, axis=-1, keepdims=True) + EPS)
        xh = x1 * rstd
        npl = np_ref[...]
        n1_b = (xh * npl).astype(BF16)
        n1_ref[...] = n1_b
        pg = _sigmoid(_dot(n1_b, wpg[...]) + bpg_ref[...])
        pe = _dot_nt(p_ref[...].astype(BF16), wple[...])
        err = x1 + pg * pe - tgt_ref[...]
        loss_t = 0.5 * inv_d * jnp.sum(err * err)
        dy = err * inv_d
        dpe_ref[...] = (dy * pg).astype(BF16)
        dpre = dy * pe * pg * (1.0 - pg)
        dpre_b = dpre.astype(BF16)
        dpre_ref[...] = dpre_b
        dn1 = _dot_nt(dpre_b, wpg[...])
        dnp = jnp.sum(dn1 * xh, axis=0, keepdims=True)
        dbpg = jnp.sum(dpre, axis=0, keepdims=True)
        gd = dn1 * npl
        dx1 = dy + rstd * (gd - xh * jnp.mean(gd * xh, axis=-1, keepdims=True))
        dx1_ref[...] = dx1
        dmerged = _dot_nt(dx1.astype(BF16), wout[...])
        dyr_b = (dmerged * g0).astype(BF16)
        dya_b = (dmerged * g1).astype(BF16)
        dyr_ref[...] = dyr_b
        dya_ref[...] = dya_b
        slab_c_ref[:, ATT:ATT + D] = (dmerged * yr * g0 * (1.0 - g0)).astype(BF16)
        slab_c_ref[:, ATT + D:ATT + 2 * D] = (dmerged * ya * g1 * (1.0 - g1)).astype(BF16)
        dyrnn = _dot_nt(dyr_b, wor[...])
        dyatt = _dot(dya_b, woa[...])
        dh_ref[...] = dyrnn * silu_r
        slab_a_ref[...] = (dyrnn * hv * szr * (1.0 + zr * (1.0 - szr))).astype(BF16)
        datt = dyatt * silu_a
        datt_ref[...] = datt
        slab_c_ref[:, 0:ATT] = (dyatt * av * sza * (1.0 + za * (1.0 - sza))).astype(BF16)
        da = datt * av
        for hh in range(NH):
            seg = slice(hh * HD, (hh + 1) * HD)
            sbar_ref[:, seg] = jnp.broadcast_to(jnp.sum(da[:, seg], axis=-1, keepdims=True), (tm, HD))

        @pl.when(first)
        def _():
            loss_ref[...] = jnp.full((8, LANES), loss_t, F32)
            dnp_ref[...] = dnp
            dbpg_ref[...] = dbpg

        @pl.when(jnp.logical_not(first))
        def _():
            loss_ref[...] += jnp.full((8, LANES), loss_t, F32)
            dnp_ref[...] += dnp
            dbpg_ref[...] += dbpg

    tok = lambda w: pl.BlockSpec((tm, w), lambda i: (i, 0))
    col = lambda w, blk: pl.BlockSpec((tm, w), lambda i: (i, blk))
    vec = lambda: pl.BlockSpec((1, D), lambda i: (0, 0))
    hbm = lambda: pl.BlockSpec(memory_space=pl.ANY)
    gb = OFF_G // 512
    in_specs = [tok(D), tok(DR), col(DR, 1), tok(ATT), col(ATT, OFF_ZA // ATT),
                col(512, gb), col(512, gb + 1), col(512, gb + 2), col(512, gb + 3),
                tok(PLE), tok(D), vec(), vec(), hbm(), hbm(), hbm(), hbm(), hbm()]
    sh = lambda w, dt: jax.ShapeDtypeStruct((T, w), dt)
    out_shape = [sh(D, F32), sh(D, BF16), sh(D, BF16), sh(D, BF16), sh(D, BF16), sh(D, BF16), sh(D, BF16),
                 sh(A_W, BF16), sh(C_W, BF16), sh(DR, F32), sh(ATT, F32), sh(ATT, F32),
                 sh(DR, BF16), sh(ATT, BF16),
                 jax.ShapeDtypeStruct((8, LANES), F32), jax.ShapeDtypeStruct((1, D), F32),
                 jax.ShapeDtypeStruct((1, D), F32)]
    out_specs = [tok(D), tok(D), tok(D), tok(D), tok(D), tok(D), tok(D), col(DR, 1), tok(C_W), tok(DR),
                 tok(ATT), tok(ATT), tok(DR), tok(ATT),
                 pl.BlockSpec((8, LANES), lambda i: (0, 0)), vec(), vec()]
    return pl.pallas_call(
        body, grid=(nt,), name="tail_fwd_bwd",
        in_specs=in_specs, out_specs=out_specs, out_shape=out_shape,
        scratch_shapes=[pltpu.VMEM((DR, D), BF16), pltpu.VMEM((D, ATT), BF16), pltpu.VMEM((D, D), BF16),
                        pltpu.VMEM((D, D), BF16), pltpu.VMEM((D, PLE), BF16)],
        compiler_params=_cp(("arbitrary",), VMEM_BIG),
    )(x, h, proj, att, proj, proj, proj, proj, proj, p, tgt, norm_ple, b_pg, w_o_rnn, w_o_att_t, w_out, w_pg, w_ple_t)


def _input_norm_bwd(x, dhn, dx1, gain, tm=512):
    def body(x_ref, dhn_ref, dx1_ref, g_ref, dx_ref, dg_ref):
        xv = x_ref[...]
        rstd = lax.rsqrt(jnp.mean(xv * xv, axis=-1, keepdims=True) + EPS)
        xh = xv * rstd
        dn = dhn_ref[...]
        dg = jnp.sum(dn * xh, axis=0, keepdims=True)
        gd = dn * g_ref[...]
        dx_ref[...] = dx1_ref[...] + rstd * (gd - xh * jnp.mean(gd * xh, axis=-1, keepdims=True))
        first = pl.program_id(0) == 0

        @pl.when(first)
        def _():
            dg_ref[...] = dg

        @pl.when(jnp.logical_not(first))
        def _():
            dg_ref[...] += dg

    tok = lambda: pl.BlockSpec((tm, D), lambda i: (i, 0))
    vec = lambda: pl.BlockSpec((1, D), lambda i: (0, 0))
    return pl.pallas_call(
        body, grid=(T // tm,), name="input_norm_bwd",
        in_specs=[tok(), tok(), tok(), vec()], out_specs=[tok(), vec()],
        out_shape=[jax.ShapeDtypeStruct((T, D), F32), jax.ShapeDtypeStruct((1, D), F32)],
        compiler_params=_cp(("arbitrary",), VMEM_MID),
    )(x, dhn, dx1, gain)


def _rope_tables():
    pos = jnp.arange(S, dtype=F32)
    inv_freq = ROPE_THETA ** (-jnp.arange(0, HD, 2, dtype=F32) / HD)
    ang = pos[:, None] * inv_freq[None, :]
    cos, sin = jnp.cos(ang), jnp.sin(ang)
    return jnp.concatenate([cos, cos], axis=1), jnp.concatenate([-sin, sin], axis=1)


def _local_step(x, p, tgt, w_in_t, w_o_rnn, w_o_att_t, w_out, w_pg, w_ple_t, conv_w, norm_mix, b_in, conv_b,
                w_rg_a, b_rg_a, w_rg_x, b_rg_x, lam, q_norm, k_norm, norm_ple, b_pg, start_reduce=None):
    if start_reduce is None:
        start_reduce = lambda arrs, tag: (jnp.zeros((8, LANES), F32), arrs)
    cos_t, sin_t = _rope_tables()
    wa_b = w_rg_a.astype(BF16)
    wx_b = w_rg_x.astype(BF16)

    hn = _rmsnorm_fwd(x, norm_mix)
    proj = _in_proj(hn, w_in_t, b_in)
    proj3 = proj.reshape(BL, S, NIN)
    h3 = _rnn_fwd(proj3, conv_w, conv_b, wa_b, b_rg_a, wx_b, b_rg_x, lam)
    att3, lse, wts = _attn_fwd(proj3, cos_t, sin_t, q_norm, k_norm)
    (dx1, merged, n1, dpre, dpe, dyr, dya, slab_a, slab_c, dh, datt, sbar, yrnn, yatt, loss8, dnp, dbpg) = _tail(
        x, proj, h3.reshape(T, DR), att3.reshape(T, ATT), p, tgt, w_o_rnn, w_o_att_t, w_out, w_pg, w_ple_t,
        norm_ple, b_pg)

    token, pending_out = start_reduce([
        _mm_tn(yrnn, dyr, 640, 512, "dw_o_rnn"),
        _mm_tn(dya, yatt, 512, 512, "dw_o_att_t"),
        _mm_tn(merged, dx1, 512, 512, "dw_out"),
        _mm_tn(n1, dpre, 512, 512, "dw_ple_gate"),
        _mm_tn(dpe, p, 512, 512, "dw_ple_t")], "out")

    slab_a3, dcw, dcb, dwa, dba, dwx, dbx, dlam = _rnn_bwd(
        proj3, h3, dh.reshape(BL, S, DR), slab_a.reshape(BL, S, A_W), conv_w, conv_b, wa_b, b_rg_a, wx_b, b_rg_x, lam,
        token)
    datt3 = datt.reshape(BL, S, ATT)
    sbar3 = sbar.reshape(BL, S, ATT)
    slabs = None
    dqn = []
    dkn = []
    for g in range(NG):
        dq, dk, dv, dqn_g, dkn_g = _attn_bwd_group(g, proj3, cos_t, sin_t, q_norm[g:g + 1], k_norm[g:g + 1],
                                                   lse, wts, datt3, sbar3, slabs)
        slabs = (dq, dk, dv)
        dqn.append(dqn_g)
        dkn.append(dkn_g)
    pieces = [slab_a3.reshape(T, A_W)] + [t.reshape(T, GW) for t in slabs] + [slab_c]
    dw_in_t, db_in = _dw_in(pieces, hn)
    token, pending_in = start_reduce([dw_in_t], "in")
    dhn = _dhn(pieces, w_in_t, token)
    grad_x, dnm = _input_norm_bwd(x, dhn, dx1, norm_mix)

    small = dict(w_rg_a=dwa, w_rg_x=dwx, norm_mix=dnm, b_in=db_in, conv_b=dcb, b_rg_a=dba, b_rg_x=dbx,
                 lru_lambda=dlam, q_norm=dqn, k_norm=dkn, norm_ple=dnp, b_ple_gate=dbpg, conv_w=dcw)
    return loss8[0, 0], grad_x, pending_out, pending_in, small


MESH = pl.DeviceIdType.MESH
HBM_SPEC = pl.BlockSpec(memory_space=pl.ANY)


def _my_pos():
    return lax.axis_index("x"), lax.axis_index("y"), lax.axis_index("c")


def _flip(pos, k):
    x, y, c = pos
    return (1 - x if k & 4 else x, 1 - y if k & 2 else y, 1 - c if k & 1 else c)


def _lin(pos):
    return 4 * pos[0] + 2 * pos[1] + pos[2]


def _chip(pos):
    return 2 * pos[0] + pos[1]


def _all_gather_two_level(shards, name):
    na = len(shards)

    def body(*refs):
        x_refs = refs[:na]
        out_refs = refs[na:2 * na]
        send_sems, recv_sems, local_sems = refs[2 * na:]
        me = _my_pos()
        sibling = _flip(me, 1)
        chips = [_flip(me, 4), _flip(me, 2), _flip(me, 6)]

        def copy(i, k, block, to, from_x=False):
            dst = out_refs[i].at[_lin(block)]
            return pltpu.make_async_remote_copy(
                src_ref=x_refs[i] if from_x else dst, dst_ref=dst,
                send_sem=send_sems.at[7 * i + k], recv_sem=recv_sems.at[7 * i + k], device_id=to, device_id_type=MESH)

        started = []
        for i in range(na):
            mine = pltpu.make_async_copy(x_refs[i], out_refs[i].at[_lin(me)], local_sems.at[i])
            mine.start()
            started.append(mine)
        sends = []
        for i in range(na):
            cps = [copy(i, 0, me, sibling, True)] + [copy(i, 1 + j, me, chip, True) for j, chip in enumerate(chips)]
            for cp in cps:
                cp.start()
            sends += cps
        for i in range(na):
            for j, chip in enumerate(chips):
                copy(i, 1 + j, chip, me).wait_recv()
                fwd = copy(i, 4 + j, chip, sibling)
                fwd.start()
                sends.append(fwd)
        for i in range(na):
            copy(i, 0, sibling, me).wait_recv()
            for j, chip in enumerate(chips):
                copy(i, 4 + j, _flip(chip, 1), me).wait_recv()
        for cp in sends:
            cp.wait_send()
        for mine in started:
            mine.wait()

    return pl.pallas_call(
        body, name=name,
        out_shape=[jax.ShapeDtypeStruct((NDEV,) + s.shape, s.dtype) for s in shards],
        in_specs=[HBM_SPEC] * na, out_specs=[HBM_SPEC] * na,
        scratch_shapes=[pltpu.SemaphoreType.DMA((7 * na,)), pltpu.SemaphoreType.DMA((7 * na,)),
                        pltpu.SemaphoreType.DMA((na,))],
    )(*shards)


def _all_gather_direct(shard, name):
    def body(x_ref, out_ref, send_sems, recv_sems, local_sem):
        me = _my_pos()
        mine = pltpu.make_async_copy(x_ref, out_ref.at[_lin(me)], local_sem)
        mine.start()
        sends = []
        for k in range(1, NDEV):
            cp = pltpu.make_async_remote_copy(
                src_ref=x_ref, dst_ref=out_ref.at[_lin(me)], send_sem=send_sems.at[k - 1],
                recv_sem=recv_sems.at[k - 1], device_id=_flip(me, k), device_id_type=MESH)
            cp.start()
            sends.append(cp)
        for k in range(1, NDEV):
            peer = _flip(me, k)
            pltpu.make_async_remote_copy(
                src_ref=x_ref, dst_ref=out_ref.at[_lin(peer)], send_sem=send_sems.at[k - 1],
                recv_sem=recv_sems.at[k - 1], device_id=peer, device_id_type=MESH).wait_recv()
        for cp in sends:
            cp.wait_send()
        mine.wait()

    return pl.pallas_call(
        body, name=name,
        out_shape=jax.ShapeDtypeStruct((NDEV,) + shard.shape, shard.dtype),
        in_specs=[HBM_SPEC], out_specs=HBM_SPEC,
        scratch_shapes=[pltpu.SemaphoreType.DMA((7,)), pltpu.SemaphoreType.DMA((7,)), pltpu.SemaphoreType.DMA],
    )(shard)


def _exchange_within_chip(parts, name):
    na = len(parts)

    def body(*refs):
        a_refs = refs[:na]
        recv_refs = refs[na:2 * na]
        send_sems, recv_sems = refs[2 * na:]
        me = _my_pos()
        c = me[2]
        sibling = _flip(me, 1)
        remote = []
        for i in range(na):
            for q in range(NCHIP):
                rc = pltpu.make_async_remote_copy(
                    src_ref=a_refs[i].at[q, 1 - c], dst_ref=recv_refs[i].at[q],
                    send_sem=send_sems.at[NCHIP * i + q], recv_sem=recv_sems.at[NCHIP * i + q],
                    device_id=sibling, device_id_type=MESH)
                rc.start()
                remote.append(rc)
        for rc in remote:
            rc.wait_recv()
        for rc in remote:
            rc.wait_send()

    return pl.pallas_call(
        body, name=name, out_shape=[jax.ShapeDtypeStruct((NCHIP,) + a.shape[2:], a.dtype) for a in parts],
        in_specs=[HBM_SPEC] * na, out_specs=[HBM_SPEC] * na,
        scratch_shapes=[pltpu.SemaphoreType.DMA((NCHIP * na,)), pltpu.SemaphoreType.DMA((NCHIP * na,))],
    )(*parts)


HBM_ONLY = pl.BlockSpec(memory_space=pltpu.HBM)
SEM_SPEC = pl.BlockSpec(memory_space=pltpu.SEMAPHORE)
SPLIT_COPY = pltpu.CompilerParams(has_side_effects=pltpu.SideEffectType.DATAFLOW_SIDE_EFFECTING)


def _chip_peers(me):
    return [_flip(me, 4), _flip(me, 2), _flip(me, 6)]


def _between_chips_start(parts, name):
    na = len(parts)

    def body(*refs):
        a_refs = refs[:na]
        land_refs = refs[na:2 * na]
        send_sems, recv_sems = refs[2 * na], refs[2 * na + 1]
        token = refs[-1]
        me = _my_pos()
        myq = _chip(me)
        for i in range(na):
            for j, peer in enumerate(_chip_peers(me)):
                pltpu.make_async_remote_copy(
                    src_ref=a_refs[i].at[_chip(peer)], dst_ref=land_refs[i].at[myq],
                    send_sem=send_sems.at[3 * i + j], recv_sem=recv_sems.at[3 * i + j],
                    device_id=peer, device_id_type=MESH).start()
        token[...] = jnp.zeros_like(token)

    hbm = [pltpu.HBM(a.shape, a.dtype) for a in parts]
    srcs = [pltpu.with_memory_space_constraint(a, pltpu.HBM) for a in parts]
    lands = [pltpu.with_memory_space_constraint(lax.empty(a.shape, a.dtype), pltpu.HBM) for a in parts]
    res = pl.pallas_call(
        body, name=name,
        out_shape=(pltpu.SemaphoreType.DMA((3 * na,)), pltpu.SemaphoreType.DMA((3 * na,)), *hbm, *hbm,
                   jax.ShapeDtypeStruct((8, LANES), F32)),
        in_specs=[HBM_ONLY] * (2 * na),
        out_specs=(SEM_SPEC, SEM_SPEC, *([HBM_ONLY] * (2 * na)), pl.BlockSpec(memory_space=pltpu.VMEM)),
        input_output_aliases={i: 2 + i for i in range(2 * na)},
        compiler_params=SPLIT_COPY,
    )(*srcs, *lands)
    return res[-1], (res[0], res[1], list(res[2:2 + na]), list(res[2 + na:2 + 2 * na]))


def _between_chips_wait(pending, after, name):
    send_sems, recv_sems, parts, lands = pending
    na = len(parts)

    def body(*refs):
        a_refs = refs[:na]
        land_refs = refs[na:2 * na]
        send_sems, recv_sems = refs[2 * na], refs[2 * na + 1]
        me = _my_pos()
        for i in range(na):
            for j, peer in enumerate(_chip_peers(me)):
                cp = pltpu.make_async_remote_copy(
                    src_ref=a_refs[i].at[_chip(peer)], dst_ref=land_refs[i].at[_chip(peer)],
                    send_sem=send_sems.at[3 * i + j], recv_sem=recv_sems.at[3 * i + j],
                    device_id=peer, device_id_type=MESH)
                cp.wait_send()
                cp.wait_recv()

    hbm = [pltpu.HBM(a.shape, a.dtype) for a in parts]
    res = pl.pallas_call(
        body, name=name, out_shape=(*hbm, *hbm),
        in_specs=[HBM_ONLY] * (2 * na) + [SEM_SPEC, SEM_SPEC, pl.BlockSpec(memory_space=pl.ANY)],
        out_specs=[HBM_ONLY] * (2 * na),
        input_output_aliases={i: i for i in range(2 * na)},
        compiler_params=SPLIT_COPY,
    )(*parts, *lands, send_sems, recv_sems, after)
    return list(res[:na]), list(res[na:])


def _exchange_all(parts, name):
    na = len(parts)

    def body(*refs):
        a_refs = refs[:na]
        out_refs = refs[na:2 * na]
        send_sems, recv_sems = refs[2 * na:]
        me = _my_pos()
        sends = []
        for i in range(na):
            for k in range(1, NDEV):
                peer = _flip(me, k)
                cp = pltpu.make_async_remote_copy(
                    src_ref=a_refs[i].at[_lin(peer)], dst_ref=out_refs[i].at[_lin(me)],
                    send_sem=send_sems.at[7 * i + k - 1], recv_sem=recv_sems.at[7 * i + k - 1],
                    device_id=peer, device_id_type=MESH)
                cp.start()
                sends.append(cp)
        for i in range(na):
            for k in range(1, NDEV):
                peer = _flip(me, k)
                pltpu.make_async_remote_copy(
                    src_ref=a_refs[i].at[_lin(peer)], dst_ref=out_refs[i].at[_lin(peer)],
                    send_sem=send_sems.at[7 * i + k - 1], recv_sem=recv_sems.at[7 * i + k - 1],
                    device_id=peer, device_id_type=MESH).wait_recv()
        for cp in sends:
            cp.wait_send()

    return pl.pallas_call(
        body, name=name, out_shape=[jax.ShapeDtypeStruct(a.shape, a.dtype) for a in parts],
        in_specs=[HBM_SPEC] * na, out_specs=[HBM_SPEC] * na,
        scratch_shapes=[pltpu.SemaphoreType.DMA((7 * na,)), pltpu.SemaphoreType.DMA((7 * na,))],
    )(*parts)


def _scalar(v):
    return jnp.asarray(v, jnp.int32).reshape(1)


def _sum_pairs(parts, theirs, name):
    na = len(parts)

    def body(c_ref, *refs):
        for i in range(na):
            o_ref = refs[2 * na + i]
            o_ref[0] = (refs[i][0, 0].astype(F32) + refs[na + i][0].astype(F32)).astype(o_ref.dtype)

    def mine_spec(a):
        return pl.BlockSpec((1, 1) + a.shape[2:], lambda q, c_ref: (q, c_ref[0], 0, 0))

    def spec(a):
        return pl.BlockSpec((1,) + a.shape[1:], lambda q, c_ref: (q, 0, 0))

    return pl.pallas_call(
        body, name=name,
        grid_spec=pltpu.PrefetchScalarGridSpec(
            num_scalar_prefetch=1, grid=(NCHIP,),
            in_specs=[mine_spec(a) for a in parts] + [spec(a) for a in theirs],
            out_specs=[spec(a) for a in theirs]),
        out_shape=[jax.ShapeDtypeStruct(a.shape, a.dtype) for a in theirs],
        compiler_params=_cp(("arbitrary",), VMEM_BIG),
    )(_scalar(lax.axis_index("c")), *parts, *theirs)


def _others(q, mine, nblk=NCHIP):
    return jnp.where(q == mine, (q + 1) % nblk, q)


def _sum_chips_rows(own, recv, tr, name):
    _, r, w = recv.shape

    def body(q_ref, own_ref, r0, r1, r2, r3, o_ref):
        myq = q_ref[0]
        acc = None
        for q, r_ref in enumerate((r0, r1, r2, r3)):
            term = jnp.where(myq == q, own_ref[0], r_ref[0]).astype(F32)
            acc = term if acc is None else acc + term
        o_ref[...] = acc

    def recv_spec(q):
        return pl.BlockSpec((1, tr, w), lambda i, q_ref: (_others(q, q_ref[0]), i, 0))

    return pl.pallas_call(
        body, name=name,
        grid_spec=pltpu.PrefetchScalarGridSpec(
            num_scalar_prefetch=1, grid=(r // tr,),
            in_specs=[pl.BlockSpec((1, tr, w), lambda i, q_ref: (q_ref[0], i, 0))] + [recv_spec(q) for q in range(NCHIP)],
            out_specs=pl.BlockSpec((tr, w), lambda i, q_ref: (i, 0))),
        out_shape=jax.ShapeDtypeStruct((r, w), F32),
        compiler_params=_cp(("arbitrary",), VMEM_MID),
    )(_scalar(_chip(_my_pos())), own, recv, recv, recv, recv)


def _sum_blocks_small(own, recv, mine, transpose, name):
    na = len(recv)
    nblk = recv[0].shape[0]

    def body(q_ref, *refs):
        me = q_ref[0]
        for i in range(na):
            acc = None
            for q in range(nblk):
                term = jnp.where(me == q, refs[i][0], refs[na * (1 + q) + i][0]).astype(F32)
                acc = term if acc is None else acc + term
            refs[na * (1 + nblk) + i][...] = acc.T if transpose[i] else acc

    def oshape(a, tr):
        r, w = a.shape[1:]
        return (w, r) if tr else (r, w)

    own_spec = lambda a: pl.BlockSpec((1,) + a.shape[1:], lambda s, q_ref: (q_ref[0], 0, 0))
    recv_spec = lambda a, q: pl.BlockSpec((1,) + a.shape[1:], lambda s, q_ref: (_others(q, q_ref[0], nblk), 0, 0))
    out_spec = lambda shp: pl.BlockSpec(shp, lambda s, q_ref: (0, 0))
    in_specs = [own_spec(a) for a in own]
    for q in range(nblk):
        in_specs += [recv_spec(a, q) for a in recv]
    return pl.pallas_call(
        body, name=name,
        grid_spec=pltpu.PrefetchScalarGridSpec(
            num_scalar_prefetch=1, grid=(1,), in_specs=in_specs,
            out_specs=[out_spec(oshape(a, tr)) for a, tr in zip(recv, transpose)]),
        out_shape=[jax.ShapeDtypeStruct(oshape(a, tr), F32) for a, tr in zip(recv, transpose)],
        compiler_params=_cp(("arbitrary",), VMEM_MID),
    )(_scalar(mine), *own, *(list(recv) * nblk))


def _rep_offsets():
    offs = []
    o = 0
    for r in REP_ROWS:
        offs.append(o)
        o += r
    return offs


def _pack_small_grads(g):
    offs = _rep_offsets()

    def body(dwa, dwx, dnm, dbin, dcb, dba, dbx, dlam, dq0, dq1, dq2, dk0, dk1, dk2, dnp, dbpg, o_ref):
        o_ref[pl.ds(REP_TOTAL_ROWS - 2, NDEV * REP_ROWS_DEV - REP_TOTAL_ROWS + 2), :] = jnp.zeros(
            (NDEV * REP_ROWS_DEV - REP_TOTAL_ROWS + 2, LANES), F32)
        for n in range(NRB):
            o_ref[pl.ds(offs[0] + n * RBW, RBW), :] = dwa[n]
            o_ref[pl.ds(offs[1] + n * RBW, RBW), :] = dwx[n]

        def put_vec(off, ref, rows):
            for k in range(rows):
                o_ref[pl.ds(off + k, 1), :] = ref[:, k * LANES:(k + 1) * LANES]

        put_vec(offs[2], dnm, REP_ROWS[2])
        put_vec(offs[3], dbin, REP_ROWS[3])
        put_vec(offs[4], dcb, REP_ROWS[4])
        put_vec(offs[5], dba, REP_ROWS[5])
        put_vec(offs[6], dbx, REP_ROWS[6])
        put_vec(offs[7], dlam, REP_ROWS[7])
        for k, ref in enumerate((dq0, dq1, dq2)):
            o_ref[pl.ds(offs[8] + k, 1), :] = ref[...]
        for k, ref in enumerate((dk0, dk1, dk2)):
            o_ref[pl.ds(offs[9] + k, 1), :] = ref[...]
        put_vec(offs[10], dnp, REP_ROWS[10])
        put_vec(offs[11], dbpg, REP_ROWS[11])

    args = [g["w_rg_a"], g["w_rg_x"], g["norm_mix"], g["b_in"], g["conv_b"], g["b_rg_a"], g["b_rg_x"],
            g["lru_lambda"], *g["q_norm"], *g["k_norm"], g["norm_ple"], g["b_ple_gate"]]
    full = lambda shp: pl.BlockSpec(shp, lambda: (0,) * len(shp))
    return pl.pallas_call(
        body, name="pack_small_grads",
        in_specs=[full(a.shape) for a in args],
        out_specs=full((NDEV * REP_ROWS_DEV, LANES)),
        out_shape=jax.ShapeDtypeStruct((NDEV * REP_ROWS_DEV, LANES), F32),
    )(*args)


def _adam_math(wv, gv, mv, vv):
    c1 = 1.0 - B1 ** STEP
    c2 = 1.0 - B2 ** STEP
    m2 = B1 * mv + (1.0 - B1) * gv
    v2 = B2 * vv + (1.0 - B2) * (gv * gv)
    delta = (-LR) * ((m2 / c1) / (jnp.sqrt(v2 / c2) + AEPS) + WD * wv)
    return delta, m2, v2


def _adamw_small(rep_flat, w, m, v):
    offs = _rep_offsets()
    n = len(REP_NAMES)

    def body(*refs):
        g_ref = refs[0]
        w_refs = refs[1:1 + n]
        m_refs = refs[1 + n:1 + 2 * n]
        v_refs = refs[1 + 2 * n:1 + 3 * n]
        outs = refs[1 + 3 * n:]
        go, do, mo, vo = outs[:n], outs[n:2 * n], outs[2 * n:3 * n], outs[3 * n:]

        def emit(i, idx, gv):
            go[i][idx] = gv
            delta, m2, v2 = _adam_math(w_refs[i][idx], gv, m_refs[i][idx], v_refs[i][idx])
            do[i][idx] = delta
            mo[i][idx] = m2
            vo[i][idx] = v2

        for i in range(n):
            if i < 2:
                for b in range(NRB):
                    emit(i, b, g_ref[pl.ds(offs[i] + b * RBW, RBW), :])
            elif REP_NAMES[i] in ("q_norm", "k_norm"):
                emit(i, slice(None), g_ref[pl.ds(offs[i], NG), :])
            else:
                gv = jnp.concatenate([g_ref[pl.ds(offs[i] + k, 1), :] for k in range(REP_ROWS[i])], axis=1)
                emit(i, slice(None), gv)

    full = lambda shp: pl.BlockSpec(shp, lambda: (0,) * len(shp))
    pspecs = [full(a.shape) for a in w]
    pshapes = [jax.ShapeDtypeStruct(a.shape, F32) for a in w]
    res = pl.pallas_call(
        body, name="adamw_small",
        in_specs=[full(rep_flat.shape)] + pspecs * 3,
        out_specs=pspecs * 4, out_shape=pshapes * 4,
        compiler_params=_cp(None, VMEM_MID),
    )(rep_flat, *w, *m, *v)
    return res[:n], res[n:2 * n], res[2 * n:3 * n], res[3 * n:]


def _adamw_rows(w, g, m, v, tr, name):
    r, c_ = w.shape

    def body(w_ref, g_ref, m_ref, v_ref, d_ref, m2_ref, v2_ref):
        delta, m2, v2 = _adam_math(w_ref[...], g_ref[...], m_ref[...], v_ref[...])
        d_ref[...] = delta
        m2_ref[...] = m2
        v2_ref[...] = v2

    spec = lambda: pl.BlockSpec((tr, c_), lambda i: (i, 0))
    shp = jax.ShapeDtypeStruct((r, c_), F32)
    return pl.pallas_call(
        body, grid=(r // tr,), name=name,
        in_specs=[spec(), spec(), spec(), spec()], out_specs=[spec(), spec(), spec()],
        out_shape=[shp, shp, shp], compiler_params=_cp(("parallel",)),
    )(w, g, m, v)


def _adamw_many(w, g, m, v):
    n = len(w)

    def body(*refs):
        for i in range(n):
            delta, m2, v2 = _adam_math(refs[i][...], refs[n + i][...], refs[2 * n + i][...], refs[3 * n + i][...])
            refs[4 * n + i][...] = delta
            refs[5 * n + i][...] = m2
            refs[6 * n + i][...] = v2

    full = lambda shp: pl.BlockSpec(shp, lambda: (0,) * len(shp))
    specs = [full(a.shape) for a in w]
    shapes = [jax.ShapeDtypeStruct(a.shape, F32) for a in w]
    res = pl.pallas_call(
        body, name="adamw_shards",
        in_specs=specs * 4, out_specs=specs * 3, out_shape=shapes * 3,
        compiler_params=_cp(None, VMEM_MID),
    )(*w, *g, *m, *v)
    return res[:n], res[n:2 * n], res[2 * n:]


def kernel(x, p, norm_mix, w_in, b_in, conv_w, conv_b, w_rg_a, b_rg_a, w_rg_x, b_rg_x, lru_lambda, q_norm, k_norm, w_o_rnn, w_o_att, w_out, norm_ple, w_ple_gate, b_ple_gate, w_ple, loss_target, m_norm_mix, m_w_in, m_b_in, m_conv_w, m_conv_b, m_w_rg_a, m_b_rg_a, m_w_rg_x, m_b_rg_x, m_lru_lambda, m_q_norm, m_k_norm, m_w_o_rnn, m_w_o_att, m_w_out, m_norm_ple, m_w_ple_gate, m_b_ple_gate, m_w_ple, v_norm_mix, v_w_in, v_b_in, v_conv_w, v_conv_b, v_w_rg_a, v_b_rg_a, v_w_rg_x, v_b_rg_x, v_lru_lambda, v_q_norm, v_k_norm, v_w_o_rnn, v_w_o_att, v_w_out, v_norm_ple, v_w_ple_gate, v_b_ple_gate, v_w_ple):
    w = dict(norm_mix=norm_mix, w_in=w_in, b_in=b_in, conv_w=conv_w, conv_b=conv_b, w_rg_a=w_rg_a, b_rg_a=b_rg_a,
             w_rg_x=w_rg_x, b_rg_x=b_rg_x, lru_lambda=lru_lambda, q_norm=q_norm, k_norm=k_norm, w_o_rnn=w_o_rnn,
             w_o_att=w_o_att, w_out=w_out, norm_ple=norm_ple, w_ple_gate=w_ple_gate, b_ple_gate=b_ple_gate,
             w_ple=w_ple)
    m = dict(norm_mix=m_norm_mix, w_in=m_w_in, b_in=m_b_in, conv_w=m_conv_w, conv_b=m_conv_b, w_rg_a=m_w_rg_a,
             b_rg_a=m_b_rg_a, w_rg_x=m_w_rg_x, b_rg_x=m_b_rg_x, lru_lambda=m_lru_lambda, q_norm=m_q_norm,
             k_norm=m_k_norm, w_o_rnn=m_w_o_rnn, w_o_att=m_w_o_att, w_out=m_w_out, norm_ple=m_norm_ple,
             w_ple_gate=m_w_ple_gate, b_ple_gate=m_b_ple_gate, w_ple=m_w_ple)
    v = dict(norm_mix=v_norm_mix, w_in=v_w_in, b_in=v_b_in, conv_w=v_conv_w, conv_b=v_conv_b, w_rg_a=v_w_rg_a,
             b_rg_a=v_b_rg_a, w_rg_x=v_w_rg_x, b_rg_x=v_b_rg_x, lru_lambda=v_lru_lambda, q_norm=v_q_norm,
             k_norm=v_k_norm, w_o_rnn=v_w_o_rnn, w_o_att=v_w_o_att, w_out=v_w_out, norm_ple=v_norm_ple,
             w_ple_gate=v_w_ple_gate, b_ple_gate=v_b_ple_gate, w_ple=v_w_ple)
    names = list(w.keys())

    shards = [w_in[0].T.astype(BF16), w_o_rnn[0].astype(BF16), w_o_att[0].T.astype(BF16), w_out[0].astype(BF16),
              w_ple_gate[0].astype(BF16), w_ple[0].T.astype(BF16), conv_w[0]]
    gat = _all_gather_two_level(shards, "gather_weights")
    w_in_t, w_o_rnn_f, w_o_att_t, w_out_f, w_pg_f, w_ple_t = [
        a.reshape((NDEV * a.shape[1], a.shape[2])) for a in gat[:6]]
    conv_f = gat[6].transpose(1, 0, 2).reshape(CONVW, DR)

    def start_reduce(arrs, tag):
        parts = [a.reshape((NCHIP, 2, a.shape[0] // NDEV, a.shape[1])) for a in arrs]
        theirs = _exchange_within_chip(parts, "reduce_within_chip_" + tag)
        return _between_chips_start(_sum_pairs(parts, theirs, "sum_pairs_" + tag), "reduce_between_chips_start_" + tag)

    loss_part, grad_x, pending_out, pending_in, small = _local_step(
        x.reshape(T, D), p.reshape(T, PLE), loss_target.reshape(T, D),
        w_in_t, w_o_rnn_f, w_o_att_t, w_out_f, w_pg_f, w_ple_t, conv_f,
        norm_mix, b_in, conv_b, w_rg_a[0], b_rg_a, w_rg_x[0], b_rg_x, lru_lambda, q_norm[0], k_norm[0],
        norm_ple, b_ple_gate, start_reduce)
    loss = lax.psum(loss_part, ("x", "y", "c"))

    me = _lin(_my_pos())
    rep_parts = _pack_small_grads(small).reshape(NDEV, REP_ROWS_DEV, LANES)
    conv_parts = small["conv_w"].reshape(CONVW, NDEV, DR // NDEV).transpose(1, 0, 2)
    small_parts = [rep_parts, conv_parts]
    g_rep, g_conv = _sum_blocks_small(small_parts, _exchange_all(small_parts, "reduce_small"), me, (False, False),
                                      "sum_small")
    rep_all = _all_gather_direct(g_rep, "gather_small").reshape(NDEV * REP_ROWS_DEV, LANES)

    myq = _chip(_my_pos())
    own_out, recv_out = _between_chips_wait(pending_out, rep_all, "reduce_between_chips_wait_out")
    own_in, recv_in = _between_chips_wait(pending_in, rep_all, "reduce_between_chips_wait_in")
    g_in_t = _sum_chips_rows(own_in[0], recv_in[0], 304, "sum_chips_w_in")
    g_o_rnn, g_o_att, g_out, g_pg, g_ple = _sum_blocks_small(
        own_out, recv_out, myq, (False, True, False, False, True), "sum_chips_out")

    grad, delta, new_m, new_v = {}, {}, {}, {}
    rep_shape = lambda a: a if a.ndim == 2 else a.reshape(a.shape[1:])
    res = _adamw_small(rep_all, [rep_shape(w[n]) for n in REP_NAMES], [rep_shape(m[n]) for n in REP_NAMES],
                       [rep_shape(v[n]) for n in REP_NAMES])
    for dst, vals in zip((grad, delta, new_m, new_v), res):
        for n, a in zip(REP_NAMES, vals):
            dst[n] = a.reshape(w[n].shape)
    g_in = g_in_t.T
    d_, m_, v_ = _adamw_rows(w_in[0], g_in, m_w_in[0], v_w_in[0], 128, "adamw_w_in")
    grad["w_in"], delta["w_in"], new_m["w_in"], new_v["w_in"] = g_in[None], d_[None], m_[None], v_[None]
    rest = ("w_o_rnn", "w_o_att", "w_out", "w_ple_gate", "w_ple", "conv_w")
    g_rest = [g_o_rnn, g_o_att, g_out, g_pg, g_ple, g_conv]
    res = _adamw_many([w[n][0] for n in rest], g_rest, [m[n][0] for n in rest], [v[n][0] for n in rest])
    for n, a in zip(rest, g_rest):
        grad[n] = a[None]
    for dst, vals in zip((delta, new_m, new_v), res):
        for n, a in zip(rest, vals):
            dst[n] = a[None]

    return (loss, grad_x.reshape(BL, S, D), *[grad[n] for n in names], *[delta[n] for n in names],
            *[new_m[n] for n in names], *[new_v[n] for n in names])
```

```python
import jax
import jax.numpy as jnp
from jax import lax
from jax.experimental import pallas as pl
from jax.experimental.pallas import tpu as pltpu

F32 = jnp.float32
BF16 = jnp.bfloat16

D = 1024
S = 2048
BL = 2
T = BL * S
NDEV = 8
NCHIP = 4
PLE = 256
DR = 1280
NRB = 10
RBW = 128
CONVW = 4
LRU_C = 8.0
HD = 128
NH = 4
PATTERNS = ((128, 1), (512, 4), (2048, 16))
NG = 3
ATT = NH * HD
GW = NG * ATT
NIN = 2 * DR + 3 * GW + ATT + 2 * D
OFF_ZR = DR
OFF_Q = 2 * DR
OFF_ZA = OFF_Q + 3 * GW
OFF_G = OFF_ZA + ATT
ROPE_THETA = 10000.0
EPS = 1e-6
SCALE = HD ** -0.5
NEG = -1e30
QB = 128
LANES = 128
CT = 512
NCT = NIN // CT
A_W = 2 * DR
C_W = ATT + 2 * D

LR, B1, B2, AEPS, WD, STEP = 0.001, 0.9, 0.999, 1e-08, 0.01, 10

NSHARD_IN = NIN // NDEV
REP_NAMES = ("w_rg_a", "w_rg_x", "norm_mix", "b_in", "conv_b", "b_rg_a", "b_rg_x", "lru_lambda", "q_norm",
             "k_norm", "norm_ple", "b_ple_gate")
REP_ROWS = (NRB * RBW, NRB * RBW, D // LANES, NIN // LANES, DR // LANES, DR // LANES, DR // LANES, DR // LANES,
            NG, NG, D // LANES, D // LANES)
REP_TOTAL_ROWS = sum(REP_ROWS)
REP_ROWS_DEV = 344
BIG_NAMES = ("w_in", "w_o_rnn", "w_o_att", "w_out", "w_ple_gate", "w_ple")

VMEM_BIG = 56 * 1024 * 1024
VMEM_MID = 40 * 1024 * 1024


def _cp(sem=None, vmem=None):
    return pltpu.CompilerParams(dimension_semantics=sem, vmem_limit_bytes=vmem)


def _dot(a, b):
    return jnp.dot(a, b, preferred_element_type=F32)


def _dot_nt(a, b):
    return lax.dot_general(a, b, (((1,), (1,)), ((), ())), preferred_element_type=F32)


def _dot_tn(a, b):
    return lax.dot_general(a, b, (((0,), (0,)), ((), ())), preferred_element_type=F32)


def _sigmoid(x):
    return jax.nn.sigmoid(x)


def _perm(j):
    jq = j - OFF_Q // CT
    inside = (j >= OFF_Q // CT) & (j < OFF_ZA // CT)
    return jnp.where(inside, OFF_Q // CT + (jq % 3) * 3 + jq // 3, j)


PIECES = ((0, A_W // CT), (OFF_Q // CT, GW // CT), (OFF_Q // CT + 3, GW // CT), (OFF_Q // CT + 6, GW // CT),
          (OFF_ZA // CT, C_W // CT))


def _rmsnorm_fwd(x, gain, tm=512):
    def body(x_ref, g_ref, o_ref):
        xv = x_ref[...]
        var = jnp.mean(xv * xv, axis=-1, keepdims=True)
        o_ref[...] = (xv * lax.rsqrt(var + EPS) * g_ref[...]).astype(BF16)

    return pl.pallas_call(
        body, grid=(T // tm,), name="rmsnorm_fwd",
        in_specs=[pl.BlockSpec((tm, D), lambda i: (i, 0)), pl.BlockSpec((1, D), lambda i: (0, 0))],
        out_specs=pl.BlockSpec((tm, D), lambda i: (i, 0)),
        out_shape=jax.ShapeDtypeStruct((T, D), BF16),
        compiler_params=_cp(("parallel",)),
    )(x, gain)


def _in_proj(hn, w_t, bias, tm=1024):
    def body(a_ref, w_ref, b_ref, o_ref):
        o_ref[...] = _dot_nt(a_ref[...], w_ref[...]) + b_ref[...]

    return pl.pallas_call(
        body, grid=(T // tm, NCT), name="in_proj",
        in_specs=[pl.BlockSpec((tm, D), lambda i, j: (i, 0)),
                  pl.BlockSpec((CT, D), lambda i, j: (_perm(j), 0)),
                  pl.BlockSpec((1, CT), lambda i, j: (0, _perm(j)))],
        out_specs=pl.BlockSpec((tm, CT), lambda i, j: (i, j)),
        out_shape=jax.ShapeDtypeStruct((T, NIN), F32),
        compiler_params=_cp(("parallel", "parallel"), VMEM_MID),
    )(hn, w_t, bias)


def _dhn(pieces, w_t, token, tm=512):
    def body(a_ref, q_ref, k_ref, v_ref, c_ref, w_hbm, _token, o_ref, w):
        @pl.when(pl.program_id(0) == 0)
        def _():
            pltpu.sync_copy(w_hbm, w)

        acc = _dot(a_ref[...], w[pl.ds(0, A_W), :])
        for kind, x_ref in enumerate((q_ref, k_ref, v_ref)):
            for g in range(NG):
                row = OFF_Q + (3 * g + kind) * CT
                acc = acc + _dot(x_ref[:, g * CT:(g + 1) * CT], w[pl.ds(row, CT), :])
        o_ref[...] = acc + _dot(c_ref[...], w[pl.ds(OFF_ZA, C_W), :])

    tok = lambda wd: pl.BlockSpec((tm, wd), lambda i: (i, 0))
    return pl.pallas_call(
        body, grid=(T // tm,), name="dhn",
        in_specs=[tok(A_W), tok(GW), tok(GW), tok(GW), tok(C_W), pl.BlockSpec(memory_space=pl.ANY),
                  pl.BlockSpec((8, LANES), lambda i: (0, 0))],
        out_specs=tok(D),
        out_shape=jax.ShapeDtypeStruct((T, D), F32),
        scratch_shapes=[pltpu.VMEM((NIN, D), BF16)],
        compiler_params=_cp(("arbitrary",), VMEM_BIG),
    )(*pieces, w_t, token)


def _dw_in(pieces, hn, tt=2048):
    nt = T // tt

    def body(a_ref, q_ref, k_ref, v_ref, c_ref, h_ref, o_ref, s_ref, acc, cs):
        j = pl.program_id(0)
        t = pl.program_id(1)

        def step(x_ref):
            xv = x_ref[...]
            p = _dot_tn(xv, h_ref[...])
            c = jnp.sum(xv.astype(F32), axis=0, keepdims=True)

            @pl.when(t == 0)
            def _():
                acc[...] = p
                cs[...] = c

            @pl.when(t > 0)
            def _():
                acc[...] += p
                cs[...] += c

        for x_ref, (lo, n) in zip((a_ref, q_ref, k_ref, v_ref, c_ref), PIECES):
            pl.when((j >= lo) & (j < lo + n))(lambda x_ref=x_ref: step(x_ref))

        @pl.when(t == nt - 1)
        def _():
            o_ref[...] = acc[...].astype(BF16)
            s_ref[...] = cs[...]

    def piece_spec(lo, n):
        def imap(j, t):
            used = (j >= lo) & (j < lo + n)
            return (jnp.where(used, t, 0), jnp.clip(j - lo, 0, n - 1))
        return pl.BlockSpec((tt, CT), imap)

    return pl.pallas_call(
        body, grid=(NCT, nt), name="dw_in",
        in_specs=[piece_spec(lo, n) for lo, n in PIECES] + [pl.BlockSpec((tt, D), lambda j, t: (t, 0))],
        out_specs=[pl.BlockSpec((CT, D), lambda j, t: (_perm(j), 0)), pl.BlockSpec((1, CT), lambda j, t: (0, _perm(j)))],
        out_shape=[jax.ShapeDtypeStruct((NIN, D), BF16), jax.ShapeDtypeStruct((1, NIN), F32)],
        scratch_shapes=[pltpu.VMEM((CT, D), F32), pltpu.VMEM((1, CT), F32)],
        compiler_params=_cp(("parallel", "arbitrary"), VMEM_MID),
    )(*pieces, hn)


def _mm_tn(a, b, ta, tt, name):
    m = a.shape[1]
    n = b.shape[1]
    nt = T // tt

    def body(a_ref, b_ref, o_ref, acc):
        t = pl.program_id(1)
        p = _dot_tn(a_ref[...].astype(BF16), b_ref[...].astype(BF16))

        @pl.when(t == 0)
        def _():
            acc[...] = p

        @pl.when(t > 0)
        def _():
            acc[...] += p

        @pl.when(t == nt - 1)
        def _():
            o_ref[...] = acc[...].astype(BF16)

    return pl.pallas_call(
        body, grid=(m // ta, nt), name=name,
        in_specs=[pl.BlockSpec((tt, ta), lambda j, t: (t, j)), pl.BlockSpec((tt, n), lambda j, t: (t, 0))],
        out_specs=pl.BlockSpec((ta, n), lambda j, t: (j, 0)),
        out_shape=jax.ShapeDtypeStruct((m, n), BF16),
        scratch_shapes=[pltpu.VMEM((ta, n), F32)],
        compiler_params=_cp(("parallel", "arbitrary"), VMEM_MID),
    )(a, b)


def _row_iota():
    return lax.broadcasted_iota(jnp.int32, (S, RBW), 0)


def _shift_down(v, d, row, fill):
    return jnp.where(row >= d, pltpu.roll(v, d, 0), fill)


def _shift_up(v, d, row, fill):
    return jnp.where(row < S - d, pltpu.roll(v, S - d, 0), fill)


SUBLANES = 8


def _scan_down(a, u, row):
    d = 1
    while d < S:
        last = 2 * d >= S
        if d < SUBLANES:
            u = a * _shift_down(u, d, row, 0.0) + u
            if not last:
                a = a * _shift_down(a, d, row, 1.0)
        else:
            u = jnp.concatenate([u[:d], a[d:] * u[:S - d] + u[d:]], axis=0)
            if not last:
                a = jnp.concatenate([a[:d], a[d:] * a[:S - d]], axis=0)
        d *= 2
    return u


def _scan_up(b, g, row):
    d = 1
    while d < S:
        last = 2 * d >= S
        if d < SUBLANES:
            g = g + b * _shift_up(g, d, row, 0.0)
            if not last:
                b = b * _shift_up(b, d, row, 0.0)
        else:
            g = jnp.concatenate([g[:S - d] + b[:S - d] * g[d:], g[S - d:]], axis=0)
            if not last:
                b = jnp.concatenate([b[:S - d] * b[d:], b[S - d:]], axis=0)
        d *= 2
    return g


def _neg_expm1(x):
    series = -x * (1.0 + x * (0.5 + x * (1.0 / 6.0 + x * (1.0 / 24.0))))
    return jnp.where(x > -0.03, series, 1.0 - jnp.exp(x))


def _softplus(x):
    return jnp.maximum(x, 0.0) + jnp.log1p(jnp.exp(-jnp.abs(x)))


def _rnn_gates(x, cw, cb, wa, ba, wx, bx, lam, row):
    xc = cb + cw[3:4, :] * x
    for j in (1, 2, 3):
        xc = xc + cw[3 - j:4 - j, :] * _shift_down(x, j, row, 0.0)
    xcb = xc.astype(BF16)
    r = _sigmoid(_dot(xcb, wa) + ba)
    i = _sigmoid(_dot(xcb, wx) + bx)
    sp = _softplus(-lam)
    log_a = (-LRU_C) * r * sp
    a = jnp.exp(log_a)
    mult = jnp.where(row == 0, 1.0, jnp.sqrt(_neg_expm1(2.0 * log_a)))
    return xc, xcb, r, i, sp, a, mult


def _rnn_fwd(proj3, conv_w, conv_b, wa, ba, wx, bx, lam):
    def body(x_ref, cw_ref, cb_ref, wa_ref, ba_ref, wx_ref, bx_ref, lam_ref, h_ref):
        row = _row_iota()
        x = x_ref[0]
        xc, _, _, i, _, a, mult = _rnn_gates(x, cw_ref[...], cb_ref[...], wa_ref[0], ba_ref[...],
                                             wx_ref[0], bx_ref[...], lam_ref[...], row)
        h_ref[0] = _scan_down(a, mult * (i * xc), row)

    vec = lambda: pl.BlockSpec((1, RBW), lambda b, n: (0, n))
    mat = lambda: pl.BlockSpec((1, RBW, RBW), lambda b, n: (n, 0, 0))
    return pl.pallas_call(
        body, grid=(BL, NRB), name="rnn_fwd",
        in_specs=[pl.BlockSpec((1, S, RBW), lambda b, n: (b, 0, n)),
                  pl.BlockSpec((CONVW, RBW), lambda b, n: (0, n)),
                  vec(), mat(), vec(), mat(), vec(), vec()],
        out_specs=pl.BlockSpec((1, S, RBW), lambda b, n: (b, 0, n)),
        out_shape=jax.ShapeDtypeStruct((BL, S, DR), F32),
        compiler_params=_cp(("parallel", "parallel"), VMEM_MID),
    )(proj3, conv_w, conv_b, wa, ba, wx, bx, lam)


def _rnn_bwd(proj3, h3, dh3, slab_a3, conv_w, conv_b, wa, ba, wx, bx, lam, token):
    def body(x_ref, h_ref, dh_ref, cw_ref, cb_ref, wa_ref, ba_ref, wx_ref, bx_ref, lam_ref, _alias, _token,
             dx_ref, dcw_ref, dcb_ref, dwa_ref, dba_ref, dwx_ref, dbx_ref, dlam_ref):
        row = _row_iota()
        x = x_ref[0]
        cw = cw_ref[...]
        wa_v = wa_ref[0]
        wx_v = wx_ref[0]
        lam_v = lam_ref[...]
        xc, xcb, r, i, sp, a, mult = _rnn_gates(x, cw, cb_ref[...], wa_v, ba_ref[...], wx_v, bx_ref[...], lam_v, row)
        h = h_ref[0]
        g = _scan_up(_shift_up(a, 1, row, 0.0), dh_ref[0], row)
        da = g * _shift_down(h, 1, row, 0.0)
        dmult = jnp.where(row == 0, 0.0, g * (i * xc))
        gm = g * mult
        di = gm * xc
        dxc = gm * i
        dlog_a = da * a - dmult * (a * a) / mult
        dr = dlog_a * ((-LRU_C) * sp)
        dsp = jnp.sum(dlog_a * ((-LRU_C) * r), axis=0, keepdims=True)
        dlam = dsp * (-_sigmoid(-lam_v))
        dpa = dr * r * (1.0 - r)
        dpx = di * i * (1.0 - i)
        dpab = dpa.astype(BF16)
        dpxb = dpx.astype(BF16)
        dwa = _dot_tn(xcb, dpab)
        dwx = _dot_tn(xcb, dpxb)
        dba = jnp.sum(dpa, axis=0, keepdims=True)
        dbx = jnp.sum(dpx, axis=0, keepdims=True)
        dxc = dxc + _dot_nt(dpab, wa_v) + _dot_nt(dpxb, wx_v)
        dcb = jnp.sum(dxc, axis=0, keepdims=True)
        dx = cw[3:4, :] * dxc
        dcw_rows = [None] * CONVW
        dcw_rows[3] = jnp.sum(dxc * x, axis=0, keepdims=True)
        for j in (1, 2, 3):
            dx = dx + cw[3 - j:4 - j, :] * _shift_up(dxc, j, row, 0.0)
            dcw_rows[3 - j] = jnp.sum(dxc * _shift_down(x, j, row, 0.0), axis=0, keepdims=True)
        dx_ref[0] = dx.astype(BF16)
        dcw = jnp.concatenate(dcw_rows, axis=0)
        first = pl.program_id(1) == 0

        @pl.when(first)
        def _():
            dcw_ref[...] = dcw
            dcb_ref[...] = dcb
            dwa_ref[0] = dwa
            dba_ref[...] = dba
            dwx_ref[0] = dwx
            dbx_ref[...] = dbx
            dlam_ref[...] = dlam

        @pl.when(jnp.logical_not(first))
        def _():
            dcw_ref[...] += dcw
            dcb_ref[...] += dcb
            dwa_ref[0] += dwa
            dba_ref[...] += dba
            dwx_ref[0] += dwx
            dbx_ref[...] += dbx
            dlam_ref[...] += dlam

    slab = lambda: pl.BlockSpec((1, S, RBW), lambda n, b: (b, 0, n))
    vec = lambda: pl.BlockSpec((1, RBW), lambda n, b: (0, n))
    mat = lambda: pl.BlockSpec((1, RBW, RBW), lambda n, b: (n, 0, 0))
    taps = lambda: pl.BlockSpec((CONVW, RBW), lambda n, b: (0, n))
    vshape = jax.ShapeDtypeStruct((1, DR), F32)
    mshape = jax.ShapeDtypeStruct((NRB, RBW, RBW), F32)
    return pl.pallas_call(
        body, grid=(NRB, BL), name="rnn_bwd",
        in_specs=[slab(), slab(), slab(), taps(), vec(), mat(), vec(), mat(), vec(), vec(),
                  pl.BlockSpec(memory_space=pl.ANY), pl.BlockSpec((8, LANES), lambda n, b: (0, 0))],
        out_specs=[slab(), taps(), vec(), mat(), vec(), mat(), vec(), vec()],
        out_shape=[jax.ShapeDtypeStruct((BL, S, A_W), BF16), jax.ShapeDtypeStruct((CONVW, DR), F32),
                   vshape, mshape, vshape, mshape, vshape, vshape],
        input_output_aliases={10: 0},
        compiler_params=_cp(("parallel", "arbitrary"), 48 * 1024 * 1024),
    )(proj3, h3, dh3, conv_w, conv_b, wa, ba, wx, bx, lam, slab_a3, token)


NQB = S // QB


def _rms_head(t, gain):
    rstd = lax.rsqrt(jnp.mean(t * t, axis=-1, keepdims=True) + EPS)
    return t * rstd * gain


def _rope(t, cs, sn):
    return t * cs + pltpu.roll(t, HD // 2, 1) * sn


def _rope_t(dy, cs, sn):
    return dy * cs - pltpu.roll(dy, HD // 2, 1) * sn


def _bdot_nt(a, b):
    return lax.dot_general(a, b, (((2,), (2,)), ((0,), (0,))), preferred_element_type=F32)


def _bdot(a, b):
    return lax.dot_general(a, b, (((2,), (1,)), ((0,), (0,))), preferred_element_type=F32)


def _bdot_tn(a, b):
    return lax.dot_general(a, b, (((1,), (1,)), ((0,), (0,))), preferred_element_type=F32)


STRIDE_MAX = 4


def _permute(buf, x, dil, dst, off=0):
    ln = S // dil
    if dil == 1:
        dst[pl.ds(off, S), :] = x.astype(dst.dtype)
        return
    buf[0] = x
    if dil <= STRIDE_MAX:
        for c in range(dil):
            dst[pl.ds(off + c * ln, ln), :] = buf.at[0][pl.ds(c, ln, stride=dil), :].astype(dst.dtype)
        return
    f, r = STRIDE_MAX, dil // STRIDE_MAX
    part = S // f
    for c1 in range(f):
        buf.at[1][pl.ds(c1 * part, part), :] = buf.at[0][pl.ds(c1, part, stride=f), :]
    for c1 in range(f):
        for c2 in range(r):
            dst[pl.ds(off + (c1 + f * c2) * ln, ln), :] = (
                buf.at[1][pl.ds(c1 * part + c2, ln, stride=r), :].astype(dst.dtype))


def _unpermute(buf, xp, dil, dst):
    ln = S // dil
    if dil == 1:
        dst[...] = xp
        return
    if dil <= STRIDE_MAX:
        for c in range(dil):
            dst[pl.ds(c, ln, stride=dil), :] = xp[c * ln:(c + 1) * ln]
        return
    f, r = STRIDE_MAX, dil // STRIDE_MAX
    part = S // f
    for c1 in range(f):
        for c2 in range(r):
            c = c1 + f * c2
            buf.at[1][pl.ds(c1 * part + c2, ln, stride=r), :] = xp[c * ln:(c + 1) * ln]
    for c1 in range(f):
        dst[pl.ds(c1, part, stride=f), :] = buf[1, pl.ds(c1 * part, part), :]


def _blocks3(ref, off=0):
    return ref[pl.ds(off, S), :].reshape(NQB, QB, HD)


def _att_prep(q_ref, k_ref, v_ref, cos_ref, sin_ref, qn, kn, dil, nat, qs, ksp, vsp):
    cs = cos_ref[...]
    sn = sin_ref[...]
    zero = jnp.zeros((QB, HD), BF16)
    ksp[pl.ds(0, QB), :] = zero
    vsp[pl.ds(0, QB), :] = zero
    _permute(nat, _rope(_rms_head(q_ref[0], qn), cs, sn), dil, qs)
    _permute(nat, _rope(_rms_head(k_ref[0], kn), cs, sn), dil, ksp, QB)
    _permute(nat, v_ref[0], dil, vsp, QB)


def _att_scores(qs, ksp, dil):
    nb = S // dil // QB
    q3 = _blocks3(qs)
    shape = (NQB, QB, QB)
    qi = lax.broadcasted_iota(jnp.int32, shape, 1)
    kj = lax.broadcasted_iota(jnp.int32, shape, 2)
    s_c = jnp.where(qi >= kj, _bdot_nt(q3, _blocks3(ksp, QB)) * SCALE, NEG)
    if nb == 1:
        return q3, s_c, None
    jj = lax.broadcasted_iota(jnp.int32, shape, 0)
    ok = (kj >= qi) & ((jj & (nb - 1)) != 0)
    s_p = jnp.where(ok, _bdot_nt(q3, _blocks3(ksp)) * SCALE, NEG)
    return q3, s_c, s_p


def _qkv_spec(kind, g):
    base = OFF_Q // HD + kind * (GW // HD) + g * NH
    return pl.BlockSpec((1, S, HD), lambda b, h: (b, 0, base + h))


def _attn_fwd(proj3, cos_t, sin_t, q_norm, k_norm):
    def body(*refs):
        qkv_refs = refs[:9]
        cos_ref, sin_ref, qn_ref, kn_ref, att_ref, lse_ref, w_ref, nat, qs, ksp, vsp, og = refs[9:]
        for g, (window, dil) in enumerate(PATTERNS):
            q_ref, k_ref, v_ref = qkv_refs[3 * g:3 * g + 3]
            _att_prep(q_ref, k_ref, v_ref, cos_ref, sin_ref, qn_ref[g:g + 1, :], kn_ref[g:g + 1, :], dil,
                      nat, qs, ksp, vsp)
            _, s_c, s_p = _att_scores(qs, ksp, dil)
            m = jnp.max(s_c, axis=-1, keepdims=True)
            if s_p is not None:
                m = jnp.maximum(m, jnp.max(s_p, axis=-1, keepdims=True))
            e_c = jnp.exp(s_c - m)
            den = jnp.sum(e_c, axis=-1, keepdims=True)
            o = _bdot(e_c.astype(BF16), _blocks3(vsp, QB))
            if s_p is not None:
                e_p = jnp.exp(s_p - m)
                den = den + jnp.sum(e_p, axis=-1, keepdims=True)
                o = o + _bdot(e_p.astype(BF16), _blocks3(vsp))
            _unpermute(nat, (o / den).reshape(S, HD), dil, og.at[g])
            _unpermute(nat, jnp.broadcast_to(m + jnp.log(den), (NQB, QB, HD)).reshape(S, HD), dil,
                       lse_ref.at[g, 0])
        l0 = lse_ref[0, 0]
        l1 = lse_ref[1, 0]
        l2 = lse_ref[2, 0]
        mx = jnp.maximum(jnp.maximum(l0, l1), l2)
        e0 = jnp.exp(l0 - mx)
        e1 = jnp.exp(l1 - mx)
        e2 = jnp.exp(l2 - mx)
        inv = 1.0 / (e0 + e1 + e2)
        w0 = e0 * inv
        w1 = e1 * inv
        w2 = e2 * inv
        w_ref[0, 0] = w0
        w_ref[1, 0] = w1
        w_ref[2, 0] = w2
        att_ref[0] = w0 * og[0] + w1 * og[1] + w2 * og[2]

    in_specs = [_qkv_spec(kind, g) for g in range(NG) for kind in range(3)]
    in_specs += [pl.BlockSpec((S, HD), lambda b, h: (0, 0)), pl.BlockSpec((S, HD), lambda b, h: (0, 0)),
                 pl.BlockSpec((NG, HD), lambda b, h: (0, 0)), pl.BlockSpec((NG, HD), lambda b, h: (0, 0))]
    stat = lambda: pl.BlockSpec((NG, 1, S, HD), lambda b, h: (0, b, 0, h))
    return pl.pallas_call(
        body, grid=(BL, NH), name="attn_fwd",
        in_specs=in_specs,
        out_specs=[pl.BlockSpec((1, S, HD), lambda b, h: (b, 0, h)), stat(), stat()],
        out_shape=[jax.ShapeDtypeStruct((BL, S, ATT), F32),
                   jax.ShapeDtypeStruct((NG, BL, S, ATT), F32),
                   jax.ShapeDtypeStruct((NG, BL, S, ATT), F32)],
        scratch_shapes=[pltpu.VMEM((2, S, HD), F32), pltpu.VMEM((S, HD), BF16), pltpu.VMEM((S + QB, HD), BF16),
                        pltpu.VMEM((S + QB, HD), BF16), pltpu.VMEM((NG, S, HD), F32)],
        compiler_params=_cp(("parallel", "parallel"), VMEM_BIG),
    )(*([proj3] * 9), cos_t, sin_t, q_norm, k_norm)


def _attn_bwd_group(g, proj3, cos_t, sin_t, qn_g, kn_g, lse, wts, datt3, sbar3, slabs):
    dil = PATTERNS[g][1]
    n_alias = 0 if slabs is None else 3

    def norm_rope_bwd(dpost, raw, gain, cs, sn):
        dn = _rope_t(dpost, cs, sn)
        rstd = lax.rsqrt(jnp.mean(raw * raw, axis=-1, keepdims=True) + EPS)
        xh = raw * rstd
        dgain = jnp.sum(dn * xh, axis=0, keepdims=True)
        gd = dn * gain
        draw = rstd * (gd - xh * jnp.mean(gd * xh, axis=-1, keepdims=True))
        return draw, dgain

    def body(*refs):
        (q_ref, k_ref, v_ref, cos_ref, sin_ref, qn_ref, kn_ref, lse_ref, w_ref, datt_ref, sbar_ref) = refs[:11]
        (dq_ref, dk_ref, dv_ref, dqn_ref, dkn_ref, nat, qs, ksp, vsp, dos, cvp, lsp, acc) = refs[11 + n_alias:]
        qn = qn_ref[...]
        kn = kn_ref[...]
        cs = cos_ref[...]
        sn = sin_ref[...]
        _att_prep(q_ref, k_ref, v_ref, cos_ref, sin_ref, qn, kn, dil, nat, qs, ksp, vsp)
        wv = w_ref[0, 0]
        _permute(nat, wv * datt_ref[0], dil, dos)
        _permute(nat, wv * sbar_ref[0], dil, cvp)
        _permute(nat, lse_ref[0, 0], dil, lsp)
        q3, s_c, s_p = _att_scores(qs, ksp, dil)
        do3 = _blocks3(dos)
        lse3 = _blocks3(lsp)[:, :, 0:1]
        cv3 = _blocks3(cvp)[:, :, 0:1]
        p_c = jnp.exp(s_c - lse3)
        ds_c = (p_c * (_bdot_nt(do3, _blocks3(vsp, QB)) - cv3)).astype(BF16)
        dq = _bdot(ds_c, _blocks3(ksp, QB))
        acc[0] = _bdot_tn(ds_c, q3).reshape(S, HD)
        acc[1] = _bdot_tn(p_c.astype(BF16), do3).reshape(S, HD)
        if s_p is not None:
            p_p = jnp.exp(s_p - lse3)
            ds_p = (p_p * (_bdot_nt(do3, _blocks3(vsp)) - cv3)).astype(BF16)
            dq = dq + _bdot(ds_p, _blocks3(ksp))
            early = pl.ds(0, S - QB)
            acc[0, early, :] += _bdot_tn(ds_p, q3).reshape(S, HD)[QB:]
            acc[1, early, :] += _bdot_tn(p_p.astype(BF16), do3).reshape(S, HD)[QB:]
        _unpermute(nat, (dq * SCALE).reshape(S, HD), dil, nat.at[0])
        draw, dqn = norm_rope_bwd(nat[0], q_ref[0], qn, cs, sn)
        dq_ref[0] = draw.astype(BF16)
        _unpermute(nat, acc[0] * SCALE, dil, nat.at[0])
        draw, dkn = norm_rope_bwd(nat[0], k_ref[0], kn, cs, sn)
        dk_ref[0] = draw.astype(BF16)
        _unpermute(nat, acc[1], dil, nat.at[0])
        dv_ref[0] = nat[0].astype(BF16)
        first = (pl.program_id(0) == 0) & (pl.program_id(1) == 0)

        @pl.when(first)
        def _():
            dqn_ref[...] = dqn
            dkn_ref[...] = dkn

        @pl.when(jnp.logical_not(first))
        def _():
            dqn_ref[...] += dqn
            dkn_ref[...] += dkn

    full = lambda r: pl.BlockSpec((r, HD), lambda b, h: (0, 0))
    stat = lambda: pl.BlockSpec((1, 1, S, HD), lambda b, h: (g, b, 0, h))
    slab = lambda: pl.BlockSpec((1, S, HD), lambda b, h: (b, 0, h))
    out_slab = lambda: pl.BlockSpec((1, S, HD), lambda b, h: (b, 0, g * NH + h))
    big = jax.ShapeDtypeStruct((BL, S, GW), BF16)
    vecs = jax.ShapeDtypeStruct((1, HD), F32)
    in_specs = [_qkv_spec(0, g), _qkv_spec(1, g), _qkv_spec(2, g), full(S), full(S), full(1), full(1),
                stat(), stat(), slab(), slab()]
    args = [proj3, proj3, proj3, cos_t, sin_t, qn_g, kn_g, lse, wts, datt3, sbar3]
    aliases = {}
    if slabs is not None:
        in_specs += [pl.BlockSpec(memory_space=pl.ANY)] * 3
        args += list(slabs)
        aliases = {11: 0, 12: 1, 13: 2}
    return pl.pallas_call(
        body, grid=(BL, NH), name="attn_bwd_g%d" % g,
        in_specs=in_specs,
        out_specs=[out_slab(), out_slab(), out_slab(), full(1), full(1)],
        out_shape=[big, big, big, vecs, vecs],
        scratch_shapes=[pltpu.VMEM((2, S, HD), F32), pltpu.VMEM((S, HD), BF16), pltpu.VMEM((S + QB, HD), BF16),
                        pltpu.VMEM((S + QB, HD), BF16), pltpu.VMEM((S, HD), BF16), pltpu.VMEM((S, HD), F32),
                        pltpu.VMEM((S, HD), F32), pltpu.VMEM((2, S, HD), F32)],
        input_output_aliases=aliases,
        compiler_params=_cp(("arbitrary", "arbitrary"), VMEM_BIG),
    )(*args)


def _tail(x, proj, h, att, p, tgt, w_o_rnn, w_o_att_t, w_out, w_pg, w_ple_t, norm_ple, b_pg, tm=256):
    nt = T // tm
    inv_d = 1.0 / D

    def body(x_ref, h_ref, zr_ref, att_ref, za_ref, g0a_ref, g0b_ref, g1a_ref, g1b_ref, p_ref, tgt_ref,
             np_ref, bpg_ref, wor_hbm, woa_hbm, wout_hbm, wpg_hbm, wple_hbm,
             dx1_ref, merged_ref, n1_ref, dpre_ref, dpe_ref, dyr_ref, dya_ref, slab_a_ref, slab_c_ref, dh_ref,
             datt_ref, sbar_ref, yrnn_ref, yatt_ref, loss_ref, dnp_ref, dbpg_ref,
             wor, woa, wout, wpg, wple):
        first = pl.program_id(0) == 0

        @pl.when(first)
        def _():
            pltpu.sync_copy(wor_hbm, wor)
            pltpu.sync_copy(woa_hbm, woa)
            pltpu.sync_copy(wout_hbm, wout)
            pltpu.sync_copy(wpg_hbm, wpg)
            pltpu.sync_copy(wple_hbm, wple)

        xv = x_ref[...]
        hv = h_ref[...]
        zr = zr_ref[...]
        av = att_ref[...]
        za = za_ref[...]
        szr = _sigmoid(zr)
        silu_r = zr * szr
        yrnn_b = (hv * silu_r).astype(BF16)
        sza = _sigmoid(za)
        silu_a = za * sza
        yatt_b = (av * silu_a).astype(BF16)
        yrnn_ref[...] = yrnn_b
        yatt_ref[...] = yatt_b
        yr = _dot(yrnn_b, wor[...])
        ya = _dot_nt(yatt_b, woa[...])
        g0 = _sigmoid(jnp.concatenate([g0a_ref[...], g0b_ref[...]], axis=1))
        g1 = _sigmoid(jnp.concatenate([g1a_ref[...], g1b_ref[...]], axis=1))
        merged_b = (g0 * yr + g1 * ya).astype(BF16)
        merged_ref[...] = merged_b
        x1 = xv + _dot(merged_b, wout[...])
        rstd = lax.rsqrt(jnp.mean(x1 * x1, axis=-1, keepdims=True) + EPS)
        xh = x1 * rstd
        npl = np_ref[...]
        n1_b = (xh * npl).astype(BF16)
        n1_ref[...] = n1_b
        pg = _sigmoid(_dot(n1_b, wpg[...]) + bpg_ref[...])
        pe = _dot_nt(p_ref[...].astype(BF16), wple[...])
        err = x1 + pg * pe - tgt_ref[...]
        loss_t = 0.5 * inv_d * jnp.sum(err * err)
        dy = err * inv_d
        dpe_ref[...] = (dy * pg).astype(BF16)
        dpre = dy * pe * pg * (1.0 - pg)
        dpre_b = dpre.astype(BF16)
        dpre_ref[...] = dpre_b
        dn1 = _dot_nt(dpre_b, wpg[...])
        dnp = jnp.sum(dn1 * xh, axis=0, keepdims=True)
        dbpg = jnp.sum(dpre, axis=0, keepdims=True)
        gd = dn1 * npl
        dx1 = dy + rstd * (gd - xh * jnp.mean(gd * xh, axis=-1, keepdims=True))
        dx1_ref[...] = dx1
        dmerged = _dot_nt(dx1.astype(BF16), wout[...])
        dyr_b = (dmerged * g0).astype(BF16)
        dya_b = (dmerged * g1).astype(BF16)
        dyr_ref[...] = dyr_b
        dya_ref[...] = dya_b
        slab_c_ref[:, ATT:ATT + D] = (dmerged * yr * g0 * (1.0 - g0)).astype(BF16)
        slab_c_ref[:, ATT + D:ATT + 2 * D] = (dmerged * ya * g1 * (1.0 - g1)).astype(BF16)
        dyrnn = _dot_nt(dyr_b, wor[...])
        dyatt = _dot(dya_b, woa[...])
        dh_ref[...] = dyrnn * silu_r
        slab_a_ref[...] = (dyrnn * hv * szr * (1.0 + zr * (1.0 - szr))).astype(BF16)
        datt = dyatt * silu_a
        datt_ref[...] = datt
        slab_c_ref[:, 0:ATT] = (dyatt * av * sza * (1.0 + za * (1.0 - sza))).astype(BF16)
        da = datt * av
        for hh in range(NH):
            seg = slice(hh * HD, (hh + 1) * HD)
            sbar_ref[:, seg] = jnp.broadcast_to(jnp.sum(da[:, seg], axis=-1, keepdims=True), (tm, HD))

        @pl.when(first)
        def _():
            loss_ref[...] = jnp.full((8, LANES), loss_t, F32)
            dnp_ref[...] = dnp
            dbpg_ref[...] = dbpg

        @pl.when(jnp.logical_not(first))
        def _():
            loss_ref[...] += jnp.full((8, LANES), loss_t, F32)
            dnp_ref[...] += dnp
            dbpg_ref[...] += dbpg

    tok = lambda w: pl.BlockSpec((tm, w), lambda i: (i, 0))
    col = lambda w, blk: pl.BlockSpec((tm, w), lambda i: (i, blk))
    vec = lambda: pl.BlockSpec((1, D), lambda i: (0, 0))
    hbm = lambda: pl.BlockSpec(memory_space=pl.ANY)
    gb = OFF_G // 512
    in_specs = [tok(D), tok(DR), col(DR, 1), tok(ATT), col(ATT, OFF_ZA // ATT),
                col(512, gb), col(512, gb + 1), col(512, gb + 2), col(512, gb + 3),
                tok(PLE), tok(D), vec(), vec(), hbm(), hbm(), hbm(), hbm(), hbm()]
    sh = lambda w, dt: jax.ShapeDtypeStruct((T, w), dt)
    out_shape = [sh(D, F32), sh(D, BF16), sh(D, BF16), sh(D, BF16), sh(D, BF16), sh(D, BF16), sh(D, BF16),
                 sh(A_W, BF16), sh(C_W, BF16), sh(DR, F32), sh(ATT, F32), sh(ATT, F32),
                 sh(DR, BF16), sh(ATT, BF16),
                 jax.ShapeDtypeStruct((8, LANES), F32), jax.ShapeDtypeStruct((1, D), F32),
                 jax.ShapeDtypeStruct((1, D), F32)]
    out_specs = [tok(D), tok(D), tok(D), tok(D), tok(D), tok(D), tok(D), col(DR, 1), tok(C_W), tok(DR),
                 tok(ATT), tok(ATT), tok(DR), tok(ATT),
                 pl.BlockSpec((8, LANES), lambda i: (0, 0)), vec(), vec()]
    return pl.pallas_call(
        body, grid=(nt,), name="tail_fwd_bwd",
        in_specs=in_specs, out_specs=out_specs, out_shape=out_shape,
        scratch_shapes=[pltpu.VMEM((DR, D), BF16), pltpu.VMEM((D, ATT), BF16), pltpu.VMEM((D, D), BF16),
                        pltpu.VMEM((D, D), BF16), pltpu.VMEM((D, PLE), BF16)],
        compiler_params=_cp(("arbitrary",), VMEM_BIG),
    )(x, h, proj, att, proj, proj, proj, proj, proj, p, tgt, norm_ple, b_pg, w_o_rnn, w_o_att_t, w_out, w_pg, w_ple_t)


def _input_norm_bwd(x, dhn, dx1, gain, tm=512):
    def body(x_ref, dhn_ref, dx1_ref, g_ref, dx_ref, dg_ref):
        xv = x_ref[...]
        rstd = lax.rsqrt(jnp.mean(xv * xv, axis=-1, keepdims=True) + EPS)
        xh = xv * rstd
        dn = dhn_ref[...]
        dg = jnp.sum(dn * xh, axis=0, keepdims=True)
        gd = dn * g_ref[...]
        dx_ref[...] = dx1_ref[...] + rstd * (gd - xh * jnp.mean(gd * xh, axis=-1, keepdims=True))
        first = pl.program_id(0) == 0

        @pl.when(first)
        def _():
            dg_ref[...] = dg

        @pl.when(jnp.logical_not(first))
        def _():
            dg_ref[...] += dg

    tok = lambda: pl.BlockSpec((tm, D), lambda i: (i, 0))
    vec = lambda: pl.BlockSpec((1, D), lambda i: (0, 0))
    return pl.pallas_call(
        body, grid=(T // tm,), name="input_norm_bwd",
        in_specs=[tok(), tok(), tok(), vec()], out_specs=[tok(), vec()],
        out_shape=[jax.ShapeDtypeStruct((T, D), F32), jax.ShapeDtypeStruct((1, D), F32)],
        compiler_params=_cp(("arbitrary",), VMEM_MID),
    )(x, dhn, dx1, gain)


def _rope_tables():
    pos = jnp.arange(S, dtype=F32)
    inv_freq = ROPE_THETA ** (-jnp.arange(0, HD, 2, dtype=F32) / HD)
    ang = pos[:, None] * inv_freq[None, :]
    cos, sin = jnp.cos(ang), jnp.sin(ang)
    return jnp.concatenate([cos, cos], axis=1), jnp.concatenate([-sin, sin], axis=1)


def _local_step(x, p, tgt, w_in_t, w_o_rnn, w_o_att_t, w_out, w_pg, w_ple_t, conv_w, norm_mix, b_in, conv_b,
                w_rg_a, b_rg_a, w_rg_x, b_rg_x, lam, q_norm, k_norm, norm_ple, b_pg, start_reduce=None):
    if start_reduce is None:
        start_reduce = lambda arrs, tag: (jnp.zeros((8, LANES), F32), arrs)
    cos_t, sin_t = _rope_tables()
    wa_b = w_rg_a.astype(BF16)
    wx_b = w_rg_x.astype(BF16)

    hn = _rmsnorm_fwd(x, norm_mix)
    proj = _in_proj(hn, w_in_t, b_in)
    proj3 = proj.reshape(BL, S, NIN)
    h3 = _rnn_fwd(proj3, conv_w, conv_b, wa_b, b_rg_a, wx_b, b_rg_x, lam)
    att3, lse, wts = _attn_fwd(proj3, cos_t, sin_t, q_norm, k_norm)
    (dx1, merged, n1, dpre, dpe, dyr, dya, slab_a, slab_c, dh, datt, sbar, yrnn, yatt, loss8, dnp, dbpg) = _tail(
        x, proj, h3.reshape(T, DR), att3.reshape(T, ATT), p, tgt, w_o_rnn, w_o_att_t, w_out, w_pg, w_ple_t,
        norm_ple, b_pg)

    token, pending_out = start_reduce([
        _mm_tn(yrnn, dyr, 640, 2048, "dw_o_rnn"),
        _mm_tn(dya, yatt, 512, 2048, "dw_o_att_t"),
        _mm_tn(merged, dx1, 512, 2048, "dw_out"),
        _mm_tn(n1, dpre, 512, 2048, "dw_ple_gate"),
        _mm_tn(dpe, p, 512, 2048, "dw_ple_t")], "out")

    slab_a3, dcw, dcb, dwa, dba, dwx, dbx, dlam = _rnn_bwd(
        proj3, h3, dh.reshape(BL, S, DR), slab_a.reshape(BL, S, A_W), conv_w, conv_b, wa_b, b_rg_a, wx_b, b_rg_x, lam,
        token)
    datt3 = datt.reshape(BL, S, ATT)
    sbar3 = sbar.reshape(BL, S, ATT)
    slabs = None
    dqn = []
    dkn = []
    for g in range(NG):
        dq, dk, dv, dqn_g, dkn_g = _attn_bwd_group(g, proj3, cos_t, sin_t, q_norm[g:g + 1], k_norm[g:g + 1],
                                                   lse, wts, datt3, sbar3, slabs)
        slabs = (dq, dk, dv)
        dqn.append(dqn_g)
        dkn.append(dkn_g)
    pieces = [slab_a3.reshape(T, A_W)] + [t.reshape(T, GW) for t in slabs] + [slab_c]
    dw_in_t, db_in = _dw_in(pieces, hn)
    token, pending_in = start_reduce([dw_in_t], "in")
    dhn = _dhn(pieces, w_in_t, token)
    grad_x, dnm = _input_norm_bwd(x, dhn, dx1, norm_mix)

    small = dict(w_rg_a=dwa, w_rg_x=dwx, norm_mix=dnm, b_in=db_in, conv_b=dcb, b_rg_a=dba, b_rg_x=dbx,
                 lru_lambda=dlam, q_norm=dqn, k_norm=dkn, norm_ple=dnp, b_ple_gate=dbpg, conv_w=dcw)
    return loss8[0, 0], grad_x, pending_out, pending_in, small


MESH = pl.DeviceIdType.MESH
HBM_SPEC = pl.BlockSpec(memory_space=pl.ANY)


def _my_pos():
    return lax.axis_index("x"), lax.axis_index("y"), lax.axis_index("c")


def _flip(pos, k):
    x, y, c = pos
    return (1 - x if k & 4 else x, 1 - y if k & 2 else y, 1 - c if k & 1 else c)


def _lin(pos):
    return 4 * pos[0] + 2 * pos[1] + pos[2]


def _chip(pos):
    return 2 * pos[0] + pos[1]


def _all_gather_two_level(shards, name):
    na = len(shards)

    def body(*refs):
        x_refs = refs[:na]
        out_refs = refs[na:2 * na]
        send_sems, recv_sems, local_sems = refs[2 * na:]
        me = _my_pos()
        sibling = _flip(me, 1)
        chips = [_flip(me, 4), _flip(me, 2), _flip(me, 6)]

        def copy(i, k, block, to, from_x=False):
            dst = out_refs[i].at[_lin(block)]
            return pltpu.make_async_remote_copy(
                src_ref=x_refs[i] if from_x else dst, dst_ref=dst,
                send_sem=send_sems.at[7 * i + k], recv_sem=recv_sems.at[7 * i + k], device_id=to, device_id_type=MESH)

        started = []
        for i in range(na):
            mine = pltpu.make_async_copy(x_refs[i], out_refs[i].at[_lin(me)], local_sems.at[i])
            mine.start()
            started.append(mine)
        sends = []
        for i in range(na):
            cps = [copy(i, 0, me, sibling, True)] + [copy(i, 1 + j, me, chip, True) for j, chip in enumerate(chips)]
            for cp in cps:
                cp.start()
            sends += cps
        for i in range(na):
            for j, chip in enumerate(chips):
                copy(i, 1 + j, chip, me).wait_recv()
                fwd = copy(i, 4 + j, chip, sibling)
                fwd.start()
                sends.append(fwd)
        for i in range(na):
            copy(i, 0, sibling, me).wait_recv()
            for j, chip in enumerate(chips):
                copy(i, 4 + j, _flip(chip, 1), me).wait_recv()
        for cp in sends:
            cp.wait_send()
        for mine in started:
            mine.wait()

    return pl.pallas_call(
        body, name=name,
        out_shape=[jax.ShapeDtypeStruct((NDEV,) + s.shape, s.dtype) for s in shards],
        in_specs=[HBM_SPEC] * na, out_specs=[HBM_SPEC] * na,
        scratch_shapes=[pltpu.SemaphoreType.DMA((7 * na,)), pltpu.SemaphoreType.DMA((7 * na,)),
                        pltpu.SemaphoreType.DMA((na,))],
    )(*shards)


def _all_gather_direct(shard, name):
    def body(x_ref, out_ref, send_sems, recv_sems, local_sem):
        me = _my_pos()
        mine = pltpu.make_async_copy(x_ref, out_ref.at[_lin(me)], local_sem)
        mine.start()
        sends = []
        for k in range(1, NDEV):
            cp = pltpu.make_async_remote_copy(
                src_ref=x_ref, dst_ref=out_ref.at[_lin(me)], send_sem=send_sems.at[k - 1],
                recv_sem=recv_sems.at[k - 1], device_id=_flip(me, k), device_id_type=MESH)
            cp.start()
            sends.append(cp)
        for k in range(1, NDEV):
            peer = _flip(me, k)
            pltpu.make_async_remote_copy(
                src_ref=x_ref, dst_ref=out_ref.at[_lin(peer)], send_sem=send_sems.at[k - 1],
                recv_sem=recv_sems.at[k - 1], device_id=peer, device_id_type=MESH).wait_recv()
        for cp in sends:
            cp.wait_send()
        mine.wait()

    return pl.pallas_call(
        body, name=name,
        out_shape=jax.ShapeDtypeStruct((NDEV,) + shard.shape, shard.dtype),
        in_specs=[HBM_SPEC], out_specs=HBM_SPEC,
        scratch_shapes=[pltpu.SemaphoreType.DMA((7,)), pltpu.SemaphoreType.DMA((7,)), pltpu.SemaphoreType.DMA],
    )(shard)


def _exchange_within_chip(parts, name):
    na = len(parts)

    def body(*refs):
        a_refs = refs[:na]
        recv_refs = refs[na:2 * na]
        send_sems, recv_sems = refs[2 * na:]
        me = _my_pos()
        c = me[2]
        sibling = _flip(me, 1)
        remote = []
        for i in range(na):
            for q in range(NCHIP):
                rc = pltpu.make_async_remote_copy(
                    src_ref=a_refs[i].at[q, 1 - c], dst_ref=recv_refs[i].at[q],
                    send_sem=send_sems.at[NCHIP * i + q], recv_sem=recv_sems.at[NCHIP * i + q],
                    device_id=sibling, device_id_type=MESH)
                rc.start()
                remote.append(rc)
        for rc in remote:
            rc.wait_recv()
        for rc in remote:
            rc.wait_send()

    return pl.pallas_call(
        body, name=name, out_shape=[jax.ShapeDtypeStruct((NCHIP,) + a.shape[2:], a.dtype) for a in parts],
        in_specs=[HBM_SPEC] * na, out_specs=[HBM_SPEC] * na,
        scratch_shapes=[pltpu.SemaphoreType.DMA((NCHIP * na,)), pltpu.SemaphoreType.DMA((NCHIP * na,))],
    )(*parts)


HBM_ONLY = pl.BlockSpec(memory_space=pltpu.HBM)
SEM_SPEC = pl.BlockSpec(memory_space=pltpu.SEMAPHORE)
SPLIT_COPY = pltpu.CompilerParams(has_side_effects=pltpu.SideEffectType.DATAFLOW_SIDE_EFFECTING)


def _chip_peers(me):
    return [_flip(me, 4), _flip(me, 2), _flip(me, 6)]


def _between_chips_start(parts, name):
    na = len(parts)

    def body(*refs):
        a_refs = refs[:na]
        land_refs = refs[na:2 * na]
        send_sems, recv_sems = refs[2 * na], refs[2 * na + 1]
        token = refs[-1]
        me = _my_pos()
        myq = _chip(me)
        for i in range(na):
            for j, peer in enumerate(_chip_peers(me)):
                pltpu.make_async_remote_copy(
                    src_ref=a_refs[i].at[_chip(peer)], dst_ref=land_refs[i].at[myq],
                    send_sem=send_sems.at[3 * i + j], recv_sem=recv_sems.at[3 * i + j],
                    device_id=peer, device_id_type=MESH).start()
        token[...] = jnp.zeros_like(token)

    hbm = [pltpu.HBM(a.shape, a.dtype) for a in parts]
    srcs = [pltpu.with_memory_space_constraint(a, pltpu.HBM) for a in parts]
    lands = [pltpu.with_memory_space_constraint(lax.empty(a.shape, a.dtype), pltpu.HBM) for a in parts]
    res = pl.pallas_call(
        body, name=name,
        out_shape=(pltpu.SemaphoreType.DMA((3 * na,)), pltpu.SemaphoreType.DMA((3 * na,)), *hbm, *hbm,
                   jax.ShapeDtypeStruct((8, LANES), F32)),
        in_specs=[HBM_ONLY] * (2 * na),
        out_specs=(SEM_SPEC, SEM_SPEC, *([HBM_ONLY] * (2 * na)), pl.BlockSpec(memory_space=pltpu.VMEM)),
        input_output_aliases={i: 2 + i for i in range(2 * na)},
        compiler_params=SPLIT_COPY,
    )(*srcs, *lands)
    return res[-1], (res[0], res[1], list(res[2:2 + na]), list(res[2 + na:2 + 2 * na]))


def _between_chips_wait(pending, after, name):
    send_sems, recv_sems, parts, lands = pending
    na = len(parts)

    def body(*refs):
        a_refs = refs[:na]
        land_refs = refs[na:2 * na]
        send_sems, recv_sems = refs[2 * na], refs[2 * na + 1]
        me = _my_pos()
        for i in range(na):
            for j, peer in enumerate(_chip_peers(me)):
                cp = pltpu.make_async_remote_copy(
                    src_ref=a_refs[i].at[_chip(peer)], dst_ref=land_refs[i].at[_chip(peer)],
                    send_sem=send_sems.at[3 * i + j], recv_sem=recv_sems.at[3 * i + j],
                    device_id=peer, device_id_type=MESH)
                cp.wait_send()
                cp.wait_recv()

    hbm = [pltpu.HBM(a.shape, a.dtype) for a in parts]
    res = pl.pallas_call(
        body, name=name, out_shape=(*hbm, *hbm),
        in_specs=[HBM_ONLY] * (2 * na) + [SEM_SPEC, SEM_SPEC, pl.BlockSpec(memory_space=pl.ANY)],
        out_specs=[HBM_ONLY] * (2 * na),
        input_output_aliases={i: i for i in range(2 * na)},
        compiler_params=SPLIT_COPY,
    )(*parts, *lands, send_sems, recv_sems, after)
    return list(res[:na]), list(res[na:])


def _exchange_all(parts, name):
    na = len(parts)

    def body(*refs):
        a_refs = refs[:na]
        out_refs = refs[na:2 * na]
        send_sems, recv_sems = refs[2 * na:]
        me = _my_pos()
        sends = []
        for i in range(na):
            for k in range(1, NDEV):
                peer = _flip(me, k)
                cp = pltpu.make_async_remote_copy(
                    src_ref=a_refs[i].at[_lin(peer)], dst_ref=out_refs[i].at[_lin(me)],
                    send_sem=send_sems.at[7 * i + k - 1], recv_sem=recv_sems.at[7 * i + k - 1],
                    device_id=peer, device_id_type=MESH)
                cp.start()
                sends.append(cp)
        for i in range(na):
            for k in range(1, NDEV):
                peer = _flip(me, k)
                pltpu.make_async_remote_copy(
                    src_ref=a_refs[i].at[_lin(peer)], dst_ref=out_refs[i].at[_lin(peer)],
                    send_sem=send_sems.at[7 * i + k - 1], recv_sem=recv_sems.at[7 * i + k - 1],
                    device_id=peer, device_id_type=MESH).wait_recv()
        for cp in sends:
            cp.wait_send()

    return pl.pallas_call(
        body, name=name, out_shape=[jax.ShapeDtypeStruct(a.shape, a.dtype) for a in parts],
        in_specs=[HBM_SPEC] * na, out_specs=[HBM_SPEC] * na,
        scratch_shapes=[pltpu.SemaphoreType.DMA((7 * na,)), pltpu.SemaphoreType.DMA((7 * na,))],
    )(*parts)


def _scalar(v):
    return jnp.asarray(v, jnp.int32).reshape(1)


def _sum_pairs(parts, theirs, name):
    na = len(parts)

    def body(c_ref, *refs):
        for i in range(na):
            o_ref = refs[2 * na + i]
            o_ref[0] = (refs[i][0, 0].astype(F32) + refs[na + i][0].astype(F32)).astype(o_ref.dtype)

    def mine_spec(a):
        return pl.BlockSpec((1, 1) + a.shape[2:], lambda q, c_ref: (q, c_ref[0], 0, 0))

    def spec(a):
        return pl.BlockSpec((1,) + a.shape[1:], lambda q, c_ref: (q, 0, 0))

    return pl.pallas_call(
        body, name=name,
        grid_spec=pltpu.PrefetchScalarGridSpec(
            num_scalar_prefetch=1, grid=(NCHIP,),
            in_specs=[mine_spec(a) for a in parts] + [spec(a) for a in theirs],
            out_specs=[spec(a) for a in theirs]),
        out_shape=[jax.ShapeDtypeStruct(a.shape, a.dtype) for a in theirs],
        compiler_params=_cp(("arbitrary",), VMEM_BIG),
    )(_scalar(lax.axis_index("c")), *parts, *theirs)


def _others(q, mine, nblk=NCHIP):
    return jnp.where(q == mine, (q + 1) % nblk, q)


def _sum_chips_adamw(own, recv, wv, mv, vv, tr, name):
    _, r, w = recv.shape

    def body(q_ref, own_ref, r0, r1, r2, r3, w_ref, m_ref, v_ref, g_ref, d_ref, m2_ref, v2_ref):
        myq = q_ref[0]
        acc = None
        for q, r_ref in enumerate((r0, r1, r2, r3)):
            term = jnp.where(myq == q, own_ref[0], r_ref[0]).astype(F32)
            acc = term if acc is None else acc + term
        g_ref[...] = acc
        delta, m2, v2 = _adam_math(w_ref[...], acc, m_ref[...], v_ref[...])
        d_ref[...] = delta
        m2_ref[...] = m2
        v2_ref[...] = v2

    def recv_spec(q):
        return pl.BlockSpec((1, tr, w), lambda i, q_ref: (_others(q, q_ref[0]), i, 0))

    rows = lambda: pl.BlockSpec((tr, w), lambda i, q_ref: (i, 0))
    shp = jax.ShapeDtypeStruct((r, w), F32)
    return pl.pallas_call(
        body, name=name,
        grid_spec=pltpu.PrefetchScalarGridSpec(
            num_scalar_prefetch=1, grid=(r // tr,),
            in_specs=[pl.BlockSpec((1, tr, w), lambda i, q_ref: (q_ref[0], i, 0))]
            + [recv_spec(q) for q in range(NCHIP)] + [rows(), rows(), rows()],
            out_specs=[rows(), rows(), rows(), rows()]),
        out_shape=[shp, shp, shp, shp],
        compiler_params=_cp(("arbitrary",), VMEM_MID),
    )(_scalar(_chip(_my_pos())), own, recv, recv, recv, recv, wv, mv, vv)


def _sum_blocks_small(own, recv, mine, transpose, name):
    na = len(recv)
    nblk = recv[0].shape[0]

    def body(q_ref, *refs):
        me = q_ref[0]
        for i in range(na):
            acc = None
            for q in range(nblk):
                term = jnp.where(me == q, refs[i][0], refs[na * (1 + q) + i][0]).astype(F32)
                acc = term if acc is None else acc + term
            refs[na * (1 + nblk) + i][...] = acc.T if transpose[i] else acc

    def oshape(a, tr):
        r, w = a.shape[1:]
        return (w, r) if tr else (r, w)

    own_spec = lambda a: pl.BlockSpec((1,) + a.shape[1:], lambda s, q_ref: (q_ref[0], 0, 0))
    recv_spec = lambda a, q: pl.BlockSpec((1,) + a.shape[1:], lambda s, q_ref: (_others(q, q_ref[0], nblk), 0, 0))
    out_spec = lambda shp: pl.BlockSpec(shp, lambda s, q_ref: (0, 0))
    in_specs = [own_spec(a) for a in own]
    for q in range(nblk):
        in_specs += [recv_spec(a, q) for a in recv]
    return pl.pallas_call(
        body, name=name,
        grid_spec=pltpu.PrefetchScalarGridSpec(
            num_scalar_prefetch=1, grid=(1,), in_specs=in_specs,
            out_specs=[out_spec(oshape(a, tr)) for a, tr in zip(recv, transpose)]),
        out_shape=[jax.ShapeDtypeStruct(oshape(a, tr), F32) for a, tr in zip(recv, transpose)],
        compiler_params=_cp(("arbitrary",), VMEM_MID),
    )(_scalar(mine), *own, *(list(recv) * nblk))


def _rep_offsets():
    offs = []
    o = 0
    for r in REP_ROWS:
        offs.append(o)
        o += r
    return offs


def _pack_small_grads(g):
    offs = _rep_offsets()

    def body(dwa, dwx, dnm, dbin, dcb, dba, dbx, dlam, dq0, dq1, dq2, dk0, dk1, dk2, dnp, dbpg, o_ref):
        o_ref[pl.ds(REP_TOTAL_ROWS - 2, NDEV * REP_ROWS_DEV - REP_TOTAL_ROWS + 2), :] = jnp.zeros(
            (NDEV * REP_ROWS_DEV - REP_TOTAL_ROWS + 2, LANES), F32)
        for n in range(NRB):
            o_ref[pl.ds(offs[0] + n * RBW, RBW), :] = dwa[n]
            o_ref[pl.ds(offs[1] + n * RBW, RBW), :] = dwx[n]

        def put_vec(off, ref, rows):
            for k in range(rows):
                o_ref[pl.ds(off + k, 1), :] = ref[:, k * LANES:(k + 1) * LANES]

        put_vec(offs[2], dnm, REP_ROWS[2])
        put_vec(offs[3], dbin, REP_ROWS[3])
        put_vec(offs[4], dcb, REP_ROWS[4])
        put_vec(offs[5], dba, REP_ROWS[5])
        put_vec(offs[6], dbx, REP_ROWS[6])
        put_vec(offs[7], dlam, REP_ROWS[7])
        for k, ref in enumerate((dq0, dq1, dq2)):
            o_ref[pl.ds(offs[8] + k, 1), :] = ref[...]
        for k, ref in enumerate((dk0, dk1, dk2)):
            o_ref[pl.ds(offs[9] + k, 1), :] = ref[...]
        put_vec(offs[10], dnp, REP_ROWS[10])
        put_vec(offs[11], dbpg, REP_ROWS[11])

    args = [g["w_rg_a"], g["w_rg_x"], g["norm_mix"], g["b_in"], g["conv_b"], g["b_rg_a"], g["b_rg_x"],
            g["lru_lambda"], *g["q_norm"], *g["k_norm"], g["norm_ple"], g["b_ple_gate"]]
    full = lambda shp: pl.BlockSpec(shp, lambda: (0,) * len(shp))
    return pl.pallas_call(
        body, name="pack_small_grads",
        in_specs=[full(a.shape) for a in args],
        out_specs=full((NDEV * REP_ROWS_DEV, LANES)),
        out_shape=jax.ShapeDtypeStruct((NDEV * REP_ROWS_DEV, LANES), F32),
    )(*args)


def _adam_math(wv, gv, mv, vv):
    c1 = 1.0 - B1 ** STEP
    c2 = 1.0 - B2 ** STEP
    m2 = B1 * mv + (1.0 - B1) * gv
    v2 = B2 * vv + (1.0 - B2) * (gv * gv)
    delta = (-LR) * ((m2 / c1) / (jnp.sqrt(v2 / c2) + AEPS) + WD * wv)
    return delta, m2, v2


def _adamw_small(rep_flat, w, m, v):
    offs = _rep_offsets()
    n = len(REP_NAMES)

    def body(*refs):
        g_ref = refs[0]
        w_refs = refs[1:1 + n]
        m_refs = refs[1 + n:1 + 2 * n]
        v_refs = refs[1 + 2 * n:1 + 3 * n]
        outs = refs[1 + 3 * n:]
        go, do, mo, vo = outs[:n], outs[n:2 * n], outs[2 * n:3 * n], outs[3 * n:]

        def emit(i, idx, gv):
            go[i][idx] = gv
            delta, m2, v2 = _adam_math(w_refs[i][idx], gv, m_refs[i][idx], v_refs[i][idx])
            do[i][idx] = delta
            mo[i][idx] = m2
            vo[i][idx] = v2

        for i in range(n):
            if i < 2:
                for b in range(NRB):
                    emit(i, b, g_ref[pl.ds(offs[i] + b * RBW, RBW), :])
            elif REP_NAMES[i] in ("q_norm", "k_norm"):
                emit(i, slice(None), g_ref[pl.ds(offs[i], NG), :])
            else:
                gv = jnp.concatenate([g_ref[pl.ds(offs[i] + k, 1), :] for k in range(REP_ROWS[i])], axis=1)
                emit(i, slice(None), gv)

    full = lambda shp: pl.BlockSpec(shp, lambda: (0,) * len(shp))
    pspecs = [full(a.shape) for a in w]
    pshapes = [jax.ShapeDtypeStruct(a.shape, F32) for a in w]
    res = pl.pallas_call(
        body, name="adamw_small",
        in_specs=[full(rep_flat.shape)] + pspecs * 3,
        out_specs=pspecs * 4, out_shape=pshapes * 4,
        compiler_params=_cp(None, VMEM_MID),
    )(rep_flat, *w, *m, *v)
    return res[:n], res[n:2 * n], res[2 * n:3 * n], res[3 * n:]


def _adamw_many(w, g, m, v):
    n = len(w)

    def body(*refs):
        for i in range(n):
            delta, m2, v2 = _adam_math(refs[i][...], refs[n + i][...], refs[2 * n + i][...], refs[3 * n + i][...])
            refs[4 * n + i][...] = delta
            refs[5 * n + i][...] = m2
            refs[6 * n + i][...] = v2

    full = lambda shp: pl.BlockSpec(shp, lambda: (0,) * len(shp))
    specs = [full(a.shape) for a in w]
    shapes = [jax.ShapeDtypeStruct(a.shape, F32) for a in w]
    res = pl.pallas_call(
        body, name="adamw_shards",
        in_specs=specs * 4, out_specs=specs * 3, out_shape=shapes * 3,
        compiler_params=_cp(None, VMEM_MID),
    )(*w, *g, *m, *v)
    return res[:n], res[n:2 * n], res[2 * n:]


def kernel(x, p, norm_mix, w_in, b_in, conv_w, conv_b, w_rg_a, b_rg_a, w_rg_x, b_rg_x, lru_lambda, q_norm, k_norm, w_o_rnn, w_o_att, w_out, norm_ple, w_ple_gate, b_ple_gate, w_ple, loss_target, m_norm_mix, m_w_in, m_b_in, m_conv_w, m_conv_b, m_w_rg_a, m_b_rg_a, m_w_rg_x, m_b_rg_x, m_lru_lambda, m_q_norm, m_k_norm, m_w_o_rnn, m_w_o_att, m_w_out, m_norm_ple, m_w_ple_gate, m_b_ple_gate, m_w_ple, v_norm_mix, v_w_in, v_b_in, v_conv_w, v_conv_b, v_w_rg_a, v_b_rg_a, v_w_rg_x, v_b_rg_x, v_lru_lambda, v_q_norm, v_k_norm, v_w_o_rnn, v_w_o_att, v_w_out, v_norm_ple, v_w_ple_gate, v_b_ple_gate, v_w_ple):
    w = dict(norm_mix=norm_mix, w_in=w_in, b_in=b_in, conv_w=conv_w, conv_b=conv_b, w_rg_a=w_rg_a, b_rg_a=b_rg_a,
             w_rg_x=w_rg_x, b_rg_x=b_rg_x, lru_lambda=lru_lambda, q_norm=q_norm, k_norm=k_norm, w_o_rnn=w_o_rnn,
             w_o_att=w_o_att, w_out=w_out, norm_ple=norm_ple, w_ple_gate=w_ple_gate, b_ple_gate=b_ple_gate,
             w_ple=w_ple)
    m = dict(norm_mix=m_norm_mix, w_in=m_w_in, b_in=m_b_in, conv_w=m_conv_w, conv_b=m_conv_b, w_rg_a=m_w_rg_a,
             b_rg_a=m_b_rg_a, w_rg_x=m_w_rg_x, b_rg_x=m_b_rg_x, lru_lambda=m_lru_lambda, q_norm=m_q_norm,
             k_norm=m_k_norm, w_o_rnn=m_w_o_rnn, w_o_att=m_w_o_att, w_out=m_w_out, norm_ple=m_norm_ple,
             w_ple_gate=m_w_ple_gate, b_ple_gate=m_b_ple_gate, w_ple=m_w_ple)
    v = dict(norm_mix=v_norm_mix, w_in=v_w_in, b_in=v_b_in, conv_w=v_conv_w, conv_b=v_conv_b, w_rg_a=v_w_rg_a,
             b_rg_a=v_b_rg_a, w_rg_x=v_w_rg_x, b_rg_x=v_b_rg_x, lru_lambda=v_lru_lambda, q_norm=v_q_norm,
             k_norm=v_k_norm, w_o_rnn=v_w_o_rnn, w_o_att=v_w_o_att, w_out=v_w_out, norm_ple=v_norm_ple,
             w_ple_gate=v_w_ple_gate, b_ple_gate=v_b_ple_gate, w_ple=v_w_ple)
    names = list(w.keys())

    shards = [w_in[0].T.astype(BF16), w_o_rnn[0].astype(BF16), w_o_att[0].T.astype(BF16), w_out[0].astype(BF16),
              w_ple_gate[0].astype(BF16), w_ple[0].T.astype(BF16), conv_w[0]]
    gat = _all_gather_two_level(shards, "gather_weights")
    w_in_t, w_o_rnn_f, w_o_att_t, w_out_f, w_pg_f, w_ple_t = [
        a.reshape((NDEV * a.shape[1], a.shape[2])) for a in gat[:6]]
    conv_f = gat[6].transpose(1, 0, 2).reshape(CONVW, DR)

    def start_reduce(arrs, tag):
        parts = [a.reshape((NCHIP, 2, a.shape[0] // NDEV, a.shape[1])) for a in arrs]
        theirs = _exchange_within_chip(parts, "reduce_within_chip_" + tag)
        return _between_chips_start(_sum_pairs(parts, theirs, "sum_pairs_" + tag), "reduce_between_chips_start_" + tag)

    loss_part, grad_x, pending_out, pending_in, small = _local_step(
        x.reshape(T, D), p.reshape(T, PLE), loss_target.reshape(T, D),
        w_in_t, w_o_rnn_f, w_o_att_t, w_out_f, w_pg_f, w_ple_t, conv_f,
        norm_mix, b_in, conv_b, w_rg_a[0], b_rg_a, w_rg_x[0], b_rg_x, lru_lambda, q_norm[0], k_norm[0],
        norm_ple, b_ple_gate, start_reduce)
    loss = lax.psum(loss_part, ("x", "y", "c"))

    me = _lin(_my_pos())
    rep_parts = _pack_small_grads(small).reshape(NDEV, REP_ROWS_DEV, LANES)
    conv_parts = small["conv_w"].reshape(CONVW, NDEV, DR // NDEV).transpose(1, 0, 2)
    small_parts = [rep_parts, conv_parts]
    g_rep, g_conv = _sum_blocks_small(small_parts, _exchange_all(small_parts, "reduce_small"), me, (False, False),
                                      "sum_small")
    rep_all = _all_gather_direct(g_rep, "gather_small").reshape(NDEV * REP_ROWS_DEV, LANES)

    myq = _chip(_my_pos())
    own_out, recv_out = _between_chips_wait(pending_out, rep_all, "reduce_between_chips_wait_out")
    own_in, recv_in = _between_chips_wait(pending_in, rep_all, "reduce_between_chips_wait_in")
    w_in_res = _sum_chips_adamw(own_in[0], recv_in[0], w_in[0].T, m_w_in[0].T, v_w_in[0].T, 304, "adamw_w_in")
    g_o_rnn, g_o_att, g_out, g_pg, g_ple = _sum_blocks_small(
        own_out, recv_out, myq, (False, True, False, False, True), "sum_chips_out")

    grad, delta, new_m, new_v = {}, {}, {}, {}
    rep_shape = lambda a: a if a.ndim == 2 else a.reshape(a.shape[1:])
    res = _adamw_small(rep_all, [rep_shape(w[n]) for n in REP_NAMES], [rep_shape(m[n]) for n in REP_NAMES],
                       [rep_shape(v[n]) for n in REP_NAMES])
    for dst, vals in zip((grad, delta, new_m, new_v), res):
        for n, a in zip(REP_NAMES, vals):
            dst[n] = a.reshape(w[n].shape)
    grad["w_in"], delta["w_in"], new_m["w_in"], new_v["w_in"] = [a.T[None] for a in w_in_res]
    rest = ("w_o_rnn", "w_o_att", "w_out", "w_ple_gate", "w_ple", "conv_w")
    g_rest = [g_o_rnn, g_o_att, g_out, g_pg, g_ple, g_conv]
    res = _adamw_many([w[n][0] for n in rest], g_rest, [m[n][0] for n in rest], [v[n][0] for n in rest])
    for n, a in zip(rest, g_rest):
        grad[n] = a[None]
    for dst, vals in zip((delta, new_m, new_v), res):
        for n, a in zip(rest, vals):
            dst[n] = a[None]

    return (loss, grad_x.reshape(BL, S, D), *[grad[n] for n in names], *[delta[n] for n in names],
            *[new_m[n] for n in names], *[new_v[n] for n in names])
```

```python
import jax
import jax.numpy as jnp
from jax import lax
from jax.experimental import pallas as pl
from jax.experimental.pallas import tpu as pltpu

F32 = jnp.float32
BF16 = jnp.bfloat16

D = 1024
S = 2048
BL = 2
T = BL * S
NDEV = 8
NCHIP = 4
PLE = 256
DR = 1280
NRB = 10
RBW = 128
CONVW = 4
LRU_C = 8.0
HD = 128
NH = 4
PATTERNS = ((128, 1), (512, 4), (2048, 16))
NG = 3
ATT = NH * HD
GW = NG * ATT
NIN = 2 * DR + 3 * GW + ATT + 2 * D
OFF_ZR = DR
OFF_Q = 2 * DR
OFF_ZA = OFF_Q + 3 * GW
OFF_G = OFF_ZA + ATT
ROPE_THETA = 10000.0
EPS = 1e-6
SCALE = HD ** -0.5
NEG = -1e30
QB = 128
LANES = 128
CT = 512
NCT = NIN // CT
A_W = 2 * DR
C_W = ATT + 2 * D

LR, B1, B2, AEPS, WD, STEP = 0.001, 0.9, 0.999, 1e-08, 0.01, 10

NSHARD_IN = NIN // NDEV
REP_NAMES = ("w_rg_a", "w_rg_x", "norm_mix", "b_in", "conv_b", "b_rg_a", "b_rg_x", "lru_lambda", "q_norm",
             "k_norm", "norm_ple", "b_ple_gate")
REP_ROWS = (NRB * RBW, NRB * RBW, D // LANES, NIN // LANES, DR // LANES, DR // LANES, DR // LANES, DR // LANES,
            NG, NG, D // LANES, D // LANES)
REP_TOTAL_ROWS = sum(REP_ROWS)
REP_ROWS_DEV = 344
BIG_NAMES = ("w_in", "w_o_rnn", "w_o_att", "w_out", "w_ple_gate", "w_ple")

VMEM_BIG = 56 * 1024 * 1024
VMEM_MID = 40 * 1024 * 1024


def _cp(sem=None, vmem=None):
    return pltpu.CompilerParams(dimension_semantics=sem, vmem_limit_bytes=vmem)


def _dot(a, b):
    return jnp.dot(a, b, preferred_element_type=F32)


def _dot_nt(a, b):
    return lax.dot_general(a, b, (((1,), (1,)), ((), ())), preferred_element_type=F32)


def _dot_tn(a, b):
    return lax.dot_general(a, b, (((0,), (0,)), ((), ())), preferred_element_type=F32)


def _sigmoid(x):
    return jax.nn.sigmoid(x)


def _perm(j):
    jq = j - OFF_Q // CT
    inside = (j >= OFF_Q // CT) & (j < OFF_ZA // CT)
    return jnp.where(inside, OFF_Q // CT + (jq % 3) * 3 + jq // 3, j)


PIECES = ((0, A_W // CT), (OFF_Q // CT, GW // CT), (OFF_Q // CT + 3, GW // CT), (OFF_Q // CT + 6, GW // CT),
          (OFF_ZA // CT, C_W // CT))


def _rmsnorm_fwd(x, gain, token, tm=512):
    def body(x_ref, g_ref, _token, o_ref):
        xv = x_ref[...]
        var = jnp.mean(xv * xv, axis=-1, keepdims=True)
        o_ref[...] = (xv * lax.rsqrt(var + EPS) * g_ref[...]).astype(BF16)

    return pl.pallas_call(
        body, grid=(T // tm,), name="rmsnorm_fwd",
        in_specs=[pl.BlockSpec((tm, D), lambda i: (i, 0)), pl.BlockSpec((1, D), lambda i: (0, 0)),
                  pl.BlockSpec((8, LANES), lambda i: (0, 0))],
        out_specs=pl.BlockSpec((tm, D), lambda i: (i, 0)),
        out_shape=jax.ShapeDtypeStruct((T, D), BF16),
        compiler_params=_cp(("parallel",)),
    )(x, gain, token)


def _in_proj(hn, w_t, bias, tm=1024):
    def body(a_ref, w_ref, b_ref, o_ref):
        o_ref[...] = _dot_nt(a_ref[...], w_ref[...]) + b_ref[...]

    return pl.pallas_call(
        body, grid=(T // tm, NCT), name="in_proj",
        in_specs=[pl.BlockSpec((tm, D), lambda i, j: (i, 0)),
                  pl.BlockSpec((CT, D), lambda i, j: (_perm(j), 0)),
                  pl.BlockSpec((1, CT), lambda i, j: (0, _perm(j)))],
        out_specs=pl.BlockSpec((tm, CT), lambda i, j: (i, j)),
        out_shape=jax.ShapeDtypeStruct((T, NIN), F32),
        compiler_params=_cp(("parallel", "parallel"), VMEM_MID),
    )(hn, w_t, bias)


def _dhn(pieces, w_t, token, tm=512):
    def body(a_ref, q_ref, k_ref, v_ref, c_ref, w_hbm, _token, o_ref, w):
        @pl.when(pl.program_id(0) == 0)
        def _():
            pltpu.sync_copy(w_hbm, w)

        acc = _dot(a_ref[...], w[pl.ds(0, A_W), :])
        for kind, x_ref in enumerate((q_ref, k_ref, v_ref)):
            for g in range(NG):
                row = OFF_Q + (3 * g + kind) * CT
                acc = acc + _dot(x_ref[:, g * CT:(g + 1) * CT], w[pl.ds(row, CT), :])
        o_ref[...] = acc + _dot(c_ref[...], w[pl.ds(OFF_ZA, C_W), :])

    tok = lambda wd: pl.BlockSpec((tm, wd), lambda i: (i, 0))
    return pl.pallas_call(
        body, grid=(T // tm,), name="dhn",
        in_specs=[tok(A_W), tok(GW), tok(GW), tok(GW), tok(C_W), pl.BlockSpec(memory_space=pl.ANY),
                  pl.BlockSpec((8, LANES), lambda i: (0, 0))],
        out_specs=tok(D),
        out_shape=jax.ShapeDtypeStruct((T, D), F32),
        scratch_shapes=[pltpu.VMEM((NIN, D), BF16)],
        compiler_params=_cp(("arbitrary",), VMEM_BIG),
    )(*pieces, w_t, token)


def _dw_in(pieces, hn, tt=2048):
    nt = T // tt

    def body(a_ref, q_ref, k_ref, v_ref, c_ref, h_ref, o_ref, s_ref, acc, cs):
        j = pl.program_id(0)
        t = pl.program_id(1)

        def step(x_ref):
            xv = x_ref[...]
            p = _dot_tn(xv, h_ref[...])
            c = jnp.sum(xv.astype(F32), axis=0, keepdims=True)

            @pl.when(t == 0)
            def _():
                acc[...] = p
                cs[...] = c

            @pl.when(t > 0)
            def _():
                acc[...] += p
                cs[...] += c

        for x_ref, (lo, n) in zip((a_ref, q_ref, k_ref, v_ref, c_ref), PIECES):
            pl.when((j >= lo) & (j < lo + n))(lambda x_ref=x_ref: step(x_ref))

        @pl.when(t == nt - 1)
        def _():
            o_ref[...] = acc[...].astype(BF16)
            s_ref[...] = cs[...]

    def piece_spec(lo, n):
        def imap(j, t):
            used = (j >= lo) & (j < lo + n)
            return (jnp.where(used, t, 0), jnp.clip(j - lo, 0, n - 1))
        return pl.BlockSpec((tt, CT), imap)

    return pl.pallas_call(
        body, grid=(NCT, nt), name="dw_in",
        in_specs=[piece_spec(lo, n) for lo, n in PIECES] + [pl.BlockSpec((tt, D), lambda j, t: (t, 0))],
        out_specs=[pl.BlockSpec((CT, D), lambda j, t: (_perm(j), 0)), pl.BlockSpec((1, CT), lambda j, t: (0, _perm(j)))],
        out_shape=[jax.ShapeDtypeStruct((NIN, D), BF16), jax.ShapeDtypeStruct((1, NIN), F32)],
        scratch_shapes=[pltpu.VMEM((CT, D), F32), pltpu.VMEM((1, CT), F32)],
        compiler_params=_cp(("parallel", "arbitrary"), VMEM_MID),
    )(*pieces, hn)


def _mm_tn(a, b, ta, tt, name):
    m = a.shape[1]
    n = b.shape[1]
    nt = T // tt

    def body(a_ref, b_ref, o_ref, acc):
        t = pl.program_id(1)
        p = _dot_tn(a_ref[...].astype(BF16), b_ref[...].astype(BF16))

        @pl.when(t == 0)
        def _():
            acc[...] = p

        @pl.when(t > 0)
        def _():
            acc[...] += p

        @pl.when(t == nt - 1)
        def _():
            o_ref[...] = acc[...].astype(BF16)

    return pl.pallas_call(
        body, grid=(m // ta, nt), name=name,
        in_specs=[pl.BlockSpec((tt, ta), lambda j, t: (t, j)), pl.BlockSpec((tt, n), lambda j, t: (t, 0))],
        out_specs=pl.BlockSpec((ta, n), lambda j, t: (j, 0)),
        out_shape=jax.ShapeDtypeStruct((m, n), BF16),
        scratch_shapes=[pltpu.VMEM((ta, n), F32)],
        compiler_params=_cp(("parallel", "arbitrary"), VMEM_MID),
    )(a, b)


def _row_iota():
    return lax.broadcasted_iota(jnp.int32, (S, RBW), 0)


def _shift_down(v, d, row, fill):
    return jnp.where(row >= d, pltpu.roll(v, d, 0), fill)


def _shift_up(v, d, row, fill):
    return jnp.where(row < S - d, pltpu.roll(v, S - d, 0), fill)


SUBLANES = 8


def _scan_down(a, u, row):
    d = 1
    while d < S:
        last = 2 * d >= S
        if d < SUBLANES:
            u = a * _shift_down(u, d, row, 0.0) + u
            if not last:
                a = a * _shift_down(a, d, row, 1.0)
        else:
            u = jnp.concatenate([u[:d], a[d:] * u[:S - d] + u[d:]], axis=0)
            if not last:
                a = jnp.concatenate([a[:d], a[d:] * a[:S - d]], axis=0)
        d *= 2
    return u


def _scan_up(b, g, row):
    d = 1
    while d < S:
        last = 2 * d >= S
        if d < SUBLANES:
            g = g + b * _shift_up(g, d, row, 0.0)
            if not last:
                b = b * _shift_up(b, d, row, 0.0)
        else:
            g = jnp.concatenate([g[:S - d] + b[:S - d] * g[d:], g[S - d:]], axis=0)
            if not last:
                b = jnp.concatenate([b[:S - d] * b[d:], b[S - d:]], axis=0)
        d *= 2
    return g


def _neg_expm1(x):
    series = -x * (1.0 + x * (0.5 + x * (1.0 / 6.0 + x * (1.0 / 24.0))))
    return jnp.where(x > -0.03, series, 1.0 - jnp.exp(x))


def _softplus(x):
    return jnp.maximum(x, 0.0) + jnp.log1p(jnp.exp(-jnp.abs(x)))


def _rnn_gates(x, cw, cb, wa, ba, wx, bx, lam, row):
    xc = cb + cw[3:4, :] * x
    for j in (1, 2, 3):
        xc = xc + cw[3 - j:4 - j, :] * _shift_down(x, j, row, 0.0)
    xcb = xc.astype(BF16)
    r = _sigmoid(_dot(xcb, wa) + ba)
    i = _sigmoid(_dot(xcb, wx) + bx)
    sp = _softplus(-lam)
    log_a = (-LRU_C) * r * sp
    a = jnp.exp(log_a)
    mult = jnp.where(row == 0, 1.0, jnp.sqrt(_neg_expm1(2.0 * log_a)))
    return xc, xcb, r, i, sp, a, mult


def _rnn_fwd(proj3, conv_w, conv_b, wa, ba, wx, bx, lam):
    def body(x_ref, cw_ref, cb_ref, wa_ref, ba_ref, wx_ref, bx_ref, lam_ref, h_ref):
        row = _row_iota()
        x = x_ref[0]
        xc, _, _, i, _, a, mult = _rnn_gates(x, cw_ref[...], cb_ref[...], wa_ref[0], ba_ref[...],
                                             wx_ref[0], bx_ref[...], lam_ref[...], row)
        h_ref[0] = _scan_down(a, mult * (i * xc), row)

    vec = lambda: pl.BlockSpec((1, RBW), lambda b, n: (0, n))
    mat = lambda: pl.BlockSpec((1, RBW, RBW), lambda b, n: (n, 0, 0))
    return pl.pallas_call(
        body, grid=(BL, NRB), name="rnn_fwd",
        in_specs=[pl.BlockSpec((1, S, RBW), lambda b, n: (b, 0, n)),
                  pl.BlockSpec((CONVW, RBW), lambda b, n: (0, n)),
                  vec(), mat(), vec(), mat(), vec(), vec()],
        out_specs=pl.BlockSpec((1, S, RBW), lambda b, n: (b, 0, n)),
        out_shape=jax.ShapeDtypeStruct((BL, S, DR), F32),
        compiler_params=_cp(("parallel", "parallel"), VMEM_MID),
    )(proj3, conv_w, conv_b, wa, ba, wx, bx, lam)


def _rnn_bwd(proj3, h3, dh3, slab_a3, conv_w, conv_b, wa, ba, wx, bx, lam, token):
    def body(x_ref, h_ref, dh_ref, cw_ref, cb_ref, wa_ref, ba_ref, wx_ref, bx_ref, lam_ref, _alias, _token,
             dx_ref, dcw_ref, dcb_ref, dwa_ref, dba_ref, dwx_ref, dbx_ref, dlam_ref):
        row = _row_iota()
        x = x_ref[0]
        cw = cw_ref[...]
        wa_v = wa_ref[0]
        wx_v = wx_ref[0]
        lam_v = lam_ref[...]
        xc, xcb, r, i, sp, a, mult = _rnn_gates(x, cw, cb_ref[...], wa_v, ba_ref[...], wx_v, bx_ref[...], lam_v, row)
        h = h_ref[0]
        g = _scan_up(_shift_up(a, 1, row, 0.0), dh_ref[0], row)
        da = g * _shift_down(h, 1, row, 0.0)
        dmult = jnp.where(row == 0, 0.0, g * (i * xc))
        gm = g * mult
        di = gm * xc
        dxc = gm * i
        dlog_a = da * a - dmult * (a * a) / mult
        dr = dlog_a * ((-LRU_C) * sp)
        dsp = jnp.sum(dlog_a * ((-LRU_C) * r), axis=0, keepdims=True)
        dlam = dsp * (-_sigmoid(-lam_v))
        dpa = dr * r * (1.0 - r)
        dpx = di * i * (1.0 - i)
        dpab = dpa.astype(BF16)
        dpxb = dpx.astype(BF16)
        dwa = _dot_tn(xcb, dpab)
        dwx = _dot_tn(xcb, dpxb)
        dba = jnp.sum(dpa, axis=0, keepdims=True)
        dbx = jnp.sum(dpx, axis=0, keepdims=True)
        dxc = dxc + _dot_nt(dpab, wa_v) + _dot_nt(dpxb, wx_v)
        dcb = jnp.sum(dxc, axis=0, keepdims=True)
        dx = cw[3:4, :] * dxc
        dcw_rows = [None] * CONVW
        dcw_rows[3] = jnp.sum(dxc * x, axis=0, keepdims=True)
        for j in (1, 2, 3):
            dx = dx + cw[3 - j:4 - j, :] * _shift_up(dxc, j, row, 0.0)
            dcw_rows[3 - j] = jnp.sum(dxc * _shift_down(x, j, row, 0.0), axis=0, keepdims=True)
        dx_ref[0] = dx.astype(BF16)
        dcw = jnp.concatenate(dcw_rows, axis=0)
        first = pl.program_id(1) == 0

        @pl.when(first)
        def _():
            dcw_ref[...] = dcw
            dcb_ref[...] = dcb
            dwa_ref[0] = dwa
            dba_ref[...] = dba
            dwx_ref[0] = dwx
            dbx_ref[...] = dbx
            dlam_ref[...] = dlam

        @pl.when(jnp.logical_not(first))
        def _():
            dcw_ref[...] += dcw
            dcb_ref[...] += dcb
            dwa_ref[0] += dwa
            dba_ref[...] += dba
            dwx_ref[0] += dwx
            dbx_ref[...] += dbx
            dlam_ref[...] += dlam

    slab = lambda: pl.BlockSpec((1, S, RBW), lambda n, b: (b, 0, n))
    vec = lambda: pl.BlockSpec((1, RBW), lambda n, b: (0, n))
    mat = lambda: pl.BlockSpec((1, RBW, RBW), lambda n, b: (n, 0, 0))
    taps = lambda: pl.BlockSpec((CONVW, RBW), lambda n, b: (0, n))
    vshape = jax.ShapeDtypeStruct((1, DR), F32)
    mshape = jax.ShapeDtypeStruct((NRB, RBW, RBW), F32)
    return pl.pallas_call(
        body, grid=(NRB, BL), name="rnn_bwd",
        in_specs=[slab(), slab(), slab(), taps(), vec(), mat(), vec(), mat(), vec(), vec(),
                  pl.BlockSpec(memory_space=pl.ANY), pl.BlockSpec((8, LANES), lambda n, b: (0, 0))],
        out_specs=[slab(), taps(), vec(), mat(), vec(), mat(), vec(), vec()],
        out_shape=[jax.ShapeDtypeStruct((BL, S, A_W), BF16), jax.ShapeDtypeStruct((CONVW, DR), F32),
                   vshape, mshape, vshape, mshape, vshape, vshape],
        input_output_aliases={10: 0},
        compiler_params=_cp(("parallel", "arbitrary"), 48 * 1024 * 1024),
    )(proj3, h3, dh3, conv_w, conv_b, wa, ba, wx, bx, lam, slab_a3, token)


NQB = S // QB


def _rms_head(t, gain):
    rstd = lax.rsqrt(jnp.mean(t * t, axis=-1, keepdims=True) + EPS)
    return t * rstd * gain


def _rope(t, cs, sn):
    return t * cs + pltpu.roll(t, HD // 2, 1) * sn


def _rope_t(dy, cs, sn):
    return dy * cs - pltpu.roll(dy, HD // 2, 1) * sn


def _bdot_nt(a, b):
    return lax.dot_general(a, b, (((2,), (2,)), ((0,), (0,))), preferred_element_type=F32)


def _bdot(a, b):
    return lax.dot_general(a, b, (((2,), (1,)), ((0,), (0,))), preferred_element_type=F32)


def _bdot_tn(a, b):
    return lax.dot_general(a, b, (((1,), (1,)), ((0,), (0,))), preferred_element_type=F32)


STRIDE_MAX = 4


def _permute(buf, x, dil, dst, off=0):
    ln = S // dil
    if dil == 1:
        dst[pl.ds(off, S), :] = x.astype(dst.dtype)
        return
    buf[0] = x
    if dil <= STRIDE_MAX:
        for c in range(dil):
            dst[pl.ds(off + c * ln, ln), :] = buf.at[0][pl.ds(c, ln, stride=dil), :].astype(dst.dtype)
        return
    f, r = STRIDE_MAX, dil // STRIDE_MAX
    part = S // f
    for c1 in range(f):
        buf.at[1][pl.ds(c1 * part, part), :] = buf.at[0][pl.ds(c1, part, stride=f), :]
    for c1 in range(f):
        for c2 in range(r):
            dst[pl.ds(off + (c1 + f * c2) * ln, ln), :] = (
                buf.at[1][pl.ds(c1 * part + c2, ln, stride=r), :].astype(dst.dtype))


def _unpermute(buf, xp, dil, dst):
    ln = S // dil
    if dil == 1:
        dst[...] = xp
        return
    if dil <= STRIDE_MAX:
        for c in range(dil):
            dst[pl.ds(c, ln, stride=dil), :] = xp[c * ln:(c + 1) * ln]
        return
    f, r = STRIDE_MAX, dil // STRIDE_MAX
    part = S // f
    for c1 in range(f):
        for c2 in range(r):
            c = c1 + f * c2
            buf.at[1][pl.ds(c1 * part + c2, ln, stride=r), :] = xp[c * ln:(c + 1) * ln]
    for c1 in range(f):
        dst[pl.ds(c1, part, stride=f), :] = buf[1, pl.ds(c1 * part, part), :]


def _blocks3(ref, off=0):
    return ref[pl.ds(off, S), :].reshape(NQB, QB, HD)


def _att_prep(q_ref, k_ref, v_ref, cos_ref, sin_ref, qn, kn, dil, nat, qs, ksp, vsp):
    cs = cos_ref[...]
    sn = sin_ref[...]
    zero = jnp.zeros((QB, HD), BF16)
    ksp[pl.ds(0, QB), :] = zero
    vsp[pl.ds(0, QB), :] = zero
    _permute(nat, _rope(_rms_head(q_ref[0], qn), cs, sn), dil, qs)
    _permute(nat, _rope(_rms_head(k_ref[0], kn), cs, sn), dil, ksp, QB)
    _permute(nat, v_ref[0], dil, vsp, QB)


def _att_scores(qs, ksp, dil):
    nb = S // dil // QB
    q3 = _blocks3(qs)
    shape = (NQB, QB, QB)
    qi = lax.broadcasted_iota(jnp.int32, shape, 1)
    kj = lax.broadcasted_iota(jnp.int32, shape, 2)
    s_c = jnp.where(qi >= kj, _bdot_nt(q3, _blocks3(ksp, QB)) * SCALE, NEG)
    if nb == 1:
        return q3, s_c, None
    jj = lax.broadcasted_iota(jnp.int32, shape, 0)
    ok = (kj >= qi) & ((jj & (nb - 1)) != 0)
    s_p = jnp.where(ok, _bdot_nt(q3, _blocks3(ksp)) * SCALE, NEG)
    return q3, s_c, s_p


def _qkv_spec(kind, g):
    base = OFF_Q // HD + kind * (GW // HD) + g * NH
    return pl.BlockSpec((1, S, HD), lambda b, h: (b, 0, base + h))


def _attn_fwd(proj3, cos_t, sin_t, q_norm, k_norm):
    def body(*refs):
        qkv_refs = refs[:9]
        cos_ref, sin_ref, qn_ref, kn_ref, att_ref, lse_ref, w_ref, nat, qs, ksp, vsp, og = refs[9:]
        for g, (window, dil) in enumerate(PATTERNS):
            q_ref, k_ref, v_ref = qkv_refs[3 * g:3 * g + 3]
            _att_prep(q_ref, k_ref, v_ref, cos_ref, sin_ref, qn_ref[g:g + 1, :], kn_ref[g:g + 1, :], dil,
                      nat, qs, ksp, vsp)
            _, s_c, s_p = _att_scores(qs, ksp, dil)
            m = jnp.max(s_c, axis=-1, keepdims=True)
            if s_p is not None:
                m = jnp.maximum(m, jnp.max(s_p, axis=-1, keepdims=True))
            e_c = jnp.exp(s_c - m)
            den = jnp.sum(e_c, axis=-1, keepdims=True)
            o = _bdot(e_c.astype(BF16), _blocks3(vsp, QB))
            if s_p is not None:
                e_p = jnp.exp(s_p - m)
                den = den + jnp.sum(e_p, axis=-1, keepdims=True)
                o = o + _bdot(e_p.astype(BF16), _blocks3(vsp))
            _unpermute(nat, (o / den).reshape(S, HD), dil, og.at[g])
            _unpermute(nat, jnp.broadcast_to(m + jnp.log(den), (NQB, QB, HD)).reshape(S, HD), dil,
                       lse_ref.at[g, 0])
        l0 = lse_ref[0, 0]
        l1 = lse_ref[1, 0]
        l2 = lse_ref[2, 0]
        mx = jnp.maximum(jnp.maximum(l0, l1), l2)
        e0 = jnp.exp(l0 - mx)
        e1 = jnp.exp(l1 - mx)
        e2 = jnp.exp(l2 - mx)
        inv = 1.0 / (e0 + e1 + e2)
        w0 = e0 * inv
        w1 = e1 * inv
        w2 = e2 * inv
        w_ref[0, 0] = w0
        w_ref[1, 0] = w1
        w_ref[2, 0] = w2
        att_ref[0] = w0 * og[0] + w1 * og[1] + w2 * og[2]

    in_specs = [_qkv_spec(kind, g) for g in range(NG) for kind in range(3)]
    in_specs += [pl.BlockSpec((S, HD), lambda b, h: (0, 0)), pl.BlockSpec((S, HD), lambda b, h: (0, 0)),
                 pl.BlockSpec((NG, HD), lambda b, h: (0, 0)), pl.BlockSpec((NG, HD), lambda b, h: (0, 0))]
    stat = lambda: pl.BlockSpec((NG, 1, S, HD), lambda b, h: (0, b, 0, h))
    return pl.pallas_call(
        body, grid=(BL, NH), name="attn_fwd",
        in_specs=in_specs,
        out_specs=[pl.BlockSpec((1, S, HD), lambda b, h: (b, 0, h)), stat(), stat()],
        out_shape=[jax.ShapeDtypeStruct((BL, S, ATT), F32),
                   jax.ShapeDtypeStruct((NG, BL, S, ATT), F32),
                   jax.ShapeDtypeStruct((NG, BL, S, ATT), F32)],
        scratch_shapes=[pltpu.VMEM((2, S, HD), F32), pltpu.VMEM((S, HD), BF16), pltpu.VMEM((S + QB, HD), BF16),
                        pltpu.VMEM((S + QB, HD), BF16), pltpu.VMEM((NG, S, HD), F32)],
        compiler_params=_cp(("parallel", "parallel"), VMEM_BIG),
    )(*([proj3] * 9), cos_t, sin_t, q_norm, k_norm)


def _attn_bwd_group(g, proj3, cos_t, sin_t, qn_g, kn_g, lse, wts, datt3, sbar3, slabs):
    dil = PATTERNS[g][1]
    n_alias = 0 if slabs is None else 3

    def norm_rope_bwd(dpost, raw, gain, cs, sn):
        dn = _rope_t(dpost, cs, sn)
        rstd = lax.rsqrt(jnp.mean(raw * raw, axis=-1, keepdims=True) + EPS)
        xh = raw * rstd
        dgain = jnp.sum(dn * xh, axis=0, keepdims=True)
        gd = dn * gain
        draw = rstd * (gd - xh * jnp.mean(gd * xh, axis=-1, keepdims=True))
        return draw, dgain

    def body(*refs):
        (q_ref, k_ref, v_ref, cos_ref, sin_ref, qn_ref, kn_ref, lse_ref, w_ref, datt_ref, sbar_ref) = refs[:11]
        (dq_ref, dk_ref, dv_ref, dqn_ref, dkn_ref, nat, qs, ksp, vsp, dos, cvp, lsp, acc) = refs[11 + n_alias:]
        qn = qn_ref[...]
        kn = kn_ref[...]
        cs = cos_ref[...]
        sn = sin_ref[...]
        _att_prep(q_ref, k_ref, v_ref, cos_ref, sin_ref, qn, kn, dil, nat, qs, ksp, vsp)
        wv = w_ref[0, 0]
        _permute(nat, wv * datt_ref[0], dil, dos)
        _permute(nat, wv * sbar_ref[0], dil, cvp)
        _permute(nat, lse_ref[0, 0], dil, lsp)
        q3, s_c, s_p = _att_scores(qs, ksp, dil)
        do3 = _blocks3(dos)
        lse3 = _blocks3(lsp)[:, :, 0:1]
        cv3 = _blocks3(cvp)[:, :, 0:1]
        p_c = jnp.exp(s_c - lse3)
        ds_c = (p_c * (_bdot_nt(do3, _blocks3(vsp, QB)) - cv3)).astype(BF16)
        dq = _bdot(ds_c, _blocks3(ksp, QB))
        acc[0] = _bdot_tn(ds_c, q3).reshape(S, HD)
        acc[1] = _bdot_tn(p_c.astype(BF16), do3).reshape(S, HD)
        if s_p is not None:
            p_p = jnp.exp(s_p - lse3)
            ds_p = (p_p * (_bdot_nt(do3, _blocks3(vsp)) - cv3)).astype(BF16)
            dq = dq + _bdot(ds_p, _blocks3(ksp))
            early = pl.ds(0, S - QB)
            acc[0, early, :] += _bdot_tn(ds_p, q3).reshape(S, HD)[QB:]
            acc[1, early, :] += _bdot_tn(p_p.astype(BF16), do3).reshape(S, HD)[QB:]
        _unpermute(nat, (dq * SCALE).reshape(S, HD), dil, nat.at[0])
        draw, dqn = norm_rope_bwd(nat[0], q_ref[0], qn, cs, sn)
        dq_ref[0] = draw.astype(BF16)
        _unpermute(nat, acc[0] * SCALE, dil, nat.at[0])
        draw, dkn = norm_rope_bwd(nat[0], k_ref[0], kn, cs, sn)
        dk_ref[0] = draw.astype(BF16)
        _unpermute(nat, acc[1], dil, nat.at[0])
        dv_ref[0] = nat[0].astype(BF16)
        first = (pl.program_id(0) == 0) & (pl.program_id(1) == 0)

        @pl.when(first)
        def _():
            dqn_ref[...] = dqn
            dkn_ref[...] = dkn

        @pl.when(jnp.logical_not(first))
        def _():
            dqn_ref[...] += dqn
            dkn_ref[...] += dkn

    full = lambda r: pl.BlockSpec((r, HD), lambda b, h: (0, 0))
    stat = lambda: pl.BlockSpec((1, 1, S, HD), lambda b, h: (g, b, 0, h))
    slab = lambda: pl.BlockSpec((1, S, HD), lambda b, h: (b, 0, h))
    out_slab = lambda: pl.BlockSpec((1, S, HD), lambda b, h: (b, 0, g * NH + h))
    big = jax.ShapeDtypeStruct((BL, S, GW), BF16)
    vecs = jax.ShapeDtypeStruct((1, HD), F32)
    in_specs = [_qkv_spec(0, g), _qkv_spec(1, g), _qkv_spec(2, g), full(S), full(S), full(1), full(1),
                stat(), stat(), slab(), slab()]
    args = [proj3, proj3, proj3, cos_t, sin_t, qn_g, kn_g, lse, wts, datt3, sbar3]
    aliases = {}
    if slabs is not None:
        in_specs += [pl.BlockSpec(memory_space=pl.ANY)] * 3
        args += list(slabs)
        aliases = {11: 0, 12: 1, 13: 2}
    return pl.pallas_call(
        body, grid=(BL, NH), name="attn_bwd_g%d" % g,
        in_specs=in_specs,
        out_specs=[out_slab(), out_slab(), out_slab(), full(1), full(1)],
        out_shape=[big, big, big, vecs, vecs],
        scratch_shapes=[pltpu.VMEM((2, S, HD), F32), pltpu.VMEM((S, HD), BF16), pltpu.VMEM((S + QB, HD), BF16),
                        pltpu.VMEM((S + QB, HD), BF16), pltpu.VMEM((S, HD), BF16), pltpu.VMEM((S, HD), F32),
                        pltpu.VMEM((S, HD), F32), pltpu.VMEM((2, S, HD), F32)],
        input_output_aliases=aliases,
        compiler_params=_cp(("arbitrary", "arbitrary"), VMEM_BIG),
    )(*args)


def _tail(x, proj, h, att, p, tgt, w_o_rnn, w_o_att_t, w_out, w_pg, w_ple_t, norm_ple, b_pg, tm=256):
    nt = T // tm
    inv_d = 1.0 / D

    def body(x_ref, h_ref, zr_ref, att_ref, za_ref, g0a_ref, g0b_ref, g1a_ref, g1b_ref, p_ref, tgt_ref,
             np_ref, bpg_ref, wor_hbm, woa_hbm, wout_hbm, wpg_hbm, wple_hbm,
             dx1_ref, merged_ref, n1_ref, dpre_ref, dpe_ref, dyr_ref, dya_ref, slab_a_ref, slab_c_ref, dh_ref,
             datt_ref, sbar_ref, yrnn_ref, yatt_ref, loss_ref, dnp_ref, dbpg_ref,
             wor, woa, wout, wpg, wple):
        first = pl.program_id(0) == 0

        @pl.when(first)
        def _():
            pltpu.sync_copy(wor_hbm, wor)
            pltpu.sync_copy(woa_hbm, woa)
            pltpu.sync_copy(wout_hbm, wout)
            pltpu.sync_copy(wpg_hbm, wpg)
            pltpu.sync_copy(wple_hbm, wple)

        xv = x_ref[...]
        hv = h_ref[...]
        zr = zr_ref[...]
        av = att_ref[...]
        za = za_ref[...]
        szr = _sigmoid(zr)
        silu_r = zr * szr
        yrnn_b = (hv * silu_r).astype(BF16)
        sza = _sigmoid(za)
        silu_a = za * sza
        yatt_b = (av * silu_a).astype(BF16)
        yrnn_ref[...] = yrnn_b
        yatt_ref[...] = yatt_b
        yr = _dot(yrnn_b, wor[...])
        ya = _dot_nt(yatt_b, woa[...])
        g0 = _sigmoid(jnp.concatenate([g0a_ref[...], g0b_ref[...]], axis=1))
        g1 = _sigmoid(jnp.concatenate([g1a_ref[...], g1b_ref[...]], axis=1))
        merged_b = (g0 * yr + g1 * ya).astype(BF16)
        merged_ref[...] = merged_b
        x1 = xv + _dot(merged_b, wout[...])
        rstd = lax.rsqrt(jnp.mean(x1 * x1, axis=-1, keepdims=True) + EPS)
        xh = x1 * rstd
        npl = np_ref[...]
        n1_b = (xh * npl).astype(BF16)
        n1_ref[...] = n1_b
        pg = _sigmoid(_dot(n1_b, wpg[...]) + bpg_ref[...])
        pe = _dot_nt(p_ref[...].astype(BF16), wple[...])
        err = x1 + pg * pe - tgt_ref[...]
        loss_t = 0.5 * inv_d * jnp.sum(err * err)
        dy = err * inv_d
        dpe_ref[...] = (dy * pg).astype(BF16)
        dpre = dy * pe * pg * (1.0 - pg)
        dpre_b = dpre.astype(BF16)
        dpre_ref[...] = dpre_b
        dn1 = _dot_nt(dpre_b, wpg[...])
        dnp = jnp.sum(dn1 * xh, axis=0, keepdims=True)
        dbpg = jnp.sum(dpre, axis=0, keepdims=True)
        gd = dn1 * npl
        dx1 = dy + rstd * (gd - xh * jnp.mean(gd * xh, axis=-1, keepdims=True))
        dx1_ref[...] = dx1
        dmerged = _dot_nt(dx1.astype(BF16), wout[...])
        dyr_b = (dmerged * g0).astype(BF16)
        dya_b = (dmerged * g1).astype(BF16)
        dyr_ref[...] = dyr_b
        dya_ref[...] = dya_b
        slab_c_ref[:, ATT:ATT + D] = (dmerged * yr * g0 * (1.0 - g0)).astype(BF16)
        slab_c_ref[:, ATT + D:ATT + 2 * D] = (dmerged * ya * g1 * (1.0 - g1)).astype(BF16)
        dyrnn = _dot_nt(dyr_b, wor[...])
        dyatt = _dot(dya_b, woa[...])
        dh_ref[...] = dyrnn * silu_r
        slab_a_ref[...] = (dyrnn * hv * szr * (1.0 + zr * (1.0 - szr))).astype(BF16)
        datt = dyatt * silu_a
        datt_ref[...] = datt
        slab_c_ref[:, 0:ATT] = (dyatt * av * sza * (1.0 + za * (1.0 - sza))).astype(BF16)
        da = datt * av
        for hh in range(NH):
            seg = slice(hh * HD, (hh + 1) * HD)
            sbar_ref[:, seg] = jnp.broadcast_to(jnp.sum(da[:, seg], axis=-1, keepdims=True), (tm, HD))

        @pl.when(first)
        def _():
            loss_ref[...] = jnp.full((8, LANES), loss_t, F32)
            dnp_ref[...] = dnp
            dbpg_ref[...] = dbpg

        @pl.when(jnp.logical_not(first))
        def _():
            loss_ref[...] += jnp.full((8, LANES), loss_t, F32)
            dnp_ref[...] += dnp
            dbpg_ref[...] += dbpg

    tok = lambda w: pl.BlockSpec((tm, w), lambda i: (i, 0))
    col = lambda w, blk: pl.BlockSpec((tm, w), lambda i: (i, blk))
    vec = lambda: pl.BlockSpec((1, D), lambda i: (0, 0))
    hbm = lambda: pl.BlockSpec(memory_space=pl.ANY)
    gb = OFF_G // 512
    in_specs = [tok(D), tok(DR), col(DR, 1), tok(ATT), col(ATT, OFF_ZA // ATT),
                col(512, gb), col(512, gb + 1), col(512, gb + 2), col(512, gb + 3),
                tok(PLE), tok(D), vec(), vec(), hbm(), hbm(), hbm(), hbm(), hbm()]
    sh = lambda w, dt: jax.ShapeDtypeStruct((T, w), dt)
    out_shape = [sh(D, F32), sh(D, BF16), sh(D, BF16), sh(D, BF16), sh(D, BF16), sh(D, BF16), sh(D, BF16),
                 sh(A_W, BF16), sh(C_W, BF16), sh(DR, F32), sh(ATT, F32), sh(ATT, F32),
                 sh(DR, BF16), sh(ATT, BF16),
                 jax.ShapeDtypeStruct((8, LANES), F32), jax.ShapeDtypeStruct((1, D), F32),
                 jax.ShapeDtypeStruct((1, D), F32)]
    out_specs = [tok(D), tok(D), tok(D), tok(D), tok(D), tok(D), tok(D), col(DR, 1), tok(C_W), tok(DR),
                 tok(ATT), tok(ATT), tok(DR), tok(ATT),
                 pl.BlockSpec((8, LANES), lambda i: (0, 0)), vec(), vec()]
    return pl.pallas_call(
        body, grid=(nt,), name="tail_fwd_bwd",
        in_specs=in_specs, out_specs=out_specs, out_shape=out_shape,
        scratch_shapes=[pltpu.VMEM((DR, D), BF16), pltpu.VMEM((D, ATT), BF16), pltpu.VMEM((D, D), BF16),
                        pltpu.VMEM((D, D), BF16), pltpu.VMEM((D, PLE), BF16)],
        compiler_params=_cp(("arbitrary",), VMEM_BIG),
    )(x, h, proj, att, proj, proj, proj, proj, proj, p, tgt, norm_ple, b_pg, w_o_rnn, w_o_att_t, w_out, w_pg, w_ple_t)


def _input_norm_bwd(x, dhn, dx1, gain, tm=512):
    def body(x_ref, dhn_ref, dx1_ref, g_ref, dx_ref, dg_ref):
        xv = x_ref[...]
        rstd = lax.rsqrt(jnp.mean(xv * xv, axis=-1, keepdims=True) + EPS)
        xh = xv * rstd
        dn = dhn_ref[...]
        dg = jnp.sum(dn * xh, axis=0, keepdims=True)
        gd = dn * g_ref[...]
        dx_ref[...] = dx1_ref[...] + rstd * (gd - xh * jnp.mean(gd * xh, axis=-1, keepdims=True))
        first = pl.program_id(0) == 0

        @pl.when(first)
        def _():
            dg_ref[...] = dg

        @pl.when(jnp.logical_not(first))
        def _():
            dg_ref[...] += dg

    tok = lambda: pl.BlockSpec((tm, D), lambda i: (i, 0))
    vec = lambda: pl.BlockSpec((1, D), lambda i: (0, 0))
    return pl.pallas_call(
        body, grid=(T // tm,), name="input_norm_bwd",
        in_specs=[tok(), tok(), tok(), vec()], out_specs=[tok(), vec()],
        out_shape=[jax.ShapeDtypeStruct((T, D), F32), jax.ShapeDtypeStruct((1, D), F32)],
        compiler_params=_cp(("arbitrary",), VMEM_MID),
    )(x, dhn, dx1, gain)


def _rope_tables():
    pos = jnp.arange(S, dtype=F32)
    inv_freq = ROPE_THETA ** (-jnp.arange(0, HD, 2, dtype=F32) / HD)
    ang = pos[:, None] * inv_freq[None, :]
    cos, sin = jnp.cos(ang), jnp.sin(ang)
    return jnp.concatenate([cos, cos], axis=1), jnp.concatenate([-sin, sin], axis=1)


def _local_step(x, p, tgt, w_in_t, other_weights, norm_mix, b_in, conv_b,
                w_rg_a, b_rg_a, w_rg_x, b_rg_x, lam, q_norm, k_norm, norm_ple, b_pg, start_reduce=None,
                entry_token=None):
    if start_reduce is None:
        start_reduce = lambda arrs, tag: (jnp.zeros((8, LANES), F32), arrs)
    if entry_token is None:
        entry_token = jnp.zeros((8, LANES), F32)
    cos_t, sin_t = _rope_tables()
    wa_b = w_rg_a.astype(BF16)
    wx_b = w_rg_x.astype(BF16)

    hn = _rmsnorm_fwd(x, norm_mix, entry_token)
    proj = _in_proj(hn, w_in_t, b_in)
    w_o_rnn, w_o_att_t, w_out, w_pg, w_ple_t, conv_w = other_weights(proj)
    proj3 = proj.reshape(BL, S, NIN)
    h3 = _rnn_fwd(proj3, conv_w, conv_b, wa_b, b_rg_a, wx_b, b_rg_x, lam)
    att3, lse, wts = _attn_fwd(proj3, cos_t, sin_t, q_norm, k_norm)
    (dx1, merged, n1, dpre, dpe, dyr, dya, slab_a, slab_c, dh, datt, sbar, yrnn, yatt, loss8, dnp, dbpg) = _tail(
        x, proj, h3.reshape(T, DR), att3.reshape(T, ATT), p, tgt, w_o_rnn, w_o_att_t, w_out, w_pg, w_ple_t,
        norm_ple, b_pg)

    token, pending_out = start_reduce([
        _mm_tn(yrnn, dyr, 640, 2048, "dw_o_rnn"),
        _mm_tn(dya, yatt, 512, 2048, "dw_o_att_t"),
        _mm_tn(merged, dx1, 512, 2048, "dw_out"),
        _mm_tn(n1, dpre, 512, 2048, "dw_ple_gate"),
        _mm_tn(dpe, p, 512, 2048, "dw_ple_t")], "out")

    slab_a3, dcw, dcb, dwa, dba, dwx, dbx, dlam = _rnn_bwd(
        proj3, h3, dh.reshape(BL, S, DR), slab_a.reshape(BL, S, A_W), conv_w, conv_b, wa_b, b_rg_a, wx_b, b_rg_x, lam,
        token)
    datt3 = datt.reshape(BL, S, ATT)
    sbar3 = sbar.reshape(BL, S, ATT)
    slabs = None
    dqn = []
    dkn = []
    for g in range(NG):
        dq, dk, dv, dqn_g, dkn_g = _attn_bwd_group(g, proj3, cos_t, sin_t, q_norm[g:g + 1], k_norm[g:g + 1],
                                                   lse, wts, datt3, sbar3, slabs)
        slabs = (dq, dk, dv)
        dqn.append(dqn_g)
        dkn.append(dkn_g)
    pieces = [slab_a3.reshape(T, A_W)] + [t.reshape(T, GW) for t in slabs] + [slab_c]
    dw_in_t, db_in = _dw_in(pieces, hn)
    token, pending_in = start_reduce([dw_in_t], "in")
    dhn = _dhn(pieces, w_in_t, token)
    grad_x, dnm = _input_norm_bwd(x, dhn, dx1, norm_mix)

    small = dict(w_rg_a=dwa, w_rg_x=dwx, norm_mix=dnm, b_in=db_in, conv_b=dcb, b_rg_a=dba, b_rg_x=dbx,
                 lru_lambda=dlam, q_norm=dqn, k_norm=dkn, norm_ple=dnp, b_ple_gate=dbpg, conv_w=dcw)
    return loss8[0, 0], grad_x, pending_out, pending_in, small


MESH = pl.DeviceIdType.MESH
HBM_SPEC = pl.BlockSpec(memory_space=pl.ANY)


def _my_pos():
    return lax.axis_index("x"), lax.axis_index("y"), lax.axis_index("c")


def _flip(pos, k):
    x, y, c = pos
    return (1 - x if k & 4 else x, 1 - y if k & 2 else y, 1 - c if k & 1 else c)


def _lin(pos):
    return 4 * pos[0] + 2 * pos[1] + pos[2]


def _chip(pos):
    return 2 * pos[0] + pos[1]


def _all_gather_two_level(shards, name):
    na = len(shards)

    def body(*refs):
        x_refs = refs[:na]
        out_refs = refs[na:2 * na]
        send_sems, recv_sems, local_sems = refs[2 * na:]
        me = _my_pos()
        sibling = _flip(me, 1)
        chips = [_flip(me, 4), _flip(me, 2), _flip(me, 6)]

        def copy(i, k, block, to, from_x=False):
            dst = out_refs[i].at[_lin(block)]
            return pltpu.make_async_remote_copy(
                src_ref=x_refs[i] if from_x else dst, dst_ref=dst,
                send_sem=send_sems.at[7 * i + k], recv_sem=recv_sems.at[7 * i + k], device_id=to, device_id_type=MESH)

        started = []
        for i in range(na):
            mine = pltpu.make_async_copy(x_refs[i], out_refs[i].at[_lin(me)], local_sems.at[i])
            mine.start()
            started.append(mine)
        sends = []
        for i in range(na):
            cps = [copy(i, 0, me, sibling, True)] + [copy(i, 1 + j, me, chip, True) for j, chip in enumerate(chips)]
            for cp in cps:
                cp.start()
            sends += cps
        for i in range(na):
            for j, chip in enumerate(chips):
                copy(i, 1 + j, chip, me).wait_recv()
                fwd = copy(i, 4 + j, chip, sibling)
                fwd.start()
                sends.append(fwd)
        for i in range(na):
            copy(i, 0, sibling, me).wait_recv()
            for j, chip in enumerate(chips):
                copy(i, 4 + j, _flip(chip, 1), me).wait_recv()
        for cp in sends:
            cp.wait_send()
        for mine in started:
            mine.wait()

    return pl.pallas_call(
        body, name=name,
        out_shape=[jax.ShapeDtypeStruct((NDEV,) + s.shape, s.dtype) for s in shards],
        in_specs=[HBM_SPEC] * na, out_specs=[HBM_SPEC] * na,
        scratch_shapes=[pltpu.SemaphoreType.DMA((7 * na,)), pltpu.SemaphoreType.DMA((7 * na,)),
                        pltpu.SemaphoreType.DMA((na,))],
    )(*shards)


def _all_gather_direct(shard, name):
    def body(x_ref, out_ref, send_sems, recv_sems, local_sem):
        me = _my_pos()
        mine = pltpu.make_async_copy(x_ref, out_ref.at[_lin(me)], local_sem)
        mine.start()
        sends = []
        for k in range(1, NDEV):
            cp = pltpu.make_async_remote_copy(
                src_ref=x_ref, dst_ref=out_ref.at[_lin(me)], send_sem=send_sems.at[k - 1],
                recv_sem=recv_sems.at[k - 1], device_id=_flip(me, k), device_id_type=MESH)
            cp.start()
            sends.append(cp)
        for k in range(1, NDEV):
            peer = _flip(me, k)
            pltpu.make_async_remote_copy(
                src_ref=x_ref, dst_ref=out_ref.at[_lin(peer)], send_sem=send_sems.at[k - 1],
                recv_sem=recv_sems.at[k - 1], device_id=peer, device_id_type=MESH).wait_recv()
        for cp in sends:
            cp.wait_send()
        mine.wait()

    return pl.pallas_call(
        body, name=name,
        out_shape=jax.ShapeDtypeStruct((NDEV,) + shard.shape, shard.dtype),
        in_specs=[HBM_SPEC], out_specs=HBM_SPEC,
        scratch_shapes=[pltpu.SemaphoreType.DMA((7,)), pltpu.SemaphoreType.DMA((7,)), pltpu.SemaphoreType.DMA],
    )(shard)


def _exchange_within_chip(parts, name):
    na = len(parts)

    def body(*refs):
        a_refs = refs[:na]
        recv_refs = refs[na:2 * na]
        send_sems, recv_sems = refs[2 * na:]
        me = _my_pos()
        c = me[2]
        sibling = _flip(me, 1)
        remote = []
        for i in range(na):
            for q in range(NCHIP):
                rc = pltpu.make_async_remote_copy(
                    src_ref=a_refs[i].at[q, 1 - c], dst_ref=recv_refs[i].at[q],
                    send_sem=send_sems.at[NCHIP * i + q], recv_sem=recv_sems.at[NCHIP * i + q],
                    device_id=sibling, device_id_type=MESH)
                rc.start()
                remote.append(rc)
        for rc in remote:
            rc.wait_recv()
        for rc in remote:
            rc.wait_send()

    return pl.pallas_call(
        body, name=name, out_shape=[jax.ShapeDtypeStruct((NCHIP,) + a.shape[2:], a.dtype) for a in parts],
        in_specs=[HBM_SPEC] * na, out_specs=[HBM_SPEC] * na,
        scratch_shapes=[pltpu.SemaphoreType.DMA((NCHIP * na,)), pltpu.SemaphoreType.DMA((NCHIP * na,))],
    )(*parts)


HBM_ONLY = pl.BlockSpec(memory_space=pltpu.HBM)
SEM_SPEC = pl.BlockSpec(memory_space=pltpu.SEMAPHORE)
SPLIT_COPY = pltpu.CompilerParams(has_side_effects=pltpu.SideEffectType.DATAFLOW_SIDE_EFFECTING)


def _chip_peers(me):
    return [_flip(me, 4), _flip(me, 2), _flip(me, 6)]


def _between_chips_start(parts, name):
    na = len(parts)

    def body(*refs):
        a_refs = refs[:na]
        land_refs = refs[na:2 * na]
        send_sems, recv_sems = refs[2 * na], refs[2 * na + 1]
        token = refs[-1]
        me = _my_pos()
        myq = _chip(me)
        for i in range(na):
            for j, peer in enumerate(_chip_peers(me)):
                pltpu.make_async_remote_copy(
                    src_ref=a_refs[i].at[_chip(peer)], dst_ref=land_refs[i].at[myq],
                    send_sem=send_sems.at[3 * i + j], recv_sem=recv_sems.at[3 * i + j],
                    device_id=peer, device_id_type=MESH).start()
        token[...] = jnp.zeros_like(token)

    hbm = [pltpu.HBM(a.shape, a.dtype) for a in parts]
    srcs = [pltpu.with_memory_space_constraint(a, pltpu.HBM) for a in parts]
    lands = [pltpu.with_memory_space_constraint(lax.empty(a.shape, a.dtype), pltpu.HBM) for a in parts]
    res = pl.pallas_call(
        body, name=name,
        out_shape=(pltpu.SemaphoreType.DMA((3 * na,)), pltpu.SemaphoreType.DMA((3 * na,)), *hbm, *hbm,
                   jax.ShapeDtypeStruct((8, LANES), F32)),
        in_specs=[HBM_ONLY] * (2 * na),
        out_specs=(SEM_SPEC, SEM_SPEC, *([HBM_ONLY] * (2 * na)), pl.BlockSpec(memory_space=pltpu.VMEM)),
        input_output_aliases={i: 2 + i for i in range(2 * na)},
        compiler_params=SPLIT_COPY,
    )(*srcs, *lands)
    return res[-1], (res[0], res[1], list(res[2:2 + na]), list(res[2 + na:2 + 2 * na]))


def _between_chips_wait(pending, after, name):
    send_sems, recv_sems, parts, lands = pending
    na = len(parts)

    def body(*refs):
        a_refs = refs[:na]
        land_refs = refs[na:2 * na]
        send_sems, recv_sems = refs[2 * na], refs[2 * na + 1]
        me = _my_pos()
        for i in range(na):
            for j, peer in enumerate(_chip_peers(me)):
                cp = pltpu.make_async_remote_copy(
                    src_ref=a_refs[i].at[_chip(peer)], dst_ref=land_refs[i].at[_chip(peer)],
                    send_sem=send_sems.at[3 * i + j], recv_sem=recv_sems.at[3 * i + j],
                    device_id=peer, device_id_type=MESH)
                cp.wait_send()
                cp.wait_recv()

    hbm = [pltpu.HBM(a.shape, a.dtype) for a in parts]
    res = pl.pallas_call(
        body, name=name, out_shape=(*hbm, *hbm),
        in_specs=[HBM_ONLY] * (2 * na) + [SEM_SPEC, SEM_SPEC, pl.BlockSpec(memory_space=pl.ANY)],
        out_specs=[HBM_ONLY] * (2 * na),
        input_output_aliases={i: i for i in range(2 * na)},
        compiler_params=SPLIT_COPY,
    )(*parts, *lands, send_sems, recv_sems, after)
    return list(res[:na]), list(res[na:])


def _gather_start(shards, after, name):
    na = len(shards)

    def body(*refs):
        x_refs = refs[:na]
        land_refs = refs[na:2 * na]
        send_sems, recv_sems = refs[2 * na + 1], refs[2 * na + 2]
        token = refs[-1]
        me = _my_pos()
        for i in range(na):
            for k in range(1, NDEV):
                pltpu.make_async_remote_copy(
                    src_ref=x_refs[i], dst_ref=land_refs[i].at[_lin(me)],
                    send_sem=send_sems.at[7 * i + k - 1], recv_sem=recv_sems.at[7 * i + k - 1],
                    device_id=_flip(me, k), device_id_type=MESH).start()
        token[...] = jnp.zeros_like(token)

    src_t = [pltpu.HBM(a.shape, a.dtype) for a in shards]
    land_t = [pltpu.HBM((NDEV,) + a.shape, a.dtype) for a in shards]
    srcs = [pltpu.with_memory_space_constraint(a, pltpu.HBM) for a in shards]
    lands = [pltpu.with_memory_space_constraint(lax.empty((NDEV,) + a.shape, a.dtype), pltpu.HBM) for a in shards]
    res = pl.pallas_call(
        body, name=name,
        out_shape=(pltpu.SemaphoreType.DMA((7 * na,)), pltpu.SemaphoreType.DMA((7 * na,)), *src_t, *land_t,
                   jax.ShapeDtypeStruct((8, LANES), F32)),
        in_specs=[HBM_ONLY] * (2 * na) + [pl.BlockSpec(memory_space=pl.ANY)],
        out_specs=(SEM_SPEC, SEM_SPEC, *([HBM_ONLY] * (2 * na)), pl.BlockSpec(memory_space=pltpu.VMEM)),
        input_output_aliases={i: 2 + i for i in range(2 * na)},
        compiler_params=SPLIT_COPY,
    )(*srcs, *lands, after)
    return res[-1], (res[0], res[1], list(res[2:2 + na]), list(res[2 + na:2 + 2 * na]))


def _gather_wait(pending, after, name):
    send_sems, recv_sems, shards, lands = pending
    na = len(shards)

    def body(*refs):
        x_refs = refs[:na]
        land_refs = refs[na:2 * na]
        send_sems, recv_sems = refs[2 * na], refs[2 * na + 1]
        me = _my_pos()
        for i in range(na):
            for k in range(1, NDEV):
                peer = _flip(me, k)
                cp = pltpu.make_async_remote_copy(
                    src_ref=x_refs[i], dst_ref=land_refs[i].at[_lin(peer)],
                    send_sem=send_sems.at[7 * i + k - 1], recv_sem=recv_sems.at[7 * i + k - 1],
                    device_id=peer, device_id_type=MESH)
                cp.wait_send()
                cp.wait_recv()

    src_t = [pltpu.HBM(a.shape, a.dtype) for a in shards]
    land_t = [pltpu.HBM(a.shape, a.dtype) for a in lands]
    res = pl.pallas_call(
        body, name=name, out_shape=(*src_t, *land_t),
        in_specs=[HBM_ONLY] * (2 * na) + [SEM_SPEC, SEM_SPEC, pl.BlockSpec(memory_space=pl.ANY)],
        out_specs=[HBM_ONLY] * (2 * na),
        input_output_aliases={i: i for i in range(2 * na)},
        compiler_params=SPLIT_COPY,
    )(*shards, *lands, send_sems, recv_sems, after)
    return list(res[na:])


def _exchange_all(parts, name):
    na = len(parts)

    def body(*refs):
        a_refs = refs[:na]
        out_refs = refs[na:2 * na]
        send_sems, recv_sems = refs[2 * na:]
        me = _my_pos()
        sends = []
        for i in range(na):
            for k in range(1, NDEV):
                peer = _flip(me, k)
                cp = pltpu.make_async_remote_copy(
                    src_ref=a_refs[i].at[_lin(peer)], dst_ref=out_refs[i].at[_lin(me)],
                    send_sem=send_sems.at[7 * i + k - 1], recv_sem=recv_sems.at[7 * i + k - 1],
                    device_id=peer, device_id_type=MESH)
                cp.start()
                sends.append(cp)
        for i in range(na):
            for k in range(1, NDEV):
                peer = _flip(me, k)
                pltpu.make_async_remote_copy(
                    src_ref=a_refs[i].at[_lin(peer)], dst_ref=out_refs[i].at[_lin(peer)],
                    send_sem=send_sems.at[7 * i + k - 1], recv_sem=recv_sems.at[7 * i + k - 1],
                    device_id=peer, device_id_type=MESH).wait_recv()
        for cp in sends:
            cp.wait_send()

    return pl.pallas_call(
        body, name=name, out_shape=[jax.ShapeDtypeStruct(a.shape, a.dtype) for a in parts],
        in_specs=[HBM_SPEC] * na, out_specs=[HBM_SPEC] * na,
        scratch_shapes=[pltpu.SemaphoreType.DMA((7 * na,)), pltpu.SemaphoreType.DMA((7 * na,))],
    )(*parts)


def _scalar(v):
    return jnp.asarray(v, jnp.int32).reshape(1)


def _sum_pairs(parts, theirs, name):
    na = len(parts)

    def body(c_ref, *refs):
        for i in range(na):
            o_ref = refs[2 * na + i]
            o_ref[0] = (refs[i][0, 0].astype(F32) + refs[na + i][0].astype(F32)).astype(o_ref.dtype)

    def mine_spec(a):
        return pl.BlockSpec((1, 1) + a.shape[2:], lambda q, c_ref: (q, c_ref[0], 0, 0))

    def spec(a):
        return pl.BlockSpec((1,) + a.shape[1:], lambda q, c_ref: (q, 0, 0))

    return pl.pallas_call(
        body, name=name,
        grid_spec=pltpu.PrefetchScalarGridSpec(
            num_scalar_prefetch=1, grid=(NCHIP,),
            in_specs=[mine_spec(a) for a in parts] + [spec(a) for a in theirs],
            out_specs=[spec(a) for a in theirs]),
        out_shape=[jax.ShapeDtypeStruct(a.shape, a.dtype) for a in theirs],
        compiler_params=_cp(("arbitrary",), VMEM_BIG),
    )(_scalar(lax.axis_index("c")), *parts, *theirs)


def _others(q, mine, nblk=NCHIP):
    return jnp.where(q == mine, (q + 1) % nblk, q)


def _sum_chips_adamw(own, recv, wv, mv, vv, tr, name):
    _, r, w = recv.shape

    def body(q_ref, own_ref, r0, r1, r2, r3, w_ref, m_ref, v_ref, g_ref, d_ref, m2_ref, v2_ref):
        myq = q_ref[0]
        acc = None
        for q, r_ref in enumerate((r0, r1, r2, r3)):
            term = jnp.where(myq == q, own_ref[0], r_ref[0]).astype(F32)
            acc = term if acc is None else acc + term
        g_ref[...] = acc
        delta, m2, v2 = _adam_math(w_ref[...], acc, m_ref[...], v_ref[...])
        d_ref[...] = delta
        m2_ref[...] = m2
        v2_ref[...] = v2

    def recv_spec(q):
        return pl.BlockSpec((1, tr, w), lambda i, q_ref: (_others(q, q_ref[0]), i, 0))

    rows = lambda: pl.BlockSpec((tr, w), lambda i, q_ref: (i, 0))
    shp = jax.ShapeDtypeStruct((r, w), F32)
    return pl.pallas_call(
        body, name=name,
        grid_spec=pltpu.PrefetchScalarGridSpec(
            num_scalar_prefetch=1, grid=(r // tr,),
            in_specs=[pl.BlockSpec((1, tr, w), lambda i, q_ref: (q_ref[0], i, 0))]
            + [recv_spec(q) for q in range(NCHIP)] + [rows(), rows(), rows()],
            out_specs=[rows(), rows(), rows(), rows()]),
        out_shape=[shp, shp, shp, shp],
        compiler_params=_cp(("arbitrary",), VMEM_MID),
    )(_scalar(_chip(_my_pos())), own, recv, recv, recv, recv, wv, mv, vv)


def _sum_blocks_small(own, recv, mine, transpose, name):
    na = len(recv)
    nblk = recv[0].shape[0]

    def body(q_ref, *refs):
        me = q_ref[0]
        for i in range(na):
            acc = None
            for q in range(nblk):
                term = jnp.where(me == q, refs[i][0], refs[na * (1 + q) + i][0]).astype(F32)
                acc = term if acc is None else acc + term
            refs[na * (1 + nblk) + i][...] = acc.T if transpose[i] else acc

    def oshape(a, tr):
        r, w = a.shape[1:]
        return (w, r) if tr else (r, w)

    own_spec = lambda a: pl.BlockSpec((1,) + a.shape[1:], lambda s, q_ref: (q_ref[0], 0, 0))
    recv_spec = lambda a, q: pl.BlockSpec((1,) + a.shape[1:], lambda s, q_ref: (_others(q, q_ref[0], nblk), 0, 0))
    out_spec = lambda shp: pl.BlockSpec(shp, lambda s, q_ref: (0, 0))
    in_specs = [own_spec(a) for a in own]
    for q in range(nblk):
        in_specs += [recv_spec(a, q) for a in recv]
    return pl.pallas_call(
        body, name=name,
        grid_spec=pltpu.PrefetchScalarGridSpec(
            num_scalar_prefetch=1, grid=(1,), in_specs=in_specs,
            out_specs=[out_spec(oshape(a, tr)) for a, tr in zip(recv, transpose)]),
        out_shape=[jax.ShapeDtypeStruct(oshape(a, tr), F32) for a, tr in zip(recv, transpose)],
        compiler_params=_cp(("arbitrary",), VMEM_MID),
    )(_scalar(mine), *own, *(list(recv) * nblk))


def _rep_offsets():
    offs = []
    o = 0
    for r in REP_ROWS:
        offs.append(o)
        o += r
    return offs


def _pack_small_grads(g):
    offs = _rep_offsets()

    def body(dwa, dwx, dnm, dbin, dcb, dba, dbx, dlam, dq0, dq1, dq2, dk0, dk1, dk2, dnp, dbpg, o_ref):
        o_ref[pl.ds(REP_TOTAL_ROWS - 2, NDEV * REP_ROWS_DEV - REP_TOTAL_ROWS + 2), :] = jnp.zeros(
            (NDEV * REP_ROWS_DEV - REP_TOTAL_ROWS + 2, LANES), F32)
        for n in range(NRB):
            o_ref[pl.ds(offs[0] + n * RBW, RBW), :] = dwa[n]
            o_ref[pl.ds(offs[1] + n * RBW, RBW), :] = dwx[n]

        def put_vec(off, ref, rows):
            for k in range(rows):
                o_ref[pl.ds(off + k, 1), :] = ref[:, k * LANES:(k + 1) * LANES]

        put_vec(offs[2], dnm, REP_ROWS[2])
        put_vec(offs[3], dbin, REP_ROWS[3])
        put_vec(offs[4], dcb, REP_ROWS[4])
        put_vec(offs[5], dba, REP_ROWS[5])
        put_vec(offs[6], dbx, REP_ROWS[6])
        put_vec(offs[7], dlam, REP_ROWS[7])
        for k, ref in enumerate((dq0, dq1, dq2)):
            o_ref[pl.ds(offs[8] + k, 1), :] = ref[...]
        for k, ref in enumerate((dk0, dk1, dk2)):
            o_ref[pl.ds(offs[9] + k, 1), :] = ref[...]
        put_vec(offs[10], dnp, REP_ROWS[10])
        put_vec(offs[11], dbpg, REP_ROWS[11])

    args = [g["w_rg_a"], g["w_rg_x"], g["norm_mix"], g["b_in"], g["conv_b"], g["b_rg_a"], g["b_rg_x"],
            g["lru_lambda"], *g["q_norm"], *g["k_norm"], g["norm_ple"], g["b_ple_gate"]]
    full = lambda shp: pl.BlockSpec(shp, lambda: (0,) * len(shp))
    return pl.pallas_call(
        body, name="pack_small_grads",
        in_specs=[full(a.shape) for a in args],
        out_specs=full((NDEV * REP_ROWS_DEV, LANES)),
        out_shape=jax.ShapeDtypeStruct((NDEV * REP_ROWS_DEV, LANES), F32),
    )(*args)


def _adam_math(wv, gv, mv, vv):
    c1 = 1.0 - B1 ** STEP
    c2 = 1.0 - B2 ** STEP
    m2 = B1 * mv + (1.0 - B1) * gv
    v2 = B2 * vv + (1.0 - B2) * (gv * gv)
    delta = (-LR) * ((m2 / c1) / (jnp.sqrt(v2 / c2) + AEPS) + WD * wv)
    return delta, m2, v2


def _adamw_small(rep_flat, w, m, v):
    offs = _rep_offsets()
    n = len(REP_NAMES)

    def body(*refs):
        g_ref = refs[0]
        w_refs = refs[1:1 + n]
        m_refs = refs[1 + n:1 + 2 * n]
        v_refs = refs[1 + 2 * n:1 + 3 * n]
        outs = refs[1 + 3 * n:]
        go, do, mo, vo = outs[:n], outs[n:2 * n], outs[2 * n:3 * n], outs[3 * n:]

        def emit(i, idx, gv):
            go[i][idx] = gv
            delta, m2, v2 = _adam_math(w_refs[i][idx], gv, m_refs[i][idx], v_refs[i][idx])
            do[i][idx] = delta
            mo[i][idx] = m2
            vo[i][idx] = v2

        for i in range(n):
            if i < 2:
                for b in range(NRB):
                    emit(i, b, g_ref[pl.ds(offs[i] + b * RBW, RBW), :])
            elif REP_NAMES[i] in ("q_norm", "k_norm"):
                emit(i, slice(None), g_ref[pl.ds(offs[i], NG), :])
            else:
                gv = jnp.concatenate([g_ref[pl.ds(offs[i] + k, 1), :] for k in range(REP_ROWS[i])], axis=1)
                emit(i, slice(None), gv)

    full = lambda shp: pl.BlockSpec(shp, lambda: (0,) * len(shp))
    pspecs = [full(a.shape) for a in w]
    pshapes = [jax.ShapeDtypeStruct(a.shape, F32) for a in w]
    res = pl.pallas_call(
        body, name="adamw_small",
        in_specs=[full(rep_flat.shape)] + pspecs * 3,
        out_specs=pspecs * 4, out_shape=pshapes * 4,
        compiler_params=_cp(None, VMEM_MID),
    )(rep_flat, *w, *m, *v)
    return res[:n], res[n:2 * n], res[2 * n:3 * n], res[3 * n:]


def _adamw_many(w, g, m, v):
    n = len(w)

    def body(*refs):
        for i in range(n):
            delta, m2, v2 = _adam_math(refs[i][...], refs[n + i][...], refs[2 * n + i][...], refs[3 * n + i][...])
            refs[4 * n + i][...] = delta
            refs[5 * n + i][...] = m2
            refs[6 * n + i][...] = v2

    full = lambda shp: pl.BlockSpec(shp, lambda: (0,) * len(shp))
    specs = [full(a.shape) for a in w]
    shapes = [jax.ShapeDtypeStruct(a.shape, F32) for a in w]
    res = pl.pallas_call(
        body, name="adamw_shards",
        in_specs=specs * 4, out_specs=specs * 3, out_shape=shapes * 3,
        compiler_params=_cp(None, VMEM_MID),
    )(*w, *g, *m, *v)
    return res[:n], res[n:2 * n], res[2 * n:]


def kernel(x, p, norm_mix, w_in, b_in, conv_w, conv_b, w_rg_a, b_rg_a, w_rg_x, b_rg_x, lru_lambda, q_norm, k_norm, w_o_rnn, w_o_att, w_out, norm_ple, w_ple_gate, b_ple_gate, w_ple, loss_target, m_norm_mix, m_w_in, m_b_in, m_conv_w, m_conv_b, m_w_rg_a, m_b_rg_a, m_w_rg_x, m_b_rg_x, m_lru_lambda, m_q_norm, m_k_norm, m_w_o_rnn, m_w_o_att, m_w_out, m_norm_ple, m_w_ple_gate, m_b_ple_gate, m_w_ple, v_norm_mix, v_w_in, v_b_in, v_conv_w, v_conv_b, v_w_rg_a, v_b_rg_a, v_w_rg_x, v_b_rg_x, v_lru_lambda, v_q_norm, v_k_norm, v_w_o_rnn, v_w_o_att, v_w_out, v_norm_ple, v_w_ple_gate, v_b_ple_gate, v_w_ple):
    w = dict(norm_mix=norm_mix, w_in=w_in, b_in=b_in, conv_w=conv_w, conv_b=conv_b, w_rg_a=w_rg_a, b_rg_a=b_rg_a,
             w_rg_x=w_rg_x, b_rg_x=b_rg_x, lru_lambda=lru_lambda, q_norm=q_norm, k_norm=k_norm, w_o_rnn=w_o_rnn,
             w_o_att=w_o_att, w_out=w_out, norm_ple=norm_ple, w_ple_gate=w_ple_gate, b_ple_gate=b_ple_gate,
             w_ple=w_ple)
    m = dict(norm_mix=m_norm_mix, w_in=m_w_in, b_in=m_b_in, conv_w=m_conv_w, conv_b=m_conv_b, w_rg_a=m_w_rg_a,
             b_rg_a=m_b_rg_a, w_rg_x=m_w_rg_x, b_rg_x=m_b_rg_x, lru_lambda=m_lru_lambda, q_norm=m_q_norm,
             k_norm=m_k_norm, w_o_rnn=m_w_o_rnn, w_o_att=m_w_o_att, w_out=m_w_out, norm_ple=m_norm_ple,
             w_ple_gate=m_w_ple_gate, b_ple_gate=m_b_ple_gate, w_ple=m_w_ple)
    v = dict(norm_mix=v_norm_mix, w_in=v_w_in, b_in=v_b_in, conv_w=v_conv_w, conv_b=v_conv_b, w_rg_a=v_w_rg_a,
             b_rg_a=v_b_rg_a, w_rg_x=v_w_rg_x, b_rg_x=v_b_rg_x, lru_lambda=v_lru_lambda, q_norm=v_q_norm,
             k_norm=v_k_norm, w_o_rnn=v_w_o_rnn, w_o_att=v_w_o_att, w_out=v_w_out, norm_ple=v_norm_ple,
             w_ple_gate=v_w_ple_gate, b_ple_gate=v_b_ple_gate, w_ple=v_w_ple)
    names = list(w.keys())

    shards = [w_in[0].T.astype(BF16), w_o_rnn[0].astype(BF16), w_o_att[0].T.astype(BF16), w_out[0].astype(BF16),
              w_ple_gate[0].astype(BF16), w_ple[0].T.astype(BF16), conv_w[0]]
    w_in_g = _all_gather_two_level(shards[:1], "gather_w_in")[0]
    w_in_t = w_in_g.reshape(NIN, D)
    entry_token, gather_pending = _gather_start(shards[1:], w_in_g, "gather_others_start")
    me = _lin(_my_pos())

    def other_weights(after):
        landed = _gather_wait(gather_pending, after, "gather_others_wait")
        full = [lax.dynamic_update_slice(a, s[None], (me, 0, 0)) for a, s in zip(landed, shards[1:])]
        conv_f = full[5].transpose(1, 0, 2).reshape(CONVW, DR)
        return (*[a.reshape((NDEV * a.shape[1], a.shape[2])) for a in full[:5]], conv_f)

    def start_reduce(arrs, tag):
        parts = [a.reshape((NCHIP, 2, a.shape[0] // NDEV, a.shape[1])) for a in arrs]
        theirs = _exchange_within_chip(parts, "reduce_within_chip_" + tag)
        return _between_chips_start(_sum_pairs(parts, theirs, "sum_pairs_" + tag), "reduce_between_chips_start_" + tag)

    loss_part, grad_x, pending_out, pending_in, small = _local_step(
        x.reshape(T, D), p.reshape(T, PLE), loss_target.reshape(T, D),
        w_in_t, other_weights,
        norm_mix, b_in, conv_b, w_rg_a[0], b_rg_a, w_rg_x[0], b_rg_x, lru_lambda, q_norm[0], k_norm[0],
        norm_ple, b_ple_gate, start_reduce, entry_token)
    loss = lax.psum(loss_part, ("x", "y", "c"))

    rep_parts = _pack_small_grads(small).reshape(NDEV, REP_ROWS_DEV, LANES)
    conv_parts = small["conv_w"].reshape(CONVW, NDEV, DR // NDEV).transpose(1, 0, 2)
    small_parts = [rep_parts, conv_parts]
    g_rep, g_conv = _sum_blocks_small(small_parts, _exchange_all(small_parts, "reduce_small"), me, (False, False),
                                      "sum_small")
    rep_all = _all_gather_direct(g_rep, "gather_small").reshape(NDEV * REP_ROWS_DEV, LANES)

    myq = _chip(_my_pos())
    own_out, recv_out = _between_chips_wait(pending_out, rep_all, "reduce_between_chips_wait_out")
    own_in, recv_in = _between_chips_wait(pending_in, rep_all, "reduce_between_chips_wait_in")
    w_in_res = _sum_chips_adamw(own_in[0], recv_in[0], w_in[0].T, m_w_in[0].T, v_w_in[0].T, 304, "adamw_w_in")
    g_o_rnn, g_o_att, g_out, g_pg, g_ple = _sum_blocks_small(
        own_out, recv_out, myq, (False, True, False, False, True), "sum_chips_out")

    grad, delta, new_m, new_v = {}, {}, {}, {}
    rep_shape = lambda a: a if a.ndim == 2 else a.reshape(a.shape[1:])
    res = _adamw_small(rep_all, [rep_shape(w[n]) for n in REP_NAMES], [rep_shape(m[n]) for n in REP_NAMES],
                       [rep_shape(v[n]) for n in REP_NAMES])
    for dst, vals in zip((grad, delta, new_m, new_v), res):
        for n, a in zip(REP_NAMES, vals):
            dst[n] = a.reshape(w[n].shape)
    grad["w_in"], delta["w_in"], new_m["w_in"], new_v["w_in"] = [a.T[None] for a in w_in_res]
    rest = ("w_o_rnn", "w_o_att", "w_out", "w_ple_gate", "w_ple", "conv_w")
    g_rest = [g_o_rnn, g_o_att, g_out, g_pg, g_ple, g_conv]
    res = _adamw_many([w[n][0] for n in rest], g_rest, [m[n][0] for n in rest], [v[n][0] for n in rest])
    for n, a in zip(rest, g_rest):
        grad[n] = a[None]
    for dst, vals in zip((delta, new_m, new_v), res):
        for n, a in zip(rest, vals):
            dst[n] = a[None]

    return (loss, grad_x.reshape(BL, S, D), *[grad[n] for n in names], *[delta[n] for n in names],
            *[new_m[n] for n in names], *[new_v[n] for n in names])
```

```python
import jax
import jax.numpy as jnp
from jax import lax
from jax.experimental import pallas as pl
from jax.experimental.pallas import tpu as pltpu

F32 = jnp.float32
BF16 = jnp.bfloat16

D = 1024
S = 2048
BL = 2
T = BL * S
NDEV = 8
NCHIP = 4
PLE = 256
DR = 1280
NRB = 10
RBW = 128
CONVW = 4
LRU_C = 8.0
HD = 128
NH = 4
PATTERNS = ((128, 1), (512, 4), (2048, 16))
NG = 3
ATT = NH * HD
GW = NG * ATT
NIN = 2 * DR + 3 * GW + ATT + 2 * D
OFF_ZR = DR
OFF_Q = 2 * DR
OFF_ZA = OFF_Q + 3 * GW
OFF_G = OFF_ZA + ATT
ROPE_THETA = 10000.0
EPS = 1e-6
SCALE = HD ** -0.5
NEG = -1e30
QB = 128
LANES = 128
CT = 512
NCT = NIN // CT
A_W = 2 * DR
C_W = ATT + 2 * D

LR, B1, B2, AEPS, WD, STEP = 0.001, 0.9, 0.999, 1e-08, 0.01, 10

NSHARD_IN = NIN // NDEV
REP_NAMES = ("w_rg_a", "w_rg_x", "norm_mix", "b_in", "conv_b", "b_rg_a", "b_rg_x", "lru_lambda", "q_norm",
             "k_norm", "norm_ple", "b_ple_gate")
REP_ROWS = (NRB * RBW, NRB * RBW, D // LANES, NIN // LANES, DR // LANES, DR // LANES, DR // LANES, DR // LANES,
            NG, NG, D // LANES, D // LANES)
REP_TOTAL_ROWS = sum(REP_ROWS)
REP_ROWS_DEV = 344
BIG_NAMES = ("w_in", "w_o_rnn", "w_o_att", "w_out", "w_ple_gate", "w_ple")

VMEM_BIG = 56 * 1024 * 1024
VMEM_MID = 40 * 1024 * 1024


def _cp(sem=None, vmem=None):
    return pltpu.CompilerParams(dimension_semantics=sem, vmem_limit_bytes=vmem)


def _hbm(*arrays):
    return [pltpu.with_memory_space_constraint(a, pltpu.HBM) for a in arrays]


def _dot(a, b):
    return jnp.dot(a, b, preferred_element_type=F32)


def _dot_nt(a, b):
    return lax.dot_general(a, b, (((1,), (1,)), ((), ())), preferred_element_type=F32)


def _dot_tn(a, b):
    return lax.dot_general(a, b, (((0,), (0,)), ((), ())), preferred_element_type=F32)


def _sigmoid(x):
    return jax.nn.sigmoid(x)


def _perm(j):
    jq = j - OFF_Q // CT
    inside = (j >= OFF_Q // CT) & (j < OFF_ZA // CT)
    return jnp.where(inside, OFF_Q // CT + (jq % 3) * 3 + jq // 3, j)


PIECES = ((0, A_W // CT), (OFF_Q // CT, GW // CT), (OFF_Q // CT + 3, GW // CT), (OFF_Q // CT + 6, GW // CT),
          (OFF_ZA // CT, C_W // CT))


def _rmsnorm_fwd(x, gain, token, tm=512):
    def body(x_ref, g_ref, _token, o_ref):
        xv = x_ref[...]
        var = jnp.mean(xv * xv, axis=-1, keepdims=True)
        o_ref[...] = (xv * lax.rsqrt(var + EPS) * g_ref[...]).astype(BF16)

    return pl.pallas_call(
        body, grid=(T // tm,), name="rmsnorm_fwd",
        in_specs=[pl.BlockSpec((tm, D), lambda i: (i, 0)), pl.BlockSpec((1, D), lambda i: (0, 0)),
                  pl.BlockSpec((8, LANES), lambda i: (0, 0))],
        out_specs=pl.BlockSpec((tm, D), lambda i: (i, 0)),
        out_shape=jax.ShapeDtypeStruct((T, D), BF16),
        compiler_params=_cp(("parallel",)),
    )(*_hbm(x, gain, token))


def _in_proj(hn, w_t, bias, tm=1024):
    def body(a_ref, w_ref, b_ref, o_ref):
        o_ref[...] = _dot_nt(a_ref[...], w_ref[...]) + b_ref[...]

    return pl.pallas_call(
        body, grid=(T // tm, NCT), name="in_proj",
        in_specs=[pl.BlockSpec((tm, D), lambda i, j: (i, 0)),
                  pl.BlockSpec((CT, D), lambda i, j: (_perm(j), 0)),
                  pl.BlockSpec((1, CT), lambda i, j: (0, _perm(j)))],
        out_specs=pl.BlockSpec((tm, CT), lambda i, j: (i, j)),
        out_shape=jax.ShapeDtypeStruct((T, NIN), F32),
        compiler_params=_cp(("parallel", "parallel"), VMEM_MID),
    )(*_hbm(hn, w_t, bias))


def _dhn(pieces, w_t, token, tm=512):
    def body(a_ref, q_ref, k_ref, v_ref, c_ref, w_hbm, _token, o_ref, w):
        @pl.when(pl.program_id(0) == 0)
        def _():
            pltpu.sync_copy(w_hbm, w)

        acc = _dot(a_ref[...], w[pl.ds(0, A_W), :])
        for kind, x_ref in enumerate((q_ref, k_ref, v_ref)):
            for g in range(NG):
                row = OFF_Q + (3 * g + kind) * CT
                acc = acc + _dot(x_ref[:, g * CT:(g + 1) * CT], w[pl.ds(row, CT), :])
        o_ref[...] = acc + _dot(c_ref[...], w[pl.ds(OFF_ZA, C_W), :])

    tok = lambda wd: pl.BlockSpec((tm, wd), lambda i: (i, 0))
    return pl.pallas_call(
        body, grid=(T // tm,), name="dhn",
        in_specs=[tok(A_W), tok(GW), tok(GW), tok(GW), tok(C_W), pl.BlockSpec(memory_space=pl.ANY),
                  pl.BlockSpec((8, LANES), lambda i: (0, 0))],
        out_specs=tok(D),
        out_shape=jax.ShapeDtypeStruct((T, D), F32),
        scratch_shapes=[pltpu.VMEM((NIN, D), BF16)],
        compiler_params=_cp(("arbitrary",), VMEM_BIG),
    )(*_hbm(*pieces, w_t, token))


def _dw_in(pieces, hn, tt=2048):
    nt = T // tt

    def body(a_ref, q_ref, k_ref, v_ref, c_ref, h_ref, o_ref, s_ref, acc, cs):
        j = pl.program_id(0)
        t = pl.program_id(1)

        def step(x_ref):
            xv = x_ref[...]
            p = _dot_tn(xv, h_ref[...])
            c = jnp.sum(xv.astype(F32), axis=0, keepdims=True)

            @pl.when(t == 0)
            def _():
                acc[...] = p
                cs[...] = c

            @pl.when(t > 0)
            def _():
                acc[...] += p
                cs[...] += c

        for x_ref, (lo, n) in zip((a_ref, q_ref, k_ref, v_ref, c_ref), PIECES):
            pl.when((j >= lo) & (j < lo + n))(lambda x_ref=x_ref: step(x_ref))

        @pl.when(t == nt - 1)
        def _():
            o_ref[...] = acc[...].astype(BF16)
            s_ref[...] = cs[...]

    def piece_spec(lo, n):
        def imap(j, t):
            used = (j >= lo) & (j < lo + n)
            return (jnp.where(used, t, 0), jnp.clip(j - lo, 0, n - 1))
        return pl.BlockSpec((tt, CT), imap)

    return pl.pallas_call(
        body, grid=(NCT, nt), name="dw_in",
        in_specs=[piece_spec(lo, n) for lo, n in PIECES] + [pl.BlockSpec((tt, D), lambda j, t: (t, 0))],
        out_specs=[pl.BlockSpec((CT, D), lambda j, t: (_perm(j), 0)), pl.BlockSpec((1, CT), lambda j, t: (0, _perm(j)))],
        out_shape=[jax.ShapeDtypeStruct((NIN, D), BF16), jax.ShapeDtypeStruct((1, NIN), F32)],
        scratch_shapes=[pltpu.VMEM((CT, D), F32), pltpu.VMEM((1, CT), F32)],
        compiler_params=_cp(("parallel", "arbitrary"), VMEM_MID),
    )(*_hbm(*pieces, hn))


def _mm_tn(a, b, ta, tt, name):
    m = a.shape[1]
    n = b.shape[1]
    nt = T // tt

    def body(a_ref, b_ref, o_ref, acc):
        t = pl.program_id(1)
        p = _dot_tn(a_ref[...].astype(BF16), b_ref[...].astype(BF16))

        @pl.when(t == 0)
        def _():
            acc[...] = p

        @pl.when(t > 0)
        def _():
            acc[...] += p

        @pl.when(t == nt - 1)
        def _():
            o_ref[...] = acc[...].astype(BF16)

    return pl.pallas_call(
        body, grid=(m // ta, nt), name=name,
        in_specs=[pl.BlockSpec((tt, ta), lambda j, t: (t, j)), pl.BlockSpec((tt, n), lambda j, t: (t, 0))],
        out_specs=pl.BlockSpec((ta, n), lambda j, t: (j, 0)),
        out_shape=jax.ShapeDtypeStruct((m, n), BF16),
        scratch_shapes=[pltpu.VMEM((ta, n), F32)],
        compiler_params=_cp(("parallel", "arbitrary"), VMEM_MID),
    )(*_hbm(a, b))


def _row_iota():
    return lax.broadcasted_iota(jnp.int32, (S, RBW), 0)


def _shift_down(v, d, row, fill):
    return jnp.where(row >= d, pltpu.roll(v, d, 0), fill)


def _shift_up(v, d, row, fill):
    return jnp.where(row < S - d, pltpu.roll(v, S - d, 0), fill)


SUBLANES = 8


def _scan_down(a, u, row):
    d = 1
    while d < S:
        last = 2 * d >= S
        if d < SUBLANES:
            u = a * _shift_down(u, d, row, 0.0) + u
            if not last:
                a = a * _shift_down(a, d, row, 1.0)
        else:
            u = jnp.concatenate([u[:d], a[d:] * u[:S - d] + u[d:]], axis=0)
            if not last:
                a = jnp.concatenate([a[:d], a[d:] * a[:S - d]], axis=0)
        d *= 2
    return u


def _scan_up(b, g, row):
    d = 1
    while d < S:
        last = 2 * d >= S
        if d < SUBLANES:
            g = g + b * _shift_up(g, d, row, 0.0)
            if not last:
                b = b * _shift_up(b, d, row, 0.0)
        else:
            g = jnp.concatenate([g[:S - d] + b[:S - d] * g[d:], g[S - d:]], axis=0)
            if not last:
                b = jnp.concatenate([b[:S - d] * b[d:], b[S - d:]], axis=0)
        d *= 2
    return g


def _neg_expm1(x):
    series = -x * (1.0 + x * (0.5 + x * (1.0 / 6.0 + x * (1.0 / 24.0))))
    return jnp.where(x > -0.03, series, 1.0 - jnp.exp(x))


def _softplus(x):
    return jnp.maximum(x, 0.0) + jnp.log1p(jnp.exp(-jnp.abs(x)))


def _rnn_gates(x, cw, cb, wa, ba, wx, bx, lam, row):
    xc = cb + cw[3:4, :] * x
    for j in (1, 2, 3):
        xc = xc + cw[3 - j:4 - j, :] * _shift_down(x, j, row, 0.0)
    xcb = xc.astype(BF16)
    r = _sigmoid(_dot(xcb, wa) + ba)
    i = _sigmoid(_dot(xcb, wx) + bx)
    sp = _softplus(-lam)
    log_a = (-LRU_C) * r * sp
    a = jnp.exp(log_a)
    mult = jnp.where(row == 0, 1.0, jnp.sqrt(_neg_expm1(2.0 * log_a)))
    return xc, xcb, r, i, sp, a, mult


def _rnn_fwd(proj3, conv_w, conv_b, wa, ba, wx, bx, lam):
    def body(x_ref, cw_ref, cb_ref, wa_ref, ba_ref, wx_ref, bx_ref, lam_ref, h_ref):
        row = _row_iota()
        x = x_ref[0]
        xc, _, _, i, _, a, mult = _rnn_gates(x, cw_ref[...], cb_ref[...], wa_ref[0], ba_ref[...],
                                             wx_ref[0], bx_ref[...], lam_ref[...], row)
        h_ref[0] = _scan_down(a, mult * (i * xc), row)

    vec = lambda: pl.BlockSpec((1, RBW), lambda b, n: (0, n))
    mat = lambda: pl.BlockSpec((1, RBW, RBW), lambda b, n: (n, 0, 0))
    return pl.pallas_call(
        body, grid=(BL, NRB), name="rnn_fwd",
        in_specs=[pl.BlockSpec((1, S, RBW), lambda b, n: (b, 0, n)),
                  pl.BlockSpec((CONVW, RBW), lambda b, n: (0, n)),
                  vec(), mat(), vec(), mat(), vec(), vec()],
        out_specs=pl.BlockSpec((1, S, RBW), lambda b, n: (b, 0, n)),
        out_shape=jax.ShapeDtypeStruct((BL, S, DR), F32),
        compiler_params=_cp(("parallel", "parallel"), VMEM_MID),
    )(*_hbm(proj3, conv_w, conv_b, wa, ba, wx, bx, lam))


def _rnn_bwd(proj3, h3, dh3, slab_a3, conv_w, conv_b, wa, ba, wx, bx, lam, token):
    def body(x_ref, h_ref, dh_ref, cw_ref, cb_ref, wa_ref, ba_ref, wx_ref, bx_ref, lam_ref, _alias, _token,
             dx_ref, dcw_ref, dcb_ref, dwa_ref, dba_ref, dwx_ref, dbx_ref, dlam_ref):
        row = _row_iota()
        x = x_ref[0]
        cw = cw_ref[...]
        wa_v = wa_ref[0]
        wx_v = wx_ref[0]
        lam_v = lam_ref[...]
        xc, xcb, r, i, sp, a, mult = _rnn_gates(x, cw, cb_ref[...], wa_v, ba_ref[...], wx_v, bx_ref[...], lam_v, row)
        h = h_ref[0]
        g = _scan_up(_shift_up(a, 1, row, 0.0), dh_ref[0], row)
        da = g * _shift_down(h, 1, row, 0.0)
        dmult = jnp.where(row == 0, 0.0, g * (i * xc))
        gm = g * mult
        di = gm * xc
        dxc = gm * i
        dlog_a = da * a - dmult * (a * a) / mult
        dr = dlog_a * ((-LRU_C) * sp)
        dsp = jnp.sum(dlog_a * ((-LRU_C) * r), axis=0, keepdims=True)
        dlam = dsp * (-_sigmoid(-lam_v))
        dpa = dr * r * (1.0 - r)
        dpx = di * i * (1.0 - i)
        dpab = dpa.astype(BF16)
        dpxb = dpx.astype(BF16)
        dwa = _dot_tn(xcb, dpab)
        dwx = _dot_tn(xcb, dpxb)
        dba = jnp.sum(dpa, axis=0, keepdims=True)
        dbx = jnp.sum(dpx, axis=0, keepdims=True)
        dxc = dxc + _dot_nt(dpab, wa_v) + _dot_nt(dpxb, wx_v)
        dcb = jnp.sum(dxc, axis=0, keepdims=True)
        dx = cw[3:4, :] * dxc
        dcw_rows = [None] * CONVW
        dcw_rows[3] = jnp.sum(dxc * x, axis=0, keepdims=True)
        for j in (1, 2, 3):
            dx = dx + cw[3 - j:4 - j, :] * _shift_up(dxc, j, row, 0.0)
            dcw_rows[3 - j] = jnp.sum(dxc * _shift_down(x, j, row, 0.0), axis=0, keepdims=True)
        dx_ref[0] = dx.astype(BF16)
        dcw = jnp.concatenate(dcw_rows, axis=0)
        first = pl.program_id(1) == 0

        @pl.when(first)
        def _():
            dcw_ref[...] = dcw
            dcb_ref[...] = dcb
            dwa_ref[0] = dwa
            dba_ref[...] = dba
            dwx_ref[0] = dwx
            dbx_ref[...] = dbx
            dlam_ref[...] = dlam

        @pl.when(jnp.logical_not(first))
        def _():
            dcw_ref[...] += dcw
            dcb_ref[...] += dcb
            dwa_ref[0] += dwa
            dba_ref[...] += dba
            dwx_ref[0] += dwx
            dbx_ref[...] += dbx
            dlam_ref[...] += dlam

    slab = lambda: pl.BlockSpec((1, S, RBW), lambda n, b: (b, 0, n))
    vec = lambda: pl.BlockSpec((1, RBW), lambda n, b: (0, n))
    mat = lambda: pl.BlockSpec((1, RBW, RBW), lambda n, b: (n, 0, 0))
    taps = lambda: pl.BlockSpec((CONVW, RBW), lambda n, b: (0, n))
    vshape = jax.ShapeDtypeStruct((1, DR), F32)
    mshape = jax.ShapeDtypeStruct((NRB, RBW, RBW), F32)
    return pl.pallas_call(
        body, grid=(NRB, BL), name="rnn_bwd",
        in_specs=[slab(), slab(), slab(), taps(), vec(), mat(), vec(), mat(), vec(), vec(),
                  pl.BlockSpec(memory_space=pl.ANY), pl.BlockSpec((8, LANES), lambda n, b: (0, 0))],
        out_specs=[slab(), taps(), vec(), mat(), vec(), mat(), vec(), vec()],
        out_shape=[jax.ShapeDtypeStruct((BL, S, A_W), BF16), jax.ShapeDtypeStruct((CONVW, DR), F32),
                   vshape, mshape, vshape, mshape, vshape, vshape],
        input_output_aliases={10: 0},
        compiler_params=_cp(("parallel", "arbitrary"), 48 * 1024 * 1024),
    )(*_hbm(proj3, h3, dh3, conv_w, conv_b, wa, ba, wx, bx, lam, slab_a3, token))


NQB = S // QB


def _rms_head(t, gain):
    rstd = lax.rsqrt(jnp.mean(t * t, axis=-1, keepdims=True) + EPS)
    return t * rstd * gain


def _rope(t, cs, sn):
    return t * cs + pltpu.roll(t, HD // 2, 1) * sn


def _rope_t(dy, cs, sn):
    return dy * cs - pltpu.roll(dy, HD // 2, 1) * sn


def _bdot_nt(a, b):
    return lax.dot_general(a, b, (((2,), (2,)), ((0,), (0,))), preferred_element_type=F32)


def _bdot(a, b):
    return lax.dot_general(a, b, (((2,), (1,)), ((0,), (0,))), preferred_element_type=F32)


def _bdot_tn(a, b):
    return lax.dot_general(a, b, (((1,), (1,)), ((0,), (0,))), preferred_element_type=F32)


STRIDE_MAX = 4


def _permute(buf, x, dil, dst, off=0):
    ln = S // dil
    if dil == 1:
        dst[pl.ds(off, S), :] = x.astype(dst.dtype)
        return
    buf[0] = x
    if dil <= STRIDE_MAX:
        for c in range(dil):
            dst[pl.ds(off + c * ln, ln), :] = buf.at[0][pl.ds(c, ln, stride=dil), :].astype(dst.dtype)
        return
    f, r = STRIDE_MAX, dil // STRIDE_MAX
    part = S // f
    for c1 in range(f):
        buf.at[1][pl.ds(c1 * part, part), :] = buf.at[0][pl.ds(c1, part, stride=f), :]
    for c1 in range(f):
        for c2 in range(r):
            dst[pl.ds(off + (c1 + f * c2) * ln, ln), :] = (
                buf.at[1][pl.ds(c1 * part + c2, ln, stride=r), :].astype(dst.dtype))


def _unpermute(buf, xp, dil, dst):
    ln = S // dil
    if dil == 1:
        dst[...] = xp
        return
    if dil <= STRIDE_MAX:
        for c in range(dil):
            dst[pl.ds(c, ln, stride=dil), :] = xp[c * ln:(c + 1) * ln]
        return
    f, r = STRIDE_MAX, dil // STRIDE_MAX
    part = S // f
    for c1 in range(f):
        for c2 in range(r):
            c = c1 + f * c2
            buf.at[1][pl.ds(c1 * part + c2, ln, stride=r), :] = xp[c * ln:(c + 1) * ln]
    for c1 in range(f):
        dst[pl.ds(c1, part, stride=f), :] = buf[1, pl.ds(c1 * part, part), :]


def _blocks3(ref, off=0):
    return ref[pl.ds(off, S), :].reshape(NQB, QB, HD)


def _att_prep(q_ref, k_ref, v_ref, cos_ref, sin_ref, qn, kn, dil, nat, qs, ksp, vsp):
    cs = cos_ref[...]
    sn = sin_ref[...]
    zero = jnp.zeros((QB, HD), BF16)
    ksp[pl.ds(0, QB), :] = zero
    vsp[pl.ds(0, QB), :] = zero
    _permute(nat, _rope(_rms_head(q_ref[0], qn), cs, sn), dil, qs)
    _permute(nat, _rope(_rms_head(k_ref[0], kn), cs, sn), dil, ksp, QB)
    _permute(nat, v_ref[0], dil, vsp, QB)


def _att_scores(qs, ksp, dil):
    nb = S // dil // QB
    q3 = _blocks3(qs)
    shape = (NQB, QB, QB)
    qi = lax.broadcasted_iota(jnp.int32, shape, 1)
    kj = lax.broadcasted_iota(jnp.int32, shape, 2)
    s_c = jnp.where(qi >= kj, _bdot_nt(q3, _blocks3(ksp, QB)) * SCALE, NEG)
    if nb == 1:
        return q3, s_c, None
    jj = lax.broadcasted_iota(jnp.int32, shape, 0)
    ok = (kj >= qi) & ((jj & (nb - 1)) != 0)
    s_p = jnp.where(ok, _bdot_nt(q3, _blocks3(ksp)) * SCALE, NEG)
    return q3, s_c, s_p


def _qkv_spec(kind, g):
    base = OFF_Q // HD + kind * (GW // HD) + g * NH
    return pl.BlockSpec((1, S, HD), lambda b, h: (b, 0, base + h))


def _attn_fwd(proj3, cos_t, sin_t, q_norm, k_norm):
    def body(*refs):
        qkv_refs = refs[:9]
        cos_ref, sin_ref, qn_ref, kn_ref, att_ref, lse_ref, w_ref, nat, qs, ksp, vsp, og = refs[9:]
        for g, (window, dil) in enumerate(PATTERNS):
            q_ref, k_ref, v_ref = qkv_refs[3 * g:3 * g + 3]
            _att_prep(q_ref, k_ref, v_ref, cos_ref, sin_ref, qn_ref[g:g + 1, :], kn_ref[g:g + 1, :], dil,
                      nat, qs, ksp, vsp)
            _, s_c, s_p = _att_scores(qs, ksp, dil)
            m = jnp.max(s_c, axis=-1, keepdims=True)
            if s_p is not None:
                m = jnp.maximum(m, jnp.max(s_p, axis=-1, keepdims=True))
            e_c = jnp.exp(s_c - m)
            den = jnp.sum(e_c, axis=-1, keepdims=True)
            o = _bdot(e_c.astype(BF16), _blocks3(vsp, QB))
            if s_p is not None:
                e_p = jnp.exp(s_p - m)
                den = den + jnp.sum(e_p, axis=-1, keepdims=True)
                o = o + _bdot(e_p.astype(BF16), _blocks3(vsp))
            _unpermute(nat, (o / den).reshape(S, HD), dil, og.at[g])
            _unpermute(nat, jnp.broadcast_to(m + jnp.log(den), (NQB, QB, HD)).reshape(S, HD), dil,
                       lse_ref.at[g, 0])
        l0 = lse_ref[0, 0]
        l1 = lse_ref[1, 0]
        l2 = lse_ref[2, 0]
        mx = jnp.maximum(jnp.maximum(l0, l1), l2)
        e0 = jnp.exp(l0 - mx)
        e1 = jnp.exp(l1 - mx)
        e2 = jnp.exp(l2 - mx)
        inv = 1.0 / (e0 + e1 + e2)
        w0 = e0 * inv
        w1 = e1 * inv
        w2 = e2 * inv
        w_ref[0, 0] = w0
        w_ref[1, 0] = w1
        w_ref[2, 0] = w2
        att_ref[0] = w0 * og[0] + w1 * og[1] + w2 * og[2]

    in_specs = [_qkv_spec(kind, g) for g in range(NG) for kind in range(3)]
    in_specs += [pl.BlockSpec((S, HD), lambda b, h: (0, 0)), pl.BlockSpec((S, HD), lambda b, h: (0, 0)),
                 pl.BlockSpec((NG, HD), lambda b, h: (0, 0)), pl.BlockSpec((NG, HD), lambda b, h: (0, 0))]
    stat = lambda: pl.BlockSpec((NG, 1, S, HD), lambda b, h: (0, b, 0, h))
    return pl.pallas_call(
        body, grid=(BL, NH), name="attn_fwd",
        in_specs=in_specs,
        out_specs=[pl.BlockSpec((1, S, HD), lambda b, h: (b, 0, h)), stat(), stat()],
        out_shape=[jax.ShapeDtypeStruct((BL, S, ATT), F32),
                   jax.ShapeDtypeStruct((NG, BL, S, ATT), F32),
                   jax.ShapeDtypeStruct((NG, BL, S, ATT), F32)],
        scratch_shapes=[pltpu.VMEM((2, S, HD), F32), pltpu.VMEM((S, HD), BF16), pltpu.VMEM((S + QB, HD), BF16),
                        pltpu.VMEM((S + QB, HD), BF16), pltpu.VMEM((NG, S, HD), F32)],
        compiler_params=_cp(("parallel", "parallel"), VMEM_BIG),
    )(*_hbm(*([proj3] * 9), cos_t, sin_t, q_norm, k_norm))


def _attn_bwd_group(g, proj3, cos_t, sin_t, qn_g, kn_g, lse, wts, datt3, sbar3, slabs):
    dil = PATTERNS[g][1]
    n_alias = 0 if slabs is None else 3

    def norm_rope_bwd(dpost, raw, gain, cs, sn):
        dn = _rope_t(dpost, cs, sn)
        rstd = lax.rsqrt(jnp.mean(raw * raw, axis=-1, keepdims=True) + EPS)
        xh = raw * rstd
        dgain = jnp.sum(dn * xh, axis=0, keepdims=True)
        gd = dn * gain
        draw = rstd * (gd - xh * jnp.mean(gd * xh, axis=-1, keepdims=True))
        return draw, dgain

    def body(*refs):
        (q_ref, k_ref, v_ref, cos_ref, sin_ref, qn_ref, kn_ref, lse_ref, w_ref, datt_ref, sbar_ref) = refs[:11]
        (dq_ref, dk_ref, dv_ref, dqn_ref, dkn_ref, nat, qs, ksp, vsp, dos, cvp, lsp, acc) = refs[11 + n_alias:]
        qn = qn_ref[...]
        kn = kn_ref[...]
        cs = cos_ref[...]
        sn = sin_ref[...]
        _att_prep(q_ref, k_ref, v_ref, cos_ref, sin_ref, qn, kn, dil, nat, qs, ksp, vsp)
        wv = w_ref[0, 0]
        _permute(nat, wv * datt_ref[0], dil, dos)
        _permute(nat, wv * sbar_ref[0], dil, cvp)
        _permute(nat, lse_ref[0, 0], dil, lsp)
        q3, s_c, s_p = _att_scores(qs, ksp, dil)
        do3 = _blocks3(dos)
        lse3 = _blocks3(lsp)[:, :, 0:1]
        cv3 = _blocks3(cvp)[:, :, 0:1]
        p_c = jnp.exp(s_c - lse3)
        ds_c = (p_c * (_bdot_nt(do3, _blocks3(vsp, QB)) - cv3)).astype(BF16)
        dq = _bdot(ds_c, _blocks3(ksp, QB))
        acc[0] = _bdot_tn(ds_c, q3).reshape(S, HD)
        acc[1] = _bdot_tn(p_c.astype(BF16), do3).reshape(S, HD)
        if s_p is not None:
            p_p = jnp.exp(s_p - lse3)
            ds_p = (p_p * (_bdot_nt(do3, _blocks3(vsp)) - cv3)).astype(BF16)
            dq = dq + _bdot(ds_p, _blocks3(ksp))
            early = pl.ds(0, S - QB)
            acc[0, early, :] += _bdot_tn(ds_p, q3).reshape(S, HD)[QB:]
            acc[1, early, :] += _bdot_tn(p_p.astype(BF16), do3).reshape(S, HD)[QB:]
        _unpermute(nat, (dq * SCALE).reshape(S, HD), dil, nat.at[0])
        draw, dqn = norm_rope_bwd(nat[0], q_ref[0], qn, cs, sn)
        dq_ref[0] = draw.astype(BF16)
        _unpermute(nat, acc[0] * SCALE, dil, nat.at[0])
        draw, dkn = norm_rope_bwd(nat[0], k_ref[0], kn, cs, sn)
        dk_ref[0] = draw.astype(BF16)
        _unpermute(nat, acc[1], dil, nat.at[0])
        dv_ref[0] = nat[0].astype(BF16)
        first = (pl.program_id(0) == 0) & (pl.program_id(1) == 0)

        @pl.when(first)
        def _():
            dqn_ref[...] = dqn
            dkn_ref[...] = dkn

        @pl.when(jnp.logical_not(first))
        def _():
            dqn_ref[...] += dqn
            dkn_ref[...] += dkn

    full = lambda r: pl.BlockSpec((r, HD), lambda b, h: (0, 0))
    stat = lambda: pl.BlockSpec((1, 1, S, HD), lambda b, h: (g, b, 0, h))
    slab = lambda: pl.BlockSpec((1, S, HD), lambda b, h: (b, 0, h))
    out_slab = lambda: pl.BlockSpec((1, S, HD), lambda b, h: (b, 0, g * NH + h))
    big = jax.ShapeDtypeStruct((BL, S, GW), BF16)
    vecs = jax.ShapeDtypeStruct((1, HD), F32)
    in_specs = [_qkv_spec(0, g), _qkv_spec(1, g), _qkv_spec(2, g), full(S), full(S), full(1), full(1),
                stat(), stat(), slab(), slab()]
    args = [proj3, proj3, proj3, cos_t, sin_t, qn_g, kn_g, lse, wts, datt3, sbar3]
    aliases = {}
    if slabs is not None:
        in_specs += [pl.BlockSpec(memory_space=pl.ANY)] * 3
        args += list(slabs)
        aliases = {11: 0, 12: 1, 13: 2}
    return pl.pallas_call(
        body, grid=(BL, NH), name="attn_bwd_g%d" % g,
        in_specs=in_specs,
        out_specs=[out_slab(), out_slab(), out_slab(), full(1), full(1)],
        out_shape=[big, big, big, vecs, vecs],
        scratch_shapes=[pltpu.VMEM((2, S, HD), F32), pltpu.VMEM((S, HD), BF16), pltpu.VMEM((S + QB, HD), BF16),
                        pltpu.VMEM((S + QB, HD), BF16), pltpu.VMEM((S, HD), BF16), pltpu.VMEM((S, HD), F32),
                        pltpu.VMEM((S, HD), F32), pltpu.VMEM((2, S, HD), F32)],
        input_output_aliases=aliases,
        compiler_params=_cp(("arbitrary", "arbitrary"), VMEM_BIG),
    )(*_hbm(*args))


def _tail(x, proj, h, att, p, tgt, w_o_rnn, w_o_att_t, w_out, w_pg, w_ple_t, norm_ple, b_pg, tm=256):
    nt = T // tm
    inv_d = 1.0 / D

    def body(x_ref, h_ref, zr_ref, att_ref, za_ref, g0a_ref, g0b_ref, g1a_ref, g1b_ref, p_ref, tgt_ref,
             np_ref, bpg_ref, wor_hbm, woa_hbm, wout_hbm, wpg_hbm, wple_hbm,
             dx1_ref, merged_ref, n1_ref, dpre_ref, dpe_ref, dyr_ref, dya_ref, slab_a_ref, slab_c_ref, dh_ref,
             datt_ref, sbar_ref, yrnn_ref, yatt_ref, loss_ref, dnp_ref, dbpg_ref,
             wor, woa, wout, wpg, wple):
        first = pl.program_id(0) == 0

        @pl.when(first)
        def _():
            pltpu.sync_copy(wor_hbm, wor)
            pltpu.sync_copy(woa_hbm, woa)
            pltpu.sync_copy(wout_hbm, wout)
            pltpu.sync_copy(wpg_hbm, wpg)
            pltpu.sync_copy(wple_hbm, wple)

        xv = x_ref[...]
        hv = h_ref[...]
        zr = zr_ref[...]
        av = att_ref[...]
        za = za_ref[...]
        szr = _sigmoid(zr)
        silu_r = zr * szr
        yrnn_b = (hv * silu_r).astype(BF16)
        sza = _sigmoid(za)
        silu_a = za * sza
        yatt_b = (av * silu_a).astype(BF16)
        yrnn_ref[...] = yrnn_b
        yatt_ref[...] = yatt_b
        yr = _dot(yrnn_b, wor[...])
        ya = _dot_nt(yatt_b, woa[...])
        g0 = _sigmoid(jnp.concatenate([g0a_ref[...], g0b_ref[...]], axis=1))
        g1 = _sigmoid(jnp.concatenate([g1a_ref[...], g1b_ref[...]], axis=1))
        merged_b = (g0 * yr + g1 * ya).astype(BF16)
        merged_ref[...] = merged_b
        x1 = xv + _dot(merged_b, wout[...])
        rstd = lax.rsqrt(jnp.mean(x1 * x1, axis=-1, keepdims=True) + EPS)
        xh = x1 * rstd
        npl = np_ref[...]
        n1_b = (xh * npl).astype(BF16)
        n1_ref[...] = n1_b
        pg = _sigmoid(_dot(n1_b, wpg[...]) + bpg_ref[...])
        pe = _dot_nt(p_ref[...].astype(BF16), wple[...])
        err = x1 + pg * pe - tgt_ref[...]
        loss_t = 0.5 * inv_d * jnp.sum(err * err)
        dy = err * inv_d
        dpe_ref[...] = (dy * pg).astype(BF16)
        dpre = dy * pe * pg * (1.0 - pg)
        dpre_b = dpre.astype(BF16)
        dpre_ref[...] = dpre_b
        dn1 = _dot_nt(dpre_b, wpg[...])
        dnp = jnp.sum(dn1 * xh, axis=0, keepdims=True)
        dbpg = jnp.sum(dpre, axis=0, keepdims=True)
        gd = dn1 * npl
        dx1 = dy + rstd * (gd - xh * jnp.mean(gd * xh, axis=-1, keepdims=True))
        dx1_ref[...] = dx1
        dmerged = _dot_nt(dx1.astype(BF16), wout[...])
        dyr_b = (dmerged * g0).astype(BF16)
        dya_b = (dmerged * g1).astype(BF16)
        dyr_ref[...] = dyr_b
        dya_ref[...] = dya_b
        slab_c_ref[:, ATT:ATT + D] = (dmerged * yr * g0 * (1.0 - g0)).astype(BF16)
        slab_c_ref[:, ATT + D:ATT + 2 * D] = (dmerged * ya * g1 * (1.0 - g1)).astype(BF16)
        dyrnn = _dot_nt(dyr_b, wor[...])
        dyatt = _dot(dya_b, woa[...])
        dh_ref[...] = dyrnn * silu_r
        slab_a_ref[...] = (dyrnn * hv * szr * (1.0 + zr * (1.0 - szr))).astype(BF16)
        datt = dyatt * silu_a
        datt_ref[...] = datt
        slab_c_ref[:, 0:ATT] = (dyatt * av * sza * (1.0 + za * (1.0 - sza))).astype(BF16)
        da = datt * av
        for hh in range(NH):
            seg = slice(hh * HD, (hh + 1) * HD)
            sbar_ref[:, seg] = jnp.broadcast_to(jnp.sum(da[:, seg], axis=-1, keepdims=True), (tm, HD))

        @pl.when(first)
        def _():
            loss_ref[...] = jnp.full((8, LANES), loss_t, F32)
            dnp_ref[...] = dnp
            dbpg_ref[...] = dbpg

        @pl.when(jnp.logical_not(first))
        def _():
            loss_ref[...] += jnp.full((8, LANES), loss_t, F32)
            dnp_ref[...] += dnp
            dbpg_ref[...] += dbpg

    tok = lambda w: pl.BlockSpec((tm, w), lambda i: (i, 0))
    col = lambda w, blk: pl.BlockSpec((tm, w), lambda i: (i, blk))
    vec = lambda: pl.BlockSpec((1, D), lambda i: (0, 0))
    hbm = lambda: pl.BlockSpec(memory_space=pl.ANY)
    gb = OFF_G // 512
    in_specs = [tok(D), tok(DR), col(DR, 1), tok(ATT), col(ATT, OFF_ZA // ATT),
                col(512, gb), col(512, gb + 1), col(512, gb + 2), col(512, gb + 3),
                tok(PLE), tok(D), vec(), vec(), hbm(), hbm(), hbm(), hbm(), hbm()]
    sh = lambda w, dt: jax.ShapeDtypeStruct((T, w), dt)
    out_shape = [sh(D, F32), sh(D, BF16), sh(D, BF16), sh(D, BF16), sh(D, BF16), sh(D, BF16), sh(D, BF16),
                 sh(A_W, BF16), sh(C_W, BF16), sh(DR, F32), sh(ATT, F32), sh(ATT, F32),
                 sh(DR, BF16), sh(ATT, BF16),
                 jax.ShapeDtypeStruct((8, LANES), F32), jax.ShapeDtypeStruct((1, D), F32),
                 jax.ShapeDtypeStruct((1, D), F32)]
    out_specs = [tok(D), tok(D), tok(D), tok(D), tok(D), tok(D), tok(D), col(DR, 1), tok(C_W), tok(DR),
                 tok(ATT), tok(ATT), tok(DR), tok(ATT),
                 pl.BlockSpec((8, LANES), lambda i: (0, 0)), vec(), vec()]
    return pl.pallas_call(
        body, grid=(nt,), name="tail_fwd_bwd",
        in_specs=in_specs, out_specs=out_specs, out_shape=out_shape,
        scratch_shapes=[pltpu.VMEM((DR, D), BF16), pltpu.VMEM((D, ATT), BF16), pltpu.VMEM((D, D), BF16),
                        pltpu.VMEM((D, D), BF16), pltpu.VMEM((D, PLE), BF16)],
        compiler_params=_cp(("arbitrary",), VMEM_BIG),
    )(*_hbm(x, h, proj, att, proj, proj, proj, proj, proj, p, tgt, norm_ple, b_pg, w_o_rnn, w_o_att_t, w_out, w_pg,
            w_ple_t))


def _input_norm_bwd(x, dhn, dx1, gain, tm=512):
    def body(x_ref, dhn_ref, dx1_ref, g_ref, dx_ref, dg_ref):
        xv = x_ref[...]
        rstd = lax.rsqrt(jnp.mean(xv * xv, axis=-1, keepdims=True) + EPS)
        xh = xv * rstd
        dn = dhn_ref[...]
        dg = jnp.sum(dn * xh, axis=0, keepdims=True)
        gd = dn * g_ref[...]
        dx_ref[...] = dx1_ref[...] + rstd * (gd - xh * jnp.mean(gd * xh, axis=-1, keepdims=True))
        first = pl.program_id(0) == 0

        @pl.when(first)
        def _():
            dg_ref[...] = dg

        @pl.when(jnp.logical_not(first))
        def _():
            dg_ref[...] += dg

    tok = lambda: pl.BlockSpec((tm, D), lambda i: (i, 0))
    vec = lambda: pl.BlockSpec((1, D), lambda i: (0, 0))
    return pl.pallas_call(
        body, grid=(T // tm,), name="input_norm_bwd",
        in_specs=[tok(), tok(), tok(), vec()], out_specs=[tok(), vec()],
        out_shape=[jax.ShapeDtypeStruct((T, D), F32), jax.ShapeDtypeStruct((1, D), F32)],
        compiler_params=_cp(("arbitrary",), VMEM_MID),
    )(*_hbm(x, dhn, dx1, gain))


def _rope_tables():
    pos = jnp.arange(S, dtype=F32)
    inv_freq = ROPE_THETA ** (-jnp.arange(0, HD, 2, dtype=F32) / HD)
    ang = pos[:, None] * inv_freq[None, :]
    cos, sin = jnp.cos(ang), jnp.sin(ang)
    return jnp.concatenate([cos, cos], axis=1), jnp.concatenate([-sin, sin], axis=1)


def _local_step(x, p, tgt, w_in_t, other_weights, norm_mix, b_in, conv_b,
                w_rg_a, b_rg_a, w_rg_x, b_rg_x, lam, q_norm, k_norm, norm_ple, b_pg, start_reduce=None,
                entry_token=None):
    if start_reduce is None:
        start_reduce = lambda arrs, tag: (jnp.zeros((8, LANES), F32), arrs)
    if entry_token is None:
        entry_token = jnp.zeros((8, LANES), F32)
    cos_t, sin_t = _rope_tables()
    wa_b = w_rg_a.astype(BF16)
    wx_b = w_rg_x.astype(BF16)

    hn = _rmsnorm_fwd(x, norm_mix, entry_token)
    proj = _in_proj(hn, w_in_t, b_in)
    w_o_rnn, w_o_att_t, w_out, w_pg, w_ple_t, conv_w = other_weights(proj)
    proj3 = proj.reshape(BL, S, NIN)
    h3 = _rnn_fwd(proj3, conv_w, conv_b, wa_b, b_rg_a, wx_b, b_rg_x, lam)
    att3, lse, wts = _attn_fwd(proj3, cos_t, sin_t, q_norm, k_norm)
    (dx1, merged, n1, dpre, dpe, dyr, dya, slab_a, slab_c, dh, datt, sbar, yrnn, yatt, loss8, dnp, dbpg) = _tail(
        x, proj, h3.reshape(T, DR), att3.reshape(T, ATT), p, tgt, w_o_rnn, w_o_att_t, w_out, w_pg, w_ple_t,
        norm_ple, b_pg)

    token, pending_out = start_reduce([
        _mm_tn(yrnn, dyr, 640, 2048, "dw_o_rnn"),
        _mm_tn(dya, yatt, 512, 2048, "dw_o_att_t"),
        _mm_tn(merged, dx1, 512, 2048, "dw_out"),
        _mm_tn(n1, dpre, 512, 2048, "dw_ple_gate"),
        _mm_tn(dpe, p, 512, 2048, "dw_ple_t")], "out")

    slab_a3, dcw, dcb, dwa, dba, dwx, dbx, dlam = _rnn_bwd(
        proj3, h3, dh.reshape(BL, S, DR), slab_a.reshape(BL, S, A_W), conv_w, conv_b, wa_b, b_rg_a, wx_b, b_rg_x, lam,
        token)
    datt3 = datt.reshape(BL, S, ATT)
    sbar3 = sbar.reshape(BL, S, ATT)
    slabs = None
    dqn = []
    dkn = []
    for g in range(NG):
        dq, dk, dv, dqn_g, dkn_g = _attn_bwd_group(g, proj3, cos_t, sin_t, q_norm[g:g + 1], k_norm[g:g + 1],
                                                   lse, wts, datt3, sbar3, slabs)
        slabs = (dq, dk, dv)
        dqn.append(dqn_g)
        dkn.append(dkn_g)
    pieces = [slab_a3.reshape(T, A_W)] + [t.reshape(T, GW) for t in slabs] + [slab_c]
    dw_in_t, db_in = _dw_in(pieces, hn)
    token, pending_in = start_reduce([dw_in_t], "in")
    dhn = _dhn(pieces, w_in_t, token)
    grad_x, dnm = _input_norm_bwd(x, dhn, dx1, norm_mix)

    small = dict(w_rg_a=dwa, w_rg_x=dwx, norm_mix=dnm, b_in=db_in, conv_b=dcb, b_rg_a=dba, b_rg_x=dbx,
                 lru_lambda=dlam, q_norm=dqn, k_norm=dkn, norm_ple=dnp, b_ple_gate=dbpg, conv_w=dcw)
    return loss8[0, 0], grad_x, pending_out, pending_in, small


MESH = pl.DeviceIdType.MESH
HBM_SPEC = pl.BlockSpec(memory_space=pl.ANY)


def _my_pos():
    return lax.axis_index("x"), lax.axis_index("y"), lax.axis_index("c")


def _flip(pos, k):
    x, y, c = pos
    return (1 - x if k & 4 else x, 1 - y if k & 2 else y, 1 - c if k & 1 else c)


def _lin(pos):
    return 4 * pos[0] + 2 * pos[1] + pos[2]


def _chip(pos):
    return 2 * pos[0] + pos[1]


def _all_gather_two_level(shards, name):
    na = len(shards)

    def body(*refs):
        x_refs = refs[:na]
        out_refs = refs[na:2 * na]
        send_sems, recv_sems, local_sems = refs[2 * na:]
        me = _my_pos()
        sibling = _flip(me, 1)
        chips = [_flip(me, 4), _flip(me, 2), _flip(me, 6)]

        def copy(i, k, block, to, from_x=False):
            dst = out_refs[i].at[_lin(block)]
            return pltpu.make_async_remote_copy(
                src_ref=x_refs[i] if from_x else dst, dst_ref=dst,
                send_sem=send_sems.at[7 * i + k], recv_sem=recv_sems.at[7 * i + k], device_id=to, device_id_type=MESH)

        started = []
        for i in range(na):
            mine = pltpu.make_async_copy(x_refs[i], out_refs[i].at[_lin(me)], local_sems.at[i])
            mine.start()
            started.append(mine)
        sends = []
        for i in range(na):
            cps = [copy(i, 0, me, sibling, True)] + [copy(i, 1 + j, me, chip, True) for j, chip in enumerate(chips)]
            for cp in cps:
                cp.start()
            sends += cps
        for i in range(na):
            for j, chip in enumerate(chips):
                copy(i, 1 + j, chip, me).wait_recv()
                fwd = copy(i, 4 + j, chip, sibling)
                fwd.start()
                sends.append(fwd)
        for i in range(na):
            copy(i, 0, sibling, me).wait_recv()
            for j, chip in enumerate(chips):
                copy(i, 4 + j, _flip(chip, 1), me).wait_recv()
        for cp in sends:
            cp.wait_send()
        for mine in started:
            mine.wait()

    return pl.pallas_call(
        body, name=name,
        out_shape=[jax.ShapeDtypeStruct((NDEV,) + s.shape, s.dtype) for s in shards],
        in_specs=[HBM_SPEC] * na, out_specs=[HBM_SPEC] * na,
        scratch_shapes=[pltpu.SemaphoreType.DMA((7 * na,)), pltpu.SemaphoreType.DMA((7 * na,)),
                        pltpu.SemaphoreType.DMA((na,))],
    )(*shards)


def _all_gather_direct(shard, name):
    def body(x_ref, out_ref, send_sems, recv_sems, local_sem):
        me = _my_pos()
        mine = pltpu.make_async_copy(x_ref, out_ref.at[_lin(me)], local_sem)
        mine.start()
        sends = []
        for k in range(1, NDEV):
            cp = pltpu.make_async_remote_copy(
                src_ref=x_ref, dst_ref=out_ref.at[_lin(me)], send_sem=send_sems.at[k - 1],
                recv_sem=recv_sems.at[k - 1], device_id=_flip(me, k), device_id_type=MESH)
            cp.start()
            sends.append(cp)
        for k in range(1, NDEV):
            peer = _flip(me, k)
            pltpu.make_async_remote_copy(
                src_ref=x_ref, dst_ref=out_ref.at[_lin(peer)], send_sem=send_sems.at[k - 1],
                recv_sem=recv_sems.at[k - 1], device_id=peer, device_id_type=MESH).wait_recv()
        for cp in sends:
            cp.wait_send()
        mine.wait()

    return pl.pallas_call(
        body, name=name,
        out_shape=jax.ShapeDtypeStruct((NDEV,) + shard.shape, shard.dtype),
        in_specs=[HBM_SPEC], out_specs=HBM_SPEC,
        scratch_shapes=[pltpu.SemaphoreType.DMA((7,)), pltpu.SemaphoreType.DMA((7,)), pltpu.SemaphoreType.DMA],
    )(shard)


def _exchange_within_chip(parts, name):
    na = len(parts)

    def body(*refs):
        a_refs = refs[:na]
        recv_refs = refs[na:2 * na]
        send_sems, recv_sems = refs[2 * na:]
        me = _my_pos()
        c = me[2]
        sibling = _flip(me, 1)
        remote = []
        for i in range(na):
            for q in range(NCHIP):
                rc = pltpu.make_async_remote_copy(
                    src_ref=a_refs[i].at[q, 1 - c], dst_ref=recv_refs[i].at[q],
                    send_sem=send_sems.at[NCHIP * i + q], recv_sem=recv_sems.at[NCHIP * i + q],
                    device_id=sibling, device_id_type=MESH)
                rc.start()
                remote.append(rc)
        for rc in remote:
            rc.wait_recv()
        for rc in remote:
            rc.wait_send()

    return pl.pallas_call(
        body, name=name, out_shape=[jax.ShapeDtypeStruct((NCHIP,) + a.shape[2:], a.dtype) for a in parts],
        in_specs=[HBM_SPEC] * na, out_specs=[HBM_SPEC] * na,
        scratch_shapes=[pltpu.SemaphoreType.DMA((NCHIP * na,)), pltpu.SemaphoreType.DMA((NCHIP * na,))],
    )(*parts)


HBM_ONLY = pl.BlockSpec(memory_space=pltpu.HBM)
SEM_SPEC = pl.BlockSpec(memory_space=pltpu.SEMAPHORE)
SPLIT_COPY = pltpu.CompilerParams(has_side_effects=pltpu.SideEffectType.DATAFLOW_SIDE_EFFECTING)


def _chip_peers(me):
    return [_flip(me, 4), _flip(me, 2), _flip(me, 6)]


def _between_chips_start(parts, name):
    na = len(parts)

    def body(*refs):
        a_refs = refs[:na]
        land_refs = refs[na:2 * na]
        send_sems, recv_sems = refs[2 * na], refs[2 * na + 1]
        token = refs[-1]
        me = _my_pos()
        myq = _chip(me)
        for i in range(na):
            for j, peer in enumerate(_chip_peers(me)):
                pltpu.make_async_remote_copy(
                    src_ref=a_refs[i].at[_chip(peer)], dst_ref=land_refs[i].at[myq],
                    send_sem=send_sems.at[3 * i + j], recv_sem=recv_sems.at[3 * i + j],
                    device_id=peer, device_id_type=MESH).start()
        token[...] = jnp.zeros_like(token)

    hbm = [pltpu.HBM(a.shape, a.dtype) for a in parts]
    srcs = [pltpu.with_memory_space_constraint(a, pltpu.HBM) for a in parts]
    lands = [pltpu.with_memory_space_constraint(lax.empty(a.shape, a.dtype), pltpu.HBM) for a in parts]
    res = pl.pallas_call(
        body, name=name,
        out_shape=(pltpu.SemaphoreType.DMA((3 * na,)), pltpu.SemaphoreType.DMA((3 * na,)), *hbm, *hbm,
                   jax.ShapeDtypeStruct((8, LANES), F32)),
        in_specs=[HBM_ONLY] * (2 * na),
        out_specs=(SEM_SPEC, SEM_SPEC, *([HBM_ONLY] * (2 * na)), pl.BlockSpec(memory_space=pltpu.VMEM)),
        input_output_aliases={i: 2 + i for i in range(2 * na)},
        compiler_params=SPLIT_COPY,
    )(*srcs, *lands)
    return res[-1], (res[0], res[1], list(res[2:2 + na]), list(res[2 + na:2 + 2 * na]))


def _between_chips_wait(pending, after, name):
    send_sems, recv_sems, parts, lands = pending
    na = len(parts)

    def body(*refs):
        a_refs = refs[:na]
        land_refs = refs[na:2 * na]
        send_sems, recv_sems = refs[2 * na], refs[2 * na + 1]
        me = _my_pos()
        for i in range(na):
            for j, peer in enumerate(_chip_peers(me)):
                cp = pltpu.make_async_remote_copy(
                    src_ref=a_refs[i].at[_chip(peer)], dst_ref=land_refs[i].at[_chip(peer)],
                    send_sem=send_sems.at[3 * i + j], recv_sem=recv_sems.at[3 * i + j],
                    device_id=peer, device_id_type=MESH)
                cp.wait_send()
                cp.wait_recv()

    hbm = [pltpu.HBM(a.shape, a.dtype) for a in parts]
    res = pl.pallas_call(
        body, name=name, out_shape=(*hbm, *hbm),
        in_specs=[HBM_ONLY] * (2 * na) + [SEM_SPEC, SEM_SPEC, pl.BlockSpec(memory_space=pl.ANY)],
        out_specs=[HBM_ONLY] * (2 * na),
        input_output_aliases={i: i for i in range(2 * na)},
        compiler_params=SPLIT_COPY,
    )(*parts, *lands, send_sems, recv_sems, after)
    return list(res[:na]), list(res[na:])


def _gather_start(shards, after, name):
    na = len(shards)

    def body(*refs):
        x_refs = refs[:na]
        land_refs = refs[na:2 * na]
        send_sems, recv_sems = refs[2 * na + 1], refs[2 * na + 2]
        token = refs[-1]
        me = _my_pos()
        for i in range(na):
            for k in range(1, NDEV):
                pltpu.make_async_remote_copy(
                    src_ref=x_refs[i], dst_ref=land_refs[i].at[_lin(me)],
                    send_sem=send_sems.at[7 * i + k - 1], recv_sem=recv_sems.at[7 * i + k - 1],
                    device_id=_flip(me, k), device_id_type=MESH).start()
        token[...] = jnp.zeros_like(token)

    src_t = [pltpu.HBM(a.shape, a.dtype) for a in shards]
    land_t = [pltpu.HBM((NDEV,) + a.shape, a.dtype) for a in shards]
    srcs = [pltpu.with_memory_space_constraint(a, pltpu.HBM) for a in shards]
    lands = [pltpu.with_memory_space_constraint(lax.empty((NDEV,) + a.shape, a.dtype), pltpu.HBM) for a in shards]
    res = pl.pallas_call(
        body, name=name,
        out_shape=(pltpu.SemaphoreType.DMA((7 * na,)), pltpu.SemaphoreType.DMA((7 * na,)), *src_t, *land_t,
                   jax.ShapeDtypeStruct((8, LANES), F32)),
        in_specs=[HBM_ONLY] * (2 * na) + [pl.BlockSpec(memory_space=pl.ANY)],
        out_specs=(SEM_SPEC, SEM_SPEC, *([HBM_ONLY] * (2 * na)), pl.BlockSpec(memory_space=pltpu.VMEM)),
        input_output_aliases={i: 2 + i for i in range(2 * na)},
        compiler_params=SPLIT_COPY,
    )(*srcs, *lands, after)
    return res[-1], (res[0], res[1], list(res[2:2 + na]), list(res[2 + na:2 + 2 * na]))


def _gather_wait(pending, after, name):
    send_sems, recv_sems, shards, lands = pending
    na = len(shards)

    def body(*refs):
        x_refs = refs[:na]
        land_refs = refs[na:2 * na]
        send_sems, recv_sems = refs[2 * na], refs[2 * na + 1]
        me = _my_pos()
        for i in range(na):
            for k in range(1, NDEV):
                peer = _flip(me, k)
                cp = pltpu.make_async_remote_copy(
                    src_ref=x_refs[i], dst_ref=land_refs[i].at[_lin(peer)],
                    send_sem=send_sems.at[7 * i + k - 1], recv_sem=recv_sems.at[7 * i + k - 1],
                    device_id=peer, device_id_type=MESH)
                cp.wait_send()
                cp.wait_recv()

    src_t = [pltpu.HBM(a.shape, a.dtype) for a in shards]
    land_t = [pltpu.HBM(a.shape, a.dtype) for a in lands]
    res = pl.pallas_call(
        body, name=name, out_shape=(*src_t, *land_t),
        in_specs=[HBM_ONLY] * (2 * na) + [SEM_SPEC, SEM_SPEC, pl.BlockSpec(memory_space=pl.ANY)],
        out_specs=[HBM_ONLY] * (2 * na),
        input_output_aliases={i: i for i in range(2 * na)},
        compiler_params=SPLIT_COPY,
    )(*shards, *lands, send_sems, recv_sems, after)
    return list(res[na:])


def _exchange_all(parts, name):
    na = len(parts)

    def body(*refs):
        a_refs = refs[:na]
        out_refs = refs[na:2 * na]
        send_sems, recv_sems = refs[2 * na:]
        me = _my_pos()
        sends = []
        for i in range(na):
            for k in range(1, NDEV):
                peer = _flip(me, k)
                cp = pltpu.make_async_remote_copy(
                    src_ref=a_refs[i].at[_lin(peer)], dst_ref=out_refs[i].at[_lin(me)],
                    send_sem=send_sems.at[7 * i + k - 1], recv_sem=recv_sems.at[7 * i + k - 1],
                    device_id=peer, device_id_type=MESH)
                cp.start()
                sends.append(cp)
        for i in range(na):
            for k in range(1, NDEV):
                peer = _flip(me, k)
                pltpu.make_async_remote_copy(
                    src_ref=a_refs[i].at[_lin(peer)], dst_ref=out_refs[i].at[_lin(peer)],
                    send_sem=send_sems.at[7 * i + k - 1], recv_sem=recv_sems.at[7 * i + k - 1],
                    device_id=peer, device_id_type=MESH).wait_recv()
        for cp in sends:
            cp.wait_send()

    return pl.pallas_call(
        body, name=name, out_shape=[jax.ShapeDtypeStruct(a.shape, a.dtype) for a in parts],
        in_specs=[HBM_SPEC] * na, out_specs=[HBM_SPEC] * na,
        scratch_shapes=[pltpu.SemaphoreType.DMA((7 * na,)), pltpu.SemaphoreType.DMA((7 * na,))],
    )(*parts)


def _scalar(v):
    return jnp.asarray(v, jnp.int32).reshape(1)


def _sum_pairs(parts, theirs, name):
    na = len(parts)

    def body(c_ref, *refs):
        for i in range(na):
            o_ref = refs[2 * na + i]
            o_ref[0] = (refs[i][0, 0].astype(F32) + refs[na + i][0].astype(F32)).astype(o_ref.dtype)

    def mine_spec(a):
        return pl.BlockSpec((1, 1) + a.shape[2:], lambda q, c_ref: (q, c_ref[0], 0, 0))

    def spec(a):
        return pl.BlockSpec((1,) + a.shape[1:], lambda q, c_ref: (q, 0, 0))

    return pl.pallas_call(
        body, name=name,
        grid_spec=pltpu.PrefetchScalarGridSpec(
            num_scalar_prefetch=1, grid=(NCHIP,),
            in_specs=[mine_spec(a) for a in parts] + [spec(a) for a in theirs],
            out_specs=[spec(a) for a in theirs]),
        out_shape=[jax.ShapeDtypeStruct(a.shape, a.dtype) for a in theirs],
        compiler_params=_cp(("arbitrary",), VMEM_BIG),
    )(_scalar(lax.axis_index("c")), *parts, *theirs)


def _others(q, mine, nblk=NCHIP):
    return jnp.where(q == mine, (q + 1) % nblk, q)


def _sum_chips_adamw(own, recv, wv, mv, vv, tr, name):
    _, r, w = recv.shape

    def body(q_ref, own_ref, r0, r1, r2, r3, w_ref, m_ref, v_ref, g_ref, d_ref, m2_ref, v2_ref):
        myq = q_ref[0]
        acc = None
        for q, r_ref in enumerate((r0, r1, r2, r3)):
            term = jnp.where(myq == q, own_ref[0], r_ref[0]).astype(F32)
            acc = term if acc is None else acc + term
        g_ref[...] = acc
        delta, m2, v2 = _adam_math(w_ref[...], acc, m_ref[...], v_ref[...])
        d_ref[...] = delta
        m2_ref[...] = m2
        v2_ref[...] = v2

    def recv_spec(q):
        return pl.BlockSpec((1, tr, w), lambda i, q_ref: (_others(q, q_ref[0]), i, 0))

    rows = lambda: pl.BlockSpec((tr, w), lambda i, q_ref: (i, 0))
    shp = jax.ShapeDtypeStruct((r, w), F32)
    return pl.pallas_call(
        body, name=name,
        grid_spec=pltpu.PrefetchScalarGridSpec(
            num_scalar_prefetch=1, grid=(r // tr,),
            in_specs=[pl.BlockSpec((1, tr, w), lambda i, q_ref: (q_ref[0], i, 0))]
            + [recv_spec(q) for q in range(NCHIP)] + [rows(), rows(), rows()],
            out_specs=[rows(), rows(), rows(), rows()]),
        out_shape=[shp, shp, shp, shp],
        compiler_params=_cp(("arbitrary",), VMEM_MID),
    )(_scalar(_chip(_my_pos())), *_hbm(own, recv, recv, recv, recv, wv, mv, vv))


def _sum_blocks_small(own, recv, mine, transpose, name):
    na = len(recv)
    nblk = recv[0].shape[0]

    def body(q_ref, *refs):
        me = q_ref[0]
        for i in range(na):
            acc = None
            for q in range(nblk):
                term = jnp.where(me == q, refs[i][0], refs[na * (1 + q) + i][0]).astype(F32)
                acc = term if acc is None else acc + term
            refs[na * (1 + nblk) + i][...] = acc.T if transpose[i] else acc

    def oshape(a, tr):
        r, w = a.shape[1:]
        return (w, r) if tr else (r, w)

    own_spec = lambda a: pl.BlockSpec((1,) + a.shape[1:], lambda s, q_ref: (q_ref[0], 0, 0))
    recv_spec = lambda a, q: pl.BlockSpec((1,) + a.shape[1:], lambda s, q_ref: (_others(q, q_ref[0], nblk), 0, 0))
    out_spec = lambda shp: pl.BlockSpec(shp, lambda s, q_ref: (0, 0))
    in_specs = [own_spec(a) for a in own]
    for q in range(nblk):
        in_specs += [recv_spec(a, q) for a in recv]
    return pl.pallas_call(
        body, name=name,
        grid_spec=pltpu.PrefetchScalarGridSpec(
            num_scalar_prefetch=1, grid=(1,), in_specs=in_specs,
            out_specs=[out_spec(oshape(a, tr)) for a, tr in zip(recv, transpose)]),
        out_shape=[jax.ShapeDtypeStruct(oshape(a, tr), F32) for a, tr in zip(recv, transpose)],
        compiler_params=_cp(("arbitrary",), VMEM_MID),
    )(_scalar(mine), *own, *(list(recv) * nblk))


def _rep_offsets():
    offs = []
    o = 0
    for r in REP_ROWS:
        offs.append(o)
        o += r
    return offs


def _pack_small_grads(g):
    offs = _rep_offsets()

    def body(dwa, dwx, dnm, dbin, dcb, dba, dbx, dlam, dq0, dq1, dq2, dk0, dk1, dk2, dnp, dbpg, o_ref):
        o_ref[pl.ds(REP_TOTAL_ROWS - 2, NDEV * REP_ROWS_DEV - REP_TOTAL_ROWS + 2), :] = jnp.zeros(
            (NDEV * REP_ROWS_DEV - REP_TOTAL_ROWS + 2, LANES), F32)
        for n in range(NRB):
            o_ref[pl.ds(offs[0] + n * RBW, RBW), :] = dwa[n]
            o_ref[pl.ds(offs[1] + n * RBW, RBW), :] = dwx[n]

        def put_vec(off, ref, rows):
            for k in range(rows):
                o_ref[pl.ds(off + k, 1), :] = ref[:, k * LANES:(k + 1) * LANES]

        put_vec(offs[2], dnm, REP_ROWS[2])
        put_vec(offs[3], dbin, REP_ROWS[3])
        put_vec(offs[4], dcb, REP_ROWS[4])
        put_vec(offs[5], dba, REP_ROWS[5])
        put_vec(offs[6], dbx, REP_ROWS[6])
        put_vec(offs[7], dlam, REP_ROWS[7])
        for k, ref in enumerate((dq0, dq1, dq2)):
            o_ref[pl.ds(offs[8] + k, 1), :] = ref[...]
        for k, ref in enumerate((dk0, dk1, dk2)):
            o_ref[pl.ds(offs[9] + k, 1), :] = ref[...]
        put_vec(offs[10], dnp, REP_ROWS[10])
        put_vec(offs[11], dbpg, REP_ROWS[11])

    args = [g["w_rg_a"], g["w_rg_x"], g["norm_mix"], g["b_in"], g["conv_b"], g["b_rg_a"], g["b_rg_x"],
            g["lru_lambda"], *g["q_norm"], *g["k_norm"], g["norm_ple"], g["b_ple_gate"]]
    full = lambda shp: pl.BlockSpec(shp, lambda: (0,) * len(shp))
    return pl.pallas_call(
        body, name="pack_small_grads",
        in_specs=[full(a.shape) for a in args],
        out_specs=full((NDEV * REP_ROWS_DEV, LANES)),
        out_shape=jax.ShapeDtypeStruct((NDEV * REP_ROWS_DEV, LANES), F32),
    )(*_hbm(*args))


def _adam_math(wv, gv, mv, vv):
    c1 = 1.0 - B1 ** STEP
    c2 = 1.0 - B2 ** STEP
    m2 = B1 * mv + (1.0 - B1) * gv
    v2 = B2 * vv + (1.0 - B2) * (gv * gv)
    delta = (-LR) * ((m2 / c1) / (jnp.sqrt(v2 / c2) + AEPS) + WD * wv)
    return delta, m2, v2


def _adamw_small(rep_flat, w, m, v):
    offs = _rep_offsets()
    n = len(REP_NAMES)

    def body(*refs):
        g_ref = refs[0]
        w_refs = refs[1:1 + n]
        m_refs = refs[1 + n:1 + 2 * n]
        v_refs = refs[1 + 2 * n:1 + 3 * n]
        outs = refs[1 + 3 * n:]
        go, do, mo, vo = outs[:n], outs[n:2 * n], outs[2 * n:3 * n], outs[3 * n:]

        def emit(i, idx, gv):
            go[i][idx] = gv
            delta, m2, v2 = _adam_math(w_refs[i][idx], gv, m_refs[i][idx], v_refs[i][idx])
            do[i][idx] = delta
            mo[i][idx] = m2
            vo[i][idx] = v2

        for i in range(n):
            if i < 2:
                for b in range(NRB):
                    emit(i, b, g_ref[pl.ds(offs[i] + b * RBW, RBW), :])
            elif REP_NAMES[i] in ("q_norm", "k_norm"):
                emit(i, slice(None), g_ref[pl.ds(offs[i], NG), :])
            else:
                gv = jnp.concatenate([g_ref[pl.ds(offs[i] + k, 1), :] for k in range(REP_ROWS[i])], axis=1)
                emit(i, slice(None), gv)

    full = lambda shp: pl.BlockSpec(shp, lambda: (0,) * len(shp))
    pspecs = [full(a.shape) for a in w]
    pshapes = [jax.ShapeDtypeStruct(a.shape, F32) for a in w]
    res = pl.pallas_call(
        body, name="adamw_small",
        in_specs=[full(rep_flat.shape)] + pspecs * 3,
        out_specs=pspecs * 4, out_shape=pshapes * 4,
        compiler_params=_cp(None, VMEM_MID),
    )(*_hbm(rep_flat, *w, *m, *v))
    return res[:n], res[n:2 * n], res[2 * n:3 * n], res[3 * n:]


def _adamw_many(w, g, m, v):
    n = len(w)

    def body(*refs):
        for i in range(n):
            delta, m2, v2 = _adam_math(refs[i][...], refs[n + i][...], refs[2 * n + i][...], refs[3 * n + i][...])
            refs[4 * n + i][...] = delta
            refs[5 * n + i][...] = m2
            refs[6 * n + i][...] = v2

    full = lambda shp: pl.BlockSpec(shp, lambda: (0,) * len(shp))
    specs = [full(a.shape) for a in w]
    shapes = [jax.ShapeDtypeStruct(a.shape, F32) for a in w]
    res = pl.pallas_call(
        body, name="adamw_shards",
        in_specs=specs * 4, out_specs=specs * 3, out_shape=shapes * 3,
        compiler_params=_cp(None, VMEM_MID),
    )(*_hbm(*w, *g, *m, *v))
    return res[:n], res[n:2 * n], res[2 * n:]


def kernel(x, p, norm_mix, w_in, b_in, conv_w, conv_b, w_rg_a, b_rg_a, w_rg_x, b_rg_x, lru_lambda, q_norm, k_norm, w_o_rnn, w_o_att, w_out, norm_ple, w_ple_gate, b_ple_gate, w_ple, loss_target, m_norm_mix, m_w_in, m_b_in, m_conv_w, m_conv_b, m_w_rg_a, m_b_rg_a, m_w_rg_x, m_b_rg_x, m_lru_lambda, m_q_norm, m_k_norm, m_w_o_rnn, m_w_o_att, m_w_out, m_norm_ple, m_w_ple_gate, m_b_ple_gate, m_w_ple, v_norm_mix, v_w_in, v_b_in, v_conv_w, v_conv_b, v_w_rg_a, v_b_rg_a, v_w_rg_x, v_b_rg_x, v_lru_lambda, v_q_norm, v_k_norm, v_w_o_rnn, v_w_o_att, v_w_out, v_norm_ple, v_w_ple_gate, v_b_ple_gate, v_w_ple):
    w = dict(norm_mix=norm_mix, w_in=w_in, b_in=b_in, conv_w=conv_w, conv_b=conv_b, w_rg_a=w_rg_a, b_rg_a=b_rg_a,
             w_rg_x=w_rg_x, b_rg_x=b_rg_x, lru_lambda=lru_lambda, q_norm=q_norm, k_norm=k_norm, w_o_rnn=w_o_rnn,
             w_o_att=w_o_att, w_out=w_out, norm_ple=norm_ple, w_ple_gate=w_ple_gate, b_ple_gate=b_ple_gate,
             w_ple=w_ple)
    m = dict(norm_mix=m_norm_mix, w_in=m_w_in, b_in=m_b_in, conv_w=m_conv_w, conv_b=m_conv_b, w_rg_a=m_w_rg_a,
             b_rg_a=m_b_rg_a, w_rg_x=m_w_rg_x, b_rg_x=m_b_rg_x, lru_lambda=m_lru_lambda, q_norm=m_q_norm,
             k_norm=m_k_norm, w_o_rnn=m_w_o_rnn, w_o_att=m_w_o_att, w_out=m_w_out, norm_ple=m_norm_ple,
             w_ple_gate=m_w_ple_gate, b_ple_gate=m_b_ple_gate, w_ple=m_w_ple)
    v = dict(norm_mix=v_norm_mix, w_in=v_w_in, b_in=v_b_in, conv_w=v_conv_w, conv_b=v_conv_b, w_rg_a=v_w_rg_a,
             b_rg_a=v_b_rg_a, w_rg_x=v_w_rg_x, b_rg_x=v_b_rg_x, lru_lambda=v_lru_lambda, q_norm=v_q_norm,
             k_norm=v_k_norm, w_o_rnn=v_w_o_rnn, w_o_att=v_w_o_att, w_out=v_w_out, norm_ple=v_norm_ple,
             w_ple_gate=v_w_ple_gate, b_ple_gate=v_b_ple_gate, w_ple=v_w_ple)
    names = list(w.keys())

    shards = [w_in[0].T.astype(BF16), w_o_rnn[0].astype(BF16), w_o_att[0].T.astype(BF16), w_out[0].astype(BF16),
              w_ple_gate[0].astype(BF16), w_ple[0].T.astype(BF16), conv_w[0]]
    w_in_g = _all_gather_two_level(shards[:1], "gather_w_in")[0]
    w_in_t = w_in_g.reshape(NIN, D)
    entry_token, gather_pending = _gather_start(shards[1:], w_in_g, "gather_others_start")
    me = _lin(_my_pos())

    def other_weights(after):
        landed = _gather_wait(gather_pending, after, "gather_others_wait")
        full = [lax.dynamic_update_slice(a, s[None], (me, 0, 0)) for a, s in zip(landed, shards[1:])]
        conv_f = full[5].transpose(1, 0, 2).reshape(CONVW, DR)
        return (*[a.reshape((NDEV * a.shape[1], a.shape[2])) for a in full[:5]], conv_f)

    def start_reduce(arrs, tag):
        parts = [a.reshape((NCHIP, 2, a.shape[0] // NDEV, a.shape[1])) for a in arrs]
        theirs = _exchange_within_chip(parts, "reduce_within_chip_" + tag)
        return _between_chips_start(_sum_pairs(parts, theirs, "sum_pairs_" + tag), "reduce_between_chips_start_" + tag)

    loss_part, grad_x, pending_out, pending_in, small = _local_step(
        x.reshape(T, D), p.reshape(T, PLE), loss_target.reshape(T, D),
        w_in_t, other_weights,
        norm_mix, b_in, conv_b, w_rg_a[0], b_rg_a, w_rg_x[0], b_rg_x, lru_lambda, q_norm[0], k_norm[0],
        norm_ple, b_ple_gate, start_reduce, entry_token)
    loss = lax.psum(loss_part, ("x", "y", "c"))

    rep_parts = _pack_small_grads(small).reshape(NDEV, REP_ROWS_DEV, LANES)
    conv_parts = small["conv_w"].reshape(CONVW, NDEV, DR // NDEV).transpose(1, 0, 2)
    small_parts = [rep_parts, conv_parts]
    g_rep, g_conv = _sum_blocks_small(small_parts, _exchange_all(small_parts, "reduce_small"), me, (False, False),
                                      "sum_small")
    rep_all = _all_gather_direct(g_rep, "gather_small").reshape(NDEV * REP_ROWS_DEV, LANES)

    myq = _chip(_my_pos())
    own_out, recv_out = _between_chips_wait(pending_out, rep_all, "reduce_between_chips_wait_out")
    own_in, recv_in = _between_chips_wait(pending_in, rep_all, "reduce_between_chips_wait_in")
    w_in_res = _sum_chips_adamw(own_in[0], recv_in[0], w_in[0].T, m_w_in[0].T, v_w_in[0].T, 304, "adamw_w_in")
    g_o_rnn, g_o_att, g_out, g_pg, g_ple = _sum_blocks_small(
        own_out, recv_out, myq, (False, True, False, False, True), "sum_chips_out")

    grad, delta, new_m, new_v = {}, {}, {}, {}
    rep_shape = lambda a: a if a.ndim == 2 else a.reshape(a.shape[1:])
    res = _adamw_small(rep_all, [rep_shape(w[n]) for n in REP_NAMES], [rep_shape(m[n]) for n in REP_NAMES],
                       [rep_shape(v[n]) for n in REP_NAMES])
    for dst, vals in zip((grad, delta, new_m, new_v), res):
        for n, a in zip(REP_NAMES, vals):
            dst[n] = a.reshape(w[n].shape)
    grad["w_in"], delta["w_in"], new_m["w_in"], new_v["w_in"] = [a.T[None] for a in w_in_res]
    rest = ("w_o_rnn", "w_o_att", "w_out", "w_ple_gate", "w_ple", "conv_w")
    g_rest = [g_o_rnn, g_o_att, g_out, g_pg, g_ple, g_conv]
    res = _adamw_many([w[n][0] for n in rest], g_rest, [m[n][0] for n in rest], [v[n][0] for n in rest])
    for n, a in zip(rest, g_rest):
        grad[n] = a[None]
    for dst, vals in zip((delta, new_m, new_v), res):
        for n, a in zip(rest, vals):
            dst[n] = a[None]

    return (loss, grad_x.reshape(BL, S, D), *[grad[n] for n in names], *[delta[n] for n in names],
            *[new_m[n] for n in names], *[new_v[n] for n in names])
```

```python
import jax
import jax.numpy as jnp
from jax import lax
from jax.experimental import pallas as pl
from jax.experimental.pallas import tpu as pltpu

F32 = jnp.float32
BF16 = jnp.bfloat16

D = 1024
S = 2048
BL = 2
T = BL * S
NDEV = 8
NCHIP = 4
PLE = 256
DR = 1280
NRB = 10
RBW = 128
CONVW = 4
LRU_C = 8.0
HD = 128
NH = 4
PATTERNS = ((128, 1), (512, 4), (2048, 16))
NG = 3
ATT = NH * HD
GW = NG * ATT
NIN = 2 * DR + 3 * GW + ATT + 2 * D
OFF_ZR = DR
OFF_Q = 2 * DR
OFF_ZA = OFF_Q + 3 * GW
OFF_G = OFF_ZA + ATT
ROPE_THETA = 10000.0
EPS = 1e-6
SCALE = HD ** -0.5
NEG = -1e30
QB = 128
LANES = 128
CT = 512
NCT = NIN // CT
A_W = 2 * DR
C_W = ATT + 2 * D

LR, B1, B2, AEPS, WD, STEP = 0.001, 0.9, 0.999, 1e-08, 0.01, 10

NSHARD_IN = NIN // NDEV
REP_NAMES = ("w_rg_a", "w_rg_x", "norm_mix", "b_in", "conv_b", "b_rg_a", "b_rg_x", "lru_lambda", "q_norm",
             "k_norm", "norm_ple", "b_ple_gate")
REP_ROWS = (NRB * RBW, NRB * RBW, D // LANES, NIN // LANES, DR // LANES, DR // LANES, DR // LANES, DR // LANES,
            NG, NG, D // LANES, D // LANES)
REP_TOTAL_ROWS = sum(REP_ROWS)
REP_ROWS_DEV = 344
BIG_NAMES = ("w_in", "w_o_rnn", "w_o_att", "w_out", "w_ple_gate", "w_ple")

VMEM_BIG = 56 * 1024 * 1024
VMEM_MID = 40 * 1024 * 1024


def _cp(sem=None, vmem=None):
    return pltpu.CompilerParams(dimension_semantics=sem, vmem_limit_bytes=vmem)


def _hbm(*arrays):
    return [pltpu.with_memory_space_constraint(a, pltpu.HBM) for a in arrays]


def _dot(a, b):
    return jnp.dot(a, b, preferred_element_type=F32)


def _dot_nt(a, b):
    return lax.dot_general(a, b, (((1,), (1,)), ((), ())), preferred_element_type=F32)


def _dot_tn(a, b):
    return lax.dot_general(a, b, (((0,), (0,)), ((), ())), preferred_element_type=F32)


def _sigmoid(x):
    return jax.nn.sigmoid(x)


def _perm(j):
    jq = j - OFF_Q // CT
    inside = (j >= OFF_Q // CT) & (j < OFF_ZA // CT)
    return jnp.where(inside, OFF_Q // CT + (jq % 3) * 3 + jq // 3, j)


PIECES = ((0, A_W // CT), (OFF_Q // CT, GW // CT), (OFF_Q // CT + 3, GW // CT), (OFF_Q // CT + 6, GW // CT),
          (OFF_ZA // CT, C_W // CT))


def _rmsnorm_fwd(x, gain, token, tm=512):
    def body(x_ref, g_ref, _token, o_ref):
        xv = x_ref[...]
        var = jnp.mean(xv * xv, axis=-1, keepdims=True)
        o_ref[...] = (xv * lax.rsqrt(var + EPS) * g_ref[...]).astype(BF16)

    return pl.pallas_call(
        body, grid=(T // tm,), name="rmsnorm_fwd",
        in_specs=[pl.BlockSpec((tm, D), lambda i: (i, 0)), pl.BlockSpec((1, D), lambda i: (0, 0)),
                  pl.BlockSpec((8, LANES), lambda i: (0, 0))],
        out_specs=pl.BlockSpec((tm, D), lambda i: (i, 0)),
        out_shape=jax.ShapeDtypeStruct((T, D), BF16),
        compiler_params=_cp(("parallel",)),
    )(*_hbm(x, gain, token))


def _in_proj(hn, w_t, bias, tm=1024):
    def body(a_ref, w_ref, b_ref, o_ref):
        o_ref[...] = (_dot_nt(a_ref[...], w_ref[...]) + b_ref[...]).astype(BF16)

    return pl.pallas_call(
        body, grid=(T // tm, NCT), name="in_proj",
        in_specs=[pl.BlockSpec((tm, D), lambda i, j: (i, 0)),
                  pl.BlockSpec((CT, D), lambda i, j: (_perm(j), 0)),
                  pl.BlockSpec((1, CT), lambda i, j: (0, _perm(j)))],
        out_specs=pl.BlockSpec((tm, CT), lambda i, j: (i, j)),
        out_shape=jax.ShapeDtypeStruct((T, NIN), BF16),
        compiler_params=_cp(("parallel", "parallel"), VMEM_MID),
    )(*_hbm(hn, w_t, bias))


def _dhn(pieces, w_t, token, tm=512):
    def body(a_ref, q_ref, k_ref, v_ref, c_ref, w_hbm, _token, o_ref, w):
        @pl.when(pl.program_id(0) == 0)
        def _():
            pltpu.sync_copy(w_hbm, w)

        acc = _dot(a_ref[...], w[pl.ds(0, A_W), :])
        for kind, x_ref in enumerate((q_ref, k_ref, v_ref)):
            for g in range(NG):
                row = OFF_Q + (3 * g + kind) * CT
                acc = acc + _dot(x_ref[:, g * CT:(g + 1) * CT], w[pl.ds(row, CT), :])
        o_ref[...] = acc + _dot(c_ref[...], w[pl.ds(OFF_ZA, C_W), :])

    tok = lambda wd: pl.BlockSpec((tm, wd), lambda i: (i, 0))
    return pl.pallas_call(
        body, grid=(T // tm,), name="dhn",
        in_specs=[tok(A_W), tok(GW), tok(GW), tok(GW), tok(C_W), pl.BlockSpec(memory_space=pl.ANY),
                  pl.BlockSpec((8, LANES), lambda i: (0, 0))],
        out_specs=tok(D),
        out_shape=jax.ShapeDtypeStruct((T, D), F32),
        scratch_shapes=[pltpu.VMEM((NIN, D), BF16)],
        compiler_params=_cp(("arbitrary",), VMEM_BIG),
    )(*_hbm(*pieces, w_t, token))


def _dw_in(pieces, hn, tt=2048):
    nt = T // tt

    def body(a_ref, q_ref, k_ref, v_ref, c_ref, h_ref, o_ref, s_ref, acc, cs):
        j = pl.program_id(0)
        t = pl.program_id(1)

        def step(x_ref):
            xv = x_ref[...]
            p = _dot_tn(xv, h_ref[...])
            c = jnp.sum(xv.astype(F32), axis=0, keepdims=True)

            @pl.when(t == 0)
            def _():
                acc[...] = p
                cs[...] = c

            @pl.when(t > 0)
            def _():
                acc[...] += p
                cs[...] += c

        for x_ref, (lo, n) in zip((a_ref, q_ref, k_ref, v_ref, c_ref), PIECES):
            pl.when((j >= lo) & (j < lo + n))(lambda x_ref=x_ref: step(x_ref))

        @pl.when(t == nt - 1)
        def _():
            o_ref[...] = acc[...].astype(BF16)
            s_ref[...] = cs[...]

    def piece_spec(lo, n):
        def imap(j, t):
            used = (j >= lo) & (j < lo + n)
            return (jnp.where(used, t, 0), jnp.clip(j - lo, 0, n - 1))
        return pl.BlockSpec((tt, CT), imap)

    return pl.pallas_call(
        body, grid=(NCT, nt), name="dw_in",
        in_specs=[piece_spec(lo, n) for lo, n in PIECES] + [pl.BlockSpec((tt, D), lambda j, t: (t, 0))],
        out_specs=[pl.BlockSpec((CT, D), lambda j, t: (_perm(j), 0)), pl.BlockSpec((1, CT), lambda j, t: (0, _perm(j)))],
        out_shape=[jax.ShapeDtypeStruct((NIN, D), BF16), jax.ShapeDtypeStruct((1, NIN), F32)],
        scratch_shapes=[pltpu.VMEM((CT, D), F32), pltpu.VMEM((1, CT), F32)],
        compiler_params=_cp(("parallel", "arbitrary"), VMEM_MID),
    )(*_hbm(*pieces, hn))


def _mm_tn(a, b, ta, tt, name):
    m = a.shape[1]
    n = b.shape[1]
    nt = T // tt

    def body(a_ref, b_ref, o_ref, acc):
        t = pl.program_id(1)
        p = _dot_tn(a_ref[...].astype(BF16), b_ref[...].astype(BF16))

        @pl.when(t == 0)
        def _():
            acc[...] = p

        @pl.when(t > 0)
        def _():
            acc[...] += p

        @pl.when(t == nt - 1)
        def _():
            o_ref[...] = acc[...].astype(BF16)

    return pl.pallas_call(
        body, grid=(m // ta, nt), name=name,
        in_specs=[pl.BlockSpec((tt, ta), lambda j, t: (t, j)), pl.BlockSpec((tt, n), lambda j, t: (t, 0))],
        out_specs=pl.BlockSpec((ta, n), lambda j, t: (j, 0)),
        out_shape=jax.ShapeDtypeStruct((m, n), BF16),
        scratch_shapes=[pltpu.VMEM((ta, n), F32)],
        compiler_params=_cp(("parallel", "arbitrary"), VMEM_MID),
    )(*_hbm(a, b))


def _row_iota():
    return lax.broadcasted_iota(jnp.int32, (S, RBW), 0)


def _shift_down(v, d, row, fill):
    return jnp.where(row >= d, pltpu.roll(v, d, 0), fill)


def _shift_up(v, d, row, fill):
    return jnp.where(row < S - d, pltpu.roll(v, S - d, 0), fill)


SUBLANES = 8


def _scan_down(a, u, row):
    d = 1
    while d < S:
        last = 2 * d >= S
        if d < SUBLANES:
            u = a * _shift_down(u, d, row, 0.0) + u
            if not last:
                a = a * _shift_down(a, d, row, 1.0)
        else:
            u = jnp.concatenate([u[:d], a[d:] * u[:S - d] + u[d:]], axis=0)
            if not last:
                a = jnp.concatenate([a[:d], a[d:] * a[:S - d]], axis=0)
        d *= 2
    return u


def _scan_up(b, g, row):
    d = 1
    while d < S:
        last = 2 * d >= S
        if d < SUBLANES:
            g = g + b * _shift_up(g, d, row, 0.0)
            if not last:
                b = b * _shift_up(b, d, row, 0.0)
        else:
            g = jnp.concatenate([g[:S - d] + b[:S - d] * g[d:], g[S - d:]], axis=0)
            if not last:
                b = jnp.concatenate([b[:S - d] * b[d:], b[S - d:]], axis=0)
        d *= 2
    return g


def _softplus(x):
    return jnp.maximum(x, 0.0) + jnp.log1p(jnp.exp(-jnp.abs(x)))


def _rnn_gates(x, cw, cb, wa, ba, wx, bx, lam, row):
    xc = cb + cw[3:4, :] * x
    for j in (1, 2, 3):
        xc = xc + cw[3 - j:4 - j, :] * _shift_down(x, j, row, 0.0)
    xcb = xc.astype(BF16)
    r = _sigmoid(_dot(xcb, wa) + ba)
    i = _sigmoid(_dot(xcb, wx) + bx)
    sp = _softplus(-lam)
    log_a = (-LRU_C) * r * sp
    a = jnp.exp(log_a)
    mult = jnp.where(row == 0, 1.0, jnp.sqrt(jnp.tanh(-log_a) * (1.0 + a * a)))
    return xc, xcb, r, i, sp, a, mult


def _rnn_fwd(proj3, conv_w, conv_b, wa, ba, wx, bx, lam):
    def body(x_ref, cw_ref, cb_ref, wa_ref, ba_ref, wx_ref, bx_ref, lam_ref, h_ref):
        row = _row_iota()
        x = x_ref[0].astype(F32)
        xc, _, _, i, _, a, mult = _rnn_gates(x, cw_ref[...], cb_ref[...], wa_ref[0], ba_ref[...],
                                             wx_ref[0], bx_ref[...], lam_ref[...], row)
        h_ref[0] = _scan_down(a, mult * (i * xc), row)

    vec = lambda: pl.BlockSpec((1, RBW), lambda b, n: (0, n))
    mat = lambda: pl.BlockSpec((1, RBW, RBW), lambda b, n: (n, 0, 0))
    return pl.pallas_call(
        body, grid=(BL, NRB), name="rnn_fwd",
        in_specs=[pl.BlockSpec((1, S, RBW), lambda b, n: (b, 0, n)),
                  pl.BlockSpec((CONVW, RBW), lambda b, n: (0, n)),
                  vec(), mat(), vec(), mat(), vec(), vec()],
        out_specs=pl.BlockSpec((1, S, RBW), lambda b, n: (b, 0, n)),
        out_shape=jax.ShapeDtypeStruct((BL, S, DR), F32),
        compiler_params=_cp(("parallel", "parallel"), VMEM_MID),
    )(*_hbm(proj3, conv_w, conv_b, wa, ba, wx, bx, lam))


def _rnn_bwd(proj3, h3, dh3, slab_a3, conv_w, conv_b, wa, ba, wx, bx, lam, token):
    def body(x_ref, h_ref, dh_ref, cw_ref, cb_ref, wa_ref, ba_ref, wx_ref, bx_ref, lam_ref, _alias, _token,
             dx_ref, dcw_ref, dcb_ref, dwa_ref, dba_ref, dwx_ref, dbx_ref, dlam_ref):
        row = _row_iota()
        x = x_ref[0].astype(F32)
        cw = cw_ref[...]
        wa_v = wa_ref[0]
        wx_v = wx_ref[0]
        lam_v = lam_ref[...]
        xc, xcb, r, i, sp, a, mult = _rnn_gates(x, cw, cb_ref[...], wa_v, ba_ref[...], wx_v, bx_ref[...], lam_v, row)
        h = h_ref[0]
        g = _scan_up(_shift_up(a, 1, row, 0.0), dh_ref[0], row)
        da = g * _shift_down(h, 1, row, 0.0)
        dmult = jnp.where(row == 0, 0.0, g * (i * xc))
        gm = g * mult
        di = gm * xc
        dxc = gm * i
        dlog_a = da * a - dmult * (a * a) / mult
        dr = dlog_a * ((-LRU_C) * sp)
        dsp = jnp.sum(dlog_a * ((-LRU_C) * r), axis=0, keepdims=True)
        dlam = dsp * (-_sigmoid(-lam_v))
        dpa = dr * r * (1.0 - r)
        dpx = di * i * (1.0 - i)
        dpab = dpa.astype(BF16)
        dpxb = dpx.astype(BF16)
        dwa = _dot_tn(xcb, dpab)
        dwx = _dot_tn(xcb, dpxb)
        dba = jnp.sum(dpa, axis=0, keepdims=True)
        dbx = jnp.sum(dpx, axis=0, keepdims=True)
        dxc = dxc + _dot_nt(dpab, wa_v) + _dot_nt(dpxb, wx_v)
        dcb = jnp.sum(dxc, axis=0, keepdims=True)
        dx = cw[3:4, :] * dxc
        dcw_rows = [None] * CONVW
        dcw_rows[3] = jnp.sum(dxc * x, axis=0, keepdims=True)
        for j in (1, 2, 3):
            dx = dx + cw[3 - j:4 - j, :] * _shift_up(dxc, j, row, 0.0)
            dcw_rows[3 - j] = jnp.sum(dxc * _shift_down(x, j, row, 0.0), axis=0, keepdims=True)
        dx_ref[0] = dx.astype(BF16)
        dcw = jnp.concatenate(dcw_rows, axis=0)
        first = pl.program_id(1) == 0

        @pl.when(first)
        def _():
            dcw_ref[...] = dcw
            dcb_ref[...] = dcb
            dwa_ref[0] = dwa
            dba_ref[...] = dba
            dwx_ref[0] = dwx
            dbx_ref[...] = dbx
            dlam_ref[...] = dlam

        @pl.when(jnp.logical_not(first))
        def _():
            dcw_ref[...] += dcw
            dcb_ref[...] += dcb
            dwa_ref[0] += dwa
            dba_ref[...] += dba
            dwx_ref[0] += dwx
            dbx_ref[...] += dbx
            dlam_ref[...] += dlam

    slab = lambda: pl.BlockSpec((1, S, RBW), lambda n, b: (b, 0, n))
    vec = lambda: pl.BlockSpec((1, RBW), lambda n, b: (0, n))
    mat = lambda: pl.BlockSpec((1, RBW, RBW), lambda n, b: (n, 0, 0))
    taps = lambda: pl.BlockSpec((CONVW, RBW), lambda n, b: (0, n))
    vshape = jax.ShapeDtypeStruct((1, DR), F32)
    mshape = jax.ShapeDtypeStruct((NRB, RBW, RBW), F32)
    return pl.pallas_call(
        body, grid=(NRB, BL), name="rnn_bwd",
        in_specs=[slab(), slab(), slab(), taps(), vec(), mat(), vec(), mat(), vec(), vec(),
                  pl.BlockSpec(memory_space=pl.ANY), pl.BlockSpec((8, LANES), lambda n, b: (0, 0))],
        out_specs=[slab(), taps(), vec(), mat(), vec(), mat(), vec(), vec()],
        out_shape=[jax.ShapeDtypeStruct((BL, S, A_W), BF16), jax.ShapeDtypeStruct((CONVW, DR), F32),
                   vshape, mshape, vshape, mshape, vshape, vshape],
        input_output_aliases={10: 0},
        compiler_params=_cp(("parallel", "arbitrary"), 48 * 1024 * 1024),
    )(*_hbm(proj3, h3, dh3, conv_w, conv_b, wa, ba, wx, bx, lam, slab_a3, token))


NQB = S // QB


def _rms_head(t, gain):
    rstd = lax.rsqrt(jnp.mean(t * t, axis=-1, keepdims=True) + EPS)
    return t * rstd * gain


def _rope(t, cs, sn):
    return t * cs + pltpu.roll(t, HD // 2, 1) * sn


def _rope_t(dy, cs, sn):
    return dy * cs - pltpu.roll(dy, HD // 2, 1) * sn


def _bdot_nt(a, b):
    return lax.dot_general(a, b, (((2,), (2,)), ((0,), (0,))), preferred_element_type=F32)


def _bdot(a, b):
    return lax.dot_general(a, b, (((2,), (1,)), ((0,), (0,))), preferred_element_type=F32)


def _bdot_tn(a, b):
    return lax.dot_general(a, b, (((1,), (1,)), ((0,), (0,))), preferred_element_type=F32)


STRIDE_MAX = 4


def _permute(buf, x, dil, dst, off=0):
    ln = S // dil
    if dil == 1:
        dst[pl.ds(off, S), :] = x.astype(dst.dtype)
        return
    buf[0] = x
    if dil <= STRIDE_MAX:
        for c in range(dil):
            dst[pl.ds(off + c * ln, ln), :] = buf.at[0][pl.ds(c, ln, stride=dil), :].astype(dst.dtype)
        return
    f, r = STRIDE_MAX, dil // STRIDE_MAX
    part = S // f
    for c1 in range(f):
        buf.at[1][pl.ds(c1 * part, part), :] = buf.at[0][pl.ds(c1, part, stride=f), :]
    for c1 in range(f):
        for c2 in range(r):
            dst[pl.ds(off + (c1 + f * c2) * ln, ln), :] = (
                buf.at[1][pl.ds(c1 * part + c2, ln, stride=r), :].astype(dst.dtype))


def _unpermute(buf, xp, dil, dst):
    ln = S // dil
    if dil == 1:
        dst[...] = xp
        return
    if dil <= STRIDE_MAX:
        for c in range(dil):
            dst[pl.ds(c, ln, stride=dil), :] = xp[c * ln:(c + 1) * ln]
        return
    f, r = STRIDE_MAX, dil // STRIDE_MAX
    part = S // f
    for c1 in range(f):
        for c2 in range(r):
            c = c1 + f * c2
            buf.at[1][pl.ds(c1 * part + c2, ln, stride=r), :] = xp[c * ln:(c + 1) * ln]
    for c1 in range(f):
        dst[pl.ds(c1, part, stride=f), :] = buf[1, pl.ds(c1 * part, part), :]


def _blocks3(ref, off=0):
    return ref[pl.ds(off, S), :].reshape(NQB, QB, HD)


def _att_prep(q_ref, k_ref, v_ref, cos_ref, sin_ref, qn, kn, dil, nat, qs, ksp, vsp):
    cs = cos_ref[...]
    sn = sin_ref[...]
    zero = jnp.zeros((QB, HD), BF16)
    ksp[pl.ds(0, QB), :] = zero
    vsp[pl.ds(0, QB), :] = zero
    _permute(nat, _rope(_rms_head(q_ref[0].astype(F32), qn), cs, sn), dil, qs)
    _permute(nat, _rope(_rms_head(k_ref[0].astype(F32), kn), cs, sn), dil, ksp, QB)
    _permute(nat, v_ref[0].astype(F32), dil, vsp, QB)


def _att_scores(qs, ksp, dil):
    nb = S // dil // QB
    q3 = _blocks3(qs)
    shape = (NQB, QB, QB)
    qi = lax.broadcasted_iota(jnp.int32, shape, 1)
    kj = lax.broadcasted_iota(jnp.int32, shape, 2)
    s_c = jnp.where(qi >= kj, _bdot_nt(q3, _blocks3(ksp, QB)) * SCALE, NEG)
    if nb == 1:
        return q3, s_c, None
    jj = lax.broadcasted_iota(jnp.int32, shape, 0)
    ok = (kj >= qi) & ((jj & (nb - 1)) != 0)
    s_p = jnp.where(ok, _bdot_nt(q3, _blocks3(ksp)) * SCALE, NEG)
    return q3, s_c, s_p


def _qkv_spec(kind, g):
    base = OFF_Q // HD + kind * (GW // HD) + g * NH
    return pl.BlockSpec((1, S, HD), lambda b, h: (b, 0, base + h))


def _attn_fwd(proj3, cos_t, sin_t, q_norm, k_norm):
    def body(*refs):
        qkv_refs = refs[:9]
        cos_ref, sin_ref, qn_ref, kn_ref, att_ref, lse_ref, w_ref, nat, qs, ksp, vsp, og = refs[9:]
        for g, (window, dil) in enumerate(PATTERNS):
            q_ref, k_ref, v_ref = qkv_refs[3 * g:3 * g + 3]
            _att_prep(q_ref, k_ref, v_ref, cos_ref, sin_ref, qn_ref[g:g + 1, :], kn_ref[g:g + 1, :], dil,
                      nat, qs, ksp, vsp)
            _, s_c, s_p = _att_scores(qs, ksp, dil)
            m = jnp.max(s_c, axis=-1, keepdims=True)
            if s_p is not None:
                m = jnp.maximum(m, jnp.max(s_p, axis=-1, keepdims=True))
            e_c = jnp.exp(s_c - m)
            den = jnp.sum(e_c, axis=-1, keepdims=True)
            o = _bdot(e_c.astype(BF16), _blocks3(vsp, QB))
            if s_p is not None:
                e_p = jnp.exp(s_p - m)
                den = den + jnp.sum(e_p, axis=-1, keepdims=True)
                o = o + _bdot(e_p.astype(BF16), _blocks3(vsp))
            _unpermute(nat, (o / den).reshape(S, HD), dil, og.at[g])
            _unpermute(nat, jnp.broadcast_to(m + jnp.log(den), (NQB, QB, HD)).reshape(S, HD), dil,
                       lse_ref.at[g, 0])
        l0 = lse_ref[0, 0]
        l1 = lse_ref[1, 0]
        l2 = lse_ref[2, 0]
        mx = jnp.maximum(jnp.maximum(l0, l1), l2)
        e0 = jnp.exp(l0 - mx)
        e1 = jnp.exp(l1 - mx)
        e2 = jnp.exp(l2 - mx)
        inv = 1.0 / (e0 + e1 + e2)
        w0 = e0 * inv
        w1 = e1 * inv
        w2 = e2 * inv
        w_ref[0, 0] = w0
        w_ref[1, 0] = w1
        w_ref[2, 0] = w2
        att_ref[0] = w0 * og[0] + w1 * og[1] + w2 * og[2]

    in_specs = [_qkv_spec(kind, g) for g in range(NG) for kind in range(3)]
    in_specs += [pl.BlockSpec((S, HD), lambda b, h: (0, 0)), pl.BlockSpec((S, HD), lambda b, h: (0, 0)),
                 pl.BlockSpec((NG, HD), lambda b, h: (0, 0)), pl.BlockSpec((NG, HD), lambda b, h: (0, 0))]
    stat = lambda: pl.BlockSpec((NG, 1, S, HD), lambda b, h: (0, b, 0, h))
    return pl.pallas_call(
        body, grid=(BL, NH), name="attn_fwd",
        in_specs=in_specs,
        out_specs=[pl.BlockSpec((1, S, HD), lambda b, h: (b, 0, h)), stat(), stat()],
        out_shape=[jax.ShapeDtypeStruct((BL, S, ATT), F32),
                   jax.ShapeDtypeStruct((NG, BL, S, ATT), F32),
                   jax.ShapeDtypeStruct((NG, BL, S, ATT), F32)],
        scratch_shapes=[pltpu.VMEM((2, S, HD), F32), pltpu.VMEM((S, HD), BF16), pltpu.VMEM((S + QB, HD), BF16),
                        pltpu.VMEM((S + QB, HD), BF16), pltpu.VMEM((NG, S, HD), F32)],
        compiler_params=_cp(("parallel", "parallel"), VMEM_BIG),
    )(*_hbm(*([proj3] * 9), cos_t, sin_t, q_norm, k_norm))


def _attn_bwd_group(g, proj3, cos_t, sin_t, qn_g, kn_g, lse, wts, datt3, sbar3, slabs):
    dil = PATTERNS[g][1]
    n_alias = 0 if slabs is None else 3

    def norm_rope_bwd(dpost, raw, gain, cs, sn):
        dn = _rope_t(dpost, cs, sn)
        rstd = lax.rsqrt(jnp.mean(raw * raw, axis=-1, keepdims=True) + EPS)
        xh = raw * rstd
        dgain = jnp.sum(dn * xh, axis=0, keepdims=True)
        gd = dn * gain
        draw = rstd * (gd - xh * jnp.mean(gd * xh, axis=-1, keepdims=True))
        return draw, dgain

    def body(*refs):
        (q_ref, k_ref, v_ref, cos_ref, sin_ref, qn_ref, kn_ref, lse_ref, w_ref, datt_ref, sbar_ref) = refs[:11]
        (dq_ref, dk_ref, dv_ref, dqn_ref, dkn_ref, nat, qs, ksp, vsp, dos, cvp, lsp, acc) = refs[11 + n_alias:]
        qn = qn_ref[...]
        kn = kn_ref[...]
        cs = cos_ref[...]
        sn = sin_ref[...]
        _att_prep(q_ref, k_ref, v_ref, cos_ref, sin_ref, qn, kn, dil, nat, qs, ksp, vsp)
        wv = w_ref[0, 0]
        _permute(nat, wv * datt_ref[0], dil, dos)
        _permute(nat, wv * sbar_ref[0], dil, cvp)
        _permute(nat, lse_ref[0, 0], dil, lsp)
        q3, s_c, s_p = _att_scores(qs, ksp, dil)
        do3 = _blocks3(dos)
        lse3 = _blocks3(lsp)[:, :, 0:1]
        cv3 = _blocks3(cvp)[:, :, 0:1]
        p_c = jnp.exp(s_c - lse3)
        ds_c = (p_c * (_bdot_nt(do3, _blocks3(vsp, QB)) - cv3)).astype(BF16)
        dq = _bdot(ds_c, _blocks3(ksp, QB))
        acc[0] = _bdot_tn(ds_c, q3).reshape(S, HD)
        acc[1] = _bdot_tn(p_c.astype(BF16), do3).reshape(S, HD)
        if s_p is not None:
            p_p = jnp.exp(s_p - lse3)
            ds_p = (p_p * (_bdot_nt(do3, _blocks3(vsp)) - cv3)).astype(BF16)
            dq = dq + _bdot(ds_p, _blocks3(ksp))
            early = pl.ds(0, S - QB)
            acc[0, early, :] += _bdot_tn(ds_p, q3).reshape(S, HD)[QB:]
            acc[1, early, :] += _bdot_tn(p_p.astype(BF16), do3).reshape(S, HD)[QB:]
        _unpermute(nat, (dq * SCALE).reshape(S, HD), dil, nat.at[0])
        draw, dqn = norm_rope_bwd(nat[0], q_ref[0].astype(F32), qn, cs, sn)
        dq_ref[0] = draw.astype(BF16)
        _unpermute(nat, acc[0] * SCALE, dil, nat.at[0])
        draw, dkn = norm_rope_bwd(nat[0], k_ref[0].astype(F32), kn, cs, sn)
        dk_ref[0] = draw.astype(BF16)
        _unpermute(nat, acc[1], dil, nat.at[0])
        dv_ref[0] = nat[0].astype(BF16)
        first = (pl.program_id(0) == 0) & (pl.program_id(1) == 0)

        @pl.when(first)
        def _():
            dqn_ref[...] = dqn
            dkn_ref[...] = dkn

        @pl.when(jnp.logical_not(first))
        def _():
            dqn_ref[...] += dqn
            dkn_ref[...] += dkn

    full = lambda r: pl.BlockSpec((r, HD), lambda b, h: (0, 0))
    stat = lambda: pl.BlockSpec((1, 1, S, HD), lambda b, h: (g, b, 0, h))
    slab = lambda: pl.BlockSpec((1, S, HD), lambda b, h: (b, 0, h))
    out_slab = lambda: pl.BlockSpec((1, S, HD), lambda b, h: (b, 0, g * NH + h))
    big = jax.ShapeDtypeStruct((BL, S, GW), BF16)
    vecs = jax.ShapeDtypeStruct((1, HD), F32)
    in_specs = [_qkv_spec(0, g), _qkv_spec(1, g), _qkv_spec(2, g), full(S), full(S), full(1), full(1),
                stat(), stat(), slab(), slab()]
    args = [proj3, proj3, proj3, cos_t, sin_t, qn_g, kn_g, lse, wts, datt3, sbar3]
    aliases = {}
    if slabs is not None:
        in_specs += [pl.BlockSpec(memory_space=pl.ANY)] * 3
        args += list(slabs)
        aliases = {11: 0, 12: 1, 13: 2}
    return pl.pallas_call(
        body, grid=(BL, NH), name="attn_bwd_g%d" % g,
        in_specs=in_specs,
        out_specs=[out_slab(), out_slab(), out_slab(), full(1), full(1)],
        out_shape=[big, big, big, vecs, vecs],
        scratch_shapes=[pltpu.VMEM((2, S, HD), F32), pltpu.VMEM((S, HD), BF16), pltpu.VMEM((S + QB, HD), BF16),
                        pltpu.VMEM((S + QB, HD), BF16), pltpu.VMEM((S, HD), BF16), pltpu.VMEM((S, HD), F32),
                        pltpu.VMEM((S, HD), F32), pltpu.VMEM((2, S, HD), F32)],
        input_output_aliases=aliases,
        compiler_params=_cp(("arbitrary", "arbitrary"), VMEM_BIG),
    )(*_hbm(*args))


def _tail(x, proj, h, att, p, tgt, w_o_rnn, w_o_att_t, w_out, w_pg, w_ple_t, norm_ple, b_pg, tm=256):
    nt = T // tm
    inv_d = 1.0 / D

    def body(x_ref, h_ref, zr_ref, att_ref, za_ref, g0a_ref, g0b_ref, g1a_ref, g1b_ref, p_ref, tgt_ref,
             np_ref, bpg_ref, wor_hbm, woa_hbm, wout_hbm, wpg_hbm, wple_hbm,
             dx1_ref, merged_ref, n1_ref, dpre_ref, dpe_ref, dyr_ref, dya_ref, slab_a_ref, slab_c_ref, dh_ref,
             datt_ref, sbar_ref, yrnn_ref, yatt_ref, loss_ref, dnp_ref, dbpg_ref,
             wor, woa, wout, wpg, wple):
        first = pl.program_id(0) == 0

        @pl.when(first)
        def _():
            pltpu.sync_copy(wor_hbm, wor)
            pltpu.sync_copy(woa_hbm, woa)
            pltpu.sync_copy(wout_hbm, wout)
            pltpu.sync_copy(wpg_hbm, wpg)
            pltpu.sync_copy(wple_hbm, wple)

        xv = x_ref[...]
        hv = h_ref[...]
        zr = zr_ref[...].astype(F32)
        av = att_ref[...]
        za = za_ref[...].astype(F32)
        szr = _sigmoid(zr)
        silu_r = zr * szr
        yrnn_b = (hv * silu_r).astype(BF16)
        sza = _sigmoid(za)
        silu_a = za * sza
        yatt_b = (av * silu_a).astype(BF16)
        yrnn_ref[...] = yrnn_b
        yatt_ref[...] = yatt_b
        yr = _dot(yrnn_b, wor[...])
        ya = _dot_nt(yatt_b, woa[...])
        g0 = _sigmoid(jnp.concatenate([g0a_ref[...], g0b_ref[...]], axis=1).astype(F32))
        g1 = _sigmoid(jnp.concatenate([g1a_ref[...], g1b_ref[...]], axis=1).astype(F32))
        merged_b = (g0 * yr + g1 * ya).astype(BF16)
        merged_ref[...] = merged_b
        x1 = xv + _dot(merged_b, wout[...])
        rstd = lax.rsqrt(jnp.mean(x1 * x1, axis=-1, keepdims=True) + EPS)
        xh = x1 * rstd
        npl = np_ref[...]
        n1_b = (xh * npl).astype(BF16)
        n1_ref[...] = n1_b
        pg = _sigmoid(_dot(n1_b, wpg[...]) + bpg_ref[...])
        pe = _dot_nt(p_ref[...].astype(BF16), wple[...])
        err = x1 + pg * pe - tgt_ref[...]
        loss_t = 0.5 * inv_d * jnp.sum(err * err)
        dy = err * inv_d
        dpe_ref[...] = (dy * pg).astype(BF16)
        dpre = dy * pe * pg * (1.0 - pg)
        dpre_b = dpre.astype(BF16)
        dpre_ref[...] = dpre_b
        dn1 = _dot_nt(dpre_b, wpg[...])
        dnp = jnp.sum(dn1 * xh, axis=0, keepdims=True)
        dbpg = jnp.sum(dpre, axis=0, keepdims=True)
        gd = dn1 * npl
        dx1 = dy + rstd * (gd - xh * jnp.mean(gd * xh, axis=-1, keepdims=True))
        dx1_ref[...] = dx1
        dmerged = _dot_nt(dx1.astype(BF16), wout[...])
        dyr_b = (dmerged * g0).astype(BF16)
        dya_b = (dmerged * g1).astype(BF16)
        dyr_ref[...] = dyr_b
        dya_ref[...] = dya_b
        slab_c_ref[:, ATT:ATT + D] = (dmerged * yr * g0 * (1.0 - g0)).astype(BF16)
        slab_c_ref[:, ATT + D:ATT + 2 * D] = (dmerged * ya * g1 * (1.0 - g1)).astype(BF16)
        dyrnn = _dot_nt(dyr_b, wor[...])
        dyatt = _dot(dya_b, woa[...])
        dh_ref[...] = dyrnn * silu_r
        slab_a_ref[...] = (dyrnn * hv * szr * (1.0 + zr * (1.0 - szr))).astype(BF16)
        datt = dyatt * silu_a
        datt_ref[...] = datt
        slab_c_ref[:, 0:ATT] = (dyatt * av * sza * (1.0 + za * (1.0 - sza))).astype(BF16)
        da = datt * av
        for hh in range(NH):
            seg = slice(hh * HD, (hh + 1) * HD)
            sbar_ref[:, seg] = jnp.broadcast_to(jnp.sum(da[:, seg], axis=-1, keepdims=True), (tm, HD))

        @pl.when(first)
        def _():
            loss_ref[...] = jnp.full((8, LANES), loss_t, F32)
            dnp_ref[...] = dnp
            dbpg_ref[...] = dbpg

        @pl.when(jnp.logical_not(first))
        def _():
            loss_ref[...] += jnp.full((8, LANES), loss_t, F32)
            dnp_ref[...] += dnp
            dbpg_ref[...] += dbpg

    tok = lambda w: pl.BlockSpec((tm, w), lambda i: (i, 0))
    col = lambda w, blk: pl.BlockSpec((tm, w), lambda i: (i, blk))
    vec = lambda: pl.BlockSpec((1, D), lambda i: (0, 0))
    hbm = lambda: pl.BlockSpec(memory_space=pl.ANY)
    gb = OFF_G // 512
    in_specs = [tok(D), tok(DR), col(DR, 1), tok(ATT), col(ATT, OFF_ZA // ATT),
                col(512, gb), col(512, gb + 1), col(512, gb + 2), col(512, gb + 3),
                tok(PLE), tok(D), vec(), vec(), hbm(), hbm(), hbm(), hbm(), hbm()]
    sh = lambda w, dt: jax.ShapeDtypeStruct((T, w), dt)
    out_shape = [sh(D, F32), sh(D, BF16), sh(D, BF16), sh(D, BF16), sh(D, BF16), sh(D, BF16), sh(D, BF16),
                 sh(A_W, BF16), sh(C_W, BF16), sh(DR, F32), sh(ATT, F32), sh(ATT, F32),
                 sh(DR, BF16), sh(ATT, BF16),
                 jax.ShapeDtypeStruct((8, LANES), F32), jax.ShapeDtypeStruct((1, D), F32),
                 jax.ShapeDtypeStruct((1, D), F32)]
    out_specs = [tok(D), tok(D), tok(D), tok(D), tok(D), tok(D), tok(D), col(DR, 1), tok(C_W), tok(DR),
                 tok(ATT), tok(ATT), tok(DR), tok(ATT),
                 pl.BlockSpec((8, LANES), lambda i: (0, 0)), vec(), vec()]
    return pl.pallas_call(
        body, grid=(nt,), name="tail_fwd_bwd",
        in_specs=in_specs, out_specs=out_specs, out_shape=out_shape,
        scratch_shapes=[pltpu.VMEM((DR, D), BF16), pltpu.VMEM((D, ATT), BF16), pltpu.VMEM((D, D), BF16),
                        pltpu.VMEM((D, D), BF16), pltpu.VMEM((D, PLE), BF16)],
        compiler_params=_cp(("arbitrary",), VMEM_BIG),
    )(*_hbm(x, h, proj, att, proj, proj, proj, proj, proj, p, tgt, norm_ple, b_pg, w_o_rnn, w_o_att_t, w_out, w_pg,
            w_ple_t))


def _input_norm_bwd(x, dhn, dx1, gain, tm=512):
    def body(x_ref, dhn_ref, dx1_ref, g_ref, dx_ref, dg_ref):
        xv = x_ref[...]
        rstd = lax.rsqrt(jnp.mean(xv * xv, axis=-1, keepdims=True) + EPS)
        xh = xv * rstd
        dn = dhn_ref[...]
        dg = jnp.sum(dn * xh, axis=0, keepdims=True)
        gd = dn * g_ref[...]
        dx_ref[...] = dx1_ref[...] + rstd * (gd - xh * jnp.mean(gd * xh, axis=-1, keepdims=True))
        first = pl.program_id(0) == 0

        @pl.when(first)
        def _():
            dg_ref[...] = dg

        @pl.when(jnp.logical_not(first))
        def _():
            dg_ref[...] += dg

    tok = lambda: pl.BlockSpec((tm, D), lambda i: (i, 0))
    vec = lambda: pl.BlockSpec((1, D), lambda i: (0, 0))
    return pl.pallas_call(
        body, grid=(T // tm,), name="input_norm_bwd",
        in_specs=[tok(), tok(), tok(), vec()], out_specs=[tok(), vec()],
        out_shape=[jax.ShapeDtypeStruct((T, D), F32), jax.ShapeDtypeStruct((1, D), F32)],
        compiler_params=_cp(("arbitrary",), VMEM_MID),
    )(*_hbm(x, dhn, dx1, gain))


def _rope_tables():
    pos = jnp.arange(S, dtype=F32)
    inv_freq = ROPE_THETA ** (-jnp.arange(0, HD, 2, dtype=F32) / HD)
    ang = pos[:, None] * inv_freq[None, :]
    cos, sin = jnp.cos(ang), jnp.sin(ang)
    return jnp.concatenate([cos, cos], axis=1), jnp.concatenate([-sin, sin], axis=1)


def _local_step(x, p, tgt, w_in_t, other_weights, norm_mix, b_in, conv_b,
                w_rg_a, b_rg_a, w_rg_x, b_rg_x, lam, q_norm, k_norm, norm_ple, b_pg, start_reduce=None,
                entry_token=None):
    if start_reduce is None:
        start_reduce = lambda arrs, tag: (jnp.zeros((8, LANES), F32), arrs)
    if entry_token is None:
        entry_token = jnp.zeros((8, LANES), F32)
    cos_t, sin_t = _rope_tables()
    wa_b = w_rg_a.astype(BF16)
    wx_b = w_rg_x.astype(BF16)

    hn = _rmsnorm_fwd(x, norm_mix, entry_token)
    proj = _in_proj(hn, w_in_t, b_in)
    w_o_rnn, w_o_att_t, w_out, w_pg, w_ple_t, conv_w = other_weights(proj)
    proj3 = proj.reshape(BL, S, NIN)
    h3 = _rnn_fwd(proj3, conv_w, conv_b, wa_b, b_rg_a, wx_b, b_rg_x, lam)
    att3, lse, wts = _attn_fwd(proj3, cos_t, sin_t, q_norm, k_norm)
    (dx1, merged, n1, dpre, dpe, dyr, dya, slab_a, slab_c, dh, datt, sbar, yrnn, yatt, loss8, dnp, dbpg) = _tail(
        x, proj, h3.reshape(T, DR), att3.reshape(T, ATT), p, tgt, w_o_rnn, w_o_att_t, w_out, w_pg, w_ple_t,
        norm_ple, b_pg)

    token, pending_out = start_reduce([
        _mm_tn(yrnn, dyr, 640, 2048, "dw_o_rnn"),
        _mm_tn(dya, yatt, 512, 2048, "dw_o_att_t"),
        _mm_tn(merged, dx1, 512, 2048, "dw_out"),
        _mm_tn(n1, dpre, 512, 2048, "dw_ple_gate"),
        _mm_tn(dpe, p, 512, 2048, "dw_ple_t")], "out")

    slab_a3, dcw, dcb, dwa, dba, dwx, dbx, dlam = _rnn_bwd(
        proj3, h3, dh.reshape(BL, S, DR), slab_a.reshape(BL, S, A_W), conv_w, conv_b, wa_b, b_rg_a, wx_b, b_rg_x, lam,
        token)
    datt3 = datt.reshape(BL, S, ATT)
    sbar3 = sbar.reshape(BL, S, ATT)
    slabs = None
    dqn = []
    dkn = []
    for g in range(NG):
        dq, dk, dv, dqn_g, dkn_g = _attn_bwd_group(g, proj3, cos_t, sin_t, q_norm[g:g + 1], k_norm[g:g + 1],
                                                   lse, wts, datt3, sbar3, slabs)
        slabs = (dq, dk, dv)
        dqn.append(dqn_g)
        dkn.append(dkn_g)
    pieces = [slab_a3.reshape(T, A_W)] + [t.reshape(T, GW) for t in slabs] + [slab_c]
    dw_in_t, db_in = _dw_in(pieces, hn)
    token, pending_in = start_reduce([dw_in_t], "in")
    dhn = _dhn(pieces, w_in_t, token)
    grad_x, dnm = _input_norm_bwd(x, dhn, dx1, norm_mix)

    small = dict(w_rg_a=dwa, w_rg_x=dwx, norm_mix=dnm, b_in=db_in, conv_b=dcb, b_rg_a=dba, b_rg_x=dbx,
                 lru_lambda=dlam, q_norm=dqn, k_norm=dkn, norm_ple=dnp, b_ple_gate=dbpg, conv_w=dcw)
    return loss8[0, 0], grad_x, pending_out, pending_in, small


MESH = pl.DeviceIdType.MESH
HBM_SPEC = pl.BlockSpec(memory_space=pl.ANY)


def _my_pos():
    return lax.axis_index("x"), lax.axis_index("y"), lax.axis_index("c")


def _flip(pos, k):
    x, y, c = pos
    return (1 - x if k & 4 else x, 1 - y if k & 2 else y, 1 - c if k & 1 else c)


def _lin(pos):
    return 4 * pos[0] + 2 * pos[1] + pos[2]


def _chip(pos):
    return 2 * pos[0] + pos[1]


def _all_gather_two_level(shards, name):
    na = len(shards)

    def body(*refs):
        x_refs = refs[:na]
        out_refs = refs[na:2 * na]
        send_sems, recv_sems, local_sems = refs[2 * na:]
        me = _my_pos()
        sibling = _flip(me, 1)
        chips = [_flip(me, 4), _flip(me, 2), _flip(me, 6)]

        def copy(i, k, block, to, from_x=False):
            dst = out_refs[i].at[_lin(block)]
            return pltpu.make_async_remote_copy(
                src_ref=x_refs[i] if from_x else dst, dst_ref=dst,
                send_sem=send_sems.at[7 * i + k], recv_sem=recv_sems.at[7 * i + k], device_id=to, device_id_type=MESH)

        started = []
        for i in range(na):
            mine = pltpu.make_async_copy(x_refs[i], out_refs[i].at[_lin(me)], local_sems.at[i])
            mine.start()
            started.append(mine)
        sends = []
        for i in range(na):
            cps = [copy(i, 0, me, sibling, True)] + [copy(i, 1 + j, me, chip, True) for j, chip in enumerate(chips)]
            for cp in cps:
                cp.start()
            sends += cps
        for i in range(na):
            for j, chip in enumerate(chips):
                copy(i, 1 + j, chip, me).wait_recv()
                fwd = copy(i, 4 + j, chip, sibling)
                fwd.start()
                sends.append(fwd)
        for i in range(na):
            copy(i, 0, sibling, me).wait_recv()
            for j, chip in enumerate(chips):
                copy(i, 4 + j, _flip(chip, 1), me).wait_recv()
        for cp in sends:
            cp.wait_send()
        for mine in started:
            mine.wait()

    return pl.pallas_call(
        body, name=name,
        out_shape=[jax.ShapeDtypeStruct((NDEV,) + s.shape, s.dtype) for s in shards],
        in_specs=[HBM_SPEC] * na, out_specs=[HBM_SPEC] * na,
        scratch_shapes=[pltpu.SemaphoreType.DMA((7 * na,)), pltpu.SemaphoreType.DMA((7 * na,)),
                        pltpu.SemaphoreType.DMA((na,))],
    )(*shards)


def _all_gather_direct(shard, name):
    def body(x_ref, out_ref, send_sems, recv_sems, local_sem):
        me = _my_pos()
        mine = pltpu.make_async_copy(x_ref, out_ref.at[_lin(me)], local_sem)
        mine.start()
        sends = []
        for k in range(1, NDEV):
            cp = pltpu.make_async_remote_copy(
                src_ref=x_ref, dst_ref=out_ref.at[_lin(me)], send_sem=send_sems.at[k - 1],
                recv_sem=recv_sems.at[k - 1], device_id=_flip(me, k), device_id_type=MESH)
            cp.start()
            sends.append(cp)
        for k in range(1, NDEV):
            peer = _flip(me, k)
            pltpu.make_async_remote_copy(
                src_ref=x_ref, dst_ref=out_ref.at[_lin(peer)], send_sem=send_sems.at[k - 1],
                recv_sem=recv_sems.at[k - 1], device_id=peer, device_id_type=MESH).wait_recv()
        for cp in sends:
            cp.wait_send()
        mine.wait()

    return pl.pallas_call(
        body, name=name,
        out_shape=jax.ShapeDtypeStruct((NDEV,) + shard.shape, shard.dtype),
        in_specs=[HBM_SPEC], out_specs=HBM_SPEC,
        scratch_shapes=[pltpu.SemaphoreType.DMA((7,)), pltpu.SemaphoreType.DMA((7,)), pltpu.SemaphoreType.DMA],
    )(shard)


def _exchange_within_chip(parts, name):
    na = len(parts)

    def body(*refs):
        a_refs = refs[:na]
        recv_refs = refs[na:2 * na]
        send_sems, recv_sems = refs[2 * na:]
        me = _my_pos()
        c = me[2]
        sibling = _flip(me, 1)
        remote = []
        for i in range(na):
            for q in range(NCHIP):
                rc = pltpu.make_async_remote_copy(
                    src_ref=a_refs[i].at[q, 1 - c], dst_ref=recv_refs[i].at[q],
                    send_sem=send_sems.at[NCHIP * i + q], recv_sem=recv_sems.at[NCHIP * i + q],
                    device_id=sibling, device_id_type=MESH)
                rc.start()
                remote.append(rc)
        for rc in remote:
            rc.wait_recv()
        for rc in remote:
            rc.wait_send()

    return pl.pallas_call(
        body, name=name, out_shape=[jax.ShapeDtypeStruct((NCHIP,) + a.shape[2:], a.dtype) for a in parts],
        in_specs=[HBM_SPEC] * na, out_specs=[HBM_SPEC] * na,
        scratch_shapes=[pltpu.SemaphoreType.DMA((NCHIP * na,)), pltpu.SemaphoreType.DMA((NCHIP * na,))],
    )(*parts)


HBM_ONLY = pl.BlockSpec(memory_space=pltpu.HBM)
SEM_SPEC = pl.BlockSpec(memory_space=pltpu.SEMAPHORE)
SPLIT_COPY = pltpu.CompilerParams(has_side_effects=pltpu.SideEffectType.DATAFLOW_SIDE_EFFECTING)


def _chip_peers(me):
    return [_flip(me, 4), _flip(me, 2), _flip(me, 6)]


def _between_chips_start(parts, name):
    na = len(parts)

    def body(*refs):
        a_refs = refs[:na]
        land_refs = refs[na:2 * na]
        send_sems, recv_sems = refs[2 * na], refs[2 * na + 1]
        token = refs[-1]
        me = _my_pos()
        myq = _chip(me)
        for i in range(na):
            for j, peer in enumerate(_chip_peers(me)):
                pltpu.make_async_remote_copy(
                    src_ref=a_refs[i].at[_chip(peer)], dst_ref=land_refs[i].at[myq],
                    send_sem=send_sems.at[3 * i + j], recv_sem=recv_sems.at[3 * i + j],
                    device_id=peer, device_id_type=MESH).start()
        token[...] = jnp.zeros_like(token)

    hbm = [pltpu.HBM(a.shape, a.dtype) for a in parts]
    srcs = [pltpu.with_memory_space_constraint(a, pltpu.HBM) for a in parts]
    lands = [pltpu.with_memory_space_constraint(lax.empty(a.shape, a.dtype), pltpu.HBM) for a in parts]
    res = pl.pallas_call(
        body, name=name,
        out_shape=(pltpu.SemaphoreType.DMA((3 * na,)), pltpu.SemaphoreType.DMA((3 * na,)), *hbm, *hbm,
                   jax.ShapeDtypeStruct((8, LANES), F32)),
        in_specs=[HBM_ONLY] * (2 * na),
        out_specs=(SEM_SPEC, SEM_SPEC, *([HBM_ONLY] * (2 * na)), pl.BlockSpec(memory_space=pltpu.VMEM)),
        input_output_aliases={i: 2 + i for i in range(2 * na)},
        compiler_params=SPLIT_COPY,
    )(*srcs, *lands)
    return res[-1], (res[0], res[1], list(res[2:2 + na]), list(res[2 + na:2 + 2 * na]))


def _between_chips_wait(pending, after, name):
    send_sems, recv_sems, parts, lands = pending
    na = len(parts)

    def body(*refs):
        a_refs = refs[:na]
        land_refs = refs[na:2 * na]
        send_sems, recv_sems = refs[2 * na], refs[2 * na + 1]
        me = _my_pos()
        for i in range(na):
            for j, peer in enumerate(_chip_peers(me)):
                cp = pltpu.make_async_remote_copy(
                    src_ref=a_refs[i].at[_chip(peer)], dst_ref=land_refs[i].at[_chip(peer)],
                    send_sem=send_sems.at[3 * i + j], recv_sem=recv_sems.at[3 * i + j],
                    device_id=peer, device_id_type=MESH)
                cp.wait_send()
                cp.wait_recv()

    hbm = [pltpu.HBM(a.shape, a.dtype) for a in parts]
    res = pl.pallas_call(
        body, name=name, out_shape=(*hbm, *hbm),
        in_specs=[HBM_ONLY] * (2 * na) + [SEM_SPEC, SEM_SPEC, pl.BlockSpec(memory_space=pl.ANY)],
        out_specs=[HBM_ONLY] * (2 * na),
        input_output_aliases={i: i for i in range(2 * na)},
        compiler_params=SPLIT_COPY,
    )(*parts, *lands, send_sems, recv_sems, after)
    return list(res[:na]), list(res[na:])


def _gather_start(shards, after, name):
    na = len(shards)

    def body(*refs):
        x_refs = refs[:na]
        land_refs = refs[na:2 * na]
        send_sems, recv_sems = refs[2 * na + 1], refs[2 * na + 2]
        token = refs[-1]
        me = _my_pos()
        for i in range(na):
            for k in range(1, NDEV):
                pltpu.make_async_remote_copy(
                    src_ref=x_refs[i], dst_ref=land_refs[i].at[_lin(me)],
                    send_sem=send_sems.at[7 * i + k - 1], recv_sem=recv_sems.at[7 * i + k - 1],
                    device_id=_flip(me, k), device_id_type=MESH).start()
        token[...] = jnp.zeros_like(token)

    src_t = [pltpu.HBM(a.shape, a.dtype) for a in shards]
    land_t = [pltpu.HBM((NDEV,) + a.shape, a.dtype) for a in shards]
    srcs = [pltpu.with_memory_space_constraint(a, pltpu.HBM) for a in shards]
    lands = [pltpu.with_memory_space_constraint(lax.empty((NDEV,) + a.shape, a.dtype), pltpu.HBM) for a in shards]
    res = pl.pallas_call(
        body, name=name,
        out_shape=(pltpu.SemaphoreType.DMA((7 * na,)), pltpu.SemaphoreType.DMA((7 * na,)), *src_t, *land_t,
                   jax.ShapeDtypeStruct((8, LANES), F32)),
        in_specs=[HBM_ONLY] * (2 * na) + [pl.BlockSpec(memory_space=pl.ANY)],
        out_specs=(SEM_SPEC, SEM_SPEC, *([HBM_ONLY] * (2 * na)), pl.BlockSpec(memory_space=pltpu.VMEM)),
        input_output_aliases={i: 2 + i for i in range(2 * na)},
        compiler_params=SPLIT_COPY,
    )(*srcs, *lands, after)
    return res[-1], (res[0], res[1], list(res[2:2 + na]), list(res[2 + na:2 + 2 * na]))


def _gather_wait(pending, after, name):
    send_sems, recv_sems, shards, lands = pending
    na = len(shards)

    def body(*refs):
        x_refs = refs[:na]
        land_refs = refs[na:2 * na]
        send_sems, recv_sems = refs[2 * na], refs[2 * na + 1]
        me = _my_pos()
        for i in range(na):
            for k in range(1, NDEV):
                peer = _flip(me, k)
                cp = pltpu.make_async_remote_copy(
                    src_ref=x_refs[i], dst_ref=land_refs[i].at[_lin(peer)],
                    send_sem=send_sems.at[7 * i + k - 1], recv_sem=recv_sems.at[7 * i + k - 1],
                    device_id=peer, device_id_type=MESH)
                cp.wait_send()
                cp.wait_recv()

    src_t = [pltpu.HBM(a.shape, a.dtype) for a in shards]
    land_t = [pltpu.HBM(a.shape, a.dtype) for a in lands]
    res = pl.pallas_call(
        body, name=name, out_shape=(*src_t, *land_t),
        in_specs=[HBM_ONLY] * (2 * na) + [SEM_SPEC, SEM_SPEC, pl.BlockSpec(memory_space=pl.ANY)],
        out_specs=[HBM_ONLY] * (2 * na),
        input_output_aliases={i: i for i in range(2 * na)},
        compiler_params=SPLIT_COPY,
    )(*shards, *lands, send_sems, recv_sems, after)
    return list(res[na:])


def _exchange_all(parts, name):
    na = len(parts)

    def body(*refs):
        a_refs = refs[:na]
        out_refs = refs[na:2 * na]
        send_sems, recv_sems = refs[2 * na:]
        me = _my_pos()
        sends = []
        for i in range(na):
            for k in range(1, NDEV):
                peer = _flip(me, k)
                cp = pltpu.make_async_remote_copy(
                    src_ref=a_refs[i].at[_lin(peer)], dst_ref=out_refs[i].at[_lin(me)],
                    send_sem=send_sems.at[7 * i + k - 1], recv_sem=recv_sems.at[7 * i + k - 1],
                    device_id=peer, device_id_type=MESH)
                cp.start()
                sends.append(cp)
        for i in range(na):
            for k in range(1, NDEV):
                peer = _flip(me, k)
                pltpu.make_async_remote_copy(
                    src_ref=a_refs[i].at[_lin(peer)], dst_ref=out_refs[i].at[_lin(peer)],
                    send_sem=send_sems.at[7 * i + k - 1], recv_sem=recv_sems.at[7 * i + k - 1],
                    device_id=peer, device_id_type=MESH).wait_recv()
        for cp in sends:
            cp.wait_send()

    return pl.pallas_call(
        body, name=name, out_shape=[jax.ShapeDtypeStruct(a.shape, a.dtype) for a in parts],
        in_specs=[HBM_SPEC] * na, out_specs=[HBM_SPEC] * na,
        scratch_shapes=[pltpu.SemaphoreType.DMA((7 * na,)), pltpu.SemaphoreType.DMA((7 * na,))],
    )(*parts)


def _scalar(v):
    return jnp.asarray(v, jnp.int32).reshape(1)


def _sum_pairs(parts, theirs, name):
    na = len(parts)

    def body(c_ref, *refs):
        for i in range(na):
            o_ref = refs[2 * na + i]
            o_ref[0] = (refs[i][0, 0].astype(F32) + refs[na + i][0].astype(F32)).astype(o_ref.dtype)

    def mine_spec(a):
        return pl.BlockSpec((1, 1) + a.shape[2:], lambda q, c_ref: (q, c_ref[0], 0, 0))

    def spec(a):
        return pl.BlockSpec((1,) + a.shape[1:], lambda q, c_ref: (q, 0, 0))

    return pl.pallas_call(
        body, name=name,
        grid_spec=pltpu.PrefetchScalarGridSpec(
            num_scalar_prefetch=1, grid=(NCHIP,),
            in_specs=[mine_spec(a) for a in parts] + [spec(a) for a in theirs],
            out_specs=[spec(a) for a in theirs]),
        out_shape=[jax.ShapeDtypeStruct(a.shape, a.dtype) for a in theirs],
        compiler_params=_cp(("arbitrary",), VMEM_BIG),
    )(_scalar(lax.axis_index("c")), *parts, *theirs)


def _others(q, mine, nblk=NCHIP):
    return jnp.where(q == mine, (q + 1) % nblk, q)


def _sum_chips_adamw(own, recv, wv, mv, vv, tr, name):
    _, r, w = recv.shape

    def body(q_ref, own_ref, r0, r1, r2, r3, w_ref, m_ref, v_ref, g_ref, d_ref, m2_ref, v2_ref):
        myq = q_ref[0]
        acc = None
        for q, r_ref in enumerate((r0, r1, r2, r3)):
            term = jnp.where(myq == q, own_ref[0], r_ref[0]).astype(F32)
            acc = term if acc is None else acc + term
        g_ref[...] = acc
        delta, m2, v2 = _adam_math(w_ref[...], acc, m_ref[...], v_ref[...])
        d_ref[...] = delta
        m2_ref[...] = m2
        v2_ref[...] = v2

    def recv_spec(q):
        return pl.BlockSpec((1, tr, w), lambda i, q_ref: (_others(q, q_ref[0]), i, 0))

    rows = lambda: pl.BlockSpec((tr, w), lambda i, q_ref: (i, 0))
    shp = jax.ShapeDtypeStruct((r, w), F32)
    return pl.pallas_call(
        body, name=name,
        grid_spec=pltpu.PrefetchScalarGridSpec(
            num_scalar_prefetch=1, grid=(r // tr,),
            in_specs=[pl.BlockSpec((1, tr, w), lambda i, q_ref: (q_ref[0], i, 0))]
            + [recv_spec(q) for q in range(NCHIP)] + [rows(), rows(), rows()],
            out_specs=[rows(), rows(), rows(), rows()]),
        out_shape=[shp, shp, shp, shp],
        compiler_params=_cp(("arbitrary",), VMEM_MID),
    )(_scalar(_chip(_my_pos())), *_hbm(own, recv, recv, recv, recv, wv, mv, vv))


def _sum_blocks_small(own, recv, mine, transpose, name):
    na = len(recv)
    nblk = recv[0].shape[0]

    def body(q_ref, *refs):
        me = q_ref[0]
        for i in range(na):
            acc = None
            for q in range(nblk):
                term = jnp.where(me == q, refs[i][0], refs[na * (1 + q) + i][0]).astype(F32)
                acc = term if acc is None else acc + term
            refs[na * (1 + nblk) + i][...] = acc.T if transpose[i] else acc

    def oshape(a, tr):
        r, w = a.shape[1:]
        return (w, r) if tr else (r, w)

    own_spec = lambda a: pl.BlockSpec((1,) + a.shape[1:], lambda s, q_ref: (q_ref[0], 0, 0))
    recv_spec = lambda a, q: pl.BlockSpec((1,) + a.shape[1:], lambda s, q_ref: (_others(q, q_ref[0], nblk), 0, 0))
    out_spec = lambda shp: pl.BlockSpec(shp, lambda s, q_ref: (0, 0))
    in_specs = [own_spec(a) for a in own]
    for q in range(nblk):
        in_specs += [recv_spec(a, q) for a in recv]
    return pl.pallas_call(
        body, name=name,
        grid_spec=pltpu.PrefetchScalarGridSpec(
            num_scalar_prefetch=1, grid=(1,), in_specs=in_specs,
            out_specs=[out_spec(oshape(a, tr)) for a, tr in zip(recv, transpose)]),
        out_shape=[jax.ShapeDtypeStruct(oshape(a, tr), F32) for a, tr in zip(recv, transpose)],
        compiler_params=_cp(("arbitrary",), VMEM_MID),
    )(_scalar(mine), *own, *(list(recv) * nblk))


def _rep_offsets():
    offs = []
    o = 0
    for r in REP_ROWS:
        offs.append(o)
        o += r
    return offs


def _pack_small_grads(g):
    offs = _rep_offsets()

    def body(dwa, dwx, dnm, dbin, dcb, dba, dbx, dlam, dq0, dq1, dq2, dk0, dk1, dk2, dnp, dbpg, o_ref):
        o_ref[pl.ds(REP_TOTAL_ROWS - 2, NDEV * REP_ROWS_DEV - REP_TOTAL_ROWS + 2), :] = jnp.zeros(
            (NDEV * REP_ROWS_DEV - REP_TOTAL_ROWS + 2, LANES), F32)
        for n in range(NRB):
            o_ref[pl.ds(offs[0] + n * RBW, RBW), :] = dwa[n]
            o_ref[pl.ds(offs[1] + n * RBW, RBW), :] = dwx[n]

        def put_vec(off, ref, rows):
            for k in range(rows):
                o_ref[pl.ds(off + k, 1), :] = ref[:, k * LANES:(k + 1) * LANES]

        put_vec(offs[2], dnm, REP_ROWS[2])
        put_vec(offs[3], dbin, REP_ROWS[3])
        put_vec(offs[4], dcb, REP_ROWS[4])
        put_vec(offs[5], dba, REP_ROWS[5])
        put_vec(offs[6], dbx, REP_ROWS[6])
        put_vec(offs[7], dlam, REP_ROWS[7])
        for k, ref in enumerate((dq0, dq1, dq2)):
            o_ref[pl.ds(offs[8] + k, 1), :] = ref[...]
        for k, ref in enumerate((dk0, dk1, dk2)):
            o_ref[pl.ds(offs[9] + k, 1), :] = ref[...]
        put_vec(offs[10], dnp, REP_ROWS[10])
        put_vec(offs[11], dbpg, REP_ROWS[11])

    args = [g["w_rg_a"], g["w_rg_x"], g["norm_mix"], g["b_in"], g["conv_b"], g["b_rg_a"], g["b_rg_x"],
            g["lru_lambda"], *g["q_norm"], *g["k_norm"], g["norm_ple"], g["b_ple_gate"]]
    full = lambda shp: pl.BlockSpec(shp, lambda: (0,) * len(shp))
    return pl.pallas_call(
        body, name="pack_small_grads",
        in_specs=[full(a.shape) for a in args],
        out_specs=full((NDEV * REP_ROWS_DEV, LANES)),
        out_shape=jax.ShapeDtypeStruct((NDEV * REP_ROWS_DEV, LANES), F32),
    )(*_hbm(*args))


def _adam_math(wv, gv, mv, vv):
    c1 = 1.0 - B1 ** STEP
    c2 = 1.0 - B2 ** STEP
    m2 = B1 * mv + (1.0 - B1) * gv
    v2 = B2 * vv + (1.0 - B2) * (gv * gv)
    delta = (-LR) * ((m2 / c1) / (jnp.sqrt(v2 / c2) + AEPS) + WD * wv)
    return delta, m2, v2


def _adamw_small(rep_flat, w, m, v):
    offs = _rep_offsets()
    n = len(REP_NAMES)

    def body(*refs):
        g_ref = refs[0]
        w_refs = refs[1:1 + n]
        m_refs = refs[1 + n:1 + 2 * n]
        v_refs = refs[1 + 2 * n:1 + 3 * n]
        outs = refs[1 + 3 * n:]
        go, do, mo, vo = outs[:n], outs[n:2 * n], outs[2 * n:3 * n], outs[3 * n:]

        def emit(i, idx, gv):
            go[i][idx] = gv
            delta, m2, v2 = _adam_math(w_refs[i][idx], gv, m_refs[i][idx], v_refs[i][idx])
            do[i][idx] = delta
            mo[i][idx] = m2
            vo[i][idx] = v2

        for i in range(n):
            if i < 2:
                for b in range(NRB):
                    emit(i, b, g_ref[pl.ds(offs[i] + b * RBW, RBW), :])
            elif REP_NAMES[i] in ("q_norm", "k_norm"):
                emit(i, slice(None), g_ref[pl.ds(offs[i], NG), :])
            else:
                gv = jnp.concatenate([g_ref[pl.ds(offs[i] + k, 1), :] for k in range(REP_ROWS[i])], axis=1)
                emit(i, slice(None), gv)

    full = lambda shp: pl.BlockSpec(shp, lambda: (0,) * len(shp))
    pspecs = [full(a.shape) for a in w]
    pshapes = [jax.ShapeDtypeStruct(a.shape, F32) for a in w]
    res = pl.pallas_call(
        body, name="adamw_small",
        in_specs=[full(rep_flat.shape)] + pspecs * 3,
        out_specs=pspecs * 4, out_shape=pshapes * 4,
        compiler_params=_cp(None, VMEM_MID),
    )(*_hbm(rep_flat, *w, *m, *v))
    return res[:n], res[n:2 * n], res[2 * n:3 * n], res[3 * n:]


def _adamw_many(w, g, m, v):
    n = len(w)

    def body(*refs):
        for i in range(n):
            delta, m2, v2 = _adam_math(refs[i][...], refs[n + i][...], refs[2 * n + i][...], refs[3 * n + i][...])
            refs[4 * n + i][...] = delta
            refs[5 * n + i][...] = m2
            refs[6 * n + i][...] = v2

    full = lambda shp: pl.BlockSpec(shp, lambda: (0,) * len(shp))
    specs = [full(a.shape) for a in w]
    shapes = [jax.ShapeDtypeStruct(a.shape, F32) for a in w]
    res = pl.pallas_call(
        body, name="adamw_shards",
        in_specs=specs * 4, out_specs=specs * 3, out_shape=shapes * 3,
        compiler_params=_cp(None, VMEM_MID),
    )(*_hbm(*w, *g, *m, *v))
    return res[:n], res[n:2 * n], res[2 * n:]


def kernel(x, p, norm_mix, w_in, b_in, conv_w, conv_b, w_rg_a, b_rg_a, w_rg_x, b_rg_x, lru_lambda, q_norm, k_norm, w_o_rnn, w_o_att, w_out, norm_ple, w_ple_gate, b_ple_gate, w_ple, loss_target, m_norm_mix, m_w_in, m_b_in, m_conv_w, m_conv_b, m_w_rg_a, m_b_rg_a, m_w_rg_x, m_b_rg_x, m_lru_lambda, m_q_norm, m_k_norm, m_w_o_rnn, m_w_o_att, m_w_out, m_norm_ple, m_w_ple_gate, m_b_ple_gate, m_w_ple, v_norm_mix, v_w_in, v_b_in, v_conv_w, v_conv_b, v_w_rg_a, v_b_rg_a, v_w_rg_x, v_b_rg_x, v_lru_lambda, v_q_norm, v_k_norm, v_w_o_rnn, v_w_o_att, v_w_out, v_norm_ple, v_w_ple_gate, v_b_ple_gate, v_w_ple):
    w = dict(norm_mix=norm_mix, w_in=w_in, b_in=b_in, conv_w=conv_w, conv_b=conv_b, w_rg_a=w_rg_a, b_rg_a=b_rg_a,
             w_rg_x=w_rg_x, b_rg_x=b_rg_x, lru_lambda=lru_lambda, q_norm=q_norm, k_norm=k_norm, w_o_rnn=w_o_rnn,
             w_o_att=w_o_att, w_out=w_out, norm_ple=norm_ple, w_ple_gate=w_ple_gate, b_ple_gate=b_ple_gate,
             w_ple=w_ple)
    m = dict(norm_mix=m_norm_mix, w_in=m_w_in, b_in=m_b_in, conv_w=m_conv_w, conv_b=m_conv_b, w_rg_a=m_w_rg_a,
             b_rg_a=m_b_rg_a, w_rg_x=m_w_rg_x, b_rg_x=m_b_rg_x, lru_lambda=m_lru_lambda, q_norm=m_q_norm,
             k_norm=m_k_norm, w_o_rnn=m_w_o_rnn, w_o_att=m_w_o_att, w_out=m_w_out, norm_ple=m_norm_ple,
             w_ple_gate=m_w_ple_gate, b_ple_gate=m_b_ple_gate, w_ple=m_w_ple)
    v = dict(norm_mix=v_norm_mix, w_in=v_w_in, b_in=v_b_in, conv_w=v_conv_w, conv_b=v_conv_b, w_rg_a=v_w_rg_a,
             b_rg_a=v_b_rg_a, w_rg_x=v_w_rg_x, b_rg_x=v_b_rg_x, lru_lambda=v_lru_lambda, q_norm=v_q_norm,
             k_norm=v_k_norm, w_o_rnn=v_w_o_rnn, w_o_att=v_w_o_att, w_out=v_w_out, norm_ple=v_norm_ple,
             w_ple_gate=v_w_ple_gate, b_ple_gate=v_b_ple_gate, w_ple=v_w_ple)
    names = list(w.keys())

    shards = [w_in[0].T.astype(BF16), w_o_rnn[0].astype(BF16), w_o_att[0].T.astype(BF16), w_out[0].astype(BF16),
              w_ple_gate[0].astype(BF16), w_ple[0].T.astype(BF16), conv_w[0]]
    w_in_g = _all_gather_two_level(shards[:1], "gather_w_in")[0]
    w_in_t = w_in_g.reshape(NIN, D)
    entry_token, gather_pending = _gather_start(shards[1:], w_in_g, "gather_others_start")
    me = _lin(_my_pos())

    def other_weights(after):
        landed = _gather_wait(gather_pending, after, "gather_others_wait")
        full = [lax.dynamic_update_slice(a, s[None], (me, 0, 0)) for a, s in zip(landed, shards[1:])]
        conv_f = full[5].transpose(1, 0, 2).reshape(CONVW, DR)
        return (*[a.reshape((NDEV * a.shape[1], a.shape[2])) for a in full[:5]], conv_f)

    def start_reduce(arrs, tag):
        parts = [a.reshape((NCHIP, 2, a.shape[0] // NDEV, a.shape[1])) for a in arrs]
        theirs = _exchange_within_chip(parts, "reduce_within_chip_" + tag)
        return _between_chips_start(_sum_pairs(parts, theirs, "sum_pairs_" + tag), "reduce_between_chips_start_" + tag)

    loss_part, grad_x, pending_out, pending_in, small = _local_step(
        x.reshape(T, D), p.reshape(T, PLE), loss_target.reshape(T, D),
        w_in_t, other_weights,
        norm_mix, b_in, conv_b, w_rg_a[0], b_rg_a, w_rg_x[0], b_rg_x, lru_lambda, q_norm[0], k_norm[0],
        norm_ple, b_ple_gate, start_reduce, entry_token)
    loss = lax.psum(loss_part, ("x", "y", "c"))

    rep_parts = _pack_small_grads(small).reshape(NDEV, REP_ROWS_DEV, LANES)
    conv_parts = small["conv_w"].reshape(CONVW, NDEV, DR // NDEV).transpose(1, 0, 2)
    small_parts = [rep_parts, conv_parts]
    g_rep, g_conv = _sum_blocks_small(small_parts, _exchange_all(small_parts, "reduce_small"), me, (False, False),
                                      "sum_small")
    rep_all = _all_gather_direct(g_rep, "gather_small").reshape(NDEV * REP_ROWS_DEV, LANES)

    myq = _chip(_my_pos())
    own_out, recv_out = _between_chips_wait(pending_out, rep_all, "reduce_between_chips_wait_out")
    own_in, recv_in = _between_chips_wait(pending_in, rep_all, "reduce_between_chips_wait_in")
    w_in_res = _sum_chips_adamw(own_in[0], recv_in[0], w_in[0].T, m_w_in[0].T, v_w_in[0].T, 304, "adamw_w_in")
    g_o_rnn, g_o_att, g_out, g_pg, g_ple = _sum_blocks_small(
        own_out, recv_out, myq, (False, True, False, False, True), "sum_chips_out")

    grad, delta, new_m, new_v = {}, {}, {}, {}
    rep_shape = lambda a: a if a.ndim == 2 else a.reshape(a.shape[1:])
    res = _adamw_small(rep_all, [rep_shape(w[n]) for n in REP_NAMES], [rep_shape(m[n]) for n in REP_NAMES],
                       [rep_shape(v[n]) for n in REP_NAMES])
    for dst, vals in zip((grad, delta, new_m, new_v), res):
        for n, a in zip(REP_NAMES, vals):
            dst[n] = a.reshape(w[n].shape)
    grad["w_in"], delta["w_in"], new_m["w_in"], new_v["w_in"] = [a.T[None] for a in w_in_res]
    rest = ("w_o_rnn", "w_o_att", "w_out", "w_ple_gate", "w_ple", "conv_w")
    g_rest = [g_o_rnn, g_o_att, g_out, g_pg, g_ple, g_conv]
    res = _adamw_many([w[n][0] for n in rest], g_rest, [m[n][0] for n in rest], [v[n][0] for n in rest])
    for n, a in zip(rest, g_rest):
        grad[n] = a[None]
    for dst, vals in zip((delta, new_m, new_v), res):
        for n, a in zip(rest, vals):
            dst[n] = a[None]

    return (loss, grad_x.reshape(BL, S, D), *[grad[n] for n in names], *[delta[n] for n in names],
            *[new_m[n] for n in names], *[new_v[n] for n in names])
```

```python
import jax
import jax.numpy as jnp
from jax import lax
from jax.experimental import pallas as pl
from jax.experimental.pallas import tpu as pltpu

F32 = jnp.float32
BF16 = jnp.bfloat16

D = 1024
S = 2048
BL = 2
T = BL * S
NDEV = 8
NCHIP = 4
PLE = 256
DR = 1280
NRB = 10
RBW = 128
CONVW = 4
LRU_C = 8.0
HD = 128
NH = 4
PATTERNS = ((128, 1), (512, 4), (2048, 16))
NG = 3
ATT = NH * HD
GW = NG * ATT
NIN = 2 * DR + 3 * GW + ATT + 2 * D
OFF_ZR = DR
OFF_Q = 2 * DR
OFF_ZA = OFF_Q + 3 * GW
OFF_G = OFF_ZA + ATT
ROPE_THETA = 10000.0
EPS = 1e-6
SCALE = HD ** -0.5
NEG = -1e30
QB = 128
LANES = 128
CT = 512
NCT = NIN // CT
A_W = 2 * DR
C_W = ATT + 2 * D

LR, B1, B2, AEPS, WD, STEP = 0.001, 0.9, 0.999, 1e-08, 0.01, 10

NSHARD_IN = NIN // NDEV
REP_NAMES = ("w_rg_a", "w_rg_x", "norm_mix", "b_in", "conv_b", "b_rg_a", "b_rg_x", "lru_lambda", "q_norm",
             "k_norm", "norm_ple", "b_ple_gate")
REP_ROWS = (NRB * RBW, NRB * RBW, D // LANES, NIN // LANES, DR // LANES, DR // LANES, DR // LANES, DR // LANES,
            NG, NG, D // LANES, D // LANES)
REP_TOTAL_ROWS = sum(REP_ROWS)
REP_ROWS_DEV = 344
BIG_NAMES = ("w_in", "w_o_rnn", "w_o_att", "w_out", "w_ple_gate", "w_ple")

VMEM_BIG = 56 * 1024 * 1024
VMEM_MID = 40 * 1024 * 1024


def _cp(sem=None, vmem=None):
    return pltpu.CompilerParams(dimension_semantics=sem, vmem_limit_bytes=vmem)


def _hbm(*arrays):
    return [pltpu.with_memory_space_constraint(a, pltpu.HBM) for a in arrays]


def _dot(a, b):
    return jnp.dot(a, b, preferred_element_type=F32)


def _dot_nt(a, b):
    return lax.dot_general(a, b, (((1,), (1,)), ((), ())), preferred_element_type=F32)


def _dot_tn(a, b):
    return lax.dot_general(a, b, (((0,), (0,)), ((), ())), preferred_element_type=F32)


def _sigmoid(x):
    return jax.nn.sigmoid(x)


def _perm(j):
    jq = j - OFF_Q // CT
    inside = (j >= OFF_Q // CT) & (j < OFF_ZA // CT)
    return jnp.where(inside, OFF_Q // CT + (jq % 3) * 3 + jq // 3, j)


PIECES = ((0, A_W // CT), (OFF_Q // CT, GW // CT), (OFF_Q // CT + 3, GW // CT), (OFF_Q // CT + 6, GW // CT),
          (OFF_ZA // CT, C_W // CT))


def _rmsnorm_fwd(x, gain, token, tm=512):
    def body(x_ref, g_ref, _token, o_ref):
        xv = x_ref[...]
        var = jnp.mean(xv * xv, axis=-1, keepdims=True)
        o_ref[...] = (xv * lax.rsqrt(var + EPS) * g_ref[...]).astype(BF16)

    return pl.pallas_call(
        body, grid=(T // tm,), name="rmsnorm_fwd",
        in_specs=[pl.BlockSpec((tm, D), lambda i: (i, 0)), pl.BlockSpec((1, D), lambda i: (0, 0)),
                  pl.BlockSpec((8, LANES), lambda i: (0, 0))],
        out_specs=pl.BlockSpec((tm, D), lambda i: (i, 0)),
        out_shape=jax.ShapeDtypeStruct((T, D), BF16),
        compiler_params=_cp(("parallel",)),
    )(*_hbm(x, gain, token))


def _in_proj(hn, w_t, bias, tm=2048):
    def body(a_ref, w_ref, b_ref, o_ref):
        o_ref[...] = (_dot_nt(a_ref[...], w_ref[...]) + b_ref[...]).astype(BF16)

    return pl.pallas_call(
        body, grid=(T // tm, NCT), name="in_proj",
        in_specs=[pl.BlockSpec((tm, D), lambda i, j: (i, 0)),
                  pl.BlockSpec((CT, D), lambda i, j: (_perm(j), 0)),
                  pl.BlockSpec((1, CT), lambda i, j: (0, _perm(j)))],
        out_specs=pl.BlockSpec((tm, CT), lambda i, j: (i, j)),
        out_shape=jax.ShapeDtypeStruct((T, NIN), BF16),
        compiler_params=_cp(("parallel", "parallel"), VMEM_MID),
    )(*_hbm(hn, w_t, bias))


def _dhn(pieces, w_t, token, tm=512):
    def body(a_ref, q_ref, k_ref, v_ref, c_ref, w_hbm, _token, o_ref, w):
        @pl.when(pl.program_id(0) == 0)
        def _():
            pltpu.sync_copy(w_hbm, w)

        acc = _dot(a_ref[...], w[pl.ds(0, A_W), :])
        for kind, x_ref in enumerate((q_ref, k_ref, v_ref)):
            for g in range(NG):
                row = OFF_Q + (3 * g + kind) * CT
                acc = acc + _dot(x_ref[:, g * CT:(g + 1) * CT], w[pl.ds(row, CT), :])
        o_ref[...] = acc + _dot(c_ref[...], w[pl.ds(OFF_ZA, C_W), :])

    tok = lambda wd: pl.BlockSpec((tm, wd), lambda i: (i, 0))
    return pl.pallas_call(
        body, grid=(T // tm,), name="dhn",
        in_specs=[tok(A_W), tok(GW), tok(GW), tok(GW), tok(C_W), pl.BlockSpec(memory_space=pl.ANY),
                  pl.BlockSpec((8, LANES), lambda i: (0, 0))],
        out_specs=tok(D),
        out_shape=jax.ShapeDtypeStruct((T, D), F32),
        scratch_shapes=[pltpu.VMEM((NIN, D), BF16)],
        compiler_params=_cp(("arbitrary",), VMEM_BIG),
    )(*_hbm(*pieces, w_t, token))


def _dw_in(pieces, hn):
    names = ("a", "q", "k", "v", "c")
    dw = db = None
    for piece, (lo, n), tag in zip(pieces, PIECES, names):
        def body(x_ref, h_ref, *rest):
            o_ref, s_ref = rest[-2:]
            xv = x_ref[...]
            o_ref[...] = _dot_tn(xv, h_ref[...]).astype(BF16)
            s_ref[...] = jnp.sum(xv.astype(F32), axis=0, keepdims=True)

        in_specs = [pl.BlockSpec((T, CT), lambda j: (0, j)), pl.BlockSpec((T, D), lambda j: (0, 0))]
        args = [piece, hn]
        aliases = {}
        if dw is not None:
            in_specs += [pl.BlockSpec(memory_space=pl.ANY)] * 2
            args += [dw, db]
            aliases = {2: 0, 3: 1}
        dw, db = pl.pallas_call(
            body, grid=(n,), name="dw_in_" + tag,
            in_specs=in_specs,
            out_specs=[pl.BlockSpec((CT, D), lambda j, lo=lo: (_perm(lo + j), 0)),
                       pl.BlockSpec((1, CT), lambda j, lo=lo: (0, _perm(lo + j)))],
            out_shape=[jax.ShapeDtypeStruct((NIN, D), BF16), jax.ShapeDtypeStruct((1, NIN), F32)],
            input_output_aliases=aliases,
            compiler_params=_cp(("parallel",), VMEM_MID),
        )(*_hbm(*args))
    return dw, db


def _mm_tn(a, b, ta, tt, name):
    m = a.shape[1]
    n = b.shape[1]
    nt = T // tt

    def body(a_ref, b_ref, o_ref, acc):
        t = pl.program_id(1)
        p = _dot_tn(a_ref[...].astype(BF16), b_ref[...].astype(BF16))

        @pl.when(t == 0)
        def _():
            acc[...] = p

        @pl.when(t > 0)
        def _():
            acc[...] += p

        @pl.when(t == nt - 1)
        def _():
            o_ref[...] = acc[...].astype(BF16)

    return pl.pallas_call(
        body, grid=(m // ta, nt), name=name,
        in_specs=[pl.BlockSpec((tt, ta), lambda j, t: (t, j)), pl.BlockSpec((tt, n), lambda j, t: (t, 0))],
        out_specs=pl.BlockSpec((ta, n), lambda j, t: (j, 0)),
        out_shape=jax.ShapeDtypeStruct((m, n), BF16),
        scratch_shapes=[pltpu.VMEM((ta, n), F32)],
        compiler_params=_cp(("parallel", "arbitrary"), VMEM_MID),
    )(*_hbm(a, b))


def _row_iota():
    return lax.broadcasted_iota(jnp.int32, (S, RBW), 0)


def _shift_down(v, d, row, fill):
    return jnp.where(row >= d, pltpu.roll(v, d, 0), fill)


def _shift_up(v, d, row, fill):
    return jnp.where(row < S - d, pltpu.roll(v, S - d, 0), fill)


SUBLANES = 8


def _scan_down(a, u, row):
    d = 1
    while d < S:
        last = 2 * d >= S
        if d < SUBLANES:
            u = a * _shift_down(u, d, row, 0.0) + u
            if not last:
                a = a * _shift_down(a, d, row, 1.0)
        else:
            u = jnp.concatenate([u[:d], a[d:] * u[:S - d] + u[d:]], axis=0)
            if not last:
                a = jnp.concatenate([a[:d], a[d:] * a[:S - d]], axis=0)
        d *= 2
    return u


def _scan_up(b, g, row):
    d = 1
    while d < S:
        last = 2 * d >= S
        if d < SUBLANES:
            g = g + b * _shift_up(g, d, row, 0.0)
            if not last:
                b = b * _shift_up(b, d, row, 0.0)
        else:
            g = jnp.concatenate([g[:S - d] + b[:S - d] * g[d:], g[S - d:]], axis=0)
            if not last:
                b = jnp.concatenate([b[:S - d] * b[d:], b[S - d:]], axis=0)
        d *= 2
    return g


def _softplus(x):
    return jnp.maximum(x, 0.0) + jnp.log1p(jnp.exp(-jnp.abs(x)))


def _rnn_gates(x, cw, cb, wa, ba, wx, bx, lam, row):
    xc = cb + cw[3:4, :] * x
    for j in (1, 2, 3):
        xc = xc + cw[3 - j:4 - j, :] * _shift_down(x, j, row, 0.0)
    xcb = xc.astype(BF16)
    r = _sigmoid(_dot(xcb, wa) + ba)
    i = _sigmoid(_dot(xcb, wx) + bx)
    sp = _softplus(-lam)
    log_a = (-LRU_C) * r * sp
    a = jnp.exp(log_a)
    mult = jnp.where(row == 0, 1.0, jnp.sqrt(jnp.tanh(-log_a) * (1.0 + a * a)))
    return xc, xcb, r, i, sp, a, mult


def _rnn_fwd(proj3, conv_w, conv_b, wa, ba, wx, bx, lam):
    def body(x_ref, cw_ref, cb_ref, wa_ref, ba_ref, wx_ref, bx_ref, lam_ref, h_ref):
        row = _row_iota()
        x = x_ref[0].astype(F32)
        xc, _, _, i, _, a, mult = _rnn_gates(x, cw_ref[...], cb_ref[...], wa_ref[0], ba_ref[...],
                                             wx_ref[0], bx_ref[...], lam_ref[...], row)
        h_ref[0] = _scan_down(a, mult * (i * xc), row)

    vec = lambda: pl.BlockSpec((1, RBW), lambda b, n: (0, n))
    mat = lambda: pl.BlockSpec((1, RBW, RBW), lambda b, n: (n, 0, 0))
    return pl.pallas_call(
        body, grid=(BL, NRB), name="rnn_fwd",
        in_specs=[pl.BlockSpec((1, S, RBW), lambda b, n: (b, 0, n)),
                  pl.BlockSpec((CONVW, RBW), lambda b, n: (0, n)),
                  vec(), mat(), vec(), mat(), vec(), vec()],
        out_specs=pl.BlockSpec((1, S, RBW), lambda b, n: (b, 0, n)),
        out_shape=jax.ShapeDtypeStruct((BL, S, DR), F32),
        compiler_params=_cp(("parallel", "parallel"), VMEM_MID),
    )(*_hbm(proj3, conv_w, conv_b, wa, ba, wx, bx, lam))


def _rnn_bwd(proj3, h3, dh3, slab_a3, conv_w, conv_b, wa, ba, wx, bx, lam, token):
    def body(x_ref, h_ref, dh_ref, cw_ref, cb_ref, wa_ref, ba_ref, wx_ref, bx_ref, lam_ref, _alias, _token,
             dx_ref, dcw_ref, dcb_ref, dwa_ref, dba_ref, dwx_ref, dbx_ref, dlam_ref):
        row = _row_iota()
        x = x_ref[0].astype(F32)
        cw = cw_ref[...]
        wa_v = wa_ref[0]
        wx_v = wx_ref[0]
        lam_v = lam_ref[...]
        xc, xcb, r, i, sp, a, mult = _rnn_gates(x, cw, cb_ref[...], wa_v, ba_ref[...], wx_v, bx_ref[...], lam_v, row)
        h = h_ref[0]
        g = _scan_up(_shift_up(a, 1, row, 0.0), dh_ref[0], row)
        da = g * _shift_down(h, 1, row, 0.0)
        dmult = jnp.where(row == 0, 0.0, g * (i * xc))
        gm = g * mult
        di = gm * xc
        dxc = gm * i
        dlog_a = da * a - dmult * (a * a) / mult
        dr = dlog_a * ((-LRU_C) * sp)
        dsp = jnp.sum(dlog_a * ((-LRU_C) * r), axis=0, keepdims=True)
        dlam = dsp * (-_sigmoid(-lam_v))
        dpa = dr * r * (1.0 - r)
        dpx = di * i * (1.0 - i)
        dpab = dpa.astype(BF16)
        dpxb = dpx.astype(BF16)
        dwa = _dot_tn(xcb, dpab)
        dwx = _dot_tn(xcb, dpxb)
        dba = jnp.sum(dpa, axis=0, keepdims=True)
        dbx = jnp.sum(dpx, axis=0, keepdims=True)
        dxc = dxc + _dot_nt(dpab, wa_v) + _dot_nt(dpxb, wx_v)
        dcb = jnp.sum(dxc, axis=0, keepdims=True)
        dx = cw[3:4, :] * dxc
        dcw_rows = [None] * CONVW
        dcw_rows[3] = jnp.sum(dxc * x, axis=0, keepdims=True)
        for j in (1, 2, 3):
            dx = dx + cw[3 - j:4 - j, :] * _shift_up(dxc, j, row, 0.0)
            dcw_rows[3 - j] = jnp.sum(dxc * _shift_down(x, j, row, 0.0), axis=0, keepdims=True)
        dx_ref[0] = dx.astype(BF16)
        dcw = jnp.concatenate(dcw_rows, axis=0)
        first = pl.program_id(1) == 0

        @pl.when(first)
        def _():
            dcw_ref[...] = dcw
            dcb_ref[...] = dcb
            dwa_ref[0] = dwa
            dba_ref[...] = dba
            dwx_ref[0] = dwx
            dbx_ref[...] = dbx
            dlam_ref[...] = dlam

        @pl.when(jnp.logical_not(first))
        def _():
            dcw_ref[...] += dcw
            dcb_ref[...] += dcb
            dwa_ref[0] += dwa
            dba_ref[...] += dba
            dwx_ref[0] += dwx
            dbx_ref[...] += dbx
            dlam_ref[...] += dlam

    slab = lambda: pl.BlockSpec((1, S, RBW), lambda n, b: (b, 0, n))
    vec = lambda: pl.BlockSpec((1, RBW), lambda n, b: (0, n))
    mat = lambda: pl.BlockSpec((1, RBW, RBW), lambda n, b: (n, 0, 0))
    taps = lambda: pl.BlockSpec((CONVW, RBW), lambda n, b: (0, n))
    vshape = jax.ShapeDtypeStruct((1, DR), F32)
    mshape = jax.ShapeDtypeStruct((NRB, RBW, RBW), F32)
    return pl.pallas_call(
        body, grid=(NRB, BL), name="rnn_bwd",
        in_specs=[slab(), slab(), slab(), taps(), vec(), mat(), vec(), mat(), vec(), vec(),
                  pl.BlockSpec(memory_space=pl.ANY), pl.BlockSpec((8, LANES), lambda n, b: (0, 0))],
        out_specs=[slab(), taps(), vec(), mat(), vec(), mat(), vec(), vec()],
        out_shape=[jax.ShapeDtypeStruct((BL, S, A_W), BF16), jax.ShapeDtypeStruct((CONVW, DR), F32),
                   vshape, mshape, vshape, mshape, vshape, vshape],
        input_output_aliases={10: 0},
        compiler_params=_cp(("parallel", "arbitrary"), 48 * 1024 * 1024),
    )(*_hbm(proj3, h3, dh3, conv_w, conv_b, wa, ba, wx, bx, lam, slab_a3, token))


NQB = S // QB


def _rms_head(t, gain):
    rstd = lax.rsqrt(jnp.mean(t * t, axis=-1, keepdims=True) + EPS)
    return t * rstd * gain


def _rope(t, cs, sn):
    return t * cs + pltpu.roll(t, HD // 2, 1) * sn


def _rope_t(dy, cs, sn):
    return dy * cs - pltpu.roll(dy, HD // 2, 1) * sn


def _bdot_nt(a, b):
    return lax.dot_general(a, b, (((2,), (2,)), ((0,), (0,))), preferred_element_type=F32)


def _bdot(a, b):
    return lax.dot_general(a, b, (((2,), (1,)), ((0,), (0,))), preferred_element_type=F32)


def _bdot_tn(a, b):
    return lax.dot_general(a, b, (((1,), (1,)), ((0,), (0,))), preferred_element_type=F32)


STRIDE_MAX = 4


def _permute(buf, x, dil, dst, off=0):
    ln = S // dil
    if dil == 1:
        dst[pl.ds(off, S), :] = x.astype(dst.dtype)
        return
    buf[0] = x
    if dil <= STRIDE_MAX:
        for c in range(dil):
            dst[pl.ds(off + c * ln, ln), :] = buf.at[0][pl.ds(c, ln, stride=dil), :].astype(dst.dtype)
        return
    f, r = STRIDE_MAX, dil // STRIDE_MAX
    part = S // f
    for c1 in range(f):
        buf.at[1][pl.ds(c1 * part, part), :] = buf.at[0][pl.ds(c1, part, stride=f), :]
    for c1 in range(f):
        for c2 in range(r):
            dst[pl.ds(off + (c1 + f * c2) * ln, ln), :] = (
                buf.at[1][pl.ds(c1 * part + c2, ln, stride=r), :].astype(dst.dtype))


def _unpermute(buf, xp, dil, dst):
    ln = S // dil
    if dil == 1:
        dst[...] = xp
        return
    if dil <= STRIDE_MAX:
        for c in range(dil):
            dst[pl.ds(c, ln, stride=dil), :] = xp[c * ln:(c + 1) * ln]
        return
    f, r = STRIDE_MAX, dil // STRIDE_MAX
    part = S // f
    for c1 in range(f):
        for c2 in range(r):
            c = c1 + f * c2
            buf.at[1][pl.ds(c1 * part + c2, ln, stride=r), :] = xp[c * ln:(c + 1) * ln]
    for c1 in range(f):
        dst[pl.ds(c1, part, stride=f), :] = buf[1, pl.ds(c1 * part, part), :]


def _blocks3(ref, off=0):
    return ref[pl.ds(off, S), :].reshape(NQB, QB, HD)


def _att_prep(q_ref, k_ref, v_ref, cos_ref, sin_ref, qn, kn, dil, nat, qs, ksp, vsp):
    cs = cos_ref[...]
    sn = sin_ref[...]
    zero = jnp.zeros((QB, HD), BF16)
    ksp[pl.ds(0, QB), :] = zero
    vsp[pl.ds(0, QB), :] = zero
    _permute(nat, _rope(_rms_head(q_ref[0].astype(F32), qn), cs, sn), dil, qs)
    _permute(nat, _rope(_rms_head(k_ref[0].astype(F32), kn), cs, sn), dil, ksp, QB)
    _permute(nat, v_ref[0].astype(F32), dil, vsp, QB)


def _att_scores(qs, ksp, dil):
    nb = S // dil // QB
    q3 = _blocks3(qs)
    shape = (NQB, QB, QB)
    qi = lax.broadcasted_iota(jnp.int32, shape, 1)
    kj = lax.broadcasted_iota(jnp.int32, shape, 2)
    s_c = jnp.where(qi >= kj, _bdot_nt(q3, _blocks3(ksp, QB)) * SCALE, NEG)
    if nb == 1:
        return q3, s_c, None
    jj = lax.broadcasted_iota(jnp.int32, shape, 0)
    ok = (kj >= qi) & ((jj & (nb - 1)) != 0)
    s_p = jnp.where(ok, _bdot_nt(q3, _blocks3(ksp)) * SCALE, NEG)
    return q3, s_c, s_p


def _qkv_spec(kind, g):
    base = OFF_Q // HD + kind * (GW // HD) + g * NH
    return pl.BlockSpec((1, S, HD), lambda b, h: (b, 0, base + h))


def _attn_fwd(proj3, cos_t, sin_t, q_norm, k_norm):
    def body(*refs):
        qkv_refs = refs[:9]
        cos_ref, sin_ref, qn_ref, kn_ref, att_ref, lse_ref, w_ref, nat, qs, ksp, vsp, og = refs[9:]
        for g, (window, dil) in enumerate(PATTERNS):
            q_ref, k_ref, v_ref = qkv_refs[3 * g:3 * g + 3]
            _att_prep(q_ref, k_ref, v_ref, cos_ref, sin_ref, qn_ref[g:g + 1, :], kn_ref[g:g + 1, :], dil,
                      nat, qs, ksp, vsp)
            _, s_c, s_p = _att_scores(qs, ksp, dil)
            m = jnp.max(s_c, axis=-1, keepdims=True)
            if s_p is not None:
                m = jnp.maximum(m, jnp.max(s_p, axis=-1, keepdims=True))
            e_c = jnp.exp(s_c - m)
            den = jnp.sum(e_c, axis=-1, keepdims=True)
            o = _bdot(e_c.astype(BF16), _blocks3(vsp, QB))
            if s_p is not None:
                e_p = jnp.exp(s_p - m)
                den = den + jnp.sum(e_p, axis=-1, keepdims=True)
                o = o + _bdot(e_p.astype(BF16), _blocks3(vsp))
            _unpermute(nat, (o / den).reshape(S, HD), dil, og.at[g])
            _unpermute(nat, jnp.broadcast_to(m + jnp.log(den), (NQB, QB, HD)).reshape(S, HD), dil,
                       lse_ref.at[g, 0])
        l0 = lse_ref[0, 0]
        l1 = lse_ref[1, 0]
        l2 = lse_ref[2, 0]
        mx = jnp.maximum(jnp.maximum(l0, l1), l2)
        e0 = jnp.exp(l0 - mx)
        e1 = jnp.exp(l1 - mx)
        e2 = jnp.exp(l2 - mx)
        inv = 1.0 / (e0 + e1 + e2)
        w0 = e0 * inv
        w1 = e1 * inv
        w2 = e2 * inv
        w_ref[0, 0] = w0
        w_ref[1, 0] = w1
        w_ref[2, 0] = w2
        att_ref[0] = w0 * og[0] + w1 * og[1] + w2 * og[2]

    in_specs = [_qkv_spec(kind, g) for g in range(NG) for kind in range(3)]
    in_specs += [pl.BlockSpec((S, HD), lambda b, h: (0, 0)), pl.BlockSpec((S, HD), lambda b, h: (0, 0)),
                 pl.BlockSpec((NG, HD), lambda b, h: (0, 0)), pl.BlockSpec((NG, HD), lambda b, h: (0, 0))]
    stat = lambda: pl.BlockSpec((NG, 1, S, HD), lambda b, h: (0, b, 0, h))
    return pl.pallas_call(
        body, grid=(BL, NH), name="attn_fwd",
        in_specs=in_specs,
        out_specs=[pl.BlockSpec((1, S, HD), lambda b, h: (b, 0, h)), stat(), stat()],
        out_shape=[jax.ShapeDtypeStruct((BL, S, ATT), F32),
                   jax.ShapeDtypeStruct((NG, BL, S, ATT), F32),
                   jax.ShapeDtypeStruct((NG, BL, S, ATT), F32)],
        scratch_shapes=[pltpu.VMEM((2, S, HD), F32), pltpu.VMEM((S, HD), BF16), pltpu.VMEM((S + QB, HD), BF16),
                        pltpu.VMEM((S + QB, HD), BF16), pltpu.VMEM((NG, S, HD), F32)],
        compiler_params=_cp(("parallel", "parallel"), VMEM_BIG),
    )(*_hbm(*([proj3] * 9), cos_t, sin_t, q_norm, k_norm))


def _attn_bwd_group(g, proj3, cos_t, sin_t, qn_g, kn_g, lse, wts, datt3, sbar3, slabs):
    dil = PATTERNS[g][1]
    n_alias = 0 if slabs is None else 3

    def norm_rope_bwd(dpost, raw, gain, cs, sn):
        dn = _rope_t(dpost, cs, sn)
        rstd = lax.rsqrt(jnp.mean(raw * raw, axis=-1, keepdims=True) + EPS)
        xh = raw * rstd
        dgain = jnp.sum(dn * xh, axis=0, keepdims=True)
        gd = dn * gain
        draw = rstd * (gd - xh * jnp.mean(gd * xh, axis=-1, keepdims=True))
        return draw, dgain

    def body(*refs):
        (q_ref, k_ref, v_ref, cos_ref, sin_ref, qn_ref, kn_ref, lse_ref, w_ref, datt_ref, sbar_ref) = refs[:11]
        (dq_ref, dk_ref, dv_ref, dqn_ref, dkn_ref, nat, qs, ksp, vsp, dos, cvp, lsp, acc) = refs[11 + n_alias:]
        qn = qn_ref[...]
        kn = kn_ref[...]
        cs = cos_ref[...]
        sn = sin_ref[...]
        _att_prep(q_ref, k_ref, v_ref, cos_ref, sin_ref, qn, kn, dil, nat, qs, ksp, vsp)
        wv = w_ref[0, 0]
        _permute(nat, wv * datt_ref[0], dil, dos)
        _permute(nat, wv * sbar_ref[0], dil, cvp)
        _permute(nat, lse_ref[0, 0], dil, lsp)
        q3, s_c, s_p = _att_scores(qs, ksp, dil)
        do3 = _blocks3(dos)
        lse3 = _blocks3(lsp)[:, :, 0:1]
        cv3 = _blocks3(cvp)[:, :, 0:1]
        p_c = jnp.exp(s_c - lse3)
        ds_c = (p_c * (_bdot_nt(do3, _blocks3(vsp, QB)) - cv3)).astype(BF16)
        dq = _bdot(ds_c, _blocks3(ksp, QB))
        acc[0] = _bdot_tn(ds_c, q3).reshape(S, HD)
        acc[1] = _bdot_tn(p_c.astype(BF16), do3).reshape(S, HD)
        if s_p is not None:
            p_p = jnp.exp(s_p - lse3)
            ds_p = (p_p * (_bdot_nt(do3, _blocks3(vsp)) - cv3)).astype(BF16)
            dq = dq + _bdot(ds_p, _blocks3(ksp))
            early = pl.ds(0, S - QB)
            acc[0, early, :] += _bdot_tn(ds_p, q3).reshape(S, HD)[QB:]
            acc[1, early, :] += _bdot_tn(p_p.astype(BF16), do3).reshape(S, HD)[QB:]
        _unpermute(nat, (dq * SCALE).reshape(S, HD), dil, nat.at[0])
        draw, dqn = norm_rope_bwd(nat[0], q_ref[0].astype(F32), qn, cs, sn)
        dq_ref[0] = draw.astype(BF16)
        _unpermute(nat, acc[0] * SCALE, dil, nat.at[0])
        draw, dkn = norm_rope_bwd(nat[0], k_ref[0].astype(F32), kn, cs, sn)
        dk_ref[0] = draw.astype(BF16)
        _unpermute(nat, acc[1], dil, nat.at[0])
        dv_ref[0] = nat[0].astype(BF16)
        first = (pl.program_id(0) == 0) & (pl.program_id(1) == 0)

        @pl.when(first)
        def _():
            dqn_ref[...] = dqn
            dkn_ref[...] = dkn

        @pl.when(jnp.logical_not(first))
        def _():
            dqn_ref[...] += dqn
            dkn_ref[...] += dkn

    full = lambda r: pl.BlockSpec((r, HD), lambda b, h: (0, 0))
    stat = lambda: pl.BlockSpec((1, 1, S, HD), lambda b, h: (g, b, 0, h))
    slab = lambda: pl.BlockSpec((1, S, HD), lambda b, h: (b, 0, h))
    out_slab = lambda: pl.BlockSpec((1, S, HD), lambda b, h: (b, 0, g * NH + h))
    big = jax.ShapeDtypeStruct((BL, S, GW), BF16)
    vecs = jax.ShapeDtypeStruct((1, HD), F32)
    in_specs = [_qkv_spec(0, g), _qkv_spec(1, g), _qkv_spec(2, g), full(S), full(S), full(1), full(1),
                stat(), stat(), slab(), slab()]
    args = [proj3, proj3, proj3, cos_t, sin_t, qn_g, kn_g, lse, wts, datt3, sbar3]
    aliases = {}
    if slabs is not None:
        in_specs += [pl.BlockSpec(memory_space=pl.ANY)] * 3
        args += list(slabs)
        aliases = {11: 0, 12: 1, 13: 2}
    return pl.pallas_call(
        body, grid=(BL, NH), name="attn_bwd_g%d" % g,
        in_specs=in_specs,
        out_specs=[out_slab(), out_slab(), out_slab(), full(1), full(1)],
        out_shape=[big, big, big, vecs, vecs],
        scratch_shapes=[pltpu.VMEM((2, S, HD), F32), pltpu.VMEM((S, HD), BF16), pltpu.VMEM((S + QB, HD), BF16),
                        pltpu.VMEM((S + QB, HD), BF16), pltpu.VMEM((S, HD), BF16), pltpu.VMEM((S, HD), F32),
                        pltpu.VMEM((S, HD), F32), pltpu.VMEM((2, S, HD), F32)],
        input_output_aliases=aliases,
        compiler_params=_cp(("arbitrary", "arbitrary"), VMEM_BIG),
    )(*_hbm(*args))


def _tail(x, proj, h, att, p, tgt, w_o_rnn, w_o_att_t, w_out, w_pg, w_ple_t, norm_ple, b_pg, tm=256):
    nt = T // tm
    inv_d = 1.0 / D

    def body(x_ref, h_ref, zr_ref, att_ref, za_ref, g0a_ref, g0b_ref, g1a_ref, g1b_ref, p_ref, tgt_ref,
             np_ref, bpg_ref, wor_hbm, woa_hbm, wout_hbm, wpg_hbm, wple_hbm,
             dx1_ref, merged_ref, n1_ref, dpre_ref, dpe_ref, dyr_ref, dya_ref, slab_a_ref, slab_c_ref, dh_ref,
             datt_ref, sbar_ref, yrnn_ref, yatt_ref, loss_ref, dnp_ref, dbpg_ref,
             wor, woa, wout, wpg, wple):
        first = pl.program_id(0) == 0

        @pl.when(first)
        def _():
            pltpu.sync_copy(wor_hbm, wor)
            pltpu.sync_copy(woa_hbm, woa)
            pltpu.sync_copy(wout_hbm, wout)
            pltpu.sync_copy(wpg_hbm, wpg)
            pltpu.sync_copy(wple_hbm, wple)

        xv = x_ref[...]
        hv = h_ref[...]
        zr = zr_ref[...].astype(F32)
        av = att_ref[...]
        za = za_ref[...].astype(F32)
        szr = _sigmoid(zr)
        silu_r = zr * szr
        yrnn_b = (hv * silu_r).astype(BF16)
        sza = _sigmoid(za)
        silu_a = za * sza
        yatt_b = (av * silu_a).astype(BF16)
        yrnn_ref[...] = yrnn_b
        yatt_ref[...] = yatt_b
        yr = _dot(yrnn_b, wor[...])
        ya = _dot_nt(yatt_b, woa[...])
        g0 = _sigmoid(jnp.concatenate([g0a_ref[...], g0b_ref[...]], axis=1).astype(F32))
        g1 = _sigmoid(jnp.concatenate([g1a_ref[...], g1b_ref[...]], axis=1).astype(F32))
        merged_b = (g0 * yr + g1 * ya).astype(BF16)
        merged_ref[...] = merged_b
        x1 = xv + _dot(merged_b, wout[...])
        rstd = lax.rsqrt(jnp.mean(x1 * x1, axis=-1, keepdims=True) + EPS)
        xh = x1 * rstd
        npl = np_ref[...]
        n1_b = (xh * npl).astype(BF16)
        n1_ref[...] = n1_b
        pg = _sigmoid(_dot(n1_b, wpg[...]) + bpg_ref[...])
        pe = _dot_nt(p_ref[...].astype(BF16), wple[...])
        err = x1 + pg * pe - tgt_ref[...]
        loss_t = 0.5 * inv_d * jnp.sum(err * err)
        dy = err * inv_d
        dpe_ref[...] = (dy * pg).astype(BF16)
        dpre = dy * pe * pg * (1.0 - pg)
        dpre_b = dpre.astype(BF16)
        dpre_ref[...] = dpre_b
        dn1 = _dot_nt(dpre_b, wpg[...])
        dnp = jnp.sum(dn1 * xh, axis=0, keepdims=True)
        dbpg = jnp.sum(dpre, axis=0, keepdims=True)
        gd = dn1 * npl
        dx1 = dy + rstd * (gd - xh * jnp.mean(gd * xh, axis=-1, keepdims=True))
        dx1_ref[...] = dx1
        dmerged = _dot_nt(dx1.astype(BF16), wout[...])
        dyr_b = (dmerged * g0).astype(BF16)
        dya_b = (dmerged * g1).astype(BF16)
        dyr_ref[...] = dyr_b
        dya_ref[...] = dya_b
        slab_c_ref[:, ATT:ATT + D] = (dmerged * yr * g0 * (1.0 - g0)).astype(BF16)
        slab_c_ref[:, ATT + D:ATT + 2 * D] = (dmerged * ya * g1 * (1.0 - g1)).astype(BF16)
        dyrnn = _dot_nt(dyr_b, wor[...])
        dyatt = _dot(dya_b, woa[...])
        dh_ref[...] = dyrnn * silu_r
        slab_a_ref[...] = (dyrnn * hv * szr * (1.0 + zr * (1.0 - szr))).astype(BF16)
        datt = dyatt * silu_a
        datt_ref[...] = datt
        slab_c_ref[:, 0:ATT] = (dyatt * av * sza * (1.0 + za * (1.0 - sza))).astype(BF16)
        da = datt * av
        for hh in range(NH):
            seg = slice(hh * HD, (hh + 1) * HD)
            sbar_ref[:, seg] = jnp.broadcast_to(jnp.sum(da[:, seg], axis=-1, keepdims=True), (tm, HD))

        @pl.when(first)
        def _():
            loss_ref[...] = jnp.full((8, LANES), loss_t, F32)
            dnp_ref[...] = dnp
            dbpg_ref[...] = dbpg

        @pl.when(jnp.logical_not(first))
        def _():
            loss_ref[...] += jnp.full((8, LANES), loss_t, F32)
            dnp_ref[...] += dnp
            dbpg_ref[...] += dbpg

    tok = lambda w: pl.BlockSpec((tm, w), lambda i: (i, 0))
    col = lambda w, blk: pl.BlockSpec((tm, w), lambda i: (i, blk))
    vec = lambda: pl.BlockSpec((1, D), lambda i: (0, 0))
    hbm = lambda: pl.BlockSpec(memory_space=pl.ANY)
    gb = OFF_G // 512
    in_specs = [tok(D), tok(DR), col(DR, 1), tok(ATT), col(ATT, OFF_ZA // ATT),
                col(512, gb), col(512, gb + 1), col(512, gb + 2), col(512, gb + 3),
                tok(PLE), tok(D), vec(), vec(), hbm(), hbm(), hbm(), hbm(), hbm()]
    sh = lambda w, dt: jax.ShapeDtypeStruct((T, w), dt)
    out_shape = [sh(D, F32), sh(D, BF16), sh(D, BF16), sh(D, BF16), sh(D, BF16), sh(D, BF16), sh(D, BF16),
                 sh(A_W, BF16), sh(C_W, BF16), sh(DR, F32), sh(ATT, F32), sh(ATT, F32),
                 sh(DR, BF16), sh(ATT, BF16),
                 jax.ShapeDtypeStruct((8, LANES), F32), jax.ShapeDtypeStruct((1, D), F32),
                 jax.ShapeDtypeStruct((1, D), F32)]
    out_specs = [tok(D), tok(D), tok(D), tok(D), tok(D), tok(D), tok(D), col(DR, 1), tok(C_W), tok(DR),
                 tok(ATT), tok(ATT), tok(DR), tok(ATT),
                 pl.BlockSpec((8, LANES), lambda i: (0, 0)), vec(), vec()]
    return pl.pallas_call(
        body, grid=(nt,), name="tail_fwd_bwd",
        in_specs=in_specs, out_specs=out_specs, out_shape=out_shape,
        scratch_shapes=[pltpu.VMEM((DR, D), BF16), pltpu.VMEM((D, ATT), BF16), pltpu.VMEM((D, D), BF16),
                        pltpu.VMEM((D, D), BF16), pltpu.VMEM((D, PLE), BF16)],
        compiler_params=_cp(("arbitrary",), VMEM_BIG),
    )(*_hbm(x, h, proj, att, proj, proj, proj, proj, proj, p, tgt, norm_ple, b_pg, w_o_rnn, w_o_att_t, w_out, w_pg,
            w_ple_t))


def _input_norm_bwd(x, dhn, dx1, gain, tm=512):
    def body(x_ref, dhn_ref, dx1_ref, g_ref, dx_ref, dg_ref):
        xv = x_ref[...]
        rstd = lax.rsqrt(jnp.mean(xv * xv, axis=-1, keepdims=True) + EPS)
        xh = xv * rstd
        dn = dhn_ref[...]
        dg = jnp.sum(dn * xh, axis=0, keepdims=True)
        gd = dn * g_ref[...]
        dx_ref[...] = dx1_ref[...] + rstd * (gd - xh * jnp.mean(gd * xh, axis=-1, keepdims=True))
        first = pl.program_id(0) == 0

        @pl.when(first)
        def _():
            dg_ref[...] = dg

        @pl.when(jnp.logical_not(first))
        def _():
            dg_ref[...] += dg

    tok = lambda: pl.BlockSpec((tm, D), lambda i: (i, 0))
    vec = lambda: pl.BlockSpec((1, D), lambda i: (0, 0))
    return pl.pallas_call(
        body, grid=(T // tm,), name="input_norm_bwd",
        in_specs=[tok(), tok(), tok(), vec()], out_specs=[tok(), vec()],
        out_shape=[jax.ShapeDtypeStruct((T, D), F32), jax.ShapeDtypeStruct((1, D), F32)],
        compiler_params=_cp(("arbitrary",), VMEM_MID),
    )(*_hbm(x, dhn, dx1, gain))


def _rope_tables():
    pos = jnp.arange(S, dtype=F32)
    inv_freq = ROPE_THETA ** (-jnp.arange(0, HD, 2, dtype=F32) / HD)
    ang = pos[:, None] * inv_freq[None, :]
    cos, sin = jnp.cos(ang), jnp.sin(ang)
    return jnp.concatenate([cos, cos], axis=1), jnp.concatenate([-sin, sin], axis=1)


def _local_step(x, p, tgt, w_in_t, other_weights, norm_mix, b_in, conv_b,
                w_rg_a, b_rg_a, w_rg_x, b_rg_x, lam, q_norm, k_norm, norm_ple, b_pg, start_reduce=None,
                entry_token=None):
    if start_reduce is None:
        start_reduce = lambda arrs, tag: (jnp.zeros((8, LANES), F32), arrs)
    if entry_token is None:
        entry_token = jnp.zeros((8, LANES), F32)
    cos_t, sin_t = _rope_tables()
    wa_b = w_rg_a.astype(BF16)
    wx_b = w_rg_x.astype(BF16)

    hn = _rmsnorm_fwd(x, norm_mix, entry_token)
    proj = _in_proj(hn, w_in_t, b_in)
    w_o_rnn, w_o_att_t, w_out, w_pg, w_ple_t, conv_w = other_weights(proj)
    proj3 = proj.reshape(BL, S, NIN)
    h3 = _rnn_fwd(proj3, conv_w, conv_b, wa_b, b_rg_a, wx_b, b_rg_x, lam)
    att3, lse, wts = _attn_fwd(proj3, cos_t, sin_t, q_norm, k_norm)
    (dx1, merged, n1, dpre, dpe, dyr, dya, slab_a, slab_c, dh, datt, sbar, yrnn, yatt, loss8, dnp, dbpg) = _tail(
        x, proj, h3.reshape(T, DR), att3.reshape(T, ATT), p, tgt, w_o_rnn, w_o_att_t, w_out, w_pg, w_ple_t,
        norm_ple, b_pg)

    token, pending_out = start_reduce([
        _mm_tn(yrnn, dyr, 640, 2048, "dw_o_rnn"),
        _mm_tn(dya, yatt, 512, 2048, "dw_o_att_t"),
        _mm_tn(merged, dx1, 512, 2048, "dw_out"),
        _mm_tn(n1, dpre, 512, 2048, "dw_ple_gate"),
        _mm_tn(dpe, p, 512, 2048, "dw_ple_t")], "out")

    slab_a3, dcw, dcb, dwa, dba, dwx, dbx, dlam = _rnn_bwd(
        proj3, h3, dh.reshape(BL, S, DR), slab_a.reshape(BL, S, A_W), conv_w, conv_b, wa_b, b_rg_a, wx_b, b_rg_x, lam,
        token)
    datt3 = datt.reshape(BL, S, ATT)
    sbar3 = sbar.reshape(BL, S, ATT)
    slabs = None
    dqn = []
    dkn = []
    for g in range(NG):
        dq, dk, dv, dqn_g, dkn_g = _attn_bwd_group(g, proj3, cos_t, sin_t, q_norm[g:g + 1], k_norm[g:g + 1],
                                                   lse, wts, datt3, sbar3, slabs)
        slabs = (dq, dk, dv)
        dqn.append(dqn_g)
        dkn.append(dkn_g)
    pieces = [slab_a3.reshape(T, A_W)] + [t.reshape(T, GW) for t in slabs] + [slab_c]
    dw_in_t, db_in = _dw_in(pieces, hn)
    token, pending_in = start_reduce([dw_in_t], "in")
    dhn = _dhn(pieces, w_in_t, token)
    grad_x, dnm = _input_norm_bwd(x, dhn, dx1, norm_mix)

    small = dict(w_rg_a=dwa, w_rg_x=dwx, norm_mix=dnm, b_in=db_in, conv_b=dcb, b_rg_a=dba, b_rg_x=dbx,
                 lru_lambda=dlam, q_norm=dqn, k_norm=dkn, norm_ple=dnp, b_ple_gate=dbpg, conv_w=dcw)
    return loss8[0, 0], grad_x, pending_out, pending_in, small


MESH = pl.DeviceIdType.MESH
HBM_SPEC = pl.BlockSpec(memory_space=pl.ANY)


def _my_pos():
    return lax.axis_index("x"), lax.axis_index("y"), lax.axis_index("c")


def _flip(pos, k):
    x, y, c = pos
    return (1 - x if k & 4 else x, 1 - y if k & 2 else y, 1 - c if k & 1 else c)


def _lin(pos):
    return 4 * pos[0] + 2 * pos[1] + pos[2]


def _chip(pos):
    return 2 * pos[0] + pos[1]


def _all_gather_two_level(shards, name):
    na = len(shards)

    def body(*refs):
        x_refs = refs[:na]
        out_refs = refs[na:2 * na]
        send_sems, recv_sems, local_sems = refs[2 * na:]
        me = _my_pos()
        sibling = _flip(me, 1)
        chips = [_flip(me, 4), _flip(me, 2), _flip(me, 6)]

        def copy(i, k, block, to, from_x=False):
            dst = out_refs[i].at[_lin(block)]
            return pltpu.make_async_remote_copy(
                src_ref=x_refs[i] if from_x else dst, dst_ref=dst,
                send_sem=send_sems.at[7 * i + k], recv_sem=recv_sems.at[7 * i + k], device_id=to, device_id_type=MESH)

        started = []
        for i in range(na):
            mine = pltpu.make_async_copy(x_refs[i], out_refs[i].at[_lin(me)], local_sems.at[i])
            mine.start()
            started.append(mine)
        sends = []
        for i in range(na):
            cps = [copy(i, 0, me, sibling, True)] + [copy(i, 1 + j, me, chip, True) for j, chip in enumerate(chips)]
            for cp in cps:
                cp.start()
            sends += cps
        for i in range(na):
            for j, chip in enumerate(chips):
                copy(i, 1 + j, chip, me).wait_recv()
                fwd = copy(i, 4 + j, chip, sibling)
                fwd.start()
                sends.append(fwd)
        for i in range(na):
            copy(i, 0, sibling, me).wait_recv()
            for j, chip in enumerate(chips):
                copy(i, 4 + j, _flip(chip, 1), me).wait_recv()
        for cp in sends:
            cp.wait_send()
        for mine in started:
            mine.wait()

    return pl.pallas_call(
        body, name=name,
        out_shape=[jax.ShapeDtypeStruct((NDEV,) + s.shape, s.dtype) for s in shards],
        in_specs=[HBM_SPEC] * na, out_specs=[HBM_SPEC] * na,
        scratch_shapes=[pltpu.SemaphoreType.DMA((7 * na,)), pltpu.SemaphoreType.DMA((7 * na,)),
                        pltpu.SemaphoreType.DMA((na,))],
    )(*shards)


def _all_gather_direct(shard, name):
    def body(x_ref, out_ref, send_sems, recv_sems, local_sem):
        me = _my_pos()
        mine = pltpu.make_async_copy(x_ref, out_ref.at[_lin(me)], local_sem)
        mine.start()
        sends = []
        for k in range(1, NDEV):
            cp = pltpu.make_async_remote_copy(
                src_ref=x_ref, dst_ref=out_ref.at[_lin(me)], send_sem=send_sems.at[k - 1],
                recv_sem=recv_sems.at[k - 1], device_id=_flip(me, k), device_id_type=MESH)
            cp.start()
            sends.append(cp)
        for k in range(1, NDEV):
            peer = _flip(me, k)
            pltpu.make_async_remote_copy(
                src_ref=x_ref, dst_ref=out_ref.at[_lin(peer)], send_sem=send_sems.at[k - 1],
                recv_sem=recv_sems.at[k - 1], device_id=peer, device_id_type=MESH).wait_recv()
        for cp in sends:
            cp.wait_send()
        mine.wait()

    return pl.pallas_call(
        body, name=name,
        out_shape=jax.ShapeDtypeStruct((NDEV,) + shard.shape, shard.dtype),
        in_specs=[HBM_SPEC], out_specs=HBM_SPEC,
        scratch_shapes=[pltpu.SemaphoreType.DMA((7,)), pltpu.SemaphoreType.DMA((7,)), pltpu.SemaphoreType.DMA],
    )(shard)


def _exchange_within_chip(parts, name):
    na = len(parts)

    def body(*refs):
        a_refs = refs[:na]
        recv_refs = refs[na:2 * na]
        send_sems, recv_sems = refs[2 * na:]
        me = _my_pos()
        c = me[2]
        sibling = _flip(me, 1)
        remote = []
        for i in range(na):
            for q in range(NCHIP):
                rc = pltpu.make_async_remote_copy(
                    src_ref=a_refs[i].at[q, 1 - c], dst_ref=recv_refs[i].at[q],
                    send_sem=send_sems.at[NCHIP * i + q], recv_sem=recv_sems.at[NCHIP * i + q],
                    device_id=sibling, device_id_type=MESH)
                rc.start()
                remote.append(rc)
        for rc in remote:
            rc.wait_recv()
        for rc in remote:
            rc.wait_send()

    return pl.pallas_call(
        body, name=name, out_shape=[jax.ShapeDtypeStruct((NCHIP,) + a.shape[2:], a.dtype) for a in parts],
        in_specs=[HBM_SPEC] * na, out_specs=[HBM_SPEC] * na,
        scratch_shapes=[pltpu.SemaphoreType.DMA((NCHIP * na,)), pltpu.SemaphoreType.DMA((NCHIP * na,))],
    )(*parts)


HBM_ONLY = pl.BlockSpec(memory_space=pltpu.HBM)
SEM_SPEC = pl.BlockSpec(memory_space=pltpu.SEMAPHORE)
SPLIT_COPY = pltpu.CompilerParams(has_side_effects=pltpu.SideEffectType.DATAFLOW_SIDE_EFFECTING)


def _chip_peers(me):
    return [_flip(me, 4), _flip(me, 2), _flip(me, 6)]


def _between_chips_start(parts, name):
    na = len(parts)

    def body(*refs):
        a_refs = refs[:na]
        land_refs = refs[na:2 * na]
        send_sems, recv_sems = refs[2 * na], refs[2 * na + 1]
        token = refs[-1]
        me = _my_pos()
        myq = _chip(me)
        for i in range(na):
            for j, peer in enumerate(_chip_peers(me)):
                pltpu.make_async_remote_copy(
                    src_ref=a_refs[i].at[_chip(peer)], dst_ref=land_refs[i].at[myq],
                    send_sem=send_sems.at[3 * i + j], recv_sem=recv_sems.at[3 * i + j],
                    device_id=peer, device_id_type=MESH).start()
        token[...] = jnp.zeros_like(token)

    hbm = [pltpu.HBM(a.shape, a.dtype) for a in parts]
    srcs = [pltpu.with_memory_space_constraint(a, pltpu.HBM) for a in parts]
    lands = [pltpu.with_memory_space_constraint(lax.empty(a.shape, a.dtype), pltpu.HBM) for a in parts]
    res = pl.pallas_call(
        body, name=name,
        out_shape=(pltpu.SemaphoreType.DMA((3 * na,)), pltpu.SemaphoreType.DMA((3 * na,)), *hbm, *hbm,
                   jax.ShapeDtypeStruct((8, LANES), F32)),
        in_specs=[HBM_ONLY] * (2 * na),
        out_specs=(SEM_SPEC, SEM_SPEC, *([HBM_ONLY] * (2 * na)), pl.BlockSpec(memory_space=pltpu.VMEM)),
        input_output_aliases={i: 2 + i for i in range(2 * na)},
        compiler_params=SPLIT_COPY,
    )(*srcs, *lands)
    return res[-1], (res[0], res[1], list(res[2:2 + na]), list(res[2 + na:2 + 2 * na]))


def _between_chips_wait(pending, after, name):
    send_sems, recv_sems, parts, lands = pending
    na = len(parts)

    def body(*refs):
        a_refs = refs[:na]
        land_refs = refs[na:2 * na]
        send_sems, recv_sems = refs[2 * na], refs[2 * na + 1]
        me = _my_pos()
        for i in range(na):
            for j, peer in enumerate(_chip_peers(me)):
                cp = pltpu.make_async_remote_copy(
                    src_ref=a_refs[i].at[_chip(peer)], dst_ref=land_refs[i].at[_chip(peer)],
                    send_sem=send_sems.at[3 * i + j], recv_sem=recv_sems.at[3 * i + j],
                    device_id=peer, device_id_type=MESH)
                cp.wait_send()
                cp.wait_recv()

    hbm = [pltpu.HBM(a.shape, a.dtype) for a in parts]
    res = pl.pallas_call(
        body, name=name, out_shape=(*hbm, *hbm),
        in_specs=[HBM_ONLY] * (2 * na) + [SEM_SPEC, SEM_SPEC, pl.BlockSpec(memory_space=pl.ANY)],
        out_specs=[HBM_ONLY] * (2 * na),
        input_output_aliases={i: i for i in range(2 * na)},
        compiler_params=SPLIT_COPY,
    )(*parts, *lands, send_sems, recv_sems, after)
    return list(res[:na]), list(res[na:])


def _gather_start(shards, after, name):
    na = len(shards)

    def body(*refs):
        x_refs = refs[:na]
        land_refs = refs[na:2 * na]
        send_sems, recv_sems = refs[2 * na + 1], refs[2 * na + 2]
        token = refs[-1]
        me = _my_pos()
        for i in range(na):
            for k in range(1, NDEV):
                pltpu.make_async_remote_copy(
                    src_ref=x_refs[i], dst_ref=land_refs[i].at[_lin(me)],
                    send_sem=send_sems.at[7 * i + k - 1], recv_sem=recv_sems.at[7 * i + k - 1],
                    device_id=_flip(me, k), device_id_type=MESH).start()
        token[...] = jnp.zeros_like(token)

    src_t = [pltpu.HBM(a.shape, a.dtype) for a in shards]
    land_t = [pltpu.HBM((NDEV,) + a.shape, a.dtype) for a in shards]
    srcs = [pltpu.with_memory_space_constraint(a, pltpu.HBM) for a in shards]
    lands = [pltpu.with_memory_space_constraint(lax.empty((NDEV,) + a.shape, a.dtype), pltpu.HBM) for a in shards]
    res = pl.pallas_call(
        body, name=name,
        out_shape=(pltpu.SemaphoreType.DMA((7 * na,)), pltpu.SemaphoreType.DMA((7 * na,)), *src_t, *land_t,
                   jax.ShapeDtypeStruct((8, LANES), F32)),
        in_specs=[HBM_ONLY] * (2 * na) + [pl.BlockSpec(memory_space=pl.ANY)],
        out_specs=(SEM_SPEC, SEM_SPEC, *([HBM_ONLY] * (2 * na)), pl.BlockSpec(memory_space=pltpu.VMEM)),
        input_output_aliases={i: 2 + i for i in range(2 * na)},
        compiler_params=SPLIT_COPY,
    )(*srcs, *lands, after)
    return res[-1], (res[0], res[1], list(res[2:2 + na]), list(res[2 + na:2 + 2 * na]))


def _gather_wait(pending, after, name):
    send_sems, recv_sems, shards, lands = pending
    na = len(shards)

    def body(*refs):
        x_refs = refs[:na]
        land_refs = refs[na:2 * na]
        send_sems, recv_sems = refs[2 * na], refs[2 * na + 1]
        me = _my_pos()
        for i in range(na):
            for k in range(1, NDEV):
                peer = _flip(me, k)
                cp = pltpu.make_async_remote_copy(
                    src_ref=x_refs[i], dst_ref=land_refs[i].at[_lin(peer)],
                    send_sem=send_sems.at[7 * i + k - 1], recv_sem=recv_sems.at[7 * i + k - 1],
                    device_id=peer, device_id_type=MESH)
                cp.wait_send()
                cp.wait_recv()

    src_t = [pltpu.HBM(a.shape, a.dtype) for a in shards]
    land_t = [pltpu.HBM(a.shape, a.dtype) for a in lands]
    res = pl.pallas_call(
        body, name=name, out_shape=(*src_t, *land_t),
        in_specs=[HBM_ONLY] * (2 * na) + [SEM_SPEC, SEM_SPEC, pl.BlockSpec(memory_space=pl.ANY)],
        out_specs=[HBM_ONLY] * (2 * na),
        input_output_aliases={i: i for i in range(2 * na)},
        compiler_params=SPLIT_COPY,
    )(*shards, *lands, send_sems, recv_sems, after)
    return list(res[na:])


def _exchange_all(parts, name):
    na = len(parts)

    def body(*refs):
        a_refs = refs[:na]
        out_refs = refs[na:2 * na]
        send_sems, recv_sems = refs[2 * na:]
        me = _my_pos()
        sends = []
        for i in range(na):
            for k in range(1, NDEV):
                peer = _flip(me, k)
                cp = pltpu.make_async_remote_copy(
                    src_ref=a_refs[i].at[_lin(peer)], dst_ref=out_refs[i].at[_lin(me)],
                    send_sem=send_sems.at[7 * i + k - 1], recv_sem=recv_sems.at[7 * i + k - 1],
                    device_id=peer, device_id_type=MESH)
                cp.start()
                sends.append(cp)
        for i in range(na):
            for k in range(1, NDEV):
                peer = _flip(me, k)
                pltpu.make_async_remote_copy(
                    src_ref=a_refs[i].at[_lin(peer)], dst_ref=out_refs[i].at[_lin(peer)],
                    send_sem=send_sems.at[7 * i + k - 1], recv_sem=recv_sems.at[7 * i + k - 1],
                    device_id=peer, device_id_type=MESH).wait_recv()
        for cp in sends:
            cp.wait_send()

    return pl.pallas_call(
        body, name=name, out_shape=[jax.ShapeDtypeStruct(a.shape, a.dtype) for a in parts],
        in_specs=[HBM_SPEC] * na, out_specs=[HBM_SPEC] * na,
        scratch_shapes=[pltpu.SemaphoreType.DMA((7 * na,)), pltpu.SemaphoreType.DMA((7 * na,))],
    )(*parts)


def _scalar(v):
    return jnp.asarray(v, jnp.int32).reshape(1)


def _sum_pairs(parts, theirs, name):
    na = len(parts)

    def body(c_ref, *refs):
        for i in range(na):
            o_ref = refs[2 * na + i]
            o_ref[0] = (refs[i][0, 0].astype(F32) + refs[na + i][0].astype(F32)).astype(o_ref.dtype)

    def mine_spec(a):
        return pl.BlockSpec((1, 1) + a.shape[2:], lambda q, c_ref: (q, c_ref[0], 0, 0))

    def spec(a):
        return pl.BlockSpec((1,) + a.shape[1:], lambda q, c_ref: (q, 0, 0))

    return pl.pallas_call(
        body, name=name,
        grid_spec=pltpu.PrefetchScalarGridSpec(
            num_scalar_prefetch=1, grid=(NCHIP,),
            in_specs=[mine_spec(a) for a in parts] + [spec(a) for a in theirs],
            out_specs=[spec(a) for a in theirs]),
        out_shape=[jax.ShapeDtypeStruct(a.shape, a.dtype) for a in theirs],
        compiler_params=_cp(("arbitrary",), VMEM_BIG),
    )(_scalar(lax.axis_index("c")), *parts, *theirs)


def _others(q, mine, nblk=NCHIP):
    return jnp.where(q == mine, (q + 1) % nblk, q)


def _sum_chips_adamw(own, recv, wv, mv, vv, tr, name):
    _, r, w = recv.shape

    def body(q_ref, own_ref, r0, r1, r2, r3, w_ref, m_ref, v_ref, g_ref, d_ref, m2_ref, v2_ref):
        myq = q_ref[0]
        acc = None
        for q, r_ref in enumerate((r0, r1, r2, r3)):
            term = jnp.where(myq == q, own_ref[0], r_ref[0]).astype(F32)
            acc = term if acc is None else acc + term
        g_ref[...] = acc
        delta, m2, v2 = _adam_math(w_ref[...], acc, m_ref[...], v_ref[...])
        d_ref[...] = delta
        m2_ref[...] = m2
        v2_ref[...] = v2

    def recv_spec(q):
        return pl.BlockSpec((1, tr, w), lambda i, q_ref: (_others(q, q_ref[0]), i, 0))

    rows = lambda: pl.BlockSpec((tr, w), lambda i, q_ref: (i, 0))
    shp = jax.ShapeDtypeStruct((r, w), F32)
    return pl.pallas_call(
        body, name=name,
        grid_spec=pltpu.PrefetchScalarGridSpec(
            num_scalar_prefetch=1, grid=(r // tr,),
            in_specs=[pl.BlockSpec((1, tr, w), lambda i, q_ref: (q_ref[0], i, 0))]
            + [recv_spec(q) for q in range(NCHIP)] + [rows(), rows(), rows()],
            out_specs=[rows(), rows(), rows(), rows()]),
        out_shape=[shp, shp, shp, shp],
        compiler_params=_cp(("arbitrary",), VMEM_MID),
    )(_scalar(_chip(_my_pos())), *_hbm(own, recv, recv, recv, recv, wv, mv, vv))


def _sum_blocks_small(own, recv, mine, transpose, name):
    na = len(recv)
    nblk = recv[0].shape[0]

    def body(q_ref, *refs):
        me = q_ref[0]
        for i in range(na):
            acc = None
            for q in range(nblk):
                term = jnp.where(me == q, refs[i][0], refs[na * (1 + q) + i][0]).astype(F32)
                acc = term if acc is None else acc + term
            refs[na * (1 + nblk) + i][...] = acc.T if transpose[i] else acc

    def oshape(a, tr):
        r, w = a.shape[1:]
        return (w, r) if tr else (r, w)

    own_spec = lambda a: pl.BlockSpec((1,) + a.shape[1:], lambda s, q_ref: (q_ref[0], 0, 0))
    recv_spec = lambda a, q: pl.BlockSpec((1,) + a.shape[1:], lambda s, q_ref: (_others(q, q_ref[0], nblk), 0, 0))
    out_spec = lambda shp: pl.BlockSpec(shp, lambda s, q_ref: (0, 0))
    in_specs = [own_spec(a) for a in own]
    for q in range(nblk):
        in_specs += [recv_spec(a, q) for a in recv]
    return pl.pallas_call(
        body, name=name,
        grid_spec=pltpu.PrefetchScalarGridSpec(
            num_scalar_prefetch=1, grid=(1,), in_specs=in_specs,
            out_specs=[out_spec(oshape(a, tr)) for a, tr in zip(recv, transpose)]),
        out_shape=[jax.ShapeDtypeStruct(oshape(a, tr), F32) for a, tr in zip(recv, transpose)],
        compiler_params=_cp(("arbitrary",), VMEM_MID),
    )(_scalar(mine), *own, *(list(recv) * nblk))


def _rep_offsets():
    offs = []
    o = 0
    for r in REP_ROWS:
        offs.append(o)
        o += r
    return offs


def _pack_small_grads(g):
    offs = _rep_offsets()

    def body(dwa, dwx, dnm, dbin, dcb, dba, dbx, dlam, dq0, dq1, dq2, dk0, dk1, dk2, dnp, dbpg, o_ref):
        o_ref[pl.ds(REP_TOTAL_ROWS - 2, NDEV * REP_ROWS_DEV - REP_TOTAL_ROWS + 2), :] = jnp.zeros(
            (NDEV * REP_ROWS_DEV - REP_TOTAL_ROWS + 2, LANES), F32)
        for n in range(NRB):
            o_ref[pl.ds(offs[0] + n * RBW, RBW), :] = dwa[n]
            o_ref[pl.ds(offs[1] + n * RBW, RBW), :] = dwx[n]

        def put_vec(off, ref, rows):
            for k in range(rows):
                o_ref[pl.ds(off + k, 1), :] = ref[:, k * LANES:(k + 1) * LANES]

        put_vec(offs[2], dnm, REP_ROWS[2])
        put_vec(offs[3], dbin, REP_ROWS[3])
        put_vec(offs[4], dcb, REP_ROWS[4])
        put_vec(offs[5], dba, REP_ROWS[5])
        put_vec(offs[6], dbx, REP_ROWS[6])
        put_vec(offs[7], dlam, REP_ROWS[7])
        for k, ref in enumerate((dq0, dq1, dq2)):
            o_ref[pl.ds(offs[8] + k, 1), :] = ref[...]
        for k, ref in enumerate((dk0, dk1, dk2)):
            o_ref[pl.ds(offs[9] + k, 1), :] = ref[...]
        put_vec(offs[10], dnp, REP_ROWS[10])
        put_vec(offs[11], dbpg, REP_ROWS[11])

    args = [g["w_rg_a"], g["w_rg_x"], g["norm_mix"], g["b_in"], g["conv_b"], g["b_rg_a"], g["b_rg_x"],
            g["lru_lambda"], *g["q_norm"], *g["k_norm"], g["norm_ple"], g["b_ple_gate"]]
    full = lambda shp: pl.BlockSpec(shp, lambda: (0,) * len(shp))
    return pl.pallas_call(
        body, name="pack_small_grads",
        in_specs=[full(a.shape) for a in args],
        out_specs=full((NDEV * REP_ROWS_DEV, LANES)),
        out_shape=jax.ShapeDtypeStruct((NDEV * REP_ROWS_DEV, LANES), F32),
    )(*_hbm(*args))


def _adam_math(wv, gv, mv, vv):
    c1 = 1.0 - B1 ** STEP
    c2 = 1.0 - B2 ** STEP
    m2 = B1 * mv + (1.0 - B1) * gv
    v2 = B2 * vv + (1.0 - B2) * (gv * gv)
    delta = (-LR) * ((m2 / c1) / (jnp.sqrt(v2 / c2) + AEPS) + WD * wv)
    return delta, m2, v2


def _adamw_small(rep_flat, w, m, v):
    offs = _rep_offsets()
    n = len(REP_NAMES)

    def body(*refs):
        g_ref = refs[0]
        w_refs = refs[1:1 + n]
        m_refs = refs[1 + n:1 + 2 * n]
        v_refs = refs[1 + 2 * n:1 + 3 * n]
        outs = refs[1 + 3 * n:]
        go, do, mo, vo = outs[:n], outs[n:2 * n], outs[2 * n:3 * n], outs[3 * n:]

        def emit(i, idx, gv):
            go[i][idx] = gv
            delta, m2, v2 = _adam_math(w_refs[i][idx], gv, m_refs[i][idx], v_refs[i][idx])
            do[i][idx] = delta
            mo[i][idx] = m2
            vo[i][idx] = v2

        for i in range(n):
            if i < 2:
                for b in range(NRB):
                    emit(i, b, g_ref[pl.ds(offs[i] + b * RBW, RBW), :])
            elif REP_NAMES[i] in ("q_norm", "k_norm"):
                emit(i, slice(None), g_ref[pl.ds(offs[i], NG), :])
            else:
                gv = jnp.concatenate([g_ref[pl.ds(offs[i] + k, 1), :] for k in range(REP_ROWS[i])], axis=1)
                emit(i, slice(None), gv)

    full = lambda shp: pl.BlockSpec(shp, lambda: (0,) * len(shp))
    pspecs = [full(a.shape) for a in w]
    pshapes = [jax.ShapeDtypeStruct(a.shape, F32) for a in w]
    res = pl.pallas_call(
        body, name="adamw_small",
        in_specs=[full(rep_flat.shape)] + pspecs * 3,
        out_specs=pspecs * 4, out_shape=pshapes * 4,
        compiler_params=_cp(None, VMEM_MID),
    )(*_hbm(rep_flat, *w, *m, *v))
    return res[:n], res[n:2 * n], res[2 * n:3 * n], res[3 * n:]


def _adamw_many(w, g, m, v):
    n = len(w)

    def body(*refs):
        for i in range(n):
            delta, m2, v2 = _adam_math(refs[i][...], refs[n + i][...], refs[2 * n + i][...], refs[3 * n + i][...])
            refs[4 * n + i][...] = delta
            refs[5 * n + i][...] = m2
            refs[6 * n + i][...] = v2

    full = lambda shp: pl.BlockSpec(shp, lambda: (0,) * len(shp))
    specs = [full(a.shape) for a in w]
    shapes = [jax.ShapeDtypeStruct(a.shape, F32) for a in w]
    res = pl.pallas_call(
        body, name="adamw_shards",
        in_specs=specs * 4, out_specs=specs * 3, out_shape=shapes * 3,
        compiler_params=_cp(None, VMEM_MID),
    )(*_hbm(*w, *g, *m, *v))
    return res[:n], res[n:2 * n], res[2 * n:]


def kernel(x, p, norm_mix, w_in, b_in, conv_w, conv_b, w_rg_a, b_rg_a, w_rg_x, b_rg_x, lru_lambda, q_norm, k_norm, w_o_rnn, w_o_att, w_out, norm_ple, w_ple_gate, b_ple_gate, w_ple, loss_target, m_norm_mix, m_w_in, m_b_in, m_conv_w, m_conv_b, m_w_rg_a, m_b_rg_a, m_w_rg_x, m_b_rg_x, m_lru_lambda, m_q_norm, m_k_norm, m_w_o_rnn, m_w_o_att, m_w_out, m_norm_ple, m_w_ple_gate, m_b_ple_gate, m_w_ple, v_norm_mix, v_w_in, v_b_in, v_conv_w, v_conv_b, v_w_rg_a, v_b_rg_a, v_w_rg_x, v_b_rg_x, v_lru_lambda, v_q_norm, v_k_norm, v_w_o_rnn, v_w_o_att, v_w_out, v_norm_ple, v_w_ple_gate, v_b_ple_gate, v_w_ple):
    w = dict(norm_mix=norm_mix, w_in=w_in, b_in=b_in, conv_w=conv_w, conv_b=conv_b, w_rg_a=w_rg_a, b_rg_a=b_rg_a,
             w_rg_x=w_rg_x, b_rg_x=b_rg_x, lru_lambda=lru_lambda, q_norm=q_norm, k_norm=k_norm, w_o_rnn=w_o_rnn,
             w_o_att=w_o_att, w_out=w_out, norm_ple=norm_ple, w_ple_gate=w_ple_gate, b_ple_gate=b_ple_gate,
             w_ple=w_ple)
    m = dict(norm_mix=m_norm_mix, w_in=m_w_in, b_in=m_b_in, conv_w=m_conv_w, conv_b=m_conv_b, w_rg_a=m_w_rg_a,
             b_rg_a=m_b_rg_a, w_rg_x=m_w_rg_x, b_rg_x=m_b_rg_x, lru_lambda=m_lru_lambda, q_norm=m_q_norm,
             k_norm=m_k_norm, w_o_rnn=m_w_o_rnn, w_o_att=m_w_o_att, w_out=m_w_out, norm_ple=m_norm_ple,
             w_ple_gate=m_w_ple_gate, b_ple_gate=m_b_ple_gate, w_ple=m_w_ple)
    v = dict(norm_mix=v_norm_mix, w_in=v_w_in, b_in=v_b_in, conv_w=v_conv_w, conv_b=v_conv_b, w_rg_a=v_w_rg_a,
             b_rg_a=v_b_rg_a, w_rg_x=v_w_rg_x, b_rg_x=v_b_rg_x, lru_lambda=v_lru_lambda, q_norm=v_q_norm,
             k_norm=v_k_norm, w_o_rnn=v_w_o_rnn, w_o_att=v_w_o_att, w_out=v_w_out, norm_ple=v_norm_ple,
             w_ple_gate=v_w_ple_gate, b_ple_gate=v_b_ple_gate, w_ple=v_w_ple)
    names = list(w.keys())

    shards = [w_in[0].T.astype(BF16), w_o_rnn[0].astype(BF16), w_o_att[0].T.astype(BF16), w_out[0].astype(BF16),
              w_ple_gate[0].astype(BF16), w_ple[0].T.astype(BF16), conv_w[0]]
    w_in_g = _all_gather_two_level(shards[:1], "gather_w_in")[0]
    w_in_t = w_in_g.reshape(NIN, D)
    entry_token, gather_pending = _gather_start(shards[1:], w_in_g, "gather_others_start")
    me = _lin(_my_pos())

    def other_weights(after):
        landed = _gather_wait(gather_pending, after, "gather_others_wait")
        full = [lax.dynamic_update_slice(a, s[None], (me, 0, 0)) for a, s in zip(landed, shards[1:])]
        conv_f = full[5].transpose(1, 0, 2).reshape(CONVW, DR)
        return (*[a.reshape((NDEV * a.shape[1], a.shape[2])) for a in full[:5]], conv_f)

    def start_reduce(arrs, tag):
        parts = [a.reshape((NCHIP, 2, a.shape[0] // NDEV, a.shape[1])) for a in arrs]
        theirs = _exchange_within_chip(parts, "reduce_within_chip_" + tag)
        return _between_chips_start(_sum_pairs(parts, theirs, "sum_pairs_" + tag), "reduce_between_chips_start_" + tag)

    loss_part, grad_x, pending_out, pending_in, small = _local_step(
        x.reshape(T, D), p.reshape(T, PLE), loss_target.reshape(T, D),
        w_in_t, other_weights,
        norm_mix, b_in, conv_b, w_rg_a[0], b_rg_a, w_rg_x[0], b_rg_x, lru_lambda, q_norm[0], k_norm[0],
        norm_ple, b_ple_gate, start_reduce, entry_token)
    loss = lax.psum(loss_part, ("x", "y", "c"))

    rep_parts = _pack_small_grads(small).reshape(NDEV, REP_ROWS_DEV, LANES)
    conv_parts = small["conv_w"].reshape(CONVW, NDEV, DR // NDEV).transpose(1, 0, 2)
    small_parts = [rep_parts, conv_parts]
    g_rep, g_conv = _sum_blocks_small(small_parts, _exchange_all(small_parts, "reduce_small"), me, (False, False),
                                      "sum_small")
    rep_all = _all_gather_direct(g_rep, "gather_small").reshape(NDEV * REP_ROWS_DEV, LANES)

    myq = _chip(_my_pos())
    own_out, recv_out = _between_chips_wait(pending_out, rep_all, "reduce_between_chips_wait_out")
    own_in, recv_in = _between_chips_wait(pending_in, rep_all, "reduce_between_chips_wait_in")
    w_in_res = _sum_chips_adamw(own_in[0], recv_in[0], w_in[0].T, m_w_in[0].T, v_w_in[0].T, 304, "adamw_w_in")
    g_o_rnn, g_o_att, g_out, g_pg, g_ple = _sum_blocks_small(
        own_out, recv_out, myq, (False, True, False, False, True), "sum_chips_out")

    grad, delta, new_m, new_v = {}, {}, {}, {}
    rep_shape = lambda a: a if a.ndim == 2 else a.reshape(a.shape[1:])
    res = _adamw_small(rep_all, [rep_shape(w[n]) for n in REP_NAMES], [rep_shape(m[n]) for n in REP_NAMES],
                       [rep_shape(v[n]) for n in REP_NAMES])
    for dst, vals in zip((grad, delta, new_m, new_v), res):
        for n, a in zip(REP_NAMES, vals):
            dst[n] = a.reshape(w[n].shape)
    grad["w_in"], delta["w_in"], new_m["w_in"], new_v["w_in"] = [a.T[None] for a in w_in_res]
    rest = ("w_o_rnn", "w_o_att", "w_out", "w_ple_gate", "w_ple", "conv_w")
    g_rest = [g_o_rnn, g_o_att, g_out, g_pg, g_ple, g_conv]
    res = _adamw_many([w[n][0] for n in rest], g_rest, [m[n][0] for n in rest], [v[n][0] for n in rest])
    for n, a in zip(rest, g_rest):
        grad[n] = a[None]
    for dst, vals in zip((delta, new_m, new_v), res):
        for n, a in zip(rest, vals):
            dst[n] = a[None]

    return (loss, grad_x.reshape(BL, S, D), *[grad[n] for n in names], *[delta[n] for n in names],
            *[new_m[n] for n in names], *[new_v[n] for n in names])
```

```python
import jax
import jax.numpy as jnp
from jax import lax
from jax.experimental import pallas as pl
from jax.experimental.pallas import tpu as pltpu

F32 = jnp.float32
BF16 = jnp.bfloat16

D = 1024
S = 2048
BL = 2
T = BL * S
NDEV = 8
NCHIP = 4
PLE = 256
DR = 1280
NRB = 10
RBW = 128
CONVW = 4
LRU_C = 8.0
HD = 128
NH = 4
PATTERNS = ((128, 1), (512, 4), (2048, 16))
NG = 3
ATT = NH * HD
GW = NG * ATT
NIN = 2 * DR + 3 * GW + ATT + 2 * D
OFF_ZR = DR
OFF_Q = 2 * DR
OFF_ZA = OFF_Q + 3 * GW
OFF_G = OFF_ZA + ATT
ROPE_THETA = 10000.0
EPS = 1e-6
SCALE = HD ** -0.5
NEG = -1e30
QB = 128
LANES = 128
CT = 512
NCT = NIN // CT
A_W = 2 * DR
C_W = ATT + 2 * D

LR, B1, B2, AEPS, WD, STEP = 0.001, 0.9, 0.999, 1e-08, 0.01, 10

NSHARD_IN = NIN // NDEV
REP_NAMES = ("w_rg_a", "w_rg_x", "norm_mix", "b_in", "conv_b", "b_rg_a", "b_rg_x", "lru_lambda", "q_norm",
             "k_norm", "norm_ple", "b_ple_gate")
REP_ROWS = (NRB * RBW, NRB * RBW, D // LANES, NIN // LANES, DR // LANES, DR // LANES, DR // LANES, DR // LANES,
            NG, NG, D // LANES, D // LANES)
REP_TOTAL_ROWS = sum(REP_ROWS)
REP_ROWS_DEV = 344
BIG_NAMES = ("w_in", "w_o_rnn", "w_o_att", "w_out", "w_ple_gate", "w_ple")

VMEM_BIG = 56 * 1024 * 1024
VMEM_MID = 40 * 1024 * 1024


def _cp(sem=None, vmem=None):
    return pltpu.CompilerParams(dimension_semantics=sem, vmem_limit_bytes=vmem)


def _hbm(*arrays):
    return [pltpu.with_memory_space_constraint(a, pltpu.HBM) for a in arrays]


def _dot(a, b):
    return jnp.dot(a, b, preferred_element_type=F32)


def _dot_nt(a, b):
    return lax.dot_general(a, b, (((1,), (1,)), ((), ())), preferred_element_type=F32)


def _dot_tn(a, b):
    return lax.dot_general(a, b, (((0,), (0,)), ((), ())), preferred_element_type=F32)


def _sigmoid(x):
    return jax.nn.sigmoid(x)


def _perm(j):
    jq = j - OFF_Q // CT
    inside = (j >= OFF_Q // CT) & (j < OFF_ZA // CT)
    return jnp.where(inside, OFF_Q // CT + (jq % 3) * 3 + jq // 3, j)


PIECES = ((0, A_W // CT), (OFF_Q // CT, GW // CT), (OFF_Q // CT + 3, GW // CT), (OFF_Q // CT + 6, GW // CT),
          (OFF_ZA // CT, C_W // CT))


def _rmsnorm_fwd(x, gain, token, tm=512):
    def body(x_ref, g_ref, _token, o_ref):
        xv = x_ref[...]
        var = jnp.mean(xv * xv, axis=-1, keepdims=True)
        o_ref[...] = (xv * lax.rsqrt(var + EPS) * g_ref[...]).astype(BF16)

    return pl.pallas_call(
        body, grid=(T // tm,), name="rmsnorm_fwd",
        in_specs=[pl.BlockSpec((tm, D), lambda i: (i, 0)), pl.BlockSpec((1, D), lambda i: (0, 0)),
                  pl.BlockSpec((8, LANES), lambda i: (0, 0))],
        out_specs=pl.BlockSpec((tm, D), lambda i: (i, 0)),
        out_shape=jax.ShapeDtypeStruct((T, D), BF16),
        compiler_params=_cp(("parallel",)),
    )(*_hbm(x, gain, token))


CHIP_COLS = NIN // NCHIP


def _in_proj_chips(hn, w_rows, bias, chips, proj, name, tm=1024):
    n = w_rows.shape[0]

    def body(chips_ref, a_ref, w_ref, b_ref, *rest):
        o_ref = rest[-1]
        o_ref[...] = (_dot_nt(a_ref[...], w_ref[0]) + b_ref[...]).astype(BF16)

    in_specs = [pl.BlockSpec((tm, D), lambda s, i, ch: (i, 0)),
                pl.BlockSpec((1, CHIP_COLS, D), lambda s, i, ch: (s, 0, 0)),
                pl.BlockSpec((1, CHIP_COLS), lambda s, i, ch: (0, ch[s]))]
    args = [hn, w_rows, bias]
    aliases = {}
    if proj is not None:
        in_specs.append(pl.BlockSpec(memory_space=pl.ANY))
        args.append(proj)
        aliases = {4: 0}
    return pl.pallas_call(
        body, name=name,
        grid_spec=pltpu.PrefetchScalarGridSpec(
            num_scalar_prefetch=1, grid=(n, T // tm), in_specs=in_specs,
            out_specs=pl.BlockSpec((tm, CHIP_COLS), lambda s, i, ch: (i, ch[s]))),
        out_shape=jax.ShapeDtypeStruct((T, NIN), BF16),
        input_output_aliases=aliases,
        compiler_params=_cp(("arbitrary", "arbitrary"), VMEM_BIG),
    )(chips, *_hbm(*args))


def _dhn(pieces, w_bufs, chips, token, tm=512):
    nb = len(w_bufs)

    def body(chips_ref, a_ref, q_ref, k_ref, v_ref, c_ref, *rest):
        w_hbm = rest[:nb]
        o_ref, w = rest[nb + 1], rest[nb + 2]

        @pl.when(pl.program_id(0) == 0)
        def _():
            s = 0
            for buf in w_hbm:
                for r in range(buf.shape[0]):
                    row = pl.multiple_of(chips_ref[s] * CHIP_COLS, 128)
                    pltpu.sync_copy(buf.at[r], w.at[pl.ds(row, CHIP_COLS), :])
                    s += 1

        acc = _dot(a_ref[...], w[pl.ds(0, A_W), :])
        for kind, x_ref in enumerate((q_ref, k_ref, v_ref)):
            for g in range(NG):
                row = OFF_Q + (3 * g + kind) * CT
                acc = acc + _dot(x_ref[:, g * CT:(g + 1) * CT], w[pl.ds(row, CT), :])
        o_ref[...] = acc + _dot(c_ref[...], w[pl.ds(OFF_ZA, C_W), :])

    tok = lambda wd: pl.BlockSpec((tm, wd), lambda i, ch: (i, 0))
    return pl.pallas_call(
        body, name="dhn",
        grid_spec=pltpu.PrefetchScalarGridSpec(
            num_scalar_prefetch=1, grid=(T // tm,),
            in_specs=[tok(A_W), tok(GW), tok(GW), tok(GW), tok(C_W)] + [pl.BlockSpec(memory_space=pl.ANY)] * nb
            + [pl.BlockSpec((8, LANES), lambda i, ch: (0, 0))],
            out_specs=tok(D),
            scratch_shapes=[pltpu.VMEM((NIN, D), BF16)]),
        out_shape=jax.ShapeDtypeStruct((T, D), F32),
        compiler_params=_cp(("arbitrary",), VMEM_BIG),
    )(chips, *_hbm(*pieces, *w_bufs, token))


def _dw_in(pieces, hn):
    names = ("a", "q", "k", "v", "c")
    dw = db = None
    for piece, (lo, n), tag in zip(pieces, PIECES, names):
        def body(x_ref, h_ref, *rest):
            o_ref, s_ref = rest[-2:]
            xv = x_ref[...]
            o_ref[...] = _dot_tn(xv, h_ref[...]).astype(BF16)
            s_ref[...] = jnp.sum(xv.astype(F32), axis=0, keepdims=True)

        in_specs = [pl.BlockSpec((T, CT), lambda j: (0, j)), pl.BlockSpec((T, D), lambda j: (0, 0))]
        args = [piece, hn]
        aliases = {}
        if dw is not None:
            in_specs += [pl.BlockSpec(memory_space=pl.ANY)] * 2
            args += [dw, db]
            aliases = {2: 0, 3: 1}
        dw, db = pl.pallas_call(
            body, grid=(n,), name="dw_in_" + tag,
            in_specs=in_specs,
            out_specs=[pl.BlockSpec((CT, D), lambda j, lo=lo: (_perm(lo + j), 0)),
                       pl.BlockSpec((1, CT), lambda j, lo=lo: (0, _perm(lo + j)))],
            out_shape=[jax.ShapeDtypeStruct((NIN, D), BF16), jax.ShapeDtypeStruct((1, NIN), F32)],
            input_output_aliases=aliases,
            compiler_params=_cp(("parallel",), VMEM_MID),
        )(*_hbm(*args))
    return dw, db


def _mm_tn(a, b, ta, tt, name):
    m = a.shape[1]
    n = b.shape[1]
    nt = T // tt

    def body(a_ref, b_ref, o_ref, acc):
        t = pl.program_id(1)
        p = _dot_tn(a_ref[...].astype(BF16), b_ref[...].astype(BF16))

        @pl.when(t == 0)
        def _():
            acc[...] = p

        @pl.when(t > 0)
        def _():
            acc[...] += p

        @pl.when(t == nt - 1)
        def _():
            o_ref[...] = acc[...].astype(BF16)

    return pl.pallas_call(
        body, grid=(m // ta, nt), name=name,
        in_specs=[pl.BlockSpec((tt, ta), lambda j, t: (t, j)), pl.BlockSpec((tt, n), lambda j, t: (t, 0))],
        out_specs=pl.BlockSpec((ta, n), lambda j, t: (j, 0)),
        out_shape=jax.ShapeDtypeStruct((m, n), BF16),
        scratch_shapes=[pltpu.VMEM((ta, n), F32)],
        compiler_params=_cp(("parallel", "arbitrary"), VMEM_MID),
    )(*_hbm(a, b))


def _row_iota():
    return lax.broadcasted_iota(jnp.int32, (S, RBW), 0)


def _shift_down(v, d, row, fill):
    return jnp.where(row >= d, pltpu.roll(v, d, 0), fill)


def _shift_up(v, d, row, fill):
    return jnp.where(row < S - d, pltpu.roll(v, S - d, 0), fill)


SUBLANES = 8


def _scan_down(a, u, row):
    d = 1
    while d < S:
        last = 2 * d >= S
        if d < SUBLANES:
            u = a * _shift_down(u, d, row, 0.0) + u
            if not last:
                a = a * _shift_down(a, d, row, 1.0)
        else:
            u = jnp.concatenate([u[:d], a[d:] * u[:S - d] + u[d:]], axis=0)
            if not last:
                a = jnp.concatenate([a[:d], a[d:] * a[:S - d]], axis=0)
        d *= 2
    return u


def _scan_up(b, g, row):
    d = 1
    while d < S:
        last = 2 * d >= S
        if d < SUBLANES:
            g = g + b * _shift_up(g, d, row, 0.0)
            if not last:
                b = b * _shift_up(b, d, row, 0.0)
        else:
            g = jnp.concatenate([g[:S - d] + b[:S - d] * g[d:], g[S - d:]], axis=0)
            if not last:
                b = jnp.concatenate([b[:S - d] * b[d:], b[S - d:]], axis=0)
        d *= 2
    return g


def _softplus(x):
    return jnp.maximum(x, 0.0) + jnp.log1p(jnp.exp(-jnp.abs(x)))


def _rnn_gates(x, cw, cb, wa, ba, wx, bx, lam, row):
    xc = cb + cw[3:4, :] * x
    for j in (1, 2, 3):
        xc = xc + cw[3 - j:4 - j, :] * _shift_down(x, j, row, 0.0)
    xcb = xc.astype(BF16)
    r = _sigmoid(_dot(xcb, wa) + ba)
    i = _sigmoid(_dot(xcb, wx) + bx)
    sp = _softplus(-lam)
    log_a = (-LRU_C) * r * sp
    a = jnp.exp(log_a)
    mult = jnp.where(row == 0, 1.0, jnp.sqrt(jnp.tanh(-log_a) * (1.0 + a * a)))
    return xc, xcb, r, i, sp, a, mult


def _rnn_fwd(proj3, conv_w, conv_b, wa, ba, wx, bx, lam):
    def body(x_ref, cw_ref, cb_ref, wa_ref, ba_ref, wx_ref, bx_ref, lam_ref, h_ref):
        row = _row_iota()
        x = x_ref[0].astype(F32)
        xc, _, _, i, _, a, mult = _rnn_gates(x, cw_ref[...], cb_ref[...], wa_ref[0], ba_ref[...],
                                             wx_ref[0], bx_ref[...], lam_ref[...], row)
        h_ref[0] = _scan_down(a, mult * (i * xc), row)

    vec = lambda: pl.BlockSpec((1, RBW), lambda b, n: (0, n))
    mat = lambda: pl.BlockSpec((1, RBW, RBW), lambda b, n: (n, 0, 0))
    return pl.pallas_call(
        body, grid=(BL, NRB), name="rnn_fwd",
        in_specs=[pl.BlockSpec((1, S, RBW), lambda b, n: (b, 0, n)),
                  pl.BlockSpec((CONVW, RBW), lambda b, n: (0, n)),
                  vec(), mat(), vec(), mat(), vec(), vec()],
        out_specs=pl.BlockSpec((1, S, RBW), lambda b, n: (b, 0, n)),
        out_shape=jax.ShapeDtypeStruct((BL, S, DR), F32),
        compiler_params=_cp(("parallel", "parallel"), VMEM_MID),
    )(*_hbm(proj3, conv_w, conv_b, wa, ba, wx, bx, lam))


def _rnn_bwd(proj3, h3, dh3, slab_a3, conv_w, conv_b, wa, ba, wx, bx, lam, token):
    def body(x_ref, h_ref, dh_ref, cw_ref, cb_ref, wa_ref, ba_ref, wx_ref, bx_ref, lam_ref, _alias, _token,
             dx_ref, dcw_ref, dcb_ref, dwa_ref, dba_ref, dwx_ref, dbx_ref, dlam_ref):
        row = _row_iota()
        x = x_ref[0].astype(F32)
        cw = cw_ref[...]
        wa_v = wa_ref[0]
        wx_v = wx_ref[0]
        lam_v = lam_ref[...]
        xc, xcb, r, i, sp, a, mult = _rnn_gates(x, cw, cb_ref[...], wa_v, ba_ref[...], wx_v, bx_ref[...], lam_v, row)
        h = h_ref[0]
        g = _scan_up(_shift_up(a, 1, row, 0.0), dh_ref[0], row)
        da = g * _shift_down(h, 1, row, 0.0)
        dmult = jnp.where(row == 0, 0.0, g * (i * xc))
        gm = g * mult
        di = gm * xc
        dxc = gm * i
        dlog_a = da * a - dmult * (a * a) / mult
        dr = dlog_a * ((-LRU_C) * sp)
        dsp = jnp.sum(dlog_a * ((-LRU_C) * r), axis=0, keepdims=True)
        dlam = dsp * (-_sigmoid(-lam_v))
        dpa = dr * r * (1.0 - r)
        dpx = di * i * (1.0 - i)
        dpab = dpa.astype(BF16)
        dpxb = dpx.astype(BF16)
        dwa = _dot_tn(xcb, dpab)
        dwx = _dot_tn(xcb, dpxb)
        dba = jnp.sum(dpa, axis=0, keepdims=True)
        dbx = jnp.sum(dpx, axis=0, keepdims=True)
        dxc = dxc + _dot_nt(dpab, wa_v) + _dot_nt(dpxb, wx_v)
        dcb = jnp.sum(dxc, axis=0, keepdims=True)
        dx = cw[3:4, :] * dxc
        dcw_rows = [None] * CONVW
        dcw_rows[3] = jnp.sum(dxc * x, axis=0, keepdims=True)
        for j in (1, 2, 3):
            dx = dx + cw[3 - j:4 - j, :] * _shift_up(dxc, j, row, 0.0)
            dcw_rows[3 - j] = jnp.sum(dxc * _shift_down(x, j, row, 0.0), axis=0, keepdims=True)
        dx_ref[0] = dx.astype(BF16)
        dcw = jnp.concatenate(dcw_rows, axis=0)
        first = pl.program_id(1) == 0

        @pl.when(first)
        def _():
            dcw_ref[...] = dcw
            dcb_ref[...] = dcb
            dwa_ref[0] = dwa
            dba_ref[...] = dba
            dwx_ref[0] = dwx
            dbx_ref[...] = dbx
            dlam_ref[...] = dlam

        @pl.when(jnp.logical_not(first))
        def _():
            dcw_ref[...] += dcw
            dcb_ref[...] += dcb
            dwa_ref[0] += dwa
            dba_ref[...] += dba
            dwx_ref[0] += dwx
            dbx_ref[...] += dbx
            dlam_ref[...] += dlam

    slab = lambda: pl.BlockSpec((1, S, RBW), lambda n, b: (b, 0, n))
    vec = lambda: pl.BlockSpec((1, RBW), lambda n, b: (0, n))
    mat = lambda: pl.BlockSpec((1, RBW, RBW), lambda n, b: (n, 0, 0))
    taps = lambda: pl.BlockSpec((CONVW, RBW), lambda n, b: (0, n))
    vshape = jax.ShapeDtypeStruct((1, DR), F32)
    mshape = jax.ShapeDtypeStruct((NRB, RBW, RBW), F32)
    return pl.pallas_call(
        body, grid=(NRB, BL), name="rnn_bwd",
        in_specs=[slab(), slab(), slab(), taps(), vec(), mat(), vec(), mat(), vec(), vec(),
                  pl.BlockSpec(memory_space=pl.ANY), pl.BlockSpec((8, LANES), lambda n, b: (0, 0))],
        out_specs=[slab(), taps(), vec(), mat(), vec(), mat(), vec(), vec()],
        out_shape=[jax.ShapeDtypeStruct((BL, S, A_W), BF16), jax.ShapeDtypeStruct((CONVW, DR), F32),
                   vshape, mshape, vshape, mshape, vshape, vshape],
        input_output_aliases={10: 0},
        compiler_params=_cp(("parallel", "arbitrary"), 48 * 1024 * 1024),
    )(*_hbm(proj3, h3, dh3, conv_w, conv_b, wa, ba, wx, bx, lam, slab_a3, token))


NQB = S // QB


def _rms_head(t, gain):
    rstd = lax.rsqrt(jnp.mean(t * t, axis=-1, keepdims=True) + EPS)
    return t * rstd * gain


def _rope(t, cs, sn):
    return t * cs + pltpu.roll(t, HD // 2, 1) * sn


def _rope_t(dy, cs, sn):
    return dy * cs - pltpu.roll(dy, HD // 2, 1) * sn


def _bdot_nt(a, b):
    return lax.dot_general(a, b, (((2,), (2,)), ((0,), (0,))), preferred_element_type=F32)


def _bdot(a, b):
    return lax.dot_general(a, b, (((2,), (1,)), ((0,), (0,))), preferred_element_type=F32)


def _bdot_tn(a, b):
    return lax.dot_general(a, b, (((1,), (1,)), ((0,), (0,))), preferred_element_type=F32)


STRIDE_MAX = 4


def _permute(buf, x, dil, dst, off=0):
    ln = S // dil
    if dil == 1:
        dst[pl.ds(off, S), :] = x.astype(dst.dtype)
        return
    buf[0] = x
    if dil <= STRIDE_MAX:
        for c in range(dil):
            dst[pl.ds(off + c * ln, ln), :] = buf.at[0][pl.ds(c, ln, stride=dil), :].astype(dst.dtype)
        return
    f, r = STRIDE_MAX, dil // STRIDE_MAX
    part = S // f
    for c1 in range(f):
        buf.at[1][pl.ds(c1 * part, part), :] = buf.at[0][pl.ds(c1, part, stride=f), :]
    for c1 in range(f):
        for c2 in range(r):
            dst[pl.ds(off + (c1 + f * c2) * ln, ln), :] = (
                buf.at[1][pl.ds(c1 * part + c2, ln, stride=r), :].astype(dst.dtype))


def _unpermute(buf, xp, dil, dst):
    ln = S // dil
    if dil == 1:
        dst[...] = xp
        return
    if dil <= STRIDE_MAX:
        for c in range(dil):
            dst[pl.ds(c, ln, stride=dil), :] = xp[c * ln:(c + 1) * ln]
        return
    f, r = STRIDE_MAX, dil // STRIDE_MAX
    part = S // f
    for c1 in range(f):
        for c2 in range(r):
            c = c1 + f * c2
            buf.at[1][pl.ds(c1 * part + c2, ln, stride=r), :] = xp[c * ln:(c + 1) * ln]
    for c1 in range(f):
        dst[pl.ds(c1, part, stride=f), :] = buf[1, pl.ds(c1 * part, part), :]


def _blocks3(ref, off=0):
    return ref[pl.ds(off, S), :].reshape(NQB, QB, HD)


def _att_prep(q_ref, k_ref, v_ref, cos_ref, sin_ref, qn, kn, dil, nat, qs, ksp, vsp):
    cs = cos_ref[...]
    sn = sin_ref[...]
    zero = jnp.zeros((QB, HD), BF16)
    ksp[pl.ds(0, QB), :] = zero
    vsp[pl.ds(0, QB), :] = zero
    _permute(nat, _rope(_rms_head(q_ref[0].astype(F32), qn), cs, sn), dil, qs)
    _permute(nat, _rope(_rms_head(k_ref[0].astype(F32), kn), cs, sn), dil, ksp, QB)
    _permute(nat, v_ref[0].astype(F32), dil, vsp, QB)


def _att_scores(qs, ksp, dil):
    nb = S // dil // QB
    q3 = _blocks3(qs)
    shape = (NQB, QB, QB)
    qi = lax.broadcasted_iota(jnp.int32, shape, 1)
    kj = lax.broadcasted_iota(jnp.int32, shape, 2)
    s_c = jnp.where(qi >= kj, _bdot_nt(q3, _blocks3(ksp, QB)) * SCALE, NEG)
    if nb == 1:
        return q3, s_c, None
    jj = lax.broadcasted_iota(jnp.int32, shape, 0)
    ok = (kj >= qi) & ((jj & (nb - 1)) != 0)
    s_p = jnp.where(ok, _bdot_nt(q3, _blocks3(ksp)) * SCALE, NEG)
    return q3, s_c, s_p


def _qkv_spec(kind, g):
    base = OFF_Q // HD + (3 * g + kind) * NH
    return pl.BlockSpec((1, S, HD), lambda b, h: (b, 0, base + h))


def _attn_fwd(proj3, cos_t, sin_t, q_norm, k_norm):
    def body(*refs):
        qkv_refs = refs[:9]
        cos_ref, sin_ref, qn_ref, kn_ref, att_ref, lse_ref, w_ref, nat, qs, ksp, vsp, og = refs[9:]
        for g, (window, dil) in enumerate(PATTERNS):
            q_ref, k_ref, v_ref = qkv_refs[3 * g:3 * g + 3]
            _att_prep(q_ref, k_ref, v_ref, cos_ref, sin_ref, qn_ref[g:g + 1, :], kn_ref[g:g + 1, :], dil,
                      nat, qs, ksp, vsp)
            _, s_c, s_p = _att_scores(qs, ksp, dil)
            m = jnp.max(s_c, axis=-1, keepdims=True)
            if s_p is not None:
                m = jnp.maximum(m, jnp.max(s_p, axis=-1, keepdims=True))
            e_c = jnp.exp(s_c - m)
            den = jnp.sum(e_c, axis=-1, keepdims=True)
            o = _bdot(e_c.astype(BF16), _blocks3(vsp, QB))
            if s_p is not None:
                e_p = jnp.exp(s_p - m)
                den = den + jnp.sum(e_p, axis=-1, keepdims=True)
                o = o + _bdot(e_p.astype(BF16), _blocks3(vsp))
            _unpermute(nat, (o / den).reshape(S, HD), dil, og.at[g])
            _unpermute(nat, jnp.broadcast_to(m + jnp.log(den), (NQB, QB, HD)).reshape(S, HD), dil,
                       lse_ref.at[g, 0])
        l0 = lse_ref[0, 0]
        l1 = lse_ref[1, 0]
        l2 = lse_ref[2, 0]
        mx = jnp.maximum(jnp.maximum(l0, l1), l2)
        e0 = jnp.exp(l0 - mx)
        e1 = jnp.exp(l1 - mx)
        e2 = jnp.exp(l2 - mx)
        inv = 1.0 / (e0 + e1 + e2)
        w0 = e0 * inv
        w1 = e1 * inv
        w2 = e2 * inv
        w_ref[0, 0] = w0
        w_ref[1, 0] = w1
        w_ref[2, 0] = w2
        att_ref[0] = w0 * og[0] + w1 * og[1] + w2 * og[2]

    in_specs = [_qkv_spec(kind, g) for g in range(NG) for kind in range(3)]
    in_specs += [pl.BlockSpec((S, HD), lambda b, h: (0, 0)), pl.BlockSpec((S, HD), lambda b, h: (0, 0)),
                 pl.BlockSpec((NG, HD), lambda b, h: (0, 0)), pl.BlockSpec((NG, HD), lambda b, h: (0, 0))]
    stat = lambda: pl.BlockSpec((NG, 1, S, HD), lambda b, h: (0, b, 0, h))
    return pl.pallas_call(
        body, grid=(BL, NH), name="attn_fwd",
        in_specs=in_specs,
        out_specs=[pl.BlockSpec((1, S, HD), lambda b, h: (b, 0, h)), stat(), stat()],
        out_shape=[jax.ShapeDtypeStruct((BL, S, ATT), F32),
                   jax.ShapeDtypeStruct((NG, BL, S, ATT), F32),
                   jax.ShapeDtypeStruct((NG, BL, S, ATT), F32)],
        scratch_shapes=[pltpu.VMEM((2, S, HD), F32), pltpu.VMEM((S, HD), BF16), pltpu.VMEM((S + QB, HD), BF16),
                        pltpu.VMEM((S + QB, HD), BF16), pltpu.VMEM((NG, S, HD), F32)],
        compiler_params=_cp(("parallel", "parallel"), VMEM_BIG),
    )(*_hbm(*([proj3] * 9), cos_t, sin_t, q_norm, k_norm))


def _attn_bwd_group(g, proj3, cos_t, sin_t, qn_g, kn_g, lse, wts, datt3, sbar3, slabs):
    dil = PATTERNS[g][1]
    n_alias = 0 if slabs is None else 3

    def norm_rope_bwd(dpost, raw, gain, cs, sn):
        dn = _rope_t(dpost, cs, sn)
        rstd = lax.rsqrt(jnp.mean(raw * raw, axis=-1, keepdims=True) + EPS)
        xh = raw * rstd
        dgain = jnp.sum(dn * xh, axis=0, keepdims=True)
        gd = dn * gain
        draw = rstd * (gd - xh * jnp.mean(gd * xh, axis=-1, keepdims=True))
        return draw, dgain

    def body(*refs):
        (q_ref, k_ref, v_ref, cos_ref, sin_ref, qn_ref, kn_ref, lse_ref, w_ref, datt_ref, sbar_ref) = refs[:11]
        (dq_ref, dk_ref, dv_ref, dqn_ref, dkn_ref, nat, qs, ksp, vsp, dos, cvp, lsp, acc) = refs[11 + n_alias:]
        qn = qn_ref[...]
        kn = kn_ref[...]
        cs = cos_ref[...]
        sn = sin_ref[...]
        _att_prep(q_ref, k_ref, v_ref, cos_ref, sin_ref, qn, kn, dil, nat, qs, ksp, vsp)
        wv = w_ref[0, 0]
        _permute(nat, wv * datt_ref[0], dil, dos)
        _permute(nat, wv * sbar_ref[0], dil, cvp)
        _permute(nat, lse_ref[0, 0], dil, lsp)
        q3, s_c, s_p = _att_scores(qs, ksp, dil)
        do3 = _blocks3(dos)
        lse3 = _blocks3(lsp)[:, :, 0:1]
        cv3 = _blocks3(cvp)[:, :, 0:1]
        p_c = jnp.exp(s_c - lse3)
        ds_c = (p_c * (_bdot_nt(do3, _blocks3(vsp, QB)) - cv3)).astype(BF16)
        dq = _bdot(ds_c, _blocks3(ksp, QB))
        acc[0] = _bdot_tn(ds_c, q3).reshape(S, HD)
        acc[1] = _bdot_tn(p_c.astype(BF16), do3).reshape(S, HD)
        if s_p is not None:
            p_p = jnp.exp(s_p - lse3)
            ds_p = (p_p * (_bdot_nt(do3, _blocks3(vsp)) - cv3)).astype(BF16)
            dq = dq + _bdot(ds_p, _blocks3(ksp))
            early = pl.ds(0, S - QB)
            acc[0, early, :] += _bdot_tn(ds_p, q3).reshape(S, HD)[QB:]
            acc[1, early, :] += _bdot_tn(p_p.astype(BF16), do3).reshape(S, HD)[QB:]
        _unpermute(nat, (dq * SCALE).reshape(S, HD), dil, nat.at[0])
        draw, dqn = norm_rope_bwd(nat[0], q_ref[0].astype(F32), qn, cs, sn)
        dq_ref[0] = draw.astype(BF16)
        _unpermute(nat, acc[0] * SCALE, dil, nat.at[0])
        draw, dkn = norm_rope_bwd(nat[0], k_ref[0].astype(F32), kn, cs, sn)
        dk_ref[0] = draw.astype(BF16)
        _unpermute(nat, acc[1], dil, nat.at[0])
        dv_ref[0] = nat[0].astype(BF16)
        first = (pl.program_id(0) == 0) & (pl.program_id(1) == 0)

        @pl.when(first)
        def _():
            dqn_ref[...] = dqn
            dkn_ref[...] = dkn

        @pl.when(jnp.logical_not(first))
        def _():
            dqn_ref[...] += dqn
            dkn_ref[...] += dkn

    full = lambda r: pl.BlockSpec((r, HD), lambda b, h: (0, 0))
    stat = lambda: pl.BlockSpec((1, 1, S, HD), lambda b, h: (g, b, 0, h))
    slab = lambda: pl.BlockSpec((1, S, HD), lambda b, h: (b, 0, h))
    out_slab = lambda: pl.BlockSpec((1, S, HD), lambda b, h: (b, 0, g * NH + h))
    big = jax.ShapeDtypeStruct((BL, S, GW), BF16)
    vecs = jax.ShapeDtypeStruct((1, HD), F32)
    in_specs = [_qkv_spec(0, g), _qkv_spec(1, g), _qkv_spec(2, g), full(S), full(S), full(1), full(1),
                stat(), stat(), slab(), slab()]
    args = [proj3, proj3, proj3, cos_t, sin_t, qn_g, kn_g, lse, wts, datt3, sbar3]
    aliases = {}
    if slabs is not None:
        in_specs += [pl.BlockSpec(memory_space=pl.ANY)] * 3
        args += list(slabs)
        aliases = {11: 0, 12: 1, 13: 2}
    return pl.pallas_call(
        body, grid=(BL, NH), name="attn_bwd_g%d" % g,
        in_specs=in_specs,
        out_specs=[out_slab(), out_slab(), out_slab(), full(1), full(1)],
        out_shape=[big, big, big, vecs, vecs],
        scratch_shapes=[pltpu.VMEM((2, S, HD), F32), pltpu.VMEM((S, HD), BF16), pltpu.VMEM((S + QB, HD), BF16),
                        pltpu.VMEM((S + QB, HD), BF16), pltpu.VMEM((S, HD), BF16), pltpu.VMEM((S, HD), F32),
                        pltpu.VMEM((S, HD), F32), pltpu.VMEM((2, S, HD), F32)],
        input_output_aliases=aliases,
        compiler_params=_cp(("arbitrary", "arbitrary"), VMEM_BIG),
    )(*_hbm(*args))


def _tail(x, proj, h, att, p, tgt, w_o_rnn, w_o_att_t, w_out, w_pg, w_ple_t, norm_ple, b_pg, tm=256):
    nt = T // tm
    inv_d = 1.0 / D

    def body(x_ref, h_ref, zr_ref, att_ref, za_ref, g0a_ref, g0b_ref, g1a_ref, g1b_ref, p_ref, tgt_ref,
             np_ref, bpg_ref, wor_hbm, woa_hbm, wout_hbm, wpg_hbm, wple_hbm,
             dx1_ref, merged_ref, n1_ref, dpre_ref, dpe_ref, dyr_ref, dya_ref, slab_a_ref, slab_c_ref, dh_ref,
             datt_ref, sbar_ref, yrnn_ref, yatt_ref, loss_ref, dnp_ref, dbpg_ref,
             wor, woa, wout, wpg, wple):
        first = pl.program_id(0) == 0

        @pl.when(first)
        def _():
            pltpu.sync_copy(wor_hbm, wor)
            pltpu.sync_copy(woa_hbm, woa)
            pltpu.sync_copy(wout_hbm, wout)
            pltpu.sync_copy(wpg_hbm, wpg)
            pltpu.sync_copy(wple_hbm, wple)

        xv = x_ref[...]
        hv = h_ref[...]
        zr = zr_ref[...].astype(F32)
        av = att_ref[...]
        za = za_ref[...].astype(F32)
        szr = _sigmoid(zr)
        silu_r = zr * szr
        yrnn_b = (hv * silu_r).astype(BF16)
        sza = _sigmoid(za)
        silu_a = za * sza
        yatt_b = (av * silu_a).astype(BF16)
        yrnn_ref[...] = yrnn_b
        yatt_ref[...] = yatt_b
        yr = _dot(yrnn_b, wor[...])
        ya = _dot_nt(yatt_b, woa[...])
        g0 = _sigmoid(jnp.concatenate([g0a_ref[...], g0b_ref[...]], axis=1).astype(F32))
        g1 = _sigmoid(jnp.concatenate([g1a_ref[...], g1b_ref[...]], axis=1).astype(F32))
        merged_b = (g0 * yr + g1 * ya).astype(BF16)
        merged_ref[...] = merged_b
        x1 = xv + _dot(merged_b, wout[...])
        rstd = lax.rsqrt(jnp.mean(x1 * x1, axis=-1, keepdims=True) + EPS)
        xh = x1 * rstd
        npl = np_ref[...]
        n1_b = (xh * npl).astype(BF16)
        n1_ref[...] = n1_b
        pg = _sigmoid(_dot(n1_b, wpg[...]) + bpg_ref[...])
        pe = _dot_nt(p_ref[...].astype(BF16), wple[...])
        err = x1 + pg * pe - tgt_ref[...]
        loss_t = 0.5 * inv_d * jnp.sum(err * err)
        dy = err * inv_d
        dpe_ref[...] = (dy * pg).astype(BF16)
        dpre = dy * pe * pg * (1.0 - pg)
        dpre_b = dpre.astype(BF16)
        dpre_ref[...] = dpre_b
        dn1 = _dot_nt(dpre_b, wpg[...])
        dnp = jnp.sum(dn1 * xh, axis=0, keepdims=True)
        dbpg = jnp.sum(dpre, axis=0, keepdims=True)
        gd = dn1 * npl
        dx1 = dy + rstd * (gd - xh * jnp.mean(gd * xh, axis=-1, keepdims=True))
        dx1_ref[...] = dx1
        dmerged = _dot_nt(dx1.astype(BF16), wout[...])
        dyr_b = (dmerged * g0).astype(BF16)
        dya_b = (dmerged * g1).astype(BF16)
        dyr_ref[...] = dyr_b
        dya_ref[...] = dya_b
        slab_c_ref[:, ATT:ATT + D] = (dmerged * yr * g0 * (1.0 - g0)).astype(BF16)
        slab_c_ref[:, ATT + D:ATT + 2 * D] = (dmerged * ya * g1 * (1.0 - g1)).astype(BF16)
        dyrnn = _dot_nt(dyr_b, wor[...])
        dyatt = _dot(dya_b, woa[...])
        dh_ref[...] = dyrnn * silu_r
        slab_a_ref[...] = (dyrnn * hv * szr * (1.0 + zr * (1.0 - szr))).astype(BF16)
        datt = dyatt * silu_a
        datt_ref[...] = datt
        slab_c_ref[:, 0:ATT] = (dyatt * av * sza * (1.0 + za * (1.0 - sza))).astype(BF16)
        da = datt * av
        for hh in range(NH):
            seg = slice(hh * HD, (hh + 1) * HD)
            sbar_ref[:, seg] = jnp.broadcast_to(jnp.sum(da[:, seg], axis=-1, keepdims=True), (tm, HD))

        @pl.when(first)
        def _():
            loss_ref[...] = jnp.full((8, LANES), loss_t, F32)
            dnp_ref[...] = dnp
            dbpg_ref[...] = dbpg

        @pl.when(jnp.logical_not(first))
        def _():
            loss_ref[...] += jnp.full((8, LANES), loss_t, F32)
            dnp_ref[...] += dnp
            dbpg_ref[...] += dbpg

    tok = lambda w: pl.BlockSpec((tm, w), lambda i: (i, 0))
    col = lambda w, blk: pl.BlockSpec((tm, w), lambda i: (i, blk))
    vec = lambda: pl.BlockSpec((1, D), lambda i: (0, 0))
    hbm = lambda: pl.BlockSpec(memory_space=pl.ANY)
    gb = OFF_G // 512
    in_specs = [tok(D), tok(DR), col(DR, 1), tok(ATT), col(ATT, OFF_ZA // ATT),
                col(512, gb), col(512, gb + 1), col(512, gb + 2), col(512, gb + 3),
                tok(PLE), tok(D), vec(), vec(), hbm(), hbm(), hbm(), hbm(), hbm()]
    sh = lambda w, dt: jax.ShapeDtypeStruct((T, w), dt)
    out_shape = [sh(D, F32), sh(D, BF16), sh(D, BF16), sh(D, BF16), sh(D, BF16), sh(D, BF16), sh(D, BF16),
                 sh(A_W, BF16), sh(C_W, BF16), sh(DR, F32), sh(ATT, F32), sh(ATT, F32),
                 sh(DR, BF16), sh(ATT, BF16),
                 jax.ShapeDtypeStruct((8, LANES), F32), jax.ShapeDtypeStruct((1, D), F32),
                 jax.ShapeDtypeStruct((1, D), F32)]
    out_specs = [tok(D), tok(D), tok(D), tok(D), tok(D), tok(D), tok(D), col(DR, 1), tok(C_W), tok(DR),
                 tok(ATT), tok(ATT), tok(DR), tok(ATT),
                 pl.BlockSpec((8, LANES), lambda i: (0, 0)), vec(), vec()]
    return pl.pallas_call(
        body, grid=(nt,), name="tail_fwd_bwd",
        in_specs=in_specs, out_specs=out_specs, out_shape=out_shape,
        scratch_shapes=[pltpu.VMEM((DR, D), BF16), pltpu.VMEM((D, ATT), BF16), pltpu.VMEM((D, D), BF16),
                        pltpu.VMEM((D, D), BF16), pltpu.VMEM((D, PLE), BF16)],
        compiler_params=_cp(("arbitrary",), VMEM_BIG),
    )(*_hbm(x, h, proj, att, proj, proj, proj, proj, proj, p, tgt, norm_ple, b_pg, w_o_rnn, w_o_att_t, w_out, w_pg,
            w_ple_t))


def _input_norm_bwd(x, dhn, dx1, gain, tm=512):
    def body(x_ref, dhn_ref, dx1_ref, g_ref, dx_ref, dg_ref):
        xv = x_ref[...]
        rstd = lax.rsqrt(jnp.mean(xv * xv, axis=-1, keepdims=True) + EPS)
        xh = xv * rstd
        dn = dhn_ref[...]
        dg = jnp.sum(dn * xh, axis=0, keepdims=True)
        gd = dn * g_ref[...]
        dx_ref[...] = dx1_ref[...] + rstd * (gd - xh * jnp.mean(gd * xh, axis=-1, keepdims=True))
        first = pl.program_id(0) == 0

        @pl.when(first)
        def _():
            dg_ref[...] = dg

        @pl.when(jnp.logical_not(first))
        def _():
            dg_ref[...] += dg

    tok = lambda: pl.BlockSpec((tm, D), lambda i: (i, 0))
    vec = lambda: pl.BlockSpec((1, D), lambda i: (0, 0))
    return pl.pallas_call(
        body, grid=(T // tm,), name="input_norm_bwd",
        in_specs=[tok(), tok(), tok(), vec()], out_specs=[tok(), vec()],
        out_shape=[jax.ShapeDtypeStruct((T, D), F32), jax.ShapeDtypeStruct((1, D), F32)],
        compiler_params=_cp(("arbitrary",), VMEM_MID),
    )(*_hbm(x, dhn, dx1, gain))


def _rope_tables():
    pos = jnp.arange(S, dtype=F32)
    inv_freq = ROPE_THETA ** (-jnp.arange(0, HD, 2, dtype=F32) / HD)
    ang = pos[:, None] * inv_freq[None, :]
    cos, sin = jnp.cos(ang), jnp.sin(ang)
    return jnp.concatenate([cos, cos], axis=1), jnp.concatenate([-sin, sin], axis=1)


def _local_step(x, p, tgt, project, other_weights, norm_mix, conv_b,
                w_rg_a, b_rg_a, w_rg_x, b_rg_x, lam, q_norm, k_norm, norm_ple, b_pg, start_reduce=None,
                entry_token=None):
    if start_reduce is None:
        start_reduce = lambda arrs, tag: (jnp.zeros((8, LANES), F32), arrs)
    if entry_token is None:
        entry_token = jnp.zeros((8, LANES), F32)
    cos_t, sin_t = _rope_tables()
    wa_b = w_rg_a.astype(BF16)
    wx_b = w_rg_x.astype(BF16)

    hn = _rmsnorm_fwd(x, norm_mix, entry_token)
    proj, w_bufs, chips = project(hn)
    w_o_rnn, w_o_att_t, w_out, w_pg, w_ple_t, conv_w = other_weights(proj)
    proj3 = proj.reshape(BL, S, NIN)
    h3 = _rnn_fwd(proj3, conv_w, conv_b, wa_b, b_rg_a, wx_b, b_rg_x, lam)
    att3, lse, wts = _attn_fwd(proj3, cos_t, sin_t, q_norm, k_norm)
    (dx1, merged, n1, dpre, dpe, dyr, dya, slab_a, slab_c, dh, datt, sbar, yrnn, yatt, loss8, dnp, dbpg) = _tail(
        x, proj, h3.reshape(T, DR), att3.reshape(T, ATT), p, tgt, w_o_rnn, w_o_att_t, w_out, w_pg, w_ple_t,
        norm_ple, b_pg)

    token, pending_out = start_reduce([
        _mm_tn(yrnn, dyr, 640, 2048, "dw_o_rnn"),
        _mm_tn(dya, yatt, 512, 2048, "dw_o_att_t"),
        _mm_tn(merged, dx1, 512, 2048, "dw_out"),
        _mm_tn(n1, dpre, 512, 2048, "dw_ple_gate"),
        _mm_tn(dpe, p, 512, 2048, "dw_ple_t")], "out")

    slab_a3, dcw, dcb, dwa, dba, dwx, dbx, dlam = _rnn_bwd(
        proj3, h3, dh.reshape(BL, S, DR), slab_a.reshape(BL, S, A_W), conv_w, conv_b, wa_b, b_rg_a, wx_b, b_rg_x, lam,
        token)
    datt3 = datt.reshape(BL, S, ATT)
    sbar3 = sbar.reshape(BL, S, ATT)
    slabs = None
    dqn = []
    dkn = []
    for g in range(NG):
        dq, dk, dv, dqn_g, dkn_g = _attn_bwd_group(g, proj3, cos_t, sin_t, q_norm[g:g + 1], k_norm[g:g + 1],
                                                   lse, wts, datt3, sbar3, slabs)
        slabs = (dq, dk, dv)
        dqn.append(dqn_g)
        dkn.append(dkn_g)
    pieces = [slab_a3.reshape(T, A_W)] + [t.reshape(T, GW) for t in slabs] + [slab_c]
    dw_in_t, db_in = _dw_in(pieces, hn)
    token, pending_in = start_reduce([dw_in_t], "in")
    dhn = _dhn(pieces, w_bufs, chips, token)
    grad_x, dnm = _input_norm_bwd(x, dhn, dx1, norm_mix)

    small = dict(w_rg_a=dwa, w_rg_x=dwx, norm_mix=dnm, b_in=db_in, conv_b=dcb, b_rg_a=dba, b_rg_x=dbx,
                 lru_lambda=dlam, q_norm=dqn, k_norm=dkn, norm_ple=dnp, b_ple_gate=dbpg, conv_w=dcw)
    return loss8[0, 0], grad_x, pending_out, pending_in, small


MESH = pl.DeviceIdType.MESH
HBM_SPEC = pl.BlockSpec(memory_space=pl.ANY)


def _my_pos():
    return lax.axis_index("x"), lax.axis_index("y"), lax.axis_index("c")


def _flip(pos, k):
    x, y, c = pos
    return (1 - x if k & 4 else x, 1 - y if k & 2 else y, 1 - c if k & 1 else c)


def _lin(pos):
    return 4 * pos[0] + 2 * pos[1] + pos[2]


def _chip(pos):
    return 2 * pos[0] + pos[1]


def _all_gather_two_level(shards, name):
    na = len(shards)

    def body(*refs):
        x_refs = refs[:na]
        out_refs = refs[na:2 * na]
        send_sems, recv_sems, local_sems = refs[2 * na:]
        me = _my_pos()
        sibling = _flip(me, 1)
        chips = [_flip(me, 4), _flip(me, 2), _flip(me, 6)]

        def copy(i, k, block, to, from_x=False):
            dst = out_refs[i].at[_lin(block)]
            return pltpu.make_async_remote_copy(
                src_ref=x_refs[i] if from_x else dst, dst_ref=dst,
                send_sem=send_sems.at[7 * i + k], recv_sem=recv_sems.at[7 * i + k], device_id=to, device_id_type=MESH)

        started = []
        for i in range(na):
            mine = pltpu.make_async_copy(x_refs[i], out_refs[i].at[_lin(me)], local_sems.at[i])
            mine.start()
            started.append(mine)
        sends = []
        for i in range(na):
            cps = [copy(i, 0, me, sibling, True)] + [copy(i, 1 + j, me, chip, True) for j, chip in enumerate(chips)]
            for cp in cps:
                cp.start()
            sends += cps
        for i in range(na):
            for j, chip in enumerate(chips):
                copy(i, 1 + j, chip, me).wait_recv()
                fwd = copy(i, 4 + j, chip, sibling)
                fwd.start()
                sends.append(fwd)
        for i in range(na):
            copy(i, 0, sibling, me).wait_recv()
            for j, chip in enumerate(chips):
                copy(i, 4 + j, _flip(chip, 1), me).wait_recv()
        for cp in sends:
            cp.wait_send()
        for mine in started:
            mine.wait()

    return pl.pallas_call(
        body, name=name,
        out_shape=[jax.ShapeDtypeStruct((NDEV,) + s.shape, s.dtype) for s in shards],
        in_specs=[HBM_SPEC] * na, out_specs=[HBM_SPEC] * na,
        scratch_shapes=[pltpu.SemaphoreType.DMA((7 * na,)), pltpu.SemaphoreType.DMA((7 * na,)),
                        pltpu.SemaphoreType.DMA((na,))],
    )(*shards)


def _all_gather_direct(shard, name):
    def body(x_ref, out_ref, send_sems, recv_sems, local_sem):
        me = _my_pos()
        mine = pltpu.make_async_copy(x_ref, out_ref.at[_lin(me)], local_sem)
        mine.start()
        sends = []
        for k in range(1, NDEV):
            cp = pltpu.make_async_remote_copy(
                src_ref=x_ref, dst_ref=out_ref.at[_lin(me)], send_sem=send_sems.at[k - 1],
                recv_sem=recv_sems.at[k - 1], device_id=_flip(me, k), device_id_type=MESH)
            cp.start()
            sends.append(cp)
        for k in range(1, NDEV):
            peer = _flip(me, k)
            pltpu.make_async_remote_copy(
                src_ref=x_ref, dst_ref=out_ref.at[_lin(peer)], send_sem=send_sems.at[k - 1],
                recv_sem=recv_sems.at[k - 1], device_id=peer, device_id_type=MESH).wait_recv()
        for cp in sends:
            cp.wait_send()
        mine.wait()

    return pl.pallas_call(
        body, name=name,
        out_shape=jax.ShapeDtypeStruct((NDEV,) + shard.shape, shard.dtype),
        in_specs=[HBM_SPEC], out_specs=HBM_SPEC,
        scratch_shapes=[pltpu.SemaphoreType.DMA((7,)), pltpu.SemaphoreType.DMA((7,)), pltpu.SemaphoreType.DMA],
    )(shard)


def _exchange_within_chip(parts, name):
    na = len(parts)

    def body(*refs):
        a_refs = refs[:na]
        recv_refs = refs[na:2 * na]
        send_sems, recv_sems = refs[2 * na:]
        me = _my_pos()
        c = me[2]
        sibling = _flip(me, 1)
        remote = []
        for i in range(na):
            for q in range(NCHIP):
                rc = pltpu.make_async_remote_copy(
                    src_ref=a_refs[i].at[q, 1 - c], dst_ref=recv_refs[i].at[q],
                    send_sem=send_sems.at[NCHIP * i + q], recv_sem=recv_sems.at[NCHIP * i + q],
                    device_id=sibling, device_id_type=MESH)
                rc.start()
                remote.append(rc)
        for rc in remote:
            rc.wait_recv()
        for rc in remote:
            rc.wait_send()

    return pl.pallas_call(
        body, name=name, out_shape=[jax.ShapeDtypeStruct((NCHIP,) + a.shape[2:], a.dtype) for a in parts],
        in_specs=[HBM_SPEC] * na, out_specs=[HBM_SPEC] * na,
        scratch_shapes=[pltpu.SemaphoreType.DMA((NCHIP * na,)), pltpu.SemaphoreType.DMA((NCHIP * na,))],
    )(*parts)


HBM_ONLY = pl.BlockSpec(memory_space=pltpu.HBM)
SEM_SPEC = pl.BlockSpec(memory_space=pltpu.SEMAPHORE)
SPLIT_COPY = pltpu.CompilerParams(has_side_effects=pltpu.SideEffectType.DATAFLOW_SIDE_EFFECTING)


def _chip_peers(me):
    return [_flip(me, 4), _flip(me, 2), _flip(me, 6)]


def _between_chips_start(parts, name):
    na = len(parts)

    def body(*refs):
        a_refs = refs[:na]
        land_refs = refs[na:2 * na]
        send_sems, recv_sems = refs[2 * na], refs[2 * na + 1]
        token = refs[-1]
        me = _my_pos()
        myq = _chip(me)
        for i in range(na):
            for j, peer in enumerate(_chip_peers(me)):
                pltpu.make_async_remote_copy(
                    src_ref=a_refs[i].at[_chip(peer)], dst_ref=land_refs[i].at[myq],
                    send_sem=send_sems.at[3 * i + j], recv_sem=recv_sems.at[3 * i + j],
                    device_id=peer, device_id_type=MESH).start()
        token[...] = jnp.zeros_like(token)

    hbm = [pltpu.HBM(a.shape, a.dtype) for a in parts]
    srcs = [pltpu.with_memory_space_constraint(a, pltpu.HBM) for a in parts]
    lands = [pltpu.with_memory_space_constraint(lax.empty(a.shape, a.dtype), pltpu.HBM) for a in parts]
    res = pl.pallas_call(
        body, name=name,
        out_shape=(pltpu.SemaphoreType.DMA((3 * na,)), pltpu.SemaphoreType.DMA((3 * na,)), *hbm, *hbm,
                   jax.ShapeDtypeStruct((8, LANES), F32)),
        in_specs=[HBM_ONLY] * (2 * na),
        out_specs=(SEM_SPEC, SEM_SPEC, *([HBM_ONLY] * (2 * na)), pl.BlockSpec(memory_space=pltpu.VMEM)),
        input_output_aliases={i: 2 + i for i in range(2 * na)},
        compiler_params=SPLIT_COPY,
    )(*srcs, *lands)
    return res[-1], (res[0], res[1], list(res[2:2 + na]), list(res[2 + na:2 + 2 * na]))


def _between_chips_wait(pending, after, name):
    send_sems, recv_sems, parts, lands = pending
    na = len(parts)

    def body(*refs):
        a_refs = refs[:na]
        land_refs = refs[na:2 * na]
        send_sems, recv_sems = refs[2 * na], refs[2 * na + 1]
        me = _my_pos()
        for i in range(na):
            for j, peer in enumerate(_chip_peers(me)):
                cp = pltpu.make_async_remote_copy(
                    src_ref=a_refs[i].at[_chip(peer)], dst_ref=land_refs[i].at[_chip(peer)],
                    send_sem=send_sems.at[3 * i + j], recv_sem=recv_sems.at[3 * i + j],
                    device_id=peer, device_id_type=MESH)
                cp.wait_send()
                cp.wait_recv()

    hbm = [pltpu.HBM(a.shape, a.dtype) for a in parts]
    res = pl.pallas_call(
        body, name=name, out_shape=(*hbm, *hbm),
        in_specs=[HBM_ONLY] * (2 * na) + [SEM_SPEC, SEM_SPEC, pl.BlockSpec(memory_space=pl.ANY)],
        out_specs=[HBM_ONLY] * (2 * na),
        input_output_aliases={i: i for i in range(2 * na)},
        compiler_params=SPLIT_COPY,
    )(*parts, *lands, send_sems, recv_sems, after)
    return list(res[:na]), list(res[na:])


def _remote(src, dst, send_sems, recv_sems, idx, peer):
    return pltpu.make_async_remote_copy(src_ref=src, dst_ref=dst, send_sem=send_sems.at[idx],
                                        recv_sem=recv_sems.at[idx], device_id=peer, device_id_type=MESH)


def _copies_own(bufs, me):
    return [(bufs[0], bufs[1].at[me[2]], 0, _flip(me, 1))]


def _copies_near(bufs, me):
    return [(bufs[0], bufs[1].at[0, me[2]], 0, _flip(me, 2)), (bufs[0], bufs[1].at[1, me[2]], 1, _flip(me, 4))]


def _copies_far(bufs, me):
    return [(bufs[0], bufs[1].at[me[2]], 0, _flip(me, 6))]


def _copies_others(bufs, me):
    na = len(bufs) // 2
    return [(bufs[i], bufs[na + i].at[_lin(me)], 7 * i + k - 1, _flip(me, k))
            for i in range(na) for k in range(1, NDEV)]


def _gather_start(w_shard, others, name):
    na = len(others)
    shp = w_shard.shape
    lands_t = [(2,) + shp, (2, 2) + shp, (2,) + shp] + [(NDEV,) + a.shape for a in others]
    land_dt = [w_shard.dtype] * 3 + [a.dtype for a in others]
    n_in = 1 + na + 3 + na
    counts = (1, 2, 1, 7 * na)

    def body(*refs):
        w_ref = refs[0]
        x_refs = refs[1:1 + na]
        own, near, far = refs[1 + na:4 + na]
        lands = refs[4 + na:n_in]
        sems = refs[n_in:n_in + 8]
        token = refs[-1]
        me = _my_pos()
        groups = ((_copies_own, [w_ref, own]), (_copies_near, [w_ref, near]), (_copies_far, [w_ref, far]),
                  (_copies_others, list(x_refs) + list(lands)))
        for gi, (copies, bufs) in enumerate(groups):
            for src, dst, idx, peer in copies(bufs, me):
                _remote(src, dst, sems[2 * gi], sems[2 * gi + 1], idx, peer).start()
        token[...] = jnp.zeros_like(token)

    ins = [w_shard, *others] + [lax.empty(s, d) for s, d in zip(lands_t, land_dt)]
    ins = [pltpu.with_memory_space_constraint(a, pltpu.HBM) for a in ins]
    sem_t = []
    for cnt in counts:
        sem_t += [pltpu.SemaphoreType.DMA((cnt,)), pltpu.SemaphoreType.DMA((cnt,))]
    res = pl.pallas_call(
        body, name=name,
        out_shape=(*sem_t, *[pltpu.HBM(a.shape, a.dtype) for a in ins], jax.ShapeDtypeStruct((8, LANES), F32)),
        in_specs=[HBM_ONLY] * n_in,
        out_specs=(*([SEM_SPEC] * 8), *([HBM_ONLY] * n_in), pl.BlockSpec(memory_space=pltpu.VMEM)),
        input_output_aliases={i: 8 + i for i in range(n_in)},
        compiler_params=SPLIT_COPY,
    )(*ins)
    sems, thru = res[:8], res[8:8 + n_in]
    w_thru, oth_thru = thru[0], list(thru[1:1 + na])
    own, near, far = thru[1 + na:4 + na]
    state = dict(own=(sems[0], sems[1], [w_thru, own]), near=(sems[2], sems[3], [w_thru, near]),
                 far=(sems[4], sems[5], [w_thru, far]), others=(sems[6], sems[7], oth_thru + list(thru[4 + na:])))
    return res[-1], state


def _gather_wait(group, send_sems, recv_sems, bufs, after, name):
    nb = len(bufs)
    copies = dict(own=_copies_own, near=_copies_near, far=_copies_far, others=_copies_others)[group]

    def body(*refs):
        b = refs[:nb]
        ss, rs = refs[nb], refs[nb + 1]
        me = _my_pos()
        for src, dst, idx, peer in copies(b, me):
            if group == "others":
                landed = b[nb // 2 + idx // 7].at[_lin(peer)]
            elif group == "own":
                landed = b[1].at[1 - me[2]]
            else:
                landed = dst
            cp = _remote(src, landed, ss, rs, idx, peer)
            cp.wait_send()
            cp.wait_recv()

    res = pl.pallas_call(
        body, name=name, out_shape=[pltpu.HBM(a.shape, a.dtype) for a in bufs],
        in_specs=[HBM_ONLY] * nb + [SEM_SPEC, SEM_SPEC, pl.BlockSpec(memory_space=pl.ANY)],
        out_specs=[HBM_ONLY] * nb,
        input_output_aliases={i: i for i in range(nb)},
        compiler_params=SPLIT_COPY,
    )(*bufs, send_sems, recv_sems, after)
    return list(res)


def _forward_to_sibling(buf, name):
    n = buf.shape[0]

    def body(_in_ref, out_ref, send_sems, recv_sems):
        me = _my_pos()
        c = me[2]
        sibling = _flip(me, 1)
        sends = []
        for r in range(n):
            cp = _remote(out_ref.at[r, c], out_ref.at[r, c], send_sems, recv_sems, r, sibling)
            cp.start()
            sends.append(cp)
        for r in range(n):
            _remote(out_ref.at[r, c], out_ref.at[r, 1 - c], send_sems, recv_sems, r, sibling).wait_recv()
        for cp in sends:
            cp.wait_send()

    return pl.pallas_call(
        body, name=name, out_shape=jax.ShapeDtypeStruct(buf.shape, buf.dtype),
        in_specs=[HBM_SPEC], out_specs=HBM_SPEC,
        scratch_shapes=[pltpu.SemaphoreType.DMA((n,)), pltpu.SemaphoreType.DMA((n,))],
        input_output_aliases={0: 0},
    )(buf)


def _exchange_all(parts, name):
    na = len(parts)

    def body(*refs):
        a_refs = refs[:na]
        out_refs = refs[na:2 * na]
        send_sems, recv_sems = refs[2 * na:]
        me = _my_pos()
        sends = []
        for i in range(na):
            for k in range(1, NDEV):
                peer = _flip(me, k)
                cp = pltpu.make_async_remote_copy(
                    src_ref=a_refs[i].at[_lin(peer)], dst_ref=out_refs[i].at[_lin(me)],
                    send_sem=send_sems.at[7 * i + k - 1], recv_sem=recv_sems.at[7 * i + k - 1],
                    device_id=peer, device_id_type=MESH)
                cp.start()
                sends.append(cp)
        for i in range(na):
            for k in range(1, NDEV):
                peer = _flip(me, k)
                pltpu.make_async_remote_copy(
                    src_ref=a_refs[i].at[_lin(peer)], dst_ref=out_refs[i].at[_lin(peer)],
                    send_sem=send_sems.at[7 * i + k - 1], recv_sem=recv_sems.at[7 * i + k - 1],
                    device_id=peer, device_id_type=MESH).wait_recv()
        for cp in sends:
            cp.wait_send()

    return pl.pallas_call(
        body, name=name, out_shape=[jax.ShapeDtypeStruct(a.shape, a.dtype) for a in parts],
        in_specs=[HBM_SPEC] * na, out_specs=[HBM_SPEC] * na,
        scratch_shapes=[pltpu.SemaphoreType.DMA((7 * na,)), pltpu.SemaphoreType.DMA((7 * na,))],
    )(*parts)


def _scalar(v):
    return jnp.asarray(v, jnp.int32).reshape(1)


def _sum_pairs(parts, theirs, name):
    na = len(parts)

    def body(c_ref, *refs):
        for i in range(na):
            o_ref = refs[2 * na + i]
            o_ref[0] = (refs[i][0, 0].astype(F32) + refs[na + i][0].astype(F32)).astype(o_ref.dtype)

    def mine_spec(a):
        return pl.BlockSpec((1, 1) + a.shape[2:], lambda q, c_ref: (q, c_ref[0], 0, 0))

    def spec(a):
        return pl.BlockSpec((1,) + a.shape[1:], lambda q, c_ref: (q, 0, 0))

    return pl.pallas_call(
        body, name=name,
        grid_spec=pltpu.PrefetchScalarGridSpec(
            num_scalar_prefetch=1, grid=(NCHIP,),
            in_specs=[mine_spec(a) for a in parts] + [spec(a) for a in theirs],
            out_specs=[spec(a) for a in theirs]),
        out_shape=[jax.ShapeDtypeStruct(a.shape, a.dtype) for a in theirs],
        compiler_params=_cp(("arbitrary",), VMEM_BIG),
    )(_scalar(lax.axis_index("c")), *parts, *theirs)


def _others(q, mine, nblk=NCHIP):
    return jnp.where(q == mine, (q + 1) % nblk, q)


def _sum_chips_adamw(own, recv, wv, mv, vv, tr, name):
    _, r, w = recv.shape

    def body(q_ref, own_ref, r0, r1, r2, r3, w_ref, m_ref, v_ref, g_ref, d_ref, m2_ref, v2_ref):
        myq = q_ref[0]
        acc = None
        for q, r_ref in enumerate((r0, r1, r2, r3)):
            term = jnp.where(myq == q, own_ref[0], r_ref[0]).astype(F32)
            acc = term if acc is None else acc + term
        g_ref[...] = acc
        delta, m2, v2 = _adam_math(w_ref[...], acc, m_ref[...], v_ref[...])
        d_ref[...] = delta
        m2_ref[...] = m2
        v2_ref[...] = v2

    def recv_spec(q):
        return pl.BlockSpec((1, tr, w), lambda i, q_ref: (_others(q, q_ref[0]), i, 0))

    rows = lambda: pl.BlockSpec((tr, w), lambda i, q_ref: (i, 0))
    shp = jax.ShapeDtypeStruct((r, w), F32)
    return pl.pallas_call(
        body, name=name,
        grid_spec=pltpu.PrefetchScalarGridSpec(
            num_scalar_prefetch=1, grid=(r // tr,),
            in_specs=[pl.BlockSpec((1, tr, w), lambda i, q_ref: (q_ref[0], i, 0))]
            + [recv_spec(q) for q in range(NCHIP)] + [rows(), rows(), rows()],
            out_specs=[rows(), rows(), rows(), rows()]),
        out_shape=[shp, shp, shp, shp],
        compiler_params=_cp(("arbitrary",), VMEM_MID),
    )(_scalar(_chip(_my_pos())), *_hbm(own, recv, recv, recv, recv, wv, mv, vv))


def _sum_blocks_small(own, recv, mine, transpose, name):
    na = len(recv)
    nblk = recv[0].shape[0]

    def body(q_ref, *refs):
        me = q_ref[0]
        for i in range(na):
            acc = None
            for q in range(nblk):
                term = jnp.where(me == q, refs[i][0], refs[na * (1 + q) + i][0]).astype(F32)
                acc = term if acc is None else acc + term
            refs[na * (1 + nblk) + i][...] = acc.T if transpose[i] else acc

    def oshape(a, tr):
        r, w = a.shape[1:]
        return (w, r) if tr else (r, w)

    own_spec = lambda a: pl.BlockSpec((1,) + a.shape[1:], lambda s, q_ref: (q_ref[0], 0, 0))
    recv_spec = lambda a, q: pl.BlockSpec((1,) + a.shape[1:], lambda s, q_ref: (_others(q, q_ref[0], nblk), 0, 0))
    out_spec = lambda shp: pl.BlockSpec(shp, lambda s, q_ref: (0, 0))
    in_specs = [own_spec(a) for a in own]
    for q in range(nblk):
        in_specs += [recv_spec(a, q) for a in recv]
    return pl.pallas_call(
        body, name=name,
        grid_spec=pltpu.PrefetchScalarGridSpec(
            num_scalar_prefetch=1, grid=(1,), in_specs=in_specs,
            out_specs=[out_spec(oshape(a, tr)) for a, tr in zip(recv, transpose)]),
        out_shape=[jax.ShapeDtypeStruct(oshape(a, tr), F32) for a, tr in zip(recv, transpose)],
        compiler_params=_cp(("arbitrary",), VMEM_MID),
    )(_scalar(mine), *own, *(list(recv) * nblk))


def _rep_offsets():
    offs = []
    o = 0
    for r in REP_ROWS:
        offs.append(o)
        o += r
    return offs


def _pack_small_grads(g):
    offs = _rep_offsets()

    def body(dwa, dwx, dnm, dbin, dcb, dba, dbx, dlam, dq0, dq1, dq2, dk0, dk1, dk2, dnp, dbpg, o_ref):
        o_ref[pl.ds(REP_TOTAL_ROWS - 2, NDEV * REP_ROWS_DEV - REP_TOTAL_ROWS + 2), :] = jnp.zeros(
            (NDEV * REP_ROWS_DEV - REP_TOTAL_ROWS + 2, LANES), F32)
        for n in range(NRB):
            o_ref[pl.ds(offs[0] + n * RBW, RBW), :] = dwa[n]
            o_ref[pl.ds(offs[1] + n * RBW, RBW), :] = dwx[n]

        def put_vec(off, ref, rows):
            for k in range(rows):
                o_ref[pl.ds(off + k, 1), :] = ref[:, k * LANES:(k + 1) * LANES]

        put_vec(offs[2], dnm, REP_ROWS[2])
        put_vec(offs[3], dbin, REP_ROWS[3])
        put_vec(offs[4], dcb, REP_ROWS[4])
        put_vec(offs[5], dba, REP_ROWS[5])
        put_vec(offs[6], dbx, REP_ROWS[6])
        put_vec(offs[7], dlam, REP_ROWS[7])
        for k, ref in enumerate((dq0, dq1, dq2)):
            o_ref[pl.ds(offs[8] + k, 1), :] = ref[...]
        for k, ref in enumerate((dk0, dk1, dk2)):
            o_ref[pl.ds(offs[9] + k, 1), :] = ref[...]
        put_vec(offs[10], dnp, REP_ROWS[10])
        put_vec(offs[11], dbpg, REP_ROWS[11])

    args = [g["w_rg_a"], g["w_rg_x"], g["norm_mix"], g["b_in"], g["conv_b"], g["b_rg_a"], g["b_rg_x"],
            g["lru_lambda"], *g["q_norm"], *g["k_norm"], g["norm_ple"], g["b_ple_gate"]]
    full = lambda shp: pl.BlockSpec(shp, lambda: (0,) * len(shp))
    return pl.pallas_call(
        body, name="pack_small_grads",
        in_specs=[full(a.shape) for a in args],
        out_specs=full((NDEV * REP_ROWS_DEV, LANES)),
        out_shape=jax.ShapeDtypeStruct((NDEV * REP_ROWS_DEV, LANES), F32),
    )(*_hbm(*args))


def _adam_math(wv, gv, mv, vv):
    c1 = 1.0 - B1 ** STEP
    c2 = 1.0 - B2 ** STEP
    m2 = B1 * mv + (1.0 - B1) * gv
    v2 = B2 * vv + (1.0 - B2) * (gv * gv)
    delta = (-LR) * ((m2 / c1) / (jnp.sqrt(v2 / c2) + AEPS) + WD * wv)
    return delta, m2, v2


def _adamw_small(rep_flat, w, m, v):
    offs = _rep_offsets()
    n = len(REP_NAMES)

    def body(*refs):
        g_ref = refs[0]
        w_refs = refs[1:1 + n]
        m_refs = refs[1 + n:1 + 2 * n]
        v_refs = refs[1 + 2 * n:1 + 3 * n]
        outs = refs[1 + 3 * n:]
        go, do, mo, vo = outs[:n], outs[n:2 * n], outs[2 * n:3 * n], outs[3 * n:]

        def emit(i, idx, gv):
            go[i][idx] = gv
            delta, m2, v2 = _adam_math(w_refs[i][idx], gv, m_refs[i][idx], v_refs[i][idx])
            do[i][idx] = delta
            mo[i][idx] = m2
            vo[i][idx] = v2

        for i in range(n):
            if i < 2:
                for b in range(NRB):
                    emit(i, b, g_ref[pl.ds(offs[i] + b * RBW, RBW), :])
            elif REP_NAMES[i] in ("q_norm", "k_norm"):
                emit(i, slice(None), g_ref[pl.ds(offs[i], NG), :])
            else:
                gv = jnp.concatenate([g_ref[pl.ds(offs[i] + k, 1), :] for k in range(REP_ROWS[i])], axis=1)
                emit(i, slice(None), gv)

    full = lambda shp: pl.BlockSpec(shp, lambda: (0,) * len(shp))
    pspecs = [full(a.shape) for a in w]
    pshapes = [jax.ShapeDtypeStruct(a.shape, F32) for a in w]
    res = pl.pallas_call(
        body, name="adamw_small",
        in_specs=[full(rep_flat.shape)] + pspecs * 3,
        out_specs=pspecs * 4, out_shape=pshapes * 4,
        compiler_params=_cp(None, VMEM_MID),
    )(*_hbm(rep_flat, *w, *m, *v))
    return res[:n], res[n:2 * n], res[2 * n:3 * n], res[3 * n:]


def _adamw_many(w, g, m, v):
    n = len(w)

    def body(*refs):
        for i in range(n):
            delta, m2, v2 = _adam_math(refs[i][...], refs[n + i][...], refs[2 * n + i][...], refs[3 * n + i][...])
            refs[4 * n + i][...] = delta
            refs[5 * n + i][...] = m2
            refs[6 * n + i][...] = v2

    full = lambda shp: pl.BlockSpec(shp, lambda: (0,) * len(shp))
    specs = [full(a.shape) for a in w]
    shapes = [jax.ShapeDtypeStruct(a.shape, F32) for a in w]
    res = pl.pallas_call(
        body, name="adamw_shards",
        in_specs=specs * 4, out_specs=specs * 3, out_shape=shapes * 3,
        compiler_params=_cp(None, VMEM_MID),
    )(*_hbm(*w, *g, *m, *v))
    return res[:n], res[n:2 * n], res[2 * n:]


def kernel(x, p, norm_mix, w_in, b_in, conv_w, conv_b, w_rg_a, b_rg_a, w_rg_x, b_rg_x, lru_lambda, q_norm, k_norm, w_o_rnn, w_o_att, w_out, norm_ple, w_ple_gate, b_ple_gate, w_ple, loss_target, m_norm_mix, m_w_in, m_b_in, m_conv_w, m_conv_b, m_w_rg_a, m_b_rg_a, m_w_rg_x, m_b_rg_x, m_lru_lambda, m_q_norm, m_k_norm, m_w_o_rnn, m_w_o_att, m_w_out, m_norm_ple, m_w_ple_gate, m_b_ple_gate, m_w_ple, v_norm_mix, v_w_in, v_b_in, v_conv_w, v_conv_b, v_w_rg_a, v_b_rg_a, v_w_rg_x, v_b_rg_x, v_lru_lambda, v_q_norm, v_k_norm, v_w_o_rnn, v_w_o_att, v_w_out, v_norm_ple, v_w_ple_gate, v_b_ple_gate, v_w_ple):
    w = dict(norm_mix=norm_mix, w_in=w_in, b_in=b_in, conv_w=conv_w, conv_b=conv_b, w_rg_a=w_rg_a, b_rg_a=b_rg_a,
             w_rg_x=w_rg_x, b_rg_x=b_rg_x, lru_lambda=lru_lambda, q_norm=q_norm, k_norm=k_norm, w_o_rnn=w_o_rnn,
             w_o_att=w_o_att, w_out=w_out, norm_ple=norm_ple, w_ple_gate=w_ple_gate, b_ple_gate=b_ple_gate,
             w_ple=w_ple)
    m = dict(norm_mix=m_norm_mix, w_in=m_w_in, b_in=m_b_in, conv_w=m_conv_w, conv_b=m_conv_b, w_rg_a=m_w_rg_a,
             b_rg_a=m_b_rg_a, w_rg_x=m_w_rg_x, b_rg_x=m_b_rg_x, lru_lambda=m_lru_lambda, q_norm=m_q_norm,
             k_norm=m_k_norm, w_o_rnn=m_w_o_rnn, w_o_att=m_w_o_att, w_out=m_w_out, norm_ple=m_norm_ple,
             w_ple_gate=m_w_ple_gate, b_ple_gate=m_b_ple_gate, w_ple=m_w_ple)
    v = dict(norm_mix=v_norm_mix, w_in=v_w_in, b_in=v_b_in, conv_w=v_conv_w, conv_b=v_conv_b, w_rg_a=v_w_rg_a,
             b_rg_a=v_b_rg_a, w_rg_x=v_w_rg_x, b_rg_x=v_b_rg_x, lru_lambda=v_lru_lambda, q_norm=v_q_norm,
             k_norm=v_k_norm, w_o_rnn=v_w_o_rnn, w_o_att=v_w_o_att, w_out=v_w_out, norm_ple=v_norm_ple,
             w_ple_gate=v_w_ple_gate, b_ple_gate=v_b_ple_gate, w_ple=v_w_ple)
    names = list(w.keys())

    shards = [w_in[0].T.astype(BF16), w_o_rnn[0].astype(BF16), w_o_att[0].T.astype(BF16), w_out[0].astype(BF16),
              w_ple_gate[0].astype(BF16), w_ple[0].T.astype(BF16), conv_w[0]]
    entry_token, gather = _gather_start(shards[0], shards[1:], "gather_start")
    pos = _my_pos()
    me, my_core, my_chip = _lin(pos), pos[2], _chip(pos)

    def project(hn):
        w_src, own = _gather_wait("own", *gather["own"], hn, "gather_wait_own")
        own = lax.dynamic_update_slice(own, shards[0][None], (my_core, 0, 0)).reshape(1, CHIP_COLS, D)
        chips = [jnp.stack([my_chip]), jnp.stack([my_chip ^ 1, my_chip ^ 2]), jnp.stack([my_chip ^ 3])]
        chips = [c.astype(jnp.int32) for c in chips]
        proj = _in_proj_chips(hn, own, b_in, chips[0], None, "in_proj_own")
        w_src, near = _gather_wait("near", gather["near"][0], gather["near"][1], [w_src, gather["near"][2][1]], proj,
                                   "gather_wait_near")
        near = _forward_to_sibling(near, "gather_forward_near").reshape(2, CHIP_COLS, D)
        proj = _in_proj_chips(hn, near, b_in, chips[1], proj, "in_proj_near")
        w_src, far = _gather_wait("far", gather["far"][0], gather["far"][1], [w_src, gather["far"][2][1]], proj,
                                  "gather_wait_far")
        far = _forward_to_sibling(far[None], "gather_forward_far").reshape(1, CHIP_COLS, D)
        proj = _in_proj_chips(hn, far, b_in, chips[2], proj, "in_proj_far")
        return proj, [own, near, far], jnp.concatenate(chips)

    def other_weights(after):
        bufs = _gather_wait("others", *gather["others"], after, "gather_wait_others")
        na = len(bufs) // 2
        full = [lax.dynamic_update_slice(a, s[None], (me, 0, 0)) for a, s in zip(bufs[na:], bufs[:na])]
        conv_f = full[5].transpose(1, 0, 2).reshape(CONVW, DR)
        return (*[a.reshape((NDEV * a.shape[1], a.shape[2])) for a in full[:5]], conv_f)

    def start_reduce(arrs, tag):
        parts = [a.reshape((NCHIP, 2, a.shape[0] // NDEV, a.shape[1])) for a in arrs]
        theirs = _exchange_within_chip(parts, "reduce_within_chip_" + tag)
        return _between_chips_start(_sum_pairs(parts, theirs, "sum_pairs_" + tag), "reduce_between_chips_start_" + tag)

    loss_part, grad_x, pending_out, pending_in, small = _local_step(
        x.reshape(T, D), p.reshape(T, PLE), loss_target.reshape(T, D),
        project, other_weights,
        norm_mix, conv_b, w_rg_a[0], b_rg_a, w_rg_x[0], b_rg_x, lru_lambda, q_norm[0], k_norm[0],
        norm_ple, b_ple_gate, start_reduce, entry_token)
    loss = lax.psum(loss_part, ("x", "y", "c"))

    rep_parts = _pack_small_grads(small).reshape(NDEV, REP_ROWS_DEV, LANES)
    conv_parts = small["conv_w"].reshape(CONVW, NDEV, DR // NDEV).transpose(1, 0, 2)
    small_parts = [rep_parts, conv_parts]
    g_rep, g_conv = _sum_blocks_small(small_parts, _exchange_all(small_parts, "reduce_small"), me, (False, False),
                                      "sum_small")
    rep_all = _all_gather_direct(g_rep, "gather_small").reshape(NDEV * REP_ROWS_DEV, LANES)

    myq = _chip(_my_pos())
    own_out, recv_out = _between_chips_wait(pending_out, rep_all, "reduce_between_chips_wait_out")
    own_in, recv_in = _between_chips_wait(pending_in, rep_all, "reduce_between_chips_wait_in")
    w_in_res = _sum_chips_adamw(own_in[0], recv_in[0], w_in[0].T, m_w_in[0].T, v_w_in[0].T, 304, "adamw_w_in")
    g_o_rnn, g_o_att, g_out, g_pg, g_ple = _sum_blocks_small(
        own_out, recv_out, myq, (False, True, False, False, True), "sum_chips_out")

    grad, delta, new_m, new_v = {}, {}, {}, {}
    rep_shape = lambda a: a if a.ndim == 2 else a.reshape(a.shape[1:])
    res = _adamw_small(rep_all, [rep_shape(w[n]) for n in REP_NAMES], [rep_shape(m[n]) for n in REP_NAMES],
                       [rep_shape(v[n]) for n in REP_NAMES])
    for dst, vals in zip((grad, delta, new_m, new_v), res):
        for n, a in zip(REP_NAMES, vals):
            dst[n] = a.reshape(w[n].shape)
    grad["w_in"], delta["w_in"], new_m["w_in"], new_v["w_in"] = [a.T[None] for a in w_in_res]
    rest = ("w_o_rnn", "w_o_att", "w_out", "w_ple_gate", "w_ple", "conv_w")
    g_rest = [g_o_rnn, g_o_att, g_out, g_pg, g_ple, g_conv]
    res = _adamw_many([w[n][0] for n in rest], g_rest, [m[n][0] for n in rest], [v[n][0] for n in rest])
    for n, a in zip(rest, g_rest):
        grad[n] = a[None]
    for dst, vals in zip((delta, new_m, new_v), res):
        for n, a in zip(rest, vals):
            dst[n] = a[None]

    return (loss, grad_x.reshape(BL, S, D), *[grad[n] for n in names], *[delta[n] for n in names],
            *[new_m[n] for n in names], *[new_v[n] for n in names])
```

```python
import jax
import jax.numpy as jnp
from jax import lax
from jax.experimental import pallas as pl
from jax.experimental.pallas import tpu as pltpu

F32 = jnp.float32
BF16 = jnp.bfloat16

D = 1024
S = 2048
BL = 2
T = BL * S
NDEV = 8
NCHIP = 4
PLE = 256
DR = 1280
NRB = 10
RBW = 128
CONVW = 4
LRU_C = 8.0
HD = 128
NH = 4
PATTERNS = ((128, 1), (512, 4), (2048, 16))
NG = 3
ATT = NH * HD
GW = NG * ATT
NIN = 2 * DR + 3 * GW + ATT + 2 * D
OFF_ZR = DR
OFF_Q = 2 * DR
OFF_ZA = OFF_Q + 3 * GW
OFF_G = OFF_ZA + ATT
ROPE_THETA = 10000.0
EPS = 1e-6
SCALE = HD ** -0.5
NEG = -1e30
QB = 128
LANES = 128
CT = 512
NCT = NIN // CT
A_W = 2 * DR
C_W = ATT + 2 * D

LR, B1, B2, AEPS, WD, STEP = 0.001, 0.9, 0.999, 1e-08, 0.01, 10

NSHARD_IN = NIN // NDEV
REP_NAMES = ("w_rg_a", "w_rg_x", "norm_mix", "b_in", "conv_b", "b_rg_a", "b_rg_x", "lru_lambda", "q_norm",
             "k_norm", "norm_ple", "b_ple_gate")
REP_ROWS = (NRB * RBW, NRB * RBW, D // LANES, NIN // LANES, DR // LANES, DR // LANES, DR // LANES, DR // LANES,
            NG, NG, D // LANES, D // LANES)
REP_TOTAL_ROWS = sum(REP_ROWS)
REP_ROWS_DEV = 344
BIG_NAMES = ("w_in", "w_o_rnn", "w_o_att", "w_out", "w_ple_gate", "w_ple")

VMEM_BIG = 56 * 1024 * 1024
VMEM_MID = 40 * 1024 * 1024


def _cp(sem=None, vmem=None):
    return pltpu.CompilerParams(dimension_semantics=sem, vmem_limit_bytes=vmem)


def _hbm(*arrays):
    return [pltpu.with_memory_space_constraint(a, pltpu.HBM) for a in arrays]


def _dot(a, b):
    return jnp.dot(a, b, preferred_element_type=F32)


def _dot_nt(a, b):
    return lax.dot_general(a, b, (((1,), (1,)), ((), ())), preferred_element_type=F32)


def _dot_tn(a, b):
    return lax.dot_general(a, b, (((0,), (0,)), ((), ())), preferred_element_type=F32)


def _sigmoid(x):
    return jax.nn.sigmoid(x)


def _perm(j):
    jq = j - OFF_Q // CT
    inside = (j >= OFF_Q // CT) & (j < OFF_ZA // CT)
    return jnp.where(inside, OFF_Q // CT + (jq % 3) * 3 + jq // 3, j)


PIECES = ((0, A_W // CT), (OFF_Q // CT, GW // CT), (OFF_Q // CT + 3, GW // CT), (OFF_Q // CT + 6, GW // CT),
          (OFF_ZA // CT, C_W // CT))


def _rmsnorm_fwd(x, gain, token, tm=512):
    def body(x_ref, g_ref, _token, o_ref):
        xv = x_ref[...]
        var = jnp.mean(xv * xv, axis=-1, keepdims=True)
        o_ref[...] = (xv * lax.rsqrt(var + EPS) * g_ref[...]).astype(BF16)

    return pl.pallas_call(
        body, grid=(T // tm,), name="rmsnorm_fwd",
        in_specs=[pl.BlockSpec((tm, D), lambda i: (i, 0)), pl.BlockSpec((1, D), lambda i: (0, 0)),
                  pl.BlockSpec((8, LANES), lambda i: (0, 0))],
        out_specs=pl.BlockSpec((tm, D), lambda i: (i, 0)),
        out_shape=jax.ShapeDtypeStruct((T, D), BF16),
        compiler_params=_cp(("parallel",)),
    )(*_hbm(x, gain, token))


CHIP_COLS = NIN // NCHIP


def _in_proj_chips(hn, w_rows, bias, chips, proj, token, name, tm=1024):
    n = w_rows.shape[0]

    def body(chips_ref, a_ref, w_ref, b_ref, _token, *rest):
        o_ref = rest[-1]
        o_ref[...] = (_dot_nt(a_ref[...], w_ref[0]) + b_ref[...]).astype(BF16)

    in_specs = [pl.BlockSpec((tm, D), lambda s, i, ch: (i, 0)),
                pl.BlockSpec((1, CHIP_COLS, D), lambda s, i, ch: (s, 0, 0)),
                pl.BlockSpec((1, CHIP_COLS), lambda s, i, ch: (0, ch[s])),
                pl.BlockSpec((8, LANES), lambda s, i, ch: (0, 0))]
    args = [hn, w_rows, bias, token]
    aliases = {}
    if proj is not None:
        in_specs.append(pl.BlockSpec(memory_space=pl.ANY))
        args.append(proj)
        aliases = {5: 0}
    return pl.pallas_call(
        body, name=name,
        grid_spec=pltpu.PrefetchScalarGridSpec(
            num_scalar_prefetch=1, grid=(n, T // tm), in_specs=in_specs,
            out_specs=pl.BlockSpec((tm, CHIP_COLS), lambda s, i, ch: (i, ch[s]))),
        out_shape=jax.ShapeDtypeStruct((T, NIN), BF16),
        input_output_aliases=aliases,
        compiler_params=_cp(("arbitrary", "arbitrary"), VMEM_BIG),
    )(chips, *_hbm(*args))


def _dhn(pieces, w_bufs, chips, token, tm=512):
    nb = len(w_bufs)

    def body(chips_ref, a_ref, q_ref, k_ref, v_ref, c_ref, *rest):
        w_hbm = rest[:nb]
        o_ref, w = rest[nb + 1], rest[nb + 2]

        @pl.when(pl.program_id(0) == 0)
        def _():
            s = 0
            for buf in w_hbm:
                for r in range(buf.shape[0]):
                    row = pl.multiple_of(chips_ref[s] * CHIP_COLS, 128)
                    pltpu.sync_copy(buf.at[r], w.at[pl.ds(row, CHIP_COLS), :])
                    s += 1

        acc = _dot(a_ref[...], w[pl.ds(0, A_W), :])
        for kind, x_ref in enumerate((q_ref, k_ref, v_ref)):
            for g in range(NG):
                row = OFF_Q + (3 * g + kind) * CT
                acc = acc + _dot(x_ref[:, g * CT:(g + 1) * CT], w[pl.ds(row, CT), :])
        o_ref[...] = acc + _dot(c_ref[...], w[pl.ds(OFF_ZA, C_W), :])

    tok = lambda wd: pl.BlockSpec((tm, wd), lambda i, ch: (i, 0))
    return pl.pallas_call(
        body, name="dhn",
        grid_spec=pltpu.PrefetchScalarGridSpec(
            num_scalar_prefetch=1, grid=(T // tm,),
            in_specs=[tok(A_W), tok(GW), tok(GW), tok(GW), tok(C_W)] + [pl.BlockSpec(memory_space=pl.ANY)] * nb
            + [pl.BlockSpec((8, LANES), lambda i, ch: (0, 0))],
            out_specs=tok(D),
            scratch_shapes=[pltpu.VMEM((NIN, D), BF16)]),
        out_shape=jax.ShapeDtypeStruct((T, D), F32),
        compiler_params=_cp(("arbitrary",), VMEM_BIG),
    )(chips, *_hbm(*pieces, *w_bufs, token))


def _dw_in(pieces, hn):
    names = ("a", "q", "k", "v", "c")
    dw = db = None
    for piece, (lo, n), tag in zip(pieces, PIECES, names):
        def body(x_ref, h_ref, *rest):
            o_ref, s_ref = rest[-2:]
            xv = x_ref[...]
            o_ref[...] = _dot_tn(xv, h_ref[...]).astype(BF16)
            s_ref[...] = jnp.sum(xv.astype(F32), axis=0, keepdims=True)

        in_specs = [pl.BlockSpec((T, CT), lambda j: (0, j)), pl.BlockSpec((T, D), lambda j: (0, 0))]
        args = [piece, hn]
        aliases = {}
        if dw is not None:
            in_specs += [pl.BlockSpec(memory_space=pl.ANY)] * 2
            args += [dw, db]
            aliases = {2: 0, 3: 1}
        dw, db = pl.pallas_call(
            body, grid=(n,), name="dw_in_" + tag,
            in_specs=in_specs,
            out_specs=[pl.BlockSpec((CT, D), lambda j, lo=lo: (_perm(lo + j), 0)),
                       pl.BlockSpec((1, CT), lambda j, lo=lo: (0, _perm(lo + j)))],
            out_shape=[jax.ShapeDtypeStruct((NIN, D), BF16), jax.ShapeDtypeStruct((1, NIN), F32)],
            input_output_aliases=aliases,
            compiler_params=_cp(("parallel",), VMEM_MID),
        )(*_hbm(*args))
    return dw, db


def _mm_tn(a, b, ta, tt, name):
    m = a.shape[1]
    n = b.shape[1]
    nt = T // tt

    def body(a_ref, b_ref, o_ref, acc):
        t = pl.program_id(1)
        p = _dot_tn(a_ref[...].astype(BF16), b_ref[...].astype(BF16))

        @pl.when(t == 0)
        def _():
            acc[...] = p

        @pl.when(t > 0)
        def _():
            acc[...] += p

        @pl.when(t == nt - 1)
        def _():
            o_ref[...] = acc[...].astype(BF16)

    return pl.pallas_call(
        body, grid=(m // ta, nt), name=name,
        in_specs=[pl.BlockSpec((tt, ta), lambda j, t: (t, j)), pl.BlockSpec((tt, n), lambda j, t: (t, 0))],
        out_specs=pl.BlockSpec((ta, n), lambda j, t: (j, 0)),
        out_shape=jax.ShapeDtypeStruct((m, n), BF16),
        scratch_shapes=[pltpu.VMEM((ta, n), F32)],
        compiler_params=_cp(("parallel", "arbitrary"), VMEM_MID),
    )(*_hbm(a, b))


def _row_iota():
    return lax.broadcasted_iota(jnp.int32, (S, RBW), 0)


def _shift_down(v, d, row, fill):
    return jnp.where(row >= d, pltpu.roll(v, d, 0), fill)


def _shift_up(v, d, row, fill):
    return jnp.where(row < S - d, pltpu.roll(v, S - d, 0), fill)


SUBLANES = 8


def _scan_down(a, u, row):
    d = 1
    while d < S:
        last = 2 * d >= S
        if d < SUBLANES:
            u = a * _shift_down(u, d, row, 0.0) + u
            if not last:
                a = a * _shift_down(a, d, row, 1.0)
        else:
            u = jnp.concatenate([u[:d], a[d:] * u[:S - d] + u[d:]], axis=0)
            if not last:
                a = jnp.concatenate([a[:d], a[d:] * a[:S - d]], axis=0)
        d *= 2
    return u


def _scan_up(b, g, row):
    d = 1
    while d < S:
        last = 2 * d >= S
        if d < SUBLANES:
            g = g + b * _shift_up(g, d, row, 0.0)
            if not last:
                b = b * _shift_up(b, d, row, 0.0)
        else:
            g = jnp.concatenate([g[:S - d] + b[:S - d] * g[d:], g[S - d:]], axis=0)
            if not last:
                b = jnp.concatenate([b[:S - d] * b[d:], b[S - d:]], axis=0)
        d *= 2
    return g


def _softplus(x):
    return jnp.maximum(x, 0.0) + jnp.log1p(jnp.exp(-jnp.abs(x)))


def _rnn_gates(x, cw, cb, wa, ba, wx, bx, lam, row):
    xc = cb + cw[3:4, :] * x
    for j in (1, 2, 3):
        xc = xc + cw[3 - j:4 - j, :] * _shift_down(x, j, row, 0.0)
    xcb = xc.astype(BF16)
    r = _sigmoid(_dot(xcb, wa) + ba)
    i = _sigmoid(_dot(xcb, wx) + bx)
    sp = _softplus(-lam)
    log_a = (-LRU_C) * r * sp
    a = jnp.exp(log_a)
    mult = jnp.where(row == 0, 1.0, jnp.sqrt(jnp.tanh(-log_a) * (1.0 + a * a)))
    return xc, xcb, r, i, sp, a, mult


def _rnn_fwd(proj3, conv_w, conv_b, wa, ba, wx, bx, lam, token):
    def body(x_ref, cw_ref, cb_ref, wa_ref, ba_ref, wx_ref, bx_ref, lam_ref, _token, h_ref):
        row = _row_iota()
        x = x_ref[0].astype(F32)
        xc, _, _, i, _, a, mult = _rnn_gates(x, cw_ref[...], cb_ref[...], wa_ref[0], ba_ref[...],
                                             wx_ref[0], bx_ref[...], lam_ref[...], row)
        h_ref[0] = _scan_down(a, mult * (i * xc), row)

    vec = lambda: pl.BlockSpec((1, RBW), lambda b, n: (0, n))
    mat = lambda: pl.BlockSpec((1, RBW, RBW), lambda b, n: (n, 0, 0))
    return pl.pallas_call(
        body, grid=(BL, NRB), name="rnn_fwd",
        in_specs=[pl.BlockSpec((1, S, RBW), lambda b, n: (b, 0, n)),
                  pl.BlockSpec((CONVW, RBW), lambda b, n: (0, n)),
                  vec(), mat(), vec(), mat(), vec(), vec(), pl.BlockSpec((8, LANES), lambda b, n: (0, 0))],
        out_specs=pl.BlockSpec((1, S, RBW), lambda b, n: (b, 0, n)),
        out_shape=jax.ShapeDtypeStruct((BL, S, DR), F32),
        compiler_params=_cp(("parallel", "parallel"), VMEM_MID),
    )(*_hbm(proj3, conv_w, conv_b, wa, ba, wx, bx, lam, token))


def _rnn_bwd(proj3, h3, dh3, slab_a3, conv_w, conv_b, wa, ba, wx, bx, lam, token):
    def body(x_ref, h_ref, dh_ref, cw_ref, cb_ref, wa_ref, ba_ref, wx_ref, bx_ref, lam_ref, _alias, _token,
             dx_ref, dcw_ref, dcb_ref, dwa_ref, dba_ref, dwx_ref, dbx_ref, dlam_ref):
        row = _row_iota()
        x = x_ref[0].astype(F32)
        cw = cw_ref[...]
        wa_v = wa_ref[0]
        wx_v = wx_ref[0]
        lam_v = lam_ref[...]
        xc, xcb, r, i, sp, a, mult = _rnn_gates(x, cw, cb_ref[...], wa_v, ba_ref[...], wx_v, bx_ref[...], lam_v, row)
        h = h_ref[0]
        g = _scan_up(_shift_up(a, 1, row, 0.0), dh_ref[0], row)
        da = g * _shift_down(h, 1, row, 0.0)
        dmult = jnp.where(row == 0, 0.0, g * (i * xc))
        gm = g * mult
        di = gm * xc
        dxc = gm * i
        dlog_a = da * a - dmult * (a * a) / mult
        dr = dlog_a * ((-LRU_C) * sp)
        dsp = jnp.sum(dlog_a * ((-LRU_C) * r), axis=0, keepdims=True)
        dlam = dsp * (-_sigmoid(-lam_v))
        dpa = dr * r * (1.0 - r)
        dpx = di * i * (1.0 - i)
        dpab = dpa.astype(BF16)
        dpxb = dpx.astype(BF16)
        dwa = _dot_tn(xcb, dpab)
        dwx = _dot_tn(xcb, dpxb)
        dba = jnp.sum(dpa, axis=0, keepdims=True)
        dbx = jnp.sum(dpx, axis=0, keepdims=True)
        dxc = dxc + _dot_nt(dpab, wa_v) + _dot_nt(dpxb, wx_v)
        dcb = jnp.sum(dxc, axis=0, keepdims=True)
        dx = cw[3:4, :] * dxc
        dcw_rows = [None] * CONVW
        dcw_rows[3] = jnp.sum(dxc * x, axis=0, keepdims=True)
        for j in (1, 2, 3):
            dx = dx + cw[3 - j:4 - j, :] * _shift_up(dxc, j, row, 0.0)
            dcw_rows[3 - j] = jnp.sum(dxc * _shift_down(x, j, row, 0.0), axis=0, keepdims=True)
        dx_ref[0] = dx.astype(BF16)
        dcw = jnp.concatenate(dcw_rows, axis=0)
        first = pl.program_id(1) == 0

        @pl.when(first)
        def _():
            dcw_ref[...] = dcw
            dcb_ref[...] = dcb
            dwa_ref[0] = dwa
            dba_ref[...] = dba
            dwx_ref[0] = dwx
            dbx_ref[...] = dbx
            dlam_ref[...] = dlam

        @pl.when(jnp.logical_not(first))
        def _():
            dcw_ref[...] += dcw
            dcb_ref[...] += dcb
            dwa_ref[0] += dwa
            dba_ref[...] += dba
            dwx_ref[0] += dwx
            dbx_ref[...] += dbx
            dlam_ref[...] += dlam

    slab = lambda: pl.BlockSpec((1, S, RBW), lambda n, b: (b, 0, n))
    vec = lambda: pl.BlockSpec((1, RBW), lambda n, b: (0, n))
    mat = lambda: pl.BlockSpec((1, RBW, RBW), lambda n, b: (n, 0, 0))
    taps = lambda: pl.BlockSpec((CONVW, RBW), lambda n, b: (0, n))
    vshape = jax.ShapeDtypeStruct((1, DR), F32)
    mshape = jax.ShapeDtypeStruct((NRB, RBW, RBW), F32)
    return pl.pallas_call(
        body, grid=(NRB, BL), name="rnn_bwd",
        in_specs=[slab(), slab(), slab(), taps(), vec(), mat(), vec(), mat(), vec(), vec(),
                  pl.BlockSpec(memory_space=pl.ANY), pl.BlockSpec((8, LANES), lambda n, b: (0, 0))],
        out_specs=[slab(), taps(), vec(), mat(), vec(), mat(), vec(), vec()],
        out_shape=[jax.ShapeDtypeStruct((BL, S, A_W), BF16), jax.ShapeDtypeStruct((CONVW, DR), F32),
                   vshape, mshape, vshape, mshape, vshape, vshape],
        input_output_aliases={10: 0},
        compiler_params=_cp(("parallel", "arbitrary"), 48 * 1024 * 1024),
    )(*_hbm(proj3, h3, dh3, conv_w, conv_b, wa, ba, wx, bx, lam, slab_a3, token))


NQB = S // QB


def _rms_head(t, gain):
    rstd = lax.rsqrt(jnp.mean(t * t, axis=-1, keepdims=True) + EPS)
    return t * rstd * gain


def _rope(t, cs, sn):
    return t * cs + pltpu.roll(t, HD // 2, 1) * sn


def _rope_t(dy, cs, sn):
    return dy * cs - pltpu.roll(dy, HD // 2, 1) * sn


def _bdot_nt(a, b):
    return lax.dot_general(a, b, (((2,), (2,)), ((0,), (0,))), preferred_element_type=F32)


def _bdot(a, b):
    return lax.dot_general(a, b, (((2,), (1,)), ((0,), (0,))), preferred_element_type=F32)


def _bdot_tn(a, b):
    return lax.dot_general(a, b, (((1,), (1,)), ((0,), (0,))), preferred_element_type=F32)


STRIDE_MAX = 4


def _permute(buf, x, dil, dst, off=0):
    ln = S // dil
    if dil == 1:
        dst[pl.ds(off, S), :] = x.astype(dst.dtype)
        return
    buf[0] = x
    if dil <= STRIDE_MAX:
        for c in range(dil):
            dst[pl.ds(off + c * ln, ln), :] = buf.at[0][pl.ds(c, ln, stride=dil), :].astype(dst.dtype)
        return
    f, r = STRIDE_MAX, dil // STRIDE_MAX
    part = S // f
    for c1 in range(f):
        buf.at[1][pl.ds(c1 * part, part), :] = buf.at[0][pl.ds(c1, part, stride=f), :]
    for c1 in range(f):
        for c2 in range(r):
            dst[pl.ds(off + (c1 + f * c2) * ln, ln), :] = (
                buf.at[1][pl.ds(c1 * part + c2, ln, stride=r), :].astype(dst.dtype))


def _unpermute(buf, xp, dil, dst):
    ln = S // dil
    if dil == 1:
        dst[...] = xp
        return
    if dil <= STRIDE_MAX:
        for c in range(dil):
            dst[pl.ds(c, ln, stride=dil), :] = xp[c * ln:(c + 1) * ln]
        return
    f, r = STRIDE_MAX, dil // STRIDE_MAX
    part = S // f
    for c1 in range(f):
        for c2 in range(r):
            c = c1 + f * c2
            buf.at[1][pl.ds(c1 * part + c2, ln, stride=r), :] = xp[c * ln:(c + 1) * ln]
    for c1 in range(f):
        dst[pl.ds(c1, part, stride=f), :] = buf[1, pl.ds(c1 * part, part), :]


def _blocks3(ref, off=0):
    return ref[pl.ds(off, S), :].reshape(NQB, QB, HD)


def _att_prep(q_ref, k_ref, v_ref, cos_ref, sin_ref, qn, kn, dil, nat, qs, ksp, vsp):
    cs = cos_ref[...]
    sn = sin_ref[...]
    zero = jnp.zeros((QB, HD), BF16)
    ksp[pl.ds(0, QB), :] = zero
    vsp[pl.ds(0, QB), :] = zero
    _permute(nat, _rope(_rms_head(q_ref[0].astype(F32), qn), cs, sn), dil, qs)
    _permute(nat, _rope(_rms_head(k_ref[0].astype(F32), kn), cs, sn), dil, ksp, QB)
    _permute(nat, v_ref[0].astype(F32), dil, vsp, QB)


def _att_scores(qs, ksp, dil):
    nb = S // dil // QB
    q3 = _blocks3(qs)
    shape = (NQB, QB, QB)
    qi = lax.broadcasted_iota(jnp.int32, shape, 1)
    kj = lax.broadcasted_iota(jnp.int32, shape, 2)
    s_c = jnp.where(qi >= kj, _bdot_nt(q3, _blocks3(ksp, QB)) * SCALE, NEG)
    if nb == 1:
        return q3, s_c, None
    jj = lax.broadcasted_iota(jnp.int32, shape, 0)
    ok = (kj >= qi) & ((jj & (nb - 1)) != 0)
    s_p = jnp.where(ok, _bdot_nt(q3, _blocks3(ksp)) * SCALE, NEG)
    return q3, s_c, s_p


def _qkv_spec(kind, g):
    base = OFF_Q // HD + (3 * g + kind) * NH
    return pl.BlockSpec((1, S, HD), lambda b, h: (b, 0, base + h))


def _attn_fwd(proj3, cos_t, sin_t, q_norm, k_norm):
    def body(*refs):
        qkv_refs = refs[:9]
        cos_ref, sin_ref, qn_ref, kn_ref, att_ref, lse_ref, w_ref, nat, qs, ksp, vsp, og = refs[9:]
        for g, (window, dil) in enumerate(PATTERNS):
            q_ref, k_ref, v_ref = qkv_refs[3 * g:3 * g + 3]
            _att_prep(q_ref, k_ref, v_ref, cos_ref, sin_ref, qn_ref[g:g + 1, :], kn_ref[g:g + 1, :], dil,
                      nat, qs, ksp, vsp)
            _, s_c, s_p = _att_scores(qs, ksp, dil)
            m = jnp.max(s_c, axis=-1, keepdims=True)
            if s_p is not None:
                m = jnp.maximum(m, jnp.max(s_p, axis=-1, keepdims=True))
            e_c = jnp.exp(s_c - m)
            den = jnp.sum(e_c, axis=-1, keepdims=True)
            o = _bdot(e_c.astype(BF16), _blocks3(vsp, QB))
            if s_p is not None:
                e_p = jnp.exp(s_p - m)
                den = den + jnp.sum(e_p, axis=-1, keepdims=True)
                o = o + _bdot(e_p.astype(BF16), _blocks3(vsp))
            _unpermute(nat, (o / den).reshape(S, HD), dil, og.at[g])
            _unpermute(nat, jnp.broadcast_to(m + jnp.log(den), (NQB, QB, HD)).reshape(S, HD), dil,
                       lse_ref.at[g, 0])
        l0 = lse_ref[0, 0]
        l1 = lse_ref[1, 0]
        l2 = lse_ref[2, 0]
        mx = jnp.maximum(jnp.maximum(l0, l1), l2)
        e0 = jnp.exp(l0 - mx)
        e1 = jnp.exp(l1 - mx)
        e2 = jnp.exp(l2 - mx)
        inv = 1.0 / (e0 + e1 + e2)
        w0 = e0 * inv
        w1 = e1 * inv
        w2 = e2 * inv
        w_ref[0, 0] = w0
        w_ref[1, 0] = w1
        w_ref[2, 0] = w2
        att_ref[0] = w0 * og[0] + w1 * og[1] + w2 * og[2]

    in_specs = [_qkv_spec(kind, g) for g in range(NG) for kind in range(3)]
    in_specs += [pl.BlockSpec((S, HD), lambda b, h: (0, 0)), pl.BlockSpec((S, HD), lambda b, h: (0, 0)),
                 pl.BlockSpec((NG, HD), lambda b, h: (0, 0)), pl.BlockSpec((NG, HD), lambda b, h: (0, 0))]
    stat = lambda: pl.BlockSpec((NG, 1, S, HD), lambda b, h: (0, b, 0, h))
    return pl.pallas_call(
        body, grid=(BL, NH), name="attn_fwd",
        in_specs=in_specs,
        out_specs=[pl.BlockSpec((1, S, HD), lambda b, h: (b, 0, h)), stat(), stat()],
        out_shape=[jax.ShapeDtypeStruct((BL, S, ATT), F32),
                   jax.ShapeDtypeStruct((NG, BL, S, ATT), F32),
                   jax.ShapeDtypeStruct((NG, BL, S, ATT), F32)],
        scratch_shapes=[pltpu.VMEM((2, S, HD), F32), pltpu.VMEM((S, HD), BF16), pltpu.VMEM((S + QB, HD), BF16),
                        pltpu.VMEM((S + QB, HD), BF16), pltpu.VMEM((NG, S, HD), F32)],
        compiler_params=_cp(("parallel", "parallel"), VMEM_BIG),
    )(*_hbm(*([proj3] * 9), cos_t, sin_t, q_norm, k_norm))


def _attn_bwd_group(g, proj3, cos_t, sin_t, qn_g, kn_g, lse, wts, datt3, sbar3, slabs):
    dil = PATTERNS[g][1]
    n_alias = 0 if slabs is None else 3

    def norm_rope_bwd(dpost, raw, gain, cs, sn):
        dn = _rope_t(dpost, cs, sn)
        rstd = lax.rsqrt(jnp.mean(raw * raw, axis=-1, keepdims=True) + EPS)
        xh = raw * rstd
        dgain = jnp.sum(dn * xh, axis=0, keepdims=True)
        gd = dn * gain
        draw = rstd * (gd - xh * jnp.mean(gd * xh, axis=-1, keepdims=True))
        return draw, dgain

    def body(*refs):
        (q_ref, k_ref, v_ref, cos_ref, sin_ref, qn_ref, kn_ref, lse_ref, w_ref, datt_ref, sbar_ref) = refs[:11]
        (dq_ref, dk_ref, dv_ref, dqn_ref, dkn_ref, nat, qs, ksp, vsp, dos, cvp, lsp, acc) = refs[11 + n_alias:]
        qn = qn_ref[...]
        kn = kn_ref[...]
        cs = cos_ref[...]
        sn = sin_ref[...]
        _att_prep(q_ref, k_ref, v_ref, cos_ref, sin_ref, qn, kn, dil, nat, qs, ksp, vsp)
        wv = w_ref[0, 0]
        _permute(nat, wv * datt_ref[0], dil, dos)
        _permute(nat, wv * sbar_ref[0], dil, cvp)
        _permute(nat, lse_ref[0, 0], dil, lsp)
        q3, s_c, s_p = _att_scores(qs, ksp, dil)
        do3 = _blocks3(dos)
        lse3 = _blocks3(lsp)[:, :, 0:1]
        cv3 = _blocks3(cvp)[:, :, 0:1]
        p_c = jnp.exp(s_c - lse3)
        ds_c = (p_c * (_bdot_nt(do3, _blocks3(vsp, QB)) - cv3)).astype(BF16)
        dq = _bdot(ds_c, _blocks3(ksp, QB))
        acc[0] = _bdot_tn(ds_c, q3).reshape(S, HD)
        acc[1] = _bdot_tn(p_c.astype(BF16), do3).reshape(S, HD)
        if s_p is not None:
            p_p = jnp.exp(s_p - lse3)
            ds_p = (p_p * (_bdot_nt(do3, _blocks3(vsp)) - cv3)).astype(BF16)
            dq = dq + _bdot(ds_p, _blocks3(ksp))
            early = pl.ds(0, S - QB)
            acc[0, early, :] += _bdot_tn(ds_p, q3).reshape(S, HD)[QB:]
            acc[1, early, :] += _bdot_tn(p_p.astype(BF16), do3).reshape(S, HD)[QB:]
        _unpermute(nat, (dq * SCALE).reshape(S, HD), dil, nat.at[0])
        draw, dqn = norm_rope_bwd(nat[0], q_ref[0].astype(F32), qn, cs, sn)
        dq_ref[0] = draw.astype(BF16)
        _unpermute(nat, acc[0] * SCALE, dil, nat.at[0])
        draw, dkn = norm_rope_bwd(nat[0], k_ref[0].astype(F32), kn, cs, sn)
        dk_ref[0] = draw.astype(BF16)
        _unpermute(nat, acc[1], dil, nat.at[0])
        dv_ref[0] = nat[0].astype(BF16)
        first = (pl.program_id(0) == 0) & (pl.program_id(1) == 0)

        @pl.when(first)
        def _():
            dqn_ref[...] = dqn
            dkn_ref[...] = dkn

        @pl.when(jnp.logical_not(first))
        def _():
            dqn_ref[...] += dqn
            dkn_ref[...] += dkn

    full = lambda r: pl.BlockSpec((r, HD), lambda b, h: (0, 0))
    stat = lambda: pl.BlockSpec((1, 1, S, HD), lambda b, h: (g, b, 0, h))
    slab = lambda: pl.BlockSpec((1, S, HD), lambda b, h: (b, 0, h))
    out_slab = lambda: pl.BlockSpec((1, S, HD), lambda b, h: (b, 0, g * NH + h))
    big = jax.ShapeDtypeStruct((BL, S, GW), BF16)
    vecs = jax.ShapeDtypeStruct((1, HD), F32)
    in_specs = [_qkv_spec(0, g), _qkv_spec(1, g), _qkv_spec(2, g), full(S), full(S), full(1), full(1),
                stat(), stat(), slab(), slab()]
    args = [proj3, proj3, proj3, cos_t, sin_t, qn_g, kn_g, lse, wts, datt3, sbar3]
    aliases = {}
    if slabs is not None:
        in_specs += [pl.BlockSpec(memory_space=pl.ANY)] * 3
        args += list(slabs)
        aliases = {11: 0, 12: 1, 13: 2}
    return pl.pallas_call(
        body, grid=(BL, NH), name="attn_bwd_g%d" % g,
        in_specs=in_specs,
        out_specs=[out_slab(), out_slab(), out_slab(), full(1), full(1)],
        out_shape=[big, big, big, vecs, vecs],
        scratch_shapes=[pltpu.VMEM((2, S, HD), F32), pltpu.VMEM((S, HD), BF16), pltpu.VMEM((S + QB, HD), BF16),
                        pltpu.VMEM((S + QB, HD), BF16), pltpu.VMEM((S, HD), BF16), pltpu.VMEM((S, HD), F32),
                        pltpu.VMEM((S, HD), F32), pltpu.VMEM((2, S, HD), F32)],
        input_output_aliases=aliases,
        compiler_params=_cp(("arbitrary", "arbitrary"), VMEM_BIG),
    )(*_hbm(*args))


def _tail(x, proj, h, att, p, tgt, w_o_rnn, w_o_att_t, w_out, w_pg, w_ple_t, norm_ple, b_pg, tm=256):
    nt = T // tm
    inv_d = 1.0 / D

    def body(x_ref, h_ref, zr_ref, att_ref, za_ref, g0a_ref, g0b_ref, g1a_ref, g1b_ref, p_ref, tgt_ref,
             np_ref, bpg_ref, wor_hbm, woa_hbm, wout_hbm, wpg_hbm, wple_hbm,
             dx1_ref, merged_ref, n1_ref, dpre_ref, dpe_ref, dyr_ref, dya_ref, slab_a_ref, slab_c_ref, dh_ref,
             datt_ref, sbar_ref, yrnn_ref, yatt_ref, loss_ref, dnp_ref, dbpg_ref,
             wor, woa, wout, wpg, wple):
        first = pl.program_id(0) == 0

        @pl.when(first)
        def _():
            pltpu.sync_copy(wor_hbm, wor)
            pltpu.sync_copy(woa_hbm, woa)
            pltpu.sync_copy(wout_hbm, wout)
            pltpu.sync_copy(wpg_hbm, wpg)
            pltpu.sync_copy(wple_hbm, wple)

        xv = x_ref[...]
        hv = h_ref[...]
        zr = zr_ref[...].astype(F32)
        av = att_ref[...]
        za = za_ref[...].astype(F32)
        szr = _sigmoid(zr)
        silu_r = zr * szr
        yrnn_b = (hv * silu_r).astype(BF16)
        sza = _sigmoid(za)
        silu_a = za * sza
        yatt_b = (av * silu_a).astype(BF16)
        yrnn_ref[...] = yrnn_b
        yatt_ref[...] = yatt_b
        yr = _dot(yrnn_b, wor[...])
        ya = _dot_nt(yatt_b, woa[...])
        g0 = _sigmoid(jnp.concatenate([g0a_ref[...], g0b_ref[...]], axis=1).astype(F32))
        g1 = _sigmoid(jnp.concatenate([g1a_ref[...], g1b_ref[...]], axis=1).astype(F32))
        merged_b = (g0 * yr + g1 * ya).astype(BF16)
        merged_ref[...] = merged_b
        x1 = xv + _dot(merged_b, wout[...])
        rstd = lax.rsqrt(jnp.mean(x1 * x1, axis=-1, keepdims=True) + EPS)
        xh = x1 * rstd
        npl = np_ref[...]
        n1_b = (xh * npl).astype(BF16)
        n1_ref[...] = n1_b
        pg = _sigmoid(_dot(n1_b, wpg[...]) + bpg_ref[...])
        pe = _dot_nt(p_ref[...].astype(BF16), wple[...])
        err = x1 + pg * pe - tgt_ref[...]
        loss_t = 0.5 * inv_d * jnp.sum(err * err)
        dy = err * inv_d
        dpe_ref[...] = (dy * pg).astype(BF16)
        dpre = dy * pe * pg * (1.0 - pg)
        dpre_b = dpre.astype(BF16)
        dpre_ref[...] = dpre_b
        dn1 = _dot_nt(dpre_b, wpg[...])
        dnp = jnp.sum(dn1 * xh, axis=0, keepdims=True)
        dbpg = jnp.sum(dpre, axis=0, keepdims=True)
        gd = dn1 * npl
        dx1 = dy + rstd * (gd - xh * jnp.mean(gd * xh, axis=-1, keepdims=True))
        dx1_ref[...] = dx1
        dmerged = _dot_nt(dx1.astype(BF16), wout[...])
        dyr_b = (dmerged * g0).astype(BF16)
        dya_b = (dmerged * g1).astype(BF16)
        dyr_ref[...] = dyr_b
        dya_ref[...] = dya_b
        slab_c_ref[:, ATT:ATT + D] = (dmerged * yr * g0 * (1.0 - g0)).astype(BF16)
        slab_c_ref[:, ATT + D:ATT + 2 * D] = (dmerged * ya * g1 * (1.0 - g1)).astype(BF16)
        dyrnn = _dot_nt(dyr_b, wor[...])
        dyatt = _dot(dya_b, woa[...])
        dh_ref[...] = dyrnn * silu_r
        slab_a_ref[...] = (dyrnn * hv * szr * (1.0 + zr * (1.0 - szr))).astype(BF16)
        datt = dyatt * silu_a
        datt_ref[...] = datt
        slab_c_ref[:, 0:ATT] = (dyatt * av * sza * (1.0 + za * (1.0 - sza))).astype(BF16)
        da = datt * av
        for hh in range(NH):
            seg = slice(hh * HD, (hh + 1) * HD)
            sbar_ref[:, seg] = jnp.broadcast_to(jnp.sum(da[:, seg], axis=-1, keepdims=True), (tm, HD))

        @pl.when(first)
        def _():
            loss_ref[...] = jnp.full((8, LANES), loss_t, F32)
            dnp_ref[...] = dnp
            dbpg_ref[...] = dbpg

        @pl.when(jnp.logical_not(first))
        def _():
            loss_ref[...] += jnp.full((8, LANES), loss_t, F32)
            dnp_ref[...] += dnp
            dbpg_ref[...] += dbpg

    tok = lambda w: pl.BlockSpec((tm, w), lambda i: (i, 0))
    col = lambda w, blk: pl.BlockSpec((tm, w), lambda i: (i, blk))
    vec = lambda: pl.BlockSpec((1, D), lambda i: (0, 0))
    hbm = lambda: pl.BlockSpec(memory_space=pl.ANY)
    gb = OFF_G // 512
    in_specs = [tok(D), tok(DR), col(DR, 1), tok(ATT), col(ATT, OFF_ZA // ATT),
                col(512, gb), col(512, gb + 1), col(512, gb + 2), col(512, gb + 3),
                tok(PLE), tok(D), vec(), vec(), hbm(), hbm(), hbm(), hbm(), hbm()]
    sh = lambda w, dt: jax.ShapeDtypeStruct((T, w), dt)
    out_shape = [sh(D, F32), sh(D, BF16), sh(D, BF16), sh(D, BF16), sh(D, BF16), sh(D, BF16), sh(D, BF16),
                 sh(A_W, BF16), sh(C_W, BF16), sh(DR, F32), sh(ATT, F32), sh(ATT, F32),
                 sh(DR, BF16), sh(ATT, BF16),
                 jax.ShapeDtypeStruct((8, LANES), F32), jax.ShapeDtypeStruct((1, D), F32),
                 jax.ShapeDtypeStruct((1, D), F32)]
    out_specs = [tok(D), tok(D), tok(D), tok(D), tok(D), tok(D), tok(D), col(DR, 1), tok(C_W), tok(DR),
                 tok(ATT), tok(ATT), tok(DR), tok(ATT),
                 pl.BlockSpec((8, LANES), lambda i: (0, 0)), vec(), vec()]
    return pl.pallas_call(
        body, grid=(nt,), name="tail_fwd_bwd",
        in_specs=in_specs, out_specs=out_specs, out_shape=out_shape,
        scratch_shapes=[pltpu.VMEM((DR, D), BF16), pltpu.VMEM((D, ATT), BF16), pltpu.VMEM((D, D), BF16),
                        pltpu.VMEM((D, D), BF16), pltpu.VMEM((D, PLE), BF16)],
        compiler_params=_cp(("arbitrary",), VMEM_BIG),
    )(*_hbm(x, h, proj, att, proj, proj, proj, proj, proj, p, tgt, norm_ple, b_pg, w_o_rnn, w_o_att_t, w_out, w_pg,
            w_ple_t))


def _input_norm_bwd(x, dhn, dx1, gain, tm=512):
    def body(x_ref, dhn_ref, dx1_ref, g_ref, dx_ref, dg_ref):
        xv = x_ref[...]
        rstd = lax.rsqrt(jnp.mean(xv * xv, axis=-1, keepdims=True) + EPS)
        xh = xv * rstd
        dn = dhn_ref[...]
        dg = jnp.sum(dn * xh, axis=0, keepdims=True)
        gd = dn * g_ref[...]
        dx_ref[...] = dx1_ref[...] + rstd * (gd - xh * jnp.mean(gd * xh, axis=-1, keepdims=True))
        first = pl.program_id(0) == 0

        @pl.when(first)
        def _():
            dg_ref[...] = dg

        @pl.when(jnp.logical_not(first))
        def _():
            dg_ref[...] += dg

    tok = lambda: pl.BlockSpec((tm, D), lambda i: (i, 0))
    vec = lambda: pl.BlockSpec((1, D), lambda i: (0, 0))
    return pl.pallas_call(
        body, grid=(T // tm,), name="input_norm_bwd",
        in_specs=[tok(), tok(), tok(), vec()], out_specs=[tok(), vec()],
        out_shape=[jax.ShapeDtypeStruct((T, D), F32), jax.ShapeDtypeStruct((1, D), F32)],
        compiler_params=_cp(("arbitrary",), VMEM_MID),
    )(*_hbm(x, dhn, dx1, gain))


def _rope_tables():
    pos = jnp.arange(S, dtype=F32)
    inv_freq = ROPE_THETA ** (-jnp.arange(0, HD, 2, dtype=F32) / HD)
    ang = pos[:, None] * inv_freq[None, :]
    cos, sin = jnp.cos(ang), jnp.sin(ang)
    return jnp.concatenate([cos, cos], axis=1), jnp.concatenate([-sin, sin], axis=1)


def _local_step(x, p, tgt, project, other_weights, norm_mix, conv_b,
                w_rg_a, b_rg_a, w_rg_x, b_rg_x, lam, q_norm, k_norm, norm_ple, b_pg, start_reduce=None,
                entry_token=None):
    if start_reduce is None:
        start_reduce = lambda arrs, tag: (jnp.zeros((8, LANES), F32), arrs)
    if entry_token is None:
        entry_token = jnp.zeros((8, LANES), F32)
    cos_t, sin_t = _rope_tables()
    wa_b = w_rg_a.astype(BF16)
    wx_b = w_rg_x.astype(BF16)

    hn = _rmsnorm_fwd(x, norm_mix, entry_token)
    proj, w_bufs, chips, conv_w, token = project(hn)
    proj3 = proj.reshape(BL, S, NIN)
    h3 = _rnn_fwd(proj3, conv_w, conv_b, wa_b, b_rg_a, wx_b, b_rg_x, lam, token)
    att3, lse, wts = _attn_fwd(proj3, cos_t, sin_t, q_norm, k_norm)
    w_o_rnn, w_o_att_t, w_out, w_pg, w_ple_t = other_weights(att3)
    (dx1, merged, n1, dpre, dpe, dyr, dya, slab_a, slab_c, dh, datt, sbar, yrnn, yatt, loss8, dnp, dbpg) = _tail(
        x, proj, h3.reshape(T, DR), att3.reshape(T, ATT), p, tgt, w_o_rnn, w_o_att_t, w_out, w_pg, w_ple_t,
        norm_ple, b_pg)

    token, pending_out = start_reduce([
        _mm_tn(yrnn, dyr, 640, 2048, "dw_o_rnn"),
        _mm_tn(dya, yatt, 512, 2048, "dw_o_att_t"),
        _mm_tn(merged, dx1, 512, 2048, "dw_out"),
        _mm_tn(n1, dpre, 512, 2048, "dw_ple_gate"),
        _mm_tn(dpe, p, 512, 2048, "dw_ple_t")], "out")

    slab_a3, dcw, dcb, dwa, dba, dwx, dbx, dlam = _rnn_bwd(
        proj3, h3, dh.reshape(BL, S, DR), slab_a.reshape(BL, S, A_W), conv_w, conv_b, wa_b, b_rg_a, wx_b, b_rg_x, lam,
        token)
    datt3 = datt.reshape(BL, S, ATT)
    sbar3 = sbar.reshape(BL, S, ATT)
    slabs = None
    dqn = []
    dkn = []
    for g in range(NG):
        dq, dk, dv, dqn_g, dkn_g = _attn_bwd_group(g, proj3, cos_t, sin_t, q_norm[g:g + 1], k_norm[g:g + 1],
                                                   lse, wts, datt3, sbar3, slabs)
        slabs = (dq, dk, dv)
        dqn.append(dqn_g)
        dkn.append(dkn_g)
    pieces = [slab_a3.reshape(T, A_W)] + [t.reshape(T, GW) for t in slabs] + [slab_c]
    dw_in_t, db_in = _dw_in(pieces, hn)
    token, pending_in = start_reduce([dw_in_t], "in")
    dhn = _dhn(pieces, w_bufs, chips, token)
    grad_x, dnm = _input_norm_bwd(x, dhn, dx1, norm_mix)

    small = dict(w_rg_a=dwa, w_rg_x=dwx, norm_mix=dnm, b_in=db_in, conv_b=dcb, b_rg_a=dba, b_rg_x=dbx,
                 lru_lambda=dlam, q_norm=dqn, k_norm=dkn, norm_ple=dnp, b_ple_gate=dbpg, conv_w=dcw)
    return loss8[0, 0], grad_x, pending_out, pending_in, small


MESH = pl.DeviceIdType.MESH
HBM_SPEC = pl.BlockSpec(memory_space=pl.ANY)


def _my_pos():
    return lax.axis_index("x"), lax.axis_index("y"), lax.axis_index("c")


def _flip(pos, k):
    x, y, c = pos
    return (1 - x if k & 4 else x, 1 - y if k & 2 else y, 1 - c if k & 1 else c)


def _lin(pos):
    return 4 * pos[0] + 2 * pos[1] + pos[2]


def _chip(pos):
    return 2 * pos[0] + pos[1]


def _all_gather_two_level(shards, name):
    na = len(shards)

    def body(*refs):
        x_refs = refs[:na]
        out_refs = refs[na:2 * na]
        send_sems, recv_sems, local_sems = refs[2 * na:]
        me = _my_pos()
        sibling = _flip(me, 1)
        chips = [_flip(me, 4), _flip(me, 2), _flip(me, 6)]

        def copy(i, k, block, to, from_x=False):
            dst = out_refs[i].at[_lin(block)]
            return pltpu.make_async_remote_copy(
                src_ref=x_refs[i] if from_x else dst, dst_ref=dst,
                send_sem=send_sems.at[7 * i + k], recv_sem=recv_sems.at[7 * i + k], device_id=to, device_id_type=MESH)

        started = []
        for i in range(na):
            mine = pltpu.make_async_copy(x_refs[i], out_refs[i].at[_lin(me)], local_sems.at[i])
            mine.start()
            started.append(mine)
        sends = []
        for i in range(na):
            cps = [copy(i, 0, me, sibling, True)] + [copy(i, 1 + j, me, chip, True) for j, chip in enumerate(chips)]
            for cp in cps:
                cp.start()
            sends += cps
        for i in range(na):
            for j, chip in enumerate(chips):
                copy(i, 1 + j, chip, me).wait_recv()
                fwd = copy(i, 4 + j, chip, sibling)
                fwd.start()
                sends.append(fwd)
        for i in range(na):
            copy(i, 0, sibling, me).wait_recv()
            for j, chip in enumerate(chips):
                copy(i, 4 + j, _flip(chip, 1), me).wait_recv()
        for cp in sends:
            cp.wait_send()
        for mine in started:
            mine.wait()

    return pl.pallas_call(
        body, name=name,
        out_shape=[jax.ShapeDtypeStruct((NDEV,) + s.shape, s.dtype) for s in shards],
        in_specs=[HBM_SPEC] * na, out_specs=[HBM_SPEC] * na,
        scratch_shapes=[pltpu.SemaphoreType.DMA((7 * na,)), pltpu.SemaphoreType.DMA((7 * na,)),
                        pltpu.SemaphoreType.DMA((na,))],
    )(*shards)


def _all_gather_direct(shard, name):
    def body(x_ref, out_ref, send_sems, recv_sems, local_sem):
        me = _my_pos()
        mine = pltpu.make_async_copy(x_ref, out_ref.at[_lin(me)], local_sem)
        mine.start()
        sends = []
        for k in range(1, NDEV):
            cp = pltpu.make_async_remote_copy(
                src_ref=x_ref, dst_ref=out_ref.at[_lin(me)], send_sem=send_sems.at[k - 1],
                recv_sem=recv_sems.at[k - 1], device_id=_flip(me, k), device_id_type=MESH)
            cp.start()
            sends.append(cp)
        for k in range(1, NDEV):
            peer = _flip(me, k)
            pltpu.make_async_remote_copy(
                src_ref=x_ref, dst_ref=out_ref.at[_lin(peer)], send_sem=send_sems.at[k - 1],
                recv_sem=recv_sems.at[k - 1], device_id=peer, device_id_type=MESH).wait_recv()
        for cp in sends:
            cp.wait_send()
        mine.wait()

    return pl.pallas_call(
        body, name=name,
        out_shape=jax.ShapeDtypeStruct((NDEV,) + shard.shape, shard.dtype),
        in_specs=[HBM_SPEC], out_specs=HBM_SPEC,
        scratch_shapes=[pltpu.SemaphoreType.DMA((7,)), pltpu.SemaphoreType.DMA((7,)), pltpu.SemaphoreType.DMA],
    )(shard)


def _exchange_within_chip(parts, name):
    na = len(parts)

    def body(*refs):
        a_refs = refs[:na]
        recv_refs = refs[na:2 * na]
        send_sems, recv_sems = refs[2 * na:]
        me = _my_pos()
        c = me[2]
        sibling = _flip(me, 1)
        remote = []
        for i in range(na):
            for q in range(NCHIP):
                rc = pltpu.make_async_remote_copy(
                    src_ref=a_refs[i].at[q, 1 - c], dst_ref=recv_refs[i].at[q],
                    send_sem=send_sems.at[NCHIP * i + q], recv_sem=recv_sems.at[NCHIP * i + q],
                    device_id=sibling, device_id_type=MESH)
                rc.start()
                remote.append(rc)
        for rc in remote:
            rc.wait_recv()
        for rc in remote:
            rc.wait_send()

    return pl.pallas_call(
        body, name=name, out_shape=[jax.ShapeDtypeStruct((NCHIP,) + a.shape[2:], a.dtype) for a in parts],
        in_specs=[HBM_SPEC] * na, out_specs=[HBM_SPEC] * na,
        scratch_shapes=[pltpu.SemaphoreType.DMA((NCHIP * na,)), pltpu.SemaphoreType.DMA((NCHIP * na,))],
    )(*parts)


HBM_ONLY = pl.BlockSpec(memory_space=pltpu.HBM)
SEM_SPEC = pl.BlockSpec(memory_space=pltpu.SEMAPHORE)
SPLIT_COPY = pltpu.CompilerParams(has_side_effects=pltpu.SideEffectType.DATAFLOW_SIDE_EFFECTING)


def _chip_peers(me):
    return [_flip(me, 4), _flip(me, 2), _flip(me, 6)]


def _between_chips_start(parts, name):
    na = len(parts)

    def body(*refs):
        a_refs = refs[:na]
        land_refs = refs[na:2 * na]
        send_sems, recv_sems = refs[2 * na], refs[2 * na + 1]
        token = refs[-1]
        me = _my_pos()
        myq = _chip(me)
        for i in range(na):
            for j, peer in enumerate(_chip_peers(me)):
                pltpu.make_async_remote_copy(
                    src_ref=a_refs[i].at[_chip(peer)], dst_ref=land_refs[i].at[myq],
                    send_sem=send_sems.at[3 * i + j], recv_sem=recv_sems.at[3 * i + j],
                    device_id=peer, device_id_type=MESH).start()
        token[...] = jnp.zeros_like(token)

    hbm = [pltpu.HBM(a.shape, a.dtype) for a in parts]
    srcs = [pltpu.with_memory_space_constraint(a, pltpu.HBM) for a in parts]
    lands = [pltpu.with_memory_space_constraint(lax.empty(a.shape, a.dtype), pltpu.HBM) for a in parts]
    res = pl.pallas_call(
        body, name=name,
        out_shape=(pltpu.SemaphoreType.DMA((3 * na,)), pltpu.SemaphoreType.DMA((3 * na,)), *hbm, *hbm,
                   jax.ShapeDtypeStruct((8, LANES), F32)),
        in_specs=[HBM_ONLY] * (2 * na),
        out_specs=(SEM_SPEC, SEM_SPEC, *([HBM_ONLY] * (2 * na)), pl.BlockSpec(memory_space=pltpu.VMEM)),
        input_output_aliases={i: 2 + i for i in range(2 * na)},
        compiler_params=SPLIT_COPY,
    )(*srcs, *lands)
    return res[-1], (res[0], res[1], list(res[2:2 + na]), list(res[2 + na:2 + 2 * na]))


def _between_chips_wait(pending, after, name):
    send_sems, recv_sems, parts, lands = pending
    na = len(parts)

    def body(*refs):
        a_refs = refs[:na]
        land_refs = refs[na:2 * na]
        send_sems, recv_sems = refs[2 * na], refs[2 * na + 1]
        me = _my_pos()
        for i in range(na):
            for j, peer in enumerate(_chip_peers(me)):
                cp = pltpu.make_async_remote_copy(
                    src_ref=a_refs[i].at[_chip(peer)], dst_ref=land_refs[i].at[_chip(peer)],
                    send_sem=send_sems.at[3 * i + j], recv_sem=recv_sems.at[3 * i + j],
                    device_id=peer, device_id_type=MESH)
                cp.wait_send()
                cp.wait_recv()

    hbm = [pltpu.HBM(a.shape, a.dtype) for a in parts]
    res = pl.pallas_call(
        body, name=name, out_shape=(*hbm, *hbm),
        in_specs=[HBM_ONLY] * (2 * na) + [SEM_SPEC, SEM_SPEC, pl.BlockSpec(memory_space=pl.ANY)],
        out_specs=[HBM_ONLY] * (2 * na),
        input_output_aliases={i: i for i in range(2 * na)},
        compiler_params=SPLIT_COPY,
    )(*parts, *lands, send_sems, recv_sems, after)
    return list(res[:na]), list(res[na:])


def _remote(src, dst, send_sems, recv_sems, idx, peer):
    return pltpu.make_async_remote_copy(src_ref=src, dst_ref=dst, send_sem=send_sems.at[idx],
                                        recv_sem=recv_sems.at[idx], device_id=peer, device_id_type=MESH)


def _copies_own(bufs, me):
    return [(bufs[0], bufs[1].at[me[2]], 0, _flip(me, 1))]


def _copies_near(bufs, me):
    return [(bufs[0], bufs[1].at[0, me[2]], 0, _flip(me, 2)), (bufs[0], bufs[1].at[1, me[2]], 1, _flip(me, 4))]


def _copies_far(bufs, me):
    return [(bufs[0], bufs[1].at[me[2]], 0, _flip(me, 6))]


def _copies_others(bufs, me):
    na = len(bufs) // 2
    return [(bufs[i], bufs[na + i].at[_lin(me)], 7 * i + k - 1, _flip(me, k))
            for i in range(na) for k in range(1, NDEV)]


GROUP_COPIES = dict(own=_copies_own, near=_copies_near, far=_copies_far, others=_copies_others)
GROUP_COUNT = dict(own=1, near=2, far=1)


def _gather_start(bufs, groups, after, name):
    nb = len(bufs)
    ng = len(groups)

    def body(*refs):
        b = refs[:nb]
        sems = refs[nb + 1:nb + 1 + 2 * ng]
        token = refs[-1]
        me = _my_pos()
        for gi, (group, idx) in enumerate(groups):
            for src, dst, k, peer in GROUP_COPIES[group]([b[i] for i in idx], me):
                _remote(src, dst, sems[2 * gi], sems[2 * gi + 1], k, peer).start()
        token[...] = jnp.zeros_like(token)

    sem_t = []
    for group, idx in groups:
        cnt = 7 * (len(idx) // 2) if group == "others" else GROUP_COUNT[group]
        sem_t += [pltpu.SemaphoreType.DMA((cnt,)), pltpu.SemaphoreType.DMA((cnt,))]
    ins = [pltpu.with_memory_space_constraint(a, pltpu.HBM) for a in bufs]
    res = pl.pallas_call(
        body, name=name,
        out_shape=(*sem_t, *[pltpu.HBM(a.shape, a.dtype) for a in bufs], jax.ShapeDtypeStruct((8, LANES), F32)),
        in_specs=[HBM_ONLY] * nb + [pl.BlockSpec(memory_space=pl.ANY)],
        out_specs=(*([SEM_SPEC] * (2 * ng)), *([HBM_ONLY] * nb), pl.BlockSpec(memory_space=pltpu.VMEM)),
        input_output_aliases={i: 2 * ng + i for i in range(nb)},
        compiler_params=SPLIT_COPY,
    )(*ins, after)
    return res[-1], list(res[2 * ng:2 * ng + nb]), [(res[2 * gi], res[2 * gi + 1]) for gi in range(ng)]


def _gather_wait(group, send_sems, recv_sems, bufs, after, name):
    nb = len(bufs)
    copies = dict(own=_copies_own, near=_copies_near, far=_copies_far, others=_copies_others)[group]

    def body(*refs):
        b = refs[:nb]
        ss, rs = refs[nb], refs[nb + 1]
        me = _my_pos()
        for src, dst, idx, peer in copies(b, me):
            if group == "others":
                landed = b[nb // 2 + idx // 7].at[_lin(peer)]
            elif group == "own":
                landed = b[1].at[1 - me[2]]
            else:
                landed = dst
            cp = _remote(src, landed, ss, rs, idx, peer)
            cp.wait_send()
            cp.wait_recv()

    res = pl.pallas_call(
        body, name=name, out_shape=[pltpu.HBM(a.shape, a.dtype) for a in bufs],
        in_specs=[HBM_ONLY] * nb + [SEM_SPEC, SEM_SPEC, pl.BlockSpec(memory_space=pl.ANY)],
        out_specs=[HBM_ONLY] * nb,
        input_output_aliases={i: i for i in range(nb)},
        compiler_params=SPLIT_COPY,
    )(*bufs, send_sems, recv_sems, after)
    return list(res)


def _forward_to_sibling(buf, name):
    n = buf.shape[0]

    def body(_in_ref, out_ref, send_sems, recv_sems):
        me = _my_pos()
        c = me[2]
        sibling = _flip(me, 1)
        sends = []
        for r in range(n):
            cp = _remote(out_ref.at[r, c], out_ref.at[r, c], send_sems, recv_sems, r, sibling)
            cp.start()
            sends.append(cp)
        for r in range(n):
            _remote(out_ref.at[r, c], out_ref.at[r, 1 - c], send_sems, recv_sems, r, sibling).wait_recv()
        for cp in sends:
            cp.wait_send()

    return pl.pallas_call(
        body, name=name, out_shape=jax.ShapeDtypeStruct(buf.shape, buf.dtype),
        in_specs=[HBM_SPEC], out_specs=HBM_SPEC,
        scratch_shapes=[pltpu.SemaphoreType.DMA((n,)), pltpu.SemaphoreType.DMA((n,))],
        input_output_aliases={0: 0},
    )(buf)


def _exchange_all(parts, name):
    na = len(parts)

    def body(*refs):
        a_refs = refs[:na]
        out_refs = refs[na:2 * na]
        send_sems, recv_sems = refs[2 * na:]
        me = _my_pos()
        sends = []
        for i in range(na):
            for k in range(1, NDEV):
                peer = _flip(me, k)
                cp = pltpu.make_async_remote_copy(
                    src_ref=a_refs[i].at[_lin(peer)], dst_ref=out_refs[i].at[_lin(me)],
                    send_sem=send_sems.at[7 * i + k - 1], recv_sem=recv_sems.at[7 * i + k - 1],
                    device_id=peer, device_id_type=MESH)
                cp.start()
                sends.append(cp)
        for i in range(na):
            for k in range(1, NDEV):
                peer = _flip(me, k)
                pltpu.make_async_remote_copy(
                    src_ref=a_refs[i].at[_lin(peer)], dst_ref=out_refs[i].at[_lin(peer)],
                    send_sem=send_sems.at[7 * i + k - 1], recv_sem=recv_sems.at[7 * i + k - 1],
                    device_id=peer, device_id_type=MESH).wait_recv()
        for cp in sends:
            cp.wait_send()

    return pl.pallas_call(
        body, name=name, out_shape=[jax.ShapeDtypeStruct(a.shape, a.dtype) for a in parts],
        in_specs=[HBM_SPEC] * na, out_specs=[HBM_SPEC] * na,
        scratch_shapes=[pltpu.SemaphoreType.DMA((7 * na,)), pltpu.SemaphoreType.DMA((7 * na,))],
    )(*parts)


def _scalar(v):
    return jnp.asarray(v, jnp.int32).reshape(1)


def _sum_pairs(parts, theirs, name):
    na = len(parts)

    def body(c_ref, *refs):
        for i in range(na):
            o_ref = refs[2 * na + i]
            o_ref[0] = (refs[i][0, 0].astype(F32) + refs[na + i][0].astype(F32)).astype(o_ref.dtype)

    def mine_spec(a):
        return pl.BlockSpec((1, 1) + a.shape[2:], lambda q, c_ref: (q, c_ref[0], 0, 0))

    def spec(a):
        return pl.BlockSpec((1,) + a.shape[1:], lambda q, c_ref: (q, 0, 0))

    return pl.pallas_call(
        body, name=name,
        grid_spec=pltpu.PrefetchScalarGridSpec(
            num_scalar_prefetch=1, grid=(NCHIP,),
            in_specs=[mine_spec(a) for a in parts] + [spec(a) for a in theirs],
            out_specs=[spec(a) for a in theirs]),
        out_shape=[jax.ShapeDtypeStruct(a.shape, a.dtype) for a in theirs],
        compiler_params=_cp(("arbitrary",), VMEM_BIG),
    )(_scalar(lax.axis_index("c")), *parts, *theirs)


def _others(q, mine, nblk=NCHIP):
    return jnp.where(q == mine, (q + 1) % nblk, q)


def _sum_chips_adamw(own, recv, wv, mv, vv, tr, name):
    _, r, w = recv.shape

    def body(q_ref, own_ref, r0, r1, r2, r3, w_ref, m_ref, v_ref, g_ref, d_ref, m2_ref, v2_ref):
        myq = q_ref[0]
        acc = None
        for q, r_ref in enumerate((r0, r1, r2, r3)):
            term = jnp.where(myq == q, own_ref[0], r_ref[0]).astype(F32)
            acc = term if acc is None else acc + term
        g_ref[...] = acc
        delta, m2, v2 = _adam_math(w_ref[...], acc, m_ref[...], v_ref[...])
        d_ref[...] = delta
        m2_ref[...] = m2
        v2_ref[...] = v2

    def recv_spec(q):
        return pl.BlockSpec((1, tr, w), lambda i, q_ref: (_others(q, q_ref[0]), i, 0))

    rows = lambda: pl.BlockSpec((tr, w), lambda i, q_ref: (i, 0))
    shp = jax.ShapeDtypeStruct((r, w), F32)
    return pl.pallas_call(
        body, name=name,
        grid_spec=pltpu.PrefetchScalarGridSpec(
            num_scalar_prefetch=1, grid=(r // tr,),
            in_specs=[pl.BlockSpec((1, tr, w), lambda i, q_ref: (q_ref[0], i, 0))]
            + [recv_spec(q) for q in range(NCHIP)] + [rows(), rows(), rows()],
            out_specs=[rows(), rows(), rows(), rows()]),
        out_shape=[shp, shp, shp, shp],
        compiler_params=_cp(("arbitrary",), VMEM_MID),
    )(_scalar(_chip(_my_pos())), *_hbm(own, recv, recv, recv, recv, wv, mv, vv))


def _sum_blocks_small(own, recv, mine, transpose, name):
    na = len(recv)
    nblk = recv[0].shape[0]

    def body(q_ref, *refs):
        me = q_ref[0]
        for i in range(na):
            acc = None
            for q in range(nblk):
                term = jnp.where(me == q, refs[i][0], refs[na * (1 + q) + i][0]).astype(F32)
                acc = term if acc is None else acc + term
            refs[na * (1 + nblk) + i][...] = acc.T if transpose[i] else acc

    def oshape(a, tr):
        r, w = a.shape[1:]
        return (w, r) if tr else (r, w)

    own_spec = lambda a: pl.BlockSpec((1,) + a.shape[1:], lambda s, q_ref: (q_ref[0], 0, 0))
    recv_spec = lambda a, q: pl.BlockSpec((1,) + a.shape[1:], lambda s, q_ref: (_others(q, q_ref[0], nblk), 0, 0))
    out_spec = lambda shp: pl.BlockSpec(shp, lambda s, q_ref: (0, 0))
    in_specs = [own_spec(a) for a in own]
    for q in range(nblk):
        in_specs += [recv_spec(a, q) for a in recv]
    return pl.pallas_call(
        body, name=name,
        grid_spec=pltpu.PrefetchScalarGridSpec(
            num_scalar_prefetch=1, grid=(1,), in_specs=in_specs,
            out_specs=[out_spec(oshape(a, tr)) for a, tr in zip(recv, transpose)]),
        out_shape=[jax.ShapeDtypeStruct(oshape(a, tr), F32) for a, tr in zip(recv, transpose)],
        compiler_params=_cp(("arbitrary",), VMEM_MID),
    )(_scalar(mine), *own, *(list(recv) * nblk))


def _rep_offsets():
    offs = []
    o = 0
    for r in REP_ROWS:
        offs.append(o)
        o += r
    return offs


def _pack_small_grads(g):
    offs = _rep_offsets()

    def body(dwa, dwx, dnm, dbin, dcb, dba, dbx, dlam, dq0, dq1, dq2, dk0, dk1, dk2, dnp, dbpg, o_ref):
        o_ref[pl.ds(REP_TOTAL_ROWS - 2, NDEV * REP_ROWS_DEV - REP_TOTAL_ROWS + 2), :] = jnp.zeros(
            (NDEV * REP_ROWS_DEV - REP_TOTAL_ROWS + 2, LANES), F32)
        for n in range(NRB):
            o_ref[pl.ds(offs[0] + n * RBW, RBW), :] = dwa[n]
            o_ref[pl.ds(offs[1] + n * RBW, RBW), :] = dwx[n]

        def put_vec(off, ref, rows):
            for k in range(rows):
                o_ref[pl.ds(off + k, 1), :] = ref[:, k * LANES:(k + 1) * LANES]

        put_vec(offs[2], dnm, REP_ROWS[2])
        put_vec(offs[3], dbin, REP_ROWS[3])
        put_vec(offs[4], dcb, REP_ROWS[4])
        put_vec(offs[5], dba, REP_ROWS[5])
        put_vec(offs[6], dbx, REP_ROWS[6])
        put_vec(offs[7], dlam, REP_ROWS[7])
        for k, ref in enumerate((dq0, dq1, dq2)):
            o_ref[pl.ds(offs[8] + k, 1), :] = ref[...]
        for k, ref in enumerate((dk0, dk1, dk2)):
            o_ref[pl.ds(offs[9] + k, 1), :] = ref[...]
        put_vec(offs[10], dnp, REP_ROWS[10])
        put_vec(offs[11], dbpg, REP_ROWS[11])

    args = [g["w_rg_a"], g["w_rg_x"], g["norm_mix"], g["b_in"], g["conv_b"], g["b_rg_a"], g["b_rg_x"],
            g["lru_lambda"], *g["q_norm"], *g["k_norm"], g["norm_ple"], g["b_ple_gate"]]
    full = lambda shp: pl.BlockSpec(shp, lambda: (0,) * len(shp))
    return pl.pallas_call(
        body, name="pack_small_grads",
        in_specs=[full(a.shape) for a in args],
        out_specs=full((NDEV * REP_ROWS_DEV, LANES)),
        out_shape=jax.ShapeDtypeStruct((NDEV * REP_ROWS_DEV, LANES), F32),
    )(*_hbm(*args))


def _adam_math(wv, gv, mv, vv):
    c1 = 1.0 - B1 ** STEP
    c2 = 1.0 - B2 ** STEP
    m2 = B1 * mv + (1.0 - B1) * gv
    v2 = B2 * vv + (1.0 - B2) * (gv * gv)
    delta = (-LR) * ((m2 / c1) / (jnp.sqrt(v2 / c2) + AEPS) + WD * wv)
    return delta, m2, v2


def _adamw_small(rep_flat, w, m, v):
    offs = _rep_offsets()
    n = len(REP_NAMES)

    def body(*refs):
        g_ref = refs[0]
        w_refs = refs[1:1 + n]
        m_refs = refs[1 + n:1 + 2 * n]
        v_refs = refs[1 + 2 * n:1 + 3 * n]
        outs = refs[1 + 3 * n:]
        go, do, mo, vo = outs[:n], outs[n:2 * n], outs[2 * n:3 * n], outs[3 * n:]

        def emit(i, idx, gv):
            go[i][idx] = gv
            delta, m2, v2 = _adam_math(w_refs[i][idx], gv, m_refs[i][idx], v_refs[i][idx])
            do[i][idx] = delta
            mo[i][idx] = m2
            vo[i][idx] = v2

        for i in range(n):
            if i < 2:
                for b in range(NRB):
                    emit(i, b, g_ref[pl.ds(offs[i] + b * RBW, RBW), :])
            elif REP_NAMES[i] in ("q_norm", "k_norm"):
                emit(i, slice(None), g_ref[pl.ds(offs[i], NG), :])
            else:
                gv = jnp.concatenate([g_ref[pl.ds(offs[i] + k, 1), :] for k in range(REP_ROWS[i])], axis=1)
                emit(i, slice(None), gv)

    full = lambda shp: pl.BlockSpec(shp, lambda: (0,) * len(shp))
    pspecs = [full(a.shape) for a in w]
    pshapes = [jax.ShapeDtypeStruct(a.shape, F32) for a in w]
    res = pl.pallas_call(
        body, name="adamw_small",
        in_specs=[full(rep_flat.shape)] + pspecs * 3,
        out_specs=pspecs * 4, out_shape=pshapes * 4,
        compiler_params=_cp(None, VMEM_MID),
    )(*_hbm(rep_flat, *w, *m, *v))
    return res[:n], res[n:2 * n], res[2 * n:3 * n], res[3 * n:]


def _adamw_many(w, g, m, v):
    n = len(w)

    def body(*refs):
        for i in range(n):
            delta, m2, v2 = _adam_math(refs[i][...], refs[n + i][...], refs[2 * n + i][...], refs[3 * n + i][...])
            refs[4 * n + i][...] = delta
            refs[5 * n + i][...] = m2
            refs[6 * n + i][...] = v2

    full = lambda shp: pl.BlockSpec(shp, lambda: (0,) * len(shp))
    specs = [full(a.shape) for a in w]
    shapes = [jax.ShapeDtypeStruct(a.shape, F32) for a in w]
    res = pl.pallas_call(
        body, name="adamw_shards",
        in_specs=specs * 4, out_specs=specs * 3, out_shape=shapes * 3,
        compiler_params=_cp(None, VMEM_MID),
    )(*_hbm(*w, *g, *m, *v))
    return res[:n], res[n:2 * n], res[2 * n:]


def kernel(x, p, norm_mix, w_in, b_in, conv_w, conv_b, w_rg_a, b_rg_a, w_rg_x, b_rg_x, lru_lambda, q_norm, k_norm, w_o_rnn, w_o_att, w_out, norm_ple, w_ple_gate, b_ple_gate, w_ple, loss_target, m_norm_mix, m_w_in, m_b_in, m_conv_w, m_conv_b, m_w_rg_a, m_b_rg_a, m_w_rg_x, m_b_rg_x, m_lru_lambda, m_q_norm, m_k_norm, m_w_o_rnn, m_w_o_att, m_w_out, m_norm_ple, m_w_ple_gate, m_b_ple_gate, m_w_ple, v_norm_mix, v_w_in, v_b_in, v_conv_w, v_conv_b, v_w_rg_a, v_b_rg_a, v_w_rg_x, v_b_rg_x, v_lru_lambda, v_q_norm, v_k_norm, v_w_o_rnn, v_w_o_att, v_w_out, v_norm_ple, v_w_ple_gate, v_b_ple_gate, v_w_ple):
    w = dict(norm_mix=norm_mix, w_in=w_in, b_in=b_in, conv_w=conv_w, conv_b=conv_b, w_rg_a=w_rg_a, b_rg_a=b_rg_a,
             w_rg_x=w_rg_x, b_rg_x=b_rg_x, lru_lambda=lru_lambda, q_norm=q_norm, k_norm=k_norm, w_o_rnn=w_o_rnn,
             w_o_att=w_o_att, w_out=w_out, norm_ple=norm_ple, w_ple_gate=w_ple_gate, b_ple_gate=b_ple_gate,
             w_ple=w_ple)
    m = dict(norm_mix=m_norm_mix, w_in=m_w_in, b_in=m_b_in, conv_w=m_conv_w, conv_b=m_conv_b, w_rg_a=m_w_rg_a,
             b_rg_a=m_b_rg_a, w_rg_x=m_w_rg_x, b_rg_x=m_b_rg_x, lru_lambda=m_lru_lambda, q_norm=m_q_norm,
             k_norm=m_k_norm, w_o_rnn=m_w_o_rnn, w_o_att=m_w_o_att, w_out=m_w_out, norm_ple=m_norm_ple,
             w_ple_gate=m_w_ple_gate, b_ple_gate=m_b_ple_gate, w_ple=m_w_ple)
    v = dict(norm_mix=v_norm_mix, w_in=v_w_in, b_in=v_b_in, conv_w=v_conv_w, conv_b=v_conv_b, w_rg_a=v_w_rg_a,
             b_rg_a=v_b_rg_a, w_rg_x=v_w_rg_x, b_rg_x=v_b_rg_x, lru_lambda=v_lru_lambda, q_norm=v_q_norm,
             k_norm=v_k_norm, w_o_rnn=v_w_o_rnn, w_o_att=v_w_o_att, w_out=v_w_out, norm_ple=v_norm_ple,
             w_ple_gate=v_w_ple_gate, b_ple_gate=v_b_ple_gate, w_ple=v_w_ple)
    names = list(w.keys())

    shards = [w_in[0].T.astype(BF16), w_o_rnn[0].astype(BF16), w_o_att[0].T.astype(BF16), w_out[0].astype(BF16),
              w_ple_gate[0].astype(BF16), w_ple[0].T.astype(BF16), conv_w[0]]
    pos = _my_pos()
    me, my_core, my_chip = _lin(pos), pos[2], _chip(pos)
    hbm_empty = lambda shp, dt: lax.empty(shp, dt)
    w_shard, conv_shard = shards[0], shards[6]
    shp = w_shard.shape
    entry_token, bufs, sems = _gather_start(
        [w_shard, hbm_empty((2,) + shp, BF16), hbm_empty((2, 2) + shp, BF16), conv_shard,
         hbm_empty((NDEV,) + conv_shard.shape, F32)],
        [("own", (0, 1)), ("near", (0, 2)), ("others", (3, 4))], norm_mix, "gather_start_near")
    w_src, own_l, near_l, conv_src, conv_l = bufs
    sem_own, sem_near, sem_conv = sems
    gather_out = {}

    def project(hn):
        w_thru, own = _gather_wait("own", *sem_own, [w_src, own_l], hn, "gather_wait_own")
        own = lax.dynamic_update_slice(own, w_shard[None], (my_core, 0, 0)).reshape(1, CHIP_COLS, D)
        chips = [jnp.stack([my_chip]), jnp.stack([my_chip ^ 1, my_chip ^ 2]), jnp.stack([my_chip ^ 3])]
        chips = [c.astype(jnp.int32) for c in chips]
        proj = _in_proj_chips(hn, own, b_in, chips[0], None, entry_token, "in_proj_own")
        w_thru, near = _gather_wait("near", *sem_near, [w_thru, near_l], proj, "gather_wait_near")
        near = _forward_to_sibling(near, "gather_forward_near")
        token, (w_thru, far_l), (sem_far,) = _gather_start(
            [w_thru, hbm_empty((2,) + shp, BF16)], [("far", (0, 1))], near, "gather_start_far")
        near = near.reshape(2, CHIP_COLS, D)
        proj = _in_proj_chips(hn, near, b_in, chips[1], proj, token, "in_proj_near")
        w_thru, far = _gather_wait("far", *sem_far, [w_thru, far_l], proj, "gather_wait_far")
        far = _forward_to_sibling(far[None], "gather_forward_far").reshape(1, CHIP_COLS, D)
        proj = _in_proj_chips(hn, far, b_in, chips[2], proj, token, "in_proj_far")
        conv_thru, conv_g = _gather_wait("others", *sem_conv, [conv_src, conv_l], proj, "gather_wait_conv")
        conv_g = lax.dynamic_update_slice(conv_g, conv_shard[None], (me, 0, 0))
        conv_f = conv_g.transpose(1, 0, 2).reshape(CONVW, DR)
        srcs = list(shards[1:6])
        token, obufs, (sem_out,) = _gather_start(
            srcs + [hbm_empty((NDEV,) + a.shape, BF16) for a in srcs], [("others", tuple(range(10)))], proj,
            "gather_start_out")
        gather_out.update(bufs=obufs, sems=sem_out)
        return proj, [own, near, far], jnp.concatenate(chips), conv_f, token

    def other_weights(after):
        obufs = _gather_wait("others", *gather_out["sems"], gather_out["bufs"], after, "gather_wait_out")
        full = [lax.dynamic_update_slice(a, s[None], (me, 0, 0)) for a, s in zip(obufs[5:], shards[1:6])]
        return [a.reshape((NDEV * a.shape[1], a.shape[2])) for a in full]

    def start_reduce(arrs, tag):
        parts = [a.reshape((NCHIP, 2, a.shape[0] // NDEV, a.shape[1])) for a in arrs]
        theirs = _exchange_within_chip(parts, "reduce_within_chip_" + tag)
        return _between_chips_start(_sum_pairs(parts, theirs, "sum_pairs_" + tag), "reduce_between_chips_start_" + tag)

    loss_part, grad_x, pending_out, pending_in, small = _local_step(
        x.reshape(T, D), p.reshape(T, PLE), loss_target.reshape(T, D),
        project, other_weights,
        norm_mix, conv_b, w_rg_a[0], b_rg_a, w_rg_x[0], b_rg_x, lru_lambda, q_norm[0], k_norm[0],
        norm_ple, b_ple_gate, start_reduce, entry_token)
    loss = lax.psum(loss_part, ("x", "y", "c"))

    rep_parts = _pack_small_grads(small).reshape(NDEV, REP_ROWS_DEV, LANES)
    conv_parts = small["conv_w"].reshape(CONVW, NDEV, DR // NDEV).transpose(1, 0, 2)
    small_parts = [rep_parts, conv_parts]
    g_rep, g_conv = _sum_blocks_small(small_parts, _exchange_all(small_parts, "reduce_small"), me, (False, False),
                                      "sum_small")
    rep_all = _all_gather_direct(g_rep, "gather_small").reshape(NDEV * REP_ROWS_DEV, LANES)

    myq = _chip(_my_pos())
    own_out, recv_out = _between_chips_wait(pending_out, rep_all, "reduce_between_chips_wait_out")
    own_in, recv_in = _between_chips_wait(pending_in, rep_all, "reduce_between_chips_wait_in")
    w_in_res = _sum_chips_adamw(own_in[0], recv_in[0], w_in[0].T, m_w_in[0].T, v_w_in[0].T, 304, "adamw_w_in")
    g_o_rnn, g_o_att, g_out, g_pg, g_ple = _sum_blocks_small(
        own_out, recv_out, myq, (False, True, False, False, True), "sum_chips_out")

    grad, delta, new_m, new_v = {}, {}, {}, {}
    rep_shape = lambda a: a if a.ndim == 2 else a.reshape(a.shape[1:])
    res = _adamw_small(rep_all, [rep_shape(w[n]) for n in REP_NAMES], [rep_shape(m[n]) for n in REP_NAMES],
                       [rep_shape(v[n]) for n in REP_NAMES])
    for dst, vals in zip((grad, delta, new_m, new_v), res):
        for n, a in zip(REP_NAMES, vals):
            dst[n] = a.reshape(w[n].shape)
    grad["w_in"], delta["w_in"], new_m["w_in"], new_v["w_in"] = [a.T[None] for a in w_in_res]
    rest = ("w_o_rnn", "w_o_att", "w_out", "w_ple_gate", "w_ple", "conv_w")
    g_rest = [g_o_rnn, g_o_att, g_out, g_pg, g_ple, g_conv]
    res = _adamw_many([w[n][0] for n in rest], g_rest, [m[n][0] for n in rest], [v[n][0] for n in rest])
    for n, a in zip(rest, g_rest):
        grad[n] = a[None]
    for dst, vals in zip((delta, new_m, new_v), res):
        for n, a in zip(rest, vals):
            dst[n] = a[None]

    return (loss, grad_x.reshape(BL, S, D), *[grad[n] for n in names], *[delta[n] for n in names],
            *[new_m[n] for n in names], *[new_v[n] for n in names])
```

```python
import jax
import jax.numpy as jnp
from jax import lax
from jax.experimental import pallas as pl
from jax.experimental.pallas import tpu as pltpu

F32 = jnp.float32
BF16 = jnp.bfloat16

D = 1024
S = 2048
BL = 2
T = BL * S
NDEV = 8
NCHIP = 4
PLE = 256
DR = 1280
NRB = 10
RBW = 128
CONVW = 4
LRU_C = 8.0
HD = 128
NH = 4
PATTERNS = ((128, 1), (512, 4), (2048, 16))
NG = 3
ATT = NH * HD
GW = NG * ATT
NIN = 2 * DR + 3 * GW + ATT + 2 * D
OFF_ZR = DR
OFF_Q = 2 * DR
OFF_ZA = OFF_Q + 3 * GW
OFF_G = OFF_ZA + ATT
ROPE_THETA = 10000.0
EPS = 1e-6
SCALE = HD ** -0.5
NEG = -1e30
QB = 128
LANES = 128
CT = 512
NCT = NIN // CT
A_W = 2 * DR
C_W = ATT + 2 * D

LR, B1, B2, AEPS, WD, STEP = 0.001, 0.9, 0.999, 1e-08, 0.01, 10

NSHARD_IN = NIN // NDEV
REP_NAMES = ("w_rg_a", "w_rg_x", "norm_mix", "b_in", "conv_b", "b_rg_a", "b_rg_x", "lru_lambda", "q_norm",
             "k_norm", "norm_ple", "b_ple_gate")
REP_ROWS = (NRB * RBW, NRB * RBW, D // LANES, NIN // LANES, DR // LANES, DR // LANES, DR // LANES, DR // LANES,
            NG, NG, D // LANES, D // LANES)
REP_TOTAL_ROWS = sum(REP_ROWS)
REP_ROWS_DEV = 344
BIG_NAMES = ("w_in", "w_o_rnn", "w_o_att", "w_out", "w_ple_gate", "w_ple")

VMEM_BIG = 56 * 1024 * 1024
VMEM_MID = 40 * 1024 * 1024


def _cp(sem=None, vmem=None):
    return pltpu.CompilerParams(dimension_semantics=sem, vmem_limit_bytes=vmem)


def _hbm(*arrays):
    return [pltpu.with_memory_space_constraint(a, pltpu.HBM) for a in arrays]


def _dot(a, b):
    return jnp.dot(a, b, preferred_element_type=F32)


def _dot_nt(a, b):
    return lax.dot_general(a, b, (((1,), (1,)), ((), ())), preferred_element_type=F32)


def _dot_tn(a, b):
    return lax.dot_general(a, b, (((0,), (0,)), ((), ())), preferred_element_type=F32)


def _sigmoid(x):
    return jax.nn.sigmoid(x)


def _perm(j):
    jq = j - OFF_Q // CT
    inside = (j >= OFF_Q // CT) & (j < OFF_ZA // CT)
    return jnp.where(inside, OFF_Q // CT + (jq % 3) * 3 + jq // 3, j)


PIECES = ((0, A_W // CT), (OFF_Q // CT, GW // CT), (OFF_Q // CT + 3, GW // CT), (OFF_Q // CT + 6, GW // CT),
          (OFF_ZA // CT, C_W // CT))


def _rmsnorm_fwd(x, gain, token, tm=512):
    def body(x_ref, g_ref, _token, o_ref):
        xv = x_ref[...]
        var = jnp.mean(xv * xv, axis=-1, keepdims=True)
        o_ref[...] = (xv * lax.rsqrt(var + EPS) * g_ref[...]).astype(BF16)

    return pl.pallas_call(
        body, grid=(T // tm,), name="rmsnorm_fwd",
        in_specs=[pl.BlockSpec((tm, D), lambda i: (i, 0)), pl.BlockSpec((1, D), lambda i: (0, 0)),
                  pl.BlockSpec((8, LANES), lambda i: (0, 0))],
        out_specs=pl.BlockSpec((tm, D), lambda i: (i, 0)),
        out_shape=jax.ShapeDtypeStruct((T, D), BF16),
        compiler_params=_cp(("parallel",)),
    )(*_hbm(x, gain, token))


CHIP_COLS = NIN // NCHIP


def _in_proj_chips(hn, w_rows, bias, chips, proj, token, name, tm=1024):
    n = w_rows.shape[0]

    def body(chips_ref, a_ref, w_ref, b_ref, _token, *rest):
        o_ref = rest[-1]
        o_ref[...] = (_dot_nt(a_ref[...], w_ref[0]) + b_ref[...]).astype(BF16)

    in_specs = [pl.BlockSpec((tm, D), lambda s, i, ch: (i, 0)),
                pl.BlockSpec((1, CHIP_COLS, D), lambda s, i, ch: (s, 0, 0)),
                pl.BlockSpec((1, CHIP_COLS), lambda s, i, ch: (0, ch[s])),
                pl.BlockSpec((8, LANES), lambda s, i, ch: (0, 0))]
    args = [hn, w_rows, bias, token]
    aliases = {}
    if proj is not None:
        in_specs.append(pl.BlockSpec(memory_space=pl.ANY))
        args.append(proj)
        aliases = {5: 0}
    return pl.pallas_call(
        body, name=name,
        grid_spec=pltpu.PrefetchScalarGridSpec(
            num_scalar_prefetch=1, grid=(n, T // tm), in_specs=in_specs,
            out_specs=pl.BlockSpec((tm, CHIP_COLS), lambda s, i, ch: (i, ch[s]))),
        out_shape=jax.ShapeDtypeStruct((T, NIN), BF16),
        input_output_aliases=aliases,
        compiler_params=_cp(("arbitrary", "arbitrary"), VMEM_BIG),
    )(chips, *_hbm(*args))


def _dhn(pieces, w_bufs, chips, token, tm=512):
    nb = len(w_bufs)

    def body(chips_ref, a_ref, q_ref, k_ref, v_ref, c_ref, *rest):
        w_hbm = rest[:nb]
        o_ref, w = rest[nb + 1], rest[nb + 2]

        @pl.when(pl.program_id(0) == 0)
        def _():
            s = 0
            for buf in w_hbm:
                for r in range(buf.shape[0]):
                    row = pl.multiple_of(chips_ref[s] * CHIP_COLS, 128)
                    pltpu.sync_copy(buf.at[r], w.at[pl.ds(row, CHIP_COLS), :])
                    s += 1

        acc = _dot(a_ref[...], w[pl.ds(0, A_W), :])
        for kind, x_ref in enumerate((q_ref, k_ref, v_ref)):
            for g in range(NG):
                row = OFF_Q + (3 * g + kind) * CT
                acc = acc + _dot(x_ref[:, g * CT:(g + 1) * CT], w[pl.ds(row, CT), :])
        o_ref[...] = acc + _dot(c_ref[...], w[pl.ds(OFF_ZA, C_W), :])

    tok = lambda wd: pl.BlockSpec((tm, wd), lambda i, ch: (i, 0))
    return pl.pallas_call(
        body, name="dhn",
        grid_spec=pltpu.PrefetchScalarGridSpec(
            num_scalar_prefetch=1, grid=(T // tm,),
            in_specs=[tok(A_W), tok(GW), tok(GW), tok(GW), tok(C_W)] + [pl.BlockSpec(memory_space=pl.ANY)] * nb
            + [pl.BlockSpec((8, LANES), lambda i, ch: (0, 0))],
            out_specs=tok(D),
            scratch_shapes=[pltpu.VMEM((NIN, D), BF16)]),
        out_shape=jax.ShapeDtypeStruct((T, D), F32),
        compiler_params=_cp(("arbitrary",), VMEM_BIG),
    )(chips, *_hbm(*pieces, *w_bufs, token))


def _dw_in(pieces, hn):
    names = ("a", "q", "k", "v", "c")
    dw = db = None
    for piece, (lo, n), tag in zip(pieces, PIECES, names):
        def body(x_ref, h_ref, *rest):
            o_ref, s_ref = rest[-2:]
            xv = x_ref[...]
            o_ref[...] = _dot_tn(xv, h_ref[...]).astype(BF16)
            s_ref[...] = jnp.sum(xv.astype(F32), axis=0, keepdims=True)

        in_specs = [pl.BlockSpec((T, CT), lambda j: (0, j)), pl.BlockSpec((T, D), lambda j: (0, 0))]
        args = [piece, hn]
        aliases = {}
        if dw is not None:
            in_specs += [pl.BlockSpec(memory_space=pl.ANY)] * 2
            args += [dw, db]
            aliases = {2: 0, 3: 1}
        dw, db = pl.pallas_call(
            body, grid=(n,), name="dw_in_" + tag,
            in_specs=in_specs,
            out_specs=[pl.BlockSpec((CT, D), lambda j, lo=lo: (_perm(lo + j), 0)),
                       pl.BlockSpec((1, CT), lambda j, lo=lo: (0, _perm(lo + j)))],
            out_shape=[jax.ShapeDtypeStruct((NIN, D), BF16), jax.ShapeDtypeStruct((1, NIN), F32)],
            input_output_aliases=aliases,
            compiler_params=_cp(("parallel",), VMEM_MID),
        )(*_hbm(*args))
    return dw, db


def _mm_tn(a, b, ta, tt, name):
    m = a.shape[1]
    n = b.shape[1]
    nt = T // tt

    def body(a_ref, b_ref, o_ref, acc):
        t = pl.program_id(1)
        p = _dot_tn(a_ref[...].astype(BF16), b_ref[...].astype(BF16))

        @pl.when(t == 0)
        def _():
            acc[...] = p

        @pl.when(t > 0)
        def _():
            acc[...] += p

        @pl.when(t == nt - 1)
        def _():
            o_ref[...] = acc[...].astype(BF16)

    return pl.pallas_call(
        body, grid=(m // ta, nt), name=name,
        in_specs=[pl.BlockSpec((tt, ta), lambda j, t: (t, j)), pl.BlockSpec((tt, n), lambda j, t: (t, 0))],
        out_specs=pl.BlockSpec((ta, n), lambda j, t: (j, 0)),
        out_shape=jax.ShapeDtypeStruct((m, n), BF16),
        scratch_shapes=[pltpu.VMEM((ta, n), F32)],
        compiler_params=_cp(("parallel", "arbitrary"), VMEM_MID),
    )(*_hbm(a, b))


def _row_iota():
    return lax.broadcasted_iota(jnp.int32, (S, RBW), 0)


def _shift_down(v, d, row, fill):
    return jnp.where(row >= d, pltpu.roll(v, d, 0), fill)


def _shift_up(v, d, row, fill):
    return jnp.where(row < S - d, pltpu.roll(v, S - d, 0), fill)


SUBLANES = 8


def _scan_down(a, u, row):
    d = 1
    while d < S:
        last = 2 * d >= S
        if d < SUBLANES:
            u = a * _shift_down(u, d, row, 0.0) + u
            if not last:
                a = a * _shift_down(a, d, row, 1.0)
        else:
            u = jnp.concatenate([u[:d], a[d:] * u[:S - d] + u[d:]], axis=0)
            if not last:
                a = jnp.concatenate([a[:d], a[d:] * a[:S - d]], axis=0)
        d *= 2
    return u


def _scan_up(b, g, row):
    d = 1
    while d < S:
        last = 2 * d >= S
        if d < SUBLANES:
            g = g + b * _shift_up(g, d, row, 0.0)
            if not last:
                b = b * _shift_up(b, d, row, 0.0)
        else:
            g = jnp.concatenate([g[:S - d] + b[:S - d] * g[d:], g[S - d:]], axis=0)
            if not last:
                b = jnp.concatenate([b[:S - d] * b[d:], b[S - d:]], axis=0)
        d *= 2
    return g


def _softplus(x):
    return jnp.maximum(x, 0.0) + jnp.log1p(jnp.exp(-jnp.abs(x)))


def _rnn_gates(x, cw, cb, wa, ba, wx, bx, lam, row):
    xs = [_shift_down(x, j, row, 0.0) for j in (1, 2, 3)]
    xc = cb + cw[3:4, :] * x
    for j in (1, 2, 3):
        xc = xc + cw[3 - j:4 - j, :] * xs[j - 1]
    xcb = xc.astype(BF16)
    r = _sigmoid(_dot(xcb, wa) + ba)
    i = _sigmoid(_dot(xcb, wx) + bx)
    sp = _softplus(-lam)
    log_a = (-LRU_C) * r * sp
    a = jnp.exp(log_a)
    mult = jnp.where(row == 0, 1.0, jnp.sqrt(jnp.tanh(-log_a) * (1.0 + a * a)))
    return xc, xcb, r, i, sp, a, mult, xs


def _rnn_fwd(proj3, conv_w, conv_b, wa, ba, wx, bx, lam, token):
    def body(x_ref, cw_ref, cb_ref, wa_ref, ba_ref, wx_ref, bx_ref, lam_ref, _token, h_ref):
        row = _row_iota()
        x = x_ref[0].astype(F32)
        xc, _, _, i, _, a, mult, _ = _rnn_gates(x, cw_ref[...], cb_ref[...], wa_ref[0], ba_ref[...],
                                             wx_ref[0], bx_ref[...], lam_ref[...], row)
        h_ref[0] = _scan_down(a, mult * (i * xc), row)

    vec = lambda: pl.BlockSpec((1, RBW), lambda b, n: (0, n))
    mat = lambda: pl.BlockSpec((1, RBW, RBW), lambda b, n: (n, 0, 0))
    return pl.pallas_call(
        body, grid=(BL, NRB), name="rnn_fwd",
        in_specs=[pl.BlockSpec((1, S, RBW), lambda b, n: (b, 0, n)),
                  pl.BlockSpec((CONVW, RBW), lambda b, n: (0, n)),
                  vec(), mat(), vec(), mat(), vec(), vec(), pl.BlockSpec((8, LANES), lambda b, n: (0, 0))],
        out_specs=pl.BlockSpec((1, S, RBW), lambda b, n: (b, 0, n)),
        out_shape=jax.ShapeDtypeStruct((BL, S, DR), F32),
        compiler_params=_cp(("parallel", "parallel"), VMEM_MID),
    )(*_hbm(proj3, conv_w, conv_b, wa, ba, wx, bx, lam, token))


def _rnn_bwd(proj3, h3, dh3, slab_a3, conv_w, conv_b, wa, ba, wx, bx, lam, token):
    def body(x_ref, h_ref, dh_ref, cw_ref, cb_ref, wa_ref, ba_ref, wx_ref, bx_ref, lam_ref, _alias, _token,
             dx_ref, dcw_ref, dcb_ref, dwa_ref, dba_ref, dwx_ref, dbx_ref, dlam_ref):
        row = _row_iota()
        x = x_ref[0].astype(F32)
        cw = cw_ref[...]
        wa_v = wa_ref[0]
        wx_v = wx_ref[0]
        lam_v = lam_ref[...]
        xc, xcb, r, i, sp, a, mult, xs = _rnn_gates(x, cw, cb_ref[...], wa_v, ba_ref[...], wx_v, bx_ref[...], lam_v,
                                                    row)
        h = h_ref[0]
        g = _scan_up(_shift_up(a, 1, row, 0.0), dh_ref[0], row)
        da = g * _shift_down(h, 1, row, 0.0)
        dmult = jnp.where(row == 0, 0.0, g * (i * xc))
        gm = g * mult
        di = gm * xc
        dxc = gm * i
        dlog_a = da * a - dmult * (a * a) / mult
        dr = dlog_a * ((-LRU_C) * sp)
        dsp = jnp.sum(dlog_a * ((-LRU_C) * r), axis=0, keepdims=True)
        dlam = dsp * (-_sigmoid(-lam_v))
        dpa = dr * r * (1.0 - r)
        dpx = di * i * (1.0 - i)
        dpab = dpa.astype(BF16)
        dpxb = dpx.astype(BF16)
        dwa = _dot_tn(xcb, dpab)
        dwx = _dot_tn(xcb, dpxb)
        dba = jnp.sum(dpa, axis=0, keepdims=True)
        dbx = jnp.sum(dpx, axis=0, keepdims=True)
        dxc = dxc + _dot_nt(dpab, wa_v) + _dot_nt(dpxb, wx_v)
        dcb = jnp.sum(dxc, axis=0, keepdims=True)
        dx = cw[3:4, :] * dxc
        dcw_rows = [None] * CONVW
        dcw_rows[3] = jnp.sum(dxc * x, axis=0, keepdims=True)
        for j in (1, 2, 3):
            dx = dx + cw[3 - j:4 - j, :] * _shift_up(dxc, j, row, 0.0)
            dcw_rows[3 - j] = jnp.sum(dxc * xs[j - 1], axis=0, keepdims=True)
        dx_ref[0] = dx.astype(BF16)
        dcw = jnp.concatenate(dcw_rows, axis=0)
        first = pl.program_id(1) == 0

        @pl.when(first)
        def _():
            dcw_ref[...] = dcw
            dcb_ref[...] = dcb
            dwa_ref[0] = dwa
            dba_ref[...] = dba
            dwx_ref[0] = dwx
            dbx_ref[...] = dbx
            dlam_ref[...] = dlam

        @pl.when(jnp.logical_not(first))
        def _():
            dcw_ref[...] += dcw
            dcb_ref[...] += dcb
            dwa_ref[0] += dwa
            dba_ref[...] += dba
            dwx_ref[0] += dwx
            dbx_ref[...] += dbx
            dlam_ref[...] += dlam

    slab = lambda: pl.BlockSpec((1, S, RBW), lambda n, b: (b, 0, n))
    vec = lambda: pl.BlockSpec((1, RBW), lambda n, b: (0, n))
    mat = lambda: pl.BlockSpec((1, RBW, RBW), lambda n, b: (n, 0, 0))
    taps = lambda: pl.BlockSpec((CONVW, RBW), lambda n, b: (0, n))
    vshape = jax.ShapeDtypeStruct((1, DR), F32)
    mshape = jax.ShapeDtypeStruct((NRB, RBW, RBW), F32)
    return pl.pallas_call(
        body, grid=(NRB, BL), name="rnn_bwd",
        in_specs=[slab(), slab(), slab(), taps(), vec(), mat(), vec(), mat(), vec(), vec(),
                  pl.BlockSpec(memory_space=pl.ANY), pl.BlockSpec((8, LANES), lambda n, b: (0, 0))],
        out_specs=[slab(), taps(), vec(), mat(), vec(), mat(), vec(), vec()],
        out_shape=[jax.ShapeDtypeStruct((BL, S, A_W), BF16), jax.ShapeDtypeStruct((CONVW, DR), F32),
                   vshape, mshape, vshape, mshape, vshape, vshape],
        input_output_aliases={10: 0},
        compiler_params=_cp(("parallel", "arbitrary"), 48 * 1024 * 1024),
    )(*_hbm(proj3, h3, dh3, conv_w, conv_b, wa, ba, wx, bx, lam, slab_a3, token))


NQB = S // QB


def _rms_head(t, gain):
    rstd = lax.rsqrt(jnp.mean(t * t, axis=-1, keepdims=True) + EPS)
    return t * rstd * gain


def _rope(t, cs, sn):
    return t * cs + pltpu.roll(t, HD // 2, 1) * sn


def _rope_t(dy, cs, sn):
    return dy * cs - pltpu.roll(dy, HD // 2, 1) * sn


def _bdot_nt(a, b):
    return lax.dot_general(a, b, (((2,), (2,)), ((0,), (0,))), preferred_element_type=F32)


def _bdot(a, b):
    return lax.dot_general(a, b, (((2,), (1,)), ((0,), (0,))), preferred_element_type=F32)


def _bdot_tn(a, b):
    return lax.dot_general(a, b, (((1,), (1,)), ((0,), (0,))), preferred_element_type=F32)


STRIDE_MAX = 4


def _permute(buf, x, dil, dst, off=0):
    ln = S // dil
    if dil == 1:
        dst[pl.ds(off, S), :] = x.astype(dst.dtype)
        return
    buf[0] = x
    if dil <= STRIDE_MAX:
        for c in range(dil):
            dst[pl.ds(off + c * ln, ln), :] = buf.at[0][pl.ds(c, ln, stride=dil), :].astype(dst.dtype)
        return
    f, r = STRIDE_MAX, dil // STRIDE_MAX
    part = S // f
    for c1 in range(f):
        buf.at[1][pl.ds(c1 * part, part), :] = buf.at[0][pl.ds(c1, part, stride=f), :]
    for c1 in range(f):
        for c2 in range(r):
            dst[pl.ds(off + (c1 + f * c2) * ln, ln), :] = (
                buf.at[1][pl.ds(c1 * part + c2, ln, stride=r), :].astype(dst.dtype))


def _unpermute(buf, xp, dil, dst):
    ln = S // dil
    if dil == 1:
        dst[...] = xp
        return
    if dil <= STRIDE_MAX:
        for c in range(dil):
            dst[pl.ds(c, ln, stride=dil), :] = xp[c * ln:(c + 1) * ln]
        return
    f, r = STRIDE_MAX, dil // STRIDE_MAX
    part = S // f
    for c1 in range(f):
        for c2 in range(r):
            c = c1 + f * c2
            buf.at[1][pl.ds(c1 * part + c2, ln, stride=r), :] = xp[c * ln:(c + 1) * ln]
    for c1 in range(f):
        dst[pl.ds(c1, part, stride=f), :] = buf[1, pl.ds(c1 * part, part), :]


def _blocks3(ref, off=0):
    return ref[pl.ds(off, S), :].reshape(NQB, QB, HD)


def _att_prep(q_ref, k_ref, v_ref, cos_ref, sin_ref, qn, kn, dil, nat, qs, ksp, vsp):
    cs = cos_ref[...]
    sn = sin_ref[...]
    zero = jnp.zeros((QB, HD), BF16)
    ksp[pl.ds(0, QB), :] = zero
    vsp[pl.ds(0, QB), :] = zero
    _permute(nat, _rope(_rms_head(q_ref[0].astype(F32), qn), cs, sn), dil, qs)
    _permute(nat, _rope(_rms_head(k_ref[0].astype(F32), kn), cs, sn), dil, ksp, QB)
    _permute(nat, v_ref[0].astype(F32), dil, vsp, QB)


def _att_scores(qs, ksp, dil):
    nb = S // dil // QB
    q3 = _blocks3(qs)
    shape = (NQB, QB, QB)
    qi = lax.broadcasted_iota(jnp.int32, shape, 1)
    kj = lax.broadcasted_iota(jnp.int32, shape, 2)
    s_c = jnp.where(qi >= kj, _bdot_nt(q3, _blocks3(ksp, QB)) * SCALE, NEG)
    if nb == 1:
        return q3, s_c, None
    jj = lax.broadcasted_iota(jnp.int32, shape, 0)
    ok = (kj >= qi) & ((jj & (nb - 1)) != 0)
    s_p = jnp.where(ok, _bdot_nt(q3, _blocks3(ksp)) * SCALE, NEG)
    return q3, s_c, s_p


def _qkv_spec(kind, g):
    base = OFF_Q // HD + (3 * g + kind) * NH
    return pl.BlockSpec((1, S, HD), lambda b, h: (b, 0, base + h))


def _attn_fwd(proj3, cos_t, sin_t, q_norm, k_norm):
    def body(*refs):
        qkv_refs = refs[:9]
        cos_ref, sin_ref, qn_ref, kn_ref, att_ref, lse_ref, w_ref, nat, qs, ksp, vsp, og = refs[9:]
        for g, (window, dil) in enumerate(PATTERNS):
            q_ref, k_ref, v_ref = qkv_refs[3 * g:3 * g + 3]
            _att_prep(q_ref, k_ref, v_ref, cos_ref, sin_ref, qn_ref[g:g + 1, :], kn_ref[g:g + 1, :], dil,
                      nat, qs, ksp, vsp)
            _, s_c, s_p = _att_scores(qs, ksp, dil)
            m = jnp.max(s_c, axis=-1, keepdims=True)
            if s_p is not None:
                m = jnp.maximum(m, jnp.max(s_p, axis=-1, keepdims=True))
            e_c = jnp.exp(s_c - m)
            den = jnp.sum(e_c, axis=-1, keepdims=True)
            o = _bdot(e_c.astype(BF16), _blocks3(vsp, QB))
            if s_p is not None:
                e_p = jnp.exp(s_p - m)
                den = den + jnp.sum(e_p, axis=-1, keepdims=True)
                o = o + _bdot(e_p.astype(BF16), _blocks3(vsp))
            _unpermute(nat, (o / den).reshape(S, HD), dil, og.at[g])
            _unpermute(nat, jnp.broadcast_to(m + jnp.log(den), (NQB, QB, HD)).reshape(S, HD), dil,
                       lse_ref.at[g, 0])
        l0 = lse_ref[0, 0]
        l1 = lse_ref[1, 0]
        l2 = lse_ref[2, 0]
        mx = jnp.maximum(jnp.maximum(l0, l1), l2)
        e0 = jnp.exp(l0 - mx)
        e1 = jnp.exp(l1 - mx)
        e2 = jnp.exp(l2 - mx)
        inv = 1.0 / (e0 + e1 + e2)
        w0 = e0 * inv
        w1 = e1 * inv
        w2 = e2 * inv
        w_ref[0, 0] = w0
        w_ref[1, 0] = w1
        w_ref[2, 0] = w2
        att_ref[0] = w0 * og[0] + w1 * og[1] + w2 * og[2]

    in_specs = [_qkv_spec(kind, g) for g in range(NG) for kind in range(3)]
    in_specs += [pl.BlockSpec((S, HD), lambda b, h: (0, 0)), pl.BlockSpec((S, HD), lambda b, h: (0, 0)),
                 pl.BlockSpec((NG, HD), lambda b, h: (0, 0)), pl.BlockSpec((NG, HD), lambda b, h: (0, 0))]
    stat = lambda: pl.BlockSpec((NG, 1, S, HD), lambda b, h: (0, b, 0, h))
    return pl.pallas_call(
        body, grid=(BL, NH), name="attn_fwd",
        in_specs=in_specs,
        out_specs=[pl.BlockSpec((1, S, HD), lambda b, h: (b, 0, h)), stat(), stat()],
        out_shape=[jax.ShapeDtypeStruct((BL, S, ATT), F32),
                   jax.ShapeDtypeStruct((NG, BL, S, ATT), F32),
                   jax.ShapeDtypeStruct((NG, BL, S, ATT), F32)],
        scratch_shapes=[pltpu.VMEM((2, S, HD), F32), pltpu.VMEM((S, HD), BF16), pltpu.VMEM((S + QB, HD), BF16),
                        pltpu.VMEM((S + QB, HD), BF16), pltpu.VMEM((NG, S, HD), F32)],
        compiler_params=_cp(("parallel", "parallel"), VMEM_BIG),
    )(*_hbm(*([proj3] * 9), cos_t, sin_t, q_norm, k_norm))


def _attn_bwd_group(g, proj3, cos_t, sin_t, qn_g, kn_g, lse, wts, datt3, sbar3, slabs):
    dil = PATTERNS[g][1]
    n_alias = 0 if slabs is None else 3

    def norm_rope_bwd(dpost, raw, gain, cs, sn):
        dn = _rope_t(dpost, cs, sn)
        rstd = lax.rsqrt(jnp.mean(raw * raw, axis=-1, keepdims=True) + EPS)
        xh = raw * rstd
        dgain = jnp.sum(dn * xh, axis=0, keepdims=True)
        gd = dn * gain
        draw = rstd * (gd - xh * jnp.mean(gd * xh, axis=-1, keepdims=True))
        return draw, dgain

    def body(*refs):
        (q_ref, k_ref, v_ref, cos_ref, sin_ref, qn_ref, kn_ref, lse_ref, w_ref, datt_ref, sbar_ref) = refs[:11]
        (dq_ref, dk_ref, dv_ref, dqn_ref, dkn_ref, nat, qs, ksp, vsp, dos, cvp, lsp, acc) = refs[11 + n_alias:]
        qn = qn_ref[...]
        kn = kn_ref[...]
        cs = cos_ref[...]
        sn = sin_ref[...]
        _att_prep(q_ref, k_ref, v_ref, cos_ref, sin_ref, qn, kn, dil, nat, qs, ksp, vsp)
        wv = w_ref[0, 0]
        _permute(nat, wv * datt_ref[0], dil, dos)
        _permute(nat, wv * sbar_ref[0], dil, cvp)
        _permute(nat, lse_ref[0, 0], dil, lsp)
        q3, s_c, s_p = _att_scores(qs, ksp, dil)
        do3 = _blocks3(dos)
        lse3 = _blocks3(lsp)[:, :, 0:1]
        cv3 = _blocks3(cvp)[:, :, 0:1]
        p_c = jnp.exp(s_c - lse3)
        ds_c = (p_c * (_bdot_nt(do3, _blocks3(vsp, QB)) - cv3)).astype(BF16)
        dq = _bdot(ds_c, _blocks3(ksp, QB))
        acc[0] = _bdot_tn(ds_c, q3).reshape(S, HD)
        acc[1] = _bdot_tn(p_c.astype(BF16), do3).reshape(S, HD)
        if s_p is not None:
            p_p = jnp.exp(s_p - lse3)
            ds_p = (p_p * (_bdot_nt(do3, _blocks3(vsp)) - cv3)).astype(BF16)
            dq = dq + _bdot(ds_p, _blocks3(ksp))
            early = pl.ds(0, S - QB)
            acc[0, early, :] += _bdot_tn(ds_p, q3).reshape(S, HD)[QB:]
            acc[1, early, :] += _bdot_tn(p_p.astype(BF16), do3).reshape(S, HD)[QB:]
        _unpermute(nat, (dq * SCALE).reshape(S, HD), dil, nat.at[0])
        draw, dqn = norm_rope_bwd(nat[0], q_ref[0].astype(F32), qn, cs, sn)
        dq_ref[0] = draw.astype(BF16)
        _unpermute(nat, acc[0] * SCALE, dil, nat.at[0])
        draw, dkn = norm_rope_bwd(nat[0], k_ref[0].astype(F32), kn, cs, sn)
        dk_ref[0] = draw.astype(BF16)
        _unpermute(nat, acc[1], dil, nat.at[0])
        dv_ref[0] = nat[0].astype(BF16)
        first = (pl.program_id(0) == 0) & (pl.program_id(1) == 0)

        @pl.when(first)
        def _():
            dqn_ref[...] = dqn
            dkn_ref[...] = dkn

        @pl.when(jnp.logical_not(first))
        def _():
            dqn_ref[...] += dqn
            dkn_ref[...] += dkn

    full = lambda r: pl.BlockSpec((r, HD), lambda b, h: (0, 0))
    stat = lambda: pl.BlockSpec((1, 1, S, HD), lambda b, h: (g, b, 0, h))
    slab = lambda: pl.BlockSpec((1, S, HD), lambda b, h: (b, 0, h))
    out_slab = lambda: pl.BlockSpec((1, S, HD), lambda b, h: (b, 0, g * NH + h))
    big = jax.ShapeDtypeStruct((BL, S, GW), BF16)
    vecs = jax.ShapeDtypeStruct((1, HD), F32)
    in_specs = [_qkv_spec(0, g), _qkv_spec(1, g), _qkv_spec(2, g), full(S), full(S), full(1), full(1),
                stat(), stat(), slab(), slab()]
    args = [proj3, proj3, proj3, cos_t, sin_t, qn_g, kn_g, lse, wts, datt3, sbar3]
    aliases = {}
    if slabs is not None:
        in_specs += [pl.BlockSpec(memory_space=pl.ANY)] * 3
        args += list(slabs)
        aliases = {11: 0, 12: 1, 13: 2}
    return pl.pallas_call(
        body, grid=(BL, NH), name="attn_bwd_g%d" % g,
        in_specs=in_specs,
        out_specs=[out_slab(), out_slab(), out_slab(), full(1), full(1)],
        out_shape=[big, big, big, vecs, vecs],
        scratch_shapes=[pltpu.VMEM((2, S, HD), F32), pltpu.VMEM((S, HD), BF16), pltpu.VMEM((S + QB, HD), BF16),
                        pltpu.VMEM((S + QB, HD), BF16), pltpu.VMEM((S, HD), BF16), pltpu.VMEM((S, HD), F32),
                        pltpu.VMEM((S, HD), F32), pltpu.VMEM((2, S, HD), F32)],
        input_output_aliases=aliases,
        compiler_params=_cp(("arbitrary", "arbitrary"), VMEM_BIG),
    )(*_hbm(*args))


def _tail(x, proj, h, att, p, tgt, w_o_rnn, w_o_att_t, w_out, w_pg, w_ple_t, norm_ple, b_pg, tm=256):
    nt = T // tm
    inv_d = 1.0 / D

    def body(x_ref, h_ref, zr_ref, att_ref, za_ref, g0a_ref, g0b_ref, g1a_ref, g1b_ref, p_ref, tgt_ref,
             np_ref, bpg_ref, wor_hbm, woa_hbm, wout_hbm, wpg_hbm, wple_hbm,
             dx1_ref, merged_ref, n1_ref, dpre_ref, dpe_ref, dyr_ref, dya_ref, slab_a_ref, slab_c_ref, dh_ref,
             datt_ref, sbar_ref, yrnn_ref, yatt_ref, loss_ref, dnp_ref, dbpg_ref,
             wor, woa, wout, wpg, wple):
        first = pl.program_id(0) == 0

        @pl.when(first)
        def _():
            pltpu.sync_copy(wor_hbm, wor)
            pltpu.sync_copy(woa_hbm, woa)
            pltpu.sync_copy(wout_hbm, wout)
            pltpu.sync_copy(wpg_hbm, wpg)
            pltpu.sync_copy(wple_hbm, wple)

        xv = x_ref[...]
        hv = h_ref[...]
        zr = zr_ref[...].astype(F32)
        av = att_ref[...]
        za = za_ref[...].astype(F32)
        szr = _sigmoid(zr)
        silu_r = zr * szr
        yrnn_b = (hv * silu_r).astype(BF16)
        sza = _sigmoid(za)
        silu_a = za * sza
        yatt_b = (av * silu_a).astype(BF16)
        yrnn_ref[...] = yrnn_b
        yatt_ref[...] = yatt_b
        yr = _dot(yrnn_b, wor[...])
        ya = _dot_nt(yatt_b, woa[...])
        g0 = _sigmoid(jnp.concatenate([g0a_ref[...], g0b_ref[...]], axis=1).astype(F32))
        g1 = _sigmoid(jnp.concatenate([g1a_ref[...], g1b_ref[...]], axis=1).astype(F32))
        merged_b = (g0 * yr + g1 * ya).astype(BF16)
        merged_ref[...] = merged_b
        x1 = xv + _dot(merged_b, wout[...])
        rstd = lax.rsqrt(jnp.mean(x1 * x1, axis=-1, keepdims=True) + EPS)
        xh = x1 * rstd
        npl = np_ref[...]
        n1_b = (xh * npl).astype(BF16)
        n1_ref[...] = n1_b
        pg = _sigmoid(_dot(n1_b, wpg[...]) + bpg_ref[...])
        pe = _dot_nt(p_ref[...].astype(BF16), wple[...])
        err = x1 + pg * pe - tgt_ref[...]
        loss_t = 0.5 * inv_d * jnp.sum(err * err)
        dy = err * inv_d
        dpe_ref[...] = (dy * pg).astype(BF16)
        dpre = dy * pe * pg * (1.0 - pg)
        dpre_b = dpre.astype(BF16)
        dpre_ref[...] = dpre_b
        dn1 = _dot_nt(dpre_b, wpg[...])
        dnp = jnp.sum(dn1 * xh, axis=0, keepdims=True)
        dbpg = jnp.sum(dpre, axis=0, keepdims=True)
        gd = dn1 * npl
        dx1 = dy + rstd * (gd - xh * jnp.mean(gd * xh, axis=-1, keepdims=True))
        dx1_ref[...] = dx1
        dmerged = _dot_nt(dx1.astype(BF16), wout[...])
        dyr_b = (dmerged * g0).astype(BF16)
        dya_b = (dmerged * g1).astype(BF16)
        dyr_ref[...] = dyr_b
        dya_ref[...] = dya_b
        slab_c_ref[:, ATT:ATT + D] = (dmerged * yr * g0 * (1.0 - g0)).astype(BF16)
        slab_c_ref[:, ATT + D:ATT + 2 * D] = (dmerged * ya * g1 * (1.0 - g1)).astype(BF16)
        dyrnn = _dot_nt(dyr_b, wor[...])
        dyatt = _dot(dya_b, woa[...])
        dh_ref[...] = dyrnn * silu_r
        slab_a_ref[...] = (dyrnn * hv * szr * (1.0 + zr * (1.0 - szr))).astype(BF16)
        datt = dyatt * silu_a
        datt_ref[...] = datt
        slab_c_ref[:, 0:ATT] = (dyatt * av * sza * (1.0 + za * (1.0 - sza))).astype(BF16)
        da = datt * av
        for hh in range(NH):
            seg = slice(hh * HD, (hh + 1) * HD)
            sbar_ref[:, seg] = jnp.broadcast_to(jnp.sum(da[:, seg], axis=-1, keepdims=True), (tm, HD))

        @pl.when(first)
        def _():
            loss_ref[...] = jnp.full((8, LANES), loss_t, F32)
            dnp_ref[...] = dnp
            dbpg_ref[...] = dbpg

        @pl.when(jnp.logical_not(first))
        def _():
            loss_ref[...] += jnp.full((8, LANES), loss_t, F32)
            dnp_ref[...] += dnp
            dbpg_ref[...] += dbpg

    tok = lambda w: pl.BlockSpec((tm, w), lambda i: (i, 0))
    col = lambda w, blk: pl.BlockSpec((tm, w), lambda i: (i, blk))
    vec = lambda: pl.BlockSpec((1, D), lambda i: (0, 0))
    hbm = lambda: pl.BlockSpec(memory_space=pl.ANY)
    gb = OFF_G // 512
    in_specs = [tok(D), tok(DR), col(DR, 1), tok(ATT), col(ATT, OFF_ZA // ATT),
                col(512, gb), col(512, gb + 1), col(512, gb + 2), col(512, gb + 3),
                tok(PLE), tok(D), vec(), vec(), hbm(), hbm(), hbm(), hbm(), hbm()]
    sh = lambda w, dt: jax.ShapeDtypeStruct((T, w), dt)
    out_shape = [sh(D, F32), sh(D, BF16), sh(D, BF16), sh(D, BF16), sh(D, BF16), sh(D, BF16), sh(D, BF16),
                 sh(A_W, BF16), sh(C_W, BF16), sh(DR, F32), sh(ATT, F32), sh(ATT, F32),
                 sh(DR, BF16), sh(ATT, BF16),
                 jax.ShapeDtypeStruct((8, LANES), F32), jax.ShapeDtypeStruct((1, D), F32),
                 jax.ShapeDtypeStruct((1, D), F32)]
    out_specs = [tok(D), tok(D), tok(D), tok(D), tok(D), tok(D), tok(D), col(DR, 1), tok(C_W), tok(DR),
                 tok(ATT), tok(ATT), tok(DR), tok(ATT),
                 pl.BlockSpec((8, LANES), lambda i: (0, 0)), vec(), vec()]
    return pl.pallas_call(
        body, grid=(nt,), name="tail_fwd_bwd",
        in_specs=in_specs, out_specs=out_specs, out_shape=out_shape,
        scratch_shapes=[pltpu.VMEM((DR, D), BF16), pltpu.VMEM((D, ATT), BF16), pltpu.VMEM((D, D), BF16),
                        pltpu.VMEM((D, D), BF16), pltpu.VMEM((D, PLE), BF16)],
        compiler_params=_cp(("arbitrary",), VMEM_BIG),
    )(*_hbm(x, h, proj, att, proj, proj, proj, proj, proj, p, tgt, norm_ple, b_pg, w_o_rnn, w_o_att_t, w_out, w_pg,
            w_ple_t))


def _input_norm_bwd(x, dhn, dx1, gain, tm=512):
    def body(x_ref, dhn_ref, dx1_ref, g_ref, dx_ref, dg_ref):
        xv = x_ref[...]
        rstd = lax.rsqrt(jnp.mean(xv * xv, axis=-1, keepdims=True) + EPS)
        xh = xv * rstd
        dn = dhn_ref[...]
        dg = jnp.sum(dn * xh, axis=0, keepdims=True)
        gd = dn * g_ref[...]
        dx_ref[...] = dx1_ref[...] + rstd * (gd - xh * jnp.mean(gd * xh, axis=-1, keepdims=True))
        first = pl.program_id(0) == 0

        @pl.when(first)
        def _():
            dg_ref[...] = dg

        @pl.when(jnp.logical_not(first))
        def _():
            dg_ref[...] += dg

    tok = lambda: pl.BlockSpec((tm, D), lambda i: (i, 0))
    vec = lambda: pl.BlockSpec((1, D), lambda i: (0, 0))
    return pl.pallas_call(
        body, grid=(T // tm,), name="input_norm_bwd",
        in_specs=[tok(), tok(), tok(), vec()], out_specs=[tok(), vec()],
        out_shape=[jax.ShapeDtypeStruct((T, D), F32), jax.ShapeDtypeStruct((1, D), F32)],
        compiler_params=_cp(("arbitrary",), VMEM_MID),
    )(*_hbm(x, dhn, dx1, gain))


def _rope_tables():
    pos = jnp.arange(S, dtype=F32)
    inv_freq = ROPE_THETA ** (-jnp.arange(0, HD, 2, dtype=F32) / HD)
    ang = pos[:, None] * inv_freq[None, :]
    cos, sin = jnp.cos(ang), jnp.sin(ang)
    return jnp.concatenate([cos, cos], axis=1), jnp.concatenate([-sin, sin], axis=1)


def _local_step(x, p, tgt, project, other_weights, norm_mix, conv_b,
                w_rg_a, b_rg_a, w_rg_x, b_rg_x, lam, q_norm, k_norm, norm_ple, b_pg, start_reduce=None,
                entry_token=None):
    if start_reduce is None:
        start_reduce = lambda arrs, tag: (jnp.zeros((8, LANES), F32), arrs)
    if entry_token is None:
        entry_token = jnp.zeros((8, LANES), F32)
    cos_t, sin_t = _rope_tables()
    wa_b = w_rg_a.astype(BF16)
    wx_b = w_rg_x.astype(BF16)

    hn = _rmsnorm_fwd(x, norm_mix, entry_token)
    proj, w_bufs, chips, conv_w, token = project(hn)
    proj3 = proj.reshape(BL, S, NIN)
    h3 = _rnn_fwd(proj3, conv_w, conv_b, wa_b, b_rg_a, wx_b, b_rg_x, lam, token)
    att3, lse, wts = _attn_fwd(proj3, cos_t, sin_t, q_norm, k_norm)
    w_o_rnn, w_o_att_t, w_out, w_pg, w_ple_t = other_weights(att3)
    (dx1, merged, n1, dpre, dpe, dyr, dya, slab_a, slab_c, dh, datt, sbar, yrnn, yatt, loss8, dnp, dbpg) = _tail(
        x, proj, h3.reshape(T, DR), att3.reshape(T, ATT), p, tgt, w_o_rnn, w_o_att_t, w_out, w_pg, w_ple_t,
        norm_ple, b_pg)

    token, pending_out = start_reduce([
        _mm_tn(yrnn, dyr, 640, 2048, "dw_o_rnn"),
        _mm_tn(dya, yatt, 512, 2048, "dw_o_att_t"),
        _mm_tn(merged, dx1, 512, 2048, "dw_out"),
        _mm_tn(n1, dpre, 512, 2048, "dw_ple_gate"),
        _mm_tn(dpe, p, 512, 2048, "dw_ple_t")], "out")

    slab_a3, dcw, dcb, dwa, dba, dwx, dbx, dlam = _rnn_bwd(
        proj3, h3, dh.reshape(BL, S, DR), slab_a.reshape(BL, S, A_W), conv_w, conv_b, wa_b, b_rg_a, wx_b, b_rg_x, lam,
        token)
    datt3 = datt.reshape(BL, S, ATT)
    sbar3 = sbar.reshape(BL, S, ATT)
    slabs = None
    dqn = []
    dkn = []
    for g in range(NG):
        dq, dk, dv, dqn_g, dkn_g = _attn_bwd_group(g, proj3, cos_t, sin_t, q_norm[g:g + 1], k_norm[g:g + 1],
                                                   lse, wts, datt3, sbar3, slabs)
        slabs = (dq, dk, dv)
        dqn.append(dqn_g)
        dkn.append(dkn_g)
    pieces = [slab_a3.reshape(T, A_W)] + [t.reshape(T, GW) for t in slabs] + [slab_c]
    dw_in_t, db_in = _dw_in(pieces, hn)
    token, pending_in = start_reduce([dw_in_t], "in")
    dhn = _dhn(pieces, w_bufs, chips, token)
    grad_x, dnm = _input_norm_bwd(x, dhn, dx1, norm_mix)

    small = dict(w_rg_a=dwa, w_rg_x=dwx, norm_mix=dnm, b_in=db_in, conv_b=dcb, b_rg_a=dba, b_rg_x=dbx,
                 lru_lambda=dlam, q_norm=dqn, k_norm=dkn, norm_ple=dnp, b_ple_gate=dbpg, conv_w=dcw, loss=loss8)
    return grad_x, pending_out, pending_in, small


MESH = pl.DeviceIdType.MESH
HBM_SPEC = pl.BlockSpec(memory_space=pl.ANY)


def _my_pos():
    return lax.axis_index("x"), lax.axis_index("y"), lax.axis_index("c")


def _flip(pos, k):
    x, y, c = pos
    return (1 - x if k & 4 else x, 1 - y if k & 2 else y, 1 - c if k & 1 else c)


def _lin(pos):
    return 4 * pos[0] + 2 * pos[1] + pos[2]


def _chip(pos):
    return 2 * pos[0] + pos[1]


def _all_gather_two_level(shards, name):
    na = len(shards)

    def body(*refs):
        x_refs = refs[:na]
        out_refs = refs[na:2 * na]
        send_sems, recv_sems, local_sems = refs[2 * na:]
        me = _my_pos()
        sibling = _flip(me, 1)
        chips = [_flip(me, 4), _flip(me, 2), _flip(me, 6)]

        def copy(i, k, block, to, from_x=False):
            dst = out_refs[i].at[_lin(block)]
            return pltpu.make_async_remote_copy(
                src_ref=x_refs[i] if from_x else dst, dst_ref=dst,
                send_sem=send_sems.at[7 * i + k], recv_sem=recv_sems.at[7 * i + k], device_id=to, device_id_type=MESH)

        started = []
        for i in range(na):
            mine = pltpu.make_async_copy(x_refs[i], out_refs[i].at[_lin(me)], local_sems.at[i])
            mine.start()
            started.append(mine)
        sends = []
        for i in range(na):
            cps = [copy(i, 0, me, sibling, True)] + [copy(i, 1 + j, me, chip, True) for j, chip in enumerate(chips)]
            for cp in cps:
                cp.start()
            sends += cps
        for i in range(na):
            for j, chip in enumerate(chips):
                copy(i, 1 + j, chip, me).wait_recv()
                fwd = copy(i, 4 + j, chip, sibling)
                fwd.start()
                sends.append(fwd)
        for i in range(na):
            copy(i, 0, sibling, me).wait_recv()
            for j, chip in enumerate(chips):
                copy(i, 4 + j, _flip(chip, 1), me).wait_recv()
        for cp in sends:
            cp.wait_send()
        for mine in started:
            mine.wait()

    return pl.pallas_call(
        body, name=name,
        out_shape=[jax.ShapeDtypeStruct((NDEV,) + s.shape, s.dtype) for s in shards],
        in_specs=[HBM_SPEC] * na, out_specs=[HBM_SPEC] * na,
        scratch_shapes=[pltpu.SemaphoreType.DMA((7 * na,)), pltpu.SemaphoreType.DMA((7 * na,)),
                        pltpu.SemaphoreType.DMA((na,))],
    )(*shards)


def _all_gather_direct(shard, name):
    def body(x_ref, out_ref, send_sems, recv_sems, local_sem):
        me = _my_pos()
        mine = pltpu.make_async_copy(x_ref, out_ref.at[_lin(me)], local_sem)
        mine.start()
        sends = []
        for k in range(1, NDEV):
            cp = pltpu.make_async_remote_copy(
                src_ref=x_ref, dst_ref=out_ref.at[_lin(me)], send_sem=send_sems.at[k - 1],
                recv_sem=recv_sems.at[k - 1], device_id=_flip(me, k), device_id_type=MESH)
            cp.start()
            sends.append(cp)
        for k in range(1, NDEV):
            peer = _flip(me, k)
            pltpu.make_async_remote_copy(
                src_ref=x_ref, dst_ref=out_ref.at[_lin(peer)], send_sem=send_sems.at[k - 1],
                recv_sem=recv_sems.at[k - 1], device_id=peer, device_id_type=MESH).wait_recv()
        for cp in sends:
            cp.wait_send()
        mine.wait()

    return pl.pallas_call(
        body, name=name,
        out_shape=jax.ShapeDtypeStruct((NDEV,) + shard.shape, shard.dtype),
        in_specs=[HBM_SPEC], out_specs=HBM_SPEC,
        scratch_shapes=[pltpu.SemaphoreType.DMA((7,)), pltpu.SemaphoreType.DMA((7,)), pltpu.SemaphoreType.DMA],
    )(shard)


def _exchange_within_chip(parts, name):
    na = len(parts)

    def body(*refs):
        a_refs = refs[:na]
        recv_refs = refs[na:2 * na]
        send_sems, recv_sems = refs[2 * na:]
        me = _my_pos()
        c = me[2]
        sibling = _flip(me, 1)
        remote = []
        for i in range(na):
            for q in range(NCHIP):
                rc = pltpu.make_async_remote_copy(
                    src_ref=a_refs[i].at[q, 1 - c], dst_ref=recv_refs[i].at[q],
                    send_sem=send_sems.at[NCHIP * i + q], recv_sem=recv_sems.at[NCHIP * i + q],
                    device_id=sibling, device_id_type=MESH)
                rc.start()
                remote.append(rc)
        for rc in remote:
            rc.wait_recv()
        for rc in remote:
            rc.wait_send()

    return pl.pallas_call(
        body, name=name, out_shape=[jax.ShapeDtypeStruct((NCHIP,) + a.shape[2:], a.dtype) for a in parts],
        in_specs=[HBM_SPEC] * na, out_specs=[HBM_SPEC] * na,
        scratch_shapes=[pltpu.SemaphoreType.DMA((NCHIP * na,)), pltpu.SemaphoreType.DMA((NCHIP * na,))],
    )(*parts)


HBM_ONLY = pl.BlockSpec(memory_space=pltpu.HBM)
SEM_SPEC = pl.BlockSpec(memory_space=pltpu.SEMAPHORE)
SPLIT_COPY = pltpu.CompilerParams(has_side_effects=pltpu.SideEffectType.DATAFLOW_SIDE_EFFECTING)


def _chip_peers(me):
    return [_flip(me, 4), _flip(me, 2), _flip(me, 6)]


def _between_chips_start(parts, name):
    na = len(parts)

    def body(*refs):
        a_refs = refs[:na]
        land_refs = refs[na:2 * na]
        send_sems, recv_sems = refs[2 * na], refs[2 * na + 1]
        token = refs[-1]
        me = _my_pos()
        myq = _chip(me)
        for i in range(na):
            for j, peer in enumerate(_chip_peers(me)):
                pltpu.make_async_remote_copy(
                    src_ref=a_refs[i].at[_chip(peer)], dst_ref=land_refs[i].at[myq],
                    send_sem=send_sems.at[3 * i + j], recv_sem=recv_sems.at[3 * i + j],
                    device_id=peer, device_id_type=MESH).start()
        token[...] = jnp.zeros_like(token)

    hbm = [pltpu.HBM(a.shape, a.dtype) for a in parts]
    srcs = [pltpu.with_memory_space_constraint(a, pltpu.HBM) for a in parts]
    lands = [pltpu.with_memory_space_constraint(lax.empty(a.shape, a.dtype), pltpu.HBM) for a in parts]
    res = pl.pallas_call(
        body, name=name,
        out_shape=(pltpu.SemaphoreType.DMA((3 * na,)), pltpu.SemaphoreType.DMA((3 * na,)), *hbm, *hbm,
                   jax.ShapeDtypeStruct((8, LANES), F32)),
        in_specs=[HBM_ONLY] * (2 * na),
        out_specs=(SEM_SPEC, SEM_SPEC, *([HBM_ONLY] * (2 * na)), pl.BlockSpec(memory_space=pltpu.VMEM)),
        input_output_aliases={i: 2 + i for i in range(2 * na)},
        compiler_params=SPLIT_COPY,
    )(*srcs, *lands)
    return res[-1], (res[0], res[1], list(res[2:2 + na]), list(res[2 + na:2 + 2 * na]))


def _between_chips_wait(pending, after, name):
    send_sems, recv_sems, parts, lands = pending
    na = len(parts)

    def body(*refs):
        a_refs = refs[:na]
        land_refs = refs[na:2 * na]
        send_sems, recv_sems = refs[2 * na], refs[2 * na + 1]
        me = _my_pos()
        for i in range(na):
            for j, peer in enumerate(_chip_peers(me)):
                cp = pltpu.make_async_remote_copy(
                    src_ref=a_refs[i].at[_chip(peer)], dst_ref=land_refs[i].at[_chip(peer)],
                    send_sem=send_sems.at[3 * i + j], recv_sem=recv_sems.at[3 * i + j],
                    device_id=peer, device_id_type=MESH)
                cp.wait_send()
                cp.wait_recv()

    hbm = [pltpu.HBM(a.shape, a.dtype) for a in parts]
    res = pl.pallas_call(
        body, name=name, out_shape=(*hbm, *hbm),
        in_specs=[HBM_ONLY] * (2 * na) + [SEM_SPEC, SEM_SPEC, pl.BlockSpec(memory_space=pl.ANY)],
        out_specs=[HBM_ONLY] * (2 * na),
        input_output_aliases={i: i for i in range(2 * na)},
        compiler_params=SPLIT_COPY,
    )(*parts, *lands, send_sems, recv_sems, after)
    return list(res[:na]), list(res[na:])


def _remote(src, dst, send_sems, recv_sems, idx, peer):
    return pltpu.make_async_remote_copy(src_ref=src, dst_ref=dst, send_sem=send_sems.at[idx],
                                        recv_sem=recv_sems.at[idx], device_id=peer, device_id_type=MESH)


def _copies_own(bufs, me):
    return [(bufs[0], bufs[1].at[me[2]], 0, _flip(me, 1))]


def _copies_near(bufs, me):
    return [(bufs[0], bufs[1].at[0, me[2]], 0, _flip(me, 2)), (bufs[0], bufs[1].at[1, me[2]], 1, _flip(me, 4))]


def _copies_far(bufs, me):
    return [(bufs[0], bufs[1].at[me[2]], 0, _flip(me, 6))]


def _copies_others(bufs, me):
    na = len(bufs) // 2
    return [(bufs[i], bufs[na + i].at[_lin(me)], 7 * i + k - 1, _flip(me, k))
            for i in range(na) for k in range(1, NDEV)]


GROUP_COPIES = dict(own=_copies_own, near=_copies_near, far=_copies_far, others=_copies_others)
GROUP_COUNT = dict(own=1, near=2, far=1)


def _gather_start(bufs, groups, after, name):
    nb = len(bufs)
    ng = len(groups)

    def body(*refs):
        b = refs[:nb]
        sems = refs[nb + 1:nb + 1 + 2 * ng]
        token = refs[-1]
        me = _my_pos()
        for gi, (group, idx) in enumerate(groups):
            for src, dst, k, peer in GROUP_COPIES[group]([b[i] for i in idx], me):
                _remote(src, dst, sems[2 * gi], sems[2 * gi + 1], k, peer).start()
        token[...] = jnp.zeros_like(token)

    sem_t = []
    for group, idx in groups:
        cnt = 7 * (len(idx) // 2) if group == "others" else GROUP_COUNT[group]
        sem_t += [pltpu.SemaphoreType.DMA((cnt,)), pltpu.SemaphoreType.DMA((cnt,))]
    ins = [pltpu.with_memory_space_constraint(a, pltpu.HBM) for a in bufs]
    res = pl.pallas_call(
        body, name=name,
        out_shape=(*sem_t, *[pltpu.HBM(a.shape, a.dtype) for a in bufs], jax.ShapeDtypeStruct((8, LANES), F32)),
        in_specs=[HBM_ONLY] * nb + [pl.BlockSpec(memory_space=pl.ANY)],
        out_specs=(*([SEM_SPEC] * (2 * ng)), *([HBM_ONLY] * nb), pl.BlockSpec(memory_space=pltpu.VMEM)),
        input_output_aliases={i: 2 * ng + i for i in range(nb)},
        compiler_params=SPLIT_COPY,
    )(*ins, after)
    return res[-1], list(res[2 * ng:2 * ng + nb]), [(res[2 * gi], res[2 * gi + 1]) for gi in range(ng)]


def _gather_wait(group, send_sems, recv_sems, bufs, after, name):
    nb = len(bufs)
    copies = dict(own=_copies_own, near=_copies_near, far=_copies_far, others=_copies_others)[group]

    def body(*refs):
        b = refs[:nb]
        ss, rs = refs[nb], refs[nb + 1]
        me = _my_pos()
        for src, dst, idx, peer in copies(b, me):
            if group == "others":
                landed = b[nb // 2 + idx // 7].at[_lin(peer)]
            elif group == "own":
                landed = b[1].at[1 - me[2]]
            else:
                landed = dst
            cp = _remote(src, landed, ss, rs, idx, peer)
            cp.wait_send()
            cp.wait_recv()

    res = pl.pallas_call(
        body, name=name, out_shape=[pltpu.HBM(a.shape, a.dtype) for a in bufs],
        in_specs=[HBM_ONLY] * nb + [SEM_SPEC, SEM_SPEC, pl.BlockSpec(memory_space=pl.ANY)],
        out_specs=[HBM_ONLY] * nb,
        input_output_aliases={i: i for i in range(nb)},
        compiler_params=SPLIT_COPY,
    )(*bufs, send_sems, recv_sems, after)
    return list(res)


def _forward_to_sibling(buf, name):
    n = buf.shape[0]

    def body(_in_ref, out_ref, send_sems, recv_sems):
        me = _my_pos()
        c = me[2]
        sibling = _flip(me, 1)
        sends = []
        for r in range(n):
            cp = _remote(out_ref.at[r, c], out_ref.at[r, c], send_sems, recv_sems, r, sibling)
            cp.start()
            sends.append(cp)
        for r in range(n):
            _remote(out_ref.at[r, c], out_ref.at[r, 1 - c], send_sems, recv_sems, r, sibling).wait_recv()
        for cp in sends:
            cp.wait_send()

    return pl.pallas_call(
        body, name=name, out_shape=jax.ShapeDtypeStruct(buf.shape, buf.dtype),
        in_specs=[HBM_SPEC], out_specs=HBM_SPEC,
        scratch_shapes=[pltpu.SemaphoreType.DMA((n,)), pltpu.SemaphoreType.DMA((n,))],
        input_output_aliases={0: 0},
    )(buf)


def _allreduce_small(rep, conv, name):
    def body(rep_hbm, conv_hbm, out_ref, conv_out, stage_r, stage_c, send_sems, recv_sems, local_sems):
        me = _my_pos()
        mi = _lin(me)
        peers = [_flip(me, k) for k in range(1, NDEV)]
        own = [pltpu.make_async_copy(rep_hbm.at[mi], stage_r.at[mi], local_sems.at[0]),
               pltpu.make_async_copy(conv_hbm.at[mi], stage_c.at[mi], local_sems.at[1])]
        for cp in own:
            cp.start()
        sends = []
        for j, peer in enumerate(peers):
            for src, stage, base in ((rep_hbm, stage_r, 0), (conv_hbm, stage_c, 7)):
                cp = _remote(src.at[_lin(peer)], stage.at[mi], send_sems, recv_sems, base + j, peer)
                cp.start()
                sends.append(cp)
        for j, peer in enumerate(peers):
            for src, stage, base in ((rep_hbm, stage_r, 0), (conv_hbm, stage_c, 7)):
                _remote(src.at[mi], stage.at[_lin(peer)], send_sems, recv_sems, base + j, peer).wait_recv()
        for cp in own:
            cp.wait()
        acc_r = stage_r[0]
        acc_c = stage_c[0]
        for q in range(1, NDEV):
            acc_r = acc_r + stage_r[q]
            acc_c = acc_c + stage_c[q]
        out_ref[mi] = acc_r
        conv_out[...] = acc_c
        for j, peer in enumerate(peers):
            cp = _remote(out_ref.at[mi], out_ref.at[mi], send_sems, recv_sems, 14 + j, peer)
            cp.start()
            sends.append(cp)
        for j, peer in enumerate(peers):
            _remote(out_ref.at[mi], out_ref.at[_lin(peer)], send_sems, recv_sems, 14 + j, peer).wait_recv()
        for cp in sends:
            cp.wait_send()

    vmem = pl.BlockSpec(memory_space=pltpu.VMEM)
    return pl.pallas_call(
        body, name=name,
        out_shape=[jax.ShapeDtypeStruct(rep.shape, F32), jax.ShapeDtypeStruct(conv.shape[1:], F32)],
        in_specs=[HBM_SPEC, HBM_SPEC], out_specs=[vmem, vmem],
        scratch_shapes=[pltpu.VMEM(rep.shape, F32), pltpu.VMEM(conv.shape, F32),
                        pltpu.SemaphoreType.DMA((21,)), pltpu.SemaphoreType.DMA((21,)), pltpu.SemaphoreType.DMA((2,))],
    )(rep, conv)


def _exchange_all(parts, name):
    na = len(parts)

    def body(*refs):
        a_refs = refs[:na]
        out_refs = refs[na:2 * na]
        send_sems, recv_sems = refs[2 * na:]
        me = _my_pos()
        sends = []
        for i in range(na):
            for k in range(1, NDEV):
                peer = _flip(me, k)
                cp = pltpu.make_async_remote_copy(
                    src_ref=a_refs[i].at[_lin(peer)], dst_ref=out_refs[i].at[_lin(me)],
                    send_sem=send_sems.at[7 * i + k - 1], recv_sem=recv_sems.at[7 * i + k - 1],
                    device_id=peer, device_id_type=MESH)
                cp.start()
                sends.append(cp)
        for i in range(na):
            for k in range(1, NDEV):
                peer = _flip(me, k)
                pltpu.make_async_remote_copy(
                    src_ref=a_refs[i].at[_lin(peer)], dst_ref=out_refs[i].at[_lin(peer)],
                    send_sem=send_sems.at[7 * i + k - 1], recv_sem=recv_sems.at[7 * i + k - 1],
                    device_id=peer, device_id_type=MESH).wait_recv()
        for cp in sends:
            cp.wait_send()

    return pl.pallas_call(
        body, name=name, out_shape=[jax.ShapeDtypeStruct(a.shape, a.dtype) for a in parts],
        in_specs=[HBM_SPEC] * na, out_specs=[HBM_SPEC] * na,
        scratch_shapes=[pltpu.SemaphoreType.DMA((7 * na,)), pltpu.SemaphoreType.DMA((7 * na,))],
    )(*parts)


def _scalar(v):
    return jnp.asarray(v, jnp.int32).reshape(1)


def _sum_pairs(parts, theirs, name):
    na = len(parts)

    def body(c_ref, *refs):
        for i in range(na):
            o_ref = refs[2 * na + i]
            o_ref[0] = (refs[i][0, 0].astype(F32) + refs[na + i][0].astype(F32)).astype(o_ref.dtype)

    def mine_spec(a):
        return pl.BlockSpec((1, 1) + a.shape[2:], lambda q, c_ref: (q, c_ref[0], 0, 0))

    def spec(a):
        return pl.BlockSpec((1,) + a.shape[1:], lambda q, c_ref: (q, 0, 0))

    return pl.pallas_call(
        body, name=name,
        grid_spec=pltpu.PrefetchScalarGridSpec(
            num_scalar_prefetch=1, grid=(NCHIP,),
            in_specs=[mine_spec(a) for a in parts] + [spec(a) for a in theirs],
            out_specs=[spec(a) for a in theirs]),
        out_shape=[jax.ShapeDtypeStruct(a.shape, a.dtype) for a in theirs],
        compiler_params=_cp(("arbitrary",), VMEM_BIG),
    )(_scalar(lax.axis_index("c")), *parts, *theirs)


def _others(q, mine, nblk=NCHIP):
    return jnp.where(q == mine, (q + 1) % nblk, q)


def _sum_chips_adamw(own, recv, wv, mv, vv, tr, name):
    _, r, w = recv.shape

    def body(q_ref, own_ref, r0, r1, r2, r3, w_ref, m_ref, v_ref, g_ref, d_ref, m2_ref, v2_ref):
        myq = q_ref[0]
        acc = None
        for q, r_ref in enumerate((r0, r1, r2, r3)):
            term = jnp.where(myq == q, own_ref[0], r_ref[0]).astype(F32)
            acc = term if acc is None else acc + term
        g_ref[...] = acc
        delta, m2, v2 = _adam_math(w_ref[...], acc, m_ref[...], v_ref[...])
        d_ref[...] = delta
        m2_ref[...] = m2
        v2_ref[...] = v2

    def recv_spec(q):
        return pl.BlockSpec((1, tr, w), lambda i, q_ref: (_others(q, q_ref[0]), i, 0))

    rows = lambda: pl.BlockSpec((tr, w), lambda i, q_ref: (i, 0))
    shp = jax.ShapeDtypeStruct((r, w), F32)
    return pl.pallas_call(
        body, name=name,
        grid_spec=pltpu.PrefetchScalarGridSpec(
            num_scalar_prefetch=1, grid=(r // tr,),
            in_specs=[pl.BlockSpec((1, tr, w), lambda i, q_ref: (q_ref[0], i, 0))]
            + [recv_spec(q) for q in range(NCHIP)] + [rows(), rows(), rows()],
            out_specs=[rows(), rows(), rows(), rows()]),
        out_shape=[shp, shp, shp, shp],
        compiler_params=_cp(("arbitrary",), VMEM_MID),
    )(_scalar(_chip(_my_pos())), *_hbm(own, recv, recv, recv, recv, wv, mv, vv))


def _sum_blocks_small(own, recv, mine, transpose, name):
    na = len(recv)
    nblk = recv[0].shape[0]

    def body(q_ref, *refs):
        me = q_ref[0]
        for i in range(na):
            acc = None
            for q in range(nblk):
                term = jnp.where(me == q, refs[i][0], refs[na * (1 + q) + i][0]).astype(F32)
                acc = term if acc is None else acc + term
            refs[na * (1 + nblk) + i][...] = acc.T if transpose[i] else acc

    def oshape(a, tr):
        r, w = a.shape[1:]
        return (w, r) if tr else (r, w)

    own_spec = lambda a: pl.BlockSpec((1,) + a.shape[1:], lambda s, q_ref: (q_ref[0], 0, 0))
    recv_spec = lambda a, q: pl.BlockSpec((1,) + a.shape[1:], lambda s, q_ref: (_others(q, q_ref[0], nblk), 0, 0))
    out_spec = lambda shp: pl.BlockSpec(shp, lambda s, q_ref: (0, 0))
    in_specs = [own_spec(a) for a in own]
    for q in range(nblk):
        in_specs += [recv_spec(a, q) for a in recv]
    return pl.pallas_call(
        body, name=name,
        grid_spec=pltpu.PrefetchScalarGridSpec(
            num_scalar_prefetch=1, grid=(1,), in_specs=in_specs,
            out_specs=[out_spec(oshape(a, tr)) for a, tr in zip(recv, transpose)]),
        out_shape=[jax.ShapeDtypeStruct(oshape(a, tr), F32) for a, tr in zip(recv, transpose)],
        compiler_params=_cp(("arbitrary",), VMEM_MID),
    )(_scalar(mine), *own, *(list(recv) * nblk))


def _rep_offsets():
    offs = []
    o = 0
    for r in REP_ROWS:
        offs.append(o)
        o += r
    return offs


LOSS_ROW = REP_TOTAL_ROWS


def _pack_small_grads(g):
    offs = _rep_offsets()

    def body(dwa, dwx, dnm, dbin, dcb, dba, dbx, dlam, dq0, dq1, dq2, dk0, dk1, dk2, dnp, dbpg, loss, o_ref):
        o_ref[pl.ds(REP_TOTAL_ROWS - 2, NDEV * REP_ROWS_DEV - REP_TOTAL_ROWS + 2), :] = jnp.zeros(
            (NDEV * REP_ROWS_DEV - REP_TOTAL_ROWS + 2, LANES), F32)
        o_ref[pl.ds(LOSS_ROW, 1), :] = loss[0:1, :]
        for n in range(NRB):
            o_ref[pl.ds(offs[0] + n * RBW, RBW), :] = dwa[n]
            o_ref[pl.ds(offs[1] + n * RBW, RBW), :] = dwx[n]

        def put_vec(off, ref, rows):
            for k in range(rows):
                o_ref[pl.ds(off + k, 1), :] = ref[:, k * LANES:(k + 1) * LANES]

        put_vec(offs[2], dnm, REP_ROWS[2])
        put_vec(offs[3], dbin, REP_ROWS[3])
        put_vec(offs[4], dcb, REP_ROWS[4])
        put_vec(offs[5], dba, REP_ROWS[5])
        put_vec(offs[6], dbx, REP_ROWS[6])
        put_vec(offs[7], dlam, REP_ROWS[7])
        for k, ref in enumerate((dq0, dq1, dq2)):
            o_ref[pl.ds(offs[8] + k, 1), :] = ref[...]
        for k, ref in enumerate((dk0, dk1, dk2)):
            o_ref[pl.ds(offs[9] + k, 1), :] = ref[...]
        put_vec(offs[10], dnp, REP_ROWS[10])
        put_vec(offs[11], dbpg, REP_ROWS[11])

    args = [g["w_rg_a"], g["w_rg_x"], g["norm_mix"], g["b_in"], g["conv_b"], g["b_rg_a"], g["b_rg_x"],
            g["lru_lambda"], *g["q_norm"], *g["k_norm"], g["norm_ple"], g["b_ple_gate"], g["loss"]]
    full = lambda shp: pl.BlockSpec(shp, lambda: (0,) * len(shp))
    return pl.pallas_call(
        body, name="pack_small_grads",
        in_specs=[full(a.shape) for a in args],
        out_specs=full((NDEV * REP_ROWS_DEV, LANES)),
        out_shape=jax.ShapeDtypeStruct((NDEV * REP_ROWS_DEV, LANES), F32),
    )(*_hbm(*args))


def _adam_math(wv, gv, mv, vv):
    c1 = 1.0 - B1 ** STEP
    c2 = 1.0 - B2 ** STEP
    m2 = B1 * mv + (1.0 - B1) * gv
    v2 = B2 * vv + (1.0 - B2) * (gv * gv)
    delta = (-LR) * ((m2 / c1) / (jnp.sqrt(v2 / c2) + AEPS) + WD * wv)
    return delta, m2, v2


def _adamw_small(rep_flat, w, m, v):
    offs = _rep_offsets()
    n = len(REP_NAMES)

    def body(*refs):
        g_ref = refs[0]
        w_refs = refs[1:1 + n]
        m_refs = refs[1 + n:1 + 2 * n]
        v_refs = refs[1 + 2 * n:1 + 3 * n]
        outs = refs[1 + 3 * n:]
        go, do, mo, vo = outs[:n], outs[n:2 * n], outs[2 * n:3 * n], outs[3 * n:]

        def emit(i, idx, gv):
            go[i][idx] = gv
            delta, m2, v2 = _adam_math(w_refs[i][idx], gv, m_refs[i][idx], v_refs[i][idx])
            do[i][idx] = delta
            mo[i][idx] = m2
            vo[i][idx] = v2

        for i in range(n):
            if i < 2:
                for b in range(NRB):
                    emit(i, b, g_ref[pl.ds(offs[i] + b * RBW, RBW), :])
            elif REP_NAMES[i] in ("q_norm", "k_norm"):
                emit(i, slice(None), g_ref[pl.ds(offs[i], NG), :])
            else:
                gv = jnp.concatenate([g_ref[pl.ds(offs[i] + k, 1), :] for k in range(REP_ROWS[i])], axis=1)
                emit(i, slice(None), gv)

    full = lambda shp: pl.BlockSpec(shp, lambda: (0,) * len(shp))
    pspecs = [full(a.shape) for a in w]
    pshapes = [jax.ShapeDtypeStruct(a.shape, F32) for a in w]
    res = pl.pallas_call(
        body, name="adamw_small",
        in_specs=[full(rep_flat.shape)] + pspecs * 3,
        out_specs=pspecs * 4, out_shape=pshapes * 4,
        compiler_params=_cp(None, VMEM_MID),
    )(*_hbm(rep_flat, *w, *m, *v))
    return res[:n], res[n:2 * n], res[2 * n:3 * n], res[3 * n:]


def _adamw_many(w, g, m, v):
    n = len(w)

    def body(*refs):
        for i in range(n):
            delta, m2, v2 = _adam_math(refs[i][...], refs[n + i][...], refs[2 * n + i][...], refs[3 * n + i][...])
            refs[4 * n + i][...] = delta
            refs[5 * n + i][...] = m2
            refs[6 * n + i][...] = v2

    full = lambda shp: pl.BlockSpec(shp, lambda: (0,) * len(shp))
    specs = [full(a.shape) for a in w]
    shapes = [jax.ShapeDtypeStruct(a.shape, F32) for a in w]
    res = pl.pallas_call(
        body, name="adamw_shards",
        in_specs=specs * 4, out_specs=specs * 3, out_shape=shapes * 3,
        compiler_params=_cp(None, VMEM_MID),
    )(*_hbm(*w, *g, *m, *v))
    return res[:n], res[n:2 * n], res[2 * n:]


def kernel(x, p, norm_mix, w_in, b_in, conv_w, conv_b, w_rg_a, b_rg_a, w_rg_x, b_rg_x, lru_lambda, q_norm, k_norm, w_o_rnn, w_o_att, w_out, norm_ple, w_ple_gate, b_ple_gate, w_ple, loss_target, m_norm_mix, m_w_in, m_b_in, m_conv_w, m_conv_b, m_w_rg_a, m_b_rg_a, m_w_rg_x, m_b_rg_x, m_lru_lambda, m_q_norm, m_k_norm, m_w_o_rnn, m_w_o_att, m_w_out, m_norm_ple, m_w_ple_gate, m_b_ple_gate, m_w_ple, v_norm_mix, v_w_in, v_b_in, v_conv_w, v_conv_b, v_w_rg_a, v_b_rg_a, v_w_rg_x, v_b_rg_x, v_lru_lambda, v_q_norm, v_k_norm, v_w_o_rnn, v_w_o_att, v_w_out, v_norm_ple, v_w_ple_gate, v_b_ple_gate, v_w_ple):
    w = dict(norm_mix=norm_mix, w_in=w_in, b_in=b_in, conv_w=conv_w, conv_b=conv_b, w_rg_a=w_rg_a, b_rg_a=b_rg_a,
             w_rg_x=w_rg_x, b_rg_x=b_rg_x, lru_lambda=lru_lambda, q_norm=q_norm, k_norm=k_norm, w_o_rnn=w_o_rnn,
             w_o_att=w_o_att, w_out=w_out, norm_ple=norm_ple, w_ple_gate=w_ple_gate, b_ple_gate=b_ple_gate,
             w_ple=w_ple)
    m = dict(norm_mix=m_norm_mix, w_in=m_w_in, b_in=m_b_in, conv_w=m_conv_w, conv_b=m_conv_b, w_rg_a=m_w_rg_a,
             b_rg_a=m_b_rg_a, w_rg_x=m_w_rg_x, b_rg_x=m_b_rg_x, lru_lambda=m_lru_lambda, q_norm=m_q_norm,
             k_norm=m_k_norm, w_o_rnn=m_w_o_rnn, w_o_att=m_w_o_att, w_out=m_w_out, norm_ple=m_norm_ple,
             w_ple_gate=m_w_ple_gate, b_ple_gate=m_b_ple_gate, w_ple=m_w_ple)
    v = dict(norm_mix=v_norm_mix, w_in=v_w_in, b_in=v_b_in, conv_w=v_conv_w, conv_b=v_conv_b, w_rg_a=v_w_rg_a,
             b_rg_a=v_b_rg_a, w_rg_x=v_w_rg_x, b_rg_x=v_b_rg_x, lru_lambda=v_lru_lambda, q_norm=v_q_norm,
             k_norm=v_k_norm, w_o_rnn=v_w_o_rnn, w_o_att=v_w_o_att, w_out=v_w_out, norm_ple=v_norm_ple,
             w_ple_gate=v_w_ple_gate, b_ple_gate=v_b_ple_gate, w_ple=v_w_ple)
    names = list(w.keys())

    shards = [w_in[0].T.astype(BF16), w_o_rnn[0].astype(BF16), w_o_att[0].T.astype(BF16), w_out[0].astype(BF16),
              w_ple_gate[0].astype(BF16), w_ple[0].T.astype(BF16), conv_w[0]]
    pos = _my_pos()
    me, my_core, my_chip = _lin(pos), pos[2], _chip(pos)
    hbm_empty = lambda shp, dt: lax.empty(shp, dt)
    w_shard, conv_shard = shards[0], shards[6]
    shp = w_shard.shape
    entry_token, bufs, sems = _gather_start(
        [w_shard, hbm_empty((2,) + shp, BF16), hbm_empty((2, 2) + shp, BF16), conv_shard,
         hbm_empty((NDEV,) + conv_shard.shape, F32)],
        [("own", (0, 1)), ("near", (0, 2)), ("others", (3, 4))], norm_mix, "gather_start_near")
    w_src, own_l, near_l, conv_src, conv_l = bufs
    sem_own, sem_near, sem_conv = sems
    gather_out = {}

    def project(hn):
        w_thru, own = _gather_wait("own", *sem_own, [w_src, own_l], hn, "gather_wait_own")
        own = lax.dynamic_update_slice(own, w_shard[None], (my_core, 0, 0)).reshape(1, CHIP_COLS, D)
        chips = [jnp.stack([my_chip]), jnp.stack([my_chip ^ 1, my_chip ^ 2]), jnp.stack([my_chip ^ 3])]
        chips = [c.astype(jnp.int32) for c in chips]
        proj = _in_proj_chips(hn, own, b_in, chips[0], None, entry_token, "in_proj_own")
        w_thru, near = _gather_wait("near", *sem_near, [w_thru, near_l], proj, "gather_wait_near")
        near = _forward_to_sibling(near, "gather_forward_near")
        token, (w_thru, far_l), (sem_far,) = _gather_start(
            [w_thru, hbm_empty((2,) + shp, BF16)], [("far", (0, 1))], near, "gather_start_far")
        near = near.reshape(2, CHIP_COLS, D)
        proj = _in_proj_chips(hn, near, b_in, chips[1], proj, token, "in_proj_near")
        w_thru, far = _gather_wait("far", *sem_far, [w_thru, far_l], proj, "gather_wait_far")
        far = _forward_to_sibling(far[None], "gather_forward_far").reshape(1, CHIP_COLS, D)
        proj = _in_proj_chips(hn, far, b_in, chips[2], proj, token, "in_proj_far")
        conv_thru, conv_g = _gather_wait("others", *sem_conv, [conv_src, conv_l], proj, "gather_wait_conv")
        conv_g = lax.dynamic_update_slice(conv_g, conv_shard[None], (me, 0, 0))
        conv_f = conv_g.transpose(1, 0, 2).reshape(CONVW, DR)
        srcs = list(shards[1:6])
        token, obufs, (sem_out,) = _gather_start(
            srcs + [hbm_empty((NDEV,) + a.shape, BF16) for a in srcs], [("others", tuple(range(10)))], proj,
            "gather_start_out")
        gather_out.update(bufs=obufs, sems=sem_out)
        return proj, [own, near, far], jnp.concatenate(chips), conv_f, token

    def other_weights(after):
        obufs = _gather_wait("others", *gather_out["sems"], gather_out["bufs"], after, "gather_wait_out")
        full = [lax.dynamic_update_slice(a, s[None], (me, 0, 0)) for a, s in zip(obufs[5:], shards[1:6])]
        return [a.reshape((NDEV * a.shape[1], a.shape[2])) for a in full]

    def start_reduce(arrs, tag):
        parts = [a.reshape((NCHIP, 2, a.shape[0] // NDEV, a.shape[1])) for a in arrs]
        theirs = _exchange_within_chip(parts, "reduce_within_chip_" + tag)
        return _between_chips_start(_sum_pairs(parts, theirs, "sum_pairs_" + tag), "reduce_between_chips_start_" + tag)

    grad_x, pending_out, pending_in, small = _local_step(
        x.reshape(T, D), p.reshape(T, PLE), loss_target.reshape(T, D),
        project, other_weights,
        norm_mix, conv_b, w_rg_a[0], b_rg_a, w_rg_x[0], b_rg_x, lru_lambda, q_norm[0], k_norm[0],
        norm_ple, b_ple_gate, start_reduce, entry_token)

    rep_parts = _pack_small_grads(small).reshape(NDEV, REP_ROWS_DEV, LANES)
    conv_parts = small["conv_w"].reshape(CONVW, NDEV, DR // NDEV).transpose(1, 0, 2)
    rep_all, g_conv = _allreduce_small(rep_parts, conv_parts, "allreduce_small")
    rep_all = rep_all.reshape(NDEV * REP_ROWS_DEV, LANES)
    loss = rep_all[LOSS_ROW, 0]

    myq = _chip(_my_pos())
    own_out, recv_out = _between_chips_wait(pending_out, rep_all, "reduce_between_chips_wait_out")
    own_in, recv_in = _between_chips_wait(pending_in, rep_all, "reduce_between_chips_wait_in")
    w_in_res = _sum_chips_adamw(own_in[0], recv_in[0], w_in[0].T, m_w_in[0].T, v_w_in[0].T, 304, "adamw_w_in")
    g_o_rnn, g_o_att, g_out, g_pg, g_ple = _sum_blocks_small(
        own_out, recv_out, myq, (False, True, False, False, True), "sum_chips_out")

    grad, delta, new_m, new_v = {}, {}, {}, {}
    rep_shape = lambda a: a if a.ndim == 2 else a.reshape(a.shape[1:])
    res = _adamw_small(rep_all, [rep_shape(w[n]) for n in REP_NAMES], [rep_shape(m[n]) for n in REP_NAMES],
                       [rep_shape(v[n]) for n in REP_NAMES])
    for dst, vals in zip((grad, delta, new_m, new_v), res):
        for n, a in zip(REP_NAMES, vals):
            dst[n] = a.reshape(w[n].shape)
    grad["w_in"], delta["w_in"], new_m["w_in"], new_v["w_in"] = [a.T[None] for a in w_in_res]
    rest = ("w_o_rnn", "w_o_att", "w_out", "w_ple_gate", "w_ple", "conv_w")
    g_rest = [g_o_rnn, g_o_att, g_out, g_pg, g_ple, g_conv]
    res = _adamw_many([w[n][0] for n in rest], g_rest, [m[n][0] for n in rest], [v[n][0] for n in rest])
    for n, a in zip(rest, g_rest):
        grad[n] = a[None]
    for dst, vals in zip((delta, new_m, new_v), res):
        for n, a in zip(rest, vals):
            dst[n] = a[None]

    return (loss, grad_x.reshape(BL, S, D), *[grad[n] for n in names], *[delta[n] for n in names],
            *[new_m[n] for n in names], *[new_v[n] for n in names])
```

```python
import jax
import jax.numpy as jnp
from jax import lax
from jax.experimental import pallas as pl
from jax.experimental.pallas import tpu as pltpu

F32 = jnp.float32
BF16 = jnp.bfloat16

D = 1024
S = 2048
BL = 2
T = BL * S
NDEV = 8
NCHIP = 4
PLE = 256
DR = 1280
NRB = 10
RBW = 128
CONVW = 4
LRU_C = 8.0
HD = 128
NH = 4
PATTERNS = ((128, 1), (512, 4), (2048, 16))
NG = 3
ATT = NH * HD
GW = NG * ATT
NIN = 2 * DR + 3 * GW + ATT + 2 * D
OFF_ZR = DR
OFF_Q = 2 * DR
OFF_ZA = OFF_Q + 3 * GW
OFF_G = OFF_ZA + ATT
ROPE_THETA = 10000.0
EPS = 1e-6
SCALE = HD ** -0.5
NEG = -1e30
QB = 128
LANES = 128
CT = 512
NCT = NIN // CT
A_W = 2 * DR
C_W = ATT + 2 * D

LR, B1, B2, AEPS, WD, STEP = 0.001, 0.9, 0.999, 1e-08, 0.01, 10

NSHARD_IN = NIN // NDEV
REP_NAMES = ("w_rg_a", "w_rg_x", "norm_mix", "b_in", "conv_b", "b_rg_a", "b_rg_x", "lru_lambda", "q_norm",
             "k_norm", "norm_ple", "b_ple_gate")
REP_ROWS = (NRB * RBW, NRB * RBW, D // LANES, NIN // LANES, DR // LANES, DR // LANES, DR // LANES, DR // LANES,
            NG, NG, D // LANES, D // LANES)
REP_TOTAL_ROWS = sum(REP_ROWS)
REP_ROWS_DEV = 344
BIG_NAMES = ("w_in", "w_o_rnn", "w_o_att", "w_out", "w_ple_gate", "w_ple")

VMEM_BIG = 56 * 1024 * 1024
VMEM_MID = 40 * 1024 * 1024


def _cp(sem=None, vmem=None):
    return pltpu.CompilerParams(dimension_semantics=sem, vmem_limit_bytes=vmem)


def _hbm(*arrays):
    return [pltpu.with_memory_space_constraint(a, pltpu.HBM) for a in arrays]


def _dot(a, b):
    return jnp.dot(a, b, preferred_element_type=F32)


def _dot_nt(a, b):
    return lax.dot_general(a, b, (((1,), (1,)), ((), ())), preferred_element_type=F32)


def _dot_tn(a, b):
    return lax.dot_general(a, b, (((0,), (0,)), ((), ())), preferred_element_type=F32)


def _sigmoid(x):
    return jax.nn.sigmoid(x)


def _perm(j):
    jq = j - OFF_Q // CT
    inside = (j >= OFF_Q // CT) & (j < OFF_ZA // CT)
    return jnp.where(inside, OFF_Q // CT + (jq % 3) * 3 + jq // 3, j)


PIECES = ((0, A_W // CT), (OFF_Q // CT, GW // CT), (OFF_Q // CT + 3, GW // CT), (OFF_Q // CT + 6, GW // CT),
          (OFF_ZA // CT, C_W // CT))


def _rmsnorm_fwd(x, gain, token, tm=512):
    def body(x_ref, g_ref, _token, o_ref):
        xv = x_ref[...]
        var = jnp.mean(xv * xv, axis=-1, keepdims=True)
        o_ref[...] = (xv * lax.rsqrt(var + EPS) * g_ref[...]).astype(BF16)

    return pl.pallas_call(
        body, grid=(T // tm,), name="rmsnorm_fwd",
        in_specs=[pl.BlockSpec((tm, D), lambda i: (i, 0)), pl.BlockSpec((1, D), lambda i: (0, 0)),
                  pl.BlockSpec((8, LANES), lambda i: (0, 0))],
        out_specs=pl.BlockSpec((tm, D), lambda i: (i, 0)),
        out_shape=jax.ShapeDtypeStruct((T, D), BF16),
        compiler_params=_cp(("parallel",)),
    )(*_hbm(x, gain, token))


CHIP_COLS = NIN // NCHIP


def _in_proj_chips(hn, w_rows, bias, chips, proj, token, name, tm=1024):
    n = w_rows.shape[0]

    def body(chips_ref, a_ref, w_ref, b_ref, _token, *rest):
        o_ref = rest[-1]
        o_ref[...] = (_dot_nt(a_ref[...], w_ref[0]) + b_ref[...]).astype(BF16)

    in_specs = [pl.BlockSpec((tm, D), lambda s, i, ch: (i, 0)),
                pl.BlockSpec((1, CHIP_COLS, D), lambda s, i, ch: (s, 0, 0)),
                pl.BlockSpec((1, CHIP_COLS), lambda s, i, ch: (0, ch[s])),
                pl.BlockSpec((8, LANES), lambda s, i, ch: (0, 0))]
    args = [hn, w_rows, bias, token]
    aliases = {}
    if proj is not None:
        in_specs.append(pl.BlockSpec(memory_space=pl.ANY))
        args.append(proj)
        aliases = {5: 0}
    return pl.pallas_call(
        body, name=name,
        grid_spec=pltpu.PrefetchScalarGridSpec(
            num_scalar_prefetch=1, grid=(n, T // tm), in_specs=in_specs,
            out_specs=pl.BlockSpec((tm, CHIP_COLS), lambda s, i, ch: (i, ch[s]))),
        out_shape=jax.ShapeDtypeStruct((T, NIN), BF16),
        input_output_aliases=aliases,
        compiler_params=_cp(("arbitrary", "arbitrary"), VMEM_BIG),
    )(chips, *_hbm(*args))


def _grad_x(pieces, w_bufs, chips, token, x, dx1, gain, tm=512):
    nb = len(w_bufs)

    def body(chips_ref, a_ref, q_ref, k_ref, v_ref, c_ref, *rest):
        w_hbm = rest[:nb]
        x_ref, dx1_ref, g_ref, dx_ref, dg_ref, w = rest[nb + 1:]
        first = pl.program_id(0) == 0

        @pl.when(first)
        def _():
            s = 0
            for buf in w_hbm:
                for r in range(buf.shape[0]):
                    row = pl.multiple_of(chips_ref[s] * CHIP_COLS, 128)
                    pltpu.sync_copy(buf.at[r], w.at[pl.ds(row, CHIP_COLS), :])
                    s += 1

        acc = _dot(a_ref[...], w[pl.ds(0, A_W), :])
        for kind, p_ref in enumerate((q_ref, k_ref, v_ref)):
            for g in range(NG):
                row = OFF_Q + (3 * g + kind) * CT
                acc = acc + _dot(p_ref[:, g * CT:(g + 1) * CT], w[pl.ds(row, CT), :])
        dn = acc + _dot(c_ref[...], w[pl.ds(OFF_ZA, C_W), :])
        xv = x_ref[...]
        rstd = lax.rsqrt(jnp.mean(xv * xv, axis=-1, keepdims=True) + EPS)
        xh = xv * rstd
        dg = jnp.sum(dn * xh, axis=0, keepdims=True)
        gd = dn * g_ref[...]
        dx_ref[...] = dx1_ref[...] + rstd * (gd - xh * jnp.mean(gd * xh, axis=-1, keepdims=True))

        @pl.when(first)
        def _():
            dg_ref[...] = dg

        @pl.when(jnp.logical_not(first))
        def _():
            dg_ref[...] += dg

    tok = lambda wd: pl.BlockSpec((tm, wd), lambda i, ch: (i, 0))
    vec = lambda: pl.BlockSpec((1, D), lambda i, ch: (0, 0))
    return pl.pallas_call(
        body, name="grad_x",
        grid_spec=pltpu.PrefetchScalarGridSpec(
            num_scalar_prefetch=1, grid=(T // tm,),
            in_specs=[tok(A_W), tok(GW), tok(GW), tok(GW), tok(C_W)] + [pl.BlockSpec(memory_space=pl.ANY)] * nb
            + [pl.BlockSpec((8, LANES), lambda i, ch: (0, 0)), tok(D), tok(D), vec()],
            out_specs=[tok(D), vec()],
            scratch_shapes=[pltpu.VMEM((NIN, D), BF16)]),
        out_shape=[jax.ShapeDtypeStruct((T, D), F32), jax.ShapeDtypeStruct((1, D), F32)],
        compiler_params=_cp(("arbitrary",), VMEM_BIG),
    )(chips, *_hbm(*pieces, *w_bufs, token, x, dx1, gain))


def _dw_in(pieces, hn):
    def body(a_ref, q_ref, k_ref, v_ref, c_ref, h_hbm, o_ref, s_ref, h):
        j = pl.program_id(0)

        @pl.when(j == 0)
        def _():
            pltpu.sync_copy(h_hbm, h)

        def step(x_ref):
            xv = x_ref[...]
            o_ref[...] = _dot_tn(xv, h[...]).astype(BF16)
            s_ref[...] = jnp.sum(xv.astype(F32), axis=0, keepdims=True)

        for x_ref, (lo, n) in zip((a_ref, q_ref, k_ref, v_ref, c_ref), PIECES):
            pl.when((j >= lo) & (j < lo + n))(lambda x_ref=x_ref: step(x_ref))

    def piece_spec(lo, n):
        return pl.BlockSpec((T, CT), lambda j: (0, jnp.clip(j - lo, 0, n - 1)))

    return pl.pallas_call(
        body, grid=(NCT,), name="dw_in",
        in_specs=[piece_spec(lo, n) for lo, n in PIECES] + [pl.BlockSpec(memory_space=pl.ANY)],
        out_specs=[pl.BlockSpec((CT, D), lambda j: (_perm(j), 0)), pl.BlockSpec((1, CT), lambda j: (0, _perm(j)))],
        out_shape=[jax.ShapeDtypeStruct((NIN, D), BF16), jax.ShapeDtypeStruct((1, NIN), F32)],
        scratch_shapes=[pltpu.VMEM((T, D), BF16)],
        compiler_params=_cp(("arbitrary",), VMEM_BIG),
    )(*_hbm(*pieces, hn))


def _mm_tn(a, b, ta, tt, name):
    m = a.shape[1]
    n = b.shape[1]
    nt = T // tt

    def body(a_ref, b_ref, o_ref, acc):
        t = pl.program_id(1)
        p = _dot_tn(a_ref[...].astype(BF16), b_ref[...].astype(BF16))

        @pl.when(t == 0)
        def _():
            acc[...] = p

        @pl.when(t > 0)
        def _():
            acc[...] += p

        @pl.when(t == nt - 1)
        def _():
            o_ref[...] = acc[...].astype(BF16)

    return pl.pallas_call(
        body, grid=(m // ta, nt), name=name,
        in_specs=[pl.BlockSpec((tt, ta), lambda j, t: (t, j)), pl.BlockSpec((tt, n), lambda j, t: (t, 0))],
        out_specs=pl.BlockSpec((ta, n), lambda j, t: (j, 0)),
        out_shape=jax.ShapeDtypeStruct((m, n), BF16),
        scratch_shapes=[pltpu.VMEM((ta, n), F32)],
        compiler_params=_cp(("parallel", "arbitrary"), VMEM_MID),
    )(*_hbm(a, b))


def _row_iota():
    return lax.broadcasted_iota(jnp.int32, (S, RBW), 0)


def _shift_down(v, d, row, fill):
    return jnp.where(row >= d, pltpu.roll(v, d, 0), fill)


def _shift_up(v, d, row, fill):
    return jnp.where(row < S - d, pltpu.roll(v, S - d, 0), fill)


SUBLANES = 8


def _scan_down(a, u, row):
    d = 1
    while d < S:
        last = 2 * d >= S
        if d < SUBLANES:
            u = a * _shift_down(u, d, row, 0.0) + u
            if not last:
                a = a * _shift_down(a, d, row, 1.0)
        else:
            u = jnp.concatenate([u[:d], a[d:] * u[:S - d] + u[d:]], axis=0)
            if not last:
                a = jnp.concatenate([a[:d], a[d:] * a[:S - d]], axis=0)
        d *= 2
    return u


def _scan_up(b, g, row):
    d = 1
    while d < S:
        last = 2 * d >= S
        if d < SUBLANES:
            g = g + b * _shift_up(g, d, row, 0.0)
            if not last:
                b = b * _shift_up(b, d, row, 0.0)
        else:
            g = jnp.concatenate([g[:S - d] + b[:S - d] * g[d:], g[S - d:]], axis=0)
            if not last:
                b = jnp.concatenate([b[:S - d] * b[d:], b[S - d:]], axis=0)
        d *= 2
    return g


def _softplus(x):
    return jnp.maximum(x, 0.0) + jnp.log1p(jnp.exp(-jnp.abs(x)))


def _rnn_gates(x, cw, cb, wa, ba, wx, bx, lam, row):
    xs = [_shift_down(x, j, row, 0.0) for j in (1, 2, 3)]
    xc = cb + cw[3:4, :] * x
    for j in (1, 2, 3):
        xc = xc + cw[3 - j:4 - j, :] * xs[j - 1]
    xcb = xc.astype(BF16)
    r = _sigmoid(_dot(xcb, wa) + ba)
    i = _sigmoid(_dot(xcb, wx) + bx)
    sp = _softplus(-lam)
    log_a = (-LRU_C) * r * sp
    a = jnp.exp(log_a)
    mult = jnp.where(row == 0, 1.0, jnp.sqrt(jnp.tanh(-log_a) * (1.0 + a * a)))
    return xc, xcb, r, i, sp, a, mult, xs


def _rnn_fwd(proj3, conv_w, conv_b, wa, ba, wx, bx, lam, token):
    def body(x_ref, cw_ref, cb_ref, wa_ref, ba_ref, wx_ref, bx_ref, lam_ref, _token, h_ref):
        row = _row_iota()
        x = x_ref[0].astype(F32)
        xc, _, _, i, _, a, mult, _ = _rnn_gates(x, cw_ref[...], cb_ref[...], wa_ref[0], ba_ref[...],
                                             wx_ref[0], bx_ref[...], lam_ref[...], row)
        h_ref[0] = _scan_down(a, mult * (i * xc), row)

    vec = lambda: pl.BlockSpec((1, RBW), lambda b, n: (0, n))
    mat = lambda: pl.BlockSpec((1, RBW, RBW), lambda b, n: (n, 0, 0))
    return pl.pallas_call(
        body, grid=(BL, NRB), name="rnn_fwd",
        in_specs=[pl.BlockSpec((1, S, RBW), lambda b, n: (b, 0, n)),
                  pl.BlockSpec((CONVW, RBW), lambda b, n: (0, n)),
                  vec(), mat(), vec(), mat(), vec(), vec(), pl.BlockSpec((8, LANES), lambda b, n: (0, 0))],
        out_specs=pl.BlockSpec((1, S, RBW), lambda b, n: (b, 0, n)),
        out_shape=jax.ShapeDtypeStruct((BL, S, DR), F32),
        compiler_params=_cp(("parallel", "parallel"), VMEM_MID),
    )(*_hbm(proj3, conv_w, conv_b, wa, ba, wx, bx, lam, token))


def _rnn_bwd(proj3, h3, dh3, slab_a3, conv_w, conv_b, wa, ba, wx, bx, lam, token):
    def body(x_ref, h_ref, dh_ref, cw_ref, cb_ref, wa_ref, ba_ref, wx_ref, bx_ref, lam_ref, _alias, _token,
             dx_ref, dcw_ref, dcb_ref, dwa_ref, dba_ref, dwx_ref, dbx_ref, dlam_ref):
        row = _row_iota()
        x = x_ref[0].astype(F32)
        cw = cw_ref[...]
        wa_v = wa_ref[0]
        wx_v = wx_ref[0]
        lam_v = lam_ref[...]
        xc, xcb, r, i, sp, a, mult, xs = _rnn_gates(x, cw, cb_ref[...], wa_v, ba_ref[...], wx_v, bx_ref[...], lam_v,
                                                    row)
        h = h_ref[0]
        g = _scan_up(_shift_up(a, 1, row, 0.0), dh_ref[0], row)
        da = g * _shift_down(h, 1, row, 0.0)
        dmult = jnp.where(row == 0, 0.0, g * (i * xc))
        gm = g * mult
        di = gm * xc
        dxc = gm * i
        dlog_a = da * a - dmult * (a * a) / mult
        dr = dlog_a * ((-LRU_C) * sp)
        dsp = jnp.sum(dlog_a * ((-LRU_C) * r), axis=0, keepdims=True)
        dlam = dsp * (-_sigmoid(-lam_v))
        dpa = dr * r * (1.0 - r)
        dpx = di * i * (1.0 - i)
        dpab = dpa.astype(BF16)
        dpxb = dpx.astype(BF16)
        dwa = _dot_tn(xcb, dpab)
        dwx = _dot_tn(xcb, dpxb)
        dba = jnp.sum(dpa, axis=0, keepdims=True)
        dbx = jnp.sum(dpx, axis=0, keepdims=True)
        dxc = dxc + _dot_nt(dpab, wa_v) + _dot_nt(dpxb, wx_v)
        dcb = jnp.sum(dxc, axis=0, keepdims=True)
        dx = cw[3:4, :] * dxc
        dcw_rows = [None] * CONVW
        dcw_rows[3] = jnp.sum(dxc * x, axis=0, keepdims=True)
        for j in (1, 2, 3):
            dx = dx + cw[3 - j:4 - j, :] * _shift_up(dxc, j, row, 0.0)
            dcw_rows[3 - j] = jnp.sum(dxc * xs[j - 1], axis=0, keepdims=True)
        dx_ref[0] = dx.astype(BF16)
        dcw = jnp.concatenate(dcw_rows, axis=0)
        first = pl.program_id(1) == 0

        @pl.when(first)
        def _():
            dcw_ref[...] = dcw
            dcb_ref[...] = dcb
            dwa_ref[0] = dwa
            dba_ref[...] = dba
            dwx_ref[0] = dwx
            dbx_ref[...] = dbx
            dlam_ref[...] = dlam

        @pl.when(jnp.logical_not(first))
        def _():
            dcw_ref[...] += dcw
            dcb_ref[...] += dcb
            dwa_ref[0] += dwa
            dba_ref[...] += dba
            dwx_ref[0] += dwx
            dbx_ref[...] += dbx
            dlam_ref[...] += dlam

    slab = lambda: pl.BlockSpec((1, S, RBW), lambda n, b: (b, 0, n))
    vec = lambda: pl.BlockSpec((1, RBW), lambda n, b: (0, n))
    mat = lambda: pl.BlockSpec((1, RBW, RBW), lambda n, b: (n, 0, 0))
    taps = lambda: pl.BlockSpec((CONVW, RBW), lambda n, b: (0, n))
    vshape = jax.ShapeDtypeStruct((1, DR), F32)
    mshape = jax.ShapeDtypeStruct((NRB, RBW, RBW), F32)
    return pl.pallas_call(
        body, grid=(NRB, BL), name="rnn_bwd",
        in_specs=[slab(), slab(), slab(), taps(), vec(), mat(), vec(), mat(), vec(), vec(),
                  pl.BlockSpec(memory_space=pl.ANY), pl.BlockSpec((8, LANES), lambda n, b: (0, 0))],
        out_specs=[slab(), taps(), vec(), mat(), vec(), mat(), vec(), vec()],
        out_shape=[jax.ShapeDtypeStruct((BL, S, A_W), BF16), jax.ShapeDtypeStruct((CONVW, DR), F32),
                   vshape, mshape, vshape, mshape, vshape, vshape],
        input_output_aliases={10: 0},
        compiler_params=_cp(("parallel", "arbitrary"), 48 * 1024 * 1024),
    )(*_hbm(proj3, h3, dh3, conv_w, conv_b, wa, ba, wx, bx, lam, slab_a3, token))


NQB = S // QB


def _rms_head(t, gain):
    rstd = lax.rsqrt(jnp.mean(t * t, axis=-1, keepdims=True) + EPS)
    return t * rstd * gain


def _rope(t, cs, sn):
    return t * cs + pltpu.roll(t, HD // 2, 1) * sn


def _rope_t(dy, cs, sn):
    return dy * cs - pltpu.roll(dy, HD // 2, 1) * sn


def _bdot_nt(a, b):
    return lax.dot_general(a, b, (((2,), (2,)), ((0,), (0,))), preferred_element_type=F32)


def _bdot(a, b):
    return lax.dot_general(a, b, (((2,), (1,)), ((0,), (0,))), preferred_element_type=F32)


def _bdot_tn(a, b):
    return lax.dot_general(a, b, (((1,), (1,)), ((0,), (0,))), preferred_element_type=F32)


STRIDE_MAX = 4


def _permute(buf, x, dil, dst, off=0):
    ln = S // dil
    if dil == 1:
        dst[pl.ds(off, S), :] = x.astype(dst.dtype)
        return
    buf[0] = x
    if dil <= STRIDE_MAX:
        for c in range(dil):
            dst[pl.ds(off + c * ln, ln), :] = buf.at[0][pl.ds(c, ln, stride=dil), :].astype(dst.dtype)
        return
    f, r = STRIDE_MAX, dil // STRIDE_MAX
    part = S // f
    for c1 in range(f):
        buf.at[1][pl.ds(c1 * part, part), :] = buf.at[0][pl.ds(c1, part, stride=f), :]
    for c1 in range(f):
        for c2 in range(r):
            dst[pl.ds(off + (c1 + f * c2) * ln, ln), :] = (
                buf.at[1][pl.ds(c1 * part + c2, ln, stride=r), :].astype(dst.dtype))


def _unpermute(buf, xp, dil, dst):
    ln = S // dil
    if dil == 1:
        dst[...] = xp
        return
    if dil <= STRIDE_MAX:
        for c in range(dil):
            dst[pl.ds(c, ln, stride=dil), :] = xp[c * ln:(c + 1) * ln]
        return
    f, r = STRIDE_MAX, dil // STRIDE_MAX
    part = S // f
    for c1 in range(f):
        for c2 in range(r):
            c = c1 + f * c2
            buf.at[1][pl.ds(c1 * part + c2, ln, stride=r), :] = xp[c * ln:(c + 1) * ln]
    for c1 in range(f):
        dst[pl.ds(c1, part, stride=f), :] = buf[1, pl.ds(c1 * part, part), :]


def _blocks3(ref, off=0):
    return ref[pl.ds(off, S), :].reshape(NQB, QB, HD)


def _att_prep(q_ref, k_ref, v_ref, cos_ref, sin_ref, qn, kn, dil, nat, qs, ksp, vsp):
    cs = cos_ref[...]
    sn = sin_ref[...]
    zero = jnp.zeros((QB, HD), BF16)
    ksp[pl.ds(0, QB), :] = zero
    vsp[pl.ds(0, QB), :] = zero
    _permute(nat, _rope(_rms_head(q_ref[0].astype(F32), qn), cs, sn), dil, qs)
    _permute(nat, _rope(_rms_head(k_ref[0].astype(F32), kn), cs, sn), dil, ksp, QB)
    _permute(nat, v_ref[0].astype(F32), dil, vsp, QB)


def _att_scores(qs, ksp, dil):
    nb = S // dil // QB
    q3 = _blocks3(qs)
    shape = (NQB, QB, QB)
    qi = lax.broadcasted_iota(jnp.int32, shape, 1)
    kj = lax.broadcasted_iota(jnp.int32, shape, 2)
    s_c = jnp.where(qi >= kj, _bdot_nt(q3, _blocks3(ksp, QB)) * SCALE, NEG)
    if nb == 1:
        return q3, s_c, None
    jj = lax.broadcasted_iota(jnp.int32, shape, 0)
    ok = (kj >= qi) & ((jj & (nb - 1)) != 0)
    s_p = jnp.where(ok, _bdot_nt(q3, _blocks3(ksp)) * SCALE, NEG)
    return q3, s_c, s_p


def _qkv_spec(kind, g):
    base = OFF_Q // HD + (3 * g + kind) * NH
    return pl.BlockSpec((1, S, HD), lambda b, h: (b, 0, base + h))


def _attn_fwd(proj3, cos_t, sin_t, q_norm, k_norm):
    def body(*refs):
        qkv_refs = refs[:9]
        cos_ref, sin_ref, qn_ref, kn_ref, att_ref, lse_ref, w_ref, nat, qs, ksp, vsp, og = refs[9:]
        for g, (window, dil) in enumerate(PATTERNS):
            q_ref, k_ref, v_ref = qkv_refs[3 * g:3 * g + 3]
            _att_prep(q_ref, k_ref, v_ref, cos_ref, sin_ref, qn_ref[g:g + 1, :], kn_ref[g:g + 1, :], dil,
                      nat, qs, ksp, vsp)
            _, s_c, s_p = _att_scores(qs, ksp, dil)
            m = jnp.max(s_c, axis=-1, keepdims=True)
            if s_p is not None:
                m = jnp.maximum(m, jnp.max(s_p, axis=-1, keepdims=True))
            e_c = jnp.exp(s_c - m)
            den = jnp.sum(e_c, axis=-1, keepdims=True)
            o = _bdot(e_c.astype(BF16), _blocks3(vsp, QB))
            if s_p is not None:
                e_p = jnp.exp(s_p - m)
                den = den + jnp.sum(e_p, axis=-1, keepdims=True)
                o = o + _bdot(e_p.astype(BF16), _blocks3(vsp))
            _unpermute(nat, (o / den).reshape(S, HD), dil, og.at[g])
            _unpermute(nat, jnp.broadcast_to(m + jnp.log(den), (NQB, QB, HD)).reshape(S, HD), dil,
                       lse_ref.at[g, 0])
        l0 = lse_ref[0, 0]
        l1 = lse_ref[1, 0]
        l2 = lse_ref[2, 0]
        mx = jnp.maximum(jnp.maximum(l0, l1), l2)
        e0 = jnp.exp(l0 - mx)
        e1 = jnp.exp(l1 - mx)
        e2 = jnp.exp(l2 - mx)
        inv = 1.0 / (e0 + e1 + e2)
        w0 = e0 * inv
        w1 = e1 * inv
        w2 = e2 * inv
        w_ref[0, 0] = w0
        w_ref[1, 0] = w1
        w_ref[2, 0] = w2
        att_ref[0] = w0 * og[0] + w1 * og[1] + w2 * og[2]

    in_specs = [_qkv_spec(kind, g) for g in range(NG) for kind in range(3)]
    in_specs += [pl.BlockSpec((S, HD), lambda b, h: (0, 0)), pl.BlockSpec((S, HD), lambda b, h: (0, 0)),
                 pl.BlockSpec((NG, HD), lambda b, h: (0, 0)), pl.BlockSpec((NG, HD), lambda b, h: (0, 0))]
    stat = lambda: pl.BlockSpec((NG, 1, S, HD), lambda b, h: (0, b, 0, h))
    return pl.pallas_call(
        body, grid=(BL, NH), name="attn_fwd",
        in_specs=in_specs,
        out_specs=[pl.BlockSpec((1, S, HD), lambda b, h: (b, 0, h)), stat(), stat()],
        out_shape=[jax.ShapeDtypeStruct((BL, S, ATT), F32),
                   jax.ShapeDtypeStruct((NG, BL, S, ATT), F32),
                   jax.ShapeDtypeStruct((NG, BL, S, ATT), F32)],
        scratch_shapes=[pltpu.VMEM((2, S, HD), F32), pltpu.VMEM((S, HD), BF16), pltpu.VMEM((S + QB, HD), BF16),
                        pltpu.VMEM((S + QB, HD), BF16), pltpu.VMEM((NG, S, HD), F32)],
        compiler_params=_cp(("parallel", "parallel"), VMEM_BIG),
    )(*_hbm(*([proj3] * 9), cos_t, sin_t, q_norm, k_norm))


def _attn_bwd_group(g, proj3, cos_t, sin_t, qn_g, kn_g, lse, wts, datt3, sbar3, slabs):
    dil = PATTERNS[g][1]
    n_alias = 0 if slabs is None else 3

    def norm_rope_bwd(dpost, raw, gain, cs, sn):
        dn = _rope_t(dpost, cs, sn)
        rstd = lax.rsqrt(jnp.mean(raw * raw, axis=-1, keepdims=True) + EPS)
        xh = raw * rstd
        dgain = jnp.sum(dn * xh, axis=0, keepdims=True)
        gd = dn * gain
        draw = rstd * (gd - xh * jnp.mean(gd * xh, axis=-1, keepdims=True))
        return draw, dgain

    def body(*refs):
        (q_ref, k_ref, v_ref, cos_ref, sin_ref, qn_ref, kn_ref, lse_ref, w_ref, datt_ref, sbar_ref) = refs[:11]
        (dq_ref, dk_ref, dv_ref, dqn_ref, dkn_ref, nat, qs, ksp, vsp, dos, cvp, lsp, acc) = refs[11 + n_alias:]
        qn = qn_ref[...]
        kn = kn_ref[...]
        cs = cos_ref[...]
        sn = sin_ref[...]
        _att_prep(q_ref, k_ref, v_ref, cos_ref, sin_ref, qn, kn, dil, nat, qs, ksp, vsp)
        wv = w_ref[0, 0]
        _permute(nat, wv * datt_ref[0], dil, dos)
        _permute(nat, wv * sbar_ref[0], dil, cvp)
        _permute(nat, lse_ref[0, 0], dil, lsp)
        q3, s_c, s_p = _att_scores(qs, ksp, dil)
        do3 = _blocks3(dos)
        lse3 = _blocks3(lsp)[:, :, 0:1]
        cv3 = _blocks3(cvp)[:, :, 0:1]
        p_c = jnp.exp(s_c - lse3)
        ds_c = (p_c * (_bdot_nt(do3, _blocks3(vsp, QB)) - cv3)).astype(BF16)
        dq = _bdot(ds_c, _blocks3(ksp, QB))
        acc[0] = _bdot_tn(ds_c, q3).reshape(S, HD)
        acc[1] = _bdot_tn(p_c.astype(BF16), do3).reshape(S, HD)
        if s_p is not None:
            p_p = jnp.exp(s_p - lse3)
            ds_p = (p_p * (_bdot_nt(do3, _blocks3(vsp)) - cv3)).astype(BF16)
            dq = dq + _bdot(ds_p, _blocks3(ksp))
            early = pl.ds(0, S - QB)
            acc[0, early, :] += _bdot_tn(ds_p, q3).reshape(S, HD)[QB:]
            acc[1, early, :] += _bdot_tn(p_p.astype(BF16), do3).reshape(S, HD)[QB:]
        _unpermute(nat, (dq * SCALE).reshape(S, HD), dil, nat.at[0])
        draw, dqn = norm_rope_bwd(nat[0], q_ref[0].astype(F32), qn, cs, sn)
        dq_ref[0] = draw.astype(BF16)
        _unpermute(nat, acc[0] * SCALE, dil, nat.at[0])
        draw, dkn = norm_rope_bwd(nat[0], k_ref[0].astype(F32), kn, cs, sn)
        dk_ref[0] = draw.astype(BF16)
        _unpermute(nat, acc[1], dil, nat.at[0])
        dv_ref[0] = nat[0].astype(BF16)
        first = (pl.program_id(0) == 0) & (pl.program_id(1) == 0)

        @pl.when(first)
        def _():
            dqn_ref[...] = dqn
            dkn_ref[...] = dkn

        @pl.when(jnp.logical_not(first))
        def _():
            dqn_ref[...] += dqn
            dkn_ref[...] += dkn

    full = lambda r: pl.BlockSpec((r, HD), lambda b, h: (0, 0))
    stat = lambda: pl.BlockSpec((1, 1, S, HD), lambda b, h: (g, b, 0, h))
    slab = lambda: pl.BlockSpec((1, S, HD), lambda b, h: (b, 0, h))
    out_slab = lambda: pl.BlockSpec((1, S, HD), lambda b, h: (b, 0, g * NH + h))
    big = jax.ShapeDtypeStruct((BL, S, GW), BF16)
    vecs = jax.ShapeDtypeStruct((1, HD), F32)
    in_specs = [_qkv_spec(0, g), _qkv_spec(1, g), _qkv_spec(2, g), full(S), full(S), full(1), full(1),
                stat(), stat(), slab(), slab()]
    args = [proj3, proj3, proj3, cos_t, sin_t, qn_g, kn_g, lse, wts, datt3, sbar3]
    aliases = {}
    if slabs is not None:
        in_specs += [pl.BlockSpec(memory_space=pl.ANY)] * 3
        args += list(slabs)
        aliases = {11: 0, 12: 1, 13: 2}
    return pl.pallas_call(
        body, grid=(BL, NH), name="attn_bwd_g%d" % g,
        in_specs=in_specs,
        out_specs=[out_slab(), out_slab(), out_slab(), full(1), full(1)],
        out_shape=[big, big, big, vecs, vecs],
        scratch_shapes=[pltpu.VMEM((2, S, HD), F32), pltpu.VMEM((S, HD), BF16), pltpu.VMEM((S + QB, HD), BF16),
                        pltpu.VMEM((S + QB, HD), BF16), pltpu.VMEM((S, HD), BF16), pltpu.VMEM((S, HD), F32),
                        pltpu.VMEM((S, HD), F32), pltpu.VMEM((2, S, HD), F32)],
        input_output_aliases=aliases,
        compiler_params=_cp(("arbitrary", "arbitrary"), VMEM_BIG),
    )(*_hbm(*args))


def _tail(x, proj, h, att, p, tgt, w_o_rnn, w_o_att_t, w_out, w_pg, w_ple_t, norm_ple, b_pg, tm=256):
    nt = T // tm
    inv_d = 1.0 / D

    def body(x_ref, h_ref, zr_ref, att_ref, za_ref, g0a_ref, g0b_ref, g1a_ref, g1b_ref, p_ref, tgt_ref,
             np_ref, bpg_ref, wor_hbm, woa_hbm, wout_hbm, wpg_hbm, wple_hbm,
             dx1_ref, merged_ref, n1_ref, dpre_ref, dpe_ref, dyr_ref, dya_ref, slab_a_ref, slab_c_ref, dh_ref,
             datt_ref, sbar_ref, yrnn_ref, yatt_ref, loss_ref, dnp_ref, dbpg_ref,
             wor, woa, wout, wpg, wple):
        first = pl.program_id(0) == 0

        @pl.when(first)
        def _():
            pltpu.sync_copy(wor_hbm, wor)
            pltpu.sync_copy(woa_hbm, woa)
            pltpu.sync_copy(wout_hbm, wout)
            pltpu.sync_copy(wpg_hbm, wpg)
            pltpu.sync_copy(wple_hbm, wple)

        xv = x_ref[...]
        hv = h_ref[...]
        zr = zr_ref[...].astype(F32)
        av = att_ref[...]
        za = za_ref[...].astype(F32)
        szr = _sigmoid(zr)
        silu_r = zr * szr
        yrnn_b = (hv * silu_r).astype(BF16)
        sza = _sigmoid(za)
        silu_a = za * sza
        yatt_b = (av * silu_a).astype(BF16)
        yrnn_ref[...] = yrnn_b
        yatt_ref[...] = yatt_b
        yr = _dot(yrnn_b, wor[...])
        ya = _dot_nt(yatt_b, woa[...])
        g0 = _sigmoid(jnp.concatenate([g0a_ref[...], g0b_ref[...]], axis=1).astype(F32))
        g1 = _sigmoid(jnp.concatenate([g1a_ref[...], g1b_ref[...]], axis=1).astype(F32))
        merged_b = (g0 * yr + g1 * ya).astype(BF16)
        merged_ref[...] = merged_b
        x1 = xv + _dot(merged_b, wout[...])
        rstd = lax.rsqrt(jnp.mean(x1 * x1, axis=-1, keepdims=True) + EPS)
        xh = x1 * rstd
        npl = np_ref[...]
        n1_b = (xh * npl).astype(BF16)
        n1_ref[...] = n1_b
        pg = _sigmoid(_dot(n1_b, wpg[...]) + bpg_ref[...])
        pe = _dot_nt(p_ref[...].astype(BF16), wple[...])
        err = x1 + pg * pe - tgt_ref[...]
        loss_t = 0.5 * inv_d * jnp.sum(err * err)
        dy = err * inv_d
        dpe_ref[...] = (dy * pg).astype(BF16)
        dpre = dy * pe * pg * (1.0 - pg)
        dpre_b = dpre.astype(BF16)
        dpre_ref[...] = dpre_b
        dn1 = _dot_nt(dpre_b, wpg[...])
        dnp = jnp.sum(dn1 * xh, axis=0, keepdims=True)
        dbpg = jnp.sum(dpre, axis=0, keepdims=True)
        gd = dn1 * npl
        dx1 = dy + rstd * (gd - xh * jnp.mean(gd * xh, axis=-1, keepdims=True))
        dx1_ref[...] = dx1
        dmerged = _dot_nt(dx1.astype(BF16), wout[...])
        dyr_b = (dmerged * g0).astype(BF16)
        dya_b = (dmerged * g1).astype(BF16)
        dyr_ref[...] = dyr_b
        dya_ref[...] = dya_b
        slab_c_ref[:, ATT:ATT + D] = (dmerged * yr * g0 * (1.0 - g0)).astype(BF16)
        slab_c_ref[:, ATT + D:ATT + 2 * D] = (dmerged * ya * g1 * (1.0 - g1)).astype(BF16)
        dyrnn = _dot_nt(dyr_b, wor[...])
        dyatt = _dot(dya_b, woa[...])
        dh_ref[...] = dyrnn * silu_r
        slab_a_ref[...] = (dyrnn * hv * szr * (1.0 + zr * (1.0 - szr))).astype(BF16)
        datt = dyatt * silu_a
        datt_ref[...] = datt
        slab_c_ref[:, 0:ATT] = (dyatt * av * sza * (1.0 + za * (1.0 - sza))).astype(BF16)
        da = datt * av
        for hh in range(NH):
            seg = slice(hh * HD, (hh + 1) * HD)
            sbar_ref[:, seg] = jnp.broadcast_to(jnp.sum(da[:, seg], axis=-1, keepdims=True), (tm, HD))

        @pl.when(first)
        def _():
            loss_ref[...] = jnp.full((8, LANES), loss_t, F32)
            dnp_ref[...] = dnp
            dbpg_ref[...] = dbpg

        @pl.when(jnp.logical_not(first))
        def _():
            loss_ref[...] += jnp.full((8, LANES), loss_t, F32)
            dnp_ref[...] += dnp
            dbpg_ref[...] += dbpg

    tok = lambda w: pl.BlockSpec((tm, w), lambda i: (i, 0))
    col = lambda w, blk: pl.BlockSpec((tm, w), lambda i: (i, blk))
    vec = lambda: pl.BlockSpec((1, D), lambda i: (0, 0))
    hbm = lambda: pl.BlockSpec(memory_space=pl.ANY)
    gb = OFF_G // 512
    in_specs = [tok(D), tok(DR), col(DR, 1), tok(ATT), col(ATT, OFF_ZA // ATT),
                col(512, gb), col(512, gb + 1), col(512, gb + 2), col(512, gb + 3),
                tok(PLE), tok(D), vec(), vec(), hbm(), hbm(), hbm(), hbm(), hbm()]
    sh = lambda w, dt: jax.ShapeDtypeStruct((T, w), dt)
    out_shape = [sh(D, F32), sh(D, BF16), sh(D, BF16), sh(D, BF16), sh(D, BF16), sh(D, BF16), sh(D, BF16),
                 sh(A_W, BF16), sh(C_W, BF16), sh(DR, F32), sh(ATT, F32), sh(ATT, F32),
                 sh(DR, BF16), sh(ATT, BF16),
                 jax.ShapeDtypeStruct((8, LANES), F32), jax.ShapeDtypeStruct((1, D), F32),
                 jax.ShapeDtypeStruct((1, D), F32)]
    out_specs = [tok(D), tok(D), tok(D), tok(D), tok(D), tok(D), tok(D), col(DR, 1), tok(C_W), tok(DR),
                 tok(ATT), tok(ATT), tok(DR), tok(ATT),
                 pl.BlockSpec((8, LANES), lambda i: (0, 0)), vec(), vec()]
    return pl.pallas_call(
        body, grid=(nt,), name="tail_fwd_bwd",
        in_specs=in_specs, out_specs=out_specs, out_shape=out_shape,
        scratch_shapes=[pltpu.VMEM((DR, D), BF16), pltpu.VMEM((D, ATT), BF16), pltpu.VMEM((D, D), BF16),
                        pltpu.VMEM((D, D), BF16), pltpu.VMEM((D, PLE), BF16)],
        compiler_params=_cp(("arbitrary",), VMEM_BIG),
    )(*_hbm(x, h, proj, att, proj, proj, proj, proj, proj, p, tgt, norm_ple, b_pg, w_o_rnn, w_o_att_t, w_out, w_pg,
            w_ple_t))


def _input_norm_bwd(x, dhn, dx1, gain, tm=512):
    def body(x_ref, dhn_ref, dx1_ref, g_ref, dx_ref, dg_ref):
        xv = x_ref[...]
        rstd = lax.rsqrt(jnp.mean(xv * xv, axis=-1, keepdims=True) + EPS)
        xh = xv * rstd
        dn = dhn_ref[...]
        dg = jnp.sum(dn * xh, axis=0, keepdims=True)
        gd = dn * g_ref[...]
        dx_ref[...] = dx1_ref[...] + rstd * (gd - xh * jnp.mean(gd * xh, axis=-1, keepdims=True))
        first = pl.program_id(0) == 0

        @pl.when(first)
        def _():
            dg_ref[...] = dg

        @pl.when(jnp.logical_not(first))
        def _():
            dg_ref[...] += dg

    tok = lambda: pl.BlockSpec((tm, D), lambda i: (i, 0))
    vec = lambda: pl.BlockSpec((1, D), lambda i: (0, 0))
    return pl.pallas_call(
        body, grid=(T // tm,), name="input_norm_bwd",
        in_specs=[tok(), tok(), tok(), vec()], out_specs=[tok(), vec()],
        out_shape=[jax.ShapeDtypeStruct((T, D), F32), jax.ShapeDtypeStruct((1, D), F32)],
        compiler_params=_cp(("arbitrary",), VMEM_MID),
    )(*_hbm(x, dhn, dx1, gain))


def _rope_tables():
    pos = jnp.arange(S, dtype=F32)
    inv_freq = ROPE_THETA ** (-jnp.arange(0, HD, 2, dtype=F32) / HD)
    ang = pos[:, None] * inv_freq[None, :]
    cos, sin = jnp.cos(ang), jnp.sin(ang)
    return jnp.concatenate([cos, cos], axis=1), jnp.concatenate([-sin, sin], axis=1)


def _local_step(x, p, tgt, project, other_weights, norm_mix, conv_b,
                w_rg_a, b_rg_a, w_rg_x, b_rg_x, lam, q_norm, k_norm, norm_ple, b_pg, start_reduce=None,
                entry_token=None):
    if start_reduce is None:
        start_reduce = lambda arrs, tag: (jnp.zeros((8, LANES), F32), arrs)
    if entry_token is None:
        entry_token = jnp.zeros((8, LANES), F32)
    cos_t, sin_t = _rope_tables()
    wa_b = w_rg_a.astype(BF16)
    wx_b = w_rg_x.astype(BF16)

    hn = _rmsnorm_fwd(x, norm_mix, entry_token)
    proj, w_bufs, chips, conv_w, token = project(hn)
    proj3 = proj.reshape(BL, S, NIN)
    h3 = _rnn_fwd(proj3, conv_w, conv_b, wa_b, b_rg_a, wx_b, b_rg_x, lam, token)
    att3, lse, wts = _attn_fwd(proj3, cos_t, sin_t, q_norm, k_norm)
    w_o_rnn, w_o_att_t, w_out, w_pg, w_ple_t = other_weights(att3)
    (dx1, merged, n1, dpre, dpe, dyr, dya, slab_a, slab_c, dh, datt, sbar, yrnn, yatt, loss8, dnp, dbpg) = _tail(
        x, proj, h3.reshape(T, DR), att3.reshape(T, ATT), p, tgt, w_o_rnn, w_o_att_t, w_out, w_pg, w_ple_t,
        norm_ple, b_pg)

    token, pending_out = start_reduce([
        _mm_tn(yrnn, dyr, 640, 2048, "dw_o_rnn"),
        _mm_tn(dya, yatt, 512, 2048, "dw_o_att_t"),
        _mm_tn(merged, dx1, 512, 2048, "dw_out"),
        _mm_tn(n1, dpre, 512, 2048, "dw_ple_gate"),
        _mm_tn(dpe, p, 512, 2048, "dw_ple_t")], "out")

    slab_a3, dcw, dcb, dwa, dba, dwx, dbx, dlam = _rnn_bwd(
        proj3, h3, dh.reshape(BL, S, DR), slab_a.reshape(BL, S, A_W), conv_w, conv_b, wa_b, b_rg_a, wx_b, b_rg_x, lam,
        token)
    datt3 = datt.reshape(BL, S, ATT)
    sbar3 = sbar.reshape(BL, S, ATT)
    slabs = None
    dqn = []
    dkn = []
    for g in range(NG):
        dq, dk, dv, dqn_g, dkn_g = _attn_bwd_group(g, proj3, cos_t, sin_t, q_norm[g:g + 1], k_norm[g:g + 1],
                                                   lse, wts, datt3, sbar3, slabs)
        slabs = (dq, dk, dv)
        dqn.append(dqn_g)
        dkn.append(dkn_g)
    pieces = [slab_a3.reshape(T, A_W)] + [t.reshape(T, GW) for t in slabs] + [slab_c]
    dw_in_t, db_in = _dw_in(pieces, hn)
    token, pending_in = start_reduce([dw_in_t], "in")
    grad_x, dnm = _grad_x(pieces, w_bufs, chips, token, x, dx1, norm_mix)

    small = dict(w_rg_a=dwa, w_rg_x=dwx, norm_mix=dnm, b_in=db_in, conv_b=dcb, b_rg_a=dba, b_rg_x=dbx,
                 lru_lambda=dlam, q_norm=dqn, k_norm=dkn, norm_ple=dnp, b_ple_gate=dbpg, conv_w=dcw, loss=loss8)
    return grad_x, pending_out, pending_in, small


MESH = pl.DeviceIdType.MESH
HBM_SPEC = pl.BlockSpec(memory_space=pl.ANY)


def _my_pos():
    return lax.axis_index("x"), lax.axis_index("y"), lax.axis_index("c")


def _flip(pos, k):
    x, y, c = pos
    return (1 - x if k & 4 else x, 1 - y if k & 2 else y, 1 - c if k & 1 else c)


def _lin(pos):
    return 4 * pos[0] + 2 * pos[1] + pos[2]


def _chip(pos):
    return 2 * pos[0] + pos[1]


def _all_gather_two_level(shards, name):
    na = len(shards)

    def body(*refs):
        x_refs = refs[:na]
        out_refs = refs[na:2 * na]
        send_sems, recv_sems, local_sems = refs[2 * na:]
        me = _my_pos()
        sibling = _flip(me, 1)
        chips = [_flip(me, 4), _flip(me, 2), _flip(me, 6)]

        def copy(i, k, block, to, from_x=False):
            dst = out_refs[i].at[_lin(block)]
            return pltpu.make_async_remote_copy(
                src_ref=x_refs[i] if from_x else dst, dst_ref=dst,
                send_sem=send_sems.at[7 * i + k], recv_sem=recv_sems.at[7 * i + k], device_id=to, device_id_type=MESH)

        started = []
        for i in range(na):
            mine = pltpu.make_async_copy(x_refs[i], out_refs[i].at[_lin(me)], local_sems.at[i])
            mine.start()
            started.append(mine)
        sends = []
        for i in range(na):
            cps = [copy(i, 0, me, sibling, True)] + [copy(i, 1 + j, me, chip, True) for j, chip in enumerate(chips)]
            for cp in cps:
                cp.start()
            sends += cps
        for i in range(na):
            for j, chip in enumerate(chips):
                copy(i, 1 + j, chip, me).wait_recv()
                fwd = copy(i, 4 + j, chip, sibling)
                fwd.start()
                sends.append(fwd)
        for i in range(na):
            copy(i, 0, sibling, me).wait_recv()
            for j, chip in enumerate(chips):
                copy(i, 4 + j, _flip(chip, 1), me).wait_recv()
        for cp in sends:
            cp.wait_send()
        for mine in started:
            mine.wait()

    return pl.pallas_call(
        body, name=name,
        out_shape=[jax.ShapeDtypeStruct((NDEV,) + s.shape, s.dtype) for s in shards],
        in_specs=[HBM_SPEC] * na, out_specs=[HBM_SPEC] * na,
        scratch_shapes=[pltpu.SemaphoreType.DMA((7 * na,)), pltpu.SemaphoreType.DMA((7 * na,)),
                        pltpu.SemaphoreType.DMA((na,))],
    )(*shards)


def _all_gather_direct(shard, name):
    def body(x_ref, out_ref, send_sems, recv_sems, local_sem):
        me = _my_pos()
        mine = pltpu.make_async_copy(x_ref, out_ref.at[_lin(me)], local_sem)
        mine.start()
        sends = []
        for k in range(1, NDEV):
            cp = pltpu.make_async_remote_copy(
                src_ref=x_ref, dst_ref=out_ref.at[_lin(me)], send_sem=send_sems.at[k - 1],
                recv_sem=recv_sems.at[k - 1], device_id=_flip(me, k), device_id_type=MESH)
            cp.start()
            sends.append(cp)
        for k in range(1, NDEV):
            peer = _flip(me, k)
            pltpu.make_async_remote_copy(
                src_ref=x_ref, dst_ref=out_ref.at[_lin(peer)], send_sem=send_sems.at[k - 1],
                recv_sem=recv_sems.at[k - 1], device_id=peer, device_id_type=MESH).wait_recv()
        for cp in sends:
            cp.wait_send()
        mine.wait()

    return pl.pallas_call(
        body, name=name,
        out_shape=jax.ShapeDtypeStruct((NDEV,) + shard.shape, shard.dtype),
        in_specs=[HBM_SPEC], out_specs=HBM_SPEC,
        scratch_shapes=[pltpu.SemaphoreType.DMA((7,)), pltpu.SemaphoreType.DMA((7,)), pltpu.SemaphoreType.DMA],
    )(shard)


def _exchange_within_chip(parts, name):
    na = len(parts)

    def body(*refs):
        a_refs = refs[:na]
        recv_refs = refs[na:2 * na]
        send_sems, recv_sems = refs[2 * na:]
        me = _my_pos()
        c = me[2]
        sibling = _flip(me, 1)
        remote = []
        for i in range(na):
            for q in range(NCHIP):
                rc = pltpu.make_async_remote_copy(
                    src_ref=a_refs[i].at[q, 1 - c], dst_ref=recv_refs[i].at[q],
                    send_sem=send_sems.at[NCHIP * i + q], recv_sem=recv_sems.at[NCHIP * i + q],
                    device_id=sibling, device_id_type=MESH)
                rc.start()
                remote.append(rc)
        for rc in remote:
            rc.wait_recv()
        for rc in remote:
            rc.wait_send()

    return pl.pallas_call(
        body, name=name, out_shape=[jax.ShapeDtypeStruct((NCHIP,) + a.shape[2:], a.dtype) for a in parts],
        in_specs=[HBM_SPEC] * na, out_specs=[HBM_SPEC] * na,
        scratch_shapes=[pltpu.SemaphoreType.DMA((NCHIP * na,)), pltpu.SemaphoreType.DMA((NCHIP * na,))],
    )(*parts)


HBM_ONLY = pl.BlockSpec(memory_space=pltpu.HBM)
SEM_SPEC = pl.BlockSpec(memory_space=pltpu.SEMAPHORE)
SPLIT_COPY = pltpu.CompilerParams(has_side_effects=pltpu.SideEffectType.DATAFLOW_SIDE_EFFECTING)


def _chip_peers(me):
    return [_flip(me, 4), _flip(me, 2), _flip(me, 6)]


def _between_chips_start(parts, name):
    na = len(parts)

    def body(*refs):
        a_refs = refs[:na]
        land_refs = refs[na:2 * na]
        send_sems, recv_sems = refs[2 * na], refs[2 * na + 1]
        token = refs[-1]
        me = _my_pos()
        myq = _chip(me)
        for i in range(na):
            for j, peer in enumerate(_chip_peers(me)):
                pltpu.make_async_remote_copy(
                    src_ref=a_refs[i].at[_chip(peer)], dst_ref=land_refs[i].at[myq],
                    send_sem=send_sems.at[3 * i + j], recv_sem=recv_sems.at[3 * i + j],
                    device_id=peer, device_id_type=MESH).start()
        token[...] = jnp.zeros_like(token)

    hbm = [pltpu.HBM(a.shape, a.dtype) for a in parts]
    srcs = [pltpu.with_memory_space_constraint(a, pltpu.HBM) for a in parts]
    lands = [pltpu.with_memory_space_constraint(lax.empty(a.shape, a.dtype), pltpu.HBM) for a in parts]
    res = pl.pallas_call(
        body, name=name,
        out_shape=(pltpu.SemaphoreType.DMA((3 * na,)), pltpu.SemaphoreType.DMA((3 * na,)), *hbm, *hbm,
                   jax.ShapeDtypeStruct((8, LANES), F32)),
        in_specs=[HBM_ONLY] * (2 * na),
        out_specs=(SEM_SPEC, SEM_SPEC, *([HBM_ONLY] * (2 * na)), pl.BlockSpec(memory_space=pltpu.VMEM)),
        input_output_aliases={i: 2 + i for i in range(2 * na)},
        compiler_params=SPLIT_COPY,
    )(*srcs, *lands)
    return res[-1], (res[0], res[1], list(res[2:2 + na]), list(res[2 + na:2 + 2 * na]))


def _between_chips_wait(pending, after, name):
    send_sems, recv_sems, parts, lands = pending
    na = len(parts)

    def body(*refs):
        a_refs = refs[:na]
        land_refs = refs[na:2 * na]
        send_sems, recv_sems = refs[2 * na], refs[2 * na + 1]
        me = _my_pos()
        for i in range(na):
            for j, peer in enumerate(_chip_peers(me)):
                cp = pltpu.make_async_remote_copy(
                    src_ref=a_refs[i].at[_chip(peer)], dst_ref=land_refs[i].at[_chip(peer)],
                    send_sem=send_sems.at[3 * i + j], recv_sem=recv_sems.at[3 * i + j],
                    device_id=peer, device_id_type=MESH)
                cp.wait_send()
                cp.wait_recv()

    hbm = [pltpu.HBM(a.shape, a.dtype) for a in parts]
    res = pl.pallas_call(
        body, name=name, out_shape=(*hbm, *hbm),
        in_specs=[HBM_ONLY] * (2 * na) + [SEM_SPEC, SEM_SPEC, pl.BlockSpec(memory_space=pl.ANY)],
        out_specs=[HBM_ONLY] * (2 * na),
        input_output_aliases={i: i for i in range(2 * na)},
        compiler_params=SPLIT_COPY,
    )(*parts, *lands, send_sems, recv_sems, after)
    return list(res[:na]), list(res[na:])


def _remote(src, dst, send_sems, recv_sems, idx, peer):
    return pltpu.make_async_remote_copy(src_ref=src, dst_ref=dst, send_sem=send_sems.at[idx],
                                        recv_sem=recv_sems.at[idx], device_id=peer, device_id_type=MESH)


def _copies_own(bufs, me):
    return [(bufs[0], bufs[1].at[me[2]], 0, _flip(me, 1))]


def _copies_near(bufs, me):
    return [(bufs[0], bufs[1].at[0, me[2]], 0, _flip(me, 2)), (bufs[0], bufs[1].at[1, me[2]], 1, _flip(me, 4))]


def _copies_far(bufs, me):
    return [(bufs[0], bufs[1].at[me[2]], 0, _flip(me, 6))]


def _copies_others(bufs, me):
    na = len(bufs) // 2
    return [(bufs[i], bufs[na + i].at[_lin(me)], 7 * i + k - 1, _flip(me, k))
            for i in range(na) for k in range(1, NDEV)]


GROUP_COPIES = dict(own=_copies_own, near=_copies_near, far=_copies_far, others=_copies_others)
GROUP_COUNT = dict(own=1, near=2, far=1)


def _gather_start(bufs, groups, after, name):
    nb = len(bufs)
    ng = len(groups)

    def body(*refs):
        b = refs[:nb]
        sems = refs[nb + 1:nb + 1 + 2 * ng]
        token = refs[-1]
        me = _my_pos()
        for gi, (group, idx) in enumerate(groups):
            for src, dst, k, peer in GROUP_COPIES[group]([b[i] for i in idx], me):
                _remote(src, dst, sems[2 * gi], sems[2 * gi + 1], k, peer).start()
        token[...] = jnp.zeros_like(token)

    sem_t = []
    for group, idx in groups:
        cnt = 7 * (len(idx) // 2) if group == "others" else GROUP_COUNT[group]
        sem_t += [pltpu.SemaphoreType.DMA((cnt,)), pltpu.SemaphoreType.DMA((cnt,))]
    ins = [pltpu.with_memory_space_constraint(a, pltpu.HBM) for a in bufs]
    res = pl.pallas_call(
        body, name=name,
        out_shape=(*sem_t, *[pltpu.HBM(a.shape, a.dtype) for a in bufs], jax.ShapeDtypeStruct((8, LANES), F32)),
        in_specs=[HBM_ONLY] * nb + [pl.BlockSpec(memory_space=pl.ANY)],
        out_specs=(*([SEM_SPEC] * (2 * ng)), *([HBM_ONLY] * nb), pl.BlockSpec(memory_space=pltpu.VMEM)),
        input_output_aliases={i: 2 * ng + i for i in range(nb)},
        compiler_params=SPLIT_COPY,
    )(*ins, after)
    return res[-1], list(res[2 * ng:2 * ng + nb]), [(res[2 * gi], res[2 * gi + 1]) for gi in range(ng)]


def _gather_wait(group, send_sems, recv_sems, bufs, after, name):
    nb = len(bufs)
    copies = dict(own=_copies_own, near=_copies_near, far=_copies_far, others=_copies_others)[group]

    def body(*refs):
        b = refs[:nb]
        ss, rs = refs[nb], refs[nb + 1]
        me = _my_pos()
        for src, dst, idx, peer in copies(b, me):
            if group == "others":
                landed = b[nb // 2 + idx // 7].at[_lin(peer)]
            elif group == "own":
                landed = b[1].at[1 - me[2]]
            else:
                landed = dst
            cp = _remote(src, landed, ss, rs, idx, peer)
            cp.wait_send()
            cp.wait_recv()

    res = pl.pallas_call(
        body, name=name, out_shape=[pltpu.HBM(a.shape, a.dtype) for a in bufs],
        in_specs=[HBM_ONLY] * nb + [SEM_SPEC, SEM_SPEC, pl.BlockSpec(memory_space=pl.ANY)],
        out_specs=[HBM_ONLY] * nb,
        input_output_aliases={i: i for i in range(nb)},
        compiler_params=SPLIT_COPY,
    )(*bufs, send_sems, recv_sems, after)
    return list(res)


def _forward_to_sibling(buf, name):
    n = buf.shape[0]

    def body(_in_ref, out_ref, send_sems, recv_sems):
        me = _my_pos()
        c = me[2]
        sibling = _flip(me, 1)
        sends = []
        for r in range(n):
            cp = _remote(out_ref.at[r, c], out_ref.at[r, c], send_sems, recv_sems, r, sibling)
            cp.start()
            sends.append(cp)
        for r in range(n):
            _remote(out_ref.at[r, c], out_ref.at[r, 1 - c], send_sems, recv_sems, r, sibling).wait_recv()
        for cp in sends:
            cp.wait_send()

    return pl.pallas_call(
        body, name=name, out_shape=jax.ShapeDtypeStruct(buf.shape, buf.dtype),
        in_specs=[HBM_SPEC], out_specs=HBM_SPEC,
        scratch_shapes=[pltpu.SemaphoreType.DMA((n,)), pltpu.SemaphoreType.DMA((n,))],
        input_output_aliases={0: 0},
    )(buf)


def _allreduce_small(rep, conv, name):
    def body(rep_hbm, conv_hbm, out_ref, conv_out, stage_r, stage_c, send_sems, recv_sems, local_sems):
        me = _my_pos()
        mi = _lin(me)
        peers = [_flip(me, k) for k in range(1, NDEV)]
        own = [pltpu.make_async_copy(rep_hbm.at[mi], stage_r.at[mi], local_sems.at[0]),
               pltpu.make_async_copy(conv_hbm.at[mi], stage_c.at[mi], local_sems.at[1])]
        for cp in own:
            cp.start()
        sends = []
        for j, peer in enumerate(peers):
            for src, stage, base in ((rep_hbm, stage_r, 0), (conv_hbm, stage_c, 7)):
                cp = _remote(src.at[_lin(peer)], stage.at[mi], send_sems, recv_sems, base + j, peer)
                cp.start()
                sends.append(cp)
        for j, peer in enumerate(peers):
            for src, stage, base in ((rep_hbm, stage_r, 0), (conv_hbm, stage_c, 7)):
                _remote(src.at[mi], stage.at[_lin(peer)], send_sems, recv_sems, base + j, peer).wait_recv()
        for cp in own:
            cp.wait()
        acc_r = stage_r[0]
        acc_c = stage_c[0]
        for q in range(1, NDEV):
            acc_r = acc_r + stage_r[q]
            acc_c = acc_c + stage_c[q]
        out_ref[mi] = acc_r
        conv_out[...] = acc_c
        for j, peer in enumerate(peers):
            cp = _remote(out_ref.at[mi], out_ref.at[mi], send_sems, recv_sems, 14 + j, peer)
            cp.start()
            sends.append(cp)
        for j, peer in enumerate(peers):
            _remote(out_ref.at[mi], out_ref.at[_lin(peer)], send_sems, recv_sems, 14 + j, peer).wait_recv()
        for cp in sends:
            cp.wait_send()

    vmem = pl.BlockSpec(memory_space=pltpu.VMEM)
    return pl.pallas_call(
        body, name=name,
        out_shape=[jax.ShapeDtypeStruct(rep.shape, F32), jax.ShapeDtypeStruct(conv.shape[1:], F32)],
        in_specs=[HBM_SPEC, HBM_SPEC], out_specs=[vmem, vmem],
        scratch_shapes=[pltpu.VMEM(rep.shape, F32), pltpu.VMEM(conv.shape, F32),
                        pltpu.SemaphoreType.DMA((21,)), pltpu.SemaphoreType.DMA((21,)), pltpu.SemaphoreType.DMA((2,))],
    )(rep, conv)


def _exchange_all(parts, name):
    na = len(parts)

    def body(*refs):
        a_refs = refs[:na]
        out_refs = refs[na:2 * na]
        send_sems, recv_sems = refs[2 * na:]
        me = _my_pos()
        sends = []
        for i in range(na):
            for k in range(1, NDEV):
                peer = _flip(me, k)
                cp = pltpu.make_async_remote_copy(
                    src_ref=a_refs[i].at[_lin(peer)], dst_ref=out_refs[i].at[_lin(me)],
                    send_sem=send_sems.at[7 * i + k - 1], recv_sem=recv_sems.at[7 * i + k - 1],
                    device_id=peer, device_id_type=MESH)
                cp.start()
                sends.append(cp)
        for i in range(na):
            for k in range(1, NDEV):
                peer = _flip(me, k)
                pltpu.make_async_remote_copy(
                    src_ref=a_refs[i].at[_lin(peer)], dst_ref=out_refs[i].at[_lin(peer)],
                    send_sem=send_sems.at[7 * i + k - 1], recv_sem=recv_sems.at[7 * i + k - 1],
                    device_id=peer, device_id_type=MESH).wait_recv()
        for cp in sends:
            cp.wait_send()

    return pl.pallas_call(
        body, name=name, out_shape=[jax.ShapeDtypeStruct(a.shape, a.dtype) for a in parts],
        in_specs=[HBM_SPEC] * na, out_specs=[HBM_SPEC] * na,
        scratch_shapes=[pltpu.SemaphoreType.DMA((7 * na,)), pltpu.SemaphoreType.DMA((7 * na,))],
    )(*parts)


def _scalar(v):
    return jnp.asarray(v, jnp.int32).reshape(1)


def _sum_pairs(parts, theirs, name):
    na = len(parts)

    def body(c_ref, *refs):
        for i in range(na):
            o_ref = refs[2 * na + i]
            o_ref[0] = (refs[i][0, 0].astype(F32) + refs[na + i][0].astype(F32)).astype(o_ref.dtype)

    def mine_spec(a):
        return pl.BlockSpec((1, 1) + a.shape[2:], lambda q, c_ref: (q, c_ref[0], 0, 0))

    def spec(a):
        return pl.BlockSpec((1,) + a.shape[1:], lambda q, c_ref: (q, 0, 0))

    return pl.pallas_call(
        body, name=name,
        grid_spec=pltpu.PrefetchScalarGridSpec(
            num_scalar_prefetch=1, grid=(NCHIP,),
            in_specs=[mine_spec(a) for a in parts] + [spec(a) for a in theirs],
            out_specs=[spec(a) for a in theirs]),
        out_shape=[jax.ShapeDtypeStruct(a.shape, a.dtype) for a in theirs],
        compiler_params=_cp(("arbitrary",), VMEM_BIG),
    )(_scalar(lax.axis_index("c")), *parts, *theirs)


def _others(q, mine, nblk=NCHIP):
    return jnp.where(q == mine, (q + 1) % nblk, q)


def _sum_chips_adamw(own, recv, wv, mv, vv, tr, name):
    _, r, w = recv.shape

    def body(q_ref, own_ref, r0, r1, r2, r3, w_ref, m_ref, v_ref, g_ref, d_ref, m2_ref, v2_ref):
        myq = q_ref[0]
        acc = None
        for q, r_ref in enumerate((r0, r1, r2, r3)):
            term = jnp.where(myq == q, own_ref[0], r_ref[0]).astype(F32)
            acc = term if acc is None else acc + term
        g_ref[...] = acc
        delta, m2, v2 = _adam_math(w_ref[...], acc, m_ref[...], v_ref[...])
        d_ref[...] = delta
        m2_ref[...] = m2
        v2_ref[...] = v2

    def recv_spec(q):
        return pl.BlockSpec((1, tr, w), lambda i, q_ref: (_others(q, q_ref[0]), i, 0))

    rows = lambda: pl.BlockSpec((tr, w), lambda i, q_ref: (i, 0))
    shp = jax.ShapeDtypeStruct((r, w), F32)
    return pl.pallas_call(
        body, name=name,
        grid_spec=pltpu.PrefetchScalarGridSpec(
            num_scalar_prefetch=1, grid=(r // tr,),
            in_specs=[pl.BlockSpec((1, tr, w), lambda i, q_ref: (q_ref[0], i, 0))]
            + [recv_spec(q) for q in range(NCHIP)] + [rows(), rows(), rows()],
            out_specs=[rows(), rows(), rows(), rows()]),
        out_shape=[shp, shp, shp, shp],
        compiler_params=_cp(("arbitrary",), VMEM_MID),
    )(_scalar(_chip(_my_pos())), *_hbm(own, recv, recv, recv, recv, wv, mv, vv))


def _sum_blocks_small(own, recv, mine, transpose, name):
    na = len(recv)
    nblk = recv[0].shape[0]

    def body(q_ref, *refs):
        me = q_ref[0]
        for i in range(na):
            acc = None
            for q in range(nblk):
                term = jnp.where(me == q, refs[i][0], refs[na * (1 + q) + i][0]).astype(F32)
                acc = term if acc is None else acc + term
            refs[na * (1 + nblk) + i][...] = acc.T if transpose[i] else acc

    def oshape(a, tr):
        r, w = a.shape[1:]
        return (w, r) if tr else (r, w)

    own_spec = lambda a: pl.BlockSpec((1,) + a.shape[1:], lambda s, q_ref: (q_ref[0], 0, 0))
    recv_spec = lambda a, q: pl.BlockSpec((1,) + a.shape[1:], lambda s, q_ref: (_others(q, q_ref[0], nblk), 0, 0))
    out_spec = lambda shp: pl.BlockSpec(shp, lambda s, q_ref: (0, 0))
    in_specs = [own_spec(a) for a in own]
    for q in range(nblk):
        in_specs += [recv_spec(a, q) for a in recv]
    return pl.pallas_call(
        body, name=name,
        grid_spec=pltpu.PrefetchScalarGridSpec(
            num_scalar_prefetch=1, grid=(1,), in_specs=in_specs,
            out_specs=[out_spec(oshape(a, tr)) for a, tr in zip(recv, transpose)]),
        out_shape=[jax.ShapeDtypeStruct(oshape(a, tr), F32) for a, tr in zip(recv, transpose)],
        compiler_params=_cp(("arbitrary",), VMEM_MID),
    )(_scalar(mine), *own, *(list(recv) * nblk))


def _rep_offsets():
    offs = []
    o = 0
    for r in REP_ROWS:
        offs.append(o)
        o += r
    return offs


LOSS_ROW = REP_TOTAL_ROWS


def _pack_small_grads(g):
    offs = _rep_offsets()

    def body(dwa, dwx, dnm, dbin, dcb, dba, dbx, dlam, dq0, dq1, dq2, dk0, dk1, dk2, dnp, dbpg, loss, o_ref):
        o_ref[pl.ds(REP_TOTAL_ROWS - 2, NDEV * REP_ROWS_DEV - REP_TOTAL_ROWS + 2), :] = jnp.zeros(
            (NDEV * REP_ROWS_DEV - REP_TOTAL_ROWS + 2, LANES), F32)
        o_ref[pl.ds(LOSS_ROW, 1), :] = loss[0:1, :]
        for n in range(NRB):
            o_ref[pl.ds(offs[0] + n * RBW, RBW), :] = dwa[n]
            o_ref[pl.ds(offs[1] + n * RBW, RBW), :] = dwx[n]

        def put_vec(off, ref, rows):
            for k in range(rows):
                o_ref[pl.ds(off + k, 1), :] = ref[:, k * LANES:(k + 1) * LANES]

        put_vec(offs[2], dnm, REP_ROWS[2])
        put_vec(offs[3], dbin, REP_ROWS[3])
        put_vec(offs[4], dcb, REP_ROWS[4])
        put_vec(offs[5], dba, REP_ROWS[5])
        put_vec(offs[6], dbx, REP_ROWS[6])
        put_vec(offs[7], dlam, REP_ROWS[7])
        for k, ref in enumerate((dq0, dq1, dq2)):
            o_ref[pl.ds(offs[8] + k, 1), :] = ref[...]
        for k, ref in enumerate((dk0, dk1, dk2)):
            o_ref[pl.ds(offs[9] + k, 1), :] = ref[...]
        put_vec(offs[10], dnp, REP_ROWS[10])
        put_vec(offs[11], dbpg, REP_ROWS[11])

    args = [g["w_rg_a"], g["w_rg_x"], g["norm_mix"], g["b_in"], g["conv_b"], g["b_rg_a"], g["b_rg_x"],
            g["lru_lambda"], *g["q_norm"], *g["k_norm"], g["norm_ple"], g["b_ple_gate"], g["loss"]]
    full = lambda shp: pl.BlockSpec(shp, lambda: (0,) * len(shp))
    return pl.pallas_call(
        body, name="pack_small_grads",
        in_specs=[full(a.shape) for a in args],
        out_specs=full((NDEV * REP_ROWS_DEV, LANES)),
        out_shape=jax.ShapeDtypeStruct((NDEV * REP_ROWS_DEV, LANES), F32),
    )(*_hbm(*args))


def _adam_math(wv, gv, mv, vv):
    c1 = 1.0 - B1 ** STEP
    c2 = 1.0 - B2 ** STEP
    m2 = B1 * mv + (1.0 - B1) * gv
    v2 = B2 * vv + (1.0 - B2) * (gv * gv)
    delta = (-LR) * ((m2 / c1) / (jnp.sqrt(v2 / c2) + AEPS) + WD * wv)
    return delta, m2, v2


def _adamw_small(rep_flat, w, m, v):
    offs = _rep_offsets()
    n = len(REP_NAMES)

    def body(*refs):
        g_ref = refs[0]
        w_refs = refs[1:1 + n]
        m_refs = refs[1 + n:1 + 2 * n]
        v_refs = refs[1 + 2 * n:1 + 3 * n]
        outs = refs[1 + 3 * n:]
        go, do, mo, vo = outs[:n], outs[n:2 * n], outs[2 * n:3 * n], outs[3 * n:]

        def emit(i, idx, gv):
            go[i][idx] = gv
            delta, m2, v2 = _adam_math(w_refs[i][idx], gv, m_refs[i][idx], v_refs[i][idx])
            do[i][idx] = delta
            mo[i][idx] = m2
            vo[i][idx] = v2

        for i in range(n):
            if i < 2:
                for b in range(NRB):
                    emit(i, b, g_ref[pl.ds(offs[i] + b * RBW, RBW), :])
            elif REP_NAMES[i] in ("q_norm", "k_norm"):
                emit(i, slice(None), g_ref[pl.ds(offs[i], NG), :])
            else:
                gv = jnp.concatenate([g_ref[pl.ds(offs[i] + k, 1), :] for k in range(REP_ROWS[i])], axis=1)
                emit(i, slice(None), gv)

    full = lambda shp: pl.BlockSpec(shp, lambda: (0,) * len(shp))
    pspecs = [full(a.shape) for a in w]
    pshapes = [jax.ShapeDtypeStruct(a.shape, F32) for a in w]
    res = pl.pallas_call(
        body, name="adamw_small",
        in_specs=[full(rep_flat.shape)] + pspecs * 3,
        out_specs=pspecs * 4, out_shape=pshapes * 4,
        compiler_params=_cp(None, VMEM_MID),
    )(*_hbm(rep_flat, *w, *m, *v))
    return res[:n], res[n:2 * n], res[2 * n:3 * n], res[3 * n:]


def _adamw_many(w, g, m, v):
    n = len(w)

    def body(*refs):
        for i in range(n):
            delta, m2, v2 = _adam_math(refs[i][...], refs[n + i][...], refs[2 * n + i][...], refs[3 * n + i][...])
            refs[4 * n + i][...] = delta
            refs[5 * n + i][...] = m2
            refs[6 * n + i][...] = v2

    full = lambda shp: pl.BlockSpec(shp, lambda: (0,) * len(shp))
    specs = [full(a.shape) for a in w]
    shapes = [jax.ShapeDtypeStruct(a.shape, F32) for a in w]
    res = pl.pallas_call(
        body, name="adamw_shards",
        in_specs=specs * 4, out_specs=specs * 3, out_shape=shapes * 3,
        compiler_params=_cp(None, VMEM_MID),
    )(*_hbm(*w, *g, *m, *v))
    return res[:n], res[n:2 * n], res[2 * n:]


def kernel(x, p, norm_mix, w_in, b_in, conv_w, conv_b, w_rg_a, b_rg_a, w_rg_x, b_rg_x, lru_lambda, q_norm, k_norm, w_o_rnn, w_o_att, w_out, norm_ple, w_ple_gate, b_ple_gate, w_ple, loss_target, m_norm_mix, m_w_in, m_b_in, m_conv_w, m_conv_b, m_w_rg_a, m_b_rg_a, m_w_rg_x, m_b_rg_x, m_lru_lambda, m_q_norm, m_k_norm, m_w_o_rnn, m_w_o_att, m_w_out, m_norm_ple, m_w_ple_gate, m_b_ple_gate, m_w_ple, v_norm_mix, v_w_in, v_b_in, v_conv_w, v_conv_b, v_w_rg_a, v_b_rg_a, v_w_rg_x, v_b_rg_x, v_lru_lambda, v_q_norm, v_k_norm, v_w_o_rnn, v_w_o_att, v_w_out, v_norm_ple, v_w_ple_gate, v_b_ple_gate, v_w_ple):
    w = dict(norm_mix=norm_mix, w_in=w_in, b_in=b_in, conv_w=conv_w, conv_b=conv_b, w_rg_a=w_rg_a, b_rg_a=b_rg_a,
             w_rg_x=w_rg_x, b_rg_x=b_rg_x, lru_lambda=lru_lambda, q_norm=q_norm, k_norm=k_norm, w_o_rnn=w_o_rnn,
             w_o_att=w_o_att, w_out=w_out, norm_ple=norm_ple, w_ple_gate=w_ple_gate, b_ple_gate=b_ple_gate,
             w_ple=w_ple)
    m = dict(norm_mix=m_norm_mix, w_in=m_w_in, b_in=m_b_in, conv_w=m_conv_w, conv_b=m_conv_b, w_rg_a=m_w_rg_a,
             b_rg_a=m_b_rg_a, w_rg_x=m_w_rg_x, b_rg_x=m_b_rg_x, lru_lambda=m_lru_lambda, q_norm=m_q_norm,
             k_norm=m_k_norm, w_o_rnn=m_w_o_rnn, w_o_att=m_w_o_att, w_out=m_w_out, norm_ple=m_norm_ple,
             w_ple_gate=m_w_ple_gate, b_ple_gate=m_b_ple_gate, w_ple=m_w_ple)
    v = dict(norm_mix=v_norm_mix, w_in=v_w_in, b_in=v_b_in, conv_w=v_conv_w, conv_b=v_conv_b, w_rg_a=v_w_rg_a,
             b_rg_a=v_b_rg_a, w_rg_x=v_w_rg_x, b_rg_x=v_b_rg_x, lru_lambda=v_lru_lambda, q_norm=v_q_norm,
             k_norm=v_k_norm, w_o_rnn=v_w_o_rnn, w_o_att=v_w_o_att, w_out=v_w_out, norm_ple=v_norm_ple,
             w_ple_gate=v_w_ple_gate, b_ple_gate=v_b_ple_gate, w_ple=v_w_ple)
    names = list(w.keys())

    shards = [w_in[0].T.astype(BF16), w_o_rnn[0].astype(BF16), w_o_att[0].T.astype(BF16), w_out[0].astype(BF16),
              w_ple_gate[0].astype(BF16), w_ple[0].T.astype(BF16), conv_w[0]]
    pos = _my_pos()
    me, my_core, my_chip = _lin(pos), pos[2], _chip(pos)
    hbm_empty = lambda shp, dt: lax.empty(shp, dt)
    w_shard, conv_shard = shards[0], shards[6]
    shp = w_shard.shape
    entry_token, bufs, sems = _gather_start(
        [w_shard, hbm_empty((2,) + shp, BF16), hbm_empty((2, 2) + shp, BF16), conv_shard,
         hbm_empty((NDEV,) + conv_shard.shape, F32)],
        [("own", (0, 1)), ("near", (0, 2)), ("others", (3, 4))], norm_mix, "gather_start_near")
    w_src, own_l, near_l, conv_src, conv_l = bufs
    sem_own, sem_near, sem_conv = sems
    gather_out = {}

    def project(hn):
        w_thru, own = _gather_wait("own", *sem_own, [w_src, own_l], hn, "gather_wait_own")
        own = lax.dynamic_update_slice(own, w_shard[None], (my_core, 0, 0)).reshape(1, CHIP_COLS, D)
        chips = [jnp.stack([my_chip]), jnp.stack([my_chip ^ 1, my_chip ^ 2]), jnp.stack([my_chip ^ 3])]
        chips = [c.astype(jnp.int32) for c in chips]
        proj = _in_proj_chips(hn, own, b_in, chips[0], None, entry_token, "in_proj_own")
        w_thru, near = _gather_wait("near", *sem_near, [w_thru, near_l], proj, "gather_wait_near")
        near = _forward_to_sibling(near, "gather_forward_near")
        token, (w_thru, far_l), (sem_far,) = _gather_start(
            [w_thru, hbm_empty((2,) + shp, BF16)], [("far", (0, 1))], near, "gather_start_far")
        near = near.reshape(2, CHIP_COLS, D)
        proj = _in_proj_chips(hn, near, b_in, chips[1], proj, token, "in_proj_near")
        w_thru, far = _gather_wait("far", *sem_far, [w_thru, far_l], proj, "gather_wait_far")
        far = _forward_to_sibling(far[None], "gather_forward_far").reshape(1, CHIP_COLS, D)
        proj = _in_proj_chips(hn, far, b_in, chips[2], proj, token, "in_proj_far")
        conv_thru, conv_g = _gather_wait("others", *sem_conv, [conv_src, conv_l], proj, "gather_wait_conv")
        conv_g = lax.dynamic_update_slice(conv_g, conv_shard[None], (me, 0, 0))
        conv_f = conv_g.transpose(1, 0, 2).reshape(CONVW, DR)
        srcs = list(shards[1:6])
        token, obufs, (sem_out,) = _gather_start(
            srcs + [hbm_empty((NDEV,) + a.shape, BF16) for a in srcs], [("others", tuple(range(10)))], proj,
            "gather_start_out")
        gather_out.update(bufs=obufs, sems=sem_out)
        return proj, [own, near, far], jnp.concatenate(chips), conv_f, token

    def other_weights(after):
        obufs = _gather_wait("others", *gather_out["sems"], gather_out["bufs"], after, "gather_wait_out")
        full = [lax.dynamic_update_slice(a, s[None], (me, 0, 0)) for a, s in zip(obufs[5:], shards[1:6])]
        return [a.reshape((NDEV * a.shape[1], a.shape[2])) for a in full]

    def start_reduce(arrs, tag):
        parts = [a.reshape((NCHIP, 2, a.shape[0] // NDEV, a.shape[1])) for a in arrs]
        theirs = _exchange_within_chip(parts, "reduce_within_chip_" + tag)
        return _between_chips_start(_sum_pairs(parts, theirs, "sum_pairs_" + tag), "reduce_between_chips_start_" + tag)

    grad_x, pending_out, pending_in, small = _local_step(
        x.reshape(T, D), p.reshape(T, PLE), loss_target.reshape(T, D),
        project, other_weights,
        norm_mix, conv_b, w_rg_a[0], b_rg_a, w_rg_x[0], b_rg_x, lru_lambda, q_norm[0], k_norm[0],
        norm_ple, b_ple_gate, start_reduce, entry_token)

    rep_parts = _pack_small_grads(small).reshape(NDEV, REP_ROWS_DEV, LANES)
    conv_parts = small["conv_w"].reshape(CONVW, NDEV, DR // NDEV).transpose(1, 0, 2)
    rep_all, g_conv = _allreduce_small(rep_parts, conv_parts, "allreduce_small")
    rep_all = rep_all.reshape(NDEV * REP_ROWS_DEV, LANES)
    loss = rep_all[LOSS_ROW, 0]

    myq = _chip(_my_pos())
    own_out, recv_out = _between_chips_wait(pending_out, rep_all, "reduce_between_chips_wait_out")
    own_in, recv_in = _between_chips_wait(pending_in, rep_all, "reduce_between_chips_wait_in")
    w_in_res = _sum_chips_adamw(own_in[0], recv_in[0], w_in[0].T, m_w_in[0].T, v_w_in[0].T, 304, "adamw_w_in")
    g_o_rnn, g_o_att, g_out, g_pg, g_ple = _sum_blocks_small(
        own_out, recv_out, myq, (False, True, False, False, True), "sum_chips_out")

    grad, delta, new_m, new_v = {}, {}, {}, {}
    rep_shape = lambda a: a if a.ndim == 2 else a.reshape(a.shape[1:])
    res = _adamw_small(rep_all, [rep_shape(w[n]) for n in REP_NAMES], [rep_shape(m[n]) for n in REP_NAMES],
                       [rep_shape(v[n]) for n in REP_NAMES])
    for dst, vals in zip((grad, delta, new_m, new_v), res):
        for n, a in zip(REP_NAMES, vals):
            dst[n] = a.reshape(w[n].shape)
    grad["w_in"], delta["w_in"], new_m["w_in"], new_v["w_in"] = [a.T[None] for a in w_in_res]
    rest = ("w_o_rnn", "w_o_att", "w_out", "w_ple_gate", "w_ple", "conv_w")
    g_rest = [g_o_rnn, g_o_att, g_out, g_pg, g_ple, g_conv]
    res = _adamw_many([w[n][0] for n in rest], g_rest, [m[n][0] for n in rest], [v[n][0] for n in rest])
    for n, a in zip(rest, g_rest):
        grad[n] = a[None]
    for dst, vals in zip((delta, new_m, new_v), res):
        for n, a in zip(rest, vals):
            dst[n] = a[None]

    return (loss, grad_x.reshape(BL, S, D), *[grad[n] for n in names], *[delta[n] for n in names],
            *[new_m[n] for n in names], *[new_v[n] for n in names])
```

```python
import jax
import jax.numpy as jnp
from jax import lax
from jax.experimental import pallas as pl
from jax.experimental.pallas import tpu as pltpu

F32 = jnp.float32
BF16 = jnp.bfloat16

D = 1024
S = 2048
BL = 2
T = BL * S
NDEV = 8
NCHIP = 4
PLE = 256
DR = 1280
NRB = 10
RBW = 128
CONVW = 4
LRU_C = 8.0
HD = 128
NH = 4
PATTERNS = ((128, 1), (512, 4), (2048, 16))
NG = 3
ATT = NH * HD
GW = NG * ATT
NIN = 2 * DR + 3 * GW + ATT + 2 * D
OFF_ZR = DR
OFF_Q = 2 * DR
OFF_ZA = OFF_Q + 3 * GW
OFF_G = OFF_ZA + ATT
ROPE_THETA = 10000.0
EPS = 1e-6
SCALE = HD ** -0.5
NEG = -1e30
QB = 128
LANES = 128
CT = 512
NCT = NIN // CT
A_W = 2 * DR
C_W = ATT + 2 * D

LR, B1, B2, AEPS, WD, STEP = 0.001, 0.9, 0.999, 1e-08, 0.01, 10

NSHARD_IN = NIN // NDEV
REP_NAMES = ("w_rg_a", "w_rg_x", "norm_mix", "b_in", "conv_b", "b_rg_a", "b_rg_x", "lru_lambda", "q_norm",
             "k_norm", "norm_ple", "b_ple_gate")
REP_ROWS = (NRB * RBW, NRB * RBW, D // LANES, NIN // LANES, DR // LANES, DR // LANES, DR // LANES, DR // LANES,
            NG, NG, D // LANES, D // LANES)
REP_TOTAL_ROWS = sum(REP_ROWS)
REP_ROWS_DEV = 344
BIG_NAMES = ("w_in", "w_o_rnn", "w_o_att", "w_out", "w_ple_gate", "w_ple")

VMEM_BIG = 56 * 1024 * 1024
VMEM_MID = 40 * 1024 * 1024


def _cp(sem=None, vmem=None):
    return pltpu.CompilerParams(dimension_semantics=sem, vmem_limit_bytes=vmem)


def _hbm(*arrays):
    return [pltpu.with_memory_space_constraint(a, pltpu.HBM) for a in arrays]


def _dot(a, b):
    return jnp.dot(a, b, preferred_element_type=F32)


def _dot_nt(a, b):
    return lax.dot_general(a, b, (((1,), (1,)), ((), ())), preferred_element_type=F32)


def _dot_tn(a, b):
    return lax.dot_general(a, b, (((0,), (0,)), ((), ())), preferred_element_type=F32)


def _sigmoid(x):
    return jax.nn.sigmoid(x)


def _perm(j):
    jq = j - OFF_Q // CT
    inside = (j >= OFF_Q // CT) & (j < OFF_ZA // CT)
    return jnp.where(inside, OFF_Q // CT + (jq % 3) * 3 + jq // 3, j)


PIECES = ((0, A_W // CT), (OFF_Q // CT, GW // CT), (OFF_Q // CT + 3, GW // CT), (OFF_Q // CT + 6, GW // CT),
          (OFF_ZA // CT, C_W // CT))


def _rmsnorm_fwd(x, gain, token, tm=512):
    def body(x_ref, g_ref, _token, o_ref):
        xv = x_ref[...]
        var = jnp.mean(xv * xv, axis=-1, keepdims=True)
        o_ref[...] = (xv * lax.rsqrt(var + EPS) * g_ref[...]).astype(BF16)

    return pl.pallas_call(
        body, grid=(T // tm,), name="rmsnorm_fwd",
        in_specs=[pl.BlockSpec((tm, D), lambda i: (i, 0)), pl.BlockSpec((1, D), lambda i: (0, 0)),
                  pl.BlockSpec((8, LANES), lambda i: (0, 0))],
        out_specs=pl.BlockSpec((tm, D), lambda i: (i, 0)),
        out_shape=jax.ShapeDtypeStruct((T, D), BF16),
        compiler_params=_cp(("parallel",)),
    )(*_hbm(x, gain, token))


CHIP_COLS = NIN // NCHIP


def _in_proj_chips(hn, w_rows, bias, chips, proj, token, name, tm=1024):
    n = w_rows.shape[0]

    def body(chips_ref, a_ref, w_ref, b_ref, _token, *rest):
        o_ref = rest[-1]
        o_ref[...] = (_dot_nt(a_ref[...], w_ref[0]) + b_ref[...]).astype(BF16)

    in_specs = [pl.BlockSpec((tm, D), lambda s, i, ch: (i, 0)),
                pl.BlockSpec((1, CHIP_COLS, D), lambda s, i, ch: (s, 0, 0)),
                pl.BlockSpec((1, CHIP_COLS), lambda s, i, ch: (0, ch[s])),
                pl.BlockSpec((8, LANES), lambda s, i, ch: (0, 0))]
    args = [hn, w_rows, bias, token]
    aliases = {}
    if proj is not None:
        in_specs.append(pl.BlockSpec(memory_space=pl.ANY))
        args.append(proj)
        aliases = {5: 0}
    return pl.pallas_call(
        body, name=name,
        grid_spec=pltpu.PrefetchScalarGridSpec(
            num_scalar_prefetch=1, grid=(n, T // tm), in_specs=in_specs,
            out_specs=pl.BlockSpec((tm, CHIP_COLS), lambda s, i, ch: (i, ch[s]))),
        out_shape=jax.ShapeDtypeStruct((T, NIN), BF16),
        input_output_aliases=aliases,
        compiler_params=_cp(("arbitrary", "arbitrary"), VMEM_BIG),
    )(chips, *_hbm(*args))


def _grad_x(pieces, w_bufs, chips, token, x, dx1, gain, tm=512):
    nb = len(w_bufs)

    def body(chips_ref, a_ref, q_ref, k_ref, v_ref, c_ref, *rest):
        w_hbm = rest[:nb]
        x_ref, dx1_ref, g_ref, dx_ref, dg_ref, w = rest[nb + 1:]
        first = pl.program_id(0) == 0

        @pl.when(first)
        def _():
            s = 0
            for buf in w_hbm:
                for r in range(buf.shape[0]):
                    row = pl.multiple_of(chips_ref[s] * CHIP_COLS, 128)
                    pltpu.sync_copy(buf.at[r], w.at[pl.ds(row, CHIP_COLS), :])
                    s += 1

        acc = _dot(a_ref[...], w[pl.ds(0, A_W), :])
        for kind, p_ref in enumerate((q_ref, k_ref, v_ref)):
            for g in range(NG):
                row = OFF_Q + (3 * g + kind) * CT
                acc = acc + _dot(p_ref[:, g * CT:(g + 1) * CT], w[pl.ds(row, CT), :])
        dn = acc + _dot(c_ref[...], w[pl.ds(OFF_ZA, C_W), :])
        xv = x_ref[...]
        rstd = lax.rsqrt(jnp.mean(xv * xv, axis=-1, keepdims=True) + EPS)
        xh = xv * rstd
        dg = jnp.sum(dn * xh, axis=0, keepdims=True)
        gd = dn * g_ref[...]
        dx_ref[...] = dx1_ref[...] + rstd * (gd - xh * jnp.mean(gd * xh, axis=-1, keepdims=True))

        @pl.when(first)
        def _():
            dg_ref[...] = dg

        @pl.when(jnp.logical_not(first))
        def _():
            dg_ref[...] += dg

    tok = lambda wd: pl.BlockSpec((tm, wd), lambda i, ch: (i, 0))
    vec = lambda: pl.BlockSpec((1, D), lambda i, ch: (0, 0))
    return pl.pallas_call(
        body, name="grad_x",
        grid_spec=pltpu.PrefetchScalarGridSpec(
            num_scalar_prefetch=1, grid=(T // tm,),
            in_specs=[tok(A_W), tok(GW), tok(GW), tok(GW), tok(C_W)] + [pl.BlockSpec(memory_space=pl.ANY)] * nb
            + [pl.BlockSpec((8, LANES), lambda i, ch: (0, 0)), tok(D), tok(D), vec()],
            out_specs=[tok(D), vec()],
            scratch_shapes=[pltpu.VMEM((NIN, D), BF16)]),
        out_shape=[jax.ShapeDtypeStruct((T, D), F32), jax.ShapeDtypeStruct((1, D), F32)],
        compiler_params=_cp(("arbitrary",), VMEM_BIG),
    )(chips, *_hbm(*pieces, *w_bufs, token, x, dx1, gain))


def _dw_in(pieces, hn):
    def body(a_ref, q_ref, k_ref, v_ref, c_ref, h_hbm, o_ref, s_ref, h):
        j = pl.program_id(0)

        @pl.when(j == 0)
        def _():
            pltpu.sync_copy(h_hbm, h)

        def step(x_ref):
            xv = x_ref[...]
            o_ref[...] = _dot_tn(xv, h[...]).astype(BF16)
            s_ref[...] = jnp.sum(xv.astype(F32), axis=0, keepdims=True)

        for x_ref, (lo, n) in zip((a_ref, q_ref, k_ref, v_ref, c_ref), PIECES):
            pl.when((j >= lo) & (j < lo + n))(lambda x_ref=x_ref: step(x_ref))

    def piece_spec(lo, n):
        return pl.BlockSpec((T, CT), lambda j: (0, jnp.clip(j - lo, 0, n - 1)))

    return pl.pallas_call(
        body, grid=(NCT,), name="dw_in",
        in_specs=[piece_spec(lo, n) for lo, n in PIECES] + [pl.BlockSpec(memory_space=pl.ANY)],
        out_specs=[pl.BlockSpec((CT, D), lambda j: (_perm(j), 0)), pl.BlockSpec((1, CT), lambda j: (0, _perm(j)))],
        out_shape=[jax.ShapeDtypeStruct((NIN, D), BF16), jax.ShapeDtypeStruct((1, NIN), F32)],
        scratch_shapes=[pltpu.VMEM((T, D), BF16)],
        compiler_params=_cp(("arbitrary",), VMEM_BIG),
    )(*_hbm(*pieces, hn))


def _mm_tn(a, b, ta, tt, name):
    m = a.shape[1]
    n = b.shape[1]
    nt = T // tt

    def body(a_ref, b_ref, o_ref, acc):
        t = pl.program_id(1)
        p = _dot_tn(a_ref[...].astype(BF16), b_ref[...].astype(BF16))

        @pl.when(t == 0)
        def _():
            acc[...] = p

        @pl.when(t > 0)
        def _():
            acc[...] += p

        @pl.when(t == nt - 1)
        def _():
            o_ref[...] = acc[...].astype(BF16)

    return pl.pallas_call(
        body, grid=(m // ta, nt), name=name,
        in_specs=[pl.BlockSpec((tt, ta), lambda j, t: (t, j)), pl.BlockSpec((tt, n), lambda j, t: (t, 0))],
        out_specs=pl.BlockSpec((ta, n), lambda j, t: (j, 0)),
        out_shape=jax.ShapeDtypeStruct((m, n), BF16),
        scratch_shapes=[pltpu.VMEM((ta, n), F32)],
        compiler_params=_cp(("parallel", "arbitrary"), VMEM_MID),
    )(*_hbm(a, b))


def _row_iota():
    return lax.broadcasted_iota(jnp.int32, (S, RBW), 0)


SUBLANES = 8
N_SHIFT_BUFS = 4


class _Shifter:
    def __init__(self, bufs):
        self.bufs = bufs
        self.k = 0

    def _store(self, v, fill, front):
        b = self.bufs.at[self.k % N_SHIFT_BUFS]
        self.k += 1
        b[pl.ds(0 if front else SUBLANES + S, SUBLANES), :] = jnp.full((SUBLANES, RBW), fill, F32)
        b[pl.ds(SUBLANES, S), :] = v
        return b

    def down(self, v, ds, fill):
        b = self._store(v, fill, True)
        return [b[pl.ds(SUBLANES - d, S), :] for d in ds]

    def up(self, v, ds, fill):
        b = self._store(v, fill, False)
        return [b[pl.ds(SUBLANES + d, S), :] for d in ds]


def _shift_down(v, d, sh, fill):
    return sh.down(v, (d,), fill)[0]


def _shift_up(v, d, sh, fill):
    return sh.up(v, (d,), fill)[0]


def _scan_down(a, u, row):
    d = 1
    while d < S:
        last = 2 * d >= S
        if d < SUBLANES:
            u = a * _shift_down(u, d, row, 0.0) + u
            if not last:
                a = a * _shift_down(a, d, row, 1.0)
        else:
            u = jnp.concatenate([u[:d], a[d:] * u[:S - d] + u[d:]], axis=0)
            if not last:
                a = jnp.concatenate([a[:d], a[d:] * a[:S - d]], axis=0)
        d *= 2
    return u


def _scan_up(b, g, row):
    d = 1
    while d < S:
        last = 2 * d >= S
        if d < SUBLANES:
            g = g + b * _shift_up(g, d, row, 0.0)
            if not last:
                b = b * _shift_up(b, d, row, 0.0)
        else:
            g = jnp.concatenate([g[:S - d] + b[:S - d] * g[d:], g[S - d:]], axis=0)
            if not last:
                b = jnp.concatenate([b[:S - d] * b[d:], b[S - d:]], axis=0)
        d *= 2
    return g


def _softplus(x):
    return jnp.maximum(x, 0.0) + jnp.log1p(jnp.exp(-jnp.abs(x)))


def _rnn_gates(x, cw, cb, wa, ba, wx, bx, lam, row, pad):
    xs = pad.down(x, (1, 2, 3), 0.0)
    xc = cb + cw[3:4, :] * x
    for j in (1, 2, 3):
        xc = xc + cw[3 - j:4 - j, :] * xs[j - 1]
    xcb = xc.astype(BF16)
    r = _sigmoid(_dot(xcb, wa) + ba)
    i = _sigmoid(_dot(xcb, wx) + bx)
    sp = _softplus(-lam)
    log_a = (-LRU_C) * r * sp
    a = jnp.exp(log_a)
    mult = jnp.where(row == 0, 1.0, jnp.sqrt(jnp.tanh(-log_a) * (1.0 + a * a)))
    return xc, xcb, r, i, sp, a, mult, xs


def _rnn_fwd(proj3, conv_w, conv_b, wa, ba, wx, bx, lam, token):
    def body(x_ref, cw_ref, cb_ref, wa_ref, ba_ref, wx_ref, bx_ref, lam_ref, _token, h_ref, pad):
        row = _row_iota()
        sh = _Shifter(pad)
        x = x_ref[0].astype(F32)
        xc, _, _, i, _, a, mult, _ = _rnn_gates(x, cw_ref[...], cb_ref[...], wa_ref[0], ba_ref[...],
                                             wx_ref[0], bx_ref[...], lam_ref[...], row, sh)
        h_ref[0] = _scan_down(a, mult * (i * xc), sh)

    vec = lambda: pl.BlockSpec((1, RBW), lambda b, n: (0, n))
    mat = lambda: pl.BlockSpec((1, RBW, RBW), lambda b, n: (n, 0, 0))
    return pl.pallas_call(
        body, grid=(BL, NRB), name="rnn_fwd",
        in_specs=[pl.BlockSpec((1, S, RBW), lambda b, n: (b, 0, n)),
                  pl.BlockSpec((CONVW, RBW), lambda b, n: (0, n)),
                  vec(), mat(), vec(), mat(), vec(), vec(), pl.BlockSpec((8, LANES), lambda b, n: (0, 0))],
        out_specs=pl.BlockSpec((1, S, RBW), lambda b, n: (b, 0, n)),
        out_shape=jax.ShapeDtypeStruct((BL, S, DR), F32),
        scratch_shapes=[pltpu.VMEM((N_SHIFT_BUFS, S + 2 * SUBLANES, RBW), F32)],
        compiler_params=_cp(("parallel", "parallel"), VMEM_MID),
    )(*_hbm(proj3, conv_w, conv_b, wa, ba, wx, bx, lam, token))


def _rnn_bwd(proj3, h3, dh3, slab_a3, conv_w, conv_b, wa, ba, wx, bx, lam, token):
    def body(x_ref, h_ref, dh_ref, cw_ref, cb_ref, wa_ref, ba_ref, wx_ref, bx_ref, lam_ref, _alias, _token,
             dx_ref, dcw_ref, dcb_ref, dwa_ref, dba_ref, dwx_ref, dbx_ref, dlam_ref, pad):
        row = _row_iota()
        sh = _Shifter(pad)
        x = x_ref[0].astype(F32)
        cw = cw_ref[...]
        wa_v = wa_ref[0]
        wx_v = wx_ref[0]
        lam_v = lam_ref[...]
        xc, xcb, r, i, sp, a, mult, xs = _rnn_gates(x, cw, cb_ref[...], wa_v, ba_ref[...], wx_v, bx_ref[...], lam_v,
                                                    row, sh)
        h = h_ref[0]
        g = _scan_up(_shift_up(a, 1, sh, 0.0), dh_ref[0], sh)
        da = g * _shift_down(h, 1, sh, 0.0)
        dmult = jnp.where(row == 0, 0.0, g * (i * xc))
        gm = g * mult
        di = gm * xc
        dxc = gm * i
        dlog_a = da * a - dmult * (a * a) / mult
        dr = dlog_a * ((-LRU_C) * sp)
        dsp = jnp.sum(dlog_a * ((-LRU_C) * r), axis=0, keepdims=True)
        dlam = dsp * (-_sigmoid(-lam_v))
        dpa = dr * r * (1.0 - r)
        dpx = di * i * (1.0 - i)
        dpab = dpa.astype(BF16)
        dpxb = dpx.astype(BF16)
        dwa = _dot_tn(xcb, dpab)
        dwx = _dot_tn(xcb, dpxb)
        dba = jnp.sum(dpa, axis=0, keepdims=True)
        dbx = jnp.sum(dpx, axis=0, keepdims=True)
        dxc = dxc + _dot_nt(dpab, wa_v) + _dot_nt(dpxb, wx_v)
        dcb = jnp.sum(dxc, axis=0, keepdims=True)
        dx = cw[3:4, :] * dxc
        dcw_rows = [None] * CONVW
        dcw_rows[3] = jnp.sum(dxc * x, axis=0, keepdims=True)
        dxc_up = sh.up(dxc, (1, 2, 3), 0.0)
        for j in (1, 2, 3):
            dx = dx + cw[3 - j:4 - j, :] * dxc_up[j - 1]
            dcw_rows[3 - j] = jnp.sum(dxc * xs[j - 1], axis=0, keepdims=True)
        dx_ref[0] = dx.astype(BF16)
        dcw = jnp.concatenate(dcw_rows, axis=0)
        first = pl.program_id(1) == 0

        @pl.when(first)
        def _():
            dcw_ref[...] = dcw
            dcb_ref[...] = dcb
            dwa_ref[0] = dwa
            dba_ref[...] = dba
            dwx_ref[0] = dwx
            dbx_ref[...] = dbx
            dlam_ref[...] = dlam

        @pl.when(jnp.logical_not(first))
        def _():
            dcw_ref[...] += dcw
            dcb_ref[...] += dcb
            dwa_ref[0] += dwa
            dba_ref[...] += dba
            dwx_ref[0] += dwx
            dbx_ref[...] += dbx
            dlam_ref[...] += dlam

    slab = lambda: pl.BlockSpec((1, S, RBW), lambda n, b: (b, 0, n))
    vec = lambda: pl.BlockSpec((1, RBW), lambda n, b: (0, n))
    mat = lambda: pl.BlockSpec((1, RBW, RBW), lambda n, b: (n, 0, 0))
    taps = lambda: pl.BlockSpec((CONVW, RBW), lambda n, b: (0, n))
    vshape = jax.ShapeDtypeStruct((1, DR), F32)
    mshape = jax.ShapeDtypeStruct((NRB, RBW, RBW), F32)
    return pl.pallas_call(
        body, grid=(NRB, BL), name="rnn_bwd",
        in_specs=[slab(), slab(), slab(), taps(), vec(), mat(), vec(), mat(), vec(), vec(),
                  pl.BlockSpec(memory_space=pl.ANY), pl.BlockSpec((8, LANES), lambda n, b: (0, 0))],
        out_specs=[slab(), taps(), vec(), mat(), vec(), mat(), vec(), vec()],
        out_shape=[jax.ShapeDtypeStruct((BL, S, A_W), BF16), jax.ShapeDtypeStruct((CONVW, DR), F32),
                   vshape, mshape, vshape, mshape, vshape, vshape],
        input_output_aliases={10: 0},
        scratch_shapes=[pltpu.VMEM((N_SHIFT_BUFS, S + 2 * SUBLANES, RBW), F32)],
        compiler_params=_cp(("parallel", "arbitrary"), 48 * 1024 * 1024),
    )(*_hbm(proj3, h3, dh3, conv_w, conv_b, wa, ba, wx, bx, lam, slab_a3, token))


NQB = S // QB


def _rms_head(t, gain):
    rstd = lax.rsqrt(jnp.mean(t * t, axis=-1, keepdims=True) + EPS)
    return t * rstd * gain


def _rope(t, cs, sn):
    return t * cs + pltpu.roll(t, HD // 2, 1) * sn


def _rope_t(dy, cs, sn):
    return dy * cs - pltpu.roll(dy, HD // 2, 1) * sn


def _bdot_nt(a, b):
    return lax.dot_general(a, b, (((2,), (2,)), ((0,), (0,))), preferred_element_type=F32)


def _bdot(a, b):
    return lax.dot_general(a, b, (((2,), (1,)), ((0,), (0,))), preferred_element_type=F32)


def _bdot_tn(a, b):
    return lax.dot_general(a, b, (((1,), (1,)), ((0,), (0,))), preferred_element_type=F32)


STRIDE_MAX = 4


def _permute(buf, x, dil, dst, off=0):
    ln = S // dil
    if dil == 1:
        dst[pl.ds(off, S), :] = x.astype(dst.dtype)
        return
    buf[0] = x
    if dil <= STRIDE_MAX:
        for c in range(dil):
            dst[pl.ds(off + c * ln, ln), :] = buf.at[0][pl.ds(c, ln, stride=dil), :].astype(dst.dtype)
        return
    f, r = STRIDE_MAX, dil // STRIDE_MAX
    part = S // f
    for c1 in range(f):
        buf.at[1][pl.ds(c1 * part, part), :] = buf.at[0][pl.ds(c1, part, stride=f), :]
    for c1 in range(f):
        for c2 in range(r):
            dst[pl.ds(off + (c1 + f * c2) * ln, ln), :] = (
                buf.at[1][pl.ds(c1 * part + c2, ln, stride=r), :].astype(dst.dtype))


def _unpermute(buf, xp, dil, dst):
    ln = S // dil
    if dil == 1:
        dst[...] = xp
        return
    if dil <= STRIDE_MAX:
        for c in range(dil):
            dst[pl.ds(c, ln, stride=dil), :] = xp[c * ln:(c + 1) * ln]
        return
    f, r = STRIDE_MAX, dil // STRIDE_MAX
    part = S // f
    for c1 in range(f):
        for c2 in range(r):
            c = c1 + f * c2
            buf.at[1][pl.ds(c1 * part + c2, ln, stride=r), :] = xp[c * ln:(c + 1) * ln]
    for c1 in range(f):
        dst[pl.ds(c1, part, stride=f), :] = buf[1, pl.ds(c1 * part, part), :]


def _blocks3(ref, off=0):
    return ref[pl.ds(off, S), :].reshape(NQB, QB, HD)


def _att_prep(q_ref, k_ref, v_ref, cos_ref, sin_ref, qn, kn, dil, nat, qs, ksp, vsp):
    cs = cos_ref[...]
    sn = sin_ref[...]
    zero = jnp.zeros((QB, HD), BF16)
    ksp[pl.ds(0, QB), :] = zero
    vsp[pl.ds(0, QB), :] = zero
    _permute(nat, _rope(_rms_head(q_ref[0].astype(F32), qn), cs, sn), dil, qs)
    _permute(nat, _rope(_rms_head(k_ref[0].astype(F32), kn), cs, sn), dil, ksp, QB)
    _permute(nat, v_ref[0].astype(F32), dil, vsp, QB)


def _att_scores(qs, ksp, dil):
    nb = S // dil // QB
    q3 = _blocks3(qs)
    shape = (NQB, QB, QB)
    qi = lax.broadcasted_iota(jnp.int32, shape, 1)
    kj = lax.broadcasted_iota(jnp.int32, shape, 2)
    s_c = jnp.where(qi >= kj, _bdot_nt(q3, _blocks3(ksp, QB)) * SCALE, NEG)
    if nb == 1:
        return q3, s_c, None
    jj = lax.broadcasted_iota(jnp.int32, shape, 0)
    ok = (kj >= qi) & ((jj & (nb - 1)) != 0)
    s_p = jnp.where(ok, _bdot_nt(q3, _blocks3(ksp)) * SCALE, NEG)
    return q3, s_c, s_p


def _qkv_spec(kind, g):
    base = OFF_Q // HD + (3 * g + kind) * NH
    return pl.BlockSpec((1, S, HD), lambda b, h: (b, 0, base + h))


def _attn_fwd(proj3, cos_t, sin_t, q_norm, k_norm):
    def body(*refs):
        qkv_refs = refs[:9]
        cos_ref, sin_ref, qn_ref, kn_ref, att_ref, lse_ref, w_ref, nat, qs, ksp, vsp, og = refs[9:]
        for g, (window, dil) in enumerate(PATTERNS):
            q_ref, k_ref, v_ref = qkv_refs[3 * g:3 * g + 3]
            _att_prep(q_ref, k_ref, v_ref, cos_ref, sin_ref, qn_ref[g:g + 1, :], kn_ref[g:g + 1, :], dil,
                      nat, qs, ksp, vsp)
            _, s_c, s_p = _att_scores(qs, ksp, dil)
            m = jnp.max(s_c, axis=-1, keepdims=True)
            if s_p is not None:
                m = jnp.maximum(m, jnp.max(s_p, axis=-1, keepdims=True))
            e_c = jnp.exp(s_c - m)
            den = jnp.sum(e_c, axis=-1, keepdims=True)
            o = _bdot(e_c.astype(BF16), _blocks3(vsp, QB))
            if s_p is not None:
                e_p = jnp.exp(s_p - m)
                den = den + jnp.sum(e_p, axis=-1, keepdims=True)
                o = o + _bdot(e_p.astype(BF16), _blocks3(vsp))
            _unpermute(nat, (o / den).reshape(S, HD), dil, og.at[g])
            _unpermute(nat, jnp.broadcast_to(m + jnp.log(den), (NQB, QB, HD)).reshape(S, HD), dil,
                       lse_ref.at[g, 0])
        l0 = lse_ref[0, 0]
        l1 = lse_ref[1, 0]
        l2 = lse_ref[2, 0]
        mx = jnp.maximum(jnp.maximum(l0, l1), l2)
        e0 = jnp.exp(l0 - mx)
        e1 = jnp.exp(l1 - mx)
        e2 = jnp.exp(l2 - mx)
        inv = 1.0 / (e0 + e1 + e2)
        w0 = e0 * inv
        w1 = e1 * inv
        w2 = e2 * inv
        w_ref[0, 0] = w0
        w_ref[1, 0] = w1
        w_ref[2, 0] = w2
        att_ref[0] = w0 * og[0] + w1 * og[1] + w2 * og[2]

    in_specs = [_qkv_spec(kind, g) for g in range(NG) for kind in range(3)]
    in_specs += [pl.BlockSpec((S, HD), lambda b, h: (0, 0)), pl.BlockSpec((S, HD), lambda b, h: (0, 0)),
                 pl.BlockSpec((NG, HD), lambda b, h: (0, 0)), pl.BlockSpec((NG, HD), lambda b, h: (0, 0))]
    stat = lambda: pl.BlockSpec((NG, 1, S, HD), lambda b, h: (0, b, 0, h))
    return pl.pallas_call(
        body, grid=(BL, NH), name="attn_fwd",
        in_specs=in_specs,
        out_specs=[pl.BlockSpec((1, S, HD), lambda b, h: (b, 0, h)), stat(), stat()],
        out_shape=[jax.ShapeDtypeStruct((BL, S, ATT), F32),
                   jax.ShapeDtypeStruct((NG, BL, S, ATT), F32),
                   jax.ShapeDtypeStruct((NG, BL, S, ATT), F32)],
        scratch_shapes=[pltpu.VMEM((2, S, HD), F32), pltpu.VMEM((S, HD), BF16), pltpu.VMEM((S + QB, HD), BF16),
                        pltpu.VMEM((S + QB, HD), BF16), pltpu.VMEM((NG, S, HD), F32)],
        compiler_params=_cp(("parallel", "parallel"), VMEM_BIG),
    )(*_hbm(*([proj3] * 9), cos_t, sin_t, q_norm, k_norm))


def _attn_bwd_group(g, proj3, cos_t, sin_t, qn_g, kn_g, lse, wts, datt3, sbar3, slabs):
    dil = PATTERNS[g][1]
    n_alias = 0 if slabs is None else 3

    def norm_rope_bwd(dpost, raw, gain, cs, sn):
        dn = _rope_t(dpost, cs, sn)
        rstd = lax.rsqrt(jnp.mean(raw * raw, axis=-1, keepdims=True) + EPS)
        xh = raw * rstd
        dgain = jnp.sum(dn * xh, axis=0, keepdims=True)
        gd = dn * gain
        draw = rstd * (gd - xh * jnp.mean(gd * xh, axis=-1, keepdims=True))
        return draw, dgain

    def body(*refs):
        (q_ref, k_ref, v_ref, cos_ref, sin_ref, qn_ref, kn_ref, lse_ref, w_ref, datt_ref, sbar_ref) = refs[:11]
        (dq_ref, dk_ref, dv_ref, dqn_ref, dkn_ref, nat, qs, ksp, vsp, dos, cvp, lsp, acc) = refs[11 + n_alias:]
        qn = qn_ref[...]
        kn = kn_ref[...]
        cs = cos_ref[...]
        sn = sin_ref[...]
        _att_prep(q_ref, k_ref, v_ref, cos_ref, sin_ref, qn, kn, dil, nat, qs, ksp, vsp)
        wv = w_ref[0, 0]
        _permute(nat, wv * datt_ref[0], dil, dos)
        _permute(nat, wv * sbar_ref[0], dil, cvp)
        _permute(nat, lse_ref[0, 0], dil, lsp)
        q3, s_c, s_p = _att_scores(qs, ksp, dil)
        do3 = _blocks3(dos)
        lse3 = _blocks3(lsp)[:, :, 0:1]
        cv3 = _blocks3(cvp)[:, :, 0:1]
        p_c = jnp.exp(s_c - lse3)
        ds_c = (p_c * (_bdot_nt(do3, _blocks3(vsp, QB)) - cv3)).astype(BF16)
        dq = _bdot(ds_c, _blocks3(ksp, QB))
        acc[0] = _bdot_tn(ds_c, q3).reshape(S, HD)
        acc[1] = _bdot_tn(p_c.astype(BF16), do3).reshape(S, HD)
        if s_p is not None:
            p_p = jnp.exp(s_p - lse3)
            ds_p = (p_p * (_bdot_nt(do3, _blocks3(vsp)) - cv3)).astype(BF16)
            dq = dq + _bdot(ds_p, _blocks3(ksp))
            early = pl.ds(0, S - QB)
            acc[0, early, :] += _bdot_tn(ds_p, q3).reshape(S, HD)[QB:]
            acc[1, early, :] += _bdot_tn(p_p.astype(BF16), do3).reshape(S, HD)[QB:]
        _unpermute(nat, (dq * SCALE).reshape(S, HD), dil, nat.at[0])
        draw, dqn = norm_rope_bwd(nat[0], q_ref[0].astype(F32), qn, cs, sn)
        dq_ref[0] = draw.astype(BF16)
        _unpermute(nat, acc[0] * SCALE, dil, nat.at[0])
        draw, dkn = norm_rope_bwd(nat[0], k_ref[0].astype(F32), kn, cs, sn)
        dk_ref[0] = draw.astype(BF16)
        _unpermute(nat, acc[1], dil, nat.at[0])
        dv_ref[0] = nat[0].astype(BF16)
        first = (pl.program_id(0) == 0) & (pl.program_id(1) == 0)

        @pl.when(first)
        def _():
            dqn_ref[...] = dqn
            dkn_ref[...] = dkn

        @pl.when(jnp.logical_not(first))
        def _():
            dqn_ref[...] += dqn
            dkn_ref[...] += dkn

    full = lambda r: pl.BlockSpec((r, HD), lambda b, h: (0, 0))
    stat = lambda: pl.BlockSpec((1, 1, S, HD), lambda b, h: (g, b, 0, h))
    slab = lambda: pl.BlockSpec((1, S, HD), lambda b, h: (b, 0, h))
    out_slab = lambda: pl.BlockSpec((1, S, HD), lambda b, h: (b, 0, g * NH + h))
    big = jax.ShapeDtypeStruct((BL, S, GW), BF16)
    vecs = jax.ShapeDtypeStruct((1, HD), F32)
    in_specs = [_qkv_spec(0, g), _qkv_spec(1, g), _qkv_spec(2, g), full(S), full(S), full(1), full(1),
                stat(), stat(), slab(), slab()]
    args = [proj3, proj3, proj3, cos_t, sin_t, qn_g, kn_g, lse, wts, datt3, sbar3]
    aliases = {}
    if slabs is not None:
        in_specs += [pl.BlockSpec(memory_space=pl.ANY)] * 3
        args += list(slabs)
        aliases = {11: 0, 12: 1, 13: 2}
    return pl.pallas_call(
        body, grid=(BL, NH), name="attn_bwd_g%d" % g,
        in_specs=in_specs,
        out_specs=[out_slab(), out_slab(), out_slab(), full(1), full(1)],
        out_shape=[big, big, big, vecs, vecs],
        scratch_shapes=[pltpu.VMEM((2, S, HD), F32), pltpu.VMEM((S, HD), BF16), pltpu.VMEM((S + QB, HD), BF16),
                        pltpu.VMEM((S + QB, HD), BF16), pltpu.VMEM((S, HD), BF16), pltpu.VMEM((S, HD), F32),
                        pltpu.VMEM((S, HD), F32), pltpu.VMEM((2, S, HD), F32)],
        input_output_aliases=aliases,
        compiler_params=_cp(("arbitrary", "arbitrary"), VMEM_BIG),
    )(*_hbm(*args))


def _tail(x, proj, h, att, p, tgt, w_o_rnn, w_o_att_t, w_out, w_pg, w_ple_t, norm_ple, b_pg, tm=256):
    nt = T // tm
    inv_d = 1.0 / D

    def body(x_ref, h_ref, zr_ref, att_ref, za_ref, g0a_ref, g0b_ref, g1a_ref, g1b_ref, p_ref, tgt_ref,
             np_ref, bpg_ref, wor_hbm, woa_hbm, wout_hbm, wpg_hbm, wple_hbm,
             dx1_ref, merged_ref, n1_ref, dpre_ref, dpe_ref, dyr_ref, dya_ref, slab_a_ref, slab_c_ref, dh_ref,
             datt_ref, sbar_ref, yrnn_ref, yatt_ref, loss_ref, dnp_ref, dbpg_ref,
             wor, woa, wout, wpg, wple):
        first = pl.program_id(0) == 0

        @pl.when(first)
        def _():
            pltpu.sync_copy(wor_hbm, wor)
            pltpu.sync_copy(woa_hbm, woa)
            pltpu.sync_copy(wout_hbm, wout)
            pltpu.sync_copy(wpg_hbm, wpg)
            pltpu.sync_copy(wple_hbm, wple)

        xv = x_ref[...]
        hv = h_ref[...]
        zr = zr_ref[...].astype(F32)
        av = att_ref[...]
        za = za_ref[...].astype(F32)
        szr = _sigmoid(zr)
        silu_r = zr * szr
        yrnn_b = (hv * silu_r).astype(BF16)
        sza = _sigmoid(za)
        silu_a = za * sza
        yatt_b = (av * silu_a).astype(BF16)
        yrnn_ref[...] = yrnn_b
        yatt_ref[...] = yatt_b
        yr = _dot(yrnn_b, wor[...])
        ya = _dot_nt(yatt_b, woa[...])
        g0 = _sigmoid(jnp.concatenate([g0a_ref[...], g0b_ref[...]], axis=1).astype(F32))
        g1 = _sigmoid(jnp.concatenate([g1a_ref[...], g1b_ref[...]], axis=1).astype(F32))
        merged_b = (g0 * yr + g1 * ya).astype(BF16)
        merged_ref[...] = merged_b
        x1 = xv + _dot(merged_b, wout[...])
        rstd = lax.rsqrt(jnp.mean(x1 * x1, axis=-1, keepdims=True) + EPS)
        xh = x1 * rstd
        npl = np_ref[...]
        n1_b = (xh * npl).astype(BF16)
        n1_ref[...] = n1_b
        pg = _sigmoid(_dot(n1_b, wpg[...]) + bpg_ref[...])
        pe = _dot_nt(p_ref[...].astype(BF16), wple[...])
        err = x1 + pg * pe - tgt_ref[...]
        loss_t = 0.5 * inv_d * jnp.sum(err * err)
        dy = err * inv_d
        dpe_ref[...] = (dy * pg).astype(BF16)
        dpre = dy * pe * pg * (1.0 - pg)
        dpre_b = dpre.astype(BF16)
        dpre_ref[...] = dpre_b
        dn1 = _dot_nt(dpre_b, wpg[...])
        dnp = jnp.sum(dn1 * xh, axis=0, keepdims=True)
        dbpg = jnp.sum(dpre, axis=0, keepdims=True)
        gd = dn1 * npl
        dx1 = dy + rstd * (gd - xh * jnp.mean(gd * xh, axis=-1, keepdims=True))
        dx1_ref[...] = dx1
        dmerged = _dot_nt(dx1.astype(BF16), wout[...])
        dyr_b = (dmerged * g0).astype(BF16)
        dya_b = (dmerged * g1).astype(BF16)
        dyr_ref[...] = dyr_b
        dya_ref[...] = dya_b
        slab_c_ref[:, ATT:ATT + D] = (dmerged * yr * g0 * (1.0 - g0)).astype(BF16)
        slab_c_ref[:, ATT + D:ATT + 2 * D] = (dmerged * ya * g1 * (1.0 - g1)).astype(BF16)
        dyrnn = _dot_nt(dyr_b, wor[...])
        dyatt = _dot(dya_b, woa[...])
        dh_ref[...] = dyrnn * silu_r
        slab_a_ref[...] = (dyrnn * hv * szr * (1.0 + zr * (1.0 - szr))).astype(BF16)
        datt = dyatt * silu_a
        datt_ref[...] = datt
        slab_c_ref[:, 0:ATT] = (dyatt * av * sza * (1.0 + za * (1.0 - sza))).astype(BF16)
        da = datt * av
        for hh in range(NH):
            seg = slice(hh * HD, (hh + 1) * HD)
            sbar_ref[:, seg] = jnp.broadcast_to(jnp.sum(da[:, seg], axis=-1, keepdims=True), (tm, HD))

        @pl.when(first)
        def _():
            loss_ref[...] = jnp.full((8, LANES), loss_t, F32)
            dnp_ref[...] = dnp
            dbpg_ref[...] = dbpg

        @pl.when(jnp.logical_not(first))
        def _():
            loss_ref[...] += jnp.full((8, LANES), loss_t, F32)
            dnp_ref[...] += dnp
            dbpg_ref[...] += dbpg

    tok = lambda w: pl.BlockSpec((tm, w), lambda i: (i, 0))
    col = lambda w, blk: pl.BlockSpec((tm, w), lambda i: (i, blk))
    vec = lambda: pl.BlockSpec((1, D), lambda i: (0, 0))
    hbm = lambda: pl.BlockSpec(memory_space=pl.ANY)
    gb = OFF_G // 512
    in_specs = [tok(D), tok(DR), col(DR, 1), tok(ATT), col(ATT, OFF_ZA // ATT),
                col(512, gb), col(512, gb + 1), col(512, gb + 2), col(512, gb + 3),
                tok(PLE), tok(D), vec(), vec(), hbm(), hbm(), hbm(), hbm(), hbm()]
    sh = lambda w, dt: jax.ShapeDtypeStruct((T, w), dt)
    out_shape = [sh(D, F32), sh(D, BF16), sh(D, BF16), sh(D, BF16), sh(D, BF16), sh(D, BF16), sh(D, BF16),
                 sh(A_W, BF16), sh(C_W, BF16), sh(DR, F32), sh(ATT, F32), sh(ATT, F32),
                 sh(DR, BF16), sh(ATT, BF16),
                 jax.ShapeDtypeStruct((8, LANES), F32), jax.ShapeDtypeStruct((1, D), F32),
                 jax.ShapeDtypeStruct((1, D), F32)]
    out_specs = [tok(D), tok(D), tok(D), tok(D), tok(D), tok(D), tok(D), col(DR, 1), tok(C_W), tok(DR),
                 tok(ATT), tok(ATT), tok(DR), tok(ATT),
                 pl.BlockSpec((8, LANES), lambda i: (0, 0)), vec(), vec()]
    return pl.pallas_call(
        body, grid=(nt,), name="tail_fwd_bwd",
        in_specs=in_specs, out_specs=out_specs, out_shape=out_shape,
        scratch_shapes=[pltpu.VMEM((DR, D), BF16), pltpu.VMEM((D, ATT), BF16), pltpu.VMEM((D, D), BF16),
                        pltpu.VMEM((D, D), BF16), pltpu.VMEM((D, PLE), BF16)],
        compiler_params=_cp(("arbitrary",), VMEM_BIG),
    )(*_hbm(x, h, proj, att, proj, proj, proj, proj, proj, p, tgt, norm_ple, b_pg, w_o_rnn, w_o_att_t, w_out, w_pg,
            w_ple_t))


def _input_norm_bwd(x, dhn, dx1, gain, tm=512):
    def body(x_ref, dhn_ref, dx1_ref, g_ref, dx_ref, dg_ref):
        xv = x_ref[...]
        rstd = lax.rsqrt(jnp.mean(xv * xv, axis=-1, keepdims=True) + EPS)
        xh = xv * rstd
        dn = dhn_ref[...]
        dg = jnp.sum(dn * xh, axis=0, keepdims=True)
        gd = dn * g_ref[...]
        dx_ref[...] = dx1_ref[...] + rstd * (gd - xh * jnp.mean(gd * xh, axis=-1, keepdims=True))
        first = pl.program_id(0) == 0

        @pl.when(first)
        def _():
            dg_ref[...] = dg

        @pl.when(jnp.logical_not(first))
        def _():
            dg_ref[...] += dg

    tok = lambda: pl.BlockSpec((tm, D), lambda i: (i, 0))
    vec = lambda: pl.BlockSpec((1, D), lambda i: (0, 0))
    return pl.pallas_call(
        body, grid=(T // tm,), name="input_norm_bwd",
        in_specs=[tok(), tok(), tok(), vec()], out_specs=[tok(), vec()],
        out_shape=[jax.ShapeDtypeStruct((T, D), F32), jax.ShapeDtypeStruct((1, D), F32)],
        compiler_params=_cp(("arbitrary",), VMEM_MID),
    )(*_hbm(x, dhn, dx1, gain))


def _rope_tables():
    pos = jnp.arange(S, dtype=F32)
    inv_freq = ROPE_THETA ** (-jnp.arange(0, HD, 2, dtype=F32) / HD)
    ang = pos[:, None] * inv_freq[None, :]
    cos, sin = jnp.cos(ang), jnp.sin(ang)
    return jnp.concatenate([cos, cos], axis=1), jnp.concatenate([-sin, sin], axis=1)


def _local_step(x, p, tgt, project, other_weights, norm_mix, conv_b,
                w_rg_a, b_rg_a, w_rg_x, b_rg_x, lam, q_norm, k_norm, norm_ple, b_pg, start_reduce=None,
                entry_token=None):
    if start_reduce is None:
        start_reduce = lambda arrs, tag: (jnp.zeros((8, LANES), F32), arrs)
    if entry_token is None:
        entry_token = jnp.zeros((8, LANES), F32)
    cos_t, sin_t = _rope_tables()
    wa_b = w_rg_a.astype(BF16)
    wx_b = w_rg_x.astype(BF16)

    hn = _rmsnorm_fwd(x, norm_mix, entry_token)
    proj, w_bufs, chips, conv_w, token = project(hn)
    proj3 = proj.reshape(BL, S, NIN)
    h3 = _rnn_fwd(proj3, conv_w, conv_b, wa_b, b_rg_a, wx_b, b_rg_x, lam, token)
    att3, lse, wts = _attn_fwd(proj3, cos_t, sin_t, q_norm, k_norm)
    w_o_rnn, w_o_att_t, w_out, w_pg, w_ple_t = other_weights(att3)
    (dx1, merged, n1, dpre, dpe, dyr, dya, slab_a, slab_c, dh, datt, sbar, yrnn, yatt, loss8, dnp, dbpg) = _tail(
        x, proj, h3.reshape(T, DR), att3.reshape(T, ATT), p, tgt, w_o_rnn, w_o_att_t, w_out, w_pg, w_ple_t,
        norm_ple, b_pg)

    token, pending_out = start_reduce([
        _mm_tn(yrnn, dyr, 640, 2048, "dw_o_rnn"),
        _mm_tn(dya, yatt, 512, 2048, "dw_o_att_t"),
        _mm_tn(merged, dx1, 512, 2048, "dw_out"),
        _mm_tn(n1, dpre, 512, 2048, "dw_ple_gate"),
        _mm_tn(dpe, p, 512, 2048, "dw_ple_t")], "out")

    slab_a3, dcw, dcb, dwa, dba, dwx, dbx, dlam = _rnn_bwd(
        proj3, h3, dh.reshape(BL, S, DR), slab_a.reshape(BL, S, A_W), conv_w, conv_b, wa_b, b_rg_a, wx_b, b_rg_x, lam,
        token)
    datt3 = datt.reshape(BL, S, ATT)
    sbar3 = sbar.reshape(BL, S, ATT)
    slabs = None
    dqn = []
    dkn = []
    for g in range(NG):
        dq, dk, dv, dqn_g, dkn_g = _attn_bwd_group(g, proj3, cos_t, sin_t, q_norm[g:g + 1], k_norm[g:g + 1],
                                                   lse, wts, datt3, sbar3, slabs)
        slabs = (dq, dk, dv)
        dqn.append(dqn_g)
        dkn.append(dkn_g)
    pieces = [slab_a3.reshape(T, A_W)] + [t.reshape(T, GW) for t in slabs] + [slab_c]
    dw_in_t, db_in = _dw_in(pieces, hn)
    token, pending_in = start_reduce([dw_in_t], "in")
    grad_x, dnm = _grad_x(pieces, w_bufs, chips, token, x, dx1, norm_mix)

    small = dict(w_rg_a=dwa, w_rg_x=dwx, norm_mix=dnm, b_in=db_in, conv_b=dcb, b_rg_a=dba, b_rg_x=dbx,
                 lru_lambda=dlam, q_norm=dqn, k_norm=dkn, norm_ple=dnp, b_ple_gate=dbpg, conv_w=dcw, loss=loss8)
    return grad_x, pending_out, pending_in, small


MESH = pl.DeviceIdType.MESH
HBM_SPEC = pl.BlockSpec(memory_space=pl.ANY)


def _my_pos():
    return lax.axis_index("x"), lax.axis_index("y"), lax.axis_index("c")


def _flip(pos, k):
    x, y, c = pos
    return (1 - x if k & 4 else x, 1 - y if k & 2 else y, 1 - c if k & 1 else c)


def _lin(pos):
    return 4 * pos[0] + 2 * pos[1] + pos[2]


def _chip(pos):
    return 2 * pos[0] + pos[1]


def _all_gather_two_level(shards, name):
    na = len(shards)

    def body(*refs):
        x_refs = refs[:na]
        out_refs = refs[na:2 * na]
        send_sems, recv_sems, local_sems = refs[2 * na:]
        me = _my_pos()
        sibling = _flip(me, 1)
        chips = [_flip(me, 4), _flip(me, 2), _flip(me, 6)]

        def copy(i, k, block, to, from_x=False):
            dst = out_refs[i].at[_lin(block)]
            return pltpu.make_async_remote_copy(
                src_ref=x_refs[i] if from_x else dst, dst_ref=dst,
                send_sem=send_sems.at[7 * i + k], recv_sem=recv_sems.at[7 * i + k], device_id=to, device_id_type=MESH)

        started = []
        for i in range(na):
            mine = pltpu.make_async_copy(x_refs[i], out_refs[i].at[_lin(me)], local_sems.at[i])
            mine.start()
            started.append(mine)
        sends = []
        for i in range(na):
            cps = [copy(i, 0, me, sibling, True)] + [copy(i, 1 + j, me, chip, True) for j, chip in enumerate(chips)]
            for cp in cps:
                cp.start()
            sends += cps
        for i in range(na):
            for j, chip in enumerate(chips):
                copy(i, 1 + j, chip, me).wait_recv()
                fwd = copy(i, 4 + j, chip, sibling)
                fwd.start()
                sends.append(fwd)
        for i in range(na):
            copy(i, 0, sibling, me).wait_recv()
            for j, chip in enumerate(chips):
                copy(i, 4 + j, _flip(chip, 1), me).wait_recv()
        for cp in sends:
            cp.wait_send()
        for mine in started:
            mine.wait()

    return pl.pallas_call(
        body, name=name,
        out_shape=[jax.ShapeDtypeStruct((NDEV,) + s.shape, s.dtype) for s in shards],
        in_specs=[HBM_SPEC] * na, out_specs=[HBM_SPEC] * na,
        scratch_shapes=[pltpu.SemaphoreType.DMA((7 * na,)), pltpu.SemaphoreType.DMA((7 * na,)),
                        pltpu.SemaphoreType.DMA((na,))],
    )(*shards)


def _all_gather_direct(shard, name):
    def body(x_ref, out_ref, send_sems, recv_sems, local_sem):
        me = _my_pos()
        mine = pltpu.make_async_copy(x_ref, out_ref.at[_lin(me)], local_sem)
        mine.start()
        sends = []
        for k in range(1, NDEV):
            cp = pltpu.make_async_remote_copy(
                src_ref=x_ref, dst_ref=out_ref.at[_lin(me)], send_sem=send_sems.at[k - 1],
                recv_sem=recv_sems.at[k - 1], device_id=_flip(me, k), device_id_type=MESH)
            cp.start()
            sends.append(cp)
        for k in range(1, NDEV):
            peer = _flip(me, k)
            pltpu.make_async_remote_copy(
                src_ref=x_ref, dst_ref=out_ref.at[_lin(peer)], send_sem=send_sems.at[k - 1],
                recv_sem=recv_sems.at[k - 1], device_id=peer, device_id_type=MESH).wait_recv()
        for cp in sends:
            cp.wait_send()
        mine.wait()

    return pl.pallas_call(
        body, name=name,
        out_shape=jax.ShapeDtypeStruct((NDEV,) + shard.shape, shard.dtype),
        in_specs=[HBM_SPEC], out_specs=HBM_SPEC,
        scratch_shapes=[pltpu.SemaphoreType.DMA((7,)), pltpu.SemaphoreType.DMA((7,)), pltpu.SemaphoreType.DMA],
    )(shard)


def _exchange_within_chip(parts, name):
    na = len(parts)

    def body(*refs):
        a_refs = refs[:na]
        recv_refs = refs[na:2 * na]
        send_sems, recv_sems = refs[2 * na:]
        me = _my_pos()
        c = me[2]
        sibling = _flip(me, 1)
        remote = []
        for i in range(na):
            for q in range(NCHIP):
                rc = pltpu.make_async_remote_copy(
                    src_ref=a_refs[i].at[q, 1 - c], dst_ref=recv_refs[i].at[q],
                    send_sem=send_sems.at[NCHIP * i + q], recv_sem=recv_sems.at[NCHIP * i + q],
                    device_id=sibling, device_id_type=MESH)
                rc.start()
                remote.append(rc)
        for rc in remote:
            rc.wait_recv()
        for rc in remote:
            rc.wait_send()

    return pl.pallas_call(
        body, name=name, out_shape=[jax.ShapeDtypeStruct((NCHIP,) + a.shape[2:], a.dtype) for a in parts],
        in_specs=[HBM_SPEC] * na, out_specs=[HBM_SPEC] * na,
        scratch_shapes=[pltpu.SemaphoreType.DMA((NCHIP * na,)), pltpu.SemaphoreType.DMA((NCHIP * na,))],
    )(*parts)


HBM_ONLY = pl.BlockSpec(memory_space=pltpu.HBM)
SEM_SPEC = pl.BlockSpec(memory_space=pltpu.SEMAPHORE)
SPLIT_COPY = pltpu.CompilerParams(has_side_effects=pltpu.SideEffectType.DATAFLOW_SIDE_EFFECTING)


def _chip_peers(me):
    return [_flip(me, 4), _flip(me, 2), _flip(me, 6)]


def _between_chips_start(parts, name):
    na = len(parts)

    def body(*refs):
        a_refs = refs[:na]
        land_refs = refs[na:2 * na]
        send_sems, recv_sems = refs[2 * na], refs[2 * na + 1]
        token = refs[-1]
        me = _my_pos()
        myq = _chip(me)
        for i in range(na):
            for j, peer in enumerate(_chip_peers(me)):
                pltpu.make_async_remote_copy(
                    src_ref=a_refs[i].at[_chip(peer)], dst_ref=land_refs[i].at[myq],
                    send_sem=send_sems.at[3 * i + j], recv_sem=recv_sems.at[3 * i + j],
                    device_id=peer, device_id_type=MESH).start()
        token[...] = jnp.zeros_like(token)

    hbm = [pltpu.HBM(a.shape, a.dtype) for a in parts]
    srcs = [pltpu.with_memory_space_constraint(a, pltpu.HBM) for a in parts]
    lands = [pltpu.with_memory_space_constraint(lax.empty(a.shape, a.dtype), pltpu.HBM) for a in parts]
    res = pl.pallas_call(
        body, name=name,
        out_shape=(pltpu.SemaphoreType.DMA((3 * na,)), pltpu.SemaphoreType.DMA((3 * na,)), *hbm, *hbm,
                   jax.ShapeDtypeStruct((8, LANES), F32)),
        in_specs=[HBM_ONLY] * (2 * na),
        out_specs=(SEM_SPEC, SEM_SPEC, *([HBM_ONLY] * (2 * na)), pl.BlockSpec(memory_space=pltpu.VMEM)),
        input_output_aliases={i: 2 + i for i in range(2 * na)},
        compiler_params=SPLIT_COPY,
    )(*srcs, *lands)
    return res[-1], (res[0], res[1], list(res[2:2 + na]), list(res[2 + na:2 + 2 * na]))


def _between_chips_wait(pending, after, name):
    send_sems, recv_sems, parts, lands = pending
    na = len(parts)

    def body(*refs):
        a_refs = refs[:na]
        land_refs = refs[na:2 * na]
        send_sems, recv_sems = refs[2 * na], refs[2 * na + 1]
        me = _my_pos()
        for i in range(na):
            for j, peer in enumerate(_chip_peers(me)):
                cp = pltpu.make_async_remote_copy(
                    src_ref=a_refs[i].at[_chip(peer)], dst_ref=land_refs[i].at[_chip(peer)],
                    send_sem=send_sems.at[3 * i + j], recv_sem=recv_sems.at[3 * i + j],
                    device_id=peer, device_id_type=MESH)
                cp.wait_send()
                cp.wait_recv()

    hbm = [pltpu.HBM(a.shape, a.dtype) for a in parts]
    res = pl.pallas_call(
        body, name=name, out_shape=(*hbm, *hbm),
        in_specs=[HBM_ONLY] * (2 * na) + [SEM_SPEC, SEM_SPEC, pl.BlockSpec(memory_space=pl.ANY)],
        out_specs=[HBM_ONLY] * (2 * na),
        input_output_aliases={i: i for i in range(2 * na)},
        compiler_params=SPLIT_COPY,
    )(*parts, *lands, send_sems, recv_sems, after)
    return list(res[:na]), list(res[na:])


def _remote(src, dst, send_sems, recv_sems, idx, peer):
    return pltpu.make_async_remote_copy(src_ref=src, dst_ref=dst, send_sem=send_sems.at[idx],
                                        recv_sem=recv_sems.at[idx], device_id=peer, device_id_type=MESH)


def _copies_own(bufs, me):
    return [(bufs[0], bufs[1].at[me[2]], 0, _flip(me, 1))]


def _copies_near(bufs, me):
    return [(bufs[0], bufs[1].at[0, me[2]], 0, _flip(me, 2)), (bufs[0], bufs[1].at[1, me[2]], 1, _flip(me, 4))]


def _copies_far(bufs, me):
    return [(bufs[0], bufs[1].at[me[2]], 0, _flip(me, 6))]


def _copies_others(bufs, me):
    na = len(bufs) // 2
    return [(bufs[i], bufs[na + i].at[_lin(me)], 7 * i + k - 1, _flip(me, k))
            for i in range(na) for k in range(1, NDEV)]


GROUP_COPIES = dict(own=_copies_own, near=_copies_near, far=_copies_far, others=_copies_others)
GROUP_COUNT = dict(own=1, near=2, far=1)


def _gather_start(bufs, groups, after, name):
    nb = len(bufs)
    ng = len(groups)

    def body(*refs):
        b = refs[:nb]
        sems = refs[nb + 1:nb + 1 + 2 * ng]
        token = refs[-1]
        me = _my_pos()
        for gi, (group, idx) in enumerate(groups):
            for src, dst, k, peer in GROUP_COPIES[group]([b[i] for i in idx], me):
                _remote(src, dst, sems[2 * gi], sems[2 * gi + 1], k, peer).start()
        token[...] = jnp.zeros_like(token)

    sem_t = []
    for group, idx in groups:
        cnt = 7 * (len(idx) // 2) if group == "others" else GROUP_COUNT[group]
        sem_t += [pltpu.SemaphoreType.DMA((cnt,)), pltpu.SemaphoreType.DMA((cnt,))]
    ins = [pltpu.with_memory_space_constraint(a, pltpu.HBM) for a in bufs]
    res = pl.pallas_call(
        body, name=name,
        out_shape=(*sem_t, *[pltpu.HBM(a.shape, a.dtype) for a in bufs], jax.ShapeDtypeStruct((8, LANES), F32)),
        in_specs=[HBM_ONLY] * nb + [pl.BlockSpec(memory_space=pl.ANY)],
        out_specs=(*([SEM_SPEC] * (2 * ng)), *([HBM_ONLY] * nb), pl.BlockSpec(memory_space=pltpu.VMEM)),
        input_output_aliases={i: 2 * ng + i for i in range(nb)},
        compiler_params=SPLIT_COPY,
    )(*ins, after)
    return res[-1], list(res[2 * ng:2 * ng + nb]), [(res[2 * gi], res[2 * gi + 1]) for gi in range(ng)]


def _gather_wait(group, send_sems, recv_sems, bufs, after, name):
    nb = len(bufs)
    copies = dict(own=_copies_own, near=_copies_near, far=_copies_far, others=_copies_others)[group]

    def body(*refs):
        b = refs[:nb]
        ss, rs = refs[nb], refs[nb + 1]
        me = _my_pos()
        for src, dst, idx, peer in copies(b, me):
            if group == "others":
                landed = b[nb // 2 + idx // 7].at[_lin(peer)]
            elif group == "own":
                landed = b[1].at[1 - me[2]]
            else:
                landed = dst
            cp = _remote(src, landed, ss, rs, idx, peer)
            cp.wait_send()
            cp.wait_recv()

    res = pl.pallas_call(
        body, name=name, out_shape=[pltpu.HBM(a.shape, a.dtype) for a in bufs],
        in_specs=[HBM_ONLY] * nb + [SEM_SPEC, SEM_SPEC, pl.BlockSpec(memory_space=pl.ANY)],
        out_specs=[HBM_ONLY] * nb,
        input_output_aliases={i: i for i in range(nb)},
        compiler_params=SPLIT_COPY,
    )(*bufs, send_sems, recv_sems, after)
    return list(res)


def _forward_to_sibling(buf, name):
    n = buf.shape[0]

    def body(_in_ref, out_ref, send_sems, recv_sems):
        me = _my_pos()
        c = me[2]
        sibling = _flip(me, 1)
        sends = []
        for r in range(n):
            cp = _remote(out_ref.at[r, c], out_ref.at[r, c], send_sems, recv_sems, r, sibling)
            cp.start()
            sends.append(cp)
        for r in range(n):
            _remote(out_ref.at[r, c], out_ref.at[r, 1 - c], send_sems, recv_sems, r, sibling).wait_recv()
        for cp in sends:
            cp.wait_send()

    return pl.pallas_call(
        body, name=name, out_shape=jax.ShapeDtypeStruct(buf.shape, buf.dtype),
        in_specs=[HBM_SPEC], out_specs=HBM_SPEC,
        scratch_shapes=[pltpu.SemaphoreType.DMA((n,)), pltpu.SemaphoreType.DMA((n,))],
        input_output_aliases={0: 0},
    )(buf)


def _allreduce_small(rep, conv, name):
    def body(rep_hbm, conv_hbm, out_ref, conv_out, stage_r, stage_c, send_sems, recv_sems, local_sems):
        me = _my_pos()
        mi = _lin(me)
        peers = [_flip(me, k) for k in range(1, NDEV)]
        own = [pltpu.make_async_copy(rep_hbm.at[mi], stage_r.at[mi], local_sems.at[0]),
               pltpu.make_async_copy(conv_hbm.at[mi], stage_c.at[mi], local_sems.at[1])]
        for cp in own:
            cp.start()
        sends = []
        for j, peer in enumerate(peers):
            for src, stage, base in ((rep_hbm, stage_r, 0), (conv_hbm, stage_c, 7)):
                cp = _remote(src.at[_lin(peer)], stage.at[mi], send_sems, recv_sems, base + j, peer)
                cp.start()
                sends.append(cp)
        for j, peer in enumerate(peers):
            for src, stage, base in ((rep_hbm, stage_r, 0), (conv_hbm, stage_c, 7)):
                _remote(src.at[mi], stage.at[_lin(peer)], send_sems, recv_sems, base + j, peer).wait_recv()
        for cp in own:
            cp.wait()
        acc_r = stage_r[0]
        acc_c = stage_c[0]
        for q in range(1, NDEV):
            acc_r = acc_r + stage_r[q]
            acc_c = acc_c + stage_c[q]
        out_ref[mi] = acc_r
        conv_out[...] = acc_c
        for j, peer in enumerate(peers):
            cp = _remote(out_ref.at[mi], out_ref.at[mi], send_sems, recv_sems, 14 + j, peer)
            cp.start()
            sends.append(cp)
        for j, peer in enumerate(peers):
            _remote(out_ref.at[mi], out_ref.at[_lin(peer)], send_sems, recv_sems, 14 + j, peer).wait_recv()
        for cp in sends:
            cp.wait_send()

    vmem = pl.BlockSpec(memory_space=pltpu.VMEM)
    return pl.pallas_call(
        body, name=name,
        out_shape=[jax.ShapeDtypeStruct(rep.shape, F32), jax.ShapeDtypeStruct(conv.shape[1:], F32)],
        in_specs=[HBM_SPEC, HBM_SPEC], out_specs=[vmem, vmem],
        scratch_shapes=[pltpu.VMEM(rep.shape, F32), pltpu.VMEM(conv.shape, F32),
                        pltpu.SemaphoreType.DMA((21,)), pltpu.SemaphoreType.DMA((21,)), pltpu.SemaphoreType.DMA((2,))],
    )(rep, conv)


def _exchange_all(parts, name):
    na = len(parts)

    def body(*refs):
        a_refs = refs[:na]
        out_refs = refs[na:2 * na]
        send_sems, recv_sems = refs[2 * na:]
        me = _my_pos()
        sends = []
        for i in range(na):
            for k in range(1, NDEV):
                peer = _flip(me, k)
                cp = pltpu.make_async_remote_copy(
                    src_ref=a_refs[i].at[_lin(peer)], dst_ref=out_refs[i].at[_lin(me)],
                    send_sem=send_sems.at[7 * i + k - 1], recv_sem=recv_sems.at[7 * i + k - 1],
                    device_id=peer, device_id_type=MESH)
                cp.start()
                sends.append(cp)
        for i in range(na):
            for k in range(1, NDEV):
                peer = _flip(me, k)
                pltpu.make_async_remote_copy(
                    src_ref=a_refs[i].at[_lin(peer)], dst_ref=out_refs[i].at[_lin(peer)],
                    send_sem=send_sems.at[7 * i + k - 1], recv_sem=recv_sems.at[7 * i + k - 1],
                    device_id=peer, device_id_type=MESH).wait_recv()
        for cp in sends:
            cp.wait_send()

    return pl.pallas_call(
        body, name=name, out_shape=[jax.ShapeDtypeStruct(a.shape, a.dtype) for a in parts],
        in_specs=[HBM_SPEC] * na, out_specs=[HBM_SPEC] * na,
        scratch_shapes=[pltpu.SemaphoreType.DMA((7 * na,)), pltpu.SemaphoreType.DMA((7 * na,))],
    )(*parts)


def _scalar(v):
    return jnp.asarray(v, jnp.int32).reshape(1)


def _sum_pairs(parts, theirs, name):
    na = len(parts)

    def body(c_ref, *refs):
        for i in range(na):
            o_ref = refs[2 * na + i]
            o_ref[0] = (refs[i][0, 0].astype(F32) + refs[na + i][0].astype(F32)).astype(o_ref.dtype)

    def mine_spec(a):
        return pl.BlockSpec((1, 1) + a.shape[2:], lambda q, c_ref: (q, c_ref[0], 0, 0))

    def spec(a):
        return pl.BlockSpec((1,) + a.shape[1:], lambda q, c_ref: (q, 0, 0))

    return pl.pallas_call(
        body, name=name,
        grid_spec=pltpu.PrefetchScalarGridSpec(
            num_scalar_prefetch=1, grid=(NCHIP,),
            in_specs=[mine_spec(a) for a in parts] + [spec(a) for a in theirs],
            out_specs=[spec(a) for a in theirs]),
        out_shape=[jax.ShapeDtypeStruct(a.shape, a.dtype) for a in theirs],
        compiler_params=_cp(("arbitrary",), VMEM_BIG),
    )(_scalar(lax.axis_index("c")), *parts, *theirs)


def _others(q, mine, nblk=NCHIP):
    return jnp.where(q == mine, (q + 1) % nblk, q)


def _sum_chips_adamw(own, recv, wv, mv, vv, tr, name):
    _, r, w = recv.shape

    def body(q_ref, own_ref, r0, r1, r2, r3, w_ref, m_ref, v_ref, g_ref, d_ref, m2_ref, v2_ref):
        myq = q_ref[0]
        acc = None
        for q, r_ref in enumerate((r0, r1, r2, r3)):
            term = jnp.where(myq == q, own_ref[0], r_ref[0]).astype(F32)
            acc = term if acc is None else acc + term
        g_ref[...] = acc
        delta, m2, v2 = _adam_math(w_ref[...], acc, m_ref[...], v_ref[...])
        d_ref[...] = delta
        m2_ref[...] = m2
        v2_ref[...] = v2

    def recv_spec(q):
        return pl.BlockSpec((1, tr, w), lambda i, q_ref: (_others(q, q_ref[0]), i, 0))

    rows = lambda: pl.BlockSpec((tr, w), lambda i, q_ref: (i, 0))
    shp = jax.ShapeDtypeStruct((r, w), F32)
    return pl.pallas_call(
        body, name=name,
        grid_spec=pltpu.PrefetchScalarGridSpec(
            num_scalar_prefetch=1, grid=(r // tr,),
            in_specs=[pl.BlockSpec((1, tr, w), lambda i, q_ref: (q_ref[0], i, 0))]
            + [recv_spec(q) for q in range(NCHIP)] + [rows(), rows(), rows()],
            out_specs=[rows(), rows(), rows(), rows()]),
        out_shape=[shp, shp, shp, shp],
        compiler_params=_cp(("arbitrary",), VMEM_MID),
    )(_scalar(_chip(_my_pos())), *_hbm(own, recv, recv, recv, recv, wv, mv, vv))


def _sum_blocks_small(own, recv, mine, transpose, name):
    na = len(recv)
    nblk = recv[0].shape[0]

    def body(q_ref, *refs):
        me = q_ref[0]
        for i in range(na):
            acc = None
            for q in range(nblk):
                term = jnp.where(me == q, refs[i][0], refs[na * (1 + q) + i][0]).astype(F32)
                acc = term if acc is None else acc + term
            refs[na * (1 + nblk) + i][...] = acc.T if transpose[i] else acc

    def oshape(a, tr):
        r, w = a.shape[1:]
        return (w, r) if tr else (r, w)

    own_spec = lambda a: pl.BlockSpec((1,) + a.shape[1:], lambda s, q_ref: (q_ref[0], 0, 0))
    recv_spec = lambda a, q: pl.BlockSpec((1,) + a.shape[1:], lambda s, q_ref: (_others(q, q_ref[0], nblk), 0, 0))
    out_spec = lambda shp: pl.BlockSpec(shp, lambda s, q_ref: (0, 0))
    in_specs = [own_spec(a) for a in own]
    for q in range(nblk):
        in_specs += [recv_spec(a, q) for a in recv]
    return pl.pallas_call(
        body, name=name,
        grid_spec=pltpu.PrefetchScalarGridSpec(
            num_scalar_prefetch=1, grid=(1,), in_specs=in_specs,
            out_specs=[out_spec(oshape(a, tr)) for a, tr in zip(recv, transpose)]),
        out_shape=[jax.ShapeDtypeStruct(oshape(a, tr), F32) for a, tr in zip(recv, transpose)],
        compiler_params=_cp(("arbitrary",), VMEM_MID),
    )(_scalar(mine), *own, *(list(recv) * nblk))


def _rep_offsets():
    offs = []
    o = 0
    for r in REP_ROWS:
        offs.append(o)
        o += r
    return offs


LOSS_ROW = REP_TOTAL_ROWS


def _pack_small_grads(g):
    offs = _rep_offsets()

    def body(dwa, dwx, dnm, dbin, dcb, dba, dbx, dlam, dq0, dq1, dq2, dk0, dk1, dk2, dnp, dbpg, loss, o_ref):
        o_ref[pl.ds(REP_TOTAL_ROWS - 2, NDEV * REP_ROWS_DEV - REP_TOTAL_ROWS + 2), :] = jnp.zeros(
            (NDEV * REP_ROWS_DEV - REP_TOTAL_ROWS + 2, LANES), F32)
        o_ref[pl.ds(LOSS_ROW, 1), :] = loss[0:1, :]
        for n in range(NRB):
            o_ref[pl.ds(offs[0] + n * RBW, RBW), :] = dwa[n]
            o_ref[pl.ds(offs[1] + n * RBW, RBW), :] = dwx[n]

        def put_vec(off, ref, rows):
            for k in range(rows):
                o_ref[pl.ds(off + k, 1), :] = ref[:, k * LANES:(k + 1) * LANES]

        put_vec(offs[2], dnm, REP_ROWS[2])
        put_vec(offs[3], dbin, REP_ROWS[3])
        put_vec(offs[4], dcb, REP_ROWS[4])
        put_vec(offs[5], dba, REP_ROWS[5])
        put_vec(offs[6], dbx, REP_ROWS[6])
        put_vec(offs[7], dlam, REP_ROWS[7])
        for k, ref in enumerate((dq0, dq1, dq2)):
            o_ref[pl.ds(offs[8] + k, 1), :] = ref[...]
        for k, ref in enumerate((dk0, dk1, dk2)):
            o_ref[pl.ds(offs[9] + k, 1), :] = ref[...]
        put_vec(offs[10], dnp, REP_ROWS[10])
        put_vec(offs[11], dbpg, REP_ROWS[11])

    args = [g["w_rg_a"], g["w_rg_x"], g["norm_mix"], g["b_in"], g["conv_b"], g["b_rg_a"], g["b_rg_x"],
            g["lru_lambda"], *g["q_norm"], *g["k_norm"], g["norm_ple"], g["b_ple_gate"], g["loss"]]
    full = lambda shp: pl.BlockSpec(shp, lambda: (0,) * len(shp))
    return pl.pallas_call(
        body, name="pack_small_grads",
        in_specs=[full(a.shape) for a in args],
        out_specs=full((NDEV * REP_ROWS_DEV, LANES)),
        out_shape=jax.ShapeDtypeStruct((NDEV * REP_ROWS_DEV, LANES), F32),
    )(*_hbm(*args))


def _adam_math(wv, gv, mv, vv):
    c1 = 1.0 - B1 ** STEP
    c2 = 1.0 - B2 ** STEP
    m2 = B1 * mv + (1.0 - B1) * gv
    v2 = B2 * vv + (1.0 - B2) * (gv * gv)
    delta = (-LR) * ((m2 / c1) / (jnp.sqrt(v2 / c2) + AEPS) + WD * wv)
    return delta, m2, v2


def _adamw_small(rep_flat, w, m, v):
    offs = _rep_offsets()
    n = len(REP_NAMES)

    def body(*refs):
        g_ref = refs[0]
        w_refs = refs[1:1 + n]
        m_refs = refs[1 + n:1 + 2 * n]
        v_refs = refs[1 + 2 * n:1 + 3 * n]
        outs = refs[1 + 3 * n:]
        go, do, mo, vo = outs[:n], outs[n:2 * n], outs[2 * n:3 * n], outs[3 * n:]

        def emit(i, idx, gv):
            go[i][idx] = gv
            delta, m2, v2 = _adam_math(w_refs[i][idx], gv, m_refs[i][idx], v_refs[i][idx])
            do[i][idx] = delta
            mo[i][idx] = m2
            vo[i][idx] = v2

        for i in range(n):
            if i < 2:
                for b in range(NRB):
                    emit(i, b, g_ref[pl.ds(offs[i] + b * RBW, RBW), :])
            elif REP_NAMES[i] in ("q_norm", "k_norm"):
                emit(i, slice(None), g_ref[pl.ds(offs[i], NG), :])
            else:
                gv = jnp.concatenate([g_ref[pl.ds(offs[i] + k, 1), :] for k in range(REP_ROWS[i])], axis=1)
                emit(i, slice(None), gv)

    full = lambda shp: pl.BlockSpec(shp, lambda: (0,) * len(shp))
    pspecs = [full(a.shape) for a in w]
    pshapes = [jax.ShapeDtypeStruct(a.shape, F32) for a in w]
    res = pl.pallas_call(
        body, name="adamw_small",
        in_specs=[full(rep_flat.shape)] + pspecs * 3,
        out_specs=pspecs * 4, out_shape=pshapes * 4,
        compiler_params=_cp(None, VMEM_MID),
    )(*_hbm(rep_flat, *w, *m, *v))
    return res[:n], res[n:2 * n], res[2 * n:3 * n], res[3 * n:]


def _adamw_many(w, g, m, v):
    n = len(w)

    def body(*refs):
        for i in range(n):
            delta, m2, v2 = _adam_math(refs[i][...], refs[n + i][...], refs[2 * n + i][...], refs[3 * n + i][...])
            refs[4 * n + i][...] = delta
            refs[5 * n + i][...] = m2
            refs[6 * n + i][...] = v2

    full = lambda shp: pl.BlockSpec(shp, lambda: (0,) * len(shp))
    specs = [full(a.shape) for a in w]
    shapes = [jax.ShapeDtypeStruct(a.shape, F32) for a in w]
    res = pl.pallas_call(
        body, name="adamw_shards",
        in_specs=specs * 4, out_specs=specs * 3, out_shape=shapes * 3,
        compiler_params=_cp(None, VMEM_MID),
    )(*_hbm(*w, *g, *m, *v))
    return res[:n], res[n:2 * n], res[2 * n:]


def kernel(x, p, norm_mix, w_in, b_in, conv_w, conv_b, w_rg_a, b_rg_a, w_rg_x, b_rg_x, lru_lambda, q_norm, k_norm, w_o_rnn, w_o_att, w_out, norm_ple, w_ple_gate, b_ple_gate, w_ple, loss_target, m_norm_mix, m_w_in, m_b_in, m_conv_w, m_conv_b, m_w_rg_a, m_b_rg_a, m_w_rg_x, m_b_rg_x, m_lru_lambda, m_q_norm, m_k_norm, m_w_o_rnn, m_w_o_att, m_w_out, m_norm_ple, m_w_ple_gate, m_b_ple_gate, m_w_ple, v_norm_mix, v_w_in, v_b_in, v_conv_w, v_conv_b, v_w_rg_a, v_b_rg_a, v_w_rg_x, v_b_rg_x, v_lru_lambda, v_q_norm, v_k_norm, v_w_o_rnn, v_w_o_att, v_w_out, v_norm_ple, v_w_ple_gate, v_b_ple_gate, v_w_ple):
    w = dict(norm_mix=norm_mix, w_in=w_in, b_in=b_in, conv_w=conv_w, conv_b=conv_b, w_rg_a=w_rg_a, b_rg_a=b_rg_a,
             w_rg_x=w_rg_x, b_rg_x=b_rg_x, lru_lambda=lru_lambda, q_norm=q_norm, k_norm=k_norm, w_o_rnn=w_o_rnn,
             w_o_att=w_o_att, w_out=w_out, norm_ple=norm_ple, w_ple_gate=w_ple_gate, b_ple_gate=b_ple_gate,
             w_ple=w_ple)
    m = dict(norm_mix=m_norm_mix, w_in=m_w_in, b_in=m_b_in, conv_w=m_conv_w, conv_b=m_conv_b, w_rg_a=m_w_rg_a,
             b_rg_a=m_b_rg_a, w_rg_x=m_w_rg_x, b_rg_x=m_b_rg_x, lru_lambda=m_lru_lambda, q_norm=m_q_norm,
             k_norm=m_k_norm, w_o_rnn=m_w_o_rnn, w_o_att=m_w_o_att, w_out=m_w_out, norm_ple=m_norm_ple,
             w_ple_gate=m_w_ple_gate, b_ple_gate=m_b_ple_gate, w_ple=m_w_ple)
    v = dict(norm_mix=v_norm_mix, w_in=v_w_in, b_in=v_b_in, conv_w=v_conv_w, conv_b=v_conv_b, w_rg_a=v_w_rg_a,
             b_rg_a=v_b_rg_a, w_rg_x=v_w_rg_x, b_rg_x=v_b_rg_x, lru_lambda=v_lru_lambda, q_norm=v_q_norm,
             k_norm=v_k_norm, w_o_rnn=v_w_o_rnn, w_o_att=v_w_o_att, w_out=v_w_out, norm_ple=v_norm_ple,
             w_ple_gate=v_w_ple_gate, b_ple_gate=v_b_ple_gate, w_ple=v_w_ple)
    names = list(w.keys())

    shards = [w_in[0].T.astype(BF16), w_o_rnn[0].astype(BF16), w_o_att[0].T.astype(BF16), w_out[0].astype(BF16),
              w_ple_gate[0].astype(BF16), w_ple[0].T.astype(BF16), conv_w[0]]
    pos = _my_pos()
    me, my_core, my_chip = _lin(pos), pos[2], _chip(pos)
    hbm_empty = lambda shp, dt: lax.empty(shp, dt)
    w_shard, conv_shard = shards[0], shards[6]
    shp = w_shard.shape
    entry_token, bufs, sems = _gather_start(
        [w_shard, hbm_empty((2,) + shp, BF16), hbm_empty((2, 2) + shp, BF16), conv_shard,
         hbm_empty((NDEV,) + conv_shard.shape, F32)],
        [("own", (0, 1)), ("near", (0, 2)), ("others", (3, 4))], norm_mix, "gather_start_near")
    w_src, own_l, near_l, conv_src, conv_l = bufs
    sem_own, sem_near, sem_conv = sems
    gather_out = {}

    def project(hn):
        w_thru, own = _gather_wait("own", *sem_own, [w_src, own_l], hn, "gather_wait_own")
        own = lax.dynamic_update_slice(own, w_shard[None], (my_core, 0, 0)).reshape(1, CHIP_COLS, D)
        chips = [jnp.stack([my_chip]), jnp.stack([my_chip ^ 1, my_chip ^ 2]), jnp.stack([my_chip ^ 3])]
        chips = [c.astype(jnp.int32) for c in chips]
        proj = _in_proj_chips(hn, own, b_in, chips[0], None, entry_token, "in_proj_own")
        w_thru, near = _gather_wait("near", *sem_near, [w_thru, near_l], proj, "gather_wait_near")
        near = _forward_to_sibling(near, "gather_forward_near")
        token, (w_thru, far_l), (sem_far,) = _gather_start(
            [w_thru, hbm_empty((2,) + shp, BF16)], [("far", (0, 1))], near, "gather_start_far")
        near = near.reshape(2, CHIP_COLS, D)
        proj = _in_proj_chips(hn, near, b_in, chips[1], proj, token, "in_proj_near")
        w_thru, far = _gather_wait("far", *sem_far, [w_thru, far_l], proj, "gather_wait_far")
        far = _forward_to_sibling(far[None], "gather_forward_far").reshape(1, CHIP_COLS, D)
        proj = _in_proj_chips(hn, far, b_in, chips[2], proj, token, "in_proj_far")
        conv_thru, conv_g = _gather_wait("others", *sem_conv, [conv_src, conv_l], proj, "gather_wait_conv")
        conv_g = lax.dynamic_update_slice(conv_g, conv_shard[None], (me, 0, 0))
        conv_f = conv_g.transpose(1, 0, 2).reshape(CONVW, DR)
        srcs = list(shards[1:6])
        token, obufs, (sem_out,) = _gather_start(
            srcs + [hbm_empty((NDEV,) + a.shape, BF16) for a in srcs], [("others", tuple(range(10)))], proj,
            "gather_start_out")
        gather_out.update(bufs=obufs, sems=sem_out)
        return proj, [own, near, far], jnp.concatenate(chips), conv_f, token

    def other_weights(after):
        obufs = _gather_wait("others", *gather_out["sems"], gather_out["bufs"], after, "gather_wait_out")
        full = [lax.dynamic_update_slice(a, s[None], (me, 0, 0)) for a, s in zip(obufs[5:], shards[1:6])]
        return [a.reshape((NDEV * a.shape[1], a.shape[2])) for a in full]

    def start_reduce(arrs, tag):
        parts = [a.reshape((NCHIP, 2, a.shape[0] // NDEV, a.shape[1])) for a in arrs]
        theirs = _exchange_within_chip(parts, "reduce_within_chip_" + tag)
        return _between_chips_start(_sum_pairs(parts, theirs, "sum_pairs_" + tag), "reduce_between_chips_start_" + tag)

    grad_x, pending_out, pending_in, small = _local_step(
        x.reshape(T, D), p.reshape(T, PLE), loss_target.reshape(T, D),
        project, other_weights,
        norm_mix, conv_b, w_rg_a[0], b_rg_a, w_rg_x[0], b_rg_x, lru_lambda, q_norm[0], k_norm[0],
        norm_ple, b_ple_gate, start_reduce, entry_token)

    rep_parts = _pack_small_grads(small).reshape(NDEV, REP_ROWS_DEV, LANES)
    conv_parts = small["conv_w"].reshape(CONVW, NDEV, DR // NDEV).transpose(1, 0, 2)
    rep_all, g_conv = _allreduce_small(rep_parts, conv_parts, "allreduce_small")
    rep_all = rep_all.reshape(NDEV * REP_ROWS_DEV, LANES)
    loss = rep_all[LOSS_ROW, 0]

    myq = _chip(_my_pos())
    own_out, recv_out = _between_chips_wait(pending_out, rep_all, "reduce_between_chips_wait_out")
    own_in, recv_in = _between_chips_wait(pending_in, rep_all, "reduce_between_chips_wait_in")
    w_in_res = _sum_chips_adamw(own_in[0], recv_in[0], w_in[0].T, m_w_in[0].T, v_w_in[0].T, 304, "adamw_w_in")
    g_o_rnn, g_o_att, g_out, g_pg, g_ple = _sum_blocks_small(
        own_out, recv_out, myq, (False, True, False, False, True), "sum_chips_out")

    grad, delta, new_m, new_v = {}, {}, {}, {}
    rep_shape = lambda a: a if a.ndim == 2 else a.reshape(a.shape[1:])
    res = _adamw_small(rep_all, [rep_shape(w[n]) for n in REP_NAMES], [rep_shape(m[n]) for n in REP_NAMES],
                       [rep_shape(v[n]) for n in REP_NAMES])
    for dst, vals in zip((grad, delta, new_m, new_v), res):
        for n, a in zip(REP_NAMES, vals):
            dst[n] = a.reshape(w[n].shape)
    grad["w_in"], delta["w_in"], new_m["w_in"], new_v["w_in"] = [a.T[None] for a in w_in_res]
    rest = ("w_o_rnn", "w_o_att", "w_out", "w_ple_gate", "w_ple", "conv_w")
    g_rest = [g_o_rnn, g_o_att, g_out, g_pg, g_ple, g_conv]
    res = _adamw_many([w[n][0] for n in rest], g_rest, [m[n][0] for n in rest], [v[n][0] for n in rest])
    for n, a in zip(rest, g_rest):
        grad[n] = a[None]
    for dst, vals in zip((delta, new_m, new_v), res):
        for n, a in zip(rest, vals):
            dst[n] = a[None]

    return (loss, grad_x.reshape(BL, S, D), *[grad[n] for n in names], *[delta[n] for n in names],
            *[new_m[n] for n in names], *[new_v[n] for n in names])
```

```python
import jax
import jax.numpy as jnp
from jax import lax
from jax.experimental import pallas as pl
from jax.experimental.pallas import tpu as pltpu

F32 = jnp.float32
BF16 = jnp.bfloat16

D = 1024
S = 2048
BL = 2
T = BL * S
NDEV = 8
NCHIP = 4
PLE = 256
DR = 1280
NRB = 10
RBW = 128
CONVW = 4
LRU_C = 8.0
HD = 128
NH = 4
PATTERNS = ((128, 1), (512, 4), (2048, 16))
NG = 3
ATT = NH * HD
GW = NG * ATT
NIN = 2 * DR + 3 * GW + ATT + 2 * D
OFF_ZR = DR
OFF_Q = 2 * DR
OFF_ZA = OFF_Q + 3 * GW
OFF_G = OFF_ZA + ATT
ROPE_THETA = 10000.0
EPS = 1e-6
SCALE = HD ** -0.5
NEG = -1e30
QB = 128
LANES = 128
CT = 512
NCT = NIN // CT
A_W = 2 * DR
C_W = ATT + 2 * D

LR, B1, B2, AEPS, WD, STEP = 0.001, 0.9, 0.999, 1e-08, 0.01, 10

NSHARD_IN = NIN // NDEV
REP_NAMES = ("w_rg_a", "w_rg_x", "norm_mix", "b_in", "conv_b", "b_rg_a", "b_rg_x", "lru_lambda", "q_norm",
             "k_norm", "norm_ple", "b_ple_gate")
REP_ROWS = (NRB * RBW, NRB * RBW, D // LANES, NIN // LANES, DR // LANES, DR // LANES, DR // LANES, DR // LANES,
            NG, NG, D // LANES, D // LANES)
REP_TOTAL_ROWS = sum(REP_ROWS)
REP_ROWS_DEV = 344
BIG_NAMES = ("w_in", "w_o_rnn", "w_o_att", "w_out", "w_ple_gate", "w_ple")

VMEM_BIG = 56 * 1024 * 1024
VMEM_MID = 40 * 1024 * 1024


def _cp(sem=None, vmem=None):
    return pltpu.CompilerParams(dimension_semantics=sem, vmem_limit_bytes=vmem)


def _hbm(*arrays):
    return [pltpu.with_memory_space_constraint(a, pltpu.HBM) for a in arrays]


def _dot(a, b):
    return jnp.dot(a, b, preferred_element_type=F32)


def _dot_nt(a, b):
    return lax.dot_general(a, b, (((1,), (1,)), ((), ())), preferred_element_type=F32)


def _dot_tn(a, b):
    return lax.dot_general(a, b, (((0,), (0,)), ((), ())), preferred_element_type=F32)


def _sigmoid(x):
    return jax.nn.sigmoid(x)


def _perm(j):
    jq = j - OFF_Q // CT
    inside = (j >= OFF_Q // CT) & (j < OFF_ZA // CT)
    return jnp.where(inside, OFF_Q // CT + (jq % 3) * 3 + jq // 3, j)


PIECES = ((0, A_W // CT), (OFF_Q // CT, GW // CT), (OFF_Q // CT + 3, GW // CT), (OFF_Q // CT + 6, GW // CT),
          (OFF_ZA // CT, C_W // CT))


def _rmsnorm_fwd(x, gain, token, tm=512):
    def body(x_ref, g_ref, _token, o_ref):
        xv = x_ref[...]
        var = jnp.mean(xv * xv, axis=-1, keepdims=True)
        o_ref[...] = (xv * lax.rsqrt(var + EPS) * g_ref[...]).astype(BF16)

    return pl.pallas_call(
        body, grid=(T // tm,), name="rmsnorm_fwd",
        in_specs=[pl.BlockSpec((tm, D), lambda i: (i, 0)), pl.BlockSpec((1, D), lambda i: (0, 0)),
                  pl.BlockSpec((8, LANES), lambda i: (0, 0))],
        out_specs=pl.BlockSpec((tm, D), lambda i: (i, 0)),
        out_shape=jax.ShapeDtypeStruct((T, D), BF16),
        compiler_params=_cp(("parallel",)),
    )(*_hbm(x, gain, token))


CHIP_COLS = NIN // NCHIP


def _in_proj_chips(hn, w_rows, bias, chips, proj, token, name, tm=1024):
    n = w_rows.shape[0]

    def body(chips_ref, a_ref, w_ref, b_ref, _token, *rest):
        o_ref = rest[-1]
        o_ref[...] = (_dot_nt(a_ref[...], w_ref[0]) + b_ref[...]).astype(BF16)

    in_specs = [pl.BlockSpec((tm, D), lambda s, i, ch: (i, 0)),
                pl.BlockSpec((1, CHIP_COLS, D), lambda s, i, ch: (s, 0, 0)),
                pl.BlockSpec((1, CHIP_COLS), lambda s, i, ch: (0, ch[s])),
                pl.BlockSpec((8, LANES), lambda s, i, ch: (0, 0))]
    args = [hn, w_rows, bias, token]
    aliases = {}
    if proj is not None:
        in_specs.append(pl.BlockSpec(memory_space=pl.ANY))
        args.append(proj)
        aliases = {5: 0}
    return pl.pallas_call(
        body, name=name,
        grid_spec=pltpu.PrefetchScalarGridSpec(
            num_scalar_prefetch=1, grid=(n, T // tm), in_specs=in_specs,
            out_specs=pl.BlockSpec((tm, CHIP_COLS), lambda s, i, ch: (i, ch[s]))),
        out_shape=jax.ShapeDtypeStruct((T, NIN), BF16),
        input_output_aliases=aliases,
        compiler_params=_cp(("arbitrary", "arbitrary"), VMEM_BIG),
    )(chips, *_hbm(*args))


def _grad_x(pieces, w_bufs, chips, token, x, dx1, gain, tm=512):
    nb = len(w_bufs)

    def body(chips_ref, a_ref, q_ref, k_ref, v_ref, c_ref, *rest):
        w_hbm = rest[:nb]
        x_ref, dx1_ref, g_ref, dx_ref, dg_ref, w = rest[nb + 1:]
        first = pl.program_id(0) == 0

        @pl.when(first)
        def _():
            s = 0
            for buf in w_hbm:
                for r in range(buf.shape[0]):
                    row = pl.multiple_of(chips_ref[s] * CHIP_COLS, 128)
                    pltpu.sync_copy(buf.at[r], w.at[pl.ds(row, CHIP_COLS), :])
                    s += 1

        acc = _dot(a_ref[...], w[pl.ds(0, A_W), :])
        for kind, p_ref in enumerate((q_ref, k_ref, v_ref)):
            for g in range(NG):
                row = OFF_Q + (3 * g + kind) * CT
                acc = acc + _dot(p_ref[:, g * CT:(g + 1) * CT], w[pl.ds(row, CT), :])
        dn = acc + _dot(c_ref[...], w[pl.ds(OFF_ZA, C_W), :])
        xv = x_ref[...]
        rstd = lax.rsqrt(jnp.mean(xv * xv, axis=-1, keepdims=True) + EPS)
        xh = xv * rstd
        dg = jnp.sum(dn * xh, axis=0, keepdims=True)
        gd = dn * g_ref[...]
        dx_ref[...] = dx1_ref[...] + rstd * (gd - xh * jnp.mean(gd * xh, axis=-1, keepdims=True))

        @pl.when(first)
        def _():
            dg_ref[...] = dg

        @pl.when(jnp.logical_not(first))
        def _():
            dg_ref[...] += dg

    tok = lambda wd: pl.BlockSpec((tm, wd), lambda i, ch: (i, 0))
    vec = lambda: pl.BlockSpec((1, D), lambda i, ch: (0, 0))
    return pl.pallas_call(
        body, name="grad_x",
        grid_spec=pltpu.PrefetchScalarGridSpec(
            num_scalar_prefetch=1, grid=(T // tm,),
            in_specs=[tok(A_W), tok(GW), tok(GW), tok(GW), tok(C_W)] + [pl.BlockSpec(memory_space=pl.ANY)] * nb
            + [pl.BlockSpec((8, LANES), lambda i, ch: (0, 0)), tok(D), tok(D), vec()],
            out_specs=[tok(D), vec()],
            scratch_shapes=[pltpu.VMEM((NIN, D), BF16)]),
        out_shape=[jax.ShapeDtypeStruct((T, D), F32), jax.ShapeDtypeStruct((1, D), F32)],
        compiler_params=_cp(("arbitrary",), VMEM_BIG),
    )(chips, *_hbm(*pieces, *w_bufs, token, x, dx1, gain))


def _dw_in(pieces, hn):
    def body(a_ref, q_ref, k_ref, v_ref, c_ref, h_hbm, o_ref, s_ref, h):
        j = pl.program_id(0)

        @pl.when(j == 0)
        def _():
            pltpu.sync_copy(h_hbm, h)

        def step(x_ref):
            xv = x_ref[...]
            o_ref[...] = _dot_tn(xv, h[...]).astype(BF16)
            s_ref[...] = jnp.sum(xv.astype(F32), axis=0, keepdims=True)

        for x_ref, (lo, n) in zip((a_ref, q_ref, k_ref, v_ref, c_ref), PIECES):
            pl.when((j >= lo) & (j < lo + n))(lambda x_ref=x_ref: step(x_ref))

    def piece_spec(lo, n):
        return pl.BlockSpec((T, CT), lambda j: (0, jnp.clip(j - lo, 0, n - 1)))

    return pl.pallas_call(
        body, grid=(NCT,), name="dw_in",
        in_specs=[piece_spec(lo, n) for lo, n in PIECES] + [pl.BlockSpec(memory_space=pl.ANY)],
        out_specs=[pl.BlockSpec((CT, D), lambda j: (_perm(j), 0)), pl.BlockSpec((1, CT), lambda j: (0, _perm(j)))],
        out_shape=[jax.ShapeDtypeStruct((NIN, D), BF16), jax.ShapeDtypeStruct((1, NIN), F32)],
        scratch_shapes=[pltpu.VMEM((T, D), BF16)],
        compiler_params=_cp(("arbitrary",), VMEM_BIG),
    )(*_hbm(*pieces, hn))


def _mm_tn(a, b, ta, tt, name):
    m = a.shape[1]
    n = b.shape[1]
    nt = T // tt

    def body(a_ref, b_ref, o_ref, acc):
        t = pl.program_id(1)
        p = _dot_tn(a_ref[...].astype(BF16), b_ref[...].astype(BF16))

        @pl.when(t == 0)
        def _():
            acc[...] = p

        @pl.when(t > 0)
        def _():
            acc[...] += p

        @pl.when(t == nt - 1)
        def _():
            o_ref[...] = acc[...].astype(BF16)

    return pl.pallas_call(
        body, grid=(m // ta, nt), name=name,
        in_specs=[pl.BlockSpec((tt, ta), lambda j, t: (t, j)), pl.BlockSpec((tt, n), lambda j, t: (t, 0))],
        out_specs=pl.BlockSpec((ta, n), lambda j, t: (j, 0)),
        out_shape=jax.ShapeDtypeStruct((m, n), BF16),
        scratch_shapes=[pltpu.VMEM((ta, n), F32)],
        compiler_params=_cp(("parallel", "arbitrary"), VMEM_MID),
    )(*_hbm(a, b))


def _row_iota():
    return lax.broadcasted_iota(jnp.int32, (S, RBW), 0)


SUBLANES = 8
N_SHIFT_BUFS = 4


class _Shifter:
    def __init__(self, bufs):
        self.bufs = bufs
        self.k = 0

    def _store(self, v, fill, front):
        b = self.bufs.at[self.k % N_SHIFT_BUFS]
        self.k += 1
        b[pl.ds(0 if front else SUBLANES + S, SUBLANES), :] = jnp.full((SUBLANES, RBW), fill, F32)
        b[pl.ds(SUBLANES, S), :] = v
        return b

    def down(self, v, ds, fill):
        b = self._store(v, fill, True)
        return [b[pl.ds(SUBLANES - d, S), :] for d in ds]

    def up(self, v, ds, fill):
        b = self._store(v, fill, False)
        return [b[pl.ds(SUBLANES + d, S), :] for d in ds]


def _shift_down(v, d, sh, fill):
    return sh.down(v, (d,), fill)[0]


def _shift_up(v, d, sh, fill):
    return sh.up(v, (d,), fill)[0]


def _scan_down(a, u, row):
    d = 1
    while d < S:
        last = 2 * d >= S
        if d < SUBLANES:
            u = a * _shift_down(u, d, row, 0.0) + u
            if not last:
                a = a * _shift_down(a, d, row, 1.0)
        else:
            u = jnp.concatenate([u[:d], a[d:] * u[:S - d] + u[d:]], axis=0)
            if not last:
                a = jnp.concatenate([a[:d], a[d:] * a[:S - d]], axis=0)
        d *= 2
    return u


def _scan_up(b, g, row):
    d = 1
    while d < S:
        last = 2 * d >= S
        if d < SUBLANES:
            g = g + b * _shift_up(g, d, row, 0.0)
            if not last:
                b = b * _shift_up(b, d, row, 0.0)
        else:
            g = jnp.concatenate([g[:S - d] + b[:S - d] * g[d:], g[S - d:]], axis=0)
            if not last:
                b = jnp.concatenate([b[:S - d] * b[d:], b[S - d:]], axis=0)
        d *= 2
    return g


def _softplus(x):
    return jnp.maximum(x, 0.0) + jnp.log1p(jnp.exp(-jnp.abs(x)))


def _rnn_gates(x, cw, cb, wa, ba, wx, bx, lam, row, pad):
    xs = pad.down(x, (1, 2, 3), 0.0)
    xc = cb + cw[3:4, :] * x
    for j in (1, 2, 3):
        xc = xc + cw[3 - j:4 - j, :] * xs[j - 1]
    xcb = xc.astype(BF16)
    r = _sigmoid(_dot(xcb, wa) + ba)
    i = _sigmoid(_dot(xcb, wx) + bx)
    sp = _softplus(-lam)
    log_a = (-LRU_C) * r * sp
    a = jnp.exp(log_a)
    mult = jnp.where(row == 0, 1.0, jnp.sqrt(jnp.tanh(-log_a) * (1.0 + a * a)))
    return xc, xcb, r, i, sp, a, mult, xs


RNN_PER_STEP = 2
RNN_W = RNN_PER_STEP * RBW


def _lanes(k):
    return pl.ds(k * RBW, RBW)


def _rnn_fwd(proj3, conv_w, conv_b, wa, ba, wx, bx, lam, token):
    def body(x_ref, cw_ref, cb_ref, wa_ref, ba_ref, wx_ref, bx_ref, lam_ref, _token, h_ref, pad):
        row = _row_iota()
        sh = _Shifter(pad)
        for k in range(RNN_PER_STEP):
            ln = _lanes(k)
            x = x_ref[0, :, ln].astype(F32)
            xc, _, _, i, _, a, mult, _ = _rnn_gates(x, cw_ref[:, ln], cb_ref[:, ln], wa_ref[k], ba_ref[:, ln],
                                                 wx_ref[k], bx_ref[:, ln], lam_ref[:, ln], row, sh)
            h_ref[0, :, ln] = _scan_down(a, mult * (i * xc), sh)

    vec = lambda: pl.BlockSpec((1, RNN_W), lambda b, n: (0, n))
    mat = lambda: pl.BlockSpec((RNN_PER_STEP, RBW, RBW), lambda b, n: (n, 0, 0))
    return pl.pallas_call(
        body, grid=(BL, NRB // RNN_PER_STEP), name="rnn_fwd",
        in_specs=[pl.BlockSpec((1, S, RNN_W), lambda b, n: (b, 0, n)),
                  pl.BlockSpec((CONVW, RNN_W), lambda b, n: (0, n)),
                  vec(), mat(), vec(), mat(), vec(), vec(), pl.BlockSpec((8, LANES), lambda b, n: (0, 0))],
        out_specs=pl.BlockSpec((1, S, RNN_W), lambda b, n: (b, 0, n)),
        out_shape=jax.ShapeDtypeStruct((BL, S, DR), F32),
        scratch_shapes=[pltpu.VMEM((N_SHIFT_BUFS, S + 2 * SUBLANES, RBW), F32)],
        compiler_params=_cp(("parallel", "parallel"), VMEM_MID),
    )(*_hbm(proj3, conv_w, conv_b, wa, ba, wx, bx, lam, token))


def _rnn_bwd(proj3, h3, dh3, slab_a3, conv_w, conv_b, wa, ba, wx, bx, lam, token):
    def body(x_ref, h_ref, dh_ref, cw_ref, cb_ref, wa_ref, ba_ref, wx_ref, bx_ref, lam_ref, _alias, _token,
             dx_ref, dcw_ref, dcb_ref, dwa_ref, dba_ref, dwx_ref, dbx_ref, dlam_ref, pad):
        row = _row_iota()
        sh = _Shifter(pad)
        first = pl.program_id(1) == 0

        def accumulate(ref, idx, val):
            @pl.when(first)
            def _():
                ref[idx] = val

            @pl.when(jnp.logical_not(first))
            def _():
                ref[idx] += val

        for k in range(RNN_PER_STEP):
            ln = _lanes(k)
            x = x_ref[0, :, ln].astype(F32)
            cw = cw_ref[:, ln]
            wa_v = wa_ref[k]
            wx_v = wx_ref[k]
            lam_v = lam_ref[:, ln]
            xc, xcb, r, i, sp, a, mult, xs = _rnn_gates(x, cw, cb_ref[:, ln], wa_v, ba_ref[:, ln], wx_v,
                                                        bx_ref[:, ln], lam_v, row, sh)
            h = h_ref[0, :, ln]
            g = _scan_up(_shift_up(a, 1, sh, 0.0), dh_ref[0, :, ln], sh)
            da = g * _shift_down(h, 1, sh, 0.0)
            dmult = jnp.where(row == 0, 0.0, g * (i * xc))
            gm = g * mult
            di = gm * xc
            dxc = gm * i
            dlog_a = da * a - dmult * (a * a) / mult
            dr = dlog_a * ((-LRU_C) * sp)
            dsp = jnp.sum(dlog_a * ((-LRU_C) * r), axis=0, keepdims=True)
            dpa = dr * r * (1.0 - r)
            dpx = di * i * (1.0 - i)
            dpab = dpa.astype(BF16)
            dpxb = dpx.astype(BF16)
            dxc = dxc + _dot_nt(dpab, wa_v) + _dot_nt(dpxb, wx_v)
            dx = cw[3:4, :] * dxc
            dcw_rows = [None] * CONVW
            dcw_rows[3] = jnp.sum(dxc * x, axis=0, keepdims=True)
            dxc_up = sh.up(dxc, (1, 2, 3), 0.0)
            for j in (1, 2, 3):
                dx = dx + cw[3 - j:4 - j, :] * dxc_up[j - 1]
                dcw_rows[3 - j] = jnp.sum(dxc * xs[j - 1], axis=0, keepdims=True)
            dx_ref[0, :, ln] = dx.astype(BF16)
            lanes = (slice(None), ln)
            accumulate(dcw_ref, lanes, jnp.concatenate(dcw_rows, axis=0))
            accumulate(dcb_ref, lanes, jnp.sum(dxc, axis=0, keepdims=True))
            accumulate(dwa_ref, k, _dot_tn(xcb, dpab))
            accumulate(dba_ref, lanes, jnp.sum(dpa, axis=0, keepdims=True))
            accumulate(dwx_ref, k, _dot_tn(xcb, dpxb))
            accumulate(dbx_ref, lanes, jnp.sum(dpx, axis=0, keepdims=True))
            accumulate(dlam_ref, lanes, dsp * (-_sigmoid(-lam_v)))

    slab = lambda: pl.BlockSpec((1, S, RNN_W), lambda n, b: (b, 0, n))
    vec = lambda: pl.BlockSpec((1, RNN_W), lambda n, b: (0, n))
    mat = lambda: pl.BlockSpec((RNN_PER_STEP, RBW, RBW), lambda n, b: (n, 0, 0))
    taps = lambda: pl.BlockSpec((CONVW, RNN_W), lambda n, b: (0, n))
    vshape = jax.ShapeDtypeStruct((1, DR), F32)
    mshape = jax.ShapeDtypeStruct((NRB, RBW, RBW), F32)
    return pl.pallas_call(
        body, grid=(NRB // RNN_PER_STEP, BL), name="rnn_bwd",
        in_specs=[slab(), slab(), slab(), taps(), vec(), mat(), vec(), mat(), vec(), vec(),
                  pl.BlockSpec(memory_space=pl.ANY), pl.BlockSpec((8, LANES), lambda n, b: (0, 0))],
        out_specs=[slab(), taps(), vec(), mat(), vec(), mat(), vec(), vec()],
        out_shape=[jax.ShapeDtypeStruct((BL, S, A_W), BF16), jax.ShapeDtypeStruct((CONVW, DR), F32),
                   vshape, mshape, vshape, mshape, vshape, vshape],
        input_output_aliases={10: 0},
        scratch_shapes=[pltpu.VMEM((N_SHIFT_BUFS, S + 2 * SUBLANES, RBW), F32)],
        compiler_params=_cp(("parallel", "arbitrary"), 48 * 1024 * 1024),
    )(*_hbm(proj3, h3, dh3, conv_w, conv_b, wa, ba, wx, bx, lam, slab_a3, token))


NQB = S // QB


def _rms_head(t, gain):
    rstd = lax.rsqrt(jnp.mean(t * t, axis=-1, keepdims=True) + EPS)
    return t * rstd * gain


def _rope(t, cs, sn):
    return t * cs + pltpu.roll(t, HD // 2, 1) * sn


def _rope_t(dy, cs, sn):
    return dy * cs - pltpu.roll(dy, HD // 2, 1) * sn


def _bdot_nt(a, b):
    return lax.dot_general(a, b, (((2,), (2,)), ((0,), (0,))), preferred_element_type=F32)


def _bdot(a, b):
    return lax.dot_general(a, b, (((2,), (1,)), ((0,), (0,))), preferred_element_type=F32)


def _bdot_tn(a, b):
    return lax.dot_general(a, b, (((1,), (1,)), ((0,), (0,))), preferred_element_type=F32)


STRIDE_MAX = 4


def _permute(buf, x, dil, dst, off=0):
    ln = S // dil
    if dil == 1:
        dst[pl.ds(off, S), :] = x.astype(dst.dtype)
        return
    buf[0] = x
    if dil <= STRIDE_MAX:
        for c in range(dil):
            dst[pl.ds(off + c * ln, ln), :] = buf.at[0][pl.ds(c, ln, stride=dil), :].astype(dst.dtype)
        return
    f, r = STRIDE_MAX, dil // STRIDE_MAX
    part = S // f
    for c1 in range(f):
        buf.at[1][pl.ds(c1 * part, part), :] = buf.at[0][pl.ds(c1, part, stride=f), :]
    for c1 in range(f):
        for c2 in range(r):
            dst[pl.ds(off + (c1 + f * c2) * ln, ln), :] = (
                buf.at[1][pl.ds(c1 * part + c2, ln, stride=r), :].astype(dst.dtype))


def _unpermute(buf, xp, dil, dst):
    ln = S // dil
    if dil == 1:
        dst[...] = xp
        return
    if dil <= STRIDE_MAX:
        for c in range(dil):
            dst[pl.ds(c, ln, stride=dil), :] = xp[c * ln:(c + 1) * ln]
        return
    f, r = STRIDE_MAX, dil // STRIDE_MAX
    part = S // f
    for c1 in range(f):
        for c2 in range(r):
            c = c1 + f * c2
            buf.at[1][pl.ds(c1 * part + c2, ln, stride=r), :] = xp[c * ln:(c + 1) * ln]
    for c1 in range(f):
        dst[pl.ds(c1, part, stride=f), :] = buf[1, pl.ds(c1 * part, part), :]


def _blocks3(ref, off=0):
    return ref[pl.ds(off, S), :].reshape(NQB, QB, HD)


def _att_prep(q_ref, k_ref, v_ref, cos_ref, sin_ref, qn, kn, dil, nat, qs, ksp, vsp):
    cs = cos_ref[...]
    sn = sin_ref[...]
    zero = jnp.zeros((QB, HD), BF16)
    ksp[pl.ds(0, QB), :] = zero
    vsp[pl.ds(0, QB), :] = zero
    _permute(nat, _rope(_rms_head(q_ref[0].astype(F32), qn), cs, sn), dil, qs)
    _permute(nat, _rope(_rms_head(k_ref[0].astype(F32), kn), cs, sn), dil, ksp, QB)
    _permute(nat, v_ref[0].astype(F32), dil, vsp, QB)


def _att_scores(qs, ksp, dil):
    nb = S // dil // QB
    q3 = _blocks3(qs)
    shape = (NQB, QB, QB)
    qi = lax.broadcasted_iota(jnp.int32, shape, 1)
    kj = lax.broadcasted_iota(jnp.int32, shape, 2)
    s_c = jnp.where(qi >= kj, _bdot_nt(q3, _blocks3(ksp, QB)) * SCALE, NEG)
    if nb == 1:
        return q3, s_c, None
    jj = lax.broadcasted_iota(jnp.int32, shape, 0)
    ok = (kj >= qi) & ((jj & (nb - 1)) != 0)
    s_p = jnp.where(ok, _bdot_nt(q3, _blocks3(ksp)) * SCALE, NEG)
    return q3, s_c, s_p


def _qkv_spec(kind, g):
    base = OFF_Q // HD + (3 * g + kind) * NH
    return pl.BlockSpec((1, S, HD), lambda b, h: (b, 0, base + h))


def _attn_fwd(proj3, cos_t, sin_t, q_norm, k_norm):
    def body(*refs):
        qkv_refs = refs[:9]
        cos_ref, sin_ref, qn_ref, kn_ref, att_ref, lse_ref, w_ref, nat, qs, ksp, vsp, og = refs[9:]
        for g, (window, dil) in enumerate(PATTERNS):
            q_ref, k_ref, v_ref = qkv_refs[3 * g:3 * g + 3]
            _att_prep(q_ref, k_ref, v_ref, cos_ref, sin_ref, qn_ref[g:g + 1, :], kn_ref[g:g + 1, :], dil,
                      nat, qs, ksp, vsp)
            _, s_c, s_p = _att_scores(qs, ksp, dil)
            m = jnp.max(s_c, axis=-1, keepdims=True)
            if s_p is not None:
                m = jnp.maximum(m, jnp.max(s_p, axis=-1, keepdims=True))
            e_c = jnp.exp(s_c - m)
            den = jnp.sum(e_c, axis=-1, keepdims=True)
            o = _bdot(e_c.astype(BF16), _blocks3(vsp, QB))
            if s_p is not None:
                e_p = jnp.exp(s_p - m)
                den = den + jnp.sum(e_p, axis=-1, keepdims=True)
                o = o + _bdot(e_p.astype(BF16), _blocks3(vsp))
            _unpermute(nat, (o / den).reshape(S, HD), dil, og.at[g])
            _unpermute(nat, jnp.broadcast_to(m + jnp.log(den), (NQB, QB, HD)).reshape(S, HD), dil,
                       lse_ref.at[g, 0])
        l0 = lse_ref[0, 0]
        l1 = lse_ref[1, 0]
        l2 = lse_ref[2, 0]
        mx = jnp.maximum(jnp.maximum(l0, l1), l2)
        e0 = jnp.exp(l0 - mx)
        e1 = jnp.exp(l1 - mx)
        e2 = jnp.exp(l2 - mx)
        inv = 1.0 / (e0 + e1 + e2)
        w0 = e0 * inv
        w1 = e1 * inv
        w2 = e2 * inv
        w_ref[0, 0] = w0
        w_ref[1, 0] = w1
        w_ref[2, 0] = w2
        att_ref[0] = w0 * og[0] + w1 * og[1] + w2 * og[2]

    in_specs = [_qkv_spec(kind, g) for g in range(NG) for kind in range(3)]
    in_specs += [pl.BlockSpec((S, HD), lambda b, h: (0, 0)), pl.BlockSpec((S, HD), lambda b, h: (0, 0)),
                 pl.BlockSpec((NG, HD), lambda b, h: (0, 0)), pl.BlockSpec((NG, HD), lambda b, h: (0, 0))]
    stat = lambda: pl.BlockSpec((NG, 1, S, HD), lambda b, h: (0, b, 0, h))
    return pl.pallas_call(
        body, grid=(BL, NH), name="attn_fwd",
        in_specs=in_specs,
        out_specs=[pl.BlockSpec((1, S, HD), lambda b, h: (b, 0, h)), stat(), stat()],
        out_shape=[jax.ShapeDtypeStruct((BL, S, ATT), F32),
                   jax.ShapeDtypeStruct((NG, BL, S, ATT), F32),
                   jax.ShapeDtypeStruct((NG, BL, S, ATT), F32)],
        scratch_shapes=[pltpu.VMEM((2, S, HD), F32), pltpu.VMEM((S, HD), BF16), pltpu.VMEM((S + QB, HD), BF16),
                        pltpu.VMEM((S + QB, HD), BF16), pltpu.VMEM((NG, S, HD), F32)],
        compiler_params=_cp(("parallel", "parallel"), VMEM_BIG),
    )(*_hbm(*([proj3] * 9), cos_t, sin_t, q_norm, k_norm))


def _attn_bwd_group(g, proj3, cos_t, sin_t, qn_g, kn_g, lse, wts, datt3, sbar3, slabs):
    dil = PATTERNS[g][1]
    n_alias = 0 if slabs is None else 3

    def norm_rope_bwd(dpost, raw, gain, cs, sn):
        dn = _rope_t(dpost, cs, sn)
        rstd = lax.rsqrt(jnp.mean(raw * raw, axis=-1, keepdims=True) + EPS)
        xh = raw * rstd
        dgain = jnp.sum(dn * xh, axis=0, keepdims=True)
        gd = dn * gain
        draw = rstd * (gd - xh * jnp.mean(gd * xh, axis=-1, keepdims=True))
        return draw, dgain

    def body(*refs):
        (q_ref, k_ref, v_ref, cos_ref, sin_ref, qn_ref, kn_ref, lse_ref, w_ref, datt_ref, sbar_ref) = refs[:11]
        (dq_ref, dk_ref, dv_ref, dqn_ref, dkn_ref, nat, qs, ksp, vsp, dos, cvp, lsp, acc) = refs[11 + n_alias:]
        qn = qn_ref[...]
        kn = kn_ref[...]
        cs = cos_ref[...]
        sn = sin_ref[...]
        _att_prep(q_ref, k_ref, v_ref, cos_ref, sin_ref, qn, kn, dil, nat, qs, ksp, vsp)
        wv = w_ref[0, 0]
        _permute(nat, wv * datt_ref[0], dil, dos)
        _permute(nat, wv * sbar_ref[0], dil, cvp)
        _permute(nat, lse_ref[0, 0], dil, lsp)
        q3, s_c, s_p = _att_scores(qs, ksp, dil)
        do3 = _blocks3(dos)
        lse3 = _blocks3(lsp)[:, :, 0:1]
        cv3 = _blocks3(cvp)[:, :, 0:1]
        p_c = jnp.exp(s_c - lse3)
        ds_c = (p_c * (_bdot_nt(do3, _blocks3(vsp, QB)) - cv3)).astype(BF16)
        dq = _bdot(ds_c, _blocks3(ksp, QB))
        acc[0] = _bdot_tn(ds_c, q3).reshape(S, HD)
        acc[1] = _bdot_tn(p_c.astype(BF16), do3).reshape(S, HD)
        if s_p is not None:
            p_p = jnp.exp(s_p - lse3)
            ds_p = (p_p * (_bdot_nt(do3, _blocks3(vsp)) - cv3)).astype(BF16)
            dq = dq + _bdot(ds_p, _blocks3(ksp))
            early = pl.ds(0, S - QB)
            acc[0, early, :] += _bdot_tn(ds_p, q3).reshape(S, HD)[QB:]
            acc[1, early, :] += _bdot_tn(p_p.astype(BF16), do3).reshape(S, HD)[QB:]
        _unpermute(nat, (dq * SCALE).reshape(S, HD), dil, nat.at[0])
        draw, dqn = norm_rope_bwd(nat[0], q_ref[0].astype(F32), qn, cs, sn)
        dq_ref[0] = draw.astype(BF16)
        _unpermute(nat, acc[0] * SCALE, dil, nat.at[0])
        draw, dkn = norm_rope_bwd(nat[0], k_ref[0].astype(F32), kn, cs, sn)
        dk_ref[0] = draw.astype(BF16)
        _unpermute(nat, acc[1], dil, nat.at[0])
        dv_ref[0] = nat[0].astype(BF16)
        first = (pl.program_id(0) == 0) & (pl.program_id(1) == 0)

        @pl.when(first)
        def _():
            dqn_ref[...] = dqn
            dkn_ref[...] = dkn

        @pl.when(jnp.logical_not(first))
        def _():
            dqn_ref[...] += dqn
            dkn_ref[...] += dkn

    full = lambda r: pl.BlockSpec((r, HD), lambda b, h: (0, 0))
    stat = lambda: pl.BlockSpec((1, 1, S, HD), lambda b, h: (g, b, 0, h))
    slab = lambda: pl.BlockSpec((1, S, HD), lambda b, h: (b, 0, h))
    out_slab = lambda: pl.BlockSpec((1, S, HD), lambda b, h: (b, 0, g * NH + h))
    big = jax.ShapeDtypeStruct((BL, S, GW), BF16)
    vecs = jax.ShapeDtypeStruct((1, HD), F32)
    in_specs = [_qkv_spec(0, g), _qkv_spec(1, g), _qkv_spec(2, g), full(S), full(S), full(1), full(1),
                stat(), stat(), slab(), slab()]
    args = [proj3, proj3, proj3, cos_t, sin_t, qn_g, kn_g, lse, wts, datt3, sbar3]
    aliases = {}
    if slabs is not None:
        in_specs += [pl.BlockSpec(memory_space=pl.ANY)] * 3
        args += list(slabs)
        aliases = {11: 0, 12: 1, 13: 2}
    return pl.pallas_call(
        body, grid=(BL, NH), name="attn_bwd_g%d" % g,
        in_specs=in_specs,
        out_specs=[out_slab(), out_slab(), out_slab(), full(1), full(1)],
        out_shape=[big, big, big, vecs, vecs],
        scratch_shapes=[pltpu.VMEM((2, S, HD), F32), pltpu.VMEM((S, HD), BF16), pltpu.VMEM((S + QB, HD), BF16),
                        pltpu.VMEM((S + QB, HD), BF16), pltpu.VMEM((S, HD), BF16), pltpu.VMEM((S, HD), F32),
                        pltpu.VMEM((S, HD), F32), pltpu.VMEM((2, S, HD), F32)],
        input_output_aliases=aliases,
        compiler_params=_cp(("arbitrary", "arbitrary"), VMEM_BIG),
    )(*_hbm(*args))


def _tail(x, proj, h, att, p, tgt, w_o_rnn, w_o_att_t, w_out, w_pg, w_ple_t, norm_ple, b_pg, tm=256):
    nt = T // tm
    inv_d = 1.0 / D

    def body(x_ref, h_ref, zr_ref, att_ref, za_ref, g0a_ref, g0b_ref, g1a_ref, g1b_ref, p_ref, tgt_ref,
             np_ref, bpg_ref, wor_hbm, woa_hbm, wout_hbm, wpg_hbm, wple_hbm,
             dx1_ref, merged_ref, n1_ref, dpre_ref, dpe_ref, dyr_ref, dya_ref, slab_a_ref, slab_c_ref, dh_ref,
             datt_ref, sbar_ref, yrnn_ref, yatt_ref, loss_ref, dnp_ref, dbpg_ref,
             wor, woa, wout, wpg, wple):
        first = pl.program_id(0) == 0

        @pl.when(first)
        def _():
            pltpu.sync_copy(wor_hbm, wor)
            pltpu.sync_copy(woa_hbm, woa)
            pltpu.sync_copy(wout_hbm, wout)
            pltpu.sync_copy(wpg_hbm, wpg)
            pltpu.sync_copy(wple_hbm, wple)

        xv = x_ref[...]
        hv = h_ref[...]
        zr = zr_ref[...].astype(F32)
        av = att_ref[...]
        za = za_ref[...].astype(F32)
        szr = _sigmoid(zr)
        silu_r = zr * szr
        yrnn_b = (hv * silu_r).astype(BF16)
        sza = _sigmoid(za)
        silu_a = za * sza
        yatt_b = (av * silu_a).astype(BF16)
        yrnn_ref[...] = yrnn_b
        yatt_ref[...] = yatt_b
        yr = _dot(yrnn_b, wor[...])
        ya = _dot_nt(yatt_b, woa[...])
        g0 = _sigmoid(jnp.concatenate([g0a_ref[...], g0b_ref[...]], axis=1).astype(F32))
        g1 = _sigmoid(jnp.concatenate([g1a_ref[...], g1b_ref[...]], axis=1).astype(F32))
        merged_b = (g0 * yr + g1 * ya).astype(BF16)
        merged_ref[...] = merged_b
        x1 = xv + _dot(merged_b, wout[...])
        rstd = lax.rsqrt(jnp.mean(x1 * x1, axis=-1, keepdims=True) + EPS)
        xh = x1 * rstd
        npl = np_ref[...]
        n1_b = (xh * npl).astype(BF16)
        n1_ref[...] = n1_b
        pg = _sigmoid(_dot(n1_b, wpg[...]) + bpg_ref[...])
        pe = _dot_nt(p_ref[...].astype(BF16), wple[...])
        err = x1 + pg * pe - tgt_ref[...]
        loss_t = 0.5 * inv_d * jnp.sum(err * err)
        dy = err * inv_d
        dpe_ref[...] = (dy * pg).astype(BF16)
        dpre = dy * pe * pg * (1.0 - pg)
        dpre_b = dpre.astype(BF16)
        dpre_ref[...] = dpre_b
        dn1 = _dot_nt(dpre_b, wpg[...])
        dnp = jnp.sum(dn1 * xh, axis=0, keepdims=True)
        dbpg = jnp.sum(dpre, axis=0, keepdims=True)
        gd = dn1 * npl
        dx1 = dy + rstd * (gd - xh * jnp.mean(gd * xh, axis=-1, keepdims=True))
        dx1_ref[...] = dx1
        dmerged = _dot_nt(dx1.astype(BF16), wout[...])
        dyr_b = (dmerged * g0).astype(BF16)
        dya_b = (dmerged * g1).astype(BF16)
        dyr_ref[...] = dyr_b
        dya_ref[...] = dya_b
        slab_c_ref[:, ATT:ATT + D] = (dmerged * yr * g0 * (1.0 - g0)).astype(BF16)
        slab_c_ref[:, ATT + D:ATT + 2 * D] = (dmerged * ya * g1 * (1.0 - g1)).astype(BF16)
        dyrnn = _dot_nt(dyr_b, wor[...])
        dyatt = _dot(dya_b, woa[...])
        dh_ref[...] = dyrnn * silu_r
        slab_a_ref[...] = (dyrnn * hv * szr * (1.0 + zr * (1.0 - szr))).astype(BF16)
        datt = dyatt * silu_a
        datt_ref[...] = datt
        slab_c_ref[:, 0:ATT] = (dyatt * av * sza * (1.0 + za * (1.0 - sza))).astype(BF16)
        da = datt * av
        for hh in range(NH):
            seg = slice(hh * HD, (hh + 1) * HD)
            sbar_ref[:, seg] = jnp.broadcast_to(jnp.sum(da[:, seg], axis=-1, keepdims=True), (tm, HD))

        @pl.when(first)
        def _():
            loss_ref[...] = jnp.full((8, LANES), loss_t, F32)
            dnp_ref[...] = dnp
            dbpg_ref[...] = dbpg

        @pl.when(jnp.logical_not(first))
        def _():
            loss_ref[...] += jnp.full((8, LANES), loss_t, F32)
            dnp_ref[...] += dnp
            dbpg_ref[...] += dbpg

    tok = lambda w: pl.BlockSpec((tm, w), lambda i: (i, 0))
    col = lambda w, blk: pl.BlockSpec((tm, w), lambda i: (i, blk))
    vec = lambda: pl.BlockSpec((1, D), lambda i: (0, 0))
    hbm = lambda: pl.BlockSpec(memory_space=pl.ANY)
    gb = OFF_G // 512
    in_specs = [tok(D), tok(DR), col(DR, 1), tok(ATT), col(ATT, OFF_ZA // ATT),
                col(512, gb), col(512, gb + 1), col(512, gb + 2), col(512, gb + 3),
                tok(PLE), tok(D), vec(), vec(), hbm(), hbm(), hbm(), hbm(), hbm()]
    sh = lambda w, dt: jax.ShapeDtypeStruct((T, w), dt)
    out_shape = [sh(D, F32), sh(D, BF16), sh(D, BF16), sh(D, BF16), sh(D, BF16), sh(D, BF16), sh(D, BF16),
                 sh(A_W, BF16), sh(C_W, BF16), sh(DR, F32), sh(ATT, F32), sh(ATT, F32),
                 sh(DR, BF16), sh(ATT, BF16),
                 jax.ShapeDtypeStruct((8, LANES), F32), jax.ShapeDtypeStruct((1, D), F32),
                 jax.ShapeDtypeStruct((1, D), F32)]
    out_specs = [tok(D), tok(D), tok(D), tok(D), tok(D), tok(D), tok(D), col(DR, 1), tok(C_W), tok(DR),
                 tok(ATT), tok(ATT), tok(DR), tok(ATT),
                 pl.BlockSpec((8, LANES), lambda i: (0, 0)), vec(), vec()]
    return pl.pallas_call(
        body, grid=(nt,), name="tail_fwd_bwd",
        in_specs=in_specs, out_specs=out_specs, out_shape=out_shape,
        scratch_shapes=[pltpu.VMEM((DR, D), BF16), pltpu.VMEM((D, ATT), BF16), pltpu.VMEM((D, D), BF16),
                        pltpu.VMEM((D, D), BF16), pltpu.VMEM((D, PLE), BF16)],
        compiler_params=_cp(("arbitrary",), VMEM_BIG),
    )(*_hbm(x, h, proj, att, proj, proj, proj, proj, proj, p, tgt, norm_ple, b_pg, w_o_rnn, w_o_att_t, w_out, w_pg,
            w_ple_t))


def _input_norm_bwd(x, dhn, dx1, gain, tm=512):
    def body(x_ref, dhn_ref, dx1_ref, g_ref, dx_ref, dg_ref):
        xv = x_ref[...]
        rstd = lax.rsqrt(jnp.mean(xv * xv, axis=-1, keepdims=True) + EPS)
        xh = xv * rstd
        dn = dhn_ref[...]
        dg = jnp.sum(dn * xh, axis=0, keepdims=True)
        gd = dn * g_ref[...]
        dx_ref[...] = dx1_ref[...] + rstd * (gd - xh * jnp.mean(gd * xh, axis=-1, keepdims=True))
        first = pl.program_id(0) == 0

        @pl.when(first)
        def _():
            dg_ref[...] = dg

        @pl.when(jnp.logical_not(first))
        def _():
            dg_ref[...] += dg

    tok = lambda: pl.BlockSpec((tm, D), lambda i: (i, 0))
    vec = lambda: pl.BlockSpec((1, D), lambda i: (0, 0))
    return pl.pallas_call(
        body, grid=(T // tm,), name="input_norm_bwd",
        in_specs=[tok(), tok(), tok(), vec()], out_specs=[tok(), vec()],
        out_shape=[jax.ShapeDtypeStruct((T, D), F32), jax.ShapeDtypeStruct((1, D), F32)],
        compiler_params=_cp(("arbitrary",), VMEM_MID),
    )(*_hbm(x, dhn, dx1, gain))


def _rope_tables():
    pos = jnp.arange(S, dtype=F32)
    inv_freq = ROPE_THETA ** (-jnp.arange(0, HD, 2, dtype=F32) / HD)
    ang = pos[:, None] * inv_freq[None, :]
    cos, sin = jnp.cos(ang), jnp.sin(ang)
    return jnp.concatenate([cos, cos], axis=1), jnp.concatenate([-sin, sin], axis=1)


def _local_step(x, p, tgt, project, other_weights, norm_mix, conv_b,
                w_rg_a, b_rg_a, w_rg_x, b_rg_x, lam, q_norm, k_norm, norm_ple, b_pg, start_reduce=None,
                entry_token=None):
    if start_reduce is None:
        start_reduce = lambda arrs, tag: (jnp.zeros((8, LANES), F32), arrs)
    if entry_token is None:
        entry_token = jnp.zeros((8, LANES), F32)
    cos_t, sin_t = _rope_tables()
    wa_b = w_rg_a.astype(BF16)
    wx_b = w_rg_x.astype(BF16)

    hn = _rmsnorm_fwd(x, norm_mix, entry_token)
    proj, w_bufs, chips, conv_w, token = project(hn)
    proj3 = proj.reshape(BL, S, NIN)
    h3 = _rnn_fwd(proj3, conv_w, conv_b, wa_b, b_rg_a, wx_b, b_rg_x, lam, token)
    att3, lse, wts = _attn_fwd(proj3, cos_t, sin_t, q_norm, k_norm)
    w_o_rnn, w_o_att_t, w_out, w_pg, w_ple_t = other_weights(att3)
    (dx1, merged, n1, dpre, dpe, dyr, dya, slab_a, slab_c, dh, datt, sbar, yrnn, yatt, loss8, dnp, dbpg) = _tail(
        x, proj, h3.reshape(T, DR), att3.reshape(T, ATT), p, tgt, w_o_rnn, w_o_att_t, w_out, w_pg, w_ple_t,
        norm_ple, b_pg)

    token, pending_out = start_reduce([
        _mm_tn(yrnn, dyr, 640, 2048, "dw_o_rnn"),
        _mm_tn(dya, yatt, 512, 2048, "dw_o_att_t"),
        _mm_tn(merged, dx1, 512, 2048, "dw_out"),
        _mm_tn(n1, dpre, 512, 2048, "dw_ple_gate"),
        _mm_tn(dpe, p, 512, 2048, "dw_ple_t")], "out")

    slab_a3, dcw, dcb, dwa, dba, dwx, dbx, dlam = _rnn_bwd(
        proj3, h3, dh.reshape(BL, S, DR), slab_a.reshape(BL, S, A_W), conv_w, conv_b, wa_b, b_rg_a, wx_b, b_rg_x, lam,
        token)
    datt3 = datt.reshape(BL, S, ATT)
    sbar3 = sbar.reshape(BL, S, ATT)
    slabs = None
    dqn = []
    dkn = []
    for g in range(NG):
        dq, dk, dv, dqn_g, dkn_g = _attn_bwd_group(g, proj3, cos_t, sin_t, q_norm[g:g + 1], k_norm[g:g + 1],
                                                   lse, wts, datt3, sbar3, slabs)
        slabs = (dq, dk, dv)
        dqn.append(dqn_g)
        dkn.append(dkn_g)
    pieces = [slab_a3.reshape(T, A_W)] + [t.reshape(T, GW) for t in slabs] + [slab_c]
    dw_in_t, db_in = _dw_in(pieces, hn)
    token, pending_in = start_reduce([dw_in_t], "in")
    grad_x, dnm = _grad_x(pieces, w_bufs, chips, token, x, dx1, norm_mix)

    small = dict(w_rg_a=dwa, w_rg_x=dwx, norm_mix=dnm, b_in=db_in, conv_b=dcb, b_rg_a=dba, b_rg_x=dbx,
                 lru_lambda=dlam, q_norm=dqn, k_norm=dkn, norm_ple=dnp, b_ple_gate=dbpg, conv_w=dcw, loss=loss8)
    return grad_x, pending_out, pending_in, small


MESH = pl.DeviceIdType.MESH
HBM_SPEC = pl.BlockSpec(memory_space=pl.ANY)


def _my_pos():
    return lax.axis_index("x"), lax.axis_index("y"), lax.axis_index("c")


def _flip(pos, k):
    x, y, c = pos
    return (1 - x if k & 4 else x, 1 - y if k & 2 else y, 1 - c if k & 1 else c)


def _lin(pos):
    return 4 * pos[0] + 2 * pos[1] + pos[2]


def _chip(pos):
    return 2 * pos[0] + pos[1]


def _all_gather_two_level(shards, name):
    na = len(shards)

    def body(*refs):
        x_refs = refs[:na]
        out_refs = refs[na:2 * na]
        send_sems, recv_sems, local_sems = refs[2 * na:]
        me = _my_pos()
        sibling = _flip(me, 1)
        chips = [_flip(me, 4), _flip(me, 2), _flip(me, 6)]

        def copy(i, k, block, to, from_x=False):
            dst = out_refs[i].at[_lin(block)]
            return pltpu.make_async_remote_copy(
                src_ref=x_refs[i] if from_x else dst, dst_ref=dst,
                send_sem=send_sems.at[7 * i + k], recv_sem=recv_sems.at[7 * i + k], device_id=to, device_id_type=MESH)

        started = []
        for i in range(na):
            mine = pltpu.make_async_copy(x_refs[i], out_refs[i].at[_lin(me)], local_sems.at[i])
            mine.start()
            started.append(mine)
        sends = []
        for i in range(na):
            cps = [copy(i, 0, me, sibling, True)] + [copy(i, 1 + j, me, chip, True) for j, chip in enumerate(chips)]
            for cp in cps:
                cp.start()
            sends += cps
        for i in range(na):
            for j, chip in enumerate(chips):
                copy(i, 1 + j, chip, me).wait_recv()
                fwd = copy(i, 4 + j, chip, sibling)
                fwd.start()
                sends.append(fwd)
        for i in range(na):
            copy(i, 0, sibling, me).wait_recv()
            for j, chip in enumerate(chips):
                copy(i, 4 + j, _flip(chip, 1), me).wait_recv()
        for cp in sends:
            cp.wait_send()
        for mine in started:
            mine.wait()

    return pl.pallas_call(
        body, name=name,
        out_shape=[jax.ShapeDtypeStruct((NDEV,) + s.shape, s.dtype) for s in shards],
        in_specs=[HBM_SPEC] * na, out_specs=[HBM_SPEC] * na,
        scratch_shapes=[pltpu.SemaphoreType.DMA((7 * na,)), pltpu.SemaphoreType.DMA((7 * na,)),
                        pltpu.SemaphoreType.DMA((na,))],
    )(*shards)


def _all_gather_direct(shard, name):
    def body(x_ref, out_ref, send_sems, recv_sems, local_sem):
        me = _my_pos()
        mine = pltpu.make_async_copy(x_ref, out_ref.at[_lin(me)], local_sem)
        mine.start()
        sends = []
        for k in range(1, NDEV):
            cp = pltpu.make_async_remote_copy(
                src_ref=x_ref, dst_ref=out_ref.at[_lin(me)], send_sem=send_sems.at[k - 1],
                recv_sem=recv_sems.at[k - 1], device_id=_flip(me, k), device_id_type=MESH)
            cp.start()
            sends.append(cp)
        for k in range(1, NDEV):
            peer = _flip(me, k)
            pltpu.make_async_remote_copy(
                src_ref=x_ref, dst_ref=out_ref.at[_lin(peer)], send_sem=send_sems.at[k - 1],
                recv_sem=recv_sems.at[k - 1], device_id=peer, device_id_type=MESH).wait_recv()
        for cp in sends:
            cp.wait_send()
        mine.wait()

    return pl.pallas_call(
        body, name=name,
        out_shape=jax.ShapeDtypeStruct((NDEV,) + shard.shape, shard.dtype),
        in_specs=[HBM_SPEC], out_specs=HBM_SPEC,
        scratch_shapes=[pltpu.SemaphoreType.DMA((7,)), pltpu.SemaphoreType.DMA((7,)), pltpu.SemaphoreType.DMA],
    )(shard)


def _exchange_within_chip(parts, name):
    na = len(parts)

    def body(*refs):
        a_refs = refs[:na]
        recv_refs = refs[na:2 * na]
        send_sems, recv_sems = refs[2 * na:]
        me = _my_pos()
        c = me[2]
        sibling = _flip(me, 1)
        remote = []
        for i in range(na):
            for q in range(NCHIP):
                rc = pltpu.make_async_remote_copy(
                    src_ref=a_refs[i].at[q, 1 - c], dst_ref=recv_refs[i].at[q],
                    send_sem=send_sems.at[NCHIP * i + q], recv_sem=recv_sems.at[NCHIP * i + q],
                    device_id=sibling, device_id_type=MESH)
                rc.start()
                remote.append(rc)
        for rc in remote:
            rc.wait_recv()
        for rc in remote:
            rc.wait_send()

    return pl.pallas_call(
        body, name=name, out_shape=[jax.ShapeDtypeStruct((NCHIP,) + a.shape[2:], a.dtype) for a in parts],
        in_specs=[HBM_SPEC] * na, out_specs=[HBM_SPEC] * na,
        scratch_shapes=[pltpu.SemaphoreType.DMA((NCHIP * na,)), pltpu.SemaphoreType.DMA((NCHIP * na,))],
    )(*parts)


HBM_ONLY = pl.BlockSpec(memory_space=pltpu.HBM)
SEM_SPEC = pl.BlockSpec(memory_space=pltpu.SEMAPHORE)
SPLIT_COPY = pltpu.CompilerParams(has_side_effects=pltpu.SideEffectType.DATAFLOW_SIDE_EFFECTING)


def _chip_peers(me):
    return [_flip(me, 4), _flip(me, 2), _flip(me, 6)]


def _between_chips_start(parts, name):
    na = len(parts)

    def body(*refs):
        a_refs = refs[:na]
        land_refs = refs[na:2 * na]
        send_sems, recv_sems = refs[2 * na], refs[2 * na + 1]
        token = refs[-1]
        me = _my_pos()
        myq = _chip(me)
        for i in range(na):
            for j, peer in enumerate(_chip_peers(me)):
                pltpu.make_async_remote_copy(
                    src_ref=a_refs[i].at[_chip(peer)], dst_ref=land_refs[i].at[myq],
                    send_sem=send_sems.at[3 * i + j], recv_sem=recv_sems.at[3 * i + j],
                    device_id=peer, device_id_type=MESH).start()
        token[...] = jnp.zeros_like(token)

    hbm = [pltpu.HBM(a.shape, a.dtype) for a in parts]
    srcs = [pltpu.with_memory_space_constraint(a, pltpu.HBM) for a in parts]
    lands = [pltpu.with_memory_space_constraint(lax.empty(a.shape, a.dtype), pltpu.HBM) for a in parts]
    res = pl.pallas_call(
        body, name=name,
        out_shape=(pltpu.SemaphoreType.DMA((3 * na,)), pltpu.SemaphoreType.DMA((3 * na,)), *hbm, *hbm,
                   jax.ShapeDtypeStruct((8, LANES), F32)),
        in_specs=[HBM_ONLY] * (2 * na),
        out_specs=(SEM_SPEC, SEM_SPEC, *([HBM_ONLY] * (2 * na)), pl.BlockSpec(memory_space=pltpu.VMEM)),
        input_output_aliases={i: 2 + i for i in range(2 * na)},
        compiler_params=SPLIT_COPY,
    )(*srcs, *lands)
    return res[-1], (res[0], res[1], list(res[2:2 + na]), list(res[2 + na:2 + 2 * na]))


def _between_chips_wait(pending, after, name):
    send_sems, recv_sems, parts, lands = pending
    na = len(parts)

    def body(*refs):
        a_refs = refs[:na]
        land_refs = refs[na:2 * na]
        send_sems, recv_sems = refs[2 * na], refs[2 * na + 1]
        me = _my_pos()
        for i in range(na):
            for j, peer in enumerate(_chip_peers(me)):
                cp = pltpu.make_async_remote_copy(
                    src_ref=a_refs[i].at[_chip(peer)], dst_ref=land_refs[i].at[_chip(peer)],
                    send_sem=send_sems.at[3 * i + j], recv_sem=recv_sems.at[3 * i + j],
                    device_id=peer, device_id_type=MESH)
                cp.wait_send()
                cp.wait_recv()

    hbm = [pltpu.HBM(a.shape, a.dtype) for a in parts]
    res = pl.pallas_call(
        body, name=name, out_shape=(*hbm, *hbm),
        in_specs=[HBM_ONLY] * (2 * na) + [SEM_SPEC, SEM_SPEC, pl.BlockSpec(memory_space=pl.ANY)],
        out_specs=[HBM_ONLY] * (2 * na),
        input_output_aliases={i: i for i in range(2 * na)},
        compiler_params=SPLIT_COPY,
    )(*parts, *lands, send_sems, recv_sems, after)
    return list(res[:na]), list(res[na:])


def _remote(src, dst, send_sems, recv_sems, idx, peer):
    return pltpu.make_async_remote_copy(src_ref=src, dst_ref=dst, send_sem=send_sems.at[idx],
                                        recv_sem=recv_sems.at[idx], device_id=peer, device_id_type=MESH)


def _copies_own(bufs, me):
    return [(bufs[0], bufs[1].at[me[2]], 0, _flip(me, 1))]


def _copies_near(bufs, me):
    return [(bufs[0], bufs[1].at[0, me[2]], 0, _flip(me, 2)), (bufs[0], bufs[1].at[1, me[2]], 1, _flip(me, 4))]


def _copies_far(bufs, me):
    return [(bufs[0], bufs[1].at[me[2]], 0, _flip(me, 6))]


def _copies_others(bufs, me):
    na = len(bufs) // 2
    return [(bufs[i], bufs[na + i].at[_lin(me)], 7 * i + k - 1, _flip(me, k))
            for i in range(na) for k in range(1, NDEV)]


GROUP_COPIES = dict(own=_copies_own, near=_copies_near, far=_copies_far, others=_copies_others)
GROUP_COUNT = dict(own=1, near=2, far=1)


def _gather_start(bufs, groups, after, name):
    nb = len(bufs)
    ng = len(groups)

    def body(*refs):
        b = refs[:nb]
        sems = refs[nb + 1:nb + 1 + 2 * ng]
        token = refs[-1]
        me = _my_pos()
        for gi, (group, idx) in enumerate(groups):
            for src, dst, k, peer in GROUP_COPIES[group]([b[i] for i in idx], me):
                _remote(src, dst, sems[2 * gi], sems[2 * gi + 1], k, peer).start()
        token[...] = jnp.zeros_like(token)

    sem_t = []
    for group, idx in groups:
        cnt = 7 * (len(idx) // 2) if group == "others" else GROUP_COUNT[group]
        sem_t += [pltpu.SemaphoreType.DMA((cnt,)), pltpu.SemaphoreType.DMA((cnt,))]
    ins = [pltpu.with_memory_space_constraint(a, pltpu.HBM) for a in bufs]
    res = pl.pallas_call(
        body, name=name,
        out_shape=(*sem_t, *[pltpu.HBM(a.shape, a.dtype) for a in bufs], jax.ShapeDtypeStruct((8, LANES), F32)),
        in_specs=[HBM_ONLY] * nb + [pl.BlockSpec(memory_space=pl.ANY)],
        out_specs=(*([SEM_SPEC] * (2 * ng)), *([HBM_ONLY] * nb), pl.BlockSpec(memory_space=pltpu.VMEM)),
        input_output_aliases={i: 2 * ng + i for i in range(nb)},
        compiler_params=SPLIT_COPY,
    )(*ins, after)
    return res[-1], list(res[2 * ng:2 * ng + nb]), [(res[2 * gi], res[2 * gi + 1]) for gi in range(ng)]


def _gather_wait(group, send_sems, recv_sems, bufs, after, name):
    nb = len(bufs)
    copies = dict(own=_copies_own, near=_copies_near, far=_copies_far, others=_copies_others)[group]

    def body(*refs):
        b = refs[:nb]
        ss, rs = refs[nb], refs[nb + 1]
        me = _my_pos()
        for src, dst, idx, peer in copies(b, me):
            if group == "others":
                landed = b[nb // 2 + idx // 7].at[_lin(peer)]
            elif group == "own":
                landed = b[1].at[1 - me[2]]
            else:
                landed = dst
            cp = _remote(src, landed, ss, rs, idx, peer)
            cp.wait_send()
            cp.wait_recv()

    res = pl.pallas_call(
        body, name=name, out_shape=[pltpu.HBM(a.shape, a.dtype) for a in bufs],
        in_specs=[HBM_ONLY] * nb + [SEM_SPEC, SEM_SPEC, pl.BlockSpec(memory_space=pl.ANY)],
        out_specs=[HBM_ONLY] * nb,
        input_output_aliases={i: i for i in range(nb)},
        compiler_params=SPLIT_COPY,
    )(*bufs, send_sems, recv_sems, after)
    return list(res)


def _forward_to_sibling(buf, name):
    n = buf.shape[0]

    def body(_in_ref, out_ref, send_sems, recv_sems):
        me = _my_pos()
        c = me[2]
        sibling = _flip(me, 1)
        sends = []
        for r in range(n):
            cp = _remote(out_ref.at[r, c], out_ref.at[r, c], send_sems, recv_sems, r, sibling)
            cp.start()
            sends.append(cp)
        for r in range(n):
            _remote(out_ref.at[r, c], out_ref.at[r, 1 - c], send_sems, recv_sems, r, sibling).wait_recv()
        for cp in sends:
            cp.wait_send()

    return pl.pallas_call(
        body, name=name, out_shape=jax.ShapeDtypeStruct(buf.shape, buf.dtype),
        in_specs=[HBM_SPEC], out_specs=HBM_SPEC,
        scratch_shapes=[pltpu.SemaphoreType.DMA((n,)), pltpu.SemaphoreType.DMA((n,))],
        input_output_aliases={0: 0},
    )(buf)


def _allreduce_small(rep, conv, name):
    def body(rep_hbm, conv_hbm, out_ref, conv_out, stage_r, stage_c, send_sems, recv_sems, local_sems):
        me = _my_pos()
        mi = _lin(me)
        peers = [_flip(me, k) for k in range(1, NDEV)]
        own = [pltpu.make_async_copy(rep_hbm.at[mi], stage_r.at[mi], local_sems.at[0]),
               pltpu.make_async_copy(conv_hbm.at[mi], stage_c.at[mi], local_sems.at[1])]
        for cp in own:
            cp.start()
        sends = []
        for j, peer in enumerate(peers):
            for src, stage, base in ((rep_hbm, stage_r, 0), (conv_hbm, stage_c, 7)):
                cp = _remote(src.at[_lin(peer)], stage.at[mi], send_sems, recv_sems, base + j, peer)
                cp.start()
                sends.append(cp)
        for j, peer in enumerate(peers):
            for src, stage, base in ((rep_hbm, stage_r, 0), (conv_hbm, stage_c, 7)):
                _remote(src.at[mi], stage.at[_lin(peer)], send_sems, recv_sems, base + j, peer).wait_recv()
        for cp in own:
            cp.wait()
        acc_r = stage_r[0]
        acc_c = stage_c[0]
        for q in range(1, NDEV):
            acc_r = acc_r + stage_r[q]
            acc_c = acc_c + stage_c[q]
        out_ref[mi] = acc_r
        conv_out[...] = acc_c
        for j, peer in enumerate(peers):
            cp = _remote(out_ref.at[mi], out_ref.at[mi], send_sems, recv_sems, 14 + j, peer)
            cp.start()
            sends.append(cp)
        for j, peer in enumerate(peers):
            _remote(out_ref.at[mi], out_ref.at[_lin(peer)], send_sems, recv_sems, 14 + j, peer).wait_recv()
        for cp in sends:
            cp.wait_send()

    vmem = pl.BlockSpec(memory_space=pltpu.VMEM)
    return pl.pallas_call(
        body, name=name,
        out_shape=[jax.ShapeDtypeStruct(rep.shape, F32), jax.ShapeDtypeStruct(conv.shape[1:], F32)],
        in_specs=[HBM_SPEC, HBM_SPEC], out_specs=[vmem, vmem],
        scratch_shapes=[pltpu.VMEM(rep.shape, F32), pltpu.VMEM(conv.shape, F32),
                        pltpu.SemaphoreType.DMA((21,)), pltpu.SemaphoreType.DMA((21,)), pltpu.SemaphoreType.DMA((2,))],
    )(rep, conv)


def _exchange_all(parts, name):
    na = len(parts)

    def body(*refs):
        a_refs = refs[:na]
        out_refs = refs[na:2 * na]
        send_sems, recv_sems = refs[2 * na:]
        me = _my_pos()
        sends = []
        for i in range(na):
            for k in range(1, NDEV):
                peer = _flip(me, k)
                cp = pltpu.make_async_remote_copy(
                    src_ref=a_refs[i].at[_lin(peer)], dst_ref=out_refs[i].at[_lin(me)],
                    send_sem=send_sems.at[7 * i + k - 1], recv_sem=recv_sems.at[7 * i + k - 1],
                    device_id=peer, device_id_type=MESH)
                cp.start()
                sends.append(cp)
        for i in range(na):
            for k in range(1, NDEV):
                peer = _flip(me, k)
                pltpu.make_async_remote_copy(
                    src_ref=a_refs[i].at[_lin(peer)], dst_ref=out_refs[i].at[_lin(peer)],
                    send_sem=send_sems.at[7 * i + k - 1], recv_sem=recv_sems.at[7 * i + k - 1],
                    device_id=peer, device_id_type=MESH).wait_recv()
        for cp in sends:
            cp.wait_send()

    return pl.pallas_call(
        body, name=name, out_shape=[jax.ShapeDtypeStruct(a.shape, a.dtype) for a in parts],
        in_specs=[HBM_SPEC] * na, out_specs=[HBM_SPEC] * na,
        scratch_shapes=[pltpu.SemaphoreType.DMA((7 * na,)), pltpu.SemaphoreType.DMA((7 * na,))],
    )(*parts)


def _scalar(v):
    return jnp.asarray(v, jnp.int32).reshape(1)


def _sum_pairs(parts, theirs, name):
    na = len(parts)

    def body(c_ref, *refs):
        for i in range(na):
            o_ref = refs[2 * na + i]
            o_ref[0] = (refs[i][0, 0].astype(F32) + refs[na + i][0].astype(F32)).astype(o_ref.dtype)

    def mine_spec(a):
        return pl.BlockSpec((1, 1) + a.shape[2:], lambda q, c_ref: (q, c_ref[0], 0, 0))

    def spec(a):
        return pl.BlockSpec((1,) + a.shape[1:], lambda q, c_ref: (q, 0, 0))

    return pl.pallas_call(
        body, name=name,
        grid_spec=pltpu.PrefetchScalarGridSpec(
            num_scalar_prefetch=1, grid=(NCHIP,),
            in_specs=[mine_spec(a) for a in parts] + [spec(a) for a in theirs],
            out_specs=[spec(a) for a in theirs]),
        out_shape=[jax.ShapeDtypeStruct(a.shape, a.dtype) for a in theirs],
        compiler_params=_cp(("arbitrary",), VMEM_BIG),
    )(_scalar(lax.axis_index("c")), *parts, *theirs)


def _others(q, mine, nblk=NCHIP):
    return jnp.where(q == mine, (q + 1) % nblk, q)


def _sum_chips_adamw(own, recv, wv, mv, vv, tr, name):
    _, r, w = recv.shape

    def body(q_ref, own_ref, r0, r1, r2, r3, w_ref, m_ref, v_ref, g_ref, d_ref, m2_ref, v2_ref):
        myq = q_ref[0]
        acc = None
        for q, r_ref in enumerate((r0, r1, r2, r3)):
            term = jnp.where(myq == q, own_ref[0], r_ref[0]).astype(F32)
            acc = term if acc is None else acc + term
        g_ref[...] = acc
        delta, m2, v2 = _adam_math(w_ref[...], acc, m_ref[...], v_ref[...])
        d_ref[...] = delta
        m2_ref[...] = m2
        v2_ref[...] = v2

    def recv_spec(q):
        return pl.BlockSpec((1, tr, w), lambda i, q_ref: (_others(q, q_ref[0]), i, 0))

    rows = lambda: pl.BlockSpec((tr, w), lambda i, q_ref: (i, 0))
    shp = jax.ShapeDtypeStruct((r, w), F32)
    return pl.pallas_call(
        body, name=name,
        grid_spec=pltpu.PrefetchScalarGridSpec(
            num_scalar_prefetch=1, grid=(r // tr,),
            in_specs=[pl.BlockSpec((1, tr, w), lambda i, q_ref: (q_ref[0], i, 0))]
            + [recv_spec(q) for q in range(NCHIP)] + [rows(), rows(), rows()],
            out_specs=[rows(), rows(), rows(), rows()]),
        out_shape=[shp, shp, shp, shp],
        compiler_params=_cp(("arbitrary",), VMEM_MID),
    )(_scalar(_chip(_my_pos())), *_hbm(own, recv, recv, recv, recv, wv, mv, vv))


def _sum_blocks_small(own, recv, mine, transpose, name):
    na = len(recv)
    nblk = recv[0].shape[0]

    def body(q_ref, *refs):
        me = q_ref[0]
        for i in range(na):
            acc = None
            for q in range(nblk):
                term = jnp.where(me == q, refs[i][0], refs[na * (1 + q) + i][0]).astype(F32)
                acc = term if acc is None else acc + term
            refs[na * (1 + nblk) + i][...] = acc.T if transpose[i] else acc

    def oshape(a, tr):
        r, w = a.shape[1:]
        return (w, r) if tr else (r, w)

    own_spec = lambda a: pl.BlockSpec((1,) + a.shape[1:], lambda s, q_ref: (q_ref[0], 0, 0))
    recv_spec = lambda a, q: pl.BlockSpec((1,) + a.shape[1:], lambda s, q_ref: (_others(q, q_ref[0], nblk), 0, 0))
    out_spec = lambda shp: pl.BlockSpec(shp, lambda s, q_ref: (0, 0))
    in_specs = [own_spec(a) for a in own]
    for q in range(nblk):
        in_specs += [recv_spec(a, q) for a in recv]
    return pl.pallas_call(
        body, name=name,
        grid_spec=pltpu.PrefetchScalarGridSpec(
            num_scalar_prefetch=1, grid=(1,), in_specs=in_specs,
            out_specs=[out_spec(oshape(a, tr)) for a, tr in zip(recv, transpose)]),
        out_shape=[jax.ShapeDtypeStruct(oshape(a, tr), F32) for a, tr in zip(recv, transpose)],
        compiler_params=_cp(("arbitrary",), VMEM_MID),
    )(_scalar(mine), *own, *(list(recv) * nblk))


def _rep_offsets():
    offs = []
    o = 0
    for r in REP_ROWS:
        offs.append(o)
        o += r
    return offs


LOSS_ROW = REP_TOTAL_ROWS


def _pack_small_grads(g):
    offs = _rep_offsets()

    def body(dwa, dwx, dnm, dbin, dcb, dba, dbx, dlam, dq0, dq1, dq2, dk0, dk1, dk2, dnp, dbpg, loss, o_ref):
        o_ref[pl.ds(REP_TOTAL_ROWS - 2, NDEV * REP_ROWS_DEV - REP_TOTAL_ROWS + 2), :] = jnp.zeros(
            (NDEV * REP_ROWS_DEV - REP_TOTAL_ROWS + 2, LANES), F32)
        o_ref[pl.ds(LOSS_ROW, 1), :] = loss[0:1, :]
        for n in range(NRB):
            o_ref[pl.ds(offs[0] + n * RBW, RBW), :] = dwa[n]
            o_ref[pl.ds(offs[1] + n * RBW, RBW), :] = dwx[n]

        def put_vec(off, ref, rows):
            for k in range(rows):
                o_ref[pl.ds(off + k, 1), :] = ref[:, k * LANES:(k + 1) * LANES]

        put_vec(offs[2], dnm, REP_ROWS[2])
        put_vec(offs[3], dbin, REP_ROWS[3])
        put_vec(offs[4], dcb, REP_ROWS[4])
        put_vec(offs[5], dba, REP_ROWS[5])
        put_vec(offs[6], dbx, REP_ROWS[6])
        put_vec(offs[7], dlam, REP_ROWS[7])
        for k, ref in enumerate((dq0, dq1, dq2)):
            o_ref[pl.ds(offs[8] + k, 1), :] = ref[...]
        for k, ref in enumerate((dk0, dk1, dk2)):
            o_ref[pl.ds(offs[9] + k, 1), :] = ref[...]
        put_vec(offs[10], dnp, REP_ROWS[10])
        put_vec(offs[11], dbpg, REP_ROWS[11])

    args = [g["w_rg_a"], g["w_rg_x"], g["norm_mix"], g["b_in"], g["conv_b"], g["b_rg_a"], g["b_rg_x"],
            g["lru_lambda"], *g["q_norm"], *g["k_norm"], g["norm_ple"], g["b_ple_gate"], g["loss"]]
    full = lambda shp: pl.BlockSpec(shp, lambda: (0,) * len(shp))
    return pl.pallas_call(
        body, name="pack_small_grads",
        in_specs=[full(a.shape) for a in args],
        out_specs=full((NDEV * REP_ROWS_DEV, LANES)),
        out_shape=jax.ShapeDtypeStruct((NDEV * REP_ROWS_DEV, LANES), F32),
    )(*_hbm(*args))


def _adam_math(wv, gv, mv, vv):
    c1 = 1.0 - B1 ** STEP
    c2 = 1.0 - B2 ** STEP
    m2 = B1 * mv + (1.0 - B1) * gv
    v2 = B2 * vv + (1.0 - B2) * (gv * gv)
    delta = (-LR) * ((m2 / c1) / (jnp.sqrt(v2 / c2) + AEPS) + WD * wv)
    return delta, m2, v2


def _adamw_small(rep_flat, w, m, v):
    offs = _rep_offsets()
    n = len(REP_NAMES)

    def body(*refs):
        g_ref = refs[0]
        w_refs = refs[1:1 + n]
        m_refs = refs[1 + n:1 + 2 * n]
        v_refs = refs[1 + 2 * n:1 + 3 * n]
        outs = refs[1 + 3 * n:]
        go, do, mo, vo = outs[:n], outs[n:2 * n], outs[2 * n:3 * n], outs[3 * n:]

        def emit(i, idx, gv):
            go[i][idx] = gv
            delta, m2, v2 = _adam_math(w_refs[i][idx], gv, m_refs[i][idx], v_refs[i][idx])
            do[i][idx] = delta
            mo[i][idx] = m2
            vo[i][idx] = v2

        for i in range(n):
            if i < 2:
                for b in range(NRB):
                    emit(i, b, g_ref[pl.ds(offs[i] + b * RBW, RBW), :])
            elif REP_NAMES[i] in ("q_norm", "k_norm"):
                emit(i, slice(None), g_ref[pl.ds(offs[i], NG), :])
            else:
                gv = jnp.concatenate([g_ref[pl.ds(offs[i] + k, 1), :] for k in range(REP_ROWS[i])], axis=1)
                emit(i, slice(None), gv)

    full = lambda shp: pl.BlockSpec(shp, lambda: (0,) * len(shp))
    pspecs = [full(a.shape) for a in w]
    pshapes = [jax.ShapeDtypeStruct(a.shape, F32) for a in w]
    res = pl.pallas_call(
        body, name="adamw_small",
        in_specs=[full(rep_flat.shape)] + pspecs * 3,
        out_specs=pspecs * 4, out_shape=pshapes * 4,
        compiler_params=_cp(None, VMEM_MID),
    )(*_hbm(rep_flat, *w, *m, *v))
    return res[:n], res[n:2 * n], res[2 * n:3 * n], res[3 * n:]


def _adamw_many(w, g, m, v):
    n = len(w)

    def body(*refs):
        for i in range(n):
            delta, m2, v2 = _adam_math(refs[i][...], refs[n + i][...], refs[2 * n + i][...], refs[3 * n + i][...])
            refs[4 * n + i][...] = delta
            refs[5 * n + i][...] = m2
            refs[6 * n + i][...] = v2

    full = lambda shp: pl.BlockSpec(shp, lambda: (0,) * len(shp))
    specs = [full(a.shape) for a in w]
    shapes = [jax.ShapeDtypeStruct(a.shape, F32) for a in w]
    res = pl.pallas_call(
        body, name="adamw_shards",
        in_specs=specs * 4, out_specs=specs * 3, out_shape=shapes * 3,
        compiler_params=_cp(None, VMEM_MID),
    )(*_hbm(*w, *g, *m, *v))
    return res[:n], res[n:2 * n], res[2 * n:]


def kernel(x, p, norm_mix, w_in, b_in, conv_w, conv_b, w_rg_a, b_rg_a, w_rg_x, b_rg_x, lru_lambda, q_norm, k_norm, w_o_rnn, w_o_att, w_out, norm_ple, w_ple_gate, b_ple_gate, w_ple, loss_target, m_norm_mix, m_w_in, m_b_in, m_conv_w, m_conv_b, m_w_rg_a, m_b_rg_a, m_w_rg_x, m_b_rg_x, m_lru_lambda, m_q_norm, m_k_norm, m_w_o_rnn, m_w_o_att, m_w_out, m_norm_ple, m_w_ple_gate, m_b_ple_gate, m_w_ple, v_norm_mix, v_w_in, v_b_in, v_conv_w, v_conv_b, v_w_rg_a, v_b_rg_a, v_w_rg_x, v_b_rg_x, v_lru_lambda, v_q_norm, v_k_norm, v_w_o_rnn, v_w_o_att, v_w_out, v_norm_ple, v_w_ple_gate, v_b_ple_gate, v_w_ple):
    w = dict(norm_mix=norm_mix, w_in=w_in, b_in=b_in, conv_w=conv_w, conv_b=conv_b, w_rg_a=w_rg_a, b_rg_a=b_rg_a,
             w_rg_x=w_rg_x, b_rg_x=b_rg_x, lru_lambda=lru_lambda, q_norm=q_norm, k_norm=k_norm, w_o_rnn=w_o_rnn,
             w_o_att=w_o_att, w_out=w_out, norm_ple=norm_ple, w_ple_gate=w_ple_gate, b_ple_gate=b_ple_gate,
             w_ple=w_ple)
    m = dict(norm_mix=m_norm_mix, w_in=m_w_in, b_in=m_b_in, conv_w=m_conv_w, conv_b=m_conv_b, w_rg_a=m_w_rg_a,
             b_rg_a=m_b_rg_a, w_rg_x=m_w_rg_x, b_rg_x=m_b_rg_x, lru_lambda=m_lru_lambda, q_norm=m_q_norm,
             k_norm=m_k_norm, w_o_rnn=m_w_o_rnn, w_o_att=m_w_o_att, w_out=m_w_out, norm_ple=m_norm_ple,
             w_ple_gate=m_w_ple_gate, b_ple_gate=m_b_ple_gate, w_ple=m_w_ple)
    v = dict(norm_mix=v_norm_mix, w_in=v_w_in, b_in=v_b_in, conv_w=v_conv_w, conv_b=v_conv_b, w_rg_a=v_w_rg_a,
             b_rg_a=v_b_rg_a, w_rg_x=v_w_rg_x, b_rg_x=v_b_rg_x, lru_lambda=v_lru_lambda, q_norm=v_q_norm,
             k_norm=v_k_norm, w_o_rnn=v_w_o_rnn, w_o_att=v_w_o_att, w_out=v_w_out, norm_ple=v_norm_ple,
             w_ple_gate=v_w_ple_gate, b_ple_gate=v_b_ple_gate, w_ple=v_w_ple)
    names = list(w.keys())

    shards = [w_in[0].T.astype(BF16), w_o_rnn[0].astype(BF16), w_o_att[0].T.astype(BF16), w_out[0].astype(BF16),
              w_ple_gate[0].astype(BF16), w_ple[0].T.astype(BF16), conv_w[0]]
    pos = _my_pos()
    me, my_core, my_chip = _lin(pos), pos[2], _chip(pos)
    hbm_empty = lambda shp, dt: lax.empty(shp, dt)
    w_shard, conv_shard = shards[0], shards[6]
    shp = w_shard.shape
    entry_token, bufs, sems = _gather_start(
        [w_shard, hbm_empty((2,) + shp, BF16), hbm_empty((2, 2) + shp, BF16), conv_shard,
         hbm_empty((NDEV,) + conv_shard.shape, F32)],
        [("own", (0, 1)), ("near", (0, 2)), ("others", (3, 4))], norm_mix, "gather_start_near")
    w_src, own_l, near_l, conv_src, conv_l = bufs
    sem_own, sem_near, sem_conv = sems
    gather_out = {}

    def project(hn):
        w_thru, own = _gather_wait("own", *sem_own, [w_src, own_l], hn, "gather_wait_own")
        own = lax.dynamic_update_slice(own, w_shard[None], (my_core, 0, 0)).reshape(1, CHIP_COLS, D)
        chips = [jnp.stack([my_chip]), jnp.stack([my_chip ^ 1, my_chip ^ 2]), jnp.stack([my_chip ^ 3])]
        chips = [c.astype(jnp.int32) for c in chips]
        proj = _in_proj_chips(hn, own, b_in, chips[0], None, entry_token, "in_proj_own")
        w_thru, near = _gather_wait("near", *sem_near, [w_thru, near_l], proj, "gather_wait_near")
        near = _forward_to_sibling(near, "gather_forward_near")
        token, (w_thru, far_l), (sem_far,) = _gather_start(
            [w_thru, hbm_empty((2,) + shp, BF16)], [("far", (0, 1))], near, "gather_start_far")
        near = near.reshape(2, CHIP_COLS, D)
        proj = _in_proj_chips(hn, near, b_in, chips[1], proj, token, "in_proj_near")
        w_thru, far = _gather_wait("far", *sem_far, [w_thru, far_l], proj, "gather_wait_far")
        far = _forward_to_sibling(far[None], "gather_forward_far").reshape(1, CHIP_COLS, D)
        proj = _in_proj_chips(hn, far, b_in, chips[2], proj, token, "in_proj_far")
        conv_thru, conv_g = _gather_wait("others", *sem_conv, [conv_src, conv_l], proj, "gather_wait_conv")
        conv_g = lax.dynamic_update_slice(conv_g, conv_shard[None], (me, 0, 0))
        conv_f = conv_g.transpose(1, 0, 2).reshape(CONVW, DR)
        srcs = list(shards[1:6])
        token, obufs, (sem_out,) = _gather_start(
            srcs + [hbm_empty((NDEV,) + a.shape, BF16) for a in srcs], [("others", tuple(range(10)))], proj,
            "gather_start_out")
        gather_out.update(bufs=obufs, sems=sem_out)
        return proj, [own, near, far], jnp.concatenate(chips), conv_f, token

    def other_weights(after):
        obufs = _gather_wait("others", *gather_out["sems"], gather_out["bufs"], after, "gather_wait_out")
        full = [lax.dynamic_update_slice(a, s[None], (me, 0, 0)) for a, s in zip(obufs[5:], shards[1:6])]
        return [a.reshape((NDEV * a.shape[1], a.shape[2])) for a in full]

    def start_reduce(arrs, tag):
        parts = [a.reshape((NCHIP, 2, a.shape[0] // NDEV, a.shape[1])) for a in arrs]
        theirs = _exchange_within_chip(parts, "reduce_within_chip_" + tag)
        return _between_chips_start(_sum_pairs(parts, theirs, "sum_pairs_" + tag), "reduce_between_chips_start_" + tag)

    grad_x, pending_out, pending_in, small = _local_step(
        x.reshape(T, D), p.reshape(T, PLE), loss_target.reshape(T, D),
        project, other_weights,
        norm_mix, conv_b, w_rg_a[0], b_rg_a, w_rg_x[0], b_rg_x, lru_lambda, q_norm[0], k_norm[0],
        norm_ple, b_ple_gate, start_reduce, entry_token)

    rep_parts = _pack_small_grads(small).reshape(NDEV, REP_ROWS_DEV, LANES)
    conv_parts = small["conv_w"].reshape(CONVW, NDEV, DR // NDEV).transpose(1, 0, 2)
    rep_all, g_conv = _allreduce_small(rep_parts, conv_parts, "allreduce_small")
    rep_all = rep_all.reshape(NDEV * REP_ROWS_DEV, LANES)
    loss = rep_all[LOSS_ROW, 0]

    myq = _chip(_my_pos())
    own_out, recv_out = _between_chips_wait(pending_out, rep_all, "reduce_between_chips_wait_out")
    own_in, recv_in = _between_chips_wait(pending_in, rep_all, "reduce_between_chips_wait_in")
    w_in_res = _sum_chips_adamw(own_in[0], recv_in[0], w_in[0].T, m_w_in[0].T, v_w_in[0].T, 304, "adamw_w_in")
    g_o_rnn, g_o_att, g_out, g_pg, g_ple = _sum_blocks_small(
        own_out, recv_out, myq, (False, True, False, False, True), "sum_chips_out")

    grad, delta, new_m, new_v = {}, {}, {}, {}
    rep_shape = lambda a: a if a.ndim == 2 else a.reshape(a.shape[1:])
    res = _adamw_small(rep_all, [rep_shape(w[n]) for n in REP_NAMES], [rep_shape(m[n]) for n in REP_NAMES],
                       [rep_shape(v[n]) for n in REP_NAMES])
    for dst, vals in zip((grad, delta, new_m, new_v), res):
        for n, a in zip(REP_NAMES, vals):
            dst[n] = a.reshape(w[n].shape)
    grad["w_in"], delta["w_in"], new_m["w_in"], new_v["w_in"] = [a.T[None] for a in w_in_res]
    rest = ("w_o_rnn", "w_o_att", "w_out", "w_ple_gate", "w_ple", "conv_w")
    g_rest = [g_o_rnn, g_o_att, g_out, g_pg, g_ple, g_conv]
    res = _adamw_many([w[n][0] for n in rest], g_rest, [m[n][0] for n in rest], [v[n][0] for n in rest])
    for n, a in zip(rest, g_rest):
        grad[n] = a[None]
    for dst, vals in zip((delta, new_m, new_v), res):
        for n, a in zip(rest, vals):
            dst[n] = a[None]

    return (loss, grad_x.reshape(BL, S, D), *[grad[n] for n in names], *[delta[n] for n in names],
            *[new_m[n] for n in names], *[new_v[n] for n in names])
```

```python
import jax
import jax.numpy as jnp
from jax import lax
from jax.experimental import pallas as pl
from jax.experimental.pallas import tpu as pltpu

F32 = jnp.float32
BF16 = jnp.bfloat16

D = 1024
S = 2048
BL = 2
T = BL * S
NDEV = 8
NCHIP = 4
PLE = 256
DR = 1280
NRB = 10
RBW = 128
CONVW = 4
LRU_C = 8.0
HD = 128
NH = 4
PATTERNS = ((128, 1), (512, 4), (2048, 16))
NG = 3
ATT = NH * HD
GW = NG * ATT
NIN = 2 * DR + 3 * GW + ATT + 2 * D
OFF_ZR = DR
OFF_Q = 2 * DR
OFF_ZA = OFF_Q + 3 * GW
OFF_G = OFF_ZA + ATT
ROPE_THETA = 10000.0
EPS = 1e-6
SCALE = HD ** -0.5
NEG = -1e30
QB = 128
LANES = 128
CT = 512
NCT = NIN // CT
A_W = 2 * DR
C_W = ATT + 2 * D

LR, B1, B2, AEPS, WD, STEP = 0.001, 0.9, 0.999, 1e-08, 0.01, 10

NSHARD_IN = NIN // NDEV
REP_NAMES = ("w_rg_a", "w_rg_x", "norm_mix", "b_in", "conv_b", "b_rg_a", "b_rg_x", "lru_lambda", "q_norm",
             "k_norm", "norm_ple", "b_ple_gate")
REP_ROWS = (NRB * RBW, NRB * RBW, D // LANES, NIN // LANES, DR // LANES, DR // LANES, DR // LANES, DR // LANES,
            NG, NG, D // LANES, D // LANES)
REP_TOTAL_ROWS = sum(REP_ROWS)
REP_ROWS_DEV = 344
BIG_NAMES = ("w_in", "w_o_rnn", "w_o_att", "w_out", "w_ple_gate", "w_ple")

VMEM_BIG = 56 * 1024 * 1024
VMEM_MID = 40 * 1024 * 1024


def _cp(sem=None, vmem=None):
    return pltpu.CompilerParams(dimension_semantics=sem, vmem_limit_bytes=vmem)


def _hbm(*arrays):
    return [pltpu.with_memory_space_constraint(a, pltpu.HBM) for a in arrays]


def _dot(a, b):
    return jnp.dot(a, b, preferred_element_type=F32)


def _dot_nt(a, b):
    return lax.dot_general(a, b, (((1,), (1,)), ((), ())), preferred_element_type=F32)


def _dot_tn(a, b):
    return lax.dot_general(a, b, (((0,), (0,)), ((), ())), preferred_element_type=F32)


def _sigmoid(x):
    return jax.nn.sigmoid(x)


def _perm(j):
    jq = j - OFF_Q // CT
    inside = (j >= OFF_Q // CT) & (j < OFF_ZA // CT)
    return jnp.where(inside, OFF_Q // CT + (jq % 3) * 3 + jq // 3, j)


PIECES = ((0, A_W // CT), (OFF_Q // CT, GW // CT), (OFF_Q // CT + 3, GW // CT), (OFF_Q // CT + 6, GW // CT),
          (OFF_ZA // CT, C_W // CT))


def _rmsnorm_fwd(x, gain, token, tm=512):
    def body(x_ref, g_ref, _token, o_ref):
        xv = x_ref[...]
        var = jnp.mean(xv * xv, axis=-1, keepdims=True)
        o_ref[...] = (xv * lax.rsqrt(var + EPS) * g_ref[...]).astype(BF16)

    return pl.pallas_call(
        body, grid=(T // tm,), name="rmsnorm_fwd",
        in_specs=[pl.BlockSpec((tm, D), lambda i: (i, 0)), pl.BlockSpec((1, D), lambda i: (0, 0)),
                  pl.BlockSpec((8, LANES), lambda i: (0, 0))],
        out_specs=pl.BlockSpec((tm, D), lambda i: (i, 0)),
        out_shape=jax.ShapeDtypeStruct((T, D), BF16),
        compiler_params=_cp(("parallel",)),
    )(*_hbm(x, gain, token))


CHIP_COLS = NIN // NCHIP


def _in_proj_chips(hn, w_rows, bias, chips, proj, token, name, tm=1024):
    n = w_rows.shape[0]

    def body(chips_ref, a_ref, w_ref, b_ref, _token, *rest):
        o_ref = rest[-1]
        o_ref[...] = (_dot_nt(a_ref[...], w_ref[0]) + b_ref[...]).astype(BF16)

    in_specs = [pl.BlockSpec((tm, D), lambda s, i, ch: (i, 0)),
                pl.BlockSpec((1, CHIP_COLS, D), lambda s, i, ch: (s, 0, 0)),
                pl.BlockSpec((1, CHIP_COLS), lambda s, i, ch: (0, ch[s])),
                pl.BlockSpec((8, LANES), lambda s, i, ch: (0, 0))]
    args = [hn, w_rows, bias, token]
    aliases = {}
    if proj is not None:
        in_specs.append(pl.BlockSpec(memory_space=pl.ANY))
        args.append(proj)
        aliases = {5: 0}
    return pl.pallas_call(
        body, name=name,
        grid_spec=pltpu.PrefetchScalarGridSpec(
            num_scalar_prefetch=1, grid=(n, T // tm), in_specs=in_specs,
            out_specs=pl.BlockSpec((tm, CHIP_COLS), lambda s, i, ch: (i, ch[s]))),
        out_shape=jax.ShapeDtypeStruct((T, NIN), BF16),
        input_output_aliases=aliases,
        compiler_params=_cp(("arbitrary", "arbitrary"), VMEM_BIG),
    )(chips, *_hbm(*args))


def _grad_x(pieces, w_bufs, chips, token, x, dx1, gain, tm=512):
    nb = len(w_bufs)

    def body(chips_ref, a_ref, q_ref, k_ref, v_ref, c_ref, *rest):
        w_hbm = rest[:nb]
        x_ref, dx1_ref, g_ref, dx_ref, dg_ref, w = rest[nb + 1:]
        first = pl.program_id(0) == 0

        @pl.when(first)
        def _():
            s = 0
            for buf in w_hbm:
                for r in range(buf.shape[0]):
                    row = pl.multiple_of(chips_ref[s] * CHIP_COLS, 128)
                    pltpu.sync_copy(buf.at[r], w.at[pl.ds(row, CHIP_COLS), :])
                    s += 1

        acc = _dot(a_ref[...], w[pl.ds(0, A_W), :])
        for kind, p_ref in enumerate((q_ref, k_ref, v_ref)):
            for g in range(NG):
                row = OFF_Q + (3 * g + kind) * CT
                acc = acc + _dot(p_ref[:, g * CT:(g + 1) * CT], w[pl.ds(row, CT), :])
        dn = acc + _dot(c_ref[...], w[pl.ds(OFF_ZA, C_W), :])
        xv = x_ref[...]
        rstd = lax.rsqrt(jnp.mean(xv * xv, axis=-1, keepdims=True) + EPS)
        xh = xv * rstd
        dg = jnp.sum(dn * xh, axis=0, keepdims=True)
        gd = dn * g_ref[...]
        dx_ref[...] = dx1_ref[...] + rstd * (gd - xh * jnp.mean(gd * xh, axis=-1, keepdims=True))

        @pl.when(first)
        def _():
            dg_ref[...] = dg

        @pl.when(jnp.logical_not(first))
        def _():
            dg_ref[...] += dg

    tok = lambda wd: pl.BlockSpec((tm, wd), lambda i, ch: (i, 0))
    vec = lambda: pl.BlockSpec((1, D), lambda i, ch: (0, 0))
    return pl.pallas_call(
        body, name="grad_x",
        grid_spec=pltpu.PrefetchScalarGridSpec(
            num_scalar_prefetch=1, grid=(T // tm,),
            in_specs=[tok(A_W), tok(GW), tok(GW), tok(GW), tok(C_W)] + [pl.BlockSpec(memory_space=pl.ANY)] * nb
            + [pl.BlockSpec((8, LANES), lambda i, ch: (0, 0)), tok(D), tok(D), vec()],
            out_specs=[tok(D), vec()],
            scratch_shapes=[pltpu.VMEM((NIN, D), BF16)]),
        out_shape=[jax.ShapeDtypeStruct((T, D), F32), jax.ShapeDtypeStruct((1, D), F32)],
        compiler_params=_cp(("arbitrary",), VMEM_BIG),
    )(chips, *_hbm(*pieces, *w_bufs, token, x, dx1, gain))


def _dw_in(pieces, hn):
    def body(a_ref, q_ref, k_ref, v_ref, c_ref, h_hbm, o_ref, s_ref, h):
        j = pl.program_id(0)

        @pl.when(j == 0)
        def _():
            pltpu.sync_copy(h_hbm, h)

        def step(x_ref):
            xv = x_ref[...]
            o_ref[...] = _dot_tn(xv, h[...]).astype(BF16)
            s_ref[...] = jnp.sum(xv.astype(F32), axis=0, keepdims=True)

        for x_ref, (lo, n) in zip((a_ref, q_ref, k_ref, v_ref, c_ref), PIECES):
            pl.when((j >= lo) & (j < lo + n))(lambda x_ref=x_ref: step(x_ref))

    def piece_spec(lo, n):
        return pl.BlockSpec((T, CT), lambda j: (0, jnp.clip(j - lo, 0, n - 1)))

    return pl.pallas_call(
        body, grid=(NCT,), name="dw_in",
        in_specs=[piece_spec(lo, n) for lo, n in PIECES] + [pl.BlockSpec(memory_space=pl.ANY)],
        out_specs=[pl.BlockSpec((CT, D), lambda j: (_perm(j), 0)), pl.BlockSpec((1, CT), lambda j: (0, _perm(j)))],
        out_shape=[jax.ShapeDtypeStruct((NIN, D), BF16), jax.ShapeDtypeStruct((1, NIN), F32)],
        scratch_shapes=[pltpu.VMEM((T, D), BF16)],
        compiler_params=_cp(("arbitrary",), VMEM_BIG),
    )(*_hbm(*pieces, hn))


def _mm_tn(a, b, ta, tt, name):
    m = a.shape[1]
    n = b.shape[1]
    nt = T // tt

    def body(a_ref, b_ref, o_ref, acc):
        t = pl.program_id(1)
        p = _dot_tn(a_ref[...].astype(BF16), b_ref[...].astype(BF16))

        @pl.when(t == 0)
        def _():
            acc[...] = p

        @pl.when(t > 0)
        def _():
            acc[...] += p

        @pl.when(t == nt - 1)
        def _():
            o_ref[...] = acc[...].astype(BF16)

    return pl.pallas_call(
        body, grid=(m // ta, nt), name=name,
        in_specs=[pl.BlockSpec((tt, ta), lambda j, t: (t, j)), pl.BlockSpec((tt, n), lambda j, t: (t, 0))],
        out_specs=pl.BlockSpec((ta, n), lambda j, t: (j, 0)),
        out_shape=jax.ShapeDtypeStruct((m, n), BF16),
        scratch_shapes=[pltpu.VMEM((ta, n), F32)],
        compiler_params=_cp(("parallel", "arbitrary"), VMEM_MID),
    )(*_hbm(a, b))


def _row_iota():
    return lax.broadcasted_iota(jnp.int32, (S, RBW), 0)


SUBLANES = 8
N_SHIFT_BUFS = 4


class _Shifter:
    def __init__(self, bufs):
        self.bufs = bufs
        self.k = 0

    def _store(self, v, fill, front):
        b = self.bufs.at[self.k % N_SHIFT_BUFS]
        self.k += 1
        b[pl.ds(0 if front else SUBLANES + S, SUBLANES), :] = jnp.full((SUBLANES, RBW), fill, F32)
        b[pl.ds(SUBLANES, S), :] = v
        return b

    def down(self, v, ds, fill):
        b = self._store(v, fill, True)
        return [b[pl.ds(SUBLANES - d, S), :] for d in ds]

    def up(self, v, ds, fill):
        b = self._store(v, fill, False)
        return [b[pl.ds(SUBLANES + d, S), :] for d in ds]


def _shift_down(v, d, sh, fill):
    return sh.down(v, (d,), fill)[0]


def _shift_up(v, d, sh, fill):
    return sh.up(v, (d,), fill)[0]


def _scan_down(a, u, row):
    d = 1
    while d < S:
        last = 2 * d >= S
        if d < SUBLANES:
            u = a * _shift_down(u, d, row, 0.0) + u
            if not last:
                a = a * _shift_down(a, d, row, 1.0)
        else:
            u = jnp.concatenate([u[:d], a[d:] * u[:S - d] + u[d:]], axis=0)
            if not last:
                a = jnp.concatenate([a[:d], a[d:] * a[:S - d]], axis=0)
        d *= 2
    return u


def _scan_up(b, g, row):
    d = 1
    while d < S:
        last = 2 * d >= S
        if d < SUBLANES:
            g = g + b * _shift_up(g, d, row, 0.0)
            if not last:
                b = b * _shift_up(b, d, row, 0.0)
        else:
            g = jnp.concatenate([g[:S - d] + b[:S - d] * g[d:], g[S - d:]], axis=0)
            if not last:
                b = jnp.concatenate([b[:S - d] * b[d:], b[S - d:]], axis=0)
        d *= 2
    return g


def _softplus(x):
    return jnp.maximum(x, 0.0) + jnp.log1p(jnp.exp(-jnp.abs(x)))


def _rnn_gates(x, cw, cb, wa, ba, wx, bx, lam, row, pad):
    xs = pad.down(x, (1, 2, 3), 0.0)
    xc = cb + cw[3:4, :] * x
    for j in (1, 2, 3):
        xc = xc + cw[3 - j:4 - j, :] * xs[j - 1]
    xcb = xc.astype(BF16)
    r = _sigmoid(_dot(xcb, wa) + ba)
    i = _sigmoid(_dot(xcb, wx) + bx)
    sp = _softplus(-lam)
    log_a = (-LRU_C) * r * sp
    a = jnp.exp(log_a)
    mult = jnp.where(row == 0, 1.0, jnp.sqrt(jnp.tanh(-log_a) * (1.0 + a * a)))
    return xc, xcb, r, i, sp, a, mult, xs


def _rnn_fwd(proj3, conv_w, conv_b, wa, ba, wx, bx, lam, token):
    def body(x_ref, cw_ref, cb_ref, wa_ref, ba_ref, wx_ref, bx_ref, lam_ref, _token, h_ref, pad):
        row = _row_iota()
        sh = _Shifter(pad)
        x = x_ref[0].astype(F32)
        xc, _, _, i, _, a, mult, _ = _rnn_gates(x, cw_ref[...], cb_ref[...], wa_ref[0], ba_ref[...],
                                             wx_ref[0], bx_ref[...], lam_ref[...], row, sh)
        h_ref[0] = _scan_down(a, mult * (i * xc), sh)

    vec = lambda: pl.BlockSpec((1, RBW), lambda b, n: (0, n))
    mat = lambda: pl.BlockSpec((1, RBW, RBW), lambda b, n: (n, 0, 0))
    return pl.pallas_call(
        body, grid=(BL, NRB), name="rnn_fwd",
        in_specs=[pl.BlockSpec((1, S, RBW), lambda b, n: (b, 0, n)),
                  pl.BlockSpec((CONVW, RBW), lambda b, n: (0, n)),
                  vec(), mat(), vec(), mat(), vec(), vec(), pl.BlockSpec((8, LANES), lambda b, n: (0, 0))],
        out_specs=pl.BlockSpec((1, S, RBW), lambda b, n: (b, 0, n)),
        out_shape=jax.ShapeDtypeStruct((BL, S, DR), F32),
        scratch_shapes=[pltpu.VMEM((N_SHIFT_BUFS, S + 2 * SUBLANES, RBW), F32)],
        compiler_params=_cp(("parallel", "parallel"), VMEM_MID),
    )(*_hbm(proj3, conv_w, conv_b, wa, ba, wx, bx, lam, token))


def _rnn_bwd(proj3, h3, dh3, slab_a3, conv_w, conv_b, wa, ba, wx, bx, lam, token):
    def body(x_ref, h_ref, dh_ref, cw_ref, cb_ref, wa_ref, ba_ref, wx_ref, bx_ref, lam_ref, _alias, _token,
             dx_ref, dcw_ref, dcb_ref, dwa_ref, dba_ref, dwx_ref, dbx_ref, dlam_ref, pad):
        row = _row_iota()
        sh = _Shifter(pad)
        x = x_ref[0].astype(F32)
        cw = cw_ref[...]
        wa_v = wa_ref[0]
        wx_v = wx_ref[0]
        lam_v = lam_ref[...]
        xc, xcb, r, i, sp, a, mult, xs = _rnn_gates(x, cw, cb_ref[...], wa_v, ba_ref[...], wx_v, bx_ref[...], lam_v,
                                                    row, sh)
        h = h_ref[0]
        g = _scan_up(_shift_up(a, 1, sh, 0.0), dh_ref[0], sh)
        da = g * _shift_down(h, 1, sh, 0.0)
        dmult = jnp.where(row == 0, 0.0, g * (i * xc))
        gm = g * mult
        di = gm * xc
        dxc = gm * i
        dlog_a = da * a - dmult * (a * a) / mult
        dr = dlog_a * ((-LRU_C) * sp)
        dsp = jnp.sum(dlog_a * ((-LRU_C) * r), axis=0, keepdims=True)
        dlam = dsp * (-_sigmoid(-lam_v))
        dpa = dr * r * (1.0 - r)
        dpx = di * i * (1.0 - i)
        dpab = dpa.astype(BF16)
        dpxb = dpx.astype(BF16)
        dwa = _dot_tn(xcb, dpab)
        dwx = _dot_tn(xcb, dpxb)
        dba = jnp.sum(dpa, axis=0, keepdims=True)
        dbx = jnp.sum(dpx, axis=0, keepdims=True)
        dxc = dxc + _dot_nt(dpab, wa_v) + _dot_nt(dpxb, wx_v)
        dcb = jnp.sum(dxc, axis=0, keepdims=True)
        dx = cw[3:4, :] * dxc
        dcw_rows = [None] * CONVW
        dcw_rows[3] = jnp.sum(dxc * x, axis=0, keepdims=True)
        dxc_up = sh.up(dxc, (1, 2, 3), 0.0)
        for j in (1, 2, 3):
            dx = dx + cw[3 - j:4 - j, :] * dxc_up[j - 1]
            dcw_rows[3 - j] = jnp.sum(dxc * xs[j - 1], axis=0, keepdims=True)
        dx_ref[0] = dx.astype(BF16)
        dcw = jnp.concatenate(dcw_rows, axis=0)
        first = pl.program_id(1) == 0

        @pl.when(first)
        def _():
            dcw_ref[...] = dcw
            dcb_ref[...] = dcb
            dwa_ref[0] = dwa
            dba_ref[...] = dba
            dwx_ref[0] = dwx
            dbx_ref[...] = dbx
            dlam_ref[...] = dlam

        @pl.when(jnp.logical_not(first))
        def _():
            dcw_ref[...] += dcw
            dcb_ref[...] += dcb
            dwa_ref[0] += dwa
            dba_ref[...] += dba
            dwx_ref[0] += dwx
            dbx_ref[...] += dbx
            dlam_ref[...] += dlam

    slab = lambda: pl.BlockSpec((1, S, RBW), lambda n, b: (b, 0, n))
    vec = lambda: pl.BlockSpec((1, RBW), lambda n, b: (0, n))
    mat = lambda: pl.BlockSpec((1, RBW, RBW), lambda n, b: (n, 0, 0))
    taps = lambda: pl.BlockSpec((CONVW, RBW), lambda n, b: (0, n))
    vshape = jax.ShapeDtypeStruct((1, DR), F32)
    mshape = jax.ShapeDtypeStruct((NRB, RBW, RBW), F32)
    return pl.pallas_call(
        body, grid=(NRB, BL), name="rnn_bwd",
        in_specs=[slab(), slab(), slab(), taps(), vec(), mat(), vec(), mat(), vec(), vec(),
                  pl.BlockSpec(memory_space=pl.ANY), pl.BlockSpec((8, LANES), lambda n, b: (0, 0))],
        out_specs=[slab(), taps(), vec(), mat(), vec(), mat(), vec(), vec()],
        out_shape=[jax.ShapeDtypeStruct((BL, S, A_W), BF16), jax.ShapeDtypeStruct((CONVW, DR), F32),
                   vshape, mshape, vshape, mshape, vshape, vshape],
        input_output_aliases={10: 0},
        scratch_shapes=[pltpu.VMEM((N_SHIFT_BUFS, S + 2 * SUBLANES, RBW), F32)],
        compiler_params=_cp(("parallel", "arbitrary"), 48 * 1024 * 1024),
    )(*_hbm(proj3, h3, dh3, conv_w, conv_b, wa, ba, wx, bx, lam, slab_a3, token))


NQB = S // QB


def _rms_head(t, gain):
    rstd = lax.rsqrt(jnp.mean(t * t, axis=-1, keepdims=True) + EPS)
    return t * rstd * gain


def _rope(t, cs, sn):
    return t * cs + pltpu.roll(t, HD // 2, 1) * sn


def _rope_t(dy, cs, sn):
    return dy * cs - pltpu.roll(dy, HD // 2, 1) * sn


def _bdot_nt(a, b):
    return lax.dot_general(a, b, (((2,), (2,)), ((0,), (0,))), preferred_element_type=F32)


def _bdot(a, b):
    return lax.dot_general(a, b, (((2,), (1,)), ((0,), (0,))), preferred_element_type=F32)


def _bdot_tn(a, b):
    return lax.dot_general(a, b, (((1,), (1,)), ((0,), (0,))), preferred_element_type=F32)


STRIDE_MAX = 4


def _permute(buf, x, dil, dst, off=0):
    ln = S // dil
    if dil == 1:
        dst[pl.ds(off, S), :] = x.astype(dst.dtype)
        return
    buf[0] = x
    if dil <= STRIDE_MAX:
        for c in range(dil):
            dst[pl.ds(off + c * ln, ln), :] = buf.at[0][pl.ds(c, ln, stride=dil), :].astype(dst.dtype)
        return
    f, r = STRIDE_MAX, dil // STRIDE_MAX
    part = S // f
    for c1 in range(f):
        buf.at[1][pl.ds(c1 * part, part), :] = buf.at[0][pl.ds(c1, part, stride=f), :]
    for c1 in range(f):
        for c2 in range(r):
            dst[pl.ds(off + (c1 + f * c2) * ln, ln), :] = (
                buf.at[1][pl.ds(c1 * part + c2, ln, stride=r), :].astype(dst.dtype))


def _unpermute(buf, xp, dil, dst):
    ln = S // dil
    if dil == 1:
        dst[...] = xp
        return
    if dil <= STRIDE_MAX:
        for c in range(dil):
            dst[pl.ds(c, ln, stride=dil), :] = xp[c * ln:(c + 1) * ln]
        return
    f, r = STRIDE_MAX, dil // STRIDE_MAX
    part = S // f
    for c1 in range(f):
        for c2 in range(r):
            c = c1 + f * c2
            buf.at[1][pl.ds(c1 * part + c2, ln, stride=r), :] = xp[c * ln:(c + 1) * ln]
    for c1 in range(f):
        dst[pl.ds(c1, part, stride=f), :] = buf[1, pl.ds(c1 * part, part), :]


def _blocks3(ref, off=0):
    return ref[pl.ds(off, S), :].reshape(NQB, QB, HD)


def _att_prep(q_ref, k_ref, v_ref, cos_ref, sin_ref, qn, kn, dil, nat, qs, ksp, vsp):
    cs = cos_ref[...]
    sn = sin_ref[...]
    zero = jnp.zeros((QB, HD), BF16)
    ksp[pl.ds(0, QB), :] = zero
    vsp[pl.ds(0, QB), :] = zero
    _permute(nat, _rope(_rms_head(q_ref[0].astype(F32), qn), cs, sn), dil, qs)
    _permute(nat, _rope(_rms_head(k_ref[0].astype(F32), kn), cs, sn), dil, ksp, QB)
    _permute(nat, v_ref[0].astype(F32), dil, vsp, QB)


def _att_scores(qs, ksp, dil):
    nb = S // dil // QB
    q3 = _blocks3(qs)
    shape = (NQB, QB, QB)
    qi = lax.broadcasted_iota(jnp.int32, shape, 1)
    kj = lax.broadcasted_iota(jnp.int32, shape, 2)
    s_c = jnp.where(qi >= kj, _bdot_nt(q3, _blocks3(ksp, QB)) * SCALE, NEG)
    if nb == 1:
        return q3, s_c, None
    jj = lax.broadcasted_iota(jnp.int32, shape, 0)
    ok = (kj >= qi) & ((jj & (nb - 1)) != 0)
    s_p = jnp.where(ok, _bdot_nt(q3, _blocks3(ksp)) * SCALE, NEG)
    return q3, s_c, s_p


def _qkv_spec(kind, g):
    base = OFF_Q // HD + (3 * g + kind) * NH
    return pl.BlockSpec((1, S, HD), lambda b, h: (b, 0, base + h))


def _attn_fwd(proj3, cos_t, sin_t, q_norm, k_norm):
    def body(*refs):
        qkv_refs = refs[:9]
        cos_ref, sin_ref, qn_ref, kn_ref, att_ref, lse_ref, w_ref, nat, qs, ksp, vsp, og = refs[9:]
        for g, (window, dil) in enumerate(PATTERNS):
            q_ref, k_ref, v_ref = qkv_refs[3 * g:3 * g + 3]
            _att_prep(q_ref, k_ref, v_ref, cos_ref, sin_ref, qn_ref[g:g + 1, :], kn_ref[g:g + 1, :], dil,
                      nat, qs, ksp, vsp)
            _, s_c, s_p = _att_scores(qs, ksp, dil)
            m = jnp.max(s_c, axis=-1, keepdims=True)
            if s_p is not None:
                m = jnp.maximum(m, jnp.max(s_p, axis=-1, keepdims=True))
            e_c = jnp.exp(s_c - m)
            den = jnp.sum(e_c, axis=-1, keepdims=True)
            o = _bdot(e_c.astype(BF16), _blocks3(vsp, QB))
            if s_p is not None:
                e_p = jnp.exp(s_p - m)
                den = den + jnp.sum(e_p, axis=-1, keepdims=True)
                o = o + _bdot(e_p.astype(BF16), _blocks3(vsp))
            _unpermute(nat, (o / den).reshape(S, HD), dil, og.at[g])
            _unpermute(nat, jnp.broadcast_to(m + jnp.log(den), (NQB, QB, HD)).reshape(S, HD), dil,
                       lse_ref.at[g, 0])
        l0 = lse_ref[0, 0]
        l1 = lse_ref[1, 0]
        l2 = lse_ref[2, 0]
        mx = jnp.maximum(jnp.maximum(l0, l1), l2)
        e0 = jnp.exp(l0 - mx)
        e1 = jnp.exp(l1 - mx)
        e2 = jnp.exp(l2 - mx)
        inv = 1.0 / (e0 + e1 + e2)
        w0 = e0 * inv
        w1 = e1 * inv
        w2 = e2 * inv
        w_ref[0, 0] = w0
        w_ref[1, 0] = w1
        w_ref[2, 0] = w2
        att_ref[0] = w0 * og[0] + w1 * og[1] + w2 * og[2]

    in_specs = [_qkv_spec(kind, g) for g in range(NG) for kind in range(3)]
    in_specs += [pl.BlockSpec((S, HD), lambda b, h: (0, 0)), pl.BlockSpec((S, HD), lambda b, h: (0, 0)),
                 pl.BlockSpec((NG, HD), lambda b, h: (0, 0)), pl.BlockSpec((NG, HD), lambda b, h: (0, 0))]
    stat = lambda: pl.BlockSpec((NG, 1, S, HD), lambda b, h: (0, b, 0, h))
    return pl.pallas_call(
        body, grid=(BL, NH), name="attn_fwd",
        in_specs=in_specs,
        out_specs=[pl.BlockSpec((1, S, HD), lambda b, h: (b, 0, h)), stat(), stat()],
        out_shape=[jax.ShapeDtypeStruct((BL, S, ATT), F32),
                   jax.ShapeDtypeStruct((NG, BL, S, ATT), F32),
                   jax.ShapeDtypeStruct((NG, BL, S, ATT), F32)],
        scratch_shapes=[pltpu.VMEM((2, S, HD), F32), pltpu.VMEM((S, HD), BF16), pltpu.VMEM((S + QB, HD), BF16),
                        pltpu.VMEM((S + QB, HD), BF16), pltpu.VMEM((NG, S, HD), F32)],
        compiler_params=_cp(("parallel", "parallel"), VMEM_BIG),
    )(*_hbm(*([proj3] * 9), cos_t, sin_t, q_norm, k_norm))


def _attn_bwd_group(g, proj3, cos_t, sin_t, qn_g, kn_g, lse, wts, datt3, sbar3, slabs):
    dil = PATTERNS[g][1]
    n_alias = 0 if slabs is None else 3

    def norm_rope_bwd(dpost, raw, gain, cs, sn):
        dn = _rope_t(dpost, cs, sn)
        rstd = lax.rsqrt(jnp.mean(raw * raw, axis=-1, keepdims=True) + EPS)
        xh = raw * rstd
        dgain = jnp.sum(dn * xh, axis=0, keepdims=True)
        gd = dn * gain
        draw = rstd * (gd - xh * jnp.mean(gd * xh, axis=-1, keepdims=True))
        return draw, dgain

    def body(*refs):
        (q_ref, k_ref, v_ref, cos_ref, sin_ref, qn_ref, kn_ref, lse_ref, w_ref, datt_ref, sbar_ref) = refs[:11]
        (dq_ref, dk_ref, dv_ref, dqn_ref, dkn_ref, nat, qs, ksp, vsp, dos, cvp, lsp, acc) = refs[11 + n_alias:]
        qn = qn_ref[...]
        kn = kn_ref[...]
        cs = cos_ref[...]
        sn = sin_ref[...]
        _att_prep(q_ref, k_ref, v_ref, cos_ref, sin_ref, qn, kn, dil, nat, qs, ksp, vsp)
        wv = w_ref[0, 0]
        _permute(nat, wv * datt_ref[0], dil, dos)
        _permute(nat, wv * sbar_ref[0], dil, cvp)
        _permute(nat, lse_ref[0, 0], dil, lsp)
        q3, s_c, s_p = _att_scores(qs, ksp, dil)
        do3 = _blocks3(dos)
        lse3 = _blocks3(lsp)[:, :, 0:1]
        cv3 = _blocks3(cvp)[:, :, 0:1]
        p_c = jnp.exp(s_c - lse3)
        ds_c = (p_c * (_bdot_nt(do3, _blocks3(vsp, QB)) - cv3)).astype(BF16)
        dq = _bdot(ds_c, _blocks3(ksp, QB))
        acc[0] = _bdot_tn(ds_c, q3).reshape(S, HD)
        acc[1] = _bdot_tn(p_c.astype(BF16), do3).reshape(S, HD)
        if s_p is not None:
            p_p = jnp.exp(s_p - lse3)
            ds_p = (p_p * (_bdot_nt(do3, _blocks3(vsp)) - cv3)).astype(BF16)
            dq = dq + _bdot(ds_p, _blocks3(ksp))
            early = pl.ds(0, S - QB)
            acc[0, early, :] += _bdot_tn(ds_p, q3).reshape(S, HD)[QB:]
            acc[1, early, :] += _bdot_tn(p_p.astype(BF16), do3).reshape(S, HD)[QB:]
        _unpermute(nat, (dq * SCALE).reshape(S, HD), dil, nat.at[0])
        draw, dqn = norm_rope_bwd(nat[0], q_ref[0].astype(F32), qn, cs, sn)
        dq_ref[0] = draw.astype(BF16)
        _unpermute(nat, acc[0] * SCALE, dil, nat.at[0])
        draw, dkn = norm_rope_bwd(nat[0], k_ref[0].astype(F32), kn, cs, sn)
        dk_ref[0] = draw.astype(BF16)
        _unpermute(nat, acc[1], dil, nat.at[0])
        dv_ref[0] = nat[0].astype(BF16)
        first = (pl.program_id(0) == 0) & (pl.program_id(1) == 0)

        @pl.when(first)
        def _():
            dqn_ref[...] = dqn
            dkn_ref[...] = dkn

        @pl.when(jnp.logical_not(first))
        def _():
            dqn_ref[...] += dqn
            dkn_ref[...] += dkn

    full = lambda r: pl.BlockSpec((r, HD), lambda b, h: (0, 0))
    stat = lambda: pl.BlockSpec((1, 1, S, HD), lambda b, h: (g, b, 0, h))
    slab = lambda: pl.BlockSpec((1, S, HD), lambda b, h: (b, 0, h))
    out_slab = lambda: pl.BlockSpec((1, S, HD), lambda b, h: (b, 0, g * NH + h))
    big = jax.ShapeDtypeStruct((BL, S, GW), BF16)
    vecs = jax.ShapeDtypeStruct((1, HD), F32)
    in_specs = [_qkv_spec(0, g), _qkv_spec(1, g), _qkv_spec(2, g), full(S), full(S), full(1), full(1),
                stat(), stat(), slab(), slab()]
    args = [proj3, proj3, proj3, cos_t, sin_t, qn_g, kn_g, lse, wts, datt3, sbar3]
    aliases = {}
    if slabs is not None:
        in_specs += [pl.BlockSpec(memory_space=pl.ANY)] * 3
        args += list(slabs)
        aliases = {11: 0, 12: 1, 13: 2}
    return pl.pallas_call(
        body, grid=(BL, NH), name="attn_bwd_g%d" % g,
        in_specs=in_specs,
        out_specs=[out_slab(), out_slab(), out_slab(), full(1), full(1)],
        out_shape=[big, big, big, vecs, vecs],
        scratch_shapes=[pltpu.VMEM((2, S, HD), F32), pltpu.VMEM((S, HD), BF16), pltpu.VMEM((S + QB, HD), BF16),
                        pltpu.VMEM((S + QB, HD), BF16), pltpu.VMEM((S, HD), BF16), pltpu.VMEM((S, HD), F32),
                        pltpu.VMEM((S, HD), F32), pltpu.VMEM((2, S, HD), F32)],
        input_output_aliases=aliases,
        compiler_params=_cp(("arbitrary", "arbitrary"), VMEM_BIG),
    )(*_hbm(*args))


def _tail(x, proj, h, att, p, tgt, w_o_rnn, w_o_att_t, w_out, w_pg, w_ple_t, norm_ple, b_pg, tm=256):
    nt = T // tm
    inv_d = 1.0 / D

    def body(x_ref, h_ref, zr_ref, att_ref, za_ref, g0a_ref, g0b_ref, g1a_ref, g1b_ref, p_ref, tgt_ref,
             np_ref, bpg_ref, wor_hbm, woa_hbm, wout_hbm, wpg_hbm, wple_hbm,
             dx1_ref, merged_ref, n1_ref, dpre_ref, dpe_ref, dyr_ref, dya_ref, slab_a_ref, slab_c_ref, dh_ref,
             datt_ref, sbar_ref, yrnn_ref, yatt_ref, loss_ref, dnp_ref, dbpg_ref,
             wor, woa, wout, wpg, wple):
        first = pl.program_id(0) == 0

        @pl.when(first)
        def _():
            pltpu.sync_copy(wor_hbm, wor)
            pltpu.sync_copy(woa_hbm, woa)
            pltpu.sync_copy(wout_hbm, wout)
            pltpu.sync_copy(wpg_hbm, wpg)
            pltpu.sync_copy(wple_hbm, wple)

        xv = x_ref[...]
        hv = h_ref[...]
        zr = zr_ref[...].astype(F32)
        av = att_ref[...]
        za = za_ref[...].astype(F32)
        szr = _sigmoid(zr)
        silu_r = zr * szr
        yrnn_b = (hv * silu_r).astype(BF16)
        sza = _sigmoid(za)
        silu_a = za * sza
        yatt_b = (av * silu_a).astype(BF16)
        yrnn_ref[...] = yrnn_b
        yatt_ref[...] = yatt_b
        yr = _dot(yrnn_b, wor[...])
        ya = _dot_nt(yatt_b, woa[...])
        g0 = _sigmoid(jnp.concatenate([g0a_ref[...], g0b_ref[...]], axis=1).astype(F32))
        g1 = _sigmoid(jnp.concatenate([g1a_ref[...], g1b_ref[...]], axis=1).astype(F32))
        merged_b = (g0 * yr + g1 * ya).astype(BF16)
        merged_ref[...] = merged_b
        x1 = xv + _dot(merged_b, wout[...])
        rstd = lax.rsqrt(jnp.mean(x1 * x1, axis=-1, keepdims=True) + EPS)
        xh = x1 * rstd
        npl = np_ref[...]
        n1_b = (xh * npl).astype(BF16)
        n1_ref[...] = n1_b
        pg = _sigmoid(_dot(n1_b, wpg[...]) + bpg_ref[...])
        pe = _dot_nt(p_ref[...].astype(BF16), wple[...])
        err = x1 + pg * pe - tgt_ref[...]
        loss_t = 0.5 * inv_d * jnp.sum(err * err)
        dy = err * inv_d
        dpe_ref[...] = (dy * pg).astype(BF16)
        dpre = dy * pe * pg * (1.0 - pg)
        dpre_b = dpre.astype(BF16)
        dpre_ref[...] = dpre_b
        dn1 = _dot_nt(dpre_b, wpg[...])
        dnp = jnp.sum(dn1 * xh, axis=0, keepdims=True)
        dbpg = jnp.sum(dpre, axis=0, keepdims=True)
        gd = dn1 * npl
        dx1 = dy + rstd * (gd - xh * jnp.mean(gd * xh, axis=-1, keepdims=True))
        dx1_ref[...] = dx1
        dmerged = _dot_nt(dx1.astype(BF16), wout[...])
        dyr_b = (dmerged * g0).astype(BF16)
        dya_b = (dmerged * g1).astype(BF16)
        dyr_ref[...] = dyr_b
        dya_ref[...] = dya_b
        slab_c_ref[:, ATT:ATT + D] = (dmerged * yr * g0 * (1.0 - g0)).astype(BF16)
        slab_c_ref[:, ATT + D:ATT + 2 * D] = (dmerged * ya * g1 * (1.0 - g1)).astype(BF16)
        dyrnn = _dot_nt(dyr_b, wor[...])
        dyatt = _dot(dya_b, woa[...])
        dh_ref[...] = dyrnn * silu_r
        slab_a_ref[...] = (dyrnn * hv * szr * (1.0 + zr * (1.0 - szr))).astype(BF16)
        datt = dyatt * silu_a
        datt_ref[...] = datt
        slab_c_ref[:, 0:ATT] = (dyatt * av * sza * (1.0 + za * (1.0 - sza))).astype(BF16)
        da = datt * av
        for hh in range(NH):
            seg = slice(hh * HD, (hh + 1) * HD)
            sbar_ref[:, seg] = jnp.broadcast_to(jnp.sum(da[:, seg], axis=-1, keepdims=True), (tm, HD))

        @pl.when(first)
        def _():
            loss_ref[...] = jnp.full((8, LANES), loss_t, F32)
            dnp_ref[...] = dnp
            dbpg_ref[...] = dbpg

        @pl.when(jnp.logical_not(first))
        def _():
            loss_ref[...] += jnp.full((8, LANES), loss_t, F32)
            dnp_ref[...] += dnp
            dbpg_ref[...] += dbpg

    tok = lambda w: pl.BlockSpec((tm, w), lambda i: (i, 0))
    col = lambda w, blk: pl.BlockSpec((tm, w), lambda i: (i, blk))
    vec = lambda: pl.BlockSpec((1, D), lambda i: (0, 0))
    hbm = lambda: pl.BlockSpec(memory_space=pl.ANY)
    gb = OFF_G // 512
    in_specs = [tok(D), tok(DR), col(DR, 1), tok(ATT), col(ATT, OFF_ZA // ATT),
                col(512, gb), col(512, gb + 1), col(512, gb + 2), col(512, gb + 3),
                tok(PLE), tok(D), vec(), vec(), hbm(), hbm(), hbm(), hbm(), hbm()]
    sh = lambda w, dt: jax.ShapeDtypeStruct((T, w), dt)
    out_shape = [sh(D, F32), sh(D, BF16), sh(D, BF16), sh(D, BF16), sh(D, BF16), sh(D, BF16), sh(D, BF16),
                 sh(A_W, BF16), sh(C_W, BF16), sh(DR, F32), sh(ATT, F32), sh(ATT, F32),
                 sh(DR, BF16), sh(ATT, BF16),
                 jax.ShapeDtypeStruct((8, LANES), F32), jax.ShapeDtypeStruct((1, D), F32),
                 jax.ShapeDtypeStruct((1, D), F32)]
    out_specs = [tok(D), tok(D), tok(D), tok(D), tok(D), tok(D), tok(D), col(DR, 1), tok(C_W), tok(DR),
                 tok(ATT), tok(ATT), tok(DR), tok(ATT),
                 pl.BlockSpec((8, LANES), lambda i: (0, 0)), vec(), vec()]
    return pl.pallas_call(
        body, grid=(nt,), name="tail_fwd_bwd",
        in_specs=in_specs, out_specs=out_specs, out_shape=out_shape,
        scratch_shapes=[pltpu.VMEM((DR, D), BF16), pltpu.VMEM((D, ATT), BF16), pltpu.VMEM((D, D), BF16),
                        pltpu.VMEM((D, D), BF16), pltpu.VMEM((D, PLE), BF16)],
        compiler_params=_cp(("arbitrary",), VMEM_BIG),
    )(*_hbm(x, h, proj, att, proj, proj, proj, proj, proj, p, tgt, norm_ple, b_pg, w_o_rnn, w_o_att_t, w_out, w_pg,
            w_ple_t))


def _input_norm_bwd(x, dhn, dx1, gain, tm=512):
    def body(x_ref, dhn_ref, dx1_ref, g_ref, dx_ref, dg_ref):
        xv = x_ref[...]
        rstd = lax.rsqrt(jnp.mean(xv * xv, axis=-1, keepdims=True) + EPS)
        xh = xv * rstd
        dn = dhn_ref[...]
        dg = jnp.sum(dn * xh, axis=0, keepdims=True)
        gd = dn * g_ref[...]
        dx_ref[...] = dx1_ref[...] + rstd * (gd - xh * jnp.mean(gd * xh, axis=-1, keepdims=True))
        first = pl.program_id(0) == 0

        @pl.when(first)
        def _():
            dg_ref[...] = dg

        @pl.when(jnp.logical_not(first))
        def _():
            dg_ref[...] += dg

    tok = lambda: pl.BlockSpec((tm, D), lambda i: (i, 0))
    vec = lambda: pl.BlockSpec((1, D), lambda i: (0, 0))
    return pl.pallas_call(
        body, grid=(T // tm,), name="input_norm_bwd",
        in_specs=[tok(), tok(), tok(), vec()], out_specs=[tok(), vec()],
        out_shape=[jax.ShapeDtypeStruct((T, D), F32), jax.ShapeDtypeStruct((1, D), F32)],
        compiler_params=_cp(("arbitrary",), VMEM_MID),
    )(*_hbm(x, dhn, dx1, gain))


def _rope_tables():
    pos = jnp.arange(S, dtype=F32)
    inv_freq = ROPE_THETA ** (-jnp.arange(0, HD, 2, dtype=F32) / HD)
    ang = pos[:, None] * inv_freq[None, :]
    cos, sin = jnp.cos(ang), jnp.sin(ang)
    return jnp.concatenate([cos, cos], axis=1), jnp.concatenate([-sin, sin], axis=1)


def _local_step(x, p, tgt, project, other_weights, norm_mix, conv_b,
                w_rg_a, b_rg_a, w_rg_x, b_rg_x, lam, q_norm, k_norm, norm_ple, b_pg, start_reduce=None,
                entry_token=None):
    if start_reduce is None:
        start_reduce = lambda arrs, tag: (jnp.zeros((8, LANES), F32), arrs)
    if entry_token is None:
        entry_token = jnp.zeros((8, LANES), F32)
    cos_t, sin_t = _rope_tables()
    wa_b = w_rg_a.astype(BF16)
    wx_b = w_rg_x.astype(BF16)

    hn = _rmsnorm_fwd(x, norm_mix, entry_token)
    proj, w_bufs, chips, conv_w, token = project(hn)
    proj3 = proj.reshape(BL, S, NIN)
    h3 = _rnn_fwd(proj3, conv_w, conv_b, wa_b, b_rg_a, wx_b, b_rg_x, lam, token)
    att3, lse, wts = _attn_fwd(proj3, cos_t, sin_t, q_norm, k_norm)
    w_o_rnn, w_o_att_t, w_out, w_pg, w_ple_t = other_weights(att3)
    (dx1, merged, n1, dpre, dpe, dyr, dya, slab_a, slab_c, dh, datt, sbar, yrnn, yatt, loss8, dnp, dbpg) = _tail(
        x, proj, h3.reshape(T, DR), att3.reshape(T, ATT), p, tgt, w_o_rnn, w_o_att_t, w_out, w_pg, w_ple_t,
        norm_ple, b_pg)

    token, pending_out = start_reduce([
        _mm_tn(yrnn, dyr, 640, 2048, "dw_o_rnn"),
        _mm_tn(dya, yatt, 512, 2048, "dw_o_att_t"),
        _mm_tn(merged, dx1, 512, 2048, "dw_out"),
        _mm_tn(n1, dpre, 512, 2048, "dw_ple_gate"),
        _mm_tn(dpe, p, 512, 2048, "dw_ple_t")], "out")

    slab_a3, dcw, dcb, dwa, dba, dwx, dbx, dlam = _rnn_bwd(
        proj3, h3, dh.reshape(BL, S, DR), slab_a.reshape(BL, S, A_W), conv_w, conv_b, wa_b, b_rg_a, wx_b, b_rg_x, lam,
        token)
    datt3 = datt.reshape(BL, S, ATT)
    sbar3 = sbar.reshape(BL, S, ATT)
    slabs = None
    dqn = []
    dkn = []
    for g in range(NG):
        dq, dk, dv, dqn_g, dkn_g = _attn_bwd_group(g, proj3, cos_t, sin_t, q_norm[g:g + 1], k_norm[g:g + 1],
                                                   lse, wts, datt3, sbar3, slabs)
        slabs = (dq, dk, dv)
        dqn.append(dqn_g)
        dkn.append(dkn_g)
    pieces = [slab_a3.reshape(T, A_W)] + [t.reshape(T, GW) for t in slabs] + [slab_c]
    dw_in_t, db_in = _dw_in(pieces, hn)
    token, pending_in = start_reduce([dw_in_t], "in")
    grad_x, dnm = _grad_x(pieces, w_bufs, chips, token, x, dx1, norm_mix)

    small = dict(w_rg_a=dwa, w_rg_x=dwx, norm_mix=dnm, b_in=db_in, conv_b=dcb, b_rg_a=dba, b_rg_x=dbx,
                 lru_lambda=dlam, q_norm=dqn, k_norm=dkn, norm_ple=dnp, b_ple_gate=dbpg, conv_w=dcw, loss=loss8)
    return grad_x, pending_out, pending_in, small


MESH = pl.DeviceIdType.MESH
HBM_SPEC = pl.BlockSpec(memory_space=pl.ANY)


def _my_pos():
    return lax.axis_index("x"), lax.axis_index("y"), lax.axis_index("c")


def _flip(pos, k):
    x, y, c = pos
    return (1 - x if k & 4 else x, 1 - y if k & 2 else y, 1 - c if k & 1 else c)


def _lin(pos):
    return 4 * pos[0] + 2 * pos[1] + pos[2]


def _chip(pos):
    return 2 * pos[0] + pos[1]


def _all_gather_two_level(shards, name):
    na = len(shards)

    def body(*refs):
        x_refs = refs[:na]
        out_refs = refs[na:2 * na]
        send_sems, recv_sems, local_sems = refs[2 * na:]
        me = _my_pos()
        sibling = _flip(me, 1)
        chips = [_flip(me, 4), _flip(me, 2), _flip(me, 6)]

        def copy(i, k, block, to, from_x=False):
            dst = out_refs[i].at[_lin(block)]
            return pltpu.make_async_remote_copy(
                src_ref=x_refs[i] if from_x else dst, dst_ref=dst,
                send_sem=send_sems.at[7 * i + k], recv_sem=recv_sems.at[7 * i + k], device_id=to, device_id_type=MESH)

        started = []
        for i in range(na):
            mine = pltpu.make_async_copy(x_refs[i], out_refs[i].at[_lin(me)], local_sems.at[i])
            mine.start()
            started.append(mine)
        sends = []
        for i in range(na):
            cps = [copy(i, 0, me, sibling, True)] + [copy(i, 1 + j, me, chip, True) for j, chip in enumerate(chips)]
            for cp in cps:
                cp.start()
            sends += cps
        for i in range(na):
            for j, chip in enumerate(chips):
                copy(i, 1 + j, chip, me).wait_recv()
                fwd = copy(i, 4 + j, chip, sibling)
                fwd.start()
                sends.append(fwd)
        for i in range(na):
            copy(i, 0, sibling, me).wait_recv()
            for j, chip in enumerate(chips):
                copy(i, 4 + j, _flip(chip, 1), me).wait_recv()
        for cp in sends:
            cp.wait_send()
        for mine in started:
            mine.wait()

    return pl.pallas_call(
        body, name=name,
        out_shape=[jax.ShapeDtypeStruct((NDEV,) + s.shape, s.dtype) for s in shards],
        in_specs=[HBM_SPEC] * na, out_specs=[HBM_SPEC] * na,
        scratch_shapes=[pltpu.SemaphoreType.DMA((7 * na,)), pltpu.SemaphoreType.DMA((7 * na,)),
                        pltpu.SemaphoreType.DMA((na,))],
    )(*shards)


def _all_gather_direct(shard, name):
    def body(x_ref, out_ref, send_sems, recv_sems, local_sem):
        me = _my_pos()
        mine = pltpu.make_async_copy(x_ref, out_ref.at[_lin(me)], local_sem)
        mine.start()
        sends = []
        for k in range(1, NDEV):
            cp = pltpu.make_async_remote_copy(
                src_ref=x_ref, dst_ref=out_ref.at[_lin(me)], send_sem=send_sems.at[k - 1],
                recv_sem=recv_sems.at[k - 1], device_id=_flip(me, k), device_id_type=MESH)
            cp.start()
            sends.append(cp)
        for k in range(1, NDEV):
            peer = _flip(me, k)
            pltpu.make_async_remote_copy(
                src_ref=x_ref, dst_ref=out_ref.at[_lin(peer)], send_sem=send_sems.at[k - 1],
                recv_sem=recv_sems.at[k - 1], device_id=peer, device_id_type=MESH).wait_recv()
        for cp in sends:
            cp.wait_send()
        mine.wait()

    return pl.pallas_call(
        body, name=name,
        out_shape=jax.ShapeDtypeStruct((NDEV,) + shard.shape, shard.dtype),
        in_specs=[HBM_SPEC], out_specs=HBM_SPEC,
        scratch_shapes=[pltpu.SemaphoreType.DMA((7,)), pltpu.SemaphoreType.DMA((7,)), pltpu.SemaphoreType.DMA],
    )(shard)


def _exchange_within_chip(parts, name):
    na = len(parts)

    def body(*refs):
        a_refs = refs[:na]
        recv_refs = refs[na:2 * na]
        send_sems, recv_sems = refs[2 * na:]
        me = _my_pos()
        c = me[2]
        sibling = _flip(me, 1)
        remote = []
        for i in range(na):
            for q in range(NCHIP):
                rc = pltpu.make_async_remote_copy(
                    src_ref=a_refs[i].at[q, 1 - c], dst_ref=recv_refs[i].at[q],
                    send_sem=send_sems.at[NCHIP * i + q], recv_sem=recv_sems.at[NCHIP * i + q],
                    device_id=sibling, device_id_type=MESH)
                rc.start()
                remote.append(rc)
        for rc in remote:
            rc.wait_recv()
        for rc in remote:
            rc.wait_send()

    return pl.pallas_call(
        body, name=name, out_shape=[jax.ShapeDtypeStruct((NCHIP,) + a.shape[2:], a.dtype) for a in parts],
        in_specs=[HBM_SPEC] * na, out_specs=[HBM_SPEC] * na,
        scratch_shapes=[pltpu.SemaphoreType.DMA((NCHIP * na,)), pltpu.SemaphoreType.DMA((NCHIP * na,))],
    )(*parts)


HBM_ONLY = pl.BlockSpec(memory_space=pltpu.HBM)
SEM_SPEC = pl.BlockSpec(memory_space=pltpu.SEMAPHORE)
SPLIT_COPY = pltpu.CompilerParams(has_side_effects=pltpu.SideEffectType.DATAFLOW_SIDE_EFFECTING)


def _chip_peers(me):
    return [_flip(me, 4), _flip(me, 2), _flip(me, 6)]


def _between_chips_start(parts, name):
    na = len(parts)

    def body(*refs):
        a_refs = refs[:na]
        land_refs = refs[na:2 * na]
        send_sems, recv_sems = refs[2 * na], refs[2 * na + 1]
        token = refs[-1]
        me = _my_pos()
        myq = _chip(me)
        for i in range(na):
            for j, peer in enumerate(_chip_peers(me)):
                pltpu.make_async_remote_copy(
                    src_ref=a_refs[i].at[_chip(peer)], dst_ref=land_refs[i].at[myq],
                    send_sem=send_sems.at[3 * i + j], recv_sem=recv_sems.at[3 * i + j],
                    device_id=peer, device_id_type=MESH).start()
        token[...] = jnp.zeros_like(token)

    hbm = [pltpu.HBM(a.shape, a.dtype) for a in parts]
    srcs = [pltpu.with_memory_space_constraint(a, pltpu.HBM) for a in parts]
    lands = [pltpu.with_memory_space_constraint(lax.empty(a.shape, a.dtype), pltpu.HBM) for a in parts]
    res = pl.pallas_call(
        body, name=name,
        out_shape=(pltpu.SemaphoreType.DMA((3 * na,)), pltpu.SemaphoreType.DMA((3 * na,)), *hbm, *hbm,
                   jax.ShapeDtypeStruct((8, LANES), F32)),
        in_specs=[HBM_ONLY] * (2 * na),
        out_specs=(SEM_SPEC, SEM_SPEC, *([HBM_ONLY] * (2 * na)), pl.BlockSpec(memory_space=pltpu.VMEM)),
        input_output_aliases={i: 2 + i for i in range(2 * na)},
        compiler_params=SPLIT_COPY,
    )(*srcs, *lands)
    return res[-1], (res[0], res[1], list(res[2:2 + na]), list(res[2 + na:2 + 2 * na]))


def _between_chips_wait(pending, after, name):
    send_sems, recv_sems, parts, lands = pending
    na = len(parts)

    def body(*refs):
        a_refs = refs[:na]
        land_refs = refs[na:2 * na]
        send_sems, recv_sems = refs[2 * na], refs[2 * na + 1]
        me = _my_pos()
        for i in range(na):
            for j, peer in enumerate(_chip_peers(me)):
                cp = pltpu.make_async_remote_copy(
                    src_ref=a_refs[i].at[_chip(peer)], dst_ref=land_refs[i].at[_chip(peer)],
                    send_sem=send_sems.at[3 * i + j], recv_sem=recv_sems.at[3 * i + j],
                    device_id=peer, device_id_type=MESH)
                cp.wait_send()
                cp.wait_recv()

    hbm = [pltpu.HBM(a.shape, a.dtype) for a in parts]
    res = pl.pallas_call(
        body, name=name, out_shape=(*hbm, *hbm),
        in_specs=[HBM_ONLY] * (2 * na) + [SEM_SPEC, SEM_SPEC, pl.BlockSpec(memory_space=pl.ANY)],
        out_specs=[HBM_ONLY] * (2 * na),
        input_output_aliases={i: i for i in range(2 * na)},
        compiler_params=SPLIT_COPY,
    )(*parts, *lands, send_sems, recv_sems, after)
    return list(res[:na]), list(res[na:])


def _remote(src, dst, send_sems, recv_sems, idx, peer):
    return pltpu.make_async_remote_copy(src_ref=src, dst_ref=dst, send_sem=send_sems.at[idx],
                                        recv_sem=recv_sems.at[idx], device_id=peer, device_id_type=MESH)


def _copies_own(bufs, me):
    return [(bufs[0], bufs[1].at[me[2]], 0, _flip(me, 1))]


def _copies_near(bufs, me):
    return [(bufs[0], bufs[1].at[0, me[2]], 0, _flip(me, 2)), (bufs[0], bufs[1].at[1, me[2]], 1, _flip(me, 4))]


def _copies_far(bufs, me):
    return [(bufs[0], bufs[1].at[me[2]], 0, _flip(me, 6))]


def _copies_others(bufs, me):
    na = len(bufs) // 2
    return [(bufs[i], bufs[na + i].at[_lin(me)], 7 * i + k - 1, _flip(me, k))
            for i in range(na) for k in range(1, NDEV)]


def _copies_exchange(bufs, me):
    na = len(bufs) // 2
    return [(bufs[i].at[_lin(_flip(me, k))], bufs[na + i].at[_lin(me)], 7 * i + k - 1, _flip(me, k))
            for i in range(na) for k in range(1, NDEV)]


GROUP_COPIES = dict(own=_copies_own, near=_copies_near, far=_copies_far, others=_copies_others,
                    exchange=_copies_exchange)
GROUP_COUNT = dict(own=1, near=2, far=1)
TO_ALL = ("others", "exchange")


def _gather_start(bufs, groups, after, name):
    nb = len(bufs)
    ng = len(groups)

    def body(*refs):
        b = refs[:nb]
        sems = refs[nb + 1:nb + 1 + 2 * ng]
        token = refs[-1]
        me = _my_pos()
        for gi, (group, idx) in enumerate(groups):
            for src, dst, k, peer in GROUP_COPIES[group]([b[i] for i in idx], me):
                _remote(src, dst, sems[2 * gi], sems[2 * gi + 1], k, peer).start()
        token[...] = jnp.zeros_like(token)

    sem_t = []
    for group, idx in groups:
        cnt = 7 * (len(idx) // 2) if group in TO_ALL else GROUP_COUNT[group]
        sem_t += [pltpu.SemaphoreType.DMA((cnt,)), pltpu.SemaphoreType.DMA((cnt,))]
    ins = [pltpu.with_memory_space_constraint(a, pltpu.HBM) for a in bufs]
    res = pl.pallas_call(
        body, name=name,
        out_shape=(*sem_t, *[pltpu.HBM(a.shape, a.dtype) for a in bufs], jax.ShapeDtypeStruct((8, LANES), F32)),
        in_specs=[HBM_ONLY] * nb + [pl.BlockSpec(memory_space=pl.ANY)],
        out_specs=(*([SEM_SPEC] * (2 * ng)), *([HBM_ONLY] * nb), pl.BlockSpec(memory_space=pltpu.VMEM)),
        input_output_aliases={i: 2 * ng + i for i in range(nb)},
        compiler_params=SPLIT_COPY,
    )(*ins, after)
    return res[-1], list(res[2 * ng:2 * ng + nb]), [(res[2 * gi], res[2 * gi + 1]) for gi in range(ng)]


def _gather_wait(group, send_sems, recv_sems, bufs, after, name):
    nb = len(bufs)
    copies = GROUP_COPIES[group]

    def body(*refs):
        b = refs[:nb]
        ss, rs = refs[nb], refs[nb + 1]
        me = _my_pos()
        for src, dst, idx, peer in copies(b, me):
            if group in TO_ALL:
                landed = b[nb // 2 + idx // 7].at[_lin(peer)]
            elif group == "own":
                landed = b[1].at[1 - me[2]]
            else:
                landed = dst
            cp = _remote(src, landed, ss, rs, idx, peer)
            cp.wait_send()
            cp.wait_recv()

    res = pl.pallas_call(
        body, name=name, out_shape=[pltpu.HBM(a.shape, a.dtype) for a in bufs],
        in_specs=[HBM_ONLY] * nb + [SEM_SPEC, SEM_SPEC, pl.BlockSpec(memory_space=pl.ANY)],
        out_specs=[HBM_ONLY] * nb,
        input_output_aliases={i: i for i in range(nb)},
        compiler_params=SPLIT_COPY,
    )(*bufs, send_sems, recv_sems, after)
    return list(res)


def _forward_to_sibling(buf, name):
    n = buf.shape[0]

    def body(_in_ref, out_ref, send_sems, recv_sems):
        me = _my_pos()
        c = me[2]
        sibling = _flip(me, 1)
        sends = []
        for r in range(n):
            cp = _remote(out_ref.at[r, c], out_ref.at[r, c], send_sems, recv_sems, r, sibling)
            cp.start()
            sends.append(cp)
        for r in range(n):
            _remote(out_ref.at[r, c], out_ref.at[r, 1 - c], send_sems, recv_sems, r, sibling).wait_recv()
        for cp in sends:
            cp.wait_send()

    return pl.pallas_call(
        body, name=name, out_shape=jax.ShapeDtypeStruct(buf.shape, buf.dtype),
        in_specs=[HBM_SPEC], out_specs=HBM_SPEC,
        scratch_shapes=[pltpu.SemaphoreType.DMA((n,)), pltpu.SemaphoreType.DMA((n,))],
        input_output_aliases={0: 0},
    )(buf)


def _allreduce_small(rep, conv, name):
    def body(rep_hbm, conv_hbm, out_ref, conv_out, stage_r, stage_c, send_sems, recv_sems, local_sems):
        me = _my_pos()
        mi = _lin(me)
        peers = [_flip(me, k) for k in range(1, NDEV)]
        own = [pltpu.make_async_copy(rep_hbm.at[mi], stage_r.at[mi], local_sems.at[0]),
               pltpu.make_async_copy(conv_hbm.at[mi], stage_c.at[mi], local_sems.at[1])]
        for cp in own:
            cp.start()
        sends = []
        for j, peer in enumerate(peers):
            for src, stage, base in ((rep_hbm, stage_r, 0), (conv_hbm, stage_c, 7)):
                cp = _remote(src.at[_lin(peer)], stage.at[mi], send_sems, recv_sems, base + j, peer)
                cp.start()
                sends.append(cp)
        for j, peer in enumerate(peers):
            for src, stage, base in ((rep_hbm, stage_r, 0), (conv_hbm, stage_c, 7)):
                _remote(src.at[mi], stage.at[_lin(peer)], send_sems, recv_sems, base + j, peer).wait_recv()
        for cp in own:
            cp.wait()
        acc_r = stage_r[0]
        acc_c = stage_c[0]
        for q in range(1, NDEV):
            acc_r = acc_r + stage_r[q]
            acc_c = acc_c + stage_c[q]
        out_ref[mi] = acc_r
        conv_out[...] = acc_c
        for j, peer in enumerate(peers):
            cp = _remote(out_ref.at[mi], out_ref.at[mi], send_sems, recv_sems, 14 + j, peer)
            cp.start()
            sends.append(cp)
        for j, peer in enumerate(peers):
            _remote(out_ref.at[mi], out_ref.at[_lin(peer)], send_sems, recv_sems, 14 + j, peer).wait_recv()
        for cp in sends:
            cp.wait_send()

    vmem = pl.BlockSpec(memory_space=pltpu.VMEM)
    return pl.pallas_call(
        body, name=name,
        out_shape=[jax.ShapeDtypeStruct(rep.shape, F32), jax.ShapeDtypeStruct(conv.shape[1:], F32)],
        in_specs=[HBM_SPEC, HBM_SPEC], out_specs=[vmem, vmem],
        scratch_shapes=[pltpu.VMEM(rep.shape, F32), pltpu.VMEM(conv.shape, F32),
                        pltpu.SemaphoreType.DMA((21,)), pltpu.SemaphoreType.DMA((21,)), pltpu.SemaphoreType.DMA((2,))],
    )(rep, conv)


def _exchange_all(parts, name):
    na = len(parts)

    def body(*refs):
        a_refs = refs[:na]
        out_refs = refs[na:2 * na]
        send_sems, recv_sems = refs[2 * na:]
        me = _my_pos()
        sends = []
        for i in range(na):
            for k in range(1, NDEV):
                peer = _flip(me, k)
                cp = pltpu.make_async_remote_copy(
                    src_ref=a_refs[i].at[_lin(peer)], dst_ref=out_refs[i].at[_lin(me)],
                    send_sem=send_sems.at[7 * i + k - 1], recv_sem=recv_sems.at[7 * i + k - 1],
                    device_id=peer, device_id_type=MESH)
                cp.start()
                sends.append(cp)
        for i in range(na):
            for k in range(1, NDEV):
                peer = _flip(me, k)
                pltpu.make_async_remote_copy(
                    src_ref=a_refs[i].at[_lin(peer)], dst_ref=out_refs[i].at[_lin(peer)],
                    send_sem=send_sems.at[7 * i + k - 1], recv_sem=recv_sems.at[7 * i + k - 1],
                    device_id=peer, device_id_type=MESH).wait_recv()
        for cp in sends:
            cp.wait_send()

    return pl.pallas_call(
        body, name=name, out_shape=[jax.ShapeDtypeStruct(a.shape, a.dtype) for a in parts],
        in_specs=[HBM_SPEC] * na, out_specs=[HBM_SPEC] * na,
        scratch_shapes=[pltpu.SemaphoreType.DMA((7 * na,)), pltpu.SemaphoreType.DMA((7 * na,))],
    )(*parts)


def _scalar(v):
    return jnp.asarray(v, jnp.int32).reshape(1)


def _sum_pairs(parts, theirs, name):
    na = len(parts)

    def body(c_ref, *refs):
        for i in range(na):
            o_ref = refs[2 * na + i]
            o_ref[0] = (refs[i][0, 0].astype(F32) + refs[na + i][0].astype(F32)).astype(o_ref.dtype)

    def mine_spec(a):
        return pl.BlockSpec((1, 1) + a.shape[2:], lambda q, c_ref: (q, c_ref[0], 0, 0))

    def spec(a):
        return pl.BlockSpec((1,) + a.shape[1:], lambda q, c_ref: (q, 0, 0))

    return pl.pallas_call(
        body, name=name,
        grid_spec=pltpu.PrefetchScalarGridSpec(
            num_scalar_prefetch=1, grid=(NCHIP,),
            in_specs=[mine_spec(a) for a in parts] + [spec(a) for a in theirs],
            out_specs=[spec(a) for a in theirs]),
        out_shape=[jax.ShapeDtypeStruct(a.shape, a.dtype) for a in theirs],
        compiler_params=_cp(("arbitrary",), VMEM_BIG),
    )(_scalar(lax.axis_index("c")), *parts, *theirs)


def _others(q, mine, nblk=NCHIP):
    return jnp.where(q == mine, (q + 1) % nblk, q)


def _sum_chips_adamw(own, recv, wv, mv, vv, token, tr, name):
    _, r, w = recv.shape

    def body(q_ref, own_ref, r0, r1, r2, r3, w_ref, m_ref, v_ref, _token, g_ref, d_ref, m2_ref, v2_ref):
        myq = q_ref[0]
        acc = None
        for q, r_ref in enumerate((r0, r1, r2, r3)):
            term = jnp.where(myq == q, own_ref[0], r_ref[0]).astype(F32)
            acc = term if acc is None else acc + term
        g_ref[...] = acc
        delta, m2, v2 = _adam_math(w_ref[...], acc, m_ref[...], v_ref[...])
        d_ref[...] = delta
        m2_ref[...] = m2
        v2_ref[...] = v2

    def recv_spec(q):
        return pl.BlockSpec((1, tr, w), lambda i, q_ref: (_others(q, q_ref[0]), i, 0))

    rows = lambda: pl.BlockSpec((tr, w), lambda i, q_ref: (i, 0))
    shp = jax.ShapeDtypeStruct((r, w), F32)
    return pl.pallas_call(
        body, name=name,
        grid_spec=pltpu.PrefetchScalarGridSpec(
            num_scalar_prefetch=1, grid=(r // tr,),
            in_specs=[pl.BlockSpec((1, tr, w), lambda i, q_ref: (q_ref[0], i, 0))]
            + [recv_spec(q) for q in range(NCHIP)] + [rows(), rows(), rows()]
            + [pl.BlockSpec((8, LANES), lambda i, q_ref: (0, 0))],
            out_specs=[rows(), rows(), rows(), rows()]),
        out_shape=[shp, shp, shp, shp],
        compiler_params=_cp(("arbitrary",), VMEM_MID),
    )(_scalar(_chip(_my_pos())), *_hbm(own, recv, recv, recv, recv, wv, mv, vv, token))


def _sum_blocks_small(own, recv, mine, transpose, name):
    na = len(recv)
    nblk = recv[0].shape[0]

    def body(q_ref, *refs):
        me = q_ref[0]
        for i in range(na):
            acc = None
            for q in range(nblk):
                term = jnp.where(me == q, refs[i][0], refs[na * (1 + q) + i][0]).astype(F32)
                acc = term if acc is None else acc + term
            refs[na * (1 + nblk) + i][...] = acc.T if transpose[i] else acc

    def oshape(a, tr):
        r, w = a.shape[1:]
        return (w, r) if tr else (r, w)

    own_spec = lambda a: pl.BlockSpec((1,) + a.shape[1:], lambda s, q_ref: (q_ref[0], 0, 0))
    recv_spec = lambda a, q: pl.BlockSpec((1,) + a.shape[1:], lambda s, q_ref: (_others(q, q_ref[0], nblk), 0, 0))
    out_spec = lambda shp: pl.BlockSpec(shp, lambda s, q_ref: (0, 0))
    in_specs = [own_spec(a) for a in own]
    for q in range(nblk):
        in_specs += [recv_spec(a, q) for a in recv]
    return pl.pallas_call(
        body, name=name,
        grid_spec=pltpu.PrefetchScalarGridSpec(
            num_scalar_prefetch=1, grid=(1,), in_specs=in_specs,
            out_specs=[out_spec(oshape(a, tr)) for a, tr in zip(recv, transpose)]),
        out_shape=[jax.ShapeDtypeStruct(oshape(a, tr), F32) for a, tr in zip(recv, transpose)],
        compiler_params=_cp(("arbitrary",), VMEM_MID),
    )(_scalar(mine), *own, *(list(recv) * nblk))


def _rep_offsets():
    offs = []
    o = 0
    for r in REP_ROWS:
        offs.append(o)
        o += r
    return offs


LOSS_ROW = REP_TOTAL_ROWS


def _pack_small_grads(g):
    offs = _rep_offsets()

    def body(dwa, dwx, dnm, dbin, dcb, dba, dbx, dlam, dq0, dq1, dq2, dk0, dk1, dk2, dnp, dbpg, loss, o_ref):
        o_ref[pl.ds(REP_TOTAL_ROWS - 2, NDEV * REP_ROWS_DEV - REP_TOTAL_ROWS + 2), :] = jnp.zeros(
            (NDEV * REP_ROWS_DEV - REP_TOTAL_ROWS + 2, LANES), F32)
        o_ref[pl.ds(LOSS_ROW, 1), :] = loss[0:1, :]
        for n in range(NRB):
            o_ref[pl.ds(offs[0] + n * RBW, RBW), :] = dwa[n]
            o_ref[pl.ds(offs[1] + n * RBW, RBW), :] = dwx[n]

        def put_vec(off, ref, rows):
            for k in range(rows):
                o_ref[pl.ds(off + k, 1), :] = ref[:, k * LANES:(k + 1) * LANES]

        put_vec(offs[2], dnm, REP_ROWS[2])
        put_vec(offs[3], dbin, REP_ROWS[3])
        put_vec(offs[4], dcb, REP_ROWS[4])
        put_vec(offs[5], dba, REP_ROWS[5])
        put_vec(offs[6], dbx, REP_ROWS[6])
        put_vec(offs[7], dlam, REP_ROWS[7])
        for k, ref in enumerate((dq0, dq1, dq2)):
            o_ref[pl.ds(offs[8] + k, 1), :] = ref[...]
        for k, ref in enumerate((dk0, dk1, dk2)):
            o_ref[pl.ds(offs[9] + k, 1), :] = ref[...]
        put_vec(offs[10], dnp, REP_ROWS[10])
        put_vec(offs[11], dbpg, REP_ROWS[11])

    args = [g["w_rg_a"], g["w_rg_x"], g["norm_mix"], g["b_in"], g["conv_b"], g["b_rg_a"], g["b_rg_x"],
            g["lru_lambda"], *g["q_norm"], *g["k_norm"], g["norm_ple"], g["b_ple_gate"], g["loss"]]
    full = lambda shp: pl.BlockSpec(shp, lambda: (0,) * len(shp))
    return pl.pallas_call(
        body, name="pack_small_grads",
        in_specs=[full(a.shape) for a in args],
        out_specs=full((NDEV * REP_ROWS_DEV, LANES)),
        out_shape=jax.ShapeDtypeStruct((NDEV * REP_ROWS_DEV, LANES), F32),
    )(*_hbm(*args))


def _adam_math(wv, gv, mv, vv):
    c1 = 1.0 - B1 ** STEP
    c2 = 1.0 - B2 ** STEP
    m2 = B1 * mv + (1.0 - B1) * gv
    v2 = B2 * vv + (1.0 - B2) * (gv * gv)
    delta = (-LR) * ((m2 / c1) / (jnp.sqrt(v2 / c2) + AEPS) + WD * wv)
    return delta, m2, v2


def _adamw_small(rep_flat, w, m, v):
    offs = _rep_offsets()
    n = len(REP_NAMES)

    def body(*refs):
        g_ref = refs[0]
        w_refs = refs[1:1 + n]
        m_refs = refs[1 + n:1 + 2 * n]
        v_refs = refs[1 + 2 * n:1 + 3 * n]
        outs = refs[1 + 3 * n:]
        go, do, mo, vo = outs[:n], outs[n:2 * n], outs[2 * n:3 * n], outs[3 * n:]

        def emit(i, idx, gv):
            go[i][idx] = gv
            delta, m2, v2 = _adam_math(w_refs[i][idx], gv, m_refs[i][idx], v_refs[i][idx])
            do[i][idx] = delta
            mo[i][idx] = m2
            vo[i][idx] = v2

        for i in range(n):
            if i < 2:
                for b in range(NRB):
                    emit(i, b, g_ref[pl.ds(offs[i] + b * RBW, RBW), :])
            elif REP_NAMES[i] in ("q_norm", "k_norm"):
                emit(i, slice(None), g_ref[pl.ds(offs[i], NG), :])
            else:
                gv = jnp.concatenate([g_ref[pl.ds(offs[i] + k, 1), :] for k in range(REP_ROWS[i])], axis=1)
                emit(i, slice(None), gv)

    full = lambda shp: pl.BlockSpec(shp, lambda: (0,) * len(shp))
    pspecs = [full(a.shape) for a in w]
    pshapes = [jax.ShapeDtypeStruct(a.shape, F32) for a in w]
    res = pl.pallas_call(
        body, name="adamw_small",
        in_specs=[full(rep_flat.shape)] + pspecs * 3,
        out_specs=pspecs * 4, out_shape=pshapes * 4,
        compiler_params=_cp(None, VMEM_MID),
    )(*_hbm(rep_flat, *w, *m, *v))
    return res[:n], res[n:2 * n], res[2 * n:3 * n], res[3 * n:]


def _adamw_many(w, g, m, v, token):
    n = len(w)

    def body(*refs):
        for i in range(n):
            delta, m2, v2 = _adam_math(refs[i][...], refs[n + i][...], refs[2 * n + i][...], refs[3 * n + i][...])
            refs[4 * n + 1 + i][...] = delta
            refs[5 * n + 1 + i][...] = m2
            refs[6 * n + 1 + i][...] = v2

    full = lambda shp: pl.BlockSpec(shp, lambda: (0,) * len(shp))
    specs = [full(a.shape) for a in w]
    shapes = [jax.ShapeDtypeStruct(a.shape, F32) for a in w]
    res = pl.pallas_call(
        body, name="adamw_shards",
        in_specs=specs * 4 + [full((8, LANES))], out_specs=specs * 3, out_shape=shapes * 3,
        compiler_params=_cp(None, VMEM_MID),
    )(*_hbm(*w, *g, *m, *v, token))
    return res[:n], res[n:2 * n], res[2 * n:]


def kernel(x, p, norm_mix, w_in, b_in, conv_w, conv_b, w_rg_a, b_rg_a, w_rg_x, b_rg_x, lru_lambda, q_norm, k_norm, w_o_rnn, w_o_att, w_out, norm_ple, w_ple_gate, b_ple_gate, w_ple, loss_target, m_norm_mix, m_w_in, m_b_in, m_conv_w, m_conv_b, m_w_rg_a, m_b_rg_a, m_w_rg_x, m_b_rg_x, m_lru_lambda, m_q_norm, m_k_norm, m_w_o_rnn, m_w_o_att, m_w_out, m_norm_ple, m_w_ple_gate, m_b_ple_gate, m_w_ple, v_norm_mix, v_w_in, v_b_in, v_conv_w, v_conv_b, v_w_rg_a, v_b_rg_a, v_w_rg_x, v_b_rg_x, v_lru_lambda, v_q_norm, v_k_norm, v_w_o_rnn, v_w_o_att, v_w_out, v_norm_ple, v_w_ple_gate, v_b_ple_gate, v_w_ple):
    w = dict(norm_mix=norm_mix, w_in=w_in, b_in=b_in, conv_w=conv_w, conv_b=conv_b, w_rg_a=w_rg_a, b_rg_a=b_rg_a,
             w_rg_x=w_rg_x, b_rg_x=b_rg_x, lru_lambda=lru_lambda, q_norm=q_norm, k_norm=k_norm, w_o_rnn=w_o_rnn,
             w_o_att=w_o_att, w_out=w_out, norm_ple=norm_ple, w_ple_gate=w_ple_gate, b_ple_gate=b_ple_gate,
             w_ple=w_ple)
    m = dict(norm_mix=m_norm_mix, w_in=m_w_in, b_in=m_b_in, conv_w=m_conv_w, conv_b=m_conv_b, w_rg_a=m_w_rg_a,
             b_rg_a=m_b_rg_a, w_rg_x=m_w_rg_x, b_rg_x=m_b_rg_x, lru_lambda=m_lru_lambda, q_norm=m_q_norm,
             k_norm=m_k_norm, w_o_rnn=m_w_o_rnn, w_o_att=m_w_o_att, w_out=m_w_out, norm_ple=m_norm_ple,
             w_ple_gate=m_w_ple_gate, b_ple_gate=m_b_ple_gate, w_ple=m_w_ple)
    v = dict(norm_mix=v_norm_mix, w_in=v_w_in, b_in=v_b_in, conv_w=v_conv_w, conv_b=v_conv_b, w_rg_a=v_w_rg_a,
             b_rg_a=v_b_rg_a, w_rg_x=v_w_rg_x, b_rg_x=v_b_rg_x, lru_lambda=v_lru_lambda, q_norm=v_q_norm,
             k_norm=v_k_norm, w_o_rnn=v_w_o_rnn, w_o_att=v_w_o_att, w_out=v_w_out, norm_ple=v_norm_ple,
             w_ple_gate=v_w_ple_gate, b_ple_gate=v_b_ple_gate, w_ple=v_w_ple)
    names = list(w.keys())

    shards = [w_in[0].T.astype(BF16), w_o_rnn[0].astype(BF16), w_o_att[0].T.astype(BF16), w_out[0].astype(BF16),
              w_ple_gate[0].astype(BF16), w_ple[0].T.astype(BF16), conv_w[0]]
    pos = _my_pos()
    me, my_core, my_chip = _lin(pos), pos[2], _chip(pos)
    hbm_empty = lambda shp, dt: lax.empty(shp, dt)
    w_shard, conv_shard = shards[0], shards[6]
    shp = w_shard.shape
    entry_token, bufs, sems = _gather_start(
        [w_shard, hbm_empty((2,) + shp, BF16), hbm_empty((2, 2) + shp, BF16), conv_shard,
         hbm_empty((NDEV,) + conv_shard.shape, F32)],
        [("own", (0, 1)), ("near", (0, 2)), ("others", (3, 4))], norm_mix, "gather_start_near")
    w_src, own_l, near_l, conv_src, conv_l = bufs
    sem_own, sem_near, sem_conv = sems
    gather_out = {}

    def project(hn):
        w_thru, own = _gather_wait("own", *sem_own, [w_src, own_l], hn, "gather_wait_own")
        own = lax.dynamic_update_slice(own, w_shard[None], (my_core, 0, 0)).reshape(1, CHIP_COLS, D)
        chips = [jnp.stack([my_chip]), jnp.stack([my_chip ^ 1, my_chip ^ 2]), jnp.stack([my_chip ^ 3])]
        chips = [c.astype(jnp.int32) for c in chips]
        proj = _in_proj_chips(hn, own, b_in, chips[0], None, entry_token, "in_proj_own")
        w_thru, near = _gather_wait("near", *sem_near, [w_thru, near_l], proj, "gather_wait_near")
        near = _forward_to_sibling(near, "gather_forward_near")
        token, (w_thru, far_l), (sem_far,) = _gather_start(
            [w_thru, hbm_empty((2,) + shp, BF16)], [("far", (0, 1))], near, "gather_start_far")
        near = near.reshape(2, CHIP_COLS, D)
        proj = _in_proj_chips(hn, near, b_in, chips[1], proj, token, "in_proj_near")
        w_thru, far = _gather_wait("far", *sem_far, [w_thru, far_l], proj, "gather_wait_far")
        far = _forward_to_sibling(far[None], "gather_forward_far").reshape(1, CHIP_COLS, D)
        proj = _in_proj_chips(hn, far, b_in, chips[2], proj, token, "in_proj_far")
        conv_thru, conv_g = _gather_wait("others", *sem_conv, [conv_src, conv_l], proj, "gather_wait_conv")
        conv_g = lax.dynamic_update_slice(conv_g, conv_shard[None], (me, 0, 0))
        conv_f = conv_g.transpose(1, 0, 2).reshape(CONVW, DR)
        srcs = list(shards[1:6])
        token, obufs, (sem_out,) = _gather_start(
            srcs + [hbm_empty((NDEV,) + a.shape, BF16) for a in srcs], [("others", tuple(range(10)))], proj,
            "gather_start_out")
        gather_out.update(bufs=obufs, sems=sem_out)
        return proj, [own, near, far], jnp.concatenate(chips), conv_f, token

    def other_weights(after):
        obufs = _gather_wait("others", *gather_out["sems"], gather_out["bufs"], after, "gather_wait_out")
        full = [lax.dynamic_update_slice(a, s[None], (me, 0, 0)) for a, s in zip(obufs[5:], shards[1:6])]
        return [a.reshape((NDEV * a.shape[1], a.shape[2])) for a in full]

    def start_reduce(arrs, tag):
        parts = [a.reshape((NCHIP, 2, a.shape[0] // NDEV, a.shape[1])) for a in arrs]
        theirs = _exchange_within_chip(parts, "reduce_within_chip_" + tag)
        return _between_chips_start(_sum_pairs(parts, theirs, "sum_pairs_" + tag), "reduce_between_chips_start_" + tag)

    grad_x, pending_out, pending_in, small = _local_step(
        x.reshape(T, D), p.reshape(T, PLE), loss_target.reshape(T, D),
        project, other_weights,
        norm_mix, conv_b, w_rg_a[0], b_rg_a, w_rg_x[0], b_rg_x, lru_lambda, q_norm[0], k_norm[0],
        norm_ple, b_ple_gate, start_reduce, entry_token)

    rep_parts = _pack_small_grads(small).reshape(NDEV, REP_ROWS_DEV, LANES)
    conv_parts = small["conv_w"].reshape(CONVW, NDEV, DR // NDEV).transpose(1, 0, 2)
    smalls = [rep_parts, conv_parts]
    token, sbufs, (sem_x,) = _gather_start(
        smalls + [lax.empty(a.shape, F32) for a in smalls], [("exchange", (0, 1, 2, 3))], small["norm_mix"],
        "reduce_small_start")

    myq = _chip(_my_pos())
    own_in, recv_in = _between_chips_wait(pending_in, token, "reduce_between_chips_wait_in")
    w_in_res = _sum_chips_adamw(own_in[0], recv_in[0], w_in[0].T, m_w_in[0].T, v_w_in[0].T, token, 304, "adamw_w_in")
    sbufs = _gather_wait("exchange", *sem_x, sbufs, w_in_res[0], "reduce_small_wait")
    g_rep, g_conv = _sum_blocks_small(sbufs[:2], sbufs[2:], me, (False, False), "sum_small")
    token, gbufs, (sem_g,) = _gather_start(
        [g_rep, lax.empty((NDEV,) + g_rep.shape, F32)], [("others", (0, 1))], g_conv, "gather_small_start")
    own_out, recv_out = _between_chips_wait(pending_out, token, "reduce_between_chips_wait_out")
    g_o_rnn, g_o_att, g_out, g_pg, g_ple = _sum_blocks_small(
        own_out, recv_out, myq, (False, True, False, False, True), "sum_chips_out")

    grad, delta, new_m, new_v = {}, {}, {}, {}
    rest = ("w_o_rnn", "w_o_att", "w_out", "w_ple_gate", "w_ple", "conv_w")
    g_rest = [g_o_rnn, g_o_att, g_out, g_pg, g_ple, g_conv]
    rest_res = _adamw_many([w[n][0] for n in rest], g_rest, [m[n][0] for n in rest], [v[n][0] for n in rest], token)
    _, rep_all = _gather_wait("others", *sem_g, gbufs, rest_res[0][0], "gather_small_wait")
    rep_all = lax.dynamic_update_slice(rep_all, g_rep[None], (me, 0, 0)).reshape(NDEV * REP_ROWS_DEV, LANES)
    loss = rep_all[LOSS_ROW, 0]
    rep_shape = lambda a: a if a.ndim == 2 else a.reshape(a.shape[1:])
    res = _adamw_small(rep_all, [rep_shape(w[n]) for n in REP_NAMES], [rep_shape(m[n]) for n in REP_NAMES],
                       [rep_shape(v[n]) for n in REP_NAMES])
    for dst, vals in zip((grad, delta, new_m, new_v), res):
        for n, a in zip(REP_NAMES, vals):
            dst[n] = a.reshape(w[n].shape)
    grad["w_in"], delta["w_in"], new_m["w_in"], new_v["w_in"] = [a.T[None] for a in w_in_res]
    for n, a in zip(rest, g_rest):
        grad[n] = a[None]
    for dst, vals in zip((delta, new_m, new_v), rest_res):
        for n, a in zip(rest, vals):
            dst[n] = a[None]

    return (loss, grad_x.reshape(BL, S, D), *[grad[n] for n in names], *[delta[n] for n in names],
            *[new_m[n] for n in names], *[new_v[n] for n in names])
```

```python
import jax
import jax.numpy as jnp
from jax import lax
from jax.experimental import pallas as pl
from jax.experimental.pallas import tpu as pltpu

F32 = jnp.float32
BF16 = jnp.bfloat16

D = 1024
S = 2048
BL = 2
T = BL * S
NDEV = 8
NCHIP = 4
PLE = 256
DR = 1280
NRB = 10
RBW = 128
CONVW = 4
LRU_C = 8.0
HD = 128
NH = 4
PATTERNS = ((128, 1), (512, 4), (2048, 16))
NG = 3
ATT = NH * HD
GW = NG * ATT
NIN = 2 * DR + 3 * GW + ATT + 2 * D
OFF_ZR = DR
OFF_Q = 2 * DR
OFF_ZA = OFF_Q + 3 * GW
OFF_G = OFF_ZA + ATT
ROPE_THETA = 10000.0
EPS = 1e-6
SCALE = HD ** -0.5
NEG = -1e30
QB = 128
LANES = 128
CT = 512
NCT = NIN // CT
A_W = 2 * DR
C_W = ATT + 2 * D

LR, B1, B2, AEPS, WD, STEP = 0.001, 0.9, 0.999, 1e-08, 0.01, 10

NSHARD_IN = NIN // NDEV
REP_NAMES = ("w_rg_a", "w_rg_x", "norm_mix", "b_in", "conv_b", "b_rg_a", "b_rg_x", "lru_lambda", "q_norm",
             "k_norm", "norm_ple", "b_ple_gate")
REP_ROWS = (NRB * RBW, NRB * RBW, D // LANES, NIN // LANES, DR // LANES, DR // LANES, DR // LANES, DR // LANES,
            NG, NG, D // LANES, D // LANES)
REP_TOTAL_ROWS = sum(REP_ROWS)
REP_ROWS_DEV = 344
BIG_NAMES = ("w_in", "w_o_rnn", "w_o_att", "w_out", "w_ple_gate", "w_ple")

VMEM_BIG = 56 * 1024 * 1024
VMEM_MID = 40 * 1024 * 1024


def _cp(sem=None, vmem=None):
    return pltpu.CompilerParams(dimension_semantics=sem, vmem_limit_bytes=vmem)


def _hbm(*arrays):
    return [pltpu.with_memory_space_constraint(a, pltpu.HBM) for a in arrays]


def _dot(a, b):
    return jnp.dot(a, b, preferred_element_type=F32)


def _dot_nt(a, b):
    return lax.dot_general(a, b, (((1,), (1,)), ((), ())), preferred_element_type=F32)


def _dot_tn(a, b):
    return lax.dot_general(a, b, (((0,), (0,)), ((), ())), preferred_element_type=F32)


def _sigmoid(x):
    return jax.nn.sigmoid(x)


def _perm(j):
    jq = j - OFF_Q // CT
    inside = (j >= OFF_Q // CT) & (j < OFF_ZA // CT)
    return jnp.where(inside, OFF_Q // CT + (jq % 3) * 3 + jq // 3, j)


PIECES = ((0, A_W // CT), (OFF_Q // CT, GW // CT), (OFF_Q // CT + 3, GW // CT), (OFF_Q // CT + 6, GW // CT),
          (OFF_ZA // CT, C_W // CT))


def _rmsnorm_fwd(x, gain, token, tm=512):
    def body(x_ref, g_ref, _token, o_ref):
        xv = x_ref[...]
        var = jnp.mean(xv * xv, axis=-1, keepdims=True)
        o_ref[...] = (xv * lax.rsqrt(var + EPS) * g_ref[...]).astype(BF16)

    return pl.pallas_call(
        body, grid=(T // tm,), name="rmsnorm_fwd",
        in_specs=[pl.BlockSpec((tm, D), lambda i: (i, 0)), pl.BlockSpec((1, D), lambda i: (0, 0)),
                  pl.BlockSpec((8, LANES), lambda i: (0, 0))],
        out_specs=pl.BlockSpec((tm, D), lambda i: (i, 0)),
        out_shape=jax.ShapeDtypeStruct((T, D), BF16),
        compiler_params=_cp(("parallel",)),
    )(*_hbm(x, gain, token))


CHIP_COLS = NIN // NCHIP


def _in_proj_chips(hn, w_rows, bias, chips, proj, token, name, tm=1024):
    n = w_rows.shape[0]

    def body(chips_ref, a_ref, w_ref, b_ref, _token, *rest):
        o_ref = rest[-1]
        o_ref[...] = (_dot_nt(a_ref[...], w_ref[0]) + b_ref[...]).astype(BF16)

    in_specs = [pl.BlockSpec((tm, D), lambda s, i, ch: (i, 0)),
                pl.BlockSpec((1, CHIP_COLS, D), lambda s, i, ch: (s, 0, 0)),
                pl.BlockSpec((1, CHIP_COLS), lambda s, i, ch: (0, ch[s])),
                pl.BlockSpec((8, LANES), lambda s, i, ch: (0, 0))]
    args = [hn, w_rows, bias, token]
    aliases = {}
    if proj is not None:
        in_specs.append(pl.BlockSpec(memory_space=pl.ANY))
        args.append(proj)
        aliases = {5: 0}
    return pl.pallas_call(
        body, name=name,
        grid_spec=pltpu.PrefetchScalarGridSpec(
            num_scalar_prefetch=1, grid=(n, T // tm), in_specs=in_specs,
            out_specs=pl.BlockSpec((tm, CHIP_COLS), lambda s, i, ch: (i, ch[s]))),
        out_shape=jax.ShapeDtypeStruct((T, NIN), BF16),
        input_output_aliases=aliases,
        compiler_params=_cp(("arbitrary", "arbitrary"), VMEM_BIG),
    )(chips, *_hbm(*args))


def _grad_x(pieces, w_bufs, chips, token, x, dx1, gain, tm=512):
    nb = len(w_bufs)

    def body(chips_ref, a_ref, q_ref, k_ref, v_ref, c_ref, *rest):
        w_hbm = rest[:nb]
        x_ref, dx1_ref, g_ref, dx_ref, dg_ref, w = rest[nb + 1:]
        first = pl.program_id(0) == 0

        @pl.when(first)
        def _():
            s = 0
            for buf in w_hbm:
                for r in range(buf.shape[0]):
                    row = pl.multiple_of(chips_ref[s] * CHIP_COLS, 128)
                    pltpu.sync_copy(buf.at[r], w.at[pl.ds(row, CHIP_COLS), :])
                    s += 1

        acc = _dot(a_ref[...], w[pl.ds(0, A_W), :])
        for kind, p_ref in enumerate((q_ref, k_ref, v_ref)):
            for g in range(NG):
                row = OFF_Q + (3 * g + kind) * CT
                acc = acc + _dot(p_ref[:, g * CT:(g + 1) * CT], w[pl.ds(row, CT), :])
        dn = acc + _dot(c_ref[...], w[pl.ds(OFF_ZA, C_W), :])
        xv = x_ref[...]
        rstd = lax.rsqrt(jnp.mean(xv * xv, axis=-1, keepdims=True) + EPS)
        xh = xv * rstd
        dg = jnp.sum(dn * xh, axis=0, keepdims=True)
        gd = dn * g_ref[...]
        dx_ref[...] = dx1_ref[...] + rstd * (gd - xh * jnp.mean(gd * xh, axis=-1, keepdims=True))

        @pl.when(first)
        def _():
            dg_ref[...] = dg

        @pl.when(jnp.logical_not(first))
        def _():
            dg_ref[...] += dg

    tok = lambda wd: pl.BlockSpec((tm, wd), lambda i, ch: (i, 0))
    vec = lambda: pl.BlockSpec((1, D), lambda i, ch: (0, 0))
    return pl.pallas_call(
        body, name="grad_x",
        grid_spec=pltpu.PrefetchScalarGridSpec(
            num_scalar_prefetch=1, grid=(T // tm,),
            in_specs=[tok(A_W), tok(GW), tok(GW), tok(GW), tok(C_W)] + [pl.BlockSpec(memory_space=pl.ANY)] * nb
            + [pl.BlockSpec((8, LANES), lambda i, ch: (0, 0)), tok(D), tok(D), vec()],
            out_specs=[tok(D), vec()],
            scratch_shapes=[pltpu.VMEM((NIN, D), BF16)]),
        out_shape=[jax.ShapeDtypeStruct((T, D), F32), jax.ShapeDtypeStruct((1, D), F32)],
        compiler_params=_cp(("arbitrary",), VMEM_BIG),
    )(chips, *_hbm(*pieces, *w_bufs, token, x, dx1, gain))


def _dw_in(pieces, hn):
    def body(a_ref, q_ref, k_ref, v_ref, c_ref, h_hbm, o_ref, s_ref, h):
        j = pl.program_id(0)

        @pl.when(j == 0)
        def _():
            pltpu.sync_copy(h_hbm, h)

        def step(x_ref):
            xv = x_ref[...]
            o_ref[...] = _dot_tn(xv, h[...]).astype(BF16)
            s_ref[...] = jnp.sum(xv.astype(F32), axis=0, keepdims=True)

        for x_ref, (lo, n) in zip((a_ref, q_ref, k_ref, v_ref, c_ref), PIECES):
            pl.when((j >= lo) & (j < lo + n))(lambda x_ref=x_ref: step(x_ref))

    def piece_spec(lo, n):
        return pl.BlockSpec((T, CT), lambda j: (0, jnp.clip(j - lo, 0, n - 1)))

    return pl.pallas_call(
        body, grid=(NCT,), name="dw_in",
        in_specs=[piece_spec(lo, n) for lo, n in PIECES] + [pl.BlockSpec(memory_space=pl.ANY)],
        out_specs=[pl.BlockSpec((CT, D), lambda j: (_perm(j), 0)), pl.BlockSpec((1, CT), lambda j: (0, _perm(j)))],
        out_shape=[jax.ShapeDtypeStruct((NIN, D), BF16), jax.ShapeDtypeStruct((1, NIN), F32)],
        scratch_shapes=[pltpu.VMEM((T, D), BF16)],
        compiler_params=_cp(("arbitrary",), VMEM_BIG),
    )(*_hbm(*pieces, hn))


def _mm_tn(a, b, ta, tt, name):
    m = a.shape[1]
    n = b.shape[1]
    nt = T // tt

    def body(a_ref, b_ref, o_ref, acc):
        t = pl.program_id(1)
        p = _dot_tn(a_ref[...].astype(BF16), b_ref[...].astype(BF16))

        @pl.when(t == 0)
        def _():
            acc[...] = p

        @pl.when(t > 0)
        def _():
            acc[...] += p

        @pl.when(t == nt - 1)
        def _():
            o_ref[...] = acc[...].astype(BF16)

    return pl.pallas_call(
        body, grid=(m // ta, nt), name=name,
        in_specs=[pl.BlockSpec((tt, ta), lambda j, t: (t, j)), pl.BlockSpec((tt, n), lambda j, t: (t, 0))],
        out_specs=pl.BlockSpec((ta, n), lambda j, t: (j, 0)),
        out_shape=jax.ShapeDtypeStruct((m, n), BF16),
        scratch_shapes=[pltpu.VMEM((ta, n), F32)],
        compiler_params=_cp(("parallel", "arbitrary"), VMEM_MID),
    )(*_hbm(a, b))


def _row_iota():
    return lax.broadcasted_iota(jnp.int32, (S, RBW), 0)


SUBLANES = 8
N_SHIFT_BUFS = 4


class _Shifter:
    def __init__(self, bufs):
        self.bufs = bufs
        self.k = 0

    def _store(self, v, fill, front):
        b = self.bufs.at[self.k % N_SHIFT_BUFS]
        self.k += 1
        b[pl.ds(0 if front else SUBLANES + S, SUBLANES), :] = jnp.full((SUBLANES, RBW), fill, F32)
        b[pl.ds(SUBLANES, S), :] = v
        return b

    def down(self, v, ds, fill):
        b = self._store(v, fill, True)
        return [b[pl.ds(SUBLANES - d, S), :] for d in ds]

    def up(self, v, ds, fill):
        b = self._store(v, fill, False)
        return [b[pl.ds(SUBLANES + d, S), :] for d in ds]


def _shift_down(v, d, sh, fill):
    return sh.down(v, (d,), fill)[0]


def _shift_up(v, d, sh, fill):
    return sh.up(v, (d,), fill)[0]


def _scan_down(a, u, row):
    d = 1
    while d < S:
        last = 2 * d >= S
        if d < SUBLANES:
            u = a * _shift_down(u, d, row, 0.0) + u
            if not last:
                a = a * _shift_down(a, d, row, 1.0)
        else:
            u = jnp.concatenate([u[:d], a[d:] * u[:S - d] + u[d:]], axis=0)
            if not last:
                a = jnp.concatenate([a[:d], a[d:] * a[:S - d]], axis=0)
        d *= 2
    return u


def _scan_up(b, g, row):
    d = 1
    while d < S:
        last = 2 * d >= S
        if d < SUBLANES:
            g = g + b * _shift_up(g, d, row, 0.0)
            if not last:
                b = b * _shift_up(b, d, row, 0.0)
        else:
            g = jnp.concatenate([g[:S - d] + b[:S - d] * g[d:], g[S - d:]], axis=0)
            if not last:
                b = jnp.concatenate([b[:S - d] * b[d:], b[S - d:]], axis=0)
        d *= 2
    return g


def _softplus(x):
    return jnp.maximum(x, 0.0) + jnp.log1p(jnp.exp(-jnp.abs(x)))


def _rnn_gates(x, cw, cb, wa, ba, wx, bx, lam, row, pad):
    xs = pad.down(x, (1, 2, 3), 0.0)
    xc = cb + cw[3:4, :] * x
    for j in (1, 2, 3):
        xc = xc + cw[3 - j:4 - j, :] * xs[j - 1]
    xcb = xc.astype(BF16)
    r = _sigmoid(_dot(xcb, wa) + ba)
    i = _sigmoid(_dot(xcb, wx) + bx)
    sp = _softplus(-lam)
    log_a = (-LRU_C) * r * sp
    a = jnp.exp(log_a)
    mult = jnp.where(row == 0, 1.0, jnp.sqrt(jnp.tanh(-log_a) * (1.0 + a * a)))
    return xc, xcb, r, i, sp, a, mult, xs


def _rnn_fwd(proj3, conv_w, conv_b, wa, ba, wx, bx, lam, token):
    def body(x_ref, cw_ref, cb_ref, wa_ref, ba_ref, wx_ref, bx_ref, lam_ref, _token, h_ref, pad):
        row = _row_iota()
        sh = _Shifter(pad)
        x = x_ref[0].astype(F32)
        xc, _, _, i, _, a, mult, _ = _rnn_gates(x, cw_ref[...], cb_ref[...], wa_ref[0], ba_ref[...],
                                             wx_ref[0], bx_ref[...], lam_ref[...], row, sh)
        h_ref[0] = _scan_down(a, mult * (i * xc), sh)

    vec = lambda: pl.BlockSpec((1, RBW), lambda b, n: (0, n))
    mat = lambda: pl.BlockSpec((1, RBW, RBW), lambda b, n: (n, 0, 0))
    return pl.pallas_call(
        body, grid=(BL, NRB), name="rnn_fwd",
        in_specs=[pl.BlockSpec((1, S, RBW), lambda b, n: (b, 0, n)),
                  pl.BlockSpec((CONVW, RBW), lambda b, n: (0, n)),
                  vec(), mat(), vec(), mat(), vec(), vec(), pl.BlockSpec((8, LANES), lambda b, n: (0, 0))],
        out_specs=pl.BlockSpec((1, S, RBW), lambda b, n: (b, 0, n)),
        out_shape=jax.ShapeDtypeStruct((BL, S, DR), F32),
        scratch_shapes=[pltpu.VMEM((N_SHIFT_BUFS, S + 2 * SUBLANES, RBW), F32)],
        compiler_params=_cp(("parallel", "parallel"), VMEM_MID),
    )(*_hbm(proj3, conv_w, conv_b, wa, ba, wx, bx, lam, token))


def _rnn_bwd(proj3, h3, dh3, slab_a3, conv_w, conv_b, wa, ba, wx, bx, lam, token):
    def body(x_ref, h_ref, dh_ref, cw_ref, cb_ref, wa_ref, ba_ref, wx_ref, bx_ref, lam_ref, _alias, _token,
             dx_ref, dcw_ref, dcb_ref, dwa_ref, dba_ref, dwx_ref, dbx_ref, dlam_ref, pad):
        row = _row_iota()
        sh = _Shifter(pad)
        x = x_ref[0].astype(F32)
        cw = cw_ref[...]
        wa_v = wa_ref[0]
        wx_v = wx_ref[0]
        lam_v = lam_ref[...]
        xc, xcb, r, i, sp, a, mult, xs = _rnn_gates(x, cw, cb_ref[...], wa_v, ba_ref[...], wx_v, bx_ref[...], lam_v,
                                                    row, sh)
        h = h_ref[0]
        g = _scan_up(_shift_up(a, 1, sh, 0.0), dh_ref[0], sh)
        da = g * _shift_down(h, 1, sh, 0.0)
        dmult = jnp.where(row == 0, 0.0, g * (i * xc))
        gm = g * mult
        di = gm * xc
        dxc = gm * i
        dlog_a = da * a - dmult * (a * a) / mult
        dr = dlog_a * ((-LRU_C) * sp)
        dsp = jnp.sum(dlog_a * ((-LRU_C) * r), axis=0, keepdims=True)
        dlam = dsp * (-_sigmoid(-lam_v))
        dpa = dr * r * (1.0 - r)
        dpx = di * i * (1.0 - i)
        dpab = dpa.astype(BF16)
        dpxb = dpx.astype(BF16)
        dwa = _dot_tn(xcb, dpab)
        dwx = _dot_tn(xcb, dpxb)
        dba = jnp.sum(dpa, axis=0, keepdims=True)
        dbx = jnp.sum(dpx, axis=0, keepdims=True)
        dxc = dxc + _dot_nt(dpab, wa_v) + _dot_nt(dpxb, wx_v)
        dcb = jnp.sum(dxc, axis=0, keepdims=True)
        dx = cw[3:4, :] * dxc
        dcw_rows = [None] * CONVW
        dcw_rows[3] = jnp.sum(dxc * x, axis=0, keepdims=True)
        dxc_up = sh.up(dxc, (1, 2, 3), 0.0)
        for j in (1, 2, 3):
            dx = dx + cw[3 - j:4 - j, :] * dxc_up[j - 1]
            dcw_rows[3 - j] = jnp.sum(dxc * xs[j - 1], axis=0, keepdims=True)
        dx_ref[0] = dx.astype(BF16)
        dcw = jnp.concatenate(dcw_rows, axis=0)
        first = pl.program_id(1) == 0

        @pl.when(first)
        def _():
            dcw_ref[...] = dcw
            dcb_ref[...] = dcb
            dwa_ref[0] = dwa
            dba_ref[...] = dba
            dwx_ref[0] = dwx
            dbx_ref[...] = dbx
            dlam_ref[...] = dlam

        @pl.when(jnp.logical_not(first))
        def _():
            dcw_ref[...] += dcw
            dcb_ref[...] += dcb
            dwa_ref[0] += dwa
            dba_ref[...] += dba
            dwx_ref[0] += dwx
            dbx_ref[...] += dbx
            dlam_ref[...] += dlam

    slab = lambda: pl.BlockSpec((1, S, RBW), lambda n, b: (b, 0, n))
    vec = lambda: pl.BlockSpec((1, RBW), lambda n, b: (0, n))
    mat = lambda: pl.BlockSpec((1, RBW, RBW), lambda n, b: (n, 0, 0))
    taps = lambda: pl.BlockSpec((CONVW, RBW), lambda n, b: (0, n))
    vshape = jax.ShapeDtypeStruct((1, DR), F32)
    mshape = jax.ShapeDtypeStruct((NRB, RBW, RBW), F32)
    return pl.pallas_call(
        body, grid=(NRB, BL), name="rnn_bwd",
        in_specs=[slab(), slab(), slab(), taps(), vec(), mat(), vec(), mat(), vec(), vec(),
                  pl.BlockSpec(memory_space=pl.ANY), pl.BlockSpec((8, LANES), lambda n, b: (0, 0))],
        out_specs=[slab(), taps(), vec(), mat(), vec(), mat(), vec(), vec()],
        out_shape=[jax.ShapeDtypeStruct((BL, S, A_W), BF16), jax.ShapeDtypeStruct((CONVW, DR), F32),
                   vshape, mshape, vshape, mshape, vshape, vshape],
        input_output_aliases={10: 0},
        scratch_shapes=[pltpu.VMEM((N_SHIFT_BUFS, S + 2 * SUBLANES, RBW), F32)],
        compiler_params=_cp(("parallel", "arbitrary"), 48 * 1024 * 1024),
    )(*_hbm(proj3, h3, dh3, conv_w, conv_b, wa, ba, wx, bx, lam, slab_a3, token))


NQB = S // QB


def _rms_head(t, gain):
    rstd = lax.rsqrt(jnp.mean(t * t, axis=-1, keepdims=True) + EPS)
    return t * rstd * gain


def _rope(t, cs, sn):
    return t * cs + pltpu.roll(t, HD // 2, 1) * sn


def _rope_t(dy, cs, sn):
    return dy * cs - pltpu.roll(dy, HD // 2, 1) * sn


def _bdot_nt(a, b):
    return lax.dot_general(a, b, (((2,), (2,)), ((0,), (0,))), preferred_element_type=F32)


def _bdot(a, b):
    return lax.dot_general(a, b, (((2,), (1,)), ((0,), (0,))), preferred_element_type=F32)


def _bdot_tn(a, b):
    return lax.dot_general(a, b, (((1,), (1,)), ((0,), (0,))), preferred_element_type=F32)


STRIDE_MAX = 4


def _permute(buf, x, dil, dst, off=0):
    ln = S // dil
    if dil == 1:
        dst[pl.ds(off, S), :] = x.astype(dst.dtype)
        return
    buf[0] = x
    if dil <= STRIDE_MAX:
        for c in range(dil):
            dst[pl.ds(off + c * ln, ln), :] = buf.at[0][pl.ds(c, ln, stride=dil), :].astype(dst.dtype)
        return
    f, r = STRIDE_MAX, dil // STRIDE_MAX
    part = S // f
    for c1 in range(f):
        buf.at[1][pl.ds(c1 * part, part), :] = buf.at[0][pl.ds(c1, part, stride=f), :]
    for c1 in range(f):
        for c2 in range(r):
            dst[pl.ds(off + (c1 + f * c2) * ln, ln), :] = (
                buf.at[1][pl.ds(c1 * part + c2, ln, stride=r), :].astype(dst.dtype))


def _unpermute(buf, xp, dil, dst):
    ln = S // dil
    if dil == 1:
        dst[...] = xp
        return
    if dil <= STRIDE_MAX:
        for c in range(dil):
            dst[pl.ds(c, ln, stride=dil), :] = xp[c * ln:(c + 1) * ln]
        return
    f, r = STRIDE_MAX, dil // STRIDE_MAX
    part = S // f
    for c1 in range(f):
        for c2 in range(r):
            c = c1 + f * c2
            buf.at[1][pl.ds(c1 * part + c2, ln, stride=r), :] = xp[c * ln:(c + 1) * ln]
    for c1 in range(f):
        dst[pl.ds(c1, part, stride=f), :] = buf[1, pl.ds(c1 * part, part), :]


def _blocks3(ref, off=0):
    return ref[pl.ds(off, S), :].reshape(NQB, QB, HD)


def _att_prep(q_ref, k_ref, v_ref, cos_ref, sin_ref, qn, kn, dil, nat, qs, ksp, vsp):
    cs = cos_ref[...]
    sn = sin_ref[...]
    zero = jnp.zeros((QB, HD), BF16)
    ksp[pl.ds(0, QB), :] = zero
    vsp[pl.ds(0, QB), :] = zero
    _permute(nat, _rope(_rms_head(q_ref[0].astype(F32), qn), cs, sn), dil, qs)
    _permute(nat, _rope(_rms_head(k_ref[0].astype(F32), kn), cs, sn), dil, ksp, QB)
    _permute(nat, v_ref[0].astype(F32), dil, vsp, QB)


def _att_scores(qs, ksp, dil):
    nb = S // dil // QB
    q3 = _blocks3(qs)
    shape = (NQB, QB, QB)
    qi = lax.broadcasted_iota(jnp.int32, shape, 1)
    kj = lax.broadcasted_iota(jnp.int32, shape, 2)
    s_c = jnp.where(qi >= kj, _bdot_nt(q3, _blocks3(ksp, QB)) * SCALE, NEG)
    if nb == 1:
        return q3, s_c, None
    jj = lax.broadcasted_iota(jnp.int32, shape, 0)
    ok = (kj >= qi) & ((jj & (nb - 1)) != 0)
    s_p = jnp.where(ok, _bdot_nt(q3, _blocks3(ksp)) * SCALE, NEG)
    return q3, s_c, s_p


def _qkv_spec(kind, g):
    base = OFF_Q // HD + (3 * g + kind) * NH
    return pl.BlockSpec((1, S, HD), lambda b, h: (b, 0, base + h))


def _attn_fwd(proj3, cos_t, sin_t, q_norm, k_norm):
    def body(*refs):
        qkv_refs = refs[:9]
        (cos_ref, sin_ref, qn_ref, kn_ref, att_ref, lse_ref, w_ref, qp_ref, kp_ref, vp_ref,
         nat, qs, ksp, vsp, og) = refs[9:]
        for g, (window, dil) in enumerate(PATTERNS):
            q_ref, k_ref, v_ref = qkv_refs[3 * g:3 * g + 3]
            _att_prep(q_ref, k_ref, v_ref, cos_ref, sin_ref, qn_ref[g:g + 1, :], kn_ref[g:g + 1, :], dil,
                      nat, qs, ksp, vsp)
            qp_ref[g, 0] = qs[...]
            kp_ref[g, 0] = ksp[pl.ds(QB, S), :]
            vp_ref[g, 0] = vsp[pl.ds(QB, S), :]
            _, s_c, s_p = _att_scores(qs, ksp, dil)
            m = jnp.max(s_c, axis=-1, keepdims=True)
            if s_p is not None:
                m = jnp.maximum(m, jnp.max(s_p, axis=-1, keepdims=True))
            e_c = jnp.exp(s_c - m)
            den = jnp.sum(e_c, axis=-1, keepdims=True)
            o = _bdot(e_c.astype(BF16), _blocks3(vsp, QB))
            if s_p is not None:
                e_p = jnp.exp(s_p - m)
                den = den + jnp.sum(e_p, axis=-1, keepdims=True)
                o = o + _bdot(e_p.astype(BF16), _blocks3(vsp))
            _unpermute(nat, (o / den).reshape(S, HD), dil, og.at[g])
            _unpermute(nat, jnp.broadcast_to(m + jnp.log(den), (NQB, QB, HD)).reshape(S, HD), dil,
                       lse_ref.at[g, 0])
        l0 = lse_ref[0, 0]
        l1 = lse_ref[1, 0]
        l2 = lse_ref[2, 0]
        mx = jnp.maximum(jnp.maximum(l0, l1), l2)
        e0 = jnp.exp(l0 - mx)
        e1 = jnp.exp(l1 - mx)
        e2 = jnp.exp(l2 - mx)
        inv = 1.0 / (e0 + e1 + e2)
        w0 = e0 * inv
        w1 = e1 * inv
        w2 = e2 * inv
        w_ref[0, 0] = w0
        w_ref[1, 0] = w1
        w_ref[2, 0] = w2
        att_ref[0] = w0 * og[0] + w1 * og[1] + w2 * og[2]

    in_specs = [_qkv_spec(kind, g) for g in range(NG) for kind in range(3)]
    in_specs += [pl.BlockSpec((S, HD), lambda b, h: (0, 0)), pl.BlockSpec((S, HD), lambda b, h: (0, 0)),
                 pl.BlockSpec((NG, HD), lambda b, h: (0, 0)), pl.BlockSpec((NG, HD), lambda b, h: (0, 0))]
    stat = lambda: pl.BlockSpec((NG, 1, S, HD), lambda b, h: (0, b, 0, h))
    return pl.pallas_call(
        body, grid=(BL, NH), name="attn_fwd",
        in_specs=in_specs,
        out_specs=[pl.BlockSpec((1, S, HD), lambda b, h: (b, 0, h)), stat(), stat(), stat(), stat(), stat()],
        out_shape=[jax.ShapeDtypeStruct((BL, S, ATT), F32),
                   jax.ShapeDtypeStruct((NG, BL, S, ATT), F32),
                   jax.ShapeDtypeStruct((NG, BL, S, ATT), F32)]
        + [jax.ShapeDtypeStruct((NG, BL, S, ATT), BF16)] * 3,
        scratch_shapes=[pltpu.VMEM((2, S, HD), F32), pltpu.VMEM((S, HD), BF16), pltpu.VMEM((S + QB, HD), BF16),
                        pltpu.VMEM((S + QB, HD), BF16), pltpu.VMEM((NG, S, HD), F32)],
        compiler_params=_cp(("parallel", "parallel"), VMEM_BIG),
    )(*_hbm(*([proj3] * 9), cos_t, sin_t, q_norm, k_norm))


def _attn_bwd_group(g, proj3, cos_t, sin_t, qn_g, kn_g, lse, wts, qkv_p, datt3, sbar3, slabs):
    dil = PATTERNS[g][1]
    n_alias = 0 if slabs is None else 3

    def norm_rope_bwd(dpost, raw, gain, cs, sn):
        dn = _rope_t(dpost, cs, sn)
        rstd = lax.rsqrt(jnp.mean(raw * raw, axis=-1, keepdims=True) + EPS)
        xh = raw * rstd
        dgain = jnp.sum(dn * xh, axis=0, keepdims=True)
        gd = dn * gain
        draw = rstd * (gd - xh * jnp.mean(gd * xh, axis=-1, keepdims=True))
        return draw, dgain

    def body(*refs):
        (q_ref, k_ref, qp_ref, kp_ref, vp_ref, cos_ref, sin_ref, qn_ref, kn_ref, lse_ref, w_ref, datt_ref,
         sbar_ref) = refs[:13]
        (dq_ref, dk_ref, dv_ref, dqn_ref, dkn_ref, nat, ksp, vsp, dos, cvp, lsp, acc) = refs[13 + n_alias:]
        qn = qn_ref[...]
        kn = kn_ref[...]
        cs = cos_ref[...]
        sn = sin_ref[...]
        qs = qp_ref.at[0, 0]
        zero = jnp.zeros((QB, HD), BF16)
        ksp[pl.ds(0, QB), :] = zero
        vsp[pl.ds(0, QB), :] = zero
        ksp[pl.ds(QB, S), :] = kp_ref[0, 0]
        vsp[pl.ds(QB, S), :] = vp_ref[0, 0]
        wv = w_ref[0, 0]
        _permute(nat, wv * datt_ref[0], dil, dos)
        _permute(nat, wv * sbar_ref[0], dil, cvp)
        _permute(nat, lse_ref[0, 0], dil, lsp)
        q3, s_c, s_p = _att_scores(qs, ksp, dil)
        do3 = _blocks3(dos)
        lse3 = _blocks3(lsp)[:, :, 0:1]
        cv3 = _blocks3(cvp)[:, :, 0:1]
        p_c = jnp.exp(s_c - lse3)
        ds_c = (p_c * (_bdot_nt(do3, _blocks3(vsp, QB)) - cv3)).astype(BF16)
        dq = _bdot(ds_c, _blocks3(ksp, QB))
        acc[0] = _bdot_tn(ds_c, q3).reshape(S, HD)
        acc[1] = _bdot_tn(p_c.astype(BF16), do3).reshape(S, HD)
        if s_p is not None:
            p_p = jnp.exp(s_p - lse3)
            ds_p = (p_p * (_bdot_nt(do3, _blocks3(vsp)) - cv3)).astype(BF16)
            dq = dq + _bdot(ds_p, _blocks3(ksp))
            early = pl.ds(0, S - QB)
            acc[0, early, :] += _bdot_tn(ds_p, q3).reshape(S, HD)[QB:]
            acc[1, early, :] += _bdot_tn(p_p.astype(BF16), do3).reshape(S, HD)[QB:]
        _unpermute(nat, (dq * SCALE).reshape(S, HD), dil, nat.at[0])
        draw, dqn = norm_rope_bwd(nat[0], q_ref[0].astype(F32), qn, cs, sn)
        dq_ref[0] = draw.astype(BF16)
        _unpermute(nat, acc[0] * SCALE, dil, nat.at[0])
        draw, dkn = norm_rope_bwd(nat[0], k_ref[0].astype(F32), kn, cs, sn)
        dk_ref[0] = draw.astype(BF16)
        _unpermute(nat, acc[1], dil, nat.at[0])
        dv_ref[0] = nat[0].astype(BF16)
        first = (pl.program_id(0) == 0) & (pl.program_id(1) == 0)

        @pl.when(first)
        def _():
            dqn_ref[...] = dqn
            dkn_ref[...] = dkn

        @pl.when(jnp.logical_not(first))
        def _():
            dqn_ref[...] += dqn
            dkn_ref[...] += dkn

    full = lambda r: pl.BlockSpec((r, HD), lambda b, h: (0, 0))
    stat = lambda: pl.BlockSpec((1, 1, S, HD), lambda b, h: (g, b, 0, h))
    slab = lambda: pl.BlockSpec((1, S, HD), lambda b, h: (b, 0, h))
    out_slab = lambda: pl.BlockSpec((1, S, HD), lambda b, h: (b, 0, g * NH + h))
    big = jax.ShapeDtypeStruct((BL, S, GW), BF16)
    vecs = jax.ShapeDtypeStruct((1, HD), F32)
    in_specs = [_qkv_spec(0, g), _qkv_spec(1, g), stat(), stat(), stat(), full(S), full(S), full(1), full(1),
                stat(), stat(), slab(), slab()]
    args = [proj3, proj3, *qkv_p, cos_t, sin_t, qn_g, kn_g, lse, wts, datt3, sbar3]
    aliases = {}
    if slabs is not None:
        in_specs += [pl.BlockSpec(memory_space=pl.ANY)] * 3
        args += list(slabs)
        aliases = {13: 0, 14: 1, 15: 2}
    return pl.pallas_call(
        body, grid=(BL, NH), name="attn_bwd_g%d" % g,
        in_specs=in_specs,
        out_specs=[out_slab(), out_slab(), out_slab(), full(1), full(1)],
        out_shape=[big, big, big, vecs, vecs],
        scratch_shapes=[pltpu.VMEM((2, S, HD), F32), pltpu.VMEM((S + QB, HD), BF16),
                        pltpu.VMEM((S + QB, HD), BF16), pltpu.VMEM((S, HD), BF16), pltpu.VMEM((S, HD), F32),
                        pltpu.VMEM((S, HD), F32), pltpu.VMEM((2, S, HD), F32)],
        input_output_aliases=aliases,
        compiler_params=_cp(("arbitrary", "arbitrary"), VMEM_BIG),
    )(*_hbm(*args))


def _tail(x, proj, h, att, p, tgt, w_o_rnn, w_o_att_t, w_out, w_pg, w_ple_t, norm_ple, b_pg, tm=256):
    nt = T // tm
    inv_d = 1.0 / D

    def body(x_ref, h_ref, zr_ref, att_ref, za_ref, g0a_ref, g0b_ref, g1a_ref, g1b_ref, p_ref, tgt_ref,
             np_ref, bpg_ref, wor_hbm, woa_hbm, wout_hbm, wpg_hbm, wple_hbm,
             dx1_ref, merged_ref, n1_ref, dpre_ref, dpe_ref, dyr_ref, dya_ref, slab_a_ref, slab_c_ref, dh_ref,
             datt_ref, sbar_ref, yrnn_ref, yatt_ref, loss_ref, dnp_ref, dbpg_ref,
             wor, woa, wout, wpg, wple):
        first = pl.program_id(0) == 0

        @pl.when(first)
        def _():
            pltpu.sync_copy(wor_hbm, wor)
            pltpu.sync_copy(woa_hbm, woa)
            pltpu.sync_copy(wout_hbm, wout)
            pltpu.sync_copy(wpg_hbm, wpg)
            pltpu.sync_copy(wple_hbm, wple)

        xv = x_ref[...]
        hv = h_ref[...]
        zr = zr_ref[...].astype(F32)
        av = att_ref[...]
        za = za_ref[...].astype(F32)
        szr = _sigmoid(zr)
        silu_r = zr * szr
        yrnn_b = (hv * silu_r).astype(BF16)
        sza = _sigmoid(za)
        silu_a = za * sza
        yatt_b = (av * silu_a).astype(BF16)
        yrnn_ref[...] = yrnn_b
        yatt_ref[...] = yatt_b
        yr = _dot(yrnn_b, wor[...])
        ya = _dot_nt(yatt_b, woa[...])
        g0 = _sigmoid(jnp.concatenate([g0a_ref[...], g0b_ref[...]], axis=1).astype(F32))
        g1 = _sigmoid(jnp.concatenate([g1a_ref[...], g1b_ref[...]], axis=1).astype(F32))
        merged_b = (g0 * yr + g1 * ya).astype(BF16)
        merged_ref[...] = merged_b
        x1 = xv + _dot(merged_b, wout[...])
        rstd = lax.rsqrt(jnp.mean(x1 * x1, axis=-1, keepdims=True) + EPS)
        xh = x1 * rstd
        npl = np_ref[...]
        n1_b = (xh * npl).astype(BF16)
        n1_ref[...] = n1_b
        pg = _sigmoid(_dot(n1_b, wpg[...]) + bpg_ref[...])
        pe = _dot_nt(p_ref[...].astype(BF16), wple[...])
        err = x1 + pg * pe - tgt_ref[...]
        loss_t = 0.5 * inv_d * jnp.sum(err * err)
        dy = err * inv_d
        dpe_ref[...] = (dy * pg).astype(BF16)
        dpre = dy * pe * pg * (1.0 - pg)
        dpre_b = dpre.astype(BF16)
        dpre_ref[...] = dpre_b
        dn1 = _dot_nt(dpre_b, wpg[...])
        dnp = jnp.sum(dn1 * xh, axis=0, keepdims=True)
        dbpg = jnp.sum(dpre, axis=0, keepdims=True)
        gd = dn1 * npl
        dx1 = dy + rstd * (gd - xh * jnp.mean(gd * xh, axis=-1, keepdims=True))
        dx1_ref[...] = dx1
        dmerged = _dot_nt(dx1.astype(BF16), wout[...])
        dyr_b = (dmerged * g0).astype(BF16)
        dya_b = (dmerged * g1).astype(BF16)
        dyr_ref[...] = dyr_b
        dya_ref[...] = dya_b
        slab_c_ref[:, ATT:ATT + D] = (dmerged * yr * g0 * (1.0 - g0)).astype(BF16)
        slab_c_ref[:, ATT + D:ATT + 2 * D] = (dmerged * ya * g1 * (1.0 - g1)).astype(BF16)
        dyrnn = _dot_nt(dyr_b, wor[...])
        dyatt = _dot(dya_b, woa[...])
        dh_ref[...] = dyrnn * silu_r
        slab_a_ref[...] = (dyrnn * hv * szr * (1.0 + zr * (1.0 - szr))).astype(BF16)
        datt = dyatt * silu_a
        datt_ref[...] = datt
        slab_c_ref[:, 0:ATT] = (dyatt * av * sza * (1.0 + za * (1.0 - sza))).astype(BF16)
        da = datt * av
        for hh in range(NH):
            seg = slice(hh * HD, (hh + 1) * HD)
            sbar_ref[:, seg] = jnp.broadcast_to(jnp.sum(da[:, seg], axis=-1, keepdims=True), (tm, HD))

        @pl.when(first)
        def _():
            loss_ref[...] = jnp.full((8, LANES), loss_t, F32)
            dnp_ref[...] = dnp
            dbpg_ref[...] = dbpg

        @pl.when(jnp.logical_not(first))
        def _():
            loss_ref[...] += jnp.full((8, LANES), loss_t, F32)
            dnp_ref[...] += dnp
            dbpg_ref[...] += dbpg

    tok = lambda w: pl.BlockSpec((tm, w), lambda i: (i, 0))
    col = lambda w, blk: pl.BlockSpec((tm, w), lambda i: (i, blk))
    vec = lambda: pl.BlockSpec((1, D), lambda i: (0, 0))
    hbm = lambda: pl.BlockSpec(memory_space=pl.ANY)
    gb = OFF_G // 512
    in_specs = [tok(D), tok(DR), col(DR, 1), tok(ATT), col(ATT, OFF_ZA // ATT),
                col(512, gb), col(512, gb + 1), col(512, gb + 2), col(512, gb + 3),
                tok(PLE), tok(D), vec(), vec(), hbm(), hbm(), hbm(), hbm(), hbm()]
    sh = lambda w, dt: jax.ShapeDtypeStruct((T, w), dt)
    out_shape = [sh(D, F32), sh(D, BF16), sh(D, BF16), sh(D, BF16), sh(D, BF16), sh(D, BF16), sh(D, BF16),
                 sh(A_W, BF16), sh(C_W, BF16), sh(DR, F32), sh(ATT, F32), sh(ATT, F32),
                 sh(DR, BF16), sh(ATT, BF16),
                 jax.ShapeDtypeStruct((8, LANES), F32), jax.ShapeDtypeStruct((1, D), F32),
                 jax.ShapeDtypeStruct((1, D), F32)]
    out_specs = [tok(D), tok(D), tok(D), tok(D), tok(D), tok(D), tok(D), col(DR, 1), tok(C_W), tok(DR),
                 tok(ATT), tok(ATT), tok(DR), tok(ATT),
                 pl.BlockSpec((8, LANES), lambda i: (0, 0)), vec(), vec()]
    return pl.pallas_call(
        body, grid=(nt,), name="tail_fwd_bwd",
        in_specs=in_specs, out_specs=out_specs, out_shape=out_shape,
        scratch_shapes=[pltpu.VMEM((DR, D), BF16), pltpu.VMEM((D, ATT), BF16), pltpu.VMEM((D, D), BF16),
                        pltpu.VMEM((D, D), BF16), pltpu.VMEM((D, PLE), BF16)],
        compiler_params=_cp(("arbitrary",), VMEM_BIG),
    )(*_hbm(x, h, proj, att, proj, proj, proj, proj, proj, p, tgt, norm_ple, b_pg, w_o_rnn, w_o_att_t, w_out, w_pg,
            w_ple_t))


def _input_norm_bwd(x, dhn, dx1, gain, tm=512):
    def body(x_ref, dhn_ref, dx1_ref, g_ref, dx_ref, dg_ref):
        xv = x_ref[...]
        rstd = lax.rsqrt(jnp.mean(xv * xv, axis=-1, keepdims=True) + EPS)
        xh = xv * rstd
        dn = dhn_ref[...]
        dg = jnp.sum(dn * xh, axis=0, keepdims=True)
        gd = dn * g_ref[...]
        dx_ref[...] = dx1_ref[...] + rstd * (gd - xh * jnp.mean(gd * xh, axis=-1, keepdims=True))
        first = pl.program_id(0) == 0

        @pl.when(first)
        def _():
            dg_ref[...] = dg

        @pl.when(jnp.logical_not(first))
        def _():
            dg_ref[...] += dg

    tok = lambda: pl.BlockSpec((tm, D), lambda i: (i, 0))
    vec = lambda: pl.BlockSpec((1, D), lambda i: (0, 0))
    return pl.pallas_call(
        body, grid=(T // tm,), name="input_norm_bwd",
        in_specs=[tok(), tok(), tok(), vec()], out_specs=[tok(), vec()],
        out_shape=[jax.ShapeDtypeStruct((T, D), F32), jax.ShapeDtypeStruct((1, D), F32)],
        compiler_params=_cp(("arbitrary",), VMEM_MID),
    )(*_hbm(x, dhn, dx1, gain))


def _rope_tables():
    pos = jnp.arange(S, dtype=F32)
    inv_freq = ROPE_THETA ** (-jnp.arange(0, HD, 2, dtype=F32) / HD)
    ang = pos[:, None] * inv_freq[None, :]
    cos, sin = jnp.cos(ang), jnp.sin(ang)
    return jnp.concatenate([cos, cos], axis=1), jnp.concatenate([-sin, sin], axis=1)


def _local_step(x, p, tgt, project, other_weights, norm_mix, conv_b,
                w_rg_a, b_rg_a, w_rg_x, b_rg_x, lam, q_norm, k_norm, norm_ple, b_pg, start_reduce=None,
                entry_token=None):
    if start_reduce is None:
        start_reduce = lambda arrs, tag: (jnp.zeros((8, LANES), F32), arrs)
    if entry_token is None:
        entry_token = jnp.zeros((8, LANES), F32)
    cos_t, sin_t = _rope_tables()
    wa_b = w_rg_a.astype(BF16)
    wx_b = w_rg_x.astype(BF16)

    hn = _rmsnorm_fwd(x, norm_mix, entry_token)
    proj, w_bufs, chips, conv_w, token = project(hn)
    proj3 = proj.reshape(BL, S, NIN)
    h3 = _rnn_fwd(proj3, conv_w, conv_b, wa_b, b_rg_a, wx_b, b_rg_x, lam, token)
    att3, lse, wts, *qkv_p = _attn_fwd(proj3, cos_t, sin_t, q_norm, k_norm)
    w_o_rnn, w_o_att_t, w_out, w_pg, w_ple_t = other_weights(att3)
    (dx1, merged, n1, dpre, dpe, dyr, dya, slab_a, slab_c, dh, datt, sbar, yrnn, yatt, loss8, dnp, dbpg) = _tail(
        x, proj, h3.reshape(T, DR), att3.reshape(T, ATT), p, tgt, w_o_rnn, w_o_att_t, w_out, w_pg, w_ple_t,
        norm_ple, b_pg)

    token, pending_out = start_reduce([
        _mm_tn(yrnn, dyr, 640, 2048, "dw_o_rnn"),
        _mm_tn(dya, yatt, 512, 2048, "dw_o_att_t"),
        _mm_tn(merged, dx1, 512, 2048, "dw_out"),
        _mm_tn(n1, dpre, 512, 2048, "dw_ple_gate"),
        _mm_tn(dpe, p, 512, 2048, "dw_ple_t")], "out")

    slab_a3, dcw, dcb, dwa, dba, dwx, dbx, dlam = _rnn_bwd(
        proj3, h3, dh.reshape(BL, S, DR), slab_a.reshape(BL, S, A_W), conv_w, conv_b, wa_b, b_rg_a, wx_b, b_rg_x, lam,
        token)
    datt3 = datt.reshape(BL, S, ATT)
    sbar3 = sbar.reshape(BL, S, ATT)
    slabs = None
    dqn = []
    dkn = []
    for g in range(NG):
        dq, dk, dv, dqn_g, dkn_g = _attn_bwd_group(g, proj3, cos_t, sin_t, q_norm[g:g + 1], k_norm[g:g + 1],
                                                   lse, wts, qkv_p, datt3, sbar3, slabs)
        slabs = (dq, dk, dv)
        dqn.append(dqn_g)
        dkn.append(dkn_g)
    pieces = [slab_a3.reshape(T, A_W)] + [t.reshape(T, GW) for t in slabs] + [slab_c]
    dw_in_t, db_in = _dw_in(pieces, hn)
    token, pending_in = start_reduce([dw_in_t], "in")
    grad_x, dnm = _grad_x(pieces, w_bufs, chips, token, x, dx1, norm_mix)

    small = dict(w_rg_a=dwa, w_rg_x=dwx, norm_mix=dnm, b_in=db_in, conv_b=dcb, b_rg_a=dba, b_rg_x=dbx,
                 lru_lambda=dlam, q_norm=dqn, k_norm=dkn, norm_ple=dnp, b_ple_gate=dbpg, conv_w=dcw, loss=loss8)
    return grad_x, pending_out, pending_in, small


MESH = pl.DeviceIdType.MESH
HBM_SPEC = pl.BlockSpec(memory_space=pl.ANY)


def _my_pos():
    return lax.axis_index("x"), lax.axis_index("y"), lax.axis_index("c")


def _flip(pos, k):
    x, y, c = pos
    return (1 - x if k & 4 else x, 1 - y if k & 2 else y, 1 - c if k & 1 else c)


def _lin(pos):
    return 4 * pos[0] + 2 * pos[1] + pos[2]


def _chip(pos):
    return 2 * pos[0] + pos[1]


def _all_gather_two_level(shards, name):
    na = len(shards)

    def body(*refs):
        x_refs = refs[:na]
        out_refs = refs[na:2 * na]
        send_sems, recv_sems, local_sems = refs[2 * na:]
        me = _my_pos()
        sibling = _flip(me, 1)
        chips = [_flip(me, 4), _flip(me, 2), _flip(me, 6)]

        def copy(i, k, block, to, from_x=False):
            dst = out_refs[i].at[_lin(block)]
            return pltpu.make_async_remote_copy(
                src_ref=x_refs[i] if from_x else dst, dst_ref=dst,
                send_sem=send_sems.at[7 * i + k], recv_sem=recv_sems.at[7 * i + k], device_id=to, device_id_type=MESH)

        started = []
        for i in range(na):
            mine = pltpu.make_async_copy(x_refs[i], out_refs[i].at[_lin(me)], local_sems.at[i])
            mine.start()
            started.append(mine)
        sends = []
        for i in range(na):
            cps = [copy(i, 0, me, sibling, True)] + [copy(i, 1 + j, me, chip, True) for j, chip in enumerate(chips)]
            for cp in cps:
                cp.start()
            sends += cps
        for i in range(na):
            for j, chip in enumerate(chips):
                copy(i, 1 + j, chip, me).wait_recv()
                fwd = copy(i, 4 + j, chip, sibling)
                fwd.start()
                sends.append(fwd)
        for i in range(na):
            copy(i, 0, sibling, me).wait_recv()
            for j, chip in enumerate(chips):
                copy(i, 4 + j, _flip(chip, 1), me).wait_recv()
        for cp in sends:
            cp.wait_send()
        for mine in started:
            mine.wait()

    return pl.pallas_call(
        body, name=name,
        out_shape=[jax.ShapeDtypeStruct((NDEV,) + s.shape, s.dtype) for s in shards],
        in_specs=[HBM_SPEC] * na, out_specs=[HBM_SPEC] * na,
        scratch_shapes=[pltpu.SemaphoreType.DMA((7 * na,)), pltpu.SemaphoreType.DMA((7 * na,)),
                        pltpu.SemaphoreType.DMA((na,))],
    )(*shards)


def _all_gather_direct(shard, name):
    def body(x_ref, out_ref, send_sems, recv_sems, local_sem):
        me = _my_pos()
        mine = pltpu.make_async_copy(x_ref, out_ref.at[_lin(me)], local_sem)
        mine.start()
        sends = []
        for k in range(1, NDEV):
            cp = pltpu.make_async_remote_copy(
                src_ref=x_ref, dst_ref=out_ref.at[_lin(me)], send_sem=send_sems.at[k - 1],
                recv_sem=recv_sems.at[k - 1], device_id=_flip(me, k), device_id_type=MESH)
            cp.start()
            sends.append(cp)
        for k in range(1, NDEV):
            peer = _flip(me, k)
            pltpu.make_async_remote_copy(
                src_ref=x_ref, dst_ref=out_ref.at[_lin(peer)], send_sem=send_sems.at[k - 1],
                recv_sem=recv_sems.at[k - 1], device_id=peer, device_id_type=MESH).wait_recv()
        for cp in sends:
            cp.wait_send()
        mine.wait()

    return pl.pallas_call(
        body, name=name,
        out_shape=jax.ShapeDtypeStruct((NDEV,) + shard.shape, shard.dtype),
        in_specs=[HBM_SPEC], out_specs=HBM_SPEC,
        scratch_shapes=[pltpu.SemaphoreType.DMA((7,)), pltpu.SemaphoreType.DMA((7,)), pltpu.SemaphoreType.DMA],
    )(shard)


def _exchange_within_chip(parts, name):
    na = len(parts)

    def body(*refs):
        a_refs = refs[:na]
        recv_refs = refs[na:2 * na]
        send_sems, recv_sems = refs[2 * na:]
        me = _my_pos()
        c = me[2]
        sibling = _flip(me, 1)
        remote = []
        for i in range(na):
            for q in range(NCHIP):
                rc = pltpu.make_async_remote_copy(
                    src_ref=a_refs[i].at[q, 1 - c], dst_ref=recv_refs[i].at[q],
                    send_sem=send_sems.at[NCHIP * i + q], recv_sem=recv_sems.at[NCHIP * i + q],
                    device_id=sibling, device_id_type=MESH)
                rc.start()
                remote.append(rc)
        for rc in remote:
            rc.wait_recv()
        for rc in remote:
            rc.wait_send()

    return pl.pallas_call(
        body, name=name, out_shape=[jax.ShapeDtypeStruct((NCHIP,) + a.shape[2:], a.dtype) for a in parts],
        in_specs=[HBM_SPEC] * na, out_specs=[HBM_SPEC] * na,
        scratch_shapes=[pltpu.SemaphoreType.DMA((NCHIP * na,)), pltpu.SemaphoreType.DMA((NCHIP * na,))],
    )(*parts)


HBM_ONLY = pl.BlockSpec(memory_space=pltpu.HBM)
SEM_SPEC = pl.BlockSpec(memory_space=pltpu.SEMAPHORE)
SPLIT_COPY = pltpu.CompilerParams(has_side_effects=pltpu.SideEffectType.DATAFLOW_SIDE_EFFECTING)


def _chip_peers(me):
    return [_flip(me, 4), _flip(me, 2), _flip(me, 6)]


def _between_chips_start(parts, name):
    na = len(parts)

    def body(*refs):
        a_refs = refs[:na]
        land_refs = refs[na:2 * na]
        send_sems, recv_sems = refs[2 * na], refs[2 * na + 1]
        token = refs[-1]
        me = _my_pos()
        myq = _chip(me)
        for i in range(na):
            for j, peer in enumerate(_chip_peers(me)):
                pltpu.make_async_remote_copy(
                    src_ref=a_refs[i].at[_chip(peer)], dst_ref=land_refs[i].at[myq],
                    send_sem=send_sems.at[3 * i + j], recv_sem=recv_sems.at[3 * i + j],
                    device_id=peer, device_id_type=MESH).start()
        token[...] = jnp.zeros_like(token)

    hbm = [pltpu.HBM(a.shape, a.dtype) for a in parts]
    srcs = [pltpu.with_memory_space_constraint(a, pltpu.HBM) for a in parts]
    lands = [pltpu.with_memory_space_constraint(lax.empty(a.shape, a.dtype), pltpu.HBM) for a in parts]
    res = pl.pallas_call(
        body, name=name,
        out_shape=(pltpu.SemaphoreType.DMA((3 * na,)), pltpu.SemaphoreType.DMA((3 * na,)), *hbm, *hbm,
                   jax.ShapeDtypeStruct((8, LANES), F32)),
        in_specs=[HBM_ONLY] * (2 * na),
        out_specs=(SEM_SPEC, SEM_SPEC, *([HBM_ONLY] * (2 * na)), pl.BlockSpec(memory_space=pltpu.VMEM)),
        input_output_aliases={i: 2 + i for i in range(2 * na)},
        compiler_params=SPLIT_COPY,
    )(*srcs, *lands)
    return res[-1], (res[0], res[1], list(res[2:2 + na]), list(res[2 + na:2 + 2 * na]))


def _between_chips_wait(pending, after, name):
    send_sems, recv_sems, parts, lands = pending
    na = len(parts)

    def body(*refs):
        a_refs = refs[:na]
        land_refs = refs[na:2 * na]
        send_sems, recv_sems = refs[2 * na], refs[2 * na + 1]
        me = _my_pos()
        for i in range(na):
            for j, peer in enumerate(_chip_peers(me)):
                cp = pltpu.make_async_remote_copy(
                    src_ref=a_refs[i].at[_chip(peer)], dst_ref=land_refs[i].at[_chip(peer)],
                    send_sem=send_sems.at[3 * i + j], recv_sem=recv_sems.at[3 * i + j],
                    device_id=peer, device_id_type=MESH)
                cp.wait_send()
                cp.wait_recv()

    hbm = [pltpu.HBM(a.shape, a.dtype) for a in parts]
    res = pl.pallas_call(
        body, name=name, out_shape=(*hbm, *hbm),
        in_specs=[HBM_ONLY] * (2 * na) + [SEM_SPEC, SEM_SPEC, pl.BlockSpec(memory_space=pl.ANY)],
        out_specs=[HBM_ONLY] * (2 * na),
        input_output_aliases={i: i for i in range(2 * na)},
        compiler_params=SPLIT_COPY,
    )(*parts, *lands, send_sems, recv_sems, after)
    return list(res[:na]), list(res[na:])


def _remote(src, dst, send_sems, recv_sems, idx, peer):
    return pltpu.make_async_remote_copy(src_ref=src, dst_ref=dst, send_sem=send_sems.at[idx],
                                        recv_sem=recv_sems.at[idx], device_id=peer, device_id_type=MESH)


def _copies_own(bufs, me):
    return [(bufs[0], bufs[1].at[me[2]], 0, _flip(me, 1))]


def _copies_near(bufs, me):
    return [(bufs[0], bufs[1].at[0, me[2]], 0, _flip(me, 2)), (bufs[0], bufs[1].at[1, me[2]], 1, _flip(me, 4))]


def _copies_far(bufs, me):
    return [(bufs[0], bufs[1].at[me[2]], 0, _flip(me, 6))]


def _copies_others(bufs, me):
    na = len(bufs) // 2
    return [(bufs[i], bufs[na + i].at[_lin(me)], 7 * i + k - 1, _flip(me, k))
            for i in range(na) for k in range(1, NDEV)]


def _copies_exchange(bufs, me):
    na = len(bufs) // 2
    return [(bufs[i].at[_lin(_flip(me, k))], bufs[na + i].at[_lin(me)], 7 * i + k - 1, _flip(me, k))
            for i in range(na) for k in range(1, NDEV)]


GROUP_COPIES = dict(own=_copies_own, near=_copies_near, far=_copies_far, others=_copies_others,
                    exchange=_copies_exchange)
GROUP_COUNT = dict(own=1, near=2, far=1)
TO_ALL = ("others", "exchange")


def _gather_start(bufs, groups, after, name):
    nb = len(bufs)
    ng = len(groups)

    def body(*refs):
        b = refs[:nb]
        sems = refs[nb + 1:nb + 1 + 2 * ng]
        token = refs[-1]
        me = _my_pos()
        for gi, (group, idx) in enumerate(groups):
            for src, dst, k, peer in GROUP_COPIES[group]([b[i] for i in idx], me):
                _remote(src, dst, sems[2 * gi], sems[2 * gi + 1], k, peer).start()
        token[...] = jnp.zeros_like(token)

    sem_t = []
    for group, idx in groups:
        cnt = 7 * (len(idx) // 2) if group in TO_ALL else GROUP_COUNT[group]
        sem_t += [pltpu.SemaphoreType.DMA((cnt,)), pltpu.SemaphoreType.DMA((cnt,))]
    ins = [pltpu.with_memory_space_constraint(a, pltpu.HBM) for a in bufs]
    res = pl.pallas_call(
        body, name=name,
        out_shape=(*sem_t, *[pltpu.HBM(a.shape, a.dtype) for a in bufs], jax.ShapeDtypeStruct((8, LANES), F32)),
        in_specs=[HBM_ONLY] * nb + [pl.BlockSpec(memory_space=pl.ANY)],
        out_specs=(*([SEM_SPEC] * (2 * ng)), *([HBM_ONLY] * nb), pl.BlockSpec(memory_space=pltpu.VMEM)),
        input_output_aliases={i: 2 * ng + i for i in range(nb)},
        compiler_params=SPLIT_COPY,
    )(*ins, after)
    return res[-1], list(res[2 * ng:2 * ng + nb]), [(res[2 * gi], res[2 * gi + 1]) for gi in range(ng)]


def _gather_wait(group, send_sems, recv_sems, bufs, after, name):
    nb = len(bufs)
    copies = GROUP_COPIES[group]

    def body(*refs):
        b = refs[:nb]
        ss, rs = refs[nb], refs[nb + 1]
        me = _my_pos()
        for src, dst, idx, peer in copies(b, me):
            if group in TO_ALL:
                landed = b[nb // 2 + idx // 7].at[_lin(peer)]
            elif group == "own":
                landed = b[1].at[1 - me[2]]
            else:
                landed = dst
            cp = _remote(src, landed, ss, rs, idx, peer)
            cp.wait_send()
            cp.wait_recv()

    res = pl.pallas_call(
        body, name=name, out_shape=[pltpu.HBM(a.shape, a.dtype) for a in bufs],
        in_specs=[HBM_ONLY] * nb + [SEM_SPEC, SEM_SPEC, pl.BlockSpec(memory_space=pl.ANY)],
        out_specs=[HBM_ONLY] * nb,
        input_output_aliases={i: i for i in range(nb)},
        compiler_params=SPLIT_COPY,
    )(*bufs, send_sems, recv_sems, after)
    return list(res)


def _forward_to_sibling(buf, name):
    n = buf.shape[0]

    def body(_in_ref, out_ref, send_sems, recv_sems):
        me = _my_pos()
        c = me[2]
        sibling = _flip(me, 1)
        sends = []
        for r in range(n):
            cp = _remote(out_ref.at[r, c], out_ref.at[r, c], send_sems, recv_sems, r, sibling)
            cp.start()
            sends.append(cp)
        for r in range(n):
            _remote(out_ref.at[r, c], out_ref.at[r, 1 - c], send_sems, recv_sems, r, sibling).wait_recv()
        for cp in sends:
            cp.wait_send()

    return pl.pallas_call(
        body, name=name, out_shape=jax.ShapeDtypeStruct(buf.shape, buf.dtype),
        in_specs=[HBM_SPEC], out_specs=HBM_SPEC,
        scratch_shapes=[pltpu.SemaphoreType.DMA((n,)), pltpu.SemaphoreType.DMA((n,))],
        input_output_aliases={0: 0},
    )(buf)


def _allreduce_small(rep, conv, name):
    def body(rep_hbm, conv_hbm, out_ref, conv_out, stage_r, stage_c, send_sems, recv_sems, local_sems):
        me = _my_pos()
        mi = _lin(me)
        peers = [_flip(me, k) for k in range(1, NDEV)]
        own = [pltpu.make_async_copy(rep_hbm.at[mi], stage_r.at[mi], local_sems.at[0]),
               pltpu.make_async_copy(conv_hbm.at[mi], stage_c.at[mi], local_sems.at[1])]
        for cp in own:
            cp.start()
        sends = []
        for j, peer in enumerate(peers):
            for src, stage, base in ((rep_hbm, stage_r, 0), (conv_hbm, stage_c, 7)):
                cp = _remote(src.at[_lin(peer)], stage.at[mi], send_sems, recv_sems, base + j, peer)
                cp.start()
                sends.append(cp)
        for j, peer in enumerate(peers):
            for src, stage, base in ((rep_hbm, stage_r, 0), (conv_hbm, stage_c, 7)):
                _remote(src.at[mi], stage.at[_lin(peer)], send_sems, recv_sems, base + j, peer).wait_recv()
        for cp in own:
            cp.wait()
        acc_r = stage_r[0]
        acc_c = stage_c[0]
        for q in range(1, NDEV):
            acc_r = acc_r + stage_r[q]
            acc_c = acc_c + stage_c[q]
        out_ref[mi] = acc_r
        conv_out[...] = acc_c
        for j, peer in enumerate(peers):
            cp = _remote(out_ref.at[mi], out_ref.at[mi], send_sems, recv_sems, 14 + j, peer)
            cp.start()
            sends.append(cp)
        for j, peer in enumerate(peers):
            _remote(out_ref.at[mi], out_ref.at[_lin(peer)], send_sems, recv_sems, 14 + j, peer).wait_recv()
        for cp in sends:
            cp.wait_send()

    vmem = pl.BlockSpec(memory_space=pltpu.VMEM)
    return pl.pallas_call(
        body, name=name,
        out_shape=[jax.ShapeDtypeStruct(rep.shape, F32), jax.ShapeDtypeStruct(conv.shape[1:], F32)],
        in_specs=[HBM_SPEC, HBM_SPEC], out_specs=[vmem, vmem],
        scratch_shapes=[pltpu.VMEM(rep.shape, F32), pltpu.VMEM(conv.shape, F32),
                        pltpu.SemaphoreType.DMA((21,)), pltpu.SemaphoreType.DMA((21,)), pltpu.SemaphoreType.DMA((2,))],
    )(rep, conv)


def _exchange_all(parts, name):
    na = len(parts)

    def body(*refs):
        a_refs = refs[:na]
        out_refs = refs[na:2 * na]
        send_sems, recv_sems = refs[2 * na:]
        me = _my_pos()
        sends = []
        for i in range(na):
            for k in range(1, NDEV):
                peer = _flip(me, k)
                cp = pltpu.make_async_remote_copy(
                    src_ref=a_refs[i].at[_lin(peer)], dst_ref=out_refs[i].at[_lin(me)],
                    send_sem=send_sems.at[7 * i + k - 1], recv_sem=recv_sems.at[7 * i + k - 1],
                    device_id=peer, device_id_type=MESH)
                cp.start()
                sends.append(cp)
        for i in range(na):
            for k in range(1, NDEV):
                peer = _flip(me, k)
                pltpu.make_async_remote_copy(
                    src_ref=a_refs[i].at[_lin(peer)], dst_ref=out_refs[i].at[_lin(peer)],
                    send_sem=send_sems.at[7 * i + k - 1], recv_sem=recv_sems.at[7 * i + k - 1],
                    device_id=peer, device_id_type=MESH).wait_recv()
        for cp in sends:
            cp.wait_send()

    return pl.pallas_call(
        body, name=name, out_shape=[jax.ShapeDtypeStruct(a.shape, a.dtype) for a in parts],
        in_specs=[HBM_SPEC] * na, out_specs=[HBM_SPEC] * na,
        scratch_shapes=[pltpu.SemaphoreType.DMA((7 * na,)), pltpu.SemaphoreType.DMA((7 * na,))],
    )(*parts)


def _scalar(v):
    return jnp.asarray(v, jnp.int32).reshape(1)


def _sum_pairs(parts, theirs, name):
    na = len(parts)

    def body(c_ref, *refs):
        for i in range(na):
            o_ref = refs[2 * na + i]
            o_ref[0] = (refs[i][0, 0].astype(F32) + refs[na + i][0].astype(F32)).astype(o_ref.dtype)

    def mine_spec(a):
        return pl.BlockSpec((1, 1) + a.shape[2:], lambda q, c_ref: (q, c_ref[0], 0, 0))

    def spec(a):
        return pl.BlockSpec((1,) + a.shape[1:], lambda q, c_ref: (q, 0, 0))

    return pl.pallas_call(
        body, name=name,
        grid_spec=pltpu.PrefetchScalarGridSpec(
            num_scalar_prefetch=1, grid=(NCHIP,),
            in_specs=[mine_spec(a) for a in parts] + [spec(a) for a in theirs],
            out_specs=[spec(a) for a in theirs]),
        out_shape=[jax.ShapeDtypeStruct(a.shape, a.dtype) for a in theirs],
        compiler_params=_cp(("arbitrary",), VMEM_BIG),
    )(_scalar(lax.axis_index("c")), *parts, *theirs)


def _others(q, mine, nblk=NCHIP):
    return jnp.where(q == mine, (q + 1) % nblk, q)


def _sum_chips_adamw(own, recv, wv, mv, vv, token, tr, name):
    _, r, w = recv.shape

    def body(q_ref, own_ref, r0, r1, r2, r3, w_ref, m_ref, v_ref, _token, g_ref, d_ref, m2_ref, v2_ref):
        myq = q_ref[0]
        acc = None
        for q, r_ref in enumerate((r0, r1, r2, r3)):
            term = jnp.where(myq == q, own_ref[0], r_ref[0]).astype(F32)
            acc = term if acc is None else acc + term
        g_ref[...] = acc
        delta, m2, v2 = _adam_math(w_ref[...], acc, m_ref[...], v_ref[...])
        d_ref[...] = delta
        m2_ref[...] = m2
        v2_ref[...] = v2

    def recv_spec(q):
        return pl.BlockSpec((1, tr, w), lambda i, q_ref: (_others(q, q_ref[0]), i, 0))

    rows = lambda: pl.BlockSpec((tr, w), lambda i, q_ref: (i, 0))
    shp = jax.ShapeDtypeStruct((r, w), F32)
    return pl.pallas_call(
        body, name=name,
        grid_spec=pltpu.PrefetchScalarGridSpec(
            num_scalar_prefetch=1, grid=(r // tr,),
            in_specs=[pl.BlockSpec((1, tr, w), lambda i, q_ref: (q_ref[0], i, 0))]
            + [recv_spec(q) for q in range(NCHIP)] + [rows(), rows(), rows()]
            + [pl.BlockSpec((8, LANES), lambda i, q_ref: (0, 0))],
            out_specs=[rows(), rows(), rows(), rows()]),
        out_shape=[shp, shp, shp, shp],
        compiler_params=_cp(("arbitrary",), VMEM_MID),
    )(_scalar(_chip(_my_pos())), *_hbm(own, recv, recv, recv, recv, wv, mv, vv, token))


def _sum_blocks_small(own, recv, mine, transpose, name):
    na = len(recv)
    nblk = recv[0].shape[0]

    def body(q_ref, *refs):
        me = q_ref[0]
        for i in range(na):
            acc = None
            for q in range(nblk):
                term = jnp.where(me == q, refs[i][0], refs[na * (1 + q) + i][0]).astype(F32)
                acc = term if acc is None else acc + term
            refs[na * (1 + nblk) + i][...] = acc.T if transpose[i] else acc

    def oshape(a, tr):
        r, w = a.shape[1:]
        return (w, r) if tr else (r, w)

    own_spec = lambda a: pl.BlockSpec((1,) + a.shape[1:], lambda s, q_ref: (q_ref[0], 0, 0))
    recv_spec = lambda a, q: pl.BlockSpec((1,) + a.shape[1:], lambda s, q_ref: (_others(q, q_ref[0], nblk), 0, 0))
    out_spec = lambda shp: pl.BlockSpec(shp, lambda s, q_ref: (0, 0))
    in_specs = [own_spec(a) for a in own]
    for q in range(nblk):
        in_specs += [recv_spec(a, q) for a in recv]
    return pl.pallas_call(
        body, name=name,
        grid_spec=pltpu.PrefetchScalarGridSpec(
            num_scalar_prefetch=1, grid=(1,), in_specs=in_specs,
            out_specs=[out_spec(oshape(a, tr)) for a, tr in zip(recv, transpose)]),
        out_shape=[jax.ShapeDtypeStruct(oshape(a, tr), F32) for a, tr in zip(recv, transpose)],
        compiler_params=_cp(("arbitrary",), VMEM_MID),
    )(_scalar(mine), *own, *(list(recv) * nblk))


def _rep_offsets():
    offs = []
    o = 0
    for r in REP_ROWS:
        offs.append(o)
        o += r
    return offs


LOSS_ROW = REP_TOTAL_ROWS


def _pack_small_grads(g):
    offs = _rep_offsets()

    def body(dwa, dwx, dnm, dbin, dcb, dba, dbx, dlam, dq0, dq1, dq2, dk0, dk1, dk2, dnp, dbpg, loss, o_ref):
        o_ref[pl.ds(REP_TOTAL_ROWS - 2, NDEV * REP_ROWS_DEV - REP_TOTAL_ROWS + 2), :] = jnp.zeros(
            (NDEV * REP_ROWS_DEV - REP_TOTAL_ROWS + 2, LANES), F32)
        o_ref[pl.ds(LOSS_ROW, 1), :] = loss[0:1, :]
        for n in range(NRB):
            o_ref[pl.ds(offs[0] + n * RBW, RBW), :] = dwa[n]
            o_ref[pl.ds(offs[1] + n * RBW, RBW), :] = dwx[n]

        def put_vec(off, ref, rows):
            for k in range(rows):
                o_ref[pl.ds(off + k, 1), :] = ref[:, k * LANES:(k + 1) * LANES]

        put_vec(offs[2], dnm, REP_ROWS[2])
        put_vec(offs[3], dbin, REP_ROWS[3])
        put_vec(offs[4], dcb, REP_ROWS[4])
        put_vec(offs[5], dba, REP_ROWS[5])
        put_vec(offs[6], dbx, REP_ROWS[6])
        put_vec(offs[7], dlam, REP_ROWS[7])
        for k, ref in enumerate((dq0, dq1, dq2)):
            o_ref[pl.ds(offs[8] + k, 1), :] = ref[...]
        for k, ref in enumerate((dk0, dk1, dk2)):
            o_ref[pl.ds(offs[9] + k, 1), :] = ref[...]
        put_vec(offs[10], dnp, REP_ROWS[10])
        put_vec(offs[11], dbpg, REP_ROWS[11])

    args = [g["w_rg_a"], g["w_rg_x"], g["norm_mix"], g["b_in"], g["conv_b"], g["b_rg_a"], g["b_rg_x"],
            g["lru_lambda"], *g["q_norm"], *g["k_norm"], g["norm_ple"], g["b_ple_gate"], g["loss"]]
    full = lambda shp: pl.BlockSpec(shp, lambda: (0,) * len(shp))
    return pl.pallas_call(
        body, name="pack_small_grads",
        in_specs=[full(a.shape) for a in args],
        out_specs=full((NDEV * REP_ROWS_DEV, LANES)),
        out_shape=jax.ShapeDtypeStruct((NDEV * REP_ROWS_DEV, LANES), F32),
    )(*_hbm(*args))


def _adam_math(wv, gv, mv, vv):
    c1 = 1.0 - B1 ** STEP
    c2 = 1.0 - B2 ** STEP
    m2 = B1 * mv + (1.0 - B1) * gv
    v2 = B2 * vv + (1.0 - B2) * (gv * gv)
    delta = (-LR) * ((m2 / c1) / (jnp.sqrt(v2 / c2) + AEPS) + WD * wv)
    return delta, m2, v2


def _adamw_small(rep_flat, w, m, v):
    offs = _rep_offsets()
    n = len(REP_NAMES)

    def body(*refs):
        g_ref = refs[0]
        w_refs = refs[1:1 + n]
        m_refs = refs[1 + n:1 + 2 * n]
        v_refs = refs[1 + 2 * n:1 + 3 * n]
        outs = refs[1 + 3 * n:]
        go, do, mo, vo = outs[:n], outs[n:2 * n], outs[2 * n:3 * n], outs[3 * n:]

        def emit(i, idx, gv):
            go[i][idx] = gv
            delta, m2, v2 = _adam_math(w_refs[i][idx], gv, m_refs[i][idx], v_refs[i][idx])
            do[i][idx] = delta
            mo[i][idx] = m2
            vo[i][idx] = v2

        for i in range(n):
            if i < 2:
                for b in range(NRB):
                    emit(i, b, g_ref[pl.ds(offs[i] + b * RBW, RBW), :])
            elif REP_NAMES[i] in ("q_norm", "k_norm"):
                emit(i, slice(None), g_ref[pl.ds(offs[i], NG), :])
            else:
                gv = jnp.concatenate([g_ref[pl.ds(offs[i] + k, 1), :] for k in range(REP_ROWS[i])], axis=1)
                emit(i, slice(None), gv)

    full = lambda shp: pl.BlockSpec(shp, lambda: (0,) * len(shp))
    pspecs = [full(a.shape) for a in w]
    pshapes = [jax.ShapeDtypeStruct(a.shape, F32) for a in w]
    res = pl.pallas_call(
        body, name="adamw_small",
        in_specs=[full(rep_flat.shape)] + pspecs * 3,
        out_specs=pspecs * 4, out_shape=pshapes * 4,
        compiler_params=_cp(None, VMEM_MID),
    )(*_hbm(rep_flat, *w, *m, *v))
    return res[:n], res[n:2 * n], res[2 * n:3 * n], res[3 * n:]


def _adamw_many(w, g, m, v, token):
    n = len(w)

    def body(*refs):
        for i in range(n):
            delta, m2, v2 = _adam_math(refs[i][...], refs[n + i][...], refs[2 * n + i][...], refs[3 * n + i][...])
            refs[4 * n + 1 + i][...] = delta
            refs[5 * n + 1 + i][...] = m2
            refs[6 * n + 1 + i][...] = v2

    full = lambda shp: pl.BlockSpec(shp, lambda: (0,) * len(shp))
    specs = [full(a.shape) for a in w]
    shapes = [jax.ShapeDtypeStruct(a.shape, F32) for a in w]
    res = pl.pallas_call(
        body, name="adamw_shards",
        in_specs=specs * 4 + [full((8, LANES))], out_specs=specs * 3, out_shape=shapes * 3,
        compiler_params=_cp(None, VMEM_MID),
    )(*_hbm(*w, *g, *m, *v, token))
    return res[:n], res[n:2 * n], res[2 * n:]


def kernel(x, p, norm_mix, w_in, b_in, conv_w, conv_b, w_rg_a, b_rg_a, w_rg_x, b_rg_x, lru_lambda, q_norm, k_norm, w_o_rnn, w_o_att, w_out, norm_ple, w_ple_gate, b_ple_gate, w_ple, loss_target, m_norm_mix, m_w_in, m_b_in, m_conv_w, m_conv_b, m_w_rg_a, m_b_rg_a, m_w_rg_x, m_b_rg_x, m_lru_lambda, m_q_norm, m_k_norm, m_w_o_rnn, m_w_o_att, m_w_out, m_norm_ple, m_w_ple_gate, m_b_ple_gate, m_w_ple, v_norm_mix, v_w_in, v_b_in, v_conv_w, v_conv_b, v_w_rg_a, v_b_rg_a, v_w_rg_x, v_b_rg_x, v_lru_lambda, v_q_norm, v_k_norm, v_w_o_rnn, v_w_o_att, v_w_out, v_norm_ple, v_w_ple_gate, v_b_ple_gate, v_w_ple):
    w = dict(norm_mix=norm_mix, w_in=w_in, b_in=b_in, conv_w=conv_w, conv_b=conv_b, w_rg_a=w_rg_a, b_rg_a=b_rg_a,
             w_rg_x=w_rg_x, b_rg_x=b_rg_x, lru_lambda=lru_lambda, q_norm=q_norm, k_norm=k_norm, w_o_rnn=w_o_rnn,
             w_o_att=w_o_att, w_out=w_out, norm_ple=norm_ple, w_ple_gate=w_ple_gate, b_ple_gate=b_ple_gate,
             w_ple=w_ple)
    m = dict(norm_mix=m_norm_mix, w_in=m_w_in, b_in=m_b_in, conv_w=m_conv_w, conv_b=m_conv_b, w_rg_a=m_w_rg_a,
             b_rg_a=m_b_rg_a, w_rg_x=m_w_rg_x, b_rg_x=m_b_rg_x, lru_lambda=m_lru_lambda, q_norm=m_q_norm,
             k_norm=m_k_norm, w_o_rnn=m_w_o_rnn, w_o_att=m_w_o_att, w_out=m_w_out, norm_ple=m_norm_ple,
             w_ple_gate=m_w_ple_gate, b_ple_gate=m_b_ple_gate, w_ple=m_w_ple)
    v = dict(norm_mix=v_norm_mix, w_in=v_w_in, b_in=v_b_in, conv_w=v_conv_w, conv_b=v_conv_b, w_rg_a=v_w_rg_a,
             b_rg_a=v_b_rg_a, w_rg_x=v_w_rg_x, b_rg_x=v_b_rg_x, lru_lambda=v_lru_lambda, q_norm=v_q_norm,
             k_norm=v_k_norm, w_o_rnn=v_w_o_rnn, w_o_att=v_w_o_att, w_out=v_w_out, norm_ple=v_norm_ple,
             w_ple_gate=v_w_ple_gate, b_ple_gate=v_b_ple_gate, w_ple=v_w_ple)
    names = list(w.keys())

    shards = [w_in[0].T.astype(BF16), w_o_rnn[0].astype(BF16), w_o_att[0].T.astype(BF16), w_out[0].astype(BF16),
              w_ple_gate[0].astype(BF16), w_ple[0].T.astype(BF16), conv_w[0]]
    pos = _my_pos()
    me, my_core, my_chip = _lin(pos), pos[2], _chip(pos)
    hbm_empty = lambda shp, dt: lax.empty(shp, dt)
    w_shard, conv_shard = shards[0], shards[6]
    shp = w_shard.shape
    entry_token, bufs, sems = _gather_start(
        [w_shard, hbm_empty((2,) + shp, BF16), hbm_empty((2, 2) + shp, BF16), conv_shard,
         hbm_empty((NDEV,) + conv_shard.shape, F32)],
        [("own", (0, 1)), ("near", (0, 2)), ("others", (3, 4))], norm_mix, "gather_start_near")
    w_src, own_l, near_l, conv_src, conv_l = bufs
    sem_own, sem_near, sem_conv = sems
    gather_out = {}

    def project(hn):
        w_thru, own = _gather_wait("own", *sem_own, [w_src, own_l], hn, "gather_wait_own")
        own = lax.dynamic_update_slice(own, w_shard[None], (my_core, 0, 0)).reshape(1, CHIP_COLS, D)
        chips = [jnp.stack([my_chip]), jnp.stack([my_chip ^ 1, my_chip ^ 2]), jnp.stack([my_chip ^ 3])]
        chips = [c.astype(jnp.int32) for c in chips]
        proj = _in_proj_chips(hn, own, b_in, chips[0], None, entry_token, "in_proj_own")
        w_thru, near = _gather_wait("near", *sem_near, [w_thru, near_l], proj, "gather_wait_near")
        near = _forward_to_sibling(near, "gather_forward_near")
        token, (w_thru, far_l), (sem_far,) = _gather_start(
            [w_thru, hbm_empty((2,) + shp, BF16)], [("far", (0, 1))], near, "gather_start_far")
        near = near.reshape(2, CHIP_COLS, D)
        proj = _in_proj_chips(hn, near, b_in, chips[1], proj, token, "in_proj_near")
        w_thru, far = _gather_wait("far", *sem_far, [w_thru, far_l], proj, "gather_wait_far")
        far = _forward_to_sibling(far[None], "gather_forward_far").reshape(1, CHIP_COLS, D)
        proj = _in_proj_chips(hn, far, b_in, chips[2], proj, token, "in_proj_far")
        conv_thru, conv_g = _gather_wait("others", *sem_conv, [conv_src, conv_l], proj, "gather_wait_conv")
        conv_g = lax.dynamic_update_slice(conv_g, conv_shard[None], (me, 0, 0))
        conv_f = conv_g.transpose(1, 0, 2).reshape(CONVW, DR)
        srcs = list(shards[1:6])
        token, obufs, (sem_out,) = _gather_start(
            srcs + [hbm_empty((NDEV,) + a.shape, BF16) for a in srcs], [("others", tuple(range(10)))], proj,
            "gather_start_out")
        gather_out.update(bufs=obufs, sems=sem_out)
        return proj, [own, near, far], jnp.concatenate(chips), conv_f, token

    def other_weights(after):
        obufs = _gather_wait("others", *gather_out["sems"], gather_out["bufs"], after, "gather_wait_out")
        full = [lax.dynamic_update_slice(a, s[None], (me, 0, 0)) for a, s in zip(obufs[5:], shards[1:6])]
        return [a.reshape((NDEV * a.shape[1], a.shape[2])) for a in full]

    def start_reduce(arrs, tag):
        parts = [a.reshape((NCHIP, 2, a.shape[0] // NDEV, a.shape[1])) for a in arrs]
        theirs = _exchange_within_chip(parts, "reduce_within_chip_" + tag)
        return _between_chips_start(_sum_pairs(parts, theirs, "sum_pairs_" + tag), "reduce_between_chips_start_" + tag)

    grad_x, pending_out, pending_in, small = _local_step(
        x.reshape(T, D), p.reshape(T, PLE), loss_target.reshape(T, D),
        project, other_weights,
        norm_mix, conv_b, w_rg_a[0], b_rg_a, w_rg_x[0], b_rg_x, lru_lambda, q_norm[0], k_norm[0],
        norm_ple, b_ple_gate, start_reduce, entry_token)

    rep_parts = _pack_small_grads(small).reshape(NDEV, REP_ROWS_DEV, LANES)
    conv_parts = small["conv_w"].reshape(CONVW, NDEV, DR // NDEV).transpose(1, 0, 2)
    smalls = [rep_parts, conv_parts]
    token, sbufs, (sem_x,) = _gather_start(
        smalls + [lax.empty(a.shape, F32) for a in smalls], [("exchange", (0, 1, 2, 3))], small["norm_mix"],
        "reduce_small_start")

    myq = _chip(_my_pos())
    own_in, recv_in = _between_chips_wait(pending_in, token, "reduce_between_chips_wait_in")
    w_in_res = _sum_chips_adamw(own_in[0], recv_in[0], w_in[0].T, m_w_in[0].T, v_w_in[0].T, token, 304, "adamw_w_in")
    sbufs = _gather_wait("exchange", *sem_x, sbufs, w_in_res[0], "reduce_small_wait")
    g_rep, g_conv = _sum_blocks_small(sbufs[:2], sbufs[2:], me, (False, False), "sum_small")
    token, gbufs, (sem_g,) = _gather_start(
        [g_rep, lax.empty((NDEV,) + g_rep.shape, F32)], [("others", (0, 1))], g_conv, "gather_small_start")
    own_out, recv_out = _between_chips_wait(pending_out, token, "reduce_between_chips_wait_out")
    g_o_rnn, g_o_att, g_out, g_pg, g_ple = _sum_blocks_small(
        own_out, recv_out, myq, (False, True, False, False, True), "sum_chips_out")

    grad, delta, new_m, new_v = {}, {}, {}, {}
    rest = ("w_o_rnn", "w_o_att", "w_out", "w_ple_gate", "w_ple", "conv_w")
    g_rest = [g_o_rnn, g_o_att, g_out, g_pg, g_ple, g_conv]
    rest_res = _adamw_many([w[n][0] for n in rest], g_rest, [m[n][0] for n in rest], [v[n][0] for n in rest], token)
    _, rep_all = _gather_wait("others", *sem_g, gbufs, rest_res[0][0], "gather_small_wait")
    rep_all = lax.dynamic_update_slice(rep_all, g_rep[None], (me, 0, 0)).reshape(NDEV * REP_ROWS_DEV, LANES)
    loss = rep_all[LOSS_ROW, 0]
    rep_shape = lambda a: a if a.ndim == 2 else a.reshape(a.shape[1:])
    res = _adamw_small(rep_all, [rep_shape(w[n]) for n in REP_NAMES], [rep_shape(m[n]) for n in REP_NAMES],
                       [rep_shape(v[n]) for n in REP_NAMES])
    for dst, vals in zip((grad, delta, new_m, new_v), res):
        for n, a in zip(REP_NAMES, vals):
            dst[n] = a.reshape(w[n].shape)
    grad["w_in"], delta["w_in"], new_m["w_in"], new_v["w_in"] = [a.T[None] for a in w_in_res]
    for n, a in zip(rest, g_rest):
        grad[n] = a[None]
    for dst, vals in zip((delta, new_m, new_v), rest_res):
        for n, a in zip(rest, vals):
            dst[n] = a[None]

    return (loss, grad_x.reshape(BL, S, D), *[grad[n] for n in names], *[delta[n] for n in names],
            *[new_m[n] for n in names], *[new_v[n] for n in names])
```

```python
import jax
import jax.numpy as jnp
from jax import lax
from jax.experimental import pallas as pl
from jax.experimental.pallas import tpu as pltpu

F32 = jnp.float32
BF16 = jnp.bfloat16

D = 1024
S = 2048
BL = 2
T = BL * S
NDEV = 8
NCHIP = 4
PLE = 256
DR = 1280
NRB = 10
RBW = 128
CONVW = 4
LRU_C = 8.0
HD = 128
NH = 4
PATTERNS = ((128, 1), (512, 4), (2048, 16))
NG = 3
ATT = NH * HD
GW = NG * ATT
NIN = 2 * DR + 3 * GW + ATT + 2 * D
OFF_ZR = DR
OFF_Q = 2 * DR
OFF_ZA = OFF_Q + 3 * GW
OFF_G = OFF_ZA + ATT
ROPE_THETA = 10000.0
EPS = 1e-6
SCALE = HD ** -0.5
NEG = -1e30
QB = 128
LANES = 128
CT = 512
NCT = NIN // CT
A_W = 2 * DR
C_W = ATT + 2 * D

LR, B1, B2, AEPS, WD, STEP = 0.001, 0.9, 0.999, 1e-08, 0.01, 10

NSHARD_IN = NIN // NDEV
REP_NAMES = ("w_rg_a", "w_rg_x", "norm_mix", "b_in", "conv_b", "b_rg_a", "b_rg_x", "lru_lambda", "q_norm",
             "k_norm", "norm_ple", "b_ple_gate")
REP_ROWS = (NRB * RBW, NRB * RBW, D // LANES, NIN // LANES, DR // LANES, DR // LANES, DR // LANES, DR // LANES,
            NG, NG, D // LANES, D // LANES)
REP_TOTAL_ROWS = sum(REP_ROWS)
REP_ROWS_DEV = 344
BIG_NAMES = ("w_in", "w_o_rnn", "w_o_att", "w_out", "w_ple_gate", "w_ple")

VMEM_BIG = 56 * 1024 * 1024
VMEM_MID = 40 * 1024 * 1024


def _cp(sem=None, vmem=None):
    return pltpu.CompilerParams(dimension_semantics=sem, vmem_limit_bytes=vmem)


def _hbm(*arrays):
    return [pltpu.with_memory_space_constraint(a, pltpu.HBM) for a in arrays]


def _dot(a, b):
    return jnp.dot(a, b, preferred_element_type=F32)


def _dot_nt(a, b):
    return lax.dot_general(a, b, (((1,), (1,)), ((), ())), preferred_element_type=F32)


def _dot_tn(a, b):
    return lax.dot_general(a, b, (((0,), (0,)), ((), ())), preferred_element_type=F32)


def _sigmoid(x):
    return jax.nn.sigmoid(x)


def _perm(j):
    jq = j - OFF_Q // CT
    inside = (j >= OFF_Q // CT) & (j < OFF_ZA // CT)
    return jnp.where(inside, OFF_Q // CT + (jq % 3) * 3 + jq // 3, j)


PIECES = ((0, A_W // CT), (OFF_Q // CT, GW // CT), (OFF_Q // CT + 3, GW // CT), (OFF_Q // CT + 6, GW // CT),
          (OFF_ZA // CT, C_W // CT))


def _rmsnorm_fwd(x, gain, token, tm=512):
    def body(x_ref, g_ref, _token, o_ref):
        xv = x_ref[...]
        var = jnp.mean(xv * xv, axis=-1, keepdims=True)
        o_ref[...] = (xv * lax.rsqrt(var + EPS) * g_ref[...]).astype(BF16)

    return pl.pallas_call(
        body, grid=(T // tm,), name="rmsnorm_fwd",
        in_specs=[pl.BlockSpec((tm, D), lambda i: (i, 0)), pl.BlockSpec((1, D), lambda i: (0, 0)),
                  pl.BlockSpec((8, LANES), lambda i: (0, 0))],
        out_specs=pl.BlockSpec((tm, D), lambda i: (i, 0)),
        out_shape=jax.ShapeDtypeStruct((T, D), BF16),
        compiler_params=_cp(("parallel",)),
    )(*_hbm(x, gain, token))


CHIP_COLS = NIN // NCHIP


def _in_proj_chips(hn, w_rows, bias, chips, proj, token, name, tm=1024):
    n = w_rows.shape[0]

    def body(chips_ref, a_ref, w_ref, b_ref, _token, *rest):
        o_ref = rest[-1]
        o_ref[...] = (_dot_nt(a_ref[...], w_ref[0]) + b_ref[...]).astype(BF16)

    in_specs = [pl.BlockSpec((tm, D), lambda s, i, ch: (i, 0)),
                pl.BlockSpec((1, CHIP_COLS, D), lambda s, i, ch: (s, 0, 0)),
                pl.BlockSpec((1, CHIP_COLS), lambda s, i, ch: (0, ch[s])),
                pl.BlockSpec((8, LANES), lambda s, i, ch: (0, 0))]
    args = [hn, w_rows, bias, token]
    aliases = {}
    if proj is not None:
        in_specs.append(pl.BlockSpec(memory_space=pl.ANY))
        args.append(proj)
        aliases = {5: 0}
    return pl.pallas_call(
        body, name=name,
        grid_spec=pltpu.PrefetchScalarGridSpec(
            num_scalar_prefetch=1, grid=(n, T // tm), in_specs=in_specs,
            out_specs=pl.BlockSpec((tm, CHIP_COLS), lambda s, i, ch: (i, ch[s]))),
        out_shape=jax.ShapeDtypeStruct((T, NIN), BF16),
        input_output_aliases=aliases,
        compiler_params=_cp(("arbitrary", "arbitrary"), VMEM_BIG),
    )(chips, *_hbm(*args))


def _grad_x(pieces, w_bufs, chips, token, x, dx1, gain, tm=512):
    nb = len(w_bufs)

    def body(chips_ref, a_ref, q_ref, k_ref, v_ref, c_ref, *rest):
        w_hbm = rest[:nb]
        x_ref, dx1_ref, g_ref, dx_ref, dg_ref, w = rest[nb + 1:]
        first = pl.program_id(0) == 0

        @pl.when(first)
        def _():
            s = 0
            for buf in w_hbm:
                for r in range(buf.shape[0]):
                    row = pl.multiple_of(chips_ref[s] * CHIP_COLS, 128)
                    pltpu.sync_copy(buf.at[r], w.at[pl.ds(row, CHIP_COLS), :])
                    s += 1

        acc = _dot(a_ref[...], w[pl.ds(0, A_W), :])
        for kind, p_ref in enumerate((q_ref, k_ref, v_ref)):
            for g in range(NG):
                row = OFF_Q + (3 * g + kind) * CT
                acc = acc + _dot(p_ref[:, g * CT:(g + 1) * CT], w[pl.ds(row, CT), :])
        dn = acc + _dot(c_ref[...], w[pl.ds(OFF_ZA, C_W), :])
        xv = x_ref[...]
        rstd = lax.rsqrt(jnp.mean(xv * xv, axis=-1, keepdims=True) + EPS)
        xh = xv * rstd
        dg = jnp.sum(dn * xh, axis=0, keepdims=True)
        gd = dn * g_ref[...]
        dx_ref[...] = dx1_ref[...] + rstd * (gd - xh * jnp.mean(gd * xh, axis=-1, keepdims=True))

        @pl.when(first)
        def _():
            dg_ref[...] = dg

        @pl.when(jnp.logical_not(first))
        def _():
            dg_ref[...] += dg

    tok = lambda wd: pl.BlockSpec((tm, wd), lambda i, ch: (i, 0))
    vec = lambda: pl.BlockSpec((1, D), lambda i, ch: (0, 0))
    return pl.pallas_call(
        body, name="grad_x",
        grid_spec=pltpu.PrefetchScalarGridSpec(
            num_scalar_prefetch=1, grid=(T // tm,),
            in_specs=[tok(A_W), tok(GW), tok(GW), tok(GW), tok(C_W)] + [pl.BlockSpec(memory_space=pl.ANY)] * nb
            + [pl.BlockSpec((8, LANES), lambda i, ch: (0, 0)), tok(D), tok(D), vec()],
            out_specs=[tok(D), vec()],
            scratch_shapes=[pltpu.VMEM((NIN, D), BF16)]),
        out_shape=[jax.ShapeDtypeStruct((T, D), F32), jax.ShapeDtypeStruct((1, D), F32)],
        compiler_params=_cp(("arbitrary",), VMEM_BIG),
    )(chips, *_hbm(*pieces, *w_bufs, token, x, dx1, gain))


def _dw_in(pieces, hn):
    def body(a_ref, q_ref, k_ref, v_ref, c_ref, h_hbm, o_ref, s_ref, h):
        j = pl.program_id(0)

        @pl.when(j == 0)
        def _():
            pltpu.sync_copy(h_hbm, h)

        def step(x_ref):
            xv = x_ref[...]
            o_ref[...] = _dot_tn(xv, h[...]).astype(BF16)
            s_ref[...] = jnp.sum(xv.astype(F32), axis=0, keepdims=True)

        for x_ref, (lo, n) in zip((a_ref, q_ref, k_ref, v_ref, c_ref), PIECES):
            pl.when((j >= lo) & (j < lo + n))(lambda x_ref=x_ref: step(x_ref))

    def piece_spec(lo, n):
        return pl.BlockSpec((T, CT), lambda j: (0, jnp.clip(j - lo, 0, n - 1)))

    return pl.pallas_call(
        body, grid=(NCT,), name="dw_in",
        in_specs=[piece_spec(lo, n) for lo, n in PIECES] + [pl.BlockSpec(memory_space=pl.ANY)],
        out_specs=[pl.BlockSpec((CT, D), lambda j: (_perm(j), 0)), pl.BlockSpec((1, CT), lambda j: (0, _perm(j)))],
        out_shape=[jax.ShapeDtypeStruct((NIN, D), BF16), jax.ShapeDtypeStruct((1, NIN), F32)],
        scratch_shapes=[pltpu.VMEM((T, D), BF16)],
        compiler_params=_cp(("arbitrary",), VMEM_BIG),
    )(*_hbm(*pieces, hn))


def _mm_tn(a, b, ta, tt, name):
    m = a.shape[1]
    n = b.shape[1]
    nt = T // tt

    def body(a_ref, b_ref, o_ref, acc):
        t = pl.program_id(1)
        p = _dot_tn(a_ref[...].astype(BF16), b_ref[...].astype(BF16))

        @pl.when(t == 0)
        def _():
            acc[...] = p

        @pl.when(t > 0)
        def _():
            acc[...] += p

        @pl.when(t == nt - 1)
        def _():
            o_ref[...] = acc[...].astype(BF16)

    return pl.pallas_call(
        body, grid=(m // ta, nt), name=name,
        in_specs=[pl.BlockSpec((tt, ta), lambda j, t: (t, j)), pl.BlockSpec((tt, n), lambda j, t: (t, 0))],
        out_specs=pl.BlockSpec((ta, n), lambda j, t: (j, 0)),
        out_shape=jax.ShapeDtypeStruct((m, n), BF16),
        scratch_shapes=[pltpu.VMEM((ta, n), F32)],
        compiler_params=_cp(("parallel", "arbitrary"), VMEM_MID),
    )(*_hbm(a, b))


def _row_iota():
    return lax.broadcasted_iota(jnp.int32, (S, RBW), 0)


SUBLANES = 8
N_SHIFT_BUFS = 4


class _Shifter:
    def __init__(self, bufs):
        self.bufs = bufs
        self.k = 0

    def _store(self, v, fill, front):
        b = self.bufs.at[self.k % N_SHIFT_BUFS]
        self.k += 1
        b[pl.ds(0 if front else SUBLANES + S, SUBLANES), :] = jnp.full((SUBLANES, RBW), fill, F32)
        b[pl.ds(SUBLANES, S), :] = v
        return b

    def down(self, v, ds, fill):
        b = self._store(v, fill, True)
        return [b[pl.ds(SUBLANES - d, S), :] for d in ds]

    def up(self, v, ds, fill):
        b = self._store(v, fill, False)
        return [b[pl.ds(SUBLANES + d, S), :] for d in ds]


def _shift_down(v, d, sh, fill):
    return sh.down(v, (d,), fill)[0]


def _shift_up(v, d, sh, fill):
    return sh.up(v, (d,), fill)[0]


def _scan_down(a, u, row):
    d = 1
    while d < S:
        last = 2 * d >= S
        if d < SUBLANES:
            u = a * _shift_down(u, d, row, 0.0) + u
            if not last:
                a = a * _shift_down(a, d, row, 1.0)
        else:
            u = jnp.concatenate([u[:d], a[d:] * u[:S - d] + u[d:]], axis=0)
            if not last:
                a = jnp.concatenate([a[:d], a[d:] * a[:S - d]], axis=0)
        d *= 2
    return u


def _scan_up(b, g, row):
    d = 1
    while d < S:
        last = 2 * d >= S
        if d < SUBLANES:
            g = g + b * _shift_up(g, d, row, 0.0)
            if not last:
                b = b * _shift_up(b, d, row, 0.0)
        else:
            g = jnp.concatenate([g[:S - d] + b[:S - d] * g[d:], g[S - d:]], axis=0)
            if not last:
                b = jnp.concatenate([b[:S - d] * b[d:], b[S - d:]], axis=0)
        d *= 2
    return g


def _softplus(x):
    return jnp.maximum(x, 0.0) + jnp.log1p(jnp.exp(-jnp.abs(x)))


def _rnn_gates(x, cw, cb, wa, ba, wx, bx, lam, row, pad):
    xs = pad.down(x, (1, 2, 3), 0.0)
    xc = cb + cw[3:4, :] * x
    for j in (1, 2, 3):
        xc = xc + cw[3 - j:4 - j, :] * xs[j - 1]
    xcb = xc.astype(BF16)
    r = _sigmoid(_dot(xcb, wa) + ba)
    i = _sigmoid(_dot(xcb, wx) + bx)
    sp = _softplus(-lam)
    log_a = (-LRU_C) * r * sp
    a = jnp.exp(log_a)
    mult = jnp.where(row == 0, 1.0, jnp.sqrt(jnp.tanh(-log_a) * (1.0 + a * a)))
    return xc, xcb, r, i, sp, a, mult, xs


def _rnn_fwd(proj3, conv_w, conv_b, wa, ba, wx, bx, lam, token):
    def body(x_ref, cw_ref, cb_ref, wa_ref, ba_ref, wx_ref, bx_ref, lam_ref, _token, h_ref, pad):
        row = _row_iota()
        sh = _Shifter(pad)
        x = x_ref[0].astype(F32)
        xc, _, _, i, _, a, mult, _ = _rnn_gates(x, cw_ref[...], cb_ref[...], wa_ref[0], ba_ref[...],
                                             wx_ref[0], bx_ref[...], lam_ref[...], row, sh)
        h_ref[0] = _scan_down(a, mult * (i * xc), sh)

    vec = lambda: pl.BlockSpec((1, RBW), lambda b, n: (0, n))
    mat = lambda: pl.BlockSpec((1, RBW, RBW), lambda b, n: (n, 0, 0))
    return pl.pallas_call(
        body, grid=(BL, NRB), name="rnn_fwd",
        in_specs=[pl.BlockSpec((1, S, RBW), lambda b, n: (b, 0, n)),
                  pl.BlockSpec((CONVW, RBW), lambda b, n: (0, n)),
                  vec(), mat(), vec(), mat(), vec(), vec(), pl.BlockSpec((8, LANES), lambda b, n: (0, 0))],
        out_specs=pl.BlockSpec((1, S, RBW), lambda b, n: (b, 0, n)),
        out_shape=jax.ShapeDtypeStruct((BL, S, DR), F32),
        scratch_shapes=[pltpu.VMEM((N_SHIFT_BUFS, S + 2 * SUBLANES, RBW), F32)],
        compiler_params=_cp(("parallel", "parallel"), VMEM_MID),
    )(*_hbm(proj3, conv_w, conv_b, wa, ba, wx, bx, lam, token))


def _rnn_bwd(proj3, h3, dh3, slab_a3, conv_w, conv_b, wa, ba, wx, bx, lam, token):
    def body(x_ref, h_ref, dh_ref, cw_ref, cb_ref, wa_ref, ba_ref, wx_ref, bx_ref, lam_ref, _alias, _token,
             dx_ref, dcw_ref, dcb_ref, dwa_ref, dba_ref, dwx_ref, dbx_ref, dlam_ref, pad):
        row = _row_iota()
        sh = _Shifter(pad)
        x = x_ref[0].astype(F32)
        cw = cw_ref[...]
        wa_v = wa_ref[0]
        wx_v = wx_ref[0]
        lam_v = lam_ref[...]
        xc, xcb, r, i, sp, a, mult, xs = _rnn_gates(x, cw, cb_ref[...], wa_v, ba_ref[...], wx_v, bx_ref[...], lam_v,
                                                    row, sh)
        h = h_ref[0]
        g = _scan_up(_shift_up(a, 1, sh, 0.0), dh_ref[0], sh)
        da = g * _shift_down(h, 1, sh, 0.0)
        dmult = jnp.where(row == 0, 0.0, g * (i * xc))
        gm = g * mult
        di = gm * xc
        dxc = gm * i
        dlog_a = da * a - dmult * (a * a) / mult
        dr = dlog_a * ((-LRU_C) * sp)
        dsp = jnp.sum(dlog_a * ((-LRU_C) * r), axis=0, keepdims=True)
        dlam = dsp * (-_sigmoid(-lam_v))
        dpa = dr * r * (1.0 - r)
        dpx = di * i * (1.0 - i)
        dpab = dpa.astype(BF16)
        dpxb = dpx.astype(BF16)
        dwa = _dot_tn(xcb, dpab)
        dwx = _dot_tn(xcb, dpxb)
        dba = jnp.sum(dpa, axis=0, keepdims=True)
        dbx = jnp.sum(dpx, axis=0, keepdims=True)
        dxc = dxc + _dot_nt(dpab, wa_v) + _dot_nt(dpxb, wx_v)
        dcb = jnp.sum(dxc, axis=0, keepdims=True)
        dx = cw[3:4, :] * dxc
        dcw_rows = [None] * CONVW
        dcw_rows[3] = jnp.sum(dxc * x, axis=0, keepdims=True)
        dxc_up = sh.up(dxc, (1, 2, 3), 0.0)
        for j in (1, 2, 3):
            dx = dx + cw[3 - j:4 - j, :] * dxc_up[j - 1]
            dcw_rows[3 - j] = jnp.sum(dxc * xs[j - 1], axis=0, keepdims=True)
        dx_ref[0] = dx.astype(BF16)
        dcw = jnp.concatenate(dcw_rows, axis=0)
        first = pl.program_id(1) == 0

        @pl.when(first)
        def _():
            dcw_ref[...] = dcw
            dcb_ref[...] = dcb
            dwa_ref[0] = dwa
            dba_ref[...] = dba
            dwx_ref[0] = dwx
            dbx_ref[...] = dbx
            dlam_ref[...] = dlam

        @pl.when(jnp.logical_not(first))
        def _():
            dcw_ref[...] += dcw
            dcb_ref[...] += dcb
            dwa_ref[0] += dwa
            dba_ref[...] += dba
            dwx_ref[0] += dwx
            dbx_ref[...] += dbx
            dlam_ref[...] += dlam

    slab = lambda: pl.BlockSpec((1, S, RBW), lambda n, b: (b, 0, n))
    vec = lambda: pl.BlockSpec((1, RBW), lambda n, b: (0, n))
    mat = lambda: pl.BlockSpec((1, RBW, RBW), lambda n, b: (n, 0, 0))
    taps = lambda: pl.BlockSpec((CONVW, RBW), lambda n, b: (0, n))
    vshape = jax.ShapeDtypeStruct((1, DR), F32)
    mshape = jax.ShapeDtypeStruct((NRB, RBW, RBW), F32)
    return pl.pallas_call(
        body, grid=(NRB, BL), name="rnn_bwd",
        in_specs=[slab(), slab(), slab(), taps(), vec(), mat(), vec(), mat(), vec(), vec(),
                  pl.BlockSpec(memory_space=pl.ANY), pl.BlockSpec((8, LANES), lambda n, b: (0, 0))],
        out_specs=[slab(), taps(), vec(), mat(), vec(), mat(), vec(), vec()],
        out_shape=[jax.ShapeDtypeStruct((BL, S, A_W), BF16), jax.ShapeDtypeStruct((CONVW, DR), F32),
                   vshape, mshape, vshape, mshape, vshape, vshape],
        input_output_aliases={10: 0},
        scratch_shapes=[pltpu.VMEM((N_SHIFT_BUFS, S + 2 * SUBLANES, RBW), F32)],
        compiler_params=_cp(("parallel", "arbitrary"), 48 * 1024 * 1024),
    )(*_hbm(proj3, h3, dh3, conv_w, conv_b, wa, ba, wx, bx, lam, slab_a3, token))


NQB = S // QB


def _rms_head(t, gain):
    rstd = lax.rsqrt(jnp.mean(t * t, axis=-1, keepdims=True) + EPS)
    return t * rstd * gain


def _rope(t, cs, sn):
    return t * cs + pltpu.roll(t, HD // 2, 1) * sn


def _rope_t(dy, cs, sn):
    return dy * cs - pltpu.roll(dy, HD // 2, 1) * sn


def _bdot_nt(a, b):
    return lax.dot_general(a, b, (((2,), (2,)), ((0,), (0,))), preferred_element_type=F32)


def _bdot(a, b):
    return lax.dot_general(a, b, (((2,), (1,)), ((0,), (0,))), preferred_element_type=F32)


def _bdot_tn(a, b):
    return lax.dot_general(a, b, (((1,), (1,)), ((0,), (0,))), preferred_element_type=F32)


STRIDE_MAX = 4


def _permute(buf, x, dil, dst, off=0):
    ln = S // dil
    if dil == 1:
        dst[pl.ds(off, S), :] = x.astype(dst.dtype)
        return
    buf[0] = x
    if dil <= STRIDE_MAX:
        for c in range(dil):
            dst[pl.ds(off + c * ln, ln), :] = buf.at[0][pl.ds(c, ln, stride=dil), :].astype(dst.dtype)
        return
    f, r = STRIDE_MAX, dil // STRIDE_MAX
    part = S // f
    for c1 in range(f):
        buf.at[1][pl.ds(c1 * part, part), :] = buf.at[0][pl.ds(c1, part, stride=f), :]
    for c1 in range(f):
        for c2 in range(r):
            dst[pl.ds(off + (c1 + f * c2) * ln, ln), :] = (
                buf.at[1][pl.ds(c1 * part + c2, ln, stride=r), :].astype(dst.dtype))


def _unpermute(buf, xp, dil, dst):
    ln = S // dil
    if dil == 1:
        dst[...] = xp
        return
    if dil <= STRIDE_MAX:
        for c in range(dil):
            dst[pl.ds(c, ln, stride=dil), :] = xp[c * ln:(c + 1) * ln]
        return
    f, r = STRIDE_MAX, dil // STRIDE_MAX
    part = S // f
    for c1 in range(f):
        for c2 in range(r):
            c = c1 + f * c2
            buf.at[1][pl.ds(c1 * part + c2, ln, stride=r), :] = xp[c * ln:(c + 1) * ln]
    for c1 in range(f):
        dst[pl.ds(c1, part, stride=f), :] = buf[1, pl.ds(c1 * part, part), :]


def _blocks3(ref, off=0):
    return ref[pl.ds(off, S), :].reshape(NQB, QB, HD)


def _att_prep(q_ref, k_ref, v_ref, cos_ref, sin_ref, qn, kn, dil, nat, qs, ksp, vsp):
    cs = cos_ref[...]
    sn = sin_ref[...]
    zero = jnp.zeros((QB, HD), BF16)
    ksp[pl.ds(0, QB), :] = zero
    vsp[pl.ds(0, QB), :] = zero
    _permute(nat, _rope(_rms_head(q_ref[0].astype(F32), qn), cs, sn), dil, qs)
    _permute(nat, _rope(_rms_head(k_ref[0].astype(F32), kn), cs, sn), dil, ksp, QB)
    _permute(nat, v_ref[0].astype(F32), dil, vsp, QB)


def _att_scores(qs, ksp, dil):
    nb = S // dil // QB
    q3 = _blocks3(qs)
    shape = (NQB, QB, QB)
    qi = lax.broadcasted_iota(jnp.int32, shape, 1)
    kj = lax.broadcasted_iota(jnp.int32, shape, 2)
    s_c = jnp.where(qi >= kj, _bdot_nt(q3, _blocks3(ksp, QB)) * SCALE, NEG)
    if nb == 1:
        return q3, s_c, None
    jj = lax.broadcasted_iota(jnp.int32, shape, 0)
    ok = (kj >= qi) & ((jj & (nb - 1)) != 0)
    s_p = jnp.where(ok, _bdot_nt(q3, _blocks3(ksp)) * SCALE, NEG)
    return q3, s_c, s_p


def _qkv_spec(kind, g):
    base = OFF_Q // HD + (3 * g + kind) * NH
    return pl.BlockSpec((1, S, HD), lambda b, h: (b, 0, base + h))


def _attn_fwd(proj3, cos_t, sin_t, q_norm, k_norm):
    def body(*refs):
        qkv_refs = refs[:9]
        (cos_ref, sin_ref, qn_ref, kn_ref, att_ref, lse_ref, w_ref, qp_ref, kp_ref, vp_ref,
         nat, qs, ksp, vsp, og) = refs[9:]
        for g, (window, dil) in enumerate(PATTERNS):
            q_ref, k_ref, v_ref = qkv_refs[3 * g:3 * g + 3]
            _att_prep(q_ref, k_ref, v_ref, cos_ref, sin_ref, qn_ref[g:g + 1, :], kn_ref[g:g + 1, :], dil,
                      nat, qs, ksp, vsp)
            qp_ref[g, 0] = qs[...]
            kp_ref[g, 0] = ksp[pl.ds(QB, S), :]
            vp_ref[g, 0] = vsp[pl.ds(QB, S), :]
            _, s_c, s_p = _att_scores(qs, ksp, dil)
            m = jnp.max(s_c, axis=-1, keepdims=True)
            if s_p is not None:
                m = jnp.maximum(m, jnp.max(s_p, axis=-1, keepdims=True))
            e_c = jnp.exp(s_c - m)
            den = jnp.sum(e_c, axis=-1, keepdims=True)
            o = _bdot(e_c.astype(BF16), _blocks3(vsp, QB))
            if s_p is not None:
                e_p = jnp.exp(s_p - m)
                den = den + jnp.sum(e_p, axis=-1, keepdims=True)
                o = o + _bdot(e_p.astype(BF16), _blocks3(vsp))
            _unpermute(nat, (o / den).reshape(S, HD), dil, og.at[g])
            _unpermute(nat, jnp.broadcast_to(m + jnp.log(den), (NQB, QB, HD)).reshape(S, HD), dil,
                       lse_ref.at[g, 0])
        l0 = lse_ref[0, 0]
        l1 = lse_ref[1, 0]
        l2 = lse_ref[2, 0]
        mx = jnp.maximum(jnp.maximum(l0, l1), l2)
        e0 = jnp.exp(l0 - mx)
        e1 = jnp.exp(l1 - mx)
        e2 = jnp.exp(l2 - mx)
        inv = 1.0 / (e0 + e1 + e2)
        w0 = e0 * inv
        w1 = e1 * inv
        w2 = e2 * inv
        w_ref[0, 0] = w0
        w_ref[1, 0] = w1
        w_ref[2, 0] = w2
        att_ref[0] = w0 * og[0] + w1 * og[1] + w2 * og[2]

    in_specs = [_qkv_spec(kind, g) for g in range(NG) for kind in range(3)]
    in_specs += [pl.BlockSpec((S, HD), lambda b, h: (0, 0)), pl.BlockSpec((S, HD), lambda b, h: (0, 0)),
                 pl.BlockSpec((NG, HD), lambda b, h: (0, 0)), pl.BlockSpec((NG, HD), lambda b, h: (0, 0))]
    stat = lambda: pl.BlockSpec((NG, 1, S, HD), lambda b, h: (0, b, 0, h))
    return pl.pallas_call(
        body, grid=(BL, NH), name="attn_fwd",
        in_specs=in_specs,
        out_specs=[pl.BlockSpec((1, S, HD), lambda b, h: (b, 0, h)), stat(), stat(), stat(), stat(), stat()],
        out_shape=[jax.ShapeDtypeStruct((BL, S, ATT), F32),
                   jax.ShapeDtypeStruct((NG, BL, S, ATT), F32),
                   jax.ShapeDtypeStruct((NG, BL, S, ATT), F32)]
        + [jax.ShapeDtypeStruct((NG, BL, S, ATT), BF16)] * 3,
        scratch_shapes=[pltpu.VMEM((2, S, HD), F32), pltpu.VMEM((S, HD), BF16), pltpu.VMEM((S + QB, HD), BF16),
                        pltpu.VMEM((S + QB, HD), BF16), pltpu.VMEM((NG, S, HD), F32)],
        compiler_params=_cp(("parallel", "parallel"), VMEM_BIG),
    )(*_hbm(*([proj3] * 9), cos_t, sin_t, q_norm, k_norm))


def _attn_bwd_group(g, proj3, cos_t, sin_t, qn_g, kn_g, lse, wts, qkv_p, datt3, sbar3, slabs):
    dil = PATTERNS[g][1]
    n_alias = 0 if slabs is None else 3

    def norm_rope_bwd(dpost, raw, gain, cs, sn):
        dn = _rope_t(dpost, cs, sn)
        rstd = lax.rsqrt(jnp.mean(raw * raw, axis=-1, keepdims=True) + EPS)
        xh = raw * rstd
        dgain = jnp.sum(dn * xh, axis=0, keepdims=True)
        gd = dn * gain
        draw = rstd * (gd - xh * jnp.mean(gd * xh, axis=-1, keepdims=True))
        return draw, dgain

    def body(*refs):
        (q_ref, k_ref, qp_ref, kp_ref, vp_ref, cos_ref, sin_ref, qn_ref, kn_ref, lse_ref, w_ref, datt_ref,
         sbar_ref) = refs[:13]
        (dq_ref, dk_ref, dv_ref, dqn_ref, dkn_ref, nat, ksp, vsp, dos, cvp, lsp, acc) = refs[13 + n_alias:]
        qn = qn_ref[...]
        kn = kn_ref[...]
        cs = cos_ref[...]
        sn = sin_ref[...]
        qs = qp_ref.at[0, 0]
        zero = jnp.zeros((QB, HD), BF16)
        ksp[pl.ds(0, QB), :] = zero
        vsp[pl.ds(0, QB), :] = zero
        ksp[pl.ds(QB, S), :] = kp_ref[0, 0]
        vsp[pl.ds(QB, S), :] = vp_ref[0, 0]
        wv = w_ref[0, 0]
        _permute(nat, wv * datt_ref[0], dil, dos)
        _permute(nat, wv * sbar_ref[0], dil, cvp)
        _permute(nat, lse_ref[0, 0], dil, lsp)
        q3, s_c, s_p = _att_scores(qs, ksp, dil)
        do3 = _blocks3(dos)
        lse3 = _blocks3(lsp)[:, :, 0:1]
        cv3 = _blocks3(cvp)[:, :, 0:1]
        p_c = jnp.exp(s_c - lse3)
        ds_c = (p_c * (_bdot_nt(do3, _blocks3(vsp, QB)) - cv3)).astype(BF16)
        dq = _bdot(ds_c, _blocks3(ksp, QB))
        acc[0] = _bdot_tn(ds_c, q3).reshape(S, HD)
        acc[1] = _bdot_tn(p_c.astype(BF16), do3).reshape(S, HD)
        if s_p is not None:
            p_p = jnp.exp(s_p - lse3)
            ds_p = (p_p * (_bdot_nt(do3, _blocks3(vsp)) - cv3)).astype(BF16)
            dq = dq + _bdot(ds_p, _blocks3(ksp))
            early = pl.ds(0, S - QB)
            acc[0, early, :] += _bdot_tn(ds_p, q3).reshape(S, HD)[QB:]
            acc[1, early, :] += _bdot_tn(p_p.astype(BF16), do3).reshape(S, HD)[QB:]
        _unpermute(nat, (dq * SCALE).reshape(S, HD), dil, nat.at[0])
        draw, dqn = norm_rope_bwd(nat[0], q_ref[0].astype(F32), qn, cs, sn)
        dq_ref[0] = draw.astype(BF16)
        _unpermute(nat, acc[0] * SCALE, dil, nat.at[0])
        draw, dkn = norm_rope_bwd(nat[0], k_ref[0].astype(F32), kn, cs, sn)
        dk_ref[0] = draw.astype(BF16)
        _unpermute(nat, acc[1], dil, nat.at[0])
        dv_ref[0] = nat[0].astype(BF16)
        first = (pl.program_id(0) == 0) & (pl.program_id(1) == 0)

        @pl.when(first)
        def _():
            dqn_ref[...] = dqn
            dkn_ref[...] = dkn

        @pl.when(jnp.logical_not(first))
        def _():
            dqn_ref[...] += dqn
            dkn_ref[...] += dkn

    full = lambda r: pl.BlockSpec((r, HD), lambda b, h: (0, 0))
    stat = lambda: pl.BlockSpec((1, 1, S, HD), lambda b, h: (g, b, 0, h))
    slab = lambda: pl.BlockSpec((1, S, HD), lambda b, h: (b, 0, h))
    out_slab = lambda: pl.BlockSpec((1, S, HD), lambda b, h: (b, 0, g * NH + h))
    big = jax.ShapeDtypeStruct((BL, S, GW), BF16)
    vecs = jax.ShapeDtypeStruct((1, HD), F32)
    in_specs = [_qkv_spec(0, g), _qkv_spec(1, g), stat(), stat(), stat(), full(S), full(S), full(1), full(1),
                stat(), stat(), slab(), slab()]
    args = [proj3, proj3, *qkv_p, cos_t, sin_t, qn_g, kn_g, lse, wts, datt3, sbar3]
    aliases = {}
    if slabs is not None:
        in_specs += [pl.BlockSpec(memory_space=pl.ANY)] * 3
        args += list(slabs)
        aliases = {13: 0, 14: 1, 15: 2}
    return pl.pallas_call(
        body, grid=(BL, NH), name="attn_bwd_g%d" % g,
        in_specs=in_specs,
        out_specs=[out_slab(), out_slab(), out_slab(), full(1), full(1)],
        out_shape=[big, big, big, vecs, vecs],
        scratch_shapes=[pltpu.VMEM((2, S, HD), F32), pltpu.VMEM((S + QB, HD), BF16),
                        pltpu.VMEM((S + QB, HD), BF16), pltpu.VMEM((S, HD), BF16), pltpu.VMEM((S, HD), F32),
                        pltpu.VMEM((S, HD), F32), pltpu.VMEM((2, S, HD), F32)],
        input_output_aliases=aliases,
        compiler_params=_cp(("arbitrary", "arbitrary"), VMEM_BIG),
    )(*_hbm(*args))


def _tail(x, proj, h, att, p, tgt, w_o_rnn, w_o_att_t, w_out, w_pg, w_ple_t, norm_ple, b_pg, tm=256):
    nt = T // tm
    inv_d = 1.0 / D

    def body(x_ref, h_ref, zr_ref, att_ref, za_ref, g0a_ref, g0b_ref, g1a_ref, g1b_ref, p_ref, tgt_ref,
             np_ref, bpg_ref, wor_hbm, woa_hbm, wout_hbm, wpg_hbm, wple_hbm,
             dx1_ref, merged_ref, n1_ref, dpre_ref, dpe_ref, dyr_ref, dya_ref, slab_a_ref, slab_c_ref, dh_ref,
             datt_ref, sbar_ref, yrnn_ref, yatt_ref, loss_ref, dnp_ref, dbpg_ref,
             wor, woa, wout, wpg, wple):
        first = pl.program_id(0) == 0

        @pl.when(first)
        def _():
            pltpu.sync_copy(wor_hbm, wor)
            pltpu.sync_copy(woa_hbm, woa)
            pltpu.sync_copy(wout_hbm, wout)
            pltpu.sync_copy(wpg_hbm, wpg)
            pltpu.sync_copy(wple_hbm, wple)

        xv = x_ref[...]
        hv = h_ref[...]
        zr = zr_ref[...].astype(F32)
        av = att_ref[...]
        za = za_ref[...].astype(F32)
        szr = _sigmoid(zr)
        silu_r = zr * szr
        yrnn_b = (hv * silu_r).astype(BF16)
        sza = _sigmoid(za)
        silu_a = za * sza
        yatt_b = (av * silu_a).astype(BF16)
        yrnn_ref[...] = yrnn_b
        yatt_ref[...] = yatt_b
        yr = _dot(yrnn_b, wor[...])
        ya = _dot_nt(yatt_b, woa[...])
        g0 = _sigmoid(jnp.concatenate([g0a_ref[...], g0b_ref[...]], axis=1).astype(F32))
        g1 = _sigmoid(jnp.concatenate([g1a_ref[...], g1b_ref[...]], axis=1).astype(F32))
        merged_b = (g0 * yr + g1 * ya).astype(BF16)
        merged_ref[...] = merged_b
        x1 = xv + _dot(merged_b, wout[...])
        rstd = lax.rsqrt(jnp.mean(x1 * x1, axis=-1, keepdims=True) + EPS)
        xh = x1 * rstd
        npl = np_ref[...]
        n1_b = (xh * npl).astype(BF16)
        n1_ref[...] = n1_b
        pg = _sigmoid(_dot(n1_b, wpg[...]) + bpg_ref[...])
        pe = _dot_nt(p_ref[...].astype(BF16), wple[...])
        err = x1 + pg * pe - tgt_ref[...]
        loss_t = 0.5 * inv_d * jnp.sum(err * err)
        dy = err * inv_d
        dpe_ref[...] = (dy * pg).astype(BF16)
        dpre = dy * pe * pg * (1.0 - pg)
        dpre_b = dpre.astype(BF16)
        dpre_ref[...] = dpre_b
        dn1 = _dot_nt(dpre_b, wpg[...])
        dnp = jnp.sum(dn1 * xh, axis=0, keepdims=True)
        dbpg = jnp.sum(dpre, axis=0, keepdims=True)
        gd = dn1 * npl
        dx1 = dy + rstd * (gd - xh * jnp.mean(gd * xh, axis=-1, keepdims=True))
        dx1_ref[...] = dx1
        dmerged = _dot_nt(dx1.astype(BF16), wout[...])
        dyr_b = (dmerged * g0).astype(BF16)
        dya_b = (dmerged * g1).astype(BF16)
        dyr_ref[...] = dyr_b
        dya_ref[...] = dya_b
        slab_c_ref[:, ATT:ATT + D] = (dmerged * yr * g0 * (1.0 - g0)).astype(BF16)
        slab_c_ref[:, ATT + D:ATT + 2 * D] = (dmerged * ya * g1 * (1.0 - g1)).astype(BF16)
        dyrnn = _dot_nt(dyr_b, wor[...])
        dyatt = _dot(dya_b, woa[...])
        dh_ref[...] = dyrnn * silu_r
        slab_a_ref[...] = (dyrnn * hv * szr * (1.0 + zr * (1.0 - szr))).astype(BF16)
        datt = dyatt * silu_a
        datt_ref[...] = datt
        slab_c_ref[:, 0:ATT] = (dyatt * av * sza * (1.0 + za * (1.0 - sza))).astype(BF16)
        da = datt * av
        for hh in range(NH):
            seg = slice(hh * HD, (hh + 1) * HD)
            sbar_ref[:, seg] = jnp.broadcast_to(jnp.sum(da[:, seg], axis=-1, keepdims=True), (tm, HD))

        @pl.when(first)
        def _():
            loss_ref[...] = jnp.full((8, LANES), loss_t, F32)
            dnp_ref[...] = dnp
            dbpg_ref[...] = dbpg

        @pl.when(jnp.logical_not(first))
        def _():
            loss_ref[...] += jnp.full((8, LANES), loss_t, F32)
            dnp_ref[...] += dnp
            dbpg_ref[...] += dbpg

    tok = lambda w: pl.BlockSpec((tm, w), lambda i: (i, 0))
    col = lambda w, blk: pl.BlockSpec((tm, w), lambda i: (i, blk))
    vec = lambda: pl.BlockSpec((1, D), lambda i: (0, 0))
    hbm = lambda: pl.BlockSpec(memory_space=pl.ANY)
    gb = OFF_G // 512
    in_specs = [tok(D), tok(DR), col(DR, 1), tok(ATT), col(ATT, OFF_ZA // ATT),
                col(512, gb), col(512, gb + 1), col(512, gb + 2), col(512, gb + 3),
                tok(PLE), tok(D), vec(), vec(), hbm(), hbm(), hbm(), hbm(), hbm()]
    sh = lambda w, dt: jax.ShapeDtypeStruct((T, w), dt)
    out_shape = [sh(D, F32), sh(D, BF16), sh(D, BF16), sh(D, BF16), sh(D, BF16), sh(D, BF16), sh(D, BF16),
                 sh(A_W, BF16), sh(C_W, BF16), sh(DR, F32), sh(ATT, F32), sh(ATT, F32),
                 sh(DR, BF16), sh(ATT, BF16),
                 jax.ShapeDtypeStruct((8, LANES), F32), jax.ShapeDtypeStruct((1, D), F32),
                 jax.ShapeDtypeStruct((1, D), F32)]
    out_specs = [tok(D), tok(D), tok(D), tok(D), tok(D), tok(D), tok(D), col(DR, 1), tok(C_W), tok(DR),
                 tok(ATT), tok(ATT), tok(DR), tok(ATT),
                 pl.BlockSpec((8, LANES), lambda i: (0, 0)), vec(), vec()]
    return pl.pallas_call(
        body, grid=(nt,), name="tail_fwd_bwd",
        in_specs=in_specs, out_specs=out_specs, out_shape=out_shape,
        scratch_shapes=[pltpu.VMEM((DR, D), BF16), pltpu.VMEM((D, ATT), BF16), pltpu.VMEM((D, D), BF16),
                        pltpu.VMEM((D, D), BF16), pltpu.VMEM((D, PLE), BF16)],
        compiler_params=_cp(("arbitrary",), VMEM_BIG),
    )(*_hbm(x, h, proj, att, proj, proj, proj, proj, proj, p, tgt, norm_ple, b_pg, w_o_rnn, w_o_att_t, w_out, w_pg,
            w_ple_t))


def _input_norm_bwd(x, dhn, dx1, gain, tm=512):
    def body(x_ref, dhn_ref, dx1_ref, g_ref, dx_ref, dg_ref):
        xv = x_ref[...]
        rstd = lax.rsqrt(jnp.mean(xv * xv, axis=-1, keepdims=True) + EPS)
        xh = xv * rstd
        dn = dhn_ref[...]
        dg = jnp.sum(dn * xh, axis=0, keepdims=True)
        gd = dn * g_ref[...]
        dx_ref[...] = dx1_ref[...] + rstd * (gd - xh * jnp.mean(gd * xh, axis=-1, keepdims=True))
        first = pl.program_id(0) == 0

        @pl.when(first)
        def _():
            dg_ref[...] = dg

        @pl.when(jnp.logical_not(first))
        def _():
            dg_ref[...] += dg

    tok = lambda: pl.BlockSpec((tm, D), lambda i: (i, 0))
    vec = lambda: pl.BlockSpec((1, D), lambda i: (0, 0))
    return pl.pallas_call(
        body, grid=(T // tm,), name="input_norm_bwd",
        in_specs=[tok(), tok(), tok(), vec()], out_specs=[tok(), vec()],
        out_shape=[jax.ShapeDtypeStruct((T, D), F32), jax.ShapeDtypeStruct((1, D), F32)],
        compiler_params=_cp(("arbitrary",), VMEM_MID),
    )(*_hbm(x, dhn, dx1, gain))


def _rope_tables():
    pos = jnp.arange(S, dtype=F32)
    inv_freq = ROPE_THETA ** (-jnp.arange(0, HD, 2, dtype=F32) / HD)
    ang = pos[:, None] * inv_freq[None, :]
    cos, sin = jnp.cos(ang), jnp.sin(ang)
    return jnp.concatenate([cos, cos], axis=1), jnp.concatenate([-sin, sin], axis=1)


def _local_step(x, p, tgt, project, other_weights, norm_mix, conv_b,
                w_rg_a, b_rg_a, w_rg_x, b_rg_x, lam, q_norm, k_norm, norm_ple, b_pg, start_reduce=None,
                entry_token=None):
    if start_reduce is None:
        start_reduce = lambda arrs, tag: (jnp.zeros((8, LANES), F32), arrs)
    if entry_token is None:
        entry_token = jnp.zeros((8, LANES), F32)
    cos_t, sin_t = _rope_tables()
    wa_b = w_rg_a.astype(BF16)
    wx_b = w_rg_x.astype(BF16)

    hn = _rmsnorm_fwd(x, norm_mix, entry_token)
    proj, w_bufs, chips, conv_w, token = project(hn)
    proj3 = proj.reshape(BL, S, NIN)
    h3 = _rnn_fwd(proj3, conv_w, conv_b, wa_b, b_rg_a, wx_b, b_rg_x, lam, token)
    att3, lse, wts, *qkv_p = _attn_fwd(proj3, cos_t, sin_t, q_norm, k_norm)
    w_o_rnn, w_o_att_t, w_out, w_pg, w_ple_t = other_weights(att3)
    (dx1, merged, n1, dpre, dpe, dyr, dya, slab_a, slab_c, dh, datt, sbar, yrnn, yatt, loss8, dnp, dbpg) = _tail(
        x, proj, h3.reshape(T, DR), att3.reshape(T, ATT), p, tgt, w_o_rnn, w_o_att_t, w_out, w_pg, w_ple_t,
        norm_ple, b_pg)

    token, pending_out = start_reduce([
        _mm_tn(yrnn, dyr, 640, 2048, "dw_o_rnn"),
        _mm_tn(dya, yatt, 512, 2048, "dw_o_att_t"),
        _mm_tn(merged, dx1, 512, 2048, "dw_out"),
        _mm_tn(n1, dpre, 512, 2048, "dw_ple_gate"),
        _mm_tn(dpe, p, 512, 2048, "dw_ple_t")], "out")

    slab_a3, dcw, dcb, dwa, dba, dwx, dbx, dlam = _rnn_bwd(
        proj3, h3, dh.reshape(BL, S, DR), slab_a.reshape(BL, S, A_W), conv_w, conv_b, wa_b, b_rg_a, wx_b, b_rg_x, lam,
        token)
    datt3 = datt.reshape(BL, S, ATT)
    sbar3 = sbar.reshape(BL, S, ATT)
    slabs = None
    dqn = []
    dkn = []
    for g in range(NG):
        dq, dk, dv, dqn_g, dkn_g = _attn_bwd_group(g, proj3, cos_t, sin_t, q_norm[g:g + 1], k_norm[g:g + 1],
                                                   lse, wts, qkv_p, datt3, sbar3, slabs)
        slabs = (dq, dk, dv)
        dqn.append(dqn_g)
        dkn.append(dkn_g)
    pieces = [slab_a3.reshape(T, A_W)] + [t.reshape(T, GW) for t in slabs] + [slab_c]
    dw_in_t, db_in = _dw_in(pieces, hn)
    token, pending_in = start_reduce([dw_in_t], "in")
    grad_x, dnm = _grad_x(pieces, w_bufs, chips, token, x, dx1, norm_mix)

    small = dict(w_rg_a=dwa, w_rg_x=dwx, norm_mix=dnm, b_in=db_in, conv_b=dcb, b_rg_a=dba, b_rg_x=dbx,
                 lru_lambda=dlam, q_norm=dqn, k_norm=dkn, norm_ple=dnp, b_ple_gate=dbpg, conv_w=dcw, loss=loss8)
    return grad_x, pending_out, pending_in, small


MESH = pl.DeviceIdType.MESH
HBM_SPEC = pl.BlockSpec(memory_space=pl.ANY)


def _my_pos():
    return lax.axis_index("x"), lax.axis_index("y"), lax.axis_index("c")


def _flip(pos, k):
    x, y, c = pos
    return (1 - x if k & 4 else x, 1 - y if k & 2 else y, 1 - c if k & 1 else c)


def _lin(pos):
    return 4 * pos[0] + 2 * pos[1] + pos[2]


def _chip(pos):
    return 2 * pos[0] + pos[1]


def _all_gather_two_level(shards, name):
    na = len(shards)

    def body(*refs):
        x_refs = refs[:na]
        out_refs = refs[na:2 * na]
        send_sems, recv_sems, local_sems = refs[2 * na:]
        me = _my_pos()
        sibling = _flip(me, 1)
        chips = [_flip(me, 4), _flip(me, 2), _flip(me, 6)]

        def copy(i, k, block, to, from_x=False):
            dst = out_refs[i].at[_lin(block)]
            return pltpu.make_async_remote_copy(
                src_ref=x_refs[i] if from_x else dst, dst_ref=dst,
                send_sem=send_sems.at[7 * i + k], recv_sem=recv_sems.at[7 * i + k], device_id=to, device_id_type=MESH)

        started = []
        for i in range(na):
            mine = pltpu.make_async_copy(x_refs[i], out_refs[i].at[_lin(me)], local_sems.at[i])
            mine.start()
            started.append(mine)
        sends = []
        for i in range(na):
            cps = [copy(i, 0, me, sibling, True)] + [copy(i, 1 + j, me, chip, True) for j, chip in enumerate(chips)]
            for cp in cps:
                cp.start()
            sends += cps
        for i in range(na):
            for j, chip in enumerate(chips):
                copy(i, 1 + j, chip, me).wait_recv()
                fwd = copy(i, 4 + j, chip, sibling)
                fwd.start()
                sends.append(fwd)
        for i in range(na):
            copy(i, 0, sibling, me).wait_recv()
            for j, chip in enumerate(chips):
                copy(i, 4 + j, _flip(chip, 1), me).wait_recv()
        for cp in sends:
            cp.wait_send()
        for mine in started:
            mine.wait()

    return pl.pallas_call(
        body, name=name,
        out_shape=[jax.ShapeDtypeStruct((NDEV,) + s.shape, s.dtype) for s in shards],
        in_specs=[HBM_SPEC] * na, out_specs=[HBM_SPEC] * na,
        scratch_shapes=[pltpu.SemaphoreType.DMA((7 * na,)), pltpu.SemaphoreType.DMA((7 * na,)),
                        pltpu.SemaphoreType.DMA((na,))],
    )(*shards)


def _all_gather_direct(shard, name):
    def body(x_ref, out_ref, send_sems, recv_sems, local_sem):
        me = _my_pos()
        mine = pltpu.make_async_copy(x_ref, out_ref.at[_lin(me)], local_sem)
        mine.start()
        sends = []
        for k in range(1, NDEV):
            cp = pltpu.make_async_remote_copy(
                src_ref=x_ref, dst_ref=out_ref.at[_lin(me)], send_sem=send_sems.at[k - 1],
                recv_sem=recv_sems.at[k - 1], device_id=_flip(me, k), device_id_type=MESH)
            cp.start()
            sends.append(cp)
        for k in range(1, NDEV):
            peer = _flip(me, k)
            pltpu.make_async_remote_copy(
                src_ref=x_ref, dst_ref=out_ref.at[_lin(peer)], send_sem=send_sems.at[k - 1],
                recv_sem=recv_sems.at[k - 1], device_id=peer, device_id_type=MESH).wait_recv()
        for cp in sends:
            cp.wait_send()
        mine.wait()

    return pl.pallas_call(
        body, name=name,
        out_shape=jax.ShapeDtypeStruct((NDEV,) + shard.shape, shard.dtype),
        in_specs=[HBM_SPEC], out_specs=HBM_SPEC,
        scratch_shapes=[pltpu.SemaphoreType.DMA((7,)), pltpu.SemaphoreType.DMA((7,)), pltpu.SemaphoreType.DMA],
    )(shard)


def _exchange_within_chip(parts, name):
    na = len(parts)

    def body(*refs):
        a_refs = refs[:na]
        recv_refs = refs[na:2 * na]
        send_sems, recv_sems = refs[2 * na:]
        me = _my_pos()
        c = me[2]
        sibling = _flip(me, 1)
        remote = []
        for i in range(na):
            for q in range(NCHIP):
                rc = pltpu.make_async_remote_copy(
                    src_ref=a_refs[i].at[q, 1 - c], dst_ref=recv_refs[i].at[q],
                    send_sem=send_sems.at[NCHIP * i + q], recv_sem=recv_sems.at[NCHIP * i + q],
                    device_id=sibling, device_id_type=MESH)
                rc.start()
                remote.append(rc)
        for rc in remote:
            rc.wait_recv()
        for rc in remote:
            rc.wait_send()

    return pl.pallas_call(
        body, name=name, out_shape=[jax.ShapeDtypeStruct((NCHIP,) + a.shape[2:], a.dtype) for a in parts],
        in_specs=[HBM_SPEC] * na, out_specs=[HBM_SPEC] * na,
        scratch_shapes=[pltpu.SemaphoreType.DMA((NCHIP * na,)), pltpu.SemaphoreType.DMA((NCHIP * na,))],
    )(*parts)


HBM_ONLY = pl.BlockSpec(memory_space=pltpu.HBM)
SEM_SPEC = pl.BlockSpec(memory_space=pltpu.SEMAPHORE)
SPLIT_COPY = pltpu.CompilerParams(has_side_effects=pltpu.SideEffectType.DATAFLOW_SIDE_EFFECTING)


def _chip_peers(me):
    return [_flip(me, 4), _flip(me, 2), _flip(me, 6)]


def _between_chips_start(parts, name):
    na = len(parts)

    def body(*refs):
        a_refs = refs[:na]
        land_refs = refs[na:2 * na]
        send_sems, recv_sems = refs[2 * na], refs[2 * na + 1]
        token = refs[-1]
        me = _my_pos()
        myq = _chip(me)
        for i in range(na):
            for j, peer in enumerate(_chip_peers(me)):
                pltpu.make_async_remote_copy(
                    src_ref=a_refs[i].at[_chip(peer)], dst_ref=land_refs[i].at[myq],
                    send_sem=send_sems.at[3 * i + j], recv_sem=recv_sems.at[3 * i + j],
                    device_id=peer, device_id_type=MESH).start()
        token[...] = jnp.zeros_like(token)

    hbm = [pltpu.HBM(a.shape, a.dtype) for a in parts]
    srcs = [pltpu.with_memory_space_constraint(a, pltpu.HBM) for a in parts]
    lands = [pltpu.with_memory_space_constraint(lax.empty(a.shape, a.dtype), pltpu.HBM) for a in parts]
    res = pl.pallas_call(
        body, name=name,
        out_shape=(pltpu.SemaphoreType.DMA((3 * na,)), pltpu.SemaphoreType.DMA((3 * na,)), *hbm, *hbm,
                   jax.ShapeDtypeStruct((8, LANES), F32)),
        in_specs=[HBM_ONLY] * (2 * na),
        out_specs=(SEM_SPEC, SEM_SPEC, *([HBM_ONLY] * (2 * na)), pl.BlockSpec(memory_space=pltpu.VMEM)),
        input_output_aliases={i: 2 + i for i in range(2 * na)},
        compiler_params=SPLIT_COPY,
    )(*srcs, *lands)
    return res[-1], (res[0], res[1], list(res[2:2 + na]), list(res[2 + na:2 + 2 * na]))


def _between_chips_wait(pending, after, name):
    send_sems, recv_sems, parts, lands = pending
    na = len(parts)

    def body(*refs):
        a_refs = refs[:na]
        land_refs = refs[na:2 * na]
        send_sems, recv_sems = refs[2 * na], refs[2 * na + 1]
        me = _my_pos()
        for i in range(na):
            for j, peer in enumerate(_chip_peers(me)):
                cp = pltpu.make_async_remote_copy(
                    src_ref=a_refs[i].at[_chip(peer)], dst_ref=land_refs[i].at[_chip(peer)],
                    send_sem=send_sems.at[3 * i + j], recv_sem=recv_sems.at[3 * i + j],
                    device_id=peer, device_id_type=MESH)
                cp.wait_send()
                cp.wait_recv()

    hbm = [pltpu.HBM(a.shape, a.dtype) for a in parts]
    res = pl.pallas_call(
        body, name=name, out_shape=(*hbm, *hbm),
        in_specs=[HBM_ONLY] * (2 * na) + [SEM_SPEC, SEM_SPEC, pl.BlockSpec(memory_space=pl.ANY)],
        out_specs=[HBM_ONLY] * (2 * na),
        input_output_aliases={i: i for i in range(2 * na)},
        compiler_params=SPLIT_COPY,
    )(*parts, *lands, send_sems, recv_sems, after)
    return list(res[:na]), list(res[na:])


def _remote(src, dst, send_sems, recv_sems, idx, peer):
    return pltpu.make_async_remote_copy(src_ref=src, dst_ref=dst, send_sem=send_sems.at[idx],
                                        recv_sem=recv_sems.at[idx], device_id=peer, device_id_type=MESH)


def _copies_own(bufs, me):
    return [(bufs[0], bufs[1].at[me[2]], 0, _flip(me, 1))]


def _copies_near(bufs, me):
    return [(bufs[0], bufs[1].at[0, me[2]], 0, _flip(me, 2)), (bufs[0], bufs[1].at[1, me[2]], 1, _flip(me, 4))]


def _copies_far(bufs, me):
    return [(bufs[0], bufs[1].at[me[2]], 0, _flip(me, 6))]


def _copies_others(bufs, me):
    na = len(bufs) // 2
    return [(bufs[i], bufs[na + i].at[_lin(me)], 7 * i + k - 1, _flip(me, k))
            for i in range(na) for k in range(1, NDEV)]


def _copies_exchange(bufs, me):
    na = len(bufs) // 2
    return [(bufs[i].at[_lin(_flip(me, k))], bufs[na + i].at[_lin(me)], 7 * i + k - 1, _flip(me, k))
            for i in range(na) for k in range(1, NDEV)]


GROUP_COPIES = dict(own=_copies_own, near=_copies_near, far=_copies_far, others=_copies_others,
                    exchange=_copies_exchange)
GROUP_COUNT = dict(own=1, near=2, far=1)
TO_ALL = ("others", "exchange")


def _gather_start(bufs, groups, after, name):
    nb = len(bufs)
    ng = len(groups)

    def body(*refs):
        b = refs[:nb]
        sems = refs[nb + 1:nb + 1 + 2 * ng]
        token = refs[-1]
        me = _my_pos()
        for gi, (group, idx) in enumerate(groups):
            for src, dst, k, peer in GROUP_COPIES[group]([b[i] for i in idx], me):
                _remote(src, dst, sems[2 * gi], sems[2 * gi + 1], k, peer).start()
        token[...] = jnp.zeros_like(token)

    sem_t = []
    for group, idx in groups:
        cnt = 7 * (len(idx) // 2) if group in TO_ALL else GROUP_COUNT[group]
        sem_t += [pltpu.SemaphoreType.DMA((cnt,)), pltpu.SemaphoreType.DMA((cnt,))]
    ins = [pltpu.with_memory_space_constraint(a, pltpu.HBM) for a in bufs]
    res = pl.pallas_call(
        body, name=name,
        out_shape=(*sem_t, *[pltpu.HBM(a.shape, a.dtype) for a in bufs], jax.ShapeDtypeStruct((8, LANES), F32)),
        in_specs=[HBM_ONLY] * nb + [pl.BlockSpec(memory_space=pl.ANY)],
        out_specs=(*([SEM_SPEC] * (2 * ng)), *([HBM_ONLY] * nb), pl.BlockSpec(memory_space=pltpu.VMEM)),
        input_output_aliases={i: 2 * ng + i for i in range(nb)},
        compiler_params=SPLIT_COPY,
    )(*ins, after)
    return res[-1], list(res[2 * ng:2 * ng + nb]), [(res[2 * gi], res[2 * gi + 1]) for gi in range(ng)]


def _gather_wait(group, send_sems, recv_sems, bufs, after, name):
    nb = len(bufs)
    copies = GROUP_COPIES[group]

    def body(*refs):
        b = refs[:nb]
        ss, rs = refs[nb], refs[nb + 1]
        me = _my_pos()
        for src, dst, idx, peer in copies(b, me):
            if group in TO_ALL:
                landed = b[nb // 2 + idx // 7].at[_lin(peer)]
            elif group == "own":
                landed = b[1].at[1 - me[2]]
            else:
                landed = dst
            cp = _remote(src, landed, ss, rs, idx, peer)
            cp.wait_send()
            cp.wait_recv()

    res = pl.pallas_call(
        body, name=name, out_shape=[pltpu.HBM(a.shape, a.dtype) for a in bufs],
        in_specs=[HBM_ONLY] * nb + [SEM_SPEC, SEM_SPEC, pl.BlockSpec(memory_space=pl.ANY)],
        out_specs=[HBM_ONLY] * nb,
        input_output_aliases={i: i for i in range(nb)},
        compiler_params=SPLIT_COPY,
    )(*bufs, send_sems, recv_sems, after)
    return list(res)


def _forward_to_sibling(buf, name):
    n = buf.shape[0]

    def body(_in_ref, out_ref, send_sems, recv_sems):
        me = _my_pos()
        c = me[2]
        sibling = _flip(me, 1)
        sends = []
        for r in range(n):
            cp = _remote(out_ref.at[r, c], out_ref.at[r, c], send_sems, recv_sems, r, sibling)
            cp.start()
            sends.append(cp)
        for r in range(n):
            _remote(out_ref.at[r, c], out_ref.at[r, 1 - c], send_sems, recv_sems, r, sibling).wait_recv()
        for cp in sends:
            cp.wait_send()

    return pl.pallas_call(
        body, name=name, out_shape=jax.ShapeDtypeStruct(buf.shape, buf.dtype),
        in_specs=[HBM_SPEC], out_specs=HBM_SPEC,
        scratch_shapes=[pltpu.SemaphoreType.DMA((n,)), pltpu.SemaphoreType.DMA((n,))],
        input_output_aliases={0: 0},
    )(buf)


def _allreduce_small(rep, conv, name):
    def body(rep_hbm, conv_hbm, out_ref, conv_out, stage_r, stage_c, send_sems, recv_sems, local_sems):
        me = _my_pos()
        mi = _lin(me)
        peers = [_flip(me, k) for k in range(1, NDEV)]
        own = [pltpu.make_async_copy(rep_hbm.at[mi], stage_r.at[mi], local_sems.at[0]),
               pltpu.make_async_copy(conv_hbm.at[mi], stage_c.at[mi], local_sems.at[1])]
        for cp in own:
            cp.start()
        sends = []
        for j, peer in enumerate(peers):
            for src, stage, base in ((rep_hbm, stage_r, 0), (conv_hbm, stage_c, 7)):
                cp = _remote(src.at[_lin(peer)], stage.at[mi], send_sems, recv_sems, base + j, peer)
                cp.start()
                sends.append(cp)
        for j, peer in enumerate(peers):
            for src, stage, base in ((rep_hbm, stage_r, 0), (conv_hbm, stage_c, 7)):
                _remote(src.at[mi], stage.at[_lin(peer)], send_sems, recv_sems, base + j, peer).wait_recv()
        for cp in own:
            cp.wait()
        acc_r = stage_r[0]
        acc_c = stage_c[0]
        for q in range(1, NDEV):
            acc_r = acc_r + stage_r[q]
            acc_c = acc_c + stage_c[q]
        out_ref[mi] = acc_r
        conv_out[...] = acc_c
        for j, peer in enumerate(peers):
            cp = _remote(out_ref.at[mi], out_ref.at[mi], send_sems, recv_sems, 14 + j, peer)
            cp.start()
            sends.append(cp)
        for j, peer in enumerate(peers):
            _remote(out_ref.at[mi], out_ref.at[_lin(peer)], send_sems, recv_sems, 14 + j, peer).wait_recv()
        for cp in sends:
            cp.wait_send()

    vmem = pl.BlockSpec(memory_space=pltpu.VMEM)
    return pl.pallas_call(
        body, name=name,
        out_shape=[jax.ShapeDtypeStruct(rep.shape, F32), jax.ShapeDtypeStruct(conv.shape[1:], F32)],
        in_specs=[HBM_SPEC, HBM_SPEC], out_specs=[vmem, vmem],
        scratch_shapes=[pltpu.VMEM(rep.shape, F32), pltpu.VMEM(conv.shape, F32),
                        pltpu.SemaphoreType.DMA((21,)), pltpu.SemaphoreType.DMA((21,)), pltpu.SemaphoreType.DMA((2,))],
    )(rep, conv)


def _exchange_all(parts, name):
    na = len(parts)

    def body(*refs):
        a_refs = refs[:na]
        out_refs = refs[na:2 * na]
        send_sems, recv_sems = refs[2 * na:]
        me = _my_pos()
        sends = []
        for i in range(na):
            for k in range(1, NDEV):
                peer = _flip(me, k)
                cp = pltpu.make_async_remote_copy(
                    src_ref=a_refs[i].at[_lin(peer)], dst_ref=out_refs[i].at[_lin(me)],
                    send_sem=send_sems.at[7 * i + k - 1], recv_sem=recv_sems.at[7 * i + k - 1],
                    device_id=peer, device_id_type=MESH)
                cp.start()
                sends.append(cp)
        for i in range(na):
            for k in range(1, NDEV):
                peer = _flip(me, k)
                pltpu.make_async_remote_copy(
                    src_ref=a_refs[i].at[_lin(peer)], dst_ref=out_refs[i].at[_lin(peer)],
                    send_sem=send_sems.at[7 * i + k - 1], recv_sem=recv_sems.at[7 * i + k - 1],
                    device_id=peer, device_id_type=MESH).wait_recv()
        for cp in sends:
            cp.wait_send()

    return pl.pallas_call(
        body, name=name, out_shape=[jax.ShapeDtypeStruct(a.shape, a.dtype) for a in parts],
        in_specs=[HBM_SPEC] * na, out_specs=[HBM_SPEC] * na,
        scratch_shapes=[pltpu.SemaphoreType.DMA((7 * na,)), pltpu.SemaphoreType.DMA((7 * na,))],
    )(*parts)


def _scalar(v):
    return jnp.asarray(v, jnp.int32).reshape(1)


def _sum_pairs(parts, theirs, name):
    na = len(parts)

    def body(c_ref, *refs):
        for i in range(na):
            o_ref = refs[2 * na + i]
            o_ref[0] = (refs[i][0, 0].astype(F32) + refs[na + i][0].astype(F32)).astype(o_ref.dtype)

    def mine_spec(a):
        return pl.BlockSpec((1, 1) + a.shape[2:], lambda q, c_ref: (q, c_ref[0], 0, 0))

    def spec(a):
        return pl.BlockSpec((1,) + a.shape[1:], lambda q, c_ref: (q, 0, 0))

    return pl.pallas_call(
        body, name=name,
        grid_spec=pltpu.PrefetchScalarGridSpec(
            num_scalar_prefetch=1, grid=(NCHIP,),
            in_specs=[mine_spec(a) for a in parts] + [spec(a) for a in theirs],
            out_specs=[spec(a) for a in theirs]),
        out_shape=[jax.ShapeDtypeStruct(a.shape, a.dtype) for a in theirs],
        compiler_params=_cp(("arbitrary",), VMEM_BIG),
    )(_scalar(lax.axis_index("c")), *_hbm(*parts, *theirs))


def _others(q, mine, nblk=NCHIP):
    return jnp.where(q == mine, (q + 1) % nblk, q)


def _sum_chips_adamw(own, recv, wv, mv, vv, token, tr, name):
    _, r, w = recv.shape

    def body(q_ref, own_ref, r0, r1, r2, r3, w_ref, m_ref, v_ref, _token, g_ref, d_ref, m2_ref, v2_ref):
        myq = q_ref[0]
        acc = None
        for q, r_ref in enumerate((r0, r1, r2, r3)):
            term = jnp.where(myq == q, own_ref[0], r_ref[0]).astype(F32)
            acc = term if acc is None else acc + term
        g_ref[...] = acc
        delta, m2, v2 = _adam_math(w_ref[...], acc, m_ref[...], v_ref[...])
        d_ref[...] = delta
        m2_ref[...] = m2
        v2_ref[...] = v2

    def recv_spec(q):
        return pl.BlockSpec((1, tr, w), lambda i, q_ref: (_others(q, q_ref[0]), i, 0))

    rows = lambda: pl.BlockSpec((tr, w), lambda i, q_ref: (i, 0))
    shp = jax.ShapeDtypeStruct((r, w), F32)
    return pl.pallas_call(
        body, name=name,
        grid_spec=pltpu.PrefetchScalarGridSpec(
            num_scalar_prefetch=1, grid=(r // tr,),
            in_specs=[pl.BlockSpec((1, tr, w), lambda i, q_ref: (q_ref[0], i, 0))]
            + [recv_spec(q) for q in range(NCHIP)] + [rows(), rows(), rows()]
            + [pl.BlockSpec((8, LANES), lambda i, q_ref: (0, 0))],
            out_specs=[rows(), rows(), rows(), rows()]),
        out_shape=[shp, shp, shp, shp],
        compiler_params=_cp(("arbitrary",), VMEM_MID),
    )(_scalar(_chip(_my_pos())), *_hbm(own, recv, recv, recv, recv, wv, mv, vv, token))


def _sum_blocks_small(own, recv, mine, transpose, name):
    na = len(recv)
    nblk = recv[0].shape[0]

    def body(q_ref, *refs):
        me = q_ref[0]
        for i in range(na):
            acc = None
            for q in range(nblk):
                term = jnp.where(me == q, refs[i][0], refs[na * (1 + q) + i][0]).astype(F32)
                acc = term if acc is None else acc + term
            refs[na * (1 + nblk) + i][...] = acc.T if transpose[i] else acc

    def oshape(a, tr):
        r, w = a.shape[1:]
        return (w, r) if tr else (r, w)

    own_spec = lambda a: pl.BlockSpec((1,) + a.shape[1:], lambda s, q_ref: (q_ref[0], 0, 0))
    recv_spec = lambda a, q: pl.BlockSpec((1,) + a.shape[1:], lambda s, q_ref: (_others(q, q_ref[0], nblk), 0, 0))
    out_spec = lambda shp: pl.BlockSpec(shp, lambda s, q_ref: (0, 0))
    in_specs = [own_spec(a) for a in own]
    for q in range(nblk):
        in_specs += [recv_spec(a, q) for a in recv]
    return pl.pallas_call(
        body, name=name,
        grid_spec=pltpu.PrefetchScalarGridSpec(
            num_scalar_prefetch=1, grid=(1,), in_specs=in_specs,
            out_specs=[out_spec(oshape(a, tr)) for a, tr in zip(recv, transpose)]),
        out_shape=[jax.ShapeDtypeStruct(oshape(a, tr), F32) for a, tr in zip(recv, transpose)],
        compiler_params=_cp(("arbitrary",), VMEM_MID),
    )(_scalar(mine), *_hbm(*own, *(list(recv) * nblk)))


def _rep_offsets():
    offs = []
    o = 0
    for r in REP_ROWS:
        offs.append(o)
        o += r
    return offs


LOSS_ROW = REP_TOTAL_ROWS


def _pack_small_grads(g):
    offs = _rep_offsets()

    def body(dwa, dwx, dnm, dbin, dcb, dba, dbx, dlam, dq0, dq1, dq2, dk0, dk1, dk2, dnp, dbpg, loss, o_ref):
        o_ref[pl.ds(REP_TOTAL_ROWS - 2, NDEV * REP_ROWS_DEV - REP_TOTAL_ROWS + 2), :] = jnp.zeros(
            (NDEV * REP_ROWS_DEV - REP_TOTAL_ROWS + 2, LANES), F32)
        o_ref[pl.ds(LOSS_ROW, 1), :] = loss[0:1, :]
        for n in range(NRB):
            o_ref[pl.ds(offs[0] + n * RBW, RBW), :] = dwa[n]
            o_ref[pl.ds(offs[1] + n * RBW, RBW), :] = dwx[n]

        def put_vec(off, ref, rows):
            for k in range(rows):
                o_ref[pl.ds(off + k, 1), :] = ref[:, k * LANES:(k + 1) * LANES]

        put_vec(offs[2], dnm, REP_ROWS[2])
        put_vec(offs[3], dbin, REP_ROWS[3])
        put_vec(offs[4], dcb, REP_ROWS[4])
        put_vec(offs[5], dba, REP_ROWS[5])
        put_vec(offs[6], dbx, REP_ROWS[6])
        put_vec(offs[7], dlam, REP_ROWS[7])
        for k, ref in enumerate((dq0, dq1, dq2)):
            o_ref[pl.ds(offs[8] + k, 1), :] = ref[...]
        for k, ref in enumerate((dk0, dk1, dk2)):
            o_ref[pl.ds(offs[9] + k, 1), :] = ref[...]
        put_vec(offs[10], dnp, REP_ROWS[10])
        put_vec(offs[11], dbpg, REP_ROWS[11])

    args = [g["w_rg_a"], g["w_rg_x"], g["norm_mix"], g["b_in"], g["conv_b"], g["b_rg_a"], g["b_rg_x"],
            g["lru_lambda"], *g["q_norm"], *g["k_norm"], g["norm_ple"], g["b_ple_gate"], g["loss"]]
    full = lambda shp: pl.BlockSpec(shp, lambda: (0,) * len(shp))
    return pl.pallas_call(
        body, name="pack_small_grads",
        in_specs=[full(a.shape) for a in args],
        out_specs=full((NDEV * REP_ROWS_DEV, LANES)),
        out_shape=jax.ShapeDtypeStruct((NDEV * REP_ROWS_DEV, LANES), F32),
    )(*_hbm(*args))


def _adam_math(wv, gv, mv, vv):
    c1 = 1.0 - B1 ** STEP
    c2 = 1.0 - B2 ** STEP
    m2 = B1 * mv + (1.0 - B1) * gv
    v2 = B2 * vv + (1.0 - B2) * (gv * gv)
    delta = (-LR) * ((m2 / c1) / (jnp.sqrt(v2 / c2) + AEPS) + WD * wv)
    return delta, m2, v2


def _adamw_small(rep_flat, w, m, v):
    offs = _rep_offsets()
    n = len(REP_NAMES)

    def body(*refs):
        g_ref = refs[0]
        w_refs = refs[1:1 + n]
        m_refs = refs[1 + n:1 + 2 * n]
        v_refs = refs[1 + 2 * n:1 + 3 * n]
        outs = refs[1 + 3 * n:]
        go, do, mo, vo = outs[:n], outs[n:2 * n], outs[2 * n:3 * n], outs[3 * n:]

        def emit(i, idx, gv):
            go[i][idx] = gv
            delta, m2, v2 = _adam_math(w_refs[i][idx], gv, m_refs[i][idx], v_refs[i][idx])
            do[i][idx] = delta
            mo[i][idx] = m2
            vo[i][idx] = v2

        for i in range(n):
            if i < 2:
                for b in range(NRB):
                    emit(i, b, g_ref[pl.ds(offs[i] + b * RBW, RBW), :])
            elif REP_NAMES[i] in ("q_norm", "k_norm"):
                emit(i, slice(None), g_ref[pl.ds(offs[i], NG), :])
            else:
                gv = jnp.concatenate([g_ref[pl.ds(offs[i] + k, 1), :] for k in range(REP_ROWS[i])], axis=1)
                emit(i, slice(None), gv)

    full = lambda shp: pl.BlockSpec(shp, lambda: (0,) * len(shp))
    pspecs = [full(a.shape) for a in w]
    pshapes = [jax.ShapeDtypeStruct(a.shape, F32) for a in w]
    res = pl.pallas_call(
        body, name="adamw_small",
        in_specs=[full(rep_flat.shape)] + pspecs * 3,
        out_specs=pspecs * 4, out_shape=pshapes * 4,
        compiler_params=_cp(None, VMEM_MID),
    )(*_hbm(rep_flat, *w, *m, *v))
    return res[:n], res[n:2 * n], res[2 * n:3 * n], res[3 * n:]


def _adamw_many(w, g, m, v, token):
    n = len(w)

    def body(*refs):
        for i in range(n):
            delta, m2, v2 = _adam_math(refs[i][...], refs[n + i][...], refs[2 * n + i][...], refs[3 * n + i][...])
            refs[4 * n + 1 + i][...] = delta
            refs[5 * n + 1 + i][...] = m2
            refs[6 * n + 1 + i][...] = v2

    full = lambda shp: pl.BlockSpec(shp, lambda: (0,) * len(shp))
    specs = [full(a.shape) for a in w]
    shapes = [jax.ShapeDtypeStruct(a.shape, F32) for a in w]
    res = pl.pallas_call(
        body, name="adamw_shards",
        in_specs=specs * 4 + [full((8, LANES))], out_specs=specs * 3, out_shape=shapes * 3,
        compiler_params=_cp(None, VMEM_MID),
    )(*_hbm(*w, *g, *m, *v, token))
    return res[:n], res[n:2 * n], res[2 * n:]


def kernel(x, p, norm_mix, w_in, b_in, conv_w, conv_b, w_rg_a, b_rg_a, w_rg_x, b_rg_x, lru_lambda, q_norm, k_norm, w_o_rnn, w_o_att, w_out, norm_ple, w_ple_gate, b_ple_gate, w_ple, loss_target, m_norm_mix, m_w_in, m_b_in, m_conv_w, m_conv_b, m_w_rg_a, m_b_rg_a, m_w_rg_x, m_b_rg_x, m_lru_lambda, m_q_norm, m_k_norm, m_w_o_rnn, m_w_o_att, m_w_out, m_norm_ple, m_w_ple_gate, m_b_ple_gate, m_w_ple, v_norm_mix, v_w_in, v_b_in, v_conv_w, v_conv_b, v_w_rg_a, v_b_rg_a, v_w_rg_x, v_b_rg_x, v_lru_lambda, v_q_norm, v_k_norm, v_w_o_rnn, v_w_o_att, v_w_out, v_norm_ple, v_w_ple_gate, v_b_ple_gate, v_w_ple):
    w = dict(norm_mix=norm_mix, w_in=w_in, b_in=b_in, conv_w=conv_w, conv_b=conv_b, w_rg_a=w_rg_a, b_rg_a=b_rg_a,
             w_rg_x=w_rg_x, b_rg_x=b_rg_x, lru_lambda=lru_lambda, q_norm=q_norm, k_norm=k_norm, w_o_rnn=w_o_rnn,
             w_o_att=w_o_att, w_out=w_out, norm_ple=norm_ple, w_ple_gate=w_ple_gate, b_ple_gate=b_ple_gate,
             w_ple=w_ple)
    m = dict(norm_mix=m_norm_mix, w_in=m_w_in, b_in=m_b_in, conv_w=m_conv_w, conv_b=m_conv_b, w_rg_a=m_w_rg_a,
             b_rg_a=m_b_rg_a, w_rg_x=m_w_rg_x, b_rg_x=m_b_rg_x, lru_lambda=m_lru_lambda, q_norm=m_q_norm,
             k_norm=m_k_norm, w_o_rnn=m_w_o_rnn, w_o_att=m_w_o_att, w_out=m_w_out, norm_ple=m_norm_ple,
             w_ple_gate=m_w_ple_gate, b_ple_gate=m_b_ple_gate, w_ple=m_w_ple)
    v = dict(norm_mix=v_norm_mix, w_in=v_w_in, b_in=v_b_in, conv_w=v_conv_w, conv_b=v_conv_b, w_rg_a=v_w_rg_a,
             b_rg_a=v_b_rg_a, w_rg_x=v_w_rg_x, b_rg_x=v_b_rg_x, lru_lambda=v_lru_lambda, q_norm=v_q_norm,
             k_norm=v_k_norm, w_o_rnn=v_w_o_rnn, w_o_att=v_w_o_att, w_out=v_w_out, norm_ple=v_norm_ple,
             w_ple_gate=v_w_ple_gate, b_ple_gate=v_b_ple_gate, w_ple=v_w_ple)
    names = list(w.keys())

    shards = [w_in[0].T.astype(BF16), w_o_rnn[0].astype(BF16), w_o_att[0].T.astype(BF16), w_out[0].astype(BF16),
              w_ple_gate[0].astype(BF16), w_ple[0].T.astype(BF16), conv_w[0]]
    pos = _my_pos()
    me, my_core, my_chip = _lin(pos), pos[2], _chip(pos)
    hbm_empty = lambda shp, dt: lax.empty(shp, dt)
    w_shard, conv_shard = shards[0], shards[6]
    shp = w_shard.shape
    entry_token, bufs, sems = _gather_start(
        [w_shard, hbm_empty((2,) + shp, BF16), hbm_empty((2, 2) + shp, BF16), conv_shard,
         hbm_empty((NDEV,) + conv_shard.shape, F32)],
        [("own", (0, 1)), ("near", (0, 2)), ("others", (3, 4))], norm_mix, "gather_start_near")
    w_src, own_l, near_l, conv_src, conv_l = bufs
    sem_own, sem_near, sem_conv = sems
    gather_out = {}

    def project(hn):
        w_thru, own = _gather_wait("own", *sem_own, [w_src, own_l], hn, "gather_wait_own")
        own = lax.dynamic_update_slice(own, w_shard[None], (my_core, 0, 0)).reshape(1, CHIP_COLS, D)
        chips = [jnp.stack([my_chip]), jnp.stack([my_chip ^ 1, my_chip ^ 2]), jnp.stack([my_chip ^ 3])]
        chips = [c.astype(jnp.int32) for c in chips]
        proj = _in_proj_chips(hn, own, b_in, chips[0], None, entry_token, "in_proj_own")
        w_thru, near = _gather_wait("near", *sem_near, [w_thru, near_l], proj, "gather_wait_near")
        near = _forward_to_sibling(near, "gather_forward_near")
        token, (w_thru, far_l), (sem_far,) = _gather_start(
            [w_thru, hbm_empty((2,) + shp, BF16)], [("far", (0, 1))], near, "gather_start_far")
        near = near.reshape(2, CHIP_COLS, D)
        proj = _in_proj_chips(hn, near, b_in, chips[1], proj, token, "in_proj_near")
        w_thru, far = _gather_wait("far", *sem_far, [w_thru, far_l], proj, "gather_wait_far")
        far = _forward_to_sibling(far[None], "gather_forward_far").reshape(1, CHIP_COLS, D)
        proj = _in_proj_chips(hn, far, b_in, chips[2], proj, token, "in_proj_far")
        conv_thru, conv_g = _gather_wait("others", *sem_conv, [conv_src, conv_l], proj, "gather_wait_conv")
        conv_g = lax.dynamic_update_slice(conv_g, conv_shard[None], (me, 0, 0))
        conv_f = conv_g.transpose(1, 0, 2).reshape(CONVW, DR)
        srcs = list(shards[1:6])
        token, obufs, (sem_out,) = _gather_start(
            srcs + [hbm_empty((NDEV,) + a.shape, BF16) for a in srcs], [("others", tuple(range(10)))], proj,
            "gather_start_out")
        gather_out.update(bufs=obufs, sems=sem_out)
        return proj, [own, near, far], jnp.concatenate(chips), conv_f, token

    def other_weights(after):
        obufs = _gather_wait("others", *gather_out["sems"], gather_out["bufs"], after, "gather_wait_out")
        full = [lax.dynamic_update_slice(a, s[None], (me, 0, 0)) for a, s in zip(obufs[5:], shards[1:6])]
        return [a.reshape((NDEV * a.shape[1], a.shape[2])) for a in full]

    def start_reduce(arrs, tag):
        if tag == "out":
            parts = [a.reshape((NDEV, a.shape[0] // NDEV, a.shape[1])) for a in arrs]
            token, bufs, (sems,) = _gather_start(
                parts + [lax.empty(a.shape, a.dtype) for a in parts], [("exchange", tuple(range(2 * len(parts))))],
                arrs[-1][:SUBLANES], "reduce_out_start")
            return token, (bufs, sems)
        parts = [a.reshape((NCHIP, 2, a.shape[0] // NDEV, a.shape[1])) for a in arrs]
        theirs = _exchange_within_chip(parts, "reduce_within_chip_" + tag)
        return _between_chips_start(_sum_pairs(parts, theirs, "sum_pairs_" + tag), "reduce_between_chips_start_" + tag)

    grad_x, pending_out, pending_in, small = _local_step(
        x.reshape(T, D), p.reshape(T, PLE), loss_target.reshape(T, D),
        project, other_weights,
        norm_mix, conv_b, w_rg_a[0], b_rg_a, w_rg_x[0], b_rg_x, lru_lambda, q_norm[0], k_norm[0],
        norm_ple, b_ple_gate, start_reduce, entry_token)

    rep_parts = _pack_small_grads(small).reshape(NDEV, REP_ROWS_DEV, LANES)
    conv_parts = small["conv_w"].reshape(CONVW, NDEV, DR // NDEV).transpose(1, 0, 2)
    smalls = [rep_parts, conv_parts]
    token, sbufs, (sem_x,) = _gather_start(
        smalls + [lax.empty(a.shape, F32) for a in smalls], [("exchange", (0, 1, 2, 3))], small["norm_mix"],
        "reduce_small_start")

    myq = _chip(_my_pos())
    own_in, recv_in = _between_chips_wait(pending_in, token, "reduce_between_chips_wait_in")
    w_in_res = _sum_chips_adamw(own_in[0], recv_in[0], w_in[0].T, m_w_in[0].T, v_w_in[0].T, token, 304, "adamw_w_in")
    sbufs = _gather_wait("exchange", *sem_x, sbufs, w_in_res[0], "reduce_small_wait")
    g_rep, g_conv = _sum_blocks_small(sbufs[:2], sbufs[2:], me, (False, False), "sum_small")
    token, gbufs, (sem_g,) = _gather_start(
        [g_rep, lax.empty((NDEV,) + g_rep.shape, F32)], [("others", (0, 1))], g_conv, "gather_small_start")
    obufs = _gather_wait("exchange", *pending_out[1], pending_out[0], token, "reduce_out_wait")
    g_o_rnn, g_o_att, g_out, g_pg, g_ple = _sum_blocks_small(
        obufs[:5], obufs[5:], me, (False, True, False, False, True), "sum_out")

    grad, delta, new_m, new_v = {}, {}, {}, {}
    rest = ("w_o_rnn", "w_o_att", "w_out", "w_ple_gate", "w_ple", "conv_w")
    g_rest = [g_o_rnn, g_o_att, g_out, g_pg, g_ple, g_conv]
    rest_res = _adamw_many([w[n][0] for n in rest], g_rest, [m[n][0] for n in rest], [v[n][0] for n in rest], token)
    _, rep_all = _gather_wait("others", *sem_g, gbufs, rest_res[0][0], "gather_small_wait")
    rep_all = lax.dynamic_update_slice(rep_all, g_rep[None], (me, 0, 0)).reshape(NDEV * REP_ROWS_DEV, LANES)
    loss = rep_all[LOSS_ROW, 0]
    rep_shape = lambda a: a if a.ndim == 2 else a.reshape(a.shape[1:])
    res = _adamw_small(rep_all, [rep_shape(w[n]) for n in REP_NAMES], [rep_shape(m[n]) for n in REP_NAMES],
                       [rep_shape(v[n]) for n in REP_NAMES])
    for dst, vals in zip((grad, delta, new_m, new_v), res):
        for n, a in zip(REP_NAMES, vals):
            dst[n] = a.reshape(w[n].shape)
    grad["w_in"], delta["w_in"], new_m["w_in"], new_v["w_in"] = [a.T[None] for a in w_in_res]
    for n, a in zip(rest, g_rest):
        grad[n] = a[None]
    for dst, vals in zip((delta, new_m, new_v), rest_res):
        for n, a in zip(rest, vals):
            dst[n] = a[None]

    return (loss, grad_x.reshape(BL, S, D), *[grad[n] for n in names], *[delta[n] for n in names],
            *[new_m[n] for n in names], *[new_v[n] for n in names])
```

```python
import jax
import jax.numpy as jnp
from jax import lax
from jax.experimental import pallas as pl
from jax.experimental.pallas import tpu as pltpu

F32 = jnp.float32
BF16 = jnp.bfloat16

D = 1024
S = 2048
BL = 2
T = BL * S
NDEV = 8
NCHIP = 4
PLE = 256
DR = 1280
NRB = 10
RBW = 128
CONVW = 4
LRU_C = 8.0
HD = 128
NH = 4
PATTERNS = ((128, 1), (512, 4), (2048, 16))
NG = 3
ATT = NH * HD
GW = NG * ATT
NIN = 2 * DR + 3 * GW + ATT + 2 * D
OFF_ZR = DR
OFF_Q = 2 * DR
OFF_ZA = OFF_Q + 3 * GW
OFF_G = OFF_ZA + ATT
ROPE_THETA = 10000.0
EPS = 1e-6
SCALE = HD ** -0.5
NEG = -1e30
QB = 128
LANES = 128
CT = 512
NCT = NIN // CT
A_W = 2 * DR
C_W = ATT + 2 * D

LR, B1, B2, AEPS, WD, STEP = 0.001, 0.9, 0.999, 1e-08, 0.01, 10

NSHARD_IN = NIN // NDEV
REP_NAMES = ("w_rg_a", "w_rg_x", "norm_mix", "b_in", "conv_b", "b_rg_a", "b_rg_x", "lru_lambda", "q_norm",
             "k_norm", "norm_ple", "b_ple_gate")
REP_ROWS = (NRB * RBW, NRB * RBW, D // LANES, NIN // LANES, DR // LANES, DR // LANES, DR // LANES, DR // LANES,
            NG, NG, D // LANES, D // LANES)
REP_TOTAL_ROWS = sum(REP_ROWS)
REP_ROWS_DEV = 344
BIG_NAMES = ("w_in", "w_o_rnn", "w_o_att", "w_out", "w_ple_gate", "w_ple")

VMEM_BIG = 56 * 1024 * 1024
VMEM_MID = 40 * 1024 * 1024


def _cp(sem=None, vmem=None):
    return pltpu.CompilerParams(dimension_semantics=sem, vmem_limit_bytes=vmem)


def _hbm(*arrays):
    return [pltpu.with_memory_space_constraint(a, pltpu.HBM) for a in arrays]


def _copy_together(copies):
    for cp in copies:
        cp.start()
    for cp in copies:
        cp.wait()


def _dot(a, b):
    return jnp.dot(a, b, preferred_element_type=F32)


def _dot_nt(a, b):
    return lax.dot_general(a, b, (((1,), (1,)), ((), ())), preferred_element_type=F32)


def _dot_tn(a, b):
    return lax.dot_general(a, b, (((0,), (0,)), ((), ())), preferred_element_type=F32)


def _sigmoid(x):
    return jax.nn.sigmoid(x)


def _perm(j):
    jq = j - OFF_Q // CT
    inside = (j >= OFF_Q // CT) & (j < OFF_ZA // CT)
    return jnp.where(inside, OFF_Q // CT + (jq % 3) * 3 + jq // 3, j)


PIECES = ((0, A_W // CT), (OFF_Q // CT, GW // CT), (OFF_Q // CT + 3, GW // CT), (OFF_Q // CT + 6, GW // CT),
          (OFF_ZA // CT, C_W // CT))


def _rmsnorm_fwd(x, gain, token, tm=512):
    def body(x_ref, g_ref, _token, o_ref):
        xv = x_ref[...]
        var = jnp.mean(xv * xv, axis=-1, keepdims=True)
        o_ref[...] = (xv * lax.rsqrt(var + EPS) * g_ref[...]).astype(BF16)

    return pl.pallas_call(
        body, grid=(T // tm,), name="rmsnorm_fwd",
        in_specs=[pl.BlockSpec((tm, D), lambda i: (i, 0)), pl.BlockSpec((1, D), lambda i: (0, 0)),
                  pl.BlockSpec((8, LANES), lambda i: (0, 0))],
        out_specs=pl.BlockSpec((tm, D), lambda i: (i, 0)),
        out_shape=jax.ShapeDtypeStruct((T, D), BF16),
        compiler_params=_cp(("parallel",)),
    )(*_hbm(x, gain, token))


CHIP_COLS = NIN // NCHIP


def _in_proj_chips(hn, w_rows, bias, chips, proj, token, name, tm=1024):
    n = w_rows.shape[0]

    def body(chips_ref, a_ref, w_ref, b_ref, _token, *rest):
        o_ref = rest[-1]
        o_ref[...] = (_dot_nt(a_ref[...], w_ref[0]) + b_ref[...]).astype(BF16)

    in_specs = [pl.BlockSpec((tm, D), lambda s, i, ch: (i, 0)),
                pl.BlockSpec((1, CHIP_COLS, D), lambda s, i, ch: (s, 0, 0)),
                pl.BlockSpec((1, CHIP_COLS), lambda s, i, ch: (0, ch[s])),
                pl.BlockSpec((8, LANES), lambda s, i, ch: (0, 0))]
    args = [hn, w_rows, bias, token]
    aliases = {}
    if proj is not None:
        in_specs.append(pl.BlockSpec(memory_space=pl.ANY))
        args.append(proj)
        aliases = {5: 0}
    return pl.pallas_call(
        body, name=name,
        grid_spec=pltpu.PrefetchScalarGridSpec(
            num_scalar_prefetch=1, grid=(n, T // tm), in_specs=in_specs,
            out_specs=pl.BlockSpec((tm, CHIP_COLS), lambda s, i, ch: (i, ch[s]))),
        out_shape=jax.ShapeDtypeStruct((T, NIN), BF16),
        input_output_aliases=aliases,
        compiler_params=_cp(("arbitrary", "arbitrary"), VMEM_BIG),
    )(chips, *_hbm(*args))


def _grad_x(pieces, w_bufs, chips, token, x, dx1, gain, tm=512):
    nb = len(w_bufs)

    def body(chips_ref, a_ref, q_ref, k_ref, v_ref, c_ref, *rest):
        w_hbm = rest[:nb]
        x_ref, dx1_ref, g_ref, dx_ref, dg_ref, w, sems = rest[nb + 1:]
        first = pl.program_id(0) == 0

        @pl.when(first)
        def _():
            s = 0
            copies = []
            for buf in w_hbm:
                for r in range(buf.shape[0]):
                    row = pl.multiple_of(chips_ref[s] * CHIP_COLS, 128)
                    copies.append(pltpu.make_async_copy(buf.at[r], w.at[pl.ds(row, CHIP_COLS), :], sems.at[s]))
                    s += 1
            _copy_together(copies)

        acc = _dot(a_ref[...], w[pl.ds(0, A_W), :])
        for kind, p_ref in enumerate((q_ref, k_ref, v_ref)):
            for g in range(NG):
                row = OFF_Q + (3 * g + kind) * CT
                acc = acc + _dot(p_ref[:, g * CT:(g + 1) * CT], w[pl.ds(row, CT), :])
        dn = acc + _dot(c_ref[...], w[pl.ds(OFF_ZA, C_W), :])
        xv = x_ref[...]
        rstd = lax.rsqrt(jnp.mean(xv * xv, axis=-1, keepdims=True) + EPS)
        xh = xv * rstd
        dg = jnp.sum(dn * xh, axis=0, keepdims=True)
        gd = dn * g_ref[...]
        dx_ref[...] = dx1_ref[...] + rstd * (gd - xh * jnp.mean(gd * xh, axis=-1, keepdims=True))

        @pl.when(first)
        def _():
            dg_ref[...] = dg

        @pl.when(jnp.logical_not(first))
        def _():
            dg_ref[...] += dg

    tok = lambda wd: pl.BlockSpec((tm, wd), lambda i, ch: (i, 0))
    vec = lambda: pl.BlockSpec((1, D), lambda i, ch: (0, 0))
    return pl.pallas_call(
        body, name="grad_x",
        grid_spec=pltpu.PrefetchScalarGridSpec(
            num_scalar_prefetch=1, grid=(T // tm,),
            in_specs=[tok(A_W), tok(GW), tok(GW), tok(GW), tok(C_W)] + [pl.BlockSpec(memory_space=pl.ANY)] * nb
            + [pl.BlockSpec((8, LANES), lambda i, ch: (0, 0)), tok(D), tok(D), vec()],
            out_specs=[tok(D), vec()],
            scratch_shapes=[pltpu.VMEM((NIN, D), BF16), pltpu.SemaphoreType.DMA((NCHIP,))]),
        out_shape=[jax.ShapeDtypeStruct((T, D), F32), jax.ShapeDtypeStruct((1, D), F32)],
        compiler_params=_cp(("arbitrary",), VMEM_BIG),
    )(chips, *_hbm(*pieces, *w_bufs, token, x, dx1, gain))


def _dw_in(pieces, hn):
    def body(a_ref, q_ref, k_ref, v_ref, c_ref, h_hbm, o_ref, s_ref, h):
        j = pl.program_id(0)

        @pl.when(j == 0)
        def _():
            pltpu.sync_copy(h_hbm, h)

        def step(x_ref):
            xv = x_ref[...]
            o_ref[...] = _dot_tn(xv, h[...]).astype(BF16)
            s_ref[...] = jnp.sum(xv.astype(F32), axis=0, keepdims=True)

        for x_ref, (lo, n) in zip((a_ref, q_ref, k_ref, v_ref, c_ref), PIECES):
            pl.when((j >= lo) & (j < lo + n))(lambda x_ref=x_ref: step(x_ref))

    def piece_spec(lo, n):
        return pl.BlockSpec((T, CT), lambda j: (0, jnp.clip(j - lo, 0, n - 1)))

    return pl.pallas_call(
        body, grid=(NCT,), name="dw_in",
        in_specs=[piece_spec(lo, n) for lo, n in PIECES] + [pl.BlockSpec(memory_space=pl.ANY)],
        out_specs=[pl.BlockSpec((CT, D), lambda j: (_perm(j), 0)), pl.BlockSpec((1, CT), lambda j: (0, _perm(j)))],
        out_shape=[jax.ShapeDtypeStruct((NIN, D), BF16), jax.ShapeDtypeStruct((1, NIN), F32)],
        scratch_shapes=[pltpu.VMEM((T, D), BF16)],
        compiler_params=_cp(("arbitrary",), VMEM_BIG),
    )(*_hbm(*pieces, hn))


def _mm_tn(a, b, ta, tt, name):
    m = a.shape[1]
    n = b.shape[1]
    nt = T // tt

    def body(a_ref, b_ref, o_ref, acc):
        t = pl.program_id(1)
        p = _dot_tn(a_ref[...].astype(BF16), b_ref[...].astype(BF16))

        @pl.when(t == 0)
        def _():
            acc[...] = p

        @pl.when(t > 0)
        def _():
            acc[...] += p

        @pl.when(t == nt - 1)
        def _():
            o_ref[...] = acc[...].astype(BF16)

    return pl.pallas_call(
        body, grid=(m // ta, nt), name=name,
        in_specs=[pl.BlockSpec((tt, ta), lambda j, t: (t, j)), pl.BlockSpec((tt, n), lambda j, t: (t, 0))],
        out_specs=pl.BlockSpec((ta, n), lambda j, t: (j, 0)),
        out_shape=jax.ShapeDtypeStruct((m, n), BF16),
        scratch_shapes=[pltpu.VMEM((ta, n), F32)],
        compiler_params=_cp(("parallel", "arbitrary"), VMEM_MID),
    )(*_hbm(a, b))


def _row_iota():
    return lax.broadcasted_iota(jnp.int32, (S, RBW), 0)


SUBLANES = 8
N_SHIFT_BUFS = 4


class _Shifter:
    def __init__(self, bufs):
        self.bufs = bufs
        self.k = 0

    def _store(self, v, fill, front):
        b = self.bufs.at[self.k % N_SHIFT_BUFS]
        self.k += 1
        b[pl.ds(0 if front else SUBLANES + S, SUBLANES), :] = jnp.full((SUBLANES, RBW), fill, F32)
        b[pl.ds(SUBLANES, S), :] = v
        return b

    def down(self, v, ds, fill):
        b = self._store(v, fill, True)
        return [b[pl.ds(SUBLANES - d, S), :] for d in ds]

    def up(self, v, ds, fill):
        b = self._store(v, fill, False)
        return [b[pl.ds(SUBLANES + d, S), :] for d in ds]


def _shift_down(v, d, sh, fill):
    return sh.down(v, (d,), fill)[0]


def _shift_up(v, d, sh, fill):
    return sh.up(v, (d,), fill)[0]


def _scan_down(a, u, row):
    d = 1
    while d < S:
        last = 2 * d >= S
        if d < SUBLANES:
            u = a * _shift_down(u, d, row, 0.0) + u
            if not last:
                a = a * _shift_down(a, d, row, 1.0)
        else:
            u = jnp.concatenate([u[:d], a[d:] * u[:S - d] + u[d:]], axis=0)
            if not last:
                a = jnp.concatenate([a[:d], a[d:] * a[:S - d]], axis=0)
        d *= 2
    return u


def _scan_up(b, g, row):
    d = 1
    while d < S:
        last = 2 * d >= S
        if d < SUBLANES:
            g = g + b * _shift_up(g, d, row, 0.0)
            if not last:
                b = b * _shift_up(b, d, row, 0.0)
        else:
            g = jnp.concatenate([g[:S - d] + b[:S - d] * g[d:], g[S - d:]], axis=0)
            if not last:
                b = jnp.concatenate([b[:S - d] * b[d:], b[S - d:]], axis=0)
        d *= 2
    return g


def _softplus(x):
    return jnp.maximum(x, 0.0) + jnp.log1p(jnp.exp(-jnp.abs(x)))


def _rnn_gates(x, cw, cb, wa, ba, wx, bx, lam, row, pad):
    xs = pad.down(x, (1, 2, 3), 0.0)
    xc = cb + cw[3:4, :] * x
    for j in (1, 2, 3):
        xc = xc + cw[3 - j:4 - j, :] * xs[j - 1]
    xcb = xc.astype(BF16)
    r = _sigmoid(_dot(xcb, wa) + ba)
    i = _sigmoid(_dot(xcb, wx) + bx)
    sp = _softplus(-lam)
    log_a = (-LRU_C) * r * sp
    a = jnp.exp(log_a)
    mult = jnp.where(row == 0, 1.0, jnp.sqrt(jnp.tanh(-log_a) * (1.0 + a * a)))
    return xc, xcb, r, i, sp, a, mult, xs


def _rnn_fwd(proj3, conv_w, conv_b, wa, ba, wx, bx, lam, token):
    def body(x_ref, cw_ref, cb_ref, wa_ref, ba_ref, wx_ref, bx_ref, lam_ref, _token, h_ref, pad):
        row = _row_iota()
        sh = _Shifter(pad)
        x = x_ref[0].astype(F32)
        xc, _, _, i, _, a, mult, _ = _rnn_gates(x, cw_ref[...], cb_ref[...], wa_ref[0], ba_ref[...],
                                             wx_ref[0], bx_ref[...], lam_ref[...], row, sh)
        h_ref[0] = _scan_down(a, mult * (i * xc), sh)

    vec = lambda: pl.BlockSpec((1, RBW), lambda b, n: (0, n))
    mat = lambda: pl.BlockSpec((1, RBW, RBW), lambda b, n: (n, 0, 0))
    return pl.pallas_call(
        body, grid=(BL, NRB), name="rnn_fwd",
        in_specs=[pl.BlockSpec((1, S, RBW), lambda b, n: (b, 0, n)),
                  pl.BlockSpec((CONVW, RBW), lambda b, n: (0, n)),
                  vec(), mat(), vec(), mat(), vec(), vec(), pl.BlockSpec((8, LANES), lambda b, n: (0, 0))],
        out_specs=pl.BlockSpec((1, S, RBW), lambda b, n: (b, 0, n)),
        out_shape=jax.ShapeDtypeStruct((BL, S, DR), F32),
        scratch_shapes=[pltpu.VMEM((N_SHIFT_BUFS, S + 2 * SUBLANES, RBW), F32)],
        compiler_params=_cp(("parallel", "parallel"), VMEM_MID),
    )(*_hbm(proj3, conv_w, conv_b, wa, ba, wx, bx, lam, token))


def _rnn_bwd(proj3, h3, dh3, slab_a3, conv_w, conv_b, wa, ba, wx, bx, lam, token):
    def body(x_ref, h_ref, dh_ref, cw_ref, cb_ref, wa_ref, ba_ref, wx_ref, bx_ref, lam_ref, _alias, _token,
             dx_ref, dcw_ref, dcb_ref, dwa_ref, dba_ref, dwx_ref, dbx_ref, dlam_ref, pad):
        row = _row_iota()
        sh = _Shifter(pad)
        x = x_ref[0].astype(F32)
        cw = cw_ref[...]
        wa_v = wa_ref[0]
        wx_v = wx_ref[0]
        lam_v = lam_ref[...]
        xc, xcb, r, i, sp, a, mult, xs = _rnn_gates(x, cw, cb_ref[...], wa_v, ba_ref[...], wx_v, bx_ref[...], lam_v,
                                                    row, sh)
        h = h_ref[0]
        g = _scan_up(_shift_up(a, 1, sh, 0.0), dh_ref[0], sh)
        da = g * _shift_down(h, 1, sh, 0.0)
        dmult = jnp.where(row == 0, 0.0, g * (i * xc))
        gm = g * mult
        di = gm * xc
        dxc = gm * i
        dlog_a = da * a - dmult * (a * a) / mult
        dr = dlog_a * ((-LRU_C) * sp)
        dsp = jnp.sum(dlog_a * ((-LRU_C) * r), axis=0, keepdims=True)
        dlam = dsp * (-_sigmoid(-lam_v))
        dpa = dr * r * (1.0 - r)
        dpx = di * i * (1.0 - i)
        dpab = dpa.astype(BF16)
        dpxb = dpx.astype(BF16)
        dwa = _dot_tn(xcb, dpab)
        dwx = _dot_tn(xcb, dpxb)
        dba = jnp.sum(dpa, axis=0, keepdims=True)
        dbx = jnp.sum(dpx, axis=0, keepdims=True)
        dxc = dxc + _dot_nt(dpab, wa_v) + _dot_nt(dpxb, wx_v)
        dcb = jnp.sum(dxc, axis=0, keepdims=True)
        dx = cw[3:4, :] * dxc
        dcw_rows = [None] * CONVW
        dcw_rows[3] = jnp.sum(dxc * x, axis=0, keepdims=True)
        dxc_up = sh.up(dxc, (1, 2, 3), 0.0)
        for j in (1, 2, 3):
            dx = dx + cw[3 - j:4 - j, :] * dxc_up[j - 1]
            dcw_rows[3 - j] = jnp.sum(dxc * xs[j - 1], axis=0, keepdims=True)
        dx_ref[0] = dx.astype(BF16)
        dcw = jnp.concatenate(dcw_rows, axis=0)
        first = pl.program_id(1) == 0

        @pl.when(first)
        def _():
            dcw_ref[...] = dcw
            dcb_ref[...] = dcb
            dwa_ref[0] = dwa
            dba_ref[...] = dba
            dwx_ref[0] = dwx
            dbx_ref[...] = dbx
            dlam_ref[...] = dlam

        @pl.when(jnp.logical_not(first))
        def _():
            dcw_ref[...] += dcw
            dcb_ref[...] += dcb
            dwa_ref[0] += dwa
            dba_ref[...] += dba
            dwx_ref[0] += dwx
            dbx_ref[...] += dbx
            dlam_ref[...] += dlam

    slab = lambda: pl.BlockSpec((1, S, RBW), lambda n, b: (b, 0, n))
    vec = lambda: pl.BlockSpec((1, RBW), lambda n, b: (0, n))
    mat = lambda: pl.BlockSpec((1, RBW, RBW), lambda n, b: (n, 0, 0))
    taps = lambda: pl.BlockSpec((CONVW, RBW), lambda n, b: (0, n))
    vshape = jax.ShapeDtypeStruct((1, DR), F32)
    mshape = jax.ShapeDtypeStruct((NRB, RBW, RBW), F32)
    return pl.pallas_call(
        body, grid=(NRB, BL), name="rnn_bwd",
        in_specs=[slab(), slab(), slab(), taps(), vec(), mat(), vec(), mat(), vec(), vec(),
                  pl.BlockSpec(memory_space=pl.ANY), pl.BlockSpec((8, LANES), lambda n, b: (0, 0))],
        out_specs=[slab(), taps(), vec(), mat(), vec(), mat(), vec(), vec()],
        out_shape=[jax.ShapeDtypeStruct((BL, S, A_W), BF16), jax.ShapeDtypeStruct((CONVW, DR), F32),
                   vshape, mshape, vshape, mshape, vshape, vshape],
        input_output_aliases={10: 0},
        scratch_shapes=[pltpu.VMEM((N_SHIFT_BUFS, S + 2 * SUBLANES, RBW), F32)],
        compiler_params=_cp(("parallel", "arbitrary"), 48 * 1024 * 1024),
    )(*_hbm(proj3, h3, dh3, conv_w, conv_b, wa, ba, wx, bx, lam, slab_a3, token))


NQB = S // QB


def _rms_head(t, gain):
    rstd = lax.rsqrt(jnp.mean(t * t, axis=-1, keepdims=True) + EPS)
    return t * rstd * gain


def _rope(t, cs, sn):
    return t * cs + pltpu.roll(t, HD // 2, 1) * sn


def _rope_t(dy, cs, sn):
    return dy * cs - pltpu.roll(dy, HD // 2, 1) * sn


def _bdot_nt(a, b):
    return lax.dot_general(a, b, (((2,), (2,)), ((0,), (0,))), preferred_element_type=F32)


def _bdot(a, b):
    return lax.dot_general(a, b, (((2,), (1,)), ((0,), (0,))), preferred_element_type=F32)


def _bdot_tn(a, b):
    return lax.dot_general(a, b, (((1,), (1,)), ((0,), (0,))), preferred_element_type=F32)


STRIDE_MAX = 4


def _permute(buf, x, dil, dst, off=0):
    ln = S // dil
    if dil == 1:
        dst[pl.ds(off, S), :] = x.astype(dst.dtype)
        return
    buf[0] = x
    if dil <= STRIDE_MAX:
        for c in range(dil):
            dst[pl.ds(off + c * ln, ln), :] = buf.at[0][pl.ds(c, ln, stride=dil), :].astype(dst.dtype)
        return
    f, r = STRIDE_MAX, dil // STRIDE_MAX
    part = S // f
    for c1 in range(f):
        buf.at[1][pl.ds(c1 * part, part), :] = buf.at[0][pl.ds(c1, part, stride=f), :]
    for c1 in range(f):
        for c2 in range(r):
            dst[pl.ds(off + (c1 + f * c2) * ln, ln), :] = (
                buf.at[1][pl.ds(c1 * part + c2, ln, stride=r), :].astype(dst.dtype))


def _unpermute(buf, xp, dil, dst):
    ln = S // dil
    if dil == 1:
        dst[...] = xp
        return
    if dil <= STRIDE_MAX:
        for c in range(dil):
            dst[pl.ds(c, ln, stride=dil), :] = xp[c * ln:(c + 1) * ln]
        return
    f, r = STRIDE_MAX, dil // STRIDE_MAX
    part = S // f
    for c1 in range(f):
        for c2 in range(r):
            c = c1 + f * c2
            buf.at[1][pl.ds(c1 * part + c2, ln, stride=r), :] = xp[c * ln:(c + 1) * ln]
    for c1 in range(f):
        dst[pl.ds(c1, part, stride=f), :] = buf[1, pl.ds(c1 * part, part), :]


def _blocks3(ref, off=0):
    return ref[pl.ds(off, S), :].reshape(NQB, QB, HD)


def _att_prep(q_ref, k_ref, v_ref, cos_ref, sin_ref, qn, kn, dil, nat, qs, ksp, vsp):
    cs = cos_ref[...]
    sn = sin_ref[...]
    zero = jnp.zeros((QB, HD), BF16)
    ksp[pl.ds(0, QB), :] = zero
    vsp[pl.ds(0, QB), :] = zero
    _permute(nat, _rope(_rms_head(q_ref[0].astype(F32), qn), cs, sn), dil, qs)
    _permute(nat, _rope(_rms_head(k_ref[0].astype(F32), kn), cs, sn), dil, ksp, QB)
    _permute(nat, v_ref[0].astype(F32), dil, vsp, QB)


def _att_scores(qs, ksp, dil):
    nb = S // dil // QB
    q3 = _blocks3(qs)
    shape = (NQB, QB, QB)
    qi = lax.broadcasted_iota(jnp.int32, shape, 1)
    kj = lax.broadcasted_iota(jnp.int32, shape, 2)
    s_c = jnp.where(qi >= kj, _bdot_nt(q3, _blocks3(ksp, QB)) * SCALE, NEG)
    if nb == 1:
        return q3, s_c, None
    jj = lax.broadcasted_iota(jnp.int32, shape, 0)
    ok = (kj >= qi) & ((jj & (nb - 1)) != 0)
    s_p = jnp.where(ok, _bdot_nt(q3, _blocks3(ksp)) * SCALE, NEG)
    return q3, s_c, s_p


def _qkv_spec(kind, g):
    base = OFF_Q // HD + (3 * g + kind) * NH
    return pl.BlockSpec((1, S, HD), lambda b, h: (b, 0, base + h))


def _attn_fwd(proj3, cos_t, sin_t, q_norm, k_norm):
    def body(*refs):
        qkv_refs = refs[:9]
        (cos_ref, sin_ref, qn_ref, kn_ref, att_ref, lse_ref, w_ref, qp_ref, kp_ref, vp_ref,
         nat, qs, ksp, vsp, og) = refs[9:]
        for g, (window, dil) in enumerate(PATTERNS):
            q_ref, k_ref, v_ref = qkv_refs[3 * g:3 * g + 3]
            _att_prep(q_ref, k_ref, v_ref, cos_ref, sin_ref, qn_ref[g:g + 1, :], kn_ref[g:g + 1, :], dil,
                      nat, qs, ksp, vsp)
            qp_ref[g, 0] = qs[...]
            kp_ref[g, 0] = ksp[pl.ds(QB, S), :]
            vp_ref[g, 0] = vsp[pl.ds(QB, S), :]
            _, s_c, s_p = _att_scores(qs, ksp, dil)
            m = jnp.max(s_c, axis=-1, keepdims=True)
            if s_p is not None:
                m = jnp.maximum(m, jnp.max(s_p, axis=-1, keepdims=True))
            e_c = jnp.exp(s_c - m)
            den = jnp.sum(e_c, axis=-1, keepdims=True)
            o = _bdot(e_c.astype(BF16), _blocks3(vsp, QB))
            if s_p is not None:
                e_p = jnp.exp(s_p - m)
                den = den + jnp.sum(e_p, axis=-1, keepdims=True)
                o = o + _bdot(e_p.astype(BF16), _blocks3(vsp))
            _unpermute(nat, (o / den).reshape(S, HD), dil, og.at[g])
            _unpermute(nat, jnp.broadcast_to(m + jnp.log(den), (NQB, QB, HD)).reshape(S, HD), dil,
                       lse_ref.at[g, 0])
        l0 = lse_ref[0, 0]
        l1 = lse_ref[1, 0]
        l2 = lse_ref[2, 0]
        mx = jnp.maximum(jnp.maximum(l0, l1), l2)
        e0 = jnp.exp(l0 - mx)
        e1 = jnp.exp(l1 - mx)
        e2 = jnp.exp(l2 - mx)
        inv = 1.0 / (e0 + e1 + e2)
        w0 = e0 * inv
        w1 = e1 * inv
        w2 = e2 * inv
        w_ref[0, 0] = w0
        w_ref[1, 0] = w1
        w_ref[2, 0] = w2
        att_ref[0] = w0 * og[0] + w1 * og[1] + w2 * og[2]

    in_specs = [_qkv_spec(kind, g) for g in range(NG) for kind in range(3)]
    in_specs += [pl.BlockSpec((S, HD), lambda b, h: (0, 0)), pl.BlockSpec((S, HD), lambda b, h: (0, 0)),
                 pl.BlockSpec((NG, HD), lambda b, h: (0, 0)), pl.BlockSpec((NG, HD), lambda b, h: (0, 0))]
    stat = lambda: pl.BlockSpec((NG, 1, S, HD), lambda b, h: (0, b, 0, h))
    return pl.pallas_call(
        body, grid=(BL, NH), name="attn_fwd",
        in_specs=in_specs,
        out_specs=[pl.BlockSpec((1, S, HD), lambda b, h: (b, 0, h)), stat(), stat(), stat(), stat(), stat()],
        out_shape=[jax.ShapeDtypeStruct((BL, S, ATT), F32),
                   jax.ShapeDtypeStruct((NG, BL, S, ATT), F32),
                   jax.ShapeDtypeStruct((NG, BL, S, ATT), F32)]
        + [jax.ShapeDtypeStruct((NG, BL, S, ATT), BF16)] * 3,
        scratch_shapes=[pltpu.VMEM((2, S, HD), F32), pltpu.VMEM((S, HD), BF16), pltpu.VMEM((S + QB, HD), BF16),
                        pltpu.VMEM((S + QB, HD), BF16), pltpu.VMEM((NG, S, HD), F32)],
        compiler_params=_cp(("parallel", "parallel"), VMEM_BIG),
    )(*_hbm(*([proj3] * 9), cos_t, sin_t, q_norm, k_norm))


def _attn_bwd_group(g, proj3, cos_t, sin_t, qn_g, kn_g, lse, wts, qkv_p, datt3, sbar3, slabs):
    dil = PATTERNS[g][1]
    n_alias = 0 if slabs is None else 3

    def norm_rope_bwd(dpost, raw, gain, cs, sn):
        dn = _rope_t(dpost, cs, sn)
        rstd = lax.rsqrt(jnp.mean(raw * raw, axis=-1, keepdims=True) + EPS)
        xh = raw * rstd
        dgain = jnp.sum(dn * xh, axis=0, keepdims=True)
        gd = dn * gain
        draw = rstd * (gd - xh * jnp.mean(gd * xh, axis=-1, keepdims=True))
        return draw, dgain

    def body(*refs):
        (q_ref, k_ref, qp_ref, kp_ref, vp_ref, cos_ref, sin_ref, qn_ref, kn_ref, lse_ref, w_ref, datt_ref,
         sbar_ref) = refs[:13]
        (dq_ref, dk_ref, dv_ref, dqn_ref, dkn_ref, nat, ksp, vsp, dos, cvp, lsp, acc) = refs[13 + n_alias:]
        qn = qn_ref[...]
        kn = kn_ref[...]
        cs = cos_ref[...]
        sn = sin_ref[...]
        qs = qp_ref.at[0, 0]
        zero = jnp.zeros((QB, HD), BF16)
        ksp[pl.ds(0, QB), :] = zero
        vsp[pl.ds(0, QB), :] = zero
        ksp[pl.ds(QB, S), :] = kp_ref[0, 0]
        vsp[pl.ds(QB, S), :] = vp_ref[0, 0]
        wv = w_ref[0, 0]
        _permute(nat, wv * datt_ref[0], dil, dos)
        _permute(nat, wv * sbar_ref[0], dil, cvp)
        _permute(nat, lse_ref[0, 0], dil, lsp)
        q3, s_c, s_p = _att_scores(qs, ksp, dil)
        do3 = _blocks3(dos)
        lse3 = _blocks3(lsp)[:, :, 0:1]
        cv3 = _blocks3(cvp)[:, :, 0:1]
        p_c = jnp.exp(s_c - lse3)
        ds_c = (p_c * (_bdot_nt(do3, _blocks3(vsp, QB)) - cv3)).astype(BF16)
        dq = _bdot(ds_c, _blocks3(ksp, QB))
        acc[0] = _bdot_tn(ds_c, q3).reshape(S, HD)
        acc[1] = _bdot_tn(p_c.astype(BF16), do3).reshape(S, HD)
        if s_p is not None:
            p_p = jnp.exp(s_p - lse3)
            ds_p = (p_p * (_bdot_nt(do3, _blocks3(vsp)) - cv3)).astype(BF16)
            dq = dq + _bdot(ds_p, _blocks3(ksp))
            early = pl.ds(0, S - QB)
            acc[0, early, :] += _bdot_tn(ds_p, q3).reshape(S, HD)[QB:]
            acc[1, early, :] += _bdot_tn(p_p.astype(BF16), do3).reshape(S, HD)[QB:]
        _unpermute(nat, (dq * SCALE).reshape(S, HD), dil, nat.at[0])
        draw, dqn = norm_rope_bwd(nat[0], q_ref[0].astype(F32), qn, cs, sn)
        dq_ref[0] = draw.astype(BF16)
        _unpermute(nat, acc[0] * SCALE, dil, nat.at[0])
        draw, dkn = norm_rope_bwd(nat[0], k_ref[0].astype(F32), kn, cs, sn)
        dk_ref[0] = draw.astype(BF16)
        _unpermute(nat, acc[1], dil, nat.at[0])
        dv_ref[0] = nat[0].astype(BF16)
        first = (pl.program_id(0) == 0) & (pl.program_id(1) == 0)

        @pl.when(first)
        def _():
            dqn_ref[...] = dqn
            dkn_ref[...] = dkn

        @pl.when(jnp.logical_not(first))
        def _():
            dqn_ref[...] += dqn
            dkn_ref[...] += dkn

    full = lambda r: pl.BlockSpec((r, HD), lambda b, h: (0, 0))
    stat = lambda: pl.BlockSpec((1, 1, S, HD), lambda b, h: (g, b, 0, h))
    slab = lambda: pl.BlockSpec((1, S, HD), lambda b, h: (b, 0, h))
    out_slab = lambda: pl.BlockSpec((1, S, HD), lambda b, h: (b, 0, g * NH + h))
    big = jax.ShapeDtypeStruct((BL, S, GW), BF16)
    vecs = jax.ShapeDtypeStruct((1, HD), F32)
    in_specs = [_qkv_spec(0, g), _qkv_spec(1, g), stat(), stat(), stat(), full(S), full(S), full(1), full(1),
                stat(), stat(), slab(), slab()]
    args = [proj3, proj3, *qkv_p, cos_t, sin_t, qn_g, kn_g, lse, wts, datt3, sbar3]
    aliases = {}
    if slabs is not None:
        in_specs += [pl.BlockSpec(memory_space=pl.ANY)] * 3
        args += list(slabs)
        aliases = {13: 0, 14: 1, 15: 2}
    return pl.pallas_call(
        body, grid=(BL, NH), name="attn_bwd_g%d" % g,
        in_specs=in_specs,
        out_specs=[out_slab(), out_slab(), out_slab(), full(1), full(1)],
        out_shape=[big, big, big, vecs, vecs],
        scratch_shapes=[pltpu.VMEM((2, S, HD), F32), pltpu.VMEM((S + QB, HD), BF16),
                        pltpu.VMEM((S + QB, HD), BF16), pltpu.VMEM((S, HD), BF16), pltpu.VMEM((S, HD), F32),
                        pltpu.VMEM((S, HD), F32), pltpu.VMEM((2, S, HD), F32)],
        input_output_aliases=aliases,
        compiler_params=_cp(("arbitrary", "arbitrary"), VMEM_BIG),
    )(*_hbm(*args))


def _tail(x, proj, h, att, p, tgt, w_o_rnn, w_o_att_t, w_out, w_pg, w_ple_t, norm_ple, b_pg, tm=256):
    nt = T // tm
    inv_d = 1.0 / D

    def body(x_ref, h_ref, zr_ref, att_ref, za_ref, g0a_ref, g0b_ref, g1a_ref, g1b_ref, p_ref, tgt_ref,
             np_ref, bpg_ref, wor_hbm, woa_hbm, wout_hbm, wpg_hbm, wple_hbm,
             dx1_ref, merged_ref, n1_ref, dpre_ref, dpe_ref, dyr_ref, dya_ref, slab_a_ref, slab_c_ref, dh_ref,
             datt_ref, sbar_ref, yrnn_ref, yatt_ref, loss_ref, dnp_ref, dbpg_ref,
             wor, woa, wout, wpg, wple, sems):
        first = pl.program_id(0) == 0

        @pl.when(first)
        def _():
            pairs = ((wor_hbm, wor), (woa_hbm, woa), (wout_hbm, wout), (wpg_hbm, wpg), (wple_hbm, wple))
            _copy_together([pltpu.make_async_copy(src, dst, sems.at[k]) for k, (src, dst) in enumerate(pairs)])

        xv = x_ref[...]
        hv = h_ref[...]
        zr = zr_ref[...].astype(F32)
        av = att_ref[...]
        za = za_ref[...].astype(F32)
        szr = _sigmoid(zr)
        silu_r = zr * szr
        yrnn_b = (hv * silu_r).astype(BF16)
        sza = _sigmoid(za)
        silu_a = za * sza
        yatt_b = (av * silu_a).astype(BF16)
        yrnn_ref[...] = yrnn_b
        yatt_ref[...] = yatt_b
        yr = _dot(yrnn_b, wor[...])
        ya = _dot_nt(yatt_b, woa[...])
        g0 = _sigmoid(jnp.concatenate([g0a_ref[...], g0b_ref[...]], axis=1).astype(F32))
        g1 = _sigmoid(jnp.concatenate([g1a_ref[...], g1b_ref[...]], axis=1).astype(F32))
        merged_b = (g0 * yr + g1 * ya).astype(BF16)
        merged_ref[...] = merged_b
        x1 = xv + _dot(merged_b, wout[...])
        rstd = lax.rsqrt(jnp.mean(x1 * x1, axis=-1, keepdims=True) + EPS)
        xh = x1 * rstd
        npl = np_ref[...]
        n1_b = (xh * npl).astype(BF16)
        n1_ref[...] = n1_b
        pg = _sigmoid(_dot(n1_b, wpg[...]) + bpg_ref[...])
        pe = _dot_nt(p_ref[...].astype(BF16), wple[...])
        err = x1 + pg * pe - tgt_ref[...]
        loss_t = 0.5 * inv_d * jnp.sum(err * err)
        dy = err * inv_d
        dpe_ref[...] = (dy * pg).astype(BF16)
        dpre = dy * pe * pg * (1.0 - pg)
        dpre_b = dpre.astype(BF16)
        dpre_ref[...] = dpre_b
        dn1 = _dot_nt(dpre_b, wpg[...])
        dnp = jnp.sum(dn1 * xh, axis=0, keepdims=True)
        dbpg = jnp.sum(dpre, axis=0, keepdims=True)
        gd = dn1 * npl
        dx1 = dy + rstd * (gd - xh * jnp.mean(gd * xh, axis=-1, keepdims=True))
        dx1_ref[...] = dx1
        dmerged = _dot_nt(dx1.astype(BF16), wout[...])
        dyr_b = (dmerged * g0).astype(BF16)
        dya_b = (dmerged * g1).astype(BF16)
        dyr_ref[...] = dyr_b
        dya_ref[...] = dya_b
        slab_c_ref[:, ATT:ATT + D] = (dmerged * yr * g0 * (1.0 - g0)).astype(BF16)
        slab_c_ref[:, ATT + D:ATT + 2 * D] = (dmerged * ya * g1 * (1.0 - g1)).astype(BF16)
        dyrnn = _dot_nt(dyr_b, wor[...])
        dyatt = _dot(dya_b, woa[...])
        dh_ref[...] = dyrnn * silu_r
        slab_a_ref[...] = (dyrnn * hv * szr * (1.0 + zr * (1.0 - szr))).astype(BF16)
        datt = dyatt * silu_a
        datt_ref[...] = datt
        slab_c_ref[:, 0:ATT] = (dyatt * av * sza * (1.0 + za * (1.0 - sza))).astype(BF16)
        da = datt * av
        for hh in range(NH):
            seg = slice(hh * HD, (hh + 1) * HD)
            sbar_ref[:, seg] = jnp.broadcast_to(jnp.sum(da[:, seg], axis=-1, keepdims=True), (tm, HD))

        @pl.when(first)
        def _():
            loss_ref[...] = jnp.full((8, LANES), loss_t, F32)
            dnp_ref[...] = dnp
            dbpg_ref[...] = dbpg

        @pl.when(jnp.logical_not(first))
        def _():
            loss_ref[...] += jnp.full((8, LANES), loss_t, F32)
            dnp_ref[...] += dnp
            dbpg_ref[...] += dbpg

    tok = lambda w: pl.BlockSpec((tm, w), lambda i: (i, 0))
    col = lambda w, blk: pl.BlockSpec((tm, w), lambda i: (i, blk))
    vec = lambda: pl.BlockSpec((1, D), lambda i: (0, 0))
    hbm = lambda: pl.BlockSpec(memory_space=pl.ANY)
    gb = OFF_G // 512
    in_specs = [tok(D), tok(DR), col(DR, 1), tok(ATT), col(ATT, OFF_ZA // ATT),
                col(512, gb), col(512, gb + 1), col(512, gb + 2), col(512, gb + 3),
                tok(PLE), tok(D), vec(), vec(), hbm(), hbm(), hbm(), hbm(), hbm()]
    sh = lambda w, dt: jax.ShapeDtypeStruct((T, w), dt)
    out_shape = [sh(D, F32), sh(D, BF16), sh(D, BF16), sh(D, BF16), sh(D, BF16), sh(D, BF16), sh(D, BF16),
                 sh(A_W, BF16), sh(C_W, BF16), sh(DR, F32), sh(ATT, F32), sh(ATT, F32),
                 sh(DR, BF16), sh(ATT, BF16),
                 jax.ShapeDtypeStruct((8, LANES), F32), jax.ShapeDtypeStruct((1, D), F32),
                 jax.ShapeDtypeStruct((1, D), F32)]
    out_specs = [tok(D), tok(D), tok(D), tok(D), tok(D), tok(D), tok(D), col(DR, 1), tok(C_W), tok(DR),
                 tok(ATT), tok(ATT), tok(DR), tok(ATT),
                 pl.BlockSpec((8, LANES), lambda i: (0, 0)), vec(), vec()]
    return pl.pallas_call(
        body, grid=(nt,), name="tail_fwd_bwd",
        in_specs=in_specs, out_specs=out_specs, out_shape=out_shape,
        scratch_shapes=[pltpu.VMEM((DR, D), BF16), pltpu.VMEM((D, ATT), BF16), pltpu.VMEM((D, D), BF16),
                        pltpu.VMEM((D, D), BF16), pltpu.VMEM((D, PLE), BF16), pltpu.SemaphoreType.DMA((5,))],
        compiler_params=_cp(("arbitrary",), VMEM_BIG),
    )(*_hbm(x, h, proj, att, proj, proj, proj, proj, proj, p, tgt, norm_ple, b_pg, w_o_rnn, w_o_att_t, w_out, w_pg,
            w_ple_t))


def _rope_tables():
    pos = jnp.arange(S, dtype=F32)
    inv_freq = ROPE_THETA ** (-jnp.arange(0, HD, 2, dtype=F32) / HD)
    ang = pos[:, None] * inv_freq[None, :]
    cos, sin = jnp.cos(ang), jnp.sin(ang)
    return jnp.concatenate([cos, cos], axis=1), jnp.concatenate([-sin, sin], axis=1)


def _local_step(x, p, tgt, project, other_weights, norm_mix, conv_b,
                w_rg_a, b_rg_a, w_rg_x, b_rg_x, lam, q_norm, k_norm, norm_ple, b_pg, start_reduce=None,
                entry_token=None):
    if start_reduce is None:
        start_reduce = lambda arrs, tag: (jnp.zeros((8, LANES), F32), arrs)
    if entry_token is None:
        entry_token = jnp.zeros((8, LANES), F32)
    cos_t, sin_t = _rope_tables()
    wa_b = w_rg_a.astype(BF16)
    wx_b = w_rg_x.astype(BF16)

    hn = _rmsnorm_fwd(x, norm_mix, entry_token)
    proj, w_bufs, chips, conv_w, token = project(hn)
    proj3 = proj.reshape(BL, S, NIN)
    h3 = _rnn_fwd(proj3, conv_w, conv_b, wa_b, b_rg_a, wx_b, b_rg_x, lam, token)
    att3, lse, wts, *qkv_p = _attn_fwd(proj3, cos_t, sin_t, q_norm, k_norm)
    w_o_rnn, w_o_att_t, w_out, w_pg, w_ple_t = other_weights(att3)
    (dx1, merged, n1, dpre, dpe, dyr, dya, slab_a, slab_c, dh, datt, sbar, yrnn, yatt, loss8, dnp, dbpg) = _tail(
        x, proj, h3.reshape(T, DR), att3.reshape(T, ATT), p, tgt, w_o_rnn, w_o_att_t, w_out, w_pg, w_ple_t,
        norm_ple, b_pg)

    token, pending_out = start_reduce([
        _mm_tn(yrnn, dyr, 640, 2048, "dw_o_rnn"),
        _mm_tn(dya, yatt, 512, 2048, "dw_o_att_t"),
        _mm_tn(merged, dx1, 512, 2048, "dw_out"),
        _mm_tn(n1, dpre, 512, 2048, "dw_ple_gate"),
        _mm_tn(dpe, p, 512, 2048, "dw_ple_t")], "out")

    slab_a3, dcw, dcb, dwa, dba, dwx, dbx, dlam = _rnn_bwd(
        proj3, h3, dh.reshape(BL, S, DR), slab_a.reshape(BL, S, A_W), conv_w, conv_b, wa_b, b_rg_a, wx_b, b_rg_x, lam,
        token)
    datt3 = datt.reshape(BL, S, ATT)
    sbar3 = sbar.reshape(BL, S, ATT)
    slabs = None
    dqn = []
    dkn = []
    for g in range(NG):
        dq, dk, dv, dqn_g, dkn_g = _attn_bwd_group(g, proj3, cos_t, sin_t, q_norm[g:g + 1], k_norm[g:g + 1],
                                                   lse, wts, qkv_p, datt3, sbar3, slabs)
        slabs = (dq, dk, dv)
        dqn.append(dqn_g)
        dkn.append(dkn_g)
    pieces = [slab_a3.reshape(T, A_W)] + [t.reshape(T, GW) for t in slabs] + [slab_c]
    dw_in_t, db_in = _dw_in(pieces, hn)
    token, pending_in = start_reduce([dw_in_t], "in")
    grad_x, dnm = _grad_x(pieces, w_bufs, chips, token, x, dx1, norm_mix)

    small = dict(w_rg_a=dwa, w_rg_x=dwx, norm_mix=dnm, b_in=db_in, conv_b=dcb, b_rg_a=dba, b_rg_x=dbx,
                 lru_lambda=dlam, q_norm=dqn, k_norm=dkn, norm_ple=dnp, b_ple_gate=dbpg, conv_w=dcw, loss=loss8)
    return grad_x, pending_out, pending_in, small


MESH = pl.DeviceIdType.MESH
HBM_SPEC = pl.BlockSpec(memory_space=pl.ANY)


def _my_pos():
    return lax.axis_index("x"), lax.axis_index("y"), lax.axis_index("c")


def _flip(pos, k):
    x, y, c = pos
    return (1 - x if k & 4 else x, 1 - y if k & 2 else y, 1 - c if k & 1 else c)


def _lin(pos):
    return 4 * pos[0] + 2 * pos[1] + pos[2]


def _chip(pos):
    return 2 * pos[0] + pos[1]


def _exchange_within_chip(parts, name):
    na = len(parts)

    def body(*refs):
        a_refs = refs[:na]
        recv_refs = refs[na:2 * na]
        send_sems, recv_sems = refs[2 * na:]
        me = _my_pos()
        c = me[2]
        sibling = _flip(me, 1)
        remote = []
        for i in range(na):
            for q in range(NCHIP):
                rc = pltpu.make_async_remote_copy(
                    src_ref=a_refs[i].at[q, 1 - c], dst_ref=recv_refs[i].at[q],
                    send_sem=send_sems.at[NCHIP * i + q], recv_sem=recv_sems.at[NCHIP * i + q],
                    device_id=sibling, device_id_type=MESH)
                rc.start()
                remote.append(rc)
        for rc in remote:
            rc.wait_recv()
        for rc in remote:
            rc.wait_send()

    return pl.pallas_call(
        body, name=name, out_shape=[jax.ShapeDtypeStruct((NCHIP,) + a.shape[2:], a.dtype) for a in parts],
        in_specs=[HBM_SPEC] * na, out_specs=[HBM_SPEC] * na,
        scratch_shapes=[pltpu.SemaphoreType.DMA((NCHIP * na,)), pltpu.SemaphoreType.DMA((NCHIP * na,))],
    )(*parts)


HBM_ONLY = pl.BlockSpec(memory_space=pltpu.HBM)
SEM_SPEC = pl.BlockSpec(memory_space=pltpu.SEMAPHORE)
SPLIT_COPY = pltpu.CompilerParams(has_side_effects=pltpu.SideEffectType.DATAFLOW_SIDE_EFFECTING)


def _chip_peers(me):
    return [_flip(me, 4), _flip(me, 2), _flip(me, 6)]


def _between_chips_start(parts, name):
    na = len(parts)

    def body(*refs):
        a_refs = refs[:na]
        land_refs = refs[na:2 * na]
        send_sems, recv_sems = refs[2 * na], refs[2 * na + 1]
        token = refs[-1]
        me = _my_pos()
        myq = _chip(me)
        for i in range(na):
            for j, peer in enumerate(_chip_peers(me)):
                pltpu.make_async_remote_copy(
                    src_ref=a_refs[i].at[_chip(peer)], dst_ref=land_refs[i].at[myq],
                    send_sem=send_sems.at[3 * i + j], recv_sem=recv_sems.at[3 * i + j],
                    device_id=peer, device_id_type=MESH).start()
        token[...] = jnp.zeros_like(token)

    hbm = [pltpu.HBM(a.shape, a.dtype) for a in parts]
    srcs = [pltpu.with_memory_space_constraint(a, pltpu.HBM) for a in parts]
    lands = [pltpu.with_memory_space_constraint(lax.empty(a.shape, a.dtype), pltpu.HBM) for a in parts]
    res = pl.pallas_call(
        body, name=name,
        out_shape=(pltpu.SemaphoreType.DMA((3 * na,)), pltpu.SemaphoreType.DMA((3 * na,)), *hbm, *hbm,
                   jax.ShapeDtypeStruct((8, LANES), F32)),
        in_specs=[HBM_ONLY] * (2 * na),
        out_specs=(SEM_SPEC, SEM_SPEC, *([HBM_ONLY] * (2 * na)), pl.BlockSpec(memory_space=pltpu.VMEM)),
        input_output_aliases={i: 2 + i for i in range(2 * na)},
        compiler_params=SPLIT_COPY,
    )(*srcs, *lands)
    return res[-1], (res[0], res[1], list(res[2:2 + na]), list(res[2 + na:2 + 2 * na]))


def _between_chips_wait(pending, after, name):
    send_sems, recv_sems, parts, lands = pending
    na = len(parts)

    def body(*refs):
        a_refs = refs[:na]
        land_refs = refs[na:2 * na]
        send_sems, recv_sems = refs[2 * na], refs[2 * na + 1]
        me = _my_pos()
        for i in range(na):
            for j, peer in enumerate(_chip_peers(me)):
                cp = pltpu.make_async_remote_copy(
                    src_ref=a_refs[i].at[_chip(peer)], dst_ref=land_refs[i].at[_chip(peer)],
                    send_sem=send_sems.at[3 * i + j], recv_sem=recv_sems.at[3 * i + j],
                    device_id=peer, device_id_type=MESH)
                cp.wait_send()
                cp.wait_recv()

    hbm = [pltpu.HBM(a.shape, a.dtype) for a in parts]
    res = pl.pallas_call(
        body, name=name, out_shape=(*hbm, *hbm),
        in_specs=[HBM_ONLY] * (2 * na) + [SEM_SPEC, SEM_SPEC, pl.BlockSpec(memory_space=pl.ANY)],
        out_specs=[HBM_ONLY] * (2 * na),
        input_output_aliases={i: i for i in range(2 * na)},
        compiler_params=SPLIT_COPY,
    )(*parts, *lands, send_sems, recv_sems, after)
    return list(res[:na]), list(res[na:])


def _remote(src, dst, send_sems, recv_sems, idx, peer):
    return pltpu.make_async_remote_copy(src_ref=src, dst_ref=dst, send_sem=send_sems.at[idx],
                                        recv_sem=recv_sems.at[idx], device_id=peer, device_id_type=MESH)


def _copies_own(bufs, me):
    return [(bufs[0], bufs[1].at[me[2]], 0, _flip(me, 1))]


def _copies_near(bufs, me):
    return [(bufs[0], bufs[1].at[0, me[2]], 0, _flip(me, 2)), (bufs[0], bufs[1].at[1, me[2]], 1, _flip(me, 4))]


def _copies_far(bufs, me):
    return [(bufs[0], bufs[1].at[me[2]], 0, _flip(me, 6))]


def _copies_others(bufs, me):
    na = len(bufs) // 2
    return [(bufs[i], bufs[na + i].at[_lin(me)], 7 * i + k - 1, _flip(me, k))
            for i in range(na) for k in range(1, NDEV)]


def _copies_exchange(bufs, me):
    na = len(bufs) // 2
    return [(bufs[i].at[_lin(_flip(me, k))], bufs[na + i].at[_lin(me)], 7 * i + k - 1, _flip(me, k))
            for i in range(na) for k in range(1, NDEV)]


GROUP_COPIES = dict(own=_copies_own, near=_copies_near, far=_copies_far, others=_copies_others,
                    exchange=_copies_exchange)
GROUP_COUNT = dict(own=1, near=2, far=1)
TO_ALL = ("others", "exchange")


def _gather_start(bufs, groups, after, name):
    nb = len(bufs)
    ng = len(groups)

    def body(*refs):
        b = refs[:nb]
        sems = refs[nb + 1:nb + 1 + 2 * ng]
        token = refs[-1]
        me = _my_pos()
        for gi, (group, idx) in enumerate(groups):
            for src, dst, k, peer in GROUP_COPIES[group]([b[i] for i in idx], me):
                _remote(src, dst, sems[2 * gi], sems[2 * gi + 1], k, peer).start()
        token[...] = jnp.zeros_like(token)

    sem_t = []
    for group, idx in groups:
        cnt = 7 * (len(idx) // 2) if group in TO_ALL else GROUP_COUNT[group]
        sem_t += [pltpu.SemaphoreType.DMA((cnt,)), pltpu.SemaphoreType.DMA((cnt,))]
    ins = [pltpu.with_memory_space_constraint(a, pltpu.HBM) for a in bufs]
    res = pl.pallas_call(
        body, name=name,
        out_shape=(*sem_t, *[pltpu.HBM(a.shape, a.dtype) for a in bufs], jax.ShapeDtypeStruct((8, LANES), F32)),
        in_specs=[HBM_ONLY] * nb + [pl.BlockSpec(memory_space=pl.ANY)],
        out_specs=(*([SEM_SPEC] * (2 * ng)), *([HBM_ONLY] * nb), pl.BlockSpec(memory_space=pltpu.VMEM)),
        input_output_aliases={i: 2 * ng + i for i in range(nb)},
        compiler_params=SPLIT_COPY,
    )(*ins, after)
    return res[-1], list(res[2 * ng:2 * ng + nb]), [(res[2 * gi], res[2 * gi + 1]) for gi in range(ng)]


def _gather_wait(group, send_sems, recv_sems, bufs, after, name):
    nb = len(bufs)
    copies = GROUP_COPIES[group]

    def body(*refs):
        b = refs[:nb]
        ss, rs = refs[nb], refs[nb + 1]
        me = _my_pos()
        for src, dst, idx, peer in copies(b, me):
            if group in TO_ALL:
                landed = b[nb // 2 + idx // 7].at[_lin(peer)]
            elif group == "own":
                landed = b[1].at[1 - me[2]]
            else:
                landed = dst
            cp = _remote(src, landed, ss, rs, idx, peer)
            cp.wait_send()
            cp.wait_recv()

    res = pl.pallas_call(
        body, name=name, out_shape=[pltpu.HBM(a.shape, a.dtype) for a in bufs],
        in_specs=[HBM_ONLY] * nb + [SEM_SPEC, SEM_SPEC, pl.BlockSpec(memory_space=pl.ANY)],
        out_specs=[HBM_ONLY] * nb,
        input_output_aliases={i: i for i in range(nb)},
        compiler_params=SPLIT_COPY,
    )(*bufs, send_sems, recv_sems, after)
    return list(res)


def _forward_to_sibling(buf, name):
    n = buf.shape[0]

    def body(_in_ref, out_ref, send_sems, recv_sems):
        me = _my_pos()
        c = me[2]
        sibling = _flip(me, 1)
        sends = []
        for r in range(n):
            cp = _remote(out_ref.at[r, c], out_ref.at[r, c], send_sems, recv_sems, r, sibling)
            cp.start()
            sends.append(cp)
        for r in range(n):
            _remote(out_ref.at[r, c], out_ref.at[r, 1 - c], send_sems, recv_sems, r, sibling).wait_recv()
        for cp in sends:
            cp.wait_send()

    return pl.pallas_call(
        body, name=name, out_shape=jax.ShapeDtypeStruct(buf.shape, buf.dtype),
        in_specs=[HBM_SPEC], out_specs=HBM_SPEC,
        scratch_shapes=[pltpu.SemaphoreType.DMA((n,)), pltpu.SemaphoreType.DMA((n,))],
        input_output_aliases={0: 0},
    )(buf)


def _scalar(v):
    return jnp.asarray(v, jnp.int32).reshape(1)


def _sum_pairs(parts, theirs, name):
    na = len(parts)

    def body(c_ref, *refs):
        for i in range(na):
            o_ref = refs[2 * na + i]
            o_ref[0] = (refs[i][0, 0].astype(F32) + refs[na + i][0].astype(F32)).astype(o_ref.dtype)

    def mine_spec(a):
        return pl.BlockSpec((1, 1) + a.shape[2:], lambda q, c_ref: (q, c_ref[0], 0, 0))

    def spec(a):
        return pl.BlockSpec((1,) + a.shape[1:], lambda q, c_ref: (q, 0, 0))

    return pl.pallas_call(
        body, name=name,
        grid_spec=pltpu.PrefetchScalarGridSpec(
            num_scalar_prefetch=1, grid=(NCHIP,),
            in_specs=[mine_spec(a) for a in parts] + [spec(a) for a in theirs],
            out_specs=[spec(a) for a in theirs]),
        out_shape=[jax.ShapeDtypeStruct(a.shape, a.dtype) for a in theirs],
        compiler_params=_cp(("arbitrary",), VMEM_BIG),
    )(_scalar(lax.axis_index("c")), *_hbm(*parts, *theirs))


def _others(q, mine, nblk=NCHIP):
    return jnp.where(q == mine, (q + 1) % nblk, q)


def _sum_chips_adamw(own, recv, wv, mv, vv, token, tr, name):
    _, r, w = recv.shape

    def body(q_ref, own_ref, r0, r1, r2, r3, w_ref, m_ref, v_ref, _token, g_ref, d_ref, m2_ref, v2_ref):
        myq = q_ref[0]
        acc = None
        for q, r_ref in enumerate((r0, r1, r2, r3)):
            term = jnp.where(myq == q, own_ref[0], r_ref[0]).astype(F32)
            acc = term if acc is None else acc + term
        g_ref[...] = acc
        delta, m2, v2 = _adam_math(w_ref[...], acc, m_ref[...], v_ref[...])
        d_ref[...] = delta
        m2_ref[...] = m2
        v2_ref[...] = v2

    def recv_spec(q):
        return pl.BlockSpec((1, tr, w), lambda i, q_ref: (_others(q, q_ref[0]), i, 0))

    rows = lambda: pl.BlockSpec((tr, w), lambda i, q_ref: (i, 0))
    shp = jax.ShapeDtypeStruct((r, w), F32)
    return pl.pallas_call(
        body, name=name,
        grid_spec=pltpu.PrefetchScalarGridSpec(
            num_scalar_prefetch=1, grid=(r // tr,),
            in_specs=[pl.BlockSpec((1, tr, w), lambda i, q_ref: (q_ref[0], i, 0))]
            + [recv_spec(q) for q in range(NCHIP)] + [rows(), rows(), rows()]
            + [pl.BlockSpec((8, LANES), lambda i, q_ref: (0, 0))],
            out_specs=[rows(), rows(), rows(), rows()]),
        out_shape=[shp, shp, shp, shp],
        compiler_params=_cp(("arbitrary",), VMEM_MID),
    )(_scalar(_chip(_my_pos())), *_hbm(own, recv, recv, recv, recv, wv, mv, vv, token))


def _sum_blocks_small(own, recv, mine, transpose, name):
    na = len(recv)
    nblk = recv[0].shape[0]

    def body(q_ref, *refs):
        me = q_ref[0]
        for i in range(na):
            acc = None
            for q in range(nblk):
                term = jnp.where(me == q, refs[i][0], refs[na * (1 + q) + i][0]).astype(F32)
                acc = term if acc is None else acc + term
            refs[na * (1 + nblk) + i][...] = acc.T if transpose[i] else acc

    def oshape(a, tr):
        r, w = a.shape[1:]
        return (w, r) if tr else (r, w)

    own_spec = lambda a: pl.BlockSpec((1,) + a.shape[1:], lambda s, q_ref: (q_ref[0], 0, 0))
    recv_spec = lambda a, q: pl.BlockSpec((1,) + a.shape[1:], lambda s, q_ref: (_others(q, q_ref[0], nblk), 0, 0))
    out_spec = lambda shp: pl.BlockSpec(shp, lambda s, q_ref: (0, 0))
    in_specs = [own_spec(a) for a in own]
    for q in range(nblk):
        in_specs += [recv_spec(a, q) for a in recv]
    return pl.pallas_call(
        body, name=name,
        grid_spec=pltpu.PrefetchScalarGridSpec(
            num_scalar_prefetch=1, grid=(1,), in_specs=in_specs,
            out_specs=[out_spec(oshape(a, tr)) for a, tr in zip(recv, transpose)]),
        out_shape=[jax.ShapeDtypeStruct(oshape(a, tr), F32) for a, tr in zip(recv, transpose)],
        compiler_params=_cp(("arbitrary",), VMEM_MID),
    )(_scalar(mine), *_hbm(*own, *(list(recv) * nblk)))


def _rep_offsets():
    offs = []
    o = 0
    for r in REP_ROWS:
        offs.append(o)
        o += r
    return offs


LOSS_ROW = REP_TOTAL_ROWS


def _pack_small_grads(g):
    offs = _rep_offsets()

    def body(dwa, dwx, dnm, dbin, dcb, dba, dbx, dlam, dq0, dq1, dq2, dk0, dk1, dk2, dnp, dbpg, loss, o_ref):
        o_ref[pl.ds(REP_TOTAL_ROWS - 2, NDEV * REP_ROWS_DEV - REP_TOTAL_ROWS + 2), :] = jnp.zeros(
            (NDEV * REP_ROWS_DEV - REP_TOTAL_ROWS + 2, LANES), F32)
        o_ref[pl.ds(LOSS_ROW, 1), :] = loss[0:1, :]
        for n in range(NRB):
            o_ref[pl.ds(offs[0] + n * RBW, RBW), :] = dwa[n]
            o_ref[pl.ds(offs[1] + n * RBW, RBW), :] = dwx[n]

        def put_vec(off, ref, rows):
            for k in range(rows):
                o_ref[pl.ds(off + k, 1), :] = ref[:, k * LANES:(k + 1) * LANES]

        put_vec(offs[2], dnm, REP_ROWS[2])
        put_vec(offs[3], dbin, REP_ROWS[3])
        put_vec(offs[4], dcb, REP_ROWS[4])
        put_vec(offs[5], dba, REP_ROWS[5])
        put_vec(offs[6], dbx, REP_ROWS[6])
        put_vec(offs[7], dlam, REP_ROWS[7])
        for k, ref in enumerate((dq0, dq1, dq2)):
            o_ref[pl.ds(offs[8] + k, 1), :] = ref[...]
        for k, ref in enumerate((dk0, dk1, dk2)):
            o_ref[pl.ds(offs[9] + k, 1), :] = ref[...]
        put_vec(offs[10], dnp, REP_ROWS[10])
        put_vec(offs[11], dbpg, REP_ROWS[11])

    args = [g["w_rg_a"], g["w_rg_x"], g["norm_mix"], g["b_in"], g["conv_b"], g["b_rg_a"], g["b_rg_x"],
            g["lru_lambda"], *g["q_norm"], *g["k_norm"], g["norm_ple"], g["b_ple_gate"], g["loss"]]
    full = lambda shp: pl.BlockSpec(shp, lambda: (0,) * len(shp))
    return pl.pallas_call(
        body, name="pack_small_grads",
        in_specs=[full(a.shape) for a in args],
        out_specs=full((NDEV * REP_ROWS_DEV, LANES)),
        out_shape=jax.ShapeDtypeStruct((NDEV * REP_ROWS_DEV, LANES), F32),
    )(*_hbm(*args))


def _adam_math(wv, gv, mv, vv):
    c1 = 1.0 - B1 ** STEP
    c2 = 1.0 - B2 ** STEP
    m2 = B1 * mv + (1.0 - B1) * gv
    v2 = B2 * vv + (1.0 - B2) * (gv * gv)
    delta = (-LR) * ((m2 / c1) / (jnp.sqrt(v2 / c2) + AEPS) + WD * wv)
    return delta, m2, v2


def _adamw_small(rep_flat, w, m, v):
    offs = _rep_offsets()
    n = len(REP_NAMES)

    def body(*refs):
        g_ref = refs[0]
        w_refs = refs[1:1 + n]
        m_refs = refs[1 + n:1 + 2 * n]
        v_refs = refs[1 + 2 * n:1 + 3 * n]
        outs = refs[1 + 3 * n:]
        go, do, mo, vo = outs[:n], outs[n:2 * n], outs[2 * n:3 * n], outs[3 * n:]

        def emit(i, idx, gv):
            go[i][idx] = gv
            delta, m2, v2 = _adam_math(w_refs[i][idx], gv, m_refs[i][idx], v_refs[i][idx])
            do[i][idx] = delta
            mo[i][idx] = m2
            vo[i][idx] = v2

        for i in range(n):
            if i < 2:
                for b in range(NRB):
                    emit(i, b, g_ref[pl.ds(offs[i] + b * RBW, RBW), :])
            elif REP_NAMES[i] in ("q_norm", "k_norm"):
                emit(i, slice(None), g_ref[pl.ds(offs[i], NG), :])
            else:
                gv = jnp.concatenate([g_ref[pl.ds(offs[i] + k, 1), :] for k in range(REP_ROWS[i])], axis=1)
                emit(i, slice(None), gv)

    full = lambda shp: pl.BlockSpec(shp, lambda: (0,) * len(shp))
    pspecs = [full(a.shape) for a in w]
    pshapes = [jax.ShapeDtypeStruct(a.shape, F32) for a in w]
    res = pl.pallas_call(
        body, name="adamw_small",
        in_specs=[full(rep_flat.shape)] + pspecs * 3,
        out_specs=pspecs * 4, out_shape=pshapes * 4,
        compiler_params=_cp(None, VMEM_MID),
    )(*_hbm(rep_flat, *w, *m, *v))
    return res[:n], res[n:2 * n], res[2 * n:3 * n], res[3 * n:]


def _adamw_many(w, g, m, v, token):
    n = len(w)

    def body(*refs):
        for i in range(n):
            delta, m2, v2 = _adam_math(refs[i][...], refs[n + i][...], refs[2 * n + i][...], refs[3 * n + i][...])
            refs[4 * n + 1 + i][...] = delta
            refs[5 * n + 1 + i][...] = m2
            refs[6 * n + 1 + i][...] = v2

    full = lambda shp: pl.BlockSpec(shp, lambda: (0,) * len(shp))
    specs = [full(a.shape) for a in w]
    shapes = [jax.ShapeDtypeStruct(a.shape, F32) for a in w]
    res = pl.pallas_call(
        body, name="adamw_shards",
        in_specs=specs * 4 + [full((8, LANES))], out_specs=specs * 3, out_shape=shapes * 3,
        compiler_params=_cp(None, VMEM_MID),
    )(*_hbm(*w, *g, *m, *v, token))
    return res[:n], res[n:2 * n], res[2 * n:]


def kernel(x, p, norm_mix, w_in, b_in, conv_w, conv_b, w_rg_a, b_rg_a, w_rg_x, b_rg_x, lru_lambda, q_norm, k_norm, w_o_rnn, w_o_att, w_out, norm_ple, w_ple_gate, b_ple_gate, w_ple, loss_target, m_norm_mix, m_w_in, m_b_in, m_conv_w, m_conv_b, m_w_rg_a, m_b_rg_a, m_w_rg_x, m_b_rg_x, m_lru_lambda, m_q_norm, m_k_norm, m_w_o_rnn, m_w_o_att, m_w_out, m_norm_ple, m_w_ple_gate, m_b_ple_gate, m_w_ple, v_norm_mix, v_w_in, v_b_in, v_conv_w, v_conv_b, v_w_rg_a, v_b_rg_a, v_w_rg_x, v_b_rg_x, v_lru_lambda, v_q_norm, v_k_norm, v_w_o_rnn, v_w_o_att, v_w_out, v_norm_ple, v_w_ple_gate, v_b_ple_gate, v_w_ple):
    w = dict(norm_mix=norm_mix, w_in=w_in, b_in=b_in, conv_w=conv_w, conv_b=conv_b, w_rg_a=w_rg_a, b_rg_a=b_rg_a,
             w_rg_x=w_rg_x, b_rg_x=b_rg_x, lru_lambda=lru_lambda, q_norm=q_norm, k_norm=k_norm, w_o_rnn=w_o_rnn,
             w_o_att=w_o_att, w_out=w_out, norm_ple=norm_ple, w_ple_gate=w_ple_gate, b_ple_gate=b_ple_gate,
             w_ple=w_ple)
    m = dict(norm_mix=m_norm_mix, w_in=m_w_in, b_in=m_b_in, conv_w=m_conv_w, conv_b=m_conv_b, w_rg_a=m_w_rg_a,
             b_rg_a=m_b_rg_a, w_rg_x=m_w_rg_x, b_rg_x=m_b_rg_x, lru_lambda=m_lru_lambda, q_norm=m_q_norm,
             k_norm=m_k_norm, w_o_rnn=m_w_o_rnn, w_o_att=m_w_o_att, w_out=m_w_out, norm_ple=m_norm_ple,
             w_ple_gate=m_w_ple_gate, b_ple_gate=m_b_ple_gate, w_ple=m_w_ple)
    v = dict(norm_mix=v_norm_mix, w_in=v_w_in, b_in=v_b_in, conv_w=v_conv_w, conv_b=v_conv_b, w_rg_a=v_w_rg_a,
             b_rg_a=v_b_rg_a, w_rg_x=v_w_rg_x, b_rg_x=v_b_rg_x, lru_lambda=v_lru_lambda, q_norm=v_q_norm,
             k_norm=v_k_norm, w_o_rnn=v_w_o_rnn, w_o_att=v_w_o_att, w_out=v_w_out, norm_ple=v_norm_ple,
             w_ple_gate=v_w_ple_gate, b_ple_gate=v_b_ple_gate, w_ple=v_w_ple)
    names = list(w.keys())

    shards = [w_in[0].T.astype(BF16), w_o_rnn[0].astype(BF16), w_o_att[0].T.astype(BF16), w_out[0].astype(BF16),
              w_ple_gate[0].astype(BF16), w_ple[0].T.astype(BF16), conv_w[0]]
    pos = _my_pos()
    me, my_core, my_chip = _lin(pos), pos[2], _chip(pos)
    hbm_empty = lambda shp, dt: lax.empty(shp, dt)
    w_shard, conv_shard = shards[0], shards[6]
    shp = w_shard.shape
    entry_token, bufs, sems = _gather_start(
        [w_shard, hbm_empty((2,) + shp, BF16), hbm_empty((2, 2) + shp, BF16), conv_shard,
         hbm_empty((NDEV,) + conv_shard.shape, F32)],
        [("own", (0, 1)), ("near", (0, 2)), ("others", (3, 4))], norm_mix, "gather_start_near")
    w_src, own_l, near_l, conv_src, conv_l = bufs
    sem_own, sem_near, sem_conv = sems
    gather_out = {}

    def project(hn):
        w_thru, own = _gather_wait("own", *sem_own, [w_src, own_l], hn, "gather_wait_own")
        own = lax.dynamic_update_slice(own, w_shard[None], (my_core, 0, 0)).reshape(1, CHIP_COLS, D)
        chips = [jnp.stack([my_chip]), jnp.stack([my_chip ^ 1, my_chip ^ 2]), jnp.stack([my_chip ^ 3])]
        chips = [c.astype(jnp.int32) for c in chips]
        proj = _in_proj_chips(hn, own, b_in, chips[0], None, entry_token, "in_proj_own")
        w_thru, near = _gather_wait("near", *sem_near, [w_thru, near_l], proj, "gather_wait_near")
        near = _forward_to_sibling(near, "gather_forward_near")
        token, (w_thru, far_l), (sem_far,) = _gather_start(
            [w_thru, hbm_empty((2,) + shp, BF16)], [("far", (0, 1))], near, "gather_start_far")
        near = near.reshape(2, CHIP_COLS, D)
        proj = _in_proj_chips(hn, near, b_in, chips[1], proj, token, "in_proj_near")
        w_thru, far = _gather_wait("far", *sem_far, [w_thru, far_l], proj, "gather_wait_far")
        far = _forward_to_sibling(far[None], "gather_forward_far").reshape(1, CHIP_COLS, D)
        proj = _in_proj_chips(hn, far, b_in, chips[2], proj, token, "in_proj_far")
        conv_thru, conv_g = _gather_wait("others", *sem_conv, [conv_src, conv_l], proj, "gather_wait_conv")
        conv_g = lax.dynamic_update_slice(conv_g, conv_shard[None], (me, 0, 0))
        conv_f = conv_g.transpose(1, 0, 2).reshape(CONVW, DR)
        srcs = list(shards[1:6])
        token, obufs, (sem_out,) = _gather_start(
            srcs + [hbm_empty((NDEV,) + a.shape, BF16) for a in srcs], [("others", tuple(range(10)))], proj,
            "gather_start_out")
        gather_out.update(bufs=obufs, sems=sem_out)
        return proj, [own, near, far], jnp.concatenate(chips), conv_f, token

    def other_weights(after):
        obufs = _gather_wait("others", *gather_out["sems"], gather_out["bufs"], after, "gather_wait_out")
        full = [lax.dynamic_update_slice(a, s[None], (me, 0, 0)) for a, s in zip(obufs[5:], shards[1:6])]
        return [a.reshape((NDEV * a.shape[1], a.shape[2])) for a in full]

    def start_reduce(arrs, tag):
        if tag == "out":
            parts = [a.reshape((NDEV, a.shape[0] // NDEV, a.shape[1])) for a in arrs]
            token, bufs, (sems,) = _gather_start(
                parts + [lax.empty(a.shape, a.dtype) for a in parts], [("exchange", tuple(range(2 * len(parts))))],
                arrs[-1][:SUBLANES], "reduce_out_start")
            return token, (bufs, sems)
        parts = [a.reshape((NCHIP, 2, a.shape[0] // NDEV, a.shape[1])) for a in arrs]
        theirs = _exchange_within_chip(parts, "reduce_within_chip_" + tag)
        return _between_chips_start(_sum_pairs(parts, theirs, "sum_pairs_" + tag), "reduce_between_chips_start_" + tag)

    grad_x, pending_out, pending_in, small = _local_step(
        x.reshape(T, D), p.reshape(T, PLE), loss_target.reshape(T, D),
        project, other_weights,
        norm_mix, conv_b, w_rg_a[0], b_rg_a, w_rg_x[0], b_rg_x, lru_lambda, q_norm[0], k_norm[0],
        norm_ple, b_ple_gate, start_reduce, entry_token)

    rep_parts = _pack_small_grads(small).reshape(NDEV, REP_ROWS_DEV, LANES)
    conv_parts = small["conv_w"].reshape(CONVW, NDEV, DR // NDEV).transpose(1, 0, 2)
    smalls = [rep_parts, conv_parts]
    token, sbufs, (sem_x,) = _gather_start(
        smalls + [lax.empty(a.shape, F32) for a in smalls], [("exchange", (0, 1, 2, 3))], small["norm_mix"],
        "reduce_small_start")

    own_in, recv_in = _between_chips_wait(pending_in, token, "reduce_between_chips_wait_in")
    w_in_res = _sum_chips_adamw(own_in[0], recv_in[0], w_in[0].T, m_w_in[0].T, v_w_in[0].T, token, 304, "adamw_w_in")
    sbufs = _gather_wait("exchange", *sem_x, sbufs, w_in_res[0], "reduce_small_wait")
    g_rep, g_conv = _sum_blocks_small(sbufs[:2], sbufs[2:], me, (False, False), "sum_small")
    token, gbufs, (sem_g,) = _gather_start(
        [g_rep, lax.empty((NDEV,) + g_rep.shape, F32)], [("others", (0, 1))], g_conv, "gather_small_start")
    obufs = _gather_wait("exchange", *pending_out[1], pending_out[0], token, "reduce_out_wait")
    g_o_rnn, g_o_att, g_out, g_pg, g_ple = _sum_blocks_small(
        obufs[:5], obufs[5:], me, (False, True, False, False, True), "sum_out")

    grad, delta, new_m, new_v = {}, {}, {}, {}
    rest = ("w_o_rnn", "w_o_att", "w_out", "w_ple_gate", "w_ple", "conv_w")
    g_rest = [g_o_rnn, g_o_att, g_out, g_pg, g_ple, g_conv]
    rest_res = _adamw_many([w[n][0] for n in rest], g_rest, [m[n][0] for n in rest], [v[n][0] for n in rest], token)
    _, rep_all = _gather_wait("others", *sem_g, gbufs, rest_res[0][0], "gather_small_wait")
    rep_all = lax.dynamic_update_slice(rep_all, g_rep[None], (me, 0, 0)).reshape(NDEV * REP_ROWS_DEV, LANES)
    loss = rep_all[LOSS_ROW, 0]
    rep_shape = lambda a: a if a.ndim == 2 else a.reshape(a.shape[1:])
    res = _adamw_small(rep_all, [rep_shape(w[n]) for n in REP_NAMES], [rep_shape(m[n]) for n in REP_NAMES],
                       [rep_shape(v[n]) for n in REP_NAMES])
    for dst, vals in zip((grad, delta, new_m, new_v), res):
        for n, a in zip(REP_NAMES, vals):
            dst[n] = a.reshape(w[n].shape)
    grad["w_in"], delta["w_in"], new_m["w_in"], new_v["w_in"] = [a.T[None] for a in w_in_res]
    for n, a in zip(rest, g_rest):
        grad[n] = a[None]
    for dst, vals in zip((delta, new_m, new_v), rest_res):
        for n, a in zip(rest, vals):
            dst[n] = a[None]

    return (loss, grad_x.reshape(BL, S, D), *[grad[n] for n in names], *[delta[n] for n in names],
            *[new_m[n] for n in names], *[new_v[n] for n in names])
```

```python
import jax
import jax.numpy as jnp
from jax import lax
from jax.experimental import pallas as pl
from jax.experimental.pallas import tpu as pltpu

F32 = jnp.float32
BF16 = jnp.bfloat16

D = 1024
S = 2048
BL = 2
T = BL * S
NDEV = 8
NCHIP = 4
PLE = 256
DR = 1280
NRB = 10
RBW = 128
CONVW = 4
LRU_C = 8.0
HD = 128
NH = 4
PATTERNS = ((128, 1), (512, 4), (2048, 16))
NG = 3
ATT = NH * HD
GW = NG * ATT
NIN = 2 * DR + 3 * GW + ATT + 2 * D
OFF_ZR = DR
OFF_Q = 2 * DR
OFF_ZA = OFF_Q + 3 * GW
OFF_G = OFF_ZA + ATT
ROPE_THETA = 10000.0
EPS = 1e-6
SCALE = HD ** -0.5
NEG = -1e30
QB = 128
LANES = 128
CT = 512
NCT = NIN // CT
A_W = 2 * DR
C_W = ATT + 2 * D

LR, B1, B2, AEPS, WD, STEP = 0.001, 0.9, 0.999, 1e-08, 0.01, 10

NSHARD_IN = NIN // NDEV
REP_NAMES = ("w_rg_a", "w_rg_x", "norm_mix", "b_in", "conv_b", "b_rg_a", "b_rg_x", "lru_lambda", "q_norm",
             "k_norm", "norm_ple", "b_ple_gate")
REP_ROWS = (NRB * RBW, NRB * RBW, D // LANES, NIN // LANES, DR // LANES, DR // LANES, DR // LANES, DR // LANES,
            NG, NG, D // LANES, D // LANES)
REP_TOTAL_ROWS = sum(REP_ROWS)
REP_ROWS_DEV = 344
BIG_NAMES = ("w_in", "w_o_rnn", "w_o_att", "w_out", "w_ple_gate", "w_ple")

VMEM_BIG = 56 * 1024 * 1024
VMEM_MID = 40 * 1024 * 1024


def _cp(sem=None, vmem=None):
    return pltpu.CompilerParams(dimension_semantics=sem, vmem_limit_bytes=vmem)


def _hbm(*arrays):
    return [pltpu.with_memory_space_constraint(a, pltpu.HBM) for a in arrays]


def _copy_together(copies):
    for cp in copies:
        cp.start()
    for cp in copies:
        cp.wait()


def _dot(a, b):
    return jnp.dot(a, b, preferred_element_type=F32)


def _dot_nt(a, b):
    return lax.dot_general(a, b, (((1,), (1,)), ((), ())), preferred_element_type=F32)


def _dot_tn(a, b):
    return lax.dot_general(a, b, (((0,), (0,)), ((), ())), preferred_element_type=F32)


def _sigmoid(x):
    return jax.nn.sigmoid(x)


def _perm(j):
    jq = j - OFF_Q // CT
    inside = (j >= OFF_Q // CT) & (j < OFF_ZA // CT)
    return jnp.where(inside, OFF_Q // CT + (jq % 3) * 3 + jq // 3, j)


PIECES = ((0, A_W // CT), (OFF_Q // CT, GW // CT), (OFF_Q // CT + 3, GW // CT), (OFF_Q // CT + 6, GW // CT),
          (OFF_ZA // CT, C_W // CT))


def _rmsnorm_fwd(x, gain, token, tm=512):
    def body(x_ref, g_ref, _token, o_ref):
        xv = x_ref[...]
        var = jnp.mean(xv * xv, axis=-1, keepdims=True)
        o_ref[...] = (xv * lax.rsqrt(var + EPS) * g_ref[...]).astype(BF16)

    return pl.pallas_call(
        body, grid=(T // tm,), name="rmsnorm_fwd",
        in_specs=[pl.BlockSpec((tm, D), lambda i: (i, 0)), pl.BlockSpec((1, D), lambda i: (0, 0)),
                  pl.BlockSpec((8, LANES), lambda i: (0, 0))],
        out_specs=pl.BlockSpec((tm, D), lambda i: (i, 0)),
        out_shape=jax.ShapeDtypeStruct((T, D), BF16),
        compiler_params=_cp(("parallel",)),
    )(*_hbm(x, gain, token))


CHIP_COLS = NIN // NCHIP


def _in_proj_chips(hn, w_rows, bias, chips, proj, token, name, tm=1024):
    n = w_rows.shape[0]

    def body(chips_ref, a_ref, w_ref, b_ref, _token, *rest):
        o_ref = rest[-1]
        o_ref[...] = (_dot_nt(a_ref[...], w_ref[0]) + b_ref[...]).astype(BF16)

    in_specs = [pl.BlockSpec((tm, D), lambda s, i, ch: (i, 0)),
                pl.BlockSpec((1, CHIP_COLS, D), lambda s, i, ch: (s, 0, 0)),
                pl.BlockSpec((1, CHIP_COLS), lambda s, i, ch: (0, ch[s])),
                pl.BlockSpec((8, LANES), lambda s, i, ch: (0, 0))]
    args = [hn, w_rows, bias, token]
    aliases = {}
    if proj is not None:
        in_specs.append(pl.BlockSpec(memory_space=pl.ANY))
        args.append(proj)
        aliases = {5: 0}
    return pl.pallas_call(
        body, name=name,
        grid_spec=pltpu.PrefetchScalarGridSpec(
            num_scalar_prefetch=1, grid=(n, T // tm), in_specs=in_specs,
            out_specs=pl.BlockSpec((tm, CHIP_COLS), lambda s, i, ch: (i, ch[s]))),
        out_shape=jax.ShapeDtypeStruct((T, NIN), BF16),
        input_output_aliases=aliases,
        compiler_params=_cp(("arbitrary", "arbitrary"), VMEM_BIG),
    )(chips, *_hbm(*args))


def _grad_x(pieces, w_bufs, chips, token, x, dx1, gain, tm=512):
    nb = len(w_bufs)

    def body(chips_ref, a_ref, q_ref, k_ref, v_ref, c_ref, *rest):
        w_hbm = rest[:nb]
        x_ref, dx1_ref, g_ref, dx_ref, dg_ref, w, sems = rest[nb + 1:]
        first = pl.program_id(0) == 0

        @pl.when(first)
        def _():
            s = 0
            copies = []
            for buf in w_hbm:
                for r in range(buf.shape[0]):
                    row = pl.multiple_of(chips_ref[s] * CHIP_COLS, 128)
                    copies.append(pltpu.make_async_copy(buf.at[r], w.at[pl.ds(row, CHIP_COLS), :], sems.at[s]))
                    s += 1
            _copy_together(copies)

        acc = _dot(a_ref[...], w[pl.ds(0, A_W), :])
        for kind, p_ref in enumerate((q_ref, k_ref, v_ref)):
            for g in range(NG):
                row = OFF_Q + (3 * g + kind) * CT
                acc = acc + _dot(p_ref[:, g * CT:(g + 1) * CT], w[pl.ds(row, CT), :])
        dn = acc + _dot(c_ref[...], w[pl.ds(OFF_ZA, C_W), :])
        xv = x_ref[...]
        rstd = lax.rsqrt(jnp.mean(xv * xv, axis=-1, keepdims=True) + EPS)
        xh = xv * rstd
        dg = jnp.sum(dn * xh, axis=0, keepdims=True)
        gd = dn * g_ref[...]
        dx_ref[...] = dx1_ref[...] + rstd * (gd - xh * jnp.mean(gd * xh, axis=-1, keepdims=True))

        @pl.when(first)
        def _():
            dg_ref[...] = dg

        @pl.when(jnp.logical_not(first))
        def _():
            dg_ref[...] += dg

    tok = lambda wd: pl.BlockSpec((tm, wd), lambda i, ch: (i, 0))
    vec = lambda: pl.BlockSpec((1, D), lambda i, ch: (0, 0))
    return pl.pallas_call(
        body, name="grad_x",
        grid_spec=pltpu.PrefetchScalarGridSpec(
            num_scalar_prefetch=1, grid=(T // tm,),
            in_specs=[tok(A_W), tok(GW), tok(GW), tok(GW), tok(C_W)] + [pl.BlockSpec(memory_space=pl.ANY)] * nb
            + [pl.BlockSpec((8, LANES), lambda i, ch: (0, 0)), tok(D), tok(D), vec()],
            out_specs=[tok(D), vec()],
            scratch_shapes=[pltpu.VMEM((NIN, D), BF16), pltpu.SemaphoreType.DMA((NCHIP,))]),
        out_shape=[jax.ShapeDtypeStruct((T, D), F32), jax.ShapeDtypeStruct((1, D), F32)],
        compiler_params=_cp(("arbitrary",), VMEM_BIG),
    )(chips, *_hbm(*pieces, *w_bufs, token, x, dx1, gain))


def _dw_in(pieces, hn):
    def body(a_ref, q_ref, k_ref, v_ref, c_ref, h_hbm, o_ref, s_ref, h):
        j = pl.program_id(0)

        @pl.when(j == 0)
        def _():
            pltpu.sync_copy(h_hbm, h)

        def step(x_ref):
            xv = x_ref[...]
            o_ref[...] = _dot_tn(xv, h[...]).astype(BF16)
            s_ref[...] = jnp.sum(xv.astype(F32), axis=0, keepdims=True)

        for x_ref, (lo, n) in zip((a_ref, q_ref, k_ref, v_ref, c_ref), PIECES):
            pl.when((j >= lo) & (j < lo + n))(lambda x_ref=x_ref: step(x_ref))

    def piece_spec(lo, n):
        return pl.BlockSpec((T, CT), lambda j: (0, jnp.clip(j - lo, 0, n - 1)))

    return pl.pallas_call(
        body, grid=(NCT,), name="dw_in",
        in_specs=[piece_spec(lo, n) for lo, n in PIECES] + [pl.BlockSpec(memory_space=pl.ANY)],
        out_specs=[pl.BlockSpec((CT, D), lambda j: (_perm(j), 0)), pl.BlockSpec((1, CT), lambda j: (0, _perm(j)))],
        out_shape=[jax.ShapeDtypeStruct((NIN, D), BF16), jax.ShapeDtypeStruct((1, NIN), F32)],
        scratch_shapes=[pltpu.VMEM((T, D), BF16)],
        compiler_params=_cp(("arbitrary",), VMEM_BIG),
    )(*_hbm(*pieces, hn))


def _mm_tn(a, b, ta, tt, name):
    m = a.shape[1]
    n = b.shape[1]
    nt = T // tt

    def body(a_ref, b_ref, o_ref, acc):
        t = pl.program_id(1)
        p = _dot_tn(a_ref[...].astype(BF16), b_ref[...].astype(BF16))

        @pl.when(t == 0)
        def _():
            acc[...] = p

        @pl.when(t > 0)
        def _():
            acc[...] += p

        @pl.when(t == nt - 1)
        def _():
            o_ref[...] = acc[...].astype(BF16)

    return pl.pallas_call(
        body, grid=(m // ta, nt), name=name,
        in_specs=[pl.BlockSpec((tt, ta), lambda j, t: (t, j)), pl.BlockSpec((tt, n), lambda j, t: (t, 0))],
        out_specs=pl.BlockSpec((ta, n), lambda j, t: (j, 0)),
        out_shape=jax.ShapeDtypeStruct((m, n), BF16),
        scratch_shapes=[pltpu.VMEM((ta, n), F32)],
        compiler_params=_cp(("parallel", "arbitrary"), VMEM_MID),
    )(*_hbm(a, b))


def _row_iota():
    return lax.broadcasted_iota(jnp.int32, (S, RBW), 0)


SUBLANES = 8
N_SHIFT_BUFS = 4


class _Shifter:
    def __init__(self, bufs):
        self.bufs = bufs
        self.k = 0

    def _store(self, v, fill, front):
        b = self.bufs.at[self.k % N_SHIFT_BUFS]
        self.k += 1
        b[pl.ds(0 if front else SUBLANES + S, SUBLANES), :] = jnp.full((SUBLANES, RBW), fill, F32)
        b[pl.ds(SUBLANES, S), :] = v
        return b

    def down(self, v, ds, fill):
        b = self._store(v, fill, True)
        return [b[pl.ds(SUBLANES - d, S), :] for d in ds]

    def up(self, v, ds, fill):
        b = self._store(v, fill, False)
        return [b[pl.ds(SUBLANES + d, S), :] for d in ds]


def _shift_down(v, d, sh, fill):
    return sh.down(v, (d,), fill)[0]


def _shift_up(v, d, sh, fill):
    return sh.up(v, (d,), fill)[0]


def _scan_down(a, u, row):
    d = 1
    while d < S:
        last = 2 * d >= S
        if d < SUBLANES:
            u = a * _shift_down(u, d, row, 0.0) + u
            if not last:
                a = a * _shift_down(a, d, row, 1.0)
        else:
            u = jnp.concatenate([u[:d], a[d:] * u[:S - d] + u[d:]], axis=0)
            if not last:
                a = jnp.concatenate([a[:d], a[d:] * a[:S - d]], axis=0)
        d *= 2
    return u


def _scan_up(b, g, row):
    d = 1
    while d < S:
        last = 2 * d >= S
        if d < SUBLANES:
            g = g + b * _shift_up(g, d, row, 0.0)
            if not last:
                b = b * _shift_up(b, d, row, 0.0)
        else:
            g = jnp.concatenate([g[:S - d] + b[:S - d] * g[d:], g[S - d:]], axis=0)
            if not last:
                b = jnp.concatenate([b[:S - d] * b[d:], b[S - d:]], axis=0)
        d *= 2
    return g


def _softplus(x):
    return jnp.maximum(x, 0.0) + jnp.log1p(jnp.exp(-jnp.abs(x)))


def _rnn_gates(x, cw, cb, wa, ba, wx, bx, lam, row, pad, saved=None):
    xs = pad.down(x, (1, 2, 3), 0.0)
    xc = cb + cw[3:4, :] * x
    for j in (1, 2, 3):
        xc = xc + cw[3 - j:4 - j, :] * xs[j - 1]
    xcb = xc.astype(BF16)
    if saved is not None:
        r, i, a, mult = saved
        return xc, xcb, r, i, _softplus(-lam), a, mult, xs
    r = _sigmoid(_dot(xcb, wa) + ba)
    i = _sigmoid(_dot(xcb, wx) + bx)
    sp = _softplus(-lam)
    log_a = (-LRU_C) * r * sp
    a = jnp.exp(log_a)
    mult = jnp.where(row == 0, 1.0, jnp.sqrt(jnp.tanh(-log_a) * (1.0 + a * a)))
    return xc, xcb, r, i, sp, a, mult, xs


def _rnn_fwd(proj3, conv_w, conv_b, wa, ba, wx, bx, lam, token):
    def body(x_ref, cw_ref, cb_ref, wa_ref, ba_ref, wx_ref, bx_ref, lam_ref, _token, h_ref, g_ref, pad):
        row = _row_iota()
        sh = _Shifter(pad)
        x = x_ref[0].astype(F32)
        xc, _, r, i, _, a, mult, _ = _rnn_gates(x, cw_ref[...], cb_ref[...], wa_ref[0], ba_ref[...],
                                             wx_ref[0], bx_ref[...], lam_ref[...], row, sh)
        for k, val in enumerate((r, i, a, mult)):
            g_ref[k, 0] = val
        h_ref[0] = _scan_down(a, mult * (i * xc), sh)

    vec = lambda: pl.BlockSpec((1, RBW), lambda b, n: (0, n))
    mat = lambda: pl.BlockSpec((1, RBW, RBW), lambda b, n: (n, 0, 0))
    return pl.pallas_call(
        body, grid=(BL, NRB), name="rnn_fwd",
        in_specs=[pl.BlockSpec((1, S, RBW), lambda b, n: (b, 0, n)),
                  pl.BlockSpec((CONVW, RBW), lambda b, n: (0, n)),
                  vec(), mat(), vec(), mat(), vec(), vec(), pl.BlockSpec((8, LANES), lambda b, n: (0, 0))],
        out_specs=[pl.BlockSpec((1, S, RBW), lambda b, n: (b, 0, n)),
                   pl.BlockSpec((4, 1, S, RBW), lambda b, n: (0, b, 0, n))],
        out_shape=[jax.ShapeDtypeStruct((BL, S, DR), F32), jax.ShapeDtypeStruct((4, BL, S, DR), F32)],
        scratch_shapes=[pltpu.VMEM((N_SHIFT_BUFS, S + 2 * SUBLANES, RBW), F32)],
        compiler_params=_cp(("parallel", "parallel"), VMEM_MID),
    )(*_hbm(proj3, conv_w, conv_b, wa, ba, wx, bx, lam, token))


def _rnn_bwd(proj3, h3, dh3, gates, slab_a3, conv_w, conv_b, wa, ba, wx, bx, lam, token):
    def body(x_ref, h_ref, dh_ref, g_ref, cw_ref, cb_ref, wa_ref, ba_ref, wx_ref, bx_ref, lam_ref, _alias, _token,
             dx_ref, dcw_ref, dcb_ref, dwa_ref, dba_ref, dwx_ref, dbx_ref, dlam_ref, pad):
        row = _row_iota()
        sh = _Shifter(pad)
        x = x_ref[0].astype(F32)
        cw = cw_ref[...]
        wa_v = wa_ref[0]
        wx_v = wx_ref[0]
        lam_v = lam_ref[...]
        xc, xcb, r, i, sp, a, mult, xs = _rnn_gates(x, cw, cb_ref[...], wa_v, ba_ref[...], wx_v, bx_ref[...], lam_v,
                                                    row, sh, [g_ref[k, 0] for k in range(4)])
        h = h_ref[0]
        g = _scan_up(_shift_up(a, 1, sh, 0.0), dh_ref[0], sh)
        da = g * _shift_down(h, 1, sh, 0.0)
        dmult = jnp.where(row == 0, 0.0, g * (i * xc))
        gm = g * mult
        di = gm * xc
        dxc = gm * i
        dlog_a = da * a - dmult * (a * a) / mult
        dr = dlog_a * ((-LRU_C) * sp)
        dsp = jnp.sum(dlog_a * ((-LRU_C) * r), axis=0, keepdims=True)
        dlam = dsp * (-_sigmoid(-lam_v))
        dpa = dr * r * (1.0 - r)
        dpx = di * i * (1.0 - i)
        dpab = dpa.astype(BF16)
        dpxb = dpx.astype(BF16)
        dwa = _dot_tn(xcb, dpab)
        dwx = _dot_tn(xcb, dpxb)
        dba = jnp.sum(dpa, axis=0, keepdims=True)
        dbx = jnp.sum(dpx, axis=0, keepdims=True)
        dxc = dxc + _dot_nt(dpab, wa_v) + _dot_nt(dpxb, wx_v)
        dcb = jnp.sum(dxc, axis=0, keepdims=True)
        dx = cw[3:4, :] * dxc
        dcw_rows = [None] * CONVW
        dcw_rows[3] = jnp.sum(dxc * x, axis=0, keepdims=True)
        dxc_up = sh.up(dxc, (1, 2, 3), 0.0)
        for j in (1, 2, 3):
            dx = dx + cw[3 - j:4 - j, :] * dxc_up[j - 1]
            dcw_rows[3 - j] = jnp.sum(dxc * xs[j - 1], axis=0, keepdims=True)
        dx_ref[0] = dx.astype(BF16)
        dcw = jnp.concatenate(dcw_rows, axis=0)
        first = pl.program_id(1) == 0

        @pl.when(first)
        def _():
            dcw_ref[...] = dcw
            dcb_ref[...] = dcb
            dwa_ref[0] = dwa
            dba_ref[...] = dba
            dwx_ref[0] = dwx
            dbx_ref[...] = dbx
            dlam_ref[...] = dlam

        @pl.when(jnp.logical_not(first))
        def _():
            dcw_ref[...] += dcw
            dcb_ref[...] += dcb
            dwa_ref[0] += dwa
            dba_ref[...] += dba
            dwx_ref[0] += dwx
            dbx_ref[...] += dbx
            dlam_ref[...] += dlam

    slab = lambda: pl.BlockSpec((1, S, RBW), lambda n, b: (b, 0, n))
    vec = lambda: pl.BlockSpec((1, RBW), lambda n, b: (0, n))
    mat = lambda: pl.BlockSpec((1, RBW, RBW), lambda n, b: (n, 0, 0))
    taps = lambda: pl.BlockSpec((CONVW, RBW), lambda n, b: (0, n))
    vshape = jax.ShapeDtypeStruct((1, DR), F32)
    mshape = jax.ShapeDtypeStruct((NRB, RBW, RBW), F32)
    return pl.pallas_call(
        body, grid=(NRB, BL), name="rnn_bwd",
        in_specs=[slab(), slab(), slab(), pl.BlockSpec((4, 1, S, RBW), lambda n, b: (0, b, 0, n)),
                  taps(), vec(), mat(), vec(), mat(), vec(), vec(),
                  pl.BlockSpec(memory_space=pl.ANY), pl.BlockSpec((8, LANES), lambda n, b: (0, 0))],
        out_specs=[slab(), taps(), vec(), mat(), vec(), mat(), vec(), vec()],
        out_shape=[jax.ShapeDtypeStruct((BL, S, A_W), BF16), jax.ShapeDtypeStruct((CONVW, DR), F32),
                   vshape, mshape, vshape, mshape, vshape, vshape],
        input_output_aliases={11: 0},
        scratch_shapes=[pltpu.VMEM((N_SHIFT_BUFS, S + 2 * SUBLANES, RBW), F32)],
        compiler_params=_cp(("parallel", "arbitrary"), 48 * 1024 * 1024),
    )(*_hbm(proj3, h3, dh3, gates, conv_w, conv_b, wa, ba, wx, bx, lam, slab_a3, token))


NQB = S // QB


def _rms_head(t, gain):
    rstd = lax.rsqrt(jnp.mean(t * t, axis=-1, keepdims=True) + EPS)
    return t * rstd * gain


def _rope(t, cs, sn):
    return t * cs + pltpu.roll(t, HD // 2, 1) * sn


def _rope_t(dy, cs, sn):
    return dy * cs - pltpu.roll(dy, HD // 2, 1) * sn


def _bdot_nt(a, b):
    return lax.dot_general(a, b, (((2,), (2,)), ((0,), (0,))), preferred_element_type=F32)


def _bdot(a, b):
    return lax.dot_general(a, b, (((2,), (1,)), ((0,), (0,))), preferred_element_type=F32)


def _bdot_tn(a, b):
    return lax.dot_general(a, b, (((1,), (1,)), ((0,), (0,))), preferred_element_type=F32)


STRIDE_MAX = 4


def _permute(buf, x, dil, dst, off=0):
    ln = S // dil
    if dil == 1:
        dst[pl.ds(off, S), :] = x.astype(dst.dtype)
        return
    buf[0] = x
    if dil <= STRIDE_MAX:
        for c in range(dil):
            dst[pl.ds(off + c * ln, ln), :] = buf.at[0][pl.ds(c, ln, stride=dil), :].astype(dst.dtype)
        return
    f, r = STRIDE_MAX, dil // STRIDE_MAX
    part = S // f
    for c1 in range(f):
        buf.at[1][pl.ds(c1 * part, part), :] = buf.at[0][pl.ds(c1, part, stride=f), :]
    for c1 in range(f):
        for c2 in range(r):
            dst[pl.ds(off + (c1 + f * c2) * ln, ln), :] = (
                buf.at[1][pl.ds(c1 * part + c2, ln, stride=r), :].astype(dst.dtype))


def _unpermute(buf, xp, dil, dst):
    ln = S // dil
    if dil == 1:
        dst[...] = xp
        return
    if dil <= STRIDE_MAX:
        for c in range(dil):
            dst[pl.ds(c, ln, stride=dil), :] = xp[c * ln:(c + 1) * ln]
        return
    f, r = STRIDE_MAX, dil // STRIDE_MAX
    part = S // f
    for c1 in range(f):
        for c2 in range(r):
            c = c1 + f * c2
            buf.at[1][pl.ds(c1 * part + c2, ln, stride=r), :] = xp[c * ln:(c + 1) * ln]
    for c1 in range(f):
        dst[pl.ds(c1, part, stride=f), :] = buf[1, pl.ds(c1 * part, part), :]


def _blocks3(ref, off=0):
    return ref[pl.ds(off, S), :].reshape(NQB, QB, HD)


def _att_prep(q_ref, k_ref, v_ref, cos_ref, sin_ref, qn, kn, dil, nat, qs, ksp, vsp):
    cs = cos_ref[...]
    sn = sin_ref[...]
    zero = jnp.zeros((QB, HD), BF16)
    ksp[pl.ds(0, QB), :] = zero
    vsp[pl.ds(0, QB), :] = zero
    _permute(nat, _rope(_rms_head(q_ref[0].astype(F32), qn), cs, sn), dil, qs)
    _permute(nat, _rope(_rms_head(k_ref[0].astype(F32), kn), cs, sn), dil, ksp, QB)
    _permute(nat, v_ref[0].astype(F32), dil, vsp, QB)


def _att_scores(qs, ksp, dil):
    nb = S // dil // QB
    q3 = _blocks3(qs)
    shape = (NQB, QB, QB)
    qi = lax.broadcasted_iota(jnp.int32, shape, 1)
    kj = lax.broadcasted_iota(jnp.int32, shape, 2)
    s_c = jnp.where(qi >= kj, _bdot_nt(q3, _blocks3(ksp, QB)) * SCALE, NEG)
    if nb == 1:
        return q3, s_c, None
    jj = lax.broadcasted_iota(jnp.int32, shape, 0)
    ok = (kj >= qi) & ((jj & (nb - 1)) != 0)
    s_p = jnp.where(ok, _bdot_nt(q3, _blocks3(ksp)) * SCALE, NEG)
    return q3, s_c, s_p


def _qkv_spec(kind, g):
    base = OFF_Q // HD + (3 * g + kind) * NH
    return pl.BlockSpec((1, S, HD), lambda b, h: (b, 0, base + h))


def _attn_fwd(proj3, cos_t, sin_t, q_norm, k_norm):
    def body(*refs):
        qkv_refs = refs[:9]
        (cos_ref, sin_ref, qn_ref, kn_ref, att_ref, lse_ref, w_ref, qp_ref, kp_ref, vp_ref,
         nat, qs, ksp, vsp, og) = refs[9:]
        for g, (window, dil) in enumerate(PATTERNS):
            q_ref, k_ref, v_ref = qkv_refs[3 * g:3 * g + 3]
            _att_prep(q_ref, k_ref, v_ref, cos_ref, sin_ref, qn_ref[g:g + 1, :], kn_ref[g:g + 1, :], dil,
                      nat, qs, ksp, vsp)
            qp_ref[g, 0] = qs[...]
            kp_ref[g, 0] = ksp[pl.ds(QB, S), :]
            vp_ref[g, 0] = vsp[pl.ds(QB, S), :]
            _, s_c, s_p = _att_scores(qs, ksp, dil)
            m = jnp.max(s_c, axis=-1, keepdims=True)
            if s_p is not None:
                m = jnp.maximum(m, jnp.max(s_p, axis=-1, keepdims=True))
            e_c = jnp.exp(s_c - m)
            den = jnp.sum(e_c, axis=-1, keepdims=True)
            o = _bdot(e_c.astype(BF16), _blocks3(vsp, QB))
            if s_p is not None:
                e_p = jnp.exp(s_p - m)
                den = den + jnp.sum(e_p, axis=-1, keepdims=True)
                o = o + _bdot(e_p.astype(BF16), _blocks3(vsp))
            _unpermute(nat, (o / den).reshape(S, HD), dil, og.at[g])
            _unpermute(nat, jnp.broadcast_to(m + jnp.log(den), (NQB, QB, HD)).reshape(S, HD), dil,
                       lse_ref.at[g, 0])
        l0 = lse_ref[0, 0]
        l1 = lse_ref[1, 0]
        l2 = lse_ref[2, 0]
        mx = jnp.maximum(jnp.maximum(l0, l1), l2)
        e0 = jnp.exp(l0 - mx)
        e1 = jnp.exp(l1 - mx)
        e2 = jnp.exp(l2 - mx)
        inv = 1.0 / (e0 + e1 + e2)
        w0 = e0 * inv
        w1 = e1 * inv
        w2 = e2 * inv
        w_ref[0, 0] = w0
        w_ref[1, 0] = w1
        w_ref[2, 0] = w2
        att_ref[0] = w0 * og[0] + w1 * og[1] + w2 * og[2]

    in_specs = [_qkv_spec(kind, g) for g in range(NG) for kind in range(3)]
    in_specs += [pl.BlockSpec((S, HD), lambda b, h: (0, 0)), pl.BlockSpec((S, HD), lambda b, h: (0, 0)),
                 pl.BlockSpec((NG, HD), lambda b, h: (0, 0)), pl.BlockSpec((NG, HD), lambda b, h: (0, 0))]
    stat = lambda: pl.BlockSpec((NG, 1, S, HD), lambda b, h: (0, b, 0, h))
    return pl.pallas_call(
        body, grid=(BL, NH), name="attn_fwd",
        in_specs=in_specs,
        out_specs=[pl.BlockSpec((1, S, HD), lambda b, h: (b, 0, h)), stat(), stat(), stat(), stat(), stat()],
        out_shape=[jax.ShapeDtypeStruct((BL, S, ATT), F32),
                   jax.ShapeDtypeStruct((NG, BL, S, ATT), F32),
                   jax.ShapeDtypeStruct((NG, BL, S, ATT), F32)]
        + [jax.ShapeDtypeStruct((NG, BL, S, ATT), BF16)] * 3,
        scratch_shapes=[pltpu.VMEM((2, S, HD), F32), pltpu.VMEM((S, HD), BF16), pltpu.VMEM((S + QB, HD), BF16),
                        pltpu.VMEM((S + QB, HD), BF16), pltpu.VMEM((NG, S, HD), F32)],
        compiler_params=_cp(("parallel", "parallel"), VMEM_BIG),
    )(*_hbm(*([proj3] * 9), cos_t, sin_t, q_norm, k_norm))


def _attn_bwd_group(g, proj3, cos_t, sin_t, qn_g, kn_g, lse, wts, qkv_p, datt3, sbar3, slabs):
    dil = PATTERNS[g][1]
    n_alias = 0 if slabs is None else 3

    def norm_rope_bwd(dpost, raw, gain, cs, sn):
        dn = _rope_t(dpost, cs, sn)
        rstd = lax.rsqrt(jnp.mean(raw * raw, axis=-1, keepdims=True) + EPS)
        xh = raw * rstd
        dgain = jnp.sum(dn * xh, axis=0, keepdims=True)
        gd = dn * gain
        draw = rstd * (gd - xh * jnp.mean(gd * xh, axis=-1, keepdims=True))
        return draw, dgain

    def body(*refs):
        (q_ref, k_ref, qp_ref, kp_ref, vp_ref, cos_ref, sin_ref, qn_ref, kn_ref, lse_ref, w_ref, datt_ref,
         sbar_ref) = refs[:13]
        (dq_ref, dk_ref, dv_ref, dqn_ref, dkn_ref, nat, ksp, vsp, dos, cvp, lsp, acc) = refs[13 + n_alias:]
        qn = qn_ref[...]
        kn = kn_ref[...]
        cs = cos_ref[...]
        sn = sin_ref[...]
        qs = qp_ref.at[0, 0]
        zero = jnp.zeros((QB, HD), BF16)
        ksp[pl.ds(0, QB), :] = zero
        vsp[pl.ds(0, QB), :] = zero
        ksp[pl.ds(QB, S), :] = kp_ref[0, 0]
        vsp[pl.ds(QB, S), :] = vp_ref[0, 0]
        wv = w_ref[0, 0]
        _permute(nat, wv * datt_ref[0], dil, dos)
        _permute(nat, wv * sbar_ref[0], dil, cvp)
        _permute(nat, lse_ref[0, 0], dil, lsp)
        q3, s_c, s_p = _att_scores(qs, ksp, dil)
        do3 = _blocks3(dos)
        lse3 = _blocks3(lsp)[:, :, 0:1]
        cv3 = _blocks3(cvp)[:, :, 0:1]
        p_c = jnp.exp(s_c - lse3)
        ds_c = (p_c * (_bdot_nt(do3, _blocks3(vsp, QB)) - cv3)).astype(BF16)
        dq = _bdot(ds_c, _blocks3(ksp, QB))
        acc[0] = _bdot_tn(ds_c, q3).reshape(S, HD)
        acc[1] = _bdot_tn(p_c.astype(BF16), do3).reshape(S, HD)
        if s_p is not None:
            p_p = jnp.exp(s_p - lse3)
            ds_p = (p_p * (_bdot_nt(do3, _blocks3(vsp)) - cv3)).astype(BF16)
            dq = dq + _bdot(ds_p, _blocks3(ksp))
            early = pl.ds(0, S - QB)
            acc[0, early, :] += _bdot_tn(ds_p, q3).reshape(S, HD)[QB:]
            acc[1, early, :] += _bdot_tn(p_p.astype(BF16), do3).reshape(S, HD)[QB:]
        _unpermute(nat, (dq * SCALE).reshape(S, HD), dil, nat.at[0])
        draw, dqn = norm_rope_bwd(nat[0], q_ref[0].astype(F32), qn, cs, sn)
        dq_ref[0] = draw.astype(BF16)
        _unpermute(nat, acc[0] * SCALE, dil, nat.at[0])
        draw, dkn = norm_rope_bwd(nat[0], k_ref[0].astype(F32), kn, cs, sn)
        dk_ref[0] = draw.astype(BF16)
        _unpermute(nat, acc[1], dil, nat.at[0])
        dv_ref[0] = nat[0].astype(BF16)
        first = (pl.program_id(0) == 0) & (pl.program_id(1) == 0)

        @pl.when(first)
        def _():
            dqn_ref[...] = dqn
            dkn_ref[...] = dkn

        @pl.when(jnp.logical_not(first))
        def _():
            dqn_ref[...] += dqn
            dkn_ref[...] += dkn

    full = lambda r: pl.BlockSpec((r, HD), lambda b, h: (0, 0))
    stat = lambda: pl.BlockSpec((1, 1, S, HD), lambda b, h: (g, b, 0, h))
    slab = lambda: pl.BlockSpec((1, S, HD), lambda b, h: (b, 0, h))
    out_slab = lambda: pl.BlockSpec((1, S, HD), lambda b, h: (b, 0, g * NH + h))
    big = jax.ShapeDtypeStruct((BL, S, GW), BF16)
    vecs = jax.ShapeDtypeStruct((1, HD), F32)
    in_specs = [_qkv_spec(0, g), _qkv_spec(1, g), stat(), stat(), stat(), full(S), full(S), full(1), full(1),
                stat(), stat(), slab(), slab()]
    args = [proj3, proj3, *qkv_p, cos_t, sin_t, qn_g, kn_g, lse, wts, datt3, sbar3]
    aliases = {}
    if slabs is not None:
        in_specs += [pl.BlockSpec(memory_space=pl.ANY)] * 3
        args += list(slabs)
        aliases = {13: 0, 14: 1, 15: 2}
    return pl.pallas_call(
        body, grid=(BL, NH), name="attn_bwd_g%d" % g,
        in_specs=in_specs,
        out_specs=[out_slab(), out_slab(), out_slab(), full(1), full(1)],
        out_shape=[big, big, big, vecs, vecs],
        scratch_shapes=[pltpu.VMEM((2, S, HD), F32), pltpu.VMEM((S + QB, HD), BF16),
                        pltpu.VMEM((S + QB, HD), BF16), pltpu.VMEM((S, HD), BF16), pltpu.VMEM((S, HD), F32),
                        pltpu.VMEM((S, HD), F32), pltpu.VMEM((2, S, HD), F32)],
        input_output_aliases=aliases,
        compiler_params=_cp(("arbitrary", "arbitrary"), VMEM_BIG),
    )(*_hbm(*args))


def _tail(x, proj, h, att, p, tgt, w_o_rnn, w_o_att_t, w_out, w_pg, w_ple_t, norm_ple, b_pg, tm=256):
    nt = T // tm
    inv_d = 1.0 / D

    def body(x_ref, h_ref, zr_ref, att_ref, za_ref, g0a_ref, g0b_ref, g1a_ref, g1b_ref, p_ref, tgt_ref,
             np_ref, bpg_ref, wor_hbm, woa_hbm, wout_hbm, wpg_hbm, wple_hbm,
             dx1_ref, merged_ref, n1_ref, dpre_ref, dpe_ref, dyr_ref, dya_ref, slab_a_ref, slab_c_ref, dh_ref,
             datt_ref, sbar_ref, yrnn_ref, yatt_ref, loss_ref, dnp_ref, dbpg_ref,
             wor, woa, wout, wpg, wple, sems):
        first = pl.program_id(0) == 0

        @pl.when(first)
        def _():
            pairs = ((wor_hbm, wor), (woa_hbm, woa), (wout_hbm, wout), (wpg_hbm, wpg), (wple_hbm, wple))
            _copy_together([pltpu.make_async_copy(src, dst, sems.at[k]) for k, (src, dst) in enumerate(pairs)])

        xv = x_ref[...]
        hv = h_ref[...]
        zr = zr_ref[...].astype(F32)
        av = att_ref[...]
        za = za_ref[...].astype(F32)
        szr = _sigmoid(zr)
        silu_r = zr * szr
        yrnn_b = (hv * silu_r).astype(BF16)
        sza = _sigmoid(za)
        silu_a = za * sza
        yatt_b = (av * silu_a).astype(BF16)
        yrnn_ref[...] = yrnn_b
        yatt_ref[...] = yatt_b
        yr = _dot(yrnn_b, wor[...])
        ya = _dot_nt(yatt_b, woa[...])
        g0 = _sigmoid(jnp.concatenate([g0a_ref[...], g0b_ref[...]], axis=1).astype(F32))
        g1 = _sigmoid(jnp.concatenate([g1a_ref[...], g1b_ref[...]], axis=1).astype(F32))
        merged_b = (g0 * yr + g1 * ya).astype(BF16)
        merged_ref[...] = merged_b
        x1 = xv + _dot(merged_b, wout[...])
        rstd = lax.rsqrt(jnp.mean(x1 * x1, axis=-1, keepdims=True) + EPS)
        xh = x1 * rstd
        npl = np_ref[...]
        n1_b = (xh * npl).astype(BF16)
        n1_ref[...] = n1_b
        pg = _sigmoid(_dot(n1_b, wpg[...]) + bpg_ref[...])
        pe = _dot_nt(p_ref[...].astype(BF16), wple[...])
        err = x1 + pg * pe - tgt_ref[...]
        loss_t = 0.5 * inv_d * jnp.sum(err * err)
        dy = err * inv_d
        dpe_ref[...] = (dy * pg).astype(BF16)
        dpre = dy * pe * pg * (1.0 - pg)
        dpre_b = dpre.astype(BF16)
        dpre_ref[...] = dpre_b
        dn1 = _dot_nt(dpre_b, wpg[...])
        dnp = jnp.sum(dn1 * xh, axis=0, keepdims=True)
        dbpg = jnp.sum(dpre, axis=0, keepdims=True)
        gd = dn1 * npl
        dx1 = dy + rstd * (gd - xh * jnp.mean(gd * xh, axis=-1, keepdims=True))
        dx1_ref[...] = dx1
        dmerged = _dot_nt(dx1.astype(BF16), wout[...])
        dyr_b = (dmerged * g0).astype(BF16)
        dya_b = (dmerged * g1).astype(BF16)
        dyr_ref[...] = dyr_b
        dya_ref[...] = dya_b
        slab_c_ref[:, ATT:ATT + D] = (dmerged * yr * g0 * (1.0 - g0)).astype(BF16)
        slab_c_ref[:, ATT + D:ATT + 2 * D] = (dmerged * ya * g1 * (1.0 - g1)).astype(BF16)
        dyrnn = _dot_nt(dyr_b, wor[...])
        dyatt = _dot(dya_b, woa[...])
        dh_ref[...] = dyrnn * silu_r
        slab_a_ref[...] = (dyrnn * hv * szr * (1.0 + zr * (1.0 - szr))).astype(BF16)
        datt = dyatt * silu_a
        datt_ref[...] = datt
        slab_c_ref[:, 0:ATT] = (dyatt * av * sza * (1.0 + za * (1.0 - sza))).astype(BF16)
        da = datt * av
        for hh in range(NH):
            seg = slice(hh * HD, (hh + 1) * HD)
            sbar_ref[:, seg] = jnp.broadcast_to(jnp.sum(da[:, seg], axis=-1, keepdims=True), (tm, HD))

        @pl.when(first)
        def _():
            loss_ref[...] = jnp.full((8, LANES), loss_t, F32)
            dnp_ref[...] = dnp
            dbpg_ref[...] = dbpg

        @pl.when(jnp.logical_not(first))
        def _():
            loss_ref[...] += jnp.full((8, LANES), loss_t, F32)
            dnp_ref[...] += dnp
            dbpg_ref[...] += dbpg

    tok = lambda w: pl.BlockSpec((tm, w), lambda i: (i, 0))
    col = lambda w, blk: pl.BlockSpec((tm, w), lambda i: (i, blk))
    vec = lambda: pl.BlockSpec((1, D), lambda i: (0, 0))
    hbm = lambda: pl.BlockSpec(memory_space=pl.ANY)
    gb = OFF_G // 512
    in_specs = [tok(D), tok(DR), col(DR, 1), tok(ATT), col(ATT, OFF_ZA // ATT),
                col(512, gb), col(512, gb + 1), col(512, gb + 2), col(512, gb + 3),
                tok(PLE), tok(D), vec(), vec(), hbm(), hbm(), hbm(), hbm(), hbm()]
    sh = lambda w, dt: jax.ShapeDtypeStruct((T, w), dt)
    out_shape = [sh(D, F32), sh(D, BF16), sh(D, BF16), sh(D, BF16), sh(D, BF16), sh(D, BF16), sh(D, BF16),
                 sh(A_W, BF16), sh(C_W, BF16), sh(DR, F32), sh(ATT, F32), sh(ATT, F32),
                 sh(DR, BF16), sh(ATT, BF16),
                 jax.ShapeDtypeStruct((8, LANES), F32), jax.ShapeDtypeStruct((1, D), F32),
                 jax.ShapeDtypeStruct((1, D), F32)]
    out_specs = [tok(D), tok(D), tok(D), tok(D), tok(D), tok(D), tok(D), col(DR, 1), tok(C_W), tok(DR),
                 tok(ATT), tok(ATT), tok(DR), tok(ATT),
                 pl.BlockSpec((8, LANES), lambda i: (0, 0)), vec(), vec()]
    return pl.pallas_call(
        body, grid=(nt,), name="tail_fwd_bwd",
        in_specs=in_specs, out_specs=out_specs, out_shape=out_shape,
        scratch_shapes=[pltpu.VMEM((DR, D), BF16), pltpu.VMEM((D, ATT), BF16), pltpu.VMEM((D, D), BF16),
                        pltpu.VMEM((D, D), BF16), pltpu.VMEM((D, PLE), BF16), pltpu.SemaphoreType.DMA((5,))],
        compiler_params=_cp(("arbitrary",), VMEM_BIG),
    )(*_hbm(x, h, proj, att, proj, proj, proj, proj, proj, p, tgt, norm_ple, b_pg, w_o_rnn, w_o_att_t, w_out, w_pg,
            w_ple_t))


def _rope_tables():
    pos = jnp.arange(S, dtype=F32)
    inv_freq = ROPE_THETA ** (-jnp.arange(0, HD, 2, dtype=F32) / HD)
    ang = pos[:, None] * inv_freq[None, :]
    cos, sin = jnp.cos(ang), jnp.sin(ang)
    return jnp.concatenate([cos, cos], axis=1), jnp.concatenate([-sin, sin], axis=1)


def _local_step(x, p, tgt, project, other_weights, norm_mix, conv_b,
                w_rg_a, b_rg_a, w_rg_x, b_rg_x, lam, q_norm, k_norm, norm_ple, b_pg, start_reduce=None,
                entry_token=None):
    if start_reduce is None:
        start_reduce = lambda arrs, tag: (jnp.zeros((8, LANES), F32), arrs)
    if entry_token is None:
        entry_token = jnp.zeros((8, LANES), F32)
    cos_t, sin_t = _rope_tables()
    wa_b = w_rg_a.astype(BF16)
    wx_b = w_rg_x.astype(BF16)

    hn = _rmsnorm_fwd(x, norm_mix, entry_token)
    proj, w_bufs, chips, conv_w, token = project(hn)
    proj3 = proj.reshape(BL, S, NIN)
    h3, gates = _rnn_fwd(proj3, conv_w, conv_b, wa_b, b_rg_a, wx_b, b_rg_x, lam, token)
    att3, lse, wts, *qkv_p = _attn_fwd(proj3, cos_t, sin_t, q_norm, k_norm)
    w_o_rnn, w_o_att_t, w_out, w_pg, w_ple_t = other_weights(att3)
    (dx1, merged, n1, dpre, dpe, dyr, dya, slab_a, slab_c, dh, datt, sbar, yrnn, yatt, loss8, dnp, dbpg) = _tail(
        x, proj, h3.reshape(T, DR), att3.reshape(T, ATT), p, tgt, w_o_rnn, w_o_att_t, w_out, w_pg, w_ple_t,
        norm_ple, b_pg)

    token, pending_out = start_reduce([
        _mm_tn(yrnn, dyr, 640, 2048, "dw_o_rnn"),
        _mm_tn(dya, yatt, 512, 2048, "dw_o_att_t"),
        _mm_tn(merged, dx1, 512, 2048, "dw_out"),
        _mm_tn(n1, dpre, 512, 2048, "dw_ple_gate"),
        _mm_tn(dpe, p, 512, 2048, "dw_ple_t")], "out")

    slab_a3, dcw, dcb, dwa, dba, dwx, dbx, dlam = _rnn_bwd(
        proj3, h3, dh.reshape(BL, S, DR), gates, slab_a.reshape(BL, S, A_W), conv_w, conv_b, wa_b, b_rg_a, wx_b, b_rg_x, lam,
        token)
    datt3 = datt.reshape(BL, S, ATT)
    sbar3 = sbar.reshape(BL, S, ATT)
    slabs = None
    dqn = []
    dkn = []
    for g in range(NG):
        dq, dk, dv, dqn_g, dkn_g = _attn_bwd_group(g, proj3, cos_t, sin_t, q_norm[g:g + 1], k_norm[g:g + 1],
                                                   lse, wts, qkv_p, datt3, sbar3, slabs)
        slabs = (dq, dk, dv)
        dqn.append(dqn_g)
        dkn.append(dkn_g)
    pieces = [slab_a3.reshape(T, A_W)] + [t.reshape(T, GW) for t in slabs] + [slab_c]
    dw_in_t, db_in = _dw_in(pieces, hn)
    token, pending_in = start_reduce([dw_in_t], "in")
    grad_x, dnm = _grad_x(pieces, w_bufs, chips, token, x, dx1, norm_mix)

    small = dict(w_rg_a=dwa, w_rg_x=dwx, norm_mix=dnm, b_in=db_in, conv_b=dcb, b_rg_a=dba, b_rg_x=dbx,
                 lru_lambda=dlam, q_norm=dqn, k_norm=dkn, norm_ple=dnp, b_ple_gate=dbpg, conv_w=dcw, loss=loss8)
    return grad_x, pending_out, pending_in, small


MESH = pl.DeviceIdType.MESH
HBM_SPEC = pl.BlockSpec(memory_space=pl.ANY)


def _my_pos():
    return lax.axis_index("x"), lax.axis_index("y"), lax.axis_index("c")


def _flip(pos, k):
    x, y, c = pos
    return (1 - x if k & 4 else x, 1 - y if k & 2 else y, 1 - c if k & 1 else c)


def _lin(pos):
    return 4 * pos[0] + 2 * pos[1] + pos[2]


def _chip(pos):
    return 2 * pos[0] + pos[1]


def _exchange_within_chip(parts, name):
    na = len(parts)

    def body(*refs):
        a_refs = refs[:na]
        recv_refs = refs[na:2 * na]
        send_sems, recv_sems = refs[2 * na:]
        me = _my_pos()
        c = me[2]
        sibling = _flip(me, 1)
        remote = []
        for i in range(na):
            for q in range(NCHIP):
                rc = pltpu.make_async_remote_copy(
                    src_ref=a_refs[i].at[q, 1 - c], dst_ref=recv_refs[i].at[q],
                    send_sem=send_sems.at[NCHIP * i + q], recv_sem=recv_sems.at[NCHIP * i + q],
                    device_id=sibling, device_id_type=MESH)
                rc.start()
                remote.append(rc)
        for rc in remote:
            rc.wait_recv()
        for rc in remote:
            rc.wait_send()

    return pl.pallas_call(
        body, name=name, out_shape=[jax.ShapeDtypeStruct((NCHIP,) + a.shape[2:], a.dtype) for a in parts],
        in_specs=[HBM_SPEC] * na, out_specs=[HBM_SPEC] * na,
        scratch_shapes=[pltpu.SemaphoreType.DMA((NCHIP * na,)), pltpu.SemaphoreType.DMA((NCHIP * na,))],
    )(*parts)


HBM_ONLY = pl.BlockSpec(memory_space=pltpu.HBM)
SEM_SPEC = pl.BlockSpec(memory_space=pltpu.SEMAPHORE)
SPLIT_COPY = pltpu.CompilerParams(has_side_effects=pltpu.SideEffectType.DATAFLOW_SIDE_EFFECTING)


def _chip_peers(me):
    return [_flip(me, 4), _flip(me, 2), _flip(me, 6)]


def _between_chips_start(parts, name):
    na = len(parts)

    def body(*refs):
        a_refs = refs[:na]
        land_refs = refs[na:2 * na]
        send_sems, recv_sems = refs[2 * na], refs[2 * na + 1]
        token = refs[-1]
        me = _my_pos()
        myq = _chip(me)
        for i in range(na):
            for j, peer in enumerate(_chip_peers(me)):
                pltpu.make_async_remote_copy(
                    src_ref=a_refs[i].at[_chip(peer)], dst_ref=land_refs[i].at[myq],
                    send_sem=send_sems.at[3 * i + j], recv_sem=recv_sems.at[3 * i + j],
                    device_id=peer, device_id_type=MESH).start()
        token[...] = jnp.zeros_like(token)

    hbm = [pltpu.HBM(a.shape, a.dtype) for a in parts]
    srcs = [pltpu.with_memory_space_constraint(a, pltpu.HBM) for a in parts]
    lands = [pltpu.with_memory_space_constraint(lax.empty(a.shape, a.dtype), pltpu.HBM) for a in parts]
    res = pl.pallas_call(
        body, name=name,
        out_shape=(pltpu.SemaphoreType.DMA((3 * na,)), pltpu.SemaphoreType.DMA((3 * na,)), *hbm, *hbm,
                   jax.ShapeDtypeStruct((8, LANES), F32)),
        in_specs=[HBM_ONLY] * (2 * na),
        out_specs=(SEM_SPEC, SEM_SPEC, *([HBM_ONLY] * (2 * na)), pl.BlockSpec(memory_space=pltpu.VMEM)),
        input_output_aliases={i: 2 + i for i in range(2 * na)},
        compiler_params=SPLIT_COPY,
    )(*srcs, *lands)
    return res[-1], (res[0], res[1], list(res[2:2 + na]), list(res[2 + na:2 + 2 * na]))


def _between_chips_wait(pending, after, name):
    send_sems, recv_sems, parts, lands = pending
    na = len(parts)

    def body(*refs):
        a_refs = refs[:na]
        land_refs = refs[na:2 * na]
        send_sems, recv_sems = refs[2 * na], refs[2 * na + 1]
        me = _my_pos()
        for i in range(na):
            for j, peer in enumerate(_chip_peers(me)):
                cp = pltpu.make_async_remote_copy(
                    src_ref=a_refs[i].at[_chip(peer)], dst_ref=land_refs[i].at[_chip(peer)],
                    send_sem=send_sems.at[3 * i + j], recv_sem=recv_sems.at[3 * i + j],
                    device_id=peer, device_id_type=MESH)
                cp.wait_send()
                cp.wait_recv()

    hbm = [pltpu.HBM(a.shape, a.dtype) for a in parts]
    res = pl.pallas_call(
        body, name=name, out_shape=(*hbm, *hbm),
        in_specs=[HBM_ONLY] * (2 * na) + [SEM_SPEC, SEM_SPEC, pl.BlockSpec(memory_space=pl.ANY)],
        out_specs=[HBM_ONLY] * (2 * na),
        input_output_aliases={i: i for i in range(2 * na)},
        compiler_params=SPLIT_COPY,
    )(*parts, *lands, send_sems, recv_sems, after)
    return list(res[:na]), list(res[na:])


def _remote(src, dst, send_sems, recv_sems, idx, peer):
    return pltpu.make_async_remote_copy(src_ref=src, dst_ref=dst, send_sem=send_sems.at[idx],
                                        recv_sem=recv_sems.at[idx], device_id=peer, device_id_type=MESH)


def _copies_own(bufs, me):
    return [(bufs[0], bufs[1].at[me[2]], 0, _flip(me, 1))]


def _copies_near(bufs, me):
    return [(bufs[0], bufs[1].at[0, me[2]], 0, _flip(me, 2)), (bufs[0], bufs[1].at[1, me[2]], 1, _flip(me, 4))]


def _copies_far(bufs, me):
    return [(bufs[0], bufs[1].at[me[2]], 0, _flip(me, 6))]


def _copies_others(bufs, me):
    na = len(bufs) // 2
    return [(bufs[i], bufs[na + i].at[_lin(me)], 7 * i + k - 1, _flip(me, k))
            for i in range(na) for k in range(1, NDEV)]


def _copies_exchange(bufs, me):
    na = len(bufs) // 2
    return [(bufs[i].at[_lin(_flip(me, k))], bufs[na + i].at[_lin(me)], 7 * i + k - 1, _flip(me, k))
            for i in range(na) for k in range(1, NDEV)]


GROUP_COPIES = dict(own=_copies_own, near=_copies_near, far=_copies_far, others=_copies_others,
                    exchange=_copies_exchange)
GROUP_COUNT = dict(own=1, near=2, far=1)
TO_ALL = ("others", "exchange")


def _gather_start(bufs, groups, after, name):
    nb = len(bufs)
    ng = len(groups)

    def body(*refs):
        b = refs[:nb]
        sems = refs[nb + 1:nb + 1 + 2 * ng]
        token = refs[-1]
        me = _my_pos()
        for gi, (group, idx) in enumerate(groups):
            for src, dst, k, peer in GROUP_COPIES[group]([b[i] for i in idx], me):
                _remote(src, dst, sems[2 * gi], sems[2 * gi + 1], k, peer).start()
        token[...] = jnp.zeros_like(token)

    sem_t = []
    for group, idx in groups:
        cnt = 7 * (len(idx) // 2) if group in TO_ALL else GROUP_COUNT[group]
        sem_t += [pltpu.SemaphoreType.DMA((cnt,)), pltpu.SemaphoreType.DMA((cnt,))]
    ins = [pltpu.with_memory_space_constraint(a, pltpu.HBM) for a in bufs]
    res = pl.pallas_call(
        body, name=name,
        out_shape=(*sem_t, *[pltpu.HBM(a.shape, a.dtype) for a in bufs], jax.ShapeDtypeStruct((8, LANES), F32)),
        in_specs=[HBM_ONLY] * nb + [pl.BlockSpec(memory_space=pl.ANY)],
        out_specs=(*([SEM_SPEC] * (2 * ng)), *([HBM_ONLY] * nb), pl.BlockSpec(memory_space=pltpu.VMEM)),
        input_output_aliases={i: 2 * ng + i for i in range(nb)},
        compiler_params=SPLIT_COPY,
    )(*ins, after)
    return res[-1], list(res[2 * ng:2 * ng + nb]), [(res[2 * gi], res[2 * gi + 1]) for gi in range(ng)]


def _gather_wait(group, send_sems, recv_sems, bufs, after, name):
    nb = len(bufs)
    copies = GROUP_COPIES[group]

    def body(*refs):
        b = refs[:nb]
        ss, rs = refs[nb], refs[nb + 1]
        me = _my_pos()
        for src, dst, idx, peer in copies(b, me):
            if group in TO_ALL:
                landed = b[nb // 2 + idx // 7].at[_lin(peer)]
            elif group == "own":
                landed = b[1].at[1 - me[2]]
            else:
                landed = dst
            cp = _remote(src, landed, ss, rs, idx, peer)
            cp.wait_send()
            cp.wait_recv()

    res = pl.pallas_call(
        body, name=name, out_shape=[pltpu.HBM(a.shape, a.dtype) for a in bufs],
        in_specs=[HBM_ONLY] * nb + [SEM_SPEC, SEM_SPEC, pl.BlockSpec(memory_space=pl.ANY)],
        out_specs=[HBM_ONLY] * nb,
        input_output_aliases={i: i for i in range(nb)},
        compiler_params=SPLIT_COPY,
    )(*bufs, send_sems, recv_sems, after)
    return list(res)


def _forward_to_sibling(buf, name):
    n = buf.shape[0]

    def body(_in_ref, out_ref, send_sems, recv_sems):
        me = _my_pos()
        c = me[2]
        sibling = _flip(me, 1)
        sends = []
        for r in range(n):
            cp = _remote(out_ref.at[r, c], out_ref.at[r, c], send_sems, recv_sems, r, sibling)
            cp.start()
            sends.append(cp)
        for r in range(n):
            _remote(out_ref.at[r, c], out_ref.at[r, 1 - c], send_sems, recv_sems, r, sibling).wait_recv()
        for cp in sends:
            cp.wait_send()

    return pl.pallas_call(
        body, name=name, out_shape=jax.ShapeDtypeStruct(buf.shape, buf.dtype),
        in_specs=[HBM_SPEC], out_specs=HBM_SPEC,
        scratch_shapes=[pltpu.SemaphoreType.DMA((n,)), pltpu.SemaphoreType.DMA((n,))],
        input_output_aliases={0: 0},
    )(buf)


def _scalar(v):
    return jnp.asarray(v, jnp.int32).reshape(1)


def _sum_pairs(parts, theirs, name):
    na = len(parts)

    def body(c_ref, *refs):
        for i in range(na):
            o_ref = refs[2 * na + i]
            o_ref[0] = (refs[i][0, 0].astype(F32) + refs[na + i][0].astype(F32)).astype(o_ref.dtype)

    def mine_spec(a):
        return pl.BlockSpec((1, 1) + a.shape[2:], lambda q, c_ref: (q, c_ref[0], 0, 0))

    def spec(a):
        return pl.BlockSpec((1,) + a.shape[1:], lambda q, c_ref: (q, 0, 0))

    return pl.pallas_call(
        body, name=name,
        grid_spec=pltpu.PrefetchScalarGridSpec(
            num_scalar_prefetch=1, grid=(NCHIP,),
            in_specs=[mine_spec(a) for a in parts] + [spec(a) for a in theirs],
            out_specs=[spec(a) for a in theirs]),
        out_shape=[jax.ShapeDtypeStruct(a.shape, a.dtype) for a in theirs],
        compiler_params=_cp(("arbitrary",), VMEM_BIG),
    )(_scalar(lax.axis_index("c")), *_hbm(*parts, *theirs))


def _others(q, mine, nblk=NCHIP):
    return jnp.where(q == mine, (q + 1) % nblk, q)


def _sum_chips_adamw(own, recv, wv, mv, vv, token, tr, name):
    _, r, w = recv.shape

    def body(q_ref, own_ref, r0, r1, r2, r3, w_ref, m_ref, v_ref, _token, g_ref, d_ref, m2_ref, v2_ref):
        myq = q_ref[0]
        acc = None
        for q, r_ref in enumerate((r0, r1, r2, r3)):
            term = jnp.where(myq == q, own_ref[0], r_ref[0]).astype(F32)
            acc = term if acc is None else acc + term
        g_ref[...] = acc
        delta, m2, v2 = _adam_math(w_ref[...], acc, m_ref[...], v_ref[...])
        d_ref[...] = delta
        m2_ref[...] = m2
        v2_ref[...] = v2

    def recv_spec(q):
        return pl.BlockSpec((1, tr, w), lambda i, q_ref: (_others(q, q_ref[0]), i, 0))

    rows = lambda: pl.BlockSpec((tr, w), lambda i, q_ref: (i, 0))
    shp = jax.ShapeDtypeStruct((r, w), F32)
    return pl.pallas_call(
        body, name=name,
        grid_spec=pltpu.PrefetchScalarGridSpec(
            num_scalar_prefetch=1, grid=(r // tr,),
            in_specs=[pl.BlockSpec((1, tr, w), lambda i, q_ref: (q_ref[0], i, 0))]
            + [recv_spec(q) for q in range(NCHIP)] + [rows(), rows(), rows()]
            + [pl.BlockSpec((8, LANES), lambda i, q_ref: (0, 0))],
            out_specs=[rows(), rows(), rows(), rows()]),
        out_shape=[shp, shp, shp, shp],
        compiler_params=_cp(("arbitrary",), VMEM_MID),
    )(_scalar(_chip(_my_pos())), *_hbm(own, recv, recv, recv, recv, wv, mv, vv, token))


def _sum_blocks_small(own, recv, mine, transpose, name):
    na = len(recv)
    nblk = recv[0].shape[0]

    def body(q_ref, *refs):
        me = q_ref[0]
        for i in range(na):
            acc = None
            for q in range(nblk):
                term = jnp.where(me == q, refs[i][0], refs[na * (1 + q) + i][0]).astype(F32)
                acc = term if acc is None else acc + term
            refs[na * (1 + nblk) + i][...] = acc.T if transpose[i] else acc

    def oshape(a, tr):
        r, w = a.shape[1:]
        return (w, r) if tr else (r, w)

    own_spec = lambda a: pl.BlockSpec((1,) + a.shape[1:], lambda s, q_ref: (q_ref[0], 0, 0))
    recv_spec = lambda a, q: pl.BlockSpec((1,) + a.shape[1:], lambda s, q_ref: (_others(q, q_ref[0], nblk), 0, 0))
    out_spec = lambda shp: pl.BlockSpec(shp, lambda s, q_ref: (0, 0))
    in_specs = [own_spec(a) for a in own]
    for q in range(nblk):
        in_specs += [recv_spec(a, q) for a in recv]
    return pl.pallas_call(
        body, name=name,
        grid_spec=pltpu.PrefetchScalarGridSpec(
            num_scalar_prefetch=1, grid=(1,), in_specs=in_specs,
            out_specs=[out_spec(oshape(a, tr)) for a, tr in zip(recv, transpose)]),
        out_shape=[jax.ShapeDtypeStruct(oshape(a, tr), F32) for a, tr in zip(recv, transpose)],
        compiler_params=_cp(("arbitrary",), VMEM_MID),
    )(_scalar(mine), *_hbm(*own, *(list(recv) * nblk)))


def _rep_offsets():
    offs = []
    o = 0
    for r in REP_ROWS:
        offs.append(o)
        o += r
    return offs


LOSS_ROW = REP_TOTAL_ROWS


def _pack_small_grads(g):
    offs = _rep_offsets()

    def body(dwa, dwx, dnm, dbin, dcb, dba, dbx, dlam, dq0, dq1, dq2, dk0, dk1, dk2, dnp, dbpg, loss, o_ref):
        o_ref[pl.ds(REP_TOTAL_ROWS - 2, NDEV * REP_ROWS_DEV - REP_TOTAL_ROWS + 2), :] = jnp.zeros(
            (NDEV * REP_ROWS_DEV - REP_TOTAL_ROWS + 2, LANES), F32)
        o_ref[pl.ds(LOSS_ROW, 1), :] = loss[0:1, :]
        for n in range(NRB):
            o_ref[pl.ds(offs[0] + n * RBW, RBW), :] = dwa[n]
            o_ref[pl.ds(offs[1] + n * RBW, RBW), :] = dwx[n]

        def put_vec(off, ref, rows):
            for k in range(rows):
                o_ref[pl.ds(off + k, 1), :] = ref[:, k * LANES:(k + 1) * LANES]

        put_vec(offs[2], dnm, REP_ROWS[2])
        put_vec(offs[3], dbin, REP_ROWS[3])
        put_vec(offs[4], dcb, REP_ROWS[4])
        put_vec(offs[5], dba, REP_ROWS[5])
        put_vec(offs[6], dbx, REP_ROWS[6])
        put_vec(offs[7], dlam, REP_ROWS[7])
        for k, ref in enumerate((dq0, dq1, dq2)):
            o_ref[pl.ds(offs[8] + k, 1), :] = ref[...]
        for k, ref in enumerate((dk0, dk1, dk2)):
            o_ref[pl.ds(offs[9] + k, 1), :] = ref[...]
        put_vec(offs[10], dnp, REP_ROWS[10])
        put_vec(offs[11], dbpg, REP_ROWS[11])

    args = [g["w_rg_a"], g["w_rg_x"], g["norm_mix"], g["b_in"], g["conv_b"], g["b_rg_a"], g["b_rg_x"],
            g["lru_lambda"], *g["q_norm"], *g["k_norm"], g["norm_ple"], g["b_ple_gate"], g["loss"]]
    full = lambda shp: pl.BlockSpec(shp, lambda: (0,) * len(shp))
    return pl.pallas_call(
        body, name="pack_small_grads",
        in_specs=[full(a.shape) for a in args],
        out_specs=full((NDEV * REP_ROWS_DEV, LANES)),
        out_shape=jax.ShapeDtypeStruct((NDEV * REP_ROWS_DEV, LANES), F32),
    )(*_hbm(*args))


def _adam_math(wv, gv, mv, vv):
    c1 = 1.0 - B1 ** STEP
    c2 = 1.0 - B2 ** STEP
    m2 = B1 * mv + (1.0 - B1) * gv
    v2 = B2 * vv + (1.0 - B2) * (gv * gv)
    delta = (-LR) * ((m2 / c1) / (jnp.sqrt(v2 / c2) + AEPS) + WD * wv)
    return delta, m2, v2


def _adamw_small(rep_flat, w, m, v):
    offs = _rep_offsets()
    n = len(REP_NAMES)

    def body(*refs):
        g_ref = refs[0]
        w_refs = refs[1:1 + n]
        m_refs = refs[1 + n:1 + 2 * n]
        v_refs = refs[1 + 2 * n:1 + 3 * n]
        outs = refs[1 + 3 * n:]
        go, do, mo, vo = outs[:n], outs[n:2 * n], outs[2 * n:3 * n], outs[3 * n:]

        def emit(i, idx, gv):
            go[i][idx] = gv
            delta, m2, v2 = _adam_math(w_refs[i][idx], gv, m_refs[i][idx], v_refs[i][idx])
            do[i][idx] = delta
            mo[i][idx] = m2
            vo[i][idx] = v2

        for i in range(n):
            if i < 2:
                for b in range(NRB):
                    emit(i, b, g_ref[pl.ds(offs[i] + b * RBW, RBW), :])
            elif REP_NAMES[i] in ("q_norm", "k_norm"):
                emit(i, slice(None), g_ref[pl.ds(offs[i], NG), :])
            else:
                gv = jnp.concatenate([g_ref[pl.ds(offs[i] + k, 1), :] for k in range(REP_ROWS[i])], axis=1)
                emit(i, slice(None), gv)

    full = lambda shp: pl.BlockSpec(shp, lambda: (0,) * len(shp))
    pspecs = [full(a.shape) for a in w]
    pshapes = [jax.ShapeDtypeStruct(a.shape, F32) for a in w]
    res = pl.pallas_call(
        body, name="adamw_small",
        in_specs=[full(rep_flat.shape)] + pspecs * 3,
        out_specs=pspecs * 4, out_shape=pshapes * 4,
        compiler_params=_cp(None, VMEM_MID),
    )(*_hbm(rep_flat, *w, *m, *v))
    return res[:n], res[n:2 * n], res[2 * n:3 * n], res[3 * n:]


def _adamw_many(w, g, m, v, token):
    n = len(w)

    def body(*refs):
        for i in range(n):
            delta, m2, v2 = _adam_math(refs[i][...], refs[n + i][...], refs[2 * n + i][...], refs[3 * n + i][...])
            refs[4 * n + 1 + i][...] = delta
            refs[5 * n + 1 + i][...] = m2
            refs[6 * n + 1 + i][...] = v2

    full = lambda shp: pl.BlockSpec(shp, lambda: (0,) * len(shp))
    specs = [full(a.shape) for a in w]
    shapes = [jax.ShapeDtypeStruct(a.shape, F32) for a in w]
    res = pl.pallas_call(
        body, name="adamw_shards",
        in_specs=specs * 4 + [full((8, LANES))], out_specs=specs * 3, out_shape=shapes * 3,
        compiler_params=_cp(None, VMEM_MID),
    )(*_hbm(*w, *g, *m, *v, token))
    return res[:n], res[n:2 * n], res[2 * n:]


def kernel(x, p, norm_mix, w_in, b_in, conv_w, conv_b, w_rg_a, b_rg_a, w_rg_x, b_rg_x, lru_lambda, q_norm, k_norm, w_o_rnn, w_o_att, w_out, norm_ple, w_ple_gate, b_ple_gate, w_ple, loss_target, m_norm_mix, m_w_in, m_b_in, m_conv_w, m_conv_b, m_w_rg_a, m_b_rg_a, m_w_rg_x, m_b_rg_x, m_lru_lambda, m_q_norm, m_k_norm, m_w_o_rnn, m_w_o_att, m_w_out, m_norm_ple, m_w_ple_gate, m_b_ple_gate, m_w_ple, v_norm_mix, v_w_in, v_b_in, v_conv_w, v_conv_b, v_w_rg_a, v_b_rg_a, v_w_rg_x, v_b_rg_x, v_lru_lambda, v_q_norm, v_k_norm, v_w_o_rnn, v_w_o_att, v_w_out, v_norm_ple, v_w_ple_gate, v_b_ple_gate, v_w_ple):
    w = dict(norm_mix=norm_mix, w_in=w_in, b_in=b_in, conv_w=conv_w, conv_b=conv_b, w_rg_a=w_rg_a, b_rg_a=b_rg_a,
             w_rg_x=w_rg_x, b_rg_x=b_rg_x, lru_lambda=lru_lambda, q_norm=q_norm, k_norm=k_norm, w_o_rnn=w_o_rnn,
             w_o_att=w_o_att, w_out=w_out, norm_ple=norm_ple, w_ple_gate=w_ple_gate, b_ple_gate=b_ple_gate,
             w_ple=w_ple)
    m = dict(norm_mix=m_norm_mix, w_in=m_w_in, b_in=m_b_in, conv_w=m_conv_w, conv_b=m_conv_b, w_rg_a=m_w_rg_a,
             b_rg_a=m_b_rg_a, w_rg_x=m_w_rg_x, b_rg_x=m_b_rg_x, lru_lambda=m_lru_lambda, q_norm=m_q_norm,
             k_norm=m_k_norm, w_o_rnn=m_w_o_rnn, w_o_att=m_w_o_att, w_out=m_w_out, norm_ple=m_norm_ple,
             w_ple_gate=m_w_ple_gate, b_ple_gate=m_b_ple_gate, w_ple=m_w_ple)
    v = dict(norm_mix=v_norm_mix, w_in=v_w_in, b_in=v_b_in, conv_w=v_conv_w, conv_b=v_conv_b, w_rg_a=v_w_rg_a,
             b_rg_a=v_b_rg_a, w_rg_x=v_w_rg_x, b_rg_x=v_b_rg_x, lru_lambda=v_lru_lambda, q_norm=v_q_norm,
             k_norm=v_k_norm, w_o_rnn=v_w_o_rnn, w_o_att=v_w_o_att, w_out=v_w_out, norm_ple=v_norm_ple,
             w_ple_gate=v_w_ple_gate, b_ple_gate=v_b_ple_gate, w_ple=v_w_ple)
    names = list(w.keys())

    shards = [w_in[0].T.astype(BF16), w_o_rnn[0].astype(BF16), w_o_att[0].T.astype(BF16), w_out[0].astype(BF16),
              w_ple_gate[0].astype(BF16), w_ple[0].T.astype(BF16), conv_w[0]]
    pos = _my_pos()
    me, my_core, my_chip = _lin(pos), pos[2], _chip(pos)
    hbm_empty = lambda shp, dt: lax.empty(shp, dt)
    w_shard, conv_shard = shards[0], shards[6]
    shp = w_shard.shape
    entry_token, bufs, sems = _gather_start(
        [w_shard, hbm_empty((2,) + shp, BF16), hbm_empty((2, 2) + shp, BF16), conv_shard,
         hbm_empty((NDEV,) + conv_shard.shape, F32)],
        [("own", (0, 1)), ("near", (0, 2)), ("others", (3, 4))], norm_mix, "gather_start_near")
    w_src, own_l, near_l, conv_src, conv_l = bufs
    sem_own, sem_near, sem_conv = sems
    gather_out = {}

    def project(hn):
        w_thru, own = _gather_wait("own", *sem_own, [w_src, own_l], hn, "gather_wait_own")
        own = lax.dynamic_update_slice(own, w_shard[None], (my_core, 0, 0)).reshape(1, CHIP_COLS, D)
        chips = [jnp.stack([my_chip]), jnp.stack([my_chip ^ 1, my_chip ^ 2]), jnp.stack([my_chip ^ 3])]
        chips = [c.astype(jnp.int32) for c in chips]
        proj = _in_proj_chips(hn, own, b_in, chips[0], None, entry_token, "in_proj_own")
        w_thru, near = _gather_wait("near", *sem_near, [w_thru, near_l], proj, "gather_wait_near")
        near = _forward_to_sibling(near, "gather_forward_near")
        token, (w_thru, far_l), (sem_far,) = _gather_start(
            [w_thru, hbm_empty((2,) + shp, BF16)], [("far", (0, 1))], near, "gather_start_far")
        near = near.reshape(2, CHIP_COLS, D)
        proj = _in_proj_chips(hn, near, b_in, chips[1], proj, token, "in_proj_near")
        w_thru, far = _gather_wait("far", *sem_far, [w_thru, far_l], proj, "gather_wait_far")
        far = _forward_to_sibling(far[None], "gather_forward_far").reshape(1, CHIP_COLS, D)
        proj = _in_proj_chips(hn, far, b_in, chips[2], proj, token, "in_proj_far")
        conv_thru, conv_g = _gather_wait("others", *sem_conv, [conv_src, conv_l], proj, "gather_wait_conv")
        conv_g = lax.dynamic_update_slice(conv_g, conv_shard[None], (me, 0, 0))
        conv_f = conv_g.transpose(1, 0, 2).reshape(CONVW, DR)
        srcs = list(shards[1:6])
        token, obufs, (sem_out,) = _gather_start(
            srcs + [hbm_empty((NDEV,) + a.shape, BF16) for a in srcs], [("others", tuple(range(10)))], proj,
            "gather_start_out")
        gather_out.update(bufs=obufs, sems=sem_out)
        return proj, [own, near, far], jnp.concatenate(chips), conv_f, token

    def other_weights(after):
        obufs = _gather_wait("others", *gather_out["sems"], gather_out["bufs"], after, "gather_wait_out")
        full = [lax.dynamic_update_slice(a, s[None], (me, 0, 0)) for a, s in zip(obufs[5:], shards[1:6])]
        return [a.reshape((NDEV * a.shape[1], a.shape[2])) for a in full]

    def start_reduce(arrs, tag):
        if tag == "out":
            parts = [a.reshape((NDEV, a.shape[0] // NDEV, a.shape[1])) for a in arrs]
            token, bufs, (sems,) = _gather_start(
                parts + [lax.empty(a.shape, a.dtype) for a in parts], [("exchange", tuple(range(2 * len(parts))))],
                arrs[-1][:SUBLANES], "reduce_out_start")
            return token, (bufs, sems)
        parts = [a.reshape((NCHIP, 2, a.shape[0] // NDEV, a.shape[1])) for a in arrs]
        theirs = _exchange_within_chip(parts, "reduce_within_chip_" + tag)
        return _between_chips_start(_sum_pairs(parts, theirs, "sum_pairs_" + tag), "reduce_between_chips_start_" + tag)

    grad_x, pending_out, pending_in, small = _local_step(
        x.reshape(T, D), p.reshape(T, PLE), loss_target.reshape(T, D),
        project, other_weights,
        norm_mix, conv_b, w_rg_a[0], b_rg_a, w_rg_x[0], b_rg_x, lru_lambda, q_norm[0], k_norm[0],
        norm_ple, b_ple_gate, start_reduce, entry_token)

    rep_parts = _pack_small_grads(small).reshape(NDEV, REP_ROWS_DEV, LANES)
    conv_parts = small["conv_w"].reshape(CONVW, NDEV, DR // NDEV).transpose(1, 0, 2)
    smalls = [rep_parts, conv_parts]
    token, sbufs, (sem_x,) = _gather_start(
        smalls + [lax.empty(a.shape, F32) for a in smalls], [("exchange", (0, 1, 2, 3))], small["norm_mix"],
        "reduce_small_start")

    own_in, recv_in = _between_chips_wait(pending_in, token, "reduce_between_chips_wait_in")
    w_in_res = _sum_chips_adamw(own_in[0], recv_in[0], w_in[0].T, m_w_in[0].T, v_w_in[0].T, token, 304, "adamw_w_in")
    sbufs = _gather_wait("exchange", *sem_x, sbufs, w_in_res[0], "reduce_small_wait")
    g_rep, g_conv = _sum_blocks_small(sbufs[:2], sbufs[2:], me, (False, False), "sum_small")
    token, gbufs, (sem_g,) = _gather_start(
        [g_rep, lax.empty((NDEV,) + g_rep.shape, F32)], [("others", (0, 1))], g_conv, "gather_small_start")
    obufs = _gather_wait("exchange", *pending_out[1], pending_out[0], token, "reduce_out_wait")
    g_o_rnn, g_o_att, g_out, g_pg, g_ple = _sum_blocks_small(
        obufs[:5], obufs[5:], me, (False, True, False, False, True), "sum_out")

    grad, delta, new_m, new_v = {}, {}, {}, {}
    rest = ("w_o_rnn", "w_o_att", "w_out", "w_ple_gate", "w_ple", "conv_w")
    g_rest = [g_o_rnn, g_o_att, g_out, g_pg, g_ple, g_conv]
    rest_res = _adamw_many([w[n][0] for n in rest], g_rest, [m[n][0] for n in rest], [v[n][0] for n in rest], token)
    _, rep_all = _gather_wait("others", *sem_g, gbufs, rest_res[0][0], "gather_small_wait")
    rep_all = lax.dynamic_update_slice(rep_all, g_rep[None], (me, 0, 0)).reshape(NDEV * REP_ROWS_DEV, LANES)
    loss = rep_all[LOSS_ROW, 0]
    rep_shape = lambda a: a if a.ndim == 2 else a.reshape(a.shape[1:])
    res = _adamw_small(rep_all, [rep_shape(w[n]) for n in REP_NAMES], [rep_shape(m[n]) for n in REP_NAMES],
                       [rep_shape(v[n]) for n in REP_NAMES])
    for dst, vals in zip((grad, delta, new_m, new_v), res):
        for n, a in zip(REP_NAMES, vals):
            dst[n] = a.reshape(w[n].shape)
    grad["w_in"], delta["w_in"], new_m["w_in"], new_v["w_in"] = [a.T[None] for a in w_in_res]
    for n, a in zip(rest, g_rest):
        grad[n] = a[None]
    for dst, vals in zip((delta, new_m, new_v), rest_res):
        for n, a in zip(rest, vals):
            dst[n] = a[None]

    return (loss, grad_x.reshape(BL, S, D), *[grad[n] for n in names], *[delta[n] for n in names],
            *[new_m[n] for n in names], *[new_v[n] for n in names])
```

```python
import jax
import jax.numpy as jnp
from jax import lax
from jax.experimental import pallas as pl
from jax.experimental.pallas import tpu as pltpu

F32 = jnp.float32
BF16 = jnp.bfloat16

D = 1024
S = 2048
BL = 2
T = BL * S
NDEV = 8
NCHIP = 4
PLE = 256
DR = 1280
NRB = 10
RBW = 128
CONVW = 4
LRU_C = 8.0
HD = 128
NH = 4
PATTERNS = ((128, 1), (512, 4), (2048, 16))
NG = 3
ATT = NH * HD
GW = NG * ATT
NIN = 2 * DR + 3 * GW + ATT + 2 * D
OFF_ZR = DR
OFF_Q = 2 * DR
OFF_ZA = OFF_Q + 3 * GW
OFF_G = OFF_ZA + ATT
ROPE_THETA = 10000.0
EPS = 1e-6
SCALE = HD ** -0.5
NEG = -1e30
QB = 128
LANES = 128
CT = 512
NCT = NIN // CT
A_W = 2 * DR
C_W = ATT + 2 * D

LR, B1, B2, AEPS, WD, STEP = 0.001, 0.9, 0.999, 1e-08, 0.01, 10

NSHARD_IN = NIN // NDEV
REP_NAMES = ("w_rg_a", "w_rg_x", "norm_mix", "b_in", "conv_b", "b_rg_a", "b_rg_x", "lru_lambda", "q_norm",
             "k_norm", "norm_ple", "b_ple_gate")
REP_ROWS = (NRB * RBW, NRB * RBW, D // LANES, NIN // LANES, DR // LANES, DR // LANES, DR // LANES, DR // LANES,
            NG, NG, D // LANES, D // LANES)
REP_TOTAL_ROWS = sum(REP_ROWS)
REP_ROWS_DEV = 344
BIG_NAMES = ("w_in", "w_o_rnn", "w_o_att", "w_out", "w_ple_gate", "w_ple")

VMEM_BIG = 56 * 1024 * 1024
VMEM_MID = 40 * 1024 * 1024


def _cp(sem=None, vmem=None):
    return pltpu.CompilerParams(dimension_semantics=sem, vmem_limit_bytes=vmem)


def _hbm(*arrays):
    return [pltpu.with_memory_space_constraint(a, pltpu.HBM) for a in arrays]


def _copy_together(copies):
    for cp in copies:
        cp.start()
    for cp in copies:
        cp.wait()


def _dot(a, b):
    return jnp.dot(a, b, preferred_element_type=F32)


def _dot_nt(a, b):
    return lax.dot_general(a, b, (((1,), (1,)), ((), ())), preferred_element_type=F32)


def _dot_tn(a, b):
    return lax.dot_general(a, b, (((0,), (0,)), ((), ())), preferred_element_type=F32)


def _sigmoid(x):
    return jax.nn.sigmoid(x)


def _perm(j):
    jq = j - OFF_Q // CT
    inside = (j >= OFF_Q // CT) & (j < OFF_ZA // CT)
    return jnp.where(inside, OFF_Q // CT + (jq % 3) * 3 + jq // 3, j)


PIECES = ((0, A_W // CT), (OFF_Q // CT, GW // CT), (OFF_Q // CT + 3, GW // CT), (OFF_Q // CT + 6, GW // CT),
          (OFF_ZA // CT, C_W // CT))


def _rmsnorm_fwd(x, gain, token, tm=512):
    def body(x_ref, g_ref, _token, o_ref):
        xv = x_ref[...]
        var = jnp.mean(xv * xv, axis=-1, keepdims=True)
        o_ref[...] = (xv * lax.rsqrt(var + EPS) * g_ref[...]).astype(BF16)

    return pl.pallas_call(
        body, grid=(T // tm,), name="rmsnorm_fwd",
        in_specs=[pl.BlockSpec((tm, D), lambda i: (i, 0)), pl.BlockSpec((1, D), lambda i: (0, 0)),
                  pl.BlockSpec((8, LANES), lambda i: (0, 0))],
        out_specs=pl.BlockSpec((tm, D), lambda i: (i, 0)),
        out_shape=jax.ShapeDtypeStruct((T, D), BF16),
        compiler_params=_cp(("parallel",)),
    )(*_hbm(x, gain, token))


CHIP_COLS = NIN // NCHIP


def _in_proj_chips(hn, w_rows, bias, chips, proj, token, name, tm=1024):
    n = w_rows.shape[0]

    def body(chips_ref, a_ref, w_ref, b_ref, _token, *rest):
        o_ref = rest[-1]
        o_ref[...] = (_dot_nt(a_ref[...], w_ref[0]) + b_ref[...]).astype(BF16)

    in_specs = [pl.BlockSpec((tm, D), lambda s, i, ch: (i, 0)),
                pl.BlockSpec((1, CHIP_COLS, D), lambda s, i, ch: (s, 0, 0)),
                pl.BlockSpec((1, CHIP_COLS), lambda s, i, ch: (0, ch[s])),
                pl.BlockSpec((8, LANES), lambda s, i, ch: (0, 0))]
    args = [hn, w_rows, bias, token]
    aliases = {}
    if proj is not None:
        in_specs.append(pl.BlockSpec(memory_space=pl.ANY))
        args.append(proj)
        aliases = {5: 0}
    return pl.pallas_call(
        body, name=name,
        grid_spec=pltpu.PrefetchScalarGridSpec(
            num_scalar_prefetch=1, grid=(n, T // tm), in_specs=in_specs,
            out_specs=pl.BlockSpec((tm, CHIP_COLS), lambda s, i, ch: (i, ch[s]))),
        out_shape=jax.ShapeDtypeStruct((T, NIN), BF16),
        input_output_aliases=aliases,
        compiler_params=_cp(("arbitrary", "arbitrary"), VMEM_BIG),
    )(chips, *_hbm(*args))


def _grad_x(pieces, w_bufs, chips, token, x, dx1, gain, tm=512):
    nb = len(w_bufs)

    def body(chips_ref, a_ref, q_ref, k_ref, v_ref, c_ref, *rest):
        w_hbm = rest[:nb]
        x_ref, dx1_ref, g_ref, dx_ref, dg_ref, w, sems = rest[nb + 1:]
        first = pl.program_id(0) == 0

        @pl.when(first)
        def _():
            s = 0
            copies = []
            for buf in w_hbm:
                for r in range(buf.shape[0]):
                    row = pl.multiple_of(chips_ref[s] * CHIP_COLS, 128)
                    copies.append(pltpu.make_async_copy(buf.at[r], w.at[pl.ds(row, CHIP_COLS), :], sems.at[s]))
                    s += 1
            _copy_together(copies)

        acc = _dot(a_ref[...], w[pl.ds(0, A_W), :])
        for kind, p_ref in enumerate((q_ref, k_ref, v_ref)):
            for g in range(NG):
                row = OFF_Q + (3 * g + kind) * CT
                acc = acc + _dot(p_ref[:, g * CT:(g + 1) * CT], w[pl.ds(row, CT), :])
        dn = acc + _dot(c_ref[...], w[pl.ds(OFF_ZA, C_W), :])
        xv = x_ref[...]
        rstd = lax.rsqrt(jnp.mean(xv * xv, axis=-1, keepdims=True) + EPS)
        xh = xv * rstd
        dg = jnp.sum(dn * xh, axis=0, keepdims=True)
        gd = dn * g_ref[...]
        dx_ref[...] = dx1_ref[...] + rstd * (gd - xh * jnp.mean(gd * xh, axis=-1, keepdims=True))

        @pl.when(first)
        def _():
            dg_ref[...] = dg

        @pl.when(jnp.logical_not(first))
        def _():
            dg_ref[...] += dg

    tok = lambda wd: pl.BlockSpec((tm, wd), lambda i, ch: (i, 0))
    vec = lambda: pl.BlockSpec((1, D), lambda i, ch: (0, 0))
    return pl.pallas_call(
        body, name="grad_x",
        grid_spec=pltpu.PrefetchScalarGridSpec(
            num_scalar_prefetch=1, grid=(T // tm,),
            in_specs=[tok(A_W), tok(GW), tok(GW), tok(GW), tok(C_W)] + [pl.BlockSpec(memory_space=pl.ANY)] * nb
            + [pl.BlockSpec((8, LANES), lambda i, ch: (0, 0)), tok(D), tok(D), vec()],
            out_specs=[tok(D), vec()],
            scratch_shapes=[pltpu.VMEM((NIN, D), BF16), pltpu.SemaphoreType.DMA((NCHIP,))]),
        out_shape=[jax.ShapeDtypeStruct((T, D), F32), jax.ShapeDtypeStruct((1, D), F32)],
        compiler_params=_cp(("arbitrary",), VMEM_BIG),
    )(chips, *_hbm(*pieces, *w_bufs, token, x, dx1, gain))


def _dw_in(pieces, hn):
    def body(a_ref, q_ref, k_ref, v_ref, c_ref, h_hbm, o_ref, s_ref, h):
        j = pl.program_id(0)

        @pl.when(j == 0)
        def _():
            pltpu.sync_copy(h_hbm, h)

        def step(x_ref):
            xv = x_ref[...]
            o_ref[...] = _dot_tn(xv, h[...]).astype(BF16)
            s_ref[...] = jnp.sum(xv.astype(F32), axis=0, keepdims=True)

        for x_ref, (lo, n) in zip((a_ref, q_ref, k_ref, v_ref, c_ref), PIECES):
            pl.when((j >= lo) & (j < lo + n))(lambda x_ref=x_ref: step(x_ref))

    def piece_spec(lo, n):
        return pl.BlockSpec((T, CT), lambda j: (0, jnp.clip(j - lo, 0, n - 1)))

    return pl.pallas_call(
        body, grid=(NCT,), name="dw_in",
        in_specs=[piece_spec(lo, n) for lo, n in PIECES] + [pl.BlockSpec(memory_space=pl.ANY)],
        out_specs=[pl.BlockSpec((CT, D), lambda j: (_perm(j), 0)), pl.BlockSpec((1, CT), lambda j: (0, _perm(j)))],
        out_shape=[jax.ShapeDtypeStruct((NIN, D), BF16), jax.ShapeDtypeStruct((1, NIN), F32)],
        scratch_shapes=[pltpu.VMEM((T, D), BF16)],
        compiler_params=_cp(("arbitrary",), VMEM_BIG),
    )(*_hbm(*pieces, hn))


def _mm_tn(a, b, ta, tt, name):
    m = a.shape[1]
    n = b.shape[1]
    nt = T // tt

    def body(a_ref, b_ref, o_ref, acc):
        t = pl.program_id(1)
        p = _dot_tn(a_ref[...].astype(BF16), b_ref[...].astype(BF16))

        @pl.when(t == 0)
        def _():
            acc[...] = p

        @pl.when(t > 0)
        def _():
            acc[...] += p

        @pl.when(t == nt - 1)
        def _():
            o_ref[...] = acc[...].astype(BF16)

    return pl.pallas_call(
        body, grid=(m // ta, nt), name=name,
        in_specs=[pl.BlockSpec((tt, ta), lambda j, t: (t, j)), pl.BlockSpec((tt, n), lambda j, t: (t, 0))],
        out_specs=pl.BlockSpec((ta, n), lambda j, t: (j, 0)),
        out_shape=jax.ShapeDtypeStruct((m, n), BF16),
        scratch_shapes=[pltpu.VMEM((ta, n), F32)],
        compiler_params=_cp(("parallel", "arbitrary"), VMEM_MID),
    )(*_hbm(a, b))


def _row_iota():
    return lax.broadcasted_iota(jnp.int32, (S, RBW), 0)


SUBLANES = 8
N_SHIFT_BUFS = 4


class _Shifter:
    def __init__(self, bufs):
        self.bufs = bufs
        self.k = 0

    def _store(self, v, fill, front):
        b = self.bufs.at[self.k % N_SHIFT_BUFS]
        self.k += 1
        b[pl.ds(0 if front else SUBLANES + S, SUBLANES), :] = jnp.full((SUBLANES, RBW), fill, F32)
        b[pl.ds(SUBLANES, S), :] = v
        return b

    def down(self, v, ds, fill):
        b = self._store(v, fill, True)
        return [b[pl.ds(SUBLANES - d, S), :] for d in ds]

    def up(self, v, ds, fill):
        b = self._store(v, fill, False)
        return [b[pl.ds(SUBLANES + d, S), :] for d in ds]


def _shift_down(v, d, sh, fill):
    return sh.down(v, (d,), fill)[0]


def _shift_up(v, d, sh, fill):
    return sh.up(v, (d,), fill)[0]


def _scan_down(a, u, row):
    d = 1
    while d < S:
        last = 2 * d >= S
        if d < SUBLANES:
            u = a * _shift_down(u, d, row, 0.0) + u
            if not last:
                a = a * _shift_down(a, d, row, 1.0)
        else:
            u = jnp.concatenate([u[:d], a[d:] * u[:S - d] + u[d:]], axis=0)
            if not last:
                a = jnp.concatenate([a[:d], a[d:] * a[:S - d]], axis=0)
        d *= 2
    return u


def _scan_up(b, g, row):
    d = 1
    while d < S:
        last = 2 * d >= S
        if d < SUBLANES:
            g = g + b * _shift_up(g, d, row, 0.0)
            if not last:
                b = b * _shift_up(b, d, row, 0.0)
        else:
            g = jnp.concatenate([g[:S - d] + b[:S - d] * g[d:], g[S - d:]], axis=0)
            if not last:
                b = jnp.concatenate([b[:S - d] * b[d:], b[S - d:]], axis=0)
        d *= 2
    return g


def _softplus(x):
    return jnp.maximum(x, 0.0) + jnp.log1p(jnp.exp(-jnp.abs(x)))


N_SAVED = 5


def _rnn_gates(x, cw, cb, wa, ba, wx, bx, lam, row, pad, saved=None):
    xs = pad.down(x, (1, 2, 3), 0.0)
    if saved is not None:
        r, i, a, mult, xc = saved
        return xc, xc.astype(BF16), r, i, _softplus(-lam), a, mult, xs
    xc = cb + cw[3:4, :] * x
    for j in (1, 2, 3):
        xc = xc + cw[3 - j:4 - j, :] * xs[j - 1]
    xcb = xc.astype(BF16)
    r = _sigmoid(_dot(xcb, wa) + ba)
    i = _sigmoid(_dot(xcb, wx) + bx)
    sp = _softplus(-lam)
    log_a = (-LRU_C) * r * sp
    a = jnp.exp(log_a)
    mult = jnp.where(row == 0, 1.0, jnp.sqrt(jnp.tanh(-log_a) * (1.0 + a * a)))
    return xc, xcb, r, i, sp, a, mult, xs


def _rnn_fwd(proj3, conv_w, conv_b, wa, ba, wx, bx, lam, token):
    def body(x_ref, cw_ref, cb_ref, wa_ref, ba_ref, wx_ref, bx_ref, lam_ref, _token, h_ref, g_ref, pad):
        row = _row_iota()
        sh = _Shifter(pad)
        x = x_ref[0].astype(F32)
        xc, _, r, i, _, a, mult, _ = _rnn_gates(x, cw_ref[...], cb_ref[...], wa_ref[0], ba_ref[...],
                                             wx_ref[0], bx_ref[...], lam_ref[...], row, sh)
        for k, val in enumerate((r, i, a, mult, xc)):
            g_ref[k, 0] = val
        h_ref[0] = _scan_down(a, mult * (i * xc), sh)

    vec = lambda: pl.BlockSpec((1, RBW), lambda b, n: (0, n))
    mat = lambda: pl.BlockSpec((1, RBW, RBW), lambda b, n: (n, 0, 0))
    return pl.pallas_call(
        body, grid=(BL, NRB), name="rnn_fwd",
        in_specs=[pl.BlockSpec((1, S, RBW), lambda b, n: (b, 0, n)),
                  pl.BlockSpec((CONVW, RBW), lambda b, n: (0, n)),
                  vec(), mat(), vec(), mat(), vec(), vec(), pl.BlockSpec((8, LANES), lambda b, n: (0, 0))],
        out_specs=[pl.BlockSpec((1, S, RBW), lambda b, n: (b, 0, n)),
                   pl.BlockSpec((N_SAVED, 1, S, RBW), lambda b, n: (0, b, 0, n))],
        out_shape=[jax.ShapeDtypeStruct((BL, S, DR), F32), jax.ShapeDtypeStruct((N_SAVED, BL, S, DR), F32)],
        scratch_shapes=[pltpu.VMEM((N_SHIFT_BUFS, S + 2 * SUBLANES, RBW), F32)],
        compiler_params=_cp(("parallel", "parallel"), VMEM_MID),
    )(*_hbm(proj3, conv_w, conv_b, wa, ba, wx, bx, lam, token))


def _rnn_bwd(proj3, h3, dh3, gates, slab_a3, conv_w, conv_b, wa, ba, wx, bx, lam, token):
    def body(x_ref, h_ref, dh_ref, g_ref, cw_ref, cb_ref, wa_ref, ba_ref, wx_ref, bx_ref, lam_ref, _alias, _token,
             dx_ref, dcw_ref, dcb_ref, dwa_ref, dba_ref, dwx_ref, dbx_ref, dlam_ref, pad):
        row = _row_iota()
        sh = _Shifter(pad)
        x = x_ref[0].astype(F32)
        cw = cw_ref[...]
        wa_v = wa_ref[0]
        wx_v = wx_ref[0]
        lam_v = lam_ref[...]
        xc, xcb, r, i, sp, a, mult, xs = _rnn_gates(x, cw, cb_ref[...], wa_v, ba_ref[...], wx_v, bx_ref[...], lam_v,
                                                    row, sh, [g_ref[k, 0] for k in range(N_SAVED)])
        h = h_ref[0]
        g = _scan_up(_shift_up(a, 1, sh, 0.0), dh_ref[0], sh)
        da = g * _shift_down(h, 1, sh, 0.0)
        dmult = jnp.where(row == 0, 0.0, g * (i * xc))
        gm = g * mult
        di = gm * xc
        dxc = gm * i
        dlog_a = da * a - dmult * (a * a) / mult
        dr = dlog_a * ((-LRU_C) * sp)
        dsp = jnp.sum(dlog_a * ((-LRU_C) * r), axis=0, keepdims=True)
        dlam = dsp * (-_sigmoid(-lam_v))
        dpa = dr * r * (1.0 - r)
        dpx = di * i * (1.0 - i)
        dpab = dpa.astype(BF16)
        dpxb = dpx.astype(BF16)
        dwa = _dot_tn(xcb, dpab)
        dwx = _dot_tn(xcb, dpxb)
        dba = jnp.sum(dpa, axis=0, keepdims=True)
        dbx = jnp.sum(dpx, axis=0, keepdims=True)
        dxc = dxc + _dot_nt(dpab, wa_v) + _dot_nt(dpxb, wx_v)
        dcb = jnp.sum(dxc, axis=0, keepdims=True)
        dx = cw[3:4, :] * dxc
        dcw_rows = [None] * CONVW
        dcw_rows[3] = jnp.sum(dxc * x, axis=0, keepdims=True)
        dxc_up = sh.up(dxc, (1, 2, 3), 0.0)
        for j in (1, 2, 3):
            dx = dx + cw[3 - j:4 - j, :] * dxc_up[j - 1]
            dcw_rows[3 - j] = jnp.sum(dxc * xs[j - 1], axis=0, keepdims=True)
        dx_ref[0] = dx.astype(BF16)
        dcw = jnp.concatenate(dcw_rows, axis=0)
        first = pl.program_id(1) == 0

        @pl.when(first)
        def _():
            dcw_ref[...] = dcw
            dcb_ref[...] = dcb
            dwa_ref[0] = dwa
            dba_ref[...] = dba
            dwx_ref[0] = dwx
            dbx_ref[...] = dbx
            dlam_ref[...] = dlam

        @pl.when(jnp.logical_not(first))
        def _():
            dcw_ref[...] += dcw
            dcb_ref[...] += dcb
            dwa_ref[0] += dwa
            dba_ref[...] += dba
            dwx_ref[0] += dwx
            dbx_ref[...] += dbx
            dlam_ref[...] += dlam

    slab = lambda: pl.BlockSpec((1, S, RBW), lambda n, b: (b, 0, n))
    vec = lambda: pl.BlockSpec((1, RBW), lambda n, b: (0, n))
    mat = lambda: pl.BlockSpec((1, RBW, RBW), lambda n, b: (n, 0, 0))
    taps = lambda: pl.BlockSpec((CONVW, RBW), lambda n, b: (0, n))
    vshape = jax.ShapeDtypeStruct((1, DR), F32)
    mshape = jax.ShapeDtypeStruct((NRB, RBW, RBW), F32)
    return pl.pallas_call(
        body, grid=(NRB, BL), name="rnn_bwd",
        in_specs=[slab(), slab(), slab(), pl.BlockSpec((N_SAVED, 1, S, RBW), lambda n, b: (0, b, 0, n)),
                  taps(), vec(), mat(), vec(), mat(), vec(), vec(),
                  pl.BlockSpec(memory_space=pl.ANY), pl.BlockSpec((8, LANES), lambda n, b: (0, 0))],
        out_specs=[slab(), taps(), vec(), mat(), vec(), mat(), vec(), vec()],
        out_shape=[jax.ShapeDtypeStruct((BL, S, A_W), BF16), jax.ShapeDtypeStruct((CONVW, DR), F32),
                   vshape, mshape, vshape, mshape, vshape, vshape],
        input_output_aliases={11: 0},
        scratch_shapes=[pltpu.VMEM((N_SHIFT_BUFS, S + 2 * SUBLANES, RBW), F32)],
        compiler_params=_cp(("parallel", "arbitrary"), 48 * 1024 * 1024),
    )(*_hbm(proj3, h3, dh3, gates, conv_w, conv_b, wa, ba, wx, bx, lam, slab_a3, token))


NQB = S // QB


def _rms_head(t, gain):
    rstd = lax.rsqrt(jnp.mean(t * t, axis=-1, keepdims=True) + EPS)
    return t * rstd * gain


def _rope(t, cs, sn):
    return t * cs + pltpu.roll(t, HD // 2, 1) * sn


def _rope_t(dy, cs, sn):
    return dy * cs - pltpu.roll(dy, HD // 2, 1) * sn


def _bdot_nt(a, b):
    return lax.dot_general(a, b, (((2,), (2,)), ((0,), (0,))), preferred_element_type=F32)


def _bdot(a, b):
    return lax.dot_general(a, b, (((2,), (1,)), ((0,), (0,))), preferred_element_type=F32)


def _bdot_tn(a, b):
    return lax.dot_general(a, b, (((1,), (1,)), ((0,), (0,))), preferred_element_type=F32)


STRIDE_MAX = 4


def _permute(buf, x, dil, dst, off=0):
    ln = S // dil
    if dil == 1:
        dst[pl.ds(off, S), :] = x.astype(dst.dtype)
        return
    buf[0] = x
    if dil <= STRIDE_MAX:
        for c in range(dil):
            dst[pl.ds(off + c * ln, ln), :] = buf.at[0][pl.ds(c, ln, stride=dil), :].astype(dst.dtype)
        return
    f, r = STRIDE_MAX, dil // STRIDE_MAX
    part = S // f
    for c1 in range(f):
        buf.at[1][pl.ds(c1 * part, part), :] = buf.at[0][pl.ds(c1, part, stride=f), :]
    for c1 in range(f):
        for c2 in range(r):
            dst[pl.ds(off + (c1 + f * c2) * ln, ln), :] = (
                buf.at[1][pl.ds(c1 * part + c2, ln, stride=r), :].astype(dst.dtype))


def _unpermute(buf, xp, dil, dst):
    ln = S // dil
    if dil == 1:
        dst[...] = xp
        return
    if dil <= STRIDE_MAX:
        for c in range(dil):
            dst[pl.ds(c, ln, stride=dil), :] = xp[c * ln:(c + 1) * ln]
        return
    f, r = STRIDE_MAX, dil // STRIDE_MAX
    part = S // f
    for c1 in range(f):
        for c2 in range(r):
            c = c1 + f * c2
            buf.at[1][pl.ds(c1 * part + c2, ln, stride=r), :] = xp[c * ln:(c + 1) * ln]
    for c1 in range(f):
        dst[pl.ds(c1, part, stride=f), :] = buf[1, pl.ds(c1 * part, part), :]


def _blocks3(ref, off=0):
    return ref[pl.ds(off, S), :].reshape(NQB, QB, HD)


def _att_prep(q_ref, k_ref, v_ref, cos_ref, sin_ref, qn, kn, dil, nat, qs, ksp, vsp):
    cs = cos_ref[...]
    sn = sin_ref[...]
    zero = jnp.zeros((QB, HD), BF16)
    ksp[pl.ds(0, QB), :] = zero
    vsp[pl.ds(0, QB), :] = zero
    _permute(nat, _rope(_rms_head(q_ref[0].astype(F32), qn), cs, sn), dil, qs)
    _permute(nat, _rope(_rms_head(k_ref[0].astype(F32), kn), cs, sn), dil, ksp, QB)
    _permute(nat, v_ref[0].astype(F32), dil, vsp, QB)


def _att_scores(qs, ksp, dil):
    nb = S // dil // QB
    q3 = _blocks3(qs)
    shape = (NQB, QB, QB)
    qi = lax.broadcasted_iota(jnp.int32, shape, 1)
    kj = lax.broadcasted_iota(jnp.int32, shape, 2)
    s_c = jnp.where(qi >= kj, _bdot_nt(q3, _blocks3(ksp, QB)) * SCALE, NEG)
    if nb == 1:
        return q3, s_c, None
    jj = lax.broadcasted_iota(jnp.int32, shape, 0)
    ok = (kj >= qi) & ((jj & (nb - 1)) != 0)
    s_p = jnp.where(ok, _bdot_nt(q3, _blocks3(ksp)) * SCALE, NEG)
    return q3, s_c, s_p


def _qkv_spec(kind, g):
    base = OFF_Q // HD + (3 * g + kind) * NH
    return pl.BlockSpec((1, S, HD), lambda b, h: (b, 0, base + h))


def _attn_fwd(proj3, cos_t, sin_t, q_norm, k_norm):
    def body(*refs):
        qkv_refs = refs[:9]
        (cos_ref, sin_ref, qn_ref, kn_ref, att_ref, lse_ref, w_ref, qp_ref, kp_ref, vp_ref,
         nat, qs, ksp, vsp, og) = refs[9:]
        for g, (window, dil) in enumerate(PATTERNS):
            q_ref, k_ref, v_ref = qkv_refs[3 * g:3 * g + 3]
            _att_prep(q_ref, k_ref, v_ref, cos_ref, sin_ref, qn_ref[g:g + 1, :], kn_ref[g:g + 1, :], dil,
                      nat, qs, ksp, vsp)
            qp_ref[g, 0] = qs[...]
            kp_ref[g, 0] = ksp[pl.ds(QB, S), :]
            vp_ref[g, 0] = vsp[pl.ds(QB, S), :]
            _, s_c, s_p = _att_scores(qs, ksp, dil)
            m = jnp.max(s_c, axis=-1, keepdims=True)
            if s_p is not None:
                m = jnp.maximum(m, jnp.max(s_p, axis=-1, keepdims=True))
            e_c = jnp.exp(s_c - m)
            den = jnp.sum(e_c, axis=-1, keepdims=True)
            o = _bdot(e_c.astype(BF16), _blocks3(vsp, QB))
            if s_p is not None:
                e_p = jnp.exp(s_p - m)
                den = den + jnp.sum(e_p, axis=-1, keepdims=True)
                o = o + _bdot(e_p.astype(BF16), _blocks3(vsp))
            _unpermute(nat, (o / den).reshape(S, HD), dil, og.at[g])
            _unpermute(nat, jnp.broadcast_to(m + jnp.log(den), (NQB, QB, HD)).reshape(S, HD), dil,
                       lse_ref.at[g, 0])
        l0 = lse_ref[0, 0]
        l1 = lse_ref[1, 0]
        l2 = lse_ref[2, 0]
        mx = jnp.maximum(jnp.maximum(l0, l1), l2)
        e0 = jnp.exp(l0 - mx)
        e1 = jnp.exp(l1 - mx)
        e2 = jnp.exp(l2 - mx)
        inv = 1.0 / (e0 + e1 + e2)
        w0 = e0 * inv
        w1 = e1 * inv
        w2 = e2 * inv
        w_ref[0, 0] = w0
        w_ref[1, 0] = w1
        w_ref[2, 0] = w2
        att_ref[0] = w0 * og[0] + w1 * og[1] + w2 * og[2]

    in_specs = [_qkv_spec(kind, g) for g in range(NG) for kind in range(3)]
    in_specs += [pl.BlockSpec((S, HD), lambda b, h: (0, 0)), pl.BlockSpec((S, HD), lambda b, h: (0, 0)),
                 pl.BlockSpec((NG, HD), lambda b, h: (0, 0)), pl.BlockSpec((NG, HD), lambda b, h: (0, 0))]
    stat = lambda: pl.BlockSpec((NG, 1, S, HD), lambda b, h: (0, b, 0, h))
    return pl.pallas_call(
        body, grid=(BL, NH), name="attn_fwd",
        in_specs=in_specs,
        out_specs=[pl.BlockSpec((1, S, HD), lambda b, h: (b, 0, h)), stat(), stat(), stat(), stat(), stat()],
        out_shape=[jax.ShapeDtypeStruct((BL, S, ATT), F32),
                   jax.ShapeDtypeStruct((NG, BL, S, ATT), F32),
                   jax.ShapeDtypeStruct((NG, BL, S, ATT), F32)]
        + [jax.ShapeDtypeStruct((NG, BL, S, ATT), BF16)] * 3,
        scratch_shapes=[pltpu.VMEM((2, S, HD), F32), pltpu.VMEM((S, HD), BF16), pltpu.VMEM((S + QB, HD), BF16),
                        pltpu.VMEM((S + QB, HD), BF16), pltpu.VMEM((NG, S, HD), F32)],
        compiler_params=_cp(("parallel", "parallel"), VMEM_BIG),
    )(*_hbm(*([proj3] * 9), cos_t, sin_t, q_norm, k_norm))


def _attn_bwd_group(g, proj3, cos_t, sin_t, qn_g, kn_g, lse, wts, qkv_p, datt3, sbar3, slabs):
    dil = PATTERNS[g][1]
    n_alias = 0 if slabs is None else 3

    def norm_rope_bwd(dpost, raw, gain, cs, sn):
        dn = _rope_t(dpost, cs, sn)
        rstd = lax.rsqrt(jnp.mean(raw * raw, axis=-1, keepdims=True) + EPS)
        xh = raw * rstd
        dgain = jnp.sum(dn * xh, axis=0, keepdims=True)
        gd = dn * gain
        draw = rstd * (gd - xh * jnp.mean(gd * xh, axis=-1, keepdims=True))
        return draw, dgain

    def body(*refs):
        (q_ref, k_ref, qp_ref, kp_ref, vp_ref, cos_ref, sin_ref, qn_ref, kn_ref, lse_ref, w_ref, datt_ref,
         sbar_ref) = refs[:13]
        (dq_ref, dk_ref, dv_ref, dqn_ref, dkn_ref, nat, ksp, vsp, dos, cvp, lsp, acc) = refs[13 + n_alias:]
        qn = qn_ref[...]
        kn = kn_ref[...]
        cs = cos_ref[...]
        sn = sin_ref[...]
        qs = qp_ref.at[0, 0]
        zero = jnp.zeros((QB, HD), BF16)
        ksp[pl.ds(0, QB), :] = zero
        vsp[pl.ds(0, QB), :] = zero
        ksp[pl.ds(QB, S), :] = kp_ref[0, 0]
        vsp[pl.ds(QB, S), :] = vp_ref[0, 0]
        wv = w_ref[0, 0]
        _permute(nat, wv * datt_ref[0], dil, dos)
        _permute(nat, wv * sbar_ref[0], dil, cvp)
        _permute(nat, lse_ref[0, 0], dil, lsp)
        q3, s_c, s_p = _att_scores(qs, ksp, dil)
        do3 = _blocks3(dos)
        lse3 = _blocks3(lsp)[:, :, 0:1]
        cv3 = _blocks3(cvp)[:, :, 0:1]
        p_c = jnp.exp(s_c - lse3)
        ds_c = (p_c * (_bdot_nt(do3, _blocks3(vsp, QB)) - cv3)).astype(BF16)
        dq = _bdot(ds_c, _blocks3(ksp, QB))
        acc[0] = _bdot_tn(ds_c, q3).reshape(S, HD)
        acc[1] = _bdot_tn(p_c.astype(BF16), do3).reshape(S, HD)
        if s_p is not None:
            p_p = jnp.exp(s_p - lse3)
            ds_p = (p_p * (_bdot_nt(do3, _blocks3(vsp)) - cv3)).astype(BF16)
            dq = dq + _bdot(ds_p, _blocks3(ksp))
            early = pl.ds(0, S - QB)
            acc[0, early, :] += _bdot_tn(ds_p, q3).reshape(S, HD)[QB:]
            acc[1, early, :] += _bdot_tn(p_p.astype(BF16), do3).reshape(S, HD)[QB:]
        _unpermute(nat, (dq * SCALE).reshape(S, HD), dil, nat.at[0])
        draw, dqn = norm_rope_bwd(nat[0], q_ref[0].astype(F32), qn, cs, sn)
        dq_ref[0] = draw.astype(BF16)
        _unpermute(nat, acc[0] * SCALE, dil, nat.at[0])
        draw, dkn = norm_rope_bwd(nat[0], k_ref[0].astype(F32), kn, cs, sn)
        dk_ref[0] = draw.astype(BF16)
        _unpermute(nat, acc[1], dil, nat.at[0])
        dv_ref[0] = nat[0].astype(BF16)
        first = (pl.program_id(0) == 0) & (pl.program_id(1) == 0)

        @pl.when(first)
        def _():
            dqn_ref[...] = dqn
            dkn_ref[...] = dkn

        @pl.when(jnp.logical_not(first))
        def _():
            dqn_ref[...] += dqn
            dkn_ref[...] += dkn

    full = lambda r: pl.BlockSpec((r, HD), lambda b, h: (0, 0))
    stat = lambda: pl.BlockSpec((1, 1, S, HD), lambda b, h: (g, b, 0, h))
    slab = lambda: pl.BlockSpec((1, S, HD), lambda b, h: (b, 0, h))
    out_slab = lambda: pl.BlockSpec((1, S, HD), lambda b, h: (b, 0, g * NH + h))
    big = jax.ShapeDtypeStruct((BL, S, GW), BF16)
    vecs = jax.ShapeDtypeStruct((1, HD), F32)
    in_specs = [_qkv_spec(0, g), _qkv_spec(1, g), stat(), stat(), stat(), full(S), full(S), full(1), full(1),
                stat(), stat(), slab(), slab()]
    args = [proj3, proj3, *qkv_p, cos_t, sin_t, qn_g, kn_g, lse, wts, datt3, sbar3]
    aliases = {}
    if slabs is not None:
        in_specs += [pl.BlockSpec(memory_space=pl.ANY)] * 3
        args += list(slabs)
        aliases = {13: 0, 14: 1, 15: 2}
    return pl.pallas_call(
        body, grid=(BL, NH), name="attn_bwd_g%d" % g,
        in_specs=in_specs,
        out_specs=[out_slab(), out_slab(), out_slab(), full(1), full(1)],
        out_shape=[big, big, big, vecs, vecs],
        scratch_shapes=[pltpu.VMEM((2, S, HD), F32), pltpu.VMEM((S + QB, HD), BF16),
                        pltpu.VMEM((S + QB, HD), BF16), pltpu.VMEM((S, HD), BF16), pltpu.VMEM((S, HD), F32),
                        pltpu.VMEM((S, HD), F32), pltpu.VMEM((2, S, HD), F32)],
        input_output_aliases=aliases,
        compiler_params=_cp(("arbitrary", "arbitrary"), VMEM_BIG),
    )(*_hbm(*args))


def _tail(x, proj, h, att, p, tgt, w_o_rnn, w_o_att_t, w_out, w_pg, w_ple_t, norm_ple, b_pg, tm=256):
    nt = T // tm
    inv_d = 1.0 / D

    def body(x_ref, h_ref, zr_ref, att_ref, za_ref, g0a_ref, g0b_ref, g1a_ref, g1b_ref, p_ref, tgt_ref,
             np_ref, bpg_ref, wor_hbm, woa_hbm, wout_hbm, wpg_hbm, wple_hbm,
             dx1_ref, merged_ref, n1_ref, dpre_ref, dpe_ref, dyr_ref, dya_ref, slab_a_ref, slab_c_ref, dh_ref,
             datt_ref, sbar_ref, yrnn_ref, yatt_ref, loss_ref, dnp_ref, dbpg_ref,
             wor, woa, wout, wpg, wple, sems):
        first = pl.program_id(0) == 0

        @pl.when(first)
        def _():
            pairs = ((wor_hbm, wor), (woa_hbm, woa), (wout_hbm, wout), (wpg_hbm, wpg), (wple_hbm, wple))
            _copy_together([pltpu.make_async_copy(src, dst, sems.at[k]) for k, (src, dst) in enumerate(pairs)])

        xv = x_ref[...]
        hv = h_ref[...]
        zr = zr_ref[...].astype(F32)
        av = att_ref[...]
        za = za_ref[...].astype(F32)
        szr = _sigmoid(zr)
        silu_r = zr * szr
        yrnn_b = (hv * silu_r).astype(BF16)
        sza = _sigmoid(za)
        silu_a = za * sza
        yatt_b = (av * silu_a).astype(BF16)
        yrnn_ref[...] = yrnn_b
        yatt_ref[...] = yatt_b
        yr = _dot(yrnn_b, wor[...])
        ya = _dot_nt(yatt_b, woa[...])
        g0 = _sigmoid(jnp.concatenate([g0a_ref[...], g0b_ref[...]], axis=1).astype(F32))
        g1 = _sigmoid(jnp.concatenate([g1a_ref[...], g1b_ref[...]], axis=1).astype(F32))
        merged_b = (g0 * yr + g1 * ya).astype(BF16)
        merged_ref[...] = merged_b
        x1 = xv + _dot(merged_b, wout[...])
        rstd = lax.rsqrt(jnp.mean(x1 * x1, axis=-1, keepdims=True) + EPS)
        xh = x1 * rstd
        npl = np_ref[...]
        n1_b = (xh * npl).astype(BF16)
        n1_ref[...] = n1_b
        pg = _sigmoid(_dot(n1_b, wpg[...]) + bpg_ref[...])
        pe = _dot_nt(p_ref[...].astype(BF16), wple[...])
        err = x1 + pg * pe - tgt_ref[...]
        loss_t = 0.5 * inv_d * jnp.sum(err * err)
        dy = err * inv_d
        dpe_ref[...] = (dy * pg).astype(BF16)
        dpre = dy * pe * pg * (1.0 - pg)
        dpre_b = dpre.astype(BF16)
        dpre_ref[...] = dpre_b
        dn1 = _dot_nt(dpre_b, wpg[...])
        dnp = jnp.sum(dn1 * xh, axis=0, keepdims=True)
        dbpg = jnp.sum(dpre, axis=0, keepdims=True)
        gd = dn1 * npl
        dx1 = dy + rstd * (gd - xh * jnp.mean(gd * xh, axis=-1, keepdims=True))
        dx1_ref[...] = dx1
        dmerged = _dot_nt(dx1.astype(BF16), wout[...])
        dyr_b = (dmerged * g0).astype(BF16)
        dya_b = (dmerged * g1).astype(BF16)
        dyr_ref[...] = dyr_b
        dya_ref[...] = dya_b
        slab_c_ref[:, ATT:ATT + D] = (dmerged * yr * g0 * (1.0 - g0)).astype(BF16)
        slab_c_ref[:, ATT + D:ATT + 2 * D] = (dmerged * ya * g1 * (1.0 - g1)).astype(BF16)
        dyrnn = _dot_nt(dyr_b, wor[...])
        dyatt = _dot(dya_b, woa[...])
        dh_ref[...] = dyrnn * silu_r
        slab_a_ref[...] = (dyrnn * hv * szr * (1.0 + zr * (1.0 - szr))).astype(BF16)
        datt = dyatt * silu_a
        datt_ref[...] = datt
        slab_c_ref[:, 0:ATT] = (dyatt * av * sza * (1.0 + za * (1.0 - sza))).astype(BF16)
        da = datt * av
        for hh in range(NH):
            seg = slice(hh * HD, (hh + 1) * HD)
            sbar_ref[:, seg] = jnp.broadcast_to(jnp.sum(da[:, seg], axis=-1, keepdims=True), (tm, HD))

        @pl.when(first)
        def _():
            loss_ref[...] = jnp.full((8, LANES), loss_t, F32)
            dnp_ref[...] = dnp
            dbpg_ref[...] = dbpg

        @pl.when(jnp.logical_not(first))
        def _():
            loss_ref[...] += jnp.full((8, LANES), loss_t, F32)
            dnp_ref[...] += dnp
            dbpg_ref[...] += dbpg

    tok = lambda w: pl.BlockSpec((tm, w), lambda i: (i, 0))
    col = lambda w, blk: pl.BlockSpec((tm, w), lambda i: (i, blk))
    vec = lambda: pl.BlockSpec((1, D), lambda i: (0, 0))
    hbm = lambda: pl.BlockSpec(memory_space=pl.ANY)
    gb = OFF_G // 512
    in_specs = [tok(D), tok(DR), col(DR, 1), tok(ATT), col(ATT, OFF_ZA // ATT),
                col(512, gb), col(512, gb + 1), col(512, gb + 2), col(512, gb + 3),
                tok(PLE), tok(D), vec(), vec(), hbm(), hbm(), hbm(), hbm(), hbm()]
    sh = lambda w, dt: jax.ShapeDtypeStruct((T, w), dt)
    out_shape = [sh(D, F32), sh(D, BF16), sh(D, BF16), sh(D, BF16), sh(D, BF16), sh(D, BF16), sh(D, BF16),
                 sh(A_W, BF16), sh(C_W, BF16), sh(DR, F32), sh(ATT, F32), sh(ATT, F32),
                 sh(DR, BF16), sh(ATT, BF16),
                 jax.ShapeDtypeStruct((8, LANES), F32), jax.ShapeDtypeStruct((1, D), F32),
                 jax.ShapeDtypeStruct((1, D), F32)]
    out_specs = [tok(D), tok(D), tok(D), tok(D), tok(D), tok(D), tok(D), col(DR, 1), tok(C_W), tok(DR),
                 tok(ATT), tok(ATT), tok(DR), tok(ATT),
                 pl.BlockSpec((8, LANES), lambda i: (0, 0)), vec(), vec()]
    return pl.pallas_call(
        body, grid=(nt,), name="tail_fwd_bwd",
        in_specs=in_specs, out_specs=out_specs, out_shape=out_shape,
        scratch_shapes=[pltpu.VMEM((DR, D), BF16), pltpu.VMEM((D, ATT), BF16), pltpu.VMEM((D, D), BF16),
                        pltpu.VMEM((D, D), BF16), pltpu.VMEM((D, PLE), BF16), pltpu.SemaphoreType.DMA((5,))],
        compiler_params=_cp(("arbitrary",), VMEM_BIG),
    )(*_hbm(x, h, proj, att, proj, proj, proj, proj, proj, p, tgt, norm_ple, b_pg, w_o_rnn, w_o_att_t, w_out, w_pg,
            w_ple_t))


def _rope_tables():
    pos = jnp.arange(S, dtype=F32)
    inv_freq = ROPE_THETA ** (-jnp.arange(0, HD, 2, dtype=F32) / HD)
    ang = pos[:, None] * inv_freq[None, :]
    cos, sin = jnp.cos(ang), jnp.sin(ang)
    return jnp.concatenate([cos, cos], axis=1), jnp.concatenate([-sin, sin], axis=1)


def _local_step(x, p, tgt, project, other_weights, norm_mix, conv_b,
                w_rg_a, b_rg_a, w_rg_x, b_rg_x, lam, q_norm, k_norm, norm_ple, b_pg, start_reduce=None,
                entry_token=None):
    if start_reduce is None:
        start_reduce = lambda arrs, tag: (jnp.zeros((8, LANES), F32), arrs)
    if entry_token is None:
        entry_token = jnp.zeros((8, LANES), F32)
    cos_t, sin_t = _rope_tables()
    wa_b = w_rg_a.astype(BF16)
    wx_b = w_rg_x.astype(BF16)

    hn = _rmsnorm_fwd(x, norm_mix, entry_token)
    proj, w_bufs, chips, conv_w, token = project(hn)
    proj3 = proj.reshape(BL, S, NIN)
    h3, gates = _rnn_fwd(proj3, conv_w, conv_b, wa_b, b_rg_a, wx_b, b_rg_x, lam, token)
    att3, lse, wts, *qkv_p = _attn_fwd(proj3, cos_t, sin_t, q_norm, k_norm)
    w_o_rnn, w_o_att_t, w_out, w_pg, w_ple_t = other_weights(att3)
    (dx1, merged, n1, dpre, dpe, dyr, dya, slab_a, slab_c, dh, datt, sbar, yrnn, yatt, loss8, dnp, dbpg) = _tail(
        x, proj, h3.reshape(T, DR), att3.reshape(T, ATT), p, tgt, w_o_rnn, w_o_att_t, w_out, w_pg, w_ple_t,
        norm_ple, b_pg)

    token, pending_out = start_reduce([
        _mm_tn(yrnn, dyr, 640, 2048, "dw_o_rnn"),
        _mm_tn(dya, yatt, 512, 2048, "dw_o_att_t"),
        _mm_tn(merged, dx1, 512, 2048, "dw_out"),
        _mm_tn(n1, dpre, 512, 2048, "dw_ple_gate"),
        _mm_tn(dpe, p, 512, 2048, "dw_ple_t")], "out")

    slab_a3, dcw, dcb, dwa, dba, dwx, dbx, dlam = _rnn_bwd(
        proj3, h3, dh.reshape(BL, S, DR), gates, slab_a.reshape(BL, S, A_W), conv_w, conv_b, wa_b, b_rg_a, wx_b, b_rg_x, lam,
        token)
    datt3 = datt.reshape(BL, S, ATT)
    sbar3 = sbar.reshape(BL, S, ATT)
    slabs = None
    dqn = []
    dkn = []
    for g in range(NG):
        dq, dk, dv, dqn_g, dkn_g = _attn_bwd_group(g, proj3, cos_t, sin_t, q_norm[g:g + 1], k_norm[g:g + 1],
                                                   lse, wts, qkv_p, datt3, sbar3, slabs)
        slabs = (dq, dk, dv)
        dqn.append(dqn_g)
        dkn.append(dkn_g)
    pieces = [slab_a3.reshape(T, A_W)] + [t.reshape(T, GW) for t in slabs] + [slab_c]
    dw_in_t, db_in = _dw_in(pieces, hn)
    token, pending_in = start_reduce([dw_in_t], "in")
    grad_x, dnm = _grad_x(pieces, w_bufs, chips, token, x, dx1, norm_mix)

    small = dict(w_rg_a=dwa, w_rg_x=dwx, norm_mix=dnm, b_in=db_in, conv_b=dcb, b_rg_a=dba, b_rg_x=dbx,
                 lru_lambda=dlam, q_norm=dqn, k_norm=dkn, norm_ple=dnp, b_ple_gate=dbpg, conv_w=dcw, loss=loss8)
    return grad_x, pending_out, pending_in, small


MESH = pl.DeviceIdType.MESH
HBM_SPEC = pl.BlockSpec(memory_space=pl.ANY)


def _my_pos():
    return lax.axis_index("x"), lax.axis_index("y"), lax.axis_index("c")


def _flip(pos, k):
    x, y, c = pos
    return (1 - x if k & 4 else x, 1 - y if k & 2 else y, 1 - c if k & 1 else c)


def _lin(pos):
    return 4 * pos[0] + 2 * pos[1] + pos[2]


def _chip(pos):
    return 2 * pos[0] + pos[1]


def _exchange_within_chip(parts, name):
    na = len(parts)

    def body(*refs):
        a_refs = refs[:na]
        recv_refs = refs[na:2 * na]
        send_sems, recv_sems = refs[2 * na:]
        me = _my_pos()
        c = me[2]
        sibling = _flip(me, 1)
        remote = []
        for i in range(na):
            for q in range(NCHIP):
                rc = pltpu.make_async_remote_copy(
                    src_ref=a_refs[i].at[q, 1 - c], dst_ref=recv_refs[i].at[q],
                    send_sem=send_sems.at[NCHIP * i + q], recv_sem=recv_sems.at[NCHIP * i + q],
                    device_id=sibling, device_id_type=MESH)
                rc.start()
                remote.append(rc)
        for rc in remote:
            rc.wait_recv()
        for rc in remote:
            rc.wait_send()

    return pl.pallas_call(
        body, name=name, out_shape=[jax.ShapeDtypeStruct((NCHIP,) + a.shape[2:], a.dtype) for a in parts],
        in_specs=[HBM_SPEC] * na, out_specs=[HBM_SPEC] * na,
        scratch_shapes=[pltpu.SemaphoreType.DMA((NCHIP * na,)), pltpu.SemaphoreType.DMA((NCHIP * na,))],
    )(*parts)


HBM_ONLY = pl.BlockSpec(memory_space=pltpu.HBM)
SEM_SPEC = pl.BlockSpec(memory_space=pltpu.SEMAPHORE)
SPLIT_COPY = pltpu.CompilerParams(has_side_effects=pltpu.SideEffectType.DATAFLOW_SIDE_EFFECTING)


def _chip_peers(me):
    return [_flip(me, 4), _flip(me, 2), _flip(me, 6)]


def _between_chips_start(parts, name):
    na = len(parts)

    def body(*refs):
        a_refs = refs[:na]
        land_refs = refs[na:2 * na]
        send_sems, recv_sems = refs[2 * na], refs[2 * na + 1]
        token = refs[-1]
        me = _my_pos()
        myq = _chip(me)
        for i in range(na):
            for j, peer in enumerate(_chip_peers(me)):
                pltpu.make_async_remote_copy(
                    src_ref=a_refs[i].at[_chip(peer)], dst_ref=land_refs[i].at[myq],
                    send_sem=send_sems.at[3 * i + j], recv_sem=recv_sems.at[3 * i + j],
                    device_id=peer, device_id_type=MESH).start()
        token[...] = jnp.zeros_like(token)

    hbm = [pltpu.HBM(a.shape, a.dtype) for a in parts]
    srcs = [pltpu.with_memory_space_constraint(a, pltpu.HBM) for a in parts]
    lands = [pltpu.with_memory_space_constraint(lax.empty(a.shape, a.dtype), pltpu.HBM) for a in parts]
    res = pl.pallas_call(
        body, name=name,
        out_shape=(pltpu.SemaphoreType.DMA((3 * na,)), pltpu.SemaphoreType.DMA((3 * na,)), *hbm, *hbm,
                   jax.ShapeDtypeStruct((8, LANES), F32)),
        in_specs=[HBM_ONLY] * (2 * na),
        out_specs=(SEM_SPEC, SEM_SPEC, *([HBM_ONLY] * (2 * na)), pl.BlockSpec(memory_space=pltpu.VMEM)),
        input_output_aliases={i: 2 + i for i in range(2 * na)},
        compiler_params=SPLIT_COPY,
    )(*srcs, *lands)
    return res[-1], (res[0], res[1], list(res[2:2 + na]), list(res[2 + na:2 + 2 * na]))


def _between_chips_wait(pending, after, name):
    send_sems, recv_sems, parts, lands = pending
    na = len(parts)

    def body(*refs):
        a_refs = refs[:na]
        land_refs = refs[na:2 * na]
        send_sems, recv_sems = refs[2 * na], refs[2 * na + 1]
        me = _my_pos()
        for i in range(na):
            for j, peer in enumerate(_chip_peers(me)):
                cp = pltpu.make_async_remote_copy(
                    src_ref=a_refs[i].at[_chip(peer)], dst_ref=land_refs[i].at[_chip(peer)],
                    send_sem=send_sems.at[3 * i + j], recv_sem=recv_sems.at[3 * i + j],
                    device_id=peer, device_id_type=MESH)
                cp.wait_send()
                cp.wait_recv()

    hbm = [pltpu.HBM(a.shape, a.dtype) for a in parts]
    res = pl.pallas_call(
        body, name=name, out_shape=(*hbm, *hbm),
        in_specs=[HBM_ONLY] * (2 * na) + [SEM_SPEC, SEM_SPEC, pl.BlockSpec(memory_space=pl.ANY)],
        out_specs=[HBM_ONLY] * (2 * na),
        input_output_aliases={i: i for i in range(2 * na)},
        compiler_params=SPLIT_COPY,
    )(*parts, *lands, send_sems, recv_sems, after)
    return list(res[:na]), list(res[na:])


def _remote(src, dst, send_sems, recv_sems, idx, peer):
    return pltpu.make_async_remote_copy(src_ref=src, dst_ref=dst, send_sem=send_sems.at[idx],
                                        recv_sem=recv_sems.at[idx], device_id=peer, device_id_type=MESH)


def _copies_own(bufs, me):
    return [(bufs[0], bufs[1].at[me[2]], 0, _flip(me, 1))]


def _copies_near(bufs, me):
    return [(bufs[0], bufs[1].at[0, me[2]], 0, _flip(me, 2)), (bufs[0], bufs[1].at[1, me[2]], 1, _flip(me, 4))]


def _copies_far(bufs, me):
    return [(bufs[0], bufs[1].at[me[2]], 0, _flip(me, 6))]


def _copies_others(bufs, me):
    na = len(bufs) // 2
    return [(bufs[i], bufs[na + i].at[_lin(me)], 7 * i + k - 1, _flip(me, k))
            for i in range(na) for k in range(1, NDEV)]


def _copies_exchange(bufs, me):
    na = len(bufs) // 2
    return [(bufs[i].at[_lin(_flip(me, k))], bufs[na + i].at[_lin(me)], 7 * i + k - 1, _flip(me, k))
            for i in range(na) for k in range(1, NDEV)]


GROUP_COPIES = dict(own=_copies_own, near=_copies_near, far=_copies_far, others=_copies_others,
                    exchange=_copies_exchange)
GROUP_COUNT = dict(own=1, near=2, far=1)
TO_ALL = ("others", "exchange")


def _gather_start(bufs, groups, after, name):
    nb = len(bufs)
    ng = len(groups)

    def body(*refs):
        b = refs[:nb]
        sems = refs[nb + 1:nb + 1 + 2 * ng]
        token = refs[-1]
        me = _my_pos()
        for gi, (group, idx) in enumerate(groups):
            for src, dst, k, peer in GROUP_COPIES[group]([b[i] for i in idx], me):
                _remote(src, dst, sems[2 * gi], sems[2 * gi + 1], k, peer).start()
        token[...] = jnp.zeros_like(token)

    sem_t = []
    for group, idx in groups:
        cnt = 7 * (len(idx) // 2) if group in TO_ALL else GROUP_COUNT[group]
        sem_t += [pltpu.SemaphoreType.DMA((cnt,)), pltpu.SemaphoreType.DMA((cnt,))]
    ins = [pltpu.with_memory_space_constraint(a, pltpu.HBM) for a in bufs]
    res = pl.pallas_call(
        body, name=name,
        out_shape=(*sem_t, *[pltpu.HBM(a.shape, a.dtype) for a in bufs], jax.ShapeDtypeStruct((8, LANES), F32)),
        in_specs=[HBM_ONLY] * nb + [pl.BlockSpec(memory_space=pl.ANY)],
        out_specs=(*([SEM_SPEC] * (2 * ng)), *([HBM_ONLY] * nb), pl.BlockSpec(memory_space=pltpu.VMEM)),
        input_output_aliases={i: 2 * ng + i for i in range(nb)},
        compiler_params=SPLIT_COPY,
    )(*ins, after)
    return res[-1], list(res[2 * ng:2 * ng + nb]), [(res[2 * gi], res[2 * gi + 1]) for gi in range(ng)]


def _gather_wait(group, send_sems, recv_sems, bufs, after, name):
    nb = len(bufs)
    copies = GROUP_COPIES[group]

    def body(*refs):
        b = refs[:nb]
        ss, rs = refs[nb], refs[nb + 1]
        me = _my_pos()
        for src, dst, idx, peer in copies(b, me):
            if group in TO_ALL:
                landed = b[nb // 2 + idx // 7].at[_lin(peer)]
            elif group == "own":
                landed = b[1].at[1 - me[2]]
            else:
                landed = dst
            cp = _remote(src, landed, ss, rs, idx, peer)
            cp.wait_send()
            cp.wait_recv()

    res = pl.pallas_call(
        body, name=name, out_shape=[pltpu.HBM(a.shape, a.dtype) for a in bufs],
        in_specs=[HBM_ONLY] * nb + [SEM_SPEC, SEM_SPEC, pl.BlockSpec(memory_space=pl.ANY)],
        out_specs=[HBM_ONLY] * nb,
        input_output_aliases={i: i for i in range(nb)},
        compiler_params=SPLIT_COPY,
    )(*bufs, send_sems, recv_sems, after)
    return list(res)


def _forward_to_sibling(buf, name):
    n = buf.shape[0]

    def body(_in_ref, out_ref, send_sems, recv_sems):
        me = _my_pos()
        c = me[2]
        sibling = _flip(me, 1)
        sends = []
        for r in range(n):
            cp = _remote(out_ref.at[r, c], out_ref.at[r, c], send_sems, recv_sems, r, sibling)
            cp.start()
            sends.append(cp)
        for r in range(n):
            _remote(out_ref.at[r, c], out_ref.at[r, 1 - c], send_sems, recv_sems, r, sibling).wait_recv()
        for cp in sends:
            cp.wait_send()

    return pl.pallas_call(
        body, name=name, out_shape=jax.ShapeDtypeStruct(buf.shape, buf.dtype),
        in_specs=[HBM_SPEC], out_specs=HBM_SPEC,
        scratch_shapes=[pltpu.SemaphoreType.DMA((n,)), pltpu.SemaphoreType.DMA((n,))],
        input_output_aliases={0: 0},
    )(buf)


def _scalar(v):
    return jnp.asarray(v, jnp.int32).reshape(1)


def _sum_pairs(parts, theirs, name):
    na = len(parts)

    def body(c_ref, *refs):
        for i in range(na):
            o_ref = refs[2 * na + i]
            o_ref[0] = (refs[i][0, 0].astype(F32) + refs[na + i][0].astype(F32)).astype(o_ref.dtype)

    def mine_spec(a):
        return pl.BlockSpec((1, 1) + a.shape[2:], lambda q, c_ref: (q, c_ref[0], 0, 0))

    def spec(a):
        return pl.BlockSpec((1,) + a.shape[1:], lambda q, c_ref: (q, 0, 0))

    return pl.pallas_call(
        body, name=name,
        grid_spec=pltpu.PrefetchScalarGridSpec(
            num_scalar_prefetch=1, grid=(NCHIP,),
            in_specs=[mine_spec(a) for a in parts] + [spec(a) for a in theirs],
            out_specs=[spec(a) for a in theirs]),
        out_shape=[jax.ShapeDtypeStruct(a.shape, a.dtype) for a in theirs],
        compiler_params=_cp(("arbitrary",), VMEM_BIG),
    )(_scalar(lax.axis_index("c")), *_hbm(*parts, *theirs))


def _others(q, mine, nblk=NCHIP):
    return jnp.where(q == mine, (q + 1) % nblk, q)


def _sum_chips_adamw(own, recv, wv, mv, vv, token, tr, name):
    _, r, w = recv.shape

    def body(q_ref, own_ref, r0, r1, r2, r3, w_ref, m_ref, v_ref, _token, g_ref, d_ref, m2_ref, v2_ref):
        myq = q_ref[0]
        acc = None
        for q, r_ref in enumerate((r0, r1, r2, r3)):
            term = jnp.where(myq == q, own_ref[0], r_ref[0]).astype(F32)
            acc = term if acc is None else acc + term
        g_ref[...] = acc
        delta, m2, v2 = _adam_math(w_ref[...], acc, m_ref[...], v_ref[...])
        d_ref[...] = delta
        m2_ref[...] = m2
        v2_ref[...] = v2

    def recv_spec(q):
        return pl.BlockSpec((1, tr, w), lambda i, q_ref: (_others(q, q_ref[0]), i, 0))

    rows = lambda: pl.BlockSpec((tr, w), lambda i, q_ref: (i, 0))
    shp = jax.ShapeDtypeStruct((r, w), F32)
    return pl.pallas_call(
        body, name=name,
        grid_spec=pltpu.PrefetchScalarGridSpec(
            num_scalar_prefetch=1, grid=(r // tr,),
            in_specs=[pl.BlockSpec((1, tr, w), lambda i, q_ref: (q_ref[0], i, 0))]
            + [recv_spec(q) for q in range(NCHIP)] + [rows(), rows(), rows()]
            + [pl.BlockSpec((8, LANES), lambda i, q_ref: (0, 0))],
            out_specs=[rows(), rows(), rows(), rows()]),
        out_shape=[shp, shp, shp, shp],
        compiler_params=_cp(("arbitrary",), VMEM_MID),
    )(_scalar(_chip(_my_pos())), *_hbm(own, recv, recv, recv, recv, wv, mv, vv, token))


def _sum_blocks_small(own, recv, mine, transpose, name):
    na = len(recv)
    nblk = recv[0].shape[0]

    def body(q_ref, *refs):
        me = q_ref[0]
        for i in range(na):
            acc = None
            for q in range(nblk):
                term = jnp.where(me == q, refs[i][0], refs[na * (1 + q) + i][0]).astype(F32)
                acc = term if acc is None else acc + term
            refs[na * (1 + nblk) + i][...] = acc.T if transpose[i] else acc

    def oshape(a, tr):
        r, w = a.shape[1:]
        return (w, r) if tr else (r, w)

    own_spec = lambda a: pl.BlockSpec((1,) + a.shape[1:], lambda s, q_ref: (q_ref[0], 0, 0))
    recv_spec = lambda a, q: pl.BlockSpec((1,) + a.shape[1:], lambda s, q_ref: (_others(q, q_ref[0], nblk), 0, 0))
    out_spec = lambda shp: pl.BlockSpec(shp, lambda s, q_ref: (0, 0))
    in_specs = [own_spec(a) for a in own]
    for q in range(nblk):
        in_specs += [recv_spec(a, q) for a in recv]
    return pl.pallas_call(
        body, name=name,
        grid_spec=pltpu.PrefetchScalarGridSpec(
            num_scalar_prefetch=1, grid=(1,), in_specs=in_specs,
            out_specs=[out_spec(oshape(a, tr)) for a, tr in zip(recv, transpose)]),
        out_shape=[jax.ShapeDtypeStruct(oshape(a, tr), F32) for a, tr in zip(recv, transpose)],
        compiler_params=_cp(("arbitrary",), VMEM_MID),
    )(_scalar(mine), *_hbm(*own, *(list(recv) * nblk)))


def _rep_offsets():
    offs = []
    o = 0
    for r in REP_ROWS:
        offs.append(o)
        o += r
    return offs


LOSS_ROW = REP_TOTAL_ROWS


def _pack_small_grads(g):
    offs = _rep_offsets()

    def body(dwa, dwx, dnm, dbin, dcb, dba, dbx, dlam, dq0, dq1, dq2, dk0, dk1, dk2, dnp, dbpg, loss, o_ref):
        o_ref[pl.ds(REP_TOTAL_ROWS - 2, NDEV * REP_ROWS_DEV - REP_TOTAL_ROWS + 2), :] = jnp.zeros(
            (NDEV * REP_ROWS_DEV - REP_TOTAL_ROWS + 2, LANES), F32)
        o_ref[pl.ds(LOSS_ROW, 1), :] = loss[0:1, :]
        for n in range(NRB):
            o_ref[pl.ds(offs[0] + n * RBW, RBW), :] = dwa[n]
            o_ref[pl.ds(offs[1] + n * RBW, RBW), :] = dwx[n]

        def put_vec(off, ref, rows):
            for k in range(rows):
                o_ref[pl.ds(off + k, 1), :] = ref[:, k * LANES:(k + 1) * LANES]

        put_vec(offs[2], dnm, REP_ROWS[2])
        put_vec(offs[3], dbin, REP_ROWS[3])
        put_vec(offs[4], dcb, REP_ROWS[4])
        put_vec(offs[5], dba, REP_ROWS[5])
        put_vec(offs[6], dbx, REP_ROWS[6])
        put_vec(offs[7], dlam, REP_ROWS[7])
        for k, ref in enumerate((dq0, dq1, dq2)):
            o_ref[pl.ds(offs[8] + k, 1), :] = ref[...]
        for k, ref in enumerate((dk0, dk1, dk2)):
            o_ref[pl.ds(offs[9] + k, 1), :] = ref[...]
        put_vec(offs[10], dnp, REP_ROWS[10])
        put_vec(offs[11], dbpg, REP_ROWS[11])

    args = [g["w_rg_a"], g["w_rg_x"], g["norm_mix"], g["b_in"], g["conv_b"], g["b_rg_a"], g["b_rg_x"],
            g["lru_lambda"], *g["q_norm"], *g["k_norm"], g["norm_ple"], g["b_ple_gate"], g["loss"]]
    full = lambda shp: pl.BlockSpec(shp, lambda: (0,) * len(shp))
    return pl.pallas_call(
        body, name="pack_small_grads",
        in_specs=[full(a.shape) for a in args],
        out_specs=full((NDEV * REP_ROWS_DEV, LANES)),
        out_shape=jax.ShapeDtypeStruct((NDEV * REP_ROWS_DEV, LANES), F32),
    )(*_hbm(*args))


def _adam_math(wv, gv, mv, vv):
    c1 = 1.0 - B1 ** STEP
    c2 = 1.0 - B2 ** STEP
    m2 = B1 * mv + (1.0 - B1) * gv
    v2 = B2 * vv + (1.0 - B2) * (gv * gv)
    delta = (-LR) * ((m2 / c1) / (jnp.sqrt(v2 / c2) + AEPS) + WD * wv)
    return delta, m2, v2


def _adamw_small(rep_flat, w, m, v):
    offs = _rep_offsets()
    n = len(REP_NAMES)

    def body(*refs):
        g_ref = refs[0]
        w_refs = refs[1:1 + n]
        m_refs = refs[1 + n:1 + 2 * n]
        v_refs = refs[1 + 2 * n:1 + 3 * n]
        outs = refs[1 + 3 * n:]
        go, do, mo, vo = outs[:n], outs[n:2 * n], outs[2 * n:3 * n], outs[3 * n:]

        def emit(i, idx, gv):
            go[i][idx] = gv
            delta, m2, v2 = _adam_math(w_refs[i][idx], gv, m_refs[i][idx], v_refs[i][idx])
            do[i][idx] = delta
            mo[i][idx] = m2
            vo[i][idx] = v2

        for i in range(n):
            if i < 2:
                for b in range(NRB):
                    emit(i, b, g_ref[pl.ds(offs[i] + b * RBW, RBW), :])
            elif REP_NAMES[i] in ("q_norm", "k_norm"):
                emit(i, slice(None), g_ref[pl.ds(offs[i], NG), :])
            else:
                gv = jnp.concatenate([g_ref[pl.ds(offs[i] + k, 1), :] for k in range(REP_ROWS[i])], axis=1)
                emit(i, slice(None), gv)

    full = lambda shp: pl.BlockSpec(shp, lambda: (0,) * len(shp))
    pspecs = [full(a.shape) for a in w]
    pshapes = [jax.ShapeDtypeStruct(a.shape, F32) for a in w]
    res = pl.pallas_call(
        body, name="adamw_small",
        in_specs=[full(rep_flat.shape)] + pspecs * 3,
        out_specs=pspecs * 4, out_shape=pshapes * 4,
        compiler_params=_cp(None, VMEM_MID),
    )(*_hbm(rep_flat, *w, *m, *v))
    return res[:n], res[n:2 * n], res[2 * n:3 * n], res[3 * n:]


def _adamw_many(w, g, m, v, token):
    n = len(w)

    def body(*refs):
        for i in range(n):
            delta, m2, v2 = _adam_math(refs[i][...], refs[n + i][...], refs[2 * n + i][...], refs[3 * n + i][...])
            refs[4 * n + 1 + i][...] = delta
            refs[5 * n + 1 + i][...] = m2
            refs[6 * n + 1 + i][...] = v2

    full = lambda shp: pl.BlockSpec(shp, lambda: (0,) * len(shp))
    specs = [full(a.shape) for a in w]
    shapes = [jax.ShapeDtypeStruct(a.shape, F32) for a in w]
    res = pl.pallas_call(
        body, name="adamw_shards",
        in_specs=specs * 4 + [full((8, LANES))], out_specs=specs * 3, out_shape=shapes * 3,
        compiler_params=_cp(None, VMEM_MID),
    )(*_hbm(*w, *g, *m, *v, token))
    return res[:n], res[n:2 * n], res[2 * n:]


def kernel(x, p, norm_mix, w_in, b_in, conv_w, conv_b, w_rg_a, b_rg_a, w_rg_x, b_rg_x, lru_lambda, q_norm, k_norm, w_o_rnn, w_o_att, w_out, norm_ple, w_ple_gate, b_ple_gate, w_ple, loss_target, m_norm_mix, m_w_in, m_b_in, m_conv_w, m_conv_b, m_w_rg_a, m_b_rg_a, m_w_rg_x, m_b_rg_x, m_lru_lambda, m_q_norm, m_k_norm, m_w_o_rnn, m_w_o_att, m_w_out, m_norm_ple, m_w_ple_gate, m_b_ple_gate, m_w_ple, v_norm_mix, v_w_in, v_b_in, v_conv_w, v_conv_b, v_w_rg_a, v_b_rg_a, v_w_rg_x, v_b_rg_x, v_lru_lambda, v_q_norm, v_k_norm, v_w_o_rnn, v_w_o_att, v_w_out, v_norm_ple, v_w_ple_gate, v_b_ple_gate, v_w_ple):
    w = dict(norm_mix=norm_mix, w_in=w_in, b_in=b_in, conv_w=conv_w, conv_b=conv_b, w_rg_a=w_rg_a, b_rg_a=b_rg_a,
             w_rg_x=w_rg_x, b_rg_x=b_rg_x, lru_lambda=lru_lambda, q_norm=q_norm, k_norm=k_norm, w_o_rnn=w_o_rnn,
             w_o_att=w_o_att, w_out=w_out, norm_ple=norm_ple, w_ple_gate=w_ple_gate, b_ple_gate=b_ple_gate,
             w_ple=w_ple)
    m = dict(norm_mix=m_norm_mix, w_in=m_w_in, b_in=m_b_in, conv_w=m_conv_w, conv_b=m_conv_b, w_rg_a=m_w_rg_a,
             b_rg_a=m_b_rg_a, w_rg_x=m_w_rg_x, b_rg_x=m_b_rg_x, lru_lambda=m_lru_lambda, q_norm=m_q_norm,
             k_norm=m_k_norm, w_o_rnn=m_w_o_rnn, w_o_att=m_w_o_att, w_out=m_w_out, norm_ple=m_norm_ple,
             w_ple_gate=m_w_ple_gate, b_ple_gate=m_b_ple_gate, w_ple=m_w_ple)
    v = dict(norm_mix=v_norm_mix, w_in=v_w_in, b_in=v_b_in, conv_w=v_conv_w, conv_b=v_conv_b, w_rg_a=v_w_rg_a,
             b_rg_a=v_b_rg_a, w_rg_x=v_w_rg_x, b_rg_x=v_b_rg_x, lru_lambda=v_lru_lambda, q_norm=v_q_norm,
             k_norm=v_k_norm, w_o_rnn=v_w_o_rnn, w_o_att=v_w_o_att, w_out=v_w_out, norm_ple=v_norm_ple,
             w_ple_gate=v_w_ple_gate, b_ple_gate=v_b_ple_gate, w_ple=v_w_ple)
    names = list(w.keys())

    shards = [w_in[0].T.astype(BF16), w_o_rnn[0].astype(BF16), w_o_att[0].T.astype(BF16), w_out[0].astype(BF16),
              w_ple_gate[0].astype(BF16), w_ple[0].T.astype(BF16), conv_w[0]]
    pos = _my_pos()
    me, my_core, my_chip = _lin(pos), pos[2], _chip(pos)
    hbm_empty = lambda shp, dt: lax.empty(shp, dt)
    w_shard, conv_shard = shards[0], shards[6]
    shp = w_shard.shape
    entry_token, bufs, sems = _gather_start(
        [w_shard, hbm_empty((2,) + shp, BF16), hbm_empty((2, 2) + shp, BF16), conv_shard,
         hbm_empty((NDEV,) + conv_shard.shape, F32)],
        [("own", (0, 1)), ("near", (0, 2)), ("others", (3, 4))], norm_mix, "gather_start_near")
    w_src, own_l, near_l, conv_src, conv_l = bufs
    sem_own, sem_near, sem_conv = sems
    gather_out = {}

    def project(hn):
        w_thru, own = _gather_wait("own", *sem_own, [w_src, own_l], hn, "gather_wait_own")
        own = lax.dynamic_update_slice(own, w_shard[None], (my_core, 0, 0)).reshape(1, CHIP_COLS, D)
        chips = [jnp.stack([my_chip]), jnp.stack([my_chip ^ 1, my_chip ^ 2]), jnp.stack([my_chip ^ 3])]
        chips = [c.astype(jnp.int32) for c in chips]
        proj = _in_proj_chips(hn, own, b_in, chips[0], None, entry_token, "in_proj_own")
        w_thru, near = _gather_wait("near", *sem_near, [w_thru, near_l], proj, "gather_wait_near")
        near = _forward_to_sibling(near, "gather_forward_near")
        token, (w_thru, far_l), (sem_far,) = _gather_start(
            [w_thru, hbm_empty((2,) + shp, BF16)], [("far", (0, 1))], near, "gather_start_far")
        near = near.reshape(2, CHIP_COLS, D)
        proj = _in_proj_chips(hn, near, b_in, chips[1], proj, token, "in_proj_near")
        w_thru, far = _gather_wait("far", *sem_far, [w_thru, far_l], proj, "gather_wait_far")
        far = _forward_to_sibling(far[None], "gather_forward_far").reshape(1, CHIP_COLS, D)
        proj = _in_proj_chips(hn, far, b_in, chips[2], proj, token, "in_proj_far")
        conv_thru, conv_g = _gather_wait("others", *sem_conv, [conv_src, conv_l], proj, "gather_wait_conv")
        conv_g = lax.dynamic_update_slice(conv_g, conv_shard[None], (me, 0, 0))
        conv_f = conv_g.transpose(1, 0, 2).reshape(CONVW, DR)
        srcs = list(shards[1:6])
        token, obufs, (sem_out,) = _gather_start(
            srcs + [hbm_empty((NDEV,) + a.shape, BF16) for a in srcs], [("others", tuple(range(10)))], proj,
            "gather_start_out")
        gather_out.update(bufs=obufs, sems=sem_out)
        return proj, [own, near, far], jnp.concatenate(chips), conv_f, token

    def other_weights(after):
        obufs = _gather_wait("others", *gather_out["sems"], gather_out["bufs"], after, "gather_wait_out")
        full = [lax.dynamic_update_slice(a, s[None], (me, 0, 0)) for a, s in zip(obufs[5:], shards[1:6])]
        return [a.reshape((NDEV * a.shape[1], a.shape[2])) for a in full]

    def start_reduce(arrs, tag):
        if tag == "out":
            parts = [a.reshape((NDEV, a.shape[0] // NDEV, a.shape[1])) for a in arrs]
            token, bufs, (sems,) = _gather_start(
                parts + [lax.empty(a.shape, a.dtype) for a in parts], [("exchange", tuple(range(2 * len(parts))))],
                arrs[-1][:SUBLANES], "reduce_out_start")
            return token, (bufs, sems)
        parts = [a.reshape((NCHIP, 2, a.shape[0] // NDEV, a.shape[1])) for a in arrs]
        theirs = _exchange_within_chip(parts, "reduce_within_chip_" + tag)
        return _between_chips_start(_sum_pairs(parts, theirs, "sum_pairs_" + tag), "reduce_between_chips_start_" + tag)

    grad_x, pending_out, pending_in, small = _local_step(
        x.reshape(T, D), p.reshape(T, PLE), loss_target.reshape(T, D),
        project, other_weights,
        norm_mix, conv_b, w_rg_a[0], b_rg_a, w_rg_x[0], b_rg_x, lru_lambda, q_norm[0], k_norm[0],
        norm_ple, b_ple_gate, start_reduce, entry_token)

    rep_parts = _pack_small_grads(small).reshape(NDEV, REP_ROWS_DEV, LANES)
    conv_parts = small["conv_w"].reshape(CONVW, NDEV, DR // NDEV).transpose(1, 0, 2)
    smalls = [rep_parts, conv_parts]
    token, sbufs, (sem_x,) = _gather_start(
        smalls + [lax.empty(a.shape, F32) for a in smalls], [("exchange", (0, 1, 2, 3))], small["norm_mix"],
        "reduce_small_start")

    own_in, recv_in = _between_chips_wait(pending_in, token, "reduce_between_chips_wait_in")
    w_in_res = _sum_chips_adamw(own_in[0], recv_in[0], w_in[0].T, m_w_in[0].T, v_w_in[0].T, token, 304, "adamw_w_in")
    sbufs = _gather_wait("exchange", *sem_x, sbufs, w_in_res[0], "reduce_small_wait")
    g_rep, g_conv = _sum_blocks_small(sbufs[:2], sbufs[2:], me, (False, False), "sum_small")
    token, gbufs, (sem_g,) = _gather_start(
        [g_rep, lax.empty((NDEV,) + g_rep.shape, F32)], [("others", (0, 1))], g_conv, "gather_small_start")
    obufs = _gather_wait("exchange", *pending_out[1], pending_out[0], token, "reduce_out_wait")
    g_o_rnn, g_o_att, g_out, g_pg, g_ple = _sum_blocks_small(
        obufs[:5], obufs[5:], me, (False, True, False, False, True), "sum_out")

    grad, delta, new_m, new_v = {}, {}, {}, {}
    rest = ("w_o_rnn", "w_o_att", "w_out", "w_ple_gate", "w_ple", "conv_w")
    g_rest = [g_o_rnn, g_o_att, g_out, g_pg, g_ple, g_conv]
    rest_res = _adamw_many([w[n][0] for n in rest], g_rest, [m[n][0] for n in rest], [v[n][0] for n in rest], token)
    _, rep_all = _gather_wait("others", *sem_g, gbufs, rest_res[0][0], "gather_small_wait")
    rep_all = lax.dynamic_update_slice(rep_all, g_rep[None], (me, 0, 0)).reshape(NDEV * REP_ROWS_DEV, LANES)
    loss = rep_all[LOSS_ROW, 0]
    rep_shape = lambda a: a if a.ndim == 2 else a.reshape(a.shape[1:])
    res = _adamw_small(rep_all, [rep_shape(w[n]) for n in REP_NAMES], [rep_shape(m[n]) for n in REP_NAMES],
                       [rep_shape(v[n]) for n in REP_NAMES])
    for dst, vals in zip((grad, delta, new_m, new_v), res):
        for n, a in zip(REP_NAMES, vals):
            dst[n] = a.reshape(w[n].shape)
    grad["w_in"], delta["w_in"], new_m["w_in"], new_v["w_in"] = [a.T[None] for a in w_in_res]
    for n, a in zip(rest, g_rest):
        grad[n] = a[None]
    for dst, vals in zip((delta, new_m, new_v), rest_res):
        for n, a in zip(rest, vals):
            dst[n] = a[None]

    return (loss, grad_x.reshape(BL, S, D), *[grad[n] for n in names], *[delta[n] for n in names],
            *[new_m[n] for n in names], *[new_v[n] for n in names])
```

```python
import jax
import jax.numpy as jnp
from jax import lax
from jax.experimental import pallas as pl
from jax.experimental.pallas import tpu as pltpu

F32 = jnp.float32
BF16 = jnp.bfloat16

D = 1024
S = 2048
BL = 2
T = BL * S
NDEV = 8
NCHIP = 4
PLE = 256
DR = 1280
NRB = 10
RBW = 128
CONVW = 4
LRU_C = 8.0
HD = 128
NH = 4
PATTERNS = ((128, 1), (512, 4), (2048, 16))
NG = 3
ATT = NH * HD
GW = NG * ATT
NIN = 2 * DR + 3 * GW + ATT + 2 * D
OFF_ZR = DR
OFF_Q = 2 * DR
OFF_ZA = OFF_Q + 3 * GW
OFF_G = OFF_ZA + ATT
ROPE_THETA = 10000.0
EPS = 1e-6
SCALE = HD ** -0.5
NEG = -1e30
QB = 128
LANES = 128
CT = 512
NCT = NIN // CT
A_W = 2 * DR
C_W = ATT + 2 * D

LR, B1, B2, AEPS, WD, STEP = 0.001, 0.9, 0.999, 1e-08, 0.01, 10

NSHARD_IN = NIN // NDEV
REP_NAMES = ("w_rg_a", "w_rg_x", "norm_mix", "b_in", "conv_b", "b_rg_a", "b_rg_x", "lru_lambda", "q_norm",
             "k_norm", "norm_ple", "b_ple_gate")
REP_ROWS = (NRB * RBW, NRB * RBW, D // LANES, NIN // LANES, DR // LANES, DR // LANES, DR // LANES, DR // LANES,
            NG, NG, D // LANES, D // LANES)
REP_TOTAL_ROWS = sum(REP_ROWS)
REP_ROWS_DEV = 344
BIG_NAMES = ("w_in", "w_o_rnn", "w_o_att", "w_out", "w_ple_gate", "w_ple")

VMEM_BIG = 56 * 1024 * 1024
VMEM_MID = 40 * 1024 * 1024


def _cp(sem=None, vmem=None):
    return pltpu.CompilerParams(dimension_semantics=sem, vmem_limit_bytes=vmem)


def _hbm(*arrays):
    return [pltpu.with_memory_space_constraint(a, pltpu.HBM) for a in arrays]


def _copy_together(copies):
    for cp in copies:
        cp.start()
    for cp in copies:
        cp.wait()


def _dot(a, b):
    return jnp.dot(a, b, preferred_element_type=F32)


def _dot_nt(a, b):
    return lax.dot_general(a, b, (((1,), (1,)), ((), ())), preferred_element_type=F32)


def _dot_tn(a, b):
    return lax.dot_general(a, b, (((0,), (0,)), ((), ())), preferred_element_type=F32)


def _sigmoid(x):
    return jax.nn.sigmoid(x)


def _perm(j):
    jq = j - OFF_Q // CT
    inside = (j >= OFF_Q // CT) & (j < OFF_ZA // CT)
    return jnp.where(inside, OFF_Q // CT + (jq % 3) * 3 + jq // 3, j)


PIECES = ((0, A_W // CT), (OFF_Q // CT, GW // CT), (OFF_Q // CT + 3, GW // CT), (OFF_Q // CT + 6, GW // CT),
          (OFF_ZA // CT, C_W // CT))


def _rmsnorm_fwd(x, gain, token, tm=512):
    def body(x_ref, g_ref, _token, o_ref):
        xv = x_ref[...]
        var = jnp.mean(xv * xv, axis=-1, keepdims=True)
        o_ref[...] = (xv * lax.rsqrt(var + EPS) * g_ref[...]).astype(BF16)

    return pl.pallas_call(
        body, grid=(T // tm,), name="rmsnorm_fwd",
        in_specs=[pl.BlockSpec((tm, D), lambda i: (i, 0)), pl.BlockSpec((1, D), lambda i: (0, 0)),
                  pl.BlockSpec((8, LANES), lambda i: (0, 0))],
        out_specs=pl.BlockSpec((tm, D), lambda i: (i, 0)),
        out_shape=jax.ShapeDtypeStruct((T, D), BF16),
        compiler_params=_cp(("parallel",)),
    )(*_hbm(x, gain, token))


CHIP_COLS = NIN // NCHIP


def _in_proj_chips(hn, w_rows, bias, chips, proj, token, name, tm=1024):
    n = w_rows.shape[0]

    def body(chips_ref, a_ref, w_ref, b_ref, _token, *rest):
        o_ref = rest[-1]
        o_ref[...] = (_dot_nt(a_ref[...], w_ref[0]) + b_ref[...]).astype(BF16)

    in_specs = [pl.BlockSpec((tm, D), lambda s, i, ch: (i, 0)),
                pl.BlockSpec((1, CHIP_COLS, D), lambda s, i, ch: (s, 0, 0)),
                pl.BlockSpec((1, CHIP_COLS), lambda s, i, ch: (0, ch[s])),
                pl.BlockSpec((8, LANES), lambda s, i, ch: (0, 0))]
    args = [hn, w_rows, bias, token]
    aliases = {}
    if proj is not None:
        in_specs.append(pl.BlockSpec(memory_space=pl.ANY))
        args.append(proj)
        aliases = {5: 0}
    return pl.pallas_call(
        body, name=name,
        grid_spec=pltpu.PrefetchScalarGridSpec(
            num_scalar_prefetch=1, grid=(n, T // tm), in_specs=in_specs,
            out_specs=pl.BlockSpec((tm, CHIP_COLS), lambda s, i, ch: (i, ch[s]))),
        out_shape=jax.ShapeDtypeStruct((T, NIN), BF16),
        input_output_aliases=aliases,
        compiler_params=_cp(("arbitrary", "arbitrary"), VMEM_BIG),
    )(chips, *_hbm(*args))


def _grad_x(pieces, w_bufs, chips, token, x, dx1, gain, tm=512):
    nb = len(w_bufs)

    def body(chips_ref, a_ref, q_ref, k_ref, v_ref, c_ref, *rest):
        w_hbm = rest[:nb]
        x_ref, dx1_ref, g_ref, dx_ref, dg_ref, w, sems = rest[nb + 1:]
        first = pl.program_id(0) == 0

        @pl.when(first)
        def _():
            s = 0
            copies = []
            for buf in w_hbm:
                for r in range(buf.shape[0]):
                    row = pl.multiple_of(chips_ref[s] * CHIP_COLS, 128)
                    copies.append(pltpu.make_async_copy(buf.at[r], w.at[pl.ds(row, CHIP_COLS), :], sems.at[s]))
                    s += 1
            _copy_together(copies)

        acc = _dot(a_ref[...], w[pl.ds(0, A_W), :])
        for kind, p_ref in enumerate((q_ref, k_ref, v_ref)):
            for g in range(NG):
                row = OFF_Q + (3 * g + kind) * CT
                acc = acc + _dot(p_ref[:, g * CT:(g + 1) * CT], w[pl.ds(row, CT), :])
        dn = acc + _dot(c_ref[...], w[pl.ds(OFF_ZA, C_W), :])
        xv = x_ref[...]
        rstd = lax.rsqrt(jnp.mean(xv * xv, axis=-1, keepdims=True) + EPS)
        xh = xv * rstd
        dg = jnp.sum(dn * xh, axis=0, keepdims=True)
        gd = dn * g_ref[...]
        dx_ref[...] = dx1_ref[...] + rstd * (gd - xh * jnp.mean(gd * xh, axis=-1, keepdims=True))

        @pl.when(first)
        def _():
            dg_ref[...] = dg

        @pl.when(jnp.logical_not(first))
        def _():
            dg_ref[...] += dg

    tok = lambda wd: pl.BlockSpec((tm, wd), lambda i, ch: (i, 0))
    vec = lambda: pl.BlockSpec((1, D), lambda i, ch: (0, 0))
    return pl.pallas_call(
        body, name="grad_x",
        grid_spec=pltpu.PrefetchScalarGridSpec(
            num_scalar_prefetch=1, grid=(T // tm,),
            in_specs=[tok(A_W), tok(GW), tok(GW), tok(GW), tok(C_W)] + [pl.BlockSpec(memory_space=pl.ANY)] * nb
            + [pl.BlockSpec((8, LANES), lambda i, ch: (0, 0)), tok(D), tok(D), vec()],
            out_specs=[tok(D), vec()],
            scratch_shapes=[pltpu.VMEM((NIN, D), BF16), pltpu.SemaphoreType.DMA((NCHIP,))]),
        out_shape=[jax.ShapeDtypeStruct((T, D), F32), jax.ShapeDtypeStruct((1, D), F32)],
        compiler_params=_cp(("arbitrary",), VMEM_BIG),
    )(chips, *_hbm(*pieces, *w_bufs, token, x, dx1, gain))


def _dw_in(pieces, hn):
    def body(a_ref, q_ref, k_ref, v_ref, c_ref, h_hbm, o_ref, s_ref, h):
        j = pl.program_id(0)

        @pl.when(j == 0)
        def _():
            pltpu.sync_copy(h_hbm, h)

        def step(x_ref):
            xv = x_ref[...]
            o_ref[...] = _dot_tn(xv, h[...]).astype(BF16)
            s_ref[...] = jnp.sum(xv.astype(F32), axis=0, keepdims=True)

        for x_ref, (lo, n) in zip((a_ref, q_ref, k_ref, v_ref, c_ref), PIECES):
            pl.when((j >= lo) & (j < lo + n))(lambda x_ref=x_ref: step(x_ref))

    def piece_spec(lo, n):
        return pl.BlockSpec((T, CT), lambda j: (0, jnp.clip(j - lo, 0, n - 1)))

    return pl.pallas_call(
        body, grid=(NCT,), name="dw_in",
        in_specs=[piece_spec(lo, n) for lo, n in PIECES] + [pl.BlockSpec(memory_space=pl.ANY)],
        out_specs=[pl.BlockSpec((CT, D), lambda j: (_perm(j), 0)), pl.BlockSpec((1, CT), lambda j: (0, _perm(j)))],
        out_shape=[jax.ShapeDtypeStruct((NIN, D), BF16), jax.ShapeDtypeStruct((1, NIN), F32)],
        scratch_shapes=[pltpu.VMEM((T, D), BF16)],
        compiler_params=_cp(("arbitrary",), VMEM_BIG),
    )(*_hbm(*pieces, hn))


def _mm_tn(a, b, ta, tt, name):
    m = a.shape[1]
    n = b.shape[1]
    nt = T // tt

    def body(a_ref, b_ref, o_ref, acc):
        t = pl.program_id(1)
        p = _dot_tn(a_ref[...].astype(BF16), b_ref[...].astype(BF16))
        if nt == 1:
            o_ref[...] = p.astype(BF16)
            return

        @pl.when(t == 0)
        def _():
            acc[...] = p

        @pl.when(t > 0)
        def _():
            acc[...] += p

        @pl.when(t == nt - 1)
        def _():
            o_ref[...] = acc[...].astype(BF16)

    return pl.pallas_call(
        body, grid=(m // ta, nt), name=name,
        in_specs=[pl.BlockSpec((tt, ta), lambda j, t: (t, j)), pl.BlockSpec((tt, n), lambda j, t: (t, 0))],
        out_specs=pl.BlockSpec((ta, n), lambda j, t: (j, 0)),
        out_shape=jax.ShapeDtypeStruct((m, n), BF16),
        scratch_shapes=[pltpu.VMEM((ta, n), F32)],
        compiler_params=_cp(("parallel", "arbitrary"), VMEM_MID),
    )(*_hbm(a, b))


def _row_iota():
    return lax.broadcasted_iota(jnp.int32, (S, RBW), 0)


SUBLANES = 8
N_SHIFT_BUFS = 4


class _Shifter:
    def __init__(self, bufs):
        self.bufs = bufs
        self.k = 0

    def _store(self, v, fill, front):
        b = self.bufs.at[self.k % N_SHIFT_BUFS]
        self.k += 1
        b[pl.ds(0 if front else SUBLANES + S, SUBLANES), :] = jnp.full((SUBLANES, RBW), fill, F32)
        b[pl.ds(SUBLANES, S), :] = v
        return b

    def down(self, v, ds, fill):
        b = self._store(v, fill, True)
        return [b[pl.ds(SUBLANES - d, S), :] for d in ds]

    def up(self, v, ds, fill):
        b = self._store(v, fill, False)
        return [b[pl.ds(SUBLANES + d, S), :] for d in ds]


def _shift_down(v, d, sh, fill):
    return sh.down(v, (d,), fill)[0]


def _shift_up(v, d, sh, fill):
    return sh.up(v, (d,), fill)[0]


def _scan_down(a, u, row):
    d = 1
    while d < S:
        last = 2 * d >= S
        if d < SUBLANES:
            u = a * _shift_down(u, d, row, 0.0) + u
            if not last:
                a = a * _shift_down(a, d, row, 1.0)
        else:
            u = jnp.concatenate([u[:d], a[d:] * u[:S - d] + u[d:]], axis=0)
            if not last:
                a = jnp.concatenate([a[:d], a[d:] * a[:S - d]], axis=0)
        d *= 2
    return u


def _scan_up(b, g, row):
    d = 1
    while d < S:
        last = 2 * d >= S
        if d < SUBLANES:
            g = g + b * _shift_up(g, d, row, 0.0)
            if not last:
                b = b * _shift_up(b, d, row, 0.0)
        else:
            g = jnp.concatenate([g[:S - d] + b[:S - d] * g[d:], g[S - d:]], axis=0)
            if not last:
                b = jnp.concatenate([b[:S - d] * b[d:], b[S - d:]], axis=0)
        d *= 2
    return g


def _softplus(x):
    return jnp.maximum(x, 0.0) + jnp.log1p(jnp.exp(-jnp.abs(x)))


N_SAVED = 5


def _rnn_gates(x, cw, cb, wa, ba, wx, bx, lam, row, pad, saved=None):
    xs = pad.down(x, (1, 2, 3), 0.0)
    if saved is not None:
        r, i, a, mult, xc = saved
        return xc, xc.astype(BF16), r, i, _softplus(-lam), a, mult, xs
    xc = cb + cw[3:4, :] * x
    for j in (1, 2, 3):
        xc = xc + cw[3 - j:4 - j, :] * xs[j - 1]
    xcb = xc.astype(BF16)
    r = _sigmoid(_dot(xcb, wa) + ba)
    i = _sigmoid(_dot(xcb, wx) + bx)
    sp = _softplus(-lam)
    log_a = (-LRU_C) * r * sp
    a = jnp.exp(log_a)
    mult = jnp.where(row == 0, 1.0, jnp.sqrt(jnp.tanh(-log_a) * (1.0 + a * a)))
    return xc, xcb, r, i, sp, a, mult, xs


def _rnn_fwd(proj3, conv_w, conv_b, wa, ba, wx, bx, lam, token):
    def body(x_ref, cw_ref, cb_ref, wa_ref, ba_ref, wx_ref, bx_ref, lam_ref, _token, h_ref, g_ref, pad):
        row = _row_iota()
        sh = _Shifter(pad)
        x = x_ref[0].astype(F32)
        xc, _, r, i, _, a, mult, _ = _rnn_gates(x, cw_ref[...], cb_ref[...], wa_ref[0], ba_ref[...],
                                             wx_ref[0], bx_ref[...], lam_ref[...], row, sh)
        for k, val in enumerate((r, i, a, mult, xc)):
            g_ref[k, 0] = val
        h_ref[0] = _scan_down(a, mult * (i * xc), sh)

    vec = lambda: pl.BlockSpec((1, RBW), lambda b, n: (0, n))
    mat = lambda: pl.BlockSpec((1, RBW, RBW), lambda b, n: (n, 0, 0))
    return pl.pallas_call(
        body, grid=(BL, NRB), name="rnn_fwd",
        in_specs=[pl.BlockSpec((1, S, RBW), lambda b, n: (b, 0, n)),
                  pl.BlockSpec((CONVW, RBW), lambda b, n: (0, n)),
                  vec(), mat(), vec(), mat(), vec(), vec(), pl.BlockSpec((8, LANES), lambda b, n: (0, 0))],
        out_specs=[pl.BlockSpec((1, S, RBW), lambda b, n: (b, 0, n)),
                   pl.BlockSpec((N_SAVED, 1, S, RBW), lambda b, n: (0, b, 0, n))],
        out_shape=[jax.ShapeDtypeStruct((BL, S, DR), F32), jax.ShapeDtypeStruct((N_SAVED, BL, S, DR), F32)],
        scratch_shapes=[pltpu.VMEM((N_SHIFT_BUFS, S + 2 * SUBLANES, RBW), F32)],
        compiler_params=_cp(("parallel", "parallel"), VMEM_MID),
    )(*_hbm(proj3, conv_w, conv_b, wa, ba, wx, bx, lam, token))


def _rnn_bwd(proj3, h3, dh3, gates, slab_a3, conv_w, conv_b, wa, ba, wx, bx, lam, token):
    def body(x_ref, h_ref, dh_ref, g_ref, cw_ref, cb_ref, wa_ref, ba_ref, wx_ref, bx_ref, lam_ref, _alias, _token,
             dx_ref, dcw_ref, dcb_ref, dwa_ref, dba_ref, dwx_ref, dbx_ref, dlam_ref, pad):
        row = _row_iota()
        sh = _Shifter(pad)
        x = x_ref[0].astype(F32)
        cw = cw_ref[...]
        wa_v = wa_ref[0]
        wx_v = wx_ref[0]
        lam_v = lam_ref[...]
        xc, xcb, r, i, sp, a, mult, xs = _rnn_gates(x, cw, cb_ref[...], wa_v, ba_ref[...], wx_v, bx_ref[...], lam_v,
                                                    row, sh, [g_ref[k, 0] for k in range(N_SAVED)])
        h = h_ref[0]
        g = _scan_up(_shift_up(a, 1, sh, 0.0), dh_ref[0], sh)
        da = g * _shift_down(h, 1, sh, 0.0)
        dmult = jnp.where(row == 0, 0.0, g * (i * xc))
        gm = g * mult
        di = gm * xc
        dxc = gm * i
        dlog_a = da * a - dmult * (a * a) / mult
        dr = dlog_a * ((-LRU_C) * sp)
        dsp = jnp.sum(dlog_a * ((-LRU_C) * r), axis=0, keepdims=True)
        dlam = dsp * (-_sigmoid(-lam_v))
        dpa = dr * r * (1.0 - r)
        dpx = di * i * (1.0 - i)
        dpab = dpa.astype(BF16)
        dpxb = dpx.astype(BF16)
        dwa = _dot_tn(xcb, dpab)
        dwx = _dot_tn(xcb, dpxb)
        dba = jnp.sum(dpa, axis=0, keepdims=True)
        dbx = jnp.sum(dpx, axis=0, keepdims=True)
        dxc = dxc + _dot_nt(dpab, wa_v) + _dot_nt(dpxb, wx_v)
        dcb = jnp.sum(dxc, axis=0, keepdims=True)
        dx = cw[3:4, :] * dxc
        dcw_rows = [None] * CONVW
        dcw_rows[3] = jnp.sum(dxc * x, axis=0, keepdims=True)
        dxc_up = sh.up(dxc, (1, 2, 3), 0.0)
        for j in (1, 2, 3):
            dx = dx + cw[3 - j:4 - j, :] * dxc_up[j - 1]
            dcw_rows[3 - j] = jnp.sum(dxc * xs[j - 1], axis=0, keepdims=True)
        dx_ref[0] = dx.astype(BF16)
        dcw = jnp.concatenate(dcw_rows, axis=0)
        first = pl.program_id(1) == 0

        @pl.when(first)
        def _():
            dcw_ref[...] = dcw
            dcb_ref[...] = dcb
            dwa_ref[0] = dwa
            dba_ref[...] = dba
            dwx_ref[0] = dwx
            dbx_ref[...] = dbx
            dlam_ref[...] = dlam

        @pl.when(jnp.logical_not(first))
        def _():
            dcw_ref[...] += dcw
            dcb_ref[...] += dcb
            dwa_ref[0] += dwa
            dba_ref[...] += dba
            dwx_ref[0] += dwx
            dbx_ref[...] += dbx
            dlam_ref[...] += dlam

    slab = lambda: pl.BlockSpec((1, S, RBW), lambda n, b: (b, 0, n))
    vec = lambda: pl.BlockSpec((1, RBW), lambda n, b: (0, n))
    mat = lambda: pl.BlockSpec((1, RBW, RBW), lambda n, b: (n, 0, 0))
    taps = lambda: pl.BlockSpec((CONVW, RBW), lambda n, b: (0, n))
    vshape = jax.ShapeDtypeStruct((1, DR), F32)
    mshape = jax.ShapeDtypeStruct((NRB, RBW, RBW), F32)
    return pl.pallas_call(
        body, grid=(NRB, BL), name="rnn_bwd",
        in_specs=[slab(), slab(), slab(), pl.BlockSpec((N_SAVED, 1, S, RBW), lambda n, b: (0, b, 0, n)),
                  taps(), vec(), mat(), vec(), mat(), vec(), vec(),
                  pl.BlockSpec(memory_space=pl.ANY), pl.BlockSpec((8, LANES), lambda n, b: (0, 0))],
        out_specs=[slab(), taps(), vec(), mat(), vec(), mat(), vec(), vec()],
        out_shape=[jax.ShapeDtypeStruct((BL, S, A_W), BF16), jax.ShapeDtypeStruct((CONVW, DR), F32),
                   vshape, mshape, vshape, mshape, vshape, vshape],
        input_output_aliases={11: 0},
        scratch_shapes=[pltpu.VMEM((N_SHIFT_BUFS, S + 2 * SUBLANES, RBW), F32)],
        compiler_params=_cp(("parallel", "arbitrary"), 48 * 1024 * 1024),
    )(*_hbm(proj3, h3, dh3, gates, conv_w, conv_b, wa, ba, wx, bx, lam, slab_a3, token))


NQB = S // QB


def _rms_head(t, gain):
    rstd = lax.rsqrt(jnp.mean(t * t, axis=-1, keepdims=True) + EPS)
    return t * rstd * gain


def _rope(t, cs, sn):
    return t * cs + pltpu.roll(t, HD // 2, 1) * sn


def _rope_t(dy, cs, sn):
    return dy * cs - pltpu.roll(dy, HD // 2, 1) * sn


def _bdot_nt(a, b):
    return lax.dot_general(a, b, (((2,), (2,)), ((0,), (0,))), preferred_element_type=F32)


def _bdot(a, b):
    return lax.dot_general(a, b, (((2,), (1,)), ((0,), (0,))), preferred_element_type=F32)


def _bdot_tn(a, b):
    return lax.dot_general(a, b, (((1,), (1,)), ((0,), (0,))), preferred_element_type=F32)


STRIDE_MAX = 4


def _permute(buf, x, dil, dst, off=0):
    ln = S // dil
    if dil == 1:
        dst[pl.ds(off, S), :] = x.astype(dst.dtype)
        return
    buf[0] = x
    if dil <= STRIDE_MAX:
        for c in range(dil):
            dst[pl.ds(off + c * ln, ln), :] = buf.at[0][pl.ds(c, ln, stride=dil), :].astype(dst.dtype)
        return
    f, r = STRIDE_MAX, dil // STRIDE_MAX
    part = S // f
    for c1 in range(f):
        buf.at[1][pl.ds(c1 * part, part), :] = buf.at[0][pl.ds(c1, part, stride=f), :]
    for c1 in range(f):
        for c2 in range(r):
            dst[pl.ds(off + (c1 + f * c2) * ln, ln), :] = (
                buf.at[1][pl.ds(c1 * part + c2, ln, stride=r), :].astype(dst.dtype))


def _unpermute(buf, xp, dil, dst):
    ln = S // dil
    if dil == 1:
        dst[...] = xp
        return
    if dil <= STRIDE_MAX:
        for c in range(dil):
            dst[pl.ds(c, ln, stride=dil), :] = xp[c * ln:(c + 1) * ln]
        return
    f, r = STRIDE_MAX, dil // STRIDE_MAX
    part = S // f
    for c1 in range(f):
        for c2 in range(r):
            c = c1 + f * c2
            buf.at[1][pl.ds(c1 * part + c2, ln, stride=r), :] = xp[c * ln:(c + 1) * ln]
    for c1 in range(f):
        dst[pl.ds(c1, part, stride=f), :] = buf[1, pl.ds(c1 * part, part), :]


def _blocks3(ref, off=0):
    return ref[pl.ds(off, S), :].reshape(NQB, QB, HD)


def _att_prep(q_ref, k_ref, v_ref, cos_ref, sin_ref, qn, kn, dil, nat, qs, ksp, vsp):
    cs = cos_ref[...]
    sn = sin_ref[...]
    zero = jnp.zeros((QB, HD), BF16)
    ksp[pl.ds(0, QB), :] = zero
    vsp[pl.ds(0, QB), :] = zero
    _permute(nat, _rope(_rms_head(q_ref[0].astype(F32), qn), cs, sn), dil, qs)
    _permute(nat, _rope(_rms_head(k_ref[0].astype(F32), kn), cs, sn), dil, ksp, QB)
    _permute(nat, v_ref[0].astype(F32), dil, vsp, QB)


def _att_scores(qs, ksp, dil):
    nb = S // dil // QB
    q3 = _blocks3(qs)
    shape = (NQB, QB, QB)
    qi = lax.broadcasted_iota(jnp.int32, shape, 1)
    kj = lax.broadcasted_iota(jnp.int32, shape, 2)
    s_c = jnp.where(qi >= kj, _bdot_nt(q3, _blocks3(ksp, QB)) * SCALE, NEG)
    if nb == 1:
        return q3, s_c, None
    jj = lax.broadcasted_iota(jnp.int32, shape, 0)
    ok = (kj >= qi) & ((jj & (nb - 1)) != 0)
    s_p = jnp.where(ok, _bdot_nt(q3, _blocks3(ksp)) * SCALE, NEG)
    return q3, s_c, s_p


def _qkv_spec(kind, g):
    base = OFF_Q // HD + (3 * g + kind) * NH
    return pl.BlockSpec((1, S, HD), lambda b, h: (b, 0, base + h))


def _attn_fwd(proj3, cos_t, sin_t, q_norm, k_norm):
    def body(*refs):
        qkv_refs = refs[:9]
        (cos_ref, sin_ref, qn_ref, kn_ref, att_ref, lse_ref, w_ref, qp_ref, kp_ref, vp_ref,
         nat, qs, ksp, vsp, og) = refs[9:]
        for g, (window, dil) in enumerate(PATTERNS):
            q_ref, k_ref, v_ref = qkv_refs[3 * g:3 * g + 3]
            _att_prep(q_ref, k_ref, v_ref, cos_ref, sin_ref, qn_ref[g:g + 1, :], kn_ref[g:g + 1, :], dil,
                      nat, qs, ksp, vsp)
            qp_ref[g, 0] = qs[...]
            kp_ref[g, 0] = ksp[pl.ds(QB, S), :]
            vp_ref[g, 0] = vsp[pl.ds(QB, S), :]
            _, s_c, s_p = _att_scores(qs, ksp, dil)
            m = jnp.max(s_c, axis=-1, keepdims=True)
            if s_p is not None:
                m = jnp.maximum(m, jnp.max(s_p, axis=-1, keepdims=True))
            e_c = jnp.exp(s_c - m)
            den = jnp.sum(e_c, axis=-1, keepdims=True)
            o = _bdot(e_c.astype(BF16), _blocks3(vsp, QB))
            if s_p is not None:
                e_p = jnp.exp(s_p - m)
                den = den + jnp.sum(e_p, axis=-1, keepdims=True)
                o = o + _bdot(e_p.astype(BF16), _blocks3(vsp))
            _unpermute(nat, (o / den).reshape(S, HD), dil, og.at[g])
            _unpermute(nat, jnp.broadcast_to(m + jnp.log(den), (NQB, QB, HD)).reshape(S, HD), dil,
                       lse_ref.at[g, 0])
        l0 = lse_ref[0, 0]
        l1 = lse_ref[1, 0]
        l2 = lse_ref[2, 0]
        mx = jnp.maximum(jnp.maximum(l0, l1), l2)
        e0 = jnp.exp(l0 - mx)
        e1 = jnp.exp(l1 - mx)
        e2 = jnp.exp(l2 - mx)
        inv = 1.0 / (e0 + e1 + e2)
        w0 = e0 * inv
        w1 = e1 * inv
        w2 = e2 * inv
        w_ref[0, 0] = w0
        w_ref[1, 0] = w1
        w_ref[2, 0] = w2
        att_ref[0] = w0 * og[0] + w1 * og[1] + w2 * og[2]

    in_specs = [_qkv_spec(kind, g) for g in range(NG) for kind in range(3)]
    in_specs += [pl.BlockSpec((S, HD), lambda b, h: (0, 0)), pl.BlockSpec((S, HD), lambda b, h: (0, 0)),
                 pl.BlockSpec((NG, HD), lambda b, h: (0, 0)), pl.BlockSpec((NG, HD), lambda b, h: (0, 0))]
    stat = lambda: pl.BlockSpec((NG, 1, S, HD), lambda b, h: (0, b, 0, h))
    return pl.pallas_call(
        body, grid=(BL, NH), name="attn_fwd",
        in_specs=in_specs,
        out_specs=[pl.BlockSpec((1, S, HD), lambda b, h: (b, 0, h)), stat(), stat(), stat(), stat(), stat()],
        out_shape=[jax.ShapeDtypeStruct((BL, S, ATT), F32),
                   jax.ShapeDtypeStruct((NG, BL, S, ATT), F32),
                   jax.ShapeDtypeStruct((NG, BL, S, ATT), F32)]
        + [jax.ShapeDtypeStruct((NG, BL, S, ATT), BF16)] * 3,
        scratch_shapes=[pltpu.VMEM((2, S, HD), F32), pltpu.VMEM((S, HD), BF16), pltpu.VMEM((S + QB, HD), BF16),
                        pltpu.VMEM((S + QB, HD), BF16), pltpu.VMEM((NG, S, HD), F32)],
        compiler_params=_cp(("parallel", "parallel"), VMEM_BIG),
    )(*_hbm(*([proj3] * 9), cos_t, sin_t, q_norm, k_norm))


def _attn_bwd_group(g, proj3, cos_t, sin_t, qn_g, kn_g, lse, wts, qkv_p, datt3, sbar3, slabs):
    dil = PATTERNS[g][1]
    n_alias = 0 if slabs is None else 3

    def norm_rope_bwd(dpost, raw, gain, cs, sn):
        dn = _rope_t(dpost, cs, sn)
        rstd = lax.rsqrt(jnp.mean(raw * raw, axis=-1, keepdims=True) + EPS)
        xh = raw * rstd
        dgain = jnp.sum(dn * xh, axis=0, keepdims=True)
        gd = dn * gain
        draw = rstd * (gd - xh * jnp.mean(gd * xh, axis=-1, keepdims=True))
        return draw, dgain

    def body(*refs):
        (q_ref, k_ref, qp_ref, kp_ref, vp_ref, cos_ref, sin_ref, qn_ref, kn_ref, lse_ref, w_ref, datt_ref,
         sbar_ref) = refs[:13]
        (dq_ref, dk_ref, dv_ref, dqn_ref, dkn_ref, nat, ksp, vsp, dos, cvp, lsp, acc) = refs[13 + n_alias:]
        qn = qn_ref[...]
        kn = kn_ref[...]
        cs = cos_ref[...]
        sn = sin_ref[...]
        qs = qp_ref.at[0, 0]
        zero = jnp.zeros((QB, HD), BF16)
        ksp[pl.ds(0, QB), :] = zero
        vsp[pl.ds(0, QB), :] = zero
        ksp[pl.ds(QB, S), :] = kp_ref[0, 0]
        vsp[pl.ds(QB, S), :] = vp_ref[0, 0]
        wv = w_ref[0, 0]
        _permute(nat, wv * datt_ref[0], dil, dos)
        _permute(nat, wv * sbar_ref[0], dil, cvp)
        _permute(nat, lse_ref[0, 0], dil, lsp)
        q3, s_c, s_p = _att_scores(qs, ksp, dil)
        do3 = _blocks3(dos)
        lse3 = _blocks3(lsp)[:, :, 0:1]
        cv3 = _blocks3(cvp)[:, :, 0:1]
        p_c = jnp.exp(s_c - lse3)
        ds_c = (p_c * (_bdot_nt(do3, _blocks3(vsp, QB)) - cv3)).astype(BF16)
        dq = _bdot(ds_c, _blocks3(ksp, QB))
        acc[0] = _bdot_tn(ds_c, q3).reshape(S, HD)
        acc[1] = _bdot_tn(p_c.astype(BF16), do3).reshape(S, HD)
        if s_p is not None:
            p_p = jnp.exp(s_p - lse3)
            ds_p = (p_p * (_bdot_nt(do3, _blocks3(vsp)) - cv3)).astype(BF16)
            dq = dq + _bdot(ds_p, _blocks3(ksp))
            early = pl.ds(0, S - QB)
            acc[0, early, :] += _bdot_tn(ds_p, q3).reshape(S, HD)[QB:]
            acc[1, early, :] += _bdot_tn(p_p.astype(BF16), do3).reshape(S, HD)[QB:]
        _unpermute(nat, (dq * SCALE).reshape(S, HD), dil, nat.at[0])
        draw, dqn = norm_rope_bwd(nat[0], q_ref[0].astype(F32), qn, cs, sn)
        dq_ref[0] = draw.astype(BF16)
        _unpermute(nat, acc[0] * SCALE, dil, nat.at[0])
        draw, dkn = norm_rope_bwd(nat[0], k_ref[0].astype(F32), kn, cs, sn)
        dk_ref[0] = draw.astype(BF16)
        _unpermute(nat, acc[1], dil, nat.at[0])
        dv_ref[0] = nat[0].astype(BF16)
        first = (pl.program_id(0) == 0) & (pl.program_id(1) == 0)

        @pl.when(first)
        def _():
            dqn_ref[...] = dqn
            dkn_ref[...] = dkn

        @pl.when(jnp.logical_not(first))
        def _():
            dqn_ref[...] += dqn
            dkn_ref[...] += dkn

    full = lambda r: pl.BlockSpec((r, HD), lambda b, h: (0, 0))
    stat = lambda: pl.BlockSpec((1, 1, S, HD), lambda b, h: (g, b, 0, h))
    slab = lambda: pl.BlockSpec((1, S, HD), lambda b, h: (b, 0, h))
    out_slab = lambda: pl.BlockSpec((1, S, HD), lambda b, h: (b, 0, g * NH + h))
    big = jax.ShapeDtypeStruct((BL, S, GW), BF16)
    vecs = jax.ShapeDtypeStruct((1, HD), F32)
    in_specs = [_qkv_spec(0, g), _qkv_spec(1, g), stat(), stat(), stat(), full(S), full(S), full(1), full(1),
                stat(), stat(), slab(), slab()]
    args = [proj3, proj3, *qkv_p, cos_t, sin_t, qn_g, kn_g, lse, wts, datt3, sbar3]
    aliases = {}
    if slabs is not None:
        in_specs += [pl.BlockSpec(memory_space=pl.ANY)] * 3
        args += list(slabs)
        aliases = {13: 0, 14: 1, 15: 2}
    return pl.pallas_call(
        body, grid=(BL, NH), name="attn_bwd_g%d" % g,
        in_specs=in_specs,
        out_specs=[out_slab(), out_slab(), out_slab(), full(1), full(1)],
        out_shape=[big, big, big, vecs, vecs],
        scratch_shapes=[pltpu.VMEM((2, S, HD), F32), pltpu.VMEM((S + QB, HD), BF16),
                        pltpu.VMEM((S + QB, HD), BF16), pltpu.VMEM((S, HD), BF16), pltpu.VMEM((S, HD), F32),
                        pltpu.VMEM((S, HD), F32), pltpu.VMEM((2, S, HD), F32)],
        input_output_aliases=aliases,
        compiler_params=_cp(("arbitrary", "arbitrary"), VMEM_BIG),
    )(*_hbm(*args))


def _tail(x, proj, h, att, p, tgt, w_o_rnn, w_o_att_t, w_out, w_pg, w_ple_t, norm_ple, b_pg, tm=256):
    nt = T // tm
    inv_d = 1.0 / D

    def body(x_ref, h_ref, zr_ref, att_ref, za_ref, g0a_ref, g0b_ref, g1a_ref, g1b_ref, p_ref, tgt_ref,
             np_ref, bpg_ref, wor_hbm, woa_hbm, wout_hbm, wpg_hbm, wple_hbm,
             dx1_ref, merged_ref, n1_ref, dpre_ref, dpe_ref, dyr_ref, dya_ref, slab_a_ref, slab_c_ref, dh_ref,
             datt_ref, sbar_ref, yrnn_ref, yatt_ref, loss_ref, dnp_ref, dbpg_ref,
             wor, woa, wout, wpg, wple, sems):
        first = pl.program_id(0) == 0

        @pl.when(first)
        def _():
            pairs = ((wor_hbm, wor), (woa_hbm, woa), (wout_hbm, wout), (wpg_hbm, wpg), (wple_hbm, wple))
            _copy_together([pltpu.make_async_copy(src, dst, sems.at[k]) for k, (src, dst) in enumerate(pairs)])

        xv = x_ref[...]
        hv = h_ref[...]
        zr = zr_ref[...].astype(F32)
        av = att_ref[...]
        za = za_ref[...].astype(F32)
        szr = _sigmoid(zr)
        silu_r = zr * szr
        yrnn_b = (hv * silu_r).astype(BF16)
        sza = _sigmoid(za)
        silu_a = za * sza
        yatt_b = (av * silu_a).astype(BF16)
        yrnn_ref[...] = yrnn_b
        yatt_ref[...] = yatt_b
        yr = _dot(yrnn_b, wor[...])
        ya = _dot_nt(yatt_b, woa[...])
        g0 = _sigmoid(jnp.concatenate([g0a_ref[...], g0b_ref[...]], axis=1).astype(F32))
        g1 = _sigmoid(jnp.concatenate([g1a_ref[...], g1b_ref[...]], axis=1).astype(F32))
        merged_b = (g0 * yr + g1 * ya).astype(BF16)
        merged_ref[...] = merged_b
        x1 = xv + _dot(merged_b, wout[...])
        rstd = lax.rsqrt(jnp.mean(x1 * x1, axis=-1, keepdims=True) + EPS)
        xh = x1 * rstd
        npl = np_ref[...]
        n1_b = (xh * npl).astype(BF16)
        n1_ref[...] = n1_b
        pg = _sigmoid(_dot(n1_b, wpg[...]) + bpg_ref[...])
        pe = _dot_nt(p_ref[...].astype(BF16), wple[...])
        err = x1 + pg * pe - tgt_ref[...]
        loss_t = 0.5 * inv_d * jnp.sum(err * err)
        dy = err * inv_d
        dpe_ref[...] = (dy * pg).astype(BF16)
        dpre = dy * pe * pg * (1.0 - pg)
        dpre_b = dpre.astype(BF16)
        dpre_ref[...] = dpre_b
        dn1 = _dot_nt(dpre_b, wpg[...])
        dnp = jnp.sum(dn1 * xh, axis=0, keepdims=True)
        dbpg = jnp.sum(dpre, axis=0, keepdims=True)
        gd = dn1 * npl
        dx1 = dy + rstd * (gd - xh * jnp.mean(gd * xh, axis=-1, keepdims=True))
        dx1_ref[...] = dx1
        dmerged = _dot_nt(dx1.astype(BF16), wout[...])
        dyr_b = (dmerged * g0).astype(BF16)
        dya_b = (dmerged * g1).astype(BF16)
        dyr_ref[...] = dyr_b
        dya_ref[...] = dya_b
        slab_c_ref[:, ATT:ATT + D] = (dmerged * yr * g0 * (1.0 - g0)).astype(BF16)
        slab_c_ref[:, ATT + D:ATT + 2 * D] = (dmerged * ya * g1 * (1.0 - g1)).astype(BF16)
        dyrnn = _dot_nt(dyr_b, wor[...])
        dyatt = _dot(dya_b, woa[...])
        dh_ref[...] = dyrnn * silu_r
        slab_a_ref[...] = (dyrnn * hv * szr * (1.0 + zr * (1.0 - szr))).astype(BF16)
        datt = dyatt * silu_a
        datt_ref[...] = datt
        slab_c_ref[:, 0:ATT] = (dyatt * av * sza * (1.0 + za * (1.0 - sza))).astype(BF16)
        da = datt * av
        for hh in range(NH):
            seg = slice(hh * HD, (hh + 1) * HD)
            sbar_ref[:, seg] = jnp.broadcast_to(jnp.sum(da[:, seg], axis=-1, keepdims=True), (tm, HD))

        @pl.when(first)
        def _():
            loss_ref[...] = jnp.full((8, LANES), loss_t, F32)
            dnp_ref[...] = dnp
            dbpg_ref[...] = dbpg

        @pl.when(jnp.logical_not(first))
        def _():
            loss_ref[...] += jnp.full((8, LANES), loss_t, F32)
            dnp_ref[...] += dnp
            dbpg_ref[...] += dbpg

    tok = lambda w: pl.BlockSpec((tm, w), lambda i: (i, 0))
    col = lambda w, blk: pl.BlockSpec((tm, w), lambda i: (i, blk))
    vec = lambda: pl.BlockSpec((1, D), lambda i: (0, 0))
    hbm = lambda: pl.BlockSpec(memory_space=pl.ANY)
    gb = OFF_G // 512
    in_specs = [tok(D), tok(DR), col(DR, 1), tok(ATT), col(ATT, OFF_ZA // ATT),
                col(512, gb), col(512, gb + 1), col(512, gb + 2), col(512, gb + 3),
                tok(PLE), tok(D), vec(), vec(), hbm(), hbm(), hbm(), hbm(), hbm()]
    sh = lambda w, dt: jax.ShapeDtypeStruct((T, w), dt)
    out_shape = [sh(D, F32), sh(D, BF16), sh(D, BF16), sh(D, BF16), sh(D, BF16), sh(D, BF16), sh(D, BF16),
                 sh(A_W, BF16), sh(C_W, BF16), sh(DR, F32), sh(ATT, F32), sh(ATT, F32),
                 sh(DR, BF16), sh(ATT, BF16),
                 jax.ShapeDtypeStruct((8, LANES), F32), jax.ShapeDtypeStruct((1, D), F32),
                 jax.ShapeDtypeStruct((1, D), F32)]
    out_specs = [tok(D), tok(D), tok(D), tok(D), tok(D), tok(D), tok(D), col(DR, 1), tok(C_W), tok(DR),
                 tok(ATT), tok(ATT), tok(DR), tok(ATT),
                 pl.BlockSpec((8, LANES), lambda i: (0, 0)), vec(), vec()]
    return pl.pallas_call(
        body, grid=(nt,), name="tail_fwd_bwd",
        in_specs=in_specs, out_specs=out_specs, out_shape=out_shape,
        scratch_shapes=[pltpu.VMEM((DR, D), BF16), pltpu.VMEM((D, ATT), BF16), pltpu.VMEM((D, D), BF16),
                        pltpu.VMEM((D, D), BF16), pltpu.VMEM((D, PLE), BF16), pltpu.SemaphoreType.DMA((5,))],
        compiler_params=_cp(("arbitrary",), VMEM_BIG),
    )(*_hbm(x, h, proj, att, proj, proj, proj, proj, proj, p, tgt, norm_ple, b_pg, w_o_rnn, w_o_att_t, w_out, w_pg,
            w_ple_t))


def _rope_tables():
    pos = jnp.arange(S, dtype=F32)
    inv_freq = ROPE_THETA ** (-jnp.arange(0, HD, 2, dtype=F32) / HD)
    ang = pos[:, None] * inv_freq[None, :]
    cos, sin = jnp.cos(ang), jnp.sin(ang)
    return jnp.concatenate([cos, cos], axis=1), jnp.concatenate([-sin, sin], axis=1)


def _local_step(x, p, tgt, project, other_weights, norm_mix, conv_b,
                w_rg_a, b_rg_a, w_rg_x, b_rg_x, lam, q_norm, k_norm, norm_ple, b_pg, start_reduce=None,
                entry_token=None):
    if start_reduce is None:
        start_reduce = lambda arrs, tag: (jnp.zeros((8, LANES), F32), arrs)
    if entry_token is None:
        entry_token = jnp.zeros((8, LANES), F32)
    cos_t, sin_t = _rope_tables()
    wa_b = w_rg_a.astype(BF16)
    wx_b = w_rg_x.astype(BF16)

    hn = _rmsnorm_fwd(x, norm_mix, entry_token)
    proj, w_bufs, chips, conv_w, token = project(hn)
    proj3 = proj.reshape(BL, S, NIN)
    h3, gates = _rnn_fwd(proj3, conv_w, conv_b, wa_b, b_rg_a, wx_b, b_rg_x, lam, token)
    att3, lse, wts, *qkv_p = _attn_fwd(proj3, cos_t, sin_t, q_norm, k_norm)
    w_o_rnn, w_o_att_t, w_out, w_pg, w_ple_t = other_weights(att3)
    (dx1, merged, n1, dpre, dpe, dyr, dya, slab_a, slab_c, dh, datt, sbar, yrnn, yatt, loss8, dnp, dbpg) = _tail(
        x, proj, h3.reshape(T, DR), att3.reshape(T, ATT), p, tgt, w_o_rnn, w_o_att_t, w_out, w_pg, w_ple_t,
        norm_ple, b_pg)

    token, pending_out = start_reduce([
        _mm_tn(yrnn, dyr, 640, T, "dw_o_rnn"),
        _mm_tn(dya, yatt, 512, T, "dw_o_att_t"),
        _mm_tn(merged, dx1, 512, T // 2, "dw_out"),
        _mm_tn(n1, dpre, 512, T, "dw_ple_gate"),
        _mm_tn(dpe, p, 512, T, "dw_ple_t")], "out")

    slab_a3, dcw, dcb, dwa, dba, dwx, dbx, dlam = _rnn_bwd(
        proj3, h3, dh.reshape(BL, S, DR), gates, slab_a.reshape(BL, S, A_W), conv_w, conv_b, wa_b, b_rg_a, wx_b, b_rg_x, lam,
        token)
    datt3 = datt.reshape(BL, S, ATT)
    sbar3 = sbar.reshape(BL, S, ATT)
    slabs = None
    dqn = []
    dkn = []
    for g in range(NG):
        dq, dk, dv, dqn_g, dkn_g = _attn_bwd_group(g, proj3, cos_t, sin_t, q_norm[g:g + 1], k_norm[g:g + 1],
                                                   lse, wts, qkv_p, datt3, sbar3, slabs)
        slabs = (dq, dk, dv)
        dqn.append(dqn_g)
        dkn.append(dkn_g)
    pieces = [slab_a3.reshape(T, A_W)] + [t.reshape(T, GW) for t in slabs] + [slab_c]
    dw_in_t, db_in = _dw_in(pieces, hn)
    token, pending_in = start_reduce([dw_in_t], "in")
    grad_x, dnm = _grad_x(pieces, w_bufs, chips, token, x, dx1, norm_mix)

    small = dict(w_rg_a=dwa, w_rg_x=dwx, norm_mix=dnm, b_in=db_in, conv_b=dcb, b_rg_a=dba, b_rg_x=dbx,
                 lru_lambda=dlam, q_norm=dqn, k_norm=dkn, norm_ple=dnp, b_ple_gate=dbpg, conv_w=dcw, loss=loss8)
    return grad_x, pending_out, pending_in, small


MESH = pl.DeviceIdType.MESH
HBM_SPEC = pl.BlockSpec(memory_space=pl.ANY)


def _my_pos():
    return lax.axis_index("x"), lax.axis_index("y"), lax.axis_index("c")


def _flip(pos, k):
    x, y, c = pos
    return (1 - x if k & 4 else x, 1 - y if k & 2 else y, 1 - c if k & 1 else c)


def _lin(pos):
    return 4 * pos[0] + 2 * pos[1] + pos[2]


def _chip(pos):
    return 2 * pos[0] + pos[1]


def _exchange_within_chip(parts, name):
    na = len(parts)

    def body(*refs):
        a_refs = refs[:na]
        recv_refs = refs[na:2 * na]
        send_sems, recv_sems = refs[2 * na:]
        me = _my_pos()
        c = me[2]
        sibling = _flip(me, 1)
        remote = []
        for i in range(na):
            for q in range(NCHIP):
                rc = pltpu.make_async_remote_copy(
                    src_ref=a_refs[i].at[q, 1 - c], dst_ref=recv_refs[i].at[q],
                    send_sem=send_sems.at[NCHIP * i + q], recv_sem=recv_sems.at[NCHIP * i + q],
                    device_id=sibling, device_id_type=MESH)
                rc.start()
                remote.append(rc)
        for rc in remote:
            rc.wait_recv()
        for rc in remote:
            rc.wait_send()

    return pl.pallas_call(
        body, name=name, out_shape=[jax.ShapeDtypeStruct((NCHIP,) + a.shape[2:], a.dtype) for a in parts],
        in_specs=[HBM_SPEC] * na, out_specs=[HBM_SPEC] * na,
        scratch_shapes=[pltpu.SemaphoreType.DMA((NCHIP * na,)), pltpu.SemaphoreType.DMA((NCHIP * na,))],
    )(*parts)


HBM_ONLY = pl.BlockSpec(memory_space=pltpu.HBM)
SEM_SPEC = pl.BlockSpec(memory_space=pltpu.SEMAPHORE)
SPLIT_COPY = pltpu.CompilerParams(has_side_effects=pltpu.SideEffectType.DATAFLOW_SIDE_EFFECTING)


def _chip_peers(me):
    return [_flip(me, 4), _flip(me, 2), _flip(me, 6)]


def _between_chips_start(parts, name):
    na = len(parts)

    def body(*refs):
        a_refs = refs[:na]
        land_refs = refs[na:2 * na]
        send_sems, recv_sems = refs[2 * na], refs[2 * na + 1]
        token = refs[-1]
        me = _my_pos()
        myq = _chip(me)
        for i in range(na):
            for j, peer in enumerate(_chip_peers(me)):
                pltpu.make_async_remote_copy(
                    src_ref=a_refs[i].at[_chip(peer)], dst_ref=land_refs[i].at[myq],
                    send_sem=send_sems.at[3 * i + j], recv_sem=recv_sems.at[3 * i + j],
                    device_id=peer, device_id_type=MESH).start()
        token[...] = jnp.zeros_like(token)

    hbm = [pltpu.HBM(a.shape, a.dtype) for a in parts]
    srcs = [pltpu.with_memory_space_constraint(a, pltpu.HBM) for a in parts]
    lands = [pltpu.with_memory_space_constraint(lax.empty(a.shape, a.dtype), pltpu.HBM) for a in parts]
    res = pl.pallas_call(
        body, name=name,
        out_shape=(pltpu.SemaphoreType.DMA((3 * na,)), pltpu.SemaphoreType.DMA((3 * na,)), *hbm, *hbm,
                   jax.ShapeDtypeStruct((8, LANES), F32)),
        in_specs=[HBM_ONLY] * (2 * na),
        out_specs=(SEM_SPEC, SEM_SPEC, *([HBM_ONLY] * (2 * na)), pl.BlockSpec(memory_space=pltpu.VMEM)),
        input_output_aliases={i: 2 + i for i in range(2 * na)},
        compiler_params=SPLIT_COPY,
    )(*srcs, *lands)
    return res[-1], (res[0], res[1], list(res[2:2 + na]), list(res[2 + na:2 + 2 * na]))


def _between_chips_wait(pending, after, name):
    send_sems, recv_sems, parts, lands = pending
    na = len(parts)

    def body(*refs):
        a_refs = refs[:na]
        land_refs = refs[na:2 * na]
        send_sems, recv_sems = refs[2 * na], refs[2 * na + 1]
        me = _my_pos()
        for i in range(na):
            for j, peer in enumerate(_chip_peers(me)):
                cp = pltpu.make_async_remote_copy(
                    src_ref=a_refs[i].at[_chip(peer)], dst_ref=land_refs[i].at[_chip(peer)],
                    send_sem=send_sems.at[3 * i + j], recv_sem=recv_sems.at[3 * i + j],
                    device_id=peer, device_id_type=MESH)
                cp.wait_send()
                cp.wait_recv()

    hbm = [pltpu.HBM(a.shape, a.dtype) for a in parts]
    res = pl.pallas_call(
        body, name=name, out_shape=(*hbm, *hbm),
        in_specs=[HBM_ONLY] * (2 * na) + [SEM_SPEC, SEM_SPEC, pl.BlockSpec(memory_space=pl.ANY)],
        out_specs=[HBM_ONLY] * (2 * na),
        input_output_aliases={i: i for i in range(2 * na)},
        compiler_params=SPLIT_COPY,
    )(*parts, *lands, send_sems, recv_sems, after)
    return list(res[:na]), list(res[na:])


def _remote(src, dst, send_sems, recv_sems, idx, peer):
    return pltpu.make_async_remote_copy(src_ref=src, dst_ref=dst, send_sem=send_sems.at[idx],
                                        recv_sem=recv_sems.at[idx], device_id=peer, device_id_type=MESH)


def _copies_own(bufs, me):
    return [(bufs[0], bufs[1].at[me[2]], 0, _flip(me, 1))]


def _copies_near(bufs, me):
    return [(bufs[0], bufs[1].at[0, me[2]], 0, _flip(me, 2)), (bufs[0], bufs[1].at[1, me[2]], 1, _flip(me, 4))]


def _copies_far(bufs, me):
    return [(bufs[0], bufs[1].at[me[2]], 0, _flip(me, 6))]


def _copies_others(bufs, me):
    na = len(bufs) // 2
    return [(bufs[i], bufs[na + i].at[_lin(me)], 7 * i + k - 1, _flip(me, k))
            for i in range(na) for k in range(1, NDEV)]


def _copies_exchange(bufs, me):
    na = len(bufs) // 2
    return [(bufs[i].at[_lin(_flip(me, k))], bufs[na + i].at[_lin(me)], 7 * i + k - 1, _flip(me, k))
            for i in range(na) for k in range(1, NDEV)]


GROUP_COPIES = dict(own=_copies_own, near=_copies_near, far=_copies_far, others=_copies_others,
                    exchange=_copies_exchange)
GROUP_COUNT = dict(own=1, near=2, far=1)
TO_ALL = ("others", "exchange")


def _gather_start(bufs, groups, after, name):
    nb = len(bufs)
    ng = len(groups)

    def body(*refs):
        b = refs[:nb]
        sems = refs[nb + 1:nb + 1 + 2 * ng]
        token = refs[-1]
        me = _my_pos()
        for gi, (group, idx) in enumerate(groups):
            for src, dst, k, peer in GROUP_COPIES[group]([b[i] for i in idx], me):
                _remote(src, dst, sems[2 * gi], sems[2 * gi + 1], k, peer).start()
        token[...] = jnp.zeros_like(token)

    sem_t = []
    for group, idx in groups:
        cnt = 7 * (len(idx) // 2) if group in TO_ALL else GROUP_COUNT[group]
        sem_t += [pltpu.SemaphoreType.DMA((cnt,)), pltpu.SemaphoreType.DMA((cnt,))]
    ins = [pltpu.with_memory_space_constraint(a, pltpu.HBM) for a in bufs]
    res = pl.pallas_call(
        body, name=name,
        out_shape=(*sem_t, *[pltpu.HBM(a.shape, a.dtype) for a in bufs], jax.ShapeDtypeStruct((8, LANES), F32)),
        in_specs=[HBM_ONLY] * nb + [pl.BlockSpec(memory_space=pl.ANY)],
        out_specs=(*([SEM_SPEC] * (2 * ng)), *([HBM_ONLY] * nb), pl.BlockSpec(memory_space=pltpu.VMEM)),
        input_output_aliases={i: 2 * ng + i for i in range(nb)},
        compiler_params=SPLIT_COPY,
    )(*ins, after)
    return res[-1], list(res[2 * ng:2 * ng + nb]), [(res[2 * gi], res[2 * gi + 1]) for gi in range(ng)]


def _gather_wait(group, send_sems, recv_sems, bufs, after, name):
    nb = len(bufs)
    copies = GROUP_COPIES[group]

    def body(*refs):
        b = refs[:nb]
        ss, rs = refs[nb], refs[nb + 1]
        me = _my_pos()
        for src, dst, idx, peer in copies(b, me):
            if group in TO_ALL:
                landed = b[nb // 2 + idx // 7].at[_lin(peer)]
            elif group == "own":
                landed = b[1].at[1 - me[2]]
            else:
                landed = dst
            cp = _remote(src, landed, ss, rs, idx, peer)
            cp.wait_send()
            cp.wait_recv()

    res = pl.pallas_call(
        body, name=name, out_shape=[pltpu.HBM(a.shape, a.dtype) for a in bufs],
        in_specs=[HBM_ONLY] * nb + [SEM_SPEC, SEM_SPEC, pl.BlockSpec(memory_space=pl.ANY)],
        out_specs=[HBM_ONLY] * nb,
        input_output_aliases={i: i for i in range(nb)},
        compiler_params=SPLIT_COPY,
    )(*bufs, send_sems, recv_sems, after)
    return list(res)


def _forward_to_sibling(buf, name):
    n = buf.shape[0]

    def body(_in_ref, out_ref, send_sems, recv_sems):
        me = _my_pos()
        c = me[2]
        sibling = _flip(me, 1)
        sends = []
        for r in range(n):
            cp = _remote(out_ref.at[r, c], out_ref.at[r, c], send_sems, recv_sems, r, sibling)
            cp.start()
            sends.append(cp)
        for r in range(n):
            _remote(out_ref.at[r, c], out_ref.at[r, 1 - c], send_sems, recv_sems, r, sibling).wait_recv()
        for cp in sends:
            cp.wait_send()

    return pl.pallas_call(
        body, name=name, out_shape=jax.ShapeDtypeStruct(buf.shape, buf.dtype),
        in_specs=[HBM_SPEC], out_specs=HBM_SPEC,
        scratch_shapes=[pltpu.SemaphoreType.DMA((n,)), pltpu.SemaphoreType.DMA((n,))],
        input_output_aliases={0: 0},
    )(buf)


def _scalar(v):
    return jnp.asarray(v, jnp.int32).reshape(1)


def _sum_pairs(parts, theirs, name):
    na = len(parts)

    def body(c_ref, *refs):
        for i in range(na):
            o_ref = refs[2 * na + i]
            o_ref[0] = (refs[i][0, 0].astype(F32) + refs[na + i][0].astype(F32)).astype(o_ref.dtype)

    def mine_spec(a):
        return pl.BlockSpec((1, 1) + a.shape[2:], lambda q, c_ref: (q, c_ref[0], 0, 0))

    def spec(a):
        return pl.BlockSpec((1,) + a.shape[1:], lambda q, c_ref: (q, 0, 0))

    return pl.pallas_call(
        body, name=name,
        grid_spec=pltpu.PrefetchScalarGridSpec(
            num_scalar_prefetch=1, grid=(NCHIP,),
            in_specs=[mine_spec(a) for a in parts] + [spec(a) for a in theirs],
            out_specs=[spec(a) for a in theirs]),
        out_shape=[jax.ShapeDtypeStruct(a.shape, a.dtype) for a in theirs],
        compiler_params=_cp(("arbitrary",), VMEM_BIG),
    )(_scalar(lax.axis_index("c")), *_hbm(*parts, *theirs))


def _others(q, mine, nblk=NCHIP):
    return jnp.where(q == mine, (q + 1) % nblk, q)


def _sum_chips_adamw(own, recv, wv, mv, vv, token, tr, name):
    _, r, w = recv.shape

    def body(q_ref, own_ref, r0, r1, r2, r3, w_ref, m_ref, v_ref, _token, g_ref, d_ref, m2_ref, v2_ref):
        myq = q_ref[0]
        acc = None
        for q, r_ref in enumerate((r0, r1, r2, r3)):
            term = jnp.where(myq == q, own_ref[0], r_ref[0]).astype(F32)
            acc = term if acc is None else acc + term
        g_ref[...] = acc
        delta, m2, v2 = _adam_math(w_ref[...], acc, m_ref[...], v_ref[...])
        d_ref[...] = delta
        m2_ref[...] = m2
        v2_ref[...] = v2

    def recv_spec(q):
        return pl.BlockSpec((1, tr, w), lambda i, q_ref: (_others(q, q_ref[0]), i, 0))

    rows = lambda: pl.BlockSpec((tr, w), lambda i, q_ref: (i, 0))
    shp = jax.ShapeDtypeStruct((r, w), F32)
    return pl.pallas_call(
        body, name=name,
        grid_spec=pltpu.PrefetchScalarGridSpec(
            num_scalar_prefetch=1, grid=(r // tr,),
            in_specs=[pl.BlockSpec((1, tr, w), lambda i, q_ref: (q_ref[0], i, 0))]
            + [recv_spec(q) for q in range(NCHIP)] + [rows(), rows(), rows()]
            + [pl.BlockSpec((8, LANES), lambda i, q_ref: (0, 0))],
            out_specs=[rows(), rows(), rows(), rows()]),
        out_shape=[shp, shp, shp, shp],
        compiler_params=_cp(("arbitrary",), VMEM_MID),
    )(_scalar(_chip(_my_pos())), *_hbm(own, recv, recv, recv, recv, wv, mv, vv, token))


def _sum_blocks_small(own, recv, mine, transpose, name):
    na = len(recv)
    nblk = recv[0].shape[0]

    def body(q_ref, *refs):
        me = q_ref[0]
        for i in range(na):
            acc = None
            for q in range(nblk):
                term = jnp.where(me == q, refs[i][0], refs[na * (1 + q) + i][0]).astype(F32)
                acc = term if acc is None else acc + term
            refs[na * (1 + nblk) + i][...] = acc.T if transpose[i] else acc

    def oshape(a, tr):
        r, w = a.shape[1:]
        return (w, r) if tr else (r, w)

    own_spec = lambda a: pl.BlockSpec((1,) + a.shape[1:], lambda s, q_ref: (q_ref[0], 0, 0))
    recv_spec = lambda a, q: pl.BlockSpec((1,) + a.shape[1:], lambda s, q_ref: (_others(q, q_ref[0], nblk), 0, 0))
    out_spec = lambda shp: pl.BlockSpec(shp, lambda s, q_ref: (0, 0))
    in_specs = [own_spec(a) for a in own]
    for q in range(nblk):
        in_specs += [recv_spec(a, q) for a in recv]
    return pl.pallas_call(
        body, name=name,
        grid_spec=pltpu.PrefetchScalarGridSpec(
            num_scalar_prefetch=1, grid=(1,), in_specs=in_specs,
            out_specs=[out_spec(oshape(a, tr)) for a, tr in zip(recv, transpose)]),
        out_shape=[jax.ShapeDtypeStruct(oshape(a, tr), F32) for a, tr in zip(recv, transpose)],
        compiler_params=_cp(("arbitrary",), VMEM_MID),
    )(_scalar(mine), *_hbm(*own, *(list(recv) * nblk)))


def _rep_offsets():
    offs = []
    o = 0
    for r in REP_ROWS:
        offs.append(o)
        o += r
    return offs


LOSS_ROW = REP_TOTAL_ROWS


def _pack_small_grads(g):
    offs = _rep_offsets()

    def body(dwa, dwx, dnm, dbin, dcb, dba, dbx, dlam, dq0, dq1, dq2, dk0, dk1, dk2, dnp, dbpg, loss, o_ref):
        o_ref[pl.ds(REP_TOTAL_ROWS - 2, NDEV * REP_ROWS_DEV - REP_TOTAL_ROWS + 2), :] = jnp.zeros(
            (NDEV * REP_ROWS_DEV - REP_TOTAL_ROWS + 2, LANES), F32)
        o_ref[pl.ds(LOSS_ROW, 1), :] = loss[0:1, :]
        for n in range(NRB):
            o_ref[pl.ds(offs[0] + n * RBW, RBW), :] = dwa[n]
            o_ref[pl.ds(offs[1] + n * RBW, RBW), :] = dwx[n]

        def put_vec(off, ref, rows):
            for k in range(rows):
                o_ref[pl.ds(off + k, 1), :] = ref[:, k * LANES:(k + 1) * LANES]

        put_vec(offs[2], dnm, REP_ROWS[2])
        put_vec(offs[3], dbin, REP_ROWS[3])
        put_vec(offs[4], dcb, REP_ROWS[4])
        put_vec(offs[5], dba, REP_ROWS[5])
        put_vec(offs[6], dbx, REP_ROWS[6])
        put_vec(offs[7], dlam, REP_ROWS[7])
        for k, ref in enumerate((dq0, dq1, dq2)):
            o_ref[pl.ds(offs[8] + k, 1), :] = ref[...]
        for k, ref in enumerate((dk0, dk1, dk2)):
            o_ref[pl.ds(offs[9] + k, 1), :] = ref[...]
        put_vec(offs[10], dnp, REP_ROWS[10])
        put_vec(offs[11], dbpg, REP_ROWS[11])

    args = [g["w_rg_a"], g["w_rg_x"], g["norm_mix"], g["b_in"], g["conv_b"], g["b_rg_a"], g["b_rg_x"],
            g["lru_lambda"], *g["q_norm"], *g["k_norm"], g["norm_ple"], g["b_ple_gate"], g["loss"]]
    full = lambda shp: pl.BlockSpec(shp, lambda: (0,) * len(shp))
    return pl.pallas_call(
        body, name="pack_small_grads",
        in_specs=[full(a.shape) for a in args],
        out_specs=full((NDEV * REP_ROWS_DEV, LANES)),
        out_shape=jax.ShapeDtypeStruct((NDEV * REP_ROWS_DEV, LANES), F32),
    )(*_hbm(*args))


def _adam_math(wv, gv, mv, vv):
    c1 = 1.0 - B1 ** STEP
    c2 = 1.0 - B2 ** STEP
    m2 = B1 * mv + (1.0 - B1) * gv
    v2 = B2 * vv + (1.0 - B2) * (gv * gv)
    delta = (-LR) * ((m2 / c1) / (jnp.sqrt(v2 / c2) + AEPS) + WD * wv)
    return delta, m2, v2


def _adamw_small(rep_flat, w, m, v):
    offs = _rep_offsets()
    n = len(REP_NAMES)

    def body(*refs):
        g_ref = refs[0]
        w_refs = refs[1:1 + n]
        m_refs = refs[1 + n:1 + 2 * n]
        v_refs = refs[1 + 2 * n:1 + 3 * n]
        outs = refs[1 + 3 * n:]
        go, do, mo, vo = outs[:n], outs[n:2 * n], outs[2 * n:3 * n], outs[3 * n:]

        def emit(i, idx, gv):
            go[i][idx] = gv
            delta, m2, v2 = _adam_math(w_refs[i][idx], gv, m_refs[i][idx], v_refs[i][idx])
            do[i][idx] = delta
            mo[i][idx] = m2
            vo[i][idx] = v2

        for i in range(n):
            if i < 2:
                for b in range(NRB):
                    emit(i, b, g_ref[pl.ds(offs[i] + b * RBW, RBW), :])
            elif REP_NAMES[i] in ("q_norm", "k_norm"):
                emit(i, slice(None), g_ref[pl.ds(offs[i], NG), :])
            else:
                gv = jnp.concatenate([g_ref[pl.ds(offs[i] + k, 1), :] for k in range(REP_ROWS[i])], axis=1)
                emit(i, slice(None), gv)

    full = lambda shp: pl.BlockSpec(shp, lambda: (0,) * len(shp))
    pspecs = [full(a.shape) for a in w]
    pshapes = [jax.ShapeDtypeStruct(a.shape, F32) for a in w]
    res = pl.pallas_call(
        body, name="adamw_small",
        in_specs=[full(rep_flat.shape)] + pspecs * 3,
        out_specs=pspecs * 4, out_shape=pshapes * 4,
        compiler_params=_cp(None, VMEM_MID),
    )(*_hbm(rep_flat, *w, *m, *v))
    return res[:n], res[n:2 * n], res[2 * n:3 * n], res[3 * n:]


def _adamw_many(w, g, m, v, token):
    n = len(w)

    def body(*refs):
        for i in range(n):
            delta, m2, v2 = _adam_math(refs[i][...], refs[n + i][...], refs[2 * n + i][...], refs[3 * n + i][...])
            refs[4 * n + 1 + i][...] = delta
            refs[5 * n + 1 + i][...] = m2
            refs[6 * n + 1 + i][...] = v2

    full = lambda shp: pl.BlockSpec(shp, lambda: (0,) * len(shp))
    specs = [full(a.shape) for a in w]
    shapes = [jax.ShapeDtypeStruct(a.shape, F32) for a in w]
    res = pl.pallas_call(
        body, name="adamw_shards",
        in_specs=specs * 4 + [full((8, LANES))], out_specs=specs * 3, out_shape=shapes * 3,
        compiler_params=_cp(None, VMEM_MID),
    )(*_hbm(*w, *g, *m, *v, token))
    return res[:n], res[n:2 * n], res[2 * n:]


def kernel(x, p, norm_mix, w_in, b_in, conv_w, conv_b, w_rg_a, b_rg_a, w_rg_x, b_rg_x, lru_lambda, q_norm, k_norm, w_o_rnn, w_o_att, w_out, norm_ple, w_ple_gate, b_ple_gate, w_ple, loss_target, m_norm_mix, m_w_in, m_b_in, m_conv_w, m_conv_b, m_w_rg_a, m_b_rg_a, m_w_rg_x, m_b_rg_x, m_lru_lambda, m_q_norm, m_k_norm, m_w_o_rnn, m_w_o_att, m_w_out, m_norm_ple, m_w_ple_gate, m_b_ple_gate, m_w_ple, v_norm_mix, v_w_in, v_b_in, v_conv_w, v_conv_b, v_w_rg_a, v_b_rg_a, v_w_rg_x, v_b_rg_x, v_lru_lambda, v_q_norm, v_k_norm, v_w_o_rnn, v_w_o_att, v_w_out, v_norm_ple, v_w_ple_gate, v_b_ple_gate, v_w_ple):
    w = dict(norm_mix=norm_mix, w_in=w_in, b_in=b_in, conv_w=conv_w, conv_b=conv_b, w_rg_a=w_rg_a, b_rg_a=b_rg_a,
             w_rg_x=w_rg_x, b_rg_x=b_rg_x, lru_lambda=lru_lambda, q_norm=q_norm, k_norm=k_norm, w_o_rnn=w_o_rnn,
             w_o_att=w_o_att, w_out=w_out, norm_ple=norm_ple, w_ple_gate=w_ple_gate, b_ple_gate=b_ple_gate,
             w_ple=w_ple)
    m = dict(norm_mix=m_norm_mix, w_in=m_w_in, b_in=m_b_in, conv_w=m_conv_w, conv_b=m_conv_b, w_rg_a=m_w_rg_a,
             b_rg_a=m_b_rg_a, w_rg_x=m_w_rg_x, b_rg_x=m_b_rg_x, lru_lambda=m_lru_lambda, q_norm=m_q_norm,
             k_norm=m_k_norm, w_o_rnn=m_w_o_rnn, w_o_att=m_w_o_att, w_out=m_w_out, norm_ple=m_norm_ple,
             w_ple_gate=m_w_ple_gate, b_ple_gate=m_b_ple_gate, w_ple=m_w_ple)
    v = dict(norm_mix=v_norm_mix, w_in=v_w_in, b_in=v_b_in, conv_w=v_conv_w, conv_b=v_conv_b, w_rg_a=v_w_rg_a,
             b_rg_a=v_b_rg_a, w_rg_x=v_w_rg_x, b_rg_x=v_b_rg_x, lru_lambda=v_lru_lambda, q_norm=v_q_norm,
             k_norm=v_k_norm, w_o_rnn=v_w_o_rnn, w_o_att=v_w_o_att, w_out=v_w_out, norm_ple=v_norm_ple,
             w_ple_gate=v_w_ple_gate, b_ple_gate=v_b_ple_gate, w_ple=v_w_ple)
    names = list(w.keys())

    shards = [w_in[0].T.astype(BF16), w_o_rnn[0].astype(BF16), w_o_att[0].T.astype(BF16), w_out[0].astype(BF16),
              w_ple_gate[0].astype(BF16), w_ple[0].T.astype(BF16), conv_w[0]]
    pos = _my_pos()
    me, my_core, my_chip = _lin(pos), pos[2], _chip(pos)
    hbm_empty = lambda shp, dt: lax.empty(shp, dt)
    w_shard, conv_shard = shards[0], shards[6]
    shp = w_shard.shape
    entry_token, bufs, sems = _gather_start(
        [w_shard, hbm_empty((2,) + shp, BF16), hbm_empty((2, 2) + shp, BF16), conv_shard,
         hbm_empty((NDEV,) + conv_shard.shape, F32)],
        [("own", (0, 1)), ("near", (0, 2)), ("others", (3, 4))], norm_mix, "gather_start_near")
    w_src, own_l, near_l, conv_src, conv_l = bufs
    sem_own, sem_near, sem_conv = sems
    gather_out = {}

    def project(hn):
        w_thru, own = _gather_wait("own", *sem_own, [w_src, own_l], hn, "gather_wait_own")
        own = lax.dynamic_update_slice(own, w_shard[None], (my_core, 0, 0)).reshape(1, CHIP_COLS, D)
        chips = [jnp.stack([my_chip]), jnp.stack([my_chip ^ 1, my_chip ^ 2]), jnp.stack([my_chip ^ 3])]
        chips = [c.astype(jnp.int32) for c in chips]
        proj = _in_proj_chips(hn, own, b_in, chips[0], None, entry_token, "in_proj_own")
        w_thru, near = _gather_wait("near", *sem_near, [w_thru, near_l], proj, "gather_wait_near")
        near = _forward_to_sibling(near, "gather_forward_near")
        token, (w_thru, far_l), (sem_far,) = _gather_start(
            [w_thru, hbm_empty((2,) + shp, BF16)], [("far", (0, 1))], near, "gather_start_far")
        near = near.reshape(2, CHIP_COLS, D)
        proj = _in_proj_chips(hn, near, b_in, chips[1], proj, token, "in_proj_near")
        w_thru, far = _gather_wait("far", *sem_far, [w_thru, far_l], proj, "gather_wait_far")
        far = _forward_to_sibling(far[None], "gather_forward_far").reshape(1, CHIP_COLS, D)
        proj = _in_proj_chips(hn, far, b_in, chips[2], proj, token, "in_proj_far")
        conv_thru, conv_g = _gather_wait("others", *sem_conv, [conv_src, conv_l], proj, "gather_wait_conv")
        conv_g = lax.dynamic_update_slice(conv_g, conv_shard[None], (me, 0, 0))
        conv_f = conv_g.transpose(1, 0, 2).reshape(CONVW, DR)
        srcs = list(shards[1:6])
        token, obufs, (sem_out,) = _gather_start(
            srcs + [hbm_empty((NDEV,) + a.shape, BF16) for a in srcs], [("others", tuple(range(10)))], proj,
            "gather_start_out")
        gather_out.update(bufs=obufs, sems=sem_out)
        return proj, [own, near, far], jnp.concatenate(chips), conv_f, token

    def other_weights(after):
        obufs = _gather_wait("others", *gather_out["sems"], gather_out["bufs"], after, "gather_wait_out")
        full = [lax.dynamic_update_slice(a, s[None], (me, 0, 0)) for a, s in zip(obufs[5:], shards[1:6])]
        return [a.reshape((NDEV * a.shape[1], a.shape[2])) for a in full]

    def start_reduce(arrs, tag):
        if tag == "out":
            parts = [a.reshape((NDEV, a.shape[0] // NDEV, a.shape[1])) for a in arrs]
            token, bufs, (sems,) = _gather_start(
                parts + [lax.empty(a.shape, a.dtype) for a in parts], [("exchange", tuple(range(2 * len(parts))))],
                arrs[-1][:SUBLANES], "reduce_out_start")
            return token, (bufs, sems)
        parts = [a.reshape((NCHIP, 2, a.shape[0] // NDEV, a.shape[1])) for a in arrs]
        theirs = _exchange_within_chip(parts, "reduce_within_chip_" + tag)
        return _between_chips_start(_sum_pairs(parts, theirs, "sum_pairs_" + tag), "reduce_between_chips_start_" + tag)

    grad_x, pending_out, pending_in, small = _local_step(
        x.reshape(T, D), p.reshape(T, PLE), loss_target.reshape(T, D),
        project, other_weights,
        norm_mix, conv_b, w_rg_a[0], b_rg_a, w_rg_x[0], b_rg_x, lru_lambda, q_norm[0], k_norm[0],
        norm_ple, b_ple_gate, start_reduce, entry_token)

    rep_parts = _pack_small_grads(small).reshape(NDEV, REP_ROWS_DEV, LANES)
    conv_parts = small["conv_w"].reshape(CONVW, NDEV, DR // NDEV).transpose(1, 0, 2)
    smalls = [rep_parts, conv_parts]
    token, sbufs, (sem_x,) = _gather_start(
        smalls + [lax.empty(a.shape, F32) for a in smalls], [("exchange", (0, 1, 2, 3))], small["norm_mix"],
        "reduce_small_start")

    own_in, recv_in = _between_chips_wait(pending_in, token, "reduce_between_chips_wait_in")
    w_in_res = _sum_chips_adamw(own_in[0], recv_in[0], w_in[0].T, m_w_in[0].T, v_w_in[0].T, token, 304, "adamw_w_in")
    sbufs = _gather_wait("exchange", *sem_x, sbufs, w_in_res[0], "reduce_small_wait")
    g_rep, g_conv = _sum_blocks_small(sbufs[:2], sbufs[2:], me, (False, False), "sum_small")
    token, gbufs, (sem_g,) = _gather_start(
        [g_rep, lax.empty((NDEV,) + g_rep.shape, F32)], [("others", (0, 1))], g_conv, "gather_small_start")
    obufs = _gather_wait("exchange", *pending_out[1], pending_out[0], token, "reduce_out_wait")
    g_o_rnn, g_o_att, g_out, g_pg, g_ple = _sum_blocks_small(
        obufs[:5], obufs[5:], me, (False, True, False, False, True), "sum_out")

    grad, delta, new_m, new_v = {}, {}, {}, {}
    rest = ("w_o_rnn", "w_o_att", "w_out", "w_ple_gate", "w_ple", "conv_w")
    g_rest = [g_o_rnn, g_o_att, g_out, g_pg, g_ple, g_conv]
    rest_res = _adamw_many([w[n][0] for n in rest], g_rest, [m[n][0] for n in rest], [v[n][0] for n in rest], token)
    _, rep_all = _gather_wait("others", *sem_g, gbufs, rest_res[0][0], "gather_small_wait")
    rep_all = lax.dynamic_update_slice(rep_all, g_rep[None], (me, 0, 0)).reshape(NDEV * REP_ROWS_DEV, LANES)
    loss = rep_all[LOSS_ROW, 0]
    rep_shape = lambda a: a if a.ndim == 2 else a.reshape(a.shape[1:])
    res = _adamw_small(rep_all, [rep_shape(w[n]) for n in REP_NAMES], [rep_shape(m[n]) for n in REP_NAMES],
                       [rep_shape(v[n]) for n in REP_NAMES])
    for dst, vals in zip((grad, delta, new_m, new_v), res):
        for n, a in zip(REP_NAMES, vals):
            dst[n] = a.reshape(w[n].shape)
    grad["w_in"], delta["w_in"], new_m["w_in"], new_v["w_in"] = [a.T[None] for a in w_in_res]
    for n, a in zip(rest, g_rest):
        grad[n] = a[None]
    for dst, vals in zip((delta, new_m, new_v), rest_res):
        for n, a in zip(rest, vals):
            dst[n] = a[None]

    return (loss, grad_x.reshape(BL, S, D), *[grad[n] for n in names], *[delta[n] for n in names],
            *[new_m[n] for n in names], *[new_v[n] for n in names])
```

```python
import jax
import jax.numpy as jnp
from jax import lax
from jax.experimental import pallas as pl
from jax.experimental.pallas import tpu as pltpu

F32 = jnp.float32
BF16 = jnp.bfloat16

D = 1024
S = 2048
BL = 2
T = BL * S
NDEV = 8
NCHIP = 4
PLE = 256
DR = 1280
NRB = 10
RBW = 128
CONVW = 4
LRU_C = 8.0
HD = 128
NH = 4
PATTERNS = ((128, 1), (512, 4), (2048, 16))
NG = 3
ATT = NH * HD
GW = NG * ATT
NIN = 2 * DR + 3 * GW + ATT + 2 * D
OFF_ZR = DR
OFF_Q = 2 * DR
OFF_ZA = OFF_Q + 3 * GW
OFF_G = OFF_ZA + ATT
ROPE_THETA = 10000.0
EPS = 1e-6
SCALE = HD ** -0.5
NEG = -1e30
QB = 128
LANES = 128
CT = 512
NCT = NIN // CT
A_W = 2 * DR
C_W = ATT + 2 * D

LR, B1, B2, AEPS, WD, STEP = 0.001, 0.9, 0.999, 1e-08, 0.01, 10

NSHARD_IN = NIN // NDEV
REP_NAMES = ("w_rg_a", "w_rg_x", "norm_mix", "b_in", "conv_b", "b_rg_a", "b_rg_x", "lru_lambda", "q_norm",
             "k_norm", "norm_ple", "b_ple_gate")
REP_ROWS = (NRB * RBW, NRB * RBW, D // LANES, NIN // LANES, DR // LANES, DR // LANES, DR // LANES, DR // LANES,
            NG, NG, D // LANES, D // LANES)
REP_TOTAL_ROWS = sum(REP_ROWS)
REP_ROWS_DEV = 344
BIG_NAMES = ("w_in", "w_o_rnn", "w_o_att", "w_out", "w_ple_gate", "w_ple")

VMEM_BIG = 56 * 1024 * 1024
VMEM_MID = 40 * 1024 * 1024


def _cp(sem=None, vmem=None):
    return pltpu.CompilerParams(dimension_semantics=sem, vmem_limit_bytes=vmem)


def _hbm(*arrays):
    return [pltpu.with_memory_space_constraint(a, pltpu.HBM) for a in arrays]


def _copy_together(copies):
    for cp in copies:
        cp.start()
    for cp in copies:
        cp.wait()


def _dot(a, b):
    return jnp.dot(a, b, preferred_element_type=F32)


def _dot_nt(a, b):
    return lax.dot_general(a, b, (((1,), (1,)), ((), ())), preferred_element_type=F32)


def _dot_tn(a, b):
    return lax.dot_general(a, b, (((0,), (0,)), ((), ())), preferred_element_type=F32)


def _sigmoid(x):
    return jax.nn.sigmoid(x)


def _perm(j):
    jq = j - OFF_Q // CT
    inside = (j >= OFF_Q // CT) & (j < OFF_ZA // CT)
    return jnp.where(inside, OFF_Q // CT + (jq % 3) * 3 + jq // 3, j)


PIECES = ((0, A_W // CT), (OFF_Q // CT, GW // CT), (OFF_Q // CT + 3, GW // CT), (OFF_Q // CT + 6, GW // CT),
          (OFF_ZA // CT, C_W // CT))


def _rmsnorm_fwd(x, gain, token, tm=512):
    def body(x_ref, g_ref, _token, o_ref):
        xv = x_ref[...]
        var = jnp.mean(xv * xv, axis=-1, keepdims=True)
        o_ref[...] = (xv * lax.rsqrt(var + EPS) * g_ref[...]).astype(BF16)

    return pl.pallas_call(
        body, grid=(T // tm,), name="rmsnorm_fwd",
        in_specs=[pl.BlockSpec((tm, D), lambda i: (i, 0)), pl.BlockSpec((1, D), lambda i: (0, 0)),
                  pl.BlockSpec((8, LANES), lambda i: (0, 0))],
        out_specs=pl.BlockSpec((tm, D), lambda i: (i, 0)),
        out_shape=jax.ShapeDtypeStruct((T, D), BF16),
        compiler_params=_cp(("parallel",)),
    )(*_hbm(x, gain, token))


CHIP_COLS = NIN // NCHIP


def _in_proj_chips(hn, w_rows, bias, chips, proj, token, name, tm=1024):
    n = w_rows.shape[0]

    def body(chips_ref, a_ref, w_ref, b_ref, _token, *rest):
        o_ref = rest[-1]
        o_ref[...] = (_dot_nt(a_ref[...], w_ref[0]) + b_ref[...]).astype(BF16)

    in_specs = [pl.BlockSpec((tm, D), lambda s, i, ch: (i, 0)),
                pl.BlockSpec((1, CHIP_COLS, D), lambda s, i, ch: (s, 0, 0)),
                pl.BlockSpec((1, CHIP_COLS), lambda s, i, ch: (0, ch[s])),
                pl.BlockSpec((8, LANES), lambda s, i, ch: (0, 0))]
    args = [hn, w_rows, bias, token]
    aliases = {}
    if proj is not None:
        in_specs.append(pl.BlockSpec(memory_space=pl.ANY))
        args.append(proj)
        aliases = {5: 0}
    return pl.pallas_call(
        body, name=name,
        grid_spec=pltpu.PrefetchScalarGridSpec(
            num_scalar_prefetch=1, grid=(n, T // tm), in_specs=in_specs,
            out_specs=pl.BlockSpec((tm, CHIP_COLS), lambda s, i, ch: (i, ch[s]))),
        out_shape=jax.ShapeDtypeStruct((T, NIN), BF16),
        input_output_aliases=aliases,
        compiler_params=_cp(("arbitrary", "arbitrary"), VMEM_BIG),
    )(chips, *_hbm(*args))


def _grad_x(pieces, w_bufs, chips, token, x, dx1, gain, tm=512):
    nb = len(w_bufs)

    def body(chips_ref, a_ref, q_ref, k_ref, v_ref, c_ref, *rest):
        w_hbm = rest[:nb]
        x_ref, dx1_ref, g_ref, dx_ref, dg_ref, w, sems = rest[nb + 1:]
        first = pl.program_id(0) == 0

        @pl.when(first)
        def _():
            s = 0
            copies = []
            for buf in w_hbm:
                for r in range(buf.shape[0]):
                    row = pl.multiple_of(chips_ref[s] * CHIP_COLS, 128)
                    copies.append(pltpu.make_async_copy(buf.at[r], w.at[pl.ds(row, CHIP_COLS), :], sems.at[s]))
                    s += 1
            _copy_together(copies)

        acc = _dot(a_ref[...], w[pl.ds(0, A_W), :])
        for kind, p_ref in enumerate((q_ref, k_ref, v_ref)):
            for g in range(NG):
                row = OFF_Q + (3 * g + kind) * CT
                acc = acc + _dot(p_ref[:, g * CT:(g + 1) * CT], w[pl.ds(row, CT), :])
        dn = acc + _dot(c_ref[...], w[pl.ds(OFF_ZA, C_W), :])
        xv = x_ref[...]
        rstd = lax.rsqrt(jnp.mean(xv * xv, axis=-1, keepdims=True) + EPS)
        xh = xv * rstd
        dg = jnp.sum(dn * xh, axis=0, keepdims=True)
        gd = dn * g_ref[...]
        dx_ref[...] = dx1_ref[...] + rstd * (gd - xh * jnp.mean(gd * xh, axis=-1, keepdims=True))

        @pl.when(first)
        def _():
            dg_ref[...] = dg

        @pl.when(jnp.logical_not(first))
        def _():
            dg_ref[...] += dg

    tok = lambda wd: pl.BlockSpec((tm, wd), lambda i, ch: (i, 0))
    vec = lambda: pl.BlockSpec((1, D), lambda i, ch: (0, 0))
    return pl.pallas_call(
        body, name="grad_x",
        grid_spec=pltpu.PrefetchScalarGridSpec(
            num_scalar_prefetch=1, grid=(T // tm,),
            in_specs=[tok(A_W), tok(GW), tok(GW), tok(GW), tok(C_W)] + [pl.BlockSpec(memory_space=pl.ANY)] * nb
            + [pl.BlockSpec((8, LANES), lambda i, ch: (0, 0)), tok(D), tok(D), vec()],
            out_specs=[tok(D), vec()],
            scratch_shapes=[pltpu.VMEM((NIN, D), BF16), pltpu.SemaphoreType.DMA((NCHIP,))]),
        out_shape=[jax.ShapeDtypeStruct((T, D), F32), jax.ShapeDtypeStruct((1, D), F32)],
        compiler_params=_cp(("arbitrary",), VMEM_BIG),
    )(chips, *_hbm(*pieces, *w_bufs, token, x, dx1, gain))


def _dw_in(pieces, hn):
    def body(a_ref, q_ref, k_ref, v_ref, c_ref, h_hbm, o_ref, s_ref, h):
        j = pl.program_id(0)

        @pl.when(j == 0)
        def _():
            pltpu.sync_copy(h_hbm, h)

        def step(x_ref):
            xv = x_ref[...]
            o_ref[...] = _dot_tn(xv, h[...]).astype(BF16)
            s_ref[...] = jnp.sum(xv.astype(F32), axis=0, keepdims=True)

        for x_ref, (lo, n) in zip((a_ref, q_ref, k_ref, v_ref, c_ref), PIECES):
            pl.when((j >= lo) & (j < lo + n))(lambda x_ref=x_ref: step(x_ref))

    def piece_spec(lo, n):
        return pl.BlockSpec((T, CT), lambda j: (0, jnp.clip(j - lo, 0, n - 1)))

    return pl.pallas_call(
        body, grid=(NCT,), name="dw_in",
        in_specs=[piece_spec(lo, n) for lo, n in PIECES] + [pl.BlockSpec(memory_space=pl.ANY)],
        out_specs=[pl.BlockSpec((CT, D), lambda j: (_perm(j), 0)), pl.BlockSpec((1, CT), lambda j: (0, _perm(j)))],
        out_shape=[jax.ShapeDtypeStruct((NIN, D), BF16), jax.ShapeDtypeStruct((1, NIN), F32)],
        scratch_shapes=[pltpu.VMEM((T, D), BF16)],
        compiler_params=_cp(("arbitrary",), VMEM_BIG),
    )(*_hbm(*pieces, hn))


def _mm_tn(a, b, ta, tt, name):
    m = a.shape[1]
    n = b.shape[1]
    nt = T // tt

    def body(a_ref, b_ref, o_ref, acc):
        t = pl.program_id(1)
        p = _dot_tn(a_ref[...].astype(BF16), b_ref[...].astype(BF16))
        if nt == 1:
            o_ref[...] = p.astype(BF16)
            return

        @pl.when(t == 0)
        def _():
            acc[...] = p

        @pl.when(t > 0)
        def _():
            acc[...] += p

        @pl.when(t == nt - 1)
        def _():
            o_ref[...] = acc[...].astype(BF16)

    return pl.pallas_call(
        body, grid=(m // ta, nt), name=name,
        in_specs=[pl.BlockSpec((tt, ta), lambda j, t: (t, j)), pl.BlockSpec((tt, n), lambda j, t: (t, 0))],
        out_specs=pl.BlockSpec((ta, n), lambda j, t: (j, 0)),
        out_shape=jax.ShapeDtypeStruct((m, n), BF16),
        scratch_shapes=[pltpu.VMEM((ta, n), F32)],
        compiler_params=_cp(("parallel", "arbitrary"), VMEM_MID),
    )(*_hbm(a, b))


def _row_iota():
    return lax.broadcasted_iota(jnp.int32, (S, RBW), 0)


SUBLANES = 8
N_SHIFT_BUFS = 4


class _Shifter:
    def __init__(self, bufs):
        self.bufs = bufs
        self.k = 0

    def _store(self, v, fill, front):
        b = self.bufs.at[self.k % N_SHIFT_BUFS]
        self.k += 1
        b[pl.ds(0 if front else SUBLANES + S, SUBLANES), :] = jnp.full((SUBLANES, RBW), fill, F32)
        b[pl.ds(SUBLANES, S), :] = v
        return b

    def down(self, v, ds, fill):
        b = self._store(v, fill, True)
        return [b[pl.ds(SUBLANES - d, S), :] for d in ds]

    def up(self, v, ds, fill):
        b = self._store(v, fill, False)
        return [b[pl.ds(SUBLANES + d, S), :] for d in ds]


def _shift_down(v, d, sh, fill):
    return sh.down(v, (d,), fill)[0]


def _shift_up(v, d, sh, fill):
    return sh.up(v, (d,), fill)[0]


def _scan_down(a, u, row):
    d = 1
    while d < S:
        last = 2 * d >= S
        if d < SUBLANES:
            u = a * _shift_down(u, d, row, 0.0) + u
            if not last:
                a = a * _shift_down(a, d, row, 1.0)
        else:
            u = jnp.concatenate([u[:d], a[d:] * u[:S - d] + u[d:]], axis=0)
            if not last:
                a = jnp.concatenate([a[:d], a[d:] * a[:S - d]], axis=0)
        d *= 2
    return u


def _scan_up(b, g, row):
    d = 1
    while d < S:
        last = 2 * d >= S
        if d < SUBLANES:
            g = g + b * _shift_up(g, d, row, 0.0)
            if not last:
                b = b * _shift_up(b, d, row, 0.0)
        else:
            g = jnp.concatenate([g[:S - d] + b[:S - d] * g[d:], g[S - d:]], axis=0)
            if not last:
                b = jnp.concatenate([b[:S - d] * b[d:], b[S - d:]], axis=0)
        d *= 2
    return g


def _softplus(x):
    return jnp.maximum(x, 0.0) + jnp.log1p(jnp.exp(-jnp.abs(x)))


N_SAVED = 5


def _rnn_gates(x, cw, cb, wa, ba, wx, bx, lam, row, pad, saved=None):
    xs = pad.down(x, (1, 2, 3), 0.0)
    if saved is not None:
        r, i, a, mult, xc = saved
        return xc, xc.astype(BF16), r, i, _softplus(-lam), a, mult, xs
    xc = cb + cw[3:4, :] * x
    for j in (1, 2, 3):
        xc = xc + cw[3 - j:4 - j, :] * xs[j - 1]
    xcb = xc.astype(BF16)
    r = _sigmoid(_dot(xcb, wa) + ba)
    i = _sigmoid(_dot(xcb, wx) + bx)
    sp = _softplus(-lam)
    log_a = (-LRU_C) * r * sp
    a = jnp.exp(log_a)
    mult = jnp.where(row == 0, 1.0, jnp.sqrt(jnp.tanh(-log_a) * (1.0 + a * a)))
    return xc, xcb, r, i, sp, a, mult, xs


def _rnn_fwd(proj3, conv_w, conv_b, wa, ba, wx, bx, lam, token):
    def body(x_ref, cw_ref, cb_ref, wa_ref, ba_ref, wx_ref, bx_ref, lam_ref, _token, h_ref, g_ref, pad):
        row = _row_iota()
        sh = _Shifter(pad)
        x = x_ref[0].astype(F32)
        xc, _, r, i, _, a, mult, _ = _rnn_gates(x, cw_ref[...], cb_ref[...], wa_ref[0], ba_ref[...],
                                             wx_ref[0], bx_ref[...], lam_ref[...], row, sh)
        for k, val in enumerate((r, i, a, mult, xc)):
            g_ref[k, 0] = val
        h_ref[0] = _scan_down(a, mult * (i * xc), sh)

    vec = lambda: pl.BlockSpec((1, RBW), lambda b, n: (0, n))
    mat = lambda: pl.BlockSpec((1, RBW, RBW), lambda b, n: (n, 0, 0))
    return pl.pallas_call(
        body, grid=(BL, NRB), name="rnn_fwd",
        in_specs=[pl.BlockSpec((1, S, RBW), lambda b, n: (b, 0, n)),
                  pl.BlockSpec((CONVW, RBW), lambda b, n: (0, n)),
                  vec(), mat(), vec(), mat(), vec(), vec(), pl.BlockSpec((8, LANES), lambda b, n: (0, 0))],
        out_specs=[pl.BlockSpec((1, S, RBW), lambda b, n: (b, 0, n)),
                   pl.BlockSpec((N_SAVED, 1, S, RBW), lambda b, n: (0, b, 0, n))],
        out_shape=[jax.ShapeDtypeStruct((BL, S, DR), F32), jax.ShapeDtypeStruct((N_SAVED, BL, S, DR), F32)],
        scratch_shapes=[pltpu.VMEM((N_SHIFT_BUFS, S + 2 * SUBLANES, RBW), F32)],
        compiler_params=_cp(("parallel", "parallel"), VMEM_MID),
    )(*_hbm(proj3, conv_w, conv_b, wa, ba, wx, bx, lam, token))


def _rnn_bwd(proj3, h3, dh3, gates, slab_a3, conv_w, conv_b, wa, ba, wx, bx, lam, token):
    def body(x_ref, h_ref, dh_ref, g_ref, cw_ref, cb_ref, wa_ref, ba_ref, wx_ref, bx_ref, lam_ref, _alias, _token,
             dx_ref, dcw_ref, dcb_ref, dwa_ref, dba_ref, dwx_ref, dbx_ref, dlam_ref, pad):
        row = _row_iota()
        sh = _Shifter(pad)
        x = x_ref[0].astype(F32)
        cw = cw_ref[...]
        wa_v = wa_ref[0]
        wx_v = wx_ref[0]
        lam_v = lam_ref[...]
        xc, xcb, r, i, sp, a, mult, xs = _rnn_gates(x, cw, cb_ref[...], wa_v, ba_ref[...], wx_v, bx_ref[...], lam_v,
                                                    row, sh, [g_ref[k, 0] for k in range(N_SAVED)])
        h = h_ref[0]
        g = _scan_up(_shift_up(a, 1, sh, 0.0), dh_ref[0], sh)
        da = g * _shift_down(h, 1, sh, 0.0)
        dmult = jnp.where(row == 0, 0.0, g * (i * xc))
        gm = g * mult
        di = gm * xc
        dxc = gm * i
        dlog_a = da * a - dmult * (a * a) / mult
        dr = dlog_a * ((-LRU_C) * sp)
        dsp = jnp.sum(dlog_a * ((-LRU_C) * r), axis=0, keepdims=True)
        dlam = dsp * (-_sigmoid(-lam_v))
        dpa = dr * r * (1.0 - r)
        dpx = di * i * (1.0 - i)
        dpab = dpa.astype(BF16)
        dpxb = dpx.astype(BF16)
        dwa = _dot_tn(xcb, dpab)
        dwx = _dot_tn(xcb, dpxb)
        dba = jnp.sum(dpa, axis=0, keepdims=True)
        dbx = jnp.sum(dpx, axis=0, keepdims=True)
        dxc = dxc + _dot_nt(dpab, wa_v) + _dot_nt(dpxb, wx_v)
        dcb = jnp.sum(dxc, axis=0, keepdims=True)
        dx = cw[3:4, :] * dxc
        dcw_rows = [None] * CONVW
        dcw_rows[3] = jnp.sum(dxc * x, axis=0, keepdims=True)
        dxc_up = sh.up(dxc, (1, 2, 3), 0.0)
        for j in (1, 2, 3):
            dx = dx + cw[3 - j:4 - j, :] * dxc_up[j - 1]
            dcw_rows[3 - j] = jnp.sum(dxc * xs[j - 1], axis=0, keepdims=True)
        dx_ref[0] = dx.astype(BF16)
        dcw = jnp.concatenate(dcw_rows, axis=0)
        first = pl.program_id(1) == 0

        @pl.when(first)
        def _():
            dcw_ref[...] = dcw
            dcb_ref[...] = dcb
            dwa_ref[0] = dwa
            dba_ref[...] = dba
            dwx_ref[0] = dwx
            dbx_ref[...] = dbx
            dlam_ref[...] = dlam

        @pl.when(jnp.logical_not(first))
        def _():
            dcw_ref[...] += dcw
            dcb_ref[...] += dcb
            dwa_ref[0] += dwa
            dba_ref[...] += dba
            dwx_ref[0] += dwx
            dbx_ref[...] += dbx
            dlam_ref[...] += dlam

    slab = lambda: pl.BlockSpec((1, S, RBW), lambda n, b: (b, 0, n))
    vec = lambda: pl.BlockSpec((1, RBW), lambda n, b: (0, n))
    mat = lambda: pl.BlockSpec((1, RBW, RBW), lambda n, b: (n, 0, 0))
    taps = lambda: pl.BlockSpec((CONVW, RBW), lambda n, b: (0, n))
    vshape = jax.ShapeDtypeStruct((1, DR), F32)
    mshape = jax.ShapeDtypeStruct((NRB, RBW, RBW), F32)
    return pl.pallas_call(
        body, grid=(NRB, BL), name="rnn_bwd",
        in_specs=[slab(), slab(), slab(), pl.BlockSpec((N_SAVED, 1, S, RBW), lambda n, b: (0, b, 0, n)),
                  taps(), vec(), mat(), vec(), mat(), vec(), vec(),
                  pl.BlockSpec(memory_space=pl.ANY), pl.BlockSpec((8, LANES), lambda n, b: (0, 0))],
        out_specs=[slab(), taps(), vec(), mat(), vec(), mat(), vec(), vec()],
        out_shape=[jax.ShapeDtypeStruct((BL, S, A_W), BF16), jax.ShapeDtypeStruct((CONVW, DR), F32),
                   vshape, mshape, vshape, mshape, vshape, vshape],
        input_output_aliases={11: 0},
        scratch_shapes=[pltpu.VMEM((N_SHIFT_BUFS, S + 2 * SUBLANES, RBW), F32)],
        compiler_params=_cp(("parallel", "arbitrary"), 48 * 1024 * 1024),
    )(*_hbm(proj3, h3, dh3, gates, conv_w, conv_b, wa, ba, wx, bx, lam, slab_a3, token))


NQB = S // QB


def _rms_head(t, gain):
    rstd = lax.rsqrt(jnp.mean(t * t, axis=-1, keepdims=True) + EPS)
    return t * rstd * gain


def _rope(t, cs, sn):
    return t * cs + pltpu.roll(t, HD // 2, 1) * sn


def _rope_t(dy, cs, sn):
    return dy * cs - pltpu.roll(dy, HD // 2, 1) * sn


def _bdot_nt(a, b):
    return lax.dot_general(a, b, (((2,), (2,)), ((0,), (0,))), preferred_element_type=F32)


def _bdot(a, b):
    return lax.dot_general(a, b, (((2,), (1,)), ((0,), (0,))), preferred_element_type=F32)


def _bdot_tn(a, b):
    return lax.dot_general(a, b, (((1,), (1,)), ((0,), (0,))), preferred_element_type=F32)


STRIDE_MAX = 4


def _permute(buf, x, dil, dst, off=0):
    ln = S // dil
    if dil == 1:
        dst[pl.ds(off, S), :] = x.astype(dst.dtype)
        return
    buf[0] = x
    if dil <= STRIDE_MAX:
        for c in range(dil):
            dst[pl.ds(off + c * ln, ln), :] = buf.at[0][pl.ds(c, ln, stride=dil), :].astype(dst.dtype)
        return
    f, r = STRIDE_MAX, dil // STRIDE_MAX
    part = S // f
    for c1 in range(f):
        buf.at[1][pl.ds(c1 * part, part), :] = buf.at[0][pl.ds(c1, part, stride=f), :]
    for c1 in range(f):
        for c2 in range(r):
            dst[pl.ds(off + (c1 + f * c2) * ln, ln), :] = (
                buf.at[1][pl.ds(c1 * part + c2, ln, stride=r), :].astype(dst.dtype))


def _unpermute(buf, xp, dil, dst):
    ln = S // dil
    if dil == 1:
        dst[...] = xp
        return
    if dil <= STRIDE_MAX:
        for c in range(dil):
            dst[pl.ds(c, ln, stride=dil), :] = xp[c * ln:(c + 1) * ln]
        return
    f, r = STRIDE_MAX, dil // STRIDE_MAX
    part = S // f
    for c1 in range(f):
        for c2 in range(r):
            c = c1 + f * c2
            buf.at[1][pl.ds(c1 * part + c2, ln, stride=r), :] = xp[c * ln:(c + 1) * ln]
    for c1 in range(f):
        dst[pl.ds(c1, part, stride=f), :] = buf[1, pl.ds(c1 * part, part), :]


def _blocks3(ref, off=0):
    return ref[pl.ds(off, S), :].reshape(NQB, QB, HD)


def _att_prep(q_ref, k_ref, v_ref, cos_ref, sin_ref, qn, kn, dil, nat, qs, ksp, vsp):
    cs = cos_ref[...]
    sn = sin_ref[...]
    zero = jnp.zeros((QB, HD), BF16)
    ksp[pl.ds(0, QB), :] = zero
    vsp[pl.ds(0, QB), :] = zero
    _permute(nat, _rope(_rms_head(q_ref[0].astype(F32), qn), cs, sn), dil, qs)
    _permute(nat, _rope(_rms_head(k_ref[0].astype(F32), kn), cs, sn), dil, ksp, QB)
    _permute(nat, v_ref[0].astype(F32), dil, vsp, QB)


def _att_scores(qs, ksp, dil):
    nb = S // dil // QB
    q3 = _blocks3(qs)
    shape = (NQB, QB, QB)
    qi = lax.broadcasted_iota(jnp.int32, shape, 1)
    kj = lax.broadcasted_iota(jnp.int32, shape, 2)
    s_c = jnp.where(qi >= kj, _bdot_nt(q3, _blocks3(ksp, QB)) * SCALE, NEG)
    if nb == 1:
        return q3, s_c, None
    jj = lax.broadcasted_iota(jnp.int32, shape, 0)
    ok = (kj >= qi) & ((jj & (nb - 1)) != 0)
    s_p = jnp.where(ok, _bdot_nt(q3, _blocks3(ksp)) * SCALE, NEG)
    return q3, s_c, s_p


def _qkv_spec(kind, g):
    base = OFF_Q // HD + (3 * g + kind) * NH
    return pl.BlockSpec((1, S, HD), lambda b, h: (b, 0, base + h))


def _attn_fwd(proj3, cos_t, sin_t, q_norm, k_norm):
    def body(*refs):
        qkv_refs = refs[:9]
        (cos_ref, sin_ref, qn_ref, kn_ref, att_ref, lse_ref, w_ref, qp_ref, kp_ref, vp_ref,
         nat, qs, ksp, vsp, og) = refs[9:]
        for g, (window, dil) in enumerate(PATTERNS):
            q_ref, k_ref, v_ref = qkv_refs[3 * g:3 * g + 3]
            _att_prep(q_ref, k_ref, v_ref, cos_ref, sin_ref, qn_ref[g:g + 1, :], kn_ref[g:g + 1, :], dil,
                      nat, qs, ksp, vsp)
            qp_ref[g, 0] = qs[...]
            kp_ref[g, 0] = ksp[pl.ds(QB, S), :]
            vp_ref[g, 0] = vsp[pl.ds(QB, S), :]
            _, s_c, s_p = _att_scores(qs, ksp, dil)
            m = jnp.max(s_c, axis=-1, keepdims=True)
            if s_p is not None:
                m = jnp.maximum(m, jnp.max(s_p, axis=-1, keepdims=True))
            e_c = jnp.exp(s_c - m)
            den = jnp.sum(e_c, axis=-1, keepdims=True)
            o = _bdot(e_c.astype(BF16), _blocks3(vsp, QB))
            if s_p is not None:
                e_p = jnp.exp(s_p - m)
                den = den + jnp.sum(e_p, axis=-1, keepdims=True)
                o = o + _bdot(e_p.astype(BF16), _blocks3(vsp))
            _unpermute(nat, (o / den).reshape(S, HD), dil, og.at[g])
            _unpermute(nat, jnp.broadcast_to(m + jnp.log(den), (NQB, QB, HD)).reshape(S, HD), dil,
                       lse_ref.at[g, 0])
        l0 = lse_ref[0, 0]
        l1 = lse_ref[1, 0]
        l2 = lse_ref[2, 0]
        mx = jnp.maximum(jnp.maximum(l0, l1), l2)
        e0 = jnp.exp(l0 - mx)
        e1 = jnp.exp(l1 - mx)
        e2 = jnp.exp(l2 - mx)
        inv = 1.0 / (e0 + e1 + e2)
        w0 = e0 * inv
        w1 = e1 * inv
        w2 = e2 * inv
        w_ref[0, 0] = w0
        w_ref[1, 0] = w1
        w_ref[2, 0] = w2
        att_ref[0] = w0 * og[0] + w1 * og[1] + w2 * og[2]

    in_specs = [_qkv_spec(kind, g) for g in range(NG) for kind in range(3)]
    in_specs += [pl.BlockSpec((S, HD), lambda b, h: (0, 0)), pl.BlockSpec((S, HD), lambda b, h: (0, 0)),
                 pl.BlockSpec((NG, HD), lambda b, h: (0, 0)), pl.BlockSpec((NG, HD), lambda b, h: (0, 0))]
    stat = lambda: pl.BlockSpec((NG, 1, S, HD), lambda b, h: (0, b, 0, h))
    return pl.pallas_call(
        body, grid=(BL, NH), name="attn_fwd",
        in_specs=in_specs,
        out_specs=[pl.BlockSpec((1, S, HD), lambda b, h: (b, 0, h)), stat(), stat(), stat(), stat(), stat()],
        out_shape=[jax.ShapeDtypeStruct((BL, S, ATT), F32),
                   jax.ShapeDtypeStruct((NG, BL, S, ATT), F32),
                   jax.ShapeDtypeStruct((NG, BL, S, ATT), F32)]
        + [jax.ShapeDtypeStruct((NG, BL, S, ATT), BF16)] * 3,
        scratch_shapes=[pltpu.VMEM((2, S, HD), F32), pltpu.VMEM((S, HD), BF16), pltpu.VMEM((S + QB, HD), BF16),
                        pltpu.VMEM((S + QB, HD), BF16), pltpu.VMEM((NG, S, HD), F32)],
        compiler_params=_cp(("parallel", "parallel"), VMEM_BIG),
    )(*_hbm(*([proj3] * 9), cos_t, sin_t, q_norm, k_norm))


def _attn_bwd(proj3, cos_t, sin_t, q_norm, k_norm, lse, wts, qkv_p, datt3, sbar3):
    def norm_rope_bwd(dpost, raw, gain, cs, sn):
        dn = _rope_t(dpost, cs, sn)
        rstd = lax.rsqrt(jnp.mean(raw * raw, axis=-1, keepdims=True) + EPS)
        xh = raw * rstd
        dgain = jnp.sum(dn * xh, axis=0, keepdims=True)
        gd = dn * gain
        draw = rstd * (gd - xh * jnp.mean(gd * xh, axis=-1, keepdims=True))
        return draw, dgain

    def group_body(g, refs, first):
        dil = PATTERNS[g][1]
        (q_ref, k_ref, qp_ref, kp_ref, vp_ref, cos_ref, sin_ref, qn_ref, kn_ref, lse_ref, w_ref, datt_ref,
         sbar_ref, dq_ref, dk_ref, dv_ref, dqn_ref, dkn_ref, nat, ksp, vsp, dos, cvp, lsp, acc) = refs
        qn = qn_ref[g:g + 1, :]
        kn = kn_ref[g:g + 1, :]
        cs = cos_ref[...]
        sn = sin_ref[...]
        qs = qp_ref.at[0, 0]
        zero = jnp.zeros((QB, HD), BF16)
        ksp[pl.ds(0, QB), :] = zero
        vsp[pl.ds(0, QB), :] = zero
        ksp[pl.ds(QB, S), :] = kp_ref[0, 0]
        vsp[pl.ds(QB, S), :] = vp_ref[0, 0]
        wv = w_ref[0, 0]
        _permute(nat, wv * datt_ref[0], dil, dos)
        _permute(nat, wv * sbar_ref[0], dil, cvp)
        _permute(nat, lse_ref[0, 0], dil, lsp)
        q3, s_c, s_p = _att_scores(qs, ksp, dil)
        do3 = _blocks3(dos)
        lse3 = _blocks3(lsp)[:, :, 0:1]
        cv3 = _blocks3(cvp)[:, :, 0:1]
        p_c = jnp.exp(s_c - lse3)
        ds_c = (p_c * (_bdot_nt(do3, _blocks3(vsp, QB)) - cv3)).astype(BF16)
        dq = _bdot(ds_c, _blocks3(ksp, QB))
        acc[0] = _bdot_tn(ds_c, q3).reshape(S, HD)
        acc[1] = _bdot_tn(p_c.astype(BF16), do3).reshape(S, HD)
        if s_p is not None:
            p_p = jnp.exp(s_p - lse3)
            ds_p = (p_p * (_bdot_nt(do3, _blocks3(vsp)) - cv3)).astype(BF16)
            dq = dq + _bdot(ds_p, _blocks3(ksp))
            early = pl.ds(0, S - QB)
            acc[0, early, :] += _bdot_tn(ds_p, q3).reshape(S, HD)[QB:]
            acc[1, early, :] += _bdot_tn(p_p.astype(BF16), do3).reshape(S, HD)[QB:]
        _unpermute(nat, (dq * SCALE).reshape(S, HD), dil, nat.at[0])
        draw, dqn = norm_rope_bwd(nat[0], q_ref[0].astype(F32), qn, cs, sn)
        dq_ref[0] = draw.astype(BF16)
        _unpermute(nat, acc[0] * SCALE, dil, nat.at[0])
        draw, dkn = norm_rope_bwd(nat[0], k_ref[0].astype(F32), kn, cs, sn)
        dk_ref[0] = draw.astype(BF16)
        _unpermute(nat, acc[1], dil, nat.at[0])
        dv_ref[0] = nat[0].astype(BF16)
        dqn8 = jnp.broadcast_to(dqn, (SUBLANES, HD))
        dkn8 = jnp.broadcast_to(dkn, (SUBLANES, HD))

        @pl.when(first)
        def _():
            dqn_ref[0] = dqn8
            dkn_ref[0] = dkn8

        @pl.when(jnp.logical_not(first))
        def _():
            dqn_ref[0] += dqn8
            dkn_ref[0] += dkn8

    def body(*refs):
        group = pl.program_id(0)
        first = (pl.program_id(1) == 0) & (pl.program_id(2) == 0)
        for g in range(NG):
            pl.when(group == g)(lambda g=g: group_body(g, refs, first))

    def raw_spec(kind):
        return pl.BlockSpec((1, S, HD), lambda g, b, h: (b, 0, OFF_Q // HD + (3 * g + kind) * NH + h))

    full = lambda r: pl.BlockSpec((r, HD), lambda g, b, h: (0, 0))
    stat = lambda: pl.BlockSpec((1, 1, S, HD), lambda g, b, h: (g, b, 0, h))
    slab = lambda: pl.BlockSpec((1, S, HD), lambda g, b, h: (b, 0, h))
    out_slab = lambda: pl.BlockSpec((1, S, HD), lambda g, b, h: (b, 0, g * NH + h))
    gains = lambda: pl.BlockSpec((1, SUBLANES, HD), lambda g, b, h: (g, 0, 0))
    big = jax.ShapeDtypeStruct((BL, S, GW), BF16)
    vecs = jax.ShapeDtypeStruct((NG, SUBLANES, HD), F32)
    return pl.pallas_call(
        body, grid=(NG, BL, NH), name="attn_bwd",
        in_specs=[raw_spec(0), raw_spec(1), stat(), stat(), stat(), full(S), full(S), full(NG), full(NG),
                  stat(), stat(), slab(), slab()],
        out_specs=[out_slab(), out_slab(), out_slab(), gains(), gains()],
        out_shape=[big, big, big, vecs, vecs],
        scratch_shapes=[pltpu.VMEM((2, S, HD), F32), pltpu.VMEM((S + QB, HD), BF16),
                        pltpu.VMEM((S + QB, HD), BF16), pltpu.VMEM((S, HD), BF16), pltpu.VMEM((S, HD), F32),
                        pltpu.VMEM((S, HD), F32), pltpu.VMEM((2, S, HD), F32)],
        compiler_params=_cp(("arbitrary", "arbitrary", "arbitrary"), VMEM_BIG),
    )(*_hbm(proj3, proj3, *qkv_p, cos_t, sin_t, q_norm, k_norm, lse, wts, datt3, sbar3))


def _tail(x, proj, h, att, p, tgt, w_o_rnn, w_o_att_t, w_out, w_pg, w_ple_t, norm_ple, b_pg, tm=256):
    nt = T // tm
    inv_d = 1.0 / D

    def body(x_ref, h_ref, zr_ref, att_ref, za_ref, g0a_ref, g0b_ref, g1a_ref, g1b_ref, p_ref, tgt_ref,
             np_ref, bpg_ref, wor_hbm, woa_hbm, wout_hbm, wpg_hbm, wple_hbm,
             dx1_ref, merged_ref, n1_ref, dpre_ref, dpe_ref, dyr_ref, dya_ref, slab_a_ref, slab_c_ref, dh_ref,
             datt_ref, sbar_ref, yrnn_ref, yatt_ref, loss_ref, dnp_ref, dbpg_ref,
             wor, woa, wout, wpg, wple, sems):
        first = pl.program_id(0) == 0

        @pl.when(first)
        def _():
            pairs = ((wor_hbm, wor), (woa_hbm, woa), (wout_hbm, wout), (wpg_hbm, wpg), (wple_hbm, wple))
            _copy_together([pltpu.make_async_copy(src, dst, sems.at[k]) for k, (src, dst) in enumerate(pairs)])

        xv = x_ref[...]
        hv = h_ref[...]
        zr = zr_ref[...].astype(F32)
        av = att_ref[...]
        za = za_ref[...].astype(F32)
        szr = _sigmoid(zr)
        silu_r = zr * szr
        yrnn_b = (hv * silu_r).astype(BF16)
        sza = _sigmoid(za)
        silu_a = za * sza
        yatt_b = (av * silu_a).astype(BF16)
        yrnn_ref[...] = yrnn_b
        yatt_ref[...] = yatt_b
        yr = _dot(yrnn_b, wor[...])
        ya = _dot_nt(yatt_b, woa[...])
        g0 = _sigmoid(jnp.concatenate([g0a_ref[...], g0b_ref[...]], axis=1).astype(F32))
        g1 = _sigmoid(jnp.concatenate([g1a_ref[...], g1b_ref[...]], axis=1).astype(F32))
        merged_b = (g0 * yr + g1 * ya).astype(BF16)
        merged_ref[...] = merged_b
        x1 = xv + _dot(merged_b, wout[...])
        rstd = lax.rsqrt(jnp.mean(x1 * x1, axis=-1, keepdims=True) + EPS)
        xh = x1 * rstd
        npl = np_ref[...]
        n1_b = (xh * npl).astype(BF16)
        n1_ref[...] = n1_b
        pg = _sigmoid(_dot(n1_b, wpg[...]) + bpg_ref[...])
        pe = _dot_nt(p_ref[...].astype(BF16), wple[...])
        err = x1 + pg * pe - tgt_ref[...]
        loss_t = 0.5 * inv_d * jnp.sum(err * err)
        dy = err * inv_d
        dpe_ref[...] = (dy * pg).astype(BF16)
        dpre = dy * pe * pg * (1.0 - pg)
        dpre_b = dpre.astype(BF16)
        dpre_ref[...] = dpre_b
        dn1 = _dot_nt(dpre_b, wpg[...])
        dnp = jnp.sum(dn1 * xh, axis=0, keepdims=True)
        dbpg = jnp.sum(dpre, axis=0, keepdims=True)
        gd = dn1 * npl
        dx1 = dy + rstd * (gd - xh * jnp.mean(gd * xh, axis=-1, keepdims=True))
        dx1_ref[...] = dx1
        dmerged = _dot_nt(dx1.astype(BF16), wout[...])
        dyr_b = (dmerged * g0).astype(BF16)
        dya_b = (dmerged * g1).astype(BF16)
        dyr_ref[...] = dyr_b
        dya_ref[...] = dya_b
        slab_c_ref[:, ATT:ATT + D] = (dmerged * yr * g0 * (1.0 - g0)).astype(BF16)
        slab_c_ref[:, ATT + D:ATT + 2 * D] = (dmerged * ya * g1 * (1.0 - g1)).astype(BF16)
        dyrnn = _dot_nt(dyr_b, wor[...])
        dyatt = _dot(dya_b, woa[...])
        dh_ref[...] = dyrnn * silu_r
        slab_a_ref[...] = (dyrnn * hv * szr * (1.0 + zr * (1.0 - szr))).astype(BF16)
        datt = dyatt * silu_a
        datt_ref[...] = datt
        slab_c_ref[:, 0:ATT] = (dyatt * av * sza * (1.0 + za * (1.0 - sza))).astype(BF16)
        da = datt * av
        for hh in range(NH):
            seg = slice(hh * HD, (hh + 1) * HD)
            sbar_ref[:, seg] = jnp.broadcast_to(jnp.sum(da[:, seg], axis=-1, keepdims=True), (tm, HD))

        @pl.when(first)
        def _():
            loss_ref[...] = jnp.full((8, LANES), loss_t, F32)
            dnp_ref[...] = dnp
            dbpg_ref[...] = dbpg

        @pl.when(jnp.logical_not(first))
        def _():
            loss_ref[...] += jnp.full((8, LANES), loss_t, F32)
            dnp_ref[...] += dnp
            dbpg_ref[...] += dbpg

    tok = lambda w: pl.BlockSpec((tm, w), lambda i: (i, 0))
    col = lambda w, blk: pl.BlockSpec((tm, w), lambda i: (i, blk))
    vec = lambda: pl.BlockSpec((1, D), lambda i: (0, 0))
    hbm = lambda: pl.BlockSpec(memory_space=pl.ANY)
    gb = OFF_G // 512
    in_specs = [tok(D), tok(DR), col(DR, 1), tok(ATT), col(ATT, OFF_ZA // ATT),
                col(512, gb), col(512, gb + 1), col(512, gb + 2), col(512, gb + 3),
                tok(PLE), tok(D), vec(), vec(), hbm(), hbm(), hbm(), hbm(), hbm()]
    sh = lambda w, dt: jax.ShapeDtypeStruct((T, w), dt)
    out_shape = [sh(D, F32), sh(D, BF16), sh(D, BF16), sh(D, BF16), sh(D, BF16), sh(D, BF16), sh(D, BF16),
                 sh(A_W, BF16), sh(C_W, BF16), sh(DR, F32), sh(ATT, F32), sh(ATT, F32),
                 sh(DR, BF16), sh(ATT, BF16),
                 jax.ShapeDtypeStruct((8, LANES), F32), jax.ShapeDtypeStruct((1, D), F32),
                 jax.ShapeDtypeStruct((1, D), F32)]
    out_specs = [tok(D), tok(D), tok(D), tok(D), tok(D), tok(D), tok(D), col(DR, 1), tok(C_W), tok(DR),
                 tok(ATT), tok(ATT), tok(DR), tok(ATT),
                 pl.BlockSpec((8, LANES), lambda i: (0, 0)), vec(), vec()]
    return pl.pallas_call(
        body, grid=(nt,), name="tail_fwd_bwd",
        in_specs=in_specs, out_specs=out_specs, out_shape=out_shape,
        scratch_shapes=[pltpu.VMEM((DR, D), BF16), pltpu.VMEM((D, ATT), BF16), pltpu.VMEM((D, D), BF16),
                        pltpu.VMEM((D, D), BF16), pltpu.VMEM((D, PLE), BF16), pltpu.SemaphoreType.DMA((5,))],
        compiler_params=_cp(("arbitrary",), VMEM_BIG),
    )(*_hbm(x, h, proj, att, proj, proj, proj, proj, proj, p, tgt, norm_ple, b_pg, w_o_rnn, w_o_att_t, w_out, w_pg,
            w_ple_t))


def _rope_tables():
    pos = jnp.arange(S, dtype=F32)
    inv_freq = ROPE_THETA ** (-jnp.arange(0, HD, 2, dtype=F32) / HD)
    ang = pos[:, None] * inv_freq[None, :]
    cos, sin = jnp.cos(ang), jnp.sin(ang)
    return jnp.concatenate([cos, cos], axis=1), jnp.concatenate([-sin, sin], axis=1)


def _local_step(x, p, tgt, project, other_weights, norm_mix, conv_b,
                w_rg_a, b_rg_a, w_rg_x, b_rg_x, lam, q_norm, k_norm, norm_ple, b_pg, start_reduce=None,
                entry_token=None):
    if start_reduce is None:
        start_reduce = lambda arrs, tag: (jnp.zeros((8, LANES), F32), arrs)
    if entry_token is None:
        entry_token = jnp.zeros((8, LANES), F32)
    cos_t, sin_t = _rope_tables()
    wa_b = w_rg_a.astype(BF16)
    wx_b = w_rg_x.astype(BF16)

    hn = _rmsnorm_fwd(x, norm_mix, entry_token)
    proj, w_bufs, chips, conv_w, token = project(hn)
    proj3 = proj.reshape(BL, S, NIN)
    h3, gates = _rnn_fwd(proj3, conv_w, conv_b, wa_b, b_rg_a, wx_b, b_rg_x, lam, token)
    att3, lse, wts, *qkv_p = _attn_fwd(proj3, cos_t, sin_t, q_norm, k_norm)
    w_o_rnn, w_o_att_t, w_out, w_pg, w_ple_t = other_weights(att3)
    (dx1, merged, n1, dpre, dpe, dyr, dya, slab_a, slab_c, dh, datt, sbar, yrnn, yatt, loss8, dnp, dbpg) = _tail(
        x, proj, h3.reshape(T, DR), att3.reshape(T, ATT), p, tgt, w_o_rnn, w_o_att_t, w_out, w_pg, w_ple_t,
        norm_ple, b_pg)

    token, pending_out = start_reduce([
        _mm_tn(yrnn, dyr, 640, T, "dw_o_rnn"),
        _mm_tn(dya, yatt, 512, T, "dw_o_att_t"),
        _mm_tn(merged, dx1, 512, T // 2, "dw_out"),
        _mm_tn(n1, dpre, 512, T, "dw_ple_gate"),
        _mm_tn(dpe, p, 512, T, "dw_ple_t")], "out")

    slab_a3, dcw, dcb, dwa, dba, dwx, dbx, dlam = _rnn_bwd(
        proj3, h3, dh.reshape(BL, S, DR), gates, slab_a.reshape(BL, S, A_W), conv_w, conv_b, wa_b, b_rg_a, wx_b, b_rg_x, lam,
        token)
    datt3 = datt.reshape(BL, S, ATT)
    sbar3 = sbar.reshape(BL, S, ATT)
    *slabs, dqn, dkn = _attn_bwd(proj3, cos_t, sin_t, q_norm, k_norm, lse, wts, qkv_p, datt3, sbar3)
    pieces =[slab_a3.reshape(T, A_W)] + [t.reshape(T, GW) for t in slabs] + [slab_c]
    dw_in_t, db_in = _dw_in(pieces, hn)
    token, pending_in = start_reduce([dw_in_t], "in")
    grad_x, dnm = _grad_x(pieces, w_bufs, chips, token, x, dx1, norm_mix)

    small = dict(w_rg_a=dwa, w_rg_x=dwx, norm_mix=dnm, b_in=db_in, conv_b=dcb, b_rg_a=dba, b_rg_x=dbx,
                 lru_lambda=dlam, q_norm=dqn, k_norm=dkn, norm_ple=dnp, b_ple_gate=dbpg, conv_w=dcw, loss=loss8)
    return grad_x, pending_out, pending_in, small


MESH = pl.DeviceIdType.MESH
HBM_SPEC = pl.BlockSpec(memory_space=pl.ANY)


def _my_pos():
    return lax.axis_index("x"), lax.axis_index("y"), lax.axis_index("c")


def _flip(pos, k):
    x, y, c = pos
    return (1 - x if k & 4 else x, 1 - y if k & 2 else y, 1 - c if k & 1 else c)


def _lin(pos):
    return 4 * pos[0] + 2 * pos[1] + pos[2]


def _chip(pos):
    return 2 * pos[0] + pos[1]


def _exchange_within_chip(parts, name):
    na = len(parts)

    def body(*refs):
        a_refs = refs[:na]
        recv_refs = refs[na:2 * na]
        send_sems, recv_sems = refs[2 * na:]
        me = _my_pos()
        c = me[2]
        sibling = _flip(me, 1)
        remote = []
        for i in range(na):
            for q in range(NCHIP):
                rc = pltpu.make_async_remote_copy(
                    src_ref=a_refs[i].at[q, 1 - c], dst_ref=recv_refs[i].at[q],
                    send_sem=send_sems.at[NCHIP * i + q], recv_sem=recv_sems.at[NCHIP * i + q],
                    device_id=sibling, device_id_type=MESH)
                rc.start()
                remote.append(rc)
        for rc in remote:
            rc.wait_recv()
        for rc in remote:
            rc.wait_send()

    return pl.pallas_call(
        body, name=name, out_shape=[jax.ShapeDtypeStruct((NCHIP,) + a.shape[2:], a.dtype) for a in parts],
        in_specs=[HBM_SPEC] * na, out_specs=[HBM_SPEC] * na,
        scratch_shapes=[pltpu.SemaphoreType.DMA((NCHIP * na,)), pltpu.SemaphoreType.DMA((NCHIP * na,))],
    )(*parts)


HBM_ONLY = pl.BlockSpec(memory_space=pltpu.HBM)
SEM_SPEC = pl.BlockSpec(memory_space=pltpu.SEMAPHORE)
SPLIT_COPY = pltpu.CompilerParams(has_side_effects=pltpu.SideEffectType.DATAFLOW_SIDE_EFFECTING)


def _chip_peers(me):
    return [_flip(me, 4), _flip(me, 2), _flip(me, 6)]


def _between_chips_start(parts, name):
    na = len(parts)

    def body(*refs):
        a_refs = refs[:na]
        land_refs = refs[na:2 * na]
        send_sems, recv_sems = refs[2 * na], refs[2 * na + 1]
        token = refs[-1]
        me = _my_pos()
        myq = _chip(me)
        for i in range(na):
            for j, peer in enumerate(_chip_peers(me)):
                pltpu.make_async_remote_copy(
                    src_ref=a_refs[i].at[_chip(peer)], dst_ref=land_refs[i].at[myq],
                    send_sem=send_sems.at[3 * i + j], recv_sem=recv_sems.at[3 * i + j],
                    device_id=peer, device_id_type=MESH).start()
        token[...] = jnp.zeros_like(token)

    hbm = [pltpu.HBM(a.shape, a.dtype) for a in parts]
    srcs = [pltpu.with_memory_space_constraint(a, pltpu.HBM) for a in parts]
    lands = [pltpu.with_memory_space_constraint(lax.empty(a.shape, a.dtype), pltpu.HBM) for a in parts]
    res = pl.pallas_call(
        body, name=name,
        out_shape=(pltpu.SemaphoreType.DMA((3 * na,)), pltpu.SemaphoreType.DMA((3 * na,)), *hbm, *hbm,
                   jax.ShapeDtypeStruct((8, LANES), F32)),
        in_specs=[HBM_ONLY] * (2 * na),
        out_specs=(SEM_SPEC, SEM_SPEC, *([HBM_ONLY] * (2 * na)), pl.BlockSpec(memory_space=pltpu.VMEM)),
        input_output_aliases={i: 2 + i for i in range(2 * na)},
        compiler_params=SPLIT_COPY,
    )(*srcs, *lands)
    return res[-1], (res[0], res[1], list(res[2:2 + na]), list(res[2 + na:2 + 2 * na]))


def _between_chips_wait(pending, after, name):
    send_sems, recv_sems, parts, lands = pending
    na = len(parts)

    def body(*refs):
        a_refs = refs[:na]
        land_refs = refs[na:2 * na]
        send_sems, recv_sems = refs[2 * na], refs[2 * na + 1]
        me = _my_pos()
        for i in range(na):
            for j, peer in enumerate(_chip_peers(me)):
                cp = pltpu.make_async_remote_copy(
                    src_ref=a_refs[i].at[_chip(peer)], dst_ref=land_refs[i].at[_chip(peer)],
                    send_sem=send_sems.at[3 * i + j], recv_sem=recv_sems.at[3 * i + j],
                    device_id=peer, device_id_type=MESH)
                cp.wait_send()
                cp.wait_recv()

    hbm = [pltpu.HBM(a.shape, a.dtype) for a in parts]
    res = pl.pallas_call(
        body, name=name, out_shape=(*hbm, *hbm),
        in_specs=[HBM_ONLY] * (2 * na) + [SEM_SPEC, SEM_SPEC, pl.BlockSpec(memory_space=pl.ANY)],
        out_specs=[HBM_ONLY] * (2 * na),
        input_output_aliases={i: i for i in range(2 * na)},
        compiler_params=SPLIT_COPY,
    )(*parts, *lands, send_sems, recv_sems, after)
    return list(res[:na]), list(res[na:])


def _remote(src, dst, send_sems, recv_sems, idx, peer):
    return pltpu.make_async_remote_copy(src_ref=src, dst_ref=dst, send_sem=send_sems.at[idx],
                                        recv_sem=recv_sems.at[idx], device_id=peer, device_id_type=MESH)


def _copies_own(bufs, me):
    return [(bufs[0], bufs[1].at[me[2]], 0, _flip(me, 1))]


def _copies_near(bufs, me):
    return [(bufs[0], bufs[1].at[0, me[2]], 0, _flip(me, 2)), (bufs[0], bufs[1].at[1, me[2]], 1, _flip(me, 4))]


def _copies_far(bufs, me):
    return [(bufs[0], bufs[1].at[me[2]], 0, _flip(me, 6))]


def _copies_others(bufs, me):
    na = len(bufs) // 2
    return [(bufs[i], bufs[na + i].at[_lin(me)], 7 * i + k - 1, _flip(me, k))
            for i in range(na) for k in range(1, NDEV)]


def _copies_exchange(bufs, me):
    na = len(bufs) // 2
    return [(bufs[i].at[_lin(_flip(me, k))], bufs[na + i].at[_lin(me)], 7 * i + k - 1, _flip(me, k))
            for i in range(na) for k in range(1, NDEV)]


GROUP_COPIES = dict(own=_copies_own, near=_copies_near, far=_copies_far, others=_copies_others,
                    exchange=_copies_exchange)
GROUP_COUNT = dict(own=1, near=2, far=1)
TO_ALL = ("others", "exchange")


def _gather_start(bufs, groups, after, name):
    nb = len(bufs)
    ng = len(groups)

    def body(*refs):
        b = refs[:nb]
        sems = refs[nb + 1:nb + 1 + 2 * ng]
        token = refs[-1]
        me = _my_pos()
        for gi, (group, idx) in enumerate(groups):
            for src, dst, k, peer in GROUP_COPIES[group]([b[i] for i in idx], me):
                _remote(src, dst, sems[2 * gi], sems[2 * gi + 1], k, peer).start()
        token[...] = jnp.zeros_like(token)

    sem_t = []
    for group, idx in groups:
        cnt = 7 * (len(idx) // 2) if group in TO_ALL else GROUP_COUNT[group]
        sem_t += [pltpu.SemaphoreType.DMA((cnt,)), pltpu.SemaphoreType.DMA((cnt,))]
    ins = [pltpu.with_memory_space_constraint(a, pltpu.HBM) for a in bufs]
    res = pl.pallas_call(
        body, name=name,
        out_shape=(*sem_t, *[pltpu.HBM(a.shape, a.dtype) for a in bufs], jax.ShapeDtypeStruct((8, LANES), F32)),
        in_specs=[HBM_ONLY] * nb + [pl.BlockSpec(memory_space=pl.ANY)],
        out_specs=(*([SEM_SPEC] * (2 * ng)), *([HBM_ONLY] * nb), pl.BlockSpec(memory_space=pltpu.VMEM)),
        input_output_aliases={i: 2 * ng + i for i in range(nb)},
        compiler_params=SPLIT_COPY,
    )(*ins, after)
    return res[-1], list(res[2 * ng:2 * ng + nb]), [(res[2 * gi], res[2 * gi + 1]) for gi in range(ng)]


def _gather_wait(group, send_sems, recv_sems, bufs, after, name):
    nb = len(bufs)
    copies = GROUP_COPIES[group]

    def body(*refs):
        b = refs[:nb]
        ss, rs = refs[nb], refs[nb + 1]
        me = _my_pos()
        for src, dst, idx, peer in copies(b, me):
            if group in TO_ALL:
                landed = b[nb // 2 + idx // 7].at[_lin(peer)]
            elif group == "own":
                landed = b[1].at[1 - me[2]]
            else:
                landed = dst
            cp = _remote(src, landed, ss, rs, idx, peer)
            cp.wait_send()
            cp.wait_recv()

    res = pl.pallas_call(
        body, name=name, out_shape=[pltpu.HBM(a.shape, a.dtype) for a in bufs],
        in_specs=[HBM_ONLY] * nb + [SEM_SPEC, SEM_SPEC, pl.BlockSpec(memory_space=pl.ANY)],
        out_specs=[HBM_ONLY] * nb,
        input_output_aliases={i: i for i in range(nb)},
        compiler_params=SPLIT_COPY,
    )(*bufs, send_sems, recv_sems, after)
    return list(res)


def _forward_to_sibling(buf, name):
    n = buf.shape[0]

    def body(_in_ref, out_ref, send_sems, recv_sems):
        me = _my_pos()
        c = me[2]
        sibling = _flip(me, 1)
        sends = []
        for r in range(n):
            cp = _remote(out_ref.at[r, c], out_ref.at[r, c], send_sems, recv_sems, r, sibling)
            cp.start()
            sends.append(cp)
        for r in range(n):
            _remote(out_ref.at[r, c], out_ref.at[r, 1 - c], send_sems, recv_sems, r, sibling).wait_recv()
        for cp in sends:
            cp.wait_send()

    return pl.pallas_call(
        body, name=name, out_shape=jax.ShapeDtypeStruct(buf.shape, buf.dtype),
        in_specs=[HBM_SPEC], out_specs=HBM_SPEC,
        scratch_shapes=[pltpu.SemaphoreType.DMA((n,)), pltpu.SemaphoreType.DMA((n,))],
        input_output_aliases={0: 0},
    )(buf)


def _scalar(v):
    return jnp.asarray(v, jnp.int32).reshape(1)


def _sum_pairs(parts, theirs, name):
    na = len(parts)

    def body(c_ref, *refs):
        for i in range(na):
            o_ref = refs[2 * na + i]
            o_ref[0] = (refs[i][0, 0].astype(F32) + refs[na + i][0].astype(F32)).astype(o_ref.dtype)

    def mine_spec(a):
        return pl.BlockSpec((1, 1) + a.shape[2:], lambda q, c_ref: (q, c_ref[0], 0, 0))

    def spec(a):
        return pl.BlockSpec((1,) + a.shape[1:], lambda q, c_ref: (q, 0, 0))

    return pl.pallas_call(
        body, name=name,
        grid_spec=pltpu.PrefetchScalarGridSpec(
            num_scalar_prefetch=1, grid=(NCHIP,),
            in_specs=[mine_spec(a) for a in parts] + [spec(a) for a in theirs],
            out_specs=[spec(a) for a in theirs]),
        out_shape=[jax.ShapeDtypeStruct(a.shape, a.dtype) for a in theirs],
        compiler_params=_cp(("arbitrary",), VMEM_BIG),
    )(_scalar(lax.axis_index("c")), *_hbm(*parts, *theirs))


def _others(q, mine, nblk=NCHIP):
    return jnp.where(q == mine, (q + 1) % nblk, q)


def _sum_chips_adamw(own, recv, wv, mv, vv, token, tr, name):
    _, r, w = recv.shape

    def body(q_ref, own_ref, r0, r1, r2, r3, w_ref, m_ref, v_ref, _token, g_ref, d_ref, m2_ref, v2_ref):
        myq = q_ref[0]
        acc = None
        for q, r_ref in enumerate((r0, r1, r2, r3)):
            term = jnp.where(myq == q, own_ref[0], r_ref[0]).astype(F32)
            acc = term if acc is None else acc + term
        g_ref[...] = acc
        delta, m2, v2 = _adam_math(w_ref[...], acc, m_ref[...], v_ref[...])
        d_ref[...] = delta
        m2_ref[...] = m2
        v2_ref[...] = v2

    def recv_spec(q):
        return pl.BlockSpec((1, tr, w), lambda i, q_ref: (_others(q, q_ref[0]), i, 0))

    rows = lambda: pl.BlockSpec((tr, w), lambda i, q_ref: (i, 0))
    shp = jax.ShapeDtypeStruct((r, w), F32)
    return pl.pallas_call(
        body, name=name,
        grid_spec=pltpu.PrefetchScalarGridSpec(
            num_scalar_prefetch=1, grid=(r // tr,),
            in_specs=[pl.BlockSpec((1, tr, w), lambda i, q_ref: (q_ref[0], i, 0))]
            + [recv_spec(q) for q in range(NCHIP)] + [rows(), rows(), rows()]
            + [pl.BlockSpec((8, LANES), lambda i, q_ref: (0, 0))],
            out_specs=[rows(), rows(), rows(), rows()]),
        out_shape=[shp, shp, shp, shp],
        compiler_params=_cp(("arbitrary",), VMEM_MID),
    )(_scalar(_chip(_my_pos())), *_hbm(own, recv, recv, recv, recv, wv, mv, vv, token))


def _sum_blocks_small(own, recv, mine, transpose, name):
    na = len(recv)
    nblk = recv[0].shape[0]

    def body(q_ref, *refs):
        me = q_ref[0]
        for i in range(na):
            acc = None
            for q in range(nblk):
                term = jnp.where(me == q, refs[i][0], refs[na * (1 + q) + i][0]).astype(F32)
                acc = term if acc is None else acc + term
            refs[na * (1 + nblk) + i][...] = acc.T if transpose[i] else acc

    def oshape(a, tr):
        r, w = a.shape[1:]
        return (w, r) if tr else (r, w)

    own_spec = lambda a: pl.BlockSpec((1,) + a.shape[1:], lambda s, q_ref: (q_ref[0], 0, 0))
    recv_spec = lambda a, q: pl.BlockSpec((1,) + a.shape[1:], lambda s, q_ref: (_others(q, q_ref[0], nblk), 0, 0))
    out_spec = lambda shp: pl.BlockSpec(shp, lambda s, q_ref: (0, 0))
    in_specs = [own_spec(a) for a in own]
    for q in range(nblk):
        in_specs += [recv_spec(a, q) for a in recv]
    return pl.pallas_call(
        body, name=name,
        grid_spec=pltpu.PrefetchScalarGridSpec(
            num_scalar_prefetch=1, grid=(1,), in_specs=in_specs,
            out_specs=[out_spec(oshape(a, tr)) for a, tr in zip(recv, transpose)]),
        out_shape=[jax.ShapeDtypeStruct(oshape(a, tr), F32) for a, tr in zip(recv, transpose)],
        compiler_params=_cp(("arbitrary",), VMEM_MID),
    )(_scalar(mine), *_hbm(*own, *(list(recv) * nblk)))


def _rep_offsets():
    offs = []
    o = 0
    for r in REP_ROWS:
        offs.append(o)
        o += r
    return offs


LOSS_ROW = REP_TOTAL_ROWS


def _pack_small_grads(g):
    offs = _rep_offsets()

    def body(dwa, dwx, dnm, dbin, dcb, dba, dbx, dlam, dqn, dkn, dnp, dbpg, loss, o_ref):
        o_ref[pl.ds(REP_TOTAL_ROWS - 2, NDEV * REP_ROWS_DEV - REP_TOTAL_ROWS + 2), :] = jnp.zeros(
            (NDEV * REP_ROWS_DEV - REP_TOTAL_ROWS + 2, LANES), F32)
        o_ref[pl.ds(LOSS_ROW, 1), :] = loss[0:1, :]
        for n in range(NRB):
            o_ref[pl.ds(offs[0] + n * RBW, RBW), :] = dwa[n]
            o_ref[pl.ds(offs[1] + n * RBW, RBW), :] = dwx[n]

        def put_vec(off, ref, rows):
            for k in range(rows):
                o_ref[pl.ds(off + k, 1), :] = ref[:, k * LANES:(k + 1) * LANES]

        put_vec(offs[2], dnm, REP_ROWS[2])
        put_vec(offs[3], dbin, REP_ROWS[3])
        put_vec(offs[4], dcb, REP_ROWS[4])
        put_vec(offs[5], dba, REP_ROWS[5])
        put_vec(offs[6], dbx, REP_ROWS[6])
        put_vec(offs[7], dlam, REP_ROWS[7])
        for k in range(NG):
            o_ref[pl.ds(offs[8] + k, 1), :] = dqn[k, 0:1, :]
            o_ref[pl.ds(offs[9] + k, 1), :] = dkn[k, 0:1, :]
        put_vec(offs[10], dnp, REP_ROWS[10])
        put_vec(offs[11], dbpg, REP_ROWS[11])

    args = [g["w_rg_a"], g["w_rg_x"], g["norm_mix"], g["b_in"], g["conv_b"], g["b_rg_a"], g["b_rg_x"],
            g["lru_lambda"], g["q_norm"], g["k_norm"], g["norm_ple"], g["b_ple_gate"], g["loss"]]
    full = lambda shp: pl.BlockSpec(shp, lambda: (0,) * len(shp))
    return pl.pallas_call(
        body, name="pack_small_grads",
        in_specs=[full(a.shape) for a in args],
        out_specs=full((NDEV * REP_ROWS_DEV, LANES)),
        out_shape=jax.ShapeDtypeStruct((NDEV * REP_ROWS_DEV, LANES), F32),
    )(*_hbm(*args))


def _adam_math(wv, gv, mv, vv):
    c1 = 1.0 - B1 ** STEP
    c2 = 1.0 - B2 ** STEP
    m2 = B1 * mv + (1.0 - B1) * gv
    v2 = B2 * vv + (1.0 - B2) * (gv * gv)
    delta = (-LR) * ((m2 / c1) / (jnp.sqrt(v2 / c2) + AEPS) + WD * wv)
    return delta, m2, v2


def _adamw_small(rep_flat, w, m, v):
    offs = _rep_offsets()
    n = len(REP_NAMES)

    def body(*refs):
        g_ref = refs[0]
        w_refs = refs[1:1 + n]
        m_refs = refs[1 + n:1 + 2 * n]
        v_refs = refs[1 + 2 * n:1 + 3 * n]
        outs = refs[1 + 3 * n:]
        go, do, mo, vo = outs[:n], outs[n:2 * n], outs[2 * n:3 * n], outs[3 * n:]

        def emit(i, idx, gv):
            go[i][idx] = gv
            delta, m2, v2 = _adam_math(w_refs[i][idx], gv, m_refs[i][idx], v_refs[i][idx])
            do[i][idx] = delta
            mo[i][idx] = m2
            vo[i][idx] = v2

        for i in range(n):
            if i < 2:
                for b in range(NRB):
                    emit(i, b, g_ref[pl.ds(offs[i] + b * RBW, RBW), :])
            elif REP_NAMES[i] in ("q_norm", "k_norm"):
                emit(i, slice(None), g_ref[pl.ds(offs[i], NG), :])
            else:
                gv = jnp.concatenate([g_ref[pl.ds(offs[i] + k, 1), :] for k in range(REP_ROWS[i])], axis=1)
                emit(i, slice(None), gv)

    full = lambda shp: pl.BlockSpec(shp, lambda: (0,) * len(shp))
    pspecs = [full(a.shape) for a in w]
    pshapes = [jax.ShapeDtypeStruct(a.shape, F32) for a in w]
    res = pl.pallas_call(
        body, name="adamw_small",
        in_specs=[full(rep_flat.shape)] + pspecs * 3,
        out_specs=pspecs * 4, out_shape=pshapes * 4,
        compiler_params=_cp(None, VMEM_MID),
    )(*_hbm(rep_flat, *w, *m, *v))
    return res[:n], res[n:2 * n], res[2 * n:3 * n], res[3 * n:]


def _adamw_many(w, g, m, v, token):
    n = len(w)

    def body(*refs):
        for i in range(n):
            delta, m2, v2 = _adam_math(refs[i][...], refs[n + i][...], refs[2 * n + i][...], refs[3 * n + i][...])
            refs[4 * n + 1 + i][...] = delta
            refs[5 * n + 1 + i][...] = m2
            refs[6 * n + 1 + i][...] = v2

    full = lambda shp: pl.BlockSpec(shp, lambda: (0,) * len(shp))
    specs = [full(a.shape) for a in w]
    shapes = [jax.ShapeDtypeStruct(a.shape, F32) for a in w]
    res = pl.pallas_call(
        body, name="adamw_shards",
        in_specs=specs * 4 + [full((8, LANES))], out_specs=specs * 3, out_shape=shapes * 3,
        compiler_params=_cp(None, VMEM_MID),
    )(*_hbm(*w, *g, *m, *v, token))
    return res[:n], res[n:2 * n], res[2 * n:]


def kernel(x, p, norm_mix, w_in, b_in, conv_w, conv_b, w_rg_a, b_rg_a, w_rg_x, b_rg_x, lru_lambda, q_norm, k_norm, w_o_rnn, w_o_att, w_out, norm_ple, w_ple_gate, b_ple_gate, w_ple, loss_target, m_norm_mix, m_w_in, m_b_in, m_conv_w, m_conv_b, m_w_rg_a, m_b_rg_a, m_w_rg_x, m_b_rg_x, m_lru_lambda, m_q_norm, m_k_norm, m_w_o_rnn, m_w_o_att, m_w_out, m_norm_ple, m_w_ple_gate, m_b_ple_gate, m_w_ple, v_norm_mix, v_w_in, v_b_in, v_conv_w, v_conv_b, v_w_rg_a, v_b_rg_a, v_w_rg_x, v_b_rg_x, v_lru_lambda, v_q_norm, v_k_norm, v_w_o_rnn, v_w_o_att, v_w_out, v_norm_ple, v_w_ple_gate, v_b_ple_gate, v_w_ple):
    w = dict(norm_mix=norm_mix, w_in=w_in, b_in=b_in, conv_w=conv_w, conv_b=conv_b, w_rg_a=w_rg_a, b_rg_a=b_rg_a,
             w_rg_x=w_rg_x, b_rg_x=b_rg_x, lru_lambda=lru_lambda, q_norm=q_norm, k_norm=k_norm, w_o_rnn=w_o_rnn,
             w_o_att=w_o_att, w_out=w_out, norm_ple=norm_ple, w_ple_gate=w_ple_gate, b_ple_gate=b_ple_gate,
             w_ple=w_ple)
    m = dict(norm_mix=m_norm_mix, w_in=m_w_in, b_in=m_b_in, conv_w=m_conv_w, conv_b=m_conv_b, w_rg_a=m_w_rg_a,
             b_rg_a=m_b_rg_a, w_rg_x=m_w_rg_x, b_rg_x=m_b_rg_x, lru_lambda=m_lru_lambda, q_norm=m_q_norm,
             k_norm=m_k_norm, w_o_rnn=m_w_o_rnn, w_o_att=m_w_o_att, w_out=m_w_out, norm_ple=m_norm_ple,
             w_ple_gate=m_w_ple_gate, b_ple_gate=m_b_ple_gate, w_ple=m_w_ple)
    v = dict(norm_mix=v_norm_mix, w_in=v_w_in, b_in=v_b_in, conv_w=v_conv_w, conv_b=v_conv_b, w_rg_a=v_w_rg_a,
             b_rg_a=v_b_rg_a, w_rg_x=v_w_rg_x, b_rg_x=v_b_rg_x, lru_lambda=v_lru_lambda, q_norm=v_q_norm,
             k_norm=v_k_norm, w_o_rnn=v_w_o_rnn, w_o_att=v_w_o_att, w_out=v_w_out, norm_ple=v_norm_ple,
             w_ple_gate=v_w_ple_gate, b_ple_gate=v_b_ple_gate, w_ple=v_w_ple)
    names = list(w.keys())

    shards = [w_in[0].T.astype(BF16), w_o_rnn[0].astype(BF16), w_o_att[0].T.astype(BF16), w_out[0].astype(BF16),
              w_ple_gate[0].astype(BF16), w_ple[0].T.astype(BF16), conv_w[0]]
    pos = _my_pos()
    me, my_core, my_chip = _lin(pos), pos[2], _chip(pos)
    hbm_empty = lambda shp, dt: lax.empty(shp, dt)
    w_shard, conv_shard = shards[0], shards[6]
    shp = w_shard.shape
    entry_token, bufs, sems = _gather_start(
        [w_shard, hbm_empty((2,) + shp, BF16), hbm_empty((2, 2) + shp, BF16), conv_shard,
         hbm_empty((NDEV,) + conv_shard.shape, F32)],
        [("own", (0, 1)), ("near", (0, 2)), ("others", (3, 4))], norm_mix, "gather_start_near")
    w_src, own_l, near_l, conv_src, conv_l = bufs
    sem_own, sem_near, sem_conv = sems
    gather_out = {}

    def project(hn):
        w_thru, own = _gather_wait("own", *sem_own, [w_src, own_l], hn, "gather_wait_own")
        own = lax.dynamic_update_slice(own, w_shard[None], (my_core, 0, 0)).reshape(1, CHIP_COLS, D)
        chips = [jnp.stack([my_chip]), jnp.stack([my_chip ^ 1, my_chip ^ 2]), jnp.stack([my_chip ^ 3])]
        chips = [c.astype(jnp.int32) for c in chips]
        proj = _in_proj_chips(hn, own, b_in, chips[0], None, entry_token, "in_proj_own")
        w_thru, near = _gather_wait("near", *sem_near, [w_thru, near_l], proj, "gather_wait_near")
        near = _forward_to_sibling(near, "gather_forward_near")
        token, (w_thru, far_l), (sem_far,) = _gather_start(
            [w_thru, hbm_empty((2,) + shp, BF16)], [("far", (0, 1))], near, "gather_start_far")
        near = near.reshape(2, CHIP_COLS, D)
        proj = _in_proj_chips(hn, near, b_in, chips[1], proj, token, "in_proj_near")
        w_thru, far = _gather_wait("far", *sem_far, [w_thru, far_l], proj, "gather_wait_far")
        far = _forward_to_sibling(far[None], "gather_forward_far").reshape(1, CHIP_COLS, D)
        proj = _in_proj_chips(hn, far, b_in, chips[2], proj, token, "in_proj_far")
        conv_thru, conv_g = _gather_wait("others", *sem_conv, [conv_src, conv_l], proj, "gather_wait_conv")
        conv_g = lax.dynamic_update_slice(conv_g, conv_shard[None], (me, 0, 0))
        conv_f = conv_g.transpose(1, 0, 2).reshape(CONVW, DR)
        srcs = list(shards[1:6])
        token, obufs, (sem_out,) = _gather_start(
            srcs + [hbm_empty((NDEV,) + a.shape, BF16) for a in srcs], [("others", tuple(range(10)))], proj,
            "gather_start_out")
        gather_out.update(bufs=obufs, sems=sem_out)
        return proj, [own, near, far], jnp.concatenate(chips), conv_f, token

    def other_weights(after):
        obufs = _gather_wait("others", *gather_out["sems"], gather_out["bufs"], after, "gather_wait_out")
        full = [lax.dynamic_update_slice(a, s[None], (me, 0, 0)) for a, s in zip(obufs[5:], shards[1:6])]
        return [a.reshape((NDEV * a.shape[1], a.shape[2])) for a in full]

    def start_reduce(arrs, tag):
        if tag == "out":
            parts = [a.reshape((NDEV, a.shape[0] // NDEV, a.shape[1])) for a in arrs]
            token, bufs, (sems,) = _gather_start(
                parts + [lax.empty(a.shape, a.dtype) for a in parts], [("exchange", tuple(range(2 * len(parts))))],
                arrs[-1][:SUBLANES], "reduce_out_start")
            return token, (bufs, sems)
        parts = [a.reshape((NCHIP, 2, a.shape[0] // NDEV, a.shape[1])) for a in arrs]
        theirs = _exchange_within_chip(parts, "reduce_within_chip_" + tag)
        return _between_chips_start(_sum_pairs(parts, theirs, "sum_pairs_" + tag), "reduce_between_chips_start_" + tag)

    grad_x, pending_out, pending_in, small = _local_step(
        x.reshape(T, D), p.reshape(T, PLE), loss_target.reshape(T, D),
        project, other_weights,
        norm_mix, conv_b, w_rg_a[0], b_rg_a, w_rg_x[0], b_rg_x, lru_lambda, q_norm[0], k_norm[0],
        norm_ple, b_ple_gate, start_reduce, entry_token)

    rep_parts = _pack_small_grads(small).reshape(NDEV, REP_ROWS_DEV, LANES)
    conv_parts = small["conv_w"].reshape(CONVW, NDEV, DR // NDEV).transpose(1, 0, 2)
    smalls = [rep_parts, conv_parts]
    token, sbufs, (sem_x,) = _gather_start(
        smalls + [lax.empty(a.shape, F32) for a in smalls], [("exchange", (0, 1, 2, 3))], small["norm_mix"],
        "reduce_small_start")

    own_in, recv_in = _between_chips_wait(pending_in, token, "reduce_between_chips_wait_in")
    w_in_res = _sum_chips_adamw(own_in[0], recv_in[0], w_in[0].T, m_w_in[0].T, v_w_in[0].T, token, 304, "adamw_w_in")
    sbufs = _gather_wait("exchange", *sem_x, sbufs, w_in_res[0], "reduce_small_wait")
    g_rep, g_conv = _sum_blocks_small(sbufs[:2], sbufs[2:], me, (False, False), "sum_small")
    token, gbufs, (sem_g,) = _gather_start(
        [g_rep, lax.empty((NDEV,) + g_rep.shape, F32)], [("others", (0, 1))], g_conv, "gather_small_start")
    obufs = _gather_wait("exchange", *pending_out[1], pending_out[0], token, "reduce_out_wait")
    g_o_rnn, g_o_att, g_out, g_pg, g_ple = _sum_blocks_small(
        obufs[:5], obufs[5:], me, (False, True, False, False, True), "sum_out")

    grad, delta, new_m, new_v = {}, {}, {}, {}
    rest = ("w_o_rnn", "w_o_att", "w_out", "w_ple_gate", "w_ple", "conv_w")
    g_rest = [g_o_rnn, g_o_att, g_out, g_pg, g_ple, g_conv]
    rest_res = _adamw_many([w[n][0] for n in rest], g_rest, [m[n][0] for n in rest], [v[n][0] for n in rest], token)
    _, rep_all = _gather_wait("others", *sem_g, gbufs, rest_res[0][0], "gather_small_wait")
    rep_all = lax.dynamic_update_slice(rep_all, g_rep[None], (me, 0, 0)).reshape(NDEV * REP_ROWS_DEV, LANES)
    loss = rep_all[LOSS_ROW, 0]
    rep_shape = lambda a: a if a.ndim == 2 else a.reshape(a.shape[1:])
    res = _adamw_small(rep_all, [rep_shape(w[n]) for n in REP_NAMES], [rep_shape(m[n]) for n in REP_NAMES],
                       [rep_shape(v[n]) for n in REP_NAMES])
    for dst, vals in zip((grad, delta, new_m, new_v), res):
        for n, a in zip(REP_NAMES, vals):
            dst[n] = a.reshape(w[n].shape)
    grad["w_in"], delta["w_in"], new_m["w_in"], new_v["w_in"] = [a.T[None] for a in w_in_res]
    for n, a in zip(rest, g_rest):
        grad[n] = a[None]
    for dst, vals in zip((delta, new_m, new_v), rest_res):
        for n, a in zip(rest, vals):
            dst[n] = a[None]

    return (loss, grad_x.reshape(BL, S, D), *[grad[n] for n in names], *[delta[n] for n in names],
            *[new_m[n] for n in names], *[new_v[n] for n in names])
```

```python
import jax
import jax.numpy as jnp
from jax import lax
from jax.experimental import pallas as pl
from jax.experimental.pallas import tpu as pltpu

F32 = jnp.float32
BF16 = jnp.bfloat16

D = 1024
S = 2048
BL = 2
T = BL * S
NDEV = 8
NCHIP = 4
PLE = 256
DR = 1280
NRB = 10
RBW = 128
CONVW = 4
LRU_C = 8.0
HD = 128
NH = 4
PATTERNS = ((128, 1), (512, 4), (2048, 16))
NG = 3
ATT = NH * HD
GW = NG * ATT
NIN = 2 * DR + 3 * GW + ATT + 2 * D
OFF_ZR = DR
OFF_Q = 2 * DR
OFF_ZA = OFF_Q + 3 * GW
OFF_G = OFF_ZA + ATT
ROPE_THETA = 10000.0
EPS = 1e-6
SCALE = HD ** -0.5
NEG = -1e30
QB = 128
LANES = 128
CT = 512
NCT = NIN // CT
A_W = 2 * DR
C_W = ATT + 2 * D

LR, B1, B2, AEPS, WD, STEP = 0.001, 0.9, 0.999, 1e-08, 0.01, 10

NSHARD_IN = NIN // NDEV
REP_NAMES = ("w_rg_a", "w_rg_x", "norm_mix", "b_in", "conv_b", "b_rg_a", "b_rg_x", "lru_lambda", "q_norm",
             "k_norm", "norm_ple", "b_ple_gate")
REP_ROWS = (NRB * RBW, NRB * RBW, D // LANES, NIN // LANES, DR // LANES, DR // LANES, DR // LANES, DR // LANES,
            NG, NG, D // LANES, D // LANES)
REP_TOTAL_ROWS = sum(REP_ROWS)
REP_ROWS_DEV = 344
BIG_NAMES = ("w_in", "w_o_rnn", "w_o_att", "w_out", "w_ple_gate", "w_ple")

VMEM_BIG = 56 * 1024 * 1024
VMEM_MID = 40 * 1024 * 1024


def _cp(sem=None, vmem=None):
    return pltpu.CompilerParams(dimension_semantics=sem, vmem_limit_bytes=vmem)


def _hbm(*arrays):
    return [pltpu.with_memory_space_constraint(a, pltpu.HBM) for a in arrays]


def _copy_together(copies):
    for cp in copies:
        cp.start()
    for cp in copies:
        cp.wait()


def _dot(a, b):
    return jnp.dot(a, b, preferred_element_type=F32)


def _dot_nt(a, b):
    return lax.dot_general(a, b, (((1,), (1,)), ((), ())), preferred_element_type=F32)


def _dot_tn(a, b):
    return lax.dot_general(a, b, (((0,), (0,)), ((), ())), preferred_element_type=F32)


def _sigmoid(x):
    return jax.nn.sigmoid(x)


def _perm(j):
    jq = j - OFF_Q // CT
    inside = (j >= OFF_Q // CT) & (j < OFF_ZA // CT)
    return jnp.where(inside, OFF_Q // CT + (jq % 3) * 3 + jq // 3, j)


PIECES = ((0, A_W // CT), (OFF_Q // CT, GW // CT), (OFF_Q // CT + 3, GW // CT), (OFF_Q // CT + 6, GW // CT),
          (OFF_ZA // CT, C_W // CT))


def _rmsnorm_fwd(x, gain, token, tm=512):
    def body(x_ref, g_ref, _token, o_ref):
        xv = x_ref[...]
        var = jnp.mean(xv * xv, axis=-1, keepdims=True)
        o_ref[...] = (xv * lax.rsqrt(var + EPS) * g_ref[...]).astype(BF16)

    return pl.pallas_call(
        body, grid=(T // tm,), name="rmsnorm_fwd",
        in_specs=[pl.BlockSpec((tm, D), lambda i: (i, 0)), pl.BlockSpec((1, D), lambda i: (0, 0)),
                  pl.BlockSpec((8, LANES), lambda i: (0, 0))],
        out_specs=pl.BlockSpec((tm, D), lambda i: (i, 0)),
        out_shape=jax.ShapeDtypeStruct((T, D), BF16),
        compiler_params=_cp(("parallel",)),
    )(*_hbm(x, gain, token))


CHIP_COLS = NIN // NCHIP


def _in_proj_chips(hn, w_rows, bias, chips, proj, token, name, tm=1024):
    n = w_rows.shape[0]

    def body(chips_ref, a_ref, w_ref, b_ref, _token, *rest):
        o_ref = rest[-1]
        o_ref[...] = (_dot_nt(a_ref[...], w_ref[0]) + b_ref[...]).astype(BF16)

    in_specs = [pl.BlockSpec((tm, D), lambda s, i, ch: (i, 0)),
                pl.BlockSpec((1, CHIP_COLS, D), lambda s, i, ch: (s, 0, 0)),
                pl.BlockSpec((1, CHIP_COLS), lambda s, i, ch: (0, ch[s])),
                pl.BlockSpec((8, LANES), lambda s, i, ch: (0, 0))]
    args = [hn, w_rows, bias, token]
    aliases = {}
    if proj is not None:
        in_specs.append(pl.BlockSpec(memory_space=pl.ANY))
        args.append(proj)
        aliases = {5: 0}
    return pl.pallas_call(
        body, name=name,
        grid_spec=pltpu.PrefetchScalarGridSpec(
            num_scalar_prefetch=1, grid=(n, T // tm), in_specs=in_specs,
            out_specs=pl.BlockSpec((tm, CHIP_COLS), lambda s, i, ch: (i, ch[s]))),
        out_shape=jax.ShapeDtypeStruct((T, NIN), BF16),
        input_output_aliases=aliases,
        compiler_params=_cp(("arbitrary", "arbitrary"), VMEM_BIG),
    )(chips, *_hbm(*args))


def _grad_x(pieces, w_bufs, chips, token, x, dx1, gain, tm=512):
    nb = len(w_bufs)

    def body(chips_ref, a_ref, q_ref, k_ref, v_ref, c_ref, *rest):
        w_hbm = rest[:nb]
        x_ref, dx1_ref, g_ref, dx_ref, dg_ref, w, sems = rest[nb + 1:]
        first = pl.program_id(0) == 0

        @pl.when(first)
        def _():
            s = 0
            copies = []
            for buf in w_hbm:
                for r in range(buf.shape[0]):
                    row = pl.multiple_of(chips_ref[s] * CHIP_COLS, 128)
                    copies.append(pltpu.make_async_copy(buf.at[r], w.at[pl.ds(row, CHIP_COLS), :], sems.at[s]))
                    s += 1
            _copy_together(copies)

        acc = _dot(a_ref[...], w[pl.ds(0, A_W), :])
        for kind, p_ref in enumerate((q_ref, k_ref, v_ref)):
            for g in range(NG):
                row = OFF_Q + (3 * g + kind) * CT
                acc = acc + _dot(p_ref[:, g * CT:(g + 1) * CT], w[pl.ds(row, CT), :])
        dn = acc + _dot(c_ref[...], w[pl.ds(OFF_ZA, C_W), :])
        xv = x_ref[...]
        rstd = lax.rsqrt(jnp.mean(xv * xv, axis=-1, keepdims=True) + EPS)
        xh = xv * rstd
        dg = jnp.sum(dn * xh, axis=0, keepdims=True)
        gd = dn * g_ref[...]
        dx_ref[...] = dx1_ref[...] + rstd * (gd - xh * jnp.mean(gd * xh, axis=-1, keepdims=True))

        @pl.when(first)
        def _():
            dg_ref[...] = dg

        @pl.when(jnp.logical_not(first))
        def _():
            dg_ref[...] += dg

    tok = lambda wd: pl.BlockSpec((tm, wd), lambda i, ch: (i, 0))
    vec = lambda: pl.BlockSpec((1, D), lambda i, ch: (0, 0))
    return pl.pallas_call(
        body, name="grad_x",
        grid_spec=pltpu.PrefetchScalarGridSpec(
            num_scalar_prefetch=1, grid=(T // tm,),
            in_specs=[tok(A_W), tok(GW), tok(GW), tok(GW), tok(C_W)] + [pl.BlockSpec(memory_space=pl.ANY)] * nb
            + [pl.BlockSpec((8, LANES), lambda i, ch: (0, 0)), tok(D), tok(D), vec()],
            out_specs=[tok(D), vec()],
            scratch_shapes=[pltpu.VMEM((NIN, D), BF16), pltpu.SemaphoreType.DMA((NCHIP,))]),
        out_shape=[jax.ShapeDtypeStruct((T, D), F32), jax.ShapeDtypeStruct((1, D), F32)],
        compiler_params=_cp(("arbitrary",), VMEM_BIG),
    )(chips, *_hbm(*pieces, *w_bufs, token, x, dx1, gain))


def _dw_in(pieces, hn):
    def body(a_ref, q_ref, k_ref, v_ref, c_ref, h_hbm, o_ref, s_ref, h):
        j = pl.program_id(0)

        @pl.when(j == 0)
        def _():
            pltpu.sync_copy(h_hbm, h)

        def step(x_ref):
            xv = x_ref[...]
            o_ref[...] = _dot_tn(xv, h[...]).astype(BF16)
            s_ref[...] = jnp.sum(xv.astype(F32), axis=0, keepdims=True)

        for x_ref, (lo, n) in zip((a_ref, q_ref, k_ref, v_ref, c_ref), PIECES):
            pl.when((j >= lo) & (j < lo + n))(lambda x_ref=x_ref: step(x_ref))

    def piece_spec(lo, n):
        return pl.BlockSpec((T, CT), lambda j: (0, jnp.clip(j - lo, 0, n - 1)))

    return pl.pallas_call(
        body, grid=(NCT,), name="dw_in",
        in_specs=[piece_spec(lo, n) for lo, n in PIECES] + [pl.BlockSpec(memory_space=pl.ANY)],
        out_specs=[pl.BlockSpec((CT, D), lambda j: (_perm(j), 0)), pl.BlockSpec((1, CT), lambda j: (0, _perm(j)))],
        out_shape=[jax.ShapeDtypeStruct((NIN, D), BF16), jax.ShapeDtypeStruct((1, NIN), F32)],
        scratch_shapes=[pltpu.VMEM((T, D), BF16)],
        compiler_params=_cp(("arbitrary",), VMEM_BIG),
    )(*_hbm(*pieces, hn))


def _mm_tn(a, b, ta, tt, name):
    m = a.shape[1]
    n = b.shape[1]
    nt = T // tt

    def body(a_ref, b_ref, o_ref, acc):
        t = pl.program_id(1)
        p = _dot_tn(a_ref[...].astype(BF16), b_ref[...].astype(BF16))
        if nt == 1:
            o_ref[...] = p.astype(BF16)
            return

        @pl.when(t == 0)
        def _():
            acc[...] = p

        @pl.when(t > 0)
        def _():
            acc[...] += p

        @pl.when(t == nt - 1)
        def _():
            o_ref[...] = acc[...].astype(BF16)

    return pl.pallas_call(
        body, grid=(m // ta, nt), name=name,
        in_specs=[pl.BlockSpec((tt, ta), lambda j, t: (t, j)), pl.BlockSpec((tt, n), lambda j, t: (t, 0))],
        out_specs=pl.BlockSpec((ta, n), lambda j, t: (j, 0)),
        out_shape=pltpu.HBM((m, n), BF16),
        scratch_shapes=[pltpu.VMEM((ta, n), F32)],
        compiler_params=_cp(("parallel", "arbitrary"), VMEM_MID),
    )(*_hbm(a, b))


def _row_iota():
    return lax.broadcasted_iota(jnp.int32, (S, RBW), 0)


SUBLANES = 8
N_SHIFT_BUFS = 4


class _Shifter:
    def __init__(self, bufs):
        self.bufs = bufs
        self.k = 0

    def _store(self, v, fill, front):
        b = self.bufs.at[self.k % N_SHIFT_BUFS]
        self.k += 1
        b[pl.ds(0 if front else SUBLANES + S, SUBLANES), :] = jnp.full((SUBLANES, RBW), fill, F32)
        b[pl.ds(SUBLANES, S), :] = v
        return b

    def down(self, v, ds, fill):
        b = self._store(v, fill, True)
        return [b[pl.ds(SUBLANES - d, S), :] for d in ds]

    def up(self, v, ds, fill):
        b = self._store(v, fill, False)
        return [b[pl.ds(SUBLANES + d, S), :] for d in ds]


def _shift_down(v, d, sh, fill):
    return sh.down(v, (d,), fill)[0]


def _shift_up(v, d, sh, fill):
    return sh.up(v, (d,), fill)[0]


def _scan_down(a, u, row):
    d = 1
    while d < S:
        last = 2 * d >= S
        if d < SUBLANES:
            u = a * _shift_down(u, d, row, 0.0) + u
            if not last:
                a = a * _shift_down(a, d, row, 1.0)
        else:
            u = jnp.concatenate([u[:d], a[d:] * u[:S - d] + u[d:]], axis=0)
            if not last:
                a = jnp.concatenate([a[:d], a[d:] * a[:S - d]], axis=0)
        d *= 2
    return u


def _scan_up(b, g, row):
    d = 1
    while d < S:
        last = 2 * d >= S
        if d < SUBLANES:
            g = g + b * _shift_up(g, d, row, 0.0)
            if not last:
                b = b * _shift_up(b, d, row, 0.0)
        else:
            g = jnp.concatenate([g[:S - d] + b[:S - d] * g[d:], g[S - d:]], axis=0)
            if not last:
                b = jnp.concatenate([b[:S - d] * b[d:], b[S - d:]], axis=0)
        d *= 2
    return g


def _softplus(x):
    return jnp.maximum(x, 0.0) + jnp.log1p(jnp.exp(-jnp.abs(x)))


N_SAVED = 5


def _rnn_gates(x, cw, cb, wa, ba, wx, bx, lam, row, pad, saved=None):
    xs = pad.down(x, (1, 2, 3), 0.0)
    if saved is not None:
        r, i, a, mult, xc = saved
        return xc, xc.astype(BF16), r, i, _softplus(-lam), a, mult, xs
    xc = cb + cw[3:4, :] * x
    for j in (1, 2, 3):
        xc = xc + cw[3 - j:4 - j, :] * xs[j - 1]
    xcb = xc.astype(BF16)
    r = _sigmoid(_dot(xcb, wa) + ba)
    i = _sigmoid(_dot(xcb, wx) + bx)
    sp = _softplus(-lam)
    log_a = (-LRU_C) * r * sp
    a = jnp.exp(log_a)
    mult = jnp.where(row == 0, 1.0, jnp.sqrt(jnp.tanh(-log_a) * (1.0 + a * a)))
    return xc, xcb, r, i, sp, a, mult, xs


def _rnn_fwd(proj3, conv_w, conv_b, wa, ba, wx, bx, lam, token):
    def body(x_ref, cw_ref, cb_ref, wa_ref, ba_ref, wx_ref, bx_ref, lam_ref, _token, h_ref, g_ref, pad):
        row = _row_iota()
        sh = _Shifter(pad)
        x = x_ref[0].astype(F32)
        xc, _, r, i, _, a, mult, _ = _rnn_gates(x, cw_ref[...], cb_ref[...], wa_ref[0], ba_ref[...],
                                             wx_ref[0], bx_ref[...], lam_ref[...], row, sh)
        for k, val in enumerate((r, i, a, mult, xc)):
            g_ref[k, 0] = val
        h_ref[0] = _scan_down(a, mult * (i * xc), sh)

    vec = lambda: pl.BlockSpec((1, RBW), lambda b, n: (0, n))
    mat = lambda: pl.BlockSpec((1, RBW, RBW), lambda b, n: (n, 0, 0))
    return pl.pallas_call(
        body, grid=(BL, NRB), name="rnn_fwd",
        in_specs=[pl.BlockSpec((1, S, RBW), lambda b, n: (b, 0, n)),
                  pl.BlockSpec((CONVW, RBW), lambda b, n: (0, n)),
                  vec(), mat(), vec(), mat(), vec(), vec(), pl.BlockSpec((8, LANES), lambda b, n: (0, 0))],
        out_specs=[pl.BlockSpec((1, S, RBW), lambda b, n: (b, 0, n)),
                   pl.BlockSpec((N_SAVED, 1, S, RBW), lambda b, n: (0, b, 0, n))],
        out_shape=[jax.ShapeDtypeStruct((BL, S, DR), F32), jax.ShapeDtypeStruct((N_SAVED, BL, S, DR), F32)],
        scratch_shapes=[pltpu.VMEM((N_SHIFT_BUFS, S + 2 * SUBLANES, RBW), F32)],
        compiler_params=_cp(("parallel", "parallel"), VMEM_MID),
    )(*_hbm(proj3, conv_w, conv_b, wa, ba, wx, bx, lam, token))


def _rnn_bwd(proj3, h3, dh3, gates, slab_a3, conv_w, conv_b, wa, ba, wx, bx, lam, token):
    def body(x_ref, h_ref, dh_ref, g_ref, cw_ref, cb_ref, wa_ref, ba_ref, wx_ref, bx_ref, lam_ref, _alias, _token,
             dx_ref, dcw_ref, dcb_ref, dwa_ref, dba_ref, dwx_ref, dbx_ref, dlam_ref, pad):
        row = _row_iota()
        sh = _Shifter(pad)
        x = x_ref[0].astype(F32)
        cw = cw_ref[...]
        wa_v = wa_ref[0]
        wx_v = wx_ref[0]
        lam_v = lam_ref[...]
        xc, xcb, r, i, sp, a, mult, xs = _rnn_gates(x, cw, cb_ref[...], wa_v, ba_ref[...], wx_v, bx_ref[...], lam_v,
                                                    row, sh, [g_ref[k, 0] for k in range(N_SAVED)])
        h = h_ref[0]
        g = _scan_up(_shift_up(a, 1, sh, 0.0), dh_ref[0], sh)
        da = g * _shift_down(h, 1, sh, 0.0)
        dmult = jnp.where(row == 0, 0.0, g * (i * xc))
        gm = g * mult
        di = gm * xc
        dxc = gm * i
        dlog_a = da * a - dmult * (a * a) / mult
        dr = dlog_a * ((-LRU_C) * sp)
        dsp = jnp.sum(dlog_a * ((-LRU_C) * r), axis=0, keepdims=True)
        dlam = dsp * (-_sigmoid(-lam_v))
        dpa = dr * r * (1.0 - r)
        dpx = di * i * (1.0 - i)
        dpab = dpa.astype(BF16)
        dpxb = dpx.astype(BF16)
        dwa = _dot_tn(xcb, dpab)
        dwx = _dot_tn(xcb, dpxb)
        dba = jnp.sum(dpa, axis=0, keepdims=True)
        dbx = jnp.sum(dpx, axis=0, keepdims=True)
        dxc = dxc + _dot_nt(dpab, wa_v) + _dot_nt(dpxb, wx_v)
        dcb = jnp.sum(dxc, axis=0, keepdims=True)
        dx = cw[3:4, :] * dxc
        dcw_rows = [None] * CONVW
        dcw_rows[3] = jnp.sum(dxc * x, axis=0, keepdims=True)
        dxc_up = sh.up(dxc, (1, 2, 3), 0.0)
        for j in (1, 2, 3):
            dx = dx + cw[3 - j:4 - j, :] * dxc_up[j - 1]
            dcw_rows[3 - j] = jnp.sum(dxc * xs[j - 1], axis=0, keepdims=True)
        dx_ref[0] = dx.astype(BF16)
        dcw = jnp.concatenate(dcw_rows, axis=0)
        first = pl.program_id(1) == 0

        @pl.when(first)
        def _():
            dcw_ref[...] = dcw
            dcb_ref[...] = dcb
            dwa_ref[0] = dwa
            dba_ref[...] = dba
            dwx_ref[0] = dwx
            dbx_ref[...] = dbx
            dlam_ref[...] = dlam

        @pl.when(jnp.logical_not(first))
        def _():
            dcw_ref[...] += dcw
            dcb_ref[...] += dcb
            dwa_ref[0] += dwa
            dba_ref[...] += dba
            dwx_ref[0] += dwx
            dbx_ref[...] += dbx
            dlam_ref[...] += dlam

    slab = lambda: pl.BlockSpec((1, S, RBW), lambda n, b: (b, 0, n))
    vec = lambda: pl.BlockSpec((1, RBW), lambda n, b: (0, n))
    mat = lambda: pl.BlockSpec((1, RBW, RBW), lambda n, b: (n, 0, 0))
    taps = lambda: pl.BlockSpec((CONVW, RBW), lambda n, b: (0, n))
    vshape = pltpu.HBM((1, DR), F32)
    mshape = pltpu.HBM((NRB, RBW, RBW), F32)
    return pl.pallas_call(
        body, grid=(NRB, BL), name="rnn_bwd",
        in_specs=[slab(), slab(), slab(), pl.BlockSpec((N_SAVED, 1, S, RBW), lambda n, b: (0, b, 0, n)),
                  taps(), vec(), mat(), vec(), mat(), vec(), vec(),
                  pl.BlockSpec(memory_space=pl.ANY), pl.BlockSpec((8, LANES), lambda n, b: (0, 0))],
        out_specs=[slab(), taps(), vec(), mat(), vec(), mat(), vec(), vec()],
        out_shape=[jax.ShapeDtypeStruct((BL, S, A_W), BF16), pltpu.HBM((CONVW, DR), F32),
                   vshape, mshape, vshape, mshape, vshape, vshape],
        input_output_aliases={11: 0},
        scratch_shapes=[pltpu.VMEM((N_SHIFT_BUFS, S + 2 * SUBLANES, RBW), F32)],
        compiler_params=_cp(("parallel", "arbitrary"), 48 * 1024 * 1024),
    )(*_hbm(proj3, h3, dh3, gates, conv_w, conv_b, wa, ba, wx, bx, lam, slab_a3, token))


NQB = S // QB


def _rms_head(t, gain):
    rstd = lax.rsqrt(jnp.mean(t * t, axis=-1, keepdims=True) + EPS)
    return t * rstd * gain


def _rope(t, cs, sn):
    return t * cs + pltpu.roll(t, HD // 2, 1) * sn


def _rope_t(dy, cs, sn):
    return dy * cs - pltpu.roll(dy, HD // 2, 1) * sn


def _bdot_nt(a, b):
    return lax.dot_general(a, b, (((2,), (2,)), ((0,), (0,))), preferred_element_type=F32)


def _bdot(a, b):
    return lax.dot_general(a, b, (((2,), (1,)), ((0,), (0,))), preferred_element_type=F32)


def _bdot_tn(a, b):
    return lax.dot_general(a, b, (((1,), (1,)), ((0,), (0,))), preferred_element_type=F32)


STRIDE_MAX = 4


def _permute(buf, x, dil, dst, off=0):
    ln = S // dil
    if dil == 1:
        dst[pl.ds(off, S), :] = x.astype(dst.dtype)
        return
    buf[0] = x
    if dil <= STRIDE_MAX:
        for c in range(dil):
            dst[pl.ds(off + c * ln, ln), :] = buf.at[0][pl.ds(c, ln, stride=dil), :].astype(dst.dtype)
        return
    f, r = STRIDE_MAX, dil // STRIDE_MAX
    part = S // f
    for c1 in range(f):
        buf.at[1][pl.ds(c1 * part, part), :] = buf.at[0][pl.ds(c1, part, stride=f), :]
    for c1 in range(f):
        for c2 in range(r):
            dst[pl.ds(off + (c1 + f * c2) * ln, ln), :] = (
                buf.at[1][pl.ds(c1 * part + c2, ln, stride=r), :].astype(dst.dtype))


def _unpermute(buf, xp, dil, dst):
    ln = S // dil
    if dil == 1:
        dst[...] = xp
        return
    if dil <= STRIDE_MAX:
        for c in range(dil):
            dst[pl.ds(c, ln, stride=dil), :] = xp[c * ln:(c + 1) * ln]
        return
    f, r = STRIDE_MAX, dil // STRIDE_MAX
    part = S // f
    for c1 in range(f):
        for c2 in range(r):
            c = c1 + f * c2
            buf.at[1][pl.ds(c1 * part + c2, ln, stride=r), :] = xp[c * ln:(c + 1) * ln]
    for c1 in range(f):
        dst[pl.ds(c1, part, stride=f), :] = buf[1, pl.ds(c1 * part, part), :]


def _blocks3(ref, off=0):
    return ref[pl.ds(off, S), :].reshape(NQB, QB, HD)


def _att_prep(q_ref, k_ref, v_ref, cos_ref, sin_ref, qn, kn, dil, nat, qs, ksp, vsp):
    cs = cos_ref[...]
    sn = sin_ref[...]
    zero = jnp.zeros((QB, HD), BF16)
    ksp[pl.ds(0, QB), :] = zero
    vsp[pl.ds(0, QB), :] = zero
    _permute(nat, _rope(_rms_head(q_ref[0].astype(F32), qn), cs, sn), dil, qs)
    _permute(nat, _rope(_rms_head(k_ref[0].astype(F32), kn), cs, sn), dil, ksp, QB)
    _permute(nat, v_ref[0].astype(F32), dil, vsp, QB)


def _att_scores(qs, ksp, dil):
    nb = S // dil // QB
    q3 = _blocks3(qs)
    shape = (NQB, QB, QB)
    qi = lax.broadcasted_iota(jnp.int32, shape, 1)
    kj = lax.broadcasted_iota(jnp.int32, shape, 2)
    s_c = jnp.where(qi >= kj, _bdot_nt(q3, _blocks3(ksp, QB)) * SCALE, NEG)
    if nb == 1:
        return q3, s_c, None
    jj = lax.broadcasted_iota(jnp.int32, shape, 0)
    ok = (kj >= qi) & ((jj & (nb - 1)) != 0)
    s_p = jnp.where(ok, _bdot_nt(q3, _blocks3(ksp)) * SCALE, NEG)
    return q3, s_c, s_p


def _qkv_spec(kind, g):
    base = OFF_Q // HD + (3 * g + kind) * NH
    return pl.BlockSpec((1, S, HD), lambda b, h: (b, 0, base + h))


def _attn_fwd(proj3, cos_t, sin_t, q_norm, k_norm):
    def body(*refs):
        qkv_refs = refs[:9]
        (cos_ref, sin_ref, qn_ref, kn_ref, att_ref, lse_ref, w_ref, qp_ref, kp_ref, vp_ref,
         nat, qs, ksp, vsp, og) = refs[9:]
        for g, (window, dil) in enumerate(PATTERNS):
            q_ref, k_ref, v_ref = qkv_refs[3 * g:3 * g + 3]
            _att_prep(q_ref, k_ref, v_ref, cos_ref, sin_ref, qn_ref[g:g + 1, :], kn_ref[g:g + 1, :], dil,
                      nat, qs, ksp, vsp)
            qp_ref[g, 0] = qs[...]
            kp_ref[g, 0] = ksp[pl.ds(QB, S), :]
            vp_ref[g, 0] = vsp[pl.ds(QB, S), :]
            _, s_c, s_p = _att_scores(qs, ksp, dil)
            m = jnp.max(s_c, axis=-1, keepdims=True)
            if s_p is not None:
                m = jnp.maximum(m, jnp.max(s_p, axis=-1, keepdims=True))
            e_c = jnp.exp(s_c - m)
            den = jnp.sum(e_c, axis=-1, keepdims=True)
            o = _bdot(e_c.astype(BF16), _blocks3(vsp, QB))
            if s_p is not None:
                e_p = jnp.exp(s_p - m)
                den = den + jnp.sum(e_p, axis=-1, keepdims=True)
                o = o + _bdot(e_p.astype(BF16), _blocks3(vsp))
            _unpermute(nat, (o / den).reshape(S, HD), dil, og.at[g])
            _unpermute(nat, jnp.broadcast_to(m + jnp.log(den), (NQB, QB, HD)).reshape(S, HD), dil,
                       lse_ref.at[g, 0])
        l0 = lse_ref[0, 0]
        l1 = lse_ref[1, 0]
        l2 = lse_ref[2, 0]
        mx = jnp.maximum(jnp.maximum(l0, l1), l2)
        e0 = jnp.exp(l0 - mx)
        e1 = jnp.exp(l1 - mx)
        e2 = jnp.exp(l2 - mx)
        inv = 1.0 / (e0 + e1 + e2)
        w0 = e0 * inv
        w1 = e1 * inv
        w2 = e2 * inv
        w_ref[0, 0] = w0
        w_ref[1, 0] = w1
        w_ref[2, 0] = w2
        att_ref[0] = w0 * og[0] + w1 * og[1] + w2 * og[2]

    in_specs = [_qkv_spec(kind, g) for g in range(NG) for kind in range(3)]
    in_specs += [pl.BlockSpec((S, HD), lambda b, h: (0, 0)), pl.BlockSpec((S, HD), lambda b, h: (0, 0)),
                 pl.BlockSpec((NG, HD), lambda b, h: (0, 0)), pl.BlockSpec((NG, HD), lambda b, h: (0, 0))]
    stat = lambda: pl.BlockSpec((NG, 1, S, HD), lambda b, h: (0, b, 0, h))
    return pl.pallas_call(
        body, grid=(BL, NH), name="attn_fwd",
        in_specs=in_specs,
        out_specs=[pl.BlockSpec((1, S, HD), lambda b, h: (b, 0, h)), stat(), stat(), stat(), stat(), stat()],
        out_shape=[jax.ShapeDtypeStruct((BL, S, ATT), F32),
                   jax.ShapeDtypeStruct((NG, BL, S, ATT), F32),
                   jax.ShapeDtypeStruct((NG, BL, S, ATT), F32)]
        + [jax.ShapeDtypeStruct((NG, BL, S, ATT), BF16)] * 3,
        scratch_shapes=[pltpu.VMEM((2, S, HD), F32), pltpu.VMEM((S, HD), BF16), pltpu.VMEM((S + QB, HD), BF16),
                        pltpu.VMEM((S + QB, HD), BF16), pltpu.VMEM((NG, S, HD), F32)],
        compiler_params=_cp(("parallel", "parallel"), VMEM_BIG),
    )(*_hbm(*([proj3] * 9), cos_t, sin_t, q_norm, k_norm))


def _attn_bwd(proj3, cos_t, sin_t, q_norm, k_norm, lse, wts, qkv_p, datt3, sbar3):
    def norm_rope_bwd(dpost, raw, gain, cs, sn):
        dn = _rope_t(dpost, cs, sn)
        rstd = lax.rsqrt(jnp.mean(raw * raw, axis=-1, keepdims=True) + EPS)
        xh = raw * rstd
        dgain = jnp.sum(dn * xh, axis=0, keepdims=True)
        gd = dn * gain
        draw = rstd * (gd - xh * jnp.mean(gd * xh, axis=-1, keepdims=True))
        return draw, dgain

    def group_body(g, refs, first):
        dil = PATTERNS[g][1]
        (q_ref, k_ref, qp_ref, kp_ref, vp_ref, cos_ref, sin_ref, qn_ref, kn_ref, lse_ref, w_ref, datt_ref,
         sbar_ref, dq_ref, dk_ref, dv_ref, dqn_ref, dkn_ref, nat, ksp, vsp, dos, cvp, lsp, acc) = refs
        qn = qn_ref[g:g + 1, :]
        kn = kn_ref[g:g + 1, :]
        cs = cos_ref[...]
        sn = sin_ref[...]
        qs = qp_ref.at[0, 0]
        zero = jnp.zeros((QB, HD), BF16)
        ksp[pl.ds(0, QB), :] = zero
        vsp[pl.ds(0, QB), :] = zero
        ksp[pl.ds(QB, S), :] = kp_ref[0, 0]
        vsp[pl.ds(QB, S), :] = vp_ref[0, 0]
        wv = w_ref[0, 0]
        _permute(nat, wv * datt_ref[0], dil, dos)
        _permute(nat, wv * sbar_ref[0], dil, cvp)
        _permute(nat, lse_ref[0, 0], dil, lsp)
        q3, s_c, s_p = _att_scores(qs, ksp, dil)
        do3 = _blocks3(dos)
        lse3 = _blocks3(lsp)[:, :, 0:1]
        cv3 = _blocks3(cvp)[:, :, 0:1]
        p_c = jnp.exp(s_c - lse3)
        ds_c = (p_c * (_bdot_nt(do3, _blocks3(vsp, QB)) - cv3)).astype(BF16)
        dq = _bdot(ds_c, _blocks3(ksp, QB))
        acc[0] = _bdot_tn(ds_c, q3).reshape(S, HD)
        acc[1] = _bdot_tn(p_c.astype(BF16), do3).reshape(S, HD)
        if s_p is not None:
            p_p = jnp.exp(s_p - lse3)
            ds_p = (p_p * (_bdot_nt(do3, _blocks3(vsp)) - cv3)).astype(BF16)
            dq = dq + _bdot(ds_p, _blocks3(ksp))
            early = pl.ds(0, S - QB)
            acc[0, early, :] += _bdot_tn(ds_p, q3).reshape(S, HD)[QB:]
            acc[1, early, :] += _bdot_tn(p_p.astype(BF16), do3).reshape(S, HD)[QB:]
        _unpermute(nat, (dq * SCALE).reshape(S, HD), dil, nat.at[0])
        draw, dqn = norm_rope_bwd(nat[0], q_ref[0].astype(F32), qn, cs, sn)
        dq_ref[0] = draw.astype(BF16)
        _unpermute(nat, acc[0] * SCALE, dil, nat.at[0])
        draw, dkn = norm_rope_bwd(nat[0], k_ref[0].astype(F32), kn, cs, sn)
        dk_ref[0] = draw.astype(BF16)
        _unpermute(nat, acc[1], dil, nat.at[0])
        dv_ref[0] = nat[0].astype(BF16)
        dqn8 = jnp.broadcast_to(dqn, (SUBLANES, HD))
        dkn8 = jnp.broadcast_to(dkn, (SUBLANES, HD))

        @pl.when(first)
        def _():
            dqn_ref[0] = dqn8
            dkn_ref[0] = dkn8

        @pl.when(jnp.logical_not(first))
        def _():
            dqn_ref[0] += dqn8
            dkn_ref[0] += dkn8

    def body(*refs):
        group = pl.program_id(0)
        first = (pl.program_id(1) == 0) & (pl.program_id(2) == 0)
        for g in range(NG):
            pl.when(group == g)(lambda g=g: group_body(g, refs, first))

    def raw_spec(kind):
        return pl.BlockSpec((1, S, HD), lambda g, b, h: (b, 0, OFF_Q // HD + (3 * g + kind) * NH + h))

    full = lambda r: pl.BlockSpec((r, HD), lambda g, b, h: (0, 0))
    stat = lambda: pl.BlockSpec((1, 1, S, HD), lambda g, b, h: (g, b, 0, h))
    slab = lambda: pl.BlockSpec((1, S, HD), lambda g, b, h: (b, 0, h))
    out_slab = lambda: pl.BlockSpec((1, S, HD), lambda g, b, h: (b, 0, g * NH + h))
    gains = lambda: pl.BlockSpec((1, SUBLANES, HD), lambda g, b, h: (g, 0, 0))
    big = jax.ShapeDtypeStruct((BL, S, GW), BF16)
    vecs = jax.ShapeDtypeStruct((NG, SUBLANES, HD), F32)
    return pl.pallas_call(
        body, grid=(NG, BL, NH), name="attn_bwd",
        in_specs=[raw_spec(0), raw_spec(1), stat(), stat(), stat(), full(S), full(S), full(NG), full(NG),
                  stat(), stat(), slab(), slab()],
        out_specs=[out_slab(), out_slab(), out_slab(), gains(), gains()],
        out_shape=[big, big, big, vecs, vecs],
        scratch_shapes=[pltpu.VMEM((2, S, HD), F32), pltpu.VMEM((S + QB, HD), BF16),
                        pltpu.VMEM((S + QB, HD), BF16), pltpu.VMEM((S, HD), BF16), pltpu.VMEM((S, HD), F32),
                        pltpu.VMEM((S, HD), F32), pltpu.VMEM((2, S, HD), F32)],
        compiler_params=_cp(("arbitrary", "arbitrary", "arbitrary"), VMEM_BIG),
    )(*_hbm(proj3, proj3, *qkv_p, cos_t, sin_t, q_norm, k_norm, lse, wts, datt3, sbar3))


def _tail(x, proj, h, att, p, tgt, w_o_rnn, w_o_att_t, w_out, w_pg, w_ple_t, norm_ple, b_pg, tm=256):
    nt = T // tm
    inv_d = 1.0 / D

    def body(x_ref, h_ref, zr_ref, att_ref, za_ref, g0a_ref, g0b_ref, g1a_ref, g1b_ref, p_ref, tgt_ref,
             np_ref, bpg_ref, wor_hbm, woa_hbm, wout_hbm, wpg_hbm, wple_hbm,
             dx1_ref, merged_ref, n1_ref, dpre_ref, dpe_ref, dyr_ref, dya_ref, slab_a_ref, slab_c_ref, dh_ref,
             datt_ref, sbar_ref, yrnn_ref, yatt_ref, loss_ref, dnp_ref, dbpg_ref,
             wor, woa, wout, wpg, wple, sems):
        first = pl.program_id(0) == 0

        @pl.when(first)
        def _():
            pairs = ((wor_hbm, wor), (woa_hbm, woa), (wout_hbm, wout), (wpg_hbm, wpg), (wple_hbm, wple))
            _copy_together([pltpu.make_async_copy(src, dst, sems.at[k]) for k, (src, dst) in enumerate(pairs)])

        xv = x_ref[...]
        hv = h_ref[...]
        zr = zr_ref[...].astype(F32)
        av = att_ref[...]
        za = za_ref[...].astype(F32)
        szr = _sigmoid(zr)
        silu_r = zr * szr
        yrnn_b = (hv * silu_r).astype(BF16)
        sza = _sigmoid(za)
        silu_a = za * sza
        yatt_b = (av * silu_a).astype(BF16)
        yrnn_ref[...] = yrnn_b
        yatt_ref[...] = yatt_b
        yr = _dot(yrnn_b, wor[...])
        ya = _dot_nt(yatt_b, woa[...])
        g0 = _sigmoid(jnp.concatenate([g0a_ref[...], g0b_ref[...]], axis=1).astype(F32))
        g1 = _sigmoid(jnp.concatenate([g1a_ref[...], g1b_ref[...]], axis=1).astype(F32))
        merged_b = (g0 * yr + g1 * ya).astype(BF16)
        merged_ref[...] = merged_b
        x1 = xv + _dot(merged_b, wout[...])
        rstd = lax.rsqrt(jnp.mean(x1 * x1, axis=-1, keepdims=True) + EPS)
        xh = x1 * rstd
        npl = np_ref[...]
        n1_b = (xh * npl).astype(BF16)
        n1_ref[...] = n1_b
        pg = _sigmoid(_dot(n1_b, wpg[...]) + bpg_ref[...])
        pe = _dot_nt(p_ref[...].astype(BF16), wple[...])
        err = x1 + pg * pe - tgt_ref[...]
        loss_t = 0.5 * inv_d * jnp.sum(err * err)
        dy = err * inv_d
        dpe_ref[...] = (dy * pg).astype(BF16)
        dpre = dy * pe * pg * (1.0 - pg)
        dpre_b = dpre.astype(BF16)
        dpre_ref[...] = dpre_b
        dn1 = _dot_nt(dpre_b, wpg[...])
        dnp = jnp.sum(dn1 * xh, axis=0, keepdims=True)
        dbpg = jnp.sum(dpre, axis=0, keepdims=True)
        gd = dn1 * npl
        dx1 = dy + rstd * (gd - xh * jnp.mean(gd * xh, axis=-1, keepdims=True))
        dx1_ref[...] = dx1
        dmerged = _dot_nt(dx1.astype(BF16), wout[...])
        dyr_b = (dmerged * g0).astype(BF16)
        dya_b = (dmerged * g1).astype(BF16)
        dyr_ref[...] = dyr_b
        dya_ref[...] = dya_b
        slab_c_ref[:, ATT:ATT + D] = (dmerged * yr * g0 * (1.0 - g0)).astype(BF16)
        slab_c_ref[:, ATT + D:ATT + 2 * D] = (dmerged * ya * g1 * (1.0 - g1)).astype(BF16)
        dyrnn = _dot_nt(dyr_b, wor[...])
        dyatt = _dot(dya_b, woa[...])
        dh_ref[...] = dyrnn * silu_r
        slab_a_ref[...] = (dyrnn * hv * szr * (1.0 + zr * (1.0 - szr))).astype(BF16)
        datt = dyatt * silu_a
        datt_ref[...] = datt
        slab_c_ref[:, 0:ATT] = (dyatt * av * sza * (1.0 + za * (1.0 - sza))).astype(BF16)
        da = datt * av
        for hh in range(NH):
            seg = slice(hh * HD, (hh + 1) * HD)
            sbar_ref[:, seg] = jnp.broadcast_to(jnp.sum(da[:, seg], axis=-1, keepdims=True), (tm, HD))

        @pl.when(first)
        def _():
            loss_ref[...] = jnp.full((8, LANES), loss_t, F32)
            dnp_ref[...] = dnp
            dbpg_ref[...] = dbpg

        @pl.when(jnp.logical_not(first))
        def _():
            loss_ref[...] += jnp.full((8, LANES), loss_t, F32)
            dnp_ref[...] += dnp
            dbpg_ref[...] += dbpg

    tok = lambda w: pl.BlockSpec((tm, w), lambda i: (i, 0))
    col = lambda w, blk: pl.BlockSpec((tm, w), lambda i: (i, blk))
    vec = lambda: pl.BlockSpec((1, D), lambda i: (0, 0))
    hbm = lambda: pl.BlockSpec(memory_space=pl.ANY)
    gb = OFF_G // 512
    in_specs = [tok(D), tok(DR), col(DR, 1), tok(ATT), col(ATT, OFF_ZA // ATT),
                col(512, gb), col(512, gb + 1), col(512, gb + 2), col(512, gb + 3),
                tok(PLE), tok(D), vec(), vec(), hbm(), hbm(), hbm(), hbm(), hbm()]
    sh = lambda w, dt: jax.ShapeDtypeStruct((T, w), dt)
    out_shape = [sh(D, F32), sh(D, BF16), sh(D, BF16), sh(D, BF16), sh(D, BF16), sh(D, BF16), sh(D, BF16),
                 sh(A_W, BF16), sh(C_W, BF16), sh(DR, F32), sh(ATT, F32), sh(ATT, F32),
                 sh(DR, BF16), sh(ATT, BF16),
                 jax.ShapeDtypeStruct((8, LANES), F32), jax.ShapeDtypeStruct((1, D), F32),
                 jax.ShapeDtypeStruct((1, D), F32)]
    out_specs = [tok(D), tok(D), tok(D), tok(D), tok(D), tok(D), tok(D), col(DR, 1), tok(C_W), tok(DR),
                 tok(ATT), tok(ATT), tok(DR), tok(ATT),
                 pl.BlockSpec((8, LANES), lambda i: (0, 0)), vec(), vec()]
    return pl.pallas_call(
        body, grid=(nt,), name="tail_fwd_bwd",
        in_specs=in_specs, out_specs=out_specs, out_shape=out_shape,
        scratch_shapes=[pltpu.VMEM((DR, D), BF16), pltpu.VMEM((D, ATT), BF16), pltpu.VMEM((D, D), BF16),
                        pltpu.VMEM((D, D), BF16), pltpu.VMEM((D, PLE), BF16), pltpu.SemaphoreType.DMA((5,))],
        compiler_params=_cp(("arbitrary",), VMEM_BIG),
    )(*_hbm(x, h, proj, att, proj, proj, proj, proj, proj, p, tgt, norm_ple, b_pg, w_o_rnn, w_o_att_t, w_out, w_pg,
            w_ple_t))


def _rope_tables():
    pos = jnp.arange(S, dtype=F32)
    inv_freq = ROPE_THETA ** (-jnp.arange(0, HD, 2, dtype=F32) / HD)
    ang = pos[:, None] * inv_freq[None, :]
    cos, sin = jnp.cos(ang), jnp.sin(ang)
    return jnp.concatenate([cos, cos], axis=1), jnp.concatenate([-sin, sin], axis=1)


def _local_step(x, p, tgt, project, other_weights, norm_mix, conv_b,
                w_rg_a, b_rg_a, w_rg_x, b_rg_x, lam, q_norm, k_norm, norm_ple, b_pg, start_reduce=None,
                entry_token=None):
    if start_reduce is None:
        start_reduce = lambda arrs, tag: (jnp.zeros((8, LANES), F32), arrs)
    if entry_token is None:
        entry_token = jnp.zeros((8, LANES), F32)
    cos_t, sin_t = _rope_tables()
    wa_b = w_rg_a.astype(BF16)
    wx_b = w_rg_x.astype(BF16)

    hn = _rmsnorm_fwd(x, norm_mix, entry_token)
    proj, w_bufs, chips, conv_w, token = project(hn)
    proj3 = proj.reshape(BL, S, NIN)
    h3, gates = _rnn_fwd(proj3, conv_w, conv_b, wa_b, b_rg_a, wx_b, b_rg_x, lam, token)
    att3, lse, wts, *qkv_p = _attn_fwd(proj3, cos_t, sin_t, q_norm, k_norm)
    w_o_rnn, w_o_att_t, w_out, w_pg, w_ple_t = other_weights(att3)
    (dx1, merged, n1, dpre, dpe, dyr, dya, slab_a, slab_c, dh, datt, sbar, yrnn, yatt, loss8, dnp, dbpg) = _tail(
        x, proj, h3.reshape(T, DR), att3.reshape(T, ATT), p, tgt, w_o_rnn, w_o_att_t, w_out, w_pg, w_ple_t,
        norm_ple, b_pg)

    token, pending_out = start_reduce([
        _mm_tn(yrnn, dyr, 640, T, "dw_o_rnn"),
        _mm_tn(dya, yatt, 512, T, "dw_o_att_t"),
        _mm_tn(merged, dx1, 512, T // 2, "dw_out"),
        _mm_tn(n1, dpre, 512, T, "dw_ple_gate"),
        _mm_tn(dpe, p, 512, T, "dw_ple_t")], "out")

    slab_a3, dcw, dcb, dwa, dba, dwx, dbx, dlam = _rnn_bwd(
        proj3, h3, dh.reshape(BL, S, DR), gates, slab_a.reshape(BL, S, A_W), conv_w, conv_b, wa_b, b_rg_a, wx_b, b_rg_x, lam,
        token)
    datt3 = datt.reshape(BL, S, ATT)
    sbar3 = sbar.reshape(BL, S, ATT)
    *slabs, dqn, dkn = _attn_bwd(proj3, cos_t, sin_t, q_norm, k_norm, lse, wts, qkv_p, datt3, sbar3)
    pieces =[slab_a3.reshape(T, A_W)] + [t.reshape(T, GW) for t in slabs] + [slab_c]
    dw_in_t, db_in = _dw_in(pieces, hn)
    token, pending_in = start_reduce([dw_in_t], "in")
    grad_x, dnm = _grad_x(pieces, w_bufs, chips, token, x, dx1, norm_mix)

    small = dict(w_rg_a=dwa, w_rg_x=dwx, norm_mix=dnm, b_in=db_in, conv_b=dcb, b_rg_a=dba, b_rg_x=dbx,
                 lru_lambda=dlam, q_norm=dqn, k_norm=dkn, norm_ple=dnp, b_ple_gate=dbpg, conv_w=dcw, loss=loss8)
    return grad_x, pending_out, pending_in, small


MESH = pl.DeviceIdType.MESH
HBM_SPEC = pl.BlockSpec(memory_space=pl.ANY)


def _my_pos():
    return lax.axis_index("x"), lax.axis_index("y"), lax.axis_index("c")


def _flip(pos, k):
    x, y, c = pos
    return (1 - x if k & 4 else x, 1 - y if k & 2 else y, 1 - c if k & 1 else c)


def _lin(pos):
    return 4 * pos[0] + 2 * pos[1] + pos[2]


def _chip(pos):
    return 2 * pos[0] + pos[1]


def _exchange_within_chip(parts, name):
    na = len(parts)

    def body(*refs):
        a_refs = refs[:na]
        recv_refs = refs[na:2 * na]
        send_sems, recv_sems = refs[2 * na:]
        me = _my_pos()
        c = me[2]
        sibling = _flip(me, 1)
        remote = []
        for i in range(na):
            for q in range(NCHIP):
                rc = pltpu.make_async_remote_copy(
                    src_ref=a_refs[i].at[q, 1 - c], dst_ref=recv_refs[i].at[q],
                    send_sem=send_sems.at[NCHIP * i + q], recv_sem=recv_sems.at[NCHIP * i + q],
                    device_id=sibling, device_id_type=MESH)
                rc.start()
                remote.append(rc)
        for rc in remote:
            rc.wait_recv()
        for rc in remote:
            rc.wait_send()

    return pl.pallas_call(
        body, name=name, out_shape=[jax.ShapeDtypeStruct((NCHIP,) + a.shape[2:], a.dtype) for a in parts],
        in_specs=[HBM_SPEC] * na, out_specs=[HBM_SPEC] * na,
        scratch_shapes=[pltpu.SemaphoreType.DMA((NCHIP * na,)), pltpu.SemaphoreType.DMA((NCHIP * na,))],
    )(*parts)


HBM_ONLY = pl.BlockSpec(memory_space=pltpu.HBM)
SEM_SPEC = pl.BlockSpec(memory_space=pltpu.SEMAPHORE)
SPLIT_COPY = pltpu.CompilerParams(has_side_effects=pltpu.SideEffectType.DATAFLOW_SIDE_EFFECTING)


def _chip_peers(me):
    return [_flip(me, 4), _flip(me, 2), _flip(me, 6)]


def _between_chips_start(parts, name):
    na = len(parts)

    def body(*refs):
        a_refs = refs[:na]
        land_refs = refs[na:2 * na]
        send_sems, recv_sems = refs[2 * na], refs[2 * na + 1]
        token = refs[-1]
        me = _my_pos()
        myq = _chip(me)
        for i in range(na):
            for j, peer in enumerate(_chip_peers(me)):
                pltpu.make_async_remote_copy(
                    src_ref=a_refs[i].at[_chip(peer)], dst_ref=land_refs[i].at[myq],
                    send_sem=send_sems.at[3 * i + j], recv_sem=recv_sems.at[3 * i + j],
                    device_id=peer, device_id_type=MESH).start()
        token[...] = jnp.zeros_like(token)

    hbm = [pltpu.HBM(a.shape, a.dtype) for a in parts]
    srcs = [pltpu.with_memory_space_constraint(a, pltpu.HBM) for a in parts]
    lands = [pltpu.with_memory_space_constraint(lax.empty(a.shape, a.dtype), pltpu.HBM) for a in parts]
    res = pl.pallas_call(
        body, name=name,
        out_shape=(pltpu.SemaphoreType.DMA((3 * na,)), pltpu.SemaphoreType.DMA((3 * na,)), *hbm, *hbm,
                   jax.ShapeDtypeStruct((8, LANES), F32)),
        in_specs=[HBM_ONLY] * (2 * na),
        out_specs=(SEM_SPEC, SEM_SPEC, *([HBM_ONLY] * (2 * na)), pl.BlockSpec(memory_space=pltpu.VMEM)),
        input_output_aliases={i: 2 + i for i in range(2 * na)},
        compiler_params=SPLIT_COPY,
    )(*srcs, *lands)
    return res[-1], (res[0], res[1], list(res[2:2 + na]), list(res[2 + na:2 + 2 * na]))


def _between_chips_wait(pending, after, name):
    send_sems, recv_sems, parts, lands = pending
    na = len(parts)

    def body(*refs):
        a_refs = refs[:na]
        land_refs = refs[na:2 * na]
        send_sems, recv_sems = refs[2 * na], refs[2 * na + 1]
        me = _my_pos()
        for i in range(na):
            for j, peer in enumerate(_chip_peers(me)):
                cp = pltpu.make_async_remote_copy(
                    src_ref=a_refs[i].at[_chip(peer)], dst_ref=land_refs[i].at[_chip(peer)],
                    send_sem=send_sems.at[3 * i + j], recv_sem=recv_sems.at[3 * i + j],
                    device_id=peer, device_id_type=MESH)
                cp.wait_send()
                cp.wait_recv()

    hbm = [pltpu.HBM(a.shape, a.dtype) for a in parts]
    res = pl.pallas_call(
        body, name=name, out_shape=(*hbm, *hbm),
        in_specs=[HBM_ONLY] * (2 * na) + [SEM_SPEC, SEM_SPEC, pl.BlockSpec(memory_space=pl.ANY)],
        out_specs=[HBM_ONLY] * (2 * na),
        input_output_aliases={i: i for i in range(2 * na)},
        compiler_params=SPLIT_COPY,
    )(*parts, *lands, send_sems, recv_sems, after)
    return list(res[:na]), list(res[na:])


def _remote(src, dst, send_sems, recv_sems, idx, peer):
    return pltpu.make_async_remote_copy(src_ref=src, dst_ref=dst, send_sem=send_sems.at[idx],
                                        recv_sem=recv_sems.at[idx], device_id=peer, device_id_type=MESH)


def _copies_own(bufs, me):
    return [(bufs[0], bufs[1].at[me[2]], 0, _flip(me, 1))]


def _copies_near(bufs, me):
    return [(bufs[0], bufs[1].at[0, me[2]], 0, _flip(me, 2)), (bufs[0], bufs[1].at[1, me[2]], 1, _flip(me, 4))]


def _copies_far(bufs, me):
    return [(bufs[0], bufs[1].at[me[2]], 0, _flip(me, 6))]


def _copies_others(bufs, me):
    na = len(bufs) // 2
    return [(bufs[i], bufs[na + i].at[_lin(me)], 7 * i + k - 1, _flip(me, k))
            for i in range(na) for k in range(1, NDEV)]


def _copies_exchange(bufs, me):
    na = len(bufs) // 2
    return [(bufs[i].at[_lin(_flip(me, k))], bufs[na + i].at[_lin(me)], 7 * i + k - 1, _flip(me, k))
            for i in range(na) for k in range(1, NDEV)]


GROUP_COPIES = dict(own=_copies_own, near=_copies_near, far=_copies_far, others=_copies_others,
                    exchange=_copies_exchange)
GROUP_COUNT = dict(own=1, near=2, far=1)
TO_ALL = ("others", "exchange")


def _gather_start(bufs, groups, after, name):
    nb = len(bufs)
    ng = len(groups)

    def body(*refs):
        b = refs[:nb]
        sems = refs[nb + 1:nb + 1 + 2 * ng]
        token = refs[-1]
        me = _my_pos()
        for gi, (group, idx) in enumerate(groups):
            for src, dst, k, peer in GROUP_COPIES[group]([b[i] for i in idx], me):
                _remote(src, dst, sems[2 * gi], sems[2 * gi + 1], k, peer).start()
        token[...] = jnp.zeros_like(token)

    sem_t = []
    for group, idx in groups:
        cnt = 7 * (len(idx) // 2) if group in TO_ALL else GROUP_COUNT[group]
        sem_t += [pltpu.SemaphoreType.DMA((cnt,)), pltpu.SemaphoreType.DMA((cnt,))]
    ins = [pltpu.with_memory_space_constraint(a, pltpu.HBM) for a in bufs]
    res = pl.pallas_call(
        body, name=name,
        out_shape=(*sem_t, *[pltpu.HBM(a.shape, a.dtype) for a in bufs], jax.ShapeDtypeStruct((8, LANES), F32)),
        in_specs=[HBM_ONLY] * nb + [pl.BlockSpec(memory_space=pl.ANY)],
        out_specs=(*([SEM_SPEC] * (2 * ng)), *([HBM_ONLY] * nb), pl.BlockSpec(memory_space=pltpu.VMEM)),
        input_output_aliases={i: 2 * ng + i for i in range(nb)},
        compiler_params=SPLIT_COPY,
    )(*ins, after)
    return res[-1], list(res[2 * ng:2 * ng + nb]), [(res[2 * gi], res[2 * gi + 1]) for gi in range(ng)]


def _gather_wait(group, send_sems, recv_sems, bufs, after, name):
    nb = len(bufs)
    copies = GROUP_COPIES[group]

    def body(*refs):
        b = refs[:nb]
        ss, rs = refs[nb], refs[nb + 1]
        me = _my_pos()
        for src, dst, idx, peer in copies(b, me):
            if group in TO_ALL:
                landed = b[nb // 2 + idx // 7].at[_lin(peer)]
            elif group == "own":
                landed = b[1].at[1 - me[2]]
            else:
                landed = dst
            cp = _remote(src, landed, ss, rs, idx, peer)
            cp.wait_send()
            cp.wait_recv()

    res = pl.pallas_call(
        body, name=name, out_shape=[pltpu.HBM(a.shape, a.dtype) for a in bufs],
        in_specs=[HBM_ONLY] * nb + [SEM_SPEC, SEM_SPEC, pl.BlockSpec(memory_space=pl.ANY)],
        out_specs=[HBM_ONLY] * nb,
        input_output_aliases={i: i for i in range(nb)},
        compiler_params=SPLIT_COPY,
    )(*bufs, send_sems, recv_sems, after)
    return list(res)


def _forward_to_sibling(buf, name):
    n = buf.shape[0]

    def body(_in_ref, out_ref, send_sems, recv_sems):
        me = _my_pos()
        c = me[2]
        sibling = _flip(me, 1)
        sends = []
        for r in range(n):
            cp = _remote(out_ref.at[r, c], out_ref.at[r, c], send_sems, recv_sems, r, sibling)
            cp.start()
            sends.append(cp)
        for r in range(n):
            _remote(out_ref.at[r, c], out_ref.at[r, 1 - c], send_sems, recv_sems, r, sibling).wait_recv()
        for cp in sends:
            cp.wait_send()

    return pl.pallas_call(
        body, name=name, out_shape=jax.ShapeDtypeStruct(buf.shape, buf.dtype),
        in_specs=[HBM_SPEC], out_specs=HBM_SPEC,
        scratch_shapes=[pltpu.SemaphoreType.DMA((n,)), pltpu.SemaphoreType.DMA((n,))],
        input_output_aliases={0: 0},
    )(buf)


def _scalar(v):
    return jnp.asarray(v, jnp.int32).reshape(1)


def _sum_pairs(parts, theirs, name):
    na = len(parts)

    def body(c_ref, *refs):
        for i in range(na):
            o_ref = refs[2 * na + i]
            o_ref[0] = (refs[i][0, 0].astype(F32) + refs[na + i][0].astype(F32)).astype(o_ref.dtype)

    def mine_spec(a):
        return pl.BlockSpec((1, 1) + a.shape[2:], lambda q, c_ref: (q, c_ref[0], 0, 0))

    def spec(a):
        return pl.BlockSpec((1,) + a.shape[1:], lambda q, c_ref: (q, 0, 0))

    return pl.pallas_call(
        body, name=name,
        grid_spec=pltpu.PrefetchScalarGridSpec(
            num_scalar_prefetch=1, grid=(NCHIP,),
            in_specs=[mine_spec(a) for a in parts] + [spec(a) for a in theirs],
            out_specs=[spec(a) for a in theirs]),
        out_shape=[jax.ShapeDtypeStruct(a.shape, a.dtype) for a in theirs],
        compiler_params=_cp(("arbitrary",), VMEM_BIG),
    )(_scalar(lax.axis_index("c")), *_hbm(*parts, *theirs))


def _others(q, mine, nblk=NCHIP):
    return jnp.where(q == mine, (q + 1) % nblk, q)


def _sum_chips_adamw(own, recv, wv, mv, vv, token, tr, name):
    _, r, w = recv.shape

    def body(q_ref, own_ref, r0, r1, r2, r3, w_ref, m_ref, v_ref, _token, g_ref, d_ref, m2_ref, v2_ref):
        myq = q_ref[0]
        acc = None
        for q, r_ref in enumerate((r0, r1, r2, r3)):
            term = jnp.where(myq == q, own_ref[0], r_ref[0]).astype(F32)
            acc = term if acc is None else acc + term
        g_ref[...] = acc
        delta, m2, v2 = _adam_math(w_ref[...], acc, m_ref[...], v_ref[...])
        d_ref[...] = delta
        m2_ref[...] = m2
        v2_ref[...] = v2

    def recv_spec(q):
        return pl.BlockSpec((1, tr, w), lambda i, q_ref: (_others(q, q_ref[0]), i, 0))

    rows = lambda: pl.BlockSpec((tr, w), lambda i, q_ref: (i, 0))
    shp = jax.ShapeDtypeStruct((r, w), F32)
    return pl.pallas_call(
        body, name=name,
        grid_spec=pltpu.PrefetchScalarGridSpec(
            num_scalar_prefetch=1, grid=(r // tr,),
            in_specs=[pl.BlockSpec((1, tr, w), lambda i, q_ref: (q_ref[0], i, 0))]
            + [recv_spec(q) for q in range(NCHIP)] + [rows(), rows(), rows()]
            + [pl.BlockSpec((8, LANES), lambda i, q_ref: (0, 0))],
            out_specs=[rows(), rows(), rows(), rows()]),
        out_shape=[shp, shp, shp, shp],
        compiler_params=_cp(("arbitrary",), VMEM_MID),
    )(_scalar(_chip(_my_pos())), *_hbm(own, recv, recv, recv, recv, wv, mv, vv, token))


def _sum_blocks_small(own, recv, mine, transpose, name):
    na = len(recv)
    nblk = recv[0].shape[0]

    def body(q_ref, *refs):
        me = q_ref[0]
        for i in range(na):
            acc = None
            for q in range(nblk):
                term = jnp.where(me == q, refs[i][0], refs[na * (1 + q) + i][0]).astype(F32)
                acc = term if acc is None else acc + term
            refs[na * (1 + nblk) + i][...] = acc.T if transpose[i] else acc

    def oshape(a, tr):
        r, w = a.shape[1:]
        return (w, r) if tr else (r, w)

    own_spec = lambda a: pl.BlockSpec((1,) + a.shape[1:], lambda s, q_ref: (q_ref[0], 0, 0))
    recv_spec = lambda a, q: pl.BlockSpec((1,) + a.shape[1:], lambda s, q_ref: (_others(q, q_ref[0], nblk), 0, 0))
    out_spec = lambda shp: pl.BlockSpec(shp, lambda s, q_ref: (0, 0))
    in_specs = [own_spec(a) for a in own]
    for q in range(nblk):
        in_specs += [recv_spec(a, q) for a in recv]
    return pl.pallas_call(
        body, name=name,
        grid_spec=pltpu.PrefetchScalarGridSpec(
            num_scalar_prefetch=1, grid=(1,), in_specs=in_specs,
            out_specs=[out_spec(oshape(a, tr)) for a, tr in zip(recv, transpose)]),
        out_shape=[jax.ShapeDtypeStruct(oshape(a, tr), F32) for a, tr in zip(recv, transpose)],
        compiler_params=_cp(("arbitrary",), VMEM_MID),
    )(_scalar(mine), *_hbm(*own, *(list(recv) * nblk)))


def _rep_offsets():
    offs = []
    o = 0
    for r in REP_ROWS:
        offs.append(o)
        o += r
    return offs


LOSS_ROW = REP_TOTAL_ROWS


def _pack_small_grads(g):
    offs = _rep_offsets()

    def body(dwa, dwx, dnm, dbin, dcb, dba, dbx, dlam, dqn, dkn, dnp, dbpg, loss, o_ref):
        o_ref[pl.ds(REP_TOTAL_ROWS - 2, NDEV * REP_ROWS_DEV - REP_TOTAL_ROWS + 2), :] = jnp.zeros(
            (NDEV * REP_ROWS_DEV - REP_TOTAL_ROWS + 2, LANES), F32)
        o_ref[pl.ds(LOSS_ROW, 1), :] = loss[0:1, :]
        for n in range(NRB):
            o_ref[pl.ds(offs[0] + n * RBW, RBW), :] = dwa[n]
            o_ref[pl.ds(offs[1] + n * RBW, RBW), :] = dwx[n]

        def put_vec(off, ref, rows):
            for k in range(rows):
                o_ref[pl.ds(off + k, 1), :] = ref[:, k * LANES:(k + 1) * LANES]

        put_vec(offs[2], dnm, REP_ROWS[2])
        put_vec(offs[3], dbin, REP_ROWS[3])
        put_vec(offs[4], dcb, REP_ROWS[4])
        put_vec(offs[5], dba, REP_ROWS[5])
        put_vec(offs[6], dbx, REP_ROWS[6])
        put_vec(offs[7], dlam, REP_ROWS[7])
        for k in range(NG):
            o_ref[pl.ds(offs[8] + k, 1), :] = dqn[k, 0:1, :]
            o_ref[pl.ds(offs[9] + k, 1), :] = dkn[k, 0:1, :]
        put_vec(offs[10], dnp, REP_ROWS[10])
        put_vec(offs[11], dbpg, REP_ROWS[11])

    args = [g["w_rg_a"], g["w_rg_x"], g["norm_mix"], g["b_in"], g["conv_b"], g["b_rg_a"], g["b_rg_x"],
            g["lru_lambda"], g["q_norm"], g["k_norm"], g["norm_ple"], g["b_ple_gate"], g["loss"]]
    full = lambda shp: pl.BlockSpec(shp, lambda: (0,) * len(shp))
    return pl.pallas_call(
        body, name="pack_small_grads",
        in_specs=[full(a.shape) for a in args],
        out_specs=full((NDEV * REP_ROWS_DEV, LANES)),
        out_shape=jax.ShapeDtypeStruct((NDEV * REP_ROWS_DEV, LANES), F32),
    )(*_hbm(*args))


def _adam_math(wv, gv, mv, vv):
    c1 = 1.0 - B1 ** STEP
    c2 = 1.0 - B2 ** STEP
    m2 = B1 * mv + (1.0 - B1) * gv
    v2 = B2 * vv + (1.0 - B2) * (gv * gv)
    delta = (-LR) * ((m2 / c1) / (jnp.sqrt(v2 / c2) + AEPS) + WD * wv)
    return delta, m2, v2


def _adamw_small(rep_flat, w, m, v):
    offs = _rep_offsets()
    n = len(REP_NAMES)

    def body(*refs):
        g_ref = refs[0]
        w_refs = refs[1:1 + n]
        m_refs = refs[1 + n:1 + 2 * n]
        v_refs = refs[1 + 2 * n:1 + 3 * n]
        outs = refs[1 + 3 * n:]
        go, do, mo, vo = outs[:n], outs[n:2 * n], outs[2 * n:3 * n], outs[3 * n:]

        def emit(i, idx, gv):
            go[i][idx] = gv
            delta, m2, v2 = _adam_math(w_refs[i][idx], gv, m_refs[i][idx], v_refs[i][idx])
            do[i][idx] = delta
            mo[i][idx] = m2
            vo[i][idx] = v2

        for i in range(n):
            if i < 2:
                for b in range(NRB):
                    emit(i, b, g_ref[pl.ds(offs[i] + b * RBW, RBW), :])
            elif REP_NAMES[i] in ("q_norm", "k_norm"):
                emit(i, slice(None), g_ref[pl.ds(offs[i], NG), :])
            else:
                gv = jnp.concatenate([g_ref[pl.ds(offs[i] + k, 1), :] for k in range(REP_ROWS[i])], axis=1)
                emit(i, slice(None), gv)

    full = lambda shp: pl.BlockSpec(shp, lambda: (0,) * len(shp))
    pspecs = [full(a.shape) for a in w]
    pshapes = [jax.ShapeDtypeStruct(a.shape, F32) for a in w]
    res = pl.pallas_call(
        body, name="adamw_small",
        in_specs=[full(rep_flat.shape)] + pspecs * 3,
        out_specs=pspecs * 4, out_shape=pshapes * 4,
        compiler_params=_cp(None, VMEM_MID),
    )(*_hbm(rep_flat, *w, *m, *v))
    return res[:n], res[n:2 * n], res[2 * n:3 * n], res[3 * n:]


def _adamw_many(w, g, m, v, token):
    n = len(w)

    def body(*refs):
        for i in range(n):
            delta, m2, v2 = _adam_math(refs[i][...], refs[n + i][...], refs[2 * n + i][...], refs[3 * n + i][...])
            refs[4 * n + 1 + i][...] = delta
            refs[5 * n + 1 + i][...] = m2
            refs[6 * n + 1 + i][...] = v2

    full = lambda shp: pl.BlockSpec(shp, lambda: (0,) * len(shp))
    specs = [full(a.shape) for a in w]
    shapes = [jax.ShapeDtypeStruct(a.shape, F32) for a in w]
    res = pl.pallas_call(
        body, name="adamw_shards",
        in_specs=specs * 4 + [full((8, LANES))], out_specs=specs * 3, out_shape=shapes * 3,
        compiler_params=_cp(None, VMEM_MID),
    )(*_hbm(*w, *g, *m, *v, token))
    return res[:n], res[n:2 * n], res[2 * n:]


def kernel(x, p, norm_mix, w_in, b_in, conv_w, conv_b, w_rg_a, b_rg_a, w_rg_x, b_rg_x, lru_lambda, q_norm, k_norm, w_o_rnn, w_o_att, w_out, norm_ple, w_ple_gate, b_ple_gate, w_ple, loss_target, m_norm_mix, m_w_in, m_b_in, m_conv_w, m_conv_b, m_w_rg_a, m_b_rg_a, m_w_rg_x, m_b_rg_x, m_lru_lambda, m_q_norm, m_k_norm, m_w_o_rnn, m_w_o_att, m_w_out, m_norm_ple, m_w_ple_gate, m_b_ple_gate, m_w_ple, v_norm_mix, v_w_in, v_b_in, v_conv_w, v_conv_b, v_w_rg_a, v_b_rg_a, v_w_rg_x, v_b_rg_x, v_lru_lambda, v_q_norm, v_k_norm, v_w_o_rnn, v_w_o_att, v_w_out, v_norm_ple, v_w_ple_gate, v_b_ple_gate, v_w_ple):
    w = dict(norm_mix=norm_mix, w_in=w_in, b_in=b_in, conv_w=conv_w, conv_b=conv_b, w_rg_a=w_rg_a, b_rg_a=b_rg_a,
             w_rg_x=w_rg_x, b_rg_x=b_rg_x, lru_lambda=lru_lambda, q_norm=q_norm, k_norm=k_norm, w_o_rnn=w_o_rnn,
             w_o_att=w_o_att, w_out=w_out, norm_ple=norm_ple, w_ple_gate=w_ple_gate, b_ple_gate=b_ple_gate,
             w_ple=w_ple)
    m = dict(norm_mix=m_norm_mix, w_in=m_w_in, b_in=m_b_in, conv_w=m_conv_w, conv_b=m_conv_b, w_rg_a=m_w_rg_a,
             b_rg_a=m_b_rg_a, w_rg_x=m_w_rg_x, b_rg_x=m_b_rg_x, lru_lambda=m_lru_lambda, q_norm=m_q_norm,
             k_norm=m_k_norm, w_o_rnn=m_w_o_rnn, w_o_att=m_w_o_att, w_out=m_w_out, norm_ple=m_norm_ple,
             w_ple_gate=m_w_ple_gate, b_ple_gate=m_b_ple_gate, w_ple=m_w_ple)
    v = dict(norm_mix=v_norm_mix, w_in=v_w_in, b_in=v_b_in, conv_w=v_conv_w, conv_b=v_conv_b, w_rg_a=v_w_rg_a,
             b_rg_a=v_b_rg_a, w_rg_x=v_w_rg_x, b_rg_x=v_b_rg_x, lru_lambda=v_lru_lambda, q_norm=v_q_norm,
             k_norm=v_k_norm, w_o_rnn=v_w_o_rnn, w_o_att=v_w_o_att, w_out=v_w_out, norm_ple=v_norm_ple,
             w_ple_gate=v_w_ple_gate, b_ple_gate=v_b_ple_gate, w_ple=v_w_ple)
    names = list(w.keys())

    shards = [w_in[0].T.astype(BF16), w_o_rnn[0].astype(BF16), w_o_att[0].T.astype(BF16), w_out[0].astype(BF16),
              w_ple_gate[0].astype(BF16), w_ple[0].T.astype(BF16), conv_w[0]]
    pos = _my_pos()
    me, my_core, my_chip = _lin(pos), pos[2], _chip(pos)
    hbm_empty = lambda shp, dt: lax.empty(shp, dt)
    w_shard, conv_shard = shards[0], shards[6]
    shp = w_shard.shape
    entry_token, bufs, sems = _gather_start(
        [w_shard, hbm_empty((2,) + shp, BF16), hbm_empty((2, 2) + shp, BF16), conv_shard,
         hbm_empty((NDEV,) + conv_shard.shape, F32)],
        [("own", (0, 1)), ("near", (0, 2)), ("others", (3, 4))], norm_mix, "gather_start_near")
    w_src, own_l, near_l, conv_src, conv_l = bufs
    sem_own, sem_near, sem_conv = sems
    gather_out = {}

    def project(hn):
        w_thru, own = _gather_wait("own", *sem_own, [w_src, own_l], hn, "gather_wait_own")
        own = lax.dynamic_update_slice(own, w_shard[None], (my_core, 0, 0)).reshape(1, CHIP_COLS, D)
        chips = [jnp.stack([my_chip]), jnp.stack([my_chip ^ 1, my_chip ^ 2]), jnp.stack([my_chip ^ 3])]
        chips = [c.astype(jnp.int32) for c in chips]
        proj = _in_proj_chips(hn, own, b_in, chips[0], None, entry_token, "in_proj_own")
        w_thru, near = _gather_wait("near", *sem_near, [w_thru, near_l], proj, "gather_wait_near")
        near = _forward_to_sibling(near, "gather_forward_near")
        token, (w_thru, far_l), (sem_far,) = _gather_start(
            [w_thru, hbm_empty((2,) + shp, BF16)], [("far", (0, 1))], near, "gather_start_far")
        near = near.reshape(2, CHIP_COLS, D)
        proj = _in_proj_chips(hn, near, b_in, chips[1], proj, token, "in_proj_near")
        w_thru, far = _gather_wait("far", *sem_far, [w_thru, far_l], proj, "gather_wait_far")
        far = _forward_to_sibling(far[None], "gather_forward_far").reshape(1, CHIP_COLS, D)
        proj = _in_proj_chips(hn, far, b_in, chips[2], proj, token, "in_proj_far")
        conv_thru, conv_g = _gather_wait("others", *sem_conv, [conv_src, conv_l], proj, "gather_wait_conv")
        conv_g = lax.dynamic_update_slice(conv_g, conv_shard[None], (me, 0, 0))
        conv_f = conv_g.transpose(1, 0, 2).reshape(CONVW, DR)
        srcs = list(shards[1:6])
        token, obufs, (sem_out,) = _gather_start(
            srcs + [hbm_empty((NDEV,) + a.shape, BF16) for a in srcs], [("others", tuple(range(10)))], proj,
            "gather_start_out")
        gather_out.update(bufs=obufs, sems=sem_out)
        return proj, [own, near, far], jnp.concatenate(chips), conv_f, token

    def other_weights(after):
        obufs = _gather_wait("others", *gather_out["sems"], gather_out["bufs"], after, "gather_wait_out")
        full = [lax.dynamic_update_slice(a, s[None], (me, 0, 0)) for a, s in zip(obufs[5:], shards[1:6])]
        return [a.reshape((NDEV * a.shape[1], a.shape[2])) for a in full]

    def start_reduce(arrs, tag):
        if tag == "out":
            parts = [a.reshape((NDEV, a.shape[0] // NDEV, a.shape[1])) for a in arrs]
            token, bufs, (sems,) = _gather_start(
                parts + [lax.empty(a.shape, a.dtype) for a in parts], [("exchange", tuple(range(2 * len(parts))))],
                arrs[-1][:SUBLANES], "reduce_out_start")
            return token, (bufs, sems)
        parts = [a.reshape((NCHIP, 2, a.shape[0] // NDEV, a.shape[1])) for a in arrs]
        theirs = _exchange_within_chip(parts, "reduce_within_chip_" + tag)
        return _between_chips_start(_sum_pairs(parts, theirs, "sum_pairs_" + tag), "reduce_between_chips_start_" + tag)

    grad_x, pending_out, pending_in, small = _local_step(
        x.reshape(T, D), p.reshape(T, PLE), loss_target.reshape(T, D),
        project, other_weights,
        norm_mix, conv_b, w_rg_a[0], b_rg_a, w_rg_x[0], b_rg_x, lru_lambda, q_norm[0], k_norm[0],
        norm_ple, b_ple_gate, start_reduce, entry_token)

    rep_parts = _pack_small_grads(small).reshape(NDEV, REP_ROWS_DEV, LANES)
    conv_parts = small["conv_w"].reshape(CONVW, NDEV, DR // NDEV).transpose(1, 0, 2)
    smalls = [rep_parts, conv_parts]
    token, sbufs, (sem_x,) = _gather_start(
        smalls + [lax.empty(a.shape, F32) for a in smalls], [("exchange", (0, 1, 2, 3))], small["norm_mix"],
        "reduce_small_start")

    own_in, recv_in = _between_chips_wait(pending_in, token, "reduce_between_chips_wait_in")
    w_in_res = _sum_chips_adamw(own_in[0], recv_in[0], w_in[0].T, m_w_in[0].T, v_w_in[0].T, token, 304, "adamw_w_in")
    sbufs = _gather_wait("exchange", *sem_x, sbufs, w_in_res[0], "reduce_small_wait")
    g_rep, g_conv = _sum_blocks_small(sbufs[:2], sbufs[2:], me, (False, False), "sum_small")
    token, gbufs, (sem_g,) = _gather_start(
        [g_rep, lax.empty((NDEV,) + g_rep.shape, F32)], [("others", (0, 1))], g_conv, "gather_small_start")
    obufs = _gather_wait("exchange", *pending_out[1], pending_out[0], token, "reduce_out_wait")
    g_o_rnn, g_o_att, g_out, g_pg, g_ple = _sum_blocks_small(
        obufs[:5], obufs[5:], me, (False, True, False, False, True), "sum_out")

    grad, delta, new_m, new_v = {}, {}, {}, {}
    rest = ("w_o_rnn", "w_o_att", "w_out", "w_ple_gate", "w_ple", "conv_w")
    g_rest = [g_o_rnn, g_o_att, g_out, g_pg, g_ple, g_conv]
    rest_res = _adamw_many([w[n][0] for n in rest], g_rest, [m[n][0] for n in rest], [v[n][0] for n in rest], token)
    _, rep_all = _gather_wait("others", *sem_g, gbufs, rest_res[0][0], "gather_small_wait")
    rep_all = lax.dynamic_update_slice(rep_all, g_rep[None], (me, 0, 0)).reshape(NDEV * REP_ROWS_DEV, LANES)
    loss = rep_all[LOSS_ROW, 0]
    rep_shape = lambda a: a if a.ndim == 2 else a.reshape(a.shape[1:])
    res = _adamw_small(rep_all, [rep_shape(w[n]) for n in REP_NAMES], [rep_shape(m[n]) for n in REP_NAMES],
                       [rep_shape(v[n]) for n in REP_NAMES])
    for dst, vals in zip((grad, delta, new_m, new_v), res):
        for n, a in zip(REP_NAMES, vals):
            dst[n] = a.reshape(w[n].shape)
    grad["w_in"], delta["w_in"], new_m["w_in"], new_v["w_in"] = [a.T[None] for a in w_in_res]
    for n, a in zip(rest, g_rest):
        grad[n] = a[None]
    for dst, vals in zip((delta, new_m, new_v), rest_res):
        for n, a in zip(rest, vals):
            dst[n] = a[None]

    return (loss, grad_x.reshape(BL, S, D), *[grad[n] for n in names], *[delta[n] for n in names],
            *[new_m[n] for n in names], *[new_v[n] for n in names])
```

```python
import jax
import jax.numpy as jnp
import numpy as np
from jax import lax
from jax.experimental import pallas as pl
from jax.experimental.pallas import tpu as pltpu

F32 = jnp.float32
BF16 = jnp.bfloat16

D = 1024
S = 2048
BL = 2
T = BL * S
NDEV = 8
NCHIP = 4
PLE = 256
DR = 1280
NRB = 10
RBW = 128
CONVW = 4
LRU_C = 8.0
HD = 128
NH = 4
PATTERNS = ((128, 1), (512, 4), (2048, 16))
NG = 3
ATT = NH * HD
GW = NG * ATT
NIN = 2 * DR + 3 * GW + ATT + 2 * D
OFF_ZR = DR
OFF_Q = 2 * DR
OFF_ZA = OFF_Q + 3 * GW
OFF_G = OFF_ZA + ATT
ROPE_THETA = 10000.0
EPS = 1e-6
SCALE = HD ** -0.5
NEG = -1e30
QB = 128
LANES = 128
CT = 512
NCT = NIN // CT
A_W = 2 * DR
C_W = ATT + 2 * D

LR, B1, B2, AEPS, WD, STEP = 0.001, 0.9, 0.999, 1e-08, 0.01, 10

NSHARD_IN = NIN // NDEV
REP_NAMES = ("w_rg_a", "w_rg_x", "norm_mix", "b_in", "conv_b", "b_rg_a", "b_rg_x", "lru_lambda", "q_norm",
             "k_norm", "norm_ple", "b_ple_gate")
REP_ROWS = (NRB * RBW, NRB * RBW, D // LANES, NIN // LANES, DR // LANES, DR // LANES, DR // LANES, DR // LANES,
            NG, NG, D // LANES, D // LANES)
REP_TOTAL_ROWS = sum(REP_ROWS)
REP_ROWS_DEV = 344
BIG_NAMES = ("w_in", "w_o_rnn", "w_o_att", "w_out", "w_ple_gate", "w_ple")

VMEM_BIG = 56 * 1024 * 1024
VMEM_MID = 40 * 1024 * 1024


def _cp(sem=None, vmem=None):
    return pltpu.CompilerParams(dimension_semantics=sem, vmem_limit_bytes=vmem)


def _hbm(*arrays):
    return [pltpu.with_memory_space_constraint(a, pltpu.HBM) for a in arrays]


def _copy_together(copies):
    for cp in copies:
        cp.start()
    for cp in copies:
        cp.wait()


def _dot(a, b):
    return jnp.dot(a, b, preferred_element_type=F32)


def _dot_nt(a, b):
    return lax.dot_general(a, b, (((1,), (1,)), ((), ())), preferred_element_type=F32)


def _dot_tn(a, b):
    return lax.dot_general(a, b, (((0,), (0,)), ((), ())), preferred_element_type=F32)


def _sigmoid(x):
    return jax.nn.sigmoid(x)


def _perm(j):
    jq = j - OFF_Q // CT
    inside = (j >= OFF_Q // CT) & (j < OFF_ZA // CT)
    return jnp.where(inside, OFF_Q // CT + (jq % 3) * 3 + jq // 3, j)


PIECES = ((0, A_W // CT), (OFF_Q // CT, GW // CT), (OFF_Q // CT + 3, GW // CT), (OFF_Q // CT + 6, GW // CT),
          (OFF_ZA // CT, C_W // CT))


def _rmsnorm_fwd(x, gain, token, tm=512):
    def body(x_ref, g_ref, _token, o_ref):
        xv = x_ref[...]
        var = jnp.mean(xv * xv, axis=-1, keepdims=True)
        o_ref[...] = (xv * lax.rsqrt(var + EPS) * g_ref[...]).astype(BF16)

    return pl.pallas_call(
        body, grid=(T // tm,), name="rmsnorm_fwd",
        in_specs=[pl.BlockSpec((tm, D), lambda i: (i, 0)), pl.BlockSpec((1, D), lambda i: (0, 0)),
                  pl.BlockSpec((8, LANES), lambda i: (0, 0))],
        out_specs=pl.BlockSpec((tm, D), lambda i: (i, 0)),
        out_shape=jax.ShapeDtypeStruct((T, D), BF16),
        compiler_params=_cp(("parallel",)),
    )(*_hbm(x, gain, token))


CHIP_COLS = NIN // NCHIP


def _in_proj_chips(hn, w_rows, bias, chips, proj, token, name, tm=1024):
    n = w_rows.shape[0]

    def body(chips_ref, a_ref, w_ref, b_ref, _token, *rest):
        o_ref = rest[-1]
        o_ref[...] = (_dot_nt(a_ref[...], w_ref[0]) + b_ref[...]).astype(BF16)

    in_specs = [pl.BlockSpec((tm, D), lambda s, i, ch: (i, 0)),
                pl.BlockSpec((1, CHIP_COLS, D), lambda s, i, ch: (s, 0, 0)),
                pl.BlockSpec((1, CHIP_COLS), lambda s, i, ch: (0, ch[s])),
                pl.BlockSpec((8, LANES), lambda s, i, ch: (0, 0))]
    args = [hn, w_rows, bias, token]
    aliases = {}
    if proj is not None:
        in_specs.append(pl.BlockSpec(memory_space=pl.ANY))
        args.append(proj)
        aliases = {5: 0}
    return pl.pallas_call(
        body, name=name,
        grid_spec=pltpu.PrefetchScalarGridSpec(
            num_scalar_prefetch=1, grid=(n, T // tm), in_specs=in_specs,
            out_specs=pl.BlockSpec((tm, CHIP_COLS), lambda s, i, ch: (i, ch[s]))),
        out_shape=jax.ShapeDtypeStruct((T, NIN), BF16),
        input_output_aliases=aliases,
        compiler_params=_cp(("arbitrary", "arbitrary"), VMEM_BIG),
    )(chips, *_hbm(*args))


def _grad_x(pieces, w_bufs, chips, token, x, dx1, gain, tm=512):
    nb = len(w_bufs)

    def body(chips_ref, a_ref, q_ref, k_ref, v_ref, c_ref, *rest):
        w_hbm = rest[:nb]
        x_ref, dx1_ref, g_ref, dx_ref, dg_ref, w, sems = rest[nb + 1:]
        first = pl.program_id(0) == 0

        @pl.when(first)
        def _():
            s = 0
            copies = []
            for buf in w_hbm:
                for r in range(buf.shape[0]):
                    row = pl.multiple_of(chips_ref[s] * CHIP_COLS, 128)
                    copies.append(pltpu.make_async_copy(buf.at[r], w.at[pl.ds(row, CHIP_COLS), :], sems.at[s]))
                    s += 1
            _copy_together(copies)

        acc = _dot(a_ref[...], w[pl.ds(0, A_W), :])
        for kind, p_ref in enumerate((q_ref, k_ref, v_ref)):
            for g in range(NG):
                row = OFF_Q + (3 * g + kind) * CT
                acc = acc + _dot(p_ref[:, g * CT:(g + 1) * CT], w[pl.ds(row, CT), :])
        dn = acc + _dot(c_ref[...], w[pl.ds(OFF_ZA, C_W), :])
        xv = x_ref[...]
        rstd = lax.rsqrt(jnp.mean(xv * xv, axis=-1, keepdims=True) + EPS)
        xh = xv * rstd
        dg = jnp.sum(dn * xh, axis=0, keepdims=True)
        gd = dn * g_ref[...]
        dx_ref[...] = dx1_ref[...] + rstd * (gd - xh * jnp.mean(gd * xh, axis=-1, keepdims=True))

        @pl.when(first)
        def _():
            dg_ref[...] = dg

        @pl.when(jnp.logical_not(first))
        def _():
            dg_ref[...] += dg

    tok = lambda wd: pl.BlockSpec((tm, wd), lambda i, ch: (i, 0))
    vec = lambda: pl.BlockSpec((1, D), lambda i, ch: (0, 0))
    return pl.pallas_call(
        body, name="grad_x",
        grid_spec=pltpu.PrefetchScalarGridSpec(
            num_scalar_prefetch=1, grid=(T // tm,),
            in_specs=[tok(A_W), tok(GW), tok(GW), tok(GW), tok(C_W)] + [pl.BlockSpec(memory_space=pl.ANY)] * nb
            + [pl.BlockSpec((8, LANES), lambda i, ch: (0, 0)), tok(D), tok(D), vec()],
            out_specs=[tok(D), vec()],
            scratch_shapes=[pltpu.VMEM((NIN, D), BF16), pltpu.SemaphoreType.DMA((NCHIP,))]),
        out_shape=[jax.ShapeDtypeStruct((T, D), F32), jax.ShapeDtypeStruct((1, D), F32)],
        compiler_params=_cp(("arbitrary",), VMEM_BIG),
    )(chips, *_hbm(*pieces, *w_bufs, token, x, dx1, gain))


def _dw_in(pieces, hn):
    def body(a_ref, q_ref, k_ref, v_ref, c_ref, h_hbm, o_ref, s_ref, h):
        j = pl.program_id(0)

        @pl.when(j == 0)
        def _():
            pltpu.sync_copy(h_hbm, h)

        def step(x_ref):
            xv = x_ref[...]
            o_ref[...] = _dot_tn(xv, h[...]).astype(BF16)
            s_ref[...] = jnp.sum(xv.astype(F32), axis=0, keepdims=True)

        for x_ref, (lo, n) in zip((a_ref, q_ref, k_ref, v_ref, c_ref), PIECES):
            pl.when((j >= lo) & (j < lo + n))(lambda x_ref=x_ref: step(x_ref))

    def piece_spec(lo, n):
        return pl.BlockSpec((T, CT), lambda j: (0, jnp.clip(j - lo, 0, n - 1)))

    return pl.pallas_call(
        body, grid=(NCT,), name="dw_in",
        in_specs=[piece_spec(lo, n) for lo, n in PIECES] + [pl.BlockSpec(memory_space=pl.ANY)],
        out_specs=[pl.BlockSpec((CT, D), lambda j: (_perm(j), 0)), pl.BlockSpec((1, CT), lambda j: (0, _perm(j)))],
        out_shape=[jax.ShapeDtypeStruct((NIN, D), BF16), jax.ShapeDtypeStruct((1, NIN), F32)],
        scratch_shapes=[pltpu.VMEM((T, D), BF16)],
        compiler_params=_cp(("arbitrary",), VMEM_BIG),
    )(*_hbm(*pieces, hn))


def _mm_tn(a, b, ta, tt, name):
    m = a.shape[1]
    n = b.shape[1]
    nt = T // tt

    def body(a_ref, b_ref, o_ref, acc):
        t = pl.program_id(1)
        p = _dot_tn(a_ref[...].astype(BF16), b_ref[...].astype(BF16))
        if nt == 1:
            o_ref[...] = p.astype(BF16)
            return

        @pl.when(t == 0)
        def _():
            acc[...] = p

        @pl.when(t > 0)
        def _():
            acc[...] += p

        @pl.when(t == nt - 1)
        def _():
            o_ref[...] = acc[...].astype(BF16)

    return pl.pallas_call(
        body, grid=(m // ta, nt), name=name,
        in_specs=[pl.BlockSpec((tt, ta), lambda j, t: (t, j)), pl.BlockSpec((tt, n), lambda j, t: (t, 0))],
        out_specs=pl.BlockSpec((ta, n), lambda j, t: (j, 0)),
        out_shape=pltpu.HBM((m, n), BF16),
        scratch_shapes=[pltpu.VMEM((ta, n), F32)],
        compiler_params=_cp(("parallel", "arbitrary"), VMEM_MID),
    )(*_hbm(a, b))


def _row_iota():
    return lax.broadcasted_iota(jnp.int32, (S, RBW), 0)


SUBLANES = 8
N_SHIFT_BUFS = 4


class _Shifter:
    def __init__(self, bufs):
        self.bufs = bufs
        self.k = 0

    def _store(self, v, fill, front):
        b = self.bufs.at[self.k % N_SHIFT_BUFS]
        self.k += 1
        b[pl.ds(0 if front else SUBLANES + S, SUBLANES), :] = jnp.full((SUBLANES, RBW), fill, F32)
        b[pl.ds(SUBLANES, S), :] = v
        return b

    def down(self, v, ds, fill):
        b = self._store(v, fill, True)
        return [b[pl.ds(SUBLANES - d, S), :] for d in ds]

    def up(self, v, ds, fill):
        b = self._store(v, fill, False)
        return [b[pl.ds(SUBLANES + d, S), :] for d in ds]


def _shift_down(v, d, sh, fill):
    return sh.down(v, (d,), fill)[0]


def _shift_up(v, d, sh, fill):
    return sh.up(v, (d,), fill)[0]


def _scan_down(a, u, row):
    d = 1
    while d < S:
        last = 2 * d >= S
        if d < SUBLANES:
            u = a * _shift_down(u, d, row, 0.0) + u
            if not last:
                a = a * _shift_down(a, d, row, 1.0)
        else:
            u = jnp.concatenate([u[:d], a[d:] * u[:S - d] + u[d:]], axis=0)
            if not last:
                a = jnp.concatenate([a[:d], a[d:] * a[:S - d]], axis=0)
        d *= 2
    return u


def _scan_up(b, g, row):
    d = 1
    while d < S:
        last = 2 * d >= S
        if d < SUBLANES:
            g = g + b * _shift_up(g, d, row, 0.0)
            if not last:
                b = b * _shift_up(b, d, row, 0.0)
        else:
            g = jnp.concatenate([g[:S - d] + b[:S - d] * g[d:], g[S - d:]], axis=0)
            if not last:
                b = jnp.concatenate([b[:S - d] * b[d:], b[S - d:]], axis=0)
        d *= 2
    return g


def _softplus(x):
    return jnp.maximum(x, 0.0) + jnp.log1p(jnp.exp(-jnp.abs(x)))


N_SAVED = 5


def _rnn_gates(x, cw, cb, wa, ba, wx, bx, lam, row, pad, saved=None):
    xs = pad.down(x, (1, 2, 3), 0.0)
    if saved is not None:
        r, i, a, mult, xc = saved
        return xc, xc.astype(BF16), r, i, _softplus(-lam), a, mult, xs
    xc = cb + cw[3:4, :] * x
    for j in (1, 2, 3):
        xc = xc + cw[3 - j:4 - j, :] * xs[j - 1]
    xcb = xc.astype(BF16)
    r = _sigmoid(_dot(xcb, wa) + ba)
    i = _sigmoid(_dot(xcb, wx) + bx)
    sp = _softplus(-lam)
    log_a = (-LRU_C) * r * sp
    a = jnp.exp(log_a)
    mult = jnp.where(row == 0, 1.0, jnp.sqrt(jnp.tanh(-log_a) * (1.0 + a * a)))
    return xc, xcb, r, i, sp, a, mult, xs


def _rnn_fwd(proj3, conv_w, conv_b, wa, ba, wx, bx, lam, token):
    def body(x_ref, cw_ref, cb_ref, wa_ref, ba_ref, wx_ref, bx_ref, lam_ref, _token, h_ref, g_ref, pad):
        row = _row_iota()
        sh = _Shifter(pad)
        x = x_ref[0].astype(F32)
        xc, _, r, i, _, a, mult, _ = _rnn_gates(x, cw_ref[...], cb_ref[...], wa_ref[0], ba_ref[...],
                                             wx_ref[0], bx_ref[...], lam_ref[...], row, sh)
        for k, val in enumerate((r, i, a, mult, xc)):
            g_ref[k, 0] = val
        h_ref[0] = _scan_down(a, mult * (i * xc), sh)

    vec = lambda: pl.BlockSpec((1, RBW), lambda b, n: (0, n))
    mat = lambda: pl.BlockSpec((1, RBW, RBW), lambda b, n: (n, 0, 0))
    return pl.pallas_call(
        body, grid=(BL, NRB), name="rnn_fwd",
        in_specs=[pl.BlockSpec((1, S, RBW), lambda b, n: (b, 0, n)),
                  pl.BlockSpec((CONVW, RBW), lambda b, n: (0, n)),
                  vec(), mat(), vec(), mat(), vec(), vec(), pl.BlockSpec((8, LANES), lambda b, n: (0, 0))],
        out_specs=[pl.BlockSpec((1, S, RBW), lambda b, n: (b, 0, n)),
                   pl.BlockSpec((N_SAVED, 1, S, RBW), lambda b, n: (0, b, 0, n))],
        out_shape=[jax.ShapeDtypeStruct((BL, S, DR), F32), jax.ShapeDtypeStruct((N_SAVED, BL, S, DR), F32)],
        scratch_shapes=[pltpu.VMEM((N_SHIFT_BUFS, S + 2 * SUBLANES, RBW), F32)],
        compiler_params=_cp(("parallel", "parallel"), VMEM_MID),
    )(*_hbm(proj3, conv_w, conv_b, wa, ba, wx, bx, lam, token))


def _rnn_bwd(proj3, h3, dh3, gates, slab_a3, conv_w, conv_b, wa, ba, wx, bx, lam, token):
    def body(x_ref, h_ref, dh_ref, g_ref, cw_ref, cb_ref, wa_ref, ba_ref, wx_ref, bx_ref, lam_ref, _alias, _token,
             dx_ref, dcw_ref, dcb_ref, dwa_ref, dba_ref, dwx_ref, dbx_ref, dlam_ref, pad):
        row = _row_iota()
        sh = _Shifter(pad)
        x = x_ref[0].astype(F32)
        cw = cw_ref[...]
        wa_v = wa_ref[0]
        wx_v = wx_ref[0]
        lam_v = lam_ref[...]
        xc, xcb, r, i, sp, a, mult, xs = _rnn_gates(x, cw, cb_ref[...], wa_v, ba_ref[...], wx_v, bx_ref[...], lam_v,
                                                    row, sh, [g_ref[k, 0] for k in range(N_SAVED)])
        h = h_ref[0]
        g = _scan_up(_shift_up(a, 1, sh, 0.0), dh_ref[0], sh)
        da = g * _shift_down(h, 1, sh, 0.0)
        dmult = jnp.where(row == 0, 0.0, g * (i * xc))
        gm = g * mult
        di = gm * xc
        dxc = gm * i
        dlog_a = da * a - dmult * (a * a) / mult
        dr = dlog_a * ((-LRU_C) * sp)
        dsp = jnp.sum(dlog_a * ((-LRU_C) * r), axis=0, keepdims=True)
        dlam = dsp * (-_sigmoid(-lam_v))
        dpa = dr * r * (1.0 - r)
        dpx = di * i * (1.0 - i)
        dpab = dpa.astype(BF16)
        dpxb = dpx.astype(BF16)
        dwa = _dot_tn(xcb, dpab)
        dwx = _dot_tn(xcb, dpxb)
        dba = jnp.sum(dpa, axis=0, keepdims=True)
        dbx = jnp.sum(dpx, axis=0, keepdims=True)
        dxc = dxc + _dot_nt(dpab, wa_v) + _dot_nt(dpxb, wx_v)
        dcb = jnp.sum(dxc, axis=0, keepdims=True)
        dx = cw[3:4, :] * dxc
        dcw_rows = [None] * CONVW
        dcw_rows[3] = jnp.sum(dxc * x, axis=0, keepdims=True)
        dxc_up = sh.up(dxc, (1, 2, 3), 0.0)
        for j in (1, 2, 3):
            dx = dx + cw[3 - j:4 - j, :] * dxc_up[j - 1]
            dcw_rows[3 - j] = jnp.sum(dxc * xs[j - 1], axis=0, keepdims=True)
        dx_ref[0] = dx.astype(BF16)
        dcw = jnp.concatenate(dcw_rows, axis=0)
        first = pl.program_id(1) == 0

        @pl.when(first)
        def _():
            dcw_ref[...] = dcw
            dcb_ref[...] = dcb
            dwa_ref[0] = dwa
            dba_ref[...] = dba
            dwx_ref[0] = dwx
            dbx_ref[...] = dbx
            dlam_ref[...] = dlam

        @pl.when(jnp.logical_not(first))
        def _():
            dcw_ref[...] += dcw
            dcb_ref[...] += dcb
            dwa_ref[0] += dwa
            dba_ref[...] += dba
            dwx_ref[0] += dwx
            dbx_ref[...] += dbx
            dlam_ref[...] += dlam

    slab = lambda: pl.BlockSpec((1, S, RBW), lambda n, b: (b, 0, n))
    vec = lambda: pl.BlockSpec((1, RBW), lambda n, b: (0, n))
    mat = lambda: pl.BlockSpec((1, RBW, RBW), lambda n, b: (n, 0, 0))
    taps = lambda: pl.BlockSpec((CONVW, RBW), lambda n, b: (0, n))
    vshape = pltpu.HBM((1, DR), F32)
    mshape = pltpu.HBM((NRB, RBW, RBW), F32)
    return pl.pallas_call(
        body, grid=(NRB, BL), name="rnn_bwd",
        in_specs=[slab(), slab(), slab(), pl.BlockSpec((N_SAVED, 1, S, RBW), lambda n, b: (0, b, 0, n)),
                  taps(), vec(), mat(), vec(), mat(), vec(), vec(),
                  pl.BlockSpec(memory_space=pl.ANY), pl.BlockSpec((8, LANES), lambda n, b: (0, 0))],
        out_specs=[slab(), taps(), vec(), mat(), vec(), mat(), vec(), vec()],
        out_shape=[jax.ShapeDtypeStruct((BL, S, A_W), BF16), pltpu.HBM((CONVW, DR), F32),
                   vshape, mshape, vshape, mshape, vshape, vshape],
        input_output_aliases={11: 0},
        scratch_shapes=[pltpu.VMEM((N_SHIFT_BUFS, S + 2 * SUBLANES, RBW), F32)],
        compiler_params=_cp(("parallel", "arbitrary"), 48 * 1024 * 1024),
    )(*_hbm(proj3, h3, dh3, gates, conv_w, conv_b, wa, ba, wx, bx, lam, slab_a3, token))


NQB = S // QB


def _rms_head(t, gain):
    rstd = lax.rsqrt(jnp.mean(t * t, axis=-1, keepdims=True) + EPS)
    return t * rstd * gain


def _rope(t, cs, sn):
    return t * cs + pltpu.roll(t, HD // 2, 1) * sn


def _rope_t(dy, cs, sn):
    return dy * cs - pltpu.roll(dy, HD // 2, 1) * sn


def _bdot_nt(a, b):
    return lax.dot_general(a, b, (((2,), (2,)), ((0,), (0,))), preferred_element_type=F32)


def _bdot(a, b):
    return lax.dot_general(a, b, (((2,), (1,)), ((0,), (0,))), preferred_element_type=F32)


def _bdot_tn(a, b):
    return lax.dot_general(a, b, (((1,), (1,)), ((0,), (0,))), preferred_element_type=F32)


STRIDE_MAX = 4


def _permute(buf, x, dil, dst, off=0):
    ln = S // dil
    if dil == 1:
        dst[pl.ds(off, S), :] = x.astype(dst.dtype)
        return
    buf[0] = x
    if dil <= STRIDE_MAX:
        for c in range(dil):
            dst[pl.ds(off + c * ln, ln), :] = buf.at[0][pl.ds(c, ln, stride=dil), :].astype(dst.dtype)
        return
    f, r = STRIDE_MAX, dil // STRIDE_MAX
    part = S // f
    for c1 in range(f):
        buf.at[1][pl.ds(c1 * part, part), :] = buf.at[0][pl.ds(c1, part, stride=f), :]
    for c1 in range(f):
        for c2 in range(r):
            dst[pl.ds(off + (c1 + f * c2) * ln, ln), :] = (
                buf.at[1][pl.ds(c1 * part + c2, ln, stride=r), :].astype(dst.dtype))


def _unpermute(buf, xp, dil, dst):
    ln = S // dil
    if dil == 1:
        dst[...] = xp
        return
    if dil <= STRIDE_MAX:
        for c in range(dil):
            dst[pl.ds(c, ln, stride=dil), :] = xp[c * ln:(c + 1) * ln]
        return
    f, r = STRIDE_MAX, dil // STRIDE_MAX
    part = S // f
    for c1 in range(f):
        for c2 in range(r):
            c = c1 + f * c2
            buf.at[1][pl.ds(c1 * part + c2, ln, stride=r), :] = xp[c * ln:(c + 1) * ln]
    for c1 in range(f):
        dst[pl.ds(c1, part, stride=f), :] = buf[1, pl.ds(c1 * part, part), :]


def _blocks3(ref, off=0):
    return ref[pl.ds(off, S), :].reshape(NQB, QB, HD)


def _att_prep(q_ref, k_ref, v_ref, cos_ref, sin_ref, qn, kn, dil, nat, qs, ksp, vsp):
    cs = cos_ref[...]
    sn = sin_ref[...]
    zero = jnp.zeros((QB, HD), BF16)
    ksp[pl.ds(0, QB), :] = zero
    vsp[pl.ds(0, QB), :] = zero
    _permute(nat, _rope(_rms_head(q_ref[0].astype(F32), qn), cs, sn), dil, qs)
    _permute(nat, _rope(_rms_head(k_ref[0].astype(F32), kn), cs, sn), dil, ksp, QB)
    _permute(nat, v_ref[0].astype(F32), dil, vsp, QB)


def _att_scores(qs, ksp, dil):
    nb = S // dil // QB
    q3 = _blocks3(qs)
    shape = (NQB, QB, QB)
    qi = lax.broadcasted_iota(jnp.int32, shape, 1)
    kj = lax.broadcasted_iota(jnp.int32, shape, 2)
    s_c = jnp.where(qi >= kj, _bdot_nt(q3, _blocks3(ksp, QB)) * SCALE, NEG)
    if nb == 1:
        return q3, s_c, None
    jj = lax.broadcasted_iota(jnp.int32, shape, 0)
    ok = (kj >= qi) & ((jj & (nb - 1)) != 0)
    s_p = jnp.where(ok, _bdot_nt(q3, _blocks3(ksp)) * SCALE, NEG)
    return q3, s_c, s_p


def _qkv_spec(kind, g):
    base = OFF_Q // HD + (3 * g + kind) * NH
    return pl.BlockSpec((1, S, HD), lambda b, h: (b, 0, base + h))


def _attn_fwd(proj3, cos_t, sin_t, q_norm, k_norm):
    def body(*refs):
        qkv_refs = refs[:9]
        (cos_ref, sin_ref, qn_ref, kn_ref, att_ref, lse_ref, w_ref, qp_ref, kp_ref, vp_ref,
         nat, qs, ksp, vsp, og) = refs[9:]
        for g, (window, dil) in enumerate(PATTERNS):
            q_ref, k_ref, v_ref = qkv_refs[3 * g:3 * g + 3]
            _att_prep(q_ref, k_ref, v_ref, cos_ref, sin_ref, qn_ref[g:g + 1, :], kn_ref[g:g + 1, :], dil,
                      nat, qs, ksp, vsp)
            qp_ref[g, 0] = qs[...]
            kp_ref[g, 0] = ksp[pl.ds(QB, S), :]
            vp_ref[g, 0] = vsp[pl.ds(QB, S), :]
            _, s_c, s_p = _att_scores(qs, ksp, dil)
            m = jnp.max(s_c, axis=-1, keepdims=True)
            if s_p is not None:
                m = jnp.maximum(m, jnp.max(s_p, axis=-1, keepdims=True))
            e_c = jnp.exp(s_c - m)
            den = jnp.sum(e_c, axis=-1, keepdims=True)
            o = _bdot(e_c.astype(BF16), _blocks3(vsp, QB))
            if s_p is not None:
                e_p = jnp.exp(s_p - m)
                den = den + jnp.sum(e_p, axis=-1, keepdims=True)
                o = o + _bdot(e_p.astype(BF16), _blocks3(vsp))
            _unpermute(nat, (o / den).reshape(S, HD), dil, og.at[g])
            _unpermute(nat, jnp.broadcast_to(m + jnp.log(den), (NQB, QB, HD)).reshape(S, HD), dil,
                       lse_ref.at[g, 0])
        l0 = lse_ref[0, 0]
        l1 = lse_ref[1, 0]
        l2 = lse_ref[2, 0]
        mx = jnp.maximum(jnp.maximum(l0, l1), l2)
        e0 = jnp.exp(l0 - mx)
        e1 = jnp.exp(l1 - mx)
        e2 = jnp.exp(l2 - mx)
        inv = 1.0 / (e0 + e1 + e2)
        w0 = e0 * inv
        w1 = e1 * inv
        w2 = e2 * inv
        w_ref[0, 0] = w0
        w_ref[1, 0] = w1
        w_ref[2, 0] = w2
        att_ref[0] = w0 * og[0] + w1 * og[1] + w2 * og[2]

    in_specs = [_qkv_spec(kind, g) for g in range(NG) for kind in range(3)]
    in_specs += [pl.BlockSpec((S, HD), lambda b, h: (0, 0)), pl.BlockSpec((S, HD), lambda b, h: (0, 0)),
                 pl.BlockSpec((NG, HD), lambda b, h: (0, 0)), pl.BlockSpec((NG, HD), lambda b, h: (0, 0))]
    stat = lambda: pl.BlockSpec((NG, 1, S, HD), lambda b, h: (0, b, 0, h))
    return pl.pallas_call(
        body, grid=(BL, NH), name="attn_fwd",
        in_specs=in_specs,
        out_specs=[pl.BlockSpec((1, S, HD), lambda b, h: (b, 0, h)), stat(), stat(), stat(), stat(), stat()],
        out_shape=[jax.ShapeDtypeStruct((BL, S, ATT), F32),
                   jax.ShapeDtypeStruct((NG, BL, S, ATT), F32),
                   jax.ShapeDtypeStruct((NG, BL, S, ATT), F32)]
        + [jax.ShapeDtypeStruct((NG, BL, S, ATT), BF16)] * 3,
        scratch_shapes=[pltpu.VMEM((2, S, HD), F32), pltpu.VMEM((S, HD), BF16), pltpu.VMEM((S + QB, HD), BF16),
                        pltpu.VMEM((S + QB, HD), BF16), pltpu.VMEM((NG, S, HD), F32)],
        compiler_params=_cp(("parallel", "parallel"), VMEM_BIG),
    )(*_hbm(*([proj3] * 9), cos_t, sin_t, q_norm, k_norm))


def _attn_bwd(proj3, cos_t, sin_t, q_norm, k_norm, lse, wts, qkv_p, datt3, sbar3):
    def norm_rope_bwd(dpost, raw, gain, cs, sn):
        dn = _rope_t(dpost, cs, sn)
        rstd = lax.rsqrt(jnp.mean(raw * raw, axis=-1, keepdims=True) + EPS)
        xh = raw * rstd
        dgain = jnp.sum(dn * xh, axis=0, keepdims=True)
        gd = dn * gain
        draw = rstd * (gd - xh * jnp.mean(gd * xh, axis=-1, keepdims=True))
        return draw, dgain

    def group_body(g, refs, first):
        dil = PATTERNS[g][1]
        (q_ref, k_ref, qp_ref, kp_ref, vp_ref, cos_ref, sin_ref, qn_ref, kn_ref, lse_ref, w_ref, datt_ref,
         sbar_ref, dq_ref, dk_ref, dv_ref, dqn_ref, dkn_ref, nat, ksp, vsp, dos, cvp, lsp, acc) = refs
        qn = qn_ref[g:g + 1, :]
        kn = kn_ref[g:g + 1, :]
        cs = cos_ref[...]
        sn = sin_ref[...]
        qs = qp_ref.at[0, 0]
        zero = jnp.zeros((QB, HD), BF16)
        ksp[pl.ds(0, QB), :] = zero
        vsp[pl.ds(0, QB), :] = zero
        ksp[pl.ds(QB, S), :] = kp_ref[0, 0]
        vsp[pl.ds(QB, S), :] = vp_ref[0, 0]
        wv = w_ref[0, 0]
        _permute(nat, wv * datt_ref[0], dil, dos)
        _permute(nat, wv * sbar_ref[0], dil, cvp)
        _permute(nat, lse_ref[0, 0], dil, lsp)
        q3, s_c, s_p = _att_scores(qs, ksp, dil)
        do3 = _blocks3(dos)
        lse3 = _blocks3(lsp)[:, :, 0:1]
        cv3 = _blocks3(cvp)[:, :, 0:1]
        p_c = jnp.exp(s_c - lse3)
        ds_c = (p_c * (_bdot_nt(do3, _blocks3(vsp, QB)) - cv3)).astype(BF16)
        dq = _bdot(ds_c, _blocks3(ksp, QB))
        acc[0] = _bdot_tn(ds_c, q3).reshape(S, HD)
        acc[1] = _bdot_tn(p_c.astype(BF16), do3).reshape(S, HD)
        if s_p is not None:
            p_p = jnp.exp(s_p - lse3)
            ds_p = (p_p * (_bdot_nt(do3, _blocks3(vsp)) - cv3)).astype(BF16)
            dq = dq + _bdot(ds_p, _blocks3(ksp))
            early = pl.ds(0, S - QB)
            acc[0, early, :] += _bdot_tn(ds_p, q3).reshape(S, HD)[QB:]
            acc[1, early, :] += _bdot_tn(p_p.astype(BF16), do3).reshape(S, HD)[QB:]
        _unpermute(nat, (dq * SCALE).reshape(S, HD), dil, nat.at[0])
        draw, dqn = norm_rope_bwd(nat[0], q_ref[0].astype(F32), qn, cs, sn)
        dq_ref[0] = draw.astype(BF16)
        _unpermute(nat, acc[0] * SCALE, dil, nat.at[0])
        draw, dkn = norm_rope_bwd(nat[0], k_ref[0].astype(F32), kn, cs, sn)
        dk_ref[0] = draw.astype(BF16)
        _unpermute(nat, acc[1], dil, nat.at[0])
        dv_ref[0] = nat[0].astype(BF16)
        dqn8 = jnp.broadcast_to(dqn, (SUBLANES, HD))
        dkn8 = jnp.broadcast_to(dkn, (SUBLANES, HD))

        @pl.when(first)
        def _():
            dqn_ref[0] = dqn8
            dkn_ref[0] = dkn8

        @pl.when(jnp.logical_not(first))
        def _():
            dqn_ref[0] += dqn8
            dkn_ref[0] += dkn8

    def body(*refs):
        group = pl.program_id(0)
        first = (pl.program_id(1) == 0) & (pl.program_id(2) == 0)
        for g in range(NG):
            pl.when(group == g)(lambda g=g: group_body(g, refs, first))

    def raw_spec(kind):
        return pl.BlockSpec((1, S, HD), lambda g, b, h: (b, 0, OFF_Q // HD + (3 * g + kind) * NH + h))

    full = lambda r: pl.BlockSpec((r, HD), lambda g, b, h: (0, 0))
    stat = lambda: pl.BlockSpec((1, 1, S, HD), lambda g, b, h: (g, b, 0, h))
    slab = lambda: pl.BlockSpec((1, S, HD), lambda g, b, h: (b, 0, h))
    out_slab = lambda: pl.BlockSpec((1, S, HD), lambda g, b, h: (b, 0, g * NH + h))
    gains = lambda: pl.BlockSpec((1, SUBLANES, HD), lambda g, b, h: (g, 0, 0))
    big = jax.ShapeDtypeStruct((BL, S, GW), BF16)
    vecs = jax.ShapeDtypeStruct((NG, SUBLANES, HD), F32)
    return pl.pallas_call(
        body, grid=(NG, BL, NH), name="attn_bwd",
        in_specs=[raw_spec(0), raw_spec(1), stat(), stat(), stat(), full(S), full(S), full(NG), full(NG),
                  stat(), stat(), slab(), slab()],
        out_specs=[out_slab(), out_slab(), out_slab(), gains(), gains()],
        out_shape=[big, big, big, vecs, vecs],
        scratch_shapes=[pltpu.VMEM((2, S, HD), F32), pltpu.VMEM((S + QB, HD), BF16),
                        pltpu.VMEM((S + QB, HD), BF16), pltpu.VMEM((S, HD), BF16), pltpu.VMEM((S, HD), F32),
                        pltpu.VMEM((S, HD), F32), pltpu.VMEM((2, S, HD), F32)],
        compiler_params=_cp(("arbitrary", "arbitrary", "arbitrary"), VMEM_BIG),
    )(*_hbm(proj3, proj3, *qkv_p, cos_t, sin_t, q_norm, k_norm, lse, wts, datt3, sbar3))


def _tail(x, proj, h, att, p, tgt, w_o_rnn, w_o_att_t, w_out, w_pg, w_ple_t, norm_ple, b_pg, tm=256):
    nt = T // tm
    inv_d = 1.0 / D

    def body(x_ref, h_ref, zr_ref, att_ref, za_ref, g0a_ref, g0b_ref, g1a_ref, g1b_ref, p_ref, tgt_ref,
             np_ref, bpg_ref, wor_hbm, woa_hbm, wout_hbm, wpg_hbm, wple_hbm,
             dx1_ref, merged_ref, n1_ref, dpre_ref, dpe_ref, dyr_ref, dya_ref, slab_a_ref, slab_c_ref, dh_ref,
             datt_ref, sbar_ref, yrnn_ref, yatt_ref, loss_ref, dnp_ref, dbpg_ref,
             wor, woa, wout, wpg, wple, sems):
        first = pl.program_id(0) == 0

        @pl.when(first)
        def _():
            pairs = ((wor_hbm, wor), (woa_hbm, woa), (wout_hbm, wout), (wpg_hbm, wpg), (wple_hbm, wple))
            _copy_together([pltpu.make_async_copy(src, dst, sems.at[k]) for k, (src, dst) in enumerate(pairs)])

        xv = x_ref[...]
        hv = h_ref[...]
        zr = zr_ref[...].astype(F32)
        av = att_ref[...]
        za = za_ref[...].astype(F32)
        szr = _sigmoid(zr)
        silu_r = zr * szr
        yrnn_b = (hv * silu_r).astype(BF16)
        sza = _sigmoid(za)
        silu_a = za * sza
        yatt_b = (av * silu_a).astype(BF16)
        yrnn_ref[...] = yrnn_b
        yatt_ref[...] = yatt_b
        yr = _dot(yrnn_b, wor[...])
        ya = _dot_nt(yatt_b, woa[...])
        g0 = _sigmoid(jnp.concatenate([g0a_ref[...], g0b_ref[...]], axis=1).astype(F32))
        g1 = _sigmoid(jnp.concatenate([g1a_ref[...], g1b_ref[...]], axis=1).astype(F32))
        merged_b = (g0 * yr + g1 * ya).astype(BF16)
        merged_ref[...] = merged_b
        x1 = xv + _dot(merged_b, wout[...])
        rstd = lax.rsqrt(jnp.mean(x1 * x1, axis=-1, keepdims=True) + EPS)
        xh = x1 * rstd
        npl = np_ref[...]
        n1_b = (xh * npl).astype(BF16)
        n1_ref[...] = n1_b
        pg = _sigmoid(_dot(n1_b, wpg[...]) + bpg_ref[...])
        pe = _dot_nt(p_ref[...].astype(BF16), wple[...])
        err = x1 + pg * pe - tgt_ref[...]
        loss_t = 0.5 * inv_d * jnp.sum(err * err)
        dy = err * inv_d
        dpe_ref[...] = (dy * pg).astype(BF16)
        dpre = dy * pe * pg * (1.0 - pg)
        dpre_b = dpre.astype(BF16)
        dpre_ref[...] = dpre_b
        dn1 = _dot_nt(dpre_b, wpg[...])
        dnp = jnp.sum(dn1 * xh, axis=0, keepdims=True)
        dbpg = jnp.sum(dpre, axis=0, keepdims=True)
        gd = dn1 * npl
        dx1 = dy + rstd * (gd - xh * jnp.mean(gd * xh, axis=-1, keepdims=True))
        dx1_ref[...] = dx1
        dmerged = _dot_nt(dx1.astype(BF16), wout[...])
        dyr_b = (dmerged * g0).astype(BF16)
        dya_b = (dmerged * g1).astype(BF16)
        dyr_ref[...] = dyr_b
        dya_ref[...] = dya_b
        slab_c_ref[:, ATT:ATT + D] = (dmerged * yr * g0 * (1.0 - g0)).astype(BF16)
        slab_c_ref[:, ATT + D:ATT + 2 * D] = (dmerged * ya * g1 * (1.0 - g1)).astype(BF16)
        dyrnn = _dot_nt(dyr_b, wor[...])
        dyatt = _dot(dya_b, woa[...])
        dh_ref[...] = dyrnn * silu_r
        slab_a_ref[...] = (dyrnn * hv * szr * (1.0 + zr * (1.0 - szr))).astype(BF16)
        datt = dyatt * silu_a
        datt_ref[...] = datt
        slab_c_ref[:, 0:ATT] = (dyatt * av * sza * (1.0 + za * (1.0 - sza))).astype(BF16)
        da = datt * av
        for hh in range(NH):
            seg = slice(hh * HD, (hh + 1) * HD)
            sbar_ref[:, seg] = jnp.broadcast_to(jnp.sum(da[:, seg], axis=-1, keepdims=True), (tm, HD))

        @pl.when(first)
        def _():
            loss_ref[...] = jnp.full((8, LANES), loss_t, F32)
            dnp_ref[...] = dnp
            dbpg_ref[...] = dbpg

        @pl.when(jnp.logical_not(first))
        def _():
            loss_ref[...] += jnp.full((8, LANES), loss_t, F32)
            dnp_ref[...] += dnp
            dbpg_ref[...] += dbpg

    tok = lambda w: pl.BlockSpec((tm, w), lambda i: (i, 0))
    col = lambda w, blk: pl.BlockSpec((tm, w), lambda i: (i, blk))
    vec = lambda: pl.BlockSpec((1, D), lambda i: (0, 0))
    hbm = lambda: pl.BlockSpec(memory_space=pl.ANY)
    gb = OFF_G // 512
    in_specs = [tok(D), tok(DR), col(DR, 1), tok(ATT), col(ATT, OFF_ZA // ATT),
                col(512, gb), col(512, gb + 1), col(512, gb + 2), col(512, gb + 3),
                tok(PLE), tok(D), vec(), vec(), hbm(), hbm(), hbm(), hbm(), hbm()]
    sh = lambda w, dt: jax.ShapeDtypeStruct((T, w), dt)
    out_shape = [sh(D, F32), sh(D, BF16), sh(D, BF16), sh(D, BF16), sh(D, BF16), sh(D, BF16), sh(D, BF16),
                 sh(A_W, BF16), sh(C_W, BF16), sh(DR, F32), sh(ATT, F32), sh(ATT, F32),
                 sh(DR, BF16), sh(ATT, BF16),
                 jax.ShapeDtypeStruct((8, LANES), F32), jax.ShapeDtypeStruct((1, D), F32),
                 jax.ShapeDtypeStruct((1, D), F32)]
    out_specs = [tok(D), tok(D), tok(D), tok(D), tok(D), tok(D), tok(D), col(DR, 1), tok(C_W), tok(DR),
                 tok(ATT), tok(ATT), tok(DR), tok(ATT),
                 pl.BlockSpec((8, LANES), lambda i: (0, 0)), vec(), vec()]
    return pl.pallas_call(
        body, grid=(nt,), name="tail_fwd_bwd",
        in_specs=in_specs, out_specs=out_specs, out_shape=out_shape,
        scratch_shapes=[pltpu.VMEM((DR, D), BF16), pltpu.VMEM((D, ATT), BF16), pltpu.VMEM((D, D), BF16),
                        pltpu.VMEM((D, D), BF16), pltpu.VMEM((D, PLE), BF16), pltpu.SemaphoreType.DMA((5,))],
        compiler_params=_cp(("arbitrary",), VMEM_BIG),
    )(*_hbm(x, h, proj, att, proj, proj, proj, proj, proj, p, tgt, norm_ple, b_pg, w_o_rnn, w_o_att_t, w_out, w_pg,
            w_ple_t))


def _rope_tables():
    pos = np.arange(S, dtype=np.float32)
    inv_freq = np.float32(ROPE_THETA) ** (-np.arange(0, HD, 2, dtype=np.float32) / np.float32(HD))
    ang = (pos[:, None] * inv_freq[None, :]).astype(np.float32).astype(np.float64)
    cos, sin = np.cos(ang).astype(np.float32), np.sin(ang).astype(np.float32)
    return jnp.asarray(np.concatenate([cos, cos], axis=1)), jnp.asarray(np.concatenate([-sin, sin], axis=1))


def _local_step(x, p, tgt, project, other_weights, norm_mix, conv_b,
                w_rg_a, b_rg_a, w_rg_x, b_rg_x, lam, q_norm, k_norm, norm_ple, b_pg, start_reduce=None,
                entry_token=None):
    if start_reduce is None:
        start_reduce = lambda arrs, tag: (jnp.zeros((8, LANES), F32), arrs)
    if entry_token is None:
        entry_token = jnp.zeros((8, LANES), F32)
    cos_t, sin_t = _rope_tables()
    wa_b = w_rg_a.astype(BF16)
    wx_b = w_rg_x.astype(BF16)

    hn = _rmsnorm_fwd(x, norm_mix, entry_token)
    proj, w_bufs, chips, conv_w, token = project(hn)
    proj3 = proj.reshape(BL, S, NIN)
    h3, gates = _rnn_fwd(proj3, conv_w, conv_b, wa_b, b_rg_a, wx_b, b_rg_x, lam, token)
    att3, lse, wts, *qkv_p = _attn_fwd(proj3, cos_t, sin_t, q_norm, k_norm)
    w_o_rnn, w_o_att_t, w_out, w_pg, w_ple_t = other_weights(att3)
    (dx1, merged, n1, dpre, dpe, dyr, dya, slab_a, slab_c, dh, datt, sbar, yrnn, yatt, loss8, dnp, dbpg) = _tail(
        x, proj, h3.reshape(T, DR), att3.reshape(T, ATT), p, tgt, w_o_rnn, w_o_att_t, w_out, w_pg, w_ple_t,
        norm_ple, b_pg)

    token, pending_out = start_reduce([
        _mm_tn(yrnn, dyr, 640, T, "dw_o_rnn"),
        _mm_tn(dya, yatt, 512, T, "dw_o_att_t"),
        _mm_tn(merged, dx1, 512, T // 2, "dw_out"),
        _mm_tn(n1, dpre, 512, T, "dw_ple_gate"),
        _mm_tn(dpe, p, 512, T, "dw_ple_t")], "out")

    slab_a3, dcw, dcb, dwa, dba, dwx, dbx, dlam = _rnn_bwd(
        proj3, h3, dh.reshape(BL, S, DR), gates, slab_a.reshape(BL, S, A_W), conv_w, conv_b, wa_b, b_rg_a, wx_b, b_rg_x, lam,
        token)
    datt3 = datt.reshape(BL, S, ATT)
    sbar3 = sbar.reshape(BL, S, ATT)
    *slabs, dqn, dkn = _attn_bwd(proj3, cos_t, sin_t, q_norm, k_norm, lse, wts, qkv_p, datt3, sbar3)
    pieces =[slab_a3.reshape(T, A_W)] + [t.reshape(T, GW) for t in slabs] + [slab_c]
    dw_in_t, db_in = _dw_in(pieces, hn)
    token, pending_in = start_reduce([dw_in_t], "in")
    grad_x, dnm = _grad_x(pieces, w_bufs, chips, token, x, dx1, norm_mix)

    small = dict(w_rg_a=dwa, w_rg_x=dwx, norm_mix=dnm, b_in=db_in, conv_b=dcb, b_rg_a=dba, b_rg_x=dbx,
                 lru_lambda=dlam, q_norm=dqn, k_norm=dkn, norm_ple=dnp, b_ple_gate=dbpg, conv_w=dcw, loss=loss8)
    return grad_x, pending_out, pending_in, small


MESH = pl.DeviceIdType.MESH
HBM_SPEC = pl.BlockSpec(memory_space=pl.ANY)


def _my_pos():
    return lax.axis_index("x"), lax.axis_index("y"), lax.axis_index("c")


def _flip(pos, k):
    x, y, c = pos
    return (1 - x if k & 4 else x, 1 - y if k & 2 else y, 1 - c if k & 1 else c)


def _lin(pos):
    return 4 * pos[0] + 2 * pos[1] + pos[2]


def _chip(pos):
    return 2 * pos[0] + pos[1]


def _exchange_within_chip(parts, name):
    na = len(parts)

    def body(*refs):
        a_refs = refs[:na]
        recv_refs = refs[na:2 * na]
        send_sems, recv_sems = refs[2 * na:]
        me = _my_pos()
        c = me[2]
        sibling = _flip(me, 1)
        remote = []
        for i in range(na):
            for q in range(NCHIP):
                rc = pltpu.make_async_remote_copy(
                    src_ref=a_refs[i].at[q, 1 - c], dst_ref=recv_refs[i].at[q],
                    send_sem=send_sems.at[NCHIP * i + q], recv_sem=recv_sems.at[NCHIP * i + q],
                    device_id=sibling, device_id_type=MESH)
                rc.start()
                remote.append(rc)
        for rc in remote:
            rc.wait_recv()
        for rc in remote:
            rc.wait_send()

    return pl.pallas_call(
        body, name=name, out_shape=[jax.ShapeDtypeStruct((NCHIP,) + a.shape[2:], a.dtype) for a in parts],
        in_specs=[HBM_SPEC] * na, out_specs=[HBM_SPEC] * na,
        scratch_shapes=[pltpu.SemaphoreType.DMA((NCHIP * na,)), pltpu.SemaphoreType.DMA((NCHIP * na,))],
    )(*parts)


HBM_ONLY = pl.BlockSpec(memory_space=pltpu.HBM)
SEM_SPEC = pl.BlockSpec(memory_space=pltpu.SEMAPHORE)
SPLIT_COPY = pltpu.CompilerParams(has_side_effects=pltpu.SideEffectType.DATAFLOW_SIDE_EFFECTING)


def _chip_peers(me):
    return [_flip(me, 4), _flip(me, 2), _flip(me, 6)]


def _between_chips_start(parts, name):
    na = len(parts)

    def body(*refs):
        a_refs = refs[:na]
        land_refs = refs[na:2 * na]
        send_sems, recv_sems = refs[2 * na], refs[2 * na + 1]
        token = refs[-1]
        me = _my_pos()
        myq = _chip(me)
        for i in range(na):
            for j, peer in enumerate(_chip_peers(me)):
                pltpu.make_async_remote_copy(
                    src_ref=a_refs[i].at[_chip(peer)], dst_ref=land_refs[i].at[myq],
                    send_sem=send_sems.at[3 * i + j], recv_sem=recv_sems.at[3 * i + j],
                    device_id=peer, device_id_type=MESH).start()
        token[...] = jnp.zeros_like(token)

    hbm = [pltpu.HBM(a.shape, a.dtype) for a in parts]
    srcs = [pltpu.with_memory_space_constraint(a, pltpu.HBM) for a in parts]
    lands = [pltpu.with_memory_space_constraint(lax.empty(a.shape, a.dtype), pltpu.HBM) for a in parts]
    res = pl.pallas_call(
        body, name=name,
        out_shape=(pltpu.SemaphoreType.DMA((3 * na,)), pltpu.SemaphoreType.DMA((3 * na,)), *hbm, *hbm,
                   jax.ShapeDtypeStruct((8, LANES), F32)),
        in_specs=[HBM_ONLY] * (2 * na),
        out_specs=(SEM_SPEC, SEM_SPEC, *([HBM_ONLY] * (2 * na)), pl.BlockSpec(memory_space=pltpu.VMEM)),
        input_output_aliases={i: 2 + i for i in range(2 * na)},
        compiler_params=SPLIT_COPY,
    )(*srcs, *lands)
    return res[-1], (res[0], res[1], list(res[2:2 + na]), list(res[2 + na:2 + 2 * na]))


def _between_chips_wait(pending, after, name):
    send_sems, recv_sems, parts, lands = pending
    na = len(parts)

    def body(*refs):
        a_refs = refs[:na]
        land_refs = refs[na:2 * na]
        send_sems, recv_sems = refs[2 * na], refs[2 * na + 1]
        me = _my_pos()
        for i in range(na):
            for j, peer in enumerate(_chip_peers(me)):
                cp = pltpu.make_async_remote_copy(
                    src_ref=a_refs[i].at[_chip(peer)], dst_ref=land_refs[i].at[_chip(peer)],
                    send_sem=send_sems.at[3 * i + j], recv_sem=recv_sems.at[3 * i + j],
                    device_id=peer, device_id_type=MESH)
                cp.wait_send()
                cp.wait_recv()

    hbm = [pltpu.HBM(a.shape, a.dtype) for a in parts]
    res = pl.pallas_call(
        body, name=name, out_shape=(*hbm, *hbm),
        in_specs=[HBM_ONLY] * (2 * na) + [SEM_SPEC, SEM_SPEC, pl.BlockSpec(memory_space=pl.ANY)],
        out_specs=[HBM_ONLY] * (2 * na),
        input_output_aliases={i: i for i in range(2 * na)},
        compiler_params=SPLIT_COPY,
    )(*parts, *lands, send_sems, recv_sems, after)
    return list(res[:na]), list(res[na:])


def _remote(src, dst, send_sems, recv_sems, idx, peer):
    return pltpu.make_async_remote_copy(src_ref=src, dst_ref=dst, send_sem=send_sems.at[idx],
                                        recv_sem=recv_sems.at[idx], device_id=peer, device_id_type=MESH)


def _copies_own(bufs, me):
    return [(bufs[0], bufs[1].at[me[2]], 0, _flip(me, 1))]


def _copies_near(bufs, me):
    return [(bufs[0], bufs[1].at[0, me[2]], 0, _flip(me, 2)), (bufs[0], bufs[1].at[1, me[2]], 1, _flip(me, 4))]


def _copies_far(bufs, me):
    return [(bufs[0], bufs[1].at[me[2]], 0, _flip(me, 6))]


def _copies_others(bufs, me):
    na = len(bufs) // 2
    return [(bufs[i], bufs[na + i].at[_lin(me)], 7 * i + k - 1, _flip(me, k))
            for i in range(na) for k in range(1, NDEV)]


def _copies_exchange(bufs, me):
    na = len(bufs) // 2
    return [(bufs[i].at[_lin(_flip(me, k))], bufs[na + i].at[_lin(me)], 7 * i + k - 1, _flip(me, k))
            for i in range(na) for k in range(1, NDEV)]


GROUP_COPIES = dict(own=_copies_own, near=_copies_near, far=_copies_far, others=_copies_others,
                    exchange=_copies_exchange)
GROUP_COUNT = dict(own=1, near=2, far=1)
TO_ALL = ("others", "exchange")


def _gather_start(bufs, groups, after, name):
    nb = len(bufs)
    ng = len(groups)

    def body(*refs):
        b = refs[:nb]
        sems = refs[nb + 1:nb + 1 + 2 * ng]
        token = refs[-1]
        me = _my_pos()
        for gi, (group, idx) in enumerate(groups):
            for src, dst, k, peer in GROUP_COPIES[group]([b[i] for i in idx], me):
                _remote(src, dst, sems[2 * gi], sems[2 * gi + 1], k, peer).start()
        token[...] = jnp.zeros_like(token)

    sem_t = []
    for group, idx in groups:
        cnt = 7 * (len(idx) // 2) if group in TO_ALL else GROUP_COUNT[group]
        sem_t += [pltpu.SemaphoreType.DMA((cnt,)), pltpu.SemaphoreType.DMA((cnt,))]
    ins = [pltpu.with_memory_space_constraint(a, pltpu.HBM) for a in bufs]
    res = pl.pallas_call(
        body, name=name,
        out_shape=(*sem_t, *[pltpu.HBM(a.shape, a.dtype) for a in bufs], jax.ShapeDtypeStruct((8, LANES), F32)),
        in_specs=[HBM_ONLY] * nb + [pl.BlockSpec(memory_space=pl.ANY)],
        out_specs=(*([SEM_SPEC] * (2 * ng)), *([HBM_ONLY] * nb), pl.BlockSpec(memory_space=pltpu.VMEM)),
        input_output_aliases={i: 2 * ng + i for i in range(nb)},
        compiler_params=SPLIT_COPY,
    )(*ins, after)
    return res[-1], list(res[2 * ng:2 * ng + nb]), [(res[2 * gi], res[2 * gi + 1]) for gi in range(ng)]


def _gather_wait(group, send_sems, recv_sems, bufs, after, name):
    nb = len(bufs)
    copies = GROUP_COPIES[group]
    afters = list(after) if isinstance(after, (list, tuple)) else [after]

    def body(*refs):
        b = refs[:nb]
        ss, rs = refs[nb], refs[nb + 1]
        me = _my_pos()
        for src, dst, idx, peer in copies(b, me):
            if group in TO_ALL:
                landed = b[nb // 2 + idx // 7].at[_lin(peer)]
            elif group == "own":
                landed = b[1].at[1 - me[2]]
            else:
                landed = dst
            cp = _remote(src, landed, ss, rs, idx, peer)
            cp.wait_send()
            cp.wait_recv()

    res = pl.pallas_call(
        body, name=name, out_shape=[pltpu.HBM(a.shape, a.dtype) for a in bufs],
        in_specs=[HBM_ONLY] * nb + [SEM_SPEC, SEM_SPEC] + [pl.BlockSpec(memory_space=pl.ANY)] * len(afters),
        out_specs=[HBM_ONLY] * nb,
        input_output_aliases={i: i for i in range(nb)},
        compiler_params=SPLIT_COPY,
    )(*bufs, send_sems, recv_sems, *afters)
    return list(res)


def _forward_to_sibling(buf, name):
    n = buf.shape[0]

    def body(_in_ref, out_ref, send_sems, recv_sems):
        me = _my_pos()
        c = me[2]
        sibling = _flip(me, 1)
        sends = []
        for r in range(n):
            cp = _remote(out_ref.at[r, c], out_ref.at[r, c], send_sems, recv_sems, r, sibling)
            cp.start()
            sends.append(cp)
        for r in range(n):
            _remote(out_ref.at[r, c], out_ref.at[r, 1 - c], send_sems, recv_sems, r, sibling).wait_recv()
        for cp in sends:
            cp.wait_send()

    return pl.pallas_call(
        body, name=name, out_shape=jax.ShapeDtypeStruct(buf.shape, buf.dtype),
        in_specs=[HBM_SPEC], out_specs=HBM_SPEC,
        scratch_shapes=[pltpu.SemaphoreType.DMA((n,)), pltpu.SemaphoreType.DMA((n,))],
        input_output_aliases={0: 0},
    )(buf)


def _scalar(v):
    return jnp.asarray(v, jnp.int32).reshape(1)


def _sum_pairs(parts, theirs, name):
    na = len(parts)

    def body(c_ref, *refs):
        for i in range(na):
            o_ref = refs[2 * na + i]
            o_ref[0] = (refs[i][0, 0].astype(F32) + refs[na + i][0].astype(F32)).astype(o_ref.dtype)

    def mine_spec(a):
        return pl.BlockSpec((1, 1) + a.shape[2:], lambda q, c_ref: (q, c_ref[0], 0, 0))

    def spec(a):
        return pl.BlockSpec((1,) + a.shape[1:], lambda q, c_ref: (q, 0, 0))

    return pl.pallas_call(
        body, name=name,
        grid_spec=pltpu.PrefetchScalarGridSpec(
            num_scalar_prefetch=1, grid=(NCHIP,),
            in_specs=[mine_spec(a) for a in parts] + [spec(a) for a in theirs],
            out_specs=[spec(a) for a in theirs]),
        out_shape=[jax.ShapeDtypeStruct(a.shape, a.dtype) for a in theirs],
        compiler_params=_cp(("arbitrary",), VMEM_BIG),
    )(_scalar(lax.axis_index("c")), *_hbm(*parts, *theirs))


def _others(q, mine, nblk=NCHIP):
    return jnp.where(q == mine, (q + 1) % nblk, q)


def _sum_chips_adamw(own, recv, wv, mv, vv, token, tr, name):
    _, r, w = recv.shape

    def body(q_ref, own_ref, r0, r1, r2, r3, w_ref, m_ref, v_ref, _token, g_ref, d_ref, m2_ref, v2_ref):
        myq = q_ref[0]
        acc = None
        for q, r_ref in enumerate((r0, r1, r2, r3)):
            term = jnp.where(myq == q, own_ref[0], r_ref[0]).astype(F32)
            acc = term if acc is None else acc + term
        g_ref[...] = acc
        delta, m2, v2 = _adam_math(w_ref[...], acc, m_ref[...], v_ref[...])
        d_ref[...] = delta
        m2_ref[...] = m2
        v2_ref[...] = v2

    def recv_spec(q):
        return pl.BlockSpec((1, tr, w), lambda i, q_ref: (_others(q, q_ref[0]), i, 0))

    rows = lambda: pl.BlockSpec((tr, w), lambda i, q_ref: (i, 0))
    shp = jax.ShapeDtypeStruct((r, w), F32)
    return pl.pallas_call(
        body, name=name,
        grid_spec=pltpu.PrefetchScalarGridSpec(
            num_scalar_prefetch=1, grid=(r // tr,),
            in_specs=[pl.BlockSpec((1, tr, w), lambda i, q_ref: (q_ref[0], i, 0))]
            + [recv_spec(q) for q in range(NCHIP)] + [rows(), rows(), rows()]
            + [pl.BlockSpec((8, LANES), lambda i, q_ref: (0, 0))],
            out_specs=[rows(), rows(), rows(), rows()]),
        out_shape=[shp, shp, shp, shp],
        compiler_params=_cp(("arbitrary",), VMEM_MID),
    )(_scalar(_chip(_my_pos())), *_hbm(own, recv, recv, recv, recv, wv, mv, vv, token))


def _sum_blocks_small(own, recv, mine, transpose, name):
    na = len(recv)
    nblk = recv[0].shape[0]

    def body(q_ref, *refs):
        me = q_ref[0]
        for i in range(na):
            acc = None
            for q in range(nblk):
                term = jnp.where(me == q, refs[i][0], refs[na * (1 + q) + i][0]).astype(F32)
                acc = term if acc is None else acc + term
            refs[na * (1 + nblk) + i][...] = acc.T if transpose[i] else acc

    def oshape(a, tr):
        r, w = a.shape[1:]
        return (w, r) if tr else (r, w)

    own_spec = lambda a: pl.BlockSpec((1,) + a.shape[1:], lambda s, q_ref: (q_ref[0], 0, 0))
    recv_spec = lambda a, q: pl.BlockSpec((1,) + a.shape[1:], lambda s, q_ref: (_others(q, q_ref[0], nblk), 0, 0))
    out_spec = lambda shp: pl.BlockSpec(shp, lambda s, q_ref: (0, 0))
    in_specs = [own_spec(a) for a in own]
    for q in range(nblk):
        in_specs += [recv_spec(a, q) for a in recv]
    return pl.pallas_call(
        body, name=name,
        grid_spec=pltpu.PrefetchScalarGridSpec(
            num_scalar_prefetch=1, grid=(1,), in_specs=in_specs,
            out_specs=[out_spec(oshape(a, tr)) for a, tr in zip(recv, transpose)]),
        out_shape=[jax.ShapeDtypeStruct(oshape(a, tr), F32) for a, tr in zip(recv, transpose)],
        compiler_params=_cp(("arbitrary",), VMEM_MID),
    )(_scalar(mine), *_hbm(*own, *(list(recv) * nblk)))


def _rep_offsets():
    offs = []
    o = 0
    for r in REP_ROWS:
        offs.append(o)
        o += r
    return offs


LOSS_ROW = REP_TOTAL_ROWS


def _pack_small_grads(g):
    offs = _rep_offsets()

    def body(dwa, dwx, dnm, dbin, dcb, dba, dbx, dlam, dqn, dkn, dnp, dbpg, loss, o_ref):
        o_ref[pl.ds(REP_TOTAL_ROWS - 2, NDEV * REP_ROWS_DEV - REP_TOTAL_ROWS + 2), :] = jnp.zeros(
            (NDEV * REP_ROWS_DEV - REP_TOTAL_ROWS + 2, LANES), F32)
        o_ref[pl.ds(LOSS_ROW, 1), :] = loss[0:1, :]
        for n in range(NRB):
            o_ref[pl.ds(offs[0] + n * RBW, RBW), :] = dwa[n]
            o_ref[pl.ds(offs[1] + n * RBW, RBW), :] = dwx[n]

        def put_vec(off, ref, rows):
            for k in range(rows):
                o_ref[pl.ds(off + k, 1), :] = ref[:, k * LANES:(k + 1) * LANES]

        put_vec(offs[2], dnm, REP_ROWS[2])
        put_vec(offs[3], dbin, REP_ROWS[3])
        put_vec(offs[4], dcb, REP_ROWS[4])
        put_vec(offs[5], dba, REP_ROWS[5])
        put_vec(offs[6], dbx, REP_ROWS[6])
        put_vec(offs[7], dlam, REP_ROWS[7])
        for k in range(NG):
            o_ref[pl.ds(offs[8] + k, 1), :] = dqn[k, 0:1, :]
            o_ref[pl.ds(offs[9] + k, 1), :] = dkn[k, 0:1, :]
        put_vec(offs[10], dnp, REP_ROWS[10])
        put_vec(offs[11], dbpg, REP_ROWS[11])

    args = [g["w_rg_a"], g["w_rg_x"], g["norm_mix"], g["b_in"], g["conv_b"], g["b_rg_a"], g["b_rg_x"],
            g["lru_lambda"], g["q_norm"], g["k_norm"], g["norm_ple"], g["b_ple_gate"], g["loss"]]
    full = lambda shp: pl.BlockSpec(shp, lambda: (0,) * len(shp))
    return pl.pallas_call(
        body, name="pack_small_grads",
        in_specs=[full(a.shape) for a in args],
        out_specs=full((NDEV * REP_ROWS_DEV, LANES)),
        out_shape=jax.ShapeDtypeStruct((NDEV * REP_ROWS_DEV, LANES), F32),
    )(*_hbm(*args))


def _adam_math(wv, gv, mv, vv):
    c1 = 1.0 - B1 ** STEP
    c2 = 1.0 - B2 ** STEP
    m2 = B1 * mv + (1.0 - B1) * gv
    v2 = B2 * vv + (1.0 - B2) * (gv * gv)
    delta = (-LR) * ((m2 / c1) / (jnp.sqrt(v2 / c2) + AEPS) + WD * wv)
    return delta, m2, v2


def _adamw_small(rep_flat, w, m, v):
    offs = _rep_offsets()
    n = len(REP_NAMES)

    def body(*refs):
        g_ref = refs[0]
        w_refs = refs[1:1 + n]
        m_refs = refs[1 + n:1 + 2 * n]
        v_refs = refs[1 + 2 * n:1 + 3 * n]
        outs = refs[1 + 3 * n:]
        go, do, mo, vo = outs[:n], outs[n:2 * n], outs[2 * n:3 * n], outs[3 * n:]

        def emit(i, idx, gv):
            go[i][idx] = gv
            delta, m2, v2 = _adam_math(w_refs[i][idx], gv, m_refs[i][idx], v_refs[i][idx])
            do[i][idx] = delta
            mo[i][idx] = m2
            vo[i][idx] = v2

        for i in range(n):
            if i < 2:
                for b in range(NRB):
                    emit(i, b, g_ref[pl.ds(offs[i] + b * RBW, RBW), :])
            elif REP_NAMES[i] in ("q_norm", "k_norm"):
                emit(i, slice(None), g_ref[pl.ds(offs[i], NG), :])
            else:
                gv = jnp.concatenate([g_ref[pl.ds(offs[i] + k, 1), :] for k in range(REP_ROWS[i])], axis=1)
                emit(i, slice(None), gv)

    full = lambda shp: pl.BlockSpec(shp, lambda: (0,) * len(shp))
    pspecs = [full(a.shape) for a in w]
    pshapes = [jax.ShapeDtypeStruct(a.shape, F32) for a in w]
    res = pl.pallas_call(
        body, name="adamw_small",
        in_specs=[full(rep_flat.shape)] + pspecs * 3,
        out_specs=pspecs * 4, out_shape=pshapes * 4,
        compiler_params=_cp(None, VMEM_MID),
    )(*_hbm(rep_flat, *w, *m, *v))
    return res[:n], res[n:2 * n], res[2 * n:3 * n], res[3 * n:]


def _adamw_many(w, g, m, v, token):
    n = len(w)

    def body(*refs):
        for i in range(n):
            delta, m2, v2 = _adam_math(refs[i][...], refs[n + i][...], refs[2 * n + i][...], refs[3 * n + i][...])
            refs[4 * n + 1 + i][...] = delta
            refs[5 * n + 1 + i][...] = m2
            refs[6 * n + 1 + i][...] = v2

    full = lambda shp: pl.BlockSpec(shp, lambda: (0,) * len(shp))
    specs = [full(a.shape) for a in w]
    shapes = [jax.ShapeDtypeStruct(a.shape, F32) for a in w]
    res = pl.pallas_call(
        body, name="adamw_shards",
        in_specs=specs * 4 + [full((8, LANES))], out_specs=specs * 3, out_shape=shapes * 3,
        compiler_params=_cp(None, VMEM_MID),
    )(*_hbm(*w, *g, *m, *v, token))
    return res[:n], res[n:2 * n], res[2 * n:]


def kernel(x, p, norm_mix, w_in, b_in, conv_w, conv_b, w_rg_a, b_rg_a, w_rg_x, b_rg_x, lru_lambda, q_norm, k_norm, w_o_rnn, w_o_att, w_out, norm_ple, w_ple_gate, b_ple_gate, w_ple, loss_target, m_norm_mix, m_w_in, m_b_in, m_conv_w, m_conv_b, m_w_rg_a, m_b_rg_a, m_w_rg_x, m_b_rg_x, m_lru_lambda, m_q_norm, m_k_norm, m_w_o_rnn, m_w_o_att, m_w_out, m_norm_ple, m_w_ple_gate, m_b_ple_gate, m_w_ple, v_norm_mix, v_w_in, v_b_in, v_conv_w, v_conv_b, v_w_rg_a, v_b_rg_a, v_w_rg_x, v_b_rg_x, v_lru_lambda, v_q_norm, v_k_norm, v_w_o_rnn, v_w_o_att, v_w_out, v_norm_ple, v_w_ple_gate, v_b_ple_gate, v_w_ple):
    w = dict(norm_mix=norm_mix, w_in=w_in, b_in=b_in, conv_w=conv_w, conv_b=conv_b, w_rg_a=w_rg_a, b_rg_a=b_rg_a,
             w_rg_x=w_rg_x, b_rg_x=b_rg_x, lru_lambda=lru_lambda, q_norm=q_norm, k_norm=k_norm, w_o_rnn=w_o_rnn,
             w_o_att=w_o_att, w_out=w_out, norm_ple=norm_ple, w_ple_gate=w_ple_gate, b_ple_gate=b_ple_gate,
             w_ple=w_ple)
    m = dict(norm_mix=m_norm_mix, w_in=m_w_in, b_in=m_b_in, conv_w=m_conv_w, conv_b=m_conv_b, w_rg_a=m_w_rg_a,
             b_rg_a=m_b_rg_a, w_rg_x=m_w_rg_x, b_rg_x=m_b_rg_x, lru_lambda=m_lru_lambda, q_norm=m_q_norm,
             k_norm=m_k_norm, w_o_rnn=m_w_o_rnn, w_o_att=m_w_o_att, w_out=m_w_out, norm_ple=m_norm_ple,
             w_ple_gate=m_w_ple_gate, b_ple_gate=m_b_ple_gate, w_ple=m_w_ple)
    v = dict(norm_mix=v_norm_mix, w_in=v_w_in, b_in=v_b_in, conv_w=v_conv_w, conv_b=v_conv_b, w_rg_a=v_w_rg_a,
             b_rg_a=v_b_rg_a, w_rg_x=v_w_rg_x, b_rg_x=v_b_rg_x, lru_lambda=v_lru_lambda, q_norm=v_q_norm,
             k_norm=v_k_norm, w_o_rnn=v_w_o_rnn, w_o_att=v_w_o_att, w_out=v_w_out, norm_ple=v_norm_ple,
             w_ple_gate=v_w_ple_gate, b_ple_gate=v_b_ple_gate, w_ple=v_w_ple)
    names = list(w.keys())

    w_shard, conv_shard = w_in[0].T.astype(BF16), conv_w[0]
    pos = _my_pos()
    me, my_core, my_chip = _lin(pos), pos[2], _chip(pos)
    hbm_empty = lambda shp, dt: lax.empty(shp, dt)
    shp = w_shard.shape
    entry_token, bufs, sems = _gather_start(
        [w_shard, hbm_empty((2,) + shp, BF16), hbm_empty((2, 2) + shp, BF16), conv_shard,
         hbm_empty((NDEV,) + conv_shard.shape, F32)],
        [("own", (0, 1)), ("near", (0, 2)), ("others", (3, 4))], norm_mix, "gather_start_near")
    w_src, own_l, near_l, conv_src, conv_l = bufs
    sem_own, sem_near, sem_conv = sems
    gather_out = {}

    def project(hn):
        w_thru, own = _gather_wait("own", *sem_own, [w_src, own_l], hn, "gather_wait_own")
        own = lax.dynamic_update_slice(own, w_shard[None], (my_core, 0, 0)).reshape(1, CHIP_COLS, D)
        chips = [jnp.stack([my_chip]), jnp.stack([my_chip ^ 1, my_chip ^ 2]), jnp.stack([my_chip ^ 3])]
        chips = [c.astype(jnp.int32) for c in chips]
        proj = _in_proj_chips(hn, own, b_in, chips[0], None, entry_token, "in_proj_own")
        proj, raw = lax.optimization_barrier((proj, (w_o_rnn[0], w_o_att[0], w_out[0], w_ple_gate[0], w_ple[0])))
        srcs = [raw[0].astype(BF16), raw[1].T.astype(BF16), raw[2].astype(BF16), raw[3].astype(BF16),
                raw[4].T.astype(BF16)]
        w_thru, near = _gather_wait("near", *sem_near, [w_thru, near_l], [proj] + srcs, "gather_wait_near")
        near = _forward_to_sibling(near, "gather_forward_near")
        token, (w_thru, far_l), (sem_far,) = _gather_start(
            [w_thru, hbm_empty((2,) + shp, BF16)], [("far", (0, 1))], near, "gather_start_far")
        near = near.reshape(2, CHIP_COLS, D)
        proj = _in_proj_chips(hn, near, b_in, chips[1], proj, token, "in_proj_near")
        w_thru, far = _gather_wait("far", *sem_far, [w_thru, far_l], proj, "gather_wait_far")
        far = _forward_to_sibling(far[None], "gather_forward_far").reshape(1, CHIP_COLS, D)
        proj = _in_proj_chips(hn, far, b_in, chips[2], proj, token, "in_proj_far")
        conv_thru, conv_g = _gather_wait("others", *sem_conv, [conv_src, conv_l], proj, "gather_wait_conv")
        conv_g = lax.dynamic_update_slice(conv_g, conv_shard[None], (me, 0, 0))
        conv_f = conv_g.transpose(1, 0, 2).reshape(CONVW, DR)
        token, obufs, (sem_out,) = _gather_start(
            srcs + [hbm_empty((NDEV,) + a.shape, BF16) for a in srcs], [("others", tuple(range(10)))], proj,
            "gather_start_out")
        gather_out.update(bufs=obufs, sems=sem_out, shards=srcs)
        return proj, [own, near, far], jnp.concatenate(chips), conv_f, token

    def other_weights(after):
        obufs = _gather_wait("others", *gather_out["sems"], gather_out["bufs"], after, "gather_wait_out")
        full = [lax.dynamic_update_slice(a, s[None], (me, 0, 0)) for a, s in zip(obufs[5:], gather_out["shards"])]
        return [a.reshape((NDEV * a.shape[1], a.shape[2])) for a in full]

    def start_reduce(arrs, tag):
        if tag == "out":
            parts = [a.reshape((NDEV, a.shape[0] // NDEV, a.shape[1])) for a in arrs]
            token, bufs, (sems,) = _gather_start(
                parts + [lax.empty(a.shape, a.dtype) for a in parts], [("exchange", tuple(range(2 * len(parts))))],
                arrs[-1][:SUBLANES], "reduce_out_start")
            return token, (bufs, sems)
        parts = [a.reshape((NCHIP, 2, a.shape[0] // NDEV, a.shape[1])) for a in arrs]
        theirs = _exchange_within_chip(parts, "reduce_within_chip_" + tag)
        return _between_chips_start(_sum_pairs(parts, theirs, "sum_pairs_" + tag), "reduce_between_chips_start_" + tag)

    grad_x, pending_out, pending_in, small = _local_step(
        x.reshape(T, D), p.reshape(T, PLE), loss_target.reshape(T, D),
        project, other_weights,
        norm_mix, conv_b, w_rg_a[0], b_rg_a, w_rg_x[0], b_rg_x, lru_lambda, q_norm[0], k_norm[0],
        norm_ple, b_ple_gate, start_reduce, entry_token)

    rep_parts = _pack_small_grads(small).reshape(NDEV, REP_ROWS_DEV, LANES)
    conv_parts = small["conv_w"].reshape(CONVW, NDEV, DR // NDEV).transpose(1, 0, 2)
    smalls = [rep_parts, conv_parts]
    token, sbufs, (sem_x,) = _gather_start(
        smalls + [lax.empty(a.shape, F32) for a in smalls], [("exchange", (0, 1, 2, 3))], small["norm_mix"],
        "reduce_small_start")

    own_in, recv_in = _between_chips_wait(pending_in, token, "reduce_between_chips_wait_in")
    w_in_res = _sum_chips_adamw(own_in[0], recv_in[0], w_in[0].T, m_w_in[0].T, v_w_in[0].T, token, 304, "adamw_w_in")
    sbufs = _gather_wait("exchange", *sem_x, sbufs, w_in_res[0], "reduce_small_wait")
    g_rep, g_conv = _sum_blocks_small(sbufs[:2], sbufs[2:], me, (False, False), "sum_small")
    token, gbufs, (sem_g,) = _gather_start(
        [g_rep, lax.empty((NDEV,) + g_rep.shape, F32)], [("others", (0, 1))], g_conv, "gather_small_start")
    obufs = _gather_wait("exchange", *pending_out[1], pending_out[0], token, "reduce_out_wait")
    g_o_rnn, g_o_att, g_out, g_pg, g_ple = _sum_blocks_small(
        obufs[:5], obufs[5:], me, (False, True, False, False, True), "sum_out")

    grad, delta, new_m, new_v = {}, {}, {}, {}
    rest = ("w_o_rnn", "w_o_att", "w_out", "w_ple_gate", "w_ple", "conv_w")
    g_rest = [g_o_rnn, g_o_att, g_out, g_pg, g_ple, g_conv]
    rest_res = _adamw_many([w[n][0] for n in rest], g_rest, [m[n][0] for n in rest], [v[n][0] for n in rest], token)
    _, rep_all = _gather_wait("others", *sem_g, gbufs, rest_res[0][0], "gather_small_wait")
    rep_all = lax.dynamic_update_slice(rep_all, g_rep[None], (me, 0, 0)).reshape(NDEV * REP_ROWS_DEV, LANES)
    loss = rep_all[LOSS_ROW, 0]
    rep_shape = lambda a: a if a.ndim == 2 else a.reshape(a.shape[1:])
    res = _adamw_small(rep_all, [rep_shape(w[n]) for n in REP_NAMES], [rep_shape(m[n]) for n in REP_NAMES],
                       [rep_shape(v[n]) for n in REP_NAMES])
    for dst, vals in zip((grad, delta, new_m, new_v), res):
        for n, a in zip(REP_NAMES, vals):
            dst[n] = a.reshape(w[n].shape)
    grad["w_in"], delta["w_in"], new_m["w_in"], new_v["w_in"] = [a.T[None] for a in w_in_res]
    for n, a in zip(rest, g_rest):
        grad[n] = a[None]
    for dst, vals in zip((delta, new_m, new_v), rest_res):
        for n, a in zip(rest, vals):
            dst[n] = a[None]

    return (loss, grad_x.reshape(BL, S, D), *[grad[n] for n in names], *[delta[n] for n in names],
            *[new_m[n] for n in names], *[new_v[n] for n in names])
```

```python
import jax
import jax.numpy as jnp
import numpy as np
from jax import lax
from jax.experimental import pallas as pl
from jax.experimental.pallas import tpu as pltpu

F32 = jnp.float32
BF16 = jnp.bfloat16

D = 1024
S = 2048
BL = 2
T = BL * S
NDEV = 8
NCHIP = 4
PLE = 256
DR = 1280
NRB = 10
RBW = 128
CONVW = 4
LRU_C = 8.0
HD = 128
NH = 4
PATTERNS = ((128, 1), (512, 4), (2048, 16))
NG = 3
ATT = NH * HD
GW = NG * ATT
NIN = 2 * DR + 3 * GW + ATT + 2 * D
OFF_ZR = DR
OFF_Q = 2 * DR
OFF_ZA = OFF_Q + 3 * GW
OFF_G = OFF_ZA + ATT
ROPE_THETA = 10000.0
EPS = 1e-6
SCALE = HD ** -0.5
NEG = -1e30
QB = 128
LANES = 128
CT = 512
NCT = NIN // CT
A_W = 2 * DR
C_W = ATT + 2 * D

LR, B1, B2, AEPS, WD, STEP = 0.001, 0.9, 0.999, 1e-08, 0.01, 10

NSHARD_IN = NIN // NDEV
REP_NAMES = ("w_rg_a", "w_rg_x", "norm_mix", "b_in", "conv_b", "b_rg_a", "b_rg_x", "lru_lambda", "q_norm",
             "k_norm", "norm_ple", "b_ple_gate")
REP_ROWS = (NRB * RBW, NRB * RBW, D // LANES, NIN // LANES, DR // LANES, DR // LANES, DR // LANES, DR // LANES,
            NG, NG, D // LANES, D // LANES)
REP_TOTAL_ROWS = sum(REP_ROWS)
REP_ROWS_DEV = 344
BIG_NAMES = ("w_in", "w_o_rnn", "w_o_att", "w_out", "w_ple_gate", "w_ple")

VMEM_BIG = 56 * 1024 * 1024
VMEM_MID = 40 * 1024 * 1024


def _cp(sem=None, vmem=None):
    return pltpu.CompilerParams(dimension_semantics=sem, vmem_limit_bytes=vmem)


def _hbm(*arrays):
    return [pltpu.with_memory_space_constraint(a, pltpu.HBM) for a in arrays]


def _copy_together(copies):
    for cp in copies:
        cp.start()
    for cp in copies:
        cp.wait()


def _dot(a, b):
    return jnp.dot(a, b, preferred_element_type=F32)


def _dot_nt(a, b):
    return lax.dot_general(a, b, (((1,), (1,)), ((), ())), preferred_element_type=F32)


def _dot_tn(a, b):
    return lax.dot_general(a, b, (((0,), (0,)), ((), ())), preferred_element_type=F32)


def _sigmoid(x):
    return jax.nn.sigmoid(x)


def _perm(j):
    jq = j - OFF_Q // CT
    inside = (j >= OFF_Q // CT) & (j < OFF_ZA // CT)
    return jnp.where(inside, OFF_Q // CT + (jq % 3) * 3 + jq // 3, j)


PIECES = ((0, A_W // CT), (OFF_Q // CT, GW // CT), (OFF_Q // CT + 3, GW // CT), (OFF_Q // CT + 6, GW // CT),
          (OFF_ZA // CT, C_W // CT))


def _rmsnorm_fwd(x, gain, token, tm=512):
    def body(x_ref, g_ref, _token, o_ref):
        xv = x_ref[...]
        var = jnp.mean(xv * xv, axis=-1, keepdims=True)
        o_ref[...] = (xv * lax.rsqrt(var + EPS) * g_ref[...]).astype(BF16)

    return pl.pallas_call(
        body, grid=(T // tm,), name="rmsnorm_fwd",
        in_specs=[pl.BlockSpec((tm, D), lambda i: (i, 0)), pl.BlockSpec((1, D), lambda i: (0, 0)),
                  pl.BlockSpec((8, LANES), lambda i: (0, 0))],
        out_specs=pl.BlockSpec((tm, D), lambda i: (i, 0)),
        out_shape=jax.ShapeDtypeStruct((T, D), BF16),
        compiler_params=_cp(("parallel",)),
    )(*_hbm(x, gain, token))


CHIP_COLS = NIN // NCHIP


def _in_proj_chips(hn, w_rows, bias, chips, proj, token, name, tm=1024):
    n = w_rows.shape[0]

    def body(chips_ref, a_ref, w_ref, b_ref, _token, *rest):
        o_ref = rest[-1]
        o_ref[...] = (_dot_nt(a_ref[...], w_ref[0]) + b_ref[...]).astype(BF16)

    in_specs = [pl.BlockSpec((tm, D), lambda s, i, ch: (i, 0)),
                pl.BlockSpec((1, CHIP_COLS, D), lambda s, i, ch: (s, 0, 0)),
                pl.BlockSpec((1, CHIP_COLS), lambda s, i, ch: (0, ch[s])),
                pl.BlockSpec((8, LANES), lambda s, i, ch: (0, 0))]
    args = [hn, w_rows, bias, token]
    aliases = {}
    if proj is not None:
        in_specs.append(pl.BlockSpec(memory_space=pl.ANY))
        args.append(proj)
        aliases = {5: 0}
    return pl.pallas_call(
        body, name=name,
        grid_spec=pltpu.PrefetchScalarGridSpec(
            num_scalar_prefetch=1, grid=(n, T // tm), in_specs=in_specs,
            out_specs=pl.BlockSpec((tm, CHIP_COLS), lambda s, i, ch: (i, ch[s]))),
        out_shape=jax.ShapeDtypeStruct((T, NIN), BF16),
        input_output_aliases=aliases,
        compiler_params=_cp(("arbitrary", "arbitrary"), VMEM_BIG),
    )(chips, *_hbm(*args))


def _grad_x(pieces, w_bufs, chips, token, x, dx1, gain, tm=512):
    nb = len(w_bufs)

    def body(chips_ref, a_ref, q_ref, k_ref, v_ref, c_ref, *rest):
        w_hbm = rest[:nb]
        x_ref, dx1_ref, g_ref, dx_ref, dg_ref, w, sems = rest[nb + 1:]
        first = pl.program_id(0) == 0

        @pl.when(first)
        def _():
            s = 0
            copies = []
            for buf in w_hbm:
                for r in range(buf.shape[0]):
                    row = pl.multiple_of(chips_ref[s] * CHIP_COLS, 128)
                    copies.append(pltpu.make_async_copy(buf.at[r], w.at[pl.ds(row, CHIP_COLS), :], sems.at[s]))
                    s += 1
            _copy_together(copies)

        acc = _dot(a_ref[...], w[pl.ds(0, A_W), :])
        for kind, p_ref in enumerate((q_ref, k_ref, v_ref)):
            for g in range(NG):
                row = OFF_Q + (3 * g + kind) * CT
                acc = acc + _dot(p_ref[:, g * CT:(g + 1) * CT], w[pl.ds(row, CT), :])
        dn = acc + _dot(c_ref[...], w[pl.ds(OFF_ZA, C_W), :])
        xv = x_ref[...]
        rstd = lax.rsqrt(jnp.mean(xv * xv, axis=-1, keepdims=True) + EPS)
        xh = xv * rstd
        dg = jnp.sum(dn * xh, axis=0, keepdims=True)
        gd = dn * g_ref[...]
        dx_ref[...] = dx1_ref[...] + rstd * (gd - xh * jnp.mean(gd * xh, axis=-1, keepdims=True))

        @pl.when(first)
        def _():
            dg_ref[...] = dg

        @pl.when(jnp.logical_not(first))
        def _():
            dg_ref[...] += dg

    tok = lambda wd: pl.BlockSpec((tm, wd), lambda i, ch: (i, 0))
    vec = lambda: pl.BlockSpec((1, D), lambda i, ch: (0, 0))
    return pl.pallas_call(
        body, name="grad_x",
        grid_spec=pltpu.PrefetchScalarGridSpec(
            num_scalar_prefetch=1, grid=(T // tm,),
            in_specs=[tok(A_W), tok(GW), tok(GW), tok(GW), tok(C_W)] + [pl.BlockSpec(memory_space=pl.ANY)] * nb
            + [pl.BlockSpec((8, LANES), lambda i, ch: (0, 0)), tok(D), tok(D), vec()],
            out_specs=[tok(D), vec()],
            scratch_shapes=[pltpu.VMEM((NIN, D), BF16), pltpu.SemaphoreType.DMA((NCHIP,))]),
        out_shape=[jax.ShapeDtypeStruct((T, D), F32), jax.ShapeDtypeStruct((1, D), F32)],
        compiler_params=_cp(("arbitrary",), VMEM_BIG),
    )(chips, *_hbm(*pieces, *w_bufs, token, x, dx1, gain))


def _dw_in(pieces, hn):
    def body(a_ref, q_ref, k_ref, v_ref, c_ref, h_hbm, o_ref, s_ref, h):
        j = pl.program_id(0)

        @pl.when(j == 0)
        def _():
            pltpu.sync_copy(h_hbm, h)

        def step(x_ref):
            xv = x_ref[...]
            o_ref[...] = _dot_tn(xv, h[...]).astype(BF16)
            s_ref[...] = jnp.sum(xv.astype(F32), axis=0, keepdims=True)

        for x_ref, (lo, n) in zip((a_ref, q_ref, k_ref, v_ref, c_ref), PIECES):
            pl.when((j >= lo) & (j < lo + n))(lambda x_ref=x_ref: step(x_ref))

    def piece_spec(lo, n):
        return pl.BlockSpec((T, CT), lambda j: (0, jnp.clip(j - lo, 0, n - 1)))

    return pl.pallas_call(
        body, grid=(NCT,), name="dw_in",
        in_specs=[piece_spec(lo, n) for lo, n in PIECES] + [pl.BlockSpec(memory_space=pl.ANY)],
        out_specs=[pl.BlockSpec((CT, D), lambda j: (_perm(j), 0)), pl.BlockSpec((1, CT), lambda j: (0, _perm(j)))],
        out_shape=[jax.ShapeDtypeStruct((NIN, D), BF16), jax.ShapeDtypeStruct((1, NIN), F32)],
        scratch_shapes=[pltpu.VMEM((T, D), BF16)],
        compiler_params=_cp(("arbitrary",), VMEM_BIG),
    )(*_hbm(*pieces, hn))


def _mm_tn(a, b, ta, tt, name):
    m = a.shape[1]
    n = b.shape[1]
    nt = T // tt

    def body(a_ref, b_ref, o_ref, acc):
        t = pl.program_id(1)
        p = _dot_tn(a_ref[...].astype(BF16), b_ref[...].astype(BF16))
        if nt == 1:
            o_ref[...] = p.astype(BF16)
            return

        @pl.when(t == 0)
        def _():
            acc[...] = p

        @pl.when(t > 0)
        def _():
            acc[...] += p

        @pl.when(t == nt - 1)
        def _():
            o_ref[...] = acc[...].astype(BF16)

    return pl.pallas_call(
        body, grid=(m // ta, nt), name=name,
        in_specs=[pl.BlockSpec((tt, ta), lambda j, t: (t, j)), pl.BlockSpec((tt, n), lambda j, t: (t, 0))],
        out_specs=pl.BlockSpec((ta, n), lambda j, t: (j, 0)),
        out_shape=pltpu.HBM((m, n), BF16),
        scratch_shapes=[pltpu.VMEM((ta, n), F32)],
        compiler_params=_cp(("parallel", "arbitrary"), VMEM_MID),
    )(*_hbm(a, b))


def _row_iota():
    return lax.broadcasted_iota(jnp.int32, (S, RBW), 0)


SUBLANES = 8
N_SHIFT_BUFS = 4


class _Shifter:
    def __init__(self, bufs):
        self.bufs = bufs
        self.k = 0

    def _store(self, v, fill, front):
        b = self.bufs.at[self.k % N_SHIFT_BUFS]
        self.k += 1
        b[pl.ds(0 if front else SUBLANES + S, SUBLANES), :] = jnp.full((SUBLANES, RBW), fill, F32)
        b[pl.ds(SUBLANES, S), :] = v
        return b

    def down(self, v, ds, fill):
        b = self._store(v, fill, True)
        return [b[pl.ds(SUBLANES - d, S), :] for d in ds]

    def up(self, v, ds, fill):
        b = self._store(v, fill, False)
        return [b[pl.ds(SUBLANES + d, S), :] for d in ds]


def _shift_down(v, d, sh, fill):
    return sh.down(v, (d,), fill)[0]


def _shift_up(v, d, sh, fill):
    return sh.up(v, (d,), fill)[0]


def _scan_down(a, u, row):
    d = 1
    while d < S:
        last = 2 * d >= S
        if d < SUBLANES:
            u = a * _shift_down(u, d, row, 0.0) + u
            if not last:
                a = a * _shift_down(a, d, row, 1.0)
        else:
            u = jnp.concatenate([u[:d], a[d:] * u[:S - d] + u[d:]], axis=0)
            if not last:
                a = jnp.concatenate([a[:d], a[d:] * a[:S - d]], axis=0)
        d *= 2
    return u


def _scan_up(b, g, row):
    d = 1
    while d < S:
        last = 2 * d >= S
        if d < SUBLANES:
            g = g + b * _shift_up(g, d, row, 0.0)
            if not last:
                b = b * _shift_up(b, d, row, 0.0)
        else:
            g = jnp.concatenate([g[:S - d] + b[:S - d] * g[d:], g[S - d:]], axis=0)
            if not last:
                b = jnp.concatenate([b[:S - d] * b[d:], b[S - d:]], axis=0)
        d *= 2
    return g


def _softplus(x):
    return jnp.maximum(x, 0.0) + jnp.log1p(jnp.exp(-jnp.abs(x)))


N_SAVED = 5


def _rnn_gates(x, cw, cb, wa, ba, wx, bx, lam, row, pad, saved=None):
    xs = pad.down(x, (1, 2, 3), 0.0)
    if saved is not None:
        r, i, a, mult, xc = saved
        return xc, xc.astype(BF16), r, i, _softplus(-lam), a, mult, xs
    xc = cb + cw[3:4, :] * x
    for j in (1, 2, 3):
        xc = xc + cw[3 - j:4 - j, :] * xs[j - 1]
    xcb = xc.astype(BF16)
    r = _sigmoid(_dot(xcb, wa) + ba)
    i = _sigmoid(_dot(xcb, wx) + bx)
    sp = _softplus(-lam)
    log_a = (-LRU_C) * r * sp
    a = jnp.exp(log_a)
    mult = jnp.where(row == 0, 1.0, jnp.sqrt(jnp.tanh(-log_a) * (1.0 + a * a)))
    return xc, xcb, r, i, sp, a, mult, xs


def _rnn_fwd(proj3, conv_w, conv_b, wa, ba, wx, bx, lam, token):
    def body(x_ref, cw_ref, cb_ref, wa_ref, ba_ref, wx_ref, bx_ref, lam_ref, _token, h_ref, g_ref, pad):
        row = _row_iota()
        sh = _Shifter(pad)
        x = x_ref[0].astype(F32)
        xc, _, r, i, _, a, mult, _ = _rnn_gates(x, cw_ref[...], cb_ref[...], wa_ref[0], ba_ref[...],
                                             wx_ref[0], bx_ref[...], lam_ref[...], row, sh)
        for k, val in enumerate((r, i, a, mult, xc)):
            g_ref[k, 0] = val
        h_ref[0] = _scan_down(a, mult * (i * xc), sh)

    vec = lambda: pl.BlockSpec((1, RBW), lambda b, n: (0, n))
    mat = lambda: pl.BlockSpec((1, RBW, RBW), lambda b, n: (n, 0, 0))
    return pl.pallas_call(
        body, grid=(BL, NRB), name="rnn_fwd",
        in_specs=[pl.BlockSpec((1, S, RBW), lambda b, n: (b, 0, n)),
                  pl.BlockSpec((CONVW, RBW), lambda b, n: (0, n)),
                  vec(), mat(), vec(), mat(), vec(), vec(), pl.BlockSpec((8, LANES), lambda b, n: (0, 0))],
        out_specs=[pl.BlockSpec((1, S, RBW), lambda b, n: (b, 0, n)),
                   pl.BlockSpec((N_SAVED, 1, S, RBW), lambda b, n: (0, b, 0, n))],
        out_shape=[jax.ShapeDtypeStruct((BL, S, DR), F32), jax.ShapeDtypeStruct((N_SAVED, BL, S, DR), F32)],
        scratch_shapes=[pltpu.VMEM((N_SHIFT_BUFS, S + 2 * SUBLANES, RBW), F32)],
        compiler_params=_cp(("parallel", "parallel"), VMEM_MID),
    )(*_hbm(proj3, conv_w, conv_b, wa, ba, wx, bx, lam, token))


def _rnn_bwd(proj3, h3, dh3, gates, slab_a3, conv_w, conv_b, wa, ba, wx, bx, lam, token):
    def body(x_ref, h_ref, dh_ref, g_ref, cw_ref, cb_ref, wa_ref, ba_ref, wx_ref, bx_ref, lam_ref, _alias, _token,
             dx_ref, dcw_ref, dcb_ref, dwa_ref, dba_ref, dwx_ref, dbx_ref, dlam_ref, pad):
        row = _row_iota()
        sh = _Shifter(pad)
        x = x_ref[0].astype(F32)
        cw = cw_ref[...]
        wa_v = wa_ref[0]
        wx_v = wx_ref[0]
        lam_v = lam_ref[...]
        xc, xcb, r, i, sp, a, mult, xs = _rnn_gates(x, cw, cb_ref[...], wa_v, ba_ref[...], wx_v, bx_ref[...], lam_v,
                                                    row, sh, [g_ref[k, 0] for k in range(N_SAVED)])
        h = h_ref[0]
        g = _scan_up(_shift_up(a, 1, sh, 0.0), dh_ref[0], sh)
        da = g * _shift_down(h, 1, sh, 0.0)
        dmult = jnp.where(row == 0, 0.0, g * (i * xc))
        gm = g * mult
        di = gm * xc
        dxc = gm * i
        dlog_a = da * a - dmult * (a * a) / mult
        dr = dlog_a * ((-LRU_C) * sp)
        dsp = jnp.sum(dlog_a * ((-LRU_C) * r), axis=0, keepdims=True)
        dlam = dsp * (-_sigmoid(-lam_v))
        dpa = dr * r * (1.0 - r)
        dpx = di * i * (1.0 - i)
        dpab = dpa.astype(BF16)
        dpxb = dpx.astype(BF16)
        dwa = _dot_tn(xcb, dpab)
        dwx = _dot_tn(xcb, dpxb)
        dba = jnp.sum(dpa, axis=0, keepdims=True)
        dbx = jnp.sum(dpx, axis=0, keepdims=True)
        dxc = dxc + _dot_nt(dpab, wa_v) + _dot_nt(dpxb, wx_v)
        dcb = jnp.sum(dxc, axis=0, keepdims=True)
        dx = cw[3:4, :] * dxc
        dcw_rows = [None] * CONVW
        dcw_rows[3] = jnp.sum(dxc * x, axis=0, keepdims=True)
        dxc_up = sh.up(dxc, (1, 2, 3), 0.0)
        for j in (1, 2, 3):
            dx = dx + cw[3 - j:4 - j, :] * dxc_up[j - 1]
            dcw_rows[3 - j] = jnp.sum(dxc * xs[j - 1], axis=0, keepdims=True)
        dx_ref[0] = dx.astype(BF16)
        dcw = jnp.concatenate(dcw_rows, axis=0)
        first = pl.program_id(1) == 0

        @pl.when(first)
        def _():
            dcw_ref[...] = dcw
            dcb_ref[...] = dcb
            dwa_ref[0] = dwa
            dba_ref[...] = dba
            dwx_ref[0] = dwx
            dbx_ref[...] = dbx
            dlam_ref[...] = dlam

        @pl.when(jnp.logical_not(first))
        def _():
            dcw_ref[...] += dcw
            dcb_ref[...] += dcb
            dwa_ref[0] += dwa
            dba_ref[...] += dba
            dwx_ref[0] += dwx
            dbx_ref[...] += dbx
            dlam_ref[...] += dlam

    slab = lambda: pl.BlockSpec((1, S, RBW), lambda n, b: (b, 0, n))
    vec = lambda: pl.BlockSpec((1, RBW), lambda n, b: (0, n))
    mat = lambda: pl.BlockSpec((1, RBW, RBW), lambda n, b: (n, 0, 0))
    taps = lambda: pl.BlockSpec((CONVW, RBW), lambda n, b: (0, n))
    vshape = pltpu.HBM((1, DR), F32)
    mshape = pltpu.HBM((NRB, RBW, RBW), F32)
    return pl.pallas_call(
        body, grid=(NRB, BL), name="rnn_bwd",
        in_specs=[slab(), slab(), slab(), pl.BlockSpec((N_SAVED, 1, S, RBW), lambda n, b: (0, b, 0, n)),
                  taps(), vec(), mat(), vec(), mat(), vec(), vec(),
                  pl.BlockSpec(memory_space=pl.ANY), pl.BlockSpec((8, LANES), lambda n, b: (0, 0))],
        out_specs=[slab(), taps(), vec(), mat(), vec(), mat(), vec(), vec()],
        out_shape=[jax.ShapeDtypeStruct((BL, S, A_W), BF16), pltpu.HBM((CONVW, DR), F32),
                   vshape, mshape, vshape, mshape, vshape, vshape],
        input_output_aliases={11: 0},
        scratch_shapes=[pltpu.VMEM((N_SHIFT_BUFS, S + 2 * SUBLANES, RBW), F32)],
        compiler_params=_cp(("parallel", "arbitrary"), 48 * 1024 * 1024),
    )(*_hbm(proj3, h3, dh3, gates, conv_w, conv_b, wa, ba, wx, bx, lam, slab_a3, token))


NQB = S // QB


def _rms_head(t, gain):
    rstd = lax.rsqrt(jnp.mean(t * t, axis=-1, keepdims=True) + EPS)
    return t * rstd * gain


def _rope(t, cs, sn):
    return t * cs + pltpu.roll(t, HD // 2, 1) * sn


def _rope_t(dy, cs, sn):
    return dy * cs - pltpu.roll(dy, HD // 2, 1) * sn


def _bdot_nt(a, b):
    return lax.dot_general(a, b, (((2,), (2,)), ((0,), (0,))), preferred_element_type=F32)


def _bdot(a, b):
    return lax.dot_general(a, b, (((2,), (1,)), ((0,), (0,))), preferred_element_type=F32)


def _bdot_tn(a, b):
    return lax.dot_general(a, b, (((1,), (1,)), ((0,), (0,))), preferred_element_type=F32)


STRIDE_MAX = 4


def _permute(buf, x, dil, dst, off=0):
    ln = S // dil
    if dil == 1:
        dst[pl.ds(off, S), :] = x.astype(dst.dtype)
        return
    buf[0] = x
    if dil <= STRIDE_MAX:
        for c in range(dil):
            dst[pl.ds(off + c * ln, ln), :] = buf.at[0][pl.ds(c, ln, stride=dil), :].astype(dst.dtype)
        return
    f, r = STRIDE_MAX, dil // STRIDE_MAX
    part = S // f
    for c1 in range(f):
        buf.at[1][pl.ds(c1 * part, part), :] = buf.at[0][pl.ds(c1, part, stride=f), :]
    for c1 in range(f):
        for c2 in range(r):
            dst[pl.ds(off + (c1 + f * c2) * ln, ln), :] = (
                buf.at[1][pl.ds(c1 * part + c2, ln, stride=r), :].astype(dst.dtype))


def _unpermute(buf, xp, dil, dst):
    ln = S // dil
    if dil == 1:
        dst[...] = xp
        return
    if dil <= STRIDE_MAX:
        for c in range(dil):
            dst[pl.ds(c, ln, stride=dil), :] = xp[c * ln:(c + 1) * ln]
        return
    f, r = STRIDE_MAX, dil // STRIDE_MAX
    part = S // f
    for c1 in range(f):
        for c2 in range(r):
            c = c1 + f * c2
            buf.at[1][pl.ds(c1 * part + c2, ln, stride=r), :] = xp[c * ln:(c + 1) * ln]
    for c1 in range(f):
        dst[pl.ds(c1, part, stride=f), :] = buf[1, pl.ds(c1 * part, part), :]


def _blocks3(ref, off=0):
    return ref[pl.ds(off, S), :].reshape(NQB, QB, HD)


def _att_prep(q_ref, k_ref, v_ref, cos_ref, sin_ref, qn, kn, dil, nat, qs, ksp, vsp):
    cs = cos_ref[...]
    sn = sin_ref[...]
    zero = jnp.zeros((QB, HD), BF16)
    ksp[pl.ds(0, QB), :] = zero
    vsp[pl.ds(0, QB), :] = zero
    _permute(nat, _rope(_rms_head(q_ref[0].astype(F32), qn), cs, sn), dil, qs)
    _permute(nat, _rope(_rms_head(k_ref[0].astype(F32), kn), cs, sn), dil, ksp, QB)
    _permute(nat, v_ref[0].astype(F32), dil, vsp, QB)


def _att_scores(qs, ksp, dil):
    nb = S // dil // QB
    q3 = _blocks3(qs)
    shape = (NQB, QB, QB)
    qi = lax.broadcasted_iota(jnp.int32, shape, 1)
    kj = lax.broadcasted_iota(jnp.int32, shape, 2)
    s_c = jnp.where(qi >= kj, _bdot_nt(q3, _blocks3(ksp, QB)) * SCALE, NEG)
    if nb == 1:
        return q3, s_c, None
    jj = lax.broadcasted_iota(jnp.int32, shape, 0)
    ok = (kj >= qi) & ((jj & (nb - 1)) != 0)
    s_p = jnp.where(ok, _bdot_nt(q3, _blocks3(ksp)) * SCALE, NEG)
    return q3, s_c, s_p


def _qkv_spec(kind, g):
    base = OFF_Q // HD + (3 * g + kind) * NH
    return pl.BlockSpec((1, S, HD), lambda b, h: (b, 0, base + h))


def _attn_fwd(proj3, cos_t, sin_t, q_norm, k_norm):
    def body(*refs):
        qkv_refs = refs[:9]
        (cos_ref, sin_ref, qn_ref, kn_ref, att_ref, lse_ref, w_ref, qp_ref, kp_ref, vp_ref,
         nat, qs, ksp, vsp, og) = refs[9:]
        for g, (window, dil) in enumerate(PATTERNS):
            q_ref, k_ref, v_ref = qkv_refs[3 * g:3 * g + 3]
            _att_prep(q_ref, k_ref, v_ref, cos_ref, sin_ref, qn_ref[g:g + 1, :], kn_ref[g:g + 1, :], dil,
                      nat, qs, ksp, vsp)
            qp_ref[g, 0] = qs[...]
            kp_ref[g, 0] = ksp[pl.ds(QB, S), :]
            vp_ref[g, 0] = vsp[pl.ds(QB, S), :]
            _, s_c, s_p = _att_scores(qs, ksp, dil)
            m = jnp.max(s_c, axis=-1, keepdims=True)
            if s_p is not None:
                m = jnp.maximum(m, jnp.max(s_p, axis=-1, keepdims=True))
            e_c = jnp.exp(s_c - m)
            den = jnp.sum(e_c, axis=-1, keepdims=True)
            o = _bdot(e_c.astype(BF16), _blocks3(vsp, QB))
            if s_p is not None:
                e_p = jnp.exp(s_p - m)
                den = den + jnp.sum(e_p, axis=-1, keepdims=True)
                o = o + _bdot(e_p.astype(BF16), _blocks3(vsp))
            _unpermute(nat, (o / den).reshape(S, HD), dil, og.at[g])
            _unpermute(nat, jnp.broadcast_to(m + jnp.log(den), (NQB, QB, HD)).reshape(S, HD), dil,
                       lse_ref.at[g, 0])
        l0 = lse_ref[0, 0]
        l1 = lse_ref[1, 0]
        l2 = lse_ref[2, 0]
        mx = jnp.maximum(jnp.maximum(l0, l1), l2)
        e0 = jnp.exp(l0 - mx)
        e1 = jnp.exp(l1 - mx)
        e2 = jnp.exp(l2 - mx)
        inv = 1.0 / (e0 + e1 + e2)
        w0 = e0 * inv
        w1 = e1 * inv
        w2 = e2 * inv
        w_ref[0, 0] = w0
        w_ref[1, 0] = w1
        w_ref[2, 0] = w2
        att_ref[0] = w0 * og[0] + w1 * og[1] + w2 * og[2]

    in_specs = [_qkv_spec(kind, g) for g in range(NG) for kind in range(3)]
    in_specs += [pl.BlockSpec((S, HD), lambda b, h: (0, 0)), pl.BlockSpec((S, HD), lambda b, h: (0, 0)),
                 pl.BlockSpec((NG, HD), lambda b, h: (0, 0)), pl.BlockSpec((NG, HD), lambda b, h: (0, 0))]
    stat = lambda: pl.BlockSpec((NG, 1, S, HD), lambda b, h: (0, b, 0, h))
    return pl.pallas_call(
        body, grid=(BL, NH), name="attn_fwd",
        in_specs=in_specs,
        out_specs=[pl.BlockSpec((1, S, HD), lambda b, h: (b, 0, h)), stat(), stat(), stat(), stat(), stat()],
        out_shape=[jax.ShapeDtypeStruct((BL, S, ATT), F32),
                   jax.ShapeDtypeStruct((NG, BL, S, ATT), F32),
                   jax.ShapeDtypeStruct((NG, BL, S, ATT), F32)]
        + [jax.ShapeDtypeStruct((NG, BL, S, ATT), BF16)] * 3,
        scratch_shapes=[pltpu.VMEM((2, S, HD), F32), pltpu.VMEM((S, HD), BF16), pltpu.VMEM((S + QB, HD), BF16),
                        pltpu.VMEM((S + QB, HD), BF16), pltpu.VMEM((NG, S, HD), F32)],
        compiler_params=_cp(("parallel", "parallel"), VMEM_BIG),
    )(*_hbm(*([proj3] * 9), cos_t, sin_t, q_norm, k_norm))


def _attn_bwd(proj3, cos_t, sin_t, q_norm, k_norm, lse, wts, qkv_p, datt3, sbar3):
    def norm_rope_bwd(dpost, raw, gain, cs, sn):
        dn = _rope_t(dpost, cs, sn)
        rstd = lax.rsqrt(jnp.mean(raw * raw, axis=-1, keepdims=True) + EPS)
        xh = raw * rstd
        dgain = jnp.sum(dn * xh, axis=0, keepdims=True)
        gd = dn * gain
        draw = rstd * (gd - xh * jnp.mean(gd * xh, axis=-1, keepdims=True))
        return draw, dgain

    def group_body(g, refs, first):
        dil = PATTERNS[g][1]
        (q_ref, k_ref, qp_ref, kp_ref, vp_ref, cos_ref, sin_ref, qn_ref, kn_ref, lse_ref, w_ref, datt_ref,
         sbar_ref, dq_ref, dk_ref, dv_ref, dqn_ref, dkn_ref, nat, ksp, vsp, dos, cvp, lsp, acc) = refs
        qn = qn_ref[g:g + 1, :]
        kn = kn_ref[g:g + 1, :]
        cs = cos_ref[...]
        sn = sin_ref[...]
        qs = qp_ref.at[0, 0]
        zero = jnp.zeros((QB, HD), BF16)
        ksp[pl.ds(0, QB), :] = zero
        vsp[pl.ds(0, QB), :] = zero
        ksp[pl.ds(QB, S), :] = kp_ref[0, 0]
        vsp[pl.ds(QB, S), :] = vp_ref[0, 0]
        wv = w_ref[0, 0]
        _permute(nat, wv * datt_ref[0], dil, dos)
        _permute(nat, wv * sbar_ref[0], dil, cvp)
        _permute(nat, lse_ref[0, 0], dil, lsp)
        q3, s_c, s_p = _att_scores(qs, ksp, dil)
        do3 = _blocks3(dos)
        lse3 = _blocks3(lsp)[:, :, 0:1]
        cv3 = _blocks3(cvp)[:, :, 0:1]
        p_c = jnp.exp(s_c - lse3)
        ds_c = (p_c * (_bdot_nt(do3, _blocks3(vsp, QB)) - cv3)).astype(BF16)
        dq = _bdot(ds_c, _blocks3(ksp, QB))
        acc[0] = _bdot_tn(ds_c, q3).reshape(S, HD)
        acc[1] = _bdot_tn(p_c.astype(BF16), do3).reshape(S, HD)
        if s_p is not None:
            p_p = jnp.exp(s_p - lse3)
            ds_p = (p_p * (_bdot_nt(do3, _blocks3(vsp)) - cv3)).astype(BF16)
            dq = dq + _bdot(ds_p, _blocks3(ksp))
            early = pl.ds(0, S - QB)
            acc[0, early, :] += _bdot_tn(ds_p, q3).reshape(S, HD)[QB:]
            acc[1, early, :] += _bdot_tn(p_p.astype(BF16), do3).reshape(S, HD)[QB:]
        _unpermute(nat, (dq * SCALE).reshape(S, HD), dil, nat.at[0])
        draw, dqn = norm_rope_bwd(nat[0], q_ref[0].astype(F32), qn, cs, sn)
        dq_ref[0] = draw.astype(BF16)
        _unpermute(nat, acc[0] * SCALE, dil, nat.at[0])
        draw, dkn = norm_rope_bwd(nat[0], k_ref[0].astype(F32), kn, cs, sn)
        dk_ref[0] = draw.astype(BF16)
        _unpermute(nat, acc[1], dil, nat.at[0])
        dv_ref[0] = nat[0].astype(BF16)
        dqn8 = jnp.broadcast_to(dqn, (SUBLANES, HD))
        dkn8 = jnp.broadcast_to(dkn, (SUBLANES, HD))

        @pl.when(first)
        def _():
            dqn_ref[0] = dqn8
            dkn_ref[0] = dkn8

        @pl.when(jnp.logical_not(first))
        def _():
            dqn_ref[0] += dqn8
            dkn_ref[0] += dkn8

    def body(*refs):
        group = pl.program_id(0)
        first = (pl.program_id(1) == 0) & (pl.program_id(2) == 0)
        for g in range(NG):
            pl.when(group == g)(lambda g=g: group_body(g, refs, first))

    def raw_spec(kind):
        return pl.BlockSpec((1, S, HD), lambda g, b, h: (b, 0, OFF_Q // HD + (3 * g + kind) * NH + h))

    full = lambda r: pl.BlockSpec((r, HD), lambda g, b, h: (0, 0))
    stat = lambda: pl.BlockSpec((1, 1, S, HD), lambda g, b, h: (g, b, 0, h))
    slab = lambda: pl.BlockSpec((1, S, HD), lambda g, b, h: (b, 0, h))
    out_slab = lambda: pl.BlockSpec((1, S, HD), lambda g, b, h: (b, 0, g * NH + h))
    gains = lambda: pl.BlockSpec((1, SUBLANES, HD), lambda g, b, h: (g, 0, 0))
    big = jax.ShapeDtypeStruct((BL, S, GW), BF16)
    vecs = jax.ShapeDtypeStruct((NG, SUBLANES, HD), F32)
    return pl.pallas_call(
        body, grid=(NG, BL, NH), name="attn_bwd",
        in_specs=[raw_spec(0), raw_spec(1), stat(), stat(), stat(), full(S), full(S), full(NG), full(NG),
                  stat(), stat(), slab(), slab()],
        out_specs=[out_slab(), out_slab(), out_slab(), gains(), gains()],
        out_shape=[big, big, big, vecs, vecs],
        scratch_shapes=[pltpu.VMEM((2, S, HD), F32), pltpu.VMEM((S + QB, HD), BF16),
                        pltpu.VMEM((S + QB, HD), BF16), pltpu.VMEM((S, HD), BF16), pltpu.VMEM((S, HD), F32),
                        pltpu.VMEM((S, HD), F32), pltpu.VMEM((2, S, HD), F32)],
        compiler_params=_cp(("arbitrary", "arbitrary", "arbitrary"), VMEM_BIG),
    )(*_hbm(proj3, proj3, *qkv_p, cos_t, sin_t, q_norm, k_norm, lse, wts, datt3, sbar3))


def _tail(x, proj, h, att, p, tgt, w_o_rnn, w_o_att_t, w_out, w_pg, w_ple_t, norm_ple, b_pg, tm=256):
    nt = T // tm
    inv_d = 1.0 / D

    def body(x_ref, h_ref, zr_ref, att_ref, za_ref, g0a_ref, g0b_ref, g1a_ref, g1b_ref, p_ref, tgt_ref,
             np_ref, bpg_ref, wor_hbm, woa_hbm, wout_hbm, wpg_hbm, wple_hbm,
             dx1_ref, merged_ref, n1_ref, dpre_ref, dpe_ref, dyr_ref, dya_ref, slab_a_ref, slab_c_ref, dh_ref,
             datt_ref, sbar_ref, yrnn_ref, yatt_ref, loss_ref, dnp_ref, dbpg_ref,
             wor, woa, wout, wpg, wple, sems):
        first = pl.program_id(0) == 0

        @pl.when(first)
        def _():
            pairs = ((wor_hbm, wor), (woa_hbm, woa), (wout_hbm, wout), (wpg_hbm, wpg), (wple_hbm, wple))
            _copy_together([pltpu.make_async_copy(src, dst, sems.at[k]) for k, (src, dst) in enumerate(pairs)])

        xv = x_ref[...]
        hv = h_ref[...]
        zr = zr_ref[...].astype(F32)
        av = att_ref[...]
        za = za_ref[...].astype(F32)
        szr = _sigmoid(zr)
        silu_r = zr * szr
        yrnn_b = (hv * silu_r).astype(BF16)
        sza = _sigmoid(za)
        silu_a = za * sza
        yatt_b = (av * silu_a).astype(BF16)
        yrnn_ref[...] = yrnn_b
        yatt_ref[...] = yatt_b
        yr = _dot(yrnn_b, wor[...])
        ya = _dot_nt(yatt_b, woa[...])
        g0 = _sigmoid(jnp.concatenate([g0a_ref[...], g0b_ref[...]], axis=1).astype(F32))
        g1 = _sigmoid(jnp.concatenate([g1a_ref[...], g1b_ref[...]], axis=1).astype(F32))
        merged_b = (g0 * yr + g1 * ya).astype(BF16)
        merged_ref[...] = merged_b
        x1 = xv + _dot(merged_b, wout[...])
        rstd = lax.rsqrt(jnp.mean(x1 * x1, axis=-1, keepdims=True) + EPS)
        xh = x1 * rstd
        npl = np_ref[...]
        n1_b = (xh * npl).astype(BF16)
        n1_ref[...] = n1_b
        pg = _sigmoid(_dot(n1_b, wpg[...]) + bpg_ref[...])
        pe = _dot_nt(p_ref[...].astype(BF16), wple[...])
        err = x1 + pg * pe - tgt_ref[...]
        loss_t = 0.5 * inv_d * jnp.sum(err * err)
        dy = err * inv_d
        dpe_ref[...] = (dy * pg).astype(BF16)
        dpre = dy * pe * pg * (1.0 - pg)
        dpre_b = dpre.astype(BF16)
        dpre_ref[...] = dpre_b
        dn1 = _dot_nt(dpre_b, wpg[...])
        dnp = jnp.sum(dn1 * xh, axis=0, keepdims=True)
        dbpg = jnp.sum(dpre, axis=0, keepdims=True)
        gd = dn1 * npl
        dx1 = dy + rstd * (gd - xh * jnp.mean(gd * xh, axis=-1, keepdims=True))
        dx1_ref[...] = dx1
        dmerged = _dot_nt(dx1.astype(BF16), wout[...])
        dyr_b = (dmerged * g0).astype(BF16)
        dya_b = (dmerged * g1).astype(BF16)
        dyr_ref[...] = dyr_b
        dya_ref[...] = dya_b
        slab_c_ref[:, ATT:ATT + D] = (dmerged * yr * g0 * (1.0 - g0)).astype(BF16)
        slab_c_ref[:, ATT + D:ATT + 2 * D] = (dmerged * ya * g1 * (1.0 - g1)).astype(BF16)
        dyrnn = _dot_nt(dyr_b, wor[...])
        dyatt = _dot(dya_b, woa[...])
        dh_ref[...] = dyrnn * silu_r
        slab_a_ref[...] = (dyrnn * hv * szr * (1.0 + zr * (1.0 - szr))).astype(BF16)
        datt = dyatt * silu_a
        datt_ref[...] = datt
        slab_c_ref[:, 0:ATT] = (dyatt * av * sza * (1.0 + za * (1.0 - sza))).astype(BF16)
        da = datt * av
        for hh in range(NH):
            seg = slice(hh * HD, (hh + 1) * HD)
            sbar_ref[:, seg] = jnp.broadcast_to(jnp.sum(da[:, seg], axis=-1, keepdims=True), (tm, HD))

        @pl.when(first)
        def _():
            loss_ref[...] = jnp.full((8, LANES), loss_t, F32)
            dnp_ref[...] = dnp
            dbpg_ref[...] = dbpg

        @pl.when(jnp.logical_not(first))
        def _():
            loss_ref[...] += jnp.full((8, LANES), loss_t, F32)
            dnp_ref[...] += dnp
            dbpg_ref[...] += dbpg

    tok = lambda w: pl.BlockSpec((tm, w), lambda i: (i, 0))
    col = lambda w, blk: pl.BlockSpec((tm, w), lambda i: (i, blk))
    vec = lambda: pl.BlockSpec((1, D), lambda i: (0, 0))
    hbm = lambda: pl.BlockSpec(memory_space=pl.ANY)
    gb = OFF_G // 512
    in_specs = [tok(D), tok(DR), col(DR, 1), tok(ATT), col(ATT, OFF_ZA // ATT),
                col(512, gb), col(512, gb + 1), col(512, gb + 2), col(512, gb + 3),
                tok(PLE), tok(D), vec(), vec(), hbm(), hbm(), hbm(), hbm(), hbm()]
    sh = lambda w, dt: jax.ShapeDtypeStruct((T, w), dt)
    out_shape = [sh(D, F32), sh(D, BF16), sh(D, BF16), sh(D, BF16), sh(D, BF16), sh(D, BF16), sh(D, BF16),
                 sh(A_W, BF16), sh(C_W, BF16), sh(DR, F32), sh(ATT, F32), sh(ATT, F32),
                 sh(DR, BF16), sh(ATT, BF16),
                 jax.ShapeDtypeStruct((8, LANES), F32), jax.ShapeDtypeStruct((1, D), F32),
                 jax.ShapeDtypeStruct((1, D), F32)]
    out_specs = [tok(D), tok(D), tok(D), tok(D), tok(D), tok(D), tok(D), col(DR, 1), tok(C_W), tok(DR),
                 tok(ATT), tok(ATT), tok(DR), tok(ATT),
                 pl.BlockSpec((8, LANES), lambda i: (0, 0)), vec(), vec()]
    return pl.pallas_call(
        body, grid=(nt,), name="tail_fwd_bwd",
        in_specs=in_specs, out_specs=out_specs, out_shape=out_shape,
        scratch_shapes=[pltpu.VMEM((DR, D), BF16), pltpu.VMEM((D, ATT), BF16), pltpu.VMEM((D, D), BF16),
                        pltpu.VMEM((D, D), BF16), pltpu.VMEM((D, PLE), BF16), pltpu.SemaphoreType.DMA((5,))],
        compiler_params=_cp(("arbitrary",), VMEM_BIG),
    )(*_hbm(x, h, proj, att, proj, proj, proj, proj, proj, p, tgt, norm_ple, b_pg, w_o_rnn, w_o_att_t, w_out, w_pg,
            w_ple_t))


def _rope_tables():
    pos = np.arange(S, dtype=np.float32)
    inv_freq = np.float32(ROPE_THETA) ** (-np.arange(0, HD, 2, dtype=np.float32) / np.float32(HD))
    ang = (pos[:, None] * inv_freq[None, :]).astype(np.float32).astype(np.float64)
    cos, sin = np.cos(ang).astype(np.float32), np.sin(ang).astype(np.float32)
    return jnp.asarray(np.concatenate([cos, cos], axis=1)), jnp.asarray(np.concatenate([-sin, sin], axis=1))


def _local_step(x, p, tgt, project, other_weights, norm_mix, conv_b,
                w_rg_a, b_rg_a, w_rg_x, b_rg_x, lam, q_norm, k_norm, norm_ple, b_pg, start_reduce=None,
                entry_token=None):
    if start_reduce is None:
        start_reduce = lambda arrs, tag: (jnp.zeros((8, LANES), F32), arrs)
    if entry_token is None:
        entry_token = jnp.zeros((8, LANES), F32)
    cos_t, sin_t = _rope_tables()
    wa_b = w_rg_a.astype(BF16)
    wx_b = w_rg_x.astype(BF16)

    hn = _rmsnorm_fwd(x, norm_mix, entry_token)
    proj, w_bufs, chips, conv_w, token = project(hn)
    proj3 = proj.reshape(BL, S, NIN)
    h3, gates = _rnn_fwd(proj3, conv_w, conv_b, wa_b, b_rg_a, wx_b, b_rg_x, lam, token)
    att3, lse, wts, *qkv_p = _attn_fwd(proj3, cos_t, sin_t, q_norm, k_norm)
    w_o_rnn, w_o_att_t, w_out, w_pg, w_ple_t = other_weights(att3)
    (dx1, merged, n1, dpre, dpe, dyr, dya, slab_a, slab_c, dh, datt, sbar, yrnn, yatt, loss8, dnp, dbpg) = _tail(
        x, proj, h3.reshape(T, DR), att3.reshape(T, ATT), p, tgt, w_o_rnn, w_o_att_t, w_out, w_pg, w_ple_t,
        norm_ple, b_pg)

    token, pending_out = start_reduce([
        _mm_tn(yrnn, dyr, 640, T, "dw_o_rnn"),
        _mm_tn(dya, yatt, 512, T, "dw_o_att_t"),
        _mm_tn(merged, dx1, 512, T // 2, "dw_out"),
        _mm_tn(n1, dpre, 512, T, "dw_ple_gate"),
        _mm_tn(dpe, p, 512, T, "dw_ple_t")], "out")

    slab_a3, dcw, dcb, dwa, dba, dwx, dbx, dlam = _rnn_bwd(
        proj3, h3, dh.reshape(BL, S, DR), gates, slab_a.reshape(BL, S, A_W), conv_w, conv_b, wa_b, b_rg_a, wx_b, b_rg_x, lam,
        token)
    datt3 = datt.reshape(BL, S, ATT)
    sbar3 = sbar.reshape(BL, S, ATT)
    *slabs, dqn, dkn = _attn_bwd(proj3, cos_t, sin_t, q_norm, k_norm, lse, wts, qkv_p, datt3, sbar3)
    pieces =[slab_a3.reshape(T, A_W)] + [t.reshape(T, GW) for t in slabs] + [slab_c]
    dw_in_t, db_in = _dw_in(pieces, hn)
    token, pending_in = start_reduce([dw_in_t], "in")
    grad_x, dnm = _grad_x(pieces, w_bufs, chips, token, x, dx1, norm_mix)

    small = dict(w_rg_a=dwa, w_rg_x=dwx, norm_mix=dnm, b_in=db_in, conv_b=dcb, b_rg_a=dba, b_rg_x=dbx,
                 lru_lambda=dlam, q_norm=dqn, k_norm=dkn, norm_ple=dnp, b_ple_gate=dbpg, conv_w=dcw, loss=loss8)
    return grad_x, pending_out, pending_in, small


MESH = pl.DeviceIdType.MESH
HBM_SPEC = pl.BlockSpec(memory_space=pl.ANY)


def _my_pos():
    return lax.axis_index("x"), lax.axis_index("y"), lax.axis_index("c")


def _flip(pos, k):
    x, y, c = pos
    return (1 - x if k & 4 else x, 1 - y if k & 2 else y, 1 - c if k & 1 else c)


def _lin(pos):
    return 4 * pos[0] + 2 * pos[1] + pos[2]


def _chip(pos):
    return 2 * pos[0] + pos[1]


HBM_ONLY = pl.BlockSpec(memory_space=pltpu.HBM)
SEM_SPEC = pl.BlockSpec(memory_space=pltpu.SEMAPHORE)
SPLIT_COPY = pltpu.CompilerParams(has_side_effects=pltpu.SideEffectType.DATAFLOW_SIDE_EFFECTING)


def _chip_peers(me):
    return [_flip(me, 4), _flip(me, 2), _flip(me, 6)]


def _between_chips_start(parts, name):
    na = len(parts)

    def body(*refs):
        a_refs = refs[:na]
        land_refs = refs[na:2 * na]
        send_sems, recv_sems = refs[2 * na], refs[2 * na + 1]
        token = refs[-1]
        me = _my_pos()
        myq = _chip(me)
        for i in range(na):
            for j, peer in enumerate(_chip_peers(me)):
                pltpu.make_async_remote_copy(
                    src_ref=a_refs[i].at[_chip(peer)], dst_ref=land_refs[i].at[myq],
                    send_sem=send_sems.at[3 * i + j], recv_sem=recv_sems.at[3 * i + j],
                    device_id=peer, device_id_type=MESH).start()
        token[...] = jnp.zeros_like(token)

    hbm = [pltpu.HBM(a.shape, a.dtype) for a in parts]
    srcs = [pltpu.with_memory_space_constraint(a, pltpu.HBM) for a in parts]
    lands = [pltpu.with_memory_space_constraint(lax.empty(a.shape, a.dtype), pltpu.HBM) for a in parts]
    res = pl.pallas_call(
        body, name=name,
        out_shape=(pltpu.SemaphoreType.DMA((3 * na,)), pltpu.SemaphoreType.DMA((3 * na,)), *hbm, *hbm,
                   jax.ShapeDtypeStruct((8, LANES), F32)),
        in_specs=[HBM_ONLY] * (2 * na),
        out_specs=(SEM_SPEC, SEM_SPEC, *([HBM_ONLY] * (2 * na)), pl.BlockSpec(memory_space=pltpu.VMEM)),
        input_output_aliases={i: 2 + i for i in range(2 * na)},
        compiler_params=SPLIT_COPY,
    )(*srcs, *lands)
    return res[-1], (res[0], res[1], list(res[2:2 + na]), list(res[2 + na:2 + 2 * na]))


def _between_chips_wait(pending, after, name):
    send_sems, recv_sems, parts, lands = pending
    na = len(parts)

    def body(*refs):
        a_refs = refs[:na]
        land_refs = refs[na:2 * na]
        send_sems, recv_sems = refs[2 * na], refs[2 * na + 1]
        me = _my_pos()
        for i in range(na):
            for j, peer in enumerate(_chip_peers(me)):
                cp = pltpu.make_async_remote_copy(
                    src_ref=a_refs[i].at[_chip(peer)], dst_ref=land_refs[i].at[_chip(peer)],
                    send_sem=send_sems.at[3 * i + j], recv_sem=recv_sems.at[3 * i + j],
                    device_id=peer, device_id_type=MESH)
                cp.wait_send()
                cp.wait_recv()

    hbm = [pltpu.HBM(a.shape, a.dtype) for a in parts]
    res = pl.pallas_call(
        body, name=name, out_shape=(*hbm, *hbm),
        in_specs=[HBM_ONLY] * (2 * na) + [SEM_SPEC, SEM_SPEC, pl.BlockSpec(memory_space=pl.ANY)],
        out_specs=[HBM_ONLY] * (2 * na),
        input_output_aliases={i: i for i in range(2 * na)},
        compiler_params=SPLIT_COPY,
    )(*parts, *lands, send_sems, recv_sems, after)
    return list(res[:na]), list(res[na:])


def _remote(src, dst, send_sems, recv_sems, idx, peer):
    return pltpu.make_async_remote_copy(src_ref=src, dst_ref=dst, send_sem=send_sems.at[idx],
                                        recv_sem=recv_sems.at[idx], device_id=peer, device_id_type=MESH)


def _copies_own(bufs, me):
    return [(bufs[0], bufs[1].at[me[2]], 0, _flip(me, 1))]


def _copies_near(bufs, me):
    return [(bufs[0], bufs[1].at[0, me[2]], 0, _flip(me, 2)), (bufs[0], bufs[1].at[1, me[2]], 1, _flip(me, 4))]


def _copies_far(bufs, me):
    return [(bufs[0], bufs[1].at[me[2]], 0, _flip(me, 6))]


def _copies_others(bufs, me):
    na = len(bufs) // 2
    return [(bufs[i], bufs[na + i].at[_lin(me)], 7 * i + k - 1, _flip(me, k))
            for i in range(na) for k in range(1, NDEV)]


def _copies_exchange(bufs, me):
    na = len(bufs) // 2
    return [(bufs[i].at[_lin(_flip(me, k))], bufs[na + i].at[_lin(me)], 7 * i + k - 1, _flip(me, k))
            for i in range(na) for k in range(1, NDEV)]


GROUP_COPIES = dict(own=_copies_own, near=_copies_near, far=_copies_far, others=_copies_others,
                    exchange=_copies_exchange)
GROUP_COUNT = dict(own=1, near=2, far=1)
TO_ALL = ("others", "exchange")


def _gather_start(bufs, groups, after, name):
    nb = len(bufs)
    ng = len(groups)

    def body(*refs):
        b = refs[:nb]
        sems = refs[nb + 1:nb + 1 + 2 * ng]
        token = refs[-1]
        me = _my_pos()
        for gi, (group, idx) in enumerate(groups):
            for src, dst, k, peer in GROUP_COPIES[group]([b[i] for i in idx], me):
                _remote(src, dst, sems[2 * gi], sems[2 * gi + 1], k, peer).start()
        token[...] = jnp.zeros_like(token)

    sem_t = []
    for group, idx in groups:
        cnt = 7 * (len(idx) // 2) if group in TO_ALL else GROUP_COUNT[group]
        sem_t += [pltpu.SemaphoreType.DMA((cnt,)), pltpu.SemaphoreType.DMA((cnt,))]
    ins = [pltpu.with_memory_space_constraint(a, pltpu.HBM) for a in bufs]
    res = pl.pallas_call(
        body, name=name,
        out_shape=(*sem_t, *[pltpu.HBM(a.shape, a.dtype) for a in bufs], jax.ShapeDtypeStruct((8, LANES), F32)),
        in_specs=[HBM_ONLY] * nb + [pl.BlockSpec(memory_space=pl.ANY)],
        out_specs=(*([SEM_SPEC] * (2 * ng)), *([HBM_ONLY] * nb), pl.BlockSpec(memory_space=pltpu.VMEM)),
        input_output_aliases={i: 2 * ng + i for i in range(nb)},
        compiler_params=SPLIT_COPY,
    )(*ins, after)
    return res[-1], list(res[2 * ng:2 * ng + nb]), [(res[2 * gi], res[2 * gi + 1]) for gi in range(ng)]


def _gather_wait(group, send_sems, recv_sems, bufs, after, name):
    nb = len(bufs)
    copies = GROUP_COPIES[group]
    afters = list(after) if isinstance(after, (list, tuple)) else [after]

    def body(*refs):
        b = refs[:nb]
        ss, rs = refs[nb], refs[nb + 1]
        me = _my_pos()
        for src, dst, idx, peer in copies(b, me):
            if group in TO_ALL:
                landed = b[nb // 2 + idx // 7].at[_lin(peer)]
            elif group == "own":
                landed = b[1].at[1 - me[2]]
            else:
                landed = dst
            cp = _remote(src, landed, ss, rs, idx, peer)
            cp.wait_send()
            cp.wait_recv()

    res = pl.pallas_call(
        body, name=name, out_shape=[pltpu.HBM(a.shape, a.dtype) for a in bufs],
        in_specs=[HBM_ONLY] * nb + [SEM_SPEC, SEM_SPEC] + [pl.BlockSpec(memory_space=pl.ANY)] * len(afters),
        out_specs=[HBM_ONLY] * nb,
        input_output_aliases={i: i for i in range(nb)},
        compiler_params=SPLIT_COPY,
    )(*bufs, send_sems, recv_sems, *afters)
    return list(res)


def _forward_to_sibling(buf, name):
    n = buf.shape[0]

    def body(_in_ref, out_ref, send_sems, recv_sems):
        me = _my_pos()
        c = me[2]
        sibling = _flip(me, 1)
        sends = []
        for r in range(n):
            cp = _remote(out_ref.at[r, c], out_ref.at[r, c], send_sems, recv_sems, r, sibling)
            cp.start()
            sends.append(cp)
        for r in range(n):
            _remote(out_ref.at[r, c], out_ref.at[r, 1 - c], send_sems, recv_sems, r, sibling).wait_recv()
        for cp in sends:
            cp.wait_send()

    return pl.pallas_call(
        body, name=name, out_shape=jax.ShapeDtypeStruct(buf.shape, buf.dtype),
        in_specs=[HBM_SPEC], out_specs=HBM_SPEC,
        scratch_shapes=[pltpu.SemaphoreType.DMA((n,)), pltpu.SemaphoreType.DMA((n,))],
        input_output_aliases={0: 0},
    )(buf)


def _scalar(v):
    return jnp.asarray(v, jnp.int32).reshape(1)


def _exchange_sum_within_chip(part, name):
    nq, _, rows, width = part.shape

    def body(c_ref, mine_ref, src_hbm, out_ref, land_hbm, buf, copy_sem, send_sems, recv_sems):
        q = pl.program_id(0)
        me = _my_pos()
        sibling = _flip(me, 1)

        def remote(k):
            return pltpu.make_async_remote_copy(
                src_ref=src_hbm.at[k, 1 - me[2]], dst_ref=land_hbm.at[k], send_sem=send_sems.at[k],
                recv_sem=recv_sems.at[k], device_id=sibling, device_id_type=MESH)

        @pl.when(q == 0)
        def _():
            for k in range(nq):
                remote(k).start()

        remote(q).wait_recv()
        arrived = pltpu.make_async_copy(land_hbm.at[q], buf, copy_sem.at[0])
        arrived.start()
        arrived.wait()
        out_ref[0] = (mine_ref[0, 0].astype(F32) + buf[...].astype(F32)).astype(out_ref.dtype)

        @pl.when(q == nq - 1)
        def _():
            for k in range(nq):
                remote(k).wait_send()

    blk = (rows, width)
    summed, _ = pl.pallas_call(
        body, name=name,
        grid_spec=pltpu.PrefetchScalarGridSpec(
            num_scalar_prefetch=1, grid=(nq,),
            in_specs=[pl.BlockSpec((1, 1) + blk, lambda q, c_ref: (q, c_ref[0], 0, 0)), HBM_SPEC],
            out_specs=[pl.BlockSpec((1,) + blk, lambda q, c_ref: (q, 0, 0)), HBM_SPEC],
            scratch_shapes=[pltpu.VMEM(blk, part.dtype), pltpu.SemaphoreType.DMA((1,)),
                            pltpu.SemaphoreType.DMA((nq,)), pltpu.SemaphoreType.DMA((nq,))]),
        out_shape=[jax.ShapeDtypeStruct((nq,) + blk, part.dtype)] * 2,
        compiler_params=_cp(("arbitrary",), VMEM_BIG),
    )(_scalar(lax.axis_index("c")), *_hbm(part, part))
    return summed


def _others(q, mine, nblk=NCHIP):
    return jnp.where(q == mine, (q + 1) % nblk, q)


def _sum_chips_adamw(own, recv, wv, mv, vv, token, tr, name):
    _, r, w = recv.shape

    def body(q_ref, own_ref, r0, r1, r2, r3, w_ref, m_ref, v_ref, _token, g_ref, d_ref, m2_ref, v2_ref):
        myq = q_ref[0]
        acc = None
        for q, r_ref in enumerate((r0, r1, r2, r3)):
            term = jnp.where(myq == q, own_ref[0], r_ref[0]).astype(F32)
            acc = term if acc is None else acc + term
        g_ref[...] = acc
        delta, m2, v2 = _adam_math(w_ref[...], acc, m_ref[...], v_ref[...])
        d_ref[...] = delta
        m2_ref[...] = m2
        v2_ref[...] = v2

    def recv_spec(q):
        return pl.BlockSpec((1, tr, w), lambda i, q_ref: (_others(q, q_ref[0]), i, 0))

    rows = lambda: pl.BlockSpec((tr, w), lambda i, q_ref: (i, 0))
    shp = jax.ShapeDtypeStruct((r, w), F32)
    return pl.pallas_call(
        body, name=name,
        grid_spec=pltpu.PrefetchScalarGridSpec(
            num_scalar_prefetch=1, grid=(r // tr,),
            in_specs=[pl.BlockSpec((1, tr, w), lambda i, q_ref: (q_ref[0], i, 0))]
            + [recv_spec(q) for q in range(NCHIP)] + [rows(), rows(), rows()]
            + [pl.BlockSpec((8, LANES), lambda i, q_ref: (0, 0))],
            out_specs=[rows(), rows(), rows(), rows()]),
        out_shape=[shp, shp, shp, shp],
        compiler_params=_cp(("arbitrary",), VMEM_MID),
    )(_scalar(_chip(_my_pos())), *_hbm(own, recv, recv, recv, recv, wv, mv, vv, token))


def _sum_blocks_small(own, recv, mine, transpose, name):
    na = len(recv)
    nblk = recv[0].shape[0]

    def body(q_ref, *refs):
        me = q_ref[0]
        for i in range(na):
            acc = None
            for q in range(nblk):
                term = jnp.where(me == q, refs[i][0], refs[na * (1 + q) + i][0]).astype(F32)
                acc = term if acc is None else acc + term
            refs[na * (1 + nblk) + i][...] = acc.T if transpose[i] else acc

    def oshape(a, tr):
        r, w = a.shape[1:]
        return (w, r) if tr else (r, w)

    own_spec = lambda a: pl.BlockSpec((1,) + a.shape[1:], lambda s, q_ref: (q_ref[0], 0, 0))
    recv_spec = lambda a, q: pl.BlockSpec((1,) + a.shape[1:], lambda s, q_ref: (_others(q, q_ref[0], nblk), 0, 0))
    out_spec = lambda shp: pl.BlockSpec(shp, lambda s, q_ref: (0, 0))
    in_specs = [own_spec(a) for a in own]
    for q in range(nblk):
        in_specs += [recv_spec(a, q) for a in recv]
    return pl.pallas_call(
        body, name=name,
        grid_spec=pltpu.PrefetchScalarGridSpec(
            num_scalar_prefetch=1, grid=(1,), in_specs=in_specs,
            out_specs=[out_spec(oshape(a, tr)) for a, tr in zip(recv, transpose)]),
        out_shape=[jax.ShapeDtypeStruct(oshape(a, tr), F32) for a, tr in zip(recv, transpose)],
        compiler_params=_cp(("arbitrary",), VMEM_MID),
    )(_scalar(mine), *_hbm(*own, *(list(recv) * nblk)))


def _rep_offsets():
    offs = []
    o = 0
    for r in REP_ROWS:
        offs.append(o)
        o += r
    return offs


LOSS_ROW = REP_TOTAL_ROWS


def _pack_small_grads(g):
    offs = _rep_offsets()

    def body(dwa, dwx, dnm, dbin, dcb, dba, dbx, dlam, dqn, dkn, dnp, dbpg, loss, o_ref):
        o_ref[pl.ds(REP_TOTAL_ROWS - 2, NDEV * REP_ROWS_DEV - REP_TOTAL_ROWS + 2), :] = jnp.zeros(
            (NDEV * REP_ROWS_DEV - REP_TOTAL_ROWS + 2, LANES), F32)
        o_ref[pl.ds(LOSS_ROW, 1), :] = loss[0:1, :]
        for n in range(NRB):
            o_ref[pl.ds(offs[0] + n * RBW, RBW), :] = dwa[n]
            o_ref[pl.ds(offs[1] + n * RBW, RBW), :] = dwx[n]

        def put_vec(off, ref, rows):
            for k in range(rows):
                o_ref[pl.ds(off + k, 1), :] = ref[:, k * LANES:(k + 1) * LANES]

        put_vec(offs[2], dnm, REP_ROWS[2])
        put_vec(offs[3], dbin, REP_ROWS[3])
        put_vec(offs[4], dcb, REP_ROWS[4])
        put_vec(offs[5], dba, REP_ROWS[5])
        put_vec(offs[6], dbx, REP_ROWS[6])
        put_vec(offs[7], dlam, REP_ROWS[7])
        for k in range(NG):
            o_ref[pl.ds(offs[8] + k, 1), :] = dqn[k, 0:1, :]
            o_ref[pl.ds(offs[9] + k, 1), :] = dkn[k, 0:1, :]
        put_vec(offs[10], dnp, REP_ROWS[10])
        put_vec(offs[11], dbpg, REP_ROWS[11])

    args = [g["w_rg_a"], g["w_rg_x"], g["norm_mix"], g["b_in"], g["conv_b"], g["b_rg_a"], g["b_rg_x"],
            g["lru_lambda"], g["q_norm"], g["k_norm"], g["norm_ple"], g["b_ple_gate"], g["loss"]]
    full = lambda shp: pl.BlockSpec(shp, lambda: (0,) * len(shp))
    return pl.pallas_call(
        body, name="pack_small_grads",
        in_specs=[full(a.shape) for a in args],
        out_specs=full((NDEV * REP_ROWS_DEV, LANES)),
        out_shape=jax.ShapeDtypeStruct((NDEV * REP_ROWS_DEV, LANES), F32),
    )(*_hbm(*args))


def _adam_math(wv, gv, mv, vv):
    c1 = 1.0 - B1 ** STEP
    c2 = 1.0 - B2 ** STEP
    m2 = B1 * mv + (1.0 - B1) * gv
    v2 = B2 * vv + (1.0 - B2) * (gv * gv)
    delta = (-LR) * ((m2 / c1) / (jnp.sqrt(v2 / c2) + AEPS) + WD * wv)
    return delta, m2, v2


def _adamw_small(rep_flat, w, m, v):
    offs = _rep_offsets()
    n = len(REP_NAMES)

    def body(*refs):
        g_ref = refs[0]
        w_refs = refs[1:1 + n]
        m_refs = refs[1 + n:1 + 2 * n]
        v_refs = refs[1 + 2 * n:1 + 3 * n]
        outs = refs[1 + 3 * n:]
        go, do, mo, vo = outs[:n], outs[n:2 * n], outs[2 * n:3 * n], outs[3 * n:]

        def emit(i, idx, gv):
            go[i][idx] = gv
            delta, m2, v2 = _adam_math(w_refs[i][idx], gv, m_refs[i][idx], v_refs[i][idx])
            do[i][idx] = delta
            mo[i][idx] = m2
            vo[i][idx] = v2

        for i in range(n):
            if i < 2:
                for b in range(NRB):
                    emit(i, b, g_ref[pl.ds(offs[i] + b * RBW, RBW), :])
            elif REP_NAMES[i] in ("q_norm", "k_norm"):
                emit(i, slice(None), g_ref[pl.ds(offs[i], NG), :])
            else:
                gv = jnp.concatenate([g_ref[pl.ds(offs[i] + k, 1), :] for k in range(REP_ROWS[i])], axis=1)
                emit(i, slice(None), gv)

    full = lambda shp: pl.BlockSpec(shp, lambda: (0,) * len(shp))
    pspecs = [full(a.shape) for a in w]
    pshapes = [jax.ShapeDtypeStruct(a.shape, F32) for a in w]
    res = pl.pallas_call(
        body, name="adamw_small",
        in_specs=[full(rep_flat.shape)] + pspecs * 3,
        out_specs=pspecs * 4, out_shape=pshapes * 4,
        compiler_params=_cp(None, VMEM_MID),
    )(*_hbm(rep_flat, *w, *m, *v))
    return res[:n], res[n:2 * n], res[2 * n:3 * n], res[3 * n:]


def _adamw_many(w, g, m, v, token):
    n = len(w)

    def body(*refs):
        for i in range(n):
            delta, m2, v2 = _adam_math(refs[i][...], refs[n + i][...], refs[2 * n + i][...], refs[3 * n + i][...])
            refs[4 * n + 1 + i][...] = delta
            refs[5 * n + 1 + i][...] = m2
            refs[6 * n + 1 + i][...] = v2

    full = lambda shp: pl.BlockSpec(shp, lambda: (0,) * len(shp))
    specs = [full(a.shape) for a in w]
    shapes = [jax.ShapeDtypeStruct(a.shape, F32) for a in w]
    res = pl.pallas_call(
        body, name="adamw_shards",
        in_specs=specs * 4 + [full((8, LANES))], out_specs=specs * 3, out_shape=shapes * 3,
        compiler_params=_cp(None, VMEM_MID),
    )(*_hbm(*w, *g, *m, *v, token))
    return res[:n], res[n:2 * n], res[2 * n:]


def kernel(x, p, norm_mix, w_in, b_in, conv_w, conv_b, w_rg_a, b_rg_a, w_rg_x, b_rg_x, lru_lambda, q_norm, k_norm, w_o_rnn, w_o_att, w_out, norm_ple, w_ple_gate, b_ple_gate, w_ple, loss_target, m_norm_mix, m_w_in, m_b_in, m_conv_w, m_conv_b, m_w_rg_a, m_b_rg_a, m_w_rg_x, m_b_rg_x, m_lru_lambda, m_q_norm, m_k_norm, m_w_o_rnn, m_w_o_att, m_w_out, m_norm_ple, m_w_ple_gate, m_b_ple_gate, m_w_ple, v_norm_mix, v_w_in, v_b_in, v_conv_w, v_conv_b, v_w_rg_a, v_b_rg_a, v_w_rg_x, v_b_rg_x, v_lru_lambda, v_q_norm, v_k_norm, v_w_o_rnn, v_w_o_att, v_w_out, v_norm_ple, v_w_ple_gate, v_b_ple_gate, v_w_ple):
    w = dict(norm_mix=norm_mix, w_in=w_in, b_in=b_in, conv_w=conv_w, conv_b=conv_b, w_rg_a=w_rg_a, b_rg_a=b_rg_a,
             w_rg_x=w_rg_x, b_rg_x=b_rg_x, lru_lambda=lru_lambda, q_norm=q_norm, k_norm=k_norm, w_o_rnn=w_o_rnn,
             w_o_att=w_o_att, w_out=w_out, norm_ple=norm_ple, w_ple_gate=w_ple_gate, b_ple_gate=b_ple_gate,
             w_ple=w_ple)
    m = dict(norm_mix=m_norm_mix, w_in=m_w_in, b_in=m_b_in, conv_w=m_conv_w, conv_b=m_conv_b, w_rg_a=m_w_rg_a,
             b_rg_a=m_b_rg_a, w_rg_x=m_w_rg_x, b_rg_x=m_b_rg_x, lru_lambda=m_lru_lambda, q_norm=m_q_norm,
             k_norm=m_k_norm, w_o_rnn=m_w_o_rnn, w_o_att=m_w_o_att, w_out=m_w_out, norm_ple=m_norm_ple,
             w_ple_gate=m_w_ple_gate, b_ple_gate=m_b_ple_gate, w_ple=m_w_ple)
    v = dict(norm_mix=v_norm_mix, w_in=v_w_in, b_in=v_b_in, conv_w=v_conv_w, conv_b=v_conv_b, w_rg_a=v_w_rg_a,
             b_rg_a=v_b_rg_a, w_rg_x=v_w_rg_x, b_rg_x=v_b_rg_x, lru_lambda=v_lru_lambda, q_norm=v_q_norm,
             k_norm=v_k_norm, w_o_rnn=v_w_o_rnn, w_o_att=v_w_o_att, w_out=v_w_out, norm_ple=v_norm_ple,
             w_ple_gate=v_w_ple_gate, b_ple_gate=v_b_ple_gate, w_ple=v_w_ple)
    names = list(w.keys())

    w_shard, conv_shard = w_in[0].T.astype(BF16), conv_w[0]
    pos = _my_pos()
    me, my_core, my_chip = _lin(pos), pos[2], _chip(pos)
    hbm_empty = lambda shp, dt: lax.empty(shp, dt)
    shp = w_shard.shape
    entry_token, bufs, sems = _gather_start(
        [w_shard, hbm_empty((2,) + shp, BF16), hbm_empty((2, 2) + shp, BF16), conv_shard,
         hbm_empty((NDEV,) + conv_shard.shape, F32)],
        [("own", (0, 1)), ("near", (0, 2)), ("others", (3, 4))], norm_mix, "gather_start_near")
    w_src, own_l, near_l, conv_src, conv_l = bufs
    sem_own, sem_near, sem_conv = sems
    gather_out = {}

    def project(hn):
        w_thru, own = _gather_wait("own", *sem_own, [w_src, own_l], hn, "gather_wait_own")
        own = lax.dynamic_update_slice(own, w_shard[None], (my_core, 0, 0)).reshape(1, CHIP_COLS, D)
        chips = [jnp.stack([my_chip]), jnp.stack([my_chip ^ 1, my_chip ^ 2]), jnp.stack([my_chip ^ 3])]
        chips = [c.astype(jnp.int32) for c in chips]
        proj = _in_proj_chips(hn, own, b_in, chips[0], None, entry_token, "in_proj_own")
        proj, raw = lax.optimization_barrier((proj, (w_o_rnn[0], w_o_att[0], w_out[0], w_ple_gate[0], w_ple[0])))
        srcs = [raw[0].astype(BF16), raw[1].T.astype(BF16), raw[2].astype(BF16), raw[3].astype(BF16),
                raw[4].T.astype(BF16)]
        w_thru, near = _gather_wait("near", *sem_near, [w_thru, near_l], [proj] + srcs, "gather_wait_near")
        near = _forward_to_sibling(near, "gather_forward_near")
        token, (w_thru, far_l), (sem_far,) = _gather_start(
            [w_thru, hbm_empty((2,) + shp, BF16)], [("far", (0, 1))], near, "gather_start_far")
        near = near.reshape(2, CHIP_COLS, D)
        proj = _in_proj_chips(hn, near, b_in, chips[1], proj, token, "in_proj_near")
        w_thru, far = _gather_wait("far", *sem_far, [w_thru, far_l], proj, "gather_wait_far")
        far = _forward_to_sibling(far[None], "gather_forward_far").reshape(1, CHIP_COLS, D)
        proj = _in_proj_chips(hn, far, b_in, chips[2], proj, token, "in_proj_far")
        conv_thru, conv_g = _gather_wait("others", *sem_conv, [conv_src, conv_l], proj, "gather_wait_conv")
        conv_g = lax.dynamic_update_slice(conv_g, conv_shard[None], (me, 0, 0))
        conv_f = conv_g.transpose(1, 0, 2).reshape(CONVW, DR)
        token, obufs, (sem_out,) = _gather_start(
            srcs + [hbm_empty((NDEV,) + a.shape, BF16) for a in srcs], [("others", tuple(range(10)))], proj,
            "gather_start_out")
        gather_out.update(bufs=obufs, sems=sem_out, shards=srcs)
        return proj, [own, near, far], jnp.concatenate(chips), conv_f, token

    def other_weights(after):
        obufs = _gather_wait("others", *gather_out["sems"], gather_out["bufs"], after, "gather_wait_out")
        full = [lax.dynamic_update_slice(a, s[None], (me, 0, 0)) for a, s in zip(obufs[5:], gather_out["shards"])]
        return [a.reshape((NDEV * a.shape[1], a.shape[2])) for a in full]

    def start_reduce(arrs, tag):
        if tag == "out":
            parts = [a.reshape((NDEV, a.shape[0] // NDEV, a.shape[1])) for a in arrs]
            token, bufs, (sems,) = _gather_start(
                parts + [lax.empty(a.shape, a.dtype) for a in parts], [("exchange", tuple(range(2 * len(parts))))],
                arrs[-1][:SUBLANES], "reduce_out_start")
            return token, (bufs, sems)
        parts = [a.reshape((NCHIP, 2, a.shape[0] // NDEV, a.shape[1])) for a in arrs]
        pair_sums = [_exchange_sum_within_chip(a, "reduce_within_chip_" + tag) for a in parts]
        return _between_chips_start(pair_sums, "reduce_between_chips_start_" + tag)

    grad_x, pending_out, pending_in, small = _local_step(
        x.reshape(T, D), p.reshape(T, PLE), loss_target.reshape(T, D),
        project, other_weights,
        norm_mix, conv_b, w_rg_a[0], b_rg_a, w_rg_x[0], b_rg_x, lru_lambda, q_norm[0], k_norm[0],
        norm_ple, b_ple_gate, start_reduce, entry_token)

    rep_parts = _pack_small_grads(small).reshape(NDEV, REP_ROWS_DEV, LANES)
    conv_parts = small["conv_w"].reshape(CONVW, NDEV, DR // NDEV).transpose(1, 0, 2)
    smalls = [rep_parts, conv_parts]
    token, sbufs, (sem_x,) = _gather_start(
        smalls + [lax.empty(a.shape, F32) for a in smalls], [("exchange", (0, 1, 2, 3))], small["norm_mix"],
        "reduce_small_start")

    own_in, recv_in = _between_chips_wait(pending_in, token, "reduce_between_chips_wait_in")
    w_in_res = _sum_chips_adamw(own_in[0], recv_in[0], w_in[0].T, m_w_in[0].T, v_w_in[0].T, token, 304, "adamw_w_in")
    sbufs = _gather_wait("exchange", *sem_x, sbufs, w_in_res[0], "reduce_small_wait")
    g_rep, g_conv = _sum_blocks_small(sbufs[:2], sbufs[2:], me, (False, False), "sum_small")
    token, gbufs, (sem_g,) = _gather_start(
        [g_rep, lax.empty((NDEV,) + g_rep.shape, F32)], [("others", (0, 1))], g_conv, "gather_small_start")
    obufs = _gather_wait("exchange", *pending_out[1], pending_out[0], token, "reduce_out_wait")
    g_o_rnn, g_o_att, g_out, g_pg, g_ple = _sum_blocks_small(
        obufs[:5], obufs[5:], me, (False, True, False, False, True), "sum_out")

    grad, delta, new_m, new_v = {}, {}, {}, {}
    rest = ("w_o_rnn", "w_o_att", "w_out", "w_ple_gate", "w_ple", "conv_w")
    g_rest = [g_o_rnn, g_o_att, g_out, g_pg, g_ple, g_conv]
    rest_res = _adamw_many([w[n][0] for n in rest], g_rest, [m[n][0] for n in rest], [v[n][0] for n in rest], token)
    _, rep_all = _gather_wait("others", *sem_g, gbufs, rest_res[0][0], "gather_small_wait")
    rep_all = lax.dynamic_update_slice(rep_all, g_rep[None], (me, 0, 0)).reshape(NDEV * REP_ROWS_DEV, LANES)
    loss = rep_all[LOSS_ROW, 0]
    rep_shape = lambda a: a if a.ndim == 2 else a.reshape(a.shape[1:])
    res = _adamw_small(rep_all, [rep_shape(w[n]) for n in REP_NAMES], [rep_shape(m[n]) for n in REP_NAMES],
                       [rep_shape(v[n]) for n in REP_NAMES])
    for dst, vals in zip((grad, delta, new_m, new_v), res):
        for n, a in zip(REP_NAMES, vals):
            dst[n] = a.reshape(w[n].shape)
    grad["w_in"], delta["w_in"], new_m["w_in"], new_v["w_in"] = [a.T[None] for a in w_in_res]
    for n, a in zip(rest, g_rest):
        grad[n] = a[None]
    for dst, vals in zip((delta, new_m, new_v), rest_res):
        for n, a in zip(rest, vals):
            dst[n] = a[None]

    return (loss, grad_x.reshape(BL, S, D), *[grad[n] for n in names], *[delta[n] for n in names],
            *[new_m[n] for n in names], *[new_v[n] for n in names])
```

```python
import jax
import jax.numpy as jnp
import numpy as np
from jax import lax
from jax.experimental import pallas as pl
from jax.experimental.pallas import tpu as pltpu

F32 = jnp.float32
BF16 = jnp.bfloat16

D = 1024
S = 2048
BL = 2
T = BL * S
NDEV = 8
NCHIP = 4
PLE = 256
DR = 1280
NRB = 10
RBW = 128
CONVW = 4
LRU_C = 8.0
HD = 128
NH = 4
PATTERNS = ((128, 1), (512, 4), (2048, 16))
NG = 3
ATT = NH * HD
GW = NG * ATT
NIN = 2 * DR + 3 * GW + ATT + 2 * D
OFF_ZR = DR
OFF_Q = 2 * DR
OFF_ZA = OFF_Q + 3 * GW
OFF_G = OFF_ZA + ATT
ROPE_THETA = 10000.0
EPS = 1e-6
SCALE = HD ** -0.5
NEG = -1e30
QB = 128
LANES = 128
CT = 512
NCT = NIN // CT
A_W = 2 * DR
C_W = ATT + 2 * D

LR, B1, B2, AEPS, WD, STEP = 0.001, 0.9, 0.999, 1e-08, 0.01, 10

NSHARD_IN = NIN // NDEV
REP_NAMES = ("w_rg_a", "w_rg_x", "norm_mix", "b_in", "conv_b", "b_rg_a", "b_rg_x", "lru_lambda", "q_norm",
             "k_norm", "norm_ple", "b_ple_gate")
REP_ROWS = (NRB * RBW, NRB * RBW, D // LANES, NIN // LANES, DR // LANES, DR // LANES, DR // LANES, DR // LANES,
            NG, NG, D // LANES, D // LANES)
REP_TOTAL_ROWS = sum(REP_ROWS)
REP_ROWS_DEV = 344
BIG_NAMES = ("w_in", "w_o_rnn", "w_o_att", "w_out", "w_ple_gate", "w_ple")

VMEM_BIG = 56 * 1024 * 1024
VMEM_MID = 40 * 1024 * 1024


def _cp(sem=None, vmem=None):
    return pltpu.CompilerParams(dimension_semantics=sem, vmem_limit_bytes=vmem)


def _hbm(*arrays):
    return [pltpu.with_memory_space_constraint(a, pltpu.HBM) for a in arrays]


def _copy_together(copies):
    for cp in copies:
        cp.start()
    for cp in copies:
        cp.wait()


def _dot(a, b):
    return jnp.dot(a, b, preferred_element_type=F32)


def _dot_nt(a, b):
    return lax.dot_general(a, b, (((1,), (1,)), ((), ())), preferred_element_type=F32)


def _dot_tn(a, b):
    return lax.dot_general(a, b, (((0,), (0,)), ((), ())), preferred_element_type=F32)


def _sigmoid(x):
    return jax.nn.sigmoid(x)


def _perm(j):
    jq = j - OFF_Q // CT
    inside = (j >= OFF_Q // CT) & (j < OFF_ZA // CT)
    return jnp.where(inside, OFF_Q // CT + (jq % 3) * 3 + jq // 3, j)


PIECES = ((0, A_W // CT), (OFF_Q // CT, GW // CT), (OFF_Q // CT + 3, GW // CT), (OFF_Q // CT + 6, GW // CT),
          (OFF_ZA // CT, C_W // CT))


def _rmsnorm_fwd(x, gain, token, tm=512):
    def body(x_ref, g_ref, _token, o_ref):
        xv = x_ref[...]
        var = jnp.mean(xv * xv, axis=-1, keepdims=True)
        o_ref[...] = (xv * lax.rsqrt(var + EPS) * g_ref[...]).astype(BF16)

    return pl.pallas_call(
        body, grid=(T // tm,), name="rmsnorm_fwd",
        in_specs=[pl.BlockSpec((tm, D), lambda i: (i, 0)), pl.BlockSpec((1, D), lambda i: (0, 0)),
                  pl.BlockSpec((8, LANES), lambda i: (0, 0))],
        out_specs=pl.BlockSpec((tm, D), lambda i: (i, 0)),
        out_shape=jax.ShapeDtypeStruct((T, D), BF16),
        compiler_params=_cp(("parallel",)),
    )(*_hbm(x, gain, token))


CHIP_COLS = NIN // NCHIP


def _in_proj_chips(hn, w_rows, bias, chips, proj, token, name, tm=1024):
    n = w_rows.shape[0]

    def body(chips_ref, a_ref, w_ref, b_ref, _token, *rest):
        o_ref = rest[-1]
        o_ref[...] = (_dot_nt(a_ref[...], w_ref[0]) + b_ref[...]).astype(BF16)

    in_specs = [pl.BlockSpec((tm, D), lambda s, i, ch: (i, 0)),
                pl.BlockSpec((1, CHIP_COLS, D), lambda s, i, ch: (s, 0, 0)),
                pl.BlockSpec((1, CHIP_COLS), lambda s, i, ch: (0, ch[s])),
                pl.BlockSpec((8, LANES), lambda s, i, ch: (0, 0))]
    args = [hn, w_rows, bias, token]
    aliases = {}
    if proj is not None:
        in_specs.append(pl.BlockSpec(memory_space=pl.ANY))
        args.append(proj)
        aliases = {5: 0}
    return pl.pallas_call(
        body, name=name,
        grid_spec=pltpu.PrefetchScalarGridSpec(
            num_scalar_prefetch=1, grid=(n, T // tm), in_specs=in_specs,
            out_specs=pl.BlockSpec((tm, CHIP_COLS), lambda s, i, ch: (i, ch[s]))),
        out_shape=jax.ShapeDtypeStruct((T, NIN), BF16),
        input_output_aliases=aliases,
        compiler_params=_cp(("arbitrary", "arbitrary"), VMEM_BIG),
    )(chips, *_hbm(*args))


def _grad_x(pieces, w_bufs, chips, token, x, dx1, gain, tm=512):
    nb = len(w_bufs)

    def body(chips_ref, a_ref, q_ref, k_ref, v_ref, c_ref, *rest):
        w_hbm = rest[:nb]
        x_ref, dx1_ref, g_ref, dx_ref, dg_ref, w, sems = rest[nb + 1:]
        first = pl.program_id(0) == 0

        @pl.when(first)
        def _():
            s = 0
            copies = []
            for buf in w_hbm:
                for r in range(buf.shape[0]):
                    row = pl.multiple_of(chips_ref[s] * CHIP_COLS, 128)
                    copies.append(pltpu.make_async_copy(buf.at[r], w.at[pl.ds(row, CHIP_COLS), :], sems.at[s]))
                    s += 1
            _copy_together(copies)

        acc = _dot(a_ref[...], w[pl.ds(0, A_W), :])
        for kind, p_ref in enumerate((q_ref, k_ref, v_ref)):
            for g in range(NG):
                row = OFF_Q + (3 * g + kind) * CT
                acc = acc + _dot(p_ref[:, g * CT:(g + 1) * CT], w[pl.ds(row, CT), :])
        dn = acc + _dot(c_ref[...], w[pl.ds(OFF_ZA, C_W), :])
        xv = x_ref[...]
        rstd = lax.rsqrt(jnp.mean(xv * xv, axis=-1, keepdims=True) + EPS)
        xh = xv * rstd
        dg = jnp.sum(dn * xh, axis=0, keepdims=True)
        gd = dn * g_ref[...]
        dx_ref[...] = dx1_ref[...] + rstd * (gd - xh * jnp.mean(gd * xh, axis=-1, keepdims=True))

        @pl.when(first)
        def _():
            dg_ref[...] = dg

        @pl.when(jnp.logical_not(first))
        def _():
            dg_ref[...] += dg

    tok = lambda wd: pl.BlockSpec((tm, wd), lambda i, ch: (i, 0))
    vec = lambda: pl.BlockSpec((1, D), lambda i, ch: (0, 0))
    return pl.pallas_call(
        body, name="grad_x",
        grid_spec=pltpu.PrefetchScalarGridSpec(
            num_scalar_prefetch=1, grid=(T // tm,),
            in_specs=[tok(A_W), tok(GW), tok(GW), tok(GW), tok(C_W)] + [pl.BlockSpec(memory_space=pl.ANY)] * nb
            + [pl.BlockSpec((8, LANES), lambda i, ch: (0, 0)), tok(D), tok(D), vec()],
            out_specs=[tok(D), vec()],
            scratch_shapes=[pltpu.VMEM((NIN, D), BF16), pltpu.SemaphoreType.DMA((NCHIP,))]),
        out_shape=[jax.ShapeDtypeStruct((T, D), F32), jax.ShapeDtypeStruct((1, D), F32)],
        compiler_params=_cp(("arbitrary",), VMEM_BIG),
    )(chips, *_hbm(*pieces, *w_bufs, token, x, dx1, gain))


def _dw_in(pieces, hn):
    def body(a_ref, q_ref, k_ref, v_ref, c_ref, h_hbm, o_ref, s_ref, h):
        j = pl.program_id(0)

        @pl.when(j == 0)
        def _():
            pltpu.sync_copy(h_hbm, h)

        def step(x_ref):
            xv = x_ref[...]
            o_ref[...] = _dot_tn(xv, h[...]).astype(BF16)
            s_ref[...] = jnp.sum(xv.astype(F32), axis=0, keepdims=True)

        for x_ref, (lo, n) in zip((a_ref, q_ref, k_ref, v_ref, c_ref), PIECES):
            pl.when((j >= lo) & (j < lo + n))(lambda x_ref=x_ref: step(x_ref))

    def piece_spec(lo, n):
        return pl.BlockSpec((T, CT), lambda j: (0, jnp.clip(j - lo, 0, n - 1)))

    return pl.pallas_call(
        body, grid=(NCT,), name="dw_in",
        in_specs=[piece_spec(lo, n) for lo, n in PIECES] + [pl.BlockSpec(memory_space=pl.ANY)],
        out_specs=[pl.BlockSpec((CT, D), lambda j: (_perm(j), 0)), pl.BlockSpec((1, CT), lambda j: (0, _perm(j)))],
        out_shape=[jax.ShapeDtypeStruct((NIN, D), BF16), jax.ShapeDtypeStruct((1, NIN), F32)],
        scratch_shapes=[pltpu.VMEM((T, D), BF16)],
        compiler_params=_cp(("arbitrary",), VMEM_BIG),
    )(*_hbm(*pieces, hn))


def _mm_tn(a, b, ta, tt, name):
    m = a.shape[1]
    n = b.shape[1]
    nt = T // tt

    def body(a_ref, b_ref, o_ref, acc):
        t = pl.program_id(1)
        p = _dot_tn(a_ref[...].astype(BF16), b_ref[...].astype(BF16))
        if nt == 1:
            o_ref[...] = p.astype(BF16)
            return

        @pl.when(t == 0)
        def _():
            acc[...] = p

        @pl.when(t > 0)
        def _():
            acc[...] += p

        @pl.when(t == nt - 1)
        def _():
            o_ref[...] = acc[...].astype(BF16)

    return pl.pallas_call(
        body, grid=(m // ta, nt), name=name,
        in_specs=[pl.BlockSpec((tt, ta), lambda j, t: (t, j)), pl.BlockSpec((tt, n), lambda j, t: (t, 0))],
        out_specs=pl.BlockSpec((ta, n), lambda j, t: (j, 0)),
        out_shape=pltpu.HBM((m, n), BF16),
        scratch_shapes=[pltpu.VMEM((ta, n), F32)],
        compiler_params=_cp(("parallel", "arbitrary"), VMEM_MID),
    )(*_hbm(a, b))


def _row_iota():
    return lax.broadcasted_iota(jnp.int32, (S, RBW), 0)


SUBLANES = 8
N_SHIFT_BUFS = 4


class _Shifter:
    def __init__(self, bufs):
        self.bufs = bufs
        self.k = 0

    def _store(self, v, fill, front):
        b = self.bufs.at[self.k % N_SHIFT_BUFS]
        self.k += 1
        b[pl.ds(0 if front else SUBLANES + S, SUBLANES), :] = jnp.full((SUBLANES, RBW), fill, F32)
        b[pl.ds(SUBLANES, S), :] = v
        return b

    def down(self, v, ds, fill):
        b = self._store(v, fill, True)
        return [b[pl.ds(SUBLANES - d, S), :] for d in ds]

    def up(self, v, ds, fill):
        b = self._store(v, fill, False)
        return [b[pl.ds(SUBLANES + d, S), :] for d in ds]


def _shift_down(v, d, sh, fill):
    return sh.down(v, (d,), fill)[0]


def _shift_up(v, d, sh, fill):
    return sh.up(v, (d,), fill)[0]


def _scan_down(a, u, row):
    d = 1
    while d < S:
        last = 2 * d >= S
        if d < SUBLANES:
            u = a * _shift_down(u, d, row, 0.0) + u
            if not last:
                a = a * _shift_down(a, d, row, 1.0)
        else:
            u = jnp.concatenate([u[:d], a[d:] * u[:S - d] + u[d:]], axis=0)
            if not last:
                a = jnp.concatenate([a[:d], a[d:] * a[:S - d]], axis=0)
        d *= 2
    return u


def _scan_up(b, g, row):
    d = 1
    while d < S:
        last = 2 * d >= S
        if d < SUBLANES:
            g = g + b * _shift_up(g, d, row, 0.0)
            if not last:
                b = b * _shift_up(b, d, row, 0.0)
        else:
            g = jnp.concatenate([g[:S - d] + b[:S - d] * g[d:], g[S - d:]], axis=0)
            if not last:
                b = jnp.concatenate([b[:S - d] * b[d:], b[S - d:]], axis=0)
        d *= 2
    return g


def _softplus(x):
    return jnp.maximum(x, 0.0) + jnp.log1p(jnp.exp(-jnp.abs(x)))


N_SAVED = 5


def _rnn_gates(x, cw, cb, wa, ba, wx, bx, lam, row, pad, saved=None):
    xs = pad.down(x, (1, 2, 3), 0.0)
    if saved is not None:
        r, i, a, mult, xc = saved
        return xc, xc.astype(BF16), r, i, _softplus(-lam), a, mult, xs
    xc = cb + cw[3:4, :] * x
    for j in (1, 2, 3):
        xc = xc + cw[3 - j:4 - j, :] * xs[j - 1]
    xcb = xc.astype(BF16)
    r = _sigmoid(_dot(xcb, wa) + ba)
    i = _sigmoid(_dot(xcb, wx) + bx)
    sp = _softplus(-lam)
    log_a = (-LRU_C) * r * sp
    a = jnp.exp(log_a)
    mult = jnp.where(row == 0, 1.0, jnp.sqrt(jnp.tanh(-log_a) * (1.0 + a * a)))
    return xc, xcb, r, i, sp, a, mult, xs


def _rnn_fwd(proj3, conv_w, conv_b, wa, ba, wx, bx, lam, token):
    def body(x_ref, cw_ref, cb_ref, wa_ref, ba_ref, wx_ref, bx_ref, lam_ref, _token, h_ref, g_ref, pad):
        row = _row_iota()
        sh = _Shifter(pad)
        x = x_ref[0].astype(F32)
        xc, _, r, i, _, a, mult, _ = _rnn_gates(x, cw_ref[...], cb_ref[...], wa_ref[0], ba_ref[...],
                                             wx_ref[0], bx_ref[...], lam_ref[...], row, sh)
        for k, val in enumerate((r, i, a, mult, xc)):
            g_ref[k, 0] = val
        h_ref[0] = _scan_down(a, mult * (i * xc), sh)

    vec = lambda: pl.BlockSpec((1, RBW), lambda b, n: (0, n))
    mat = lambda: pl.BlockSpec((1, RBW, RBW), lambda b, n: (n, 0, 0))
    return pl.pallas_call(
        body, grid=(BL, NRB), name="rnn_fwd",
        in_specs=[pl.BlockSpec((1, S, RBW), lambda b, n: (b, 0, n)),
                  pl.BlockSpec((CONVW, RBW), lambda b, n: (0, n)),
                  vec(), mat(), vec(), mat(), vec(), vec(), pl.BlockSpec((8, LANES), lambda b, n: (0, 0))],
        out_specs=[pl.BlockSpec((1, S, RBW), lambda b, n: (b, 0, n)),
                   pl.BlockSpec((N_SAVED, 1, S, RBW), lambda b, n: (0, b, 0, n))],
        out_shape=[jax.ShapeDtypeStruct((BL, S, DR), F32), jax.ShapeDtypeStruct((N_SAVED, BL, S, DR), F32)],
        scratch_shapes=[pltpu.VMEM((N_SHIFT_BUFS, S + 2 * SUBLANES, RBW), F32)],
        compiler_params=_cp(("parallel", "parallel"), VMEM_MID),
    )(*_hbm(proj3, conv_w, conv_b, wa, ba, wx, bx, lam, token))


def _rnn_bwd(proj3, h3, dh3, gates, slab_a3, conv_w, conv_b, wa, ba, wx, bx, lam, token):
    def body(x_ref, h_ref, dh_ref, g_ref, cw_ref, cb_ref, wa_ref, ba_ref, wx_ref, bx_ref, lam_ref, _alias, _token,
             dx_ref, dcw_ref, dcb_ref, dwa_ref, dba_ref, dwx_ref, dbx_ref, dlam_ref, pad):
        row = _row_iota()
        sh = _Shifter(pad)
        x = x_ref[0].astype(F32)
        cw = cw_ref[...]
        wa_v = wa_ref[0]
        wx_v = wx_ref[0]
        lam_v = lam_ref[...]
        xc, xcb, r, i, sp, a, mult, xs = _rnn_gates(x, cw, cb_ref[...], wa_v, ba_ref[...], wx_v, bx_ref[...], lam_v,
                                                    row, sh, [g_ref[k, 0] for k in range(N_SAVED)])
        h = h_ref[0]
        g = _scan_up(_shift_up(a, 1, sh, 0.0), dh_ref[0], sh)
        da = g * _shift_down(h, 1, sh, 0.0)
        dmult = jnp.where(row == 0, 0.0, g * (i * xc))
        gm = g * mult
        di = gm * xc
        dxc = gm * i
        dlog_a = da * a - dmult * (a * a) / mult
        dr = dlog_a * ((-LRU_C) * sp)
        dsp = jnp.sum(dlog_a * ((-LRU_C) * r), axis=0, keepdims=True)
        dlam = dsp * (-_sigmoid(-lam_v))
        dpa = dr * r * (1.0 - r)
        dpx = di * i * (1.0 - i)
        dpab = dpa.astype(BF16)
        dpxb = dpx.astype(BF16)
        dwa = _dot_tn(xcb, dpab)
        dwx = _dot_tn(xcb, dpxb)
        dba = jnp.sum(dpa, axis=0, keepdims=True)
        dbx = jnp.sum(dpx, axis=0, keepdims=True)
        dxc = dxc + _dot_nt(dpab, wa_v) + _dot_nt(dpxb, wx_v)
        dcb = jnp.sum(dxc, axis=0, keepdims=True)
        dx = cw[3:4, :] * dxc
        dcw_rows = [None] * CONVW
        dcw_rows[3] = jnp.sum(dxc * x, axis=0, keepdims=True)
        dxc_up = sh.up(dxc, (1, 2, 3), 0.0)
        for j in (1, 2, 3):
            dx = dx + cw[3 - j:4 - j, :] * dxc_up[j - 1]
            dcw_rows[3 - j] = jnp.sum(dxc * xs[j - 1], axis=0, keepdims=True)
        dx_ref[0] = dx.astype(BF16)
        dcw = jnp.concatenate(dcw_rows, axis=0)
        first = pl.program_id(1) == 0

        @pl.when(first)
        def _():
            dcw_ref[...] = dcw
            dcb_ref[...] = dcb
            dwa_ref[0] = dwa
            dba_ref[...] = dba
            dwx_ref[0] = dwx
            dbx_ref[...] = dbx
            dlam_ref[...] = dlam

        @pl.when(jnp.logical_not(first))
        def _():
            dcw_ref[...] += dcw
            dcb_ref[...] += dcb
            dwa_ref[0] += dwa
            dba_ref[...] += dba
            dwx_ref[0] += dwx
            dbx_ref[...] += dbx
            dlam_ref[...] += dlam

    slab = lambda: pl.BlockSpec((1, S, RBW), lambda n, b: (b, 0, n))
    vec = lambda: pl.BlockSpec((1, RBW), lambda n, b: (0, n))
    mat = lambda: pl.BlockSpec((1, RBW, RBW), lambda n, b: (n, 0, 0))
    taps = lambda: pl.BlockSpec((CONVW, RBW), lambda n, b: (0, n))
    vshape = pltpu.HBM((1, DR), F32)
    mshape = pltpu.HBM((NRB, RBW, RBW), F32)
    return pl.pallas_call(
        body, grid=(NRB, BL), name="rnn_bwd",
        in_specs=[slab(), slab(), slab(), pl.BlockSpec((N_SAVED, 1, S, RBW), lambda n, b: (0, b, 0, n)),
                  taps(), vec(), mat(), vec(), mat(), vec(), vec(),
                  pl.BlockSpec(memory_space=pl.ANY), pl.BlockSpec((8, LANES), lambda n, b: (0, 0))],
        out_specs=[slab(), taps(), vec(), mat(), vec(), mat(), vec(), vec()],
        out_shape=[jax.ShapeDtypeStruct((BL, S, A_W), BF16), pltpu.HBM((CONVW, DR), F32),
                   vshape, mshape, vshape, mshape, vshape, vshape],
        input_output_aliases={11: 0},
        scratch_shapes=[pltpu.VMEM((N_SHIFT_BUFS, S + 2 * SUBLANES, RBW), F32)],
        compiler_params=_cp(("parallel", "arbitrary"), 48 * 1024 * 1024),
    )(*_hbm(proj3, h3, dh3, gates, conv_w, conv_b, wa, ba, wx, bx, lam, slab_a3, token))


NQB = S // QB


def _rms_head(t, gain):
    rstd = lax.rsqrt(jnp.mean(t * t, axis=-1, keepdims=True) + EPS)
    return t * rstd * gain


def _rope(t, cs, sn):
    return t * cs + pltpu.roll(t, HD // 2, 1) * sn


def _rope_t(dy, cs, sn):
    return dy * cs - pltpu.roll(dy, HD // 2, 1) * sn


def _bdot_nt(a, b):
    return lax.dot_general(a, b, (((2,), (2,)), ((0,), (0,))), preferred_element_type=F32)


def _bdot(a, b):
    return lax.dot_general(a, b, (((2,), (1,)), ((0,), (0,))), preferred_element_type=F32)


def _bdot_tn(a, b):
    return lax.dot_general(a, b, (((1,), (1,)), ((0,), (0,))), preferred_element_type=F32)


STRIDE_MAX = 4


def _permute(buf, x, dil, dst, off=0):
    ln = S // dil
    if dil == 1:
        dst[pl.ds(off, S), :] = x.astype(dst.dtype)
        return
    buf[0] = x
    if dil <= STRIDE_MAX:
        for c in range(dil):
            dst[pl.ds(off + c * ln, ln), :] = buf.at[0][pl.ds(c, ln, stride=dil), :].astype(dst.dtype)
        return
    f, r = STRIDE_MAX, dil // STRIDE_MAX
    part = S // f
    for c1 in range(f):
        buf.at[1][pl.ds(c1 * part, part), :] = buf.at[0][pl.ds(c1, part, stride=f), :]
    for c1 in range(f):
        for c2 in range(r):
            dst[pl.ds(off + (c1 + f * c2) * ln, ln), :] = (
                buf.at[1][pl.ds(c1 * part + c2, ln, stride=r), :].astype(dst.dtype))


def _unpermute(buf, xp, dil, dst):
    ln = S // dil
    if dil == 1:
        dst[...] = xp
        return
    if dil <= STRIDE_MAX:
        for c in range(dil):
            dst[pl.ds(c, ln, stride=dil), :] = xp[c * ln:(c + 1) * ln]
        return
    f, r = STRIDE_MAX, dil // STRIDE_MAX
    part = S // f
    for c1 in range(f):
        for c2 in range(r):
            c = c1 + f * c2
            buf.at[1][pl.ds(c1 * part + c2, ln, stride=r), :] = xp[c * ln:(c + 1) * ln]
    for c1 in range(f):
        dst[pl.ds(c1, part, stride=f), :] = buf[1, pl.ds(c1 * part, part), :]


def _blocks3(ref, off=0):
    return ref[pl.ds(off, S), :].reshape(NQB, QB, HD)


def _att_prep(q_ref, k_ref, v_ref, cos_ref, sin_ref, qn, kn, dil, nat, qs, ksp, vsp):
    cs = cos_ref[...]
    sn = sin_ref[...]
    zero = jnp.zeros((QB, HD), BF16)
    ksp[pl.ds(0, QB), :] = zero
    vsp[pl.ds(0, QB), :] = zero
    _permute(nat, _rope(_rms_head(q_ref[0].astype(F32), qn), cs, sn), dil, qs)
    _permute(nat, _rope(_rms_head(k_ref[0].astype(F32), kn), cs, sn), dil, ksp, QB)
    _permute(nat, v_ref[0].astype(F32), dil, vsp, QB)


def _att_scores(qs, ksp, dil):
    nb = S // dil // QB
    q3 = _blocks3(qs)
    shape = (NQB, QB, QB)
    qi = lax.broadcasted_iota(jnp.int32, shape, 1)
    kj = lax.broadcasted_iota(jnp.int32, shape, 2)
    s_c = jnp.where(qi >= kj, _bdot_nt(q3, _blocks3(ksp, QB)) * SCALE, NEG)
    if nb == 1:
        return q3, s_c, None
    jj = lax.broadcasted_iota(jnp.int32, shape, 0)
    ok = (kj >= qi) & ((jj & (nb - 1)) != 0)
    s_p = jnp.where(ok, _bdot_nt(q3, _blocks3(ksp)) * SCALE, NEG)
    return q3, s_c, s_p


def _qkv_spec(kind, g):
    base = OFF_Q // HD + (3 * g + kind) * NH
    return pl.BlockSpec((1, S, HD), lambda b, h: (b, 0, base + h))


def _attn_fwd(proj3, cos_t, sin_t, q_norm, k_norm):
    def body(*refs):
        qkv_refs = refs[:9]
        (cos_ref, sin_ref, qn_ref, kn_ref, att_ref, lse_ref, w_ref, qp_ref, kp_ref, vp_ref,
         nat, qs, ksp, vsp, og) = refs[9:]
        for g, (window, dil) in enumerate(PATTERNS):
            q_ref, k_ref, v_ref = qkv_refs[3 * g:3 * g + 3]
            _att_prep(q_ref, k_ref, v_ref, cos_ref, sin_ref, qn_ref[g:g + 1, :], kn_ref[g:g + 1, :], dil,
                      nat, qs, ksp, vsp)
            qp_ref[g, 0] = qs[...]
            kp_ref[g, 0] = ksp[pl.ds(QB, S), :]
            vp_ref[g, 0] = vsp[pl.ds(QB, S), :]
            _, s_c, s_p = _att_scores(qs, ksp, dil)
            m = jnp.max(s_c, axis=-1, keepdims=True)
            if s_p is not None:
                m = jnp.maximum(m, jnp.max(s_p, axis=-1, keepdims=True))
            e_c = jnp.exp(s_c - m)
            den = jnp.sum(e_c, axis=-1, keepdims=True)
            o = _bdot(e_c.astype(BF16), _blocks3(vsp, QB))
            if s_p is not None:
                e_p = jnp.exp(s_p - m)
                den = den + jnp.sum(e_p, axis=-1, keepdims=True)
                o = o + _bdot(e_p.astype(BF16), _blocks3(vsp))
            _unpermute(nat, (o / den).reshape(S, HD), dil, og.at[g])
            _unpermute(nat, jnp.broadcast_to(m + jnp.log(den), (NQB, QB, HD)).reshape(S, HD), dil,
                       lse_ref.at[g, 0])
        l0 = lse_ref[0, 0]
        l1 = lse_ref[1, 0]
        l2 = lse_ref[2, 0]
        mx = jnp.maximum(jnp.maximum(l0, l1), l2)
        e0 = jnp.exp(l0 - mx)
        e1 = jnp.exp(l1 - mx)
        e2 = jnp.exp(l2 - mx)
        inv = 1.0 / (e0 + e1 + e2)
        w0 = e0 * inv
        w1 = e1 * inv
        w2 = e2 * inv
        w_ref[0, 0] = w0
        w_ref[1, 0] = w1
        w_ref[2, 0] = w2
        att_ref[0] = w0 * og[0] + w1 * og[1] + w2 * og[2]

    in_specs = [_qkv_spec(kind, g) for g in range(NG) for kind in range(3)]
    in_specs += [pl.BlockSpec((S, HD), lambda b, h: (0, 0)), pl.BlockSpec((S, HD), lambda b, h: (0, 0)),
                 pl.BlockSpec((NG, HD), lambda b, h: (0, 0)), pl.BlockSpec((NG, HD), lambda b, h: (0, 0))]
    stat = lambda: pl.BlockSpec((NG, 1, S, HD), lambda b, h: (0, b, 0, h))
    return pl.pallas_call(
        body, grid=(BL, NH), name="attn_fwd",
        in_specs=in_specs,
        out_specs=[pl.BlockSpec((1, S, HD), lambda b, h: (b, 0, h)), stat(), stat(), stat(), stat(), stat()],
        out_shape=[jax.ShapeDtypeStruct((BL, S, ATT), F32),
                   jax.ShapeDtypeStruct((NG, BL, S, ATT), F32),
                   jax.ShapeDtypeStruct((NG, BL, S, ATT), F32)]
        + [jax.ShapeDtypeStruct((NG, BL, S, ATT), BF16)] * 3,
        scratch_shapes=[pltpu.VMEM((2, S, HD), F32), pltpu.VMEM((S, HD), BF16), pltpu.VMEM((S + QB, HD), BF16),
                        pltpu.VMEM((S + QB, HD), BF16), pltpu.VMEM((NG, S, HD), F32)],
        compiler_params=_cp(("parallel", "parallel"), VMEM_BIG),
    )(*_hbm(*([proj3] * 9), cos_t, sin_t, q_norm, k_norm))


def _attn_bwd(proj3, cos_t, sin_t, q_norm, k_norm, lse, wts, qkv_p, datt3, sbar3):
    def norm_rope_bwd(dpost, raw, gain, cs, sn):
        dn = _rope_t(dpost, cs, sn)
        rstd = lax.rsqrt(jnp.mean(raw * raw, axis=-1, keepdims=True) + EPS)
        xh = raw * rstd
        dgain = jnp.sum(dn * xh, axis=0, keepdims=True)
        gd = dn * gain
        draw = rstd * (gd - xh * jnp.mean(gd * xh, axis=-1, keepdims=True))
        return draw, dgain

    def group_body(g, refs, first):
        dil = PATTERNS[g][1]
        (q_ref, k_ref, qp_ref, kp_ref, vp_ref, cos_ref, sin_ref, qn_ref, kn_ref, lse_ref, w_ref, datt_ref,
         sbar_ref, dq_ref, dk_ref, dv_ref, dqn_ref, dkn_ref, nat, ksp, vsp, dos, cvp, lsp, acc) = refs
        qn = qn_ref[g:g + 1, :]
        kn = kn_ref[g:g + 1, :]
        cs = cos_ref[...]
        sn = sin_ref[...]
        qs = qp_ref.at[0, 0]
        zero = jnp.zeros((QB, HD), BF16)
        ksp[pl.ds(0, QB), :] = zero
        vsp[pl.ds(0, QB), :] = zero
        ksp[pl.ds(QB, S), :] = kp_ref[0, 0]
        vsp[pl.ds(QB, S), :] = vp_ref[0, 0]
        wv = w_ref[0, 0]
        _permute(nat, wv * datt_ref[0], dil, dos)
        _permute(nat, wv * sbar_ref[0], dil, cvp)
        _permute(nat, lse_ref[0, 0], dil, lsp)
        q3, s_c, s_p = _att_scores(qs, ksp, dil)
        do3 = _blocks3(dos)
        lse3 = _blocks3(lsp)[:, :, 0:1]
        cv3 = _blocks3(cvp)[:, :, 0:1]
        p_c = jnp.exp(s_c - lse3)
        ds_c = (p_c * (_bdot_nt(do3, _blocks3(vsp, QB)) - cv3)).astype(BF16)
        dq = _bdot(ds_c, _blocks3(ksp, QB))
        acc[0] = _bdot_tn(ds_c, q3).reshape(S, HD)
        acc[1] = _bdot_tn(p_c.astype(BF16), do3).reshape(S, HD)
        if s_p is not None:
            p_p = jnp.exp(s_p - lse3)
            ds_p = (p_p * (_bdot_nt(do3, _blocks3(vsp)) - cv3)).astype(BF16)
            dq = dq + _bdot(ds_p, _blocks3(ksp))
            early = pl.ds(0, S - QB)
            acc[0, early, :] += _bdot_tn(ds_p, q3).reshape(S, HD)[QB:]
            acc[1, early, :] += _bdot_tn(p_p.astype(BF16), do3).reshape(S, HD)[QB:]
        _unpermute(nat, (dq * SCALE).reshape(S, HD), dil, nat.at[0])
        draw, dqn = norm_rope_bwd(nat[0], q_ref[0].astype(F32), qn, cs, sn)
        dq_ref[0] = draw.astype(BF16)
        _unpermute(nat, acc[0] * SCALE, dil, nat.at[0])
        draw, dkn = norm_rope_bwd(nat[0], k_ref[0].astype(F32), kn, cs, sn)
        dk_ref[0] = draw.astype(BF16)
        _unpermute(nat, acc[1], dil, nat.at[0])
        dv_ref[0] = nat[0].astype(BF16)
        dqn8 = jnp.broadcast_to(dqn, (SUBLANES, HD))
        dkn8 = jnp.broadcast_to(dkn, (SUBLANES, HD))

        @pl.when(first)
        def _():
            dqn_ref[0] = dqn8
            dkn_ref[0] = dkn8

        @pl.when(jnp.logical_not(first))
        def _():
            dqn_ref[0] += dqn8
            dkn_ref[0] += dkn8

    def body(*refs):
        group = pl.program_id(0)
        first = (pl.program_id(1) == 0) & (pl.program_id(2) == 0)
        for g in range(NG):
            pl.when(group == g)(lambda g=g: group_body(g, refs, first))

    def raw_spec(kind):
        return pl.BlockSpec((1, S, HD), lambda g, b, h: (b, 0, OFF_Q // HD + (3 * g + kind) * NH + h))

    full = lambda r: pl.BlockSpec((r, HD), lambda g, b, h: (0, 0))
    stat = lambda: pl.BlockSpec((1, 1, S, HD), lambda g, b, h: (g, b, 0, h))
    slab = lambda: pl.BlockSpec((1, S, HD), lambda g, b, h: (b, 0, h))
    out_slab = lambda: pl.BlockSpec((1, S, HD), lambda g, b, h: (b, 0, g * NH + h))
    gains = lambda: pl.BlockSpec((1, SUBLANES, HD), lambda g, b, h: (g, 0, 0))
    big = jax.ShapeDtypeStruct((BL, S, GW), BF16)
    vecs = jax.ShapeDtypeStruct((NG, SUBLANES, HD), F32)
    return pl.pallas_call(
        body, grid=(NG, BL, NH), name="attn_bwd",
        in_specs=[raw_spec(0), raw_spec(1), stat(), stat(), stat(), full(S), full(S), full(NG), full(NG),
                  stat(), stat(), slab(), slab()],
        out_specs=[out_slab(), out_slab(), out_slab(), gains(), gains()],
        out_shape=[big, big, big, vecs, vecs],
        scratch_shapes=[pltpu.VMEM((2, S, HD), F32), pltpu.VMEM((S + QB, HD), BF16),
                        pltpu.VMEM((S + QB, HD), BF16), pltpu.VMEM((S, HD), BF16), pltpu.VMEM((S, HD), F32),
                        pltpu.VMEM((S, HD), F32), pltpu.VMEM((2, S, HD), F32)],
        compiler_params=_cp(("arbitrary", "arbitrary", "arbitrary"), VMEM_BIG),
    )(*_hbm(proj3, proj3, *qkv_p, cos_t, sin_t, q_norm, k_norm, lse, wts, datt3, sbar3))


def _tail(x, proj, h, att, p, tgt, w_o_rnn, w_o_att_t, w_out, w_pg, w_ple_t, norm_ple, b_pg, tm=256):
    nt = T // tm
    inv_d = 1.0 / D

    def body(x_ref, h_ref, zr_ref, att_ref, za_ref, g0a_ref, g0b_ref, g1a_ref, g1b_ref, p_ref, tgt_ref,
             np_ref, bpg_ref, wor_hbm, woa_hbm, wout_hbm, wpg_hbm, wple_hbm,
             dx1_ref, merged_ref, n1_ref, dpre_ref, dpe_ref, dyr_ref, dya_ref, slab_a_ref, slab_c_ref, dh_ref,
             datt_ref, sbar_ref, yrnn_ref, yatt_ref, loss_ref, dnp_ref, dbpg_ref,
             wor, woa, wout, wpg, wple, sems):
        first = pl.program_id(0) == 0

        @pl.when(first)
        def _():
            pairs = ((wor_hbm, wor), (woa_hbm, woa), (wout_hbm, wout), (wpg_hbm, wpg), (wple_hbm, wple))
            _copy_together([pltpu.make_async_copy(src, dst, sems.at[k]) for k, (src, dst) in enumerate(pairs)])

        xv = x_ref[...]
        hv = h_ref[...]
        zr = zr_ref[...].astype(F32)
        av = att_ref[...]
        za = za_ref[...].astype(F32)
        szr = _sigmoid(zr)
        silu_r = zr * szr
        yrnn_b = (hv * silu_r).astype(BF16)
        sza = _sigmoid(za)
        silu_a = za * sza
        yatt_b = (av * silu_a).astype(BF16)
        yrnn_ref[...] = yrnn_b
        yatt_ref[...] = yatt_b
        yr = _dot(yrnn_b, wor[...])
        ya = _dot_nt(yatt_b, woa[...])
        g0 = _sigmoid(jnp.concatenate([g0a_ref[...], g0b_ref[...]], axis=1).astype(F32))
        g1 = _sigmoid(jnp.concatenate([g1a_ref[...], g1b_ref[...]], axis=1).astype(F32))
        merged_b = (g0 * yr + g1 * ya).astype(BF16)
        merged_ref[...] = merged_b
        x1 = xv + _dot(merged_b, wout[...])
        rstd = lax.rsqrt(jnp.mean(x1 * x1, axis=-1, keepdims=True) + EPS)
        xh = x1 * rstd
        npl = np_ref[...]
        n1_b = (xh * npl).astype(BF16)
        n1_ref[...] = n1_b
        pg = _sigmoid(_dot(n1_b, wpg[...]) + bpg_ref[...])
        pe = _dot_nt(p_ref[...].astype(BF16), wple[...])
        err = x1 + pg * pe - tgt_ref[...]
        loss_t = 0.5 * inv_d * jnp.sum(err * err)
        dy = err * inv_d
        dpe_ref[...] = (dy * pg).astype(BF16)
        dpre = dy * pe * pg * (1.0 - pg)
        dpre_b = dpre.astype(BF16)
        dpre_ref[...] = dpre_b
        dn1 = _dot_nt(dpre_b, wpg[...])
        dnp = jnp.sum(dn1 * xh, axis=0, keepdims=True)
        dbpg = jnp.sum(dpre, axis=0, keepdims=True)
        gd = dn1 * npl
        dx1 = dy + rstd * (gd - xh * jnp.mean(gd * xh, axis=-1, keepdims=True))
        dx1_ref[...] = dx1
        dmerged = _dot_nt(dx1.astype(BF16), wout[...])
        dyr_b = (dmerged * g0).astype(BF16)
        dya_b = (dmerged * g1).astype(BF16)
        dyr_ref[...] = dyr_b
        dya_ref[...] = dya_b
        slab_c_ref[:, ATT:ATT + D] = (dmerged * yr * g0 * (1.0 - g0)).astype(BF16)
        slab_c_ref[:, ATT + D:ATT + 2 * D] = (dmerged * ya * g1 * (1.0 - g1)).astype(BF16)
        dyrnn = _dot_nt(dyr_b, wor[...])
        dyatt = _dot(dya_b, woa[...])
        dh_ref[...] = dyrnn * silu_r
        slab_a_ref[...] = (dyrnn * hv * szr * (1.0 + zr * (1.0 - szr))).astype(BF16)
        datt = dyatt * silu_a
        datt_ref[...] = datt
        slab_c_ref[:, 0:ATT] = (dyatt * av * sza * (1.0 + za * (1.0 - sza))).astype(BF16)
        da = datt * av
        for hh in range(NH):
            seg = slice(hh * HD, (hh + 1) * HD)
            sbar_ref[:, seg] = jnp.broadcast_to(jnp.sum(da[:, seg], axis=-1, keepdims=True), (tm, HD))

        @pl.when(first)
        def _():
            loss_ref[...] = jnp.full((8, LANES), loss_t, F32)
            dnp_ref[...] = dnp
            dbpg_ref[...] = dbpg

        @pl.when(jnp.logical_not(first))
        def _():
            loss_ref[...] += jnp.full((8, LANES), loss_t, F32)
            dnp_ref[...] += dnp
            dbpg_ref[...] += dbpg

    tok = lambda w: pl.BlockSpec((tm, w), lambda i: (i, 0))
    col = lambda w, blk: pl.BlockSpec((tm, w), lambda i: (i, blk))
    vec = lambda: pl.BlockSpec((1, D), lambda i: (0, 0))
    hbm = lambda: pl.BlockSpec(memory_space=pl.ANY)
    gb = OFF_G // 512
    in_specs = [tok(D), tok(DR), col(DR, 1), tok(ATT), col(ATT, OFF_ZA // ATT),
                col(512, gb), col(512, gb + 1), col(512, gb + 2), col(512, gb + 3),
                tok(PLE), tok(D), vec(), vec(), hbm(), hbm(), hbm(), hbm(), hbm()]
    sh = lambda w, dt: jax.ShapeDtypeStruct((T, w), dt)
    out_shape = [sh(D, F32), sh(D, BF16), sh(D, BF16), sh(D, BF16), sh(D, BF16), sh(D, BF16), sh(D, BF16),
                 sh(A_W, BF16), sh(C_W, BF16), sh(DR, F32), sh(ATT, F32), sh(ATT, F32),
                 sh(DR, BF16), sh(ATT, BF16),
                 jax.ShapeDtypeStruct((8, LANES), F32), jax.ShapeDtypeStruct((1, D), F32),
                 jax.ShapeDtypeStruct((1, D), F32)]
    out_specs = [tok(D), tok(D), tok(D), tok(D), tok(D), tok(D), tok(D), col(DR, 1), tok(C_W), tok(DR),
                 tok(ATT), tok(ATT), tok(DR), tok(ATT),
                 pl.BlockSpec((8, LANES), lambda i: (0, 0)), vec(), vec()]
    return pl.pallas_call(
        body, grid=(nt,), name="tail_fwd_bwd",
        in_specs=in_specs, out_specs=out_specs, out_shape=out_shape,
        scratch_shapes=[pltpu.VMEM((DR, D), BF16), pltpu.VMEM((D, ATT), BF16), pltpu.VMEM((D, D), BF16),
                        pltpu.VMEM((D, D), BF16), pltpu.VMEM((D, PLE), BF16), pltpu.SemaphoreType.DMA((5,))],
        compiler_params=_cp(("arbitrary",), VMEM_BIG),
    )(*_hbm(x, h, proj, att, proj, proj, proj, proj, proj, p, tgt, norm_ple, b_pg, w_o_rnn, w_o_att_t, w_out, w_pg,
            w_ple_t))


def _rope_tables():
    pos = np.arange(S, dtype=np.float32)
    inv_freq = np.float32(ROPE_THETA) ** (-np.arange(0, HD, 2, dtype=np.float32) / np.float32(HD))
    ang = (pos[:, None] * inv_freq[None, :]).astype(np.float32).astype(np.float64)
    cos, sin = np.cos(ang).astype(np.float32), np.sin(ang).astype(np.float32)
    return jnp.asarray(np.concatenate([cos, cos], axis=1)), jnp.asarray(np.concatenate([-sin, sin], axis=1))


def _local_step(x, p, tgt, project, other_weights, norm_mix, conv_b,
                w_rg_a, b_rg_a, w_rg_x, b_rg_x, lam, q_norm, k_norm, norm_ple, b_pg, start_reduce=None,
                entry_token=None):
    if start_reduce is None:
        start_reduce = lambda arrs, tag: (jnp.zeros((8, LANES), F32), arrs)
    if entry_token is None:
        entry_token = jnp.zeros((8, LANES), F32)
    cos_t, sin_t = _rope_tables()
    wa_b = w_rg_a.astype(BF16)
    wx_b = w_rg_x.astype(BF16)

    hn = _rmsnorm_fwd(x, norm_mix, entry_token)
    proj, w_bufs, chips, conv_w, token = project(hn)
    proj3 = proj.reshape(BL, S, NIN)
    h3, gates = _rnn_fwd(proj3, conv_w, conv_b, wa_b, b_rg_a, wx_b, b_rg_x, lam, token)
    att3, lse, wts, *qkv_p = _attn_fwd(proj3, cos_t, sin_t, q_norm, k_norm)
    w_o_rnn, w_o_att_t, w_out, w_pg, w_ple_t = other_weights(att3)
    (dx1, merged, n1, dpre, dpe, dyr, dya, slab_a, slab_c, dh, datt, sbar, yrnn, yatt, loss8, dnp, dbpg) = _tail(
        x, proj, h3.reshape(T, DR), att3.reshape(T, ATT), p, tgt, w_o_rnn, w_o_att_t, w_out, w_pg, w_ple_t,
        norm_ple, b_pg)

    token, pending_out = start_reduce([
        _mm_tn(yrnn, dyr, 640, T, "dw_o_rnn"),
        _mm_tn(dya, yatt, 512, T, "dw_o_att_t"),
        _mm_tn(merged, dx1, 512, T // 2, "dw_out"),
        _mm_tn(n1, dpre, 512, T, "dw_ple_gate"),
        _mm_tn(dpe, p, 512, T, "dw_ple_t")], "out")

    slab_a3, dcw, dcb, dwa, dba, dwx, dbx, dlam = _rnn_bwd(
        proj3, h3, dh.reshape(BL, S, DR), gates, slab_a.reshape(BL, S, A_W), conv_w, conv_b, wa_b, b_rg_a, wx_b, b_rg_x, lam,
        token)
    datt3 = datt.reshape(BL, S, ATT)
    sbar3 = sbar.reshape(BL, S, ATT)
    *slabs, dqn, dkn = _attn_bwd(proj3, cos_t, sin_t, q_norm, k_norm, lse, wts, qkv_p, datt3, sbar3)
    pieces =[slab_a3.reshape(T, A_W)] + [t.reshape(T, GW) for t in slabs] + [slab_c]
    dw_in_t, db_in = _dw_in(pieces, hn)
    token, pending_in = start_reduce([dw_in_t], "in")
    grad_x, dnm = _grad_x(pieces, w_bufs, chips, token, x, dx1, norm_mix)

    small = dict(w_rg_a=dwa, w_rg_x=dwx, norm_mix=dnm, b_in=db_in, conv_b=dcb, b_rg_a=dba, b_rg_x=dbx,
                 lru_lambda=dlam, q_norm=dqn, k_norm=dkn, norm_ple=dnp, b_ple_gate=dbpg, conv_w=dcw, loss=loss8)
    return grad_x, pending_out, pending_in, small


MESH = pl.DeviceIdType.MESH
HBM_SPEC = pl.BlockSpec(memory_space=pl.ANY)


def _my_pos():
    return lax.axis_index("x"), lax.axis_index("y"), lax.axis_index("c")


def _flip(pos, k):
    x, y, c = pos
    return (1 - x if k & 4 else x, 1 - y if k & 2 else y, 1 - c if k & 1 else c)


def _lin(pos):
    return 4 * pos[0] + 2 * pos[1] + pos[2]


def _chip(pos):
    return 2 * pos[0] + pos[1]


HBM_ONLY = pl.BlockSpec(memory_space=pltpu.HBM)
SEM_SPEC = pl.BlockSpec(memory_space=pltpu.SEMAPHORE)
SPLIT_COPY = pltpu.CompilerParams(has_side_effects=pltpu.SideEffectType.DATAFLOW_SIDE_EFFECTING)


def _chip_peers(me):
    return [_flip(me, 4), _flip(me, 2), _flip(me, 6)]


def _between_chips_start(parts, name):
    na = len(parts)

    def body(*refs):
        a_refs = refs[:na]
        land_refs = refs[na:2 * na]
        send_sems, recv_sems = refs[2 * na], refs[2 * na + 1]
        token = refs[-1]
        me = _my_pos()
        myq = _chip(me)
        for i in range(na):
            for j, peer in enumerate(_chip_peers(me)):
                pltpu.make_async_remote_copy(
                    src_ref=a_refs[i].at[_chip(peer)], dst_ref=land_refs[i].at[myq],
                    send_sem=send_sems.at[3 * i + j], recv_sem=recv_sems.at[3 * i + j],
                    device_id=peer, device_id_type=MESH).start()
        token[...] = jnp.zeros_like(token)

    hbm = [pltpu.HBM(a.shape, a.dtype) for a in parts]
    srcs = [pltpu.with_memory_space_constraint(a, pltpu.HBM) for a in parts]
    lands = [pltpu.with_memory_space_constraint(lax.empty(a.shape, a.dtype), pltpu.HBM) for a in parts]
    res = pl.pallas_call(
        body, name=name,
        out_shape=(pltpu.SemaphoreType.DMA((3 * na,)), pltpu.SemaphoreType.DMA((3 * na,)), *hbm, *hbm,
                   jax.ShapeDtypeStruct((8, LANES), F32)),
        in_specs=[HBM_ONLY] * (2 * na),
        out_specs=(SEM_SPEC, SEM_SPEC, *([HBM_ONLY] * (2 * na)), pl.BlockSpec(memory_space=pltpu.VMEM)),
        input_output_aliases={i: 2 + i for i in range(2 * na)},
        compiler_params=SPLIT_COPY,
    )(*srcs, *lands)
    return res[-1], (res[0], res[1], list(res[2:2 + na]), list(res[2 + na:2 + 2 * na]))


def _between_chips_wait(pending, after, name):
    send_sems, recv_sems, parts, lands = pending
    na = len(parts)

    def body(*refs):
        a_refs = refs[:na]
        land_refs = refs[na:2 * na]
        send_sems, recv_sems = refs[2 * na], refs[2 * na + 1]
        me = _my_pos()
        for i in range(na):
            for j, peer in enumerate(_chip_peers(me)):
                cp = pltpu.make_async_remote_copy(
                    src_ref=a_refs[i].at[_chip(peer)], dst_ref=land_refs[i].at[_chip(peer)],
                    send_sem=send_sems.at[3 * i + j], recv_sem=recv_sems.at[3 * i + j],
                    device_id=peer, device_id_type=MESH)
                cp.wait_send()
                cp.wait_recv()

    hbm = [pltpu.HBM(a.shape, a.dtype) for a in parts]
    res = pl.pallas_call(
        body, name=name, out_shape=(*hbm, *hbm),
        in_specs=[HBM_ONLY] * (2 * na) + [SEM_SPEC, SEM_SPEC, pl.BlockSpec(memory_space=pl.ANY)],
        out_specs=[HBM_ONLY] * (2 * na),
        input_output_aliases={i: i for i in range(2 * na)},
        compiler_params=SPLIT_COPY,
    )(*parts, *lands, send_sems, recv_sems, after)
    return list(res[:na]), list(res[na:])


def _remote(src, dst, send_sems, recv_sems, idx, peer):
    return pltpu.make_async_remote_copy(src_ref=src, dst_ref=dst, send_sem=send_sems.at[idx],
                                        recv_sem=recv_sems.at[idx], device_id=peer, device_id_type=MESH)


def _copies_own(bufs, me):
    return [(bufs[0], bufs[1].at[me[2]], 0, _flip(me, 1))]


def _copies_near(bufs, me):
    return [(bufs[0], bufs[1].at[0, me[2]], 0, _flip(me, 2)), (bufs[0], bufs[1].at[1, me[2]], 1, _flip(me, 4))]


def _copies_far(bufs, me):
    return [(bufs[0], bufs[1].at[me[2]], 0, _flip(me, 6))]


def _copies_others(bufs, me):
    na = len(bufs) // 2
    return [(bufs[i], bufs[na + i].at[_lin(me)], 7 * i + k - 1, _flip(me, k))
            for i in range(na) for k in range(1, NDEV)]


def _copies_exchange(bufs, me):
    na = len(bufs) // 2
    return [(bufs[i].at[_lin(_flip(me, k))], bufs[na + i].at[_lin(me)], 7 * i + k - 1, _flip(me, k))
            for i in range(na) for k in range(1, NDEV)]


GROUP_COPIES = dict(own=_copies_own, near=_copies_near, far=_copies_far, others=_copies_others,
                    exchange=_copies_exchange)
GROUP_COUNT = dict(own=1, near=2, far=1)
TO_ALL = ("others", "exchange")


def _gather_start(bufs, groups, after, name):
    nb = len(bufs)
    ng = len(groups)

    def body(*refs):
        b = refs[:nb]
        sems = refs[nb + 1:nb + 1 + 2 * ng]
        token = refs[-1]
        me = _my_pos()
        for gi, (group, idx) in enumerate(groups):
            for src, dst, k, peer in GROUP_COPIES[group]([b[i] for i in idx], me):
                _remote(src, dst, sems[2 * gi], sems[2 * gi + 1], k, peer).start()
        token[...] = jnp.zeros_like(token)

    sem_t = []
    for group, idx in groups:
        cnt = 7 * (len(idx) // 2) if group in TO_ALL else GROUP_COUNT[group]
        sem_t += [pltpu.SemaphoreType.DMA((cnt,)), pltpu.SemaphoreType.DMA((cnt,))]
    ins = [pltpu.with_memory_space_constraint(a, pltpu.HBM) for a in bufs]
    res = pl.pallas_call(
        body, name=name,
        out_shape=(*sem_t, *[pltpu.HBM(a.shape, a.dtype) for a in bufs], jax.ShapeDtypeStruct((8, LANES), F32)),
        in_specs=[HBM_ONLY] * nb + [pl.BlockSpec(memory_space=pl.ANY)],
        out_specs=(*([SEM_SPEC] * (2 * ng)), *([HBM_ONLY] * nb), pl.BlockSpec(memory_space=pltpu.VMEM)),
        input_output_aliases={i: 2 * ng + i for i in range(nb)},
        compiler_params=SPLIT_COPY,
    )(*ins, after)
    return res[-1], list(res[2 * ng:2 * ng + nb]), [(res[2 * gi], res[2 * gi + 1]) for gi in range(ng)]


def _gather_wait(group, send_sems, recv_sems, bufs, after, name):
    nb = len(bufs)
    copies = GROUP_COPIES[group]
    afters = list(after) if isinstance(after, (list, tuple)) else [after]

    def body(*refs):
        b = refs[:nb]
        ss, rs = refs[nb], refs[nb + 1]
        me = _my_pos()
        for src, dst, idx, peer in copies(b, me):
            if group in TO_ALL:
                landed = b[nb // 2 + idx // 7].at[_lin(peer)]
            elif group == "own":
                landed = b[1].at[1 - me[2]]
            else:
                landed = dst
            cp = _remote(src, landed, ss, rs, idx, peer)
            cp.wait_send()
            cp.wait_recv()

    res = pl.pallas_call(
        body, name=name, out_shape=[pltpu.HBM(a.shape, a.dtype) for a in bufs],
        in_specs=[HBM_ONLY] * nb + [SEM_SPEC, SEM_SPEC] + [pl.BlockSpec(memory_space=pl.ANY)] * len(afters),
        out_specs=[HBM_ONLY] * nb,
        input_output_aliases={i: i for i in range(nb)},
        compiler_params=SPLIT_COPY,
    )(*bufs, send_sems, recv_sems, *afters)
    return list(res)


def _forward_to_sibling(buf, name):
    n = buf.shape[0]

    def body(_in_ref, out_ref, send_sems, recv_sems):
        me = _my_pos()
        c = me[2]
        sibling = _flip(me, 1)
        sends = []
        for r in range(n):
            cp = _remote(out_ref.at[r, c], out_ref.at[r, c], send_sems, recv_sems, r, sibling)
            cp.start()
            sends.append(cp)
        for r in range(n):
            _remote(out_ref.at[r, c], out_ref.at[r, 1 - c], send_sems, recv_sems, r, sibling).wait_recv()
        for cp in sends:
            cp.wait_send()

    return pl.pallas_call(
        body, name=name, out_shape=jax.ShapeDtypeStruct(buf.shape, buf.dtype),
        in_specs=[HBM_SPEC], out_specs=HBM_SPEC,
        scratch_shapes=[pltpu.SemaphoreType.DMA((n,)), pltpu.SemaphoreType.DMA((n,))],
        input_output_aliases={0: 0},
    )(buf)


def _scalar(v):
    return jnp.asarray(v, jnp.int32).reshape(1)


def _exchange_sum_within_chip(part, name):
    nq, _, rows, width = part.shape

    def body(c_ref, mine_ref, src_hbm, out_ref, land_hbm, buf, copy_sem, send_sems, recv_sems):
        q = pl.program_id(0)
        me = _my_pos()
        sibling = _flip(me, 1)

        def remote(k):
            return pltpu.make_async_remote_copy(
                src_ref=src_hbm.at[k, 1 - me[2]], dst_ref=land_hbm.at[k], send_sem=send_sems.at[k],
                recv_sem=recv_sems.at[k], device_id=sibling, device_id_type=MESH)

        @pl.when(q == 0)
        def _():
            remote(0).start()
            remote(1).start()

        remote(q).wait_recv()

        @pl.when(q + 2 < nq)
        def _():
            remote(q + 2).start()

        arrived =pltpu.make_async_copy(land_hbm.at[q], buf, copy_sem.at[0])
        arrived.start()
        arrived.wait()
        out_ref[0] = (mine_ref[0, 0].astype(F32) + buf[...].astype(F32)).astype(out_ref.dtype)

        @pl.when(q == nq - 1)
        def _():
            for k in range(nq):
                remote(k).wait_send()

    blk = (rows, width)
    summed, _ = pl.pallas_call(
        body, name=name,
        grid_spec=pltpu.PrefetchScalarGridSpec(
            num_scalar_prefetch=1, grid=(nq,),
            in_specs=[pl.BlockSpec((1, 1) + blk, lambda q, c_ref: (q, c_ref[0], 0, 0)), HBM_SPEC],
            out_specs=[pl.BlockSpec((1,) + blk, lambda q, c_ref: (q, 0, 0)), HBM_SPEC],
            scratch_shapes=[pltpu.VMEM(blk, part.dtype), pltpu.SemaphoreType.DMA((1,)),
                            pltpu.SemaphoreType.DMA((nq,)), pltpu.SemaphoreType.DMA((nq,))]),
        out_shape=[jax.ShapeDtypeStruct((nq,) + blk, part.dtype)] * 2,
        compiler_params=_cp(("arbitrary",), VMEM_BIG),
    )(_scalar(lax.axis_index("c")), *_hbm(part, part))
    return summed


def _others(q, mine, nblk=NCHIP):
    return jnp.where(q == mine, (q + 1) % nblk, q)


def _sum_chips_adamw(own, recv, wv, mv, vv, token, tr, name):
    _, r, w = recv.shape

    def body(q_ref, own_ref, r0, r1, r2, r3, w_ref, m_ref, v_ref, _token, g_ref, d_ref, m2_ref, v2_ref):
        myq = q_ref[0]
        acc = None
        for q, r_ref in enumerate((r0, r1, r2, r3)):
            term = jnp.where(myq == q, own_ref[0], r_ref[0]).astype(F32)
            acc = term if acc is None else acc + term
        g_ref[...] = acc
        delta, m2, v2 = _adam_math(w_ref[...], acc, m_ref[...], v_ref[...])
        d_ref[...] = delta
        m2_ref[...] = m2
        v2_ref[...] = v2

    def recv_spec(q):
        return pl.BlockSpec((1, tr, w), lambda i, q_ref: (_others(q, q_ref[0]), i, 0))

    rows = lambda: pl.BlockSpec((tr, w), lambda i, q_ref: (i, 0))
    shp = jax.ShapeDtypeStruct((r, w), F32)
    return pl.pallas_call(
        body, name=name,
        grid_spec=pltpu.PrefetchScalarGridSpec(
            num_scalar_prefetch=1, grid=(r // tr,),
            in_specs=[pl.BlockSpec((1, tr, w), lambda i, q_ref: (q_ref[0], i, 0))]
            + [recv_spec(q) for q in range(NCHIP)] + [rows(), rows(), rows()]
            + [pl.BlockSpec((8, LANES), lambda i, q_ref: (0, 0))],
            out_specs=[rows(), rows(), rows(), rows()]),
        out_shape=[shp, shp, shp, shp],
        compiler_params=_cp(("arbitrary",), VMEM_MID),
    )(_scalar(_chip(_my_pos())), *_hbm(own, recv, recv, recv, recv, wv, mv, vv, token))


def _sum_blocks_small(own, recv, mine, transpose, name):
    na = len(recv)
    nblk = recv[0].shape[0]

    def body(q_ref, *refs):
        me = q_ref[0]
        for i in range(na):
            acc = None
            for q in range(nblk):
                term = jnp.where(me == q, refs[i][0], refs[na * (1 + q) + i][0]).astype(F32)
                acc = term if acc is None else acc + term
            refs[na * (1 + nblk) + i][...] = acc.T if transpose[i] else acc

    def oshape(a, tr):
        r, w = a.shape[1:]
        return (w, r) if tr else (r, w)

    own_spec = lambda a: pl.BlockSpec((1,) + a.shape[1:], lambda s, q_ref: (q_ref[0], 0, 0))
    recv_spec = lambda a, q: pl.BlockSpec((1,) + a.shape[1:], lambda s, q_ref: (_others(q, q_ref[0], nblk), 0, 0))
    out_spec = lambda shp: pl.BlockSpec(shp, lambda s, q_ref: (0, 0))
    in_specs = [own_spec(a) for a in own]
    for q in range(nblk):
        in_specs += [recv_spec(a, q) for a in recv]
    return pl.pallas_call(
        body, name=name,
        grid_spec=pltpu.PrefetchScalarGridSpec(
            num_scalar_prefetch=1, grid=(1,), in_specs=in_specs,
            out_specs=[out_spec(oshape(a, tr)) for a, tr in zip(recv, transpose)]),
        out_shape=[jax.ShapeDtypeStruct(oshape(a, tr), F32) for a, tr in zip(recv, transpose)],
        compiler_params=_cp(("arbitrary",), VMEM_MID),
    )(_scalar(mine), *_hbm(*own, *(list(recv) * nblk)))


def _rep_offsets():
    offs = []
    o = 0
    for r in REP_ROWS:
        offs.append(o)
        o += r
    return offs


LOSS_ROW = REP_TOTAL_ROWS


def _pack_small_grads(g):
    offs = _rep_offsets()

    def body(dwa, dwx, dnm, dbin, dcb, dba, dbx, dlam, dqn, dkn, dnp, dbpg, loss, o_ref):
        o_ref[pl.ds(REP_TOTAL_ROWS - 2, NDEV * REP_ROWS_DEV - REP_TOTAL_ROWS + 2), :] = jnp.zeros(
            (NDEV * REP_ROWS_DEV - REP_TOTAL_ROWS + 2, LANES), F32)
        o_ref[pl.ds(LOSS_ROW, 1), :] = loss[0:1, :]
        for n in range(NRB):
            o_ref[pl.ds(offs[0] + n * RBW, RBW), :] = dwa[n]
            o_ref[pl.ds(offs[1] + n * RBW, RBW), :] = dwx[n]

        def put_vec(off, ref, rows):
            for k in range(rows):
                o_ref[pl.ds(off + k, 1), :] = ref[:, k * LANES:(k + 1) * LANES]

        put_vec(offs[2], dnm, REP_ROWS[2])
        put_vec(offs[3], dbin, REP_ROWS[3])
        put_vec(offs[4], dcb, REP_ROWS[4])
        put_vec(offs[5], dba, REP_ROWS[5])
        put_vec(offs[6], dbx, REP_ROWS[6])
        put_vec(offs[7], dlam, REP_ROWS[7])
        for k in range(NG):
            o_ref[pl.ds(offs[8] + k, 1), :] = dqn[k, 0:1, :]
            o_ref[pl.ds(offs[9] + k, 1), :] = dkn[k, 0:1, :]
        put_vec(offs[10], dnp, REP_ROWS[10])
        put_vec(offs[11], dbpg, REP_ROWS[11])

    args = [g["w_rg_a"], g["w_rg_x"], g["norm_mix"], g["b_in"], g["conv_b"], g["b_rg_a"], g["b_rg_x"],
            g["lru_lambda"], g["q_norm"], g["k_norm"], g["norm_ple"], g["b_ple_gate"], g["loss"]]
    full = lambda shp: pl.BlockSpec(shp, lambda: (0,) * len(shp))
    return pl.pallas_call(
        body, name="pack_small_grads",
        in_specs=[full(a.shape) for a in args],
        out_specs=full((NDEV * REP_ROWS_DEV, LANES)),
        out_shape=jax.ShapeDtypeStruct((NDEV * REP_ROWS_DEV, LANES), F32),
    )(*_hbm(*args))


def _adam_math(wv, gv, mv, vv):
    c1 = 1.0 - B1 ** STEP
    c2 = 1.0 - B2 ** STEP
    m2 = B1 * mv + (1.0 - B1) * gv
    v2 = B2 * vv + (1.0 - B2) * (gv * gv)
    delta = (-LR) * ((m2 / c1) / (jnp.sqrt(v2 / c2) + AEPS) + WD * wv)
    return delta, m2, v2


def _adamw_small(rep_flat, w, m, v):
    offs = _rep_offsets()
    n = len(REP_NAMES)

    def body(*refs):
        g_ref = refs[0]
        w_refs = refs[1:1 + n]
        m_refs = refs[1 + n:1 + 2 * n]
        v_refs = refs[1 + 2 * n:1 + 3 * n]
        outs = refs[1 + 3 * n:]
        go, do, mo, vo = outs[:n], outs[n:2 * n], outs[2 * n:3 * n], outs[3 * n:]

        def emit(i, idx, gv):
            go[i][idx] = gv
            delta, m2, v2 = _adam_math(w_refs[i][idx], gv, m_refs[i][idx], v_refs[i][idx])
            do[i][idx] = delta
            mo[i][idx] = m2
            vo[i][idx] = v2

        for i in range(n):
            if i < 2:
                for b in range(NRB):
                    emit(i, b, g_ref[pl.ds(offs[i] + b * RBW, RBW), :])
            elif REP_NAMES[i] in ("q_norm", "k_norm"):
                emit(i, slice(None), g_ref[pl.ds(offs[i], NG), :])
            else:
                gv = jnp.concatenate([g_ref[pl.ds(offs[i] + k, 1), :] for k in range(REP_ROWS[i])], axis=1)
                emit(i, slice(None), gv)

    full = lambda shp: pl.BlockSpec(shp, lambda: (0,) * len(shp))
    pspecs = [full(a.shape) for a in w]
    pshapes = [jax.ShapeDtypeStruct(a.shape, F32) for a in w]
    res = pl.pallas_call(
        body, name="adamw_small",
        in_specs=[full(rep_flat.shape)] + pspecs * 3,
        out_specs=pspecs * 4, out_shape=pshapes * 4,
        compiler_params=_cp(None, VMEM_MID),
    )(*_hbm(rep_flat, *w, *m, *v))
    return res[:n], res[n:2 * n], res[2 * n:3 * n], res[3 * n:]


def _adamw_many(w, g, m, v, token):
    n = len(w)

    def body(*refs):
        for i in range(n):
            delta, m2, v2 = _adam_math(refs[i][...], refs[n + i][...], refs[2 * n + i][...], refs[3 * n + i][...])
            refs[4 * n + 1 + i][...] = delta
            refs[5 * n + 1 + i][...] = m2
            refs[6 * n + 1 + i][...] = v2

    full = lambda shp: pl.BlockSpec(shp, lambda: (0,) * len(shp))
    specs = [full(a.shape) for a in w]
    shapes = [jax.ShapeDtypeStruct(a.shape, F32) for a in w]
    res = pl.pallas_call(
        body, name="adamw_shards",
        in_specs=specs * 4 + [full((8, LANES))], out_specs=specs * 3, out_shape=shapes * 3,
        compiler_params=_cp(None, VMEM_MID),
    )(*_hbm(*w, *g, *m, *v, token))
    return res[:n], res[n:2 * n], res[2 * n:]


def kernel(x, p, norm_mix, w_in, b_in, conv_w, conv_b, w_rg_a, b_rg_a, w_rg_x, b_rg_x, lru_lambda, q_norm, k_norm, w_o_rnn, w_o_att, w_out, norm_ple, w_ple_gate, b_ple_gate, w_ple, loss_target, m_norm_mix, m_w_in, m_b_in, m_conv_w, m_conv_b, m_w_rg_a, m_b_rg_a, m_w_rg_x, m_b_rg_x, m_lru_lambda, m_q_norm, m_k_norm, m_w_o_rnn, m_w_o_att, m_w_out, m_norm_ple, m_w_ple_gate, m_b_ple_gate, m_w_ple, v_norm_mix, v_w_in, v_b_in, v_conv_w, v_conv_b, v_w_rg_a, v_b_rg_a, v_w_rg_x, v_b_rg_x, v_lru_lambda, v_q_norm, v_k_norm, v_w_o_rnn, v_w_o_att, v_w_out, v_norm_ple, v_w_ple_gate, v_b_ple_gate, v_w_ple):
    w = dict(norm_mix=norm_mix, w_in=w_in, b_in=b_in, conv_w=conv_w, conv_b=conv_b, w_rg_a=w_rg_a, b_rg_a=b_rg_a,
             w_rg_x=w_rg_x, b_rg_x=b_rg_x, lru_lambda=lru_lambda, q_norm=q_norm, k_norm=k_norm, w_o_rnn=w_o_rnn,
             w_o_att=w_o_att, w_out=w_out, norm_ple=norm_ple, w_ple_gate=w_ple_gate, b_ple_gate=b_ple_gate,
             w_ple=w_ple)
    m = dict(norm_mix=m_norm_mix, w_in=m_w_in, b_in=m_b_in, conv_w=m_conv_w, conv_b=m_conv_b, w_rg_a=m_w_rg_a,
             b_rg_a=m_b_rg_a, w_rg_x=m_w_rg_x, b_rg_x=m_b_rg_x, lru_lambda=m_lru_lambda, q_norm=m_q_norm,
             k_norm=m_k_norm, w_o_rnn=m_w_o_rnn, w_o_att=m_w_o_att, w_out=m_w_out, norm_ple=m_norm_ple,
             w_ple_gate=m_w_ple_gate, b_ple_gate=m_b_ple_gate, w_ple=m_w_ple)
    v = dict(norm_mix=v_norm_mix, w_in=v_w_in, b_in=v_b_in, conv_w=v_conv_w, conv_b=v_conv_b, w_rg_a=v_w_rg_a,
             b_rg_a=v_b_rg_a, w_rg_x=v_w_rg_x, b_rg_x=v_b_rg_x, lru_lambda=v_lru_lambda, q_norm=v_q_norm,
             k_norm=v_k_norm, w_o_rnn=v_w_o_rnn, w_o_att=v_w_o_att, w_out=v_w_out, norm_ple=v_norm_ple,
             w_ple_gate=v_w_ple_gate, b_ple_gate=v_b_ple_gate, w_ple=v_w_ple)
    names = list(w.keys())

    w_shard, conv_shard = w_in[0].T.astype(BF16), conv_w[0]
    pos = _my_pos()
    me, my_core, my_chip = _lin(pos), pos[2], _chip(pos)
    hbm_empty = lambda shp, dt: lax.empty(shp, dt)
    shp = w_shard.shape
    entry_token, bufs, sems = _gather_start(
        [w_shard, hbm_empty((2,) + shp, BF16), hbm_empty((2, 2) + shp, BF16), conv_shard,
         hbm_empty((NDEV,) + conv_shard.shape, F32)],
        [("own", (0, 1)), ("near", (0, 2)), ("others", (3, 4))], norm_mix, "gather_start_near")
    w_src, own_l, near_l, conv_src, conv_l = bufs
    sem_own, sem_near, sem_conv = sems
    gather_out = {}

    def project(hn):
        w_thru, own = _gather_wait("own", *sem_own, [w_src, own_l], hn, "gather_wait_own")
        own = lax.dynamic_update_slice(own, w_shard[None], (my_core, 0, 0)).reshape(1, CHIP_COLS, D)
        chips = [jnp.stack([my_chip]), jnp.stack([my_chip ^ 1, my_chip ^ 2]), jnp.stack([my_chip ^ 3])]
        chips = [c.astype(jnp.int32) for c in chips]
        proj = _in_proj_chips(hn, own, b_in, chips[0], None, entry_token, "in_proj_own")
        proj, raw = lax.optimization_barrier((proj, (w_o_rnn[0], w_o_att[0], w_out[0], w_ple_gate[0], w_ple[0])))
        srcs = [raw[0].astype(BF16), raw[1].T.astype(BF16), raw[2].astype(BF16), raw[3].astype(BF16),
                raw[4].T.astype(BF16)]
        w_thru, near = _gather_wait("near", *sem_near, [w_thru, near_l], [proj] + srcs, "gather_wait_near")
        near = _forward_to_sibling(near, "gather_forward_near")
        token, (w_thru, far_l), (sem_far,) = _gather_start(
            [w_thru, hbm_empty((2,) + shp, BF16)], [("far", (0, 1))], near, "gather_start_far")
        near = near.reshape(2, CHIP_COLS, D)
        proj = _in_proj_chips(hn, near, b_in, chips[1], proj, token, "in_proj_near")
        w_thru, far = _gather_wait("far", *sem_far, [w_thru, far_l], proj, "gather_wait_far")
        far = _forward_to_sibling(far[None], "gather_forward_far").reshape(1, CHIP_COLS, D)
        proj = _in_proj_chips(hn, far, b_in, chips[2], proj, token, "in_proj_far")
        conv_thru, conv_g = _gather_wait("others", *sem_conv, [conv_src, conv_l], proj, "gather_wait_conv")
        conv_g = lax.dynamic_update_slice(conv_g, conv_shard[None], (me, 0, 0))
        conv_f = conv_g.transpose(1, 0, 2).reshape(CONVW, DR)
        token, obufs, (sem_out,) = _gather_start(
            srcs + [hbm_empty((NDEV,) + a.shape, BF16) for a in srcs], [("others", tuple(range(10)))], proj,
            "gather_start_out")
        gather_out.update(bufs=obufs, sems=sem_out, shards=srcs)
        return proj, [own, near, far], jnp.concatenate(chips), conv_f, token

    def other_weights(after):
        obufs = _gather_wait("others", *gather_out["sems"], gather_out["bufs"], after, "gather_wait_out")
        full = [lax.dynamic_update_slice(a, s[None], (me, 0, 0)) for a, s in zip(obufs[5:], gather_out["shards"])]
        return [a.reshape((NDEV * a.shape[1], a.shape[2])) for a in full]

    def start_reduce(arrs, tag):
        if tag == "out":
            parts = [a.reshape((NDEV, a.shape[0] // NDEV, a.shape[1])) for a in arrs]
            token, bufs, (sems,) = _gather_start(
                parts + [lax.empty(a.shape, a.dtype) for a in parts], [("exchange", tuple(range(2 * len(parts))))],
                arrs[-1][:SUBLANES], "reduce_out_start")
            return token, (bufs, sems)
        parts = [a.reshape((NCHIP, 2, a.shape[0] // NDEV, a.shape[1])) for a in arrs]
        pair_sums = [_exchange_sum_within_chip(a, "reduce_within_chip_" + tag) for a in parts]
        return _between_chips_start(pair_sums, "reduce_between_chips_start_" + tag)

    grad_x, pending_out, pending_in, small = _local_step(
        x.reshape(T, D), p.reshape(T, PLE), loss_target.reshape(T, D),
        project, other_weights,
        norm_mix, conv_b, w_rg_a[0], b_rg_a, w_rg_x[0], b_rg_x, lru_lambda, q_norm[0], k_norm[0],
        norm_ple, b_ple_gate, start_reduce, entry_token)

    rep_parts = _pack_small_grads(small).reshape(NDEV, REP_ROWS_DEV, LANES)
    conv_parts = small["conv_w"].reshape(CONVW, NDEV, DR // NDEV).transpose(1, 0, 2)
    smalls = [rep_parts, conv_parts]
    token, sbufs, (sem_x,) = _gather_start(
        smalls + [lax.empty(a.shape, F32) for a in smalls], [("exchange", (0, 1, 2, 3))], small["norm_mix"],
        "reduce_small_start")

    own_in, recv_in = _between_chips_wait(pending_in, token, "reduce_between_chips_wait_in")
    w_in_res = _sum_chips_adamw(own_in[0], recv_in[0], w_in[0].T, m_w_in[0].T, v_w_in[0].T, token, 304, "adamw_w_in")
    sbufs = _gather_wait("exchange", *sem_x, sbufs, w_in_res[0], "reduce_small_wait")
    g_rep, g_conv = _sum_blocks_small(sbufs[:2], sbufs[2:], me, (False, False), "sum_small")
    token, gbufs, (sem_g,) = _gather_start(
        [g_rep, lax.empty((NDEV,) + g_rep.shape, F32)], [("others", (0, 1))], g_conv, "gather_small_start")
    obufs = _gather_wait("exchange", *pending_out[1], pending_out[0], token, "reduce_out_wait")
    g_o_rnn, g_o_att, g_out, g_pg, g_ple = _sum_blocks_small(
        obufs[:5], obufs[5:], me, (False, True, False, False, True), "sum_out")

    grad, delta, new_m, new_v = {}, {}, {}, {}
    rest = ("w_o_rnn", "w_o_att", "w_out", "w_ple_gate", "w_ple", "conv_w")
    g_rest = [g_o_rnn, g_o_att, g_out, g_pg, g_ple, g_conv]
    rest_res = _adamw_many([w[n][0] for n in rest], g_rest, [m[n][0] for n in rest], [v[n][0] for n in rest], token)
    _, rep_all = _gather_wait("others", *sem_g, gbufs, rest_res[0][0], "gather_small_wait")
    rep_all = lax.dynamic_update_slice(rep_all, g_rep[None], (me, 0, 0)).reshape(NDEV * REP_ROWS_DEV, LANES)
    loss = rep_all[LOSS_ROW, 0]
    rep_shape = lambda a: a if a.ndim == 2 else a.reshape(a.shape[1:])
    res = _adamw_small(rep_all, [rep_shape(w[n]) for n in REP_NAMES], [rep_shape(m[n]) for n in REP_NAMES],
                       [rep_shape(v[n]) for n in REP_NAMES])
    for dst, vals in zip((grad, delta, new_m, new_v), res):
        for n, a in zip(REP_NAMES, vals):
            dst[n] = a.reshape(w[n].shape)
    grad["w_in"], delta["w_in"], new_m["w_in"], new_v["w_in"] = [a.T[None] for a in w_in_res]
    for n, a in zip(rest, g_rest):
        grad[n] = a[None]
    for dst, vals in zip((delta, new_m, new_v), rest_res):
        for n, a in zip(rest, vals):
            dst[n] = a[None]

    return (loss, grad_x.reshape(BL, S, D), *[grad[n] for n in names], *[delta[n] for n in names],
            *[new_m[n] for n in names], *[new_v[n] for n in names])
```

```python
import jax
import jax.numpy as jnp
import numpy as np
from jax import lax
from jax.experimental import pallas as pl
from jax.experimental.pallas import tpu as pltpu

F32 = jnp.float32
BF16 = jnp.bfloat16

D = 1024
S = 2048
BL = 2
T = BL * S
NDEV = 8
NCHIP = 4
PLE = 256
DR = 1280
NRB = 10
RBW = 128
CONVW = 4
LRU_C = 8.0
HD = 128
NH = 4
PATTERNS = ((128, 1), (512, 4), (2048, 16))
NG = 3
ATT = NH * HD
GW = NG * ATT
NIN = 2 * DR + 3 * GW + ATT + 2 * D
OFF_ZR = DR
OFF_Q = 2 * DR
OFF_ZA = OFF_Q + 3 * GW
OFF_G = OFF_ZA + ATT
ROPE_THETA = 10000.0
EPS = 1e-6
SCALE = HD ** -0.5
NEG = -1e30
QB = 128
LANES = 128
CT = 512
NCT = NIN // CT
A_W = 2 * DR
C_W = ATT + 2 * D

LR, B1, B2, AEPS, WD, STEP = 0.001, 0.9, 0.999, 1e-08, 0.01, 10

NSHARD_IN = NIN // NDEV
REP_NAMES = ("w_rg_a", "w_rg_x", "norm_mix", "b_in", "conv_b", "b_rg_a", "b_rg_x", "lru_lambda", "q_norm",
             "k_norm", "norm_ple", "b_ple_gate")
REP_ROWS = (NRB * RBW, NRB * RBW, D // LANES, NIN // LANES, DR // LANES, DR // LANES, DR // LANES, DR // LANES,
            NG, NG, D // LANES, D // LANES)
REP_TOTAL_ROWS = sum(REP_ROWS)
REP_ROWS_DEV = 344
BIG_NAMES = ("w_in", "w_o_rnn", "w_o_att", "w_out", "w_ple_gate", "w_ple")

VMEM_BIG = 56 * 1024 * 1024
VMEM_MID = 40 * 1024 * 1024


def _cp(sem=None, vmem=None):
    return pltpu.CompilerParams(dimension_semantics=sem, vmem_limit_bytes=vmem)


def _hbm(*arrays):
    return [pltpu.with_memory_space_constraint(a, pltpu.HBM) for a in arrays]


def _copy_together(copies):
    for cp in copies:
        cp.start()
    for cp in copies:
        cp.wait()


def _dot(a, b):
    return jnp.dot(a, b, preferred_element_type=F32)


def _dot_nt(a, b):
    return lax.dot_general(a, b, (((1,), (1,)), ((), ())), preferred_element_type=F32)


def _dot_tn(a, b):
    return lax.dot_general(a, b, (((0,), (0,)), ((), ())), preferred_element_type=F32)


def _sigmoid(x):
    return jax.nn.sigmoid(x)


def _perm(j):
    jq = j - OFF_Q // CT
    inside = (j >= OFF_Q // CT) & (j < OFF_ZA // CT)
    return jnp.where(inside, OFF_Q // CT + (jq % 3) * 3 + jq // 3, j)


PIECES = ((0, A_W // CT), (OFF_Q // CT, GW // CT), (OFF_Q // CT + 3, GW // CT), (OFF_Q // CT + 6, GW // CT),
          (OFF_ZA // CT, C_W // CT))


def _rmsnorm_fwd(x, gain, token, tm=512):
    def body(x_ref, g_ref, _token, o_ref):
        xv = x_ref[...]
        var = jnp.mean(xv * xv, axis=-1, keepdims=True)
        o_ref[...] = (xv * lax.rsqrt(var + EPS) * g_ref[...]).astype(BF16)

    return pl.pallas_call(
        body, grid=(T // tm,), name="rmsnorm_fwd",
        in_specs=[pl.BlockSpec((tm, D), lambda i: (i, 0)), pl.BlockSpec((1, D), lambda i: (0, 0)),
                  pl.BlockSpec((8, LANES), lambda i: (0, 0))],
        out_specs=pl.BlockSpec((tm, D), lambda i: (i, 0)),
        out_shape=jax.ShapeDtypeStruct((T, D), BF16),
        compiler_params=_cp(("parallel",)),
    )(*_hbm(x, gain, token))


CHIP_COLS = NIN // NCHIP


def _in_proj_chips(hn, w_rows, bias, chips, proj, token, name, tm=1024):
    n = w_rows.shape[0]

    def body(chips_ref, a_ref, w_ref, b_ref, _token, *rest):
        o_ref = rest[-1]
        o_ref[...] = (_dot_nt(a_ref[...], w_ref[0]) + b_ref[...]).astype(BF16)

    in_specs = [pl.BlockSpec((tm, D), lambda s, i, ch: (i, 0)),
                pl.BlockSpec((1, CHIP_COLS, D), lambda s, i, ch: (s, 0, 0)),
                pl.BlockSpec((1, CHIP_COLS), lambda s, i, ch: (0, ch[s])),
                pl.BlockSpec((8, LANES), lambda s, i, ch: (0, 0))]
    args = [hn, w_rows, bias, token]
    aliases = {}
    if proj is not None:
        in_specs.append(pl.BlockSpec(memory_space=pl.ANY))
        args.append(proj)
        aliases = {5: 0}
    return pl.pallas_call(
        body, name=name,
        grid_spec=pltpu.PrefetchScalarGridSpec(
            num_scalar_prefetch=1, grid=(n, T // tm), in_specs=in_specs,
            out_specs=pl.BlockSpec((tm, CHIP_COLS), lambda s, i, ch: (i, ch[s]))),
        out_shape=jax.ShapeDtypeStruct((T, NIN), BF16),
        input_output_aliases=aliases,
        compiler_params=_cp(("arbitrary", "arbitrary"), VMEM_BIG),
    )(chips, *_hbm(*args))


def _grad_x(pieces, w_bufs, chips, token, x, dx1, gain, tm=512):
    nb = len(w_bufs)

    def body(chips_ref, a_ref, q_ref, k_ref, v_ref, c_ref, *rest):
        w_hbm = rest[:nb]
        x_ref, dx1_ref, g_ref, dx_ref, dg_ref, w, sems = rest[nb + 1:]
        first = pl.program_id(0) == 0

        @pl.when(first)
        def _():
            s = 0
            copies = []
            for buf in w_hbm:
                for r in range(buf.shape[0]):
                    row = pl.multiple_of(chips_ref[s] * CHIP_COLS, 128)
                    copies.append(pltpu.make_async_copy(buf.at[r], w.at[pl.ds(row, CHIP_COLS), :], sems.at[s]))
                    s += 1
            _copy_together(copies)

        acc = _dot(a_ref[...], w[pl.ds(0, A_W), :])
        for kind, p_ref in enumerate((q_ref, k_ref, v_ref)):
            for g in range(NG):
                row = OFF_Q + (3 * g + kind) * CT
                acc = acc + _dot(p_ref[:, g * CT:(g + 1) * CT], w[pl.ds(row, CT), :])
        dn = acc + _dot(c_ref[...], w[pl.ds(OFF_ZA, C_W), :])
        xv = x_ref[...]
        rstd = lax.rsqrt(jnp.mean(xv * xv, axis=-1, keepdims=True) + EPS)
        xh = xv * rstd
        dg = jnp.sum(dn * xh, axis=0, keepdims=True)
        gd = dn * g_ref[...]
        dx_ref[...] = dx1_ref[...] + rstd * (gd - xh * jnp.mean(gd * xh, axis=-1, keepdims=True))

        @pl.when(first)
        def _():
            dg_ref[...] = dg

        @pl.when(jnp.logical_not(first))
        def _():
            dg_ref[...] += dg

    tok = lambda wd: pl.BlockSpec((tm, wd), lambda i, ch: (i, 0))
    vec = lambda: pl.BlockSpec((1, D), lambda i, ch: (0, 0))
    return pl.pallas_call(
        body, name="grad_x",
        grid_spec=pltpu.PrefetchScalarGridSpec(
            num_scalar_prefetch=1, grid=(T // tm,),
            in_specs=[tok(A_W), tok(GW), tok(GW), tok(GW), tok(C_W)] + [pl.BlockSpec(memory_space=pl.ANY)] * nb
            + [pl.BlockSpec((8, LANES), lambda i, ch: (0, 0)), tok(D), tok(D), vec()],
            out_specs=[tok(D), vec()],
            scratch_shapes=[pltpu.VMEM((NIN, D), BF16), pltpu.SemaphoreType.DMA((NCHIP,))]),
        out_shape=[jax.ShapeDtypeStruct((T, D), F32), jax.ShapeDtypeStruct((1, D), F32)],
        compiler_params=_cp(("arbitrary",), VMEM_BIG),
    )(chips, *_hbm(*pieces, *w_bufs, token, x, dx1, gain))


def _dw_in(pieces, hn):
    def body(a_ref, q_ref, k_ref, v_ref, c_ref, h_hbm, o_ref, s_ref, h):
        j = pl.program_id(0)

        @pl.when(j == 0)
        def _():
            pltpu.sync_copy(h_hbm, h)

        def step(x_ref):
            xv = x_ref[...]
            o_ref[...] = _dot_tn(xv, h[...]).astype(BF16)
            s_ref[...] = jnp.sum(xv.astype(F32), axis=0, keepdims=True)

        for x_ref, (lo, n) in zip((a_ref, q_ref, k_ref, v_ref, c_ref), PIECES):
            pl.when((j >= lo) & (j < lo + n))(lambda x_ref=x_ref: step(x_ref))

    def piece_spec(lo, n):
        return pl.BlockSpec((T, CT), lambda j: (0, jnp.clip(j - lo, 0, n - 1)))

    return pl.pallas_call(
        body, grid=(NCT,), name="dw_in",
        in_specs=[piece_spec(lo, n) for lo, n in PIECES] + [pl.BlockSpec(memory_space=pl.ANY)],
        out_specs=[pl.BlockSpec((CT, D), lambda j: (_perm(j), 0)), pl.BlockSpec((1, CT), lambda j: (0, _perm(j)))],
        out_shape=[jax.ShapeDtypeStruct((NIN, D), BF16), jax.ShapeDtypeStruct((1, NIN), F32)],
        scratch_shapes=[pltpu.VMEM((T, D), BF16)],
        compiler_params=_cp(("arbitrary",), VMEM_BIG),
    )(*_hbm(*pieces, hn))


def _mm_tn(a, b, ta, tt, name):
    m = a.shape[1]
    n = b.shape[1]
    nt = T // tt

    def body(a_ref, b_ref, o_ref, acc):
        t = pl.program_id(1)
        p = _dot_tn(a_ref[...].astype(BF16), b_ref[...].astype(BF16))
        if nt == 1:
            o_ref[...] = p.astype(BF16)
            return

        @pl.when(t == 0)
        def _():
            acc[...] = p

        @pl.when(t > 0)
        def _():
            acc[...] += p

        @pl.when(t == nt - 1)
        def _():
            o_ref[...] = acc[...].astype(BF16)

    return pl.pallas_call(
        body, grid=(m // ta, nt), name=name,
        in_specs=[pl.BlockSpec((tt, ta), lambda j, t: (t, j)), pl.BlockSpec((tt, n), lambda j, t: (t, 0))],
        out_specs=pl.BlockSpec((ta, n), lambda j, t: (j, 0)),
        out_shape=pltpu.HBM((m, n), BF16),
        scratch_shapes=[pltpu.VMEM((ta, n), F32)],
        compiler_params=_cp(("parallel", "arbitrary"), VMEM_MID),
    )(*_hbm(a, b))


def _row_iota():
    return lax.broadcasted_iota(jnp.int32, (S, RBW), 0)


SUBLANES = 8
N_SHIFT_BUFS = 4


class _Shifter:
    def __init__(self, bufs):
        self.bufs = bufs
        self.k = 0

    def _store(self, v, fill, front):
        b = self.bufs.at[self.k % N_SHIFT_BUFS]
        self.k += 1
        b[pl.ds(0 if front else SUBLANES + S, SUBLANES), :] = jnp.full((SUBLANES, RBW), fill, F32)
        b[pl.ds(SUBLANES, S), :] = v
        return b

    def down(self, v, ds, fill):
        b = self._store(v, fill, True)
        return [b[pl.ds(SUBLANES - d, S), :] for d in ds]

    def up(self, v, ds, fill):
        b = self._store(v, fill, False)
        return [b[pl.ds(SUBLANES + d, S), :] for d in ds]


def _shift_down(v, d, sh, fill):
    return sh.down(v, (d,), fill)[0]


def _shift_up(v, d, sh, fill):
    return sh.up(v, (d,), fill)[0]


def _scan_down(a, u, row):
    d = 1
    while d < S:
        last = 2 * d >= S
        if d < SUBLANES:
            u = a * _shift_down(u, d, row, 0.0) + u
            if not last:
                a = a * _shift_down(a, d, row, 1.0)
        else:
            u = jnp.concatenate([u[:d], a[d:] * u[:S - d] + u[d:]], axis=0)
            if not last:
                a = jnp.concatenate([a[:d], a[d:] * a[:S - d]], axis=0)
        d *= 2
    return u


def _scan_up(b, g, row):
    d = 1
    while d < S:
        last = 2 * d >= S
        if d < SUBLANES:
            g = g + b * _shift_up(g, d, row, 0.0)
            if not last:
                b = b * _shift_up(b, d, row, 0.0)
        else:
            g = jnp.concatenate([g[:S - d] + b[:S - d] * g[d:], g[S - d:]], axis=0)
            if not last:
                b = jnp.concatenate([b[:S - d] * b[d:], b[S - d:]], axis=0)
        d *= 2
    return g


def _softplus(x):
    return jnp.maximum(x, 0.0) + jnp.log1p(jnp.exp(-jnp.abs(x)))


N_SAVED = 5


def _rnn_gates(x, cw, cb, wa, ba, wx, bx, lam, row, pad, saved=None):
    xs = pad.down(x, (1, 2, 3), 0.0)
    if saved is not None:
        r, i, a, mult, xc = saved
        return xc, xc.astype(BF16), r, i, _softplus(-lam), a, mult, xs
    xc = cb + cw[3:4, :] * x
    for j in (1, 2, 3):
        xc = xc + cw[3 - j:4 - j, :] * xs[j - 1]
    xcb = xc.astype(BF16)
    r = _sigmoid(_dot(xcb, wa) + ba)
    i = _sigmoid(_dot(xcb, wx) + bx)
    sp = _softplus(-lam)
    log_a = (-LRU_C) * r * sp
    a = jnp.exp(log_a)
    mult = jnp.where(row == 0, 1.0, jnp.sqrt(jnp.tanh(-log_a) * (1.0 + a * a)))
    return xc, xcb, r, i, sp, a, mult, xs


def _rnn_fwd(proj3, conv_w, conv_b, wa, ba, wx, bx, lam, token):
    def body(x_ref, cw_ref, cb_ref, wa_ref, ba_ref, wx_ref, bx_ref, lam_ref, _token, h_ref, g_ref, pad):
        row = _row_iota()
        sh = _Shifter(pad)
        x = x_ref[0].astype(F32)
        xc, _, r, i, _, a, mult, _ = _rnn_gates(x, cw_ref[...], cb_ref[...], wa_ref[0], ba_ref[...],
                                             wx_ref[0], bx_ref[...], lam_ref[...], row, sh)
        for k, val in enumerate((r, i, a, mult, xc)):
            g_ref[k, 0] = val
        h_ref[0] = _scan_down(a, mult * (i * xc), sh)

    vec = lambda: pl.BlockSpec((1, RBW), lambda b, n: (0, n))
    mat = lambda: pl.BlockSpec((1, RBW, RBW), lambda b, n: (n, 0, 0))
    return pl.pallas_call(
        body, grid=(BL, NRB), name="rnn_fwd",
        in_specs=[pl.BlockSpec((1, S, RBW), lambda b, n: (b, 0, n)),
                  pl.BlockSpec((CONVW, RBW), lambda b, n: (0, n)),
                  vec(), mat(), vec(), mat(), vec(), vec(), pl.BlockSpec((8, LANES), lambda b, n: (0, 0))],
        out_specs=[pl.BlockSpec((1, S, RBW), lambda b, n: (b, 0, n)),
                   pl.BlockSpec((N_SAVED, 1, S, RBW), lambda b, n: (0, b, 0, n))],
        out_shape=[jax.ShapeDtypeStruct((BL, S, DR), F32), jax.ShapeDtypeStruct((N_SAVED, BL, S, DR), F32)],
        scratch_shapes=[pltpu.VMEM((N_SHIFT_BUFS, S + 2 * SUBLANES, RBW), F32)],
        compiler_params=_cp(("parallel", "parallel"), VMEM_MID),
    )(*_hbm(proj3, conv_w, conv_b, wa, ba, wx, bx, lam, token))


def _rnn_bwd(proj3, h3, dh3, gates, slab_a3, conv_w, conv_b, wa, ba, wx, bx, lam, token):
    def body(x_ref, h_ref, dh_ref, g_ref, cw_ref, cb_ref, wa_ref, ba_ref, wx_ref, bx_ref, lam_ref, _alias, _token,
             dx_ref, dcw_ref, dcb_ref, dwa_ref, dba_ref, dwx_ref, dbx_ref, dlam_ref, pad):
        row = _row_iota()
        sh = _Shifter(pad)
        x = x_ref[0].astype(F32)
        cw = cw_ref[...]
        wa_v = wa_ref[0]
        wx_v = wx_ref[0]
        lam_v = lam_ref[...]
        xc, xcb, r, i, sp, a, mult, xs = _rnn_gates(x, cw, cb_ref[...], wa_v, ba_ref[...], wx_v, bx_ref[...], lam_v,
                                                    row, sh, [g_ref[k, 0] for k in range(N_SAVED)])
        h = h_ref[0]
        g = _scan_up(_shift_up(a, 1, sh, 0.0), dh_ref[0], sh)
        da = g * _shift_down(h, 1, sh, 0.0)
        dmult = jnp.where(row == 0, 0.0, g * (i * xc))
        gm = g * mult
        di = gm * xc
        dxc = gm * i
        dlog_a = da * a - dmult * (a * a) / mult
        dr = dlog_a * ((-LRU_C) * sp)
        dsp = jnp.sum(dlog_a * ((-LRU_C) * r), axis=0, keepdims=True)
        dlam = dsp * (-_sigmoid(-lam_v))
        dpa = dr * r * (1.0 - r)
        dpx = di * i * (1.0 - i)
        dpab = dpa.astype(BF16)
        dpxb = dpx.astype(BF16)
        dwa = _dot_tn(xcb, dpab)
        dwx = _dot_tn(xcb, dpxb)
        dba = jnp.sum(dpa, axis=0, keepdims=True)
        dbx = jnp.sum(dpx, axis=0, keepdims=True)
        dxc = dxc + _dot_nt(dpab, wa_v) + _dot_nt(dpxb, wx_v)
        dcb = jnp.sum(dxc, axis=0, keepdims=True)
        dx = cw[3:4, :] * dxc
        dcw_rows = [None] * CONVW
        dcw_rows[3] = jnp.sum(dxc * x, axis=0, keepdims=True)
        dxc_up = sh.up(dxc, (1, 2, 3), 0.0)
        for j in (1, 2, 3):
            dx = dx + cw[3 - j:4 - j, :] * dxc_up[j - 1]
            dcw_rows[3 - j] = jnp.sum(dxc * xs[j - 1], axis=0, keepdims=True)
        dx_ref[0] = dx.astype(BF16)
        dcw = jnp.concatenate(dcw_rows, axis=0)
        first = pl.program_id(1) == 0

        @pl.when(first)
        def _():
            dcw_ref[...] = dcw
            dcb_ref[...] = dcb
            dwa_ref[0] = dwa
            dba_ref[...] = dba
            dwx_ref[0] = dwx
            dbx_ref[...] = dbx
            dlam_ref[...] = dlam

        @pl.when(jnp.logical_not(first))
        def _():
            dcw_ref[...] += dcw
            dcb_ref[...] += dcb
            dwa_ref[0] += dwa
            dba_ref[...] += dba
            dwx_ref[0] += dwx
            dbx_ref[...] += dbx
            dlam_ref[...] += dlam

    slab = lambda: pl.BlockSpec((1, S, RBW), lambda n, b: (b, 0, n))
    vec = lambda: pl.BlockSpec((1, RBW), lambda n, b: (0, n))
    mat = lambda: pl.BlockSpec((1, RBW, RBW), lambda n, b: (n, 0, 0))
    taps = lambda: pl.BlockSpec((CONVW, RBW), lambda n, b: (0, n))
    vshape = pltpu.HBM((1, DR), F32)
    mshape = pltpu.HBM((NRB, RBW, RBW), F32)
    return pl.pallas_call(
        body, grid=(NRB, BL), name="rnn_bwd",
        in_specs=[slab(), slab(), slab(), pl.BlockSpec((N_SAVED, 1, S, RBW), lambda n, b: (0, b, 0, n)),
                  taps(), vec(), mat(), vec(), mat(), vec(), vec(),
                  pl.BlockSpec(memory_space=pl.ANY), pl.BlockSpec((8, LANES), lambda n, b: (0, 0))],
        out_specs=[slab(), taps(), vec(), mat(), vec(), mat(), vec(), vec()],
        out_shape=[jax.ShapeDtypeStruct((BL, S, A_W), BF16), pltpu.HBM((CONVW, DR), F32),
                   vshape, mshape, vshape, mshape, vshape, vshape],
        input_output_aliases={11: 0},
        scratch_shapes=[pltpu.VMEM((N_SHIFT_BUFS, S + 2 * SUBLANES, RBW), F32)],
        compiler_params=_cp(("parallel", "arbitrary"), 48 * 1024 * 1024),
    )(*_hbm(proj3, h3, dh3, gates, conv_w, conv_b, wa, ba, wx, bx, lam, slab_a3, token))


NQB = S // QB


def _rms_head(t, gain):
    rstd = lax.rsqrt(jnp.mean(t * t, axis=-1, keepdims=True) + EPS)
    return t * rstd * gain


def _rope(t, cs, sn):
    return t * cs + pltpu.roll(t, HD // 2, 1) * sn


def _rope_t(dy, cs, sn):
    return dy * cs - pltpu.roll(dy, HD // 2, 1) * sn


def _bdot_nt(a, b):
    return lax.dot_general(a, b, (((2,), (2,)), ((0,), (0,))), preferred_element_type=F32)


def _bdot(a, b):
    return lax.dot_general(a, b, (((2,), (1,)), ((0,), (0,))), preferred_element_type=F32)


def _bdot_tn(a, b):
    return lax.dot_general(a, b, (((1,), (1,)), ((0,), (0,))), preferred_element_type=F32)


STRIDE_MAX = 4


def _permute(buf, x, dil, dst, off=0):
    ln = S // dil
    if dil == 1:
        dst[pl.ds(off, S), :] = x.astype(dst.dtype)
        return
    buf[0] = x
    if dil <= STRIDE_MAX:
        for c in range(dil):
            dst[pl.ds(off + c * ln, ln), :] = buf.at[0][pl.ds(c, ln, stride=dil), :].astype(dst.dtype)
        return
    f, r = STRIDE_MAX, dil // STRIDE_MAX
    part = S // f
    for c1 in range(f):
        buf.at[1][pl.ds(c1 * part, part), :] = buf.at[0][pl.ds(c1, part, stride=f), :]
    for c1 in range(f):
        for c2 in range(r):
            dst[pl.ds(off + (c1 + f * c2) * ln, ln), :] = (
                buf.at[1][pl.ds(c1 * part + c2, ln, stride=r), :].astype(dst.dtype))


def _unpermute(buf, xp, dil, dst):
    ln = S // dil
    if dil == 1:
        dst[...] = xp
        return
    if dil <= STRIDE_MAX:
        for c in range(dil):
            dst[pl.ds(c, ln, stride=dil), :] = xp[c * ln:(c + 1) * ln]
        return
    f, r = STRIDE_MAX, dil // STRIDE_MAX
    part = S // f
    for c1 in range(f):
        for c2 in range(r):
            c = c1 + f * c2
            buf.at[1][pl.ds(c1 * part + c2, ln, stride=r), :] = xp[c * ln:(c + 1) * ln]
    for c1 in range(f):
        dst[pl.ds(c1, part, stride=f), :] = buf[1, pl.ds(c1 * part, part), :]


def _blocks3(ref, off=0):
    return ref[pl.ds(off, S), :].reshape(NQB, QB, HD)


def _att_prep(q_ref, k_ref, v_ref, cos_ref, sin_ref, qn, kn, dil, nat, qs, ksp, vsp):
    cs = cos_ref[...]
    sn = sin_ref[...]
    zero = jnp.zeros((QB, HD), BF16)
    ksp[pl.ds(0, QB), :] = zero
    vsp[pl.ds(0, QB), :] = zero
    _permute(nat, _rope(_rms_head(q_ref[0].astype(F32), qn), cs, sn), dil, qs)
    _permute(nat, _rope(_rms_head(k_ref[0].astype(F32), kn), cs, sn), dil, ksp, QB)
    _permute(nat, v_ref[0].astype(F32), dil, vsp, QB)


def _att_scores(qs, ksp, dil):
    nb = S // dil // QB
    q3 = _blocks3(qs)
    shape = (NQB, QB, QB)
    qi = lax.broadcasted_iota(jnp.int32, shape, 1)
    kj = lax.broadcasted_iota(jnp.int32, shape, 2)
    s_c = jnp.where(qi >= kj, _bdot_nt(q3, _blocks3(ksp, QB)) * SCALE, NEG)
    if nb == 1:
        return q3, s_c, None
    jj = lax.broadcasted_iota(jnp.int32, shape, 0)
    ok = (kj >= qi) & ((jj & (nb - 1)) != 0)
    s_p = jnp.where(ok, _bdot_nt(q3, _blocks3(ksp)) * SCALE, NEG)
    return q3, s_c, s_p


def _qkv_spec(kind, g):
    base = OFF_Q // HD + (3 * g + kind) * NH
    return pl.BlockSpec((1, S, HD), lambda b, h: (b, 0, base + h))


def _attn_fwd(proj3, cos_t, sin_t, q_norm, k_norm):
    def body(*refs):
        qkv_refs = refs[:9]
        (cos_ref, sin_ref, qn_ref, kn_ref, att_ref, lse_ref, w_ref, qp_ref, kp_ref, vp_ref,
         nat, qs, ksp, vsp, og) = refs[9:]
        for g, (window, dil) in enumerate(PATTERNS):
            q_ref, k_ref, v_ref = qkv_refs[3 * g:3 * g + 3]
            _att_prep(q_ref, k_ref, v_ref, cos_ref, sin_ref, qn_ref[g:g + 1, :], kn_ref[g:g + 1, :], dil,
                      nat, qs, ksp, vsp)
            qp_ref[g, 0] = qs[...]
            kp_ref[g, 0] = ksp[pl.ds(QB, S), :]
            vp_ref[g, 0] = vsp[pl.ds(QB, S), :]
            _, s_c, s_p = _att_scores(qs, ksp, dil)
            m = jnp.max(s_c, axis=-1, keepdims=True)
            if s_p is not None:
                m = jnp.maximum(m, jnp.max(s_p, axis=-1, keepdims=True))
            e_c = jnp.exp(s_c - m)
            den = jnp.sum(e_c, axis=-1, keepdims=True)
            o = _bdot(e_c.astype(BF16), _blocks3(vsp, QB))
            if s_p is not None:
                e_p = jnp.exp(s_p - m)
                den = den + jnp.sum(e_p, axis=-1, keepdims=True)
                o = o + _bdot(e_p.astype(BF16), _blocks3(vsp))
            _unpermute(nat, (o / den).reshape(S, HD), dil, og.at[g])
            _unpermute(nat, jnp.broadcast_to(m + jnp.log(den), (NQB, QB, HD)).reshape(S, HD), dil,
                       lse_ref.at[g, 0])
        l0 = lse_ref[0, 0]
        l1 = lse_ref[1, 0]
        l2 = lse_ref[2, 0]
        mx = jnp.maximum(jnp.maximum(l0, l1), l2)
        e0 = jnp.exp(l0 - mx)
        e1 = jnp.exp(l1 - mx)
        e2 = jnp.exp(l2 - mx)
        inv = 1.0 / (e0 + e1 + e2)
        w0 = e0 * inv
        w1 = e1 * inv
        w2 = e2 * inv
        w_ref[0, 0] = w0
        w_ref[1, 0] = w1
        w_ref[2, 0] = w2
        att_ref[0] = w0 * og[0] + w1 * og[1] + w2 * og[2]

    in_specs = [_qkv_spec(kind, g) for g in range(NG) for kind in range(3)]
    in_specs += [pl.BlockSpec((S, HD), lambda b, h: (0, 0)), pl.BlockSpec((S, HD), lambda b, h: (0, 0)),
                 pl.BlockSpec((NG, HD), lambda b, h: (0, 0)), pl.BlockSpec((NG, HD), lambda b, h: (0, 0))]
    stat = lambda: pl.BlockSpec((NG, 1, S, HD), lambda b, h: (0, b, 0, h))
    return pl.pallas_call(
        body, grid=(BL, NH), name="attn_fwd",
        in_specs=in_specs,
        out_specs=[pl.BlockSpec((1, S, HD), lambda b, h: (b, 0, h)), stat(), stat(), stat(), stat(), stat()],
        out_shape=[jax.ShapeDtypeStruct((BL, S, ATT), F32),
                   jax.ShapeDtypeStruct((NG, BL, S, ATT), F32),
                   jax.ShapeDtypeStruct((NG, BL, S, ATT), F32)]
        + [jax.ShapeDtypeStruct((NG, BL, S, ATT), BF16)] * 3,
        scratch_shapes=[pltpu.VMEM((2, S, HD), F32), pltpu.VMEM((S, HD), BF16), pltpu.VMEM((S + QB, HD), BF16),
                        pltpu.VMEM((S + QB, HD), BF16), pltpu.VMEM((NG, S, HD), F32)],
        compiler_params=_cp(("parallel", "parallel"), VMEM_BIG),
    )(*_hbm(*([proj3] * 9), cos_t, sin_t, q_norm, k_norm))


def _attn_bwd(proj3, cos_t, sin_t, q_norm, k_norm, lse, wts, qkv_p, datt3, sbar3):
    def norm_rope_bwd(dpost, raw, gain, cs, sn):
        dn = _rope_t(dpost, cs, sn)
        rstd = lax.rsqrt(jnp.mean(raw * raw, axis=-1, keepdims=True) + EPS)
        xh = raw * rstd
        dgain = jnp.sum(dn * xh, axis=0, keepdims=True)
        gd = dn * gain
        draw = rstd * (gd - xh * jnp.mean(gd * xh, axis=-1, keepdims=True))
        return draw, dgain

    def group_body(g, refs, first):
        dil = PATTERNS[g][1]
        (q_ref, k_ref, qp_ref, kp_ref, vp_ref, cos_ref, sin_ref, qn_ref, kn_ref, lse_ref, w_ref, datt_ref,
         sbar_ref, dq_ref, dk_ref, dv_ref, dqn_ref, dkn_ref, nat, ksp, vsp, dos, cvp, lsp, acc) = refs
        qn = qn_ref[g:g + 1, :]
        kn = kn_ref[g:g + 1, :]
        cs = cos_ref[...]
        sn = sin_ref[...]
        qs = qp_ref.at[0, 0]
        zero = jnp.zeros((QB, HD), BF16)
        ksp[pl.ds(0, QB), :] = zero
        vsp[pl.ds(0, QB), :] = zero
        ksp[pl.ds(QB, S), :] = kp_ref[0, 0]
        vsp[pl.ds(QB, S), :] = vp_ref[0, 0]
        wv = w_ref[0, 0]
        _permute(nat, wv * datt_ref[0], dil, dos)
        _permute(nat, wv * sbar_ref[0], dil, cvp)
        _permute(nat, lse_ref[0, 0], dil, lsp)
        q3, s_c, s_p = _att_scores(qs, ksp, dil)
        do3 = _blocks3(dos)
        lse3 = _blocks3(lsp)[:, :, 0:1]
        cv3 = _blocks3(cvp)[:, :, 0:1]
        p_c = jnp.exp(s_c - lse3)
        ds_c = (p_c * (_bdot_nt(do3, _blocks3(vsp, QB)) - cv3)).astype(BF16)
        dq = _bdot(ds_c, _blocks3(ksp, QB))
        acc[0] = _bdot_tn(ds_c, q3).reshape(S, HD)
        acc[1] = _bdot_tn(p_c.astype(BF16), do3).reshape(S, HD)
        if s_p is not None:
            p_p = jnp.exp(s_p - lse3)
            ds_p = (p_p * (_bdot_nt(do3, _blocks3(vsp)) - cv3)).astype(BF16)
            dq = dq + _bdot(ds_p, _blocks3(ksp))
            early = pl.ds(0, S - QB)
            acc[0, early, :] += _bdot_tn(ds_p, q3).reshape(S, HD)[QB:]
            acc[1, early, :] += _bdot_tn(p_p.astype(BF16), do3).reshape(S, HD)[QB:]
        _unpermute(nat, (dq * SCALE).reshape(S, HD), dil, nat.at[0])
        draw, dqn = norm_rope_bwd(nat[0], q_ref[0].astype(F32), qn, cs, sn)
        dq_ref[0] = draw.astype(BF16)
        _unpermute(nat, acc[0] * SCALE, dil, nat.at[0])
        draw, dkn = norm_rope_bwd(nat[0], k_ref[0].astype(F32), kn, cs, sn)
        dk_ref[0] = draw.astype(BF16)
        _unpermute(nat, acc[1], dil, nat.at[0])
        dv_ref[0] = nat[0].astype(BF16)
        dqn8 = jnp.broadcast_to(dqn, (SUBLANES, HD))
        dkn8 = jnp.broadcast_to(dkn, (SUBLANES, HD))

        @pl.when(first)
        def _():
            dqn_ref[0] = dqn8
            dkn_ref[0] = dkn8

        @pl.when(jnp.logical_not(first))
        def _():
            dqn_ref[0] += dqn8
            dkn_ref[0] += dkn8

    def body(*refs):
        group = pl.program_id(0)
        first = (pl.program_id(1) == 0) & (pl.program_id(2) == 0)
        for g in range(NG):
            pl.when(group == g)(lambda g=g: group_body(g, refs, first))

    def raw_spec(kind):
        return pl.BlockSpec((1, S, HD), lambda g, b, h: (b, 0, OFF_Q // HD + (3 * g + kind) * NH + h))

    full = lambda r: pl.BlockSpec((r, HD), lambda g, b, h: (0, 0))
    stat = lambda: pl.BlockSpec((1, 1, S, HD), lambda g, b, h: (g, b, 0, h))
    slab = lambda: pl.BlockSpec((1, S, HD), lambda g, b, h: (b, 0, h))
    out_slab = lambda: pl.BlockSpec((1, S, HD), lambda g, b, h: (b, 0, g * NH + h))
    gains = lambda: pl.BlockSpec((1, SUBLANES, HD), lambda g, b, h: (g, 0, 0))
    big = jax.ShapeDtypeStruct((BL, S, GW), BF16)
    vecs = jax.ShapeDtypeStruct((NG, SUBLANES, HD), F32)
    return pl.pallas_call(
        body, grid=(NG, BL, NH), name="attn_bwd",
        in_specs=[raw_spec(0), raw_spec(1), stat(), stat(), stat(), full(S), full(S), full(NG), full(NG),
                  stat(), stat(), slab(), slab()],
        out_specs=[out_slab(), out_slab(), out_slab(), gains(), gains()],
        out_shape=[big, big, big, vecs, vecs],
        scratch_shapes=[pltpu.VMEM((2, S, HD), F32), pltpu.VMEM((S + QB, HD), BF16),
                        pltpu.VMEM((S + QB, HD), BF16), pltpu.VMEM((S, HD), BF16), pltpu.VMEM((S, HD), F32),
                        pltpu.VMEM((S, HD), F32), pltpu.VMEM((2, S, HD), F32)],
        compiler_params=_cp(("arbitrary", "arbitrary", "arbitrary"), VMEM_BIG),
    )(*_hbm(proj3, proj3, *qkv_p, cos_t, sin_t, q_norm, k_norm, lse, wts, datt3, sbar3))


def _tail(x, proj, h, att, p, tgt, w_o_rnn, w_o_att_t, w_out, w_pg, w_ple_t, norm_ple, b_pg, tm=256):
    nt = T // tm
    inv_d = 1.0 / D

    def body(x_ref, h_ref, zr_ref, att_ref, za_ref, g0a_ref, g0b_ref, g1a_ref, g1b_ref, p_ref, tgt_ref,
             np_ref, bpg_ref, wor_hbm, woa_hbm, wout_hbm, wpg_hbm, wple_hbm,
             dx1_ref, merged_ref, n1_ref, dpre_ref, dpe_ref, dyr_ref, dya_ref, slab_a_ref, slab_c_ref, dh_ref,
             datt_ref, sbar_ref, yrnn_ref, yatt_ref, loss_ref, dnp_ref, dbpg_ref,
             wor, woa, wout, wpg, wple, sems):
        first = pl.program_id(0) == 0

        @pl.when(first)
        def _():
            pairs = ((wor_hbm, wor), (woa_hbm, woa), (wout_hbm, wout), (wpg_hbm, wpg), (wple_hbm, wple))
            _copy_together([pltpu.make_async_copy(src, dst, sems.at[k]) for k, (src, dst) in enumerate(pairs)])

        xv = x_ref[...]
        hv = h_ref[...]
        zr = zr_ref[...].astype(F32)
        av = att_ref[...]
        za = za_ref[...].astype(F32)
        szr = _sigmoid(zr)
        silu_r = zr * szr
        yrnn_b = (hv * silu_r).astype(BF16)
        sza = _sigmoid(za)
        silu_a = za * sza
        yatt_b = (av * silu_a).astype(BF16)
        yrnn_ref[...] = yrnn_b
        yatt_ref[...] = yatt_b
        yr = _dot(yrnn_b, wor[...])
        ya = _dot_nt(yatt_b, woa[...])
        g0 = _sigmoid(jnp.concatenate([g0a_ref[...], g0b_ref[...]], axis=1).astype(F32))
        g1 = _sigmoid(jnp.concatenate([g1a_ref[...], g1b_ref[...]], axis=1).astype(F32))
        merged_b = (g0 * yr + g1 * ya).astype(BF16)
        merged_ref[...] = merged_b
        x1 = xv + _dot(merged_b, wout[...])
        rstd = lax.rsqrt(jnp.mean(x1 * x1, axis=-1, keepdims=True) + EPS)
        xh = x1 * rstd
        npl = np_ref[...]
        n1_b = (xh * npl).astype(BF16)
        n1_ref[...] = n1_b
        pg = _sigmoid(_dot(n1_b, wpg[...]) + bpg_ref[...])
        pe = _dot_nt(p_ref[...].astype(BF16), wple[...])
        err = x1 + pg * pe - tgt_ref[...]
        loss_t = 0.5 * inv_d * jnp.sum(err * err)
        dy = err * inv_d
        dpe_ref[...] = (dy * pg).astype(BF16)
        dpre = dy * pe * pg * (1.0 - pg)
        dpre_b = dpre.astype(BF16)
        dpre_ref[...] = dpre_b
        dn1 = _dot_nt(dpre_b, wpg[...])
        dnp = jnp.sum(dn1 * xh, axis=0, keepdims=True)
        dbpg = jnp.sum(dpre, axis=0, keepdims=True)
        gd = dn1 * npl
        dx1 = dy + rstd * (gd - xh * jnp.mean(gd * xh, axis=-1, keepdims=True))
        dx1_ref[...] = dx1
        dmerged = _dot_nt(dx1.astype(BF16), wout[...])
        dyr_b = (dmerged * g0).astype(BF16)
        dya_b = (dmerged * g1).astype(BF16)
        dyr_ref[...] = dyr_b
        dya_ref[...] = dya_b
        slab_c_ref[:, ATT:ATT + D] = (dmerged * yr * g0 * (1.0 - g0)).astype(BF16)
        slab_c_ref[:, ATT + D:ATT + 2 * D] = (dmerged * ya * g1 * (1.0 - g1)).astype(BF16)
        dyrnn = _dot_nt(dyr_b, wor[...])
        dyatt = _dot(dya_b, woa[...])
        dh_ref[...] = dyrnn * silu_r
        slab_a_ref[...] = (dyrnn * hv * szr * (1.0 + zr * (1.0 - szr))).astype(BF16)
        datt = dyatt * silu_a
        datt_ref[...] = datt
        slab_c_ref[:, 0:ATT] = (dyatt * av * sza * (1.0 + za * (1.0 - sza))).astype(BF16)
        da = datt * av
        for hh in range(NH):
            seg = slice(hh * HD, (hh + 1) * HD)
            sbar_ref[:, seg] = jnp.broadcast_to(jnp.sum(da[:, seg], axis=-1, keepdims=True), (tm, HD))

        @pl.when(first)
        def _():
            loss_ref[...] = jnp.full((8, LANES), loss_t, F32)
            dnp_ref[...] = dnp
            dbpg_ref[...] = dbpg

        @pl.when(jnp.logical_not(first))
        def _():
            loss_ref[...] += jnp.full((8, LANES), loss_t, F32)
            dnp_ref[...] += dnp
            dbpg_ref[...] += dbpg

    tok = lambda w: pl.BlockSpec((tm, w), lambda i: (i, 0))
    col = lambda w, blk: pl.BlockSpec((tm, w), lambda i: (i, blk))
    vec = lambda: pl.BlockSpec((1, D), lambda i: (0, 0))
    hbm = lambda: pl.BlockSpec(memory_space=pl.ANY)
    gb = OFF_G // 512
    in_specs = [tok(D), tok(DR), col(DR, 1), tok(ATT), col(ATT, OFF_ZA // ATT),
                col(512, gb), col(512, gb + 1), col(512, gb + 2), col(512, gb + 3),
                tok(PLE), tok(D), vec(), vec(), hbm(), hbm(), hbm(), hbm(), hbm()]
    sh = lambda w, dt: jax.ShapeDtypeStruct((T, w), dt)
    out_shape = [sh(D, F32), sh(D, BF16), sh(D, BF16), sh(D, BF16), sh(D, BF16), sh(D, BF16), sh(D, BF16),
                 sh(A_W, BF16), sh(C_W, BF16), sh(DR, F32), sh(ATT, F32), sh(ATT, F32),
                 sh(DR, BF16), sh(ATT, BF16),
                 jax.ShapeDtypeStruct((8, LANES), F32), jax.ShapeDtypeStruct((1, D), F32),
                 jax.ShapeDtypeStruct((1, D), F32)]
    out_specs = [tok(D), tok(D), tok(D), tok(D), tok(D), tok(D), tok(D), col(DR, 1), tok(C_W), tok(DR),
                 tok(ATT), tok(ATT), tok(DR), tok(ATT),
                 pl.BlockSpec((8, LANES), lambda i: (0, 0)), vec(), vec()]
    return pl.pallas_call(
        body, grid=(nt,), name="tail_fwd_bwd",
        in_specs=in_specs, out_specs=out_specs, out_shape=out_shape,
        scratch_shapes=[pltpu.VMEM((DR, D), BF16), pltpu.VMEM((D, ATT), BF16), pltpu.VMEM((D, D), BF16),
                        pltpu.VMEM((D, D), BF16), pltpu.VMEM((D, PLE), BF16), pltpu.SemaphoreType.DMA((5,))],
        compiler_params=_cp(("arbitrary",), VMEM_BIG),
    )(*_hbm(x, h, proj, att, proj, proj, proj, proj, proj, p, tgt, norm_ple, b_pg, w_o_rnn, w_o_att_t, w_out, w_pg,
            w_ple_t))


def _rope_tables():
    pos = np.arange(S, dtype=np.float32)
    inv_freq = np.float32(ROPE_THETA) ** (-np.arange(0, HD, 2, dtype=np.float32) / np.float32(HD))
    ang = (pos[:, None] * inv_freq[None, :]).astype(np.float32).astype(np.float64)
    cos, sin = np.cos(ang).astype(np.float32), np.sin(ang).astype(np.float32)
    return jnp.asarray(np.concatenate([cos, cos], axis=1)), jnp.asarray(np.concatenate([-sin, sin], axis=1))


def _local_step(x, p, tgt, project, other_weights, norm_mix, conv_b,
                w_rg_a, b_rg_a, w_rg_x, b_rg_x, lam, q_norm, k_norm, norm_ple, b_pg, start_reduce=None,
                entry_token=None):
    if start_reduce is None:
        start_reduce = lambda arrs, tag: (jnp.zeros((8, LANES), F32), arrs)
    if entry_token is None:
        entry_token = jnp.zeros((8, LANES), F32)
    cos_t, sin_t = _rope_tables()
    wa_b = w_rg_a.astype(BF16)
    wx_b = w_rg_x.astype(BF16)

    hn = _rmsnorm_fwd(x, norm_mix, entry_token)
    proj, w_bufs, chips, conv_w, token = project(hn)
    proj3 = proj.reshape(BL, S, NIN)
    h3, gates = _rnn_fwd(proj3, conv_w, conv_b, wa_b, b_rg_a, wx_b, b_rg_x, lam, token)
    att3, lse, wts, *qkv_p = _attn_fwd(proj3, cos_t, sin_t, q_norm, k_norm)
    w_o_rnn, w_o_att_t, w_out, w_pg, w_ple_t = other_weights(att3)
    (dx1, merged, n1, dpre, dpe, dyr, dya, slab_a, slab_c, dh, datt, sbar, yrnn, yatt, loss8, dnp, dbpg) = _tail(
        x, proj, h3.reshape(T, DR), att3.reshape(T, ATT), p, tgt, w_o_rnn, w_o_att_t, w_out, w_pg, w_ple_t,
        norm_ple, b_pg)

    token, pending_out = start_reduce([
        _mm_tn(yrnn, dyr, 640, T, "dw_o_rnn"),
        _mm_tn(dya, yatt, 512, T, "dw_o_att_t"),
        _mm_tn(merged, dx1, 512, T // 2, "dw_out"),
        _mm_tn(n1, dpre, 512, T, "dw_ple_gate"),
        _mm_tn(dpe, p, 512, T, "dw_ple_t")], "out")

    slab_a3, dcw, dcb, dwa, dba, dwx, dbx, dlam = _rnn_bwd(
        proj3, h3, dh.reshape(BL, S, DR), gates, slab_a.reshape(BL, S, A_W), conv_w, conv_b, wa_b, b_rg_a, wx_b, b_rg_x, lam,
        token)
    datt3 = datt.reshape(BL, S, ATT)
    sbar3 = sbar.reshape(BL, S, ATT)
    *slabs, dqn, dkn = _attn_bwd(proj3, cos_t, sin_t, q_norm, k_norm, lse, wts, qkv_p, datt3, sbar3)
    pieces =[slab_a3.reshape(T, A_W)] + [t.reshape(T, GW) for t in slabs] + [slab_c]
    dw_in_t, db_in = _dw_in(pieces, hn)
    token, pending_in = start_reduce([dw_in_t], "in")
    grad_x, dnm = _grad_x(pieces, w_bufs, chips, token, x, dx1, norm_mix)

    small = dict(w_rg_a=dwa, w_rg_x=dwx, norm_mix=dnm, b_in=db_in, conv_b=dcb, b_rg_a=dba, b_rg_x=dbx,
                 lru_lambda=dlam, q_norm=dqn, k_norm=dkn, norm_ple=dnp, b_ple_gate=dbpg, conv_w=dcw, loss=loss8)
    return grad_x, pending_out, pending_in, small


MESH = pl.DeviceIdType.MESH
HBM_SPEC = pl.BlockSpec(memory_space=pl.ANY)


def _my_pos():
    return lax.axis_index("x"), lax.axis_index("y"), lax.axis_index("c")


def _flip(pos, k):
    x, y, c = pos
    return (1 - x if k & 4 else x, 1 - y if k & 2 else y, 1 - c if k & 1 else c)


def _lin(pos):
    return 4 * pos[0] + 2 * pos[1] + pos[2]


def _chip(pos):
    return 2 * pos[0] + pos[1]


HBM_ONLY = pl.BlockSpec(memory_space=pltpu.HBM)
SEM_SPEC = pl.BlockSpec(memory_space=pltpu.SEMAPHORE)
SPLIT_COPY = pltpu.CompilerParams(has_side_effects=pltpu.SideEffectType.DATAFLOW_SIDE_EFFECTING)


def _chip_peers(me):
    return [_flip(me, 4), _flip(me, 2), _flip(me, 6)]


def _between_chips_start(parts, name):
    na = len(parts)

    def body(*refs):
        a_refs = refs[:na]
        land_refs = refs[na:2 * na]
        send_sems, recv_sems = refs[2 * na], refs[2 * na + 1]
        token = refs[-1]
        me = _my_pos()
        myq = _chip(me)
        for i in range(na):
            for j, peer in enumerate(_chip_peers(me)):
                pltpu.make_async_remote_copy(
                    src_ref=a_refs[i].at[_chip(peer)], dst_ref=land_refs[i].at[myq],
                    send_sem=send_sems.at[3 * i + j], recv_sem=recv_sems.at[3 * i + j],
                    device_id=peer, device_id_type=MESH).start()
        token[...] = jnp.zeros_like(token)

    hbm = [pltpu.HBM(a.shape, a.dtype) for a in parts]
    srcs = [pltpu.with_memory_space_constraint(a, pltpu.HBM) for a in parts]
    lands = [pltpu.with_memory_space_constraint(lax.empty(a.shape, a.dtype), pltpu.HBM) for a in parts]
    res = pl.pallas_call(
        body, name=name,
        out_shape=(pltpu.SemaphoreType.DMA((3 * na,)), pltpu.SemaphoreType.DMA((3 * na,)), *hbm, *hbm,
                   jax.ShapeDtypeStruct((8, LANES), F32)),
        in_specs=[HBM_ONLY] * (2 * na),
        out_specs=(SEM_SPEC, SEM_SPEC, *([HBM_ONLY] * (2 * na)), pl.BlockSpec(memory_space=pltpu.VMEM)),
        input_output_aliases={i: 2 + i for i in range(2 * na)},
        compiler_params=SPLIT_COPY,
    )(*srcs, *lands)
    return res[-1], (res[0], res[1], list(res[2:2 + na]), list(res[2 + na:2 + 2 * na]))


def _between_chips_wait(pending, after, name):
    send_sems, recv_sems, parts, lands = pending
    na = len(parts)

    def body(*refs):
        a_refs = refs[:na]
        land_refs = refs[na:2 * na]
        send_sems, recv_sems = refs[2 * na], refs[2 * na + 1]
        me = _my_pos()
        for i in range(na):
            for j, peer in enumerate(_chip_peers(me)):
                cp = pltpu.make_async_remote_copy(
                    src_ref=a_refs[i].at[_chip(peer)], dst_ref=land_refs[i].at[_chip(peer)],
                    send_sem=send_sems.at[3 * i + j], recv_sem=recv_sems.at[3 * i + j],
                    device_id=peer, device_id_type=MESH)
                cp.wait_send()
                cp.wait_recv()

    hbm = [pltpu.HBM(a.shape, a.dtype) for a in parts]
    res = pl.pallas_call(
        body, name=name, out_shape=(*hbm, *hbm),
        in_specs=[HBM_ONLY] * (2 * na) + [SEM_SPEC, SEM_SPEC, pl.BlockSpec(memory_space=pl.ANY)],
        out_specs=[HBM_ONLY] * (2 * na),
        input_output_aliases={i: i for i in range(2 * na)},
        compiler_params=SPLIT_COPY,
    )(*parts, *lands, send_sems, recv_sems, after)
    return list(res[:na]), list(res[na:])


def _remote(src, dst, send_sems, recv_sems, idx, peer):
    return pltpu.make_async_remote_copy(src_ref=src, dst_ref=dst, send_sem=send_sems.at[idx],
                                        recv_sem=recv_sems.at[idx], device_id=peer, device_id_type=MESH)


def _copies_own(bufs, me):
    return [(bufs[0], bufs[1].at[me[2]], 0, _flip(me, 1))]


def _copies_near(bufs, me):
    return [(bufs[0], bufs[1].at[0, me[2]], 0, _flip(me, 2)), (bufs[0], bufs[1].at[1, me[2]], 1, _flip(me, 4))]


def _copies_far(bufs, me):
    return [(bufs[0], bufs[1].at[me[2]], 0, _flip(me, 6))]


def _copies_others(bufs, me):
    na = len(bufs) // 2
    return [(bufs[i], bufs[na + i].at[_lin(me)], 7 * i + k - 1, _flip(me, k))
            for i in range(na) for k in range(1, NDEV)]


def _copies_exchange(bufs, me):
    na = len(bufs) // 2
    return [(bufs[i].at[_lin(_flip(me, k))], bufs[na + i].at[_lin(me)], 7 * i + k - 1, _flip(me, k))
            for i in range(na) for k in range(1, NDEV)]


GROUP_COPIES = dict(own=_copies_own, near=_copies_near, far=_copies_far, others=_copies_others,
                    exchange=_copies_exchange)
GROUP_COUNT = dict(own=1, near=2, far=1)
TO_ALL = ("others", "exchange")


def _gather_start(bufs, groups, after, name):
    nb = len(bufs)
    ng = len(groups)

    def body(*refs):
        b = refs[:nb]
        sems = refs[nb + 1:nb + 1 + 2 * ng]
        token = refs[-1]
        me = _my_pos()
        for gi, (group, idx) in enumerate(groups):
            for src, dst, k, peer in GROUP_COPIES[group]([b[i] for i in idx], me):
                _remote(src, dst, sems[2 * gi], sems[2 * gi + 1], k, peer).start()
        token[...] = jnp.zeros_like(token)

    sem_t = []
    for group, idx in groups:
        cnt = 7 * (len(idx) // 2) if group in TO_ALL else GROUP_COUNT[group]
        sem_t += [pltpu.SemaphoreType.DMA((cnt,)), pltpu.SemaphoreType.DMA((cnt,))]
    ins = [pltpu.with_memory_space_constraint(a, pltpu.HBM) for a in bufs]
    res = pl.pallas_call(
        body, name=name,
        out_shape=(*sem_t, *[pltpu.HBM(a.shape, a.dtype) for a in bufs], jax.ShapeDtypeStruct((8, LANES), F32)),
        in_specs=[HBM_ONLY] * nb + [pl.BlockSpec(memory_space=pl.ANY)],
        out_specs=(*([SEM_SPEC] * (2 * ng)), *([HBM_ONLY] * nb), pl.BlockSpec(memory_space=pltpu.VMEM)),
        input_output_aliases={i: 2 * ng + i for i in range(nb)},
        compiler_params=SPLIT_COPY,
    )(*ins, after)
    return res[-1], list(res[2 * ng:2 * ng + nb]), [(res[2 * gi], res[2 * gi + 1]) for gi in range(ng)]


def _gather_wait(group, send_sems, recv_sems, bufs, after, name):
    nb = len(bufs)
    copies = GROUP_COPIES[group]
    afters = list(after) if isinstance(after, (list, tuple)) else [after]

    def body(*refs):
        b = refs[:nb]
        ss, rs = refs[nb], refs[nb + 1]
        me = _my_pos()
        for src, dst, idx, peer in copies(b, me):
            if group in TO_ALL:
                landed = b[nb // 2 + idx // 7].at[_lin(peer)]
            elif group == "own":
                landed = b[1].at[1 - me[2]]
            else:
                landed = dst
            cp = _remote(src, landed, ss, rs, idx, peer)
            cp.wait_send()
            cp.wait_recv()

    res = pl.pallas_call(
        body, name=name, out_shape=[pltpu.HBM(a.shape, a.dtype) for a in bufs],
        in_specs=[HBM_ONLY] * nb + [SEM_SPEC, SEM_SPEC] + [pl.BlockSpec(memory_space=pl.ANY)] * len(afters),
        out_specs=[HBM_ONLY] * nb,
        input_output_aliases={i: i for i in range(nb)},
        compiler_params=SPLIT_COPY,
    )(*bufs, send_sems, recv_sems, *afters)
    return list(res)


def _forward_to_sibling(buf, name):
    n = buf.shape[0]

    def body(_in_ref, out_ref, send_sems, recv_sems):
        me = _my_pos()
        c = me[2]
        sibling = _flip(me, 1)
        sends = []
        for r in range(n):
            cp = _remote(out_ref.at[r, c], out_ref.at[r, c], send_sems, recv_sems, r, sibling)
            cp.start()
            sends.append(cp)
        for r in range(n):
            _remote(out_ref.at[r, c], out_ref.at[r, 1 - c], send_sems, recv_sems, r, sibling).wait_recv()
        for cp in sends:
            cp.wait_send()

    return pl.pallas_call(
        body, name=name, out_shape=jax.ShapeDtypeStruct(buf.shape, buf.dtype),
        in_specs=[HBM_SPEC], out_specs=HBM_SPEC,
        scratch_shapes=[pltpu.SemaphoreType.DMA((n,)), pltpu.SemaphoreType.DMA((n,))],
        input_output_aliases={0: 0},
    )(buf)


def _scalar(v):
    return jnp.asarray(v, jnp.int32).reshape(1)


def _exchange_sum_within_chip(part, name):
    nq, _, rows, width = part.shape

    def body(c_ref, mine_ref, src_hbm, out_ref, land_hbm, buf, copy_sem, send_sems, recv_sems):
        q = pl.program_id(0)
        me = _my_pos()
        sibling = _flip(me, 1)

        def remote(k):
            return pltpu.make_async_remote_copy(
                src_ref=src_hbm.at[k, 1 - me[2]], dst_ref=land_hbm.at[k], send_sem=send_sems.at[k],
                recv_sem=recv_sems.at[k], device_id=sibling, device_id_type=MESH)

        @pl.when(q == 0)
        def _():
            for k in range(nq):
                remote(k).start()

        remote(q).wait_recv()
        halves = [pl.ds(k * (rows // 2), rows // 2) for k in range(2)]
        arrived = [pltpu.make_async_copy(land_hbm.at[q, h], buf.at[h], copy_sem.at[k]) for k, h in enumerate(halves)]
        for cp in arrived:
            cp.start()
        for cp, h in zip(arrived, halves):
            cp.wait()
            out_ref[0, h, :] = (mine_ref[0, 0, h, :].astype(F32) + buf[h, :].astype(F32)).astype(out_ref.dtype)

        @pl.when(q == nq - 1)
        def _():
            for k in range(nq):
                remote(k).wait_send()

    blk = (rows, width)
    summed, _ = pl.pallas_call(
        body, name=name,
        grid_spec=pltpu.PrefetchScalarGridSpec(
            num_scalar_prefetch=1, grid=(nq,),
            in_specs=[pl.BlockSpec((1, 1) + blk, lambda q, c_ref: (q, c_ref[0], 0, 0)), HBM_SPEC],
            out_specs=[pl.BlockSpec((1,) + blk, lambda q, c_ref: (q, 0, 0)), HBM_SPEC],
            scratch_shapes=[pltpu.VMEM(blk, part.dtype), pltpu.SemaphoreType.DMA((2,)),
                            pltpu.SemaphoreType.DMA((nq,)), pltpu.SemaphoreType.DMA((nq,))]),
        out_shape=[jax.ShapeDtypeStruct((nq,) + blk, part.dtype)] * 2,
        compiler_params=_cp(("arbitrary",), VMEM_BIG),
    )(_scalar(lax.axis_index("c")), *_hbm(part, part))
    return summed


def _others(q, mine, nblk=NCHIP):
    return jnp.where(q == mine, (q + 1) % nblk, q)


def _sum_chips_adamw(own, recv, wv, mv, vv, token, tr, name):
    _, r, w = recv.shape

    def body(q_ref, own_ref, r0, r1, r2, r3, w_ref, m_ref, v_ref, _token, g_ref, d_ref, m2_ref, v2_ref):
        myq = q_ref[0]
        acc = None
        for q, r_ref in enumerate((r0, r1, r2, r3)):
            term = jnp.where(myq == q, own_ref[0], r_ref[0]).astype(F32)
            acc = term if acc is None else acc + term
        g_ref[...] = acc
        delta, m2, v2 = _adam_math(w_ref[...], acc, m_ref[...], v_ref[...])
        d_ref[...] = delta
        m2_ref[...] = m2
        v2_ref[...] = v2

    def recv_spec(q):
        return pl.BlockSpec((1, tr, w), lambda i, q_ref: (_others(q, q_ref[0]), i, 0))

    rows = lambda: pl.BlockSpec((tr, w), lambda i, q_ref: (i, 0))
    shp = jax.ShapeDtypeStruct((r, w), F32)
    return pl.pallas_call(
        body, name=name,
        grid_spec=pltpu.PrefetchScalarGridSpec(
            num_scalar_prefetch=1, grid=(r // tr,),
            in_specs=[pl.BlockSpec((1, tr, w), lambda i, q_ref: (q_ref[0], i, 0))]
            + [recv_spec(q) for q in range(NCHIP)] + [rows(), rows(), rows()]
            + [pl.BlockSpec((8, LANES), lambda i, q_ref: (0, 0))],
            out_specs=[rows(), rows(), rows(), rows()]),
        out_shape=[shp, shp, shp, shp],
        compiler_params=_cp(("arbitrary",), VMEM_MID),
    )(_scalar(_chip(_my_pos())), *_hbm(own, recv, recv, recv, recv, wv, mv, vv, token))


def _sum_blocks_small(own, recv, mine, transpose, name):
    na = len(recv)
    nblk = recv[0].shape[0]

    def body(q_ref, *refs):
        me = q_ref[0]
        for i in range(na):
            acc = None
            for q in range(nblk):
                term = jnp.where(me == q, refs[i][0], refs[na * (1 + q) + i][0]).astype(F32)
                acc = term if acc is None else acc + term
            refs[na * (1 + nblk) + i][...] = acc.T if transpose[i] else acc

    def oshape(a, tr):
        r, w = a.shape[1:]
        return (w, r) if tr else (r, w)

    own_spec = lambda a: pl.BlockSpec((1,) + a.shape[1:], lambda s, q_ref: (q_ref[0], 0, 0))
    recv_spec = lambda a, q: pl.BlockSpec((1,) + a.shape[1:], lambda s, q_ref: (_others(q, q_ref[0], nblk), 0, 0))
    out_spec = lambda shp: pl.BlockSpec(shp, lambda s, q_ref: (0, 0))
    in_specs = [own_spec(a) for a in own]
    for q in range(nblk):
        in_specs += [recv_spec(a, q) for a in recv]
    return pl.pallas_call(
        body, name=name,
        grid_spec=pltpu.PrefetchScalarGridSpec(
            num_scalar_prefetch=1, grid=(1,), in_specs=in_specs,
            out_specs=[out_spec(oshape(a, tr)) for a, tr in zip(recv, transpose)]),
        out_shape=[jax.ShapeDtypeStruct(oshape(a, tr), F32) for a, tr in zip(recv, transpose)],
        compiler_params=_cp(("arbitrary",), VMEM_MID),
    )(_scalar(mine), *_hbm(*own, *(list(recv) * nblk)))


def _rep_offsets():
    offs = []
    o = 0
    for r in REP_ROWS:
        offs.append(o)
        o += r
    return offs


LOSS_ROW = REP_TOTAL_ROWS


def _pack_small_grads(g):
    offs = _rep_offsets()

    def body(dwa, dwx, dnm, dbin, dcb, dba, dbx, dlam, dqn, dkn, dnp, dbpg, loss, o_ref):
        o_ref[pl.ds(REP_TOTAL_ROWS - 2, NDEV * REP_ROWS_DEV - REP_TOTAL_ROWS + 2), :] = jnp.zeros(
            (NDEV * REP_ROWS_DEV - REP_TOTAL_ROWS + 2, LANES), F32)
        o_ref[pl.ds(LOSS_ROW, 1), :] = loss[0:1, :]
        for n in range(NRB):
            o_ref[pl.ds(offs[0] + n * RBW, RBW), :] = dwa[n]
            o_ref[pl.ds(offs[1] + n * RBW, RBW), :] = dwx[n]

        def put_vec(off, ref, rows):
            for k in range(rows):
                o_ref[pl.ds(off + k, 1), :] = ref[:, k * LANES:(k + 1) * LANES]

        put_vec(offs[2], dnm, REP_ROWS[2])
        put_vec(offs[3], dbin, REP_ROWS[3])
        put_vec(offs[4], dcb, REP_ROWS[4])
        put_vec(offs[5], dba, REP_ROWS[5])
        put_vec(offs[6], dbx, REP_ROWS[6])
        put_vec(offs[7], dlam, REP_ROWS[7])
        for k in range(NG):
            o_ref[pl.ds(offs[8] + k, 1), :] = dqn[k, 0:1, :]
            o_ref[pl.ds(offs[9] + k, 1), :] = dkn[k, 0:1, :]
        put_vec(offs[10], dnp, REP_ROWS[10])
        put_vec(offs[11], dbpg, REP_ROWS[11])

    args = [g["w_rg_a"], g["w_rg_x"], g["norm_mix"], g["b_in"], g["conv_b"], g["b_rg_a"], g["b_rg_x"],
            g["lru_lambda"], g["q_norm"], g["k_norm"], g["norm_ple"], g["b_ple_gate"], g["loss"]]
    full = lambda shp: pl.BlockSpec(shp, lambda: (0,) * len(shp))
    return pl.pallas_call(
        body, name="pack_small_grads",
        in_specs=[full(a.shape) for a in args],
        out_specs=full((NDEV * REP_ROWS_DEV, LANES)),
        out_shape=jax.ShapeDtypeStruct((NDEV * REP_ROWS_DEV, LANES), F32),
    )(*_hbm(*args))


def _adam_math(wv, gv, mv, vv):
    c1 = 1.0 - B1 ** STEP
    c2 = 1.0 - B2 ** STEP
    m2 = B1 * mv + (1.0 - B1) * gv
    v2 = B2 * vv + (1.0 - B2) * (gv * gv)
    delta = (-LR) * ((m2 / c1) / (jnp.sqrt(v2 / c2) + AEPS) + WD * wv)
    return delta, m2, v2


def _adamw_small(rep_flat, w, m, v):
    offs = _rep_offsets()
    n = len(REP_NAMES)

    def body(*refs):
        g_ref = refs[0]
        w_refs = refs[1:1 + n]
        m_refs = refs[1 + n:1 + 2 * n]
        v_refs = refs[1 + 2 * n:1 + 3 * n]
        outs = refs[1 + 3 * n:]
        go, do, mo, vo = outs[:n], outs[n:2 * n], outs[2 * n:3 * n], outs[3 * n:]

        def emit(i, idx, gv):
            go[i][idx] = gv
            delta, m2, v2 = _adam_math(w_refs[i][idx], gv, m_refs[i][idx], v_refs[i][idx])
            do[i][idx] = delta
            mo[i][idx] = m2
            vo[i][idx] = v2

        for i in range(n):
            if i < 2:
                for b in range(NRB):
                    emit(i, b, g_ref[pl.ds(offs[i] + b * RBW, RBW), :])
            elif REP_NAMES[i] in ("q_norm", "k_norm"):
                emit(i, slice(None), g_ref[pl.ds(offs[i], NG), :])
            else:
                gv = jnp.concatenate([g_ref[pl.ds(offs[i] + k, 1), :] for k in range(REP_ROWS[i])], axis=1)
                emit(i, slice(None), gv)

    full = lambda shp: pl.BlockSpec(shp, lambda: (0,) * len(shp))
    pspecs = [full(a.shape) for a in w]
    pshapes = [jax.ShapeDtypeStruct(a.shape, F32) for a in w]
    res = pl.pallas_call(
        body, name="adamw_small",
        in_specs=[full(rep_flat.shape)] + pspecs * 3,
        out_specs=pspecs * 4, out_shape=pshapes * 4,
        compiler_params=_cp(None, VMEM_MID),
    )(*_hbm(rep_flat, *w, *m, *v))
    return res[:n], res[n:2 * n], res[2 * n:3 * n], res[3 * n:]


def _adamw_many(w, g, m, v, token):
    n = len(w)

    def body(*refs):
        for i in range(n):
            delta, m2, v2 = _adam_math(refs[i][...], refs[n + i][...], refs[2 * n + i][...], refs[3 * n + i][...])
            refs[4 * n + 1 + i][...] = delta
            refs[5 * n + 1 + i][...] = m2
            refs[6 * n + 1 + i][...] = v2

    full = lambda shp: pl.BlockSpec(shp, lambda: (0,) * len(shp))
    specs = [full(a.shape) for a in w]
    shapes = [jax.ShapeDtypeStruct(a.shape, F32) for a in w]
    res = pl.pallas_call(
        body, name="adamw_shards",
        in_specs=specs * 4 + [full((8, LANES))], out_specs=specs * 3, out_shape=shapes * 3,
        compiler_params=_cp(None, VMEM_MID),
    )(*_hbm(*w, *g, *m, *v, token))
    return res[:n], res[n:2 * n], res[2 * n:]


def kernel(x, p, norm_mix, w_in, b_in, conv_w, conv_b, w_rg_a, b_rg_a, w_rg_x, b_rg_x, lru_lambda, q_norm, k_norm, w_o_rnn, w_o_att, w_out, norm_ple, w_ple_gate, b_ple_gate, w_ple, loss_target, m_norm_mix, m_w_in, m_b_in, m_conv_w, m_conv_b, m_w_rg_a, m_b_rg_a, m_w_rg_x, m_b_rg_x, m_lru_lambda, m_q_norm, m_k_norm, m_w_o_rnn, m_w_o_att, m_w_out, m_norm_ple, m_w_ple_gate, m_b_ple_gate, m_w_ple, v_norm_mix, v_w_in, v_b_in, v_conv_w, v_conv_b, v_w_rg_a, v_b_rg_a, v_w_rg_x, v_b_rg_x, v_lru_lambda, v_q_norm, v_k_norm, v_w_o_rnn, v_w_o_att, v_w_out, v_norm_ple, v_w_ple_gate, v_b_ple_gate, v_w_ple):
    w = dict(norm_mix=norm_mix, w_in=w_in, b_in=b_in, conv_w=conv_w, conv_b=conv_b, w_rg_a=w_rg_a, b_rg_a=b_rg_a,
             w_rg_x=w_rg_x, b_rg_x=b_rg_x, lru_lambda=lru_lambda, q_norm=q_norm, k_norm=k_norm, w_o_rnn=w_o_rnn,
             w_o_att=w_o_att, w_out=w_out, norm_ple=norm_ple, w_ple_gate=w_ple_gate, b_ple_gate=b_ple_gate,
             w_ple=w_ple)
    m = dict(norm_mix=m_norm_mix, w_in=m_w_in, b_in=m_b_in, conv_w=m_conv_w, conv_b=m_conv_b, w_rg_a=m_w_rg_a,
             b_rg_a=m_b_rg_a, w_rg_x=m_w_rg_x, b_rg_x=m_b_rg_x, lru_lambda=m_lru_lambda, q_norm=m_q_norm,
             k_norm=m_k_norm, w_o_rnn=m_w_o_rnn, w_o_att=m_w_o_att, w_out=m_w_out, norm_ple=m_norm_ple,
             w_ple_gate=m_w_ple_gate, b_ple_gate=m_b_ple_gate, w_ple=m_w_ple)
    v = dict(norm_mix=v_norm_mix, w_in=v_w_in, b_in=v_b_in, conv_w=v_conv_w, conv_b=v_conv_b, w_rg_a=v_w_rg_a,
             b_rg_a=v_b_rg_a, w_rg_x=v_w_rg_x, b_rg_x=v_b_rg_x, lru_lambda=v_lru_lambda, q_norm=v_q_norm,
             k_norm=v_k_norm, w_o_rnn=v_w_o_rnn, w_o_att=v_w_o_att, w_out=v_w_out, norm_ple=v_norm_ple,
             w_ple_gate=v_w_ple_gate, b_ple_gate=v_b_ple_gate, w_ple=v_w_ple)
    names = list(w.keys())

    w_shard, conv_shard = w_in[0].T.astype(BF16), conv_w[0]
    pos = _my_pos()
    me, my_core, my_chip = _lin(pos), pos[2], _chip(pos)
    hbm_empty = lambda shp, dt: lax.empty(shp, dt)
    shp = w_shard.shape
    entry_token, bufs, sems = _gather_start(
        [w_shard, hbm_empty((2,) + shp, BF16), hbm_empty((2, 2) + shp, BF16), conv_shard,
         hbm_empty((NDEV,) + conv_shard.shape, F32)],
        [("own", (0, 1)), ("near", (0, 2)), ("others", (3, 4))], norm_mix, "gather_start_near")
    w_src, own_l, near_l, conv_src, conv_l = bufs
    sem_own, sem_near, sem_conv = sems
    gather_out = {}

    def project(hn):
        w_thru, own = _gather_wait("own", *sem_own, [w_src, own_l], hn, "gather_wait_own")
        own = lax.dynamic_update_slice(own, w_shard[None], (my_core, 0, 0)).reshape(1, CHIP_COLS, D)
        chips = [jnp.stack([my_chip]), jnp.stack([my_chip ^ 1, my_chip ^ 2]), jnp.stack([my_chip ^ 3])]
        chips = [c.astype(jnp.int32) for c in chips]
        proj = _in_proj_chips(hn, own, b_in, chips[0], None, entry_token, "in_proj_own")
        proj, raw = lax.optimization_barrier((proj, (w_o_rnn[0], w_o_att[0], w_out[0], w_ple_gate[0], w_ple[0])))
        srcs = [raw[0].astype(BF16), raw[1].T.astype(BF16), raw[2].astype(BF16), raw[3].astype(BF16),
                raw[4].T.astype(BF16)]
        w_thru, near = _gather_wait("near", *sem_near, [w_thru, near_l], [proj] + srcs, "gather_wait_near")
        near = _forward_to_sibling(near, "gather_forward_near")
        token, (w_thru, far_l), (sem_far,) = _gather_start(
            [w_thru, hbm_empty((2,) + shp, BF16)], [("far", (0, 1))], near, "gather_start_far")
        near = near.reshape(2, CHIP_COLS, D)
        proj = _in_proj_chips(hn, near, b_in, chips[1], proj, token, "in_proj_near")
        w_thru, far = _gather_wait("far", *sem_far, [w_thru, far_l], proj, "gather_wait_far")
        far = _forward_to_sibling(far[None], "gather_forward_far").reshape(1, CHIP_COLS, D)
        proj = _in_proj_chips(hn, far, b_in, chips[2], proj, token, "in_proj_far")
        conv_thru, conv_g = _gather_wait("others", *sem_conv, [conv_src, conv_l], proj, "gather_wait_conv")
        conv_g = lax.dynamic_update_slice(conv_g, conv_shard[None], (me, 0, 0))
        conv_f = conv_g.transpose(1, 0, 2).reshape(CONVW, DR)
        token, obufs, (sem_out,) = _gather_start(
            srcs + [hbm_empty((NDEV,) + a.shape, BF16) for a in srcs], [("others", tuple(range(10)))], proj,
            "gather_start_out")
        gather_out.update(bufs=obufs, sems=sem_out, shards=srcs)
        return proj, [own, near, far], jnp.concatenate(chips), conv_f, token

    def other_weights(after):
        obufs = _gather_wait("others", *gather_out["sems"], gather_out["bufs"], after, "gather_wait_out")
        full = [lax.dynamic_update_slice(a, s[None], (me, 0, 0)) for a, s in zip(obufs[5:], gather_out["shards"])]
        return [a.reshape((NDEV * a.shape[1], a.shape[2])) for a in full]

    def start_reduce(arrs, tag):
        if tag == "out":
            parts = [a.reshape((NDEV, a.shape[0] // NDEV, a.shape[1])) for a in arrs]
            token, bufs, (sems,) = _gather_start(
                parts + [lax.empty(a.shape, a.dtype) for a in parts], [("exchange", tuple(range(2 * len(parts))))],
                arrs[-1][:SUBLANES], "reduce_out_start")
            return token, (bufs, sems)
        parts = [a.reshape((NCHIP, 2, a.shape[0] // NDEV, a.shape[1])) for a in arrs]
        pair_sums = [_exchange_sum_within_chip(a, "reduce_within_chip_" + tag) for a in parts]
        return _between_chips_start(pair_sums, "reduce_between_chips_start_" + tag)

    grad_x, pending_out, pending_in, small = _local_step(
        x.reshape(T, D), p.reshape(T, PLE), loss_target.reshape(T, D),
        project, other_weights,
        norm_mix, conv_b, w_rg_a[0], b_rg_a, w_rg_x[0], b_rg_x, lru_lambda, q_norm[0], k_norm[0],
        norm_ple, b_ple_gate, start_reduce, entry_token)

    rep_parts = _pack_small_grads(small).reshape(NDEV, REP_ROWS_DEV, LANES)
    conv_parts = small["conv_w"].reshape(CONVW, NDEV, DR // NDEV).transpose(1, 0, 2)
    smalls = [rep_parts, conv_parts]
    token, sbufs, (sem_x,) = _gather_start(
        smalls + [lax.empty(a.shape, F32) for a in smalls], [("exchange", (0, 1, 2, 3))], small["norm_mix"],
        "reduce_small_start")

    own_in, recv_in = _between_chips_wait(pending_in, token, "reduce_between_chips_wait_in")
    w_in_res = _sum_chips_adamw(own_in[0], recv_in[0], w_in[0].T, m_w_in[0].T, v_w_in[0].T, token, 304, "adamw_w_in")
    sbufs = _gather_wait("exchange", *sem_x, sbufs, w_in_res[0], "reduce_small_wait")
    g_rep, g_conv = _sum_blocks_small(sbufs[:2], sbufs[2:], me, (False, False), "sum_small")
    token, gbufs, (sem_g,) = _gather_start(
        [g_rep, lax.empty((NDEV,) + g_rep.shape, F32)], [("others", (0, 1))], g_conv, "gather_small_start")
    obufs = _gather_wait("exchange", *pending_out[1], pending_out[0], token, "reduce_out_wait")
    g_o_rnn, g_o_att, g_out, g_pg, g_ple = _sum_blocks_small(
        obufs[:5], obufs[5:], me, (False, True, False, False, True), "sum_out")

    grad, delta, new_m, new_v = {}, {}, {}, {}
    rest = ("w_o_rnn", "w_o_att", "w_out", "w_ple_gate", "w_ple", "conv_w")
    g_rest = [g_o_rnn, g_o_att, g_out, g_pg, g_ple, g_conv]
    rest_res = _adamw_many([w[n][0] for n in rest], g_rest, [m[n][0] for n in rest], [v[n][0] for n in rest], token)
    _, rep_all = _gather_wait("others", *sem_g, gbufs, rest_res[0][0], "gather_small_wait")
    rep_all = lax.dynamic_update_slice(rep_all, g_rep[None], (me, 0, 0)).reshape(NDEV * REP_ROWS_DEV, LANES)
    loss = rep_all[LOSS_ROW, 0]
    rep_shape = lambda a: a if a.ndim == 2 else a.reshape(a.shape[1:])
    res = _adamw_small(rep_all, [rep_shape(w[n]) for n in REP_NAMES], [rep_shape(m[n]) for n in REP_NAMES],
                       [rep_shape(v[n]) for n in REP_NAMES])
    for dst, vals in zip((grad, delta, new_m, new_v), res):
        for n, a in zip(REP_NAMES, vals):
            dst[n] = a.reshape(w[n].shape)
    grad["w_in"], delta["w_in"], new_m["w_in"], new_v["w_in"] = [a.T[None] for a in w_in_res]
    for n, a in zip(rest, g_rest):
        grad[n] = a[None]
    for dst, vals in zip((delta, new_m, new_v), rest_res):
        for n, a in zip(rest, vals):
            dst[n] = a[None]

    return (loss, grad_x.reshape(BL, S, D), *[grad[n] for n in names], *[delta[n] for n in names],
            *[new_m[n] for n in names], *[new_v[n] for n in names])
```
